```python
import jax, jax.numpy as jnp
from jax import lax
import numpy as np

D_MODEL = 1024
BATCH = 8
SEQ = 8192
DEPTH = 1

GRID_W = 64
N_Q_HEADS = 16
N_KV_HEADS = 4
HEAD_DIM = 64
ROPE_THETA = 10000.0
Q_BLOCK = 128
SSD_EXPAND = 2
D_INNER = SSD_EXPAND * D_MODEL
SSD_HEAD_DIM = 64
N_SSD_HEADS = D_INNER // SSD_HEAD_DIM
N_SSD_GROUPS = 4
D_STATE = 128
D_CONV = 5
CHUNK = 128
D_FF = 4 * D_MODEL
EPS = 1e-6

ATTN_Q_DIM = N_Q_HEADS * HEAD_DIM
ATTN_KV_DIM = N_KV_HEADS * HEAD_DIM
CONV_DIM = D_INNER + 2 * N_SSD_GROUPS * D_STATE
D_IN_PROJ = ATTN_Q_DIM + 2 * ATTN_KV_DIM + CONV_DIM + D_INNER + 2 * N_SSD_HEADS + 2 * D_MODEL

kernel_name = "hybrid_gqa_ssd_griffin_merge_block"


def rms_norm(x, w):
    xf = x.astype(jnp.float32)
    xf = xf * lax.rsqrt(jnp.mean(xf * xf, axis=-1, keepdims=True) + EPS)
    return xf.astype(x.dtype) * w


def rotate(u, cos, sin):
    f = u.shape[-1] // 2
    u1, u2 = u[..., :f], u[..., f:]
    cos = cos[None, :, None, :]
    sin = sin[None, :, None, :]
    return jnp.concatenate([u1 * cos - u2 * sin, u2 * cos + u1 * sin], axis=-1)


def axial_rope(t, cos_r, sin_r, cos_c, sin_c):
    half = t.shape[-1] // 2
    out = jnp.concatenate([rotate(t[..., :half], cos_r, sin_r),
                           rotate(t[..., half:], cos_c, sin_c)], axis=-1)
    return out.astype(t.dtype)


def block_attention(q, k, v):
    b, s, hq, dh = q.shape
    hkv = k.shape[2]
    rep = hq // hkv
    nb = s // Q_BLOCK
    qb = q.reshape(b, nb, Q_BLOCK, hkv, rep, dh).transpose(1, 0, 2, 3, 4, 5)
    scale = dh ** -0.5

    def one_block(qi):
        sc = jnp.einsum("bqgrd,bkgd->bgrqk", qi, k).astype(jnp.float32) * scale
        p = jax.nn.softmax(sc, axis=-1).astype(v.dtype)
        return jnp.einsum("bgrqk,bkgd->bqgrd", p, v)

    out = lax.map(one_block, qb)
    return out.transpose(1, 0, 2, 3, 4, 5).reshape(b, s, hq * dh)


def segsum(a):
    t = a.shape[-1]
    cs = jnp.cumsum(a, axis=-1)
    diff = cs[..., :, None] - cs[..., None, :]
    mask = jnp.tril(jnp.ones((t, t), dtype=bool))
    return jnp.where(mask, diff, -jnp.inf)


def ssd_chunked(xdt, a, bm, cm):
    b, s, h, p = xdt.shape
    g, n = bm.shape[2], bm.shape[3]
    r = h // g
    nc = s // CHUNK
    X = xdt.astype(jnp.float32).reshape(b, nc, CHUNK, g, r, p)
    A = a.reshape(b, nc, CHUNK, g, r).transpose(0, 3, 4, 1, 2)
    Bc = bm.astype(jnp.float32).reshape(b, nc, CHUNK, g, n)
    Cc = cm.astype(jnp.float32).reshape(b, nc, CHUNK, g, n)
    A_cs = jnp.cumsum(A, axis=-1)
    CB = jnp.einsum("bclgn,bcsgn->bgcls", Cc, Bc)
    M = CB[:, :, None] * jnp.exp(segsum(A))
    y_diag = jnp.einsum("bgrcls,bcsgrp->bclgrp", M, X)
    decay_states = jnp.exp(A_cs[..., -1:] - A_cs)
    states = jnp.einsum("bclgn,bgrcl,bclgrp->cbgrpn", Bc, decay_states, X)
    chunk_decay = jnp.moveaxis(jnp.exp(A_cs[..., -1]), -1, 0)

    def step(hs, inp):
        st, dec = inp
        return dec[..., None, None] * hs + st, hs

    h0 = jnp.zeros(states.shape[1:], jnp.float32)
    _, prev = lax.scan(step, h0, (states, chunk_decay))
    y_off = jnp.einsum("bclgn,cbgrpn,bgrcl->bclgrp", Cc, prev, jnp.exp(A_cs))
    return (y_diag + y_off).reshape(b, s, h, p)


def depthwise_conv_centred(u, w, bias):
    pad = (w.shape[0] - 1) // 2
    out = lax.conv_general_dilated(u, w[:, None, :].astype(u.dtype), window_strides=(1,),
                                   padding=[(pad, pad)], dimension_numbers=("NWC", "WIO", "NWC"),
                                   feature_group_count=u.shape[-1])
    return out + bias


def ssd_mixer(xBC, z, dt_raw, conv_w, conv_b, A_log, dt_bias, ssd_D, ssd_norm_w):
    b, s, _ = xBC.shape
    xBC = jax.nn.silu(depthwise_conv_centred(xBC, conv_w, conv_b))
    gn = N_SSD_GROUPS * D_STATE
    xs = xBC[..., :D_INNER].reshape(b, s, N_SSD_HEADS, SSD_HEAD_DIM)
    bm = xBC[..., D_INNER:D_INNER + gn].reshape(b, s, N_SSD_GROUPS, D_STATE)
    cm = xBC[..., D_INNER + gn:].reshape(b, s, N_SSD_GROUPS, D_STATE)
    dt = jax.nn.softplus(dt_raw.astype(jnp.float32).reshape(b, s, 2, N_SSD_HEADS)
                         + dt_bias.astype(jnp.float32))
    A = -jnp.exp(A_log.astype(jnp.float32))
    dt_f, dt_b = dt[:, :, 0], dt[:, :, 1]
    xf = xs.astype(jnp.float32)
    y_fwd = ssd_chunked(xf * dt_f[..., None], dt_f * A[0], bm, cm)
    flip = lambda t: jnp.flip(t, axis=1)
    y_bwd = flip(ssd_chunked(flip(xf * dt_b[..., None]), flip(dt_b * A[1]), flip(bm), flip(cm)))
    y = y_fwd + y_bwd + ssd_D.astype(jnp.float32)[:, None] * xf
    y = y.reshape(b, s, D_INNER).astype(xBC.dtype)
    return rms_norm(y * jax.nn.silu(z), ssd_norm_w)


def _fwd_setup_inputs(seed: int = 0) -> dict:
    key = jax.random.key(seed)
    ks = jax.random.split(key, 20)
    f32 = jnp.float32
    L = DEPTH
    nrm = lambda k, shape, s: jax.random.normal(k, shape, f32) * s
    x = jax.random.normal(ks[0], (BATCH, SEQ, D_MODEL), f32)
    c = jax.random.normal(ks[1], (BATCH, D_MODEL), f32)
    w_ada = nrm(ks[2], (L, D_MODEL, 6 * D_MODEL), 0.5 * D_MODEL ** -0.5)
    b_ada = nrm(ks[3], (L, 6 * D_MODEL), 0.02)
    norm1_w = 1.0 + nrm(ks[4], (L, D_MODEL), 0.02)
    norm2_w = 1.0 + nrm(ks[5], (L, D_MODEL), 0.02)
    w_in = nrm(ks[6], (L, D_MODEL, D_IN_PROJ), D_MODEL ** -0.5)
    q_norm_w = 1.0 + nrm(ks[7], (L, HEAD_DIM), 0.02)
    k_norm_w = 1.0 + nrm(ks[8], (L, HEAD_DIM), 0.02)
    conv_w = nrm(ks[9], (L, D_CONV, CONV_DIM), D_CONV ** -0.5)
    conv_b = nrm(ks[10], (L, CONV_DIM), 0.02)
    A_log = jnp.log(jax.random.uniform(ks[11], (L, 2, N_SSD_HEADS), f32, 1.0, 16.0))
    dt0 = jnp.exp(jax.random.uniform(ks[12], (L, 2, N_SSD_HEADS), f32, np.log(1e-3), np.log(1e-1)))
    dt_bias = dt0 + jnp.log(-jnp.expm1(-dt0))
    ssd_D = 1.0 + nrm(ks[13], (L, N_SSD_HEADS), 0.02)
    ssd_norm_w = 1.0 + nrm(ks[14], (L, D_INNER), 0.02)
    w_attn_out = nrm(ks[15], (L, ATTN_Q_DIM, D_MODEL), ATTN_Q_DIM ** -0.5)
    w_ssd_out = nrm(ks[16], (L, D_INNER, D_MODEL), D_INNER ** -0.5)
    w_o = nrm(ks[17], (L, D_MODEL, D_MODEL), D_MODEL ** -0.5)
    w_mlp1 = nrm(ks[18], (L, D_MODEL, D_FF), D_MODEL ** -0.5)
    w_mlp2 = nrm(ks[19], (L, D_FF, D_MODEL), D_FF ** -0.5)
    return {"x": x, "c": c, "w_ada": w_ada, "b_ada": b_ada, "norm1_w": norm1_w, "norm2_w": norm2_w,
            "w_in": w_in, "q_norm_w": q_norm_w, "k_norm_w": k_norm_w, "conv_w": conv_w, "conv_b": conv_b,
            "A_log": A_log, "dt_bias": dt_bias, "ssd_D": ssd_D, "ssd_norm_w": ssd_norm_w,
            "w_attn_out": w_attn_out, "w_ssd_out": w_ssd_out, "w_o": w_o,
            "w_mlp1": w_mlp1, "w_mlp2": w_mlp2}


def _fwd_reference(x, c, w_ada, b_ada, norm1_w, norm2_w, w_in, q_norm_w, k_norm_w, conv_w, conv_b,
              A_log, dt_bias, ssd_D, ssd_norm_w, w_attn_out, w_ssd_out, w_o, w_mlp1, w_mlp2):
    b, s, d = x.shape
    rows = s // GRID_W
    pos_row = jnp.repeat(jnp.arange(rows, dtype=jnp.int32), GRID_W).astype(jnp.float32)
    pos_col = jnp.tile(jnp.arange(GRID_W, dtype=jnp.int32), rows).astype(jnp.float32)
    axis_dim = HEAD_DIM // 2
    inv_freq = ROPE_THETA ** (-jnp.arange(0, axis_dim, 2, dtype=jnp.float32) / axis_dim)
    ang_r = pos_row[:, None] * inv_freq[None, :]
    ang_c = pos_col[:, None] * inv_freq[None, :]
    cos_r, sin_r = jnp.cos(ang_r), jnp.sin(ang_r)
    cos_c, sin_c = jnp.cos(ang_c), jnp.sin(ang_c)

    sizes = [ATTN_Q_DIM, ATTN_KV_DIM, ATTN_KV_DIM, CONV_DIM, D_INNER, 2 * N_SSD_HEADS, 2 * D_MODEL]
    offsets = []
    acc = 0
    for sz in sizes[:-1]:
        acc += sz
        offsets.append(acc)

    for l in range(DEPTH):
        mod = jax.nn.silu(c) @ w_ada[l] + b_ada[l]
        shift1, scale1, gate1, shift2, scale2, gate2 = [m[:, None, :] for m in jnp.split(mod, 6, axis=-1)]

        h = rms_norm(x, norm1_w[l]) * (1.0 + scale1) + shift1
        proj = h @ w_in[l]
        q, k, v, xBC, z, dt_raw, gates = jnp.split(proj, offsets, axis=-1)

        q = rms_norm(q.reshape(b, s, N_Q_HEADS, HEAD_DIM), q_norm_w[l])
        k = rms_norm(k.reshape(b, s, N_KV_HEADS, HEAD_DIM), k_norm_w[l])
        v = v.reshape(b, s, N_KV_HEADS, HEAD_DIM)
        q = axial_rope(q, cos_r, sin_r, cos_c, sin_c)
        k = axial_rope(k, cos_r, sin_r, cos_c, sin_c)
        attn = block_attention(q, k, v)

        ssd = ssd_mixer(xBC, z, dt_raw, conv_w[l], conv_b[l], A_log[l], dt_bias[l], ssd_D[l], ssd_norm_w[l])

        g_attn = jax.nn.sigmoid(gates[..., :D_MODEL])
        g_ssd = jax.nn.sigmoid(gates[..., D_MODEL:])
        merged = g_attn * (attn @ w_attn_out[l]) + g_ssd * (ssd @ w_ssd_out[l])
        x = x + gate1 * (merged @ w_o[l])

        h2 = rms_norm(x, norm2_w[l]) * (1.0 + scale2) + shift2
        ff = jnp.square(jax.nn.relu(h2 @ w_mlp1[l])) @ w_mlp2[l]
        x = x + gate2 * ff
    return x


import jax as _jax
import jax.numpy as _jnp

TWIN_FORMAT = 'train_step'
FWD_PARAMS = ['x', 'c', 'w_ada', 'b_ada', 'norm1_w', 'norm2_w', 'w_in', 'q_norm_w', 'k_norm_w', 'conv_w', 'conv_b', 'A_log', 'dt_bias', 'ssd_D', 'ssd_norm_w', 'w_attn_out', 'w_ssd_out', 'w_o', 'w_mlp1', 'w_mlp2']
TWIN_WEIGHTS = ['w_ada', 'b_ada', 'norm1_w', 'norm2_w', 'w_in', 'q_norm_w', 'k_norm_w', 'conv_w', 'conv_b', 'A_log', 'dt_bias', 'ssd_D', 'ssd_norm_w', 'w_attn_out', 'w_ssd_out', 'w_o', 'w_mlp1', 'w_mlp2']
TWIN_DIFF_INPUT = 'x'
TWIN_INPUTS = ['x', 'c', 'w_ada', 'b_ada', 'norm1_w', 'norm2_w', 'w_in', 'q_norm_w', 'k_norm_w', 'conv_w', 'conv_b', 'A_log', 'dt_bias', 'ssd_D', 'ssd_norm_w', 'w_attn_out', 'w_ssd_out', 'w_o', 'w_mlp1', 'w_mlp2', 'loss_target', 'm_w_ada', 'm_b_ada', 'm_norm1_w', 'm_norm2_w', 'm_w_in', 'm_q_norm_w', 'm_k_norm_w', 'm_conv_w', 'm_conv_b', 'm_A_log', 'm_dt_bias', 'm_ssd_D', 'm_ssd_norm_w', 'm_w_attn_out', 'm_w_ssd_out', 'm_w_o', 'm_w_mlp1', 'm_w_mlp2', 'v_w_ada', 'v_b_ada', 'v_norm1_w', 'v_norm2_w', 'v_w_in', 'v_q_norm_w', 'v_k_norm_w', 'v_conv_w', 'v_conv_b', 'v_A_log', 'v_dt_bias', 'v_ssd_D', 'v_ssd_norm_w', 'v_w_attn_out', 'v_w_ssd_out', 'v_w_o', 'v_w_mlp1', 'v_w_mlp2']
TWIN_OUTPUTS = ['loss', 'grad_x', 'grad_w_ada', 'grad_b_ada', 'grad_norm1_w', 'grad_norm2_w', 'grad_w_in', 'grad_q_norm_w', 'grad_k_norm_w', 'grad_conv_w', 'grad_conv_b', 'grad_A_log', 'grad_dt_bias', 'grad_ssd_D', 'grad_ssd_norm_w', 'grad_w_attn_out', 'grad_w_ssd_out', 'grad_w_o', 'grad_w_mlp1', 'grad_w_mlp2', 'delta_w_ada', 'delta_b_ada', 'delta_norm1_w', 'delta_norm2_w', 'delta_w_in', 'delta_q_norm_w', 'delta_k_norm_w', 'delta_conv_w', 'delta_conv_b', 'delta_A_log', 'delta_dt_bias', 'delta_ssd_D', 'delta_ssd_norm_w', 'delta_w_attn_out', 'delta_w_ssd_out', 'delta_w_o', 'delta_w_mlp1', 'delta_w_mlp2', 'new_m_w_ada', 'new_m_b_ada', 'new_m_norm1_w', 'new_m_norm2_w', 'new_m_w_in', 'new_m_q_norm_w', 'new_m_k_norm_w', 'new_m_conv_w', 'new_m_conv_b', 'new_m_A_log', 'new_m_dt_bias', 'new_m_ssd_D', 'new_m_ssd_norm_w', 'new_m_w_attn_out', 'new_m_w_ssd_out', 'new_m_w_o', 'new_m_w_mlp1', 'new_m_w_mlp2', 'new_v_w_ada', 'new_v_b_ada', 'new_v_norm1_w', 'new_v_norm2_w', 'new_v_w_in', 'new_v_q_norm_w', 'new_v_k_norm_w', 'new_v_conv_w', 'new_v_conv_b', 'new_v_A_log', 'new_v_dt_bias', 'new_v_ssd_D', 'new_v_ssd_norm_w', 'new_v_w_attn_out', 'new_v_w_ssd_out', 'new_v_w_o', 'new_v_w_mlp1', 'new_v_w_mlp2']
TWIN_LEAF_KINDS = {'loss': 'loss', 'grad_x': 'grad_x', 'grad_w_ada': 'grad_w', 'grad_b_ada': 'grad_w', 'grad_norm1_w': 'grad_w', 'grad_norm2_w': 'grad_w', 'grad_w_in': 'grad_w', 'grad_q_norm_w': 'grad_w', 'grad_k_norm_w': 'grad_w', 'grad_conv_w': 'grad_w', 'grad_conv_b': 'grad_w', 'grad_A_log': 'grad_w', 'grad_dt_bias': 'grad_w', 'grad_ssd_D': 'grad_w', 'grad_ssd_norm_w': 'grad_w', 'grad_w_attn_out': 'grad_w', 'grad_w_ssd_out': 'grad_w', 'grad_w_o': 'grad_w', 'grad_w_mlp1': 'grad_w', 'grad_w_mlp2': 'grad_w', 'delta_w_ada': 'delta_w', 'delta_b_ada': 'delta_w', 'delta_norm1_w': 'delta_w', 'delta_norm2_w': 'delta_w', 'delta_w_in': 'delta_w', 'delta_q_norm_w': 'delta_w', 'delta_k_norm_w': 'delta_w', 'delta_conv_w': 'delta_w', 'delta_conv_b': 'delta_w', 'delta_A_log': 'delta_w', 'delta_dt_bias': 'delta_w', 'delta_ssd_D': 'delta_w', 'delta_ssd_norm_w': 'delta_w', 'delta_w_attn_out': 'delta_w', 'delta_w_ssd_out': 'delta_w', 'delta_w_o': 'delta_w', 'delta_w_mlp1': 'delta_w', 'delta_w_mlp2': 'delta_w', 'new_m_w_ada': 'new_m', 'new_m_b_ada': 'new_m', 'new_m_norm1_w': 'new_m', 'new_m_norm2_w': 'new_m', 'new_m_w_in': 'new_m', 'new_m_q_norm_w': 'new_m', 'new_m_k_norm_w': 'new_m', 'new_m_conv_w': 'new_m', 'new_m_conv_b': 'new_m', 'new_m_A_log': 'new_m', 'new_m_dt_bias': 'new_m', 'new_m_ssd_D': 'new_m', 'new_m_ssd_norm_w': 'new_m', 'new_m_w_attn_out': 'new_m', 'new_m_w_ssd_out': 'new_m', 'new_m_w_o': 'new_m', 'new_m_w_mlp1': 'new_m', 'new_m_w_mlp2': 'new_m', 'new_v_w_ada': 'new_v', 'new_v_b_ada': 'new_v', 'new_v_norm1_w': 'new_v', 'new_v_norm2_w': 'new_v', 'new_v_w_in': 'new_v', 'new_v_q_norm_w': 'new_v', 'new_v_k_norm_w': 'new_v', 'new_v_conv_w': 'new_v', 'new_v_conv_b': 'new_v', 'new_v_A_log': 'new_v', 'new_v_dt_bias': 'new_v', 'new_v_ssd_D': 'new_v', 'new_v_ssd_norm_w': 'new_v', 'new_v_w_attn_out': 'new_v', 'new_v_w_ssd_out': 'new_v', 'new_v_w_o': 'new_v', 'new_v_w_mlp1': 'new_v', 'new_v_w_mlp2': 'new_v'}


def _forward(args):
    return _fwd_reference(*[args[k] for k in FWD_PARAMS])


def _output_shape():
    out = _jax.eval_shape(lambda: _forward(_fwd_setup_inputs(0)))
    return out.shape, out.dtype

N_MICROBATCH = 1
ADAM_LR = 0.001
ADAM_B1 = 0.9
ADAM_B2 = 0.999
ADAM_EPS = 1e-08
ADAM_WD = 0.01
ADAM_STEP = 10
PER_EXAMPLE_BATCH_AXIS = {'x': 0, 'c': 0, 'loss_target': 0}
SHARED_INPUTS = []
_WEIGHT_DTYPES = {'w_ada': _jnp.float32, 'b_ada': _jnp.float32, 'norm1_w': _jnp.float32, 'norm2_w': _jnp.float32, 'w_in': _jnp.float32, 'q_norm_w': _jnp.float32, 'k_norm_w': _jnp.float32, 'conv_w': _jnp.float32, 'conv_b': _jnp.float32, 'A_log': _jnp.float32, 'dt_bias': _jnp.float32, 'ssd_D': _jnp.float32, 'ssd_norm_w': _jnp.float32, 'w_attn_out': _jnp.float32, 'w_ssd_out': _jnp.float32, 'w_o': _jnp.float32, 'w_mlp1': _jnp.float32, 'w_mlp2': _jnp.float32}
MOMENT_SCALE = {'w_ada': 6.543238e+00, 'b_ada': 1.404521e+01, 'norm1_w': 1.380645e-01, 'norm2_w': 2.442758e+01, 'w_in': 1.447487e-01, 'q_norm_w': 4.328466e-02, 'k_norm_w': 4.398303e-02, 'conv_w': 1.556356e-01, 'conv_b': 4.307987e-01, 'A_log': 8.747578e-01, 'dt_bias': 1.169901e-01, 'ssd_D': 2.384434e-01, 'ssd_norm_w': 1.547028e+00, 'w_attn_out': 4.290927e-01, 'w_ssd_out': 3.894889e-01, 'w_o': 5.253425e-01, 'w_mlp1': 6.662097e-01, 'w_mlp2': 2.730119e+00}


def _to_microbatches(a, axis):
    t = _jnp.moveaxis(a, axis, 0)
    t = t.reshape((N_MICROBATCH, t.shape[0] // N_MICROBATCH) + t.shape[1:])
    return _jnp.moveaxis(t, 1, axis + 1)


def setup_inputs(seed: int = 0) -> dict:
    inp = _fwd_setup_inputs(seed)
    key = _jax.random.fold_in(_jax.random.key(seed), 7919)
    shape, _ = _output_shape()
    out = dict(inp)
    out["loss_target"] = _jax.random.normal(_jax.random.fold_in(key, 0), shape, _jnp.float32)
    for i, name in enumerate(TWIN_WEIGHTS):
        w = inp[name].astype(_jnp.float32)
        if MOMENT_SCALE is None:
            s = _jnp.sqrt(_jnp.mean(_jnp.square(w)) + 1e-30)
        else:
            s = MOMENT_SCALE[name]
        km, kv = _jax.random.split(_jax.random.fold_in(key, i + 1))
        out[name] = w
        out["m_" + name] = s * _jax.random.normal(km, w.shape, _jnp.float32)
        out["v_" + name] = (s * s) * _jax.random.uniform(kv, w.shape, _jnp.float32, 0.5, 1.5)
    if N_MICROBATCH > 1:
        for name, axis in PER_EXAMPLE_BATCH_AXIS.items():
            out[name] = _to_microbatches(out[name], axis)
    return {'x': out['x'], 'c': out['c'], 'w_ada': out['w_ada'], 'b_ada': out['b_ada'], 'norm1_w': out['norm1_w'], 'norm2_w': out['norm2_w'], 'w_in': out['w_in'], 'q_norm_w': out['q_norm_w'], 'k_norm_w': out['k_norm_w'], 'conv_w': out['conv_w'], 'conv_b': out['conv_b'], 'A_log': out['A_log'], 'dt_bias': out['dt_bias'], 'ssd_D': out['ssd_D'], 'ssd_norm_w': out['ssd_norm_w'], 'w_attn_out': out['w_attn_out'], 'w_ssd_out': out['w_ssd_out'], 'w_o': out['w_o'], 'w_mlp1': out['w_mlp1'], 'w_mlp2': out['w_mlp2'], 'loss_target': out['loss_target'], 'm_w_ada': out['m_w_ada'], 'm_b_ada': out['m_b_ada'], 'm_norm1_w': out['m_norm1_w'], 'm_norm2_w': out['m_norm2_w'], 'm_w_in': out['m_w_in'], 'm_q_norm_w': out['m_q_norm_w'], 'm_k_norm_w': out['m_k_norm_w'], 'm_conv_w': out['m_conv_w'], 'm_conv_b': out['m_conv_b'], 'm_A_log': out['m_A_log'], 'm_dt_bias': out['m_dt_bias'], 'm_ssd_D': out['m_ssd_D'], 'm_ssd_norm_w': out['m_ssd_norm_w'], 'm_w_attn_out': out['m_w_attn_out'], 'm_w_ssd_out': out['m_w_ssd_out'], 'm_w_o': out['m_w_o'], 'm_w_mlp1': out['m_w_mlp1'], 'm_w_mlp2': out['m_w_mlp2'], 'v_w_ada': out['v_w_ada'], 'v_b_ada': out['v_b_ada'], 'v_norm1_w': out['v_norm1_w'], 'v_norm2_w': out['v_norm2_w'], 'v_w_in': out['v_w_in'], 'v_q_norm_w': out['v_q_norm_w'], 'v_k_norm_w': out['v_k_norm_w'], 'v_conv_w': out['v_conv_w'], 'v_conv_b': out['v_conv_b'], 'v_A_log': out['v_A_log'], 'v_dt_bias': out['v_dt_bias'], 'v_ssd_D': out['v_ssd_D'], 'v_ssd_norm_w': out['v_ssd_norm_w'], 'v_w_attn_out': out['v_w_attn_out'], 'v_w_ssd_out': out['v_w_ssd_out'], 'v_w_o': out['v_w_o'], 'v_w_mlp1': out['v_w_mlp1'], 'v_w_mlp2': out['v_w_mlp2']}


def _loss(weights, diff, rest, loss_target):
    with _jax.named_scope("forward"):
        args = {**rest, TWIN_DIFF_INPUT: diff, **{k: w.astype(_WEIGHT_DTYPES[k]) for k, w in weights.items()}}
        y = _forward(args)
    with _jax.named_scope("loss_head"):
        err = _jnp.square(y.astype(_jnp.float32) - loss_target)
        return 0.5 * _jnp.sum(_jnp.mean(err, axis=-1)) if err.ndim else 0.5 * err


def _adamw(w, g, m, v):
    m = ADAM_B1 * m + (1.0 - ADAM_B1) * g
    v = ADAM_B2 * v + (1.0 - ADAM_B2) * _jnp.square(g)
    m_hat = m / (1.0 - ADAM_B1 ** ADAM_STEP)
    v_hat = v / (1.0 - ADAM_B2 ** ADAM_STEP)
    delta = -ADAM_LR * (m_hat / (_jnp.sqrt(v_hat) + ADAM_EPS) + ADAM_WD * w)
    return delta, m, v


def reference(x, c, w_ada, b_ada, norm1_w, norm2_w, w_in, q_norm_w, k_norm_w, conv_w, conv_b, A_log, dt_bias, ssd_D, ssd_norm_w, w_attn_out, w_ssd_out, w_o, w_mlp1, w_mlp2, loss_target, m_w_ada, m_b_ada, m_norm1_w, m_norm2_w, m_w_in, m_q_norm_w, m_k_norm_w, m_conv_w, m_conv_b, m_A_log, m_dt_bias, m_ssd_D, m_ssd_norm_w, m_w_attn_out, m_w_ssd_out, m_w_o, m_w_mlp1, m_w_mlp2, v_w_ada, v_b_ada, v_norm1_w, v_norm2_w, v_w_in, v_q_norm_w, v_k_norm_w, v_conv_w, v_conv_b, v_A_log, v_dt_bias, v_ssd_D, v_ssd_norm_w, v_w_attn_out, v_w_ssd_out, v_w_o, v_w_mlp1, v_w_mlp2):
    given = dict(x=x, c=c, w_ada=w_ada, b_ada=b_ada, norm1_w=norm1_w, norm2_w=norm2_w, w_in=w_in, q_norm_w=q_norm_w, k_norm_w=k_norm_w, conv_w=conv_w, conv_b=conv_b, A_log=A_log, dt_bias=dt_bias, ssd_D=ssd_D, ssd_norm_w=ssd_norm_w, w_attn_out=w_attn_out, w_ssd_out=w_ssd_out, w_o=w_o, w_mlp1=w_mlp1, w_mlp2=w_mlp2, loss_target=loss_target, m_w_ada=m_w_ada, m_b_ada=m_b_ada, m_norm1_w=m_norm1_w, m_norm2_w=m_norm2_w, m_w_in=m_w_in, m_q_norm_w=m_q_norm_w, m_k_norm_w=m_k_norm_w, m_conv_w=m_conv_w, m_conv_b=m_conv_b, m_A_log=m_A_log, m_dt_bias=m_dt_bias, m_ssd_D=m_ssd_D, m_ssd_norm_w=m_ssd_norm_w, m_w_attn_out=m_w_attn_out, m_w_ssd_out=m_w_ssd_out, m_w_o=m_w_o, m_w_mlp1=m_w_mlp1, m_w_mlp2=m_w_mlp2, v_w_ada=v_w_ada, v_b_ada=v_b_ada, v_norm1_w=v_norm1_w, v_norm2_w=v_norm2_w, v_w_in=v_w_in, v_q_norm_w=v_q_norm_w, v_k_norm_w=v_k_norm_w, v_conv_w=v_conv_w, v_conv_b=v_conv_b, v_A_log=v_A_log, v_dt_bias=v_dt_bias, v_ssd_D=v_ssd_D, v_ssd_norm_w=v_ssd_norm_w, v_w_attn_out=v_w_attn_out, v_w_ssd_out=v_w_ssd_out, v_w_o=v_w_o, v_w_mlp1=v_w_mlp1, v_w_mlp2=v_w_mlp2)
    weights = {n: given[n] for n in TWIN_WEIGHTS}
    shared = {n: given[n] for n in SHARED_INPUTS}
    per_example = {n: given[n] for n in ['x', 'c']}
    grad_fn = _jax.value_and_grad(_loss, argnums=(0, 1))

    def one_microbatch(ex, loss_target):
        ex = dict(ex)
        diff = ex.pop(TWIN_DIFF_INPUT)
        return grad_fn(weights, diff, {**shared, **ex}, loss_target)

    if N_MICROBATCH == 1:
        loss, (grad_w, grad_x) = one_microbatch(per_example, given["loss_target"])
    else:
        def body(carry, xs):
            loss_sum, grad_sum = carry
            l_k, (gw_k, gx_k) = one_microbatch(xs[0], xs[1])
            with _jax.named_scope("update"):
                return (loss_sum + l_k, _jax.tree.map(_jnp.add, grad_sum, gw_k)), gx_k

        init = (_jnp.zeros((), _jnp.float32), _jax.tree.map(_jnp.zeros_like, weights))
        (loss, grad_w), grad_x = _jax.lax.scan(body, init, (per_example, given["loss_target"]))
    with _jax.named_scope("update"):
        delta_w, new_m, new_v = {}, {}, {}
        for n in TWIN_WEIGHTS:
            delta_w[n], new_m[n], new_v[n] = _adamw(weights[n], grad_w[n], given["m_" + n], given["v_" + n])
    return (loss, grad_x, *[grad_w[n] for n in TWIN_WEIGHTS], *[delta_w[n] for n in TWIN_WEIGHTS],
            *[new_m[n] for n in TWIN_WEIGHTS], *[new_v[n] for n in TWIN_WEIGHTS])
```

```python
import functools
import math

import jax
import jax.numpy as jnp
from jax import lax
from jax.experimental import pallas as pl
from jax.experimental.pallas import tpu as pltpu

F32 = jnp.float32
MMD = jnp.bfloat16
EPS = 1e-6
NEG = -1e30
MIB = 1024 * 1024
VMEM_BIG = 56 * MIB
VMEM_MID = 40 * MIB

GRID_W = 64
N_Q_HEADS, N_KV_HEADS, HEAD_DIM = 16, 4, 64
ROPE_THETA = 10000.0
SSD_HEADS, SSD_GROUPS, SSD_P, SSD_N, CHUNK = 32, 4, 64, 128, 128
HPG = SSD_HEADS // SSD_GROUPS
D_CONV = 5
ADAM_LR, ADAM_B1, ADAM_B2, ADAM_EPS, ADAM_WD, ADAM_STEP = 0.001, 0.9, 0.999, 1e-08, 0.01, 10

Z0, GA0, GS0, XS0, B0, C0, Q0, K0, V0, DT0, PW = 0, 2048, 3072, 4096, 6144, 6656, 7168, 8192, 8448, 8704, 8832

MESH = pl.DeviceIdType.MESH
NT = (((1,), (1,)), ((), ()))
TN = (((0,), (0,)), ((), ()))


def _cp(sem=None, vmem=VMEM_MID):
    return pltpu.CompilerParams(dimension_semantics=sem, vmem_limit_bytes=vmem)


def _tile(n, pref):
    t = min(n, pref)
    while n % t:
        t //= 2
    return t


def _dot(a, b, dims=None):
    if dims is None:
        return jnp.dot(a, b, preferred_element_type=F32)
    return lax.dot_general(a, b, dims, preferred_element_type=F32)


def _dot_hi(a, b):
    return jnp.dot(a, b, precision=lax.Precision.HIGHEST, preferred_element_type=F32)


def _sigmoid(x):
    return jax.nn.sigmoid(x)


def _mm(a, b, *, name, outs, nt=False, extras=(), epi=None, tm=512, tn=512, n=None, b_outer=False, vmem=VMEM_MID):
    m, k = a.shape
    if n is None:
        n = b.shape[0] if nt else b.shape[1]
    tm, tn = _tile(m, tm), _tile(n, tn)
    gi, gj = m // tm, n // tn
    if b_outer:
        grid = (gj, gi)
        ij = lambda p, q: (q, p)
    else:
        grid = (gi, gj)
        ij = lambda p, q: (p, q)
    a_spec = pl.BlockSpec((tm, k), lambda p, q: (ij(p, q)[0], 0))
    if nt:
        b_spec = pl.BlockSpec((tn, k), lambda p, q: (ij(p, q)[1], 0))
    else:
        b_spec = pl.BlockSpec((k, tn), lambda p, q: (0, ij(p, q)[1]))
    e_specs = []
    for arr, kind, off in extras:
        ob = off // tn
        assert off % tn == 0
        if kind == "tile":
            e_specs.append(pl.BlockSpec((tm, tn), lambda p, q, ob=ob: (ij(p, q)[0], ob + ij(p, q)[1])))
        else:
            e_specs.append(pl.BlockSpec((1, tn), lambda p, q, ob=ob: (0, ob + ij(p, q)[1])))
    ne = len(extras)

    def body(a_ref, b_ref, *rest):
        acc = _dot(a_ref[...], b_ref[...], NT if nt else None)
        res = epi(acc, *[e[...] for e in rest[:ne]]) if epi is not None else (acc,)
        for o_ref, r in zip(rest[ne:], res):
            o_ref[...] = r.astype(o_ref.dtype)

    out = pl.pallas_call(
        body, name=name, grid=grid,
        in_specs=[a_spec, b_spec] + e_specs,
        out_specs=[pl.BlockSpec((tm, tn), lambda p, q: ij(p, q)) for _ in outs],
        out_shape=[jax.ShapeDtypeStruct((m, n), dt) for dt in outs],
        compiler_params=_cp(("arbitrary", "arbitrary"), vmem),
    )(a, b, *[e[0] for e in extras])
    return out if len(outs) > 1 else out[0]


def _mm_tn(a, g, *, name, tk=512, tn=1024, tmm=1024, vmem=VMEM_MID):
    m, k = a.shape
    n = g.shape[1]
    tk, tn, tmm = _tile(k, tk), _tile(n, tn), _tile(m, tmm)

    def body(a_ref, g_ref, o_ref):
        p = _dot(a_ref[...], g_ref[...], TN)

        @pl.when(pl.program_id(2) == 0)
        def _():
            o_ref[...] = p

        @pl.when(pl.program_id(2) > 0)
        def _():
            o_ref[...] += p

    return pl.pallas_call(
        body, name=name, grid=(k // tk, n // tn, m // tmm),
        in_specs=[pl.BlockSpec((tmm, tk), lambda i, j, r: (r, i)), pl.BlockSpec((tmm, tn), lambda i, j, r: (r, j))],
        out_specs=pl.BlockSpec((tk, tn), lambda i, j, r: (i, j)),
        out_shape=jax.ShapeDtypeStruct((k, n), F32),
        compiler_params=_cp(("arbitrary", "arbitrary", "arbitrary"), vmem),
    )(a, g)


def _adamw(w, g, m, v, *, name):
    r, c = w.shape
    tr = _tile(r, 256) if r % 8 == 0 else r

    def body(w_ref, g_ref, m_ref, v_ref, d_ref, nm_ref, nv_ref):
        gg = g_ref[...]
        nm = ADAM_B1 * m_ref[...] + (1.0 - ADAM_B1) * gg
        nv = ADAM_B2 * v_ref[...] + (1.0 - ADAM_B2) * jnp.square(gg)
        m_hat = nm / (1.0 - ADAM_B1 ** ADAM_STEP)
        v_hat = nv / (1.0 - ADAM_B2 ** ADAM_STEP)
        d_ref[...] = -ADAM_LR * (m_hat / (jnp.sqrt(v_hat) + ADAM_EPS) + ADAM_WD * w_ref[...])
        nm_ref[...] = nm
        nv_ref[...] = nv

    spec = pl.BlockSpec((tr, c), lambda i: (i, 0))
    return pl.pallas_call(
        body, name=name, grid=(r // tr,), in_specs=[spec] * 4, out_specs=[spec] * 3,
        out_shape=[jax.ShapeDtypeStruct((r, c), F32)] * 3, compiler_params=_cp(("arbitrary",)),
    )(w, g, m, v)


def _rows_sum(a, groups, *, name):
    r = a.shape[0] // groups

    def body(a_ref, o_ref):
        acc = a_ref[0:r, :]
        for d in range(1, groups):
            acc = acc + a_ref[d * r:(d + 1) * r, :]
        o_ref[...] = acc

    return pl.pallas_call(body, name=name, out_shape=jax.ShapeDtypeStruct((r, a.shape[1]), F32))(a)


def _silu_cast(a, *, name):
    def body(a_ref, o_ref):
        x = a_ref[...]
        o_ref[...] = (x * _sigmoid(x)).astype(o_ref.dtype)

    return pl.pallas_call(body, name=name, out_shape=jax.ShapeDtypeStruct(a.shape, MMD))(a)


def _sumsq(a, *, name):
    m, n = a.shape
    tm = _tile(m, 512)

    def body(a_ref, o_ref):
        x = a_ref[...]
        p = jnp.sum(jnp.sum(x * x, axis=1, keepdims=True), axis=0, keepdims=True)

        @pl.when(pl.program_id(0) == 0)
        def _():
            o_ref[...] = p

        @pl.when(pl.program_id(0) > 0)
        def _():
            o_ref[...] += p

    return pl.pallas_call(
        body, name=name, grid=(m // tm,), in_specs=[pl.BlockSpec((tm, n), lambda i: (i, 0))],
        out_specs=pl.BlockSpec((1, 1), lambda i: (0, 0)), out_shape=jax.ShapeDtypeStruct((1, 1), F32),
        compiler_params=_cp(("arbitrary",)),
    )(a)


def _acc_rows(o_ref, p, first):
    @pl.when(first)
    def _():
        o_ref[...] = p

    @pl.when(jnp.logical_not(first))
    def _():
        o_ref[...] += p


def _ln_mod(x, w, scale, shift, *, name):
    s, d = x.shape
    tm = _tile(s, 512)

    def body(x_ref, w_ref, sc_ref, sh_ref, o_ref):
        xv = x_ref[...]
        r = lax.rsqrt(jnp.mean(xv * xv, axis=-1, keepdims=True) + EPS)
        o_ref[...] = ((xv * r) * w_ref[...] * (1.0 + sc_ref[...]) + sh_ref[...]).astype(o_ref.dtype)

    row = pl.BlockSpec((1, d), lambda i: (0, 0))
    big = pl.BlockSpec((tm, d), lambda i: (i, 0))
    return pl.pallas_call(
        body, name=name, grid=(s // tm,), in_specs=[big, row, row, row], out_specs=big,
        out_shape=jax.ShapeDtypeStruct((s, d), MMD), compiler_params=_cp(("arbitrary",)),
    )(x, w, scale, shift)


def _ln_mod_bwd(dh, x, w, scale, dres, *, name):
    s, d = x.shape
    tm = _tile(s, 512)

    def body(dh_ref, x_ref, w_ref, sc_ref, dres_ref, dx_ref, dsh_ref, dsc_ref, dw_ref):
        xv = x_ref[...]
        dhv = dh_ref[...].astype(F32)
        r = lax.rsqrt(jnp.mean(xv * xv, axis=-1, keepdims=True) + EPS)
        nv = xv * r
        wv = w_ref[...]
        g1 = 1.0 + sc_ref[...]
        dn = dhv * (wv * g1)
        dx_ref[...] = dres_ref[...] + r * (dn - nv * jnp.mean(dn * nv, axis=-1, keepdims=True))
        first = pl.program_id(0) == 0
        _acc_rows(dsh_ref, jnp.sum(dhv, axis=0, keepdims=True), first)
        _acc_rows(dsc_ref, jnp.sum(dhv * nv * wv, axis=0, keepdims=True), first)
        _acc_rows(dw_ref, jnp.sum(dhv * nv * g1, axis=0, keepdims=True), first)

    row = pl.BlockSpec((1, d), lambda i: (0, 0))
    big = pl.BlockSpec((tm, d), lambda i: (i, 0))
    return pl.pallas_call(
        body, name=name, grid=(s // tm,), in_specs=[big, big, row, row, big], out_specs=[big, row, row, row],
        out_shape=[jax.ShapeDtypeStruct((s, d), F32)] + [jax.ShapeDtypeStruct((1, d), F32)] * 3,
        compiler_params=_cp(("arbitrary",)),
    )(dh, x, w, scale, dres)


def _gate_bwd(dy, u, gate, *, name):
    s, d = dy.shape
    tm = _tile(s, 512)

    def body(dy_ref, u_ref, g_ref, du_ref, dg_ref):
        dyv = dy_ref[...]
        du_ref[...] = (dyv * g_ref[...]).astype(du_ref.dtype)
        _acc_rows(dg_ref, jnp.sum(dyv * u_ref[...].astype(F32), axis=0, keepdims=True), pl.program_id(0) == 0)

    row = pl.BlockSpec((1, d), lambda i: (0, 0))
    big = pl.BlockSpec((tm, d), lambda i: (i, 0))
    return pl.pallas_call(
        body, name=name, grid=(s // tm,), in_specs=[big, big, row], out_specs=[big, row],
        out_shape=[jax.ShapeDtypeStruct((s, d), MMD), jax.ShapeDtypeStruct((1, d), F32)],
        compiler_params=_cp(("arbitrary",)),
    )(dy, u, gate)


def _seg64(v, e):
    hi = v.astype(jnp.bfloat16)
    lo = (v - hi.astype(F32)).astype(jnp.bfloat16)
    return _dot(hi, e) + _dot(lo, e)


def _rope_tables(s):
    rows = s // GRID_W
    pos_row = jnp.repeat(jnp.arange(rows, dtype=jnp.int32), GRID_W).astype(F32)
    pos_col = jnp.tile(jnp.arange(GRID_W, dtype=jnp.int32), rows).astype(F32)
    axis_dim = HEAD_DIM // 2
    inv_freq = ROPE_THETA ** (-jnp.arange(0, axis_dim, 2, dtype=F32) / axis_dim)
    ang_r = pos_row[:, None] * inv_freq[None, :]
    ang_c = pos_col[:, None] * inv_freq[None, :]
    zero = jnp.zeros_like(ang_r)
    cos = jnp.concatenate([jnp.cos(ang_r), jnp.cos(ang_r), jnp.cos(ang_c), jnp.cos(ang_c)], axis=1)
    s_a = jnp.concatenate([-jnp.sin(ang_r), zero, -jnp.sin(ang_c), zero], axis=1)
    s_b = jnp.concatenate([zero, jnp.sin(ang_r), zero, jnp.sin(ang_c)], axis=1)
    return [jnp.tile(t, (1, 2)) for t in (cos, s_a, s_b)]


def _e128():
    i = jnp.arange(128)
    return (i[:, None] // 64 == i[None, :] // 64).astype(jnp.bfloat16)


QKW = N_Q_HEADS * HEAD_DIM + N_KV_HEADS * HEAD_DIM


def _qk_fwd(proj, wrow, scrow, tabs, *, name):
    s = proj.shape[0]
    tm = _tile(s, 512)

    def body(x_ref, w_ref, sc_ref, cos_ref, sa_ref, sb_ref, e_ref, o_ref):
        u = x_ref[...].astype(F32)
        r = lax.rsqrt(_seg64(u * u, e_ref[...]) * (1.0 / HEAD_DIM) + EPS)
        nv = (u * r) * w_ref[...]
        ro = nv * cos_ref[...] + pltpu.roll(nv, 112, 1) * sa_ref[...] + pltpu.roll(nv, 16, 1) * sb_ref[...]
        o_ref[...] = (ro * sc_ref[...]).astype(o_ref.dtype)

    tab = pl.BlockSpec((tm, 128), lambda i, j: (i, 0))
    row = pl.BlockSpec((1, 128), lambda i, j: (0, j))
    return pl.pallas_call(
        body, name=name, grid=(s // tm, QKW // 128),
        in_specs=[pl.BlockSpec((tm, 128), lambda i, j: (i, Q0 // 128 + j)), row, row, tab, tab, tab,
                  pl.BlockSpec((128, 128), lambda i, j: (0, 0))],
        out_specs=pl.BlockSpec((tm, 128), lambda i, j: (i, j)),
        out_shape=jax.ShapeDtypeStruct((s, QKW), MMD), compiler_params=_cp(("arbitrary", "arbitrary")),
    )(proj, wrow, scrow, *tabs, _e128())


def _qk_bwd(dqk, proj, wrow, scrow, tabs, *, name):
    s = proj.shape[0]
    tm = _tile(s, 512)

    def body(d_ref, x_ref, w_ref, sc_ref, cos_ref, sa_ref, sb_ref, e_ref, du_ref, dw_ref):
        e = e_ref[...]
        d = d_ref[...] * sc_ref[...]
        dn = d * cos_ref[...] + pltpu.roll(d * sa_ref[...], 16, 1) + pltpu.roll(d * sb_ref[...], 112, 1)
        u = x_ref[...].astype(F32)
        r = lax.rsqrt(_seg64(u * u, e) * (1.0 / HEAD_DIM) + EPS)
        uh = u * r
        _acc_rows(dw_ref, jnp.sum(dn * uh, axis=0, keepdims=True), pl.program_id(1) == 0)
        dnw = dn * w_ref[...]
        du_ref[...] = (r * (dnw - uh * (_seg64(dnw * uh, e) * (1.0 / HEAD_DIM)))).astype(du_ref.dtype)

    tab = pl.BlockSpec((tm, 128), lambda j, i: (i, 0))
    row = pl.BlockSpec((1, 128), lambda j, i: (0, j))
    return pl.pallas_call(
        body, name=name, grid=(QKW // 128, s // tm),
        in_specs=[pl.BlockSpec((tm, 128), lambda j, i: (i, j)), pl.BlockSpec((tm, 128), lambda j, i: (i, Q0 // 128 + j)),
                  row, row, tab, tab, tab, pl.BlockSpec((128, 128), lambda j, i: (0, 0))],
        out_specs=[pl.BlockSpec((tm, 128), lambda j, i: (i, j)), row],
        out_shape=[jax.ShapeDtypeStruct((s, QKW), MMD), jax.ShapeDtypeStruct((1, QKW), F32)],
        compiler_params=_cp(("arbitrary", "arbitrary")),
    )(dqk, proj, wrow, scrow, *tabs, _e128())


REP = N_Q_HEADS // N_KV_HEADS


def _flash_fwd(qk_h, v_h, *, name):
    s = qk_h.shape[1]
    tq, tk = _tile(s, 256), _tile(s, 512)
    nk = s // tk

    def body(q_ref, k_ref, v_ref, o_ref, lse_ref, m_ref, l_ref, acc_ref):
        j = pl.program_id(2)

        @pl.when(j == 0)
        def _():
            m_ref[...] = jnp.full_like(m_ref, NEG)
            l_ref[...] = jnp.zeros_like(l_ref)
            acc_ref[...] = jnp.zeros_like(acc_ref)

        q = q_ref[...].reshape(REP * tq, HEAD_DIM)
        sc = _dot(q, k_ref[0], NT)
        m_prev = m_ref[...]
        m_new = jnp.maximum(m_prev, jnp.max(sc, axis=-1, keepdims=True))
        p = jnp.exp(sc - m_new)
        alpha = jnp.exp(m_prev - m_new)
        l_ref[...] = alpha * l_ref[...] + jnp.sum(p, axis=-1, keepdims=True)
        acc_ref[...] = alpha * acc_ref[...] + _dot(p.astype(MMD), v_ref[0])
        m_ref[...] = m_new

        @pl.when(j == nk - 1)
        def _():
            l = l_ref[...]
            o_ref[...] = (acc_ref[...] / l).reshape(REP, tq, HEAD_DIM).astype(o_ref.dtype)
            lse_ref[...] = (m_ref[...] + jnp.log(l)).reshape(REP, tq, 1)

    return pl.pallas_call(
        body, name=name, grid=(N_KV_HEADS, s // tq, nk),
        in_specs=[pl.BlockSpec((REP, tq, HEAD_DIM), lambda g, i, j: (g, i, 0)),
                  pl.BlockSpec((1, tk, HEAD_DIM), lambda g, i, j: (N_Q_HEADS + g, j, 0)),
                  pl.BlockSpec((1, tk, HEAD_DIM), lambda g, i, j: (g, j, 0))],
        out_specs=[pl.BlockSpec((REP, tq, HEAD_DIM), lambda g, i, j: (g, i, 0)),
                   pl.BlockSpec((REP, tq, 1), lambda g, i, j: (g, i, 0))],
        out_shape=[jax.ShapeDtypeStruct((N_Q_HEADS, s, HEAD_DIM), MMD), jax.ShapeDtypeStruct((N_Q_HEADS, s, 1), F32)],
        scratch_shapes=[pltpu.VMEM((REP * tq, 1), F32), pltpu.VMEM((REP * tq, 1), F32), pltpu.VMEM((REP * tq, HEAD_DIM), F32)],
        compiler_params=_cp(("arbitrary", "arbitrary", "arbitrary")),
    )(qk_h, qk_h, v_h)


def _rowdot(a, b, *, name):
    h, s, d = a.shape
    ts = _tile(s, 1024)

    def body(a_ref, b_ref, o_ref):
        o_ref[...] = jnp.sum(a_ref[...].astype(F32) * b_ref[...].astype(F32), axis=-1, keepdims=True)

    spec = pl.BlockSpec((1, ts, d), lambda i, j: (i, j, 0))
    return pl.pallas_call(
        body, name=name, grid=(h, s // ts), in_specs=[spec, spec], out_specs=pl.BlockSpec((1, ts, 1), lambda i, j: (i, j, 0)),
        out_shape=jax.ShapeDtypeStruct((h, s, 1), F32), compiler_params=_cp(("arbitrary", "arbitrary")),
    )(a, b)


def _flash_bwd(qk_h, v_h, do_h, lse, delta, *, name):
    s = qk_h.shape[1]
    tq, tk = _tile(s, 256), _tile(s, 512)
    nq = s // tq

    def body(q_ref, k_ref, v_ref, do_ref, lse_ref, dl_ref, dq_ref, dk_ref, dv_ref, dk_acc, dv_acc):
        j, i = pl.program_id(1), pl.program_id(2)

        @pl.when(i == 0)
        def _():
            dk_acc[...] = jnp.zeros_like(dk_acc)
            dv_acc[...] = jnp.zeros_like(dv_acc)

        q = q_ref[...].reshape(REP * tq, HEAD_DIM)
        do = do_ref[...].reshape(REP * tq, HEAD_DIM)
        k = k_ref[0]
        p = jnp.exp(_dot(q, k, NT) - lse_ref[...].reshape(REP * tq, 1))
        dv_acc[...] += _dot(p.astype(MMD), do, TN)
        dp = _dot(do, v_ref[0], NT)
        ds = (p * (dp - dl_ref[...].reshape(REP * tq, 1))).astype(MMD)
        dk_acc[...] += _dot(ds, q, TN)
        dqc = _dot(ds, k).reshape(REP, tq, HEAD_DIM)
        rows = pl.ds(pl.multiple_of(i * tq, tq), tq)

        @pl.when(j == 0)
        def _():
            dq_ref[:, rows, :] = dqc

        @pl.when(j > 0)
        def _():
            dq_ref[:, rows, :] += dqc

        @pl.when(i == nq - 1)
        def _():
            dk_ref[0] = dk_acc[...]
            dv_ref[0] = dv_acc[...]

    qspec = pl.BlockSpec((REP, tq, HEAD_DIM), lambda g, j, i: (g, i, 0))
    sspec = pl.BlockSpec((REP, tq, 1), lambda g, j, i: (g, i, 0))
    kvout = pl.BlockSpec((1, tk, HEAD_DIM), lambda g, j, i: (g, j, 0))
    return pl.pallas_call(
        body, name=name, grid=(N_KV_HEADS, s // tk, nq),
        in_specs=[qspec, pl.BlockSpec((1, tk, HEAD_DIM), lambda g, j, i: (N_Q_HEADS + g, j, 0)), kvout, qspec, sspec, sspec],
        out_specs=[pl.BlockSpec((REP, s, HEAD_DIM), lambda g, j, i: (g, 0, 0)), kvout, kvout],
        out_shape=[jax.ShapeDtypeStruct((N_Q_HEADS, s, HEAD_DIM), F32), jax.ShapeDtypeStruct((N_KV_HEADS, s, HEAD_DIM), F32),
                   jax.ShapeDtypeStruct((N_KV_HEADS, s, HEAD_DIM), F32)],
        scratch_shapes=[pltpu.VMEM((tk, HEAD_DIM), F32), pltpu.VMEM((tk, HEAD_DIM), F32)],
        compiler_params=_cp(("arbitrary", "arbitrary", "arbitrary"), VMEM_BIG),
    )(qk_h, qk_h, v_h, do_h, lse, delta)


HALO = 8
CONV_W = 2048 + 2 * SSD_GROUPS * SSD_N


def _shifted(win, off, r):
    return pltpu.roll(win, (r + 2 * HALO - off) % (r + 2 * HALO), 0)[0:r]


def _conv_fwd(proj, w8, brow, *, name):
    s = proj.shape[0]
    cb = 256
    r = _tile(s, 512)

    def body(x_ref, w_ref, b_ref, o_ref, pad_ref):
        zeros = jnp.zeros((HALO, cb), F32)
        pad_ref[0:HALO, :] = zeros
        pad_ref[s + HALO:s + 2 * HALO, :] = zeros

        def fill(i, carry):
            st = pl.multiple_of(i * r, r)
            pad_ref[pl.ds(st + HALO, r), :] = x_ref[pl.ds(st, r), :].astype(F32)
            return carry

        lax.fori_loop(0, s // r, fill, 0)
        wv = w_ref[...]
        bv = b_ref[...]

        def step(i, carry):
            st = pl.multiple_of(i * r, r)
            win = pad_ref[pl.ds(st, r + 2 * HALO), :]
            acc = bv + wv[0:1, :] * _shifted(win, HALO - 2, r)
            for t in range(1, D_CONV):
                acc = acc + wv[t:t + 1, :] * _shifted(win, HALO - 2 + t, r)
            o_ref[pl.ds(st, r), :] = (acc * _sigmoid(acc)).astype(o_ref.dtype)
            return carry

        lax.fori_loop(0, s // r, step, 0)

    return pl.pallas_call(
        body, name=name, grid=(CONV_W // cb,),
        in_specs=[pl.BlockSpec((s, cb), lambda j: (0, XS0 // cb + j)), pl.BlockSpec((8, cb), lambda j: (0, j)),
                  pl.BlockSpec((1, cb), lambda j: (0, j))],
        out_specs=pl.BlockSpec((s, cb), lambda j: (0, j)),
        out_shape=jax.ShapeDtypeStruct((s, CONV_W), MMD),
        scratch_shapes=[pltpu.VMEM((s + 2 * HALO, cb), F32)],
        compiler_params=_cp(("arbitrary",), VMEM_MID),
    )(proj, w8, brow)


def _conv_bwd(proj, ga, gb, w8, brow, *, name):
    s = proj.shape[0]
    cb = 128
    r = _tile(s, 512)

    def body(x_ref, ga_ref, gb_ref, w_ref, b_ref, dx_ref, dw_ref, db_ref, xpad, dpad):
        zeros = jnp.zeros((HALO, cb), F32)
        for ref in (xpad, dpad):
            ref[0:HALO, :] = zeros
            ref[s + HALO:s + 2 * HALO, :] = zeros

        def fill(i, carry):
            st = pl.multiple_of(i * r, r)
            xpad[pl.ds(st + HALO, r), :] = x_ref[pl.ds(st, r), :].astype(F32)
            return carry

        lax.fori_loop(0, s // r, fill, 0)
        wv = w_ref[...]
        bv = b_ref[...]

        def first(i, carry):
            st = pl.multiple_of(i * r, r)
            win = xpad[pl.ds(st, r + 2 * HALO), :]
            taps = [_shifted(win, HALO - 2 + t, r) for t in range(D_CONV)]
            u = bv
            for t in range(D_CONV):
                u = u + wv[t:t + 1, :] * taps[t]
            sg = _sigmoid(u)
            du = (ga_ref[pl.ds(st, r), :] + gb_ref[pl.ds(st, r), :]) * (sg * (1.0 + u * (1.0 - sg)))
            dpad[pl.ds(st + HALO, r), :] = du
            out = [carry[0] + jnp.sum(du, axis=0, keepdims=True)]
            for t in range(D_CONV):
                out.append(carry[1 + t] + jnp.sum(du * taps[t], axis=0, keepdims=True))
            return tuple(out)

        sums = lax.fori_loop(0, s // r, first, tuple(jnp.zeros((1, cb), F32) for _ in range(1 + D_CONV)))
        db_ref[...] = sums[0]
        for t in range(D_CONV):
            dw_ref[t:t + 1, :] = sums[1 + t]
        dw_ref[D_CONV:8, :] = jnp.zeros((8 - D_CONV, cb), F32)

        def second(i, carry):
            st = pl.multiple_of(i * r, r)
            win = dpad[pl.ds(st, r + 2 * HALO), :]
            acc = wv[0:1, :] * _shifted(win, HALO + 2, r)
            for t in range(1, D_CONV):
                acc = acc + wv[t:t + 1, :] * _shifted(win, HALO + 2 - t, r)
            dx_ref[pl.ds(st, r), :] = acc.astype(dx_ref.dtype)
            return carry

        lax.fori_loop(0, s // r, second, 0)

    col = pl.BlockSpec((s, cb), lambda j: (0, j))
    return pl.pallas_call(
        body, name=name, grid=(CONV_W // cb,),
        in_specs=[pl.BlockSpec((s, cb), lambda j: (0, XS0 // cb + j)), col, col, pl.BlockSpec((8, cb), lambda j: (0, j)),
                  pl.BlockSpec((1, cb), lambda j: (0, j))],
        out_specs=[col, pl.BlockSpec((8, cb), lambda j: (0, j)), pl.BlockSpec((1, cb), lambda j: (0, j))],
        out_shape=[jax.ShapeDtypeStruct((s, CONV_W), MMD), jax.ShapeDtypeStruct((8, CONV_W), F32),
                   jax.ShapeDtypeStruct((1, CONV_W), F32)],
        scratch_shapes=[pltpu.VMEM((s + 2 * HALO, cb), F32), pltpu.VMEM((s + 2 * HALO, cb), F32)],
        compiler_params=_cp(("arbitrary",), VMEM_BIG),
    )(proj, ga, gb, w8, brow)


def _tri(lower):
    i = jnp.arange(CHUNK)
    return ((i[:, None] >= i[None, :]) if lower else (i[:, None] <= i[None, :])).astype(F32)


def _dt_fwd(raw, bias, arow, *, name):
    s = raw.shape[0]

    def body(r_ref, b_ref, a_ref, lo_ref, up_ref, dt_ref, cs_ref):
        u = r_ref[...] + b_ref[...]
        dt = jnp.maximum(u, 0.0) + jnp.log1p(jnp.exp(-jnp.abs(u)))
        dt_ref[...] = dt
        a = dt * a_ref[...]
        lane = lax.broadcasted_iota(jnp.int32, (CHUNK, 128), 1)
        cs_ref[...] = jnp.where(lane < SSD_HEADS, _dot_hi(lo_ref[...], a), _dot_hi(up_ref[...], a))

    blk = pl.BlockSpec((CHUNK, 128), lambda i: (i, 0))
    row = pl.BlockSpec((1, 128), lambda i: (0, 0))
    tri = pl.BlockSpec((CHUNK, CHUNK), lambda i: (0, 0))
    return pl.pallas_call(
        body, name=name, grid=(s // CHUNK,), in_specs=[blk, row, row, tri, tri], out_specs=[blk, blk],
        out_shape=[jax.ShapeDtypeStruct((s, 128), F32)] * 2, compiler_params=_cp(("arbitrary",)),
    )(raw, bias, arow, _tri(True), _tri(False))


def _dt_bwd(ddt, raw, bias, *, name):
    s = raw.shape[0]
    tm = _tile(s, 1024)

    def body(d_ref, r_ref, b_ref, o_ref, db_ref):
        g = d_ref[...] * _sigmoid(r_ref[...] + b_ref[...])
        o_ref[...] = g.astype(o_ref.dtype)
        _acc_rows(db_ref, jnp.sum(g, axis=0, keepdims=True), pl.program_id(0) == 0)

    blk = pl.BlockSpec((tm, 128), lambda i: (i, 0))
    row = pl.BlockSpec((1, 128), lambda i: (0, 0))
    return pl.pallas_call(
        body, name=name, grid=(s // tm,), in_specs=[blk, blk, row], out_specs=[blk, row],
        out_shape=[jax.ShapeDtypeStruct((s, 128), MMD), jax.ShapeDtypeStruct((1, 128), F32)],
        compiler_params=_cp(("arbitrary",)),
    )(ddt, raw, bias)


GW = HPG * SSD_P


def _ssd_specs(nc, rev):
    cc = (lambda c: nc - 1 - c) if rev else (lambda c: c)
    return dict(
        x=pl.BlockSpec((CHUNK, GW), lambda g, c: (cc(c), g)),
        b=pl.BlockSpec((CHUNK, SSD_N), lambda g, c: (cc(c), 2048 // SSD_N + g)),
        c=pl.BlockSpec((CHUNK, SSD_N), lambda g, c: (cc(c), 2048 // SSD_N + SSD_GROUPS + g)),
        col=pl.BlockSpec((1, CHUNK, HPG), lambda g, c: (g, cc(c), 0)),
        rowt=pl.BlockSpec((1, 1, HPG, CHUNK), lambda g, c: (g, cc(c), 0, 0)),
        drow=pl.BlockSpec((1, GW), lambda g, c: (0, g)),
        y=pl.BlockSpec((CHUNK, GW), lambda g, c: (cc(c), g)),
        h=pl.BlockSpec((1, 1, SSD_N, GW), lambda g, c: (g, cc(c), 0, 0)),
        n=pl.BlockSpec((CHUNK, SSD_N), lambda g, c: (cc(c), g)),
    )


def _ssd_fwd(xc, dt4, cs4, cst, drow, *, name):
    s = xc.shape[0]
    nc = s // CHUNK
    sp = _ssd_specs(nc, False)

    def body(x_ref, b_ref, c_ref, dt_ref, cs_ref, cst_ref, d_ref, y_ref, hp_ref, h_ref, xd_ref):
        @pl.when(pl.program_id(1) == 0)
        def _():
            h_ref[...] = jnp.zeros_like(h_ref)

        xb = x_ref[...].astype(F32)
        bm, cm = b_ref[...], c_ref[...]
        dtb, csb, csr = dt_ref[0], cs_ref[0], cst_ref[0, 0]
        dv = d_ref[...]
        h = h_ref[...]
        hp_ref[0, 0] = h
        g = _dot(cm, bm, NT)
        z = _dot(cm, h.astype(MMD))
        mask = lax.broadcasted_iota(jnp.int32, (CHUNK, CHUNK), 0) >= lax.broadcasted_iota(jnp.int32, (CHUNK, CHUNK), 1)
        tl = csb[CHUNK - 1:CHUNK, :]
        for r in range(HPG):
            sl = slice(r * SSD_P, (r + 1) * SSD_P)
            csc = csb[:, r:r + 1]
            lm = jnp.exp(jnp.where(mask, csc - csr[r:r + 1, :], NEG))
            xr = xb[:, sl]
            xs = xr * dtb[:, r:r + 1]
            yd = _dot((g * lm).astype(MMD), xs.astype(MMD))
            y_ref[:, sl] = yd + jnp.exp(csc) * z[:, sl] + dv[:, sl] * xr
            xd_ref[:, sl] = (xs * jnp.exp(tl[:, r:r + 1] - csc)).astype(MMD)
        st = _dot(bm, xd_ref[...], TN)
        et = jnp.exp(tl)
        for r in range(HPG):
            sl = slice(r * SSD_P, (r + 1) * SSD_P)
            h_ref[:, sl] = h[:, sl] * et[:, r:r + 1] + st[:, sl]

    return pl.pallas_call(
        body, name=name, grid=(SSD_GROUPS, nc),
        in_specs=[sp["x"], sp["b"], sp["c"], sp["col"], sp["col"], sp["rowt"], sp["drow"]],
        out_specs=[sp["y"], sp["h"]],
        out_shape=[jax.ShapeDtypeStruct((s, 2048), F32), jax.ShapeDtypeStruct((SSD_GROUPS, nc, SSD_N, GW), F32)],
        scratch_shapes=[pltpu.VMEM((SSD_N, GW), F32), pltpu.VMEM((CHUNK, GW), MMD)],
        compiler_params=_cp(("arbitrary", "arbitrary")),
    )(xc, xc, xc, dt4, cs4, cst, drow)


def _ssd_bwd(xc, dt4, cs4, cst, drow, arow4, dy, hprev, *, name):
    s = xc.shape[0]
    nc = s // CHUNK
    sp = _ssd_specs(nc, True)

    def body(x_ref, b_ref, c_ref, dt_ref, cs_ref, cst_ref, d_ref, a_ref, dy_ref, hp_ref, up_ref,
             dx_ref, db_ref, dc_ref, ddt_ref, da_ref, dh_ref, dz_ref, xd_ref):
        @pl.when(pl.program_id(1) == 0)
        def _():
            dh_ref[...] = jnp.zeros_like(dh_ref)
            da_ref[...] = jnp.zeros_like(da_ref)

        xb = x_ref[...].astype(F32)
        bm, cm = b_ref[...], c_ref[...]
        dtb, csb, csr = dt_ref[0], cs_ref[0], cst_ref[0, 0]
        dv = d_ref[...]
        dyb = dy_ref[...]
        hp = hp_ref[0, 0]
        hpm = hp.astype(MMD)
        dh = dh_ref[...]
        dhm = dh.astype(MMD)
        g = _dot(cm, bm, NT)
        z = _dot(cm, hpm)
        bdh = _dot(bm, dhm)
        ii = lax.broadcasted_iota(jnp.int32, (CHUNK, CHUNK), 0)
        jj = lax.broadcasted_iota(jnp.int32, (CHUNK, CHUNK), 1)
        mask = ii >= jj
        tl = csb[CHUNK - 1:CHUNK, :]
        et = jnp.exp(tl)
        dg = jnp.zeros((CHUNK, CHUNK), F32)
        dcs = jnp.zeros((CHUNK, CHUNK), F32)
        cst_acc = jnp.zeros((CHUNK, CHUNK), F32)
        sdx = jnp.zeros((CHUNK, CHUNK), F32)
        for r in range(HPG):
            sl = slice(r * SSD_P, (r + 1) * SSD_P)
            csc = csb[:, r:r + 1]
            lm = jnp.exp(jnp.where(mask, csc - csr[r:r + 1, :], NEG))
            mm = g * lm
            xr, dyr = xb[:, sl], dyb[:, sl]
            dtr = dtb[:, r:r + 1]
            xs = xr * dtr
            e = jnp.exp(csc)
            dec = jnp.exp(tl[:, r:r + 1] - csc)
            dyr_m = dyr.astype(MMD)
            dm = _dot(dyr_m, xs.astype(MMD), NT)
            w = dm * mm
            bdhr = bdh[:, sl]
            t_b = jnp.sum(xs * bdhr, axis=1, keepdims=True) * dec
            col = jnp.sum(w, axis=1, keepdims=True) + jnp.sum(dyr * (e * z[:, sl]), axis=1, keepdims=True) - t_b
            d_t = jnp.sum(t_b, axis=0, keepdims=True) + et[:, r:r + 1] * jnp.sum(
                jnp.sum(dh[:, sl] * hp[:, sl], axis=1, keepdims=True), axis=0, keepdims=True)
            col = col + jnp.where(ii[:, 0:1] == CHUNK - 1, d_t, 0.0)
            dcs = jnp.where(jj == r, col, dcs)
            cst_acc = jnp.where(ii == r, jnp.sum(w, axis=0, keepdims=True), cst_acc)
            dxs = _dot(mm.astype(MMD), dyr_m, TN) + dec * bdhr
            dx_ref[:, sl] = dxs * dtr + dv[:, sl] * dyr
            sdx = jnp.where(jj == r, jnp.sum(dxs * xr, axis=1, keepdims=True), sdx)
            dg = dg + dm * lm
            dz_ref[:, sl] = (e * dyr).astype(MMD)
            xd_ref[:, sl] = (xs * dec).astype(MMD)
            dh_ref[:, sl] = dh[:, sl] * et[:, r:r + 1]
        da = _dot_hi(up_ref[...], dcs - cst_acc.T)
        ddt_ref[0] = (da * a_ref[0] + sdx)[:, 0:HPG]
        da_ref[0] += jnp.sum(da[:, 0:HPG] * dtb, axis=0, keepdims=True)
        dgm = dg.astype(MMD)
        dzv = dz_ref[...]
        dc_ref[...] = _dot(dgm, bm) + _dot(dzv, hpm, NT)
        db_ref[...] = _dot(dgm, cm, TN) + _dot(xd_ref[...], dhm, NT)
        dh_ref[...] += _dot(cm, dzv, TN)

    return pl.pallas_call(
        body, name=name, grid=(SSD_GROUPS, nc),
        in_specs=[sp["x"], sp["b"], sp["c"], sp["col"], sp["col"], sp["rowt"], sp["drow"],
                  pl.BlockSpec((1, 1, 128), lambda g, c: (g, 0, 0)), sp["y"], sp["h"],
                  pl.BlockSpec((CHUNK, CHUNK), lambda g, c: (0, 0))],
        out_specs=[sp["y"], sp["n"], sp["n"], sp["col"], pl.BlockSpec((1, 1, HPG), lambda g, c: (g, 0, 0))],
        out_shape=[jax.ShapeDtypeStruct((s, 2048), F32), jax.ShapeDtypeStruct((s, SSD_GROUPS * SSD_N), F32),
                   jax.ShapeDtypeStruct((s, SSD_GROUPS * SSD_N), F32), jax.ShapeDtypeStruct((SSD_GROUPS, s, HPG), F32),
                   jax.ShapeDtypeStruct((SSD_GROUPS, 1, HPG), F32)],
        scratch_shapes=[pltpu.VMEM((SSD_N, GW), F32), pltpu.VMEM((CHUNK, GW), MMD), pltpu.VMEM((CHUNK, GW), MMD)],
        compiler_params=_cp(("arbitrary", "arbitrary")),
    )(xc, xc, xc, dt4, cs4, cst, drow, arow4, dy, hprev, _tri(False))


def _gnorm_fwd(ya, yb, proj, w, *, name):
    s = ya.shape[0]
    tm = _tile(s, 256)

    def body(a_ref, b_ref, z_ref, w_ref, o_ref):
        zv = z_ref[...].astype(F32)
        t = (a_ref[...] + b_ref[...]) * (zv * _sigmoid(zv))
        r = lax.rsqrt(jnp.mean(t * t, axis=-1, keepdims=True) + EPS)
        o_ref[...] = ((t * r) * w_ref[...]).astype(o_ref.dtype)

    big = pl.BlockSpec((tm, 2048), lambda i: (i, 0))
    row = pl.BlockSpec((1, 2048), lambda i: (0, 0))
    return pl.pallas_call(
        body, name=name, grid=(s // tm,), in_specs=[big, big, big, row], out_specs=big,
        out_shape=jax.ShapeDtypeStruct((s, 2048), MMD), compiler_params=_cp(("arbitrary",)),
    )(ya, yb, proj, w)


def _gnorm_bwd(dout, ya, yb, proj, w, *, name):
    s = ya.shape[0]
    tm = _tile(s, 256)

    def body(do_ref, a_ref, b_ref, z_ref, w_ref, dy_ref, dz_ref, dw_ref):
        zv = z_ref[...].astype(F32)
        sg = _sigmoid(zv)
        sz = zv * sg
        y = a_ref[...] + b_ref[...]
        t = y * sz
        r = lax.rsqrt(jnp.mean(t * t, axis=-1, keepdims=True) + EPS)
        nv = t * r
        dov = do_ref[...].astype(F32)
        _acc_rows(dw_ref, jnp.sum(dov * nv, axis=0, keepdims=True), pl.program_id(0) == 0)
        dn = dov * w_ref[...]
        dt_ = r * (dn - nv * jnp.mean(dn * nv, axis=-1, keepdims=True))
        dy_ref[...] = dt_ * sz
        dz_ref[...] = (dt_ * y * (sg * (1.0 + zv * (1.0 - sg)))).astype(dz_ref.dtype)

    big = pl.BlockSpec((tm, 2048), lambda i: (i, 0))
    row = pl.BlockSpec((1, 2048), lambda i: (0, 0))
    return pl.pallas_call(
        body, name=name, grid=(s // tm,), in_specs=[big, big, big, big, row], out_specs=[big, big, row],
        out_shape=[jax.ShapeDtypeStruct((s, 2048), F32), jax.ShapeDtypeStruct((s, 2048), MMD),
                   jax.ShapeDtypeStruct((1, 2048), F32)],
        compiler_params=_cp(("arbitrary",)),
    )(dout, ya, yb, proj, w)


def _colsum_prod(a, b, *, name):
    s, n = a.shape
    tm = _tile(s, 256)

    def body(a_ref, b_ref, o_ref):
        _acc_rows(o_ref, jnp.sum(a_ref[...].astype(F32) * b_ref[...].astype(F32), axis=0, keepdims=True),
                  pl.program_id(0) == 0)

    big = pl.BlockSpec((tm, n), lambda i: (i, 0))
    return pl.pallas_call(
        body, name=name, grid=(s // tm,), in_specs=[big, big], out_specs=pl.BlockSpec((1, n), lambda i: (0, 0)),
        out_shape=jax.ShapeDtypeStruct((1, n), F32), compiler_params=_cp(("arbitrary",)),
    )(a, b)


def _heads(a, n):
    return a.reshape(a.shape[0], n, HEAD_DIM).transpose(1, 0, 2)


def _unheads(a):
    return a.transpose(1, 0, 2).reshape(a.shape[1], a.shape[0] * HEAD_DIM)


def _per_group(a):
    return a.reshape(a.shape[0], SSD_GROUPS, HPG).transpose(1, 0, 2)


def _per_group_t(a):
    s = a.shape[0]
    return a.reshape(s // CHUNK, CHUNK, SSD_GROUPS, HPG).transpose(2, 0, 3, 1)


def _local_step(x, target, mod, wts, small):
    s, d = x.shape
    shift1, scale1, gate1, shift2, scale2, gate2 = [mod[i:i + 1] for i in range(6)]
    flip = lambda t: jnp.flip(t, axis=0)

    h1 = _ln_mod(x, small["norm1_w"], scale1, shift1, name="ln1")
    proj = _mm(h1, wts["w_in_p"], name="in_proj", outs=[MMD], tm=512, tn=2944, b_outer=True)
    dt_raw = _mm(h1, wts["w_dt"], name="dt_proj", outs=[F32], tm=512, tn=128)

    qk_w = jnp.concatenate([jnp.tile(small["q_norm_w"], (1, N_Q_HEADS)), jnp.tile(small["k_norm_w"], (1, N_KV_HEADS))], axis=1)
    qk_sc = jnp.concatenate([jnp.full((1, N_Q_HEADS * HEAD_DIM), HEAD_DIM ** -0.5, F32),
                             jnp.ones((1, N_KV_HEADS * HEAD_DIM), F32)], axis=1)
    tabs = _rope_tables(s)
    qk = _qk_fwd(proj, qk_w, qk_sc, tabs, name="qk_fwd")
    qk_h = _heads(qk, N_Q_HEADS + N_KV_HEADS)
    v_h = _heads(proj[:, V0:V0 + N_KV_HEADS * HEAD_DIM], N_KV_HEADS)
    o_h, lse = _flash_fwd(qk_h, v_h, name="flash_fwd")
    attn = _unheads(o_h)

    w8 = jnp.pad(small["conv_w"], ((0, 8 - D_CONV), (0, 0)))
    xc = _conv_fwd(proj, w8, small["conv_b"], name="conv_fwd")
    a_neg = -jnp.exp(small["A_log"])
    arow = jnp.pad(a_neg.reshape(1, 2 * SSD_HEADS), ((0, 0), (0, 128 - 2 * SSD_HEADS)))
    bias_row = jnp.pad(small["dt_bias"].reshape(1, 2 * SSD_HEADS), ((0, 0), (0, 128 - 2 * SSD_HEADS)))
    dt, cs = _dt_fwd(dt_raw, bias_row, arow, name="dt_fwd")
    drow = jnp.repeat(small["ssd_D"], SSD_P, axis=1)
    dirs = []
    for di in range(2):
        f = (lambda t: t) if di == 0 else flip
        cols = slice(di * SSD_HEADS, (di + 1) * SSD_HEADS)
        dirs.append(dict(
            xc=f(xc), dt4=_per_group(f(dt[:, cols])), cs4=_per_group(f(cs[:, cols])), cst=_per_group_t(f(cs[:, cols])),
            drow=drow if di == 0 else jnp.zeros_like(drow),
            arow4=jnp.pad(a_neg[di].reshape(SSD_GROUPS, 1, HPG), ((0, 0), (0, 0), (0, 128 - HPG)))))
    ys = []
    for di, dd in enumerate(dirs):
        y, hprev = _ssd_fwd(dd["xc"], dd["dt4"], dd["cs4"], dd["cst"], dd["drow"], name=f"ssd_fwd{di}")
        dd["hprev"] = hprev
        ys.append(y if di == 0 else flip(y))
    ssdn = _gnorm_fwd(ys[0], ys[1], proj, small["ssd_norm_w"], name="gnorm_fwd")

    a_o = _mm(attn, wts["w_attn_out"], name="attn_out", outs=[MMD], tm=512, tn=512)

    def merge_epi(acc, ao, ga, gs):
        return (_sigmoid(ga.astype(F32)) * ao.astype(F32) + _sigmoid(gs.astype(F32)) * acc, acc)

    merged, b_o = _mm(ssdn, wts["w_ssd_out"], name="ssd_out", outs=[MMD, MMD], tm=512, tn=512,
                      extras=[(a_o, "tile", 0), (proj, "tile", GA0), (proj, "tile", GS0)], epi=merge_epi)

    def res_epi(acc, res, gate):
        return (res + gate * acc, acc)

    x1, mo = _mm(merged, wts["w_o"], name="w_o", outs=[F32, MMD], tm=512, tn=512,
                 extras=[(x, "tile", 0), (gate1, "row", 0)], epi=res_epi)
    h2 = _ln_mod(x1, small["norm2_w"], scale2, shift2, name="ln2")

    def relu2_epi(acc):
        rl = jnp.maximum(acc, 0.0)
        return (rl * rl, rl)

    act, rl = _mm(h2, wts["w_mlp1"], name="mlp1", outs=[MMD, MMD], tm=512, tn=1024, epi=relu2_epi)

    def loss_epi(acc, res, gate, tgt):
        return ((res + gate * acc - tgt) * (1.0 / d), acc)

    dy, ffo = _mm(act, wts["w_mlp2"], name="mlp2", outs=[F32, MMD], tm=512, tn=512,
                  extras=[(x1, "tile", 0), (gate2, "row", 0), (target, "tile", 0)], epi=loss_epi)
    loss = _sumsq(dy, name="loss") * (0.5 * d)

    gw = {}
    gs_ = {}
    dffo, dgate2 = _gate_bwd(dy, ffo, gate2, name="gate2_bwd")
    dpre = _mm(dffo, wts["w_mlp2"], name="mlp2_dx", outs=[MMD], nt=True, tm=512, tn=1024,
               extras=[(rl, "tile", 0)], epi=lambda acc, r: (acc * (2.0 * r.astype(F32)),))
    gw["w_mlp2"] = _mm_tn(act, dffo, name="mlp2_dw")
    dh2 = _mm(dpre, wts["w_mlp1"], name="mlp1_dx", outs=[F32], nt=True, tm=512, tn=512)
    gw["w_mlp1"] = _mm_tn(h2, dpre, name="mlp1_dw")
    dx1, dshift2, dscale2, gs_["norm2_w"] = _ln_mod_bwd(dh2, x1, small["norm2_w"], scale2, dy, name="ln2_bwd")
    dmo, dgate1 = _gate_bwd(dx1, mo, gate1, name="gate1_bwd")

    def merge_bwd_epi(acc, ao, bo, ga, gs):
        sa, ss = _sigmoid(ga.astype(F32)), _sigmoid(gs.astype(F32))
        return (acc * sa, acc * ss, acc * ao.astype(F32) * sa * (1.0 - sa), acc * bo.astype(F32) * ss * (1.0 - ss))

    da_o, db_o, dga, dgs = _mm(dmo, wts["w_o"], name="w_o_dx", outs=[MMD] * 4, nt=True, tm=512, tn=512,
                               extras=[(a_o, "tile", 0), (b_o, "tile", 0), (proj, "tile", GA0), (proj, "tile", GS0)],
                               epi=merge_bwd_epi)
    gw["w_o"] = _mm_tn(merged, dmo, name="w_o_dw")
    dattn = _mm(da_o, wts["w_attn_out"], name="attn_out_dx", outs=[MMD], nt=True, tm=512, tn=512)
    gw["w_attn_out"] = _mm_tn(attn, da_o, name="attn_out_dw")
    dssdn = _mm(db_o, wts["w_ssd_out"], name="ssd_out_dx", outs=[MMD], nt=True, tm=512, tn=512)
    gw["w_ssd_out"] = _mm_tn(ssdn, db_o, name="ssd_out_dw")

    dyssd, dz, gs_["ssd_norm_w"] = _gnorm_bwd(dssdn, ys[0], ys[1], proj, small["ssd_norm_w"], name="gnorm_bwd")
    gs_["ssd_D"] = _colsum_prod(dyssd, xc[:, 0:2048], name="ssd_d_grad").reshape(SSD_HEADS, SSD_P).sum(axis=1).reshape(1, SSD_HEADS)
    dxc, ddts, das = [], [], []
    for di, dd in enumerate(dirs):
        f = (lambda t: t) if di == 0 else flip
        dxs, dbm, dcm, ddt4, da4 = _ssd_bwd(dd["xc"], dd["dt4"], dd["cs4"], dd["cst"], dd["drow"], dd["arow4"],
                                            f(dyssd), dd["hprev"], name=f"ssd_bwd{di}")
        dxc.append(f(jnp.concatenate([dxs, dbm, dcm], axis=1)))
        ddts.append(f(ddt4.transpose(1, 0, 2).reshape(s, SSD_HEADS)))
        das.append(da4.reshape(1, SSD_HEADS))
    dxbc, dw8, gs_["conv_b"] = _conv_bwd(proj, dxc[0], dxc[1], w8, small["conv_b"], name="conv_bwd")
    gs_["conv_w"] = dw8[0:D_CONV]
    gs_["A_log"] = jnp.concatenate(das, axis=0) * a_neg
    ddt = jnp.pad(jnp.concatenate(ddts, axis=1), ((0, 0), (0, 128 - 2 * SSD_HEADS)))
    ddt_raw, dbias = _dt_bwd(ddt, dt_raw, bias_row, name="dt_bwd")
    gs_["dt_bias"] = dbias[:, 0:2 * SSD_HEADS].reshape(2, SSD_HEADS)

    do_h = _heads(dattn, N_Q_HEADS)
    delta = _rowdot(o_h, do_h, name="attn_delta")
    dq_h, dk_h, dv_h = _flash_bwd(qk_h, v_h, do_h, lse, delta, name="flash_bwd")
    dqk = jnp.concatenate([_unheads(dq_h), _unheads(dk_h)], axis=1)
    dqk_u, dqk_w = _qk_bwd(dqk, proj, qk_w, qk_sc, tabs, name="qk_bwd")
    gs_["q_norm_w"] = dqk_w[:, 0:N_Q_HEADS * HEAD_DIM].reshape(N_Q_HEADS, HEAD_DIM).sum(axis=0, keepdims=True)
    gs_["k_norm_w"] = dqk_w[:, N_Q_HEADS * HEAD_DIM:].reshape(N_KV_HEADS, HEAD_DIM).sum(axis=0, keepdims=True)
    dv = _unheads(dv_h).astype(MMD)

    dproj = jnp.concatenate([dz, dga, dgs, dxbc, dqk_u, dv, ddt_raw], axis=1)
    dh1 = _mm(dproj, wts["w_in_p"], name="in_proj_dx", outs=[F32], nt=True, tm=256, tn=512, vmem=VMEM_BIG)
    gw["w_in_p"] = _mm_tn(h1, dproj, name="in_proj_dw", tk=512, tn=2944, tmm=1024, vmem=VMEM_BIG)
    grad_x, dshift1, dscale1, gs_["norm1_w"] = _ln_mod_bwd(dh1, x, small["norm1_w"], scale1, dx1, name="ln1_bwd")
    dmod = jnp.concatenate([dshift1, dscale1, dgate1, dshift2, dscale2, dgate2], axis=0)
    return loss, grad_x, dmod, gw, gs_


N_DEV = 8
N_CHIP = 4
ANY = pl.BlockSpec(memory_space=pl.ANY)


def _place():
    return lax.axis_index("x"), lax.axis_index("y"), lax.axis_index("c")


def _allgather8(v, *, name):
    m_per, n = v.shape

    def body(x_ref, out_ref, send_sems, recv_sems, local_sem):
        x, y, c = _place()
        me, sibling = (x, y, c), (x, y, 1 - c)
        chips = [(1 - x, y), (x, 1 - y), (1 - x, 1 - y)]

        def rows(px, py, pc):
            return out_ref.at[pl.ds((4 * px + 2 * py + pc) * m_per, m_per), :]

        def copy(k, block, to, src=None):
            return pltpu.make_async_remote_copy(
                src_ref=rows(*block) if src is None else src, dst_ref=rows(*block),
                send_sem=send_sems.at[k], recv_sem=recv_sems.at[k], device_id=to, device_id_type=MESH)

        mine = pltpu.make_async_copy(x_ref, rows(*me), local_sem)
        mine.start()
        first = [copy(0, me, sibling, src=x_ref)]
        first += [copy(1 + j, me, (*chip, c), src=x_ref) for j, chip in enumerate(chips)]
        for cp in first:
            cp.start()
        passed = [copy(4 + j, (*chip, c), sibling) for j, chip in enumerate(chips)]
        for j, chip in enumerate(chips):
            copy(1 + j, (*chip, c), me).wait_recv()
            passed[j].start()
        copy(0, sibling, me).wait_recv()
        for j, chip in enumerate(chips):
            copy(4 + j, (*chip, 1 - c), me).wait_recv()
        for cp in first + passed:
            cp.wait_send()
        mine.wait()

    return pl.pallas_call(
        body, name=name, out_shape=jax.ShapeDtypeStruct((N_DEV * m_per, n), v.dtype),
        in_specs=[pl.BlockSpec(memory_space=pltpu.VMEM)], out_specs=pl.BlockSpec(memory_space=pltpu.VMEM),
        scratch_shapes=[pltpu.SemaphoreType.DMA((7,)), pltpu.SemaphoreType.DMA((7,)), pltpu.SemaphoreType.DMA],
    )(v)


def _chip_exchange(src, scatter, *, name):
    shape = src.shape[1:] if scatter else src.shape

    def body(x_ref, out_ref, send_sems, recv_sems, local_sem):
        x, y, c = _place()
        k = 2 * x + y
        chips = [(1 - x, y), (x, 1 - y), (1 - x, 1 - y)]
        ids = [2 * cx + cy for cx, cy in chips]

        def outgoing(j):
            return x_ref.at[ids[j]] if scatter else x_ref

        mine = pltpu.make_async_copy(x_ref.at[k] if scatter else x_ref, out_ref.at[k], local_sem)
        mine.start()
        sends = [pltpu.make_async_remote_copy(
            src_ref=outgoing(j), dst_ref=out_ref.at[k], send_sem=send_sems.at[j], recv_sem=recv_sems.at[j],
            device_id=(cx, cy, c), device_id_type=MESH) for j, (cx, cy) in enumerate(chips)]
        for cp in sends:
            cp.start()
        for j, (cx, cy) in enumerate(chips):
            pltpu.make_async_remote_copy(
                src_ref=outgoing(j), dst_ref=out_ref.at[ids[j]], send_sem=send_sems.at[j], recv_sem=recv_sems.at[j],
                device_id=(cx, cy, c), device_id_type=MESH).wait_recv()
        for cp in sends:
            cp.wait_send()
        mine.wait()

    return pl.pallas_call(
        body, name=name, out_shape=jax.ShapeDtypeStruct((N_CHIP,) + tuple(shape), src.dtype),
        in_specs=[ANY], out_specs=ANY,
        scratch_shapes=[pltpu.SemaphoreType.DMA((3,)), pltpu.SemaphoreType.DMA((3,)), pltpu.SemaphoreType.DMA],
    )(src)


def _row_tile(r, pref=512):
    return max(t for t in range(16, pref + 1, 16) if r % t == 0)


def _sibling_swap(a, *, name):
    def body(x_ref, out_ref, send_sem, recv_sem):
        x, y, c = _place()
        cp = pltpu.make_async_remote_copy(src_ref=x_ref, dst_ref=out_ref, send_sem=send_sem, recv_sem=recv_sem,
                                          device_id=(x, y, 1 - c), device_id_type=MESH)
        cp.start()
        cp.wait()

    return pl.pallas_call(
        body, name=name, out_shape=jax.ShapeDtypeStruct(a.shape, a.dtype), in_specs=[ANY], out_specs=ANY,
        scratch_shapes=[pltpu.SemaphoreType.DMA, pltpu.SemaphoreType.DMA],
    )(a)


def _sum_slots(a, *, name):
    _, r, c = a.shape
    tr = _row_tile(r)

    def body(a_ref, o_ref):
        acc = a_ref[0].astype(F32)
        for j in range(1, N_CHIP):
            acc = acc + a_ref[j].astype(F32)
        o_ref[...] = acc

    return pl.pallas_call(
        body, name=name, grid=(r // tr,), in_specs=[pl.BlockSpec((N_CHIP, tr, c), lambda i: (0, i, 0))],
        out_specs=pl.BlockSpec((tr, c), lambda i: (i, 0)), out_shape=jax.ShapeDtypeStruct((r, c), F32),
        compiler_params=_cp(("arbitrary",)),
    )(a)


def _add2(a, b, *, name):
    r, c = a.shape
    tr = _row_tile(r)

    def body(a_ref, b_ref, o_ref):
        o_ref[...] = a_ref[...] + b_ref[...]

    spec = pl.BlockSpec((tr, c), lambda i: (i, 0))
    return pl.pallas_call(
        body, name=name, grid=(r // tr,), in_specs=[spec, spec], out_specs=spec,
        out_shape=jax.ShapeDtypeStruct((r, c), F32), compiler_params=_cp(("arbitrary",)),
    )(a, b)


BIG = ("w_in", "w_mlp1", "w_attn_out", "w_ssd_out", "w_o", "w_mlp2")
COL_SHARDED = ("w_in", "w_mlp1")
SMALL = ("b_ada", "norm1_w", "norm2_w", "q_norm_w", "k_norm_w", "conv_b", "A_log", "dt_bias", "ssd_D", "ssd_norm_w")
NAMES = ("w_ada", "b_ada", "norm1_w", "norm2_w", "w_in", "q_norm_w", "k_norm_w", "conv_w", "conv_b", "A_log", "dt_bias",
         "ssd_D", "ssd_norm_w", "w_attn_out", "w_ssd_out", "w_o", "w_mlp1", "w_mlp2")
W_IN_COLS = 8768


def _permute_in(w):
    return jnp.concatenate([w[:, 4608:6656], w[:, 6720:8768], w[:, 1536:4608], w[:, 0:1536], w[:, 6656:6720],
                            jnp.zeros((w.shape[0], PW - W_IN_COLS), w.dtype)], axis=1)


def _unpermute_in(wp):
    return jnp.concatenate([wp[:, Q0:DT0], wp[:, XS0:Q0], wp[:, Z0:GA0], wp[:, DT0:DT0 + 64], wp[:, GA0:XS0]], axis=1)


def _pad_to(v, n):
    return jnp.pad(v, (0, n - v.shape[0]))


def _step(w, m, v, loss_target):
    xi, yi, ci = _place()
    chip = 2 * xi + yi
    dev = 4 * xi + 2 * yi + ci
    x, tgt = w["x"], loss_target
    d = x.shape[1]

    cw = w["conv_w"].shape[1]
    v0 = _pad_to(jnp.concatenate([w["c"].reshape(-1), w["conv_w"].reshape(-1)]), 5120).reshape(8, 640)
    g0 = _allgather8(v0, name="ag_cond").reshape(N_DEV, 5120)
    c_all = g0[:, 0:d]
    conv_w = jnp.concatenate([g0[2 * k, d:d + D_CONV * cw].reshape(D_CONV, cw) for k in range(N_CHIP)], axis=1)
    sc = _silu_cast(c_all, name="silu_c")
    modp = _mm(sc, w["w_ada"].astype(MMD), name="ada_fwd", outs=[F32], tm=8, tn=512)
    g1 = _allgather8(modp, name="ag_mod").reshape(N_DEV, N_DEV, modp.shape[1])
    mod_all = jnp.concatenate([g1[2 * k] for k in range(N_CHIP)], axis=1)
    mod = (lax.dynamic_slice_in_dim(mod_all, dev, 1, axis=0) + w["b_ada"]).reshape(6, d)

    packed = jnp.concatenate([w[n].astype(MMD).reshape(-1, d) for n in BIG], axis=0)
    gath = _chip_exchange(packed, False, name="ag_weights")
    full, r0 = {}, 0
    for n in BIG:
        rows = w[n].size // d
        part = gath[:, r0:r0 + rows]
        if n in COL_SHARDED:
            full[n] = jnp.concatenate([part[k].reshape(w[n].shape) for k in range(N_CHIP)], axis=1)
        else:
            full[n] = part.reshape(N_CHIP * w[n].shape[0], w[n].shape[1])
        r0 += rows
    wts = {n: full[n] for n in BIG if n != "w_in"}
    wts["w_in_p"] = _permute_in(full["w_in"])
    wts["w_dt"] = jnp.pad(full["w_in"][:, 6656:6720], ((0, 0), (0, 64)))
    small = {n: w[n] for n in SMALL if n != "b_ada"}
    small["conv_w"] = conv_w

    loss, grad_x, dmod, gw, gs = _local_step(x, tgt, mod, wts, small)
    loss = lax.psum(loss[0, 0], ("x", "y", "c"))

    gw["w_in"] = _unpermute_in(gw.pop("w_in_p"))
    slots = []
    for k in range(N_CHIP):
        parts = []
        for n in BIG:
            r_, c_ = w[n].shape
            blk = gw[n][:, k * c_:(k + 1) * c_] if n in COL_SHARDED else gw[n][k * r_:(k + 1) * r_]
            parts.append(blk.astype(MMD).reshape(-1, d))
        slots.append(jnp.concatenate(parts, axis=0))
    recv = _chip_exchange(jnp.stack(slots), True, name="rs_grads")
    mine = _sum_slots(recv, name="rs_sum")
    total = _add2(mine, _sibling_swap(mine, name="rs_sibling"), name="rs_add")
    grads, r0 = {}, 0
    for n in BIG:
        rows = w[n].size // d
        grads[n] = total[r0:r0 + rows].reshape(w[n].shape)
        r0 += rows

    order = [dmod.reshape(-1)] + [gs[n].reshape(-1) for n in SMALL if n != "b_ada"] + [gs["conv_w"].reshape(-1)]
    vec = jnp.concatenate(order)
    n_small = vec.shape[0]
    n_pad = -(-n_small // 1024) * 1024
    g2 = _allgather8(_pad_to(vec, n_pad).reshape(8, n_pad // 8), name="ag_small")
    tot = _rows_sum(g2, N_DEV, name="small_sum").reshape(-1)
    dmod_all = g2.reshape(N_DEV, n_pad)[:, 0:6 * d]
    off = 0
    for n in SMALL:
        grads[n] = tot[off:off + w[n].size].reshape(w[n].shape)
        off += w[n].size
    conv_full = tot[off:off + D_CONV * N_CHIP * cw].reshape(D_CONV, N_CHIP * cw)
    grads["conv_w"] = lax.dynamic_slice_in_dim(conv_full, chip * cw, cw, axis=1)
    ada_cols = w["w_ada"].shape[1]
    dmod_mine = lax.dynamic_slice_in_dim(dmod_all, chip * ada_cols, ada_cols, axis=1).astype(MMD)
    grads["w_ada"] = _mm_tn(sc, dmod_mine, name="ada_dw", tk=512, tn=512, tmm=8)

    delta, new_m, new_v = {}, {}, {}
    pack = lambda t: jnp.concatenate([t[n].reshape(-1) for n in SMALL]).reshape(1, -1)
    ds_, ms_, vs_ = _adamw(pack(w), pack(grads), pack(m), pack(v), name="adamw_small")
    off = 0
    for n in SMALL:
        for dst, src in ((delta, ds_), (new_m, ms_), (new_v, vs_)):
            dst[n] = src[0, off:off + w[n].size].reshape(w[n].shape)
        off += w[n].size
    for n in ("w_ada", "conv_w") + BIG:
        delta[n], new_m[n], new_v[n] = _adamw(w[n], grads[n], m[n], v[n], name="adamw_" + n)
    return loss, grad_x, grads, delta, new_m, new_v


def kernel(x, c, w_ada, b_ada, norm1_w, norm2_w, w_in, q_norm_w, k_norm_w, conv_w, conv_b, A_log, dt_bias, ssd_D, ssd_norm_w, w_attn_out, w_ssd_out, w_o, w_mlp1, w_mlp2, loss_target, m_w_ada, m_b_ada, m_norm1_w, m_norm2_w, m_w_in, m_q_norm_w, m_k_norm_w, m_conv_w, m_conv_b, m_A_log, m_dt_bias, m_ssd_D, m_ssd_norm_w, m_w_attn_out, m_w_ssd_out, m_w_o, m_w_mlp1, m_w_mlp2, v_w_ada, v_b_ada, v_norm1_w, v_norm2_w, v_w_in, v_q_norm_w, v_k_norm_w, v_conv_w, v_conv_b, v_A_log, v_dt_bias, v_ssd_D, v_ssd_norm_w, v_w_attn_out, v_w_ssd_out, v_w_o, v_w_mlp1, v_w_mlp2):
    args = dict(locals())
    strip = lambda a: a[0] if a.ndim == 3 else a
    w = {n: strip(args[n]) for n in NAMES + ("x", "c")}
    m = {n: strip(args["m_" + n]) for n in NAMES}
    v = {n: strip(args["v_" + n]) for n in NAMES}
    loss, grad_x, grads, delta, new_m, new_v = _step(w, m, v, loss_target[0])
    like = lambda t, n: t.reshape(args[n].shape)
    return (loss, grad_x[None], *[like(grads[n], n) for n in NAMES], *[like(delta[n], n) for n in NAMES],
            *[like(new_m[n], n) for n in NAMES], *[like(new_v[n], n) for n in NAMES])
```

```python
import functools
import math

import jax
import jax.numpy as jnp
from jax import lax
from jax.experimental import pallas as pl
from jax.experimental.pallas import tpu as pltpu

F32 = jnp.float32
MMD = jnp.bfloat16
EPS = 1e-6
NEG = -1e30
MIB = 1024 * 1024
VMEM_BIG = 56 * MIB
VMEM_MID = 40 * MIB

GRID_W = 64
N_Q_HEADS, N_KV_HEADS, HEAD_DIM = 16, 4, 64
ROPE_THETA = 10000.0
SSD_HEADS, SSD_GROUPS, SSD_P, SSD_N, CHUNK = 32, 4, 64, 128, 128
HPG = SSD_HEADS // SSD_GROUPS
D_CONV = 5
ADAM_LR, ADAM_B1, ADAM_B2, ADAM_EPS, ADAM_WD, ADAM_STEP = 0.001, 0.9, 0.999, 1e-08, 0.01, 10

Z0, GA0, GS0, XS0, B0, C0, Q0, K0, V0, DT0, PW = 0, 2048, 3072, 4096, 6144, 6656, 7168, 8192, 8448, 8704, 8832

MESH = pl.DeviceIdType.MESH
NT = (((1,), (1,)), ((), ()))
TN = (((0,), (0,)), ((), ()))


def _cp(sem=None, vmem=VMEM_MID):
    return pltpu.CompilerParams(dimension_semantics=sem, vmem_limit_bytes=vmem)


def _tile(n, pref):
    t = min(n, pref)
    while n % t:
        t //= 2
    return t


def _dot(a, b, dims=None):
    if dims is None:
        return jnp.dot(a, b, preferred_element_type=F32)
    return lax.dot_general(a, b, dims, preferred_element_type=F32)


def _dot_hi(a, b):
    return jnp.dot(a, b, precision=lax.Precision.HIGHEST, preferred_element_type=F32)


def _sigmoid(x):
    return jax.nn.sigmoid(x)


def _mm(a, b, *, name, outs, nt=False, ta=False, extras=(), epi=None, tm=512, tn=512, n=None, b_outer=False,
        vmem=VMEM_MID):
    assert not (nt and ta)
    k, m = a.shape if ta else a.shape[::-1]
    if n is None:
        n = b.shape[0] if nt else b.shape[1]
    tm, tn = _tile(m, tm), _tile(n, tn)
    gi, gj = m // tm, n // tn
    if b_outer:
        grid = (gj, gi)
        ij = lambda p, q: (q, p)
    else:
        grid = (gi, gj)
        ij = lambda p, q: (p, q)
    if ta:
        a_spec = pl.BlockSpec((k, tm), lambda p, q: (0, ij(p, q)[0]))
    else:
        a_spec = pl.BlockSpec((tm, k), lambda p, q: (ij(p, q)[0], 0))
    if nt:
        b_spec = pl.BlockSpec((tn, k), lambda p, q: (ij(p, q)[1], 0))
    else:
        b_spec = pl.BlockSpec((k, tn), lambda p, q: (0, ij(p, q)[1]))
    e_specs = []
    for arr, kind, off in extras:
        ob = off // tn
        assert off % tn == 0
        if kind == "tile":
            e_specs.append(pl.BlockSpec((tm, tn), lambda p, q, ob=ob: (ij(p, q)[0], ob + ij(p, q)[1])))
        else:
            e_specs.append(pl.BlockSpec((1, tn), lambda p, q, ob=ob: (0, ob + ij(p, q)[1])))
    ne = len(extras)

    def body(a_ref, b_ref, *rest):
        acc = _dot(a_ref[...], b_ref[...], NT if nt else (TN if ta else None))
        res = epi(acc, *[e[...] for e in rest[:ne]]) if epi is not None else (acc,)
        for o_ref, r in zip(rest[ne:], res):
            o_ref[...] = r.astype(o_ref.dtype)

    out = pl.pallas_call(
        body, name=name, grid=grid,
        in_specs=[a_spec, b_spec] + e_specs,
        out_specs=[pl.BlockSpec((tm, tn), lambda p, q: ij(p, q)) for _ in outs],
        out_shape=[jax.ShapeDtypeStruct((m, n), dt) for dt in outs],
        compiler_params=_cp(("arbitrary", "arbitrary"), vmem),
    )(a, b, *[e[0] for e in extras])
    return out if len(outs) > 1 else out[0]


def _mm_tn(a, g, *, name, tk=512, tn=1024, tmm=1024, vmem=VMEM_MID):
    m, k = a.shape
    n = g.shape[1]
    tk, tn, tmm = _tile(k, tk), _tile(n, tn), _tile(m, tmm)

    def body(a_ref, g_ref, o_ref):
        p = _dot(a_ref[...], g_ref[...], TN)

        @pl.when(pl.program_id(2) == 0)
        def _():
            o_ref[...] = p

        @pl.when(pl.program_id(2) > 0)
        def _():
            o_ref[...] += p

    return pl.pallas_call(
        body, name=name, grid=(k // tk, n // tn, m // tmm),
        in_specs=[pl.BlockSpec((tmm, tk), lambda i, j, r: (r, i)), pl.BlockSpec((tmm, tn), lambda i, j, r: (r, j))],
        out_specs=pl.BlockSpec((tk, tn), lambda i, j, r: (i, j)),
        out_shape=jax.ShapeDtypeStruct((k, n), F32),
        compiler_params=_cp(("arbitrary", "arbitrary", "arbitrary"), vmem),
    )(a, g)


def _adamw(w, g, m, v, *, name):
    r, c = w.shape
    tr = _tile(r, 256) if r % 8 == 0 else r

    def body(w_ref, g_ref, m_ref, v_ref, d_ref, nm_ref, nv_ref):
        gg = g_ref[...]
        nm = ADAM_B1 * m_ref[...] + (1.0 - ADAM_B1) * gg
        nv = ADAM_B2 * v_ref[...] + (1.0 - ADAM_B2) * jnp.square(gg)
        m_hat = nm / (1.0 - ADAM_B1 ** ADAM_STEP)
        v_hat = nv / (1.0 - ADAM_B2 ** ADAM_STEP)
        d_ref[...] = -ADAM_LR * (m_hat / (jnp.sqrt(v_hat) + ADAM_EPS) + ADAM_WD * w_ref[...])
        nm_ref[...] = nm
        nv_ref[...] = nv

    spec = pl.BlockSpec((tr, c), lambda i: (i, 0))
    return pl.pallas_call(
        body, name=name, grid=(r // tr,), in_specs=[spec] * 4, out_specs=[spec] * 3,
        out_shape=[jax.ShapeDtypeStruct((r, c), F32)] * 3, compiler_params=_cp(("arbitrary",)),
    )(w, g, m, v)


def _rows_sum(a, groups, *, name):
    r = a.shape[0] // groups

    def body(a_ref, o_ref):
        acc = a_ref[0:r, :]
        for d in range(1, groups):
            acc = acc + a_ref[d * r:(d + 1) * r, :]
        o_ref[...] = acc

    return pl.pallas_call(body, name=name, out_shape=jax.ShapeDtypeStruct((r, a.shape[1]), F32))(a)


def _silu_cast(a, *, name):
    def body(a_ref, o_ref):
        x = a_ref[...]
        o_ref[...] = (x * _sigmoid(x)).astype(o_ref.dtype)

    return pl.pallas_call(body, name=name, out_shape=jax.ShapeDtypeStruct(a.shape, MMD))(a)


def _sumsq(a, *, name):
    m, n = a.shape
    tm = _tile(m, 512)

    def body(a_ref, o_ref):
        x = a_ref[...]
        p = jnp.sum(jnp.sum(x * x, axis=1, keepdims=True), axis=0, keepdims=True)

        @pl.when(pl.program_id(0) == 0)
        def _():
            o_ref[...] = p

        @pl.when(pl.program_id(0) > 0)
        def _():
            o_ref[...] += p

    return pl.pallas_call(
        body, name=name, grid=(m // tm,), in_specs=[pl.BlockSpec((tm, n), lambda i: (i, 0))],
        out_specs=pl.BlockSpec((1, 1), lambda i: (0, 0)), out_shape=jax.ShapeDtypeStruct((1, 1), F32),
        compiler_params=_cp(("arbitrary",)),
    )(a)


def _acc_rows(o_ref, p, first):
    @pl.when(first)
    def _():
        o_ref[...] = p

    @pl.when(jnp.logical_not(first))
    def _():
        o_ref[...] += p


def _ln_mod(x, w, scale, shift, *, name):
    s, d = x.shape
    tm = _tile(s, 512)

    def body(x_ref, w_ref, sc_ref, sh_ref, o_ref):
        xv = x_ref[...]
        r = lax.rsqrt(jnp.mean(xv * xv, axis=-1, keepdims=True) + EPS)
        o_ref[...] = ((xv * r) * w_ref[...] * (1.0 + sc_ref[...]) + sh_ref[...]).astype(o_ref.dtype)

    row = pl.BlockSpec((1, d), lambda i: (0, 0))
    big = pl.BlockSpec((tm, d), lambda i: (i, 0))
    return pl.pallas_call(
        body, name=name, grid=(s // tm,), in_specs=[big, row, row, row], out_specs=big,
        out_shape=jax.ShapeDtypeStruct((s, d), MMD), compiler_params=_cp(("arbitrary",)),
    )(x, w, scale, shift)


def _ln_mod_bwd(dh, x, w, scale, dres, *, name):
    s, d = x.shape
    tm = _tile(s, 512)

    def body(dh_ref, x_ref, w_ref, sc_ref, dres_ref, dx_ref, dsh_ref, dsc_ref, dw_ref):
        xv = x_ref[...]
        dhv = dh_ref[...].astype(F32)
        r = lax.rsqrt(jnp.mean(xv * xv, axis=-1, keepdims=True) + EPS)
        nv = xv * r
        wv = w_ref[...]
        g1 = 1.0 + sc_ref[...]
        dn = dhv * (wv * g1)
        dx_ref[...] = dres_ref[...] + r * (dn - nv * jnp.mean(dn * nv, axis=-1, keepdims=True))
        first = pl.program_id(0) == 0
        _acc_rows(dsh_ref, jnp.sum(dhv, axis=0, keepdims=True), first)
        _acc_rows(dsc_ref, jnp.sum(dhv * nv * wv, axis=0, keepdims=True), first)
        _acc_rows(dw_ref, jnp.sum(dhv * nv * g1, axis=0, keepdims=True), first)

    row = pl.BlockSpec((1, d), lambda i: (0, 0))
    big = pl.BlockSpec((tm, d), lambda i: (i, 0))
    return pl.pallas_call(
        body, name=name, grid=(s // tm,), in_specs=[big, big, row, row, big], out_specs=[big, row, row, row],
        out_shape=[jax.ShapeDtypeStruct((s, d), F32)] + [jax.ShapeDtypeStruct((1, d), F32)] * 3,
        compiler_params=_cp(("arbitrary",)),
    )(dh, x, w, scale, dres)


def _gate_bwd(dy, u, gate, *, name):
    s, d = dy.shape
    tm = _tile(s, 512)

    def body(dy_ref, u_ref, g_ref, du_ref, dg_ref):
        dyv = dy_ref[...]
        du_ref[...] = (dyv * g_ref[...]).astype(du_ref.dtype)
        _acc_rows(dg_ref, jnp.sum(dyv * u_ref[...].astype(F32), axis=0, keepdims=True), pl.program_id(0) == 0)

    row = pl.BlockSpec((1, d), lambda i: (0, 0))
    big = pl.BlockSpec((tm, d), lambda i: (i, 0))
    return pl.pallas_call(
        body, name=name, grid=(s // tm,), in_specs=[big, big, row], out_specs=[big, row],
        out_shape=[jax.ShapeDtypeStruct((s, d), MMD), jax.ShapeDtypeStruct((1, d), F32)],
        compiler_params=_cp(("arbitrary",)),
    )(dy, u, gate)


def _seg64(v, e):
    hi = v.astype(jnp.bfloat16)
    lo = (v - hi.astype(F32)).astype(jnp.bfloat16)
    return _dot(hi, e) + _dot(lo, e)


def _rope_tables(s):
    rows = s // GRID_W
    pos_row = jnp.repeat(jnp.arange(rows, dtype=jnp.int32), GRID_W).astype(F32)
    pos_col = jnp.tile(jnp.arange(GRID_W, dtype=jnp.int32), rows).astype(F32)
    axis_dim = HEAD_DIM // 2
    inv_freq = ROPE_THETA ** (-jnp.arange(0, axis_dim, 2, dtype=F32) / axis_dim)
    ang_r = pos_row[:, None] * inv_freq[None, :]
    ang_c = pos_col[:, None] * inv_freq[None, :]
    zero = jnp.zeros_like(ang_r)
    cos = jnp.concatenate([jnp.cos(ang_r), jnp.cos(ang_r), jnp.cos(ang_c), jnp.cos(ang_c)], axis=1)
    s_a = jnp.concatenate([-jnp.sin(ang_r), zero, -jnp.sin(ang_c), zero], axis=1)
    s_b = jnp.concatenate([zero, jnp.sin(ang_r), zero, jnp.sin(ang_c)], axis=1)
    return [jnp.tile(t, (1, 2)) for t in (cos, s_a, s_b)]


def _e128():
    i = jnp.arange(128)
    return (i[:, None] // 64 == i[None, :] // 64).astype(jnp.bfloat16)


QKW = N_Q_HEADS * HEAD_DIM + N_KV_HEADS * HEAD_DIM


def _qk_fwd(proj, wrow, scrow, tabs, *, name):
    s = proj.shape[0]
    tm = _tile(s, 512)

    def body(x_ref, w_ref, sc_ref, cos_ref, sa_ref, sb_ref, e_ref, o_ref, ot_ref):
        u = x_ref[...].astype(F32)
        r = lax.rsqrt(_seg64(u * u, e_ref[...]) * (1.0 / HEAD_DIM) + EPS)
        nv = (u * r) * w_ref[...]
        ro = nv * cos_ref[...] + pltpu.roll(nv, 112, 1) * sa_ref[...] + pltpu.roll(nv, 16, 1) * sb_ref[...]
        out = ro * sc_ref[...]
        o_ref[...] = out.astype(o_ref.dtype)
        ot_ref[...] = out.T.astype(ot_ref.dtype)

    tab = pl.BlockSpec((tm, 128), lambda i, j: (i, 0))
    row = pl.BlockSpec((1, 128), lambda i, j: (0, j))
    return pl.pallas_call(
        body, name=name, grid=(s // tm, QKW // 128),
        in_specs=[pl.BlockSpec((tm, 128), lambda i, j: (i, Q0 // 128 + j)), row, row, tab, tab, tab,
                  pl.BlockSpec((128, 128), lambda i, j: (0, 0))],
        out_specs=[pl.BlockSpec((tm, 128), lambda i, j: (i, j)), pl.BlockSpec((128, tm), lambda i, j: (j, i))],
        out_shape=[jax.ShapeDtypeStruct((s, QKW), MMD), jax.ShapeDtypeStruct((QKW, s), MMD)],
        compiler_params=_cp(("arbitrary", "arbitrary")),
    )(proj, wrow, scrow, *tabs, _e128())


def _qk_bwd(dqkt, proj, wrow, scrow, tabs, *, name):
    s = proj.shape[0]
    tm = _tile(s, 512)

    def body(d_ref, x_ref, w_ref, sc_ref, cos_ref, sa_ref, sb_ref, e_ref, du_ref, dw_ref):
        e = e_ref[...]
        d = d_ref[...].T * sc_ref[...]
        dn = d * cos_ref[...] + pltpu.roll(d * sa_ref[...], 16, 1) + pltpu.roll(d * sb_ref[...], 112, 1)
        u = x_ref[...].astype(F32)
        r = lax.rsqrt(_seg64(u * u, e) * (1.0 / HEAD_DIM) + EPS)
        uh = u * r
        _acc_rows(dw_ref, jnp.sum(dn * uh, axis=0, keepdims=True), pl.program_id(1) == 0)
        dnw = dn * w_ref[...]
        du_ref[...] = (r * (dnw - uh * (_seg64(dnw * uh, e) * (1.0 / HEAD_DIM)))).astype(du_ref.dtype)

    tab = pl.BlockSpec((tm, 128), lambda j, i: (i, 0))
    row = pl.BlockSpec((1, 128), lambda j, i: (0, j))
    return pl.pallas_call(
        body, name=name, grid=(QKW // 128, s // tm),
        in_specs=[pl.BlockSpec((128, tm), lambda j, i: (j, i)), pl.BlockSpec((tm, 128), lambda j, i: (i, Q0 // 128 + j)),
                  row, row, tab, tab, tab, pl.BlockSpec((128, 128), lambda j, i: (0, 0))],
        out_specs=[pl.BlockSpec((tm, 128), lambda j, i: (i, j)), row],
        out_shape=[jax.ShapeDtypeStruct((s, QKW), MMD), jax.ShapeDtypeStruct((1, QKW), F32)],
        compiler_params=_cp(("arbitrary", "arbitrary")),
    )(dqkt, proj, wrow, scrow, *tabs, _e128())


REP = N_Q_HEADS // N_KV_HEADS


def _lanes(ref):
    return jnp.concatenate([ref[r] for r in range(REP)], axis=1)


def _flash_fwd(qkt, vt, *, name):
    s = qkt.shape[2]
    tq, tk = _tile(s, 1024), _tile(s, 512)
    nk = s // tk
    lanes = REP * tq

    def body(q_ref, k_ref, v_ref, o_ref, lse_ref, m_ref, l_ref, acc_ref):
        j = pl.program_id(2)

        @pl.when(j == 0)
        def _():
            m_ref[...] = jnp.full_like(m_ref, NEG)
            l_ref[...] = jnp.zeros_like(l_ref)
            acc_ref[...] = jnp.zeros_like(acc_ref)

        st = _dot(k_ref[0], _lanes(q_ref), TN)
        m_prev = m_ref[...]
        m_new = jnp.maximum(m_prev, jnp.max(st, axis=0, keepdims=True))
        p = jnp.exp(st - m_new)
        alpha = jnp.exp(m_prev - m_new)
        l_ref[...] = alpha * l_ref[...] + jnp.sum(p, axis=0, keepdims=True)
        acc_ref[...] = alpha * acc_ref[...] + _dot(v_ref[0], p.astype(MMD))
        m_ref[...] = m_new

        @pl.when(j == nk - 1)
        def _():
            l = l_ref[...]
            o = acc_ref[...] / l
            ls = m_ref[...] + jnp.log(l)
            for r in range(REP):
                o_ref[r] = o[:, r * tq:(r + 1) * tq].astype(o_ref.dtype)
                lse_ref[r] = ls[:, r * tq:(r + 1) * tq]

    qspec = pl.BlockSpec((REP, HEAD_DIM, tq), lambda g, i, j: (g, 0, i))
    return pl.pallas_call(
        body, name=name, grid=(N_KV_HEADS, s // tq, nk),
        in_specs=[qspec, pl.BlockSpec((1, HEAD_DIM, tk), lambda g, i, j: (N_Q_HEADS + g, 0, j)),
                  pl.BlockSpec((1, HEAD_DIM, tk), lambda g, i, j: (g, 0, j))],
        out_specs=[qspec, pl.BlockSpec((REP, 1, tq), lambda g, i, j: (g, 0, i))],
        out_shape=[jax.ShapeDtypeStruct((N_Q_HEADS, HEAD_DIM, s), MMD), jax.ShapeDtypeStruct((N_Q_HEADS, 1, s), F32)],
        scratch_shapes=[pltpu.VMEM((1, lanes), F32), pltpu.VMEM((1, lanes), F32), pltpu.VMEM((HEAD_DIM, lanes), F32)],
        compiler_params=_cp(("arbitrary", "arbitrary", "arbitrary"), VMEM_BIG),
    )(qkt, qkt, vt)


def _flash_bwd(qkt, k_h, v_h, dot, ot, lse, *, name):
    s = qkt.shape[2]
    tq, tk = _tile(s, 512), _tile(s, 512)
    nk = s // tk

    def body(q_ref, kt_ref, k_ref, v_ref, do_ref, o_ref, lse_ref, dq_ref, dk_ref, dv_ref, dq_acc):
        i, j = pl.program_id(1), pl.program_id(2)
        q, do = _lanes(q_ref), _lanes(do_ref)
        delta = jnp.sum(do.astype(F32) * _lanes(o_ref).astype(F32), axis=0, keepdims=True)
        k, v = k_ref[0], v_ref[0]
        p = jnp.exp(_dot(k, q) - _lanes(lse_ref))
        dvc = _dot(p.astype(MMD), do, NT)
        ds = (p * (_dot(v, do) - delta)).astype(MMD)
        dkc = _dot(ds, q, NT)
        dqc = _dot(kt_ref[0], ds)
        rows = pl.ds(pl.multiple_of(j * tk, tk), tk)

        @pl.when(i == 0)
        def _():
            dk_ref[0, rows, :] = dkc
            dv_ref[0, rows, :] = dvc

        @pl.when(i > 0)
        def _():
            dk_ref[0, rows, :] += dkc
            dv_ref[0, rows, :] += dvc

        @pl.when(j == 0)
        def _():
            dq_acc[...] = dqc

        @pl.when(j > 0)
        def _():
            dq_acc[...] += dqc

        @pl.when(j == nk - 1)
        def _():
            acc = dq_acc[...]
            for r in range(REP):
                dq_ref[r] = acc[:, r * tq:(r + 1) * tq]

    qspec = pl.BlockSpec((REP, HEAD_DIM, tq), lambda g, i, j: (g, 0, i))
    kvin = pl.BlockSpec((1, tk, HEAD_DIM), lambda g, i, j: (g, j, 0))
    kvres = pl.BlockSpec((1, s, HEAD_DIM), lambda g, i, j: (g, 0, 0))
    return pl.pallas_call(
        body, name=name, grid=(N_KV_HEADS, s // tq, nk),
        in_specs=[qspec, pl.BlockSpec((1, HEAD_DIM, tk), lambda g, i, j: (N_Q_HEADS + g, 0, j)), kvin, kvin,
                  qspec, qspec, pl.BlockSpec((REP, 1, tq), lambda g, i, j: (g, 0, i))],
        out_specs=[qspec, kvres, kvres],
        out_shape=[jax.ShapeDtypeStruct((N_Q_HEADS, HEAD_DIM, s), F32), jax.ShapeDtypeStruct((N_KV_HEADS, s, HEAD_DIM), F32),
                   jax.ShapeDtypeStruct((N_KV_HEADS, s, HEAD_DIM), F32)],
        scratch_shapes=[pltpu.VMEM((HEAD_DIM, REP * tq), F32)],
        compiler_params=_cp(("arbitrary", "arbitrary", "arbitrary"), VMEM_BIG),
    )(qkt, qkt, k_h, v_h, dot, ot, lse)


HALO = 8
CONV_W = 2048 + 2 * SSD_GROUPS * SSD_N


def _shifted(win, off, r):
    return pltpu.roll(win, (r + 2 * HALO - off) % (r + 2 * HALO), 0)[0:r]


def _conv_fwd(proj, w8, brow, *, name):
    s = proj.shape[0]
    cb = 256
    r = _tile(s, 512)

    def body(x_ref, w_ref, b_ref, o_ref, pad_ref):
        zeros = jnp.zeros((HALO, cb), F32)
        pad_ref[0:HALO, :] = zeros
        pad_ref[s + HALO:s + 2 * HALO, :] = zeros

        def fill(i, carry):
            st = pl.multiple_of(i * r, r)
            pad_ref[pl.ds(st + HALO, r), :] = x_ref[pl.ds(st, r), :].astype(F32)
            return carry

        lax.fori_loop(0, s // r, fill, 0)
        wv = w_ref[...]
        bv = b_ref[...]

        def step(i, carry):
            st = pl.multiple_of(i * r, r)
            win = pad_ref[pl.ds(st, r + 2 * HALO), :]
            acc = bv + wv[0:1, :] * _shifted(win, HALO - 2, r)
            for t in range(1, D_CONV):
                acc = acc + wv[t:t + 1, :] * _shifted(win, HALO - 2 + t, r)
            o_ref[pl.ds(st, r), :] = (acc * _sigmoid(acc)).astype(o_ref.dtype)
            return carry

        lax.fori_loop(0, s // r, step, 0)

    return pl.pallas_call(
        body, name=name, grid=(CONV_W // cb,),
        in_specs=[pl.BlockSpec((s, cb), lambda j: (0, XS0 // cb + j)), pl.BlockSpec((8, cb), lambda j: (0, j)),
                  pl.BlockSpec((1, cb), lambda j: (0, j))],
        out_specs=pl.BlockSpec((s, cb), lambda j: (0, j)),
        out_shape=jax.ShapeDtypeStruct((s, CONV_W), MMD),
        scratch_shapes=[pltpu.VMEM((s + 2 * HALO, cb), F32)],
        compiler_params=_cp(("arbitrary",), VMEM_MID),
    )(proj, w8, brow)


def _conv_bwd(proj, ga, gb, w8, brow, *, name):
    s = proj.shape[0]
    cb = 128
    r = _tile(s, 512)

    def body(x_ref, ga_ref, gb_ref, w_ref, b_ref, dx_ref, dw_ref, db_ref, xpad, dpad):
        zeros = jnp.zeros((HALO, cb), F32)
        for ref in (xpad, dpad):
            ref[0:HALO, :] = zeros
            ref[s + HALO:s + 2 * HALO, :] = zeros

        def fill(i, carry):
            st = pl.multiple_of(i * r, r)
            xpad[pl.ds(st + HALO, r), :] = x_ref[pl.ds(st, r), :].astype(F32)
            return carry

        lax.fori_loop(0, s // r, fill, 0)
        wv = w_ref[...]
        bv = b_ref[...]

        def first(i, carry):
            st = pl.multiple_of(i * r, r)
            win = xpad[pl.ds(st, r + 2 * HALO), :]
            taps = [_shifted(win, HALO - 2 + t, r) for t in range(D_CONV)]
            u = bv
            for t in range(D_CONV):
                u = u + wv[t:t + 1, :] * taps[t]
            sg = _sigmoid(u)
            du = (ga_ref[pl.ds(st, r), :] + gb_ref[pl.ds(st, r), :]) * (sg * (1.0 + u * (1.0 - sg)))
            dpad[pl.ds(st + HALO, r), :] = du
            out = [carry[0] + jnp.sum(du, axis=0, keepdims=True)]
            for t in range(D_CONV):
                out.append(carry[1 + t] + jnp.sum(du * taps[t], axis=0, keepdims=True))
            return tuple(out)

        sums = lax.fori_loop(0, s // r, first, tuple(jnp.zeros((1, cb), F32) for _ in range(1 + D_CONV)))
        db_ref[...] = sums[0]
        for t in range(D_CONV):
            dw_ref[t:t + 1, :] = sums[1 + t]
        dw_ref[D_CONV:8, :] = jnp.zeros((8 - D_CONV, cb), F32)

        def second(i, carry):
            st = pl.multiple_of(i * r, r)
            win = dpad[pl.ds(st, r + 2 * HALO), :]
            acc = wv[0:1, :] * _shifted(win, HALO + 2, r)
            for t in range(1, D_CONV):
                acc = acc + wv[t:t + 1, :] * _shifted(win, HALO + 2 - t, r)
            dx_ref[pl.ds(st, r), :] = acc.astype(dx_ref.dtype)
            return carry

        lax.fori_loop(0, s // r, second, 0)

    col = pl.BlockSpec((s, cb), lambda j: (0, j))
    return pl.pallas_call(
        body, name=name, grid=(CONV_W // cb,),
        in_specs=[pl.BlockSpec((s, cb), lambda j: (0, XS0 // cb + j)), col, col, pl.BlockSpec((8, cb), lambda j: (0, j)),
                  pl.BlockSpec((1, cb), lambda j: (0, j))],
        out_specs=[col, pl.BlockSpec((8, cb), lambda j: (0, j)), pl.BlockSpec((1, cb), lambda j: (0, j))],
        out_shape=[jax.ShapeDtypeStruct((s, CONV_W), MMD), jax.ShapeDtypeStruct((8, CONV_W), F32),
                   jax.ShapeDtypeStruct((1, CONV_W), F32)],
        scratch_shapes=[pltpu.VMEM((s + 2 * HALO, cb), F32), pltpu.VMEM((s + 2 * HALO, cb), F32)],
        compiler_params=_cp(("arbitrary",), VMEM_BIG),
    )(proj, ga, gb, w8, brow)


def _tri(lower):
    i = jnp.arange(CHUNK)
    return ((i[:, None] >= i[None, :]) if lower else (i[:, None] <= i[None, :])).astype(F32)


def _dt_fwd(raw, bias, arow, *, name):
    s = raw.shape[0]

    def body(r_ref, b_ref, a_ref, lo_ref, up_ref, dt_ref, cs_ref):
        u = r_ref[...] + b_ref[...]
        dt = jnp.maximum(u, 0.0) + jnp.log1p(jnp.exp(-jnp.abs(u)))
        dt_ref[...] = dt
        a = dt * a_ref[...]
        lane = lax.broadcasted_iota(jnp.int32, (CHUNK, 128), 1)
        cs_ref[...] = jnp.where(lane < SSD_HEADS, _dot_hi(lo_ref[...], a), _dot_hi(up_ref[...], a))

    blk = pl.BlockSpec((CHUNK, 128), lambda i: (i, 0))
    row = pl.BlockSpec((1, 128), lambda i: (0, 0))
    tri = pl.BlockSpec((CHUNK, CHUNK), lambda i: (0, 0))
    return pl.pallas_call(
        body, name=name, grid=(s // CHUNK,), in_specs=[blk, row, row, tri, tri], out_specs=[blk, blk],
        out_shape=[jax.ShapeDtypeStruct((s, 128), F32)] * 2, compiler_params=_cp(("arbitrary",)),
    )(raw, bias, arow, _tri(True), _tri(False))


def _dt_bwd(ddt, raw, bias, *, name):
    s = raw.shape[0]
    tm = _tile(s, 1024)

    def body(d_ref, r_ref, b_ref, o_ref, db_ref):
        g = d_ref[...] * _sigmoid(r_ref[...] + b_ref[...])
        o_ref[...] = g.astype(o_ref.dtype)
        _acc_rows(db_ref, jnp.sum(g, axis=0, keepdims=True), pl.program_id(0) == 0)

    blk = pl.BlockSpec((tm, 128), lambda i: (i, 0))
    row = pl.BlockSpec((1, 128), lambda i: (0, 0))
    return pl.pallas_call(
        body, name=name, grid=(s // tm,), in_specs=[blk, blk, row], out_specs=[blk, row],
        out_shape=[jax.ShapeDtypeStruct((s, 128), MMD), jax.ShapeDtypeStruct((1, 128), F32)],
        compiler_params=_cp(("arbitrary",)),
    )(ddt, raw, bias)


GW = HPG * SSD_P


def _ssd_specs(nc, rev):
    cc = (lambda c: nc - 1 - c) if rev else (lambda c: c)
    return dict(
        x=pl.BlockSpec((CHUNK, GW), lambda g, c: (cc(c), g)),
        b=pl.BlockSpec((CHUNK, SSD_N), lambda g, c: (cc(c), 2048 // SSD_N + g)),
        c=pl.BlockSpec((CHUNK, SSD_N), lambda g, c: (cc(c), 2048 // SSD_N + SSD_GROUPS + g)),
        col=pl.BlockSpec((1, CHUNK, HPG), lambda g, c: (g, cc(c), 0)),
        rowt=pl.BlockSpec((1, 1, HPG, CHUNK), lambda g, c: (g, cc(c), 0, 0)),
        drow=pl.BlockSpec((1, GW), lambda g, c: (0, g)),
        y=pl.BlockSpec((CHUNK, GW), lambda g, c: (cc(c), g)),
        h=pl.BlockSpec((1, 1, SSD_N, GW), lambda g, c: (g, cc(c), 0, 0)),
        n=pl.BlockSpec((CHUNK, SSD_N), lambda g, c: (cc(c), g)),
    )


def _ssd_mask(anti):
    ii = lax.broadcasted_iota(jnp.int32, (CHUNK, CHUNK), 0)
    jj = lax.broadcasted_iota(jnp.int32, (CHUNK, CHUNK), 1)
    return ii, jj, (ii <= jj) if anti else (ii >= jj)


def _ssd_fwd(xc, dt4, cs4, cst, drow, anti, *, name):
    s = xc.shape[0]
    nc = s // CHUNK
    sp = _ssd_specs(nc, anti)
    trow = 0 if anti else CHUNK - 1

    def body(x_ref, b_ref, c_ref, dt_ref, cs_ref, cst_ref, d_ref, y_ref, hp_ref, h_ref, xd_ref):
        @pl.when(pl.program_id(1) == 0)
        def _():
            h_ref[...] = jnp.zeros_like(h_ref)

        xb = x_ref[...].astype(F32)
        bm, cm = b_ref[...], c_ref[...]
        dtb, csb, csr = dt_ref[0], cs_ref[0], cst_ref[0, 0]
        dv = d_ref[...]
        h = h_ref[...]
        hp_ref[0, 0] = h
        g = _dot(cm, bm, NT)
        z = _dot(cm, h.astype(MMD))
        mask = _ssd_mask(anti)[2]
        tl = csb[trow:trow + 1, :]
        for r in range(HPG):
            sl = slice(r * SSD_P, (r + 1) * SSD_P)
            csc = csb[:, r:r + 1]
            lm = jnp.exp(jnp.where(mask, csc - csr[r:r + 1, :], NEG))
            xr = xb[:, sl]
            xs = xr * dtb[:, r:r + 1]
            yd = _dot((g * lm).astype(MMD), xs.astype(MMD))
            y_ref[:, sl] = yd + jnp.exp(csc) * z[:, sl] + dv[:, sl] * xr
            xd_ref[:, sl] = (xs * jnp.exp(tl[:, r:r + 1] - csc)).astype(MMD)
        st = _dot(bm, xd_ref[...], TN)
        et = jnp.exp(tl)
        for r in range(HPG):
            sl = slice(r * SSD_P, (r + 1) * SSD_P)
            h_ref[:, sl] = h[:, sl] * et[:, r:r + 1] + st[:, sl]

    return pl.pallas_call(
        body, name=name, grid=(SSD_GROUPS, nc),
        in_specs=[sp["x"], sp["b"], sp["c"], sp["col"], sp["col"], sp["rowt"], sp["drow"]],
        out_specs=[sp["y"], sp["h"]],
        out_shape=[jax.ShapeDtypeStruct((s, 2048), F32), jax.ShapeDtypeStruct((SSD_GROUPS, nc, SSD_N, GW), F32)],
        scratch_shapes=[pltpu.VMEM((SSD_N, GW), F32), pltpu.VMEM((CHUNK, GW), MMD)],
        compiler_params=_cp(("arbitrary", "arbitrary")),
    )(xc, xc, xc, dt4, cs4, cst, drow)


def _ssd_bwd(xc, dt4, cs4, cst, drow, arow4, dy, hprev, anti, *, name):
    s = xc.shape[0]
    nc = s // CHUNK
    sp = _ssd_specs(nc, not anti)
    trow = 0 if anti else CHUNK - 1

    def body(x_ref, b_ref, c_ref, dt_ref, cs_ref, cst_ref, d_ref, a_ref, dy_ref, hp_ref, tri_ref,
             dx_ref, db_ref, dc_ref, ddt_ref, da_ref, dh_ref, dz_ref, xd_ref):
        @pl.when(pl.program_id(1) == 0)
        def _():
            dh_ref[...] = jnp.zeros_like(dh_ref)
            da_ref[...] = jnp.zeros_like(da_ref)

        xb = x_ref[...].astype(F32)
        bm, cm = b_ref[...], c_ref[...]
        dtb, csb, csr = dt_ref[0], cs_ref[0], cst_ref[0, 0]
        dv = d_ref[...]
        dyb = dy_ref[...]
        hp = hp_ref[0, 0]
        hpm = hp.astype(MMD)
        dh = dh_ref[...]
        dhm = dh.astype(MMD)
        g = _dot(cm, bm, NT)
        z = _dot(cm, hpm)
        bdh = _dot(bm, dhm)
        ii, jj, mask = _ssd_mask(anti)
        tl = csb[trow:trow + 1, :]
        et = jnp.exp(tl)
        dg = jnp.zeros((CHUNK, CHUNK), F32)
        dcs = jnp.zeros((CHUNK, CHUNK), F32)
        cst_acc = jnp.zeros((CHUNK, CHUNK), F32)
        sdx = jnp.zeros((CHUNK, CHUNK), F32)
        for r in range(HPG):
            sl = slice(r * SSD_P, (r + 1) * SSD_P)
            csc = csb[:, r:r + 1]
            lm = jnp.exp(jnp.where(mask, csc - csr[r:r + 1, :], NEG))
            mm = g * lm
            xr, dyr = xb[:, sl], dyb[:, sl]
            dtr = dtb[:, r:r + 1]
            xs = xr * dtr
            e = jnp.exp(csc)
            dec = jnp.exp(tl[:, r:r + 1] - csc)
            dyr_m = dyr.astype(MMD)
            dm = _dot(dyr_m, xs.astype(MMD), NT)
            w = dm * mm
            bdhr = bdh[:, sl]
            t_b = jnp.sum(xs * bdhr, axis=1, keepdims=True) * dec
            col = jnp.sum(w, axis=1, keepdims=True) + jnp.sum(dyr * (e * z[:, sl]), axis=1, keepdims=True) - t_b
            d_t = jnp.sum(t_b, axis=0, keepdims=True) + et[:, r:r + 1] * jnp.sum(
                jnp.sum(dh[:, sl] * hp[:, sl], axis=1, keepdims=True), axis=0, keepdims=True)
            col = col + jnp.where(ii[:, 0:1] == trow, d_t, 0.0)
            dcs = jnp.where(jj == r, col, dcs)
            cst_acc = jnp.where(ii == r, jnp.sum(w, axis=0, keepdims=True), cst_acc)
            dxs = _dot(mm.astype(MMD), dyr_m, TN) + dec * bdhr
            dx_ref[:, sl] = dxs * dtr + dv[:, sl] * dyr
            sdx = jnp.where(jj == r, jnp.sum(dxs * xr, axis=1, keepdims=True), sdx)
            dg = dg + dm * lm
            dz_ref[:, sl] = (e * dyr).astype(MMD)
            xd_ref[:, sl] = (xs * dec).astype(MMD)
            dh_ref[:, sl] = dh[:, sl] * et[:, r:r + 1]
        da = _dot_hi(tri_ref[...], dcs - cst_acc.T)
        ddt_ref[0] = (da * a_ref[0] + sdx)[:, 0:HPG]
        da_ref[0] += jnp.sum(da[:, 0:HPG] * dtb, axis=0, keepdims=True)
        dgm = dg.astype(MMD)
        dzv = dz_ref[...]
        dc_ref[...] = _dot(dgm, bm) + _dot(dzv, hpm, NT)
        db_ref[...] = _dot(dgm, cm, TN) + _dot(xd_ref[...], dhm, NT)
        dh_ref[...] += _dot(cm, dzv, TN)

    return pl.pallas_call(
        body, name=name, grid=(SSD_GROUPS, nc),
        in_specs=[sp["x"], sp["b"], sp["c"], sp["col"], sp["col"], sp["rowt"], sp["drow"],
                  pl.BlockSpec((1, 1, 128), lambda g, c: (g, 0, 0)), sp["y"], sp["h"],
                  pl.BlockSpec((CHUNK, CHUNK), lambda g, c: (0, 0))],
        out_specs=[sp["y"], sp["n"], sp["n"], sp["col"], pl.BlockSpec((1, 1, HPG), lambda g, c: (g, 0, 0))],
        out_shape=[jax.ShapeDtypeStruct((s, 2048), F32), jax.ShapeDtypeStruct((s, SSD_GROUPS * SSD_N), F32),
                   jax.ShapeDtypeStruct((s, SSD_GROUPS * SSD_N), F32), jax.ShapeDtypeStruct((SSD_GROUPS, s, HPG), F32),
                   jax.ShapeDtypeStruct((SSD_GROUPS, 1, HPG), F32)],
        scratch_shapes=[pltpu.VMEM((SSD_N, GW), F32), pltpu.VMEM((CHUNK, GW), MMD), pltpu.VMEM((CHUNK, GW), MMD)],
        compiler_params=_cp(("arbitrary", "arbitrary")),
    )(xc, xc, xc, dt4, cs4, cst, drow, arow4, dy, hprev, _tri(anti))


def _gnorm_fwd(ya, yb, proj, w, *, name):
    s = ya.shape[0]
    tm = _tile(s, 256)

    def body(a_ref, b_ref, z_ref, w_ref, o_ref):
        zv = z_ref[...].astype(F32)
        t = (a_ref[...] + b_ref[...]) * (zv * _sigmoid(zv))
        r = lax.rsqrt(jnp.mean(t * t, axis=-1, keepdims=True) + EPS)
        o_ref[...] = ((t * r) * w_ref[...]).astype(o_ref.dtype)

    big = pl.BlockSpec((tm, 2048), lambda i: (i, 0))
    row = pl.BlockSpec((1, 2048), lambda i: (0, 0))
    return pl.pallas_call(
        body, name=name, grid=(s // tm,), in_specs=[big, big, big, row], out_specs=big,
        out_shape=jax.ShapeDtypeStruct((s, 2048), MMD), compiler_params=_cp(("arbitrary",)),
    )(ya, yb, proj, w)


def _gnorm_bwd(dout, ya, yb, proj, w, *, name):
    s = ya.shape[0]
    tm = _tile(s, 256)

    def body(do_ref, a_ref, b_ref, z_ref, w_ref, dy_ref, dz_ref, dw_ref):
        zv = z_ref[...].astype(F32)
        sg = _sigmoid(zv)
        sz = zv * sg
        y = a_ref[...] + b_ref[...]
        t = y * sz
        r = lax.rsqrt(jnp.mean(t * t, axis=-1, keepdims=True) + EPS)
        nv = t * r
        dov = do_ref[...].astype(F32)
        _acc_rows(dw_ref, jnp.sum(dov * nv, axis=0, keepdims=True), pl.program_id(0) == 0)
        dn = dov * w_ref[...]
        dt_ = r * (dn - nv * jnp.mean(dn * nv, axis=-1, keepdims=True))
        dy_ref[...] = dt_ * sz
        dz_ref[...] = (dt_ * y * (sg * (1.0 + zv * (1.0 - sg)))).astype(dz_ref.dtype)

    big = pl.BlockSpec((tm, 2048), lambda i: (i, 0))
    row = pl.BlockSpec((1, 2048), lambda i: (0, 0))
    return pl.pallas_call(
        body, name=name, grid=(s // tm,), in_specs=[big, big, big, big, row], out_specs=[big, big, row],
        out_shape=[jax.ShapeDtypeStruct((s, 2048), F32), jax.ShapeDtypeStruct((s, 2048), MMD),
                   jax.ShapeDtypeStruct((1, 2048), F32)],
        compiler_params=_cp(("arbitrary",)),
    )(dout, ya, yb, proj, w)


def _colsum_prod(a, b, *, name):
    s, n = a.shape
    tm = _tile(s, 256)

    def body(a_ref, b_ref, o_ref):
        _acc_rows(o_ref, jnp.sum(a_ref[...].astype(F32) * b_ref[...].astype(F32), axis=0, keepdims=True),
                  pl.program_id(0) == 0)

    big = pl.BlockSpec((tm, n), lambda i: (i, 0))
    return pl.pallas_call(
        body, name=name, grid=(s // tm,), in_specs=[big, big], out_specs=pl.BlockSpec((1, n), lambda i: (0, 0)),
        out_shape=jax.ShapeDtypeStruct((1, n), F32), compiler_params=_cp(("arbitrary",)),
    )(a, b)


def _heads(a, n):
    return a.reshape(a.shape[0], n, HEAD_DIM).transpose(1, 0, 2)


def _unheads(a):
    return a.transpose(1, 0, 2).reshape(a.shape[1], a.shape[0] * HEAD_DIM)


def _per_group(a):
    return a.reshape(a.shape[0], SSD_GROUPS, HPG).transpose(1, 0, 2)


def _per_group_t(a):
    s = a.shape[0]
    return a.reshape(s // CHUNK, CHUNK, SSD_GROUPS, HPG).transpose(2, 0, 3, 1)


def _local_step(x, target, mod, wts, small):
    s, d = x.shape
    shift1, scale1, gate1, shift2, scale2, gate2 = [mod[i:i + 1] for i in range(6)]

    h1 = _ln_mod(x, small["norm1_w"], scale1, shift1, name="ln1")
    proj = _mm(h1, wts["w_in_p"], name="in_proj", outs=[MMD], tm=512, tn=2944, b_outer=True)
    dt_raw = _mm(h1, wts["w_dt"], name="dt_proj", outs=[F32], tm=512, tn=128)

    qk_w = jnp.concatenate([jnp.tile(small["q_norm_w"], (1, N_Q_HEADS)), jnp.tile(small["k_norm_w"], (1, N_KV_HEADS))], axis=1)
    qk_sc = jnp.concatenate([jnp.full((1, N_Q_HEADS * HEAD_DIM), HEAD_DIM ** -0.5, F32),
                             jnp.ones((1, N_KV_HEADS * HEAD_DIM), F32)], axis=1)
    tabs = _rope_tables(s)
    qk, qkt = _qk_fwd(proj, qk_w, qk_sc, tabs, name="qk_fwd")
    qkt = qkt.reshape(N_Q_HEADS + N_KV_HEADS, HEAD_DIM, s)
    k_h = _heads(qk[:, N_Q_HEADS * HEAD_DIM:], N_KV_HEADS)
    v_sd = proj[:, V0:V0 + N_KV_HEADS * HEAD_DIM]
    v_h = _heads(v_sd, N_KV_HEADS)
    vt = v_sd.T.reshape(N_KV_HEADS, HEAD_DIM, s)
    ot, lse = _flash_fwd(qkt, vt, name="flash_fwd")
    ot2 = ot.reshape(N_Q_HEADS * HEAD_DIM, s)

    w8 = jnp.pad(small["conv_w"], ((0, 8 - D_CONV), (0, 0)))
    xc = _conv_fwd(proj, w8, small["conv_b"], name="conv_fwd")
    a_neg = -jnp.exp(small["A_log"])
    arow = jnp.pad(a_neg.reshape(1, 2 * SSD_HEADS), ((0, 0), (0, 128 - 2 * SSD_HEADS)))
    bias_row = jnp.pad(small["dt_bias"].reshape(1, 2 * SSD_HEADS), ((0, 0), (0, 128 - 2 * SSD_HEADS)))
    dt, cs = _dt_fwd(dt_raw, bias_row, arow, name="dt_fwd")
    drow = jnp.repeat(small["ssd_D"], SSD_P, axis=1)
    dirs = []
    for di in range(2):
        cols = slice(di * SSD_HEADS, (di + 1) * SSD_HEADS)
        dirs.append(dict(
            dt4=_per_group(dt[:, cols]), cs4=_per_group(cs[:, cols]), cst=_per_group_t(cs[:, cols]),
            drow=drow if di == 0 else jnp.zeros_like(drow),
            arow4=jnp.pad(a_neg[di].reshape(SSD_GROUPS, 1, HPG), ((0, 0), (0, 0), (0, 128 - HPG)))))
    ys = []
    for di, dd in enumerate(dirs):
        y, dd["hprev"] = _ssd_fwd(xc, dd["dt4"], dd["cs4"], dd["cst"], dd["drow"], di == 1, name=f"ssd_fwd{di}")
        ys.append(y)
    ssdn = _gnorm_fwd(ys[0], ys[1], proj, small["ssd_norm_w"], name="gnorm_fwd")

    a_o = _mm(ot2, wts["w_attn_out"], name="attn_out", outs=[MMD], ta=True, tm=512, tn=1024)

    def merge_epi(acc, ao, ga, gs):
        return (_sigmoid(ga.astype(F32)) * ao.astype(F32) + _sigmoid(gs.astype(F32)) * acc, acc)

    merged, b_o = _mm(ssdn, wts["w_ssd_out"], name="ssd_out", outs=[MMD, MMD], tm=512, tn=512,
                      extras=[(a_o, "tile", 0), (proj, "tile", GA0), (proj, "tile", GS0)], epi=merge_epi)

    def res_epi(acc, res, gate):
        return (res + gate * acc, acc)

    x1, mo = _mm(merged, wts["w_o"], name="w_o", outs=[F32, MMD], tm=512, tn=512,
                 extras=[(x, "tile", 0), (gate1, "row", 0)], epi=res_epi)
    h2 = _ln_mod(x1, small["norm2_w"], scale2, shift2, name="ln2")

    def relu2_epi(acc):
        rl = jnp.maximum(acc, 0.0)
        return (rl * rl, rl)

    act, rl = _mm(h2, wts["w_mlp1"], name="mlp1", outs=[MMD, MMD], tm=512, tn=1024, epi=relu2_epi)

    def loss_epi(acc, res, gate, tgt):
        return ((res + gate * acc - tgt) * (1.0 / d), acc)

    dy, ffo = _mm(act, wts["w_mlp2"], name="mlp2", outs=[F32, MMD], tm=512, tn=512,
                  extras=[(x1, "tile", 0), (gate2, "row", 0), (target, "tile", 0)], epi=loss_epi)
    loss = _sumsq(dy, name="loss") * (0.5 * d)

    gw = {}
    gs_ = {}
    dffo, dgate2 = _gate_bwd(dy, ffo, gate2, name="gate2_bwd")
    dpre = _mm(dffo, wts["w_mlp2"], name="mlp2_dx", outs=[MMD], nt=True, tm=512, tn=1024,
               extras=[(rl, "tile", 0)], epi=lambda acc, r: (acc * (2.0 * r.astype(F32)),))
    gw["w_mlp2"] = _mm_tn(act, dffo, name="mlp2_dw")
    dh2 = _mm(dpre, wts["w_mlp1"], name="mlp1_dx", outs=[F32], nt=True, tm=512, tn=512)
    gw["w_mlp1"] = _mm_tn(h2, dpre, name="mlp1_dw")
    dx1, dshift2, dscale2, gs_["norm2_w"] = _ln_mod_bwd(dh2, x1, small["norm2_w"], scale2, dy, name="ln2_bwd")
    dmo, dgate1 = _gate_bwd(dx1, mo, gate1, name="gate1_bwd")

    def merge_bwd_epi(acc, ao, bo, ga, gs):
        sa, ss = _sigmoid(ga.astype(F32)), _sigmoid(gs.astype(F32))
        return (acc * sa, acc * ss, acc * ao.astype(F32) * sa * (1.0 - sa), acc * bo.astype(F32) * ss * (1.0 - ss))

    da_o, db_o, dga, dgs = _mm(dmo, wts["w_o"], name="w_o_dx", outs=[MMD] * 4, nt=True, tm=512, tn=512,
                               extras=[(a_o, "tile", 0), (b_o, "tile", 0), (proj, "tile", GA0), (proj, "tile", GS0)],
                               epi=merge_bwd_epi)
    gw["w_o"] = _mm_tn(merged, dmo, name="w_o_dw")
    dot = _mm(wts["w_attn_out"], da_o, name="attn_out_dx", outs=[MMD], nt=True, tm=512, tn=512)
    gw["w_attn_out"] = _mm(ot2, da_o, name="attn_out_dw", outs=[F32], tm=256, tn=512, vmem=VMEM_BIG)
    dssdn = _mm(db_o, wts["w_ssd_out"], name="ssd_out_dx", outs=[MMD], nt=True, tm=512, tn=512)
    gw["w_ssd_out"] = _mm_tn(ssdn, db_o, name="ssd_out_dw")

    dyssd, dz, gs_["ssd_norm_w"] = _gnorm_bwd(dssdn, ys[0], ys[1], proj, small["ssd_norm_w"], name="gnorm_bwd")
    gs_["ssd_D"] = _colsum_prod(dyssd, xc[:, 0:2048], name="ssd_d_grad").reshape(SSD_HEADS, SSD_P).sum(axis=1).reshape(1, SSD_HEADS)
    dxc, ddts, das = [], [], []
    for di, dd in enumerate(dirs):
        dxs, dbm, dcm, ddt4, da4 = _ssd_bwd(xc, dd["dt4"], dd["cs4"], dd["cst"], dd["drow"], dd["arow4"],
                                            dyssd, dd["hprev"], di == 1, name=f"ssd_bwd{di}")
        dxc.append(jnp.concatenate([dxs, dbm, dcm], axis=1))
        ddts.append(ddt4.transpose(1, 0, 2).reshape(s, SSD_HEADS))
        das.append(da4.reshape(1, SSD_HEADS))
    dxbc, dw8, gs_["conv_b"] = _conv_bwd(proj, dxc[0], dxc[1], w8, small["conv_b"], name="conv_bwd")
    gs_["conv_w"] = dw8[0:D_CONV]
    gs_["A_log"] = jnp.concatenate(das, axis=0) * a_neg
    ddt = jnp.pad(jnp.concatenate(ddts, axis=1), ((0, 0), (0, 128 - 2 * SSD_HEADS)))
    ddt_raw, dbias = _dt_bwd(ddt, dt_raw, bias_row, name="dt_bwd")
    gs_["dt_bias"] = dbias[:, 0:2 * SSD_HEADS].reshape(2, SSD_HEADS)

    dqt, dk_h, dv_h = _flash_bwd(qkt, k_h, v_h, dot.reshape(N_Q_HEADS, HEAD_DIM, s), ot, lse, name="flash_bwd")
    dqkt = jnp.concatenate([dqt.reshape(N_Q_HEADS * HEAD_DIM, s),
                            dk_h.transpose(0, 2, 1).reshape(N_KV_HEADS * HEAD_DIM, s)], axis=0)
    dqk_u, dqk_w = _qk_bwd(dqkt, proj, qk_w, qk_sc, tabs, name="qk_bwd")
    gs_["q_norm_w"] = dqk_w[:, 0:N_Q_HEADS * HEAD_DIM].reshape(N_Q_HEADS, HEAD_DIM).sum(axis=0, keepdims=True)
    gs_["k_norm_w"] = dqk_w[:, N_Q_HEADS * HEAD_DIM:].reshape(N_KV_HEADS, HEAD_DIM).sum(axis=0, keepdims=True)
    dv = _unheads(dv_h).astype(MMD)

    dproj = jnp.concatenate([dz, dga, dgs, dxbc, dqk_u, dv, ddt_raw], axis=1)
    dh1 = _mm(dproj, wts["w_in_p"], name="in_proj_dx", outs=[F32], nt=True, tm=256, tn=512, vmem=VMEM_BIG)
    gw["w_in_p"] = _mm_tn(h1, dproj, name="in_proj_dw", tk=512, tn=2944, tmm=1024, vmem=VMEM_BIG)
    grad_x, dshift1, dscale1, gs_["norm1_w"] = _ln_mod_bwd(dh1, x, small["norm1_w"], scale1, dx1, name="ln1_bwd")
    dmod = jnp.concatenate([dshift1, dscale1, dgate1, dshift2, dscale2, dgate2], axis=0)
    return loss, grad_x, dmod, gw, gs_


N_DEV = 8
N_CHIP = 4
ANY = pl.BlockSpec(memory_space=pl.ANY)


def _place():
    return lax.axis_index("x"), lax.axis_index("y"), lax.axis_index("c")


def _allgather8(v, *, name):
    m_per, n = v.shape

    def body(x_ref, out_ref, send_sems, recv_sems, local_sem):
        x, y, c = _place()
        me, sibling = (x, y, c), (x, y, 1 - c)
        chips = [(1 - x, y), (x, 1 - y), (1 - x, 1 - y)]

        def rows(px, py, pc):
            return out_ref.at[pl.ds((4 * px + 2 * py + pc) * m_per, m_per), :]

        def copy(k, block, to, src=None):
            return pltpu.make_async_remote_copy(
                src_ref=rows(*block) if src is None else src, dst_ref=rows(*block),
                send_sem=send_sems.at[k], recv_sem=recv_sems.at[k], device_id=to, device_id_type=MESH)

        mine = pltpu.make_async_copy(x_ref, rows(*me), local_sem)
        mine.start()
        first = [copy(0, me, sibling, src=x_ref)]
        first += [copy(1 + j, me, (*chip, c), src=x_ref) for j, chip in enumerate(chips)]
        for cp in first:
            cp.start()
        passed = [copy(4 + j, (*chip, c), sibling) for j, chip in enumerate(chips)]
        for j, chip in enumerate(chips):
            copy(1 + j, (*chip, c), me).wait_recv()
            passed[j].start()
        copy(0, sibling, me).wait_recv()
        for j, chip in enumerate(chips):
            copy(4 + j, (*chip, 1 - c), me).wait_recv()
        for cp in first + passed:
            cp.wait_send()
        mine.wait()

    return pl.pallas_call(
        body, name=name, out_shape=jax.ShapeDtypeStruct((N_DEV * m_per, n), v.dtype),
        in_specs=[pl.BlockSpec(memory_space=pltpu.VMEM)], out_specs=pl.BlockSpec(memory_space=pltpu.VMEM),
        scratch_shapes=[pltpu.SemaphoreType.DMA((7,)), pltpu.SemaphoreType.DMA((7,)), pltpu.SemaphoreType.DMA],
    )(v)


def _chip_exchange(src, scatter, *, name):
    shape = src.shape[1:] if scatter else src.shape

    def body(x_ref, out_ref, send_sems, recv_sems, local_sem):
        x, y, c = _place()
        k = 2 * x + y
        chips = [(1 - x, y), (x, 1 - y), (1 - x, 1 - y)]
        ids = [2 * cx + cy for cx, cy in chips]

        def outgoing(j):
            return x_ref.at[ids[j]] if scatter else x_ref

        mine = pltpu.make_async_copy(x_ref.at[k] if scatter else x_ref, out_ref.at[k], local_sem)
        mine.start()
        sends = [pltpu.make_async_remote_copy(
            src_ref=outgoing(j), dst_ref=out_ref.at[k], send_sem=send_sems.at[j], recv_sem=recv_sems.at[j],
            device_id=(cx, cy, c), device_id_type=MESH) for j, (cx, cy) in enumerate(chips)]
        for cp in sends:
            cp.start()
        for j, (cx, cy) in enumerate(chips):
            pltpu.make_async_remote_copy(
                src_ref=outgoing(j), dst_ref=out_ref.at[ids[j]], send_sem=send_sems.at[j], recv_sem=recv_sems.at[j],
                device_id=(cx, cy, c), device_id_type=MESH).wait_recv()
        for cp in sends:
            cp.wait_send()
        mine.wait()

    return pl.pallas_call(
        body, name=name, out_shape=jax.ShapeDtypeStruct((N_CHIP,) + tuple(shape), src.dtype),
        in_specs=[ANY], out_specs=ANY,
        scratch_shapes=[pltpu.SemaphoreType.DMA((3,)), pltpu.SemaphoreType.DMA((3,)), pltpu.SemaphoreType.DMA],
    )(src)


def _row_tile(r, pref=512):
    return max(t for t in range(16, pref + 1, 16) if r % t == 0)


def _sibling_swap(a, *, name):
    def body(x_ref, out_ref, send_sem, recv_sem):
        x, y, c = _place()
        cp = pltpu.make_async_remote_copy(src_ref=x_ref, dst_ref=out_ref, send_sem=send_sem, recv_sem=recv_sem,
                                          device_id=(x, y, 1 - c), device_id_type=MESH)
        cp.start()
        cp.wait()

    return pl.pallas_call(
        body, name=name, out_shape=jax.ShapeDtypeStruct(a.shape, a.dtype), in_specs=[ANY], out_specs=ANY,
        scratch_shapes=[pltpu.SemaphoreType.DMA, pltpu.SemaphoreType.DMA],
    )(a)


def _sum_slots(a, *, name):
    _, r, c = a.shape
    tr = _row_tile(r)

    def body(a_ref, o_ref):
        acc = a_ref[0].astype(F32)
        for j in range(1, N_CHIP):
            acc = acc + a_ref[j].astype(F32)
        o_ref[...] = acc

    return pl.pallas_call(
        body, name=name, grid=(r // tr,), in_specs=[pl.BlockSpec((N_CHIP, tr, c), lambda i: (0, i, 0))],
        out_specs=pl.BlockSpec((tr, c), lambda i: (i, 0)), out_shape=jax.ShapeDtypeStruct((r, c), F32),
        compiler_params=_cp(("arbitrary",)),
    )(a)


def _add2(a, b, *, name):
    r, c = a.shape
    tr = _row_tile(r)

    def body(a_ref, b_ref, o_ref):
        o_ref[...] = a_ref[...] + b_ref[...]

    spec = pl.BlockSpec((tr, c), lambda i: (i, 0))
    return pl.pallas_call(
        body, name=name, grid=(r // tr,), in_specs=[spec, spec], out_specs=spec,
        out_shape=jax.ShapeDtypeStruct((r, c), F32), compiler_params=_cp(("arbitrary",)),
    )(a, b)


BIG = ("w_in", "w_mlp1", "w_attn_out", "w_ssd_out", "w_o", "w_mlp2")
COL_SHARDED = ("w_in", "w_mlp1")
SMALL = ("b_ada", "norm1_w", "norm2_w", "q_norm_w", "k_norm_w", "conv_b", "A_log", "dt_bias", "ssd_D", "ssd_norm_w")
NAMES = ("w_ada", "b_ada", "norm1_w", "norm2_w", "w_in", "q_norm_w", "k_norm_w", "conv_w", "conv_b", "A_log", "dt_bias",
         "ssd_D", "ssd_norm_w", "w_attn_out", "w_ssd_out", "w_o", "w_mlp1", "w_mlp2")
W_IN_COLS = 8768


def _permute_in(w):
    return jnp.concatenate([w[:, 4608:6656], w[:, 6720:8768], w[:, 1536:4608], w[:, 0:1536], w[:, 6656:6720],
                            jnp.zeros((w.shape[0], PW - W_IN_COLS), w.dtype)], axis=1)


def _unpermute_in(wp):
    return jnp.concatenate([wp[:, Q0:DT0], wp[:, XS0:Q0], wp[:, Z0:GA0], wp[:, DT0:DT0 + 64], wp[:, GA0:XS0]], axis=1)


def _pad_to(v, n):
    return jnp.pad(v, (0, n - v.shape[0]))


def _step(w, m, v, loss_target):
    xi, yi, ci = _place()
    chip = 2 * xi + yi
    dev = 4 * xi + 2 * yi + ci
    x, tgt = w["x"], loss_target
    d = x.shape[1]

    cw = w["conv_w"].shape[1]
    v0 = _pad_to(jnp.concatenate([w["c"].reshape(-1), w["conv_w"].reshape(-1)]), 5120).reshape(8, 640)
    g0 = _allgather8(v0, name="ag_cond").reshape(N_DEV, 5120)
    c_all = g0[:, 0:d]
    conv_w = jnp.concatenate([g0[2 * k, d:d + D_CONV * cw].reshape(D_CONV, cw) for k in range(N_CHIP)], axis=1)
    sc = _silu_cast(c_all, name="silu_c")
    modp = _mm(sc, w["w_ada"].astype(MMD), name="ada_fwd", outs=[F32], tm=8, tn=512)
    g1 = _allgather8(modp, name="ag_mod").reshape(N_DEV, N_DEV, modp.shape[1])
    mod_all = jnp.concatenate([g1[2 * k] for k in range(N_CHIP)], axis=1)
    mod = (lax.dynamic_slice_in_dim(mod_all, dev, 1, axis=0) + w["b_ada"]).reshape(6, d)

    packed = jnp.concatenate([w[n].astype(MMD).reshape(-1, d) for n in BIG], axis=0)
    gath = _chip_exchange(packed, False, name="ag_weights")
    full, r0 = {}, 0
    for n in BIG:
        rows = w[n].size // d
        part = gath[:, r0:r0 + rows]
        if n in COL_SHARDED:
            full[n] = jnp.concatenate([part[k].reshape(w[n].shape) for k in range(N_CHIP)], axis=1)
        else:
            full[n] = part.reshape(N_CHIP * w[n].shape[0], w[n].shape[1])
        r0 += rows
    wts = {n: full[n] for n in BIG if n != "w_in"}
    wts["w_in_p"] = _permute_in(full["w_in"])
    wts["w_dt"] = jnp.pad(full["w_in"][:, 6656:6720], ((0, 0), (0, 64)))
    small = {n: w[n] for n in SMALL if n != "b_ada"}
    small["conv_w"] = conv_w

    loss, grad_x, dmod, gw, gs = _local_step(x, tgt, mod, wts, small)
    loss = lax.psum(loss[0, 0], ("x", "y", "c"))

    gw["w_in"] = _unpermute_in(gw.pop("w_in_p"))
    slots = []
    for k in range(N_CHIP):
        parts = []
        for n in BIG:
            r_, c_ = w[n].shape
            blk = gw[n][:, k * c_:(k + 1) * c_] if n in COL_SHARDED else gw[n][k * r_:(k + 1) * r_]
            parts.append(blk.astype(MMD).reshape(-1, d))
        slots.append(jnp.concatenate(parts, axis=0))
    recv = _chip_exchange(jnp.stack(slots), True, name="rs_grads")
    mine = _sum_slots(recv, name="rs_sum")
    total = _add2(mine, _sibling_swap(mine, name="rs_sibling"), name="rs_add")
    grads, r0 = {}, 0
    for n in BIG:
        rows = w[n].size // d
        grads[n] = total[r0:r0 + rows].reshape(w[n].shape)
        r0 += rows

    order = [dmod.reshape(-1)] + [gs[n].reshape(-1) for n in SMALL if n != "b_ada"] + [gs["conv_w"].reshape(-1)]
    vec = jnp.concatenate(order)
    n_small = vec.shape[0]
    n_pad = -(-n_small // 1024) * 1024
    g2 = _allgather8(_pad_to(vec, n_pad).reshape(8, n_pad // 8), name="ag_small")
    tot = _rows_sum(g2, N_DEV, name="small_sum").reshape(-1)
    dmod_all = g2.reshape(N_DEV, n_pad)[:, 0:6 * d]
    off = 0
    for n in SMALL:
        grads[n] = tot[off:off + w[n].size].reshape(w[n].shape)
        off += w[n].size
    conv_full = tot[off:off + D_CONV * N_CHIP * cw].reshape(D_CONV, N_CHIP * cw)
    grads["conv_w"] = lax.dynamic_slice_in_dim(conv_full, chip * cw, cw, axis=1)
    ada_cols = w["w_ada"].shape[1]
    dmod_mine = lax.dynamic_slice_in_dim(dmod_all, chip * ada_cols, ada_cols, axis=1).astype(MMD)
    grads["w_ada"] = _mm_tn(sc, dmod_mine, name="ada_dw", tk=512, tn=512, tmm=8)

    delta, new_m, new_v = {}, {}, {}
    pack = lambda t: jnp.concatenate([t[n].reshape(-1) for n in SMALL]).reshape(1, -1)
    ds_, ms_, vs_ = _adamw(pack(w), pack(grads), pack(m), pack(v), name="adamw_small")
    off = 0
    for n in SMALL:
        for dst, src in ((delta, ds_), (new_m, ms_), (new_v, vs_)):
            dst[n] = src[0, off:off + w[n].size].reshape(w[n].shape)
        off += w[n].size
    for n in ("w_ada", "conv_w") + BIG:
        delta[n], new_m[n], new_v[n] = _adamw(w[n], grads[n], m[n], v[n], name="adamw_" + n)
    return loss, grad_x, grads, delta, new_m, new_v


def kernel(x, c, w_ada, b_ada, norm1_w, norm2_w, w_in, q_norm_w, k_norm_w, conv_w, conv_b, A_log, dt_bias, ssd_D, ssd_norm_w, w_attn_out, w_ssd_out, w_o, w_mlp1, w_mlp2, loss_target, m_w_ada, m_b_ada, m_norm1_w, m_norm2_w, m_w_in, m_q_norm_w, m_k_norm_w, m_conv_w, m_conv_b, m_A_log, m_dt_bias, m_ssd_D, m_ssd_norm_w, m_w_attn_out, m_w_ssd_out, m_w_o, m_w_mlp1, m_w_mlp2, v_w_ada, v_b_ada, v_norm1_w, v_norm2_w, v_w_in, v_q_norm_w, v_k_norm_w, v_conv_w, v_conv_b, v_A_log, v_dt_bias, v_ssd_D, v_ssd_norm_w, v_w_attn_out, v_w_ssd_out, v_w_o, v_w_mlp1, v_w_mlp2):
    args = dict(locals())
    strip = lambda a: a[0] if a.ndim == 3 else a
    w = {n: strip(args[n]) for n in NAMES + ("x", "c")}
    m = {n: strip(args["m_" + n]) for n in NAMES}
    v = {n: strip(args["v_" + n]) for n in NAMES}
    loss, grad_x, grads, delta, new_m, new_v = _step(w, m, v, loss_target[0])
    like = lambda t, n: t.reshape(args[n].shape)
    return (loss, grad_x[None], *[like(grads[n], n) for n in NAMES], *[like(delta[n], n) for n in NAMES],
            *[like(new_m[n], n) for n in NAMES], *[like(new_v[n], n) for n in NAMES])
```

```python
import functools
import math

import jax
import jax.numpy as jnp
from jax import lax
from jax.experimental import pallas as pl
from jax.experimental.pallas import tpu as pltpu

F32 = jnp.float32
MMD = jnp.bfloat16
EPS = 1e-6
NEG = -1e30
MIB = 1024 * 1024
VMEM_BIG = 56 * MIB
VMEM_MID = 40 * MIB

GRID_W = 64
N_Q_HEADS, N_KV_HEADS, HEAD_DIM = 16, 4, 64
ROPE_THETA = 10000.0
SSD_HEADS, SSD_GROUPS, SSD_P, SSD_N, CHUNK = 32, 4, 64, 128, 128
HPG = SSD_HEADS // SSD_GROUPS
D_CONV = 5
ADAM_LR, ADAM_B1, ADAM_B2, ADAM_EPS, ADAM_WD, ADAM_STEP = 0.001, 0.9, 0.999, 1e-08, 0.01, 10

Z0, GA0, GS0, XS0, B0, C0, Q0, K0, V0, DT0, PW = 0, 2048, 3072, 4096, 6144, 6656, 7168, 8192, 8448, 8704, 8832

MESH = pl.DeviceIdType.MESH
NT = (((1,), (1,)), ((), ()))
TN = (((0,), (0,)), ((), ()))


def _cp(sem=None, vmem=VMEM_MID):
    return pltpu.CompilerParams(dimension_semantics=sem, vmem_limit_bytes=vmem)


def _tile(n, pref):
    t = min(n, pref)
    while n % t:
        t //= 2
    return t


def _dot(a, b, dims=None):
    if dims is None:
        return jnp.dot(a, b, preferred_element_type=F32)
    return lax.dot_general(a, b, dims, preferred_element_type=F32)


def _dot_hi(a, b):
    return jnp.dot(a, b, precision=lax.Precision.HIGHEST, preferred_element_type=F32)


def _sigmoid(x):
    return jax.nn.sigmoid(x)


def _mm(a, b, *, name, outs, nt=False, ta=False, extras=(), epi=None, tm=512, tn=512, n=None, b_outer=False,
        vmem=VMEM_MID):
    assert not (nt and ta)
    k, m = a.shape if ta else a.shape[::-1]
    if n is None:
        n = b.shape[0] if nt else b.shape[1]
    tm, tn = _tile(m, tm), _tile(n, tn)
    gi, gj = m // tm, n // tn
    if b_outer:
        grid = (gj, gi)
        ij = lambda p, q: (q, p)
    else:
        grid = (gi, gj)
        ij = lambda p, q: (p, q)
    if ta:
        a_spec = pl.BlockSpec((k, tm), lambda p, q: (0, ij(p, q)[0]))
    else:
        a_spec = pl.BlockSpec((tm, k), lambda p, q: (ij(p, q)[0], 0))
    if nt:
        b_spec = pl.BlockSpec((tn, k), lambda p, q: (ij(p, q)[1], 0))
    else:
        b_spec = pl.BlockSpec((k, tn), lambda p, q: (0, ij(p, q)[1]))
    e_specs = []
    for arr, kind, off in extras:
        ob = off // tn
        assert off % tn == 0
        if kind == "tile":
            e_specs.append(pl.BlockSpec((tm, tn), lambda p, q, ob=ob: (ij(p, q)[0], ob + ij(p, q)[1])))
        else:
            e_specs.append(pl.BlockSpec((1, tn), lambda p, q, ob=ob: (0, ob + ij(p, q)[1])))
    ne = len(extras)

    def body(a_ref, b_ref, *rest):
        acc = _dot(a_ref[...], b_ref[...], NT if nt else (TN if ta else None))
        res = epi(acc, *[e[...] for e in rest[:ne]]) if epi is not None else (acc,)
        for o_ref, r in zip(rest[ne:], res):
            o_ref[...] = r.astype(o_ref.dtype)

    out = pl.pallas_call(
        body, name=name, grid=grid,
        in_specs=[a_spec, b_spec] + e_specs,
        out_specs=[pl.BlockSpec((tm, tn), lambda p, q: ij(p, q)) for _ in outs],
        out_shape=[jax.ShapeDtypeStruct((m, n), dt) for dt in outs],
        compiler_params=_cp(("arbitrary", "arbitrary"), vmem),
    )(a, b, *[e[0] for e in extras])
    return out if len(outs) > 1 else out[0]


def _mm_tn(a, g, *, name, tk=512, tn=1024, tmm=1024, vmem=VMEM_MID):
    m, k = a.shape
    n = g.shape[1]
    tk, tn, tmm = _tile(k, tk), _tile(n, tn), _tile(m, tmm)

    def body(a_ref, g_ref, o_ref):
        p = _dot(a_ref[...], g_ref[...], TN)

        @pl.when(pl.program_id(2) == 0)
        def _():
            o_ref[...] = p

        @pl.when(pl.program_id(2) > 0)
        def _():
            o_ref[...] += p

    return pl.pallas_call(
        body, name=name, grid=(k // tk, n // tn, m // tmm),
        in_specs=[pl.BlockSpec((tmm, tk), lambda i, j, r: (r, i)), pl.BlockSpec((tmm, tn), lambda i, j, r: (r, j))],
        out_specs=pl.BlockSpec((tk, tn), lambda i, j, r: (i, j)),
        out_shape=jax.ShapeDtypeStruct((k, n), F32),
        compiler_params=_cp(("arbitrary", "arbitrary", "arbitrary"), vmem),
    )(a, g)


def _adamw(w, g, m, v, *, name):
    r, c = w.shape
    tr = _tile(r, 256) if r % 8 == 0 else r

    def body(w_ref, g_ref, m_ref, v_ref, d_ref, nm_ref, nv_ref):
        gg = g_ref[...]
        nm = ADAM_B1 * m_ref[...] + (1.0 - ADAM_B1) * gg
        nv = ADAM_B2 * v_ref[...] + (1.0 - ADAM_B2) * jnp.square(gg)
        m_hat = nm / (1.0 - ADAM_B1 ** ADAM_STEP)
        v_hat = nv / (1.0 - ADAM_B2 ** ADAM_STEP)
        d_ref[...] = -ADAM_LR * (m_hat / (jnp.sqrt(v_hat) + ADAM_EPS) + ADAM_WD * w_ref[...])
        nm_ref[...] = nm
        nv_ref[...] = nv

    spec = pl.BlockSpec((tr, c), lambda i: (i, 0))
    return pl.pallas_call(
        body, name=name, grid=(r // tr,), in_specs=[spec] * 4, out_specs=[spec] * 3,
        out_shape=[jax.ShapeDtypeStruct((r, c), F32)] * 3, compiler_params=_cp(("arbitrary",)),
    )(w, g, m, v)


def _rows_sum(a, groups, *, name):
    r = a.shape[0] // groups

    def body(a_ref, o_ref):
        acc = a_ref[0:r, :]
        for d in range(1, groups):
            acc = acc + a_ref[d * r:(d + 1) * r, :]
        o_ref[...] = acc

    return pl.pallas_call(body, name=name, out_shape=jax.ShapeDtypeStruct((r, a.shape[1]), F32))(a)


def _silu_cast(a, *, name):
    def body(a_ref, o_ref):
        x = a_ref[...]
        o_ref[...] = (x * _sigmoid(x)).astype(o_ref.dtype)

    return pl.pallas_call(body, name=name, out_shape=jax.ShapeDtypeStruct(a.shape, MMD))(a)


def _sumsq(a, *, name):
    m, n = a.shape
    tm = _tile(m, 512)

    def body(a_ref, o_ref):
        x = a_ref[...]
        p = jnp.sum(jnp.sum(x * x, axis=1, keepdims=True), axis=0, keepdims=True)

        @pl.when(pl.program_id(0) == 0)
        def _():
            o_ref[...] = p

        @pl.when(pl.program_id(0) > 0)
        def _():
            o_ref[...] += p

    return pl.pallas_call(
        body, name=name, grid=(m // tm,), in_specs=[pl.BlockSpec((tm, n), lambda i: (i, 0))],
        out_specs=pl.BlockSpec((1, 1), lambda i: (0, 0)), out_shape=jax.ShapeDtypeStruct((1, 1), F32),
        compiler_params=_cp(("arbitrary",)),
    )(a)


def _acc_rows(o_ref, p, first):
    @pl.when(first)
    def _():
        o_ref[...] = p

    @pl.when(jnp.logical_not(first))
    def _():
        o_ref[...] += p


def _ln_mod(x, w, scale, shift, *, name):
    s, d = x.shape
    tm = _tile(s, 512)

    def body(x_ref, w_ref, sc_ref, sh_ref, o_ref):
        xv = x_ref[...]
        r = lax.rsqrt(jnp.mean(xv * xv, axis=-1, keepdims=True) + EPS)
        o_ref[...] = ((xv * r) * w_ref[...] * (1.0 + sc_ref[...]) + sh_ref[...]).astype(o_ref.dtype)

    row = pl.BlockSpec((1, d), lambda i: (0, 0))
    big = pl.BlockSpec((tm, d), lambda i: (i, 0))
    return pl.pallas_call(
        body, name=name, grid=(s // tm,), in_specs=[big, row, row, row], out_specs=big,
        out_shape=jax.ShapeDtypeStruct((s, d), MMD), compiler_params=_cp(("arbitrary",)),
    )(x, w, scale, shift)


def _ln_mod_bwd(dh, x, w, scale, dres, *, name):
    s, d = x.shape
    tm = _tile(s, 512)

    def body(dh_ref, x_ref, w_ref, sc_ref, dres_ref, dx_ref, dsh_ref, dsc_ref, dw_ref):
        xv = x_ref[...]
        dhv = dh_ref[...].astype(F32)
        r = lax.rsqrt(jnp.mean(xv * xv, axis=-1, keepdims=True) + EPS)
        nv = xv * r
        wv = w_ref[...]
        g1 = 1.0 + sc_ref[...]
        dn = dhv * (wv * g1)
        dx_ref[...] = dres_ref[...] + r * (dn - nv * jnp.mean(dn * nv, axis=-1, keepdims=True))
        first = pl.program_id(0) == 0
        _acc_rows(dsh_ref, jnp.sum(dhv, axis=0, keepdims=True), first)
        _acc_rows(dsc_ref, jnp.sum(dhv * nv * wv, axis=0, keepdims=True), first)
        _acc_rows(dw_ref, jnp.sum(dhv * nv * g1, axis=0, keepdims=True), first)

    row = pl.BlockSpec((1, d), lambda i: (0, 0))
    big = pl.BlockSpec((tm, d), lambda i: (i, 0))
    return pl.pallas_call(
        body, name=name, grid=(s // tm,), in_specs=[big, big, row, row, big], out_specs=[big, row, row, row],
        out_shape=[jax.ShapeDtypeStruct((s, d), F32)] + [jax.ShapeDtypeStruct((1, d), F32)] * 3,
        compiler_params=_cp(("arbitrary",)),
    )(dh, x, w, scale, dres)


def _gate_bwd(dy, u, gate, *, name):
    s, d = dy.shape
    tm = _tile(s, 512)

    def body(dy_ref, u_ref, g_ref, du_ref, dg_ref):
        dyv = dy_ref[...]
        du_ref[...] = (dyv * g_ref[...]).astype(du_ref.dtype)
        _acc_rows(dg_ref, jnp.sum(dyv * u_ref[...].astype(F32), axis=0, keepdims=True), pl.program_id(0) == 0)

    row = pl.BlockSpec((1, d), lambda i: (0, 0))
    big = pl.BlockSpec((tm, d), lambda i: (i, 0))
    return pl.pallas_call(
        body, name=name, grid=(s // tm,), in_specs=[big, big, row], out_specs=[big, row],
        out_shape=[jax.ShapeDtypeStruct((s, d), MMD), jax.ShapeDtypeStruct((1, d), F32)],
        compiler_params=_cp(("arbitrary",)),
    )(dy, u, gate)


def _seg64(v, e):
    hi = v.astype(jnp.bfloat16)
    lo = (v - hi.astype(F32)).astype(jnp.bfloat16)
    return _dot(hi, e) + _dot(lo, e)


def _rope_tables(s):
    rows = s // GRID_W
    pos_row = jnp.repeat(jnp.arange(rows, dtype=jnp.int32), GRID_W).astype(F32)
    pos_col = jnp.tile(jnp.arange(GRID_W, dtype=jnp.int32), rows).astype(F32)
    axis_dim = HEAD_DIM // 2
    inv_freq = ROPE_THETA ** (-jnp.arange(0, axis_dim, 2, dtype=F32) / axis_dim)
    ang_r = pos_row[:, None] * inv_freq[None, :]
    ang_c = pos_col[:, None] * inv_freq[None, :]
    zero = jnp.zeros_like(ang_r)
    cos = jnp.concatenate([jnp.cos(ang_r), jnp.cos(ang_r), jnp.cos(ang_c), jnp.cos(ang_c)], axis=1)
    s_a = jnp.concatenate([-jnp.sin(ang_r), zero, -jnp.sin(ang_c), zero], axis=1)
    s_b = jnp.concatenate([zero, jnp.sin(ang_r), zero, jnp.sin(ang_c)], axis=1)
    return [jnp.tile(t, (1, 2)) for t in (cos, s_a, s_b)]


def _e128():
    i = jnp.arange(128)
    return (i[:, None] // 64 == i[None, :] // 64).astype(jnp.bfloat16)


QKW = N_Q_HEADS * HEAD_DIM + N_KV_HEADS * HEAD_DIM


def _qk_fwd(proj, wrow, scrow, tabs, *, name):
    s = proj.shape[0]
    tm = _tile(s, 512)

    def body(x_ref, w_ref, sc_ref, cos_ref, sa_ref, sb_ref, e_ref, o_ref, ot_ref):
        u = x_ref[...].astype(F32)
        r = lax.rsqrt(_seg64(u * u, e_ref[...]) * (1.0 / HEAD_DIM) + EPS)
        nv = (u * r) * w_ref[...]
        ro = nv * cos_ref[...] + pltpu.roll(nv, 112, 1) * sa_ref[...] + pltpu.roll(nv, 16, 1) * sb_ref[...]
        out = ro * sc_ref[...]
        o_ref[...] = out.astype(o_ref.dtype)
        ot_ref[...] = out.T.astype(ot_ref.dtype)

    tab = pl.BlockSpec((tm, 128), lambda i, j: (i, 0))
    row = pl.BlockSpec((1, 128), lambda i, j: (0, j))
    return pl.pallas_call(
        body, name=name, grid=(s // tm, QKW // 128),
        in_specs=[pl.BlockSpec((tm, 128), lambda i, j: (i, Q0 // 128 + j)), row, row, tab, tab, tab,
                  pl.BlockSpec((128, 128), lambda i, j: (0, 0))],
        out_specs=[pl.BlockSpec((tm, 128), lambda i, j: (i, j)), pl.BlockSpec((128, tm), lambda i, j: (j, i))],
        out_shape=[jax.ShapeDtypeStruct((s, QKW), MMD), jax.ShapeDtypeStruct((QKW, s), MMD)],
        compiler_params=_cp(("arbitrary", "arbitrary")),
    )(proj, wrow, scrow, *tabs, _e128())


def _qk_bwd(dqkt, proj, wrow, scrow, tabs, *, name):
    s = proj.shape[0]
    tm = _tile(s, 512)

    def body(d_ref, x_ref, w_ref, sc_ref, cos_ref, sa_ref, sb_ref, e_ref, du_ref, dw_ref):
        e = e_ref[...]
        d = d_ref[...].T * sc_ref[...]
        dn = d * cos_ref[...] + pltpu.roll(d * sa_ref[...], 16, 1) + pltpu.roll(d * sb_ref[...], 112, 1)
        u = x_ref[...].astype(F32)
        r = lax.rsqrt(_seg64(u * u, e) * (1.0 / HEAD_DIM) + EPS)
        uh = u * r
        _acc_rows(dw_ref, jnp.sum(dn * uh, axis=0, keepdims=True), pl.program_id(1) == 0)
        dnw = dn * w_ref[...]
        du_ref[...] = (r * (dnw - uh * (_seg64(dnw * uh, e) * (1.0 / HEAD_DIM)))).astype(du_ref.dtype)

    tab = pl.BlockSpec((tm, 128), lambda j, i: (i, 0))
    row = pl.BlockSpec((1, 128), lambda j, i: (0, j))
    return pl.pallas_call(
        body, name=name, grid=(QKW // 128, s // tm),
        in_specs=[pl.BlockSpec((128, tm), lambda j, i: (j, i)), pl.BlockSpec((tm, 128), lambda j, i: (i, Q0 // 128 + j)),
                  row, row, tab, tab, tab, pl.BlockSpec((128, 128), lambda j, i: (0, 0))],
        out_specs=[pl.BlockSpec((tm, 128), lambda j, i: (i, j)), row],
        out_shape=[jax.ShapeDtypeStruct((s, QKW), MMD), jax.ShapeDtypeStruct((1, QKW), F32)],
        compiler_params=_cp(("arbitrary", "arbitrary")),
    )(dqkt, proj, wrow, scrow, *tabs, _e128())


REP = N_Q_HEADS // N_KV_HEADS


def _lanes(ref):
    return jnp.concatenate([ref[r] for r in range(REP)], axis=1)


def _flash_fwd(qkt, vt, *, name):
    s = qkt.shape[2]
    tq, tk = _tile(s, 1024), _tile(s, 512)
    nk = s // tk
    lanes = REP * tq

    def body(q_ref, k_ref, v_ref, o_ref, lse_ref, m_ref, l_ref, acc_ref):
        j = pl.program_id(2)

        @pl.when(j == 0)
        def _():
            m_ref[...] = jnp.full_like(m_ref, NEG)
            l_ref[...] = jnp.zeros_like(l_ref)
            acc_ref[...] = jnp.zeros_like(acc_ref)

        st = _dot(k_ref[0], _lanes(q_ref), TN)
        m_prev = m_ref[...]
        m_new = jnp.maximum(m_prev, jnp.max(st, axis=0, keepdims=True))
        p = jnp.exp(st - m_new)
        alpha = jnp.exp(m_prev - m_new)
        l_ref[...] = alpha * l_ref[...] + jnp.sum(p, axis=0, keepdims=True)
        acc_ref[...] = alpha * acc_ref[...] + _dot(v_ref[0], p.astype(MMD))
        m_ref[...] = m_new

        @pl.when(j == nk - 1)
        def _():
            l = l_ref[...]
            o = acc_ref[...] / l
            ls = m_ref[...] + jnp.log(l)
            for r in range(REP):
                o_ref[r] = o[:, r * tq:(r + 1) * tq].astype(o_ref.dtype)
                lse_ref[r] = ls[:, r * tq:(r + 1) * tq]

    qspec = pl.BlockSpec((REP, HEAD_DIM, tq), lambda g, i, j: (g, 0, i))
    return pl.pallas_call(
        body, name=name, grid=(N_KV_HEADS, s // tq, nk),
        in_specs=[qspec, pl.BlockSpec((1, HEAD_DIM, tk), lambda g, i, j: (N_Q_HEADS + g, 0, j)),
                  pl.BlockSpec((1, HEAD_DIM, tk), lambda g, i, j: (g, 0, j))],
        out_specs=[qspec, pl.BlockSpec((REP, 1, tq), lambda g, i, j: (g, 0, i))],
        out_shape=[jax.ShapeDtypeStruct((N_Q_HEADS, HEAD_DIM, s), MMD), jax.ShapeDtypeStruct((N_Q_HEADS, 1, s), F32)],
        scratch_shapes=[pltpu.VMEM((1, lanes), F32), pltpu.VMEM((1, lanes), F32), pltpu.VMEM((HEAD_DIM, lanes), F32)],
        compiler_params=_cp(("arbitrary", "arbitrary", "arbitrary"), VMEM_BIG),
    )(qkt, qkt, vt)


def _flash_bwd(qkt, k_h, v_h, dot, ot, lse, *, name):
    s = qkt.shape[2]
    tq, tk = _tile(s, 512), _tile(s, 512)
    nk = s // tk

    def body(q_ref, kt_ref, k_ref, v_ref, do_ref, o_ref, lse_ref, dq_ref, dk_ref, dv_ref, dq_acc):
        i, j = pl.program_id(1), pl.program_id(2)
        q, do = _lanes(q_ref), _lanes(do_ref)
        delta = jnp.sum(do.astype(F32) * _lanes(o_ref).astype(F32), axis=0, keepdims=True)
        k, v = k_ref[0], v_ref[0]
        p = jnp.exp(_dot(k, q) - _lanes(lse_ref))
        dvc = _dot(p.astype(MMD), do, NT)
        ds = (p * (_dot(v, do) - delta)).astype(MMD)
        dkc = _dot(ds, q, NT)
        dqc = _dot(kt_ref[0], ds)
        rows = pl.ds(pl.multiple_of(j * tk, tk), tk)

        @pl.when(i == 0)
        def _():
            dk_ref[0, rows, :] = dkc
            dv_ref[0, rows, :] = dvc

        @pl.when(i > 0)
        def _():
            dk_ref[0, rows, :] += dkc
            dv_ref[0, rows, :] += dvc

        @pl.when(j == 0)
        def _():
            dq_acc[...] = dqc

        @pl.when(j > 0)
        def _():
            dq_acc[...] += dqc

        @pl.when(j == nk - 1)
        def _():
            acc = dq_acc[...]
            for r in range(REP):
                dq_ref[r] = acc[:, r * tq:(r + 1) * tq]

    qspec = pl.BlockSpec((REP, HEAD_DIM, tq), lambda g, i, j: (g, 0, i))
    kvin = pl.BlockSpec((1, tk, HEAD_DIM), lambda g, i, j: (g, j, 0))
    kvres = pl.BlockSpec((1, s, HEAD_DIM), lambda g, i, j: (g, 0, 0))
    return pl.pallas_call(
        body, name=name, grid=(N_KV_HEADS, s // tq, nk),
        in_specs=[qspec, pl.BlockSpec((1, HEAD_DIM, tk), lambda g, i, j: (N_Q_HEADS + g, 0, j)), kvin, kvin,
                  qspec, qspec, pl.BlockSpec((REP, 1, tq), lambda g, i, j: (g, 0, i))],
        out_specs=[qspec, kvres, kvres],
        out_shape=[jax.ShapeDtypeStruct((N_Q_HEADS, HEAD_DIM, s), F32), jax.ShapeDtypeStruct((N_KV_HEADS, s, HEAD_DIM), F32),
                   jax.ShapeDtypeStruct((N_KV_HEADS, s, HEAD_DIM), F32)],
        scratch_shapes=[pltpu.VMEM((HEAD_DIM, REP * tq), F32)],
        compiler_params=_cp(("arbitrary", "arbitrary", "arbitrary"), VMEM_BIG),
    )(qkt, qkt, k_h, v_h, dot, ot, lse)


HALO = 8
CONV_W = 2048 + 2 * SSD_GROUPS * SSD_N


def _shifted(win, off, r):
    return pltpu.roll(win, (r + 2 * HALO - off) % (r + 2 * HALO), 0)[0:r]


def _conv_fwd(proj, w8, brow, *, name):
    s = proj.shape[0]
    cb = 256
    r = _tile(s, 512)

    def body(x_ref, w_ref, b_ref, o_ref, pad_ref):
        zeros = jnp.zeros((HALO, cb), F32)
        pad_ref[0:HALO, :] = zeros
        pad_ref[s + HALO:s + 2 * HALO, :] = zeros

        def fill(i, carry):
            st = pl.multiple_of(i * r, r)
            pad_ref[pl.ds(st + HALO, r), :] = x_ref[pl.ds(st, r), :].astype(F32)
            return carry

        lax.fori_loop(0, s // r, fill, 0)
        wv = w_ref[...]
        bv = b_ref[...]

        def step(i, carry):
            st = pl.multiple_of(i * r, r)
            win = pad_ref[pl.ds(st, r + 2 * HALO), :]
            acc = bv + wv[0:1, :] * _shifted(win, HALO - 2, r)
            for t in range(1, D_CONV):
                acc = acc + wv[t:t + 1, :] * _shifted(win, HALO - 2 + t, r)
            o_ref[pl.ds(st, r), :] = (acc * _sigmoid(acc)).astype(o_ref.dtype)
            return carry

        lax.fori_loop(0, s // r, step, 0)

    return pl.pallas_call(
        body, name=name, grid=(CONV_W // cb,),
        in_specs=[pl.BlockSpec((s, cb), lambda j: (0, XS0 // cb + j)), pl.BlockSpec((8, cb), lambda j: (0, j)),
                  pl.BlockSpec((1, cb), lambda j: (0, j))],
        out_specs=pl.BlockSpec((s, cb), lambda j: (0, j)),
        out_shape=jax.ShapeDtypeStruct((s, CONV_W), MMD),
        scratch_shapes=[pltpu.VMEM((s + 2 * HALO, cb), F32)],
        compiler_params=_cp(("arbitrary",), VMEM_MID),
    )(proj, w8, brow)


def _conv_bwd(proj, ga, gb, w8, brow, *, name):
    s = proj.shape[0]
    cb = 128
    r = _tile(s, 512)

    def body(x_ref, ga_ref, gb_ref, w_ref, b_ref, dx_ref, dw_ref, db_ref, xpad, dpad):
        zeros = jnp.zeros((HALO, cb), F32)
        for ref in (xpad, dpad):
            ref[0:HALO, :] = zeros
            ref[s + HALO:s + 2 * HALO, :] = zeros

        def fill(i, carry):
            st = pl.multiple_of(i * r, r)
            xpad[pl.ds(st + HALO, r), :] = x_ref[pl.ds(st, r), :].astype(F32)
            return carry

        lax.fori_loop(0, s // r, fill, 0)
        wv = w_ref[...]
        bv = b_ref[...]

        def first(i, carry):
            st = pl.multiple_of(i * r, r)
            win = xpad[pl.ds(st, r + 2 * HALO), :]
            taps = [_shifted(win, HALO - 2 + t, r) for t in range(D_CONV)]
            u = bv
            for t in range(D_CONV):
                u = u + wv[t:t + 1, :] * taps[t]
            sg = _sigmoid(u)
            du = (ga_ref[pl.ds(st, r), :] + gb_ref[pl.ds(st, r), :]) * (sg * (1.0 + u * (1.0 - sg)))
            dpad[pl.ds(st + HALO, r), :] = du
            out = [carry[0] + jnp.sum(du, axis=0, keepdims=True)]
            for t in range(D_CONV):
                out.append(carry[1 + t] + jnp.sum(du * taps[t], axis=0, keepdims=True))
            return tuple(out)

        sums = lax.fori_loop(0, s // r, first, tuple(jnp.zeros((1, cb), F32) for _ in range(1 + D_CONV)))
        db_ref[...] = sums[0]
        for t in range(D_CONV):
            dw_ref[t:t + 1, :] = sums[1 + t]
        dw_ref[D_CONV:8, :] = jnp.zeros((8 - D_CONV, cb), F32)

        def second(i, carry):
            st = pl.multiple_of(i * r, r)
            win = dpad[pl.ds(st, r + 2 * HALO), :]
            acc = wv[0:1, :] * _shifted(win, HALO + 2, r)
            for t in range(1, D_CONV):
                acc = acc + wv[t:t + 1, :] * _shifted(win, HALO + 2 - t, r)
            dx_ref[pl.ds(st, r), :] = acc.astype(dx_ref.dtype)
            return carry

        lax.fori_loop(0, s // r, second, 0)

    col = pl.BlockSpec((s, cb), lambda j: (0, j))
    return pl.pallas_call(
        body, name=name, grid=(CONV_W // cb,),
        in_specs=[pl.BlockSpec((s, cb), lambda j: (0, XS0 // cb + j)), col, col, pl.BlockSpec((8, cb), lambda j: (0, j)),
                  pl.BlockSpec((1, cb), lambda j: (0, j))],
        out_specs=[col, pl.BlockSpec((8, cb), lambda j: (0, j)), pl.BlockSpec((1, cb), lambda j: (0, j))],
        out_shape=[jax.ShapeDtypeStruct((s, CONV_W), MMD), jax.ShapeDtypeStruct((8, CONV_W), F32),
                   jax.ShapeDtypeStruct((1, CONV_W), F32)],
        scratch_shapes=[pltpu.VMEM((s + 2 * HALO, cb), F32), pltpu.VMEM((s + 2 * HALO, cb), F32)],
        compiler_params=_cp(("arbitrary",), VMEM_BIG),
    )(proj, ga, gb, w8, brow)


def _tri(lower):
    i = jnp.arange(CHUNK)
    return ((i[:, None] >= i[None, :]) if lower else (i[:, None] <= i[None, :])).astype(F32)


def _dt_fwd(raw, bias, arow, *, name):
    s = raw.shape[0]

    def body(r_ref, b_ref, a_ref, lo_ref, up_ref, dt_ref, cs_ref):
        u = r_ref[...] + b_ref[...]
        dt = jnp.maximum(u, 0.0) + jnp.log1p(jnp.exp(-jnp.abs(u)))
        dt_ref[...] = dt
        a = dt * a_ref[...]
        lane = lax.broadcasted_iota(jnp.int32, (CHUNK, 128), 1)
        cs_ref[...] = jnp.where(lane < SSD_HEADS, _dot_hi(lo_ref[...], a), _dot_hi(up_ref[...], a))

    blk = pl.BlockSpec((CHUNK, 128), lambda i: (i, 0))
    row = pl.BlockSpec((1, 128), lambda i: (0, 0))
    tri = pl.BlockSpec((CHUNK, CHUNK), lambda i: (0, 0))
    return pl.pallas_call(
        body, name=name, grid=(s // CHUNK,), in_specs=[blk, row, row, tri, tri], out_specs=[blk, blk],
        out_shape=[jax.ShapeDtypeStruct((s, 128), F32)] * 2, compiler_params=_cp(("arbitrary",)),
    )(raw, bias, arow, _tri(True), _tri(False))


def _dt_bwd(ddt, raw, bias, *, name):
    s = raw.shape[0]
    tm = _tile(s, 1024)

    def body(d_ref, r_ref, b_ref, o_ref, db_ref):
        g = d_ref[...] * _sigmoid(r_ref[...] + b_ref[...])
        o_ref[...] = g.astype(o_ref.dtype)
        _acc_rows(db_ref, jnp.sum(g, axis=0, keepdims=True), pl.program_id(0) == 0)

    blk = pl.BlockSpec((tm, 128), lambda i: (i, 0))
    row = pl.BlockSpec((1, 128), lambda i: (0, 0))
    return pl.pallas_call(
        body, name=name, grid=(s // tm,), in_specs=[blk, blk, row], out_specs=[blk, row],
        out_shape=[jax.ShapeDtypeStruct((s, 128), MMD), jax.ShapeDtypeStruct((1, 128), F32)],
        compiler_params=_cp(("arbitrary",)),
    )(ddt, raw, bias)


GW = HPG * SSD_P


def _ssd_specs(nc, rev):
    cc = (lambda c: nc - 1 - c) if rev else (lambda c: c)
    return dict(
        x=pl.BlockSpec((CHUNK, GW), lambda g, c: (cc(c), g)),
        b=pl.BlockSpec((CHUNK, SSD_N), lambda g, c: (cc(c), 2048 // SSD_N + g)),
        c=pl.BlockSpec((CHUNK, SSD_N), lambda g, c: (cc(c), 2048 // SSD_N + SSD_GROUPS + g)),
        col=pl.BlockSpec((1, CHUNK, HPG), lambda g, c: (g, cc(c), 0)),
        lanes=pl.BlockSpec((CHUNK, 128), lambda g, c: (cc(c), 0)),
        rowt=pl.BlockSpec((1, 1, HPG, CHUNK), lambda g, c: (g, cc(c), 0, 0)),
        drow=pl.BlockSpec((1, GW), lambda g, c: (0, g)),
        y=pl.BlockSpec((CHUNK, GW), lambda g, c: (cc(c), g)),
        h=pl.BlockSpec((1, 1, SSD_N, GW), lambda g, c: (g, cc(c), 0, 0)),
        n=pl.BlockSpec((CHUNK, SSD_N), lambda g, c: (cc(c), g)),
    )


def _ssd_mask(anti):
    ii = lax.broadcasted_iota(jnp.int32, (CHUNK, CHUNK), 0)
    jj = lax.broadcasted_iota(jnp.int32, (CHUNK, CHUNK), 1)
    return ii, jj, (ii <= jj) if anti else (ii >= jj)


def _expand(x, ex):
    h1 = x.astype(jnp.bfloat16)
    r1 = x - h1.astype(F32)
    h2 = r1.astype(jnp.bfloat16)
    h3 = (r1 - h2.astype(F32)).astype(jnp.bfloat16)
    return _dot(h1, ex) + _dot(h2, ex) + _dot(h3, ex)


def _headsum(a, e):
    hi = a.astype(jnp.bfloat16)
    return _dot(hi, e) + _dot((a - hi.astype(F32)).astype(jnp.bfloat16), e)


def _expand_mats():
    lane = jnp.arange(128)[None, :, None]
    col = jnp.arange(GW)[None, None, :]
    base = (jnp.arange(2)[:, None] * SSD_HEADS + jnp.arange(SSD_GROUPS)[None, :] * HPG).reshape(2 * SSD_GROUPS, 1, 1)
    return (lane == base + col // SSD_P).astype(jnp.bfloat16)


def _headsum_mats():
    e1 = (jnp.arange(GW)[:, None] // SSD_P == jnp.arange(128)[None, :]).astype(jnp.bfloat16)
    e2 = (jnp.arange(HPG * CHUNK)[:, None] // CHUNK == jnp.arange(128)[None, :]).astype(jnp.bfloat16)
    return e1, e2


def _ssd_fwd(xc, dt, cs, cst, ex, drow, di, *, name):
    s = xc.shape[0]
    nc = s // CHUNK
    anti = di == 1
    sp = _ssd_specs(nc, anti)
    trow = 0 if anti else CHUNK - 1

    def body(x_ref, b_ref, c_ref, dt_ref, cs_ref, cst_ref, ex_ref, d_ref, y_ref, hp_ref, h_ref):
        @pl.when(pl.program_id(1) == 0)
        def _():
            h_ref[...] = jnp.zeros_like(h_ref)

        ex = ex_ref[0]
        xb = x_ref[...].astype(F32)
        bm, cm = b_ref[...], c_ref[...]
        csr = cst_ref[0, 0]
        dtf = _expand(dt_ref[...], ex)
        csf = _expand(cs_ref[...], ex)
        tl = csf[trow:trow + 1, :]
        h = h_ref[...]
        hp_ref[0, 0] = h
        g = _dot(cm, bm, NT)
        xs = xb * dtf
        xsm = xs.astype(MMD)
        base = jnp.exp(csf) * _dot(cm, h.astype(MMD)) + d_ref[...] * xb
        mask = _ssd_mask(anti)[2]
        for r in range(HPG):
            sl = slice(r * SSD_P, (r + 1) * SSD_P)
            lm = jnp.exp(jnp.where(mask, csf[:, r * SSD_P:r * SSD_P + 1] - csr[r:r + 1, :], NEG))
            y_ref[:, sl] = _dot((g * lm).astype(MMD), xsm[:, sl]) + base[:, sl]
        xd = (xs * jnp.exp(tl - csf)).astype(MMD)
        h_ref[...] = h * jnp.exp(tl) + _dot(bm, xd, TN)

    return pl.pallas_call(
        body, name=name, grid=(SSD_GROUPS, nc),
        in_specs=[sp["x"], sp["b"], sp["c"], sp["lanes"], sp["lanes"], sp["rowt"],
                  pl.BlockSpec((1, 128, GW), lambda g, c: (di * SSD_GROUPS + g, 0, 0)), sp["drow"]],
        out_specs=[sp["y"], sp["h"]],
        out_shape=[jax.ShapeDtypeStruct((s, 2048), F32), jax.ShapeDtypeStruct((SSD_GROUPS, nc, SSD_N, GW), F32)],
        scratch_shapes=[pltpu.VMEM((SSD_N, GW), F32)],
        compiler_params=_cp(("arbitrary", "arbitrary")),
    )(xc, xc, xc, dt, cs, cst, ex, drow)


def _ssd_bwd(xc, dt, cs, dt4, cst, ex, drow, arow4, dy, hprev, di, *, name):
    s = xc.shape[0]
    nc = s // CHUNK
    anti = di == 1
    sp = _ssd_specs(nc, not anti)
    trow = 0 if anti else CHUNK - 1
    e1, e2 = _headsum_mats()

    def body(x_ref, b_ref, c_ref, dt_ref, cs_ref, dt4_ref, cst_ref, ex_ref, d_ref, a_ref, dy_ref, hp_ref, tri_ref,
             e1_ref, e2_ref, dx_ref, db_ref, dc_ref, ddt_ref, da_ref, dh_ref, w_ref, dxs_ref):
        @pl.when(pl.program_id(1) == 0)
        def _():
            dh_ref[...] = jnp.zeros_like(dh_ref)
            da_ref[...] = jnp.zeros_like(da_ref)

        ex = ex_ref[0]
        e1v = e1_ref[...]
        xb = x_ref[...].astype(F32)
        bm, cm = b_ref[...], c_ref[...]
        csr = cst_ref[0, 0]
        dyb = dy_ref[...]
        dym = dyb.astype(MMD)
        hp = hp_ref[0, 0]
        hpm = hp.astype(MMD)
        dh = dh_ref[...]
        dhm = dh.astype(MMD)
        dtf = _expand(dt_ref[...], ex)
        csf = _expand(cs_ref[...], ex)
        tl = csf[trow:trow + 1, :]
        e = jnp.exp(csf)
        dec = jnp.exp(tl - csf)
        et = jnp.exp(tl)
        xs = xb * dtf
        xsm = xs.astype(MMD)
        g = _dot(cm, bm, NT)
        z = _dot(cm, hpm)
        bdh = _dot(bm, dhm)
        ii, _, mask = _ssd_mask(anti)
        dg = jnp.zeros((CHUNK, CHUNK), F32)
        wcols = jnp.zeros((CHUNK, CHUNK), F32)
        for r in range(HPG):
            sl = slice(r * SSD_P, (r + 1) * SSD_P)
            lm = jnp.exp(jnp.where(mask, csf[:, r * SSD_P:r * SSD_P + 1] - csr[r:r + 1, :], NEG))
            mm = g * lm
            dm = _dot(dym[:, sl], xsm[:, sl], NT)
            w = dm * mm
            w_ref[:, r * CHUNK:(r + 1) * CHUNK] = w
            wcols = jnp.where(ii == r, jnp.sum(w, axis=0, keepdims=True), wcols)
            dg = dg + dm * lm
            dxs_ref[:, sl] = _dot(mm.astype(MMD), dym[:, sl], TN)
        dxs = dxs_ref[...] + dec * bdh
        dx_ref[...] = dxs * dtf + d_ref[...] * dyb
        tb = xs * bdh * dec
        d_tot = jnp.sum(tb, axis=0, keepdims=True) + et * jnp.sum(dh * hp, axis=0, keepdims=True)
        d_tot = _headsum(jnp.broadcast_to(d_tot, (8, GW)), e1v)[0:1]
        dcs = (_headsum(dyb * (e * z) - tb, e1v) + _headsum(w_ref[...], e2_ref[...]) - wcols.T
               + jnp.where(ii == trow, d_tot, 0.0))
        da = _dot_hi(tri_ref[...], dcs)
        ddt_ref[0] = (da * a_ref[0] + _headsum(dxs * xb, e1v))[:, 0:HPG]
        da_ref[0] += jnp.sum(da[:, 0:HPG] * dt4_ref[0], axis=0, keepdims=True)
        dgm = dg.astype(MMD)
        dz = (e * dyb).astype(MMD)
        dc_ref[...] = _dot(dgm, bm) + _dot(dz, hpm, NT)
        db_ref[...] = _dot(dgm, cm, TN) + _dot((xs * dec).astype(MMD), dhm, NT)
        dh_ref[...] = dh * et + _dot(cm, dz, TN)

    const = lambda shape: pl.BlockSpec(shape, lambda g, c: (0,) * len(shape))
    return pl.pallas_call(
        body, name=name, grid=(SSD_GROUPS, nc),
        in_specs=[sp["x"], sp["b"], sp["c"], sp["lanes"], sp["lanes"], sp["col"], sp["rowt"],
                  pl.BlockSpec((1, 128, GW), lambda g, c: (di * SSD_GROUPS + g, 0, 0)), sp["drow"],
                  pl.BlockSpec((1, 1, 128), lambda g, c: (g, 0, 0)), sp["y"], sp["h"],
                  const((CHUNK, CHUNK)), const((GW, 128)), const((HPG * CHUNK, 128))],
        out_specs=[sp["y"], sp["n"], sp["n"], sp["col"], pl.BlockSpec((1, 1, HPG), lambda g, c: (g, 0, 0))],
        out_shape=[jax.ShapeDtypeStruct((s, 2048), F32), jax.ShapeDtypeStruct((s, SSD_GROUPS * SSD_N), F32),
                   jax.ShapeDtypeStruct((s, SSD_GROUPS * SSD_N), F32), jax.ShapeDtypeStruct((SSD_GROUPS, s, HPG), F32),
                   jax.ShapeDtypeStruct((SSD_GROUPS, 1, HPG), F32)],
        scratch_shapes=[pltpu.VMEM((SSD_N, GW), F32), pltpu.VMEM((CHUNK, HPG * CHUNK), F32), pltpu.VMEM((CHUNK, GW), F32)],
        compiler_params=_cp(("arbitrary", "arbitrary")),
    )(xc, xc, xc, dt, cs, dt4, cst, ex, drow, arow4, dy, hprev, _tri(anti), e1, e2)


def _gnorm_fwd(ya, yb, proj, w, *, name):
    s = ya.shape[0]
    tm = _tile(s, 256)

    def body(a_ref, b_ref, z_ref, w_ref, o_ref):
        zv = z_ref[...].astype(F32)
        t = (a_ref[...] + b_ref[...]) * (zv * _sigmoid(zv))
        r = lax.rsqrt(jnp.mean(t * t, axis=-1, keepdims=True) + EPS)
        o_ref[...] = ((t * r) * w_ref[...]).astype(o_ref.dtype)

    big = pl.BlockSpec((tm, 2048), lambda i: (i, 0))
    row = pl.BlockSpec((1, 2048), lambda i: (0, 0))
    return pl.pallas_call(
        body, name=name, grid=(s // tm,), in_specs=[big, big, big, row], out_specs=big,
        out_shape=jax.ShapeDtypeStruct((s, 2048), MMD), compiler_params=_cp(("arbitrary",)),
    )(ya, yb, proj, w)


def _gnorm_bwd(dout, ya, yb, proj, w, *, name):
    s = ya.shape[0]
    tm = _tile(s, 256)

    def body(do_ref, a_ref, b_ref, z_ref, w_ref, dy_ref, dz_ref, dw_ref):
        zv = z_ref[...].astype(F32)
        sg = _sigmoid(zv)
        sz = zv * sg
        y = a_ref[...] + b_ref[...]
        t = y * sz
        r = lax.rsqrt(jnp.mean(t * t, axis=-1, keepdims=True) + EPS)
        nv = t * r
        dov = do_ref[...].astype(F32)
        _acc_rows(dw_ref, jnp.sum(dov * nv, axis=0, keepdims=True), pl.program_id(0) == 0)
        dn = dov * w_ref[...]
        dt_ = r * (dn - nv * jnp.mean(dn * nv, axis=-1, keepdims=True))
        dy_ref[...] = dt_ * sz
        dz_ref[...] = (dt_ * y * (sg * (1.0 + zv * (1.0 - sg)))).astype(dz_ref.dtype)

    big = pl.BlockSpec((tm, 2048), lambda i: (i, 0))
    row = pl.BlockSpec((1, 2048), lambda i: (0, 0))
    return pl.pallas_call(
        body, name=name, grid=(s // tm,), in_specs=[big, big, big, big, row], out_specs=[big, big, row],
        out_shape=[jax.ShapeDtypeStruct((s, 2048), F32), jax.ShapeDtypeStruct((s, 2048), MMD),
                   jax.ShapeDtypeStruct((1, 2048), F32)],
        compiler_params=_cp(("arbitrary",)),
    )(dout, ya, yb, proj, w)


def _colsum_prod(a, b, *, name):
    s, n = a.shape
    tm = _tile(s, 256)

    def body(a_ref, b_ref, o_ref):
        _acc_rows(o_ref, jnp.sum(a_ref[...].astype(F32) * b_ref[...].astype(F32), axis=0, keepdims=True),
                  pl.program_id(0) == 0)

    big = pl.BlockSpec((tm, n), lambda i: (i, 0))
    return pl.pallas_call(
        body, name=name, grid=(s // tm,), in_specs=[big, big], out_specs=pl.BlockSpec((1, n), lambda i: (0, 0)),
        out_shape=jax.ShapeDtypeStruct((1, n), F32), compiler_params=_cp(("arbitrary",)),
    )(a, b)


def _heads(a, n):
    return a.reshape(a.shape[0], n, HEAD_DIM).transpose(1, 0, 2)


def _unheads(a):
    return a.transpose(1, 0, 2).reshape(a.shape[1], a.shape[0] * HEAD_DIM)


def _per_group(a):
    return a.reshape(a.shape[0], SSD_GROUPS, HPG).transpose(1, 0, 2)


def _per_group_t(a):
    s = a.shape[0]
    return a.reshape(s // CHUNK, CHUNK, SSD_GROUPS, HPG).transpose(2, 0, 3, 1)


def _local_step(x, target, mod, wts, small):
    s, d = x.shape
    shift1, scale1, gate1, shift2, scale2, gate2 = [mod[i:i + 1] for i in range(6)]

    h1 = _ln_mod(x, small["norm1_w"], scale1, shift1, name="ln1")
    proj = _mm(h1, wts["w_in_p"], name="in_proj", outs=[MMD], tm=512, tn=2944, b_outer=True)
    dt_raw = _mm(h1, wts["w_dt"], name="dt_proj", outs=[F32], tm=512, tn=128)

    qk_w = jnp.concatenate([jnp.tile(small["q_norm_w"], (1, N_Q_HEADS)), jnp.tile(small["k_norm_w"], (1, N_KV_HEADS))], axis=1)
    qk_sc = jnp.concatenate([jnp.full((1, N_Q_HEADS * HEAD_DIM), HEAD_DIM ** -0.5, F32),
                             jnp.ones((1, N_KV_HEADS * HEAD_DIM), F32)], axis=1)
    tabs = _rope_tables(s)
    qk, qkt = _qk_fwd(proj, qk_w, qk_sc, tabs, name="qk_fwd")
    qkt = qkt.reshape(N_Q_HEADS + N_KV_HEADS, HEAD_DIM, s)
    k_h = _heads(qk[:, N_Q_HEADS * HEAD_DIM:], N_KV_HEADS)
    v_sd = proj[:, V0:V0 + N_KV_HEADS * HEAD_DIM]
    v_h = _heads(v_sd, N_KV_HEADS)
    vt = v_sd.T.reshape(N_KV_HEADS, HEAD_DIM, s)
    ot, lse = _flash_fwd(qkt, vt, name="flash_fwd")
    ot2 = ot.reshape(N_Q_HEADS * HEAD_DIM, s)

    w8 = jnp.pad(small["conv_w"], ((0, 8 - D_CONV), (0, 0)))
    xc = _conv_fwd(proj, w8, small["conv_b"], name="conv_fwd")
    a_neg = -jnp.exp(small["A_log"])
    arow = jnp.pad(a_neg.reshape(1, 2 * SSD_HEADS), ((0, 0), (0, 128 - 2 * SSD_HEADS)))
    bias_row = jnp.pad(small["dt_bias"].reshape(1, 2 * SSD_HEADS), ((0, 0), (0, 128 - 2 * SSD_HEADS)))
    dt, cs = _dt_fwd(dt_raw, bias_row, arow, name="dt_fwd")
    drow = jnp.repeat(small["ssd_D"], SSD_P, axis=1)
    dirs = []
    for di in range(2):
        cols = slice(di * SSD_HEADS, (di + 1) * SSD_HEADS)
        dirs.append(dict(
            dt4=_per_group(dt[:, cols]), cst=_per_group_t(cs[:, cols]),
            drow=drow if di == 0 else jnp.zeros_like(drow),
            arow4=jnp.pad(a_neg[di].reshape(SSD_GROUPS, 1, HPG), ((0, 0), (0, 0), (0, 128 - HPG)))))
    ex = _expand_mats()
    ys = []
    for di, dd in enumerate(dirs):
        y, dd["hprev"] = _ssd_fwd(xc, dt, cs, dd["cst"], ex, dd["drow"], di, name=f"ssd_fwd{di}")
        ys.append(y)
    ssdn = _gnorm_fwd(ys[0], ys[1], proj, small["ssd_norm_w"], name="gnorm_fwd")

    a_o = _mm(ot2, wts["w_attn_out"], name="attn_out", outs=[MMD], ta=True, tm=512, tn=1024)

    def merge_epi(acc, ao, ga, gs):
        return (_sigmoid(ga.astype(F32)) * ao.astype(F32) + _sigmoid(gs.astype(F32)) * acc, acc)

    merged, b_o = _mm(ssdn, wts["w_ssd_out"], name="ssd_out", outs=[MMD, MMD], tm=512, tn=512,
                      extras=[(a_o, "tile", 0), (proj, "tile", GA0), (proj, "tile", GS0)], epi=merge_epi)

    def res_epi(acc, res, gate):
        return (res + gate * acc, acc)

    x1, mo = _mm(merged, wts["w_o"], name="w_o", outs=[F32, MMD], tm=512, tn=512,
                 extras=[(x, "tile", 0), (gate1, "row", 0)], epi=res_epi)
    h2 = _ln_mod(x1, small["norm2_w"], scale2, shift2, name="ln2")

    def relu2_epi(acc):
        rl = jnp.maximum(acc, 0.0)
        return (rl * rl, rl)

    act, rl = _mm(h2, wts["w_mlp1"], name="mlp1", outs=[MMD, MMD], tm=512, tn=1024, epi=relu2_epi)

    def loss_epi(acc, res, gate, tgt):
        return ((res + gate * acc - tgt) * (1.0 / d), acc)

    dy, ffo = _mm(act, wts["w_mlp2"], name="mlp2", outs=[F32, MMD], tm=512, tn=512,
                  extras=[(x1, "tile", 0), (gate2, "row", 0), (target, "tile", 0)], epi=loss_epi)
    loss = _sumsq(dy, name="loss") * (0.5 * d)

    gw = {}
    gs_ = {}
    dffo, dgate2 = _gate_bwd(dy, ffo, gate2, name="gate2_bwd")
    dpre = _mm(dffo, wts["w_mlp2"], name="mlp2_dx", outs=[MMD], nt=True, tm=512, tn=1024,
               extras=[(rl, "tile", 0)], epi=lambda acc, r: (acc * (2.0 * r.astype(F32)),))
    gw["w_mlp2"] = _mm_tn(act, dffo, name="mlp2_dw")
    dh2 = _mm(dpre, wts["w_mlp1"], name="mlp1_dx", outs=[F32], nt=True, tm=512, tn=512)
    gw["w_mlp1"] = _mm_tn(h2, dpre, name="mlp1_dw")
    dx1, dshift2, dscale2, gs_["norm2_w"] = _ln_mod_bwd(dh2, x1, small["norm2_w"], scale2, dy, name="ln2_bwd")
    dmo, dgate1 = _gate_bwd(dx1, mo, gate1, name="gate1_bwd")

    def merge_bwd_epi(acc, ao, bo, ga, gs):
        sa, ss = _sigmoid(ga.astype(F32)), _sigmoid(gs.astype(F32))
        return (acc * sa, acc * ss, acc * ao.astype(F32) * sa * (1.0 - sa), acc * bo.astype(F32) * ss * (1.0 - ss))

    da_o, db_o, dga, dgs = _mm(dmo, wts["w_o"], name="w_o_dx", outs=[MMD] * 4, nt=True, tm=512, tn=512,
                               extras=[(a_o, "tile", 0), (b_o, "tile", 0), (proj, "tile", GA0), (proj, "tile", GS0)],
                               epi=merge_bwd_epi)
    gw["w_o"] = _mm_tn(merged, dmo, name="w_o_dw")
    dot = _mm(wts["w_attn_out"], da_o, name="attn_out_dx", outs=[MMD], nt=True, tm=512, tn=512)
    gw["w_attn_out"] = _mm(ot2, da_o, name="attn_out_dw", outs=[F32], tm=256, tn=512, vmem=VMEM_BIG)
    dssdn = _mm(db_o, wts["w_ssd_out"], name="ssd_out_dx", outs=[MMD], nt=True, tm=512, tn=512)
    gw["w_ssd_out"] = _mm_tn(ssdn, db_o, name="ssd_out_dw")

    dyssd, dz, gs_["ssd_norm_w"] = _gnorm_bwd(dssdn, ys[0], ys[1], proj, small["ssd_norm_w"], name="gnorm_bwd")
    gs_["ssd_D"] = _colsum_prod(dyssd, xc[:, 0:2048], name="ssd_d_grad").reshape(SSD_HEADS, SSD_P).sum(axis=1).reshape(1, SSD_HEADS)
    dxc, ddts, das = [], [], []
    for di, dd in enumerate(dirs):
        dxs, dbm, dcm, ddt4, da4 = _ssd_bwd(xc, dt, cs, dd["dt4"], dd["cst"], ex, dd["drow"], dd["arow4"],
                                            dyssd, dd["hprev"], di, name=f"ssd_bwd{di}")
        dxc.append(jnp.concatenate([dxs, dbm, dcm], axis=1))
        ddts.append(ddt4.transpose(1, 0, 2).reshape(s, SSD_HEADS))
        das.append(da4.reshape(1, SSD_HEADS))
    dxbc, dw8, gs_["conv_b"] = _conv_bwd(proj, dxc[0], dxc[1], w8, small["conv_b"], name="conv_bwd")
    gs_["conv_w"] = dw8[0:D_CONV]
    gs_["A_log"] = jnp.concatenate(das, axis=0) * a_neg
    ddt = jnp.pad(jnp.concatenate(ddts, axis=1), ((0, 0), (0, 128 - 2 * SSD_HEADS)))
    ddt_raw, dbias = _dt_bwd(ddt, dt_raw, bias_row, name="dt_bwd")
    gs_["dt_bias"] = dbias[:, 0:2 * SSD_HEADS].reshape(2, SSD_HEADS)

    dqt, dk_h, dv_h = _flash_bwd(qkt, k_h, v_h, dot.reshape(N_Q_HEADS, HEAD_DIM, s), ot, lse, name="flash_bwd")
    dqkt = jnp.concatenate([dqt.reshape(N_Q_HEADS * HEAD_DIM, s),
                            dk_h.transpose(0, 2, 1).reshape(N_KV_HEADS * HEAD_DIM, s)], axis=0)
    dqk_u, dqk_w = _qk_bwd(dqkt, proj, qk_w, qk_sc, tabs, name="qk_bwd")
    gs_["q_norm_w"] = dqk_w[:, 0:N_Q_HEADS * HEAD_DIM].reshape(N_Q_HEADS, HEAD_DIM).sum(axis=0, keepdims=True)
    gs_["k_norm_w"] = dqk_w[:, N_Q_HEADS * HEAD_DIM:].reshape(N_KV_HEADS, HEAD_DIM).sum(axis=0, keepdims=True)
    dv = _unheads(dv_h).astype(MMD)

    dproj = jnp.concatenate([dz, dga, dgs, dxbc, dqk_u, dv, ddt_raw], axis=1)
    dh1 = _mm(dproj, wts["w_in_p"], name="in_proj_dx", outs=[F32], nt=True, tm=256, tn=512, vmem=VMEM_BIG)
    gw["w_in_p"] = _mm_tn(h1, dproj, name="in_proj_dw", tk=512, tn=2944, tmm=1024, vmem=VMEM_BIG)
    grad_x, dshift1, dscale1, gs_["norm1_w"] = _ln_mod_bwd(dh1, x, small["norm1_w"], scale1, dx1, name="ln1_bwd")
    dmod = jnp.concatenate([dshift1, dscale1, dgate1, dshift2, dscale2, dgate2], axis=0)
    return loss, grad_x, dmod, gw, gs_


N_DEV = 8
N_CHIP = 4
ANY = pl.BlockSpec(memory_space=pl.ANY)


def _place():
    return lax.axis_index("x"), lax.axis_index("y"), lax.axis_index("c")


def _allgather8(v, *, name):
    m_per, n = v.shape

    def body(x_ref, out_ref, send_sems, recv_sems, local_sem):
        x, y, c = _place()
        me, sibling = (x, y, c), (x, y, 1 - c)
        chips = [(1 - x, y), (x, 1 - y), (1 - x, 1 - y)]

        def rows(px, py, pc):
            return out_ref.at[pl.ds((4 * px + 2 * py + pc) * m_per, m_per), :]

        def copy(k, block, to, src=None):
            return pltpu.make_async_remote_copy(
                src_ref=rows(*block) if src is None else src, dst_ref=rows(*block),
                send_sem=send_sems.at[k], recv_sem=recv_sems.at[k], device_id=to, device_id_type=MESH)

        mine = pltpu.make_async_copy(x_ref, rows(*me), local_sem)
        mine.start()
        first = [copy(0, me, sibling, src=x_ref)]
        first += [copy(1 + j, me, (*chip, c), src=x_ref) for j, chip in enumerate(chips)]
        for cp in first:
            cp.start()
        passed = [copy(4 + j, (*chip, c), sibling) for j, chip in enumerate(chips)]
        for j, chip in enumerate(chips):
            copy(1 + j, (*chip, c), me).wait_recv()
            passed[j].start()
        copy(0, sibling, me).wait_recv()
        for j, chip in enumerate(chips):
            copy(4 + j, (*chip, 1 - c), me).wait_recv()
        for cp in first + passed:
            cp.wait_send()
        mine.wait()

    return pl.pallas_call(
        body, name=name, out_shape=jax.ShapeDtypeStruct((N_DEV * m_per, n), v.dtype),
        in_specs=[pl.BlockSpec(memory_space=pltpu.VMEM)], out_specs=pl.BlockSpec(memory_space=pltpu.VMEM),
        scratch_shapes=[pltpu.SemaphoreType.DMA((7,)), pltpu.SemaphoreType.DMA((7,)), pltpu.SemaphoreType.DMA],
    )(v)


def _chip_exchange(src, scatter, *, name):
    shape = src.shape[1:] if scatter else src.shape

    def body(x_ref, out_ref, send_sems, recv_sems, local_sem):
        x, y, c = _place()
        k = 2 * x + y
        chips = [(1 - x, y), (x, 1 - y), (1 - x, 1 - y)]
        ids = [2 * cx + cy for cx, cy in chips]

        def outgoing(j):
            return x_ref.at[ids[j]] if scatter else x_ref

        mine = pltpu.make_async_copy(x_ref.at[k] if scatter else x_ref, out_ref.at[k], local_sem)
        mine.start()
        sends = [pltpu.make_async_remote_copy(
            src_ref=outgoing(j), dst_ref=out_ref.at[k], send_sem=send_sems.at[j], recv_sem=recv_sems.at[j],
            device_id=(cx, cy, c), device_id_type=MESH) for j, (cx, cy) in enumerate(chips)]
        for cp in sends:
            cp.start()
        for j, (cx, cy) in enumerate(chips):
            pltpu.make_async_remote_copy(
                src_ref=outgoing(j), dst_ref=out_ref.at[ids[j]], send_sem=send_sems.at[j], recv_sem=recv_sems.at[j],
                device_id=(cx, cy, c), device_id_type=MESH).wait_recv()
        for cp in sends:
            cp.wait_send()
        mine.wait()

    return pl.pallas_call(
        body, name=name, out_shape=jax.ShapeDtypeStruct((N_CHIP,) + tuple(shape), src.dtype),
        in_specs=[ANY], out_specs=ANY,
        scratch_shapes=[pltpu.SemaphoreType.DMA((3,)), pltpu.SemaphoreType.DMA((3,)), pltpu.SemaphoreType.DMA],
    )(src)


def _row_tile(r, pref=512):
    return max(t for t in range(16, pref + 1, 16) if r % t == 0)


def _gather_weights(src, *, name):
    r = src.shape[0]
    hr = r // 2
    assert r == 2 * hr and hr % 16 == 0

    def body(x_ref, out_ref, send_sems, recv_sems, local_sem):
        x, y, c = _place()
        k = 2 * x + y
        chips = [(1 - x, y), (x, 1 - y), (1 - x, 1 - y)]
        ids = [2 * cx + cy for cx, cy in chips]
        mine_rows = pl.ds(pl.multiple_of(c * hr, 16), hr)
        other_rows = pl.ds(pl.multiple_of((1 - c) * hr, 16), hr)

        def copy(sem, src_ref, slot, rows, to):
            return pltpu.make_async_remote_copy(
                src_ref=src_ref, dst_ref=out_ref.at[slot, rows], send_sem=send_sems.at[sem], recv_sem=recv_sems.at[sem],
                device_id=to, device_id_type=MESH)

        mine = pltpu.make_async_copy(x_ref, out_ref.at[k], local_sem)
        mine.start()
        sends = [copy(j, x_ref.at[mine_rows], k, mine_rows, (cx, cy, c)) for j, (cx, cy) in enumerate(chips)]
        for cp in sends:
            cp.start()
        passed = [copy(3 + j, out_ref.at[ids[j], mine_rows], ids[j], mine_rows, (x, y, 1 - c)) for j in range(3)]
        for j, (cx, cy) in enumerate(chips):
            copy(j, x_ref.at[mine_rows], ids[j], mine_rows, (cx, cy, c)).wait_recv()
            passed[j].start()
        for j in range(3):
            copy(3 + j, out_ref.at[ids[j], other_rows], ids[j], other_rows, (x, y, 1 - c)).wait_recv()
        for cp in sends + passed:
            cp.wait_send()
        mine.wait()

    return pl.pallas_call(
        body, name=name, out_shape=jax.ShapeDtypeStruct((N_CHIP,) + tuple(src.shape), src.dtype),
        in_specs=[ANY], out_specs=ANY,
        scratch_shapes=[pltpu.SemaphoreType.DMA((6,)), pltpu.SemaphoreType.DMA((6,)), pltpu.SemaphoreType.DMA],
    )(src)


def _pair_swap(a, *, name):
    n, r, cols = a.shape
    hr = r // 2

    def body(x_ref, out_ref, send_sem, recv_sem):
        x, y, c = _place()
        other_rows = pl.ds(pl.multiple_of((1 - c) * hr, 16), hr)
        cp = pltpu.make_async_remote_copy(src_ref=x_ref.at[:, other_rows], dst_ref=out_ref, send_sem=send_sem,
                                          recv_sem=recv_sem, device_id=(x, y, 1 - c), device_id_type=MESH)
        cp.start()
        cp.wait()

    return pl.pallas_call(
        body, name=name, out_shape=jax.ShapeDtypeStruct((n, hr, cols), a.dtype), in_specs=[ANY], out_specs=ANY,
        scratch_shapes=[pltpu.SemaphoreType.DMA, pltpu.SemaphoreType.DMA],
    )(a)


def _pair_gather(a, *, name):
    hr, cols = a.shape

    def body(x_ref, out_ref, send_sem, recv_sem, local_sem):
        x, y, c = _place()
        mine_rows = pl.ds(pl.multiple_of(c * hr, 16), hr)
        mine = pltpu.make_async_copy(x_ref, out_ref.at[mine_rows], local_sem)
        mine.start()
        cp = pltpu.make_async_remote_copy(src_ref=x_ref, dst_ref=out_ref.at[mine_rows], send_sem=send_sem,
                                          recv_sem=recv_sem, device_id=(x, y, 1 - c), device_id_type=MESH)
        cp.start()
        pltpu.make_async_remote_copy(src_ref=x_ref, dst_ref=out_ref.at[pl.ds(pl.multiple_of((1 - c) * hr, 16), hr)],
                                     send_sem=send_sem, recv_sem=recv_sem, device_id=(x, y, 1 - c),
                                     device_id_type=MESH).wait_recv()
        cp.wait_send()
        mine.wait()

    return pl.pallas_call(
        body, name=name, out_shape=jax.ShapeDtypeStruct((2 * hr, cols), a.dtype), in_specs=[ANY], out_specs=ANY,
        scratch_shapes=[pltpu.SemaphoreType.DMA, pltpu.SemaphoreType.DMA, pltpu.SemaphoreType.DMA],
    )(a)


def _sum_slots(a, *, name):
    _, r, c = a.shape
    tr = _row_tile(r)

    def body(a_ref, o_ref):
        acc = a_ref[0].astype(F32)
        for j in range(1, N_CHIP):
            acc = acc + a_ref[j].astype(F32)
        o_ref[...] = acc

    return pl.pallas_call(
        body, name=name, grid=(r // tr,), in_specs=[pl.BlockSpec((N_CHIP, tr, c), lambda i: (0, i, 0))],
        out_specs=pl.BlockSpec((tr, c), lambda i: (i, 0)), out_shape=jax.ShapeDtypeStruct((r, c), F32),
        compiler_params=_cp(("arbitrary",)),
    )(a)


def _add2(a, b, *, name):
    r, c = a.shape
    tr = _row_tile(r)

    def body(a_ref, b_ref, o_ref):
        o_ref[...] = (a_ref[...].astype(F32) + b_ref[...].astype(F32)).astype(o_ref.dtype)

    spec = pl.BlockSpec((tr, c), lambda i: (i, 0))
    return pl.pallas_call(
        body, name=name, grid=(r // tr,), in_specs=[spec, spec], out_specs=spec,
        out_shape=jax.ShapeDtypeStruct((r, c), a.dtype), compiler_params=_cp(("arbitrary",)),
    )(a, b)


BIG = ("w_in", "w_mlp1", "w_attn_out", "w_ssd_out", "w_o", "w_mlp2")
COL_SHARDED = ("w_in", "w_mlp1")
SMALL = ("b_ada", "norm1_w", "norm2_w", "q_norm_w", "k_norm_w", "conv_b", "A_log", "dt_bias", "ssd_D", "ssd_norm_w")
NAMES = ("w_ada", "b_ada", "norm1_w", "norm2_w", "w_in", "q_norm_w", "k_norm_w", "conv_w", "conv_b", "A_log", "dt_bias",
         "ssd_D", "ssd_norm_w", "w_attn_out", "w_ssd_out", "w_o", "w_mlp1", "w_mlp2")
W_IN_COLS = 8768


def _permute_in(w):
    return jnp.concatenate([w[:, 4608:6656], w[:, 6720:8768], w[:, 1536:4608], w[:, 0:1536], w[:, 6656:6720],
                            jnp.zeros((w.shape[0], PW - W_IN_COLS), w.dtype)], axis=1)


def _unpermute_in(wp):
    return jnp.concatenate([wp[:, Q0:DT0], wp[:, XS0:Q0], wp[:, Z0:GA0], wp[:, DT0:DT0 + 64], wp[:, GA0:XS0]], axis=1)


def _pad_to(v, n):
    return jnp.pad(v, (0, n - v.shape[0]))


def _step(w, m, v, loss_target):
    xi, yi, ci = _place()
    chip = 2 * xi + yi
    dev = 4 * xi + 2 * yi + ci
    x, tgt = w["x"], loss_target
    d = x.shape[1]

    cw = w["conv_w"].shape[1]
    v0 = _pad_to(jnp.concatenate([w["c"].reshape(-1), w["conv_w"].reshape(-1)]), 5120).reshape(8, 640)
    g0 = _allgather8(v0, name="ag_cond").reshape(N_DEV, 5120)
    c_all = g0[:, 0:d]
    conv_w = jnp.concatenate([g0[2 * k, d:d + D_CONV * cw].reshape(D_CONV, cw) for k in range(N_CHIP)], axis=1)
    sc = _silu_cast(c_all, name="silu_c")
    modp = _mm(sc, w["w_ada"].astype(MMD), name="ada_fwd", outs=[F32], tm=8, tn=512)
    g1 = _allgather8(modp, name="ag_mod").reshape(N_DEV, N_DEV, modp.shape[1])
    mod_all = jnp.concatenate([g1[2 * k] for k in range(N_CHIP)], axis=1)
    mod = (lax.dynamic_slice_in_dim(mod_all, dev, 1, axis=0) + w["b_ada"]).reshape(6, d)

    packed = [w[n].astype(MMD).reshape(-1, d) for n in BIG]
    n_rows = sum(p.shape[0] for p in packed)
    pad_rows = -n_rows % 32
    packed = jnp.concatenate(packed + [jnp.zeros((pad_rows, d), MMD)], axis=0)
    gath = _gather_weights(packed, name="ag_weights")
    full, r0 = {}, 0
    for n in BIG:
        rows = w[n].size // d
        part = gath[:, r0:r0 + rows]
        if n in COL_SHARDED:
            full[n] = jnp.concatenate([part[k].reshape(w[n].shape) for k in range(N_CHIP)], axis=1)
        else:
            full[n] = part.reshape(N_CHIP * w[n].shape[0], w[n].shape[1])
        r0 += rows
    wts = {n: full[n] for n in BIG if n != "w_in"}
    wts["w_in_p"] = _permute_in(full["w_in"])
    wts["w_dt"] = jnp.pad(full["w_in"][:, 6656:6720], ((0, 0), (0, 64)))
    small = {n: w[n] for n in SMALL if n != "b_ada"}
    small["conv_w"] = conv_w

    loss, grad_x, dmod, gw, gs = _local_step(x, tgt, mod, wts, small)
    loss = lax.psum(loss[0, 0], ("x", "y", "c"))

    gw["w_in"] = _unpermute_in(gw.pop("w_in_p"))
    slots = []
    for k in range(N_CHIP):
        parts = []
        for n in BIG:
            r_, c_ = w[n].shape
            blk = gw[n][:, k * c_:(k + 1) * c_] if n in COL_SHARDED else gw[n][k * r_:(k + 1) * r_]
            parts.append(blk.astype(MMD).reshape(-1, d))
        slots.append(jnp.concatenate(parts + [jnp.zeros((pad_rows, d), MMD)], axis=0))
    slots = jnp.stack(slots)
    hr = (n_rows + pad_rows) // 2
    theirs = _pair_swap(slots, name="rs_pair")
    ours = lax.dynamic_slice_in_dim(slots, ci * hr, hr, axis=1)
    pair = _add2(ours.reshape(N_CHIP * hr, d), theirs.reshape(N_CHIP * hr, d), name="rs_pair_sum")
    recv = _chip_exchange(pair.reshape(N_CHIP, hr, d), True, name="rs_grads")
    total = _pair_gather(_sum_slots(recv, name="rs_sum"), name="rs_sibling")
    grads, r0 = {}, 0
    for n in BIG:
        rows = w[n].size // d
        grads[n] = total[r0:r0 + rows].reshape(w[n].shape)
        r0 += rows

    order = [dmod.reshape(-1)] + [gs[n].reshape(-1) for n in SMALL if n != "b_ada"] + [gs["conv_w"].reshape(-1)]
    vec = jnp.concatenate(order)
    n_small = vec.shape[0]
    n_pad = -(-n_small // 1024) * 1024
    g2 = _allgather8(_pad_to(vec, n_pad).reshape(8, n_pad // 8), name="ag_small")
    tot = _rows_sum(g2, N_DEV, name="small_sum").reshape(-1)
    dmod_all = g2.reshape(N_DEV, n_pad)[:, 0:6 * d]
    off = 0
    for n in SMALL:
        grads[n] = tot[off:off + w[n].size].reshape(w[n].shape)
        off += w[n].size
    conv_full = tot[off:off + D_CONV * N_CHIP * cw].reshape(D_CONV, N_CHIP * cw)
    grads["conv_w"] = lax.dynamic_slice_in_dim(conv_full, chip * cw, cw, axis=1)
    ada_cols = w["w_ada"].shape[1]
    dmod_mine = lax.dynamic_slice_in_dim(dmod_all, chip * ada_cols, ada_cols, axis=1).astype(MMD)
    grads["w_ada"] = _mm_tn(sc, dmod_mine, name="ada_dw", tk=512, tn=512, tmm=8)

    delta, new_m, new_v = {}, {}, {}
    pack = lambda t: jnp.concatenate([t[n].reshape(-1) for n in SMALL]).reshape(1, -1)
    ds_, ms_, vs_ = _adamw(pack(w), pack(grads), pack(m), pack(v), name="adamw_small")
    off = 0
    for n in SMALL:
        for dst, src in ((delta, ds_), (new_m, ms_), (new_v, vs_)):
            dst[n] = src[0, off:off + w[n].size].reshape(w[n].shape)
        off += w[n].size
    for n in ("w_ada", "conv_w") + BIG:
        delta[n], new_m[n], new_v[n] = _adamw(w[n], grads[n], m[n], v[n], name="adamw_" + n)
    return loss, grad_x, grads, delta, new_m, new_v


def kernel(x, c, w_ada, b_ada, norm1_w, norm2_w, w_in, q_norm_w, k_norm_w, conv_w, conv_b, A_log, dt_bias, ssd_D, ssd_norm_w, w_attn_out, w_ssd_out, w_o, w_mlp1, w_mlp2, loss_target, m_w_ada, m_b_ada, m_norm1_w, m_norm2_w, m_w_in, m_q_norm_w, m_k_norm_w, m_conv_w, m_conv_b, m_A_log, m_dt_bias, m_ssd_D, m_ssd_norm_w, m_w_attn_out, m_w_ssd_out, m_w_o, m_w_mlp1, m_w_mlp2, v_w_ada, v_b_ada, v_norm1_w, v_norm2_w, v_w_in, v_q_norm_w, v_k_norm_w, v_conv_w, v_conv_b, v_A_log, v_dt_bias, v_ssd_D, v_ssd_norm_w, v_w_attn_out, v_w_ssd_out, v_w_o, v_w_mlp1, v_w_mlp2):
    args = dict(locals())
    strip = lambda a: a[0] if a.ndim == 3 else a
    w = {n: strip(args[n]) for n in NAMES + ("x", "c")}
    m = {n: strip(args["m_" + n]) for n in NAMES}
    v = {n: strip(args["v_" + n]) for n in NAMES}
    loss, grad_x, grads, delta, new_m, new_v = _step(w, m, v, loss_target[0])
    like = lambda t, n: t.reshape(args[n].shape)
    return (loss, grad_x[None], *[like(grads[n], n) for n in NAMES], *[like(delta[n], n) for n in NAMES],
            *[like(new_m[n], n) for n in NAMES], *[like(new_v[n], n) for n in NAMES])
```

```python
import functools
import math

import jax
import jax.numpy as jnp
from jax import lax
from jax.experimental import pallas as pl
from jax.experimental.pallas import tpu as pltpu

F32 = jnp.float32
MMD = jnp.bfloat16
EPS = 1e-6
NEG = -1e30
MIB = 1024 * 1024
VMEM_BIG = 56 * MIB
VMEM_MID = 40 * MIB

GRID_W = 64
N_Q_HEADS, N_KV_HEADS, HEAD_DIM = 16, 4, 64
ROPE_THETA = 10000.0
SSD_HEADS, SSD_GROUPS, SSD_P, SSD_N, CHUNK = 32, 4, 64, 128, 128
HPG = SSD_HEADS // SSD_GROUPS
D_CONV = 5
ADAM_LR, ADAM_B1, ADAM_B2, ADAM_EPS, ADAM_WD, ADAM_STEP = 0.001, 0.9, 0.999, 1e-08, 0.01, 10

Z0, GA0, GS0, XS0, B0, C0, Q0, K0, V0, DT0, PW = 0, 2048, 3072, 4096, 6144, 6656, 7168, 8192, 8448, 8704, 8832

MESH = pl.DeviceIdType.MESH
NT = (((1,), (1,)), ((), ()))
TN = (((0,), (0,)), ((), ()))


def _cp(sem=None, vmem=VMEM_MID):
    return pltpu.CompilerParams(dimension_semantics=sem, vmem_limit_bytes=vmem)


def _tile(n, pref):
    t = min(n, pref)
    while n % t:
        t //= 2
    return t


def _dot(a, b, dims=None):
    if dims is None:
        return jnp.dot(a, b, preferred_element_type=F32)
    return lax.dot_general(a, b, dims, preferred_element_type=F32)


def _dot_hi(a, b):
    return jnp.dot(a, b, precision=lax.Precision.HIGHEST, preferred_element_type=F32)


def _sigmoid(x):
    return jax.nn.sigmoid(x)


def _mm(a, b, *, name, outs, nt=False, ta=False, extras=(), epi=None, tm=512, tn=512, n=None, b_outer=False,
        vmem=VMEM_MID):
    assert not (nt and ta)
    k, m = a.shape if ta else a.shape[::-1]
    if n is None:
        n = b.shape[0] if nt else b.shape[1]
    tm, tn = _tile(m, tm), _tile(n, tn)
    gi, gj = m // tm, n // tn
    if b_outer:
        grid = (gj, gi)
        ij = lambda p, q: (q, p)
    else:
        grid = (gi, gj)
        ij = lambda p, q: (p, q)
    if ta:
        a_spec = pl.BlockSpec((k, tm), lambda p, q: (0, ij(p, q)[0]))
    else:
        a_spec = pl.BlockSpec((tm, k), lambda p, q: (ij(p, q)[0], 0))
    if nt:
        b_spec = pl.BlockSpec((tn, k), lambda p, q: (ij(p, q)[1], 0))
    else:
        b_spec = pl.BlockSpec((k, tn), lambda p, q: (0, ij(p, q)[1]))
    e_specs = []
    for arr, kind, off in extras:
        ob = off // tn
        assert off % tn == 0
        if kind == "tile":
            e_specs.append(pl.BlockSpec((tm, tn), lambda p, q, ob=ob: (ij(p, q)[0], ob + ij(p, q)[1])))
        else:
            e_specs.append(pl.BlockSpec((1, tn), lambda p, q, ob=ob: (0, ob + ij(p, q)[1])))
    ne = len(extras)

    def body(a_ref, b_ref, *rest):
        acc = _dot(a_ref[...], b_ref[...], NT if nt else (TN if ta else None))
        res = epi(acc, *[e[...] for e in rest[:ne]]) if epi is not None else (acc,)
        for o_ref, r in zip(rest[ne:], res):
            o_ref[...] = r.astype(o_ref.dtype)

    out = pl.pallas_call(
        body, name=name, grid=grid,
        in_specs=[a_spec, b_spec] + e_specs,
        out_specs=[pl.BlockSpec((tm, tn), lambda p, q: ij(p, q)) for _ in outs],
        out_shape=[jax.ShapeDtypeStruct((m, n), dt) for dt in outs],
        compiler_params=_cp(("arbitrary", "arbitrary"), vmem),
    )(a, b, *[e[0] for e in extras])
    return out if len(outs) > 1 else out[0]


def _mm_tn(a, g, *, name, tk=512, tn=1024, tmm=1024, vmem=VMEM_MID):
    m, k = a.shape
    n = g.shape[1]
    tk, tn, tmm = _tile(k, tk), _tile(n, tn), _tile(m, tmm)

    def body(a_ref, g_ref, o_ref):
        p = _dot(a_ref[...], g_ref[...], TN)

        @pl.when(pl.program_id(2) == 0)
        def _():
            o_ref[...] = p

        @pl.when(pl.program_id(2) > 0)
        def _():
            o_ref[...] += p

    return pl.pallas_call(
        body, name=name, grid=(k // tk, n // tn, m // tmm),
        in_specs=[pl.BlockSpec((tmm, tk), lambda i, j, r: (r, i)), pl.BlockSpec((tmm, tn), lambda i, j, r: (r, j))],
        out_specs=pl.BlockSpec((tk, tn), lambda i, j, r: (i, j)),
        out_shape=jax.ShapeDtypeStruct((k, n), F32),
        compiler_params=_cp(("arbitrary", "arbitrary", "arbitrary"), vmem),
    )(a, g)


def _adamw(w, g, m, v, *, name):
    r, c = w.shape
    tr = _tile(r, 256) if r % 8 == 0 else r

    def body(w_ref, g_ref, m_ref, v_ref, d_ref, nm_ref, nv_ref):
        gg = g_ref[...]
        nm = ADAM_B1 * m_ref[...] + (1.0 - ADAM_B1) * gg
        nv = ADAM_B2 * v_ref[...] + (1.0 - ADAM_B2) * jnp.square(gg)
        m_hat = nm / (1.0 - ADAM_B1 ** ADAM_STEP)
        v_hat = nv / (1.0 - ADAM_B2 ** ADAM_STEP)
        d_ref[...] = -ADAM_LR * (m_hat / (jnp.sqrt(v_hat) + ADAM_EPS) + ADAM_WD * w_ref[...])
        nm_ref[...] = nm
        nv_ref[...] = nv

    spec = pl.BlockSpec((tr, c), lambda i: (i, 0))
    return pl.pallas_call(
        body, name=name, grid=(r // tr,), in_specs=[spec] * 4, out_specs=[spec] * 3,
        out_shape=[jax.ShapeDtypeStruct((r, c), F32)] * 3, compiler_params=_cp(("arbitrary",)),
    )(w, g, m, v)


def _rows_sum(a, groups, *, name):
    r = a.shape[0] // groups

    def body(a_ref, o_ref):
        acc = a_ref[0:r, :]
        for d in range(1, groups):
            acc = acc + a_ref[d * r:(d + 1) * r, :]
        o_ref[...] = acc

    return pl.pallas_call(body, name=name, out_shape=jax.ShapeDtypeStruct((r, a.shape[1]), F32))(a)


def _silu_cast(a, *, name):
    def body(a_ref, o_ref):
        x = a_ref[...]
        o_ref[...] = (x * _sigmoid(x)).astype(o_ref.dtype)

    return pl.pallas_call(body, name=name, out_shape=jax.ShapeDtypeStruct(a.shape, MMD))(a)


def _sumsq(a, *, name):
    m, n = a.shape
    tm = _tile(m, 512)

    def body(a_ref, o_ref):
        x = a_ref[...]
        p = jnp.sum(jnp.sum(x * x, axis=1, keepdims=True), axis=0, keepdims=True)

        @pl.when(pl.program_id(0) == 0)
        def _():
            o_ref[...] = p

        @pl.when(pl.program_id(0) > 0)
        def _():
            o_ref[...] += p

    return pl.pallas_call(
        body, name=name, grid=(m // tm,), in_specs=[pl.BlockSpec((tm, n), lambda i: (i, 0))],
        out_specs=pl.BlockSpec((1, 1), lambda i: (0, 0)), out_shape=jax.ShapeDtypeStruct((1, 1), F32),
        compiler_params=_cp(("arbitrary",)),
    )(a)


def _acc_rows(o_ref, p, first):
    @pl.when(first)
    def _():
        o_ref[...] = p

    @pl.when(jnp.logical_not(first))
    def _():
        o_ref[...] += p


def _ln_mod(x, w, scale, shift, *, name):
    s, d = x.shape
    tm = _tile(s, 512)

    def body(x_ref, w_ref, sc_ref, sh_ref, o_ref):
        xv = x_ref[...]
        r = lax.rsqrt(jnp.mean(xv * xv, axis=-1, keepdims=True) + EPS)
        o_ref[...] = ((xv * r) * w_ref[...] * (1.0 + sc_ref[...]) + sh_ref[...]).astype(o_ref.dtype)

    row = pl.BlockSpec((1, d), lambda i: (0, 0))
    big = pl.BlockSpec((tm, d), lambda i: (i, 0))
    return pl.pallas_call(
        body, name=name, grid=(s // tm,), in_specs=[big, row, row, row], out_specs=big,
        out_shape=jax.ShapeDtypeStruct((s, d), MMD), compiler_params=_cp(("arbitrary",)),
    )(x, w, scale, shift)


def _ln_mod_bwd(dh, x, w, scale, dres, *, name):
    s, d = x.shape
    tm = _tile(s, 512)

    def body(dh_ref, x_ref, w_ref, sc_ref, dres_ref, dx_ref, dsh_ref, dsc_ref, dw_ref):
        xv = x_ref[...]
        dhv = dh_ref[...].astype(F32)
        r = lax.rsqrt(jnp.mean(xv * xv, axis=-1, keepdims=True) + EPS)
        nv = xv * r
        wv = w_ref[...]
        g1 = 1.0 + sc_ref[...]
        dn = dhv * (wv * g1)
        dx_ref[...] = dres_ref[...] + r * (dn - nv * jnp.mean(dn * nv, axis=-1, keepdims=True))
        first = pl.program_id(0) == 0
        _acc_rows(dsh_ref, jnp.sum(dhv, axis=0, keepdims=True), first)
        _acc_rows(dsc_ref, jnp.sum(dhv * nv * wv, axis=0, keepdims=True), first)
        _acc_rows(dw_ref, jnp.sum(dhv * nv * g1, axis=0, keepdims=True), first)

    row = pl.BlockSpec((1, d), lambda i: (0, 0))
    big = pl.BlockSpec((tm, d), lambda i: (i, 0))
    return pl.pallas_call(
        body, name=name, grid=(s // tm,), in_specs=[big, big, row, row, big], out_specs=[big, row, row, row],
        out_shape=[jax.ShapeDtypeStruct((s, d), F32)] + [jax.ShapeDtypeStruct((1, d), F32)] * 3,
        compiler_params=_cp(("arbitrary",)),
    )(dh, x, w, scale, dres)


def _gate_bwd(dy, u, gate, *, name):
    s, d = dy.shape
    tm = _tile(s, 512)

    def body(dy_ref, u_ref, g_ref, du_ref, dg_ref):
        dyv = dy_ref[...]
        du_ref[...] = (dyv * g_ref[...]).astype(du_ref.dtype)
        _acc_rows(dg_ref, jnp.sum(dyv * u_ref[...].astype(F32), axis=0, keepdims=True), pl.program_id(0) == 0)

    row = pl.BlockSpec((1, d), lambda i: (0, 0))
    big = pl.BlockSpec((tm, d), lambda i: (i, 0))
    return pl.pallas_call(
        body, name=name, grid=(s // tm,), in_specs=[big, big, row], out_specs=[big, row],
        out_shape=[jax.ShapeDtypeStruct((s, d), MMD), jax.ShapeDtypeStruct((1, d), F32)],
        compiler_params=_cp(("arbitrary",)),
    )(dy, u, gate)


def _seg64(v, e):
    hi = v.astype(jnp.bfloat16)
    lo = (v - hi.astype(F32)).astype(jnp.bfloat16)
    return _dot(hi, e) + _dot(lo, e)


def _rope_tables(s):
    rows = s // GRID_W
    pos_row = jnp.repeat(jnp.arange(rows, dtype=jnp.int32), GRID_W).astype(F32)
    pos_col = jnp.tile(jnp.arange(GRID_W, dtype=jnp.int32), rows).astype(F32)
    axis_dim = HEAD_DIM // 2
    inv_freq = ROPE_THETA ** (-jnp.arange(0, axis_dim, 2, dtype=F32) / axis_dim)
    ang_r = pos_row[:, None] * inv_freq[None, :]
    ang_c = pos_col[:, None] * inv_freq[None, :]
    zero = jnp.zeros_like(ang_r)
    cos = jnp.concatenate([jnp.cos(ang_r), jnp.cos(ang_r), jnp.cos(ang_c), jnp.cos(ang_c)], axis=1)
    s_a = jnp.concatenate([-jnp.sin(ang_r), zero, -jnp.sin(ang_c), zero], axis=1)
    s_b = jnp.concatenate([zero, jnp.sin(ang_r), zero, jnp.sin(ang_c)], axis=1)
    return [jnp.tile(t, (1, 2)) for t in (cos, s_a, s_b)]


def _e128():
    i = jnp.arange(128)
    return (i[:, None] // 64 == i[None, :] // 64).astype(jnp.bfloat16)


QKW = N_Q_HEADS * HEAD_DIM + N_KV_HEADS * HEAD_DIM


def _qk_fwd(proj, wrow, scrow, tabs, *, name):
    s = proj.shape[0]
    tm = _tile(s, 512)

    def body(x_ref, w_ref, sc_ref, cos_ref, sa_ref, sb_ref, e_ref, o_ref, ot_ref):
        u = x_ref[...].astype(F32)
        r = lax.rsqrt(_seg64(u * u, e_ref[...]) * (1.0 / HEAD_DIM) + EPS)
        nv = (u * r) * w_ref[...]
        ro = nv * cos_ref[...] + pltpu.roll(nv, 112, 1) * sa_ref[...] + pltpu.roll(nv, 16, 1) * sb_ref[...]
        out = ro * sc_ref[...]
        o_ref[...] = out.astype(o_ref.dtype)
        ot_ref[...] = out.T.astype(ot_ref.dtype)

    tab = pl.BlockSpec((tm, 128), lambda i, j: (i, 0))
    row = pl.BlockSpec((1, 128), lambda i, j: (0, j))
    return pl.pallas_call(
        body, name=name, grid=(s // tm, QKW // 128),
        in_specs=[pl.BlockSpec((tm, 128), lambda i, j: (i, Q0 // 128 + j)), row, row, tab, tab, tab,
                  pl.BlockSpec((128, 128), lambda i, j: (0, 0))],
        out_specs=[pl.BlockSpec((tm, 128), lambda i, j: (i, j)), pl.BlockSpec((128, tm), lambda i, j: (j, i))],
        out_shape=[jax.ShapeDtypeStruct((s, QKW), MMD), jax.ShapeDtypeStruct((QKW, s), MMD)],
        compiler_params=_cp(("arbitrary", "arbitrary")),
    )(proj, wrow, scrow, *tabs, _e128())


def _qk_bwd(dqkt, proj, wrow, scrow, tabs, *, name):
    s = proj.shape[0]
    tm = _tile(s, 512)

    def body(d_ref, x_ref, w_ref, sc_ref, cos_ref, sa_ref, sb_ref, e_ref, du_ref, dw_ref):
        e = e_ref[...]
        d = d_ref[...].T * sc_ref[...]
        dn = d * cos_ref[...] + pltpu.roll(d * sa_ref[...], 16, 1) + pltpu.roll(d * sb_ref[...], 112, 1)
        u = x_ref[...].astype(F32)
        r = lax.rsqrt(_seg64(u * u, e) * (1.0 / HEAD_DIM) + EPS)
        uh = u * r
        _acc_rows(dw_ref, jnp.sum(dn * uh, axis=0, keepdims=True), pl.program_id(1) == 0)
        dnw = dn * w_ref[...]
        du_ref[...] = (r * (dnw - uh * (_seg64(dnw * uh, e) * (1.0 / HEAD_DIM)))).astype(du_ref.dtype)

    tab = pl.BlockSpec((tm, 128), lambda j, i: (i, 0))
    row = pl.BlockSpec((1, 128), lambda j, i: (0, j))
    return pl.pallas_call(
        body, name=name, grid=(QKW // 128, s // tm),
        in_specs=[pl.BlockSpec((128, tm), lambda j, i: (j, i)), pl.BlockSpec((tm, 128), lambda j, i: (i, Q0 // 128 + j)),
                  row, row, tab, tab, tab, pl.BlockSpec((128, 128), lambda j, i: (0, 0))],
        out_specs=[pl.BlockSpec((tm, 128), lambda j, i: (i, j)), row],
        out_shape=[jax.ShapeDtypeStruct((s, QKW), MMD), jax.ShapeDtypeStruct((1, QKW), F32)],
        compiler_params=_cp(("arbitrary", "arbitrary")),
    )(dqkt, proj, wrow, scrow, *tabs, _e128())


REP = N_Q_HEADS // N_KV_HEADS


def _lanes(ref):
    return jnp.concatenate([ref[r] for r in range(REP)], axis=1)


V_AUG = HEAD_DIM + 8


def _flash_fwd(qkt, vta, *, name):
    s = qkt.shape[2]
    tq, tk = _tile(s, 1024), _tile(s, 512)
    nk = s // tk
    lanes = REP * tq

    def body(q_ref, k_ref, v_ref, o_ref, lse_ref, m_ref, acc_ref):
        j = pl.program_id(2)

        @pl.when(j == 0)
        def _():
            m_ref[...] = jnp.full_like(m_ref, NEG)
            acc_ref[...] = jnp.zeros_like(acc_ref)

        st = _dot(k_ref[0], _lanes(q_ref), TN)
        m_prev = m_ref[...]
        m_new = jnp.maximum(m_prev, jnp.max(st, axis=0, keepdims=True))
        p = jnp.exp(st - m_new).astype(MMD)
        acc_ref[...] = jnp.exp(m_prev - m_new) * acc_ref[...] + _dot(v_ref[0], p)
        m_ref[...] = m_new

        @pl.when(j == nk - 1)
        def _():
            acc = acc_ref[...]
            l = acc[HEAD_DIM:HEAD_DIM + 1]
            o = acc[0:HEAD_DIM] / l
            ls = m_ref[...] + jnp.log(l)
            for r in range(REP):
                o_ref[r] = o[:, r * tq:(r + 1) * tq].astype(o_ref.dtype)
                lse_ref[r] = ls[:, r * tq:(r + 1) * tq]

    qspec = pl.BlockSpec((REP, HEAD_DIM, tq), lambda g, i, j: (g, 0, i))
    return pl.pallas_call(
        body, name=name, grid=(N_KV_HEADS, s // tq, nk),
        in_specs=[qspec, pl.BlockSpec((1, HEAD_DIM, tk), lambda g, i, j: (N_Q_HEADS + g, 0, j)),
                  pl.BlockSpec((1, V_AUG, tk), lambda g, i, j: (g, 0, j))],
        out_specs=[qspec, pl.BlockSpec((REP, 1, tq), lambda g, i, j: (g, 0, i))],
        out_shape=[jax.ShapeDtypeStruct((N_Q_HEADS, HEAD_DIM, s), MMD), jax.ShapeDtypeStruct((N_Q_HEADS, 1, s), F32)],
        scratch_shapes=[pltpu.VMEM((1, lanes), F32), pltpu.VMEM((V_AUG, lanes), F32)],
        compiler_params=_cp(("arbitrary", "arbitrary", "arbitrary"), VMEM_BIG),
    )(qkt, qkt, vta)


def _flash_bwd(qkt, k_h, v_h, dot, ot, lse, *, name):
    s = qkt.shape[2]
    tq, tk = _tile(s, 512), _tile(s, 512)
    nk = s // tk

    def body(q_ref, kt_ref, k_ref, v_ref, do_ref, o_ref, lse_ref, dq_ref, dk_ref, dv_ref, dq_acc):
        i, j = pl.program_id(1), pl.program_id(2)
        q, do = _lanes(q_ref), _lanes(do_ref)
        delta = jnp.sum(do.astype(F32) * _lanes(o_ref).astype(F32), axis=0, keepdims=True)
        k, v = k_ref[0], v_ref[0]
        p = jnp.exp(_dot(k, q) - _lanes(lse_ref))
        dvc = _dot(p.astype(MMD), do, NT)
        ds = (p * (_dot(v, do) - delta)).astype(MMD)
        dkc = _dot(ds, q, NT)
        dqc = _dot(kt_ref[0], ds)
        rows = pl.ds(pl.multiple_of(j * tk, tk), tk)

        @pl.when(i == 0)
        def _():
            dk_ref[0, rows, :] = dkc
            dv_ref[0, rows, :] = dvc

        @pl.when(i > 0)
        def _():
            dk_ref[0, rows, :] += dkc
            dv_ref[0, rows, :] += dvc

        @pl.when(j == 0)
        def _():
            dq_acc[...] = dqc

        @pl.when(j > 0)
        def _():
            dq_acc[...] += dqc

        @pl.when(j == nk - 1)
        def _():
            acc = dq_acc[...]
            for r in range(REP):
                dq_ref[r] = acc[:, r * tq:(r + 1) * tq]

    qspec = pl.BlockSpec((REP, HEAD_DIM, tq), lambda g, i, j: (g, 0, i))
    kvin = pl.BlockSpec((1, tk, HEAD_DIM), lambda g, i, j: (g, j, 0))
    kvres = pl.BlockSpec((1, s, HEAD_DIM), lambda g, i, j: (g, 0, 0))
    return pl.pallas_call(
        body, name=name, grid=(N_KV_HEADS, s // tq, nk),
        in_specs=[qspec, pl.BlockSpec((1, HEAD_DIM, tk), lambda g, i, j: (N_Q_HEADS + g, 0, j)), kvin, kvin,
                  qspec, qspec, pl.BlockSpec((REP, 1, tq), lambda g, i, j: (g, 0, i))],
        out_specs=[qspec, kvres, kvres],
        out_shape=[jax.ShapeDtypeStruct((N_Q_HEADS, HEAD_DIM, s), F32), jax.ShapeDtypeStruct((N_KV_HEADS, s, HEAD_DIM), F32),
                   jax.ShapeDtypeStruct((N_KV_HEADS, s, HEAD_DIM), F32)],
        scratch_shapes=[pltpu.VMEM((HEAD_DIM, REP * tq), F32)],
        compiler_params=_cp(("arbitrary", "arbitrary", "arbitrary"), VMEM_BIG),
    )(qkt, qkt, k_h, v_h, dot, ot, lse)


HALO = 8
CONV_W = 2048 + 2 * SSD_GROUPS * SSD_N


def _shifted(win, off, r):
    return pltpu.roll(win, (r + 2 * HALO - off) % (r + 2 * HALO), 0)[0:r]


def _conv_fwd(proj, w8, brow, *, name):
    s = proj.shape[0]
    cb = 256
    r = _tile(s, 512)

    def body(x_ref, w_ref, b_ref, o_ref, pad_ref):
        zeros = jnp.zeros((HALO, cb), F32)
        pad_ref[0:HALO, :] = zeros
        pad_ref[s + HALO:s + 2 * HALO, :] = zeros

        def fill(i, carry):
            st = pl.multiple_of(i * r, r)
            pad_ref[pl.ds(st + HALO, r), :] = x_ref[pl.ds(st, r), :].astype(F32)
            return carry

        lax.fori_loop(0, s // r, fill, 0)
        wv = w_ref[...]
        bv = b_ref[...]

        def step(i, carry):
            st = pl.multiple_of(i * r, r)
            win = pad_ref[pl.ds(st, r + 2 * HALO), :]
            acc = bv + wv[0:1, :] * _shifted(win, HALO - 2, r)
            for t in range(1, D_CONV):
                acc = acc + wv[t:t + 1, :] * _shifted(win, HALO - 2 + t, r)
            o_ref[pl.ds(st, r), :] = (acc * _sigmoid(acc)).astype(o_ref.dtype)
            return carry

        lax.fori_loop(0, s // r, step, 0)

    return pl.pallas_call(
        body, name=name, grid=(CONV_W // cb,),
        in_specs=[pl.BlockSpec((s, cb), lambda j: (0, XS0 // cb + j)), pl.BlockSpec((8, cb), lambda j: (0, j)),
                  pl.BlockSpec((1, cb), lambda j: (0, j))],
        out_specs=pl.BlockSpec((s, cb), lambda j: (0, j)),
        out_shape=jax.ShapeDtypeStruct((s, CONV_W), MMD),
        scratch_shapes=[pltpu.VMEM((s + 2 * HALO, cb), F32)],
        compiler_params=_cp(("arbitrary",), VMEM_MID),
    )(proj, w8, brow)


def _conv_bwd(proj, col0, ga, gb, w8, brow, *, name):
    s = proj.shape[0]
    width = ga.shape[1]
    cb = 128
    c0 = col0 // cb
    r = _tile(s, 512)

    def body(x_ref, ga_ref, gb_ref, w_ref, b_ref, dx_ref, dw_ref, db_ref, xpad, dpad):
        zeros = jnp.zeros((HALO, cb), F32)
        for ref in (xpad, dpad):
            ref[0:HALO, :] = zeros
            ref[s + HALO:s + 2 * HALO, :] = zeros

        def fill(i, carry):
            st = pl.multiple_of(i * r, r)
            xpad[pl.ds(st + HALO, r), :] = x_ref[pl.ds(st, r), :].astype(F32)
            return carry

        lax.fori_loop(0, s // r, fill, 0)
        wv = w_ref[...]
        bv = b_ref[...]

        def first(i, carry):
            st = pl.multiple_of(i * r, r)
            win = xpad[pl.ds(st, r + 2 * HALO), :]
            taps = [_shifted(win, HALO - 2 + t, r) for t in range(D_CONV)]
            u = bv
            for t in range(D_CONV):
                u = u + wv[t:t + 1, :] * taps[t]
            sg = _sigmoid(u)
            du = (ga_ref[pl.ds(st, r), :] + gb_ref[pl.ds(st, r), :]) * (sg * (1.0 + u * (1.0 - sg)))
            dpad[pl.ds(st + HALO, r), :] = du
            out = [carry[0] + jnp.sum(du, axis=0, keepdims=True)]
            for t in range(D_CONV):
                out.append(carry[1 + t] + jnp.sum(du * taps[t], axis=0, keepdims=True))
            return tuple(out)

        sums = lax.fori_loop(0, s // r, first, tuple(jnp.zeros((1, cb), F32) for _ in range(1 + D_CONV)))
        db_ref[...] = sums[0]
        for t in range(D_CONV):
            dw_ref[t:t + 1, :] = sums[1 + t]
        dw_ref[D_CONV:8, :] = jnp.zeros((8 - D_CONV, cb), F32)

        def second(i, carry):
            st = pl.multiple_of(i * r, r)
            win = dpad[pl.ds(st, r + 2 * HALO), :]
            acc = wv[0:1, :] * _shifted(win, HALO + 2, r)
            for t in range(1, D_CONV):
                acc = acc + wv[t:t + 1, :] * _shifted(win, HALO + 2 - t, r)
            dx_ref[pl.ds(st, r), :] = acc.astype(dx_ref.dtype)
            return carry

        lax.fori_loop(0, s // r, second, 0)

    col = pl.BlockSpec((s, cb), lambda j: (0, j))
    return pl.pallas_call(
        body, name=name, grid=(width // cb,),
        in_specs=[pl.BlockSpec((s, cb), lambda j: (0, XS0 // cb + c0 + j)), col, col,
                  pl.BlockSpec((8, cb), lambda j: (0, c0 + j)), pl.BlockSpec((1, cb), lambda j: (0, c0 + j))],
        out_specs=[col, pl.BlockSpec((8, cb), lambda j: (0, j)), pl.BlockSpec((1, cb), lambda j: (0, j))],
        out_shape=[jax.ShapeDtypeStruct((s, width), MMD), jax.ShapeDtypeStruct((8, width), F32),
                   jax.ShapeDtypeStruct((1, width), F32)],
        scratch_shapes=[pltpu.VMEM((s + 2 * HALO, cb), F32), pltpu.VMEM((s + 2 * HALO, cb), F32)],
        compiler_params=_cp(("arbitrary",), VMEM_BIG),
    )(proj, ga, gb, w8, brow)


def _tri(lower):
    i = jnp.arange(CHUNK)
    return ((i[:, None] >= i[None, :]) if lower else (i[:, None] <= i[None, :])).astype(F32)


def _dt_fwd(raw, bias, arow, *, name):
    s = raw.shape[0]

    def body(r_ref, b_ref, a_ref, lo_ref, up_ref, dt_ref, cs_ref):
        u = r_ref[...] + b_ref[...]
        dt = jnp.maximum(u, 0.0) + jnp.log1p(jnp.exp(-jnp.abs(u)))
        dt_ref[...] = dt
        a = dt * a_ref[...]
        lane = lax.broadcasted_iota(jnp.int32, (CHUNK, 128), 1)
        cs_ref[...] = jnp.where(lane < SSD_HEADS, _dot_hi(lo_ref[...], a), _dot_hi(up_ref[...], a))

    blk = pl.BlockSpec((CHUNK, 128), lambda i: (i, 0))
    row = pl.BlockSpec((1, 128), lambda i: (0, 0))
    tri = pl.BlockSpec((CHUNK, CHUNK), lambda i: (0, 0))
    return pl.pallas_call(
        body, name=name, grid=(s // CHUNK,), in_specs=[blk, row, row, tri, tri], out_specs=[blk, blk],
        out_shape=[jax.ShapeDtypeStruct((s, 128), F32)] * 2, compiler_params=_cp(("arbitrary",)),
    )(raw, bias, arow, _tri(True), _tri(False))


def _dt_bwd(ddt, raw, bias, *, name):
    s = raw.shape[0]
    tm = _tile(s, 1024)

    def body(d_ref, r_ref, b_ref, o_ref, db_ref):
        g = d_ref[...] * _sigmoid(r_ref[...] + b_ref[...])
        o_ref[...] = g.astype(o_ref.dtype)
        _acc_rows(db_ref, jnp.sum(g, axis=0, keepdims=True), pl.program_id(0) == 0)

    blk = pl.BlockSpec((tm, 128), lambda i: (i, 0))
    row = pl.BlockSpec((1, 128), lambda i: (0, 0))
    return pl.pallas_call(
        body, name=name, grid=(s // tm,), in_specs=[blk, blk, row], out_specs=[blk, row],
        out_shape=[jax.ShapeDtypeStruct((s, 128), MMD), jax.ShapeDtypeStruct((1, 128), F32)],
        compiler_params=_cp(("arbitrary",)),
    )(ddt, raw, bias)


GW = HPG * SSD_P


def _ssd_specs(nc, rev):
    cc = (lambda c: nc - 1 - c) if rev else (lambda c: c)
    return dict(
        x=pl.BlockSpec((CHUNK, GW), lambda g, c: (cc(c), g)),
        b=pl.BlockSpec((CHUNK, SSD_N), lambda g, c: (cc(c), 2048 // SSD_N + g)),
        c=pl.BlockSpec((CHUNK, SSD_N), lambda g, c: (cc(c), 2048 // SSD_N + SSD_GROUPS + g)),
        col=pl.BlockSpec((1, CHUNK, HPG), lambda g, c: (g, cc(c), 0)),
        lanes=pl.BlockSpec((CHUNK, 128), lambda g, c: (cc(c), 0)),
        rowt=pl.BlockSpec((1, 1, HPG, CHUNK), lambda g, c: (g, cc(c), 0, 0)),
        drow=pl.BlockSpec((1, GW), lambda g, c: (0, g)),
        y=pl.BlockSpec((CHUNK, GW), lambda g, c: (cc(c), g)),
        h=pl.BlockSpec((1, 1, SSD_N, GW), lambda g, c: (g, cc(c), 0, 0)),
        n=pl.BlockSpec((CHUNK, SSD_N), lambda g, c: (cc(c), g)),
    )


def _ssd_mask(anti):
    ii = lax.broadcasted_iota(jnp.int32, (CHUNK, CHUNK), 0)
    jj = lax.broadcasted_iota(jnp.int32, (CHUNK, CHUNK), 1)
    return ii, jj, (ii <= jj) if anti else (ii >= jj)


def _expand(x, ex):
    h1 = x.astype(jnp.bfloat16)
    r1 = x - h1.astype(F32)
    h2 = r1.astype(jnp.bfloat16)
    h3 = (r1 - h2.astype(F32)).astype(jnp.bfloat16)
    return _dot(h1, ex) + _dot(h2, ex) + _dot(h3, ex)


def _headsum(a, e):
    hi = a.astype(jnp.bfloat16)
    return _dot(hi, e) + _dot((a - hi.astype(F32)).astype(jnp.bfloat16), e)


def _expand_mats():
    lane = jnp.arange(128)[None, :, None]
    col = jnp.arange(GW)[None, None, :]
    base = (jnp.arange(2)[:, None] * SSD_HEADS + jnp.arange(SSD_GROUPS)[None, :] * HPG).reshape(2 * SSD_GROUPS, 1, 1)
    return (lane == base + col // SSD_P).astype(jnp.bfloat16)


def _headsum_mats():
    e1 = (jnp.arange(GW)[:, None] // SSD_P == jnp.arange(128)[None, :]).astype(jnp.bfloat16)
    e2 = (jnp.arange(HPG * CHUNK)[:, None] // CHUNK == jnp.arange(128)[None, :]).astype(jnp.bfloat16)
    return e1, e2


def _ssd_fwd(xc, dt, cs, cst, ex, drow, di, *, name):
    s = xc.shape[0]
    nc = s // CHUNK
    anti = di == 1
    sp = _ssd_specs(nc, anti)
    trow = 0 if anti else CHUNK - 1

    def body(x_ref, b_ref, c_ref, dt_ref, cs_ref, cst_ref, ex_ref, d_ref, y_ref, hp_ref, h_ref):
        @pl.when(pl.program_id(1) == 0)
        def _():
            h_ref[...] = jnp.zeros_like(h_ref)

        ex = ex_ref[0]
        xb = x_ref[...].astype(F32)
        bm, cm = b_ref[...], c_ref[...]
        csr = cst_ref[0, 0]
        dtf = _expand(dt_ref[...], ex)
        csf = _expand(cs_ref[...], ex)
        tl = csf[trow:trow + 1, :]
        h = h_ref[...]
        hp_ref[0, 0] = h
        g = _dot(cm, bm, NT)
        xs = xb * dtf
        xsm = xs.astype(MMD)
        base = jnp.exp(csf) * _dot(cm, h.astype(MMD)) + d_ref[...] * xb
        mask = _ssd_mask(anti)[2]
        for r in range(HPG):
            sl = slice(r * SSD_P, (r + 1) * SSD_P)
            lm = jnp.exp(jnp.where(mask, csf[:, r * SSD_P:r * SSD_P + 1] - csr[r:r + 1, :], NEG))
            y_ref[:, sl] = _dot((g * lm).astype(MMD), xsm[:, sl]) + base[:, sl]
        xd = (xs * jnp.exp(tl - csf)).astype(MMD)
        h_ref[...] = h * jnp.exp(tl) + _dot(bm, xd, TN)

    return pl.pallas_call(
        body, name=name, grid=(SSD_GROUPS, nc),
        in_specs=[sp["x"], sp["b"], sp["c"], sp["lanes"], sp["lanes"], sp["rowt"],
                  pl.BlockSpec((1, 128, GW), lambda g, c: (di * SSD_GROUPS + g, 0, 0)), sp["drow"]],
        out_specs=[sp["y"], sp["h"]],
        out_shape=[jax.ShapeDtypeStruct((s, 2048), F32), jax.ShapeDtypeStruct((SSD_GROUPS, nc, SSD_N, GW), F32)],
        scratch_shapes=[pltpu.VMEM((SSD_N, GW), F32)],
        compiler_params=_cp(("arbitrary", "arbitrary")),
    )(xc, xc, xc, dt, cs, cst, ex, drow)


def _ssd_bwd(xc, dt, cs, dt4, cst, ex, drow, arow4, dy, hprev, di, *, name):
    s = xc.shape[0]
    nc = s // CHUNK
    anti = di == 1
    sp = _ssd_specs(nc, not anti)
    trow = 0 if anti else CHUNK - 1
    e1, e2 = _headsum_mats()

    def body(x_ref, b_ref, c_ref, dt_ref, cs_ref, dt4_ref, cst_ref, ex_ref, d_ref, a_ref, dy_ref, hp_ref, tri_ref,
             e1_ref, e2_ref, dx_ref, db_ref, dc_ref, ddt_ref, da_ref, dh_ref, w_ref, dxs_ref):
        @pl.when(pl.program_id(1) == 0)
        def _():
            dh_ref[...] = jnp.zeros_like(dh_ref)
            da_ref[...] = jnp.zeros_like(da_ref)

        ex = ex_ref[0]
        e1v = e1_ref[...]
        xb = x_ref[...].astype(F32)
        bm, cm = b_ref[...], c_ref[...]
        csr = cst_ref[0, 0]
        dyb = dy_ref[...]
        dym = dyb.astype(MMD)
        hp = hp_ref[0, 0]
        hpm = hp.astype(MMD)
        dh = dh_ref[...]
        dhm = dh.astype(MMD)
        dtf = _expand(dt_ref[...], ex)
        csf = _expand(cs_ref[...], ex)
        tl = csf[trow:trow + 1, :]
        e = jnp.exp(csf)
        dec = jnp.exp(tl - csf)
        et = jnp.exp(tl)
        xs = xb * dtf
        xsm = xs.astype(MMD)
        g = _dot(cm, bm, NT)
        z = _dot(cm, hpm)
        bdh = _dot(bm, dhm)
        ii, _, mask = _ssd_mask(anti)
        dg = jnp.zeros((CHUNK, CHUNK), F32)
        wcols = jnp.zeros((CHUNK, CHUNK), F32)
        for r in range(HPG):
            sl = slice(r * SSD_P, (r + 1) * SSD_P)
            lm = jnp.exp(jnp.where(mask, csf[:, r * SSD_P:r * SSD_P + 1] - csr[r:r + 1, :], NEG))
            mm = g * lm
            dm = _dot(dym[:, sl], xsm[:, sl], NT)
            w = dm * mm
            w_ref[:, r * CHUNK:(r + 1) * CHUNK] = w
            wcols = jnp.where(ii == r, jnp.sum(w, axis=0, keepdims=True), wcols)
            dg = dg + dm * lm
            dxs_ref[:, sl] = _dot(mm.astype(MMD), dym[:, sl], TN)
        dxs = dxs_ref[...] + dec * bdh
        dx_ref[...] = dxs * dtf + d_ref[...] * dyb
        tb = xs * bdh * dec
        d_tot = jnp.sum(tb, axis=0, keepdims=True) + et * jnp.sum(dh * hp, axis=0, keepdims=True)
        d_tot = _headsum(jnp.broadcast_to(d_tot, (8, GW)), e1v)[0:1]
        dcs = (_headsum(dyb * (e * z) - tb, e1v) + _headsum(w_ref[...], e2_ref[...]) - wcols.T
               + jnp.where(ii == trow, d_tot, 0.0))
        da = _dot_hi(tri_ref[...], dcs)
        ddt_ref[0] = (da * a_ref[0] + _headsum(dxs * xb, e1v))[:, 0:HPG]
        da_ref[0] += jnp.sum(da[:, 0:HPG] * dt4_ref[0], axis=0, keepdims=True)
        dgm = dg.astype(MMD)
        dz = (e * dyb).astype(MMD)
        dc_ref[...] = _dot(dgm, bm) + _dot(dz, hpm, NT)
        db_ref[...] = _dot(dgm, cm, TN) + _dot((xs * dec).astype(MMD), dhm, NT)
        dh_ref[...] = dh * et + _dot(cm, dz, TN)

    const = lambda shape: pl.BlockSpec(shape, lambda g, c: (0,) * len(shape))
    return pl.pallas_call(
        body, name=name, grid=(SSD_GROUPS, nc),
        in_specs=[sp["x"], sp["b"], sp["c"], sp["lanes"], sp["lanes"], sp["col"], sp["rowt"],
                  pl.BlockSpec((1, 128, GW), lambda g, c: (di * SSD_GROUPS + g, 0, 0)), sp["drow"],
                  pl.BlockSpec((1, 1, 128), lambda g, c: (g, 0, 0)), sp["y"], sp["h"],
                  const((CHUNK, CHUNK)), const((GW, 128)), const((HPG * CHUNK, 128))],
        out_specs=[sp["y"], sp["n"], sp["n"], sp["col"], pl.BlockSpec((1, 1, HPG), lambda g, c: (g, 0, 0))],
        out_shape=[jax.ShapeDtypeStruct((s, 2048), F32), jax.ShapeDtypeStruct((s, SSD_GROUPS * SSD_N), F32),
                   jax.ShapeDtypeStruct((s, SSD_GROUPS * SSD_N), F32), jax.ShapeDtypeStruct((SSD_GROUPS, s, HPG), F32),
                   jax.ShapeDtypeStruct((SSD_GROUPS, 1, HPG), F32)],
        scratch_shapes=[pltpu.VMEM((SSD_N, GW), F32), pltpu.VMEM((CHUNK, HPG * CHUNK), F32), pltpu.VMEM((CHUNK, GW), F32)],
        compiler_params=_cp(("arbitrary", "arbitrary")),
    )(xc, xc, xc, dt, cs, dt4, cst, ex, drow, arow4, dy, hprev, _tri(anti), e1, e2)


def _gnorm_fwd(ya, yb, proj, w, *, name):
    s = ya.shape[0]
    tm = _tile(s, 256)

    def body(a_ref, b_ref, z_ref, w_ref, o_ref):
        zv = z_ref[...].astype(F32)
        t = (a_ref[...] + b_ref[...]) * (zv * _sigmoid(zv))
        r = lax.rsqrt(jnp.mean(t * t, axis=-1, keepdims=True) + EPS)
        o_ref[...] = ((t * r) * w_ref[...]).astype(o_ref.dtype)

    big = pl.BlockSpec((tm, 2048), lambda i: (i, 0))
    row = pl.BlockSpec((1, 2048), lambda i: (0, 0))
    return pl.pallas_call(
        body, name=name, grid=(s // tm,), in_specs=[big, big, big, row], out_specs=big,
        out_shape=jax.ShapeDtypeStruct((s, 2048), MMD), compiler_params=_cp(("arbitrary",)),
    )(ya, yb, proj, w)


def _gnorm_bwd(dout, ya, yb, proj, w, *, name):
    s = ya.shape[0]
    tm = _tile(s, 256)

    def body(do_ref, a_ref, b_ref, z_ref, w_ref, dy_ref, dz_ref, dw_ref):
        zv = z_ref[...].astype(F32)
        sg = _sigmoid(zv)
        sz = zv * sg
        y = a_ref[...] + b_ref[...]
        t = y * sz
        r = lax.rsqrt(jnp.mean(t * t, axis=-1, keepdims=True) + EPS)
        nv = t * r
        dov = do_ref[...].astype(F32)
        _acc_rows(dw_ref, jnp.sum(dov * nv, axis=0, keepdims=True), pl.program_id(0) == 0)
        dn = dov * w_ref[...]
        dt_ = r * (dn - nv * jnp.mean(dn * nv, axis=-1, keepdims=True))
        dy_ref[...] = dt_ * sz
        dz_ref[...] = (dt_ * y * (sg * (1.0 + zv * (1.0 - sg)))).astype(dz_ref.dtype)

    big = pl.BlockSpec((tm, 2048), lambda i: (i, 0))
    row = pl.BlockSpec((1, 2048), lambda i: (0, 0))
    return pl.pallas_call(
        body, name=name, grid=(s // tm,), in_specs=[big, big, big, big, row], out_specs=[big, big, row],
        out_shape=[jax.ShapeDtypeStruct((s, 2048), F32), jax.ShapeDtypeStruct((s, 2048), MMD),
                   jax.ShapeDtypeStruct((1, 2048), F32)],
        compiler_params=_cp(("arbitrary",)),
    )(dout, ya, yb, proj, w)


def _colsum_prod(a, b, *, name):
    s, n = a.shape
    tm = _tile(s, 256)

    def body(a_ref, b_ref, o_ref):
        _acc_rows(o_ref, jnp.sum(a_ref[...].astype(F32) * b_ref[...].astype(F32), axis=0, keepdims=True),
                  pl.program_id(0) == 0)

    big = pl.BlockSpec((tm, n), lambda i: (i, 0))
    return pl.pallas_call(
        body, name=name, grid=(s // tm,), in_specs=[big, big], out_specs=pl.BlockSpec((1, n), lambda i: (0, 0)),
        out_shape=jax.ShapeDtypeStruct((1, n), F32), compiler_params=_cp(("arbitrary",)),
    )(a, b)


def _heads(a, n):
    return a.reshape(a.shape[0], n, HEAD_DIM).transpose(1, 0, 2)


def _unheads(a):
    return a.transpose(1, 0, 2).reshape(a.shape[1], a.shape[0] * HEAD_DIM)


def _per_group(a):
    return a.reshape(a.shape[0], SSD_GROUPS, HPG).transpose(1, 0, 2)


def _per_group_t(a):
    s = a.shape[0]
    return a.reshape(s // CHUNK, CHUNK, SSD_GROUPS, HPG).transpose(2, 0, 3, 1)


def _local_step(x, target, mod, wts, small):
    s, d = x.shape
    shift1, scale1, gate1, shift2, scale2, gate2 = [mod[i:i + 1] for i in range(6)]

    h1 = _ln_mod(x, small["norm1_w"], scale1, shift1, name="ln1")
    proj = _mm(h1, wts["w_in_p"], name="in_proj", outs=[MMD], tm=512, tn=2944, b_outer=True)
    dt_raw = _mm(h1, wts["w_dt"], name="dt_proj", outs=[F32], tm=512, tn=128)

    qk_w = jnp.concatenate([jnp.tile(small["q_norm_w"], (1, N_Q_HEADS)), jnp.tile(small["k_norm_w"], (1, N_KV_HEADS))], axis=1)
    qk_sc = jnp.concatenate([jnp.full((1, N_Q_HEADS * HEAD_DIM), HEAD_DIM ** -0.5, F32),
                             jnp.ones((1, N_KV_HEADS * HEAD_DIM), F32)], axis=1)
    tabs = _rope_tables(s)
    qk, qkt = _qk_fwd(proj, qk_w, qk_sc, tabs, name="qk_fwd")
    qkt = qkt.reshape(N_Q_HEADS + N_KV_HEADS, HEAD_DIM, s)
    k_h = _heads(qk[:, N_Q_HEADS * HEAD_DIM:], N_KV_HEADS)
    v_sd = proj[:, V0:V0 + N_KV_HEADS * HEAD_DIM]
    v_h = _heads(v_sd, N_KV_HEADS)
    vta = jnp.concatenate([v_sd.T.reshape(N_KV_HEADS, HEAD_DIM, s), jnp.ones((N_KV_HEADS, V_AUG - HEAD_DIM, s), MMD)], axis=1)
    ot, lse = _flash_fwd(qkt, vta, name="flash_fwd")
    ot2 = ot.reshape(N_Q_HEADS * HEAD_DIM, s)

    w8 = jnp.pad(small["conv_w"], ((0, 8 - D_CONV), (0, 0)))
    xc = _conv_fwd(proj, w8, small["conv_b"], name="conv_fwd")
    a_neg = -jnp.exp(small["A_log"])
    arow = jnp.pad(a_neg.reshape(1, 2 * SSD_HEADS), ((0, 0), (0, 128 - 2 * SSD_HEADS)))
    bias_row = jnp.pad(small["dt_bias"].reshape(1, 2 * SSD_HEADS), ((0, 0), (0, 128 - 2 * SSD_HEADS)))
    dt, cs = _dt_fwd(dt_raw, bias_row, arow, name="dt_fwd")
    drow = jnp.repeat(small["ssd_D"], SSD_P, axis=1)
    dirs = []
    for di in range(2):
        cols = slice(di * SSD_HEADS, (di + 1) * SSD_HEADS)
        dirs.append(dict(
            dt4=_per_group(dt[:, cols]), cst=_per_group_t(cs[:, cols]),
            drow=drow if di == 0 else jnp.zeros_like(drow),
            arow4=jnp.pad(a_neg[di].reshape(SSD_GROUPS, 1, HPG), ((0, 0), (0, 0), (0, 128 - HPG)))))
    ex = _expand_mats()
    ys = []
    for di, dd in enumerate(dirs):
        y, dd["hprev"] = _ssd_fwd(xc, dt, cs, dd["cst"], ex, dd["drow"], di, name=f"ssd_fwd{di}")
        ys.append(y)
    ssdn = _gnorm_fwd(ys[0], ys[1], proj, small["ssd_norm_w"], name="gnorm_fwd")

    a_o = _mm(ot2, wts["w_attn_out"], name="attn_out", outs=[MMD], ta=True, tm=512, tn=1024)

    def merge_epi(acc, ao, ga, gs):
        return (_sigmoid(ga.astype(F32)) * ao.astype(F32) + _sigmoid(gs.astype(F32)) * acc, acc)

    merged, b_o = _mm(ssdn, wts["w_ssd_out"], name="ssd_out", outs=[MMD, MMD], tm=512, tn=512,
                      extras=[(a_o, "tile", 0), (proj, "tile", GA0), (proj, "tile", GS0)], epi=merge_epi)

    def res_epi(acc, res, gate):
        return (res + gate * acc, acc)

    x1, mo = _mm(merged, wts["w_o"], name="w_o", outs=[F32, MMD], tm=512, tn=512,
                 extras=[(x, "tile", 0), (gate1, "row", 0)], epi=res_epi)
    h2 = _ln_mod(x1, small["norm2_w"], scale2, shift2, name="ln2")

    def relu2_epi(acc):
        rl = jnp.maximum(acc, 0.0)
        return (rl * rl, rl)

    act, rl = _mm(h2, wts["w_mlp1"], name="mlp1", outs=[MMD, MMD], tm=512, tn=1024, epi=relu2_epi, b_outer=True)

    def loss_epi(acc, res, gate, tgt):
        return ((res + gate * acc - tgt) * (1.0 / d), acc)

    dy, ffo = _mm(act, wts["w_mlp2"], name="mlp2", outs=[F32, MMD], tm=512, tn=1024, vmem=VMEM_BIG,
                  extras=[(x1, "tile", 0), (gate2, "row", 0), (target, "tile", 0)], epi=loss_epi)
    loss = _sumsq(dy, name="loss") * (0.5 * d)

    gw = {}
    gs_ = {}
    dffo, dgate2 = _gate_bwd(dy, ffo, gate2, name="gate2_bwd")
    dpre = _mm(dffo, wts["w_mlp2"], name="mlp2_dx", outs=[MMD], nt=True, tm=512, tn=1024,
               extras=[(rl, "tile", 0)], epi=lambda acc, r: (acc * (2.0 * r.astype(F32)),))
    gw["w_mlp2"] = _mm_tn(act, dffo, name="mlp2_dw")
    dh2 = _mm(dpre, wts["w_mlp1"], name="mlp1_dx", outs=[F32], nt=True, tm=512, tn=1024)
    gw["w_mlp1"] = _mm_tn(h2, dpre, name="mlp1_dw")
    dx1, dshift2, dscale2, gs_["norm2_w"] = _ln_mod_bwd(dh2, x1, small["norm2_w"], scale2, dy, name="ln2_bwd")
    dmo, dgate1 = _gate_bwd(dx1, mo, gate1, name="gate1_bwd")

    def merge_bwd_epi(acc, ao, bo, ga, gs):
        sa, ss = _sigmoid(ga.astype(F32)), _sigmoid(gs.astype(F32))
        return (acc * sa, acc * ss, acc * ao.astype(F32) * sa * (1.0 - sa), acc * bo.astype(F32) * ss * (1.0 - ss))

    da_o, db_o, dga, dgs = _mm(dmo, wts["w_o"], name="w_o_dx", outs=[MMD] * 4, nt=True, tm=512, tn=512,
                               extras=[(a_o, "tile", 0), (b_o, "tile", 0), (proj, "tile", GA0), (proj, "tile", GS0)],
                               epi=merge_bwd_epi)
    gw["w_o"] = _mm_tn(merged, dmo, name="w_o_dw")
    dot = _mm(wts["w_attn_out"], da_o, name="attn_out_dx", outs=[MMD], nt=True, tm=512, tn=512)
    gw["w_attn_out"] = _mm(ot2, da_o, name="attn_out_dw", outs=[F32], tm=256, tn=512, vmem=VMEM_BIG)
    dssdn = _mm(db_o, wts["w_ssd_out"], name="ssd_out_dx", outs=[MMD], nt=True, tm=512, tn=512)
    gw["w_ssd_out"] = _mm_tn(ssdn, db_o, name="ssd_out_dw")

    dyssd, dz, gs_["ssd_norm_w"] = _gnorm_bwd(dssdn, ys[0], ys[1], proj, small["ssd_norm_w"], name="gnorm_bwd")
    gs_["ssd_D"] = _colsum_prod(dyssd, xc[:, 0:2048], name="ssd_d_grad").reshape(SSD_HEADS, SSD_P).sum(axis=1).reshape(1, SSD_HEADS)
    dxc, ddts, das = [], [], []
    for di, dd in enumerate(dirs):
        dxs, dbm, dcm, ddt4, da4 = _ssd_bwd(xc, dt, cs, dd["dt4"], dd["cst"], ex, dd["drow"], dd["arow4"],
                                            dyssd, dd["hprev"], di, name=f"ssd_bwd{di}")
        dxc.append((dxs, dbm, dcm))
        ddts.append(ddt4.transpose(1, 0, 2).reshape(s, SSD_HEADS))
        das.append(da4.reshape(1, SSD_HEADS))
    conv_parts, col0 = [], 0
    for part, (ga, gb) in enumerate(zip(*dxc)):
        conv_parts.append(_conv_bwd(proj, col0, ga, gb, w8, small["conv_b"], name=f"conv_bwd{part}"))
        col0 += ga.shape[1]
    dxbc, dw8, gs_["conv_b"] = [jnp.concatenate(t, axis=1) for t in zip(*conv_parts)]
    gs_["conv_w"] = dw8[0:D_CONV]
    gs_["A_log"] = jnp.concatenate(das, axis=0) * a_neg
    ddt = jnp.pad(jnp.concatenate(ddts, axis=1), ((0, 0), (0, 128 - 2 * SSD_HEADS)))
    ddt_raw, dbias = _dt_bwd(ddt, dt_raw, bias_row, name="dt_bwd")
    gs_["dt_bias"] = dbias[:, 0:2 * SSD_HEADS].reshape(2, SSD_HEADS)

    dqt, dk_h, dv_h = _flash_bwd(qkt, k_h, v_h, dot.reshape(N_Q_HEADS, HEAD_DIM, s), ot, lse, name="flash_bwd")
    dqkt = jnp.concatenate([dqt.reshape(N_Q_HEADS * HEAD_DIM, s),
                            dk_h.transpose(0, 2, 1).reshape(N_KV_HEADS * HEAD_DIM, s)], axis=0)
    dqk_u, dqk_w = _qk_bwd(dqkt, proj, qk_w, qk_sc, tabs, name="qk_bwd")
    gs_["q_norm_w"] = dqk_w[:, 0:N_Q_HEADS * HEAD_DIM].reshape(N_Q_HEADS, HEAD_DIM).sum(axis=0, keepdims=True)
    gs_["k_norm_w"] = dqk_w[:, N_Q_HEADS * HEAD_DIM:].reshape(N_KV_HEADS, HEAD_DIM).sum(axis=0, keepdims=True)
    dv = _unheads(dv_h).astype(MMD)

    dproj = jnp.concatenate([dz, dga, dgs, dxbc, dqk_u, dv, ddt_raw], axis=1)
    dh1 = _mm(dproj, wts["w_in_p"], name="in_proj_dx", outs=[F32], nt=True, tm=256, tn=1024, vmem=VMEM_BIG)
    gw["w_in_p"] = _mm_tn(h1, dproj, name="in_proj_dw", tk=1024, tn=2944, tmm=1024, vmem=VMEM_BIG)
    grad_x, dshift1, dscale1, gs_["norm1_w"] = _ln_mod_bwd(dh1, x, small["norm1_w"], scale1, dx1, name="ln1_bwd")
    dmod = jnp.concatenate([dshift1, dscale1, dgate1, dshift2, dscale2, dgate2], axis=0)
    return loss, grad_x, dmod, gw, gs_


N_DEV = 8
N_CHIP = 4
ANY = pl.BlockSpec(memory_space=pl.ANY)


def _place():
    return lax.axis_index("x"), lax.axis_index("y"), lax.axis_index("c")


def _allgather8(v, *, name):
    m_per, n = v.shape

    def body(x_ref, out_ref, send_sems, recv_sems, local_sem):
        x, y, c = _place()
        me, sibling = (x, y, c), (x, y, 1 - c)
        chips = [(1 - x, y), (x, 1 - y), (1 - x, 1 - y)]

        def rows(px, py, pc):
            return out_ref.at[pl.ds((4 * px + 2 * py + pc) * m_per, m_per), :]

        def copy(k, block, to, src=None):
            return pltpu.make_async_remote_copy(
                src_ref=rows(*block) if src is None else src, dst_ref=rows(*block),
                send_sem=send_sems.at[k], recv_sem=recv_sems.at[k], device_id=to, device_id_type=MESH)

        mine = pltpu.make_async_copy(x_ref, rows(*me), local_sem)
        mine.start()
        first = [copy(0, me, sibling, src=x_ref)]
        first += [copy(1 + j, me, (*chip, c), src=x_ref) for j, chip in enumerate(chips)]
        for cp in first:
            cp.start()
        passed = [copy(4 + j, (*chip, c), sibling) for j, chip in enumerate(chips)]
        for j, chip in enumerate(chips):
            copy(1 + j, (*chip, c), me).wait_recv()
            passed[j].start()
        copy(0, sibling, me).wait_recv()
        for j, chip in enumerate(chips):
            copy(4 + j, (*chip, 1 - c), me).wait_recv()
        for cp in first + passed:
            cp.wait_send()
        mine.wait()

    return pl.pallas_call(
        body, name=name, out_shape=jax.ShapeDtypeStruct((N_DEV * m_per, n), v.dtype),
        in_specs=[pl.BlockSpec(memory_space=pltpu.VMEM)], out_specs=pl.BlockSpec(memory_space=pltpu.VMEM),
        scratch_shapes=[pltpu.SemaphoreType.DMA((7,)), pltpu.SemaphoreType.DMA((7,)), pltpu.SemaphoreType.DMA],
    )(v)


def _scatter_chips(src, *, name):
    def body(x_ref, out_ref, send_sems, recv_sems):
        x, y, c = _place()
        k = 2 * x + y
        chips = [(1 - x, y), (x, 1 - y), (1 - x, 1 - y)]
        ids = [2 * cx + cy for cx, cy in chips]

        def copy(j, slot):
            return pltpu.make_async_remote_copy(
                src_ref=x_ref.at[ids[j]], dst_ref=out_ref.at[slot], send_sem=send_sems.at[j], recv_sem=recv_sems.at[j],
                device_id=(*chips[j], c), device_id_type=MESH)

        sends = [copy(j, k) for j in range(3)]
        for cp in sends:
            cp.start()
        for j in range(3):
            copy(j, ids[j]).wait_recv()
        for cp in sends:
            cp.wait_send()

    return pl.pallas_call(
        body, name=name, out_shape=jax.ShapeDtypeStruct(src.shape, src.dtype), in_specs=[ANY], out_specs=ANY,
        scratch_shapes=[pltpu.SemaphoreType.DMA((3,)), pltpu.SemaphoreType.DMA((3,))],
    )(src)


def _row_tile(r, pref=512):
    return max(t for t in range(16, pref + 1, 16) if r % t == 0)


def _gather_weights(src, *, name):
    r = src.shape[0]
    hr = r // 2
    assert r == 2 * hr and hr % 16 == 0

    def body(x_ref, out_ref, send_sems, recv_sems):
        x, y, c = _place()
        k = 2 * x + y
        chips = [(1 - x, y), (x, 1 - y), (1 - x, 1 - y)]
        ids = [2 * cx + cy for cx, cy in chips]
        mine_rows = pl.ds(pl.multiple_of(c * hr, 16), hr)
        other_rows = pl.ds(pl.multiple_of((1 - c) * hr, 16), hr)

        def copy(sem, src_ref, slot, rows, to):
            return pltpu.make_async_remote_copy(
                src_ref=src_ref, dst_ref=out_ref.at[slot, rows], send_sem=send_sems.at[sem], recv_sem=recv_sems.at[sem],
                device_id=to, device_id_type=MESH)

        sends = [copy(j, x_ref.at[mine_rows], k, mine_rows, (cx, cy, c)) for j, (cx, cy) in enumerate(chips)]
        for cp in sends:
            cp.start()
        passed = [copy(3 + j, out_ref.at[ids[j], mine_rows], ids[j], mine_rows, (x, y, 1 - c)) for j in range(3)]
        for j, (cx, cy) in enumerate(chips):
            copy(j, x_ref.at[mine_rows], ids[j], mine_rows, (cx, cy, c)).wait_recv()
            passed[j].start()
        for j in range(3):
            copy(3 + j, out_ref.at[ids[j], other_rows], ids[j], other_rows, (x, y, 1 - c)).wait_recv()
        for cp in sends + passed:
            cp.wait_send()

    return pl.pallas_call(
        body, name=name, out_shape=jax.ShapeDtypeStruct((N_CHIP,) + tuple(src.shape), src.dtype),
        in_specs=[ANY], out_specs=ANY,
        scratch_shapes=[pltpu.SemaphoreType.DMA((6,)), pltpu.SemaphoreType.DMA((6,))],
    )(src)


def _pair_swap(a, *, name):
    n, r, cols = a.shape
    hr = r // 2

    def body(x_ref, out_ref, send_sem, recv_sem):
        x, y, c = _place()
        other_rows = pl.ds(pl.multiple_of((1 - c) * hr, 16), hr)
        cp = pltpu.make_async_remote_copy(src_ref=x_ref.at[:, other_rows], dst_ref=out_ref, send_sem=send_sem,
                                          recv_sem=recv_sem, device_id=(x, y, 1 - c), device_id_type=MESH)
        cp.start()
        cp.wait()

    return pl.pallas_call(
        body, name=name, out_shape=jax.ShapeDtypeStruct((n, hr, cols), a.dtype), in_specs=[ANY], out_specs=ANY,
        scratch_shapes=[pltpu.SemaphoreType.DMA, pltpu.SemaphoreType.DMA],
    )(a)


def _sibling_copy(a, *, name):
    def body(x_ref, out_ref, send_sem, recv_sem):
        x, y, c = _place()
        cp = pltpu.make_async_remote_copy(src_ref=x_ref, dst_ref=out_ref, send_sem=send_sem, recv_sem=recv_sem,
                                          device_id=(x, y, 1 - c), device_id_type=MESH)
        cp.start()
        cp.wait()

    return pl.pallas_call(
        body, name=name, out_shape=jax.ShapeDtypeStruct(a.shape, a.dtype), in_specs=[ANY], out_specs=ANY,
        scratch_shapes=[pltpu.SemaphoreType.DMA, pltpu.SemaphoreType.DMA],
    )(a)


def _sum_slots(a, *, name):
    _, r, c = a.shape
    tr = _row_tile(r)

    def body(a_ref, o_ref):
        acc = a_ref[0].astype(F32)
        for j in range(1, N_CHIP):
            acc = acc + a_ref[j].astype(F32)
        o_ref[...] = acc

    return pl.pallas_call(
        body, name=name, grid=(r // tr,), in_specs=[pl.BlockSpec((N_CHIP, tr, c), lambda i: (0, i, 0))],
        out_specs=pl.BlockSpec((tr, c), lambda i: (i, 0)), out_shape=jax.ShapeDtypeStruct((r, c), F32),
        compiler_params=_cp(("arbitrary",)),
    )(a)


def _add2(a, b, *, name):
    r, c = a.shape
    tr = _row_tile(r)

    def body(a_ref, b_ref, o_ref):
        o_ref[...] = (a_ref[...].astype(F32) + b_ref[...].astype(F32)).astype(o_ref.dtype)

    spec = pl.BlockSpec((tr, c), lambda i: (i, 0))
    return pl.pallas_call(
        body, name=name, grid=(r // tr,), in_specs=[spec, spec], out_specs=spec,
        out_shape=jax.ShapeDtypeStruct((r, c), a.dtype), compiler_params=_cp(("arbitrary",)),
    )(a, b)


BIG = ("w_in", "w_mlp1", "w_attn_out", "w_ssd_out", "w_o", "w_mlp2")
COL_SHARDED = ("w_in", "w_mlp1")
SMALL = ("b_ada", "norm1_w", "norm2_w", "q_norm_w", "k_norm_w", "conv_b", "A_log", "dt_bias", "ssd_D", "ssd_norm_w")
NAMES = ("w_ada", "b_ada", "norm1_w", "norm2_w", "w_in", "q_norm_w", "k_norm_w", "conv_w", "conv_b", "A_log", "dt_bias",
         "ssd_D", "ssd_norm_w", "w_attn_out", "w_ssd_out", "w_o", "w_mlp1", "w_mlp2")
W_IN_COLS = 8768


def _permute_in(w):
    return jnp.concatenate([w[:, 4608:6656], w[:, 6720:8768], w[:, 1536:4608], w[:, 0:1536], w[:, 6656:6720],
                            jnp.zeros((w.shape[0], PW - W_IN_COLS), w.dtype)], axis=1)


def _unpermute_in(wp):
    return jnp.concatenate([wp[:, Q0:DT0], wp[:, XS0:Q0], wp[:, Z0:GA0], wp[:, DT0:DT0 + 64], wp[:, GA0:XS0]], axis=1)


def _pad_to(v, n):
    return jnp.pad(v, (0, n - v.shape[0]))


def _step(w, m, v, loss_target):
    xi, yi, ci = _place()
    chip = 2 * xi + yi
    dev = 4 * xi + 2 * yi + ci
    x, tgt = w["x"], loss_target
    d = x.shape[1]

    cw = w["conv_w"].shape[1]
    v0 = _pad_to(jnp.concatenate([w["c"].reshape(-1), w["conv_w"].reshape(-1)]), 5120).reshape(8, 640)
    g0 = _allgather8(v0, name="ag_cond").reshape(N_DEV, 5120)
    c_all = g0[:, 0:d]
    conv_w = jnp.concatenate([g0[2 * k, d:d + D_CONV * cw].reshape(D_CONV, cw) for k in range(N_CHIP)], axis=1)
    sc = _silu_cast(c_all, name="silu_c")
    modp = _mm(sc, w["w_ada"].astype(MMD), name="ada_fwd", outs=[F32], tm=8, tn=512)
    g1 = _allgather8(modp, name="ag_mod").reshape(N_DEV, N_DEV, modp.shape[1])
    mod_all = jnp.concatenate([g1[2 * k] for k in range(N_CHIP)], axis=1)
    mod = (lax.dynamic_slice_in_dim(mod_all, dev, 1, axis=0) + w["b_ada"]).reshape(6, d)

    packed = [w[n].astype(MMD).reshape(-1, d) for n in BIG]
    n_rows = sum(p.shape[0] for p in packed)
    pad_rows = -n_rows % 32
    packed = jnp.concatenate(packed + [jnp.zeros((pad_rows, d), MMD)], axis=0)
    gath = lax.dynamic_update_slice_in_dim(_gather_weights(packed, name="ag_weights"), packed[None], chip, axis=0)
    full, r0 = {}, 0
    for n in BIG:
        rows = w[n].size // d
        part = gath[:, r0:r0 + rows]
        if n in COL_SHARDED:
            full[n] = jnp.concatenate([part[k].reshape(w[n].shape) for k in range(N_CHIP)], axis=1)
        else:
            full[n] = part.reshape(N_CHIP * w[n].shape[0], w[n].shape[1])
        r0 += rows
    wts = {n: full[n] for n in BIG if n != "w_in"}
    wts["w_in_p"] = _permute_in(full["w_in"])
    wts["w_dt"] = jnp.pad(full["w_in"][:, 6656:6720], ((0, 0), (0, 64)))
    small = {n: w[n] for n in SMALL if n != "b_ada"}
    small["conv_w"] = conv_w

    loss, grad_x, dmod, gw, gs = _local_step(x, tgt, mod, wts, small)
    loss = lax.psum(loss[0, 0], ("x", "y", "c"))

    gw["w_in"] = _unpermute_in(gw.pop("w_in_p"))
    slots = []
    for k in range(N_CHIP):
        parts = []
        for n in BIG:
            r_, c_ = w[n].shape
            blk = gw[n][:, k * c_:(k + 1) * c_] if n in COL_SHARDED else gw[n][k * r_:(k + 1) * r_]
            parts.append(blk.astype(MMD).reshape(-1, d))
        slots.append(jnp.concatenate(parts + [jnp.zeros((pad_rows, d), MMD)], axis=0))
    slots = jnp.stack(slots)
    hr = (n_rows + pad_rows) // 2
    theirs = _pair_swap(slots, name="rs_pair")
    ours = lax.dynamic_slice_in_dim(slots, ci * hr, hr, axis=1)
    pair = _add2(ours.reshape(N_CHIP * hr, d), theirs.reshape(N_CHIP * hr, d), name="rs_pair_sum")
    pair = pair.reshape(N_CHIP, hr, d)
    recv = _scatter_chips(pair, name="rs_grads")
    recv = lax.dynamic_update_slice_in_dim(recv, lax.dynamic_slice_in_dim(pair, chip, 1, axis=0), chip, axis=0)
    half = _sum_slots(recv, name="rs_sum")
    other = _sibling_copy(half, name="rs_sibling")
    total = jnp.where(ci == 0, jnp.concatenate([half, other], axis=0), jnp.concatenate([other, half], axis=0))
    grads, r0 = {}, 0
    for n in BIG:
        rows = w[n].size // d
        grads[n] = total[r0:r0 + rows].reshape(w[n].shape)
        r0 += rows

    order = [dmod.reshape(-1)] + [gs[n].reshape(-1) for n in SMALL if n != "b_ada"] + [gs["conv_w"].reshape(-1)]
    vec = jnp.concatenate(order)
    n_small = vec.shape[0]
    n_pad = -(-n_small // 1024) * 1024
    g2 = _allgather8(_pad_to(vec, n_pad).reshape(8, n_pad // 8), name="ag_small")
    tot = _rows_sum(g2, N_DEV, name="small_sum").reshape(-1)
    dmod_all = g2.reshape(N_DEV, n_pad)[:, 0:6 * d]
    off = 0
    for n in SMALL:
        grads[n] = tot[off:off + w[n].size].reshape(w[n].shape)
        off += w[n].size
    conv_full = tot[off:off + D_CONV * N_CHIP * cw].reshape(D_CONV, N_CHIP * cw)
    grads["conv_w"] = lax.dynamic_slice_in_dim(conv_full, chip * cw, cw, axis=1)
    ada_cols = w["w_ada"].shape[1]
    dmod_mine = lax.dynamic_slice_in_dim(dmod_all, chip * ada_cols, ada_cols, axis=1).astype(MMD)
    grads["w_ada"] = _mm_tn(sc, dmod_mine, name="ada_dw", tk=512, tn=512, tmm=8)

    delta, new_m, new_v = {}, {}, {}
    pack = lambda t: jnp.concatenate([t[n].reshape(-1) for n in SMALL]).reshape(1, -1)
    ds_, ms_, vs_ = _adamw(pack(w), pack(grads), pack(m), pack(v), name="adamw_small")
    off = 0
    for n in SMALL:
        for dst, src in ((delta, ds_), (new_m, ms_), (new_v, vs_)):
            dst[n] = src[0, off:off + w[n].size].reshape(w[n].shape)
        off += w[n].size
    for n in ("w_ada", "conv_w") + BIG:
        delta[n], new_m[n], new_v[n] = _adamw(w[n], grads[n], m[n], v[n], name="adamw_" + n)
    return loss, grad_x, grads, delta, new_m, new_v


def kernel(x, c, w_ada, b_ada, norm1_w, norm2_w, w_in, q_norm_w, k_norm_w, conv_w, conv_b, A_log, dt_bias, ssd_D, ssd_norm_w, w_attn_out, w_ssd_out, w_o, w_mlp1, w_mlp2, loss_target, m_w_ada, m_b_ada, m_norm1_w, m_norm2_w, m_w_in, m_q_norm_w, m_k_norm_w, m_conv_w, m_conv_b, m_A_log, m_dt_bias, m_ssd_D, m_ssd_norm_w, m_w_attn_out, m_w_ssd_out, m_w_o, m_w_mlp1, m_w_mlp2, v_w_ada, v_b_ada, v_norm1_w, v_norm2_w, v_w_in, v_q_norm_w, v_k_norm_w, v_conv_w, v_conv_b, v_A_log, v_dt_bias, v_ssd_D, v_ssd_norm_w, v_w_attn_out, v_w_ssd_out, v_w_o, v_w_mlp1, v_w_mlp2):
    args = dict(locals())
    strip = lambda a: a[0] if a.ndim == 3 else a
    w = {n: strip(args[n]) for n in NAMES + ("x", "c")}
    m = {n: strip(args["m_" + n]) for n in NAMES}
    v = {n: strip(args["v_" + n]) for n in NAMES}
    loss, grad_x, grads, delta, new_m, new_v = _step(w, m, v, loss_target[0])
    like = lambda t, n: t.reshape(args[n].shape)
    return (loss, grad_x[None], *[like(grads[n], n) for n in NAMES], *[like(delta[n], n) for n in NAMES],
            *[like(new_m[n], n) for n in NAMES], *[like(new_v[n], n) for n in NAMES])
```

```python
import functools
import math

import jax
import jax.numpy as jnp
from jax import lax
from jax.experimental import pallas as pl
from jax.experimental.pallas import tpu as pltpu

F32 = jnp.float32
MMD = jnp.bfloat16
EPS = 1e-6
NEG = -1e30
MIB = 1024 * 1024
VMEM_BIG = 56 * MIB
VMEM_MID = 40 * MIB

GRID_W = 64
N_Q_HEADS, N_KV_HEADS, HEAD_DIM = 16, 4, 64
ROPE_THETA = 10000.0
SSD_HEADS, SSD_GROUPS, SSD_P, SSD_N, CHUNK = 32, 4, 64, 128, 128
HPG = SSD_HEADS // SSD_GROUPS
D_CONV = 5
ADAM_LR, ADAM_B1, ADAM_B2, ADAM_EPS, ADAM_WD, ADAM_STEP = 0.001, 0.9, 0.999, 1e-08, 0.01, 10

Z0, GA0, GS0, XS0, B0, C0, Q0, K0, V0, DT0, PW = 0, 2048, 3072, 4096, 6144, 6656, 7168, 8192, 8448, 8704, 8832

MESH = pl.DeviceIdType.MESH
NT = (((1,), (1,)), ((), ()))
TN = (((0,), (0,)), ((), ()))


def _cp(sem=None, vmem=VMEM_MID):
    return pltpu.CompilerParams(dimension_semantics=sem, vmem_limit_bytes=vmem)


def _tile(n, pref):
    t = min(n, pref)
    while n % t:
        t //= 2
    return t


def _dot(a, b, dims=None):
    if dims is None:
        return jnp.dot(a, b, preferred_element_type=F32)
    return lax.dot_general(a, b, dims, preferred_element_type=F32)


def _dot_hi(a, b):
    return jnp.dot(a, b, precision=lax.Precision.HIGHEST, preferred_element_type=F32)


def _sigmoid(x):
    return jax.nn.sigmoid(x)


def _mm(a, b, *, name, outs, nt=False, ta=False, extras=(), epi=None, tm=512, tn=512, n=None, b_outer=False,
        vmem=VMEM_MID):
    assert not (nt and ta)
    k, m = a.shape if ta else a.shape[::-1]
    if n is None:
        n = b.shape[0] if nt else b.shape[1]
    tm, tn = _tile(m, tm), _tile(n, tn)
    gi, gj = m // tm, n // tn
    if b_outer:
        grid = (gj, gi)
        ij = lambda p, q: (q, p)
    else:
        grid = (gi, gj)
        ij = lambda p, q: (p, q)
    if ta:
        a_spec = pl.BlockSpec((k, tm), lambda p, q: (0, ij(p, q)[0]))
    else:
        a_spec = pl.BlockSpec((tm, k), lambda p, q: (ij(p, q)[0], 0))
    if nt:
        b_spec = pl.BlockSpec((tn, k), lambda p, q: (ij(p, q)[1], 0))
    else:
        b_spec = pl.BlockSpec((k, tn), lambda p, q: (0, ij(p, q)[1]))
    e_specs = []
    for arr, kind, off in extras:
        ob = off // tn
        assert off % tn == 0
        if kind == "tile":
            e_specs.append(pl.BlockSpec((tm, tn), lambda p, q, ob=ob: (ij(p, q)[0], ob + ij(p, q)[1])))
        else:
            e_specs.append(pl.BlockSpec((1, tn), lambda p, q, ob=ob: (0, ob + ij(p, q)[1])))
    ne = len(extras)

    def body(a_ref, b_ref, *rest):
        acc = _dot(a_ref[...], b_ref[...], NT if nt else (TN if ta else None))
        res = epi(acc, *[e[...] for e in rest[:ne]]) if epi is not None else (acc,)
        for o_ref, r in zip(rest[ne:], res):
            o_ref[...] = r.astype(o_ref.dtype)

    out = pl.pallas_call(
        body, name=name, grid=grid,
        in_specs=[a_spec, b_spec] + e_specs,
        out_specs=[pl.BlockSpec((tm, tn), lambda p, q: ij(p, q)) for _ in outs],
        out_shape=[jax.ShapeDtypeStruct((m, n), dt) for dt in outs],
        compiler_params=_cp(("arbitrary", "arbitrary"), vmem),
    )(a, b, *[e[0] for e in extras])
    return out if len(outs) > 1 else out[0]


def _mm_tn(a, g, *, name, tk=512, tn=1024, tmm=1024, vmem=VMEM_MID):
    m, k = a.shape
    n = g.shape[1]
    tk, tn, tmm = _tile(k, tk), _tile(n, tn), _tile(m, tmm)

    def body(a_ref, g_ref, o_ref):
        p = _dot(a_ref[...], g_ref[...], TN)

        @pl.when(pl.program_id(2) == 0)
        def _():
            o_ref[...] = p

        @pl.when(pl.program_id(2) > 0)
        def _():
            o_ref[...] += p

    return pl.pallas_call(
        body, name=name, grid=(k // tk, n // tn, m // tmm),
        in_specs=[pl.BlockSpec((tmm, tk), lambda i, j, r: (r, i)), pl.BlockSpec((tmm, tn), lambda i, j, r: (r, j))],
        out_specs=pl.BlockSpec((tk, tn), lambda i, j, r: (i, j)),
        out_shape=jax.ShapeDtypeStruct((k, n), F32),
        compiler_params=_cp(("arbitrary", "arbitrary", "arbitrary"), vmem),
    )(a, g)


def _adamw(w, g, m, v, *, name):
    r, c = w.shape
    tr = _tile(r, 256) if r % 8 == 0 else r

    def body(w_ref, g_ref, m_ref, v_ref, d_ref, nm_ref, nv_ref):
        gg = g_ref[...]
        nm = ADAM_B1 * m_ref[...] + (1.0 - ADAM_B1) * gg
        nv = ADAM_B2 * v_ref[...] + (1.0 - ADAM_B2) * jnp.square(gg)
        m_hat = nm / (1.0 - ADAM_B1 ** ADAM_STEP)
        v_hat = nv / (1.0 - ADAM_B2 ** ADAM_STEP)
        d_ref[...] = -ADAM_LR * (m_hat / (jnp.sqrt(v_hat) + ADAM_EPS) + ADAM_WD * w_ref[...])
        nm_ref[...] = nm
        nv_ref[...] = nv

    spec = pl.BlockSpec((tr, c), lambda i: (i, 0))
    return pl.pallas_call(
        body, name=name, grid=(r // tr,), in_specs=[spec] * 4, out_specs=[spec] * 3,
        out_shape=[jax.ShapeDtypeStruct((r, c), F32)] * 3, compiler_params=_cp(("arbitrary",)),
    )(w, g, m, v)


def _rows_sum(a, groups, *, name):
    r = a.shape[0] // groups

    def body(a_ref, o_ref):
        acc = a_ref[0:r, :]
        for d in range(1, groups):
            acc = acc + a_ref[d * r:(d + 1) * r, :]
        o_ref[...] = acc

    return pl.pallas_call(body, name=name, out_shape=jax.ShapeDtypeStruct((r, a.shape[1]), F32))(a)


def _silu_cast(a, *, name):
    def body(a_ref, o_ref):
        x = a_ref[...]
        o_ref[...] = (x * _sigmoid(x)).astype(o_ref.dtype)

    return pl.pallas_call(body, name=name, out_shape=jax.ShapeDtypeStruct(a.shape, MMD))(a)


def _sumsq(a, *, name):
    m, n = a.shape
    tm = _tile(m, 512)

    def body(a_ref, o_ref):
        x = a_ref[...]
        p = jnp.sum(jnp.sum(x * x, axis=1, keepdims=True), axis=0, keepdims=True)

        @pl.when(pl.program_id(0) == 0)
        def _():
            o_ref[...] = p

        @pl.when(pl.program_id(0) > 0)
        def _():
            o_ref[...] += p

    return pl.pallas_call(
        body, name=name, grid=(m // tm,), in_specs=[pl.BlockSpec((tm, n), lambda i: (i, 0))],
        out_specs=pl.BlockSpec((1, 1), lambda i: (0, 0)), out_shape=jax.ShapeDtypeStruct((1, 1), F32),
        compiler_params=_cp(("arbitrary",)),
    )(a)


def _acc_rows(o_ref, p, first):
    @pl.when(first)
    def _():
        o_ref[...] = p

    @pl.when(jnp.logical_not(first))
    def _():
        o_ref[...] += p


def _ln_mod(x, w, scale, shift, *, name):
    s, d = x.shape
    tm = _tile(s, 512)

    def body(x_ref, w_ref, sc_ref, sh_ref, o_ref):
        xv = x_ref[...]
        r = lax.rsqrt(jnp.mean(xv * xv, axis=-1, keepdims=True) + EPS)
        o_ref[...] = ((xv * r) * w_ref[...] * (1.0 + sc_ref[...]) + sh_ref[...]).astype(o_ref.dtype)

    row = pl.BlockSpec((1, d), lambda i: (0, 0))
    big = pl.BlockSpec((tm, d), lambda i: (i, 0))
    return pl.pallas_call(
        body, name=name, grid=(s // tm,), in_specs=[big, row, row, row], out_specs=big,
        out_shape=jax.ShapeDtypeStruct((s, d), MMD), compiler_params=_cp(("arbitrary",)),
    )(x, w, scale, shift)


def _ln_mod_bwd(dh, x, w, scale, dres, *, name):
    s, d = x.shape
    tm = _tile(s, 512)

    def body(dh_ref, x_ref, w_ref, sc_ref, dres_ref, dx_ref, dsh_ref, dsc_ref, dw_ref):
        xv = x_ref[...]
        dhv = dh_ref[...].astype(F32)
        r = lax.rsqrt(jnp.mean(xv * xv, axis=-1, keepdims=True) + EPS)
        nv = xv * r
        wv = w_ref[...]
        g1 = 1.0 + sc_ref[...]
        dn = dhv * (wv * g1)
        dx_ref[...] = dres_ref[...] + r * (dn - nv * jnp.mean(dn * nv, axis=-1, keepdims=True))
        first = pl.program_id(0) == 0
        _acc_rows(dsh_ref, jnp.sum(dhv, axis=0, keepdims=True), first)
        _acc_rows(dsc_ref, jnp.sum(dhv * nv * wv, axis=0, keepdims=True), first)
        _acc_rows(dw_ref, jnp.sum(dhv * nv * g1, axis=0, keepdims=True), first)

    row = pl.BlockSpec((1, d), lambda i: (0, 0))
    big = pl.BlockSpec((tm, d), lambda i: (i, 0))
    return pl.pallas_call(
        body, name=name, grid=(s // tm,), in_specs=[big, big, row, row, big], out_specs=[big, row, row, row],
        out_shape=[jax.ShapeDtypeStruct((s, d), F32)] + [jax.ShapeDtypeStruct((1, d), F32)] * 3,
        compiler_params=_cp(("arbitrary",)),
    )(dh, x, w, scale, dres)


def _gate_bwd(dy, u, gate, *, name):
    s, d = dy.shape
    tm = _tile(s, 512)

    def body(dy_ref, u_ref, g_ref, du_ref, dg_ref):
        dyv = dy_ref[...]
        du_ref[...] = (dyv * g_ref[...]).astype(du_ref.dtype)
        _acc_rows(dg_ref, jnp.sum(dyv * u_ref[...].astype(F32), axis=0, keepdims=True), pl.program_id(0) == 0)

    row = pl.BlockSpec((1, d), lambda i: (0, 0))
    big = pl.BlockSpec((tm, d), lambda i: (i, 0))
    return pl.pallas_call(
        body, name=name, grid=(s // tm,), in_specs=[big, big, row], out_specs=[big, row],
        out_shape=[jax.ShapeDtypeStruct((s, d), MMD), jax.ShapeDtypeStruct((1, d), F32)],
        compiler_params=_cp(("arbitrary",)),
    )(dy, u, gate)


def _seg64(v, e):
    hi = v.astype(jnp.bfloat16)
    lo = (v - hi.astype(F32)).astype(jnp.bfloat16)
    return _dot(hi, e) + _dot(lo, e)


def _rope_tables(s):
    rows = s // GRID_W
    pos_row = jnp.repeat(jnp.arange(rows, dtype=jnp.int32), GRID_W).astype(F32)
    pos_col = jnp.tile(jnp.arange(GRID_W, dtype=jnp.int32), rows).astype(F32)
    axis_dim = HEAD_DIM // 2
    inv_freq = ROPE_THETA ** (-jnp.arange(0, axis_dim, 2, dtype=F32) / axis_dim)
    ang_r = pos_row[:, None] * inv_freq[None, :]
    ang_c = pos_col[:, None] * inv_freq[None, :]
    zero = jnp.zeros_like(ang_r)
    cos = jnp.concatenate([jnp.cos(ang_r), jnp.cos(ang_r), jnp.cos(ang_c), jnp.cos(ang_c)], axis=1)
    s_a = jnp.concatenate([-jnp.sin(ang_r), zero, -jnp.sin(ang_c), zero], axis=1)
    s_b = jnp.concatenate([zero, jnp.sin(ang_r), zero, jnp.sin(ang_c)], axis=1)
    return [jnp.tile(t, (1, 2)) for t in (cos, s_a, s_b)]


def _e128():
    i = jnp.arange(128)
    return (i[:, None] // 64 == i[None, :] // 64).astype(jnp.bfloat16)


QKW = N_Q_HEADS * HEAD_DIM + N_KV_HEADS * HEAD_DIM


def _qk_fwd(proj, wrow, scrow, tabs, *, name):
    s = proj.shape[0]
    tm = _tile(s, 512)

    def body(x_ref, w_ref, sc_ref, cos_ref, sa_ref, sb_ref, e_ref, o_ref, ot_ref):
        u = x_ref[...].astype(F32)
        r = lax.rsqrt(_seg64(u * u, e_ref[...]) * (1.0 / HEAD_DIM) + EPS)
        nv = (u * r) * w_ref[...]
        ro = nv * cos_ref[...] + pltpu.roll(nv, 112, 1) * sa_ref[...] + pltpu.roll(nv, 16, 1) * sb_ref[...]
        out = ro * sc_ref[...]
        o_ref[...] = out.astype(o_ref.dtype)
        ot_ref[...] = out.T.astype(ot_ref.dtype)

    tab = pl.BlockSpec((tm, 128), lambda i, j: (i, 0))
    row = pl.BlockSpec((1, 128), lambda i, j: (0, j))
    return pl.pallas_call(
        body, name=name, grid=(s // tm, QKW // 128),
        in_specs=[pl.BlockSpec((tm, 128), lambda i, j: (i, Q0 // 128 + j)), row, row, tab, tab, tab,
                  pl.BlockSpec((128, 128), lambda i, j: (0, 0))],
        out_specs=[pl.BlockSpec((tm, 128), lambda i, j: (i, j)), pl.BlockSpec((128, tm), lambda i, j: (j, i))],
        out_shape=[jax.ShapeDtypeStruct((s, QKW), MMD), jax.ShapeDtypeStruct((QKW, s), MMD)],
        compiler_params=_cp(("arbitrary", "arbitrary")),
    )(proj, wrow, scrow, *tabs, _e128())


def _qk_bwd(dqkt, proj, wrow, scrow, tabs, *, name):
    s = proj.shape[0]
    tm = _tile(s, 512)

    def body(d_ref, x_ref, w_ref, sc_ref, cos_ref, sa_ref, sb_ref, e_ref, du_ref, dw_ref):
        e = e_ref[...]
        d = d_ref[...].T * sc_ref[...]
        dn = d * cos_ref[...] + pltpu.roll(d * sa_ref[...], 16, 1) + pltpu.roll(d * sb_ref[...], 112, 1)
        u = x_ref[...].astype(F32)
        r = lax.rsqrt(_seg64(u * u, e) * (1.0 / HEAD_DIM) + EPS)
        uh = u * r
        _acc_rows(dw_ref, jnp.sum(dn * uh, axis=0, keepdims=True), pl.program_id(1) == 0)
        dnw = dn * w_ref[...]
        du_ref[...] = (r * (dnw - uh * (_seg64(dnw * uh, e) * (1.0 / HEAD_DIM)))).astype(du_ref.dtype)

    tab = pl.BlockSpec((tm, 128), lambda j, i: (i, 0))
    row = pl.BlockSpec((1, 128), lambda j, i: (0, j))
    return pl.pallas_call(
        body, name=name, grid=(QKW // 128, s // tm),
        in_specs=[pl.BlockSpec((128, tm), lambda j, i: (j, i)), pl.BlockSpec((tm, 128), lambda j, i: (i, Q0 // 128 + j)),
                  row, row, tab, tab, tab, pl.BlockSpec((128, 128), lambda j, i: (0, 0))],
        out_specs=[pl.BlockSpec((tm, 128), lambda j, i: (i, j)), row],
        out_shape=[jax.ShapeDtypeStruct((s, QKW), MMD), jax.ShapeDtypeStruct((1, QKW), F32)],
        compiler_params=_cp(("arbitrary", "arbitrary")),
    )(dqkt, proj, wrow, scrow, *tabs, _e128())


REP = N_Q_HEADS // N_KV_HEADS


def _lanes(ref):
    return jnp.concatenate([ref[r] for r in range(REP)], axis=1)


V_AUG = HEAD_DIM + 8


def _flash_fwd(qkt, vta, *, name):
    s = qkt.shape[2]
    tq, tk = _tile(s, 1024), _tile(s, 512)
    nk = s // tk
    lanes = REP * tq

    def body(q_ref, k_ref, v_ref, o_ref, lse_ref, m_ref, acc_ref):
        j = pl.program_id(2)

        @pl.when(j == 0)
        def _():
            m_ref[...] = jnp.full_like(m_ref, NEG)
            acc_ref[...] = jnp.zeros_like(acc_ref)

        st = _dot(k_ref[0], _lanes(q_ref), TN)
        m_prev = m_ref[...]
        m_new = jnp.maximum(m_prev, jnp.max(st, axis=0, keepdims=True))
        p = jnp.exp(st - m_new).astype(MMD)
        acc_ref[...] = jnp.exp(m_prev - m_new) * acc_ref[...] + _dot(v_ref[0], p)
        m_ref[...] = m_new

        @pl.when(j == nk - 1)
        def _():
            acc = acc_ref[...]
            l = acc[HEAD_DIM:HEAD_DIM + 1]
            o = acc[0:HEAD_DIM] / l
            ls = m_ref[...] + jnp.log(l)
            for r in range(REP):
                o_ref[r] = o[:, r * tq:(r + 1) * tq].astype(o_ref.dtype)
                lse_ref[r] = ls[:, r * tq:(r + 1) * tq]

    qspec = pl.BlockSpec((REP, HEAD_DIM, tq), lambda g, i, j: (g, 0, i))
    return pl.pallas_call(
        body, name=name, grid=(N_KV_HEADS, s // tq, nk),
        in_specs=[qspec, pl.BlockSpec((1, HEAD_DIM, tk), lambda g, i, j: (N_Q_HEADS + g, 0, j)),
                  pl.BlockSpec((1, V_AUG, tk), lambda g, i, j: (g, 0, j))],
        out_specs=[qspec, pl.BlockSpec((REP, 1, tq), lambda g, i, j: (g, 0, i))],
        out_shape=[jax.ShapeDtypeStruct((N_Q_HEADS, HEAD_DIM, s), MMD), jax.ShapeDtypeStruct((N_Q_HEADS, 1, s), F32)],
        scratch_shapes=[pltpu.VMEM((1, lanes), F32), pltpu.VMEM((V_AUG, lanes), F32)],
        compiler_params=_cp(("arbitrary", "arbitrary", "arbitrary"), VMEM_BIG),
    )(qkt, qkt, vta)


def _flash_bwd(qkt, k_h, v_h, dot, ot, lse, *, name):
    s = qkt.shape[2]
    tq, tk = _tile(s, 512), _tile(s, 1024)
    nk = s // tk

    def body(q_ref, kt_ref, k_ref, v_ref, do_ref, o_ref, lse_ref, dq_ref, dk_ref, dv_ref, dq_acc):
        i, j = pl.program_id(1), pl.program_id(2)
        q, do = _lanes(q_ref), _lanes(do_ref)
        delta = jnp.sum(do.astype(F32) * _lanes(o_ref).astype(F32), axis=0, keepdims=True)
        k, v = k_ref[0], v_ref[0]
        p = jnp.exp(_dot(k, q) - _lanes(lse_ref))
        dvc = _dot(p.astype(MMD), do, NT)
        ds = (p * (_dot(v, do) - delta)).astype(MMD)
        dkc = _dot(ds, q, NT)
        dqc = _dot(kt_ref[0], ds)
        rows = pl.ds(pl.multiple_of(j * tk, tk), tk)

        @pl.when(i == 0)
        def _():
            dk_ref[0, rows, :] = dkc
            dv_ref[0, rows, :] = dvc

        @pl.when(i > 0)
        def _():
            dk_ref[0, rows, :] += dkc
            dv_ref[0, rows, :] += dvc

        @pl.when(j == 0)
        def _():
            dq_acc[...] = dqc

        @pl.when(j > 0)
        def _():
            dq_acc[...] += dqc

        @pl.when(j == nk - 1)
        def _():
            acc = dq_acc[...]
            for r in range(REP):
                dq_ref[r] = acc[:, r * tq:(r + 1) * tq]

    qspec = pl.BlockSpec((REP, HEAD_DIM, tq), lambda g, i, j: (g, 0, i))
    kvin = pl.BlockSpec((1, tk, HEAD_DIM), lambda g, i, j: (g, j, 0))
    kvres = pl.BlockSpec((1, s, HEAD_DIM), lambda g, i, j: (g, 0, 0))
    return pl.pallas_call(
        body, name=name, grid=(N_KV_HEADS, s // tq, nk),
        in_specs=[qspec, pl.BlockSpec((1, HEAD_DIM, tk), lambda g, i, j: (N_Q_HEADS + g, 0, j)), kvin, kvin,
                  qspec, qspec, pl.BlockSpec((REP, 1, tq), lambda g, i, j: (g, 0, i))],
        out_specs=[qspec, kvres, kvres],
        out_shape=[jax.ShapeDtypeStruct((N_Q_HEADS, HEAD_DIM, s), F32), jax.ShapeDtypeStruct((N_KV_HEADS, s, HEAD_DIM), F32),
                   jax.ShapeDtypeStruct((N_KV_HEADS, s, HEAD_DIM), F32)],
        scratch_shapes=[pltpu.VMEM((HEAD_DIM, REP * tq), F32)],
        compiler_params=_cp(("arbitrary", "arbitrary", "arbitrary"), VMEM_BIG),
    )(qkt, qkt, k_h, v_h, dot, ot, lse)


HALO = 8
CONV_W = 2048 + 2 * SSD_GROUPS * SSD_N


def _shifted(win, off, r):
    return pltpu.roll(win, (r + 2 * HALO - off) % (r + 2 * HALO), 0)[0:r]


def _conv_fwd(proj, w8, brow, *, name):
    s = proj.shape[0]
    cb = 256
    r = _tile(s, 512)

    def body(x_ref, w_ref, b_ref, o_ref, pad_ref):
        zeros = jnp.zeros((HALO, cb), F32)
        pad_ref[0:HALO, :] = zeros
        pad_ref[s + HALO:s + 2 * HALO, :] = zeros

        def fill(i, carry):
            st = pl.multiple_of(i * r, r)
            pad_ref[pl.ds(st + HALO, r), :] = x_ref[pl.ds(st, r), :].astype(F32)
            return carry

        lax.fori_loop(0, s // r, fill, 0)
        wv = w_ref[...]
        bv = b_ref[...]

        def step(i, carry):
            st = pl.multiple_of(i * r, r)
            win = pad_ref[pl.ds(st, r + 2 * HALO), :]
            acc = bv + wv[0:1, :] * _shifted(win, HALO - 2, r)
            for t in range(1, D_CONV):
                acc = acc + wv[t:t + 1, :] * _shifted(win, HALO - 2 + t, r)
            o_ref[pl.ds(st, r), :] = (acc * _sigmoid(acc)).astype(o_ref.dtype)
            return carry

        lax.fori_loop(0, s // r, step, 0)

    return pl.pallas_call(
        body, name=name, grid=(CONV_W // cb,),
        in_specs=[pl.BlockSpec((s, cb), lambda j: (0, XS0 // cb + j)), pl.BlockSpec((8, cb), lambda j: (0, j)),
                  pl.BlockSpec((1, cb), lambda j: (0, j))],
        out_specs=pl.BlockSpec((s, cb), lambda j: (0, j)),
        out_shape=jax.ShapeDtypeStruct((s, CONV_W), MMD),
        scratch_shapes=[pltpu.VMEM((s + 2 * HALO, cb), F32)],
        compiler_params=_cp(("arbitrary",), VMEM_MID),
    )(proj, w8, brow)


def _conv_bwd(proj, col0, ga, gb, w8, brow, *, name):
    s = proj.shape[0]
    width = ga.shape[1]
    cb = 128
    c0 = col0 // cb
    r = _tile(s, 512)

    def body(x_ref, ga_ref, gb_ref, w_ref, b_ref, dx_ref, dw_ref, db_ref, xpad, dpad):
        zeros = jnp.zeros((HALO, cb), F32)
        for ref in (xpad, dpad):
            ref[0:HALO, :] = zeros
            ref[s + HALO:s + 2 * HALO, :] = zeros

        def fill(i, carry):
            st = pl.multiple_of(i * r, r)
            xpad[pl.ds(st + HALO, r), :] = x_ref[pl.ds(st, r), :].astype(F32)
            return carry

        lax.fori_loop(0, s // r, fill, 0)
        wv = w_ref[...]
        bv = b_ref[...]

        def first(i, carry):
            st = pl.multiple_of(i * r, r)
            win = xpad[pl.ds(st, r + 2 * HALO), :]
            taps = [_shifted(win, HALO - 2 + t, r) for t in range(D_CONV)]
            u = bv
            for t in range(D_CONV):
                u = u + wv[t:t + 1, :] * taps[t]
            sg = _sigmoid(u)
            du = (ga_ref[pl.ds(st, r), :] + gb_ref[pl.ds(st, r), :]) * (sg * (1.0 + u * (1.0 - sg)))
            dpad[pl.ds(st + HALO, r), :] = du
            out = [carry[0] + jnp.sum(du, axis=0, keepdims=True)]
            for t in range(D_CONV):
                out.append(carry[1 + t] + jnp.sum(du * taps[t], axis=0, keepdims=True))
            return tuple(out)

        sums = lax.fori_loop(0, s // r, first, tuple(jnp.zeros((1, cb), F32) for _ in range(1 + D_CONV)))
        db_ref[...] = sums[0]
        for t in range(D_CONV):
            dw_ref[t:t + 1, :] = sums[1 + t]
        dw_ref[D_CONV:8, :] = jnp.zeros((8 - D_CONV, cb), F32)

        def second(i, carry):
            st = pl.multiple_of(i * r, r)
            win = dpad[pl.ds(st, r + 2 * HALO), :]
            acc = wv[0:1, :] * _shifted(win, HALO + 2, r)
            for t in range(1, D_CONV):
                acc = acc + wv[t:t + 1, :] * _shifted(win, HALO + 2 - t, r)
            dx_ref[pl.ds(st, r), :] = acc.astype(dx_ref.dtype)
            return carry

        lax.fori_loop(0, s // r, second, 0)

    col = pl.BlockSpec((s, cb), lambda j: (0, j))
    return pl.pallas_call(
        body, name=name, grid=(width // cb,),
        in_specs=[pl.BlockSpec((s, cb), lambda j: (0, XS0 // cb + c0 + j)), col, col,
                  pl.BlockSpec((8, cb), lambda j: (0, c0 + j)), pl.BlockSpec((1, cb), lambda j: (0, c0 + j))],
        out_specs=[col, pl.BlockSpec((8, cb), lambda j: (0, j)), pl.BlockSpec((1, cb), lambda j: (0, j))],
        out_shape=[jax.ShapeDtypeStruct((s, width), MMD), jax.ShapeDtypeStruct((8, width), F32),
                   jax.ShapeDtypeStruct((1, width), F32)],
        scratch_shapes=[pltpu.VMEM((s + 2 * HALO, cb), F32), pltpu.VMEM((s + 2 * HALO, cb), F32)],
        compiler_params=_cp(("arbitrary",), VMEM_BIG),
    )(proj, ga, gb, w8, brow)


def _tri(lower):
    i = jnp.arange(CHUNK)
    return ((i[:, None] >= i[None, :]) if lower else (i[:, None] <= i[None, :])).astype(F32)


def _dt_fwd(raw, bias, arow, *, name):
    s = raw.shape[0]

    def body(r_ref, b_ref, a_ref, lo_ref, up_ref, dt_ref, cs_ref):
        u = r_ref[...] + b_ref[...]
        dt = jnp.maximum(u, 0.0) + jnp.log1p(jnp.exp(-jnp.abs(u)))
        dt_ref[...] = dt
        a = dt * a_ref[...]
        lane = lax.broadcasted_iota(jnp.int32, (CHUNK, 128), 1)
        cs_ref[...] = jnp.where(lane < SSD_HEADS, _dot_hi(lo_ref[...], a), _dot_hi(up_ref[...], a))

    blk = pl.BlockSpec((CHUNK, 128), lambda i: (i, 0))
    row = pl.BlockSpec((1, 128), lambda i: (0, 0))
    tri = pl.BlockSpec((CHUNK, CHUNK), lambda i: (0, 0))
    return pl.pallas_call(
        body, name=name, grid=(s // CHUNK,), in_specs=[blk, row, row, tri, tri], out_specs=[blk, blk],
        out_shape=[jax.ShapeDtypeStruct((s, 128), F32)] * 2, compiler_params=_cp(("arbitrary",)),
    )(raw, bias, arow, _tri(True), _tri(False))


def _dt_bwd(ddt, raw, bias, *, name):
    s = raw.shape[0]
    tm = _tile(s, 1024)

    def body(d_ref, r_ref, b_ref, o_ref, db_ref):
        g = d_ref[...] * _sigmoid(r_ref[...] + b_ref[...])
        o_ref[...] = g.astype(o_ref.dtype)
        _acc_rows(db_ref, jnp.sum(g, axis=0, keepdims=True), pl.program_id(0) == 0)

    blk = pl.BlockSpec((tm, 128), lambda i: (i, 0))
    row = pl.BlockSpec((1, 128), lambda i: (0, 0))
    return pl.pallas_call(
        body, name=name, grid=(s // tm,), in_specs=[blk, blk, row], out_specs=[blk, row],
        out_shape=[jax.ShapeDtypeStruct((s, 128), MMD), jax.ShapeDtypeStruct((1, 128), F32)],
        compiler_params=_cp(("arbitrary",)),
    )(ddt, raw, bias)


GW = HPG * SSD_P


GPS = 2


def _ssd_specs(nc, rev):
    cc = (lambda c: nc - 1 - c) if rev else (lambda c: c)
    nb = SSD_GROUPS // GPS
    return dict(
        x=pl.BlockSpec((CHUNK, GPS * GW), lambda g, c: (cc(c), g)),
        b=pl.BlockSpec((CHUNK, GPS * SSD_N), lambda g, c: (cc(c), 2048 // (GPS * SSD_N) + g)),
        c=pl.BlockSpec((CHUNK, GPS * SSD_N), lambda g, c: (cc(c), 2048 // (GPS * SSD_N) + nb + g)),
        col=pl.BlockSpec((GPS, CHUNK, HPG), lambda g, c: (g, cc(c), 0)),
        lanes=pl.BlockSpec((CHUNK, 128), lambda g, c: (cc(c), 0)),
        rowt=pl.BlockSpec((GPS, 1, HPG, CHUNK), lambda g, c: (g, cc(c), 0, 0)),
        drow=pl.BlockSpec((1, GPS * GW), lambda g, c: (0, g)),
        y=pl.BlockSpec((CHUNK, GPS * GW), lambda g, c: (cc(c), g)),
        h=pl.BlockSpec((GPS, 1, SSD_N, GW), lambda g, c: (g, cc(c), 0, 0)),
        n=pl.BlockSpec((CHUNK, GPS * SSD_N), lambda g, c: (cc(c), g)),
    )


def _ssd_mask(anti):
    ii = lax.broadcasted_iota(jnp.int32, (CHUNK, CHUNK), 0)
    jj = lax.broadcasted_iota(jnp.int32, (CHUNK, CHUNK), 1)
    return ii, jj, (ii <= jj) if anti else (ii >= jj)


def _expand(x, ex):
    h1 = x.astype(jnp.bfloat16)
    r1 = x - h1.astype(F32)
    h2 = r1.astype(jnp.bfloat16)
    h3 = (r1 - h2.astype(F32)).astype(jnp.bfloat16)
    return _dot(h1, ex) + _dot(h2, ex) + _dot(h3, ex)


def _headsum(a, e):
    hi = a.astype(jnp.bfloat16)
    return _dot(hi, e) + _dot((a - hi.astype(F32)).astype(jnp.bfloat16), e)


def _expand_mats():
    lane = jnp.arange(128)[None, :, None]
    col = jnp.arange(GW)[None, None, :]
    base = (jnp.arange(2)[:, None] * SSD_HEADS + jnp.arange(SSD_GROUPS)[None, :] * HPG).reshape(2 * SSD_GROUPS, 1, 1)
    return (lane == base + col // SSD_P).astype(jnp.bfloat16)


def _headsum_mats():
    e1 = (jnp.arange(GW)[:, None] // SSD_P == jnp.arange(128)[None, :]).astype(jnp.bfloat16)
    e2 = (jnp.arange(HPG * CHUNK)[:, None] // CHUNK == jnp.arange(128)[None, :]).astype(jnp.bfloat16)
    return e1, e2


def _ssd_fwd(xc, dt, cs, cst, ex, drow, di, *, name):
    s = xc.shape[0]
    nc = s // CHUNK
    anti = di == 1
    sp = _ssd_specs(nc, anti)
    trow = 0 if anti else CHUNK - 1

    def body(x_ref, b_ref, c_ref, dt_ref, cs_ref, cst_ref, ex_ref, d_ref, y_ref, hp_ref, h_ref):
        @pl.when(pl.program_id(1) == 0)
        def _():
            h_ref[...] = jnp.zeros_like(h_ref)

        mask = _ssd_mask(anti)[2]
        dtv, csv = dt_ref[...], cs_ref[...]
        for gi in range(GPS):
            cols = slice(gi * GW, (gi + 1) * GW)
            ncols = slice(gi * SSD_N, (gi + 1) * SSD_N)
            ex = ex_ref[gi]
            xb = x_ref[:, cols].astype(F32)
            bm, cm = b_ref[:, ncols], c_ref[:, ncols]
            csr = cst_ref[gi, 0]
            dtf = _expand(dtv, ex)
            csf = _expand(csv, ex)
            tl = csf[trow:trow + 1, :]
            h = h_ref[gi]
            hp_ref[gi, 0] = h
            g = _dot(cm, bm, NT)
            xs = xb * dtf
            xsm = xs.astype(MMD)
            base = jnp.exp(csf) * _dot(cm, h.astype(MMD)) + d_ref[:, cols] * xb
            for r in range(HPG):
                sl = slice(r * SSD_P, (r + 1) * SSD_P)
                lm = jnp.exp(jnp.where(mask, csf[:, r * SSD_P:r * SSD_P + 1] - csr[r:r + 1, :], NEG))
                y_ref[:, gi * GW + r * SSD_P:gi * GW + (r + 1) * SSD_P] = _dot((g * lm).astype(MMD), xsm[:, sl]) + base[:, sl]
            xd = (xs * jnp.exp(tl - csf)).astype(MMD)
            h_ref[gi] = h * jnp.exp(tl) + _dot(bm, xd, TN)

    nb = SSD_GROUPS // GPS
    return pl.pallas_call(
        body, name=name, grid=(nb, nc),
        in_specs=[sp["x"], sp["b"], sp["c"], sp["lanes"], sp["lanes"], sp["rowt"],
                  pl.BlockSpec((GPS, 128, GW), lambda g, c: (di * nb + g, 0, 0)), sp["drow"]],
        out_specs=[sp["y"], sp["h"]],
        out_shape=[jax.ShapeDtypeStruct((s, 2048), F32), jax.ShapeDtypeStruct((SSD_GROUPS, nc, SSD_N, GW), F32)],
        scratch_shapes=[pltpu.VMEM((GPS, SSD_N, GW), F32)],
        compiler_params=_cp(("arbitrary", "arbitrary")),
    )(xc, xc, xc, dt, cs, cst, ex, drow)


def _ssd_bwd(xc, dt, cs, dt4, cst, ex, drow, arow4, dy, hprev, di, *, name):
    s = xc.shape[0]
    nc = s // CHUNK
    anti = di == 1
    sp = _ssd_specs(nc, not anti)
    trow = 0 if anti else CHUNK - 1
    e1, e2 = _headsum_mats()

    def body(x_ref, b_ref, c_ref, dt_ref, cs_ref, dt4_ref, cst_ref, ex_ref, d_ref, a_ref, dy_ref, hp_ref, tri_ref,
             e1_ref, e2_ref, dx_ref, db_ref, dc_ref, ddt_ref, da_ref, dh_ref, w_ref, dxs_ref):
        @pl.when(pl.program_id(1) == 0)
        def _():
            dh_ref[...] = jnp.zeros_like(dh_ref)
            da_ref[...] = jnp.zeros_like(da_ref)

        e1v = e1_ref[...]
        ii, _, mask = _ssd_mask(anti)
        dtv, csv = dt_ref[...], cs_ref[...]
        for gi in range(GPS):
            cols = slice(gi * GW, (gi + 1) * GW)
            ncols = slice(gi * SSD_N, (gi + 1) * SSD_N)
            ex = ex_ref[gi]
            xb = x_ref[:, cols].astype(F32)
            bm, cm = b_ref[:, ncols], c_ref[:, ncols]
            csr = cst_ref[gi, 0]
            dyb = dy_ref[:, cols]
            dym = dyb.astype(MMD)
            hp = hp_ref[gi, 0]
            hpm = hp.astype(MMD)
            dh = dh_ref[gi]
            dhm = dh.astype(MMD)
            dtf = _expand(dtv, ex)
            csf = _expand(csv, ex)
            tl = csf[trow:trow + 1, :]
            e = jnp.exp(csf)
            dec = jnp.exp(tl - csf)
            et = jnp.exp(tl)
            xs = xb * dtf
            xsm = xs.astype(MMD)
            g = _dot(cm, bm, NT)
            z = _dot(cm, hpm)
            bdh = _dot(bm, dhm)
            dg = jnp.zeros((CHUNK, CHUNK), F32)
            wcols = jnp.zeros((CHUNK, CHUNK), F32)
            for r in range(HPG):
                sl = slice(r * SSD_P, (r + 1) * SSD_P)
                lm = jnp.exp(jnp.where(mask, csf[:, r * SSD_P:r * SSD_P + 1] - csr[r:r + 1, :], NEG))
                mm = g * lm
                dm = _dot(dym[:, sl], xsm[:, sl], NT)
                w = dm * mm
                w_ref[gi, :, r * CHUNK:(r + 1) * CHUNK] = w
                wcols = jnp.where(ii == r, jnp.sum(w, axis=0, keepdims=True), wcols)
                dg = dg + dm * lm
                dxs_ref[gi, :, sl] = _dot(mm.astype(MMD), dym[:, sl], TN)
            dxs = dxs_ref[gi] + dec * bdh
            dx_ref[:, cols] = dxs * dtf + d_ref[:, cols] * dyb
            tb = xs * bdh * dec
            d_tot = jnp.sum(tb, axis=0, keepdims=True) + et * jnp.sum(dh * hp, axis=0, keepdims=True)
            d_tot = _headsum(jnp.broadcast_to(d_tot, (8, GW)), e1v)[0:1]
            dcs = (_headsum(dyb * (e * z) - tb, e1v) + _headsum(w_ref[gi], e2_ref[...]) - wcols.T
                   + jnp.where(ii == trow, d_tot, 0.0))
            da = _dot_hi(tri_ref[...], dcs)
            ddt_ref[gi] = (da * a_ref[gi] + _headsum(dxs * xb, e1v))[:, 0:HPG]
            da_ref[gi] += jnp.sum(da[:, 0:HPG] * dt4_ref[gi], axis=0, keepdims=True)
            dgm = dg.astype(MMD)
            dz = (e * dyb).astype(MMD)
            dc_ref[:, ncols] = _dot(dgm, bm) + _dot(dz, hpm, NT)
            db_ref[:, ncols] = _dot(dgm, cm, TN) + _dot((xs * dec).astype(MMD), dhm, NT)
            dh_ref[gi] = dh * et + _dot(cm, dz, TN)

    nb = SSD_GROUPS // GPS
    const = lambda shape: pl.BlockSpec(shape, lambda g, c: (0,) * len(shape))
    return pl.pallas_call(
        body, name=name, grid=(nb, nc),
        in_specs=[sp["x"], sp["b"], sp["c"], sp["lanes"], sp["lanes"], sp["col"], sp["rowt"],
                  pl.BlockSpec((GPS, 128, GW), lambda g, c: (di * nb + g, 0, 0)), sp["drow"],
                  pl.BlockSpec((GPS, 1, 128), lambda g, c: (g, 0, 0)), sp["y"], sp["h"],
                  const((CHUNK, CHUNK)), const((GW, 128)), const((HPG * CHUNK, 128))],
        out_specs=[sp["y"], sp["n"], sp["n"], sp["col"], pl.BlockSpec((GPS, 1, HPG), lambda g, c: (g, 0, 0))],
        out_shape=[jax.ShapeDtypeStruct((s, 2048), F32), jax.ShapeDtypeStruct((s, SSD_GROUPS * SSD_N), F32),
                   jax.ShapeDtypeStruct((s, SSD_GROUPS * SSD_N), F32), jax.ShapeDtypeStruct((SSD_GROUPS, s, HPG), F32),
                   jax.ShapeDtypeStruct((SSD_GROUPS, 1, HPG), F32)],
        scratch_shapes=[pltpu.VMEM((GPS, SSD_N, GW), F32), pltpu.VMEM((GPS, CHUNK, HPG * CHUNK), F32),
                        pltpu.VMEM((GPS, CHUNK, GW), F32)],
        compiler_params=_cp(("arbitrary", "arbitrary")),
    )(xc, xc, xc, dt, cs, dt4, cst, ex, drow, arow4, dy, hprev, _tri(anti), e1, e2)


def _gnorm_fwd(ya, yb, proj, w, *, name):
    s = ya.shape[0]
    tm = _tile(s, 256)

    def body(a_ref, b_ref, z_ref, w_ref, o_ref):
        zv = z_ref[...].astype(F32)
        t = (a_ref[...] + b_ref[...]) * (zv * _sigmoid(zv))
        r = lax.rsqrt(jnp.mean(t * t, axis=-1, keepdims=True) + EPS)
        o_ref[...] = ((t * r) * w_ref[...]).astype(o_ref.dtype)

    big = pl.BlockSpec((tm, 2048), lambda i: (i, 0))
    row = pl.BlockSpec((1, 2048), lambda i: (0, 0))
    return pl.pallas_call(
        body, name=name, grid=(s // tm,), in_specs=[big, big, big, row], out_specs=big,
        out_shape=jax.ShapeDtypeStruct((s, 2048), MMD), compiler_params=_cp(("arbitrary",)),
    )(ya, yb, proj, w)


def _gnorm_bwd(dout, ya, yb, proj, w, *, name):
    s = ya.shape[0]
    tm = _tile(s, 256)

    def body(do_ref, a_ref, b_ref, z_ref, w_ref, dy_ref, dz_ref, dw_ref):
        zv = z_ref[...].astype(F32)
        sg = _sigmoid(zv)
        sz = zv * sg
        y = a_ref[...] + b_ref[...]
        t = y * sz
        r = lax.rsqrt(jnp.mean(t * t, axis=-1, keepdims=True) + EPS)
        nv = t * r
        dov = do_ref[...].astype(F32)
        _acc_rows(dw_ref, jnp.sum(dov * nv, axis=0, keepdims=True), pl.program_id(0) == 0)
        dn = dov * w_ref[...]
        dt_ = r * (dn - nv * jnp.mean(dn * nv, axis=-1, keepdims=True))
        dy_ref[...] = dt_ * sz
        dz_ref[...] = (dt_ * y * (sg * (1.0 + zv * (1.0 - sg)))).astype(dz_ref.dtype)

    big = pl.BlockSpec((tm, 2048), lambda i: (i, 0))
    row = pl.BlockSpec((1, 2048), lambda i: (0, 0))
    return pl.pallas_call(
        body, name=name, grid=(s // tm,), in_specs=[big, big, big, big, row], out_specs=[big, big, row],
        out_shape=[jax.ShapeDtypeStruct((s, 2048), F32), jax.ShapeDtypeStruct((s, 2048), MMD),
                   jax.ShapeDtypeStruct((1, 2048), F32)],
        compiler_params=_cp(("arbitrary",)),
    )(dout, ya, yb, proj, w)


def _colsum_prod(a, b, *, name):
    s, n = a.shape
    tm = _tile(s, 256)

    def body(a_ref, b_ref, o_ref):
        _acc_rows(o_ref, jnp.sum(a_ref[...].astype(F32) * b_ref[...].astype(F32), axis=0, keepdims=True),
                  pl.program_id(0) == 0)

    big = pl.BlockSpec((tm, n), lambda i: (i, 0))
    return pl.pallas_call(
        body, name=name, grid=(s // tm,), in_specs=[big, big], out_specs=pl.BlockSpec((1, n), lambda i: (0, 0)),
        out_shape=jax.ShapeDtypeStruct((1, n), F32), compiler_params=_cp(("arbitrary",)),
    )(a, b)


def _heads(a, n):
    return a.reshape(a.shape[0], n, HEAD_DIM).transpose(1, 0, 2)


def _unheads(a):
    return a.transpose(1, 0, 2).reshape(a.shape[1], a.shape[0] * HEAD_DIM)


def _per_group(a):
    return a.reshape(a.shape[0], SSD_GROUPS, HPG).transpose(1, 0, 2)


def _per_group_t(a):
    s = a.shape[0]
    return a.reshape(s // CHUNK, CHUNK, SSD_GROUPS, HPG).transpose(2, 0, 3, 1)


def _local_step(x, target, mod, wts, small):
    s, d = x.shape
    shift1, scale1, gate1, shift2, scale2, gate2 = [mod[i:i + 1] for i in range(6)]

    h1 = _ln_mod(x, small["norm1_w"], scale1, shift1, name="ln1")
    proj = _mm(h1, wts["w_in_p"], name="in_proj", outs=[MMD], tm=512, tn=2944, b_outer=True)
    dt_raw = _mm(h1, wts["w_dt"], name="dt_proj", outs=[F32], tm=512, tn=128)

    qk_w = jnp.concatenate([jnp.tile(small["q_norm_w"], (1, N_Q_HEADS)), jnp.tile(small["k_norm_w"], (1, N_KV_HEADS))], axis=1)
    qk_sc = jnp.concatenate([jnp.full((1, N_Q_HEADS * HEAD_DIM), HEAD_DIM ** -0.5, F32),
                             jnp.ones((1, N_KV_HEADS * HEAD_DIM), F32)], axis=1)
    tabs = _rope_tables(s)
    qk, qkt = _qk_fwd(proj, qk_w, qk_sc, tabs, name="qk_fwd")
    qkt = qkt.reshape(N_Q_HEADS + N_KV_HEADS, HEAD_DIM, s)
    k_h = _heads(qk[:, N_Q_HEADS * HEAD_DIM:], N_KV_HEADS)
    v_sd = proj[:, V0:V0 + N_KV_HEADS * HEAD_DIM]
    v_h = _heads(v_sd, N_KV_HEADS)
    vta = jnp.concatenate([v_sd.T.reshape(N_KV_HEADS, HEAD_DIM, s), jnp.ones((N_KV_HEADS, V_AUG - HEAD_DIM, s), MMD)], axis=1)
    ot, lse = _flash_fwd(qkt, vta, name="flash_fwd")
    ot2 = ot.reshape(N_Q_HEADS * HEAD_DIM, s)

    w8 = jnp.pad(small["conv_w"], ((0, 8 - D_CONV), (0, 0)))
    xc = _conv_fwd(proj, w8, small["conv_b"], name="conv_fwd")
    a_neg = -jnp.exp(small["A_log"])
    arow = jnp.pad(a_neg.reshape(1, 2 * SSD_HEADS), ((0, 0), (0, 128 - 2 * SSD_HEADS)))
    bias_row = jnp.pad(small["dt_bias"].reshape(1, 2 * SSD_HEADS), ((0, 0), (0, 128 - 2 * SSD_HEADS)))
    dt, cs = _dt_fwd(dt_raw, bias_row, arow, name="dt_fwd")
    drow = jnp.repeat(small["ssd_D"], SSD_P, axis=1)
    dirs = []
    for di in range(2):
        cols = slice(di * SSD_HEADS, (di + 1) * SSD_HEADS)
        dirs.append(dict(
            dt4=_per_group(dt[:, cols]), cst=_per_group_t(cs[:, cols]),
            drow=drow if di == 0 else jnp.zeros_like(drow),
            arow4=jnp.pad(a_neg[di].reshape(SSD_GROUPS, 1, HPG), ((0, 0), (0, 0), (0, 128 - HPG)))))
    ex = _expand_mats()
    ys = []
    for di, dd in enumerate(dirs):
        y, dd["hprev"] = _ssd_fwd(xc, dt, cs, dd["cst"], ex, dd["drow"], di, name=f"ssd_fwd{di}")
        ys.append(y)
    ssdn = _gnorm_fwd(ys[0], ys[1], proj, small["ssd_norm_w"], name="gnorm_fwd")

    a_o = _mm(ot2, wts["w_attn_out"], name="attn_out", outs=[MMD], ta=True, tm=512, tn=1024)

    def merge_epi(acc, ao, ga, gs):
        return (_sigmoid(ga.astype(F32)) * ao.astype(F32) + _sigmoid(gs.astype(F32)) * acc, acc)

    merged, b_o = _mm(ssdn, wts["w_ssd_out"], name="ssd_out", outs=[MMD, MMD], tm=512, tn=512,
                      extras=[(a_o, "tile", 0), (proj, "tile", GA0), (proj, "tile", GS0)], epi=merge_epi)

    def res_epi(acc, res, gate):
        return (res + gate * acc, acc)

    x1, mo = _mm(merged, wts["w_o"], name="w_o", outs=[F32, MMD], tm=512, tn=512,
                 extras=[(x, "tile", 0), (gate1, "row", 0)], epi=res_epi)
    h2 = _ln_mod(x1, small["norm2_w"], scale2, shift2, name="ln2")

    def relu2_epi(acc):
        rl = jnp.maximum(acc, 0.0)
        return (rl * rl, rl)

    act, rl = _mm(h2, wts["w_mlp1"], name="mlp1", outs=[MMD, MMD], tm=512, tn=1024, epi=relu2_epi, b_outer=True)

    def loss_epi(acc, res, gate, tgt):
        return ((res + gate * acc - tgt) * (1.0 / d), acc)

    dy, ffo = _mm(act, wts["w_mlp2"], name="mlp2", outs=[F32, MMD], tm=512, tn=1024, vmem=VMEM_BIG,
                  extras=[(x1, "tile", 0), (gate2, "row", 0), (target, "tile", 0)], epi=loss_epi)
    loss = _sumsq(dy, name="loss") * (0.5 * d)

    gw = {}
    gs_ = {}
    dffo, dgate2 = _gate_bwd(dy, ffo, gate2, name="gate2_bwd")
    dpre = _mm(dffo, wts["w_mlp2"], name="mlp2_dx", outs=[MMD], nt=True, tm=512, tn=1024,
               extras=[(rl, "tile", 0)], epi=lambda acc, r: (acc * (2.0 * r.astype(F32)),))
    gw["w_mlp2"] = _mm_tn(act, dffo, name="mlp2_dw")
    dh2 = _mm(dpre, wts["w_mlp1"], name="mlp1_dx", outs=[F32], nt=True, tm=512, tn=1024)
    gw["w_mlp1"] = _mm_tn(h2, dpre, name="mlp1_dw")
    dx1, dshift2, dscale2, gs_["norm2_w"] = _ln_mod_bwd(dh2, x1, small["norm2_w"], scale2, dy, name="ln2_bwd")
    dmo, dgate1 = _gate_bwd(dx1, mo, gate1, name="gate1_bwd")

    def merge_bwd_epi(acc, ao, bo, ga, gs):
        sa, ss = _sigmoid(ga.astype(F32)), _sigmoid(gs.astype(F32))
        return (acc * sa, acc * ss, acc * ao.astype(F32) * sa * (1.0 - sa), acc * bo.astype(F32) * ss * (1.0 - ss))

    da_o, db_o, dga, dgs = _mm(dmo, wts["w_o"], name="w_o_dx", outs=[MMD] * 4, nt=True, tm=512, tn=512,
                               extras=[(a_o, "tile", 0), (b_o, "tile", 0), (proj, "tile", GA0), (proj, "tile", GS0)],
                               epi=merge_bwd_epi)
    gw["w_o"] = _mm_tn(merged, dmo, name="w_o_dw")
    dot = _mm(wts["w_attn_out"], da_o, name="attn_out_dx", outs=[MMD], nt=True, tm=512, tn=512)
    gw["w_attn_out"] = _mm(ot2, da_o, name="attn_out_dw", outs=[F32], tm=256, tn=512, vmem=VMEM_BIG)
    dssdn = _mm(db_o, wts["w_ssd_out"], name="ssd_out_dx", outs=[MMD], nt=True, tm=512, tn=512)
    gw["w_ssd_out"] = _mm_tn(ssdn, db_o, name="ssd_out_dw")

    dyssd, dz, gs_["ssd_norm_w"] = _gnorm_bwd(dssdn, ys[0], ys[1], proj, small["ssd_norm_w"], name="gnorm_bwd")
    gs_["ssd_D"] = _colsum_prod(dyssd, xc[:, 0:2048], name="ssd_d_grad").reshape(SSD_HEADS, SSD_P).sum(axis=1).reshape(1, SSD_HEADS)
    dxc, ddts, das = [], [], []
    for di, dd in enumerate(dirs):
        dxs, dbm, dcm, ddt4, da4 = _ssd_bwd(xc, dt, cs, dd["dt4"], dd["cst"], ex, dd["drow"], dd["arow4"],
                                            dyssd, dd["hprev"], di, name=f"ssd_bwd{di}")
        dxc.append((dxs, dbm, dcm))
        ddts.append(ddt4.transpose(1, 0, 2).reshape(s, SSD_HEADS))
        das.append(da4.reshape(1, SSD_HEADS))
    conv_parts, col0 = [], 0
    for part, (ga, gb) in enumerate(zip(*dxc)):
        conv_parts.append(_conv_bwd(proj, col0, ga, gb, w8, small["conv_b"], name=f"conv_bwd{part}"))
        col0 += ga.shape[1]
    dxbc, dw8, gs_["conv_b"] = [jnp.concatenate(t, axis=1) for t in zip(*conv_parts)]
    gs_["conv_w"] = dw8[0:D_CONV]
    gs_["A_log"] = jnp.concatenate(das, axis=0) * a_neg
    ddt = jnp.pad(jnp.concatenate(ddts, axis=1), ((0, 0), (0, 128 - 2 * SSD_HEADS)))
    ddt_raw, dbias = _dt_bwd(ddt, dt_raw, bias_row, name="dt_bwd")
    gs_["dt_bias"] = dbias[:, 0:2 * SSD_HEADS].reshape(2, SSD_HEADS)

    dqt, dk_h, dv_h = _flash_bwd(qkt, k_h, v_h, dot.reshape(N_Q_HEADS, HEAD_DIM, s), ot, lse, name="flash_bwd")
    dqkt = jnp.concatenate([dqt.reshape(N_Q_HEADS * HEAD_DIM, s),
                            dk_h.transpose(0, 2, 1).reshape(N_KV_HEADS * HEAD_DIM, s)], axis=0)
    dqk_u, dqk_w = _qk_bwd(dqkt, proj, qk_w, qk_sc, tabs, name="qk_bwd")
    gs_["q_norm_w"] = dqk_w[:, 0:N_Q_HEADS * HEAD_DIM].reshape(N_Q_HEADS, HEAD_DIM).sum(axis=0, keepdims=True)
    gs_["k_norm_w"] = dqk_w[:, N_Q_HEADS * HEAD_DIM:].reshape(N_KV_HEADS, HEAD_DIM).sum(axis=0, keepdims=True)
    dv = _unheads(dv_h).astype(MMD)

    dproj = jnp.concatenate([dz, dga, dgs, dxbc, dqk_u, dv, ddt_raw], axis=1)
    dh1 = _mm(dproj, wts["w_in_p"], name="in_proj_dx", outs=[F32], nt=True, tm=256, tn=1024, vmem=VMEM_BIG)
    gw["w_in_p"] = _mm_tn(h1, dproj, name="in_proj_dw", tk=1024, tn=2944, tmm=1024, vmem=VMEM_BIG)
    grad_x, dshift1, dscale1, gs_["norm1_w"] = _ln_mod_bwd(dh1, x, small["norm1_w"], scale1, dx1, name="ln1_bwd")
    dmod = jnp.concatenate([dshift1, dscale1, dgate1, dshift2, dscale2, dgate2], axis=0)
    return loss, grad_x, dmod, gw, gs_


N_DEV = 8
N_CHIP = 4
ANY = pl.BlockSpec(memory_space=pl.ANY)


def _place():
    return lax.axis_index("x"), lax.axis_index("y"), lax.axis_index("c")


def _allgather8(v, *, name):
    m_per, n = v.shape

    def body(x_ref, out_ref, send_sems, recv_sems, local_sem):
        x, y, c = _place()
        me, sibling = (x, y, c), (x, y, 1 - c)
        chips = [(1 - x, y), (x, 1 - y), (1 - x, 1 - y)]

        def rows(px, py, pc):
            return out_ref.at[pl.ds((4 * px + 2 * py + pc) * m_per, m_per), :]

        def copy(k, block, to, src=None):
            return pltpu.make_async_remote_copy(
                src_ref=rows(*block) if src is None else src, dst_ref=rows(*block),
                send_sem=send_sems.at[k], recv_sem=recv_sems.at[k], device_id=to, device_id_type=MESH)

        mine = pltpu.make_async_copy(x_ref, rows(*me), local_sem)
        mine.start()
        first = [copy(0, me, sibling, src=x_ref)]
        first += [copy(1 + j, me, (*chip, c), src=x_ref) for j, chip in enumerate(chips)]
        for cp in first:
            cp.start()
        passed = [copy(4 + j, (*chip, c), sibling) for j, chip in enumerate(chips)]
        for j, chip in enumerate(chips):
            copy(1 + j, (*chip, c), me).wait_recv()
            passed[j].start()
        copy(0, sibling, me).wait_recv()
        for j, chip in enumerate(chips):
            copy(4 + j, (*chip, 1 - c), me).wait_recv()
        for cp in first + passed:
            cp.wait_send()
        mine.wait()

    return pl.pallas_call(
        body, name=name, out_shape=jax.ShapeDtypeStruct((N_DEV * m_per, n), v.dtype),
        in_specs=[pl.BlockSpec(memory_space=pltpu.VMEM)], out_specs=pl.BlockSpec(memory_space=pltpu.VMEM),
        scratch_shapes=[pltpu.SemaphoreType.DMA((7,)), pltpu.SemaphoreType.DMA((7,)), pltpu.SemaphoreType.DMA],
    )(v)


def _scatter_chips(src, *, name):
    def body(x_ref, out_ref, send_sems, recv_sems):
        x, y, c = _place()
        k = 2 * x + y
        chips = [(1 - x, y), (x, 1 - y), (1 - x, 1 - y)]
        ids = [2 * cx + cy for cx, cy in chips]

        def copy(j, slot):
            return pltpu.make_async_remote_copy(
                src_ref=x_ref.at[ids[j]], dst_ref=out_ref.at[slot], send_sem=send_sems.at[j], recv_sem=recv_sems.at[j],
                device_id=(*chips[j], c), device_id_type=MESH)

        sends = [copy(j, k) for j in range(3)]
        for cp in sends:
            cp.start()
        for j in range(3):
            copy(j, ids[j]).wait_recv()
        for cp in sends:
            cp.wait_send()

    return pl.pallas_call(
        body, name=name, out_shape=jax.ShapeDtypeStruct(src.shape, src.dtype), in_specs=[ANY], out_specs=ANY,
        scratch_shapes=[pltpu.SemaphoreType.DMA((3,)), pltpu.SemaphoreType.DMA((3,))],
    )(src)


def _row_tile(r, pref=512):
    return max(t for t in range(16, pref + 1, 16) if r % t == 0)


def _gather_weights(src, *, name):
    r = src.shape[0]
    hr = r // 2
    assert r == 2 * hr and hr % 16 == 0

    def body(x_ref, out_ref, send_sems, recv_sems):
        x, y, c = _place()
        k = 2 * x + y
        chips = [(1 - x, y), (x, 1 - y), (1 - x, 1 - y)]
        ids = [2 * cx + cy for cx, cy in chips]
        mine_rows = pl.ds(pl.multiple_of(c * hr, 16), hr)
        other_rows = pl.ds(pl.multiple_of((1 - c) * hr, 16), hr)

        def copy(sem, src_ref, slot, rows, to):
            return pltpu.make_async_remote_copy(
                src_ref=src_ref, dst_ref=out_ref.at[slot, rows], send_sem=send_sems.at[sem], recv_sem=recv_sems.at[sem],
                device_id=to, device_id_type=MESH)

        sends = [copy(j, x_ref.at[mine_rows], k, mine_rows, (cx, cy, c)) for j, (cx, cy) in enumerate(chips)]
        for cp in sends:
            cp.start()
        passed = [copy(3 + j, out_ref.at[ids[j], mine_rows], ids[j], mine_rows, (x, y, 1 - c)) for j in range(3)]
        for j, (cx, cy) in enumerate(chips):
            copy(j, x_ref.at[mine_rows], ids[j], mine_rows, (cx, cy, c)).wait_recv()
            passed[j].start()
        for j in range(3):
            copy(3 + j, out_ref.at[ids[j], other_rows], ids[j], other_rows, (x, y, 1 - c)).wait_recv()
        for cp in sends + passed:
            cp.wait_send()

    return pl.pallas_call(
        body, name=name, out_shape=jax.ShapeDtypeStruct((N_CHIP,) + tuple(src.shape), src.dtype),
        in_specs=[ANY], out_specs=ANY,
        scratch_shapes=[pltpu.SemaphoreType.DMA((6,)), pltpu.SemaphoreType.DMA((6,))],
    )(src)


def _pair_swap(a, *, name):
    n, r, cols = a.shape
    hr = r // 2

    def body(x_ref, out_ref, send_sem, recv_sem):
        x, y, c = _place()
        other_rows = pl.ds(pl.multiple_of((1 - c) * hr, 16), hr)
        cp = pltpu.make_async_remote_copy(src_ref=x_ref.at[:, other_rows], dst_ref=out_ref, send_sem=send_sem,
                                          recv_sem=recv_sem, device_id=(x, y, 1 - c), device_id_type=MESH)
        cp.start()
        cp.wait()

    return pl.pallas_call(
        body, name=name, out_shape=jax.ShapeDtypeStruct((n, hr, cols), a.dtype), in_specs=[ANY], out_specs=ANY,
        scratch_shapes=[pltpu.SemaphoreType.DMA, pltpu.SemaphoreType.DMA],
    )(a)


def _sibling_copy(a, *, name):
    def body(x_ref, out_ref, send_sem, recv_sem):
        x, y, c = _place()
        cp = pltpu.make_async_remote_copy(src_ref=x_ref, dst_ref=out_ref, send_sem=send_sem, recv_sem=recv_sem,
                                          device_id=(x, y, 1 - c), device_id_type=MESH)
        cp.start()
        cp.wait()

    return pl.pallas_call(
        body, name=name, out_shape=jax.ShapeDtypeStruct(a.shape, a.dtype), in_specs=[ANY], out_specs=ANY,
        scratch_shapes=[pltpu.SemaphoreType.DMA, pltpu.SemaphoreType.DMA],
    )(a)


def _sum_slots(a, *, name):
    _, r, c = a.shape
    tr = _row_tile(r)

    def body(a_ref, o_ref):
        acc = a_ref[0].astype(F32)
        for j in range(1, N_CHIP):
            acc = acc + a_ref[j].astype(F32)
        o_ref[...] = acc

    return pl.pallas_call(
        body, name=name, grid=(r // tr,), in_specs=[pl.BlockSpec((N_CHIP, tr, c), lambda i: (0, i, 0))],
        out_specs=pl.BlockSpec((tr, c), lambda i: (i, 0)), out_shape=jax.ShapeDtypeStruct((r, c), F32),
        compiler_params=_cp(("arbitrary",)),
    )(a)


def _add2(a, b, *, name):
    r, c = a.shape
    tr = _row_tile(r)

    def body(a_ref, b_ref, o_ref):
        o_ref[...] = (a_ref[...].astype(F32) + b_ref[...].astype(F32)).astype(o_ref.dtype)

    spec = pl.BlockSpec((tr, c), lambda i: (i, 0))
    return pl.pallas_call(
        body, name=name, grid=(r // tr,), in_specs=[spec, spec], out_specs=spec,
        out_shape=jax.ShapeDtypeStruct((r, c), a.dtype), compiler_params=_cp(("arbitrary",)),
    )(a, b)


BIG = ("w_in", "w_mlp1", "w_attn_out", "w_ssd_out", "w_o", "w_mlp2")
COL_SHARDED = ("w_in", "w_mlp1")
SMALL = ("b_ada", "norm1_w", "norm2_w", "q_norm_w", "k_norm_w", "conv_b", "A_log", "dt_bias", "ssd_D", "ssd_norm_w")
NAMES = ("w_ada", "b_ada", "norm1_w", "norm2_w", "w_in", "q_norm_w", "k_norm_w", "conv_w", "conv_b", "A_log", "dt_bias",
         "ssd_D", "ssd_norm_w", "w_attn_out", "w_ssd_out", "w_o", "w_mlp1", "w_mlp2")
W_IN_COLS = 8768


def _permute_in(w):
    return jnp.concatenate([w[:, 4608:6656], w[:, 6720:8768], w[:, 1536:4608], w[:, 0:1536], w[:, 6656:6720],
                            jnp.zeros((w.shape[0], PW - W_IN_COLS), w.dtype)], axis=1)


def _unpermute_in(wp):
    return jnp.concatenate([wp[:, Q0:DT0], wp[:, XS0:Q0], wp[:, Z0:GA0], wp[:, DT0:DT0 + 64], wp[:, GA0:XS0]], axis=1)


def _pad_to(v, n):
    return jnp.pad(v, (0, n - v.shape[0]))


def _step(w, m, v, loss_target):
    xi, yi, ci = _place()
    chip = 2 * xi + yi
    dev = 4 * xi + 2 * yi + ci
    x, tgt = w["x"], loss_target
    d = x.shape[1]

    cw = w["conv_w"].shape[1]
    v0 = _pad_to(jnp.concatenate([w["c"].reshape(-1), w["conv_w"].reshape(-1)]), 5120).reshape(8, 640)
    g0 = _allgather8(v0, name="ag_cond").reshape(N_DEV, 5120)
    c_all = g0[:, 0:d]
    conv_w = jnp.concatenate([g0[2 * k, d:d + D_CONV * cw].reshape(D_CONV, cw) for k in range(N_CHIP)], axis=1)
    sc = _silu_cast(c_all, name="silu_c")
    modp = _mm(sc, w["w_ada"].astype(MMD), name="ada_fwd", outs=[F32], tm=8, tn=512)
    g1 = _allgather8(modp, name="ag_mod").reshape(N_DEV, N_DEV, modp.shape[1])
    mod_all = jnp.concatenate([g1[2 * k] for k in range(N_CHIP)], axis=1)
    mod = (lax.dynamic_slice_in_dim(mod_all, dev, 1, axis=0) + w["b_ada"]).reshape(6, d)

    packed = [w[n].astype(MMD).reshape(-1, d) for n in BIG]
    n_rows = sum(p.shape[0] for p in packed)
    pad_rows = -n_rows % 32
    packed = jnp.concatenate(packed + [jnp.zeros((pad_rows, d), MMD)], axis=0)
    gath = lax.dynamic_update_slice_in_dim(_gather_weights(packed, name="ag_weights"), packed[None], chip, axis=0)
    full, r0 = {}, 0
    for n in BIG:
        rows = w[n].size // d
        part = gath[:, r0:r0 + rows]
        if n in COL_SHARDED:
            full[n] = jnp.concatenate([part[k].reshape(w[n].shape) for k in range(N_CHIP)], axis=1)
        else:
            full[n] = part.reshape(N_CHIP * w[n].shape[0], w[n].shape[1])
        r0 += rows
    wts = {n: full[n] for n in BIG if n != "w_in"}
    wts["w_in_p"] = _permute_in(full["w_in"])
    wts["w_dt"] = jnp.pad(full["w_in"][:, 6656:6720], ((0, 0), (0, 64)))
    small = {n: w[n] for n in SMALL if n != "b_ada"}
    small["conv_w"] = conv_w

    loss, grad_x, dmod, gw, gs = _local_step(x, tgt, mod, wts, small)

    gw["w_in"] = _unpermute_in(gw.pop("w_in_p"))
    slots = []
    for k in range(N_CHIP):
        parts = []
        for n in BIG:
            r_, c_ = w[n].shape
            blk = gw[n][:, k * c_:(k + 1) * c_] if n in COL_SHARDED else gw[n][k * r_:(k + 1) * r_]
            parts.append(blk.astype(MMD).reshape(-1, d))
        slots.append(jnp.concatenate(parts + [jnp.zeros((pad_rows, d), MMD)], axis=0))
    slots = jnp.stack(slots)
    hr = (n_rows + pad_rows) // 2
    theirs = _pair_swap(slots, name="rs_pair")
    ours = lax.dynamic_slice_in_dim(slots, ci * hr, hr, axis=1)
    pair = _add2(ours.reshape(N_CHIP * hr, d), theirs.reshape(N_CHIP * hr, d), name="rs_pair_sum")
    pair = pair.reshape(N_CHIP, hr, d)
    recv = _scatter_chips(pair, name="rs_grads")
    recv = lax.dynamic_update_slice_in_dim(recv, lax.dynamic_slice_in_dim(pair, chip, 1, axis=0), chip, axis=0)
    half = _sum_slots(recv, name="rs_sum")
    other = _sibling_copy(half, name="rs_sibling")
    total = jnp.where(ci == 0, jnp.concatenate([half, other], axis=0), jnp.concatenate([other, half], axis=0))
    grads, r0 = {}, 0
    for n in BIG:
        rows = w[n].size // d
        grads[n] = total[r0:r0 + rows].reshape(w[n].shape)
        r0 += rows

    order = ([dmod.reshape(-1)] + [gs[n].reshape(-1) for n in SMALL if n != "b_ada"] + [gs["conv_w"].reshape(-1)]
             + [loss.reshape(-1)])
    vec = jnp.concatenate(order)
    n_small = vec.shape[0]
    n_pad = -(-n_small // 1024) * 1024
    g2 = _allgather8(_pad_to(vec, n_pad).reshape(8, n_pad // 8), name="ag_small")
    tot = _rows_sum(g2, N_DEV, name="small_sum").reshape(-1)
    loss = tot[n_small - 1]
    dmod_all = g2.reshape(N_DEV, n_pad)[:, 0:6 * d]
    off = 0
    for n in SMALL:
        grads[n] = tot[off:off + w[n].size].reshape(w[n].shape)
        off += w[n].size
    conv_full = tot[off:off + D_CONV * N_CHIP * cw].reshape(D_CONV, N_CHIP * cw)
    grads["conv_w"] = lax.dynamic_slice_in_dim(conv_full, chip * cw, cw, axis=1)
    ada_cols = w["w_ada"].shape[1]
    dmod_mine = lax.dynamic_slice_in_dim(dmod_all, chip * ada_cols, ada_cols, axis=1).astype(MMD)
    grads["w_ada"] = _mm_tn(sc, dmod_mine, name="ada_dw", tk=512, tn=512, tmm=8)

    delta, new_m, new_v = {}, {}, {}
    pack = lambda t: jnp.concatenate([t[n].reshape(-1) for n in SMALL]).reshape(1, -1)
    ds_, ms_, vs_ = _adamw(pack(w), pack(grads), pack(m), pack(v), name="adamw_small")
    off = 0
    for n in SMALL:
        for dst, src in ((delta, ds_), (new_m, ms_), (new_v, vs_)):
            dst[n] = src[0, off:off + w[n].size].reshape(w[n].shape)
        off += w[n].size
    for n in ("w_ada", "conv_w") + BIG:
        delta[n], new_m[n], new_v[n] = _adamw(w[n], grads[n], m[n], v[n], name="adamw_" + n)
    return loss, grad_x, grads, delta, new_m, new_v


def kernel(x, c, w_ada, b_ada, norm1_w, norm2_w, w_in, q_norm_w, k_norm_w, conv_w, conv_b, A_log, dt_bias, ssd_D, ssd_norm_w, w_attn_out, w_ssd_out, w_o, w_mlp1, w_mlp2, loss_target, m_w_ada, m_b_ada, m_norm1_w, m_norm2_w, m_w_in, m_q_norm_w, m_k_norm_w, m_conv_w, m_conv_b, m_A_log, m_dt_bias, m_ssd_D, m_ssd_norm_w, m_w_attn_out, m_w_ssd_out, m_w_o, m_w_mlp1, m_w_mlp2, v_w_ada, v_b_ada, v_norm1_w, v_norm2_w, v_w_in, v_q_norm_w, v_k_norm_w, v_conv_w, v_conv_b, v_A_log, v_dt_bias, v_ssd_D, v_ssd_norm_w, v_w_attn_out, v_w_ssd_out, v_w_o, v_w_mlp1, v_w_mlp2):
    args = dict(locals())
    strip = lambda a: a[0] if a.ndim == 3 else a
    w = {n: strip(args[n]) for n in NAMES + ("x", "c")}
    m = {n: strip(args["m_" + n]) for n in NAMES}
    v = {n: strip(args["v_" + n]) for n in NAMES}
    loss, grad_x, grads, delta, new_m, new_v = _step(w, m, v, loss_target[0])
    like = lambda t, n: t.reshape(args[n].shape)
    return (loss, grad_x[None], *[like(grads[n], n) for n in NAMES], *[like(delta[n], n) for n in NAMES],
            *[like(new_m[n], n) for n in NAMES], *[like(new_v[n], n) for n in NAMES])
```

```python
import functools
import math

import jax
import jax.numpy as jnp
from jax import lax
from jax.experimental import pallas as pl
from jax.experimental.pallas import tpu as pltpu

F32 = jnp.float32
MMD = jnp.bfloat16
EPS = 1e-6
NEG = -1e30
MIB = 1024 * 1024
VMEM_BIG = 56 * MIB
VMEM_MID = 40 * MIB

GRID_W = 64
N_Q_HEADS, N_KV_HEADS, HEAD_DIM = 16, 4, 64
ROPE_THETA = 10000.0
SSD_HEADS, SSD_GROUPS, SSD_P, SSD_N, CHUNK = 32, 4, 64, 128, 128
HPG = SSD_HEADS // SSD_GROUPS
D_CONV = 5
ADAM_LR, ADAM_B1, ADAM_B2, ADAM_EPS, ADAM_WD, ADAM_STEP = 0.001, 0.9, 0.999, 1e-08, 0.01, 10

Z0, GA0, GS0, XS0, B0, C0, Q0, K0, V0, DT0, PW = 0, 2048, 3072, 4096, 6144, 6656, 7168, 8192, 8448, 8704, 8832

MESH = pl.DeviceIdType.MESH
NT = (((1,), (1,)), ((), ()))
TN = (((0,), (0,)), ((), ()))


def _cp(sem=None, vmem=VMEM_MID):
    return pltpu.CompilerParams(dimension_semantics=sem, vmem_limit_bytes=vmem)


def _tile(n, pref):
    t = min(n, pref)
    while n % t:
        t //= 2
    return t


def _dot(a, b, dims=None):
    if dims is None:
        return jnp.dot(a, b, preferred_element_type=F32)
    return lax.dot_general(a, b, dims, preferred_element_type=F32)


def _dot_hi(a, b):
    return jnp.dot(a, b, precision=lax.Precision.HIGHEST, preferred_element_type=F32)


def _sigmoid(x):
    return jax.nn.sigmoid(x)


def _mm(a, b, *, name, outs, nt=False, ta=False, extras=(), epi=None, tm=512, tn=512, n=None, b_outer=False,
        vmem=VMEM_MID):
    assert not (nt and ta)
    k, m = a.shape if ta else a.shape[::-1]
    if n is None:
        n = b.shape[0] if nt else b.shape[1]
    tm, tn = _tile(m, tm), _tile(n, tn)
    gi, gj = m // tm, n // tn
    if b_outer:
        grid = (gj, gi)
        ij = lambda p, q: (q, p)
    else:
        grid = (gi, gj)
        ij = lambda p, q: (p, q)
    if ta:
        a_spec = pl.BlockSpec((k, tm), lambda p, q: (0, ij(p, q)[0]))
    else:
        a_spec = pl.BlockSpec((tm, k), lambda p, q: (ij(p, q)[0], 0))
    if nt:
        b_spec = pl.BlockSpec((tn, k), lambda p, q: (ij(p, q)[1], 0))
    else:
        b_spec = pl.BlockSpec((k, tn), lambda p, q: (0, ij(p, q)[1]))
    e_specs = []
    for arr, kind, off in extras:
        ob = off // tn
        assert off % tn == 0
        if kind == "tile":
            e_specs.append(pl.BlockSpec((tm, tn), lambda p, q, ob=ob: (ij(p, q)[0], ob + ij(p, q)[1])))
        else:
            e_specs.append(pl.BlockSpec((1, tn), lambda p, q, ob=ob: (0, ob + ij(p, q)[1])))
    ne = len(extras)

    def body(a_ref, b_ref, *rest):
        acc = _dot(a_ref[...], b_ref[...], NT if nt else (TN if ta else None))
        res = epi(acc, *[e[...] for e in rest[:ne]]) if epi is not None else (acc,)
        for o_ref, r in zip(rest[ne:], res):
            o_ref[...] = r.astype(o_ref.dtype)

    out = pl.pallas_call(
        body, name=name, grid=grid,
        in_specs=[a_spec, b_spec] + e_specs,
        out_specs=[pl.BlockSpec((tm, tn), lambda p, q: ij(p, q)) for _ in outs],
        out_shape=[jax.ShapeDtypeStruct((m, n), dt) for dt in outs],
        compiler_params=_cp(("arbitrary", "arbitrary"), vmem),
    )(a, b, *[e[0] for e in extras])
    return out if len(outs) > 1 else out[0]


def _mm_tn(a, g, *, name, tk=512, tn=1024, tmm=4096, vmem=VMEM_MID):
    m, k = a.shape
    n = g.shape[1]
    tk, tn, tmm = _tile(k, tk), _tile(n, tn), _tile(m, tmm)

    def body(a_ref, g_ref, o_ref):
        p = _dot(a_ref[...], g_ref[...], TN)

        @pl.when(pl.program_id(2) == 0)
        def _():
            o_ref[...] = p

        @pl.when(pl.program_id(2) > 0)
        def _():
            o_ref[...] += p

    return pl.pallas_call(
        body, name=name, grid=(k // tk, n // tn, m // tmm),
        in_specs=[pl.BlockSpec((tmm, tk), lambda i, j, r: (r, i)), pl.BlockSpec((tmm, tn), lambda i, j, r: (r, j))],
        out_specs=pl.BlockSpec((tk, tn), lambda i, j, r: (i, j)),
        out_shape=jax.ShapeDtypeStruct((k, n), F32),
        compiler_params=_cp(("arbitrary", "arbitrary", "arbitrary"), vmem),
    )(a, g)


def _adamw(w, g, m, v, *, name):
    r, c = w.shape
    tr = _tile(r, 256) if r % 8 == 0 else r

    def body(w_ref, g_ref, m_ref, v_ref, d_ref, nm_ref, nv_ref):
        gg = g_ref[...]
        nm = ADAM_B1 * m_ref[...] + (1.0 - ADAM_B1) * gg
        nv = ADAM_B2 * v_ref[...] + (1.0 - ADAM_B2) * jnp.square(gg)
        m_hat = nm / (1.0 - ADAM_B1 ** ADAM_STEP)
        v_hat = nv / (1.0 - ADAM_B2 ** ADAM_STEP)
        d_ref[...] = -ADAM_LR * (m_hat / (jnp.sqrt(v_hat) + ADAM_EPS) + ADAM_WD * w_ref[...])
        nm_ref[...] = nm
        nv_ref[...] = nv

    spec = pl.BlockSpec((tr, c), lambda i: (i, 0))
    return pl.pallas_call(
        body, name=name, grid=(r // tr,), in_specs=[spec] * 4, out_specs=[spec] * 3,
        out_shape=[jax.ShapeDtypeStruct((r, c), F32)] * 3, compiler_params=_cp(("arbitrary",)),
    )(w, g, m, v)


def _rows_sum(a, groups, *, name):
    r = a.shape[0] // groups

    def body(a_ref, o_ref):
        acc = a_ref[0:r, :]
        for d in range(1, groups):
            acc = acc + a_ref[d * r:(d + 1) * r, :]
        o_ref[...] = acc

    return pl.pallas_call(body, name=name, out_shape=jax.ShapeDtypeStruct((r, a.shape[1]), F32))(a)


def _silu_cast(a, *, name):
    def body(a_ref, o_ref):
        x = a_ref[...]
        o_ref[...] = (x * _sigmoid(x)).astype(o_ref.dtype)

    return pl.pallas_call(body, name=name, out_shape=jax.ShapeDtypeStruct(a.shape, MMD))(a)


def _sumsq(a, *, name):
    m, n = a.shape
    tm = _tile(m, 512)

    def body(a_ref, o_ref):
        x = a_ref[...]
        p = jnp.sum(jnp.sum(x * x, axis=1, keepdims=True), axis=0, keepdims=True)

        @pl.when(pl.program_id(0) == 0)
        def _():
            o_ref[...] = p

        @pl.when(pl.program_id(0) > 0)
        def _():
            o_ref[...] += p

    return pl.pallas_call(
        body, name=name, grid=(m // tm,), in_specs=[pl.BlockSpec((tm, n), lambda i: (i, 0))],
        out_specs=pl.BlockSpec((1, 1), lambda i: (0, 0)), out_shape=jax.ShapeDtypeStruct((1, 1), F32),
        compiler_params=_cp(("arbitrary",)),
    )(a)


def _acc_rows(o_ref, p, first):
    @pl.when(first)
    def _():
        o_ref[...] = p

    @pl.when(jnp.logical_not(first))
    def _():
        o_ref[...] += p


def _ln_mod(x, w, scale, shift, *, name):
    s, d = x.shape
    tm = _tile(s, 512)

    def body(x_ref, w_ref, sc_ref, sh_ref, o_ref):
        xv = x_ref[...]
        r = lax.rsqrt(jnp.mean(xv * xv, axis=-1, keepdims=True) + EPS)
        o_ref[...] = ((xv * r) * w_ref[...] * (1.0 + sc_ref[...]) + sh_ref[...]).astype(o_ref.dtype)

    row = pl.BlockSpec((1, d), lambda i: (0, 0))
    big = pl.BlockSpec((tm, d), lambda i: (i, 0))
    return pl.pallas_call(
        body, name=name, grid=(s // tm,), in_specs=[big, row, row, row], out_specs=big,
        out_shape=jax.ShapeDtypeStruct((s, d), MMD), compiler_params=_cp(("arbitrary",)),
    )(x, w, scale, shift)


def _ln_mod_bwd(dh, x, w, scale, dres, *, name):
    s, d = x.shape
    tm = _tile(s, 512)

    def body(dh_ref, x_ref, w_ref, sc_ref, dres_ref, dx_ref, dsh_ref, dsc_ref, dw_ref):
        xv = x_ref[...]
        dhv = dh_ref[...].astype(F32)
        r = lax.rsqrt(jnp.mean(xv * xv, axis=-1, keepdims=True) + EPS)
        nv = xv * r
        wv = w_ref[...]
        g1 = 1.0 + sc_ref[...]
        dn = dhv * (wv * g1)
        dx_ref[...] = dres_ref[...] + r * (dn - nv * jnp.mean(dn * nv, axis=-1, keepdims=True))
        first = pl.program_id(0) == 0
        _acc_rows(dsh_ref, jnp.sum(dhv, axis=0, keepdims=True), first)
        _acc_rows(dsc_ref, jnp.sum(dhv * nv * wv, axis=0, keepdims=True), first)
        _acc_rows(dw_ref, jnp.sum(dhv * nv * g1, axis=0, keepdims=True), first)

    row = pl.BlockSpec((1, d), lambda i: (0, 0))
    big = pl.BlockSpec((tm, d), lambda i: (i, 0))
    return pl.pallas_call(
        body, name=name, grid=(s // tm,), in_specs=[big, big, row, row, big], out_specs=[big, row, row, row],
        out_shape=[jax.ShapeDtypeStruct((s, d), F32)] + [jax.ShapeDtypeStruct((1, d), F32)] * 3,
        compiler_params=_cp(("arbitrary",)),
    )(dh, x, w, scale, dres)


def _gate_bwd(dy, u, gate, *, name):
    s, d = dy.shape
    tm = _tile(s, 512)

    def body(dy_ref, u_ref, g_ref, du_ref, dg_ref):
        dyv = dy_ref[...]
        du_ref[...] = (dyv * g_ref[...]).astype(du_ref.dtype)
        _acc_rows(dg_ref, jnp.sum(dyv * u_ref[...].astype(F32), axis=0, keepdims=True), pl.program_id(0) == 0)

    row = pl.BlockSpec((1, d), lambda i: (0, 0))
    big = pl.BlockSpec((tm, d), lambda i: (i, 0))
    return pl.pallas_call(
        body, name=name, grid=(s // tm,), in_specs=[big, big, row], out_specs=[big, row],
        out_shape=[jax.ShapeDtypeStruct((s, d), MMD), jax.ShapeDtypeStruct((1, d), F32)],
        compiler_params=_cp(("arbitrary",)),
    )(dy, u, gate)


def _seg64(v, e):
    hi = v.astype(jnp.bfloat16)
    lo = (v - hi.astype(F32)).astype(jnp.bfloat16)
    return _dot(hi, e) + _dot(lo, e)


def _rope_tables(s):
    rows = s // GRID_W
    pos_row = jnp.repeat(jnp.arange(rows, dtype=jnp.int32), GRID_W).astype(F32)
    pos_col = jnp.tile(jnp.arange(GRID_W, dtype=jnp.int32), rows).astype(F32)
    axis_dim = HEAD_DIM // 2
    inv_freq = ROPE_THETA ** (-jnp.arange(0, axis_dim, 2, dtype=F32) / axis_dim)
    ang_r = pos_row[:, None] * inv_freq[None, :]
    ang_c = pos_col[:, None] * inv_freq[None, :]
    zero = jnp.zeros_like(ang_r)
    cos = jnp.concatenate([jnp.cos(ang_r), jnp.cos(ang_r), jnp.cos(ang_c), jnp.cos(ang_c)], axis=1)
    s_a = jnp.concatenate([-jnp.sin(ang_r), zero, -jnp.sin(ang_c), zero], axis=1)
    s_b = jnp.concatenate([zero, jnp.sin(ang_r), zero, jnp.sin(ang_c)], axis=1)
    return [jnp.tile(t, (1, 2)) for t in (cos, s_a, s_b)]


def _e128():
    i = jnp.arange(128)
    return (i[:, None] // 64 == i[None, :] // 64).astype(jnp.bfloat16)


QKW = N_Q_HEADS * HEAD_DIM + N_KV_HEADS * HEAD_DIM


def _qk_fwd(proj, wrow, scrow, tabs, *, name):
    s = proj.shape[0]
    tm = _tile(s, 512)

    def body(x_ref, w_ref, sc_ref, cos_ref, sa_ref, sb_ref, e_ref, o_ref, ot_ref):
        u = x_ref[...].astype(F32)
        r = lax.rsqrt(_seg64(u * u, e_ref[...]) * (1.0 / HEAD_DIM) + EPS)
        nv = (u * r) * w_ref[...]
        ro = nv * cos_ref[...] + pltpu.roll(nv, 112, 1) * sa_ref[...] + pltpu.roll(nv, 16, 1) * sb_ref[...]
        out = ro * sc_ref[...]
        o_ref[...] = out.astype(o_ref.dtype)
        ot_ref[...] = out.T.astype(ot_ref.dtype)

    tab = pl.BlockSpec((tm, 128), lambda i, j: (i, 0))
    row = pl.BlockSpec((1, 128), lambda i, j: (0, j))
    return pl.pallas_call(
        body, name=name, grid=(s // tm, QKW // 128),
        in_specs=[pl.BlockSpec((tm, 128), lambda i, j: (i, Q0 // 128 + j)), row, row, tab, tab, tab,
                  pl.BlockSpec((128, 128), lambda i, j: (0, 0))],
        out_specs=[pl.BlockSpec((tm, 128), lambda i, j: (i, j)), pl.BlockSpec((128, tm), lambda i, j: (j, i))],
        out_shape=[jax.ShapeDtypeStruct((s, QKW), MMD), jax.ShapeDtypeStruct((QKW, s), MMD)],
        compiler_params=_cp(("arbitrary", "arbitrary")),
    )(proj, wrow, scrow, *tabs, _e128())


def _qk_bwd(dqkt, proj, wrow, scrow, tabs, *, name):
    s = proj.shape[0]
    tm = _tile(s, 512)

    def body(d_ref, x_ref, w_ref, sc_ref, cos_ref, sa_ref, sb_ref, e_ref, du_ref, dw_ref):
        e = e_ref[...]
        d = d_ref[...].T * sc_ref[...]
        dn = d * cos_ref[...] + pltpu.roll(d * sa_ref[...], 16, 1) + pltpu.roll(d * sb_ref[...], 112, 1)
        u = x_ref[...].astype(F32)
        r = lax.rsqrt(_seg64(u * u, e) * (1.0 / HEAD_DIM) + EPS)
        uh = u * r
        _acc_rows(dw_ref, jnp.sum(dn * uh, axis=0, keepdims=True), pl.program_id(1) == 0)
        dnw = dn * w_ref[...]
        du_ref[...] = (r * (dnw - uh * (_seg64(dnw * uh, e) * (1.0 / HEAD_DIM)))).astype(du_ref.dtype)

    tab = pl.BlockSpec((tm, 128), lambda j, i: (i, 0))
    row = pl.BlockSpec((1, 128), lambda j, i: (0, j))
    return pl.pallas_call(
        body, name=name, grid=(QKW // 128, s // tm),
        in_specs=[pl.BlockSpec((128, tm), lambda j, i: (j, i)), pl.BlockSpec((tm, 128), lambda j, i: (i, Q0 // 128 + j)),
                  row, row, tab, tab, tab, pl.BlockSpec((128, 128), lambda j, i: (0, 0))],
        out_specs=[pl.BlockSpec((tm, 128), lambda j, i: (i, j)), row],
        out_shape=[jax.ShapeDtypeStruct((s, QKW), MMD), jax.ShapeDtypeStruct((1, QKW), F32)],
        compiler_params=_cp(("arbitrary", "arbitrary")),
    )(dqkt, proj, wrow, scrow, *tabs, _e128())


REP = N_Q_HEADS // N_KV_HEADS


def _lanes(ref):
    return jnp.concatenate([ref[r] for r in range(REP)], axis=1)


V_AUG = HEAD_DIM + 8


def _flash_fwd(qkt, vta, *, name):
    s = qkt.shape[2]
    tq, tk = _tile(s, 1024), _tile(s, 512)
    nk = s // tk
    lanes = REP * tq

    def body(q_ref, k_ref, v_ref, o_ref, lse_ref, m_ref, acc_ref):
        j = pl.program_id(2)

        @pl.when(j == 0)
        def _():
            m_ref[...] = jnp.full_like(m_ref, NEG)
            acc_ref[...] = jnp.zeros_like(acc_ref)

        st = _dot(k_ref[0], _lanes(q_ref), TN)
        m_prev = m_ref[...]
        m_new = jnp.maximum(m_prev, jnp.max(st, axis=0, keepdims=True))
        p = jnp.exp(st - m_new).astype(MMD)
        acc_ref[...] = jnp.exp(m_prev - m_new) * acc_ref[...] + _dot(v_ref[0], p)
        m_ref[...] = m_new

        @pl.when(j == nk - 1)
        def _():
            acc = acc_ref[...]
            l = acc[HEAD_DIM:HEAD_DIM + 1]
            o = acc[0:HEAD_DIM] / l
            ls = m_ref[...] + jnp.log(l)
            for r in range(REP):
                o_ref[r] = o[:, r * tq:(r + 1) * tq].astype(o_ref.dtype)
                lse_ref[r] = ls[:, r * tq:(r + 1) * tq]

    qspec = pl.BlockSpec((REP, HEAD_DIM, tq), lambda g, i, j: (g, 0, i))
    return pl.pallas_call(
        body, name=name, grid=(N_KV_HEADS, s // tq, nk),
        in_specs=[qspec, pl.BlockSpec((1, HEAD_DIM, tk), lambda g, i, j: (N_Q_HEADS + g, 0, j)),
                  pl.BlockSpec((1, V_AUG, tk), lambda g, i, j: (g, 0, j))],
        out_specs=[qspec, pl.BlockSpec((REP, 1, tq), lambda g, i, j: (g, 0, i))],
        out_shape=[jax.ShapeDtypeStruct((N_Q_HEADS, HEAD_DIM, s), MMD), jax.ShapeDtypeStruct((N_Q_HEADS, 1, s), F32)],
        scratch_shapes=[pltpu.VMEM((1, lanes), F32), pltpu.VMEM((V_AUG, lanes), F32)],
        compiler_params=_cp(("arbitrary", "arbitrary", "arbitrary"), VMEM_BIG),
    )(qkt, qkt, vta)


def _flash_bwd(qkt, k_h, v_h, dot, ot, lse, *, name):
    s = qkt.shape[2]
    tq, tk = _tile(s, 512), _tile(s, 1024)
    nk = s // tk

    def body(q_ref, kt_ref, k_ref, v_ref, do_ref, o_ref, lse_ref, dq_ref, dk_ref, dv_ref, dq_acc):
        i, j = pl.program_id(1), pl.program_id(2)
        q, do = _lanes(q_ref), _lanes(do_ref)
        delta = jnp.sum(do.astype(F32) * _lanes(o_ref).astype(F32), axis=0, keepdims=True)
        k, v = k_ref[0], v_ref[0]
        p = jnp.exp(_dot(k, q) - _lanes(lse_ref))
        dvc = _dot(p.astype(MMD), do, NT)
        ds = (p * (_dot(v, do) - delta)).astype(MMD)
        dkc = _dot(ds, q, NT)
        dqc = _dot(kt_ref[0], ds)
        rows = pl.ds(pl.multiple_of(j * tk, tk), tk)

        @pl.when(i == 0)
        def _():
            dk_ref[0, rows, :] = dkc
            dv_ref[0, rows, :] = dvc

        @pl.when(i > 0)
        def _():
            dk_ref[0, rows, :] += dkc
            dv_ref[0, rows, :] += dvc

        @pl.when(j == 0)
        def _():
            dq_acc[...] = dqc

        @pl.when(j > 0)
        def _():
            dq_acc[...] += dqc

        @pl.when(j == nk - 1)
        def _():
            acc = dq_acc[...]
            for r in range(REP):
                dq_ref[r] = acc[:, r * tq:(r + 1) * tq]

    qspec = pl.BlockSpec((REP, HEAD_DIM, tq), lambda g, i, j: (g, 0, i))
    kvin = pl.BlockSpec((1, tk, HEAD_DIM), lambda g, i, j: (g, j, 0))
    kvres = pl.BlockSpec((1, s, HEAD_DIM), lambda g, i, j: (g, 0, 0))
    return pl.pallas_call(
        body, name=name, grid=(N_KV_HEADS, s // tq, nk),
        in_specs=[qspec, pl.BlockSpec((1, HEAD_DIM, tk), lambda g, i, j: (N_Q_HEADS + g, 0, j)), kvin, kvin,
                  qspec, qspec, pl.BlockSpec((REP, 1, tq), lambda g, i, j: (g, 0, i))],
        out_specs=[qspec, kvres, kvres],
        out_shape=[jax.ShapeDtypeStruct((N_Q_HEADS, HEAD_DIM, s), F32), jax.ShapeDtypeStruct((N_KV_HEADS, s, HEAD_DIM), F32),
                   jax.ShapeDtypeStruct((N_KV_HEADS, s, HEAD_DIM), F32)],
        scratch_shapes=[pltpu.VMEM((HEAD_DIM, REP * tq), F32)],
        compiler_params=_cp(("arbitrary", "arbitrary", "arbitrary"), VMEM_BIG),
    )(qkt, qkt, k_h, v_h, dot, ot, lse)


HALO = 8
CONV_W = 2048 + 2 * SSD_GROUPS * SSD_N


def _shifted(win, off, r):
    return pltpu.roll(win, (r + 2 * HALO - off) % (r + 2 * HALO), 0)[0:r]


def _conv_fwd(proj, w8, brow, *, name):
    s = proj.shape[0]
    cb = 256
    r = _tile(s, 512)

    def body(x_ref, w_ref, b_ref, o_ref, pad_ref):
        zeros = jnp.zeros((HALO, cb), F32)
        pad_ref[0:HALO, :] = zeros
        pad_ref[s + HALO:s + 2 * HALO, :] = zeros

        def fill(i, carry):
            st = pl.multiple_of(i * r, r)
            pad_ref[pl.ds(st + HALO, r), :] = x_ref[pl.ds(st, r), :].astype(F32)
            return carry

        lax.fori_loop(0, s // r, fill, 0)
        wv = w_ref[...]
        bv = b_ref[...]

        def step(i, carry):
            st = pl.multiple_of(i * r, r)
            win = pad_ref[pl.ds(st, r + 2 * HALO), :]
            acc = bv + wv[0:1, :] * _shifted(win, HALO - 2, r)
            for t in range(1, D_CONV):
                acc = acc + wv[t:t + 1, :] * _shifted(win, HALO - 2 + t, r)
            o_ref[pl.ds(st, r), :] = (acc * _sigmoid(acc)).astype(o_ref.dtype)
            return carry

        lax.fori_loop(0, s // r, step, 0)

    return pl.pallas_call(
        body, name=name, grid=(CONV_W // cb,),
        in_specs=[pl.BlockSpec((s, cb), lambda j: (0, XS0 // cb + j)), pl.BlockSpec((8, cb), lambda j: (0, j)),
                  pl.BlockSpec((1, cb), lambda j: (0, j))],
        out_specs=pl.BlockSpec((s, cb), lambda j: (0, j)),
        out_shape=jax.ShapeDtypeStruct((s, CONV_W), MMD),
        scratch_shapes=[pltpu.VMEM((s + 2 * HALO, cb), F32)],
        compiler_params=_cp(("arbitrary",), VMEM_MID),
    )(proj, w8, brow)


def _conv_bwd(proj, col0, ga, gb, w8, brow, *, name):
    s = proj.shape[0]
    width = ga.shape[1]
    cb = 128
    c0 = col0 // cb
    r = _tile(s, 512)

    def body(x_ref, ga_ref, gb_ref, w_ref, b_ref, dx_ref, dw_ref, db_ref, xpad, dpad):
        zeros = jnp.zeros((HALO, cb), F32)
        for ref in (xpad, dpad):
            ref[0:HALO, :] = zeros
            ref[s + HALO:s + 2 * HALO, :] = zeros

        def fill(i, carry):
            st = pl.multiple_of(i * r, r)
            xpad[pl.ds(st + HALO, r), :] = x_ref[pl.ds(st, r), :].astype(F32)
            return carry

        lax.fori_loop(0, s // r, fill, 0)
        wv = w_ref[...]
        bv = b_ref[...]

        def first(i, carry):
            st = pl.multiple_of(i * r, r)
            win = xpad[pl.ds(st, r + 2 * HALO), :]
            taps = [_shifted(win, HALO - 2 + t, r) for t in range(D_CONV)]
            u = bv
            for t in range(D_CONV):
                u = u + wv[t:t + 1, :] * taps[t]
            sg = _sigmoid(u)
            du = (ga_ref[pl.ds(st, r), :] + gb_ref[pl.ds(st, r), :]) * (sg * (1.0 + u * (1.0 - sg)))
            dpad[pl.ds(st + HALO, r), :] = du
            out = [carry[0] + jnp.sum(du, axis=0, keepdims=True)]
            for t in range(D_CONV):
                out.append(carry[1 + t] + jnp.sum(du * taps[t], axis=0, keepdims=True))
            return tuple(out)

        sums = lax.fori_loop(0, s // r, first, tuple(jnp.zeros((1, cb), F32) for _ in range(1 + D_CONV)))
        db_ref[...] = sums[0]
        for t in range(D_CONV):
            dw_ref[t:t + 1, :] = sums[1 + t]
        dw_ref[D_CONV:8, :] = jnp.zeros((8 - D_CONV, cb), F32)

        def second(i, carry):
            st = pl.multiple_of(i * r, r)
            win = dpad[pl.ds(st, r + 2 * HALO), :]
            acc = wv[0:1, :] * _shifted(win, HALO + 2, r)
            for t in range(1, D_CONV):
                acc = acc + wv[t:t + 1, :] * _shifted(win, HALO + 2 - t, r)
            dx_ref[pl.ds(st, r), :] = acc.astype(dx_ref.dtype)
            return carry

        lax.fori_loop(0, s // r, second, 0)

    col = pl.BlockSpec((s, cb), lambda j: (0, j))
    return pl.pallas_call(
        body, name=name, grid=(width // cb,),
        in_specs=[pl.BlockSpec((s, cb), lambda j: (0, XS0 // cb + c0 + j)), col, col,
                  pl.BlockSpec((8, cb), lambda j: (0, c0 + j)), pl.BlockSpec((1, cb), lambda j: (0, c0 + j))],
        out_specs=[col, pl.BlockSpec((8, cb), lambda j: (0, j)), pl.BlockSpec((1, cb), lambda j: (0, j))],
        out_shape=[jax.ShapeDtypeStruct((s, width), MMD), jax.ShapeDtypeStruct((8, width), F32),
                   jax.ShapeDtypeStruct((1, width), F32)],
        scratch_shapes=[pltpu.VMEM((s + 2 * HALO, cb), F32), pltpu.VMEM((s + 2 * HALO, cb), F32)],
        compiler_params=_cp(("arbitrary",), VMEM_BIG),
    )(proj, ga, gb, w8, brow)


def _tri(lower):
    i = jnp.arange(CHUNK)
    return ((i[:, None] >= i[None, :]) if lower else (i[:, None] <= i[None, :])).astype(F32)


def _dt_fwd(raw, bias, arow, *, name):
    s = raw.shape[0]

    def body(r_ref, b_ref, a_ref, lo_ref, up_ref, dt_ref, cs_ref):
        u = r_ref[...] + b_ref[...]
        dt = jnp.maximum(u, 0.0) + jnp.log1p(jnp.exp(-jnp.abs(u)))
        dt_ref[...] = dt
        a = dt * a_ref[...]
        lane = lax.broadcasted_iota(jnp.int32, (CHUNK, 128), 1)
        cs_ref[...] = jnp.where(lane < SSD_HEADS, _dot_hi(lo_ref[...], a), _dot_hi(up_ref[...], a))

    blk = pl.BlockSpec((CHUNK, 128), lambda i: (i, 0))
    row = pl.BlockSpec((1, 128), lambda i: (0, 0))
    tri = pl.BlockSpec((CHUNK, CHUNK), lambda i: (0, 0))
    return pl.pallas_call(
        body, name=name, grid=(s // CHUNK,), in_specs=[blk, row, row, tri, tri], out_specs=[blk, blk],
        out_shape=[jax.ShapeDtypeStruct((s, 128), F32)] * 2, compiler_params=_cp(("arbitrary",)),
    )(raw, bias, arow, _tri(True), _tri(False))


def _dt_bwd(ddt, raw, bias, *, name):
    s = raw.shape[0]
    tm = _tile(s, 1024)

    def body(d_ref, r_ref, b_ref, o_ref, db_ref):
        g = d_ref[...] * _sigmoid(r_ref[...] + b_ref[...])
        o_ref[...] = g.astype(o_ref.dtype)
        _acc_rows(db_ref, jnp.sum(g, axis=0, keepdims=True), pl.program_id(0) == 0)

    blk = pl.BlockSpec((tm, 128), lambda i: (i, 0))
    row = pl.BlockSpec((1, 128), lambda i: (0, 0))
    return pl.pallas_call(
        body, name=name, grid=(s // tm,), in_specs=[blk, blk, row], out_specs=[blk, row],
        out_shape=[jax.ShapeDtypeStruct((s, 128), MMD), jax.ShapeDtypeStruct((1, 128), F32)],
        compiler_params=_cp(("arbitrary",)),
    )(ddt, raw, bias)


GW = HPG * SSD_P


GPS = 4


def _ssd_specs(nc, rev):
    cc = (lambda c: nc - 1 - c) if rev else (lambda c: c)
    nb = SSD_GROUPS // GPS
    return dict(
        x=pl.BlockSpec((CHUNK, GPS * GW), lambda g, c: (cc(c), g)),
        b=pl.BlockSpec((CHUNK, GPS * SSD_N), lambda g, c: (cc(c), 2048 // (GPS * SSD_N) + g)),
        c=pl.BlockSpec((CHUNK, GPS * SSD_N), lambda g, c: (cc(c), 2048 // (GPS * SSD_N) + nb + g)),
        col=pl.BlockSpec((GPS, CHUNK, HPG), lambda g, c: (g, cc(c), 0)),
        lanes=pl.BlockSpec((CHUNK, 128), lambda g, c: (cc(c), 0)),
        rowt=pl.BlockSpec((GPS, 1, HPG, CHUNK), lambda g, c: (g, cc(c), 0, 0)),
        drow=pl.BlockSpec((1, GPS * GW), lambda g, c: (0, g)),
        y=pl.BlockSpec((CHUNK, GPS * GW), lambda g, c: (cc(c), g)),
        h=pl.BlockSpec((GPS, 1, SSD_N, GW), lambda g, c: (g, cc(c), 0, 0)),
        n=pl.BlockSpec((CHUNK, GPS * SSD_N), lambda g, c: (cc(c), g)),
    )


def _ssd_mask(anti):
    ii = lax.broadcasted_iota(jnp.int32, (CHUNK, CHUNK), 0)
    jj = lax.broadcasted_iota(jnp.int32, (CHUNK, CHUNK), 1)
    return ii, jj, (ii <= jj) if anti else (ii >= jj)


def _expand(x, ex):
    h1 = x.astype(jnp.bfloat16)
    r1 = x - h1.astype(F32)
    h2 = r1.astype(jnp.bfloat16)
    h3 = (r1 - h2.astype(F32)).astype(jnp.bfloat16)
    return _dot(h1, ex) + _dot(h2, ex) + _dot(h3, ex)


def _headsum(a, e):
    hi = a.astype(jnp.bfloat16)
    return _dot(hi, e) + _dot((a - hi.astype(F32)).astype(jnp.bfloat16), e)


def _expand_mats():
    lane = jnp.arange(128)[None, :, None]
    col = jnp.arange(GW)[None, None, :]
    base = (jnp.arange(2)[:, None] * SSD_HEADS + jnp.arange(SSD_GROUPS)[None, :] * HPG).reshape(2 * SSD_GROUPS, 1, 1)
    return (lane == base + col // SSD_P).astype(jnp.bfloat16)


def _headsum_mats():
    e1 = (jnp.arange(GW)[:, None] // SSD_P == jnp.arange(128)[None, :]).astype(jnp.bfloat16)
    e2 = (jnp.arange(HPG * CHUNK)[:, None] // CHUNK == jnp.arange(128)[None, :]).astype(jnp.bfloat16)
    return e1, e2


def _ssd_fwd(xc, dt, cs, cst, ex, drow, di, *, name):
    s = xc.shape[0]
    nc = s // CHUNK
    anti = di == 1
    sp = _ssd_specs(nc, anti)
    trow = 0 if anti else CHUNK - 1

    def body(x_ref, b_ref, c_ref, dt_ref, cs_ref, cst_ref, ex_ref, d_ref, y_ref, hp_ref, h_ref):
        @pl.when(pl.program_id(1) == 0)
        def _():
            h_ref[...] = jnp.zeros_like(h_ref)

        mask = _ssd_mask(anti)[2]
        dtv, csv = dt_ref[...], cs_ref[...]
        for gi in range(GPS):
            cols = slice(gi * GW, (gi + 1) * GW)
            ncols = slice(gi * SSD_N, (gi + 1) * SSD_N)
            ex = ex_ref[gi]
            xb = x_ref[:, cols].astype(F32)
            bm, cm = b_ref[:, ncols], c_ref[:, ncols]
            csr = cst_ref[gi, 0]
            dtf = _expand(dtv, ex)
            csf = _expand(csv, ex)
            tl = csf[trow:trow + 1, :]
            h = h_ref[gi]
            hp_ref[gi, 0] = h
            g = _dot(cm, bm, NT)
            xs = xb * dtf
            xsm = xs.astype(MMD)
            base = jnp.exp(csf) * _dot(cm, h.astype(MMD)) + d_ref[:, cols] * xb
            for r in range(HPG):
                sl = slice(r * SSD_P, (r + 1) * SSD_P)
                lm = jnp.exp(jnp.where(mask, csf[:, r * SSD_P:r * SSD_P + 1] - csr[r:r + 1, :], NEG))
                y_ref[:, gi * GW + r * SSD_P:gi * GW + (r + 1) * SSD_P] = _dot((g * lm).astype(MMD), xsm[:, sl]) + base[:, sl]
            xd = (xs * jnp.exp(tl - csf)).astype(MMD)
            h_ref[gi] = h * jnp.exp(tl) + _dot(bm, xd, TN)

    nb = SSD_GROUPS // GPS
    return pl.pallas_call(
        body, name=name, grid=(nb, nc),
        in_specs=[sp["x"], sp["b"], sp["c"], sp["lanes"], sp["lanes"], sp["rowt"],
                  pl.BlockSpec((GPS, 128, GW), lambda g, c: (di * nb + g, 0, 0)), sp["drow"]],
        out_specs=[sp["y"], sp["h"]],
        out_shape=[jax.ShapeDtypeStruct((s, 2048), F32), jax.ShapeDtypeStruct((SSD_GROUPS, nc, SSD_N, GW), F32)],
        scratch_shapes=[pltpu.VMEM((GPS, SSD_N, GW), F32)],
        compiler_params=_cp(("arbitrary", "arbitrary")),
    )(xc, xc, xc, dt, cs, cst, ex, drow)


def _ssd_bwd(xc, dt, cs, dt4, cst, ex, drow, arow4, dy, hprev, di, *, name):
    s = xc.shape[0]
    nc = s // CHUNK
    anti = di == 1
    sp = _ssd_specs(nc, not anti)
    trow = 0 if anti else CHUNK - 1
    e1, e2 = _headsum_mats()

    def body(x_ref, b_ref, c_ref, dt_ref, cs_ref, dt4_ref, cst_ref, ex_ref, d_ref, a_ref, dy_ref, hp_ref, tri_ref,
             e1_ref, e2_ref, dx_ref, db_ref, dc_ref, ddt_ref, da_ref, dh_ref, w_ref, dxs_ref):
        @pl.when(pl.program_id(1) == 0)
        def _():
            dh_ref[...] = jnp.zeros_like(dh_ref)
            da_ref[...] = jnp.zeros_like(da_ref)

        e1v = e1_ref[...]
        ii, _, mask = _ssd_mask(anti)
        dtv, csv = dt_ref[...], cs_ref[...]
        for gi in range(GPS):
            cols = slice(gi * GW, (gi + 1) * GW)
            ncols = slice(gi * SSD_N, (gi + 1) * SSD_N)
            ex = ex_ref[gi]
            xb = x_ref[:, cols].astype(F32)
            bm, cm = b_ref[:, ncols], c_ref[:, ncols]
            csr = cst_ref[gi, 0]
            dyb = dy_ref[:, cols]
            dym = dyb.astype(MMD)
            hp = hp_ref[gi, 0]
            hpm = hp.astype(MMD)
            dh = dh_ref[gi]
            dhm = dh.astype(MMD)
            dtf = _expand(dtv, ex)
            csf = _expand(csv, ex)
            tl = csf[trow:trow + 1, :]
            e = jnp.exp(csf)
            dec = jnp.exp(tl - csf)
            et = jnp.exp(tl)
            xs = xb * dtf
            xsm = xs.astype(MMD)
            g = _dot(cm, bm, NT)
            z = _dot(cm, hpm)
            bdh = _dot(bm, dhm)
            dg = jnp.zeros((CHUNK, CHUNK), F32)
            wcols = jnp.zeros((CHUNK, CHUNK), F32)
            for r in range(HPG):
                sl = slice(r * SSD_P, (r + 1) * SSD_P)
                lm = jnp.exp(jnp.where(mask, csf[:, r * SSD_P:r * SSD_P + 1] - csr[r:r + 1, :], NEG))
                mm = g * lm
                dm = _dot(dym[:, sl], xsm[:, sl], NT)
                w = dm * mm
                w_ref[gi, :, r * CHUNK:(r + 1) * CHUNK] = w
                wcols = jnp.where(ii == r, jnp.sum(w, axis=0, keepdims=True), wcols)
                dg = dg + dm * lm
                dxs_ref[gi, :, sl] = _dot(mm.astype(MMD), dym[:, sl], TN)
            dxs = dxs_ref[gi] + dec * bdh
            dx_ref[:, cols] = dxs * dtf + d_ref[:, cols] * dyb
            tb = xs * bdh * dec
            d_tot = jnp.sum(tb, axis=0, keepdims=True) + et * jnp.sum(dh * hp, axis=0, keepdims=True)
            d_tot = _headsum(jnp.broadcast_to(d_tot, (8, GW)), e1v)[0:1]
            dcs = (_headsum(dyb * (e * z) - tb, e1v) + _headsum(w_ref[gi], e2_ref[...]) - wcols.T
                   + jnp.where(ii == trow, d_tot, 0.0))
            da = _dot_hi(tri_ref[...], dcs)
            ddt_ref[gi] = (da * a_ref[gi] + _headsum(dxs * xb, e1v))[:, 0:HPG]
            da_ref[gi] += jnp.sum(da[:, 0:HPG] * dt4_ref[gi], axis=0, keepdims=True)
            dgm = dg.astype(MMD)
            dz = (e * dyb).astype(MMD)
            dc_ref[:, ncols] = _dot(dgm, bm) + _dot(dz, hpm, NT)
            db_ref[:, ncols] = _dot(dgm, cm, TN) + _dot((xs * dec).astype(MMD), dhm, NT)
            dh_ref[gi] = dh * et + _dot(cm, dz, TN)

    nb = SSD_GROUPS // GPS
    const = lambda shape: pl.BlockSpec(shape, lambda g, c: (0,) * len(shape))
    return pl.pallas_call(
        body, name=name, grid=(nb, nc),
        in_specs=[sp["x"], sp["b"], sp["c"], sp["lanes"], sp["lanes"], sp["col"], sp["rowt"],
                  pl.BlockSpec((GPS, 128, GW), lambda g, c: (di * nb + g, 0, 0)), sp["drow"],
                  pl.BlockSpec((GPS, 1, 128), lambda g, c: (g, 0, 0)), sp["y"], sp["h"],
                  const((CHUNK, CHUNK)), const((GW, 128)), const((HPG * CHUNK, 128))],
        out_specs=[sp["y"], sp["n"], sp["n"], sp["col"], pl.BlockSpec((GPS, 1, HPG), lambda g, c: (g, 0, 0))],
        out_shape=[jax.ShapeDtypeStruct((s, 2048), F32), jax.ShapeDtypeStruct((s, SSD_GROUPS * SSD_N), F32),
                   jax.ShapeDtypeStruct((s, SSD_GROUPS * SSD_N), F32), jax.ShapeDtypeStruct((SSD_GROUPS, s, HPG), F32),
                   jax.ShapeDtypeStruct((SSD_GROUPS, 1, HPG), F32)],
        scratch_shapes=[pltpu.VMEM((GPS, SSD_N, GW), F32), pltpu.VMEM((GPS, CHUNK, HPG * CHUNK), F32),
                        pltpu.VMEM((GPS, CHUNK, GW), F32)],
        compiler_params=_cp(("arbitrary", "arbitrary")),
    )(xc, xc, xc, dt, cs, dt4, cst, ex, drow, arow4, dy, hprev, _tri(anti), e1, e2)


def _gnorm_fwd(ya, yb, proj, w, *, name):
    s = ya.shape[0]
    tm = _tile(s, 256)

    def body(a_ref, b_ref, z_ref, w_ref, o_ref):
        zv = z_ref[...].astype(F32)
        t = (a_ref[...] + b_ref[...]) * (zv * _sigmoid(zv))
        r = lax.rsqrt(jnp.mean(t * t, axis=-1, keepdims=True) + EPS)
        o_ref[...] = ((t * r) * w_ref[...]).astype(o_ref.dtype)

    big = pl.BlockSpec((tm, 2048), lambda i: (i, 0))
    row = pl.BlockSpec((1, 2048), lambda i: (0, 0))
    return pl.pallas_call(
        body, name=name, grid=(s // tm,), in_specs=[big, big, big, row], out_specs=big,
        out_shape=jax.ShapeDtypeStruct((s, 2048), MMD), compiler_params=_cp(("arbitrary",)),
    )(ya, yb, proj, w)


def _gnorm_bwd(dout, ya, yb, proj, w, *, name):
    s = ya.shape[0]
    tm = _tile(s, 256)

    def body(do_ref, a_ref, b_ref, z_ref, w_ref, dy_ref, dz_ref, dw_ref):
        zv = z_ref[...].astype(F32)
        sg = _sigmoid(zv)
        sz = zv * sg
        y = a_ref[...] + b_ref[...]
        t = y * sz
        r = lax.rsqrt(jnp.mean(t * t, axis=-1, keepdims=True) + EPS)
        nv = t * r
        dov = do_ref[...].astype(F32)
        _acc_rows(dw_ref, jnp.sum(dov * nv, axis=0, keepdims=True), pl.program_id(0) == 0)
        dn = dov * w_ref[...]
        dt_ = r * (dn - nv * jnp.mean(dn * nv, axis=-1, keepdims=True))
        dy_ref[...] = dt_ * sz
        dz_ref[...] = (dt_ * y * (sg * (1.0 + zv * (1.0 - sg)))).astype(dz_ref.dtype)

    big = pl.BlockSpec((tm, 2048), lambda i: (i, 0))
    row = pl.BlockSpec((1, 2048), lambda i: (0, 0))
    return pl.pallas_call(
        body, name=name, grid=(s // tm,), in_specs=[big, big, big, big, row], out_specs=[big, big, row],
        out_shape=[jax.ShapeDtypeStruct((s, 2048), F32), jax.ShapeDtypeStruct((s, 2048), MMD),
                   jax.ShapeDtypeStruct((1, 2048), F32)],
        compiler_params=_cp(("arbitrary",)),
    )(dout, ya, yb, proj, w)


def _colsum_prod(a, b, *, name):
    s, n = a.shape
    tm = _tile(s, 256)

    def body(a_ref, b_ref, o_ref):
        _acc_rows(o_ref, jnp.sum(a_ref[...].astype(F32) * b_ref[...].astype(F32), axis=0, keepdims=True),
                  pl.program_id(0) == 0)

    big = pl.BlockSpec((tm, n), lambda i: (i, 0))
    return pl.pallas_call(
        body, name=name, grid=(s // tm,), in_specs=[big, big], out_specs=pl.BlockSpec((1, n), lambda i: (0, 0)),
        out_shape=jax.ShapeDtypeStruct((1, n), F32), compiler_params=_cp(("arbitrary",)),
    )(a, b)


def _heads(a, n):
    return a.reshape(a.shape[0], n, HEAD_DIM).transpose(1, 0, 2)


def _unheads(a):
    return a.transpose(1, 0, 2).reshape(a.shape[1], a.shape[0] * HEAD_DIM)


def _per_group(a):
    return a.reshape(a.shape[0], SSD_GROUPS, HPG).transpose(1, 0, 2)


def _per_group_t(a):
    s = a.shape[0]
    return a.reshape(s // CHUNK, CHUNK, SSD_GROUPS, HPG).transpose(2, 0, 3, 1)


def _local_step(x, target, mod, wts, small):
    s, d = x.shape
    shift1, scale1, gate1, shift2, scale2, gate2 = [mod[i:i + 1] for i in range(6)]

    h1 = _ln_mod(x, small["norm1_w"], scale1, shift1, name="ln1")
    proj = _mm(h1, wts["w_in_p"], name="in_proj", outs=[MMD], tm=512, tn=2944, b_outer=True)
    dt_raw = _mm(h1, wts["w_dt"], name="dt_proj", outs=[F32], tm=512, tn=128)

    qk_w = jnp.concatenate([jnp.tile(small["q_norm_w"], (1, N_Q_HEADS)), jnp.tile(small["k_norm_w"], (1, N_KV_HEADS))], axis=1)
    qk_sc = jnp.concatenate([jnp.full((1, N_Q_HEADS * HEAD_DIM), HEAD_DIM ** -0.5, F32),
                             jnp.ones((1, N_KV_HEADS * HEAD_DIM), F32)], axis=1)
    tabs = _rope_tables(s)
    qk, qkt = _qk_fwd(proj, qk_w, qk_sc, tabs, name="qk_fwd")
    qkt = qkt.reshape(N_Q_HEADS + N_KV_HEADS, HEAD_DIM, s)
    k_h = _heads(qk[:, N_Q_HEADS * HEAD_DIM:], N_KV_HEADS)
    v_sd = proj[:, V0:V0 + N_KV_HEADS * HEAD_DIM]
    v_h = _heads(v_sd, N_KV_HEADS)
    vta = jnp.concatenate([v_sd.T.reshape(N_KV_HEADS, HEAD_DIM, s), jnp.ones((N_KV_HEADS, V_AUG - HEAD_DIM, s), MMD)], axis=1)
    ot, lse = _flash_fwd(qkt, vta, name="flash_fwd")
    ot2 = ot.reshape(N_Q_HEADS * HEAD_DIM, s)

    w8 = jnp.pad(small["conv_w"], ((0, 8 - D_CONV), (0, 0)))
    xc = _conv_fwd(proj, w8, small["conv_b"], name="conv_fwd")
    a_neg = -jnp.exp(small["A_log"])
    arow = jnp.pad(a_neg.reshape(1, 2 * SSD_HEADS), ((0, 0), (0, 128 - 2 * SSD_HEADS)))
    bias_row = jnp.pad(small["dt_bias"].reshape(1, 2 * SSD_HEADS), ((0, 0), (0, 128 - 2 * SSD_HEADS)))
    dt, cs = _dt_fwd(dt_raw, bias_row, arow, name="dt_fwd")
    drow = jnp.repeat(small["ssd_D"], SSD_P, axis=1)
    dirs = []
    for di in range(2):
        cols = slice(di * SSD_HEADS, (di + 1) * SSD_HEADS)
        dirs.append(dict(
            dt4=_per_group(dt[:, cols]), cst=_per_group_t(cs[:, cols]),
            drow=drow if di == 0 else jnp.zeros_like(drow),
            arow4=jnp.pad(a_neg[di].reshape(SSD_GROUPS, 1, HPG), ((0, 0), (0, 0), (0, 128 - HPG)))))
    ex = _expand_mats()
    ys = []
    for di, dd in enumerate(dirs):
        y, dd["hprev"] = _ssd_fwd(xc, dt, cs, dd["cst"], ex, dd["drow"], di, name=f"ssd_fwd{di}")
        ys.append(y)
    ssdn = _gnorm_fwd(ys[0], ys[1], proj, small["ssd_norm_w"], name="gnorm_fwd")

    a_o = _mm(ot2, wts["w_attn_out"], name="attn_out", outs=[MMD], ta=True, tm=512, tn=1024)

    def merge_epi(acc, ao, ga, gs):
        return (_sigmoid(ga.astype(F32)) * ao.astype(F32) + _sigmoid(gs.astype(F32)) * acc, acc)

    merged, b_o = _mm(ssdn, wts["w_ssd_out"], name="ssd_out", outs=[MMD, MMD], tm=512, tn=512,
                      extras=[(a_o, "tile", 0), (proj, "tile", GA0), (proj, "tile", GS0)], epi=merge_epi)

    def res_epi(acc, res, gate):
        return (res + gate * acc, acc)

    x1, mo = _mm(merged, wts["w_o"], name="w_o", outs=[F32, MMD], tm=512, tn=512,
                 extras=[(x, "tile", 0), (gate1, "row", 0)], epi=res_epi)
    h2 = _ln_mod(x1, small["norm2_w"], scale2, shift2, name="ln2")

    def relu2_epi(acc):
        rl = jnp.maximum(acc, 0.0)
        return (rl * rl, rl)

    act, rl = _mm(h2, wts["w_mlp1"], name="mlp1", outs=[MMD, MMD], tm=512, tn=1024, epi=relu2_epi, b_outer=True)

    def loss_epi(acc, res, gate, tgt):
        return ((res + gate * acc - tgt) * (1.0 / d), acc)

    dy, ffo = _mm(act, wts["w_mlp2"], name="mlp2", outs=[F32, MMD], tm=512, tn=1024, vmem=VMEM_BIG,
                  extras=[(x1, "tile", 0), (gate2, "row", 0), (target, "tile", 0)], epi=loss_epi)
    loss = _sumsq(dy, name="loss") * (0.5 * d)

    gw = {}
    gs_ = {}
    dffo, dgate2 = _gate_bwd(dy, ffo, gate2, name="gate2_bwd")
    dpre = _mm(dffo, wts["w_mlp2"], name="mlp2_dx", outs=[MMD], nt=True, tm=512, tn=1024,
               extras=[(rl, "tile", 0)], epi=lambda acc, r: (acc * (2.0 * r.astype(F32)),))
    gw["w_mlp2"] = _mm_tn(act, dffo, name="mlp2_dw")
    dh2 = _mm(dpre, wts["w_mlp1"], name="mlp1_dx", outs=[F32], nt=True, tm=512, tn=1024)
    gw["w_mlp1"] = _mm_tn(h2, dpre, name="mlp1_dw")
    dx1, dshift2, dscale2, gs_["norm2_w"] = _ln_mod_bwd(dh2, x1, small["norm2_w"], scale2, dy, name="ln2_bwd")
    dmo, dgate1 = _gate_bwd(dx1, mo, gate1, name="gate1_bwd")

    def merge_bwd_epi(acc, ao, bo, ga, gs):
        sa, ss = _sigmoid(ga.astype(F32)), _sigmoid(gs.astype(F32))
        return (acc * sa, acc * ss, acc * ao.astype(F32) * sa * (1.0 - sa), acc * bo.astype(F32) * ss * (1.0 - ss))

    da_o, db_o, dga, dgs = _mm(dmo, wts["w_o"], name="w_o_dx", outs=[MMD] * 4, nt=True, tm=512, tn=512,
                               extras=[(a_o, "tile", 0), (b_o, "tile", 0), (proj, "tile", GA0), (proj, "tile", GS0)],
                               epi=merge_bwd_epi)
    gw["w_o"] = _mm_tn(merged, dmo, name="w_o_dw")
    dot = _mm(wts["w_attn_out"], da_o, name="attn_out_dx", outs=[MMD], nt=True, tm=512, tn=512)
    gw["w_attn_out"] = _mm(ot2, da_o, name="attn_out_dw", outs=[F32], tm=256, tn=512, vmem=VMEM_BIG)
    dssdn = _mm(db_o, wts["w_ssd_out"], name="ssd_out_dx", outs=[MMD], nt=True, tm=512, tn=512)
    gw["w_ssd_out"] = _mm_tn(ssdn, db_o, name="ssd_out_dw")

    dyssd, dz, gs_["ssd_norm_w"] = _gnorm_bwd(dssdn, ys[0], ys[1], proj, small["ssd_norm_w"], name="gnorm_bwd")
    gs_["ssd_D"] = _colsum_prod(dyssd, xc[:, 0:2048], name="ssd_d_grad").reshape(SSD_HEADS, SSD_P).sum(axis=1).reshape(1, SSD_HEADS)
    dxc, ddts, das = [], [], []
    for di, dd in enumerate(dirs):
        dxs, dbm, dcm, ddt4, da4 = _ssd_bwd(xc, dt, cs, dd["dt4"], dd["cst"], ex, dd["drow"], dd["arow4"],
                                            dyssd, dd["hprev"], di, name=f"ssd_bwd{di}")
        dxc.append((dxs, dbm, dcm))
        ddts.append(ddt4.transpose(1, 0, 2).reshape(s, SSD_HEADS))
        das.append(da4.reshape(1, SSD_HEADS))
    conv_parts, col0 = [], 0
    for part, (ga, gb) in enumerate(zip(*dxc)):
        conv_parts.append(_conv_bwd(proj, col0, ga, gb, w8, small["conv_b"], name=f"conv_bwd{part}"))
        col0 += ga.shape[1]
    dxbc, dw8, gs_["conv_b"] = [jnp.concatenate(t, axis=1) for t in zip(*conv_parts)]
    gs_["conv_w"] = dw8[0:D_CONV]
    gs_["A_log"] = jnp.concatenate(das, axis=0) * a_neg
    ddt = jnp.pad(jnp.concatenate(ddts, axis=1), ((0, 0), (0, 128 - 2 * SSD_HEADS)))
    ddt_raw, dbias = _dt_bwd(ddt, dt_raw, bias_row, name="dt_bwd")
    gs_["dt_bias"] = dbias[:, 0:2 * SSD_HEADS].reshape(2, SSD_HEADS)

    dqt, dk_h, dv_h = _flash_bwd(qkt, k_h, v_h, dot.reshape(N_Q_HEADS, HEAD_DIM, s), ot, lse, name="flash_bwd")
    dqkt = jnp.concatenate([dqt.reshape(N_Q_HEADS * HEAD_DIM, s),
                            dk_h.transpose(0, 2, 1).reshape(N_KV_HEADS * HEAD_DIM, s)], axis=0)
    dqk_u, dqk_w = _qk_bwd(dqkt, proj, qk_w, qk_sc, tabs, name="qk_bwd")
    gs_["q_norm_w"] = dqk_w[:, 0:N_Q_HEADS * HEAD_DIM].reshape(N_Q_HEADS, HEAD_DIM).sum(axis=0, keepdims=True)
    gs_["k_norm_w"] = dqk_w[:, N_Q_HEADS * HEAD_DIM:].reshape(N_KV_HEADS, HEAD_DIM).sum(axis=0, keepdims=True)
    dv = _unheads(dv_h).astype(MMD)

    dproj = jnp.concatenate([dz, dga, dgs, dxbc, dqk_u, dv, ddt_raw], axis=1)
    dh1 = _mm(dproj, wts["w_in_p"], name="in_proj_dx", outs=[F32], nt=True, tm=256, tn=1024, vmem=VMEM_BIG)
    gw["w_in_p"] = _mm_tn(h1, dproj, name="in_proj_dw", tk=512, tn=2944, tmm=2048, vmem=VMEM_BIG)
    grad_x, dshift1, dscale1, gs_["norm1_w"] = _ln_mod_bwd(dh1, x, small["norm1_w"], scale1, dx1, name="ln1_bwd")
    dmod = jnp.concatenate([dshift1, dscale1, dgate1, dshift2, dscale2, dgate2], axis=0)
    return loss, grad_x, dmod, gw, gs_


N_DEV = 8
N_CHIP = 4
ANY = pl.BlockSpec(memory_space=pl.ANY)


def _place():
    return lax.axis_index("x"), lax.axis_index("y"), lax.axis_index("c")


def _allgather8(v, *, name):
    m_per, n = v.shape

    def body(x_ref, out_ref, send_sems, recv_sems, local_sem):
        x, y, c = _place()
        me, sibling = (x, y, c), (x, y, 1 - c)
        chips = [(1 - x, y), (x, 1 - y), (1 - x, 1 - y)]

        def rows(px, py, pc):
            return out_ref.at[pl.ds((4 * px + 2 * py + pc) * m_per, m_per), :]

        def copy(k, block, to, src=None):
            return pltpu.make_async_remote_copy(
                src_ref=rows(*block) if src is None else src, dst_ref=rows(*block),
                send_sem=send_sems.at[k], recv_sem=recv_sems.at[k], device_id=to, device_id_type=MESH)

        mine = pltpu.make_async_copy(x_ref, rows(*me), local_sem)
        mine.start()
        first = [copy(0, me, sibling, src=x_ref)]
        first += [copy(1 + j, me, (*chip, c), src=x_ref) for j, chip in enumerate(chips)]
        for cp in first:
            cp.start()
        passed = [copy(4 + j, (*chip, c), sibling) for j, chip in enumerate(chips)]
        for j, chip in enumerate(chips):
            copy(1 + j, (*chip, c), me).wait_recv()
            passed[j].start()
        copy(0, sibling, me).wait_recv()
        for j, chip in enumerate(chips):
            copy(4 + j, (*chip, 1 - c), me).wait_recv()
        for cp in first + passed:
            cp.wait_send()
        mine.wait()

    return pl.pallas_call(
        body, name=name, out_shape=jax.ShapeDtypeStruct((N_DEV * m_per, n), v.dtype),
        in_specs=[pl.BlockSpec(memory_space=pltpu.VMEM)], out_specs=pl.BlockSpec(memory_space=pltpu.VMEM),
        scratch_shapes=[pltpu.SemaphoreType.DMA((7,)), pltpu.SemaphoreType.DMA((7,)), pltpu.SemaphoreType.DMA],
    )(v)


def _scatter_chips(src, *, name):
    def body(x_ref, out_ref, send_sems, recv_sems):
        x, y, c = _place()
        k = 2 * x + y
        chips = [(1 - x, y), (x, 1 - y), (1 - x, 1 - y)]
        ids = [2 * cx + cy for cx, cy in chips]

        def copy(j, slot):
            return pltpu.make_async_remote_copy(
                src_ref=x_ref.at[ids[j]], dst_ref=out_ref.at[slot], send_sem=send_sems.at[j], recv_sem=recv_sems.at[j],
                device_id=(*chips[j], c), device_id_type=MESH)

        sends = [copy(j, k) for j in range(3)]
        for cp in sends:
            cp.start()
        for j in range(3):
            copy(j, ids[j]).wait_recv()
        for cp in sends:
            cp.wait_send()

    return pl.pallas_call(
        body, name=name, out_shape=jax.ShapeDtypeStruct(src.shape, src.dtype), in_specs=[ANY], out_specs=ANY,
        scratch_shapes=[pltpu.SemaphoreType.DMA((3,)), pltpu.SemaphoreType.DMA((3,))],
    )(src)


def _row_tile(r, pref=512):
    return max(t for t in range(16, pref + 1, 16) if r % t == 0)


def _gather_weights(src, *, name):
    r = src.shape[0]
    hr = r // 2
    assert r == 2 * hr and hr % 16 == 0

    def body(x_ref, out_ref, send_sems, recv_sems):
        x, y, c = _place()
        k = 2 * x + y
        chips = [(1 - x, y), (x, 1 - y), (1 - x, 1 - y)]
        ids = [2 * cx + cy for cx, cy in chips]
        mine_rows = pl.ds(pl.multiple_of(c * hr, 16), hr)
        other_rows = pl.ds(pl.multiple_of((1 - c) * hr, 16), hr)

        def copy(sem, src_ref, slot, rows, to):
            return pltpu.make_async_remote_copy(
                src_ref=src_ref, dst_ref=out_ref.at[slot, rows], send_sem=send_sems.at[sem], recv_sem=recv_sems.at[sem],
                device_id=to, device_id_type=MESH)

        sends = [copy(j, x_ref.at[mine_rows], k, mine_rows, (cx, cy, c)) for j, (cx, cy) in enumerate(chips)]
        for cp in sends:
            cp.start()
        passed = [copy(3 + j, out_ref.at[ids[j], mine_rows], ids[j], mine_rows, (x, y, 1 - c)) for j in range(3)]
        for j, (cx, cy) in enumerate(chips):
            copy(j, x_ref.at[mine_rows], ids[j], mine_rows, (cx, cy, c)).wait_recv()
            passed[j].start()
        for j in range(3):
            copy(3 + j, out_ref.at[ids[j], other_rows], ids[j], other_rows, (x, y, 1 - c)).wait_recv()
        for cp in sends + passed:
            cp.wait_send()

    return pl.pallas_call(
        body, name=name, out_shape=jax.ShapeDtypeStruct((N_CHIP,) + tuple(src.shape), src.dtype),
        in_specs=[ANY], out_specs=ANY,
        scratch_shapes=[pltpu.SemaphoreType.DMA((6,)), pltpu.SemaphoreType.DMA((6,))],
    )(src)


def _pair_swap(a, *, name):
    n, r, cols = a.shape
    hr = r // 2

    def body(x_ref, out_ref, send_sem, recv_sem):
        x, y, c = _place()
        other_rows = pl.ds(pl.multiple_of((1 - c) * hr, 16), hr)
        cp = pltpu.make_async_remote_copy(src_ref=x_ref.at[:, other_rows], dst_ref=out_ref, send_sem=send_sem,
                                          recv_sem=recv_sem, device_id=(x, y, 1 - c), device_id_type=MESH)
        cp.start()
        cp.wait()

    return pl.pallas_call(
        body, name=name, out_shape=jax.ShapeDtypeStruct((n, hr, cols), a.dtype), in_specs=[ANY], out_specs=ANY,
        scratch_shapes=[pltpu.SemaphoreType.DMA, pltpu.SemaphoreType.DMA],
    )(a)


def _sibling_copy(a, *, name):
    def body(x_ref, out_ref, send_sem, recv_sem):
        x, y, c = _place()
        cp = pltpu.make_async_remote_copy(src_ref=x_ref, dst_ref=out_ref, send_sem=send_sem, recv_sem=recv_sem,
                                          device_id=(x, y, 1 - c), device_id_type=MESH)
        cp.start()
        cp.wait()

    return pl.pallas_call(
        body, name=name, out_shape=jax.ShapeDtypeStruct(a.shape, a.dtype), in_specs=[ANY], out_specs=ANY,
        scratch_shapes=[pltpu.SemaphoreType.DMA, pltpu.SemaphoreType.DMA],
    )(a)


def _sum_slots(a, *, name):
    _, r, c = a.shape
    tr = _row_tile(r, 256)

    def body(a_ref, o_ref):
        acc = a_ref[0].astype(F32)
        for j in range(1, N_CHIP):
            acc = acc + a_ref[j].astype(F32)
        o_ref[...] = acc

    return pl.pallas_call(
        body, name=name, grid=(r // tr,), in_specs=[pl.BlockSpec((N_CHIP, tr, c), lambda i: (0, i, 0))],
        out_specs=pl.BlockSpec((tr, c), lambda i: (i, 0)), out_shape=jax.ShapeDtypeStruct((r, c), F32),
        compiler_params=_cp(("arbitrary",)),
    )(a)


def _add2(a, b, *, name):
    r, c = a.shape
    tr = _row_tile(r)

    def body(a_ref, b_ref, o_ref):
        o_ref[...] = (a_ref[...].astype(F32) + b_ref[...].astype(F32)).astype(o_ref.dtype)

    spec = pl.BlockSpec((tr, c), lambda i: (i, 0))
    return pl.pallas_call(
        body, name=name, grid=(r // tr,), in_specs=[spec, spec], out_specs=spec,
        out_shape=jax.ShapeDtypeStruct((r, c), a.dtype), compiler_params=_cp(("arbitrary",)),
    )(a, b)


BIG = ("w_in", "w_mlp1", "w_attn_out", "w_ssd_out", "w_o", "w_mlp2")
COL_SHARDED = ("w_mlp1", "w_in")
ROW_SHARDED = ("w_attn_out", "w_ssd_out", "w_o", "w_mlp2")
SMALL = ("b_ada", "norm1_w", "norm2_w", "q_norm_w", "k_norm_w", "conv_b", "A_log", "dt_bias", "ssd_D", "ssd_norm_w")
NAMES = ("w_ada", "b_ada", "norm1_w", "norm2_w", "w_in", "q_norm_w", "k_norm_w", "conv_w", "conv_b", "A_log", "dt_bias",
         "ssd_D", "ssd_norm_w", "w_attn_out", "w_ssd_out", "w_o", "w_mlp1", "w_mlp2")
W_IN_COLS = 8768


def _permute_in(w):
    return jnp.concatenate([w[:, 4608:6656], w[:, 6720:8768], w[:, 1536:4608], w[:, 0:1536], w[:, 6656:6720],
                            jnp.zeros((w.shape[0], PW - W_IN_COLS), w.dtype)], axis=1)


def _unpermute_in(wp):
    return jnp.concatenate([wp[:, Q0:DT0], wp[:, XS0:Q0], wp[:, Z0:GA0], wp[:, DT0:DT0 + 64], wp[:, GA0:XS0]], axis=1)


def _pad_to(v, n):
    return jnp.pad(v, (0, n - v.shape[0]))


def _step(w, m, v, loss_target):
    xi, yi, ci = _place()
    chip = 2 * xi + yi
    dev = 4 * xi + 2 * yi + ci
    x, tgt = w["x"], loss_target
    d = x.shape[1]

    cw = w["conv_w"].shape[1]
    v0 = _pad_to(jnp.concatenate([w["c"].reshape(-1), w["conv_w"].reshape(-1)]), 5120).reshape(8, 640)
    g0 = _allgather8(v0, name="ag_cond").reshape(N_DEV, 5120)
    c_all = g0[:, 0:d]
    conv_w = jnp.concatenate([g0[2 * k, d:d + D_CONV * cw].reshape(D_CONV, cw) for k in range(N_CHIP)], axis=1)
    sc = _silu_cast(c_all, name="silu_c")
    modp = _mm(sc, w["w_ada"].astype(MMD), name="ada_fwd", outs=[F32], tm=8, tn=512)
    g1 = _allgather8(modp, name="ag_mod").reshape(N_DEV, N_DEV, modp.shape[1])
    mod_all = jnp.concatenate([g1[2 * k] for k in range(N_CHIP)], axis=1)
    mod = (lax.dynamic_slice_in_dim(mod_all, dev, 1, axis=0) + w["b_ada"]).reshape(6, d)

    full = {}
    for pack, names, axis in (("cols", COL_SHARDED, 1), ("rows", ROW_SHARDED, 0)):
        mine = jnp.concatenate([w[n].astype(MMD) for n in names], axis=axis)
        gath = lax.dynamic_update_slice_in_dim(_gather_weights(mine, name="ag_weights_" + pack), mine[None], chip, axis=0)
        o = 0
        for n in names:
            size = w[n].shape[axis]
            part = gath[:, :, o:o + size] if axis == 1 else gath[:, o:o + size]
            full[n] = (jnp.concatenate([part[k] for k in range(N_CHIP)], axis=1) if axis == 1
                       else part.reshape(N_CHIP * size, w[n].shape[1]))
            o += size
    wts = {n: full[n] for n in BIG if n != "w_in"}
    wts["w_in_p"] = _permute_in(full["w_in"])
    wts["w_dt"] = jnp.pad(full["w_in"][:, 6656:6720], ((0, 0), (0, 64)))
    small = {n: w[n] for n in SMALL if n != "b_ada"}
    small["conv_w"] = conv_w

    loss, grad_x, dmod, gw, gs = _local_step(x, tgt, mod, wts, small)

    gw["w_in"] = _unpermute_in(gw.pop("w_in_p"))
    grads = {}
    for pack, names, axis in (("cols", COL_SHARDED, 1), ("rows", ROW_SHARDED, 0)):
        slots = []
        for k in range(N_CHIP):
            parts = []
            for n in names:
                size = w[n].shape[axis]
                parts.append((gw[n][:, k * size:(k + 1) * size] if axis == 1 else gw[n][k * size:(k + 1) * size]).astype(MMD))
            slots.append(jnp.concatenate(parts, axis=axis))
        slots = jnp.stack(slots)
        _, rows, cols = slots.shape
        hr = rows // 2
        theirs = _pair_swap(slots, name="rs_pair_" + pack)
        ours = lax.dynamic_slice_in_dim(slots, ci * hr, hr, axis=1)
        pair = _add2(ours.reshape(N_CHIP * hr, cols), theirs.reshape(N_CHIP * hr, cols), name="rs_pair_sum_" + pack)
        pair = pair.reshape(N_CHIP, hr, cols)
        recv = _scatter_chips(pair, name="rs_grads_" + pack)
        recv = lax.dynamic_update_slice_in_dim(recv, lax.dynamic_slice_in_dim(pair, chip, 1, axis=0), chip, axis=0)
        half = _sum_slots(recv, name="rs_sum_" + pack)
        other = _sibling_copy(half, name="rs_sibling_" + pack)
        total = jnp.where(ci == 0, jnp.concatenate([half, other], axis=0), jnp.concatenate([other, half], axis=0))
        o = 0
        for n in names:
            size = w[n].shape[axis]
            grads[n] = total[:, o:o + size] if axis == 1 else total[o:o + size]
            o += size

    order = ([dmod.reshape(-1)] + [gs[n].reshape(-1) for n in SMALL if n != "b_ada"] + [gs["conv_w"].reshape(-1)]
             + [loss.reshape(-1)])
    vec = jnp.concatenate(order)
    n_small = vec.shape[0]
    n_pad = -(-n_small // 1024) * 1024
    g2 = _allgather8(_pad_to(vec, n_pad).reshape(8, n_pad // 8), name="ag_small")
    tot = _rows_sum(g2, N_DEV, name="small_sum").reshape(-1)
    loss = tot[n_small - 1]
    dmod_all = g2.reshape(N_DEV, n_pad)[:, 0:6 * d]
    off = 0
    for n in SMALL:
        grads[n] = tot[off:off + w[n].size].reshape(w[n].shape)
        off += w[n].size
    conv_full = tot[off:off + D_CONV * N_CHIP * cw].reshape(D_CONV, N_CHIP * cw)
    grads["conv_w"] = lax.dynamic_slice_in_dim(conv_full, chip * cw, cw, axis=1)
    ada_cols = w["w_ada"].shape[1]
    dmod_mine = lax.dynamic_slice_in_dim(dmod_all, chip * ada_cols, ada_cols, axis=1).astype(MMD)
    grads["w_ada"] = _mm_tn(sc, dmod_mine, name="ada_dw", tk=512, tn=512, tmm=8)

    delta, new_m, new_v = {}, {}, {}
    pack = lambda t: jnp.concatenate([t[n].reshape(-1) for n in SMALL]).reshape(1, -1)
    ds_, ms_, vs_ = _adamw(pack(w), pack(grads), pack(m), pack(v), name="adamw_small")
    off = 0
    for n in SMALL:
        for dst, src in ((delta, ds_), (new_m, ms_), (new_v, vs_)):
            dst[n] = src[0, off:off + w[n].size].reshape(w[n].shape)
        off += w[n].size
    for n in ("w_ada", "conv_w") + BIG:
        delta[n], new_m[n], new_v[n] = _adamw(w[n], grads[n], m[n], v[n], name="adamw_" + n)
    return loss, grad_x, grads, delta, new_m, new_v


def kernel(x, c, w_ada, b_ada, norm1_w, norm2_w, w_in, q_norm_w, k_norm_w, conv_w, conv_b, A_log, dt_bias, ssd_D, ssd_norm_w, w_attn_out, w_ssd_out, w_o, w_mlp1, w_mlp2, loss_target, m_w_ada, m_b_ada, m_norm1_w, m_norm2_w, m_w_in, m_q_norm_w, m_k_norm_w, m_conv_w, m_conv_b, m_A_log, m_dt_bias, m_ssd_D, m_ssd_norm_w, m_w_attn_out, m_w_ssd_out, m_w_o, m_w_mlp1, m_w_mlp2, v_w_ada, v_b_ada, v_norm1_w, v_norm2_w, v_w_in, v_q_norm_w, v_k_norm_w, v_conv_w, v_conv_b, v_A_log, v_dt_bias, v_ssd_D, v_ssd_norm_w, v_w_attn_out, v_w_ssd_out, v_w_o, v_w_mlp1, v_w_mlp2):
    args = dict(locals())
    strip = lambda a: a[0] if a.ndim == 3 else a
    w = {n: strip(args[n]) for n in NAMES + ("x", "c")}
    m = {n: strip(args["m_" + n]) for n in NAMES}
    v = {n: strip(args["v_" + n]) for n in NAMES}
    loss, grad_x, grads, delta, new_m, new_v = _step(w, m, v, loss_target[0])
    like = lambda t, n: t.reshape(args[n].shape)
    return (loss, grad_x[None], *[like(grads[n], n) for n in NAMES], *[like(delta[n], n) for n in NAMES],
            *[like(new_m[n], n) for n in NAMES], *[like(new_v[n], n) for n in NAMES])
```

```python
import functools
import math

import jax
import jax.numpy as jnp
from jax import lax
from jax.experimental import pallas as pl
from jax.experimental.pallas import tpu as pltpu

F32 = jnp.float32
MMD = jnp.bfloat16
EPS = 1e-6
NEG = -1e30
MIB = 1024 * 1024
VMEM_BIG = 56 * MIB
VMEM_MID = 40 * MIB

GRID_W = 64
N_Q_HEADS, N_KV_HEADS, HEAD_DIM = 16, 4, 64
ROPE_THETA = 10000.0
SSD_HEADS, SSD_GROUPS, SSD_P, SSD_N, CHUNK = 32, 4, 64, 128, 128
HPG = SSD_HEADS // SSD_GROUPS
D_CONV = 5
ADAM_LR, ADAM_B1, ADAM_B2, ADAM_EPS, ADAM_WD, ADAM_STEP = 0.001, 0.9, 0.999, 1e-08, 0.01, 10

Z0, GA0, GS0, XS0, B0, C0, Q0, K0, V0, DT0, PW = 0, 2048, 3072, 4096, 6144, 6656, 7168, 8192, 8448, 8704, 8832

MESH = pl.DeviceIdType.MESH
NT = (((1,), (1,)), ((), ()))
TN = (((0,), (0,)), ((), ()))


def _cp(sem=None, vmem=VMEM_MID):
    return pltpu.CompilerParams(dimension_semantics=sem, vmem_limit_bytes=vmem)


def _tile(n, pref):
    t = min(n, pref)
    while n % t:
        t //= 2
    return t


def _dot(a, b, dims=None):
    if dims is None:
        return jnp.dot(a, b, preferred_element_type=F32)
    return lax.dot_general(a, b, dims, preferred_element_type=F32)


def _dot_hi(a, b):
    return jnp.dot(a, b, precision=lax.Precision.HIGHEST, preferred_element_type=F32)


def _sigmoid(x):
    return jax.nn.sigmoid(x)


def _mm(a, b, *, name, outs, nt=False, ta=False, extras=(), epi=None, tm=512, tn=512, n=None, b_outer=False,
        vmem=VMEM_MID):
    assert not (nt and ta)
    k, m = a.shape if ta else a.shape[::-1]
    if n is None:
        n = b.shape[0] if nt else b.shape[1]
    tm, tn = _tile(m, tm), _tile(n, tn)
    gi, gj = m // tm, n // tn
    if b_outer:
        grid = (gj, gi)
        ij = lambda p, q: (q, p)
    else:
        grid = (gi, gj)
        ij = lambda p, q: (p, q)
    if ta:
        a_spec = pl.BlockSpec((k, tm), lambda p, q: (0, ij(p, q)[0]))
    else:
        a_spec = pl.BlockSpec((tm, k), lambda p, q: (ij(p, q)[0], 0))
    if nt:
        b_spec = pl.BlockSpec((tn, k), lambda p, q: (ij(p, q)[1], 0))
    else:
        b_spec = pl.BlockSpec((k, tn), lambda p, q: (0, ij(p, q)[1]))
    e_specs = []
    for arr, kind, off in extras:
        ob = off // tn
        assert off % tn == 0
        if kind == "tile":
            e_specs.append(pl.BlockSpec((tm, tn), lambda p, q, ob=ob: (ij(p, q)[0], ob + ij(p, q)[1])))
        else:
            e_specs.append(pl.BlockSpec((1, tn), lambda p, q, ob=ob: (0, ob + ij(p, q)[1])))
    ne = len(extras)

    def body(a_ref, b_ref, *rest):
        acc = _dot(a_ref[...], b_ref[...], NT if nt else (TN if ta else None))
        res = epi(acc, *[e[...] for e in rest[:ne]]) if epi is not None else (acc,)
        for o_ref, r in zip(rest[ne:], res):
            o_ref[...] = r.astype(o_ref.dtype)

    out = pl.pallas_call(
        body, name=name, grid=grid,
        in_specs=[a_spec, b_spec] + e_specs,
        out_specs=[pl.BlockSpec((tm, tn), lambda p, q: ij(p, q)) for _ in outs],
        out_shape=[jax.ShapeDtypeStruct((m, n), dt) for dt in outs],
        compiler_params=_cp(("arbitrary", "arbitrary"), vmem),
    )(a, b, *[e[0] for e in extras])
    return out if len(outs) > 1 else out[0]


def _mm_tn(a, g, *, name, tk=512, tn=1024, tmm=4096, vmem=VMEM_MID):
    m, k = a.shape
    n = g.shape[1]
    tk, tn, tmm = _tile(k, tk), _tile(n, tn), _tile(m, tmm)

    def body(a_ref, g_ref, o_ref):
        p = _dot(a_ref[...], g_ref[...], TN)

        @pl.when(pl.program_id(2) == 0)
        def _():
            o_ref[...] = p

        @pl.when(pl.program_id(2) > 0)
        def _():
            o_ref[...] += p

    return pl.pallas_call(
        body, name=name, grid=(k // tk, n // tn, m // tmm),
        in_specs=[pl.BlockSpec((tmm, tk), lambda i, j, r: (r, i)), pl.BlockSpec((tmm, tn), lambda i, j, r: (r, j))],
        out_specs=pl.BlockSpec((tk, tn), lambda i, j, r: (i, j)),
        out_shape=jax.ShapeDtypeStruct((k, n), F32),
        compiler_params=_cp(("arbitrary", "arbitrary", "arbitrary"), vmem),
    )(a, g)


def _adamw(w, g, m, v, *, name):
    r, c = w.shape
    tr = _tile(r, 256) if r % 8 == 0 else r

    def body(w_ref, g_ref, m_ref, v_ref, d_ref, nm_ref, nv_ref):
        gg = g_ref[...]
        nm = ADAM_B1 * m_ref[...] + (1.0 - ADAM_B1) * gg
        nv = ADAM_B2 * v_ref[...] + (1.0 - ADAM_B2) * jnp.square(gg)
        m_hat = nm / (1.0 - ADAM_B1 ** ADAM_STEP)
        v_hat = nv / (1.0 - ADAM_B2 ** ADAM_STEP)
        d_ref[...] = -ADAM_LR * (m_hat / (jnp.sqrt(v_hat) + ADAM_EPS) + ADAM_WD * w_ref[...])
        nm_ref[...] = nm
        nv_ref[...] = nv

    spec = pl.BlockSpec((tr, c), lambda i: (i, 0))
    return pl.pallas_call(
        body, name=name, grid=(r // tr,), in_specs=[spec] * 4, out_specs=[spec] * 3,
        out_shape=[jax.ShapeDtypeStruct((r, c), F32)] * 3, compiler_params=_cp(("arbitrary",)),
    )(w, g, m, v)


def _rows_sum(a, groups, *, name):
    r = a.shape[0] // groups

    def body(a_ref, o_ref):
        acc = a_ref[0:r, :]
        for d in range(1, groups):
            acc = acc + a_ref[d * r:(d + 1) * r, :]
        o_ref[...] = acc

    return pl.pallas_call(body, name=name, out_shape=jax.ShapeDtypeStruct((r, a.shape[1]), F32))(a)


def _silu_cast(a, *, name):
    def body(a_ref, o_ref):
        x = a_ref[...]
        o_ref[...] = (x * _sigmoid(x)).astype(o_ref.dtype)

    return pl.pallas_call(body, name=name, out_shape=jax.ShapeDtypeStruct(a.shape, MMD))(a)


def _sumsq(a, *, name):
    m, n = a.shape
    tm = _tile(m, 512)

    def body(a_ref, o_ref):
        x = a_ref[...]
        p = jnp.sum(jnp.sum(x * x, axis=1, keepdims=True), axis=0, keepdims=True)

        @pl.when(pl.program_id(0) == 0)
        def _():
            o_ref[...] = p

        @pl.when(pl.program_id(0) > 0)
        def _():
            o_ref[...] += p

    return pl.pallas_call(
        body, name=name, grid=(m // tm,), in_specs=[pl.BlockSpec((tm, n), lambda i: (i, 0))],
        out_specs=pl.BlockSpec((1, 1), lambda i: (0, 0)), out_shape=jax.ShapeDtypeStruct((1, 1), F32),
        compiler_params=_cp(("arbitrary",)),
    )(a)


def _acc_rows(o_ref, p, first):
    @pl.when(first)
    def _():
        o_ref[...] = p

    @pl.when(jnp.logical_not(first))
    def _():
        o_ref[...] += p


def _ln_mod(x, w, scale, shift, *, name):
    s, d = x.shape
    tm = _tile(s, 512)

    def body(x_ref, w_ref, sc_ref, sh_ref, o_ref):
        xv = x_ref[...]
        r = lax.rsqrt(jnp.mean(xv * xv, axis=-1, keepdims=True) + EPS)
        o_ref[...] = ((xv * r) * w_ref[...] * (1.0 + sc_ref[...]) + sh_ref[...]).astype(o_ref.dtype)

    row = pl.BlockSpec((1, d), lambda i: (0, 0))
    big = pl.BlockSpec((tm, d), lambda i: (i, 0))
    return pl.pallas_call(
        body, name=name, grid=(s // tm,), in_specs=[big, row, row, row], out_specs=big,
        out_shape=jax.ShapeDtypeStruct((s, d), MMD), compiler_params=_cp(("arbitrary",)),
    )(x, w, scale, shift)


def _ln_mod_bwd(dh, x, w, scale, dres, *, name):
    s, d = x.shape
    tm = _tile(s, 512)

    def body(dh_ref, x_ref, w_ref, sc_ref, dres_ref, dx_ref, dsh_ref, dsc_ref, dw_ref):
        xv = x_ref[...]
        dhv = dh_ref[...].astype(F32)
        r = lax.rsqrt(jnp.mean(xv * xv, axis=-1, keepdims=True) + EPS)
        nv = xv * r
        wv = w_ref[...]
        g1 = 1.0 + sc_ref[...]
        dn = dhv * (wv * g1)
        dx_ref[...] = dres_ref[...] + r * (dn - nv * jnp.mean(dn * nv, axis=-1, keepdims=True))
        first = pl.program_id(0) == 0
        _acc_rows(dsh_ref, jnp.sum(dhv, axis=0, keepdims=True), first)
        _acc_rows(dsc_ref, jnp.sum(dhv * nv * wv, axis=0, keepdims=True), first)
        _acc_rows(dw_ref, jnp.sum(dhv * nv * g1, axis=0, keepdims=True), first)

    row = pl.BlockSpec((1, d), lambda i: (0, 0))
    big = pl.BlockSpec((tm, d), lambda i: (i, 0))
    return pl.pallas_call(
        body, name=name, grid=(s // tm,), in_specs=[big, big, row, row, big], out_specs=[big, row, row, row],
        out_shape=[jax.ShapeDtypeStruct((s, d), F32)] + [jax.ShapeDtypeStruct((1, d), F32)] * 3,
        compiler_params=_cp(("arbitrary",)),
    )(dh, x, w, scale, dres)


def _gate_bwd(dy, u, gate, *, name):
    s, d = dy.shape
    tm = _tile(s, 512)

    def body(dy_ref, u_ref, g_ref, du_ref, dg_ref):
        dyv = dy_ref[...]
        du_ref[...] = (dyv * g_ref[...]).astype(du_ref.dtype)
        _acc_rows(dg_ref, jnp.sum(dyv * u_ref[...].astype(F32), axis=0, keepdims=True), pl.program_id(0) == 0)

    row = pl.BlockSpec((1, d), lambda i: (0, 0))
    big = pl.BlockSpec((tm, d), lambda i: (i, 0))
    return pl.pallas_call(
        body, name=name, grid=(s // tm,), in_specs=[big, big, row], out_specs=[big, row],
        out_shape=[jax.ShapeDtypeStruct((s, d), MMD), jax.ShapeDtypeStruct((1, d), F32)],
        compiler_params=_cp(("arbitrary",)),
    )(dy, u, gate)


def _seg64(v, e):
    hi = v.astype(jnp.bfloat16)
    lo = (v - hi.astype(F32)).astype(jnp.bfloat16)
    return _dot(hi, e) + _dot(lo, e)


def _rope_tables(s):
    rows = s // GRID_W
    pos_row = jnp.repeat(jnp.arange(rows, dtype=jnp.int32), GRID_W).astype(F32)
    pos_col = jnp.tile(jnp.arange(GRID_W, dtype=jnp.int32), rows).astype(F32)
    axis_dim = HEAD_DIM // 2
    inv_freq = ROPE_THETA ** (-jnp.arange(0, axis_dim, 2, dtype=F32) / axis_dim)
    ang_r = pos_row[:, None] * inv_freq[None, :]
    ang_c = pos_col[:, None] * inv_freq[None, :]
    zero = jnp.zeros_like(ang_r)
    cos = jnp.concatenate([jnp.cos(ang_r), jnp.cos(ang_r), jnp.cos(ang_c), jnp.cos(ang_c)], axis=1)
    s_a = jnp.concatenate([-jnp.sin(ang_r), zero, -jnp.sin(ang_c), zero], axis=1)
    s_b = jnp.concatenate([zero, jnp.sin(ang_r), zero, jnp.sin(ang_c)], axis=1)
    return [jnp.tile(t, (1, 2)) for t in (cos, s_a, s_b)]


def _e128():
    i = jnp.arange(128)
    return (i[:, None] // 64 == i[None, :] // 64).astype(jnp.bfloat16)


QKW = N_Q_HEADS * HEAD_DIM + N_KV_HEADS * HEAD_DIM


def _qk_fwd(proj, wrow, scrow, tabs, *, name):
    s = proj.shape[0]
    tm = _tile(s, 512)

    def body(x_ref, w_ref, sc_ref, cos_ref, sa_ref, sb_ref, e_ref, o_ref, ot_ref):
        u = x_ref[...].astype(F32)
        r = lax.rsqrt(_seg64(u * u, e_ref[...]) * (1.0 / HEAD_DIM) + EPS)
        nv = (u * r) * w_ref[...]
        ro = nv * cos_ref[...] + pltpu.roll(nv, 112, 1) * sa_ref[...] + pltpu.roll(nv, 16, 1) * sb_ref[...]
        out = ro * sc_ref[...]
        o_ref[...] = out.astype(o_ref.dtype)
        ot_ref[...] = out.T.astype(ot_ref.dtype)

    tab = pl.BlockSpec((tm, 128), lambda i, j: (i, 0))
    row = pl.BlockSpec((1, 128), lambda i, j: (0, j))
    return pl.pallas_call(
        body, name=name, grid=(s // tm, QKW // 128),
        in_specs=[pl.BlockSpec((tm, 128), lambda i, j: (i, Q0 // 128 + j)), row, row, tab, tab, tab,
                  pl.BlockSpec((128, 128), lambda i, j: (0, 0))],
        out_specs=[pl.BlockSpec((tm, 128), lambda i, j: (i, j)), pl.BlockSpec((128, tm), lambda i, j: (j, i))],
        out_shape=[jax.ShapeDtypeStruct((s, QKW), MMD), jax.ShapeDtypeStruct((QKW, s), MMD)],
        compiler_params=_cp(("arbitrary", "arbitrary")),
    )(proj, wrow, scrow, *tabs, _e128())


def _qk_bwd(dqkt, proj, wrow, scrow, tabs, *, name):
    s = proj.shape[0]
    tm = _tile(s, 512)

    def body(d_ref, x_ref, w_ref, sc_ref, cos_ref, sa_ref, sb_ref, e_ref, du_ref, dw_ref):
        e = e_ref[...]
        d = d_ref[...].T * sc_ref[...]
        dn = d * cos_ref[...] + pltpu.roll(d * sa_ref[...], 16, 1) + pltpu.roll(d * sb_ref[...], 112, 1)
        u = x_ref[...].astype(F32)
        r = lax.rsqrt(_seg64(u * u, e) * (1.0 / HEAD_DIM) + EPS)
        uh = u * r
        _acc_rows(dw_ref, jnp.sum(dn * uh, axis=0, keepdims=True), pl.program_id(1) == 0)
        dnw = dn * w_ref[...]
        du_ref[...] = (r * (dnw - uh * (_seg64(dnw * uh, e) * (1.0 / HEAD_DIM)))).astype(du_ref.dtype)

    tab = pl.BlockSpec((tm, 128), lambda j, i: (i, 0))
    row = pl.BlockSpec((1, 128), lambda j, i: (0, j))
    return pl.pallas_call(
        body, name=name, grid=(QKW // 128, s // tm),
        in_specs=[pl.BlockSpec((128, tm), lambda j, i: (j, i)), pl.BlockSpec((tm, 128), lambda j, i: (i, Q0 // 128 + j)),
                  row, row, tab, tab, tab, pl.BlockSpec((128, 128), lambda j, i: (0, 0))],
        out_specs=[pl.BlockSpec((tm, 128), lambda j, i: (i, j)), row],
        out_shape=[jax.ShapeDtypeStruct((s, QKW), MMD), jax.ShapeDtypeStruct((1, QKW), F32)],
        compiler_params=_cp(("arbitrary", "arbitrary")),
    )(dqkt, proj, wrow, scrow, *tabs, _e128())


REP = N_Q_HEADS // N_KV_HEADS


def _lanes(ref):
    return jnp.concatenate([ref[r] for r in range(REP)], axis=1)


V_AUG = HEAD_DIM + 8
LOG2E = math.log2(math.e)


def _flash_fwd(qkt, vta, *, name):
    s = qkt.shape[2]
    tq, tk = _tile(s, 1024), _tile(s, 512)
    nk = s // tk
    lanes = REP * tq

    def body(q_ref, k_ref, v_ref, o_ref, lse_ref, m_ref, acc_ref):
        j = pl.program_id(2)

        @pl.when(j == 0)
        def _():
            m_ref[...] = jnp.full_like(m_ref, NEG)
            acc_ref[...] = jnp.zeros_like(acc_ref)

        st = _dot(k_ref[0], _lanes(q_ref), TN)
        m_prev = m_ref[...]
        m_new = jnp.maximum(m_prev, jnp.max(st, axis=0, keepdims=True))
        p = jnp.exp2(st - m_new).astype(MMD)
        acc_ref[...] = jnp.exp2(m_prev - m_new) * acc_ref[...] + _dot(v_ref[0], p)
        m_ref[...] = m_new

        @pl.when(j == nk - 1)
        def _():
            acc = acc_ref[...]
            l = acc[HEAD_DIM:HEAD_DIM + 1]
            o = acc[0:HEAD_DIM] / l
            ls = m_ref[...] + jnp.log(l) * LOG2E
            for r in range(REP):
                o_ref[r] = o[:, r * tq:(r + 1) * tq].astype(o_ref.dtype)
                lse_ref[r] = ls[:, r * tq:(r + 1) * tq]

    qspec = pl.BlockSpec((REP, HEAD_DIM, tq), lambda g, i, j: (g, 0, i))
    return pl.pallas_call(
        body, name=name, grid=(N_KV_HEADS, s // tq, nk),
        in_specs=[qspec, pl.BlockSpec((1, HEAD_DIM, tk), lambda g, i, j: (N_Q_HEADS + g, 0, j)),
                  pl.BlockSpec((1, V_AUG, tk), lambda g, i, j: (g, 0, j))],
        out_specs=[qspec, pl.BlockSpec((REP, 1, tq), lambda g, i, j: (g, 0, i))],
        out_shape=[jax.ShapeDtypeStruct((N_Q_HEADS, HEAD_DIM, s), MMD), jax.ShapeDtypeStruct((N_Q_HEADS, 1, s), F32)],
        scratch_shapes=[pltpu.VMEM((1, lanes), F32), pltpu.VMEM((V_AUG, lanes), F32)],
        compiler_params=_cp(("arbitrary", "arbitrary", "arbitrary"), VMEM_BIG),
    )(qkt, qkt, vta)


def _flash_bwd(qkt, k_h, v_h, dot, ot, lse, *, name):
    s = qkt.shape[2]
    tq, tk = _tile(s, 512), _tile(s, 1024)
    nk = s // tk

    def body(q_ref, kt_ref, k_ref, v_ref, do_ref, o_ref, lse_ref, dq_ref, dk_ref, dv_ref, dq_acc):
        i, j = pl.program_id(1), pl.program_id(2)
        q, do = _lanes(q_ref), _lanes(do_ref)
        delta = jnp.sum(do.astype(F32) * _lanes(o_ref).astype(F32), axis=0, keepdims=True)
        k, v = k_ref[0], v_ref[0]
        p = jnp.exp2(_dot(k, q) - _lanes(lse_ref))
        dvc = _dot(p.astype(MMD), do, NT)
        ds = (p * (_dot(v, do) - delta)).astype(MMD)
        dkc = _dot(ds, q, NT) * (1.0 / LOG2E)
        dqc = _dot(kt_ref[0], ds)
        rows = pl.ds(pl.multiple_of(j * tk, tk), tk)

        @pl.when(i == 0)
        def _():
            dk_ref[0, rows, :] = dkc
            dv_ref[0, rows, :] = dvc

        @pl.when(i > 0)
        def _():
            dk_ref[0, rows, :] += dkc
            dv_ref[0, rows, :] += dvc

        @pl.when(j == 0)
        def _():
            dq_acc[...] = dqc

        @pl.when(j > 0)
        def _():
            dq_acc[...] += dqc

        @pl.when(j == nk - 1)
        def _():
            acc = dq_acc[...]
            for r in range(REP):
                dq_ref[r] = acc[:, r * tq:(r + 1) * tq]

    qspec = pl.BlockSpec((REP, HEAD_DIM, tq), lambda g, i, j: (g, 0, i))
    kvin = pl.BlockSpec((1, tk, HEAD_DIM), lambda g, i, j: (g, j, 0))
    kvres = pl.BlockSpec((1, s, HEAD_DIM), lambda g, i, j: (g, 0, 0))
    return pl.pallas_call(
        body, name=name, grid=(N_KV_HEADS, s // tq, nk),
        in_specs=[qspec, pl.BlockSpec((1, HEAD_DIM, tk), lambda g, i, j: (N_Q_HEADS + g, 0, j)), kvin, kvin,
                  qspec, qspec, pl.BlockSpec((REP, 1, tq), lambda g, i, j: (g, 0, i))],
        out_specs=[qspec, kvres, kvres],
        out_shape=[jax.ShapeDtypeStruct((N_Q_HEADS, HEAD_DIM, s), F32), jax.ShapeDtypeStruct((N_KV_HEADS, s, HEAD_DIM), F32),
                   jax.ShapeDtypeStruct((N_KV_HEADS, s, HEAD_DIM), F32)],
        scratch_shapes=[pltpu.VMEM((HEAD_DIM, REP * tq), F32)],
        compiler_params=_cp(("arbitrary", "arbitrary", "arbitrary"), VMEM_BIG),
    )(qkt, qkt, k_h, v_h, dot, ot, lse)


HALO = 8
CONV_W = 2048 + 2 * SSD_GROUPS * SSD_N


def _shifted(win, off, r):
    return pltpu.roll(win, (r + 2 * HALO - off) % (r + 2 * HALO), 0)[0:r]


def _conv_fwd(proj, w8, brow, *, name):
    s = proj.shape[0]
    cb = 256
    r = _tile(s, 512)

    def body(x_ref, w_ref, b_ref, o_ref, pad_ref):
        zeros = jnp.zeros((HALO, cb), F32)
        pad_ref[0:HALO, :] = zeros
        pad_ref[s + HALO:s + 2 * HALO, :] = zeros

        def fill(i, carry):
            st = pl.multiple_of(i * r, r)
            pad_ref[pl.ds(st + HALO, r), :] = x_ref[pl.ds(st, r), :].astype(F32)
            return carry

        lax.fori_loop(0, s // r, fill, 0)
        wv = w_ref[...]
        bv = b_ref[...]

        def step(i, carry):
            st = pl.multiple_of(i * r, r)
            win = pad_ref[pl.ds(st, r + 2 * HALO), :]
            acc = bv + wv[0:1, :] * _shifted(win, HALO - 2, r)
            for t in range(1, D_CONV):
                acc = acc + wv[t:t + 1, :] * _shifted(win, HALO - 2 + t, r)
            o_ref[pl.ds(st, r), :] = (acc * _sigmoid(acc)).astype(o_ref.dtype)
            return carry

        lax.fori_loop(0, s // r, step, 0)

    return pl.pallas_call(
        body, name=name, grid=(CONV_W // cb,),
        in_specs=[pl.BlockSpec((s, cb), lambda j: (0, XS0 // cb + j)), pl.BlockSpec((8, cb), lambda j: (0, j)),
                  pl.BlockSpec((1, cb), lambda j: (0, j))],
        out_specs=pl.BlockSpec((s, cb), lambda j: (0, j)),
        out_shape=jax.ShapeDtypeStruct((s, CONV_W), MMD),
        scratch_shapes=[pltpu.VMEM((s + 2 * HALO, cb), F32)],
        compiler_params=_cp(("arbitrary",), VMEM_MID),
    )(proj, w8, brow)


def _conv_bwd(proj, col0, ga, gb, w8, brow, *, name):
    s = proj.shape[0]
    width = ga.shape[1]
    cb = 128
    c0 = col0 // cb
    r = _tile(s, 512)

    def body(x_ref, ga_ref, gb_ref, w_ref, b_ref, dx_ref, dw_ref, db_ref, xpad, dpad):
        zeros = jnp.zeros((HALO, cb), F32)
        for ref in (xpad, dpad):
            ref[0:HALO, :] = zeros
            ref[s + HALO:s + 2 * HALO, :] = zeros

        def fill(i, carry):
            st = pl.multiple_of(i * r, r)
            xpad[pl.ds(st + HALO, r), :] = x_ref[pl.ds(st, r), :].astype(F32)
            return carry

        lax.fori_loop(0, s // r, fill, 0)
        wv = w_ref[...]
        bv = b_ref[...]

        def first(i, carry):
            st = pl.multiple_of(i * r, r)
            win = xpad[pl.ds(st, r + 2 * HALO), :]
            taps = [_shifted(win, HALO - 2 + t, r) for t in range(D_CONV)]
            u = bv
            for t in range(D_CONV):
                u = u + wv[t:t + 1, :] * taps[t]
            sg = _sigmoid(u)
            du = (ga_ref[pl.ds(st, r), :] + gb_ref[pl.ds(st, r), :]) * (sg * (1.0 + u * (1.0 - sg)))
            dpad[pl.ds(st + HALO, r), :] = du
            out = [carry[0] + jnp.sum(du, axis=0, keepdims=True)]
            for t in range(D_CONV):
                out.append(carry[1 + t] + jnp.sum(du * taps[t], axis=0, keepdims=True))
            return tuple(out)

        sums = lax.fori_loop(0, s // r, first, tuple(jnp.zeros((1, cb), F32) for _ in range(1 + D_CONV)))
        db_ref[...] = sums[0]
        for t in range(D_CONV):
            dw_ref[t:t + 1, :] = sums[1 + t]
        dw_ref[D_CONV:8, :] = jnp.zeros((8 - D_CONV, cb), F32)

        def second(i, carry):
            st = pl.multiple_of(i * r, r)
            win = dpad[pl.ds(st, r + 2 * HALO), :]
            acc = wv[0:1, :] * _shifted(win, HALO + 2, r)
            for t in range(1, D_CONV):
                acc = acc + wv[t:t + 1, :] * _shifted(win, HALO + 2 - t, r)
            dx_ref[pl.ds(st, r), :] = acc.astype(dx_ref.dtype)
            return carry

        lax.fori_loop(0, s // r, second, 0)

    col = pl.BlockSpec((s, cb), lambda j: (0, j))
    return pl.pallas_call(
        body, name=name, grid=(width // cb,),
        in_specs=[pl.BlockSpec((s, cb), lambda j: (0, XS0 // cb + c0 + j)), col, col,
                  pl.BlockSpec((8, cb), lambda j: (0, c0 + j)), pl.BlockSpec((1, cb), lambda j: (0, c0 + j))],
        out_specs=[col, pl.BlockSpec((8, cb), lambda j: (0, j)), pl.BlockSpec((1, cb), lambda j: (0, j))],
        out_shape=[jax.ShapeDtypeStruct((s, width), MMD), jax.ShapeDtypeStruct((8, width), F32),
                   jax.ShapeDtypeStruct((1, width), F32)],
        scratch_shapes=[pltpu.VMEM((s + 2 * HALO, cb), F32), pltpu.VMEM((s + 2 * HALO, cb), F32)],
        compiler_params=_cp(("arbitrary",), VMEM_BIG),
    )(proj, ga, gb, w8, brow)


def _tri(lower):
    i = jnp.arange(CHUNK)
    return ((i[:, None] >= i[None, :]) if lower else (i[:, None] <= i[None, :])).astype(F32)


def _dt_fwd(raw, bias, arow, *, name):
    s = raw.shape[0]

    def body(r_ref, b_ref, a_ref, lo_ref, up_ref, dt_ref, cs_ref):
        u = r_ref[...] + b_ref[...]
        dt = jnp.maximum(u, 0.0) + jnp.log1p(jnp.exp(-jnp.abs(u)))
        dt_ref[...] = dt
        a = dt * a_ref[...]
        lane = lax.broadcasted_iota(jnp.int32, (CHUNK, 128), 1)
        cs_ref[...] = jnp.where(lane < SSD_HEADS, _dot_hi(lo_ref[...], a), _dot_hi(up_ref[...], a))

    blk = pl.BlockSpec((CHUNK, 128), lambda i: (i, 0))
    row = pl.BlockSpec((1, 128), lambda i: (0, 0))
    tri = pl.BlockSpec((CHUNK, CHUNK), lambda i: (0, 0))
    return pl.pallas_call(
        body, name=name, grid=(s // CHUNK,), in_specs=[blk, row, row, tri, tri], out_specs=[blk, blk],
        out_shape=[jax.ShapeDtypeStruct((s, 128), F32)] * 2, compiler_params=_cp(("arbitrary",)),
    )(raw, bias, arow, _tri(True), _tri(False))


def _dt_bwd(ddt, raw, bias, *, name):
    s = raw.shape[0]
    tm = _tile(s, 1024)

    def body(d_ref, r_ref, b_ref, o_ref, db_ref):
        g = d_ref[...] * _sigmoid(r_ref[...] + b_ref[...])
        o_ref[...] = g.astype(o_ref.dtype)
        _acc_rows(db_ref, jnp.sum(g, axis=0, keepdims=True), pl.program_id(0) == 0)

    blk = pl.BlockSpec((tm, 128), lambda i: (i, 0))
    row = pl.BlockSpec((1, 128), lambda i: (0, 0))
    return pl.pallas_call(
        body, name=name, grid=(s // tm,), in_specs=[blk, blk, row], out_specs=[blk, row],
        out_shape=[jax.ShapeDtypeStruct((s, 128), MMD), jax.ShapeDtypeStruct((1, 128), F32)],
        compiler_params=_cp(("arbitrary",)),
    )(ddt, raw, bias)


GW = HPG * SSD_P


GPS = 4


def _ssd_specs(nc, rev):
    cc = (lambda c: nc - 1 - c) if rev else (lambda c: c)
    nb = SSD_GROUPS // GPS
    return dict(
        x=pl.BlockSpec((CHUNK, GPS * GW), lambda g, c: (cc(c), g)),
        b=pl.BlockSpec((CHUNK, GPS * SSD_N), lambda g, c: (cc(c), 2048 // (GPS * SSD_N) + g)),
        c=pl.BlockSpec((CHUNK, GPS * SSD_N), lambda g, c: (cc(c), 2048 // (GPS * SSD_N) + nb + g)),
        col=pl.BlockSpec((GPS, CHUNK, HPG), lambda g, c: (g, cc(c), 0)),
        lanes=pl.BlockSpec((CHUNK, 128), lambda g, c: (cc(c), 0)),
        rowt=pl.BlockSpec((GPS, 1, HPG, CHUNK), lambda g, c: (g, cc(c), 0, 0)),
        drow=pl.BlockSpec((1, GPS * GW), lambda g, c: (0, g)),
        y=pl.BlockSpec((CHUNK, GPS * GW), lambda g, c: (cc(c), g)),
        h=pl.BlockSpec((GPS, 1, SSD_N, GW), lambda g, c: (g, cc(c), 0, 0)),
        n=pl.BlockSpec((CHUNK, GPS * SSD_N), lambda g, c: (cc(c), g)),
    )


def _ssd_mask(anti):
    ii = lax.broadcasted_iota(jnp.int32, (CHUNK, CHUNK), 0)
    jj = lax.broadcasted_iota(jnp.int32, (CHUNK, CHUNK), 1)
    return ii, jj, (ii <= jj) if anti else (ii >= jj)


def _expand(x, ex):
    h1 = x.astype(jnp.bfloat16)
    r1 = x - h1.astype(F32)
    h2 = r1.astype(jnp.bfloat16)
    h3 = (r1 - h2.astype(F32)).astype(jnp.bfloat16)
    return _dot(h1, ex) + _dot(h2, ex) + _dot(h3, ex)


def _headsum(a, e):
    hi = a.astype(jnp.bfloat16)
    return _dot(hi, e) + _dot((a - hi.astype(F32)).astype(jnp.bfloat16), e)


def _expand_mats():
    lane = jnp.arange(128)[None, :, None]
    col = jnp.arange(GW)[None, None, :]
    base = (jnp.arange(2)[:, None] * SSD_HEADS + jnp.arange(SSD_GROUPS)[None, :] * HPG).reshape(2 * SSD_GROUPS, 1, 1)
    return (lane == base + col // SSD_P).astype(jnp.bfloat16)


def _headsum_mats():
    e1 = (jnp.arange(GW)[:, None] // SSD_P == jnp.arange(128)[None, :]).astype(jnp.bfloat16)
    e2 = (jnp.arange(HPG * CHUNK)[:, None] // CHUNK == jnp.arange(128)[None, :]).astype(jnp.bfloat16)
    return e1, e2


def _ssd_fwd(xc, dt, cs, cst, ex, drow, di, *, name):
    s = xc.shape[0]
    nc = s // CHUNK
    anti = di == 1
    sp = _ssd_specs(nc, anti)
    trow = 0 if anti else CHUNK - 1

    def body(x_ref, b_ref, c_ref, dt_ref, cs_ref, cst_ref, ex_ref, d_ref, y_ref, hp_ref, h_ref):
        @pl.when(pl.program_id(1) == 0)
        def _():
            h_ref[...] = jnp.zeros_like(h_ref)

        mask = _ssd_mask(anti)[2]
        dtv, csv = dt_ref[...], cs_ref[...]
        for gi in range(GPS):
            cols = slice(gi * GW, (gi + 1) * GW)
            ncols = slice(gi * SSD_N, (gi + 1) * SSD_N)
            ex = ex_ref[gi]
            xb = x_ref[:, cols].astype(F32)
            bm, cm = b_ref[:, ncols], c_ref[:, ncols]
            csr = cst_ref[gi, 0]
            dtf = _expand(dtv, ex)
            csf = _expand(csv, ex)
            tl = csf[trow:trow + 1, :]
            h = h_ref[gi]
            hp_ref[gi, 0] = h
            g = _dot(cm, bm, NT)
            xs = xb * dtf
            xsm = xs.astype(MMD)
            base = jnp.exp(csf) * _dot(cm, h.astype(MMD)) + d_ref[:, cols] * xb
            for r in range(HPG):
                sl = slice(r * SSD_P, (r + 1) * SSD_P)
                lm = jnp.exp(jnp.where(mask, csf[:, r * SSD_P:r * SSD_P + 1] - csr[r:r + 1, :], NEG))
                y_ref[:, gi * GW + r * SSD_P:gi * GW + (r + 1) * SSD_P] = _dot((g * lm).astype(MMD), xsm[:, sl]) + base[:, sl]
            xd = (xs * jnp.exp(tl - csf)).astype(MMD)
            h_ref[gi] = h * jnp.exp(tl) + _dot(bm, xd, TN)

    nb = SSD_GROUPS // GPS
    return pl.pallas_call(
        body, name=name, grid=(nb, nc),
        in_specs=[sp["x"], sp["b"], sp["c"], sp["lanes"], sp["lanes"], sp["rowt"],
                  pl.BlockSpec((GPS, 128, GW), lambda g, c: (di * nb + g, 0, 0)), sp["drow"]],
        out_specs=[sp["y"], sp["h"]],
        out_shape=[jax.ShapeDtypeStruct((s, 2048), F32), jax.ShapeDtypeStruct((SSD_GROUPS, nc, SSD_N, GW), F32)],
        scratch_shapes=[pltpu.VMEM((GPS, SSD_N, GW), F32)],
        compiler_params=_cp(("arbitrary", "arbitrary")),
    )(xc, xc, xc, dt, cs, cst, ex, drow)


def _ssd_bwd(xc, dt, cs, dt4, cst, ex, drow, arow4, dy, hprev, di, *, name):
    s = xc.shape[0]
    nc = s // CHUNK
    anti = di == 1
    sp = _ssd_specs(nc, not anti)
    trow = 0 if anti else CHUNK - 1
    e1, e2 = _headsum_mats()

    def body(x_ref, b_ref, c_ref, dt_ref, cs_ref, dt4_ref, cst_ref, ex_ref, d_ref, a_ref, dy_ref, hp_ref, tri_ref,
             e1_ref, e2_ref, dx_ref, db_ref, dc_ref, ddt_ref, da_ref, dh_ref, w_ref, dxs_ref):
        @pl.when(pl.program_id(1) == 0)
        def _():
            dh_ref[...] = jnp.zeros_like(dh_ref)
            da_ref[...] = jnp.zeros_like(da_ref)

        e1v = e1_ref[...]
        ii, _, mask = _ssd_mask(anti)
        dtv, csv = dt_ref[...], cs_ref[...]
        for gi in range(GPS):
            cols = slice(gi * GW, (gi + 1) * GW)
            ncols = slice(gi * SSD_N, (gi + 1) * SSD_N)
            ex = ex_ref[gi]
            xb = x_ref[:, cols].astype(F32)
            bm, cm = b_ref[:, ncols], c_ref[:, ncols]
            csr = cst_ref[gi, 0]
            dyb = dy_ref[:, cols]
            dym = dyb.astype(MMD)
            hp = hp_ref[gi, 0]
            hpm = hp.astype(MMD)
            dh = dh_ref[gi]
            dhm = dh.astype(MMD)
            dtf = _expand(dtv, ex)
            csf = _expand(csv, ex)
            tl = csf[trow:trow + 1, :]
            e = jnp.exp(csf)
            dec = jnp.exp(tl - csf)
            et = jnp.exp(tl)
            xs = xb * dtf
            xsm = xs.astype(MMD)
            g = _dot(cm, bm, NT)
            z = _dot(cm, hpm)
            bdh = _dot(bm, dhm)
            dg = jnp.zeros((CHUNK, CHUNK), F32)
            wcols = jnp.zeros((CHUNK, CHUNK), F32)
            for r in range(HPG):
                sl = slice(r * SSD_P, (r + 1) * SSD_P)
                lm = jnp.exp(jnp.where(mask, csf[:, r * SSD_P:r * SSD_P + 1] - csr[r:r + 1, :], NEG))
                mm = g * lm
                dm = _dot(dym[:, sl], xsm[:, sl], NT)
                w = dm * mm
                w_ref[gi, :, r * CHUNK:(r + 1) * CHUNK] = w
                wcols = jnp.where(ii == r, jnp.sum(w, axis=0, keepdims=True), wcols)
                dg = dg + dm * lm
                dxs_ref[gi, :, sl] = _dot(mm.astype(MMD), dym[:, sl], TN)
            dxs = dxs_ref[gi] + dec * bdh
            dx_ref[:, cols] = dxs * dtf + d_ref[:, cols] * dyb
            tb = xs * bdh * dec
            d_tot = jnp.sum(tb, axis=0, keepdims=True) + et * jnp.sum(dh * hp, axis=0, keepdims=True)
            d_tot = _headsum(jnp.broadcast_to(d_tot, (8, GW)), e1v)[0:1]
            dcs = (_headsum(dyb * (e * z) - tb, e1v) + _headsum(w_ref[gi], e2_ref[...]) - wcols.T
                   + jnp.where(ii == trow, d_tot, 0.0))
            da = _dot_hi(tri_ref[...], dcs)
            ddt_ref[gi] = (da * a_ref[gi] + _headsum(dxs * xb, e1v))[:, 0:HPG]
            da_ref[gi] += jnp.sum(da[:, 0:HPG] * dt4_ref[gi], axis=0, keepdims=True)
            dgm = dg.astype(MMD)
            dz = (e * dyb).astype(MMD)
            dc_ref[:, ncols] = _dot(dgm, bm) + _dot(dz, hpm, NT)
            db_ref[:, ncols] = _dot(dgm, cm, TN) + _dot((xs * dec).astype(MMD), dhm, NT)
            dh_ref[gi] = dh * et + _dot(cm, dz, TN)

    nb = SSD_GROUPS // GPS
    const = lambda shape: pl.BlockSpec(shape, lambda g, c: (0,) * len(shape))
    return pl.pallas_call(
        body, name=name, grid=(nb, nc),
        in_specs=[sp["x"], sp["b"], sp["c"], sp["lanes"], sp["lanes"], sp["col"], sp["rowt"],
                  pl.BlockSpec((GPS, 128, GW), lambda g, c: (di * nb + g, 0, 0)), sp["drow"],
                  pl.BlockSpec((GPS, 1, 128), lambda g, c: (g, 0, 0)), sp["y"], sp["h"],
                  const((CHUNK, CHUNK)), const((GW, 128)), const((HPG * CHUNK, 128))],
        out_specs=[sp["y"], sp["n"], sp["n"], sp["col"], pl.BlockSpec((GPS, 1, HPG), lambda g, c: (g, 0, 0))],
        out_shape=[jax.ShapeDtypeStruct((s, 2048), F32), jax.ShapeDtypeStruct((s, SSD_GROUPS * SSD_N), F32),
                   jax.ShapeDtypeStruct((s, SSD_GROUPS * SSD_N), F32), jax.ShapeDtypeStruct((SSD_GROUPS, s, HPG), F32),
                   jax.ShapeDtypeStruct((SSD_GROUPS, 1, HPG), F32)],
        scratch_shapes=[pltpu.VMEM((GPS, SSD_N, GW), F32), pltpu.VMEM((GPS, CHUNK, HPG * CHUNK), F32),
                        pltpu.VMEM((GPS, CHUNK, GW), F32)],
        compiler_params=_cp(("arbitrary", "arbitrary")),
    )(xc, xc, xc, dt, cs, dt4, cst, ex, drow, arow4, dy, hprev, _tri(anti), e1, e2)


def _gnorm_fwd(ya, yb, proj, w, *, name):
    s = ya.shape[0]
    tm = _tile(s, 256)

    def body(a_ref, b_ref, z_ref, w_ref, o_ref):
        zv = z_ref[...].astype(F32)
        t = (a_ref[...] + b_ref[...]) * (zv * _sigmoid(zv))
        r = lax.rsqrt(jnp.mean(t * t, axis=-1, keepdims=True) + EPS)
        o_ref[...] = ((t * r) * w_ref[...]).astype(o_ref.dtype)

    big = pl.BlockSpec((tm, 2048), lambda i: (i, 0))
    row = pl.BlockSpec((1, 2048), lambda i: (0, 0))
    return pl.pallas_call(
        body, name=name, grid=(s // tm,), in_specs=[big, big, big, row], out_specs=big,
        out_shape=jax.ShapeDtypeStruct((s, 2048), MMD), compiler_params=_cp(("arbitrary",)),
    )(ya, yb, proj, w)


def _gnorm_bwd(dout, ya, yb, proj, w, *, name):
    s = ya.shape[0]
    tm = _tile(s, 256)

    def body(do_ref, a_ref, b_ref, z_ref, w_ref, dy_ref, dz_ref, dw_ref):
        zv = z_ref[...].astype(F32)
        sg = _sigmoid(zv)
        sz = zv * sg
        y = a_ref[...] + b_ref[...]
        t = y * sz
        r = lax.rsqrt(jnp.mean(t * t, axis=-1, keepdims=True) + EPS)
        nv = t * r
        dov = do_ref[...].astype(F32)
        _acc_rows(dw_ref, jnp.sum(dov * nv, axis=0, keepdims=True), pl.program_id(0) == 0)
        dn = dov * w_ref[...]
        dt_ = r * (dn - nv * jnp.mean(dn * nv, axis=-1, keepdims=True))
        dy_ref[...] = dt_ * sz
        dz_ref[...] = (dt_ * y * (sg * (1.0 + zv * (1.0 - sg)))).astype(dz_ref.dtype)

    big = pl.BlockSpec((tm, 2048), lambda i: (i, 0))
    row = pl.BlockSpec((1, 2048), lambda i: (0, 0))
    return pl.pallas_call(
        body, name=name, grid=(s // tm,), in_specs=[big, big, big, big, row], out_specs=[big, big, row],
        out_shape=[jax.ShapeDtypeStruct((s, 2048), F32), jax.ShapeDtypeStruct((s, 2048), MMD),
                   jax.ShapeDtypeStruct((1, 2048), F32)],
        compiler_params=_cp(("arbitrary",)),
    )(dout, ya, yb, proj, w)


def _colsum_prod(a, b, *, name):
    s, n = a.shape
    tm = _tile(s, 256)

    def body(a_ref, b_ref, o_ref):
        _acc_rows(o_ref, jnp.sum(a_ref[...].astype(F32) * b_ref[...].astype(F32), axis=0, keepdims=True),
                  pl.program_id(0) == 0)

    big = pl.BlockSpec((tm, n), lambda i: (i, 0))
    return pl.pallas_call(
        body, name=name, grid=(s // tm,), in_specs=[big, big], out_specs=pl.BlockSpec((1, n), lambda i: (0, 0)),
        out_shape=jax.ShapeDtypeStruct((1, n), F32), compiler_params=_cp(("arbitrary",)),
    )(a, b)


def _heads(a, n):
    return a.reshape(a.shape[0], n, HEAD_DIM).transpose(1, 0, 2)


def _unheads(a):
    return a.transpose(1, 0, 2).reshape(a.shape[1], a.shape[0] * HEAD_DIM)


def _per_group(a):
    return a.reshape(a.shape[0], SSD_GROUPS, HPG).transpose(1, 0, 2)


def _per_group_t(a):
    s = a.shape[0]
    return a.reshape(s // CHUNK, CHUNK, SSD_GROUPS, HPG).transpose(2, 0, 3, 1)


def _local_step(x, target, mod, wts, small, late_weights=None, late_grads=None):
    s, d = x.shape
    shift1, scale1, gate1, shift2, scale2, gate2 = [mod[i:i + 1] for i in range(6)]

    h1 = _ln_mod(x, small["norm1_w"], scale1, shift1, name="ln1")
    proj = _mm(h1, wts["w_in_p"], name="in_proj", outs=[MMD], tm=512, tn=2944, b_outer=True)
    dt_raw = _mm(h1, wts["w_dt"], name="dt_proj", outs=[F32], tm=512, tn=128)

    qk_w = jnp.concatenate([jnp.tile(small["q_norm_w"], (1, N_Q_HEADS)), jnp.tile(small["k_norm_w"], (1, N_KV_HEADS))], axis=1)
    qk_sc = jnp.concatenate([jnp.full((1, N_Q_HEADS * HEAD_DIM), HEAD_DIM ** -0.5, F32),
                             jnp.ones((1, N_KV_HEADS * HEAD_DIM), F32)], axis=1)
    qk_sc2 = jnp.concatenate([jnp.full((1, N_Q_HEADS * HEAD_DIM), HEAD_DIM ** -0.5 * LOG2E, F32),
                              jnp.ones((1, N_KV_HEADS * HEAD_DIM), F32)], axis=1)
    tabs = _rope_tables(s)
    qk, qkt = _qk_fwd(proj, qk_w, qk_sc2, tabs, name="qk_fwd")
    qkt = qkt.reshape(N_Q_HEADS + N_KV_HEADS, HEAD_DIM, s)
    k_h = _heads(qk[:, N_Q_HEADS * HEAD_DIM:], N_KV_HEADS)
    v_sd = proj[:, V0:V0 + N_KV_HEADS * HEAD_DIM]
    v_h = _heads(v_sd, N_KV_HEADS)
    vta = jnp.concatenate([v_sd.T.reshape(N_KV_HEADS, HEAD_DIM, s), jnp.ones((N_KV_HEADS, V_AUG - HEAD_DIM, s), MMD)], axis=1)
    ot, lse = _flash_fwd(qkt, vta, name="flash_fwd")
    ot2 = ot.reshape(N_Q_HEADS * HEAD_DIM, s)
    if late_weights is not None:
        wts = {**wts, **late_weights(ot)}

    w8 = jnp.pad(small["conv_w"], ((0, 8 - D_CONV), (0, 0)))
    xc = _conv_fwd(proj, w8, small["conv_b"], name="conv_fwd")
    a_neg = -jnp.exp(small["A_log"])
    arow = jnp.pad(a_neg.reshape(1, 2 * SSD_HEADS), ((0, 0), (0, 128 - 2 * SSD_HEADS)))
    bias_row = jnp.pad(small["dt_bias"].reshape(1, 2 * SSD_HEADS), ((0, 0), (0, 128 - 2 * SSD_HEADS)))
    dt, cs = _dt_fwd(dt_raw, bias_row, arow, name="dt_fwd")
    drow = jnp.repeat(small["ssd_D"], SSD_P, axis=1)
    dirs = []
    for di in range(2):
        cols = slice(di * SSD_HEADS, (di + 1) * SSD_HEADS)
        dirs.append(dict(
            dt4=_per_group(dt[:, cols]), cst=_per_group_t(cs[:, cols]),
            drow=drow if di == 0 else jnp.zeros_like(drow),
            arow4=jnp.pad(a_neg[di].reshape(SSD_GROUPS, 1, HPG), ((0, 0), (0, 0), (0, 128 - HPG)))))
    ex = _expand_mats()
    ys = []
    for di, dd in enumerate(dirs):
        y, dd["hprev"] = _ssd_fwd(xc, dt, cs, dd["cst"], ex, dd["drow"], di, name=f"ssd_fwd{di}")
        ys.append(y)
    ssdn = _gnorm_fwd(ys[0], ys[1], proj, small["ssd_norm_w"], name="gnorm_fwd")

    a_o = _mm(ot2, wts["w_attn_out"], name="attn_out", outs=[MMD], ta=True, tm=512, tn=1024)

    def merge_epi(acc, ao, ga, gs):
        return (_sigmoid(ga.astype(F32)) * ao.astype(F32) + _sigmoid(gs.astype(F32)) * acc, acc)

    merged, b_o = _mm(ssdn, wts["w_ssd_out"], name="ssd_out", outs=[MMD, MMD], tm=512, tn=512,
                      extras=[(a_o, "tile", 0), (proj, "tile", GA0), (proj, "tile", GS0)], epi=merge_epi)

    def res_epi(acc, res, gate):
        return (res + gate * acc, acc)

    x1, mo = _mm(merged, wts["w_o"], name="w_o", outs=[F32, MMD], tm=512, tn=512,
                 extras=[(x, "tile", 0), (gate1, "row", 0)], epi=res_epi)
    h2 = _ln_mod(x1, small["norm2_w"], scale2, shift2, name="ln2")

    def relu2_epi(acc):
        rl = jnp.maximum(acc, 0.0)
        return (rl * rl, rl)

    act, rl = _mm(h2, wts["w_mlp1"], name="mlp1", outs=[MMD, MMD], tm=512, tn=1024, epi=relu2_epi, b_outer=True)

    def loss_epi(acc, res, gate, tgt):
        return ((res + gate * acc - tgt) * (1.0 / d), acc)

    dy, ffo = _mm(act, wts["w_mlp2"], name="mlp2", outs=[F32, MMD], tm=512, tn=1024, vmem=VMEM_BIG,
                  extras=[(x1, "tile", 0), (gate2, "row", 0), (target, "tile", 0)], epi=loss_epi)
    loss = _sumsq(dy, name="loss") * (0.5 * d)

    gw = {}
    gs_ = {}
    dffo, dgate2 = _gate_bwd(dy, ffo, gate2, name="gate2_bwd")
    dpre = _mm(dffo, wts["w_mlp2"], name="mlp2_dx", outs=[MMD], nt=True, tm=512, tn=1024,
               extras=[(rl, "tile", 0)], epi=lambda acc, r: (acc * (2.0 * r.astype(F32)),))
    gw["w_mlp2"] = _mm_tn(act, dffo, name="mlp2_dw")
    dh2 = _mm(dpre, wts["w_mlp1"], name="mlp1_dx", outs=[F32], nt=True, tm=512, tn=1024)
    gw["w_mlp1"] = _mm_tn(h2, dpre, name="mlp1_dw")
    dx1, dshift2, dscale2, gs_["norm2_w"] = _ln_mod_bwd(dh2, x1, small["norm2_w"], scale2, dy, name="ln2_bwd")
    dmo, dgate1 = _gate_bwd(dx1, mo, gate1, name="gate1_bwd")

    def merge_bwd_epi(acc, ao, bo, ga, gs):
        sa, ss = _sigmoid(ga.astype(F32)), _sigmoid(gs.astype(F32))
        return (acc * sa, acc * ss, acc * ao.astype(F32) * sa * (1.0 - sa), acc * bo.astype(F32) * ss * (1.0 - ss))

    da_o, db_o, dga, dgs = _mm(dmo, wts["w_o"], name="w_o_dx", outs=[MMD] * 4, nt=True, tm=512, tn=512,
                               extras=[(a_o, "tile", 0), (b_o, "tile", 0), (proj, "tile", GA0), (proj, "tile", GS0)],
                               epi=merge_bwd_epi)
    gw["w_o"] = _mm_tn(merged, dmo, name="w_o_dw")
    dot = _mm(wts["w_attn_out"], da_o, name="attn_out_dx", outs=[MMD], nt=True, tm=512, tn=512)
    gw["w_attn_out"] = _mm(ot2, da_o, name="attn_out_dw", outs=[F32], tm=256, tn=512, vmem=VMEM_BIG)
    dssdn = _mm(db_o, wts["w_ssd_out"], name="ssd_out_dx", outs=[MMD], nt=True, tm=512, tn=512)
    gw["w_ssd_out"] = _mm_tn(ssdn, db_o, name="ssd_out_dw")

    norm_w = small["ssd_norm_w"] if late_grads is None else small["ssd_norm_w"] + late_grads(gw)
    dyssd, dz, gs_["ssd_norm_w"] = _gnorm_bwd(dssdn, ys[0], ys[1], proj, norm_w, name="gnorm_bwd")
    gs_["ssd_D"] = _colsum_prod(dyssd, xc[:, 0:2048], name="ssd_d_grad").reshape(SSD_HEADS, SSD_P).sum(axis=1).reshape(1, SSD_HEADS)
    dxc, ddts, das = [], [], []
    for di, dd in enumerate(dirs):
        dxs, dbm, dcm, ddt4, da4 = _ssd_bwd(xc, dt, cs, dd["dt4"], dd["cst"], ex, dd["drow"], dd["arow4"],
                                            dyssd, dd["hprev"], di, name=f"ssd_bwd{di}")
        dxc.append((dxs, dbm, dcm))
        ddts.append(ddt4.transpose(1, 0, 2).reshape(s, SSD_HEADS))
        das.append(da4.reshape(1, SSD_HEADS))
    conv_parts, col0 = [], 0
    for part, (ga, gb) in enumerate(zip(*dxc)):
        conv_parts.append(_conv_bwd(proj, col0, ga, gb, w8, small["conv_b"], name=f"conv_bwd{part}"))
        col0 += ga.shape[1]
    dxbc, dw8, gs_["conv_b"] = [jnp.concatenate(t, axis=1) for t in zip(*conv_parts)]
    gs_["conv_w"] = dw8[0:D_CONV]
    gs_["A_log"] = jnp.concatenate(das, axis=0) * a_neg
    ddt = jnp.pad(jnp.concatenate(ddts, axis=1), ((0, 0), (0, 128 - 2 * SSD_HEADS)))
    ddt_raw, dbias = _dt_bwd(ddt, dt_raw, bias_row, name="dt_bwd")
    gs_["dt_bias"] = dbias[:, 0:2 * SSD_HEADS].reshape(2, SSD_HEADS)

    dqt, dk_h, dv_h = _flash_bwd(qkt, k_h, v_h, dot.reshape(N_Q_HEADS, HEAD_DIM, s), ot, lse, name="flash_bwd")
    dqkt = jnp.concatenate([dqt.reshape(N_Q_HEADS * HEAD_DIM, s),
                            dk_h.transpose(0, 2, 1).reshape(N_KV_HEADS * HEAD_DIM, s)], axis=0)
    dqk_u, dqk_w = _qk_bwd(dqkt, proj, qk_w, qk_sc, tabs, name="qk_bwd")
    gs_["q_norm_w"] = dqk_w[:, 0:N_Q_HEADS * HEAD_DIM].reshape(N_Q_HEADS, HEAD_DIM).sum(axis=0, keepdims=True)
    gs_["k_norm_w"] = dqk_w[:, N_Q_HEADS * HEAD_DIM:].reshape(N_KV_HEADS, HEAD_DIM).sum(axis=0, keepdims=True)
    dv = _unheads(dv_h).astype(MMD)

    dproj = jnp.concatenate([dz, dga, dgs, dxbc, dqk_u, dv, ddt_raw], axis=1)
    dh1 = _mm(dproj, wts["w_in_p"], name="in_proj_dx", outs=[F32], nt=True, tm=256, tn=1024, vmem=VMEM_BIG)
    gw["w_in_p"] = _mm_tn(h1, dproj, name="in_proj_dw", tk=512, tn=2944, tmm=2048, vmem=VMEM_BIG)
    grad_x, dshift1, dscale1, gs_["norm1_w"] = _ln_mod_bwd(dh1, x, small["norm1_w"], scale1, dx1, name="ln1_bwd")
    dmod = jnp.concatenate([dshift1, dscale1, dgate1, dshift2, dscale2, dgate2], axis=0)
    return loss, grad_x, dmod, gw, gs_


N_DEV = 8
N_CHIP = 4
ANY = pl.BlockSpec(memory_space=pl.ANY)


def _place():
    return lax.axis_index("x"), lax.axis_index("y"), lax.axis_index("c")


def _allgather8(v, *, name):
    m_per, n = v.shape

    def body(x_ref, out_ref, send_sems, recv_sems, local_sem):
        x, y, c = _place()
        me, sibling = (x, y, c), (x, y, 1 - c)
        chips = [(1 - x, y), (x, 1 - y), (1 - x, 1 - y)]

        def rows(px, py, pc):
            return out_ref.at[pl.ds((4 * px + 2 * py + pc) * m_per, m_per), :]

        def copy(k, block, to, src=None):
            return pltpu.make_async_remote_copy(
                src_ref=rows(*block) if src is None else src, dst_ref=rows(*block),
                send_sem=send_sems.at[k], recv_sem=recv_sems.at[k], device_id=to, device_id_type=MESH)

        mine = pltpu.make_async_copy(x_ref, rows(*me), local_sem)
        mine.start()
        first = [copy(0, me, sibling, src=x_ref)]
        first += [copy(1 + j, me, (*chip, c), src=x_ref) for j, chip in enumerate(chips)]
        for cp in first:
            cp.start()
        passed = [copy(4 + j, (*chip, c), sibling) for j, chip in enumerate(chips)]
        for j, chip in enumerate(chips):
            copy(1 + j, (*chip, c), me).wait_recv()
            passed[j].start()
        copy(0, sibling, me).wait_recv()
        for j, chip in enumerate(chips):
            copy(4 + j, (*chip, 1 - c), me).wait_recv()
        for cp in first + passed:
            cp.wait_send()
        mine.wait()

    return pl.pallas_call(
        body, name=name, out_shape=jax.ShapeDtypeStruct((N_DEV * m_per, n), v.dtype),
        in_specs=[pl.BlockSpec(memory_space=pltpu.VMEM)], out_specs=pl.BlockSpec(memory_space=pltpu.VMEM),
        scratch_shapes=[pltpu.SemaphoreType.DMA((7,)), pltpu.SemaphoreType.DMA((7,)), pltpu.SemaphoreType.DMA],
    )(v)


def _scatter_chips(src, *, name):
    def body(x_ref, out_ref, send_sems, recv_sems):
        x, y, c = _place()
        k = 2 * x + y
        chips = [(1 - x, y), (x, 1 - y), (1 - x, 1 - y)]
        ids = [2 * cx + cy for cx, cy in chips]

        def copy(j, slot):
            return pltpu.make_async_remote_copy(
                src_ref=x_ref.at[ids[j]], dst_ref=out_ref.at[slot], send_sem=send_sems.at[j], recv_sem=recv_sems.at[j],
                device_id=(*chips[j], c), device_id_type=MESH)

        sends = [copy(j, k) for j in range(3)]
        for cp in sends:
            cp.start()
        for j in range(3):
            copy(j, ids[j]).wait_recv()
        for cp in sends:
            cp.wait_send()

    return pl.pallas_call(
        body, name=name, out_shape=jax.ShapeDtypeStruct(src.shape, src.dtype), in_specs=[ANY], out_specs=ANY,
        scratch_shapes=[pltpu.SemaphoreType.DMA((3,)), pltpu.SemaphoreType.DMA((3,))],
    )(src)


HBM = pl.BlockSpec(memory_space=pltpu.HBM)
SEM = pl.BlockSpec(memory_space=pltpu.SEMAPHORE)


def _chips_copies(x_ref, land_ref, sems, scatter):
    x, y, c = _place()
    k = 2 * x + y
    chips = [(1 - x, y), (x, 1 - y), (1 - x, 1 - y)]
    ids = [2 * cx + cy for cx, cy in chips]

    def copy(j, slot):
        return pltpu.make_async_remote_copy(
            src_ref=x_ref.at[ids[j]] if scatter else x_ref, dst_ref=land_ref.at[slot], send_sem=sems[j],
            recv_sem=sems[3 + j], device_id=(*chips[j], c), device_id_type=MESH)

    return [copy(j, k) for j in range(3)], [copy(j, ids[j]) for j in range(3)]


def _chips_start(src, scatter, *, name):
    shape = src.shape if scatter else (N_CHIP,) + tuple(src.shape)

    def body(x_ref, land_ref, *rest):
        sems, token = rest[0:6], rest[8]
        for cp in _chips_copies(x_ref, land_ref, sems, scatter)[0]:
            cp.start()
        token[...] = jnp.zeros_like(token)

    out = pl.pallas_call(
        body, name=name,
        out_shape=(pltpu.SemaphoreType.DMA(()),) * 6 + (pltpu.HBM(src.shape, src.dtype), pltpu.HBM(shape, src.dtype),
                                                       jax.ShapeDtypeStruct((8, 128), F32)),
        in_specs=(HBM, HBM), out_specs=(SEM,) * 6 + (HBM, HBM, pl.BlockSpec(memory_space=pltpu.VMEM)),
        input_output_aliases={0: 6, 1: 7},
        compiler_params=pltpu.CompilerParams(has_side_effects=pltpu.SideEffectType.DATAFLOW_SIDE_EFFECTING),
    )(pltpu.with_memory_space_constraint(src, pltpu.HBM),
      pltpu.with_memory_space_constraint(lax.empty(shape, src.dtype), pltpu.HBM))
    return out[0:6], out[6], out[7], out[8]


def _chips_wait(sems, src, land, after, scatter, *, name):
    def body(x_ref, land_ref, *rest):
        sems_ = rest[0:6]
        for cp in _chips_copies(x_ref, land_ref, sems_, scatter)[1]:
            cp.wait_send()
            cp.wait_recv()

    return pl.pallas_call(
        body, name=name, out_shape=(pltpu.HBM(src.shape, src.dtype), pltpu.HBM(land.shape, land.dtype)),
        in_specs=(HBM, HBM) + (SEM,) * 6 + (ANY,), out_specs=(HBM, HBM), input_output_aliases={0: 0, 1: 1},
        compiler_params=pltpu.CompilerParams(has_side_effects=pltpu.SideEffectType.DATAFLOW_SIDE_EFFECTING),
    )(src, land, *sems, after)


def _row_tile(r, pref=512):
    return max(t for t in range(16, pref + 1, 16) if r % t == 0)


def _gather_weights(src, *, name):
    r = src.shape[0]
    hr = r // 2
    assert r == 2 * hr and hr % 16 == 0

    def body(x_ref, out_ref, send_sems, recv_sems):
        x, y, c = _place()
        k = 2 * x + y
        chips = [(1 - x, y), (x, 1 - y), (1 - x, 1 - y)]
        ids = [2 * cx + cy for cx, cy in chips]
        mine_rows = pl.ds(pl.multiple_of(c * hr, 16), hr)
        other_rows = pl.ds(pl.multiple_of((1 - c) * hr, 16), hr)

        def copy(sem, src_ref, slot, rows, to):
            return pltpu.make_async_remote_copy(
                src_ref=src_ref, dst_ref=out_ref.at[slot, rows], send_sem=send_sems.at[sem], recv_sem=recv_sems.at[sem],
                device_id=to, device_id_type=MESH)

        sends = [copy(j, x_ref.at[mine_rows], k, mine_rows, (cx, cy, c)) for j, (cx, cy) in enumerate(chips)]
        for cp in sends:
            cp.start()
        passed = [copy(3 + j, out_ref.at[ids[j], mine_rows], ids[j], mine_rows, (x, y, 1 - c)) for j in range(3)]
        for j, (cx, cy) in enumerate(chips):
            copy(j, x_ref.at[mine_rows], ids[j], mine_rows, (cx, cy, c)).wait_recv()
            passed[j].start()
        for j in range(3):
            copy(3 + j, out_ref.at[ids[j], other_rows], ids[j], other_rows, (x, y, 1 - c)).wait_recv()
        for cp in sends + passed:
            cp.wait_send()

    return pl.pallas_call(
        body, name=name, out_shape=jax.ShapeDtypeStruct((N_CHIP,) + tuple(src.shape), src.dtype),
        in_specs=[ANY], out_specs=ANY,
        scratch_shapes=[pltpu.SemaphoreType.DMA((6,)), pltpu.SemaphoreType.DMA((6,))],
    )(src)


def _pair_swap(a, *, name):
    n, r, cols = a.shape
    hr = r // 2

    def body(x_ref, out_ref, send_sem, recv_sem):
        x, y, c = _place()
        other_rows = pl.ds(pl.multiple_of((1 - c) * hr, 16), hr)
        cp = pltpu.make_async_remote_copy(src_ref=x_ref.at[:, other_rows], dst_ref=out_ref, send_sem=send_sem,
                                          recv_sem=recv_sem, device_id=(x, y, 1 - c), device_id_type=MESH)
        cp.start()
        cp.wait()

    return pl.pallas_call(
        body, name=name, out_shape=jax.ShapeDtypeStruct((n, hr, cols), a.dtype), in_specs=[ANY], out_specs=ANY,
        scratch_shapes=[pltpu.SemaphoreType.DMA, pltpu.SemaphoreType.DMA],
    )(a)


def _sibling_copy(a, *, name):
    def body(x_ref, out_ref, send_sem, recv_sem):
        x, y, c = _place()
        cp = pltpu.make_async_remote_copy(src_ref=x_ref, dst_ref=out_ref, send_sem=send_sem, recv_sem=recv_sem,
                                          device_id=(x, y, 1 - c), device_id_type=MESH)
        cp.start()
        cp.wait()

    return pl.pallas_call(
        body, name=name, out_shape=jax.ShapeDtypeStruct(a.shape, a.dtype), in_specs=[ANY], out_specs=ANY,
        scratch_shapes=[pltpu.SemaphoreType.DMA, pltpu.SemaphoreType.DMA],
    )(a)


def _sum_slots(a, *, name):
    _, r, c = a.shape
    tr = _row_tile(r, 256)

    def body(a_ref, o_ref):
        acc = a_ref[0].astype(F32)
        for j in range(1, N_CHIP):
            acc = acc + a_ref[j].astype(F32)
        o_ref[...] = acc

    return pl.pallas_call(
        body, name=name, grid=(r // tr,), in_specs=[pl.BlockSpec((N_CHIP, tr, c), lambda i: (0, i, 0))],
        out_specs=pl.BlockSpec((tr, c), lambda i: (i, 0)), out_shape=jax.ShapeDtypeStruct((r, c), F32),
        compiler_params=_cp(("arbitrary",)),
    )(a)


def _add2(a, b, *, name):
    r, c = a.shape
    tr = _row_tile(r)

    def body(a_ref, b_ref, o_ref):
        o_ref[...] = (a_ref[...].astype(F32) + b_ref[...].astype(F32)).astype(o_ref.dtype)

    spec = pl.BlockSpec((tr, c), lambda i: (i, 0))
    return pl.pallas_call(
        body, name=name, grid=(r // tr,), in_specs=[spec, spec], out_specs=spec,
        out_shape=jax.ShapeDtypeStruct((r, c), a.dtype), compiler_params=_cp(("arbitrary",)),
    )(a, b)


BIG = ("w_in", "w_mlp1", "w_attn_out", "w_ssd_out", "w_o", "w_mlp2")
COL_SHARDED = ("w_mlp1", "w_in")
ROW_SHARDED = ("w_attn_out", "w_ssd_out", "w_o", "w_mlp2")
LATE = ROW_SHARDED + ("w_mlp1",)
SMALL = ("b_ada", "norm1_w", "norm2_w", "q_norm_w", "k_norm_w", "conv_b", "A_log", "dt_bias", "ssd_D", "ssd_norm_w")
NAMES = ("w_ada", "b_ada", "norm1_w", "norm2_w", "w_in", "q_norm_w", "k_norm_w", "conv_w", "conv_b", "A_log", "dt_bias",
         "ssd_D", "ssd_norm_w", "w_attn_out", "w_ssd_out", "w_o", "w_mlp1", "w_mlp2")
W_IN_COLS = 8768


def _permute_in(w):
    return jnp.concatenate([w[:, 4608:6656], w[:, 6720:8768], w[:, 1536:4608], w[:, 0:1536], w[:, 6656:6720],
                            jnp.zeros((w.shape[0], PW - W_IN_COLS), w.dtype)], axis=1)


def _unpermute_in(wp):
    return jnp.concatenate([wp[:, Q0:DT0], wp[:, XS0:Q0], wp[:, Z0:GA0], wp[:, DT0:DT0 + 64], wp[:, GA0:XS0]], axis=1)


def _pad_to(v, n):
    return jnp.pad(v, (0, n - v.shape[0]))


def _step(w, m, v, loss_target):
    xi, yi, ci = _place()
    chip = 2 * xi + yi
    dev = 4 * xi + 2 * yi + ci
    x, tgt = w["x"], loss_target
    d = x.shape[1]

    late_mine = jnp.concatenate([w[n].astype(MMD) for n in LATE], axis=0)
    ag_sems, ag_src, ag_land, ag_token = _chips_start(late_mine, False, name="ag_late_start")

    cw = w["conv_w"].shape[1]
    v0 = _pad_to(jnp.concatenate([w["c"].reshape(-1), w["conv_w"].reshape(-1)]), 5120).reshape(8, 640) + ag_token[0, 0]
    g0 = _allgather8(v0, name="ag_cond").reshape(N_DEV, 5120)
    c_all = g0[:, 0:d]
    conv_w = jnp.concatenate([g0[2 * k, d:d + D_CONV * cw].reshape(D_CONV, cw) for k in range(N_CHIP)], axis=1)
    sc = _silu_cast(c_all, name="silu_c")
    modp = _mm(sc, w["w_ada"].astype(MMD), name="ada_fwd", outs=[F32], tm=8, tn=512)
    g1 = _allgather8(modp, name="ag_mod").reshape(N_DEV, N_DEV, modp.shape[1])
    mod_all = jnp.concatenate([g1[2 * k] for k in range(N_CHIP)], axis=1)
    mod = (lax.dynamic_slice_in_dim(mod_all, dev, 1, axis=0) + w["b_ada"]).reshape(6, d)

    mine = w["w_in"].astype(MMD)
    gath = lax.dynamic_update_slice_in_dim(_gather_weights(mine, name="ag_w_in"), mine[None], chip, axis=0)
    w_in = jnp.concatenate([gath[k] for k in range(N_CHIP)], axis=1)
    wts = {"w_in_p": _permute_in(w_in), "w_dt": jnp.pad(w_in[:, 6656:6720], ((0, 0), (0, 64)))}
    small = {n: w[n] for n in SMALL if n != "b_ada"}
    small["conv_w"] = conv_w

    def own_slot(land, src):
        return lax.dynamic_update_slice_in_dim(land, src, chip, axis=0)

    def late_weights(after):
        src, land = _chips_wait(ag_sems, ag_src, ag_land, after, False, name="ag_late_wait")
        land = own_slot(land, src[None])
        out, o = {}, 0
        for n in LATE:
            rows = w[n].shape[0]
            part = land[:, o:o + rows]
            out[n] = (jnp.concatenate([part[k] for k in range(N_CHIP)], axis=1) if n in COL_SHARDED
                      else part.reshape(N_CHIP * rows, w[n].shape[1]))
            o += rows
        return out

    def pair_sums(slots, tag):
        _, rows, cols = slots.shape
        hr = rows // 2
        theirs = _pair_swap(slots, name="rs_pair_" + tag)
        ours = lax.dynamic_slice_in_dim(slots, ci * hr, hr, axis=1)
        pair = _add2(ours.reshape(N_CHIP * hr, cols), theirs.reshape(N_CHIP * hr, cols), name="rs_pair_sum_" + tag)
        return pair.reshape(N_CHIP, hr, cols)

    def finish(recv, pair, tag):
        recv = own_slot(recv, lax.dynamic_slice_in_dim(pair, chip, 1, axis=0))
        half = _sum_slots(recv, name="rs_sum_" + tag)
        other = _sibling_copy(half, name="rs_sibling_" + tag)
        return jnp.where(ci == 0, jnp.concatenate([half, other], axis=0), jnp.concatenate([other, half], axis=0))

    started = {}

    def late_grads(gw):
        slots = []
        for k in range(N_CHIP):
            parts = []
            for n in LATE:
                rows = w[n].shape[0]
                blk = gw[n][:, k * rows:(k + 1) * rows] if n in COL_SHARDED else gw[n][k * rows:(k + 1) * rows]
                parts.append(blk.astype(MMD))
            slots.append(jnp.concatenate(parts, axis=0))
        pair = pair_sums(jnp.stack(slots), "late")
        sems, src, land, token = _chips_start(pair, True, name="rs_late_start")
        started["late"] = (sems, src, land)
        return token[0:1, 0:1]

    loss, grad_x, dmod, gw, gs = _local_step(x, tgt, mod, wts, small, late_weights, late_grads)

    grads = {}
    g_in = _unpermute_in(gw.pop("w_in_p"))
    cols_in = w["w_in"].shape[1]
    pair = pair_sums(jnp.stack([g_in[:, k * cols_in:(k + 1) * cols_in].astype(MMD) for k in range(N_CHIP)]), "w_in")
    grads["w_in"] = finish(_scatter_chips(pair, name="rs_w_in"), pair, "w_in")
    pair, land = _chips_wait(*started["late"], grad_x, True, name="rs_late_wait")
    total, o = finish(land, pair, "late"), 0
    for n in LATE:
        rows = w[n].shape[0]
        grads[n] = total[o:o + rows]
        o += rows

    order = ([dmod.reshape(-1)] + [gs[n].reshape(-1) for n in SMALL if n != "b_ada"] + [gs["conv_w"].reshape(-1)]
             + [loss.reshape(-1)])
    vec = jnp.concatenate(order)
    n_small = vec.shape[0]
    n_pad = -(-n_small // 1024) * 1024
    g2 = _allgather8(_pad_to(vec, n_pad).reshape(8, n_pad // 8), name="ag_small")
    tot = _rows_sum(g2, N_DEV, name="small_sum").reshape(-1)
    loss = tot[n_small - 1]
    dmod_all = g2.reshape(N_DEV, n_pad)[:, 0:6 * d]
    off = 0
    for n in SMALL:
        grads[n] = tot[off:off + w[n].size].reshape(w[n].shape)
        off += w[n].size
    conv_full = tot[off:off + D_CONV * N_CHIP * cw].reshape(D_CONV, N_CHIP * cw)
    grads["conv_w"] = lax.dynamic_slice_in_dim(conv_full, chip * cw, cw, axis=1)
    ada_cols = w["w_ada"].shape[1]
    dmod_mine = lax.dynamic_slice_in_dim(dmod_all, chip * ada_cols, ada_cols, axis=1).astype(MMD)
    grads["w_ada"] = _mm_tn(sc, dmod_mine, name="ada_dw", tk=512, tn=512, tmm=8)

    delta, new_m, new_v = {}, {}, {}
    pack = lambda t: jnp.concatenate([t[n].reshape(-1) for n in SMALL]).reshape(1, -1)
    ds_, ms_, vs_ = _adamw(pack(w), pack(grads), pack(m), pack(v), name="adamw_small")
    off = 0
    for n in SMALL:
        for dst, src in ((delta, ds_), (new_m, ms_), (new_v, vs_)):
            dst[n] = src[0, off:off + w[n].size].reshape(w[n].shape)
        off += w[n].size
    for n in ("w_ada", "conv_w") + BIG:
        delta[n], new_m[n], new_v[n] = _adamw(w[n], grads[n], m[n], v[n], name="adamw_" + n)
    return loss, grad_x, grads, delta, new_m, new_v


def kernel(x, c, w_ada, b_ada, norm1_w, norm2_w, w_in, q_norm_w, k_norm_w, conv_w, conv_b, A_log, dt_bias, ssd_D, ssd_norm_w, w_attn_out, w_ssd_out, w_o, w_mlp1, w_mlp2, loss_target, m_w_ada, m_b_ada, m_norm1_w, m_norm2_w, m_w_in, m_q_norm_w, m_k_norm_w, m_conv_w, m_conv_b, m_A_log, m_dt_bias, m_ssd_D, m_ssd_norm_w, m_w_attn_out, m_w_ssd_out, m_w_o, m_w_mlp1, m_w_mlp2, v_w_ada, v_b_ada, v_norm1_w, v_norm2_w, v_w_in, v_q_norm_w, v_k_norm_w, v_conv_w, v_conv_b, v_A_log, v_dt_bias, v_ssd_D, v_ssd_norm_w, v_w_attn_out, v_w_ssd_out, v_w_o, v_w_mlp1, v_w_mlp2):
    args = dict(locals())
    strip = lambda a: a[0] if a.ndim == 3 else a
    w = {n: strip(args[n]) for n in NAMES + ("x", "c")}
    m = {n: strip(args["m_" + n]) for n in NAMES}
    v = {n: strip(args["v_" + n]) for n in NAMES}
    loss, grad_x, grads, delta, new_m, new_v = _step(w, m, v, loss_target[0])
    like = lambda t, n: t.reshape(args[n].shape)
    return (loss, grad_x[None], *[like(grads[n], n) for n in NAMES], *[like(delta[n], n) for n in NAMES],
            *[like(new_m[n], n) for n in NAMES], *[like(new_v[n], n) for n in NAMES])
```

```python
import functools
import math

import jax
import jax.numpy as jnp
from jax import lax
from jax.experimental import pallas as pl
from jax.experimental.pallas import tpu as pltpu

F32 = jnp.float32
MMD = jnp.bfloat16
EPS = 1e-6
NEG = -1e30
MIB = 1024 * 1024
VMEM_BIG = 56 * MIB
VMEM_MID = 40 * MIB

GRID_W = 64
N_Q_HEADS, N_KV_HEADS, HEAD_DIM = 16, 4, 64
ROPE_THETA = 10000.0
SSD_HEADS, SSD_GROUPS, SSD_P, SSD_N, CHUNK = 32, 4, 64, 128, 128
HPG = SSD_HEADS // SSD_GROUPS
D_CONV = 5
ADAM_LR, ADAM_B1, ADAM_B2, ADAM_EPS, ADAM_WD, ADAM_STEP = 0.001, 0.9, 0.999, 1e-08, 0.01, 10

Z0, GA0, GS0, XS0, B0, C0, Q0, K0, V0, DT0, PW = 0, 2048, 3072, 4096, 6144, 6656, 7168, 8192, 8448, 8704, 8832

MESH = pl.DeviceIdType.MESH
NT = (((1,), (1,)), ((), ()))
TN = (((0,), (0,)), ((), ()))


def _cp(sem=None, vmem=VMEM_MID):
    return pltpu.CompilerParams(dimension_semantics=sem, vmem_limit_bytes=vmem)


def _tile(n, pref):
    t = min(n, pref)
    while n % t:
        t //= 2
    return t


def _dot(a, b, dims=None):
    if dims is None:
        return jnp.dot(a, b, preferred_element_type=F32)
    return lax.dot_general(a, b, dims, preferred_element_type=F32)


def _dot_hi(a, b):
    return jnp.dot(a, b, precision=lax.Precision.HIGHEST, preferred_element_type=F32)


def _sigmoid(x):
    return jax.nn.sigmoid(x)


def _mm(a, b, *, name, outs, nt=False, ta=False, extras=(), epi=None, tm=512, tn=512, n=None, b_outer=False,
        vmem=VMEM_MID):
    assert not (nt and ta)
    k, m = a.shape if ta else a.shape[::-1]
    if n is None:
        n = b.shape[0] if nt else b.shape[1]
    tm, tn = _tile(m, tm), _tile(n, tn)
    gi, gj = m // tm, n // tn
    if b_outer:
        grid = (gj, gi)
        ij = lambda p, q: (q, p)
    else:
        grid = (gi, gj)
        ij = lambda p, q: (p, q)
    if ta:
        a_spec = pl.BlockSpec((k, tm), lambda p, q: (0, ij(p, q)[0]))
    else:
        a_spec = pl.BlockSpec((tm, k), lambda p, q: (ij(p, q)[0], 0))
    if nt:
        b_spec = pl.BlockSpec((tn, k), lambda p, q: (ij(p, q)[1], 0))
    else:
        b_spec = pl.BlockSpec((k, tn), lambda p, q: (0, ij(p, q)[1]))
    e_specs = []
    for arr, kind, off in extras:
        ob = off // tn
        assert off % tn == 0
        if kind == "tile":
            e_specs.append(pl.BlockSpec((tm, tn), lambda p, q, ob=ob: (ij(p, q)[0], ob + ij(p, q)[1])))
        else:
            e_specs.append(pl.BlockSpec((1, tn), lambda p, q, ob=ob: (0, ob + ij(p, q)[1])))
    ne = len(extras)

    def body(a_ref, b_ref, *rest):
        acc = _dot(a_ref[...], b_ref[...], NT if nt else (TN if ta else None))
        res = epi(acc, *[e[...] for e in rest[:ne]]) if epi is not None else (acc,)
        for o_ref, r in zip(rest[ne:], res):
            o_ref[...] = r.astype(o_ref.dtype)

    out = pl.pallas_call(
        body, name=name, grid=grid,
        in_specs=[a_spec, b_spec] + e_specs,
        out_specs=[pl.BlockSpec((tm, tn), lambda p, q: ij(p, q)) for _ in outs],
        out_shape=[jax.ShapeDtypeStruct((m, n), dt) for dt in outs],
        compiler_params=_cp(("arbitrary", "arbitrary"), vmem),
    )(a, b, *[e[0] for e in extras])
    return out if len(outs) > 1 else out[0]


def _mm_tn(a, g, *, name, tk=512, tn=1024, tmm=4096, vmem=VMEM_MID):
    m, k = a.shape
    n = g.shape[1]
    tk, tn, tmm = _tile(k, tk), _tile(n, tn), _tile(m, tmm)

    def body(a_ref, g_ref, o_ref):
        p = _dot(a_ref[...], g_ref[...], TN)

        @pl.when(pl.program_id(2) == 0)
        def _():
            o_ref[...] = p

        @pl.when(pl.program_id(2) > 0)
        def _():
            o_ref[...] += p

    return pl.pallas_call(
        body, name=name, grid=(k // tk, n // tn, m // tmm),
        in_specs=[pl.BlockSpec((tmm, tk), lambda i, j, r: (r, i)), pl.BlockSpec((tmm, tn), lambda i, j, r: (r, j))],
        out_specs=pl.BlockSpec((tk, tn), lambda i, j, r: (i, j)),
        out_shape=jax.ShapeDtypeStruct((k, n), F32),
        compiler_params=_cp(("arbitrary", "arbitrary", "arbitrary"), vmem),
    )(a, g)


def _adamw(w, g, m, v, *, name):
    r, c = w.shape
    tr = _tile(r, 256) if r % 8 == 0 else r

    def body(w_ref, g_ref, m_ref, v_ref, d_ref, nm_ref, nv_ref):
        gg = g_ref[...]
        nm = ADAM_B1 * m_ref[...] + (1.0 - ADAM_B1) * gg
        nv = ADAM_B2 * v_ref[...] + (1.0 - ADAM_B2) * jnp.square(gg)
        m_hat = nm / (1.0 - ADAM_B1 ** ADAM_STEP)
        v_hat = nv / (1.0 - ADAM_B2 ** ADAM_STEP)
        d_ref[...] = -ADAM_LR * (m_hat / (jnp.sqrt(v_hat) + ADAM_EPS) + ADAM_WD * w_ref[...])
        nm_ref[...] = nm
        nv_ref[...] = nv

    spec = pl.BlockSpec((tr, c), lambda i: (i, 0))
    return pl.pallas_call(
        body, name=name, grid=(r // tr,), in_specs=[spec] * 4, out_specs=[spec] * 3,
        out_shape=[jax.ShapeDtypeStruct((r, c), F32)] * 3, compiler_params=_cp(("arbitrary",)),
    )(w, g, m, v)


def _rows_sum(a, groups, *, name):
    r = a.shape[0] // groups

    def body(a_ref, o_ref):
        acc = a_ref[0:r, :]
        for d in range(1, groups):
            acc = acc + a_ref[d * r:(d + 1) * r, :]
        o_ref[...] = acc

    return pl.pallas_call(body, name=name, out_shape=jax.ShapeDtypeStruct((r, a.shape[1]), F32))(a)


def _silu_cast(a, *, name):
    def body(a_ref, o_ref):
        x = a_ref[...]
        o_ref[...] = (x * _sigmoid(x)).astype(o_ref.dtype)

    return pl.pallas_call(body, name=name, out_shape=jax.ShapeDtypeStruct(a.shape, MMD))(a)


def _sumsq(a, *, name):
    m, n = a.shape
    tm = _tile(m, 512)

    def body(a_ref, o_ref):
        x = a_ref[...]
        p = jnp.sum(jnp.sum(x * x, axis=1, keepdims=True), axis=0, keepdims=True)

        @pl.when(pl.program_id(0) == 0)
        def _():
            o_ref[...] = p

        @pl.when(pl.program_id(0) > 0)
        def _():
            o_ref[...] += p

    return pl.pallas_call(
        body, name=name, grid=(m // tm,), in_specs=[pl.BlockSpec((tm, n), lambda i: (i, 0))],
        out_specs=pl.BlockSpec((1, 1), lambda i: (0, 0)), out_shape=jax.ShapeDtypeStruct((1, 1), F32),
        compiler_params=_cp(("arbitrary",)),
    )(a)


def _acc_rows(o_ref, p, first):
    @pl.when(first)
    def _():
        o_ref[...] = p

    @pl.when(jnp.logical_not(first))
    def _():
        o_ref[...] += p


def _ln_mod(x, w, scale, shift, *, name):
    s, d = x.shape
    tm = _tile(s, 512)

    def body(x_ref, w_ref, sc_ref, sh_ref, o_ref):
        xv = x_ref[...]
        r = lax.rsqrt(jnp.mean(xv * xv, axis=-1, keepdims=True) + EPS)
        o_ref[...] = ((xv * r) * w_ref[...] * (1.0 + sc_ref[...]) + sh_ref[...]).astype(o_ref.dtype)

    row = pl.BlockSpec((1, d), lambda i: (0, 0))
    big = pl.BlockSpec((tm, d), lambda i: (i, 0))
    return pl.pallas_call(
        body, name=name, grid=(s // tm,), in_specs=[big, row, row, row], out_specs=big,
        out_shape=jax.ShapeDtypeStruct((s, d), MMD), compiler_params=_cp(("arbitrary",)),
    )(x, w, scale, shift)


def _ln_mod_bwd(dh, x, w, scale, dres, *, name):
    s, d = x.shape
    tm = _tile(s, 512)

    def body(dh_ref, x_ref, w_ref, sc_ref, dres_ref, dx_ref, dsh_ref, dsc_ref, dw_ref):
        xv = x_ref[...]
        dhv = dh_ref[...].astype(F32)
        r = lax.rsqrt(jnp.mean(xv * xv, axis=-1, keepdims=True) + EPS)
        nv = xv * r
        wv = w_ref[...]
        g1 = 1.0 + sc_ref[...]
        dn = dhv * (wv * g1)
        dx_ref[...] = dres_ref[...] + r * (dn - nv * jnp.mean(dn * nv, axis=-1, keepdims=True))
        first = pl.program_id(0) == 0
        _acc_rows(dsh_ref, jnp.sum(dhv, axis=0, keepdims=True), first)
        _acc_rows(dsc_ref, jnp.sum(dhv * nv * wv, axis=0, keepdims=True), first)
        _acc_rows(dw_ref, jnp.sum(dhv * nv * g1, axis=0, keepdims=True), first)

    row = pl.BlockSpec((1, d), lambda i: (0, 0))
    big = pl.BlockSpec((tm, d), lambda i: (i, 0))
    return pl.pallas_call(
        body, name=name, grid=(s // tm,), in_specs=[big, big, row, row, big], out_specs=[big, row, row, row],
        out_shape=[jax.ShapeDtypeStruct((s, d), F32)] + [jax.ShapeDtypeStruct((1, d), F32)] * 3,
        compiler_params=_cp(("arbitrary",)),
    )(dh, x, w, scale, dres)


def _gate_bwd(dy, u, gate, *, name):
    s, d = dy.shape
    tm = _tile(s, 512)

    def body(dy_ref, u_ref, g_ref, du_ref, dg_ref):
        dyv = dy_ref[...]
        du_ref[...] = (dyv * g_ref[...]).astype(du_ref.dtype)
        _acc_rows(dg_ref, jnp.sum(dyv * u_ref[...].astype(F32), axis=0, keepdims=True), pl.program_id(0) == 0)

    row = pl.BlockSpec((1, d), lambda i: (0, 0))
    big = pl.BlockSpec((tm, d), lambda i: (i, 0))
    return pl.pallas_call(
        body, name=name, grid=(s // tm,), in_specs=[big, big, row], out_specs=[big, row],
        out_shape=[jax.ShapeDtypeStruct((s, d), MMD), jax.ShapeDtypeStruct((1, d), F32)],
        compiler_params=_cp(("arbitrary",)),
    )(dy, u, gate)


def _seg64(v, e):
    hi = v.astype(jnp.bfloat16)
    lo = (v - hi.astype(F32)).astype(jnp.bfloat16)
    return _dot(hi, e) + _dot(lo, e)


def _rope_tables(s):
    rows = s // GRID_W
    pos_row = jnp.repeat(jnp.arange(rows, dtype=jnp.int32), GRID_W).astype(F32)
    pos_col = jnp.tile(jnp.arange(GRID_W, dtype=jnp.int32), rows).astype(F32)
    axis_dim = HEAD_DIM // 2
    inv_freq = ROPE_THETA ** (-jnp.arange(0, axis_dim, 2, dtype=F32) / axis_dim)
    ang_r = pos_row[:, None] * inv_freq[None, :]
    ang_c = pos_col[:, None] * inv_freq[None, :]
    zero = jnp.zeros_like(ang_r)
    cos = jnp.concatenate([jnp.cos(ang_r), jnp.cos(ang_r), jnp.cos(ang_c), jnp.cos(ang_c)], axis=1)
    s_a = jnp.concatenate([-jnp.sin(ang_r), zero, -jnp.sin(ang_c), zero], axis=1)
    s_b = jnp.concatenate([zero, jnp.sin(ang_r), zero, jnp.sin(ang_c)], axis=1)
    return [jnp.tile(t, (1, 2)) for t in (cos, s_a, s_b)]


def _e128():
    i = jnp.arange(128)
    return (i[:, None] // 64 == i[None, :] // 64).astype(jnp.bfloat16)


QKW = N_Q_HEADS * HEAD_DIM + N_KV_HEADS * HEAD_DIM


def _qk_fwd(proj, wrow, scrow, tabs, *, name):
    s = proj.shape[0]
    tm = _tile(s, 1024)

    def body(x_ref, w_ref, sc_ref, cos_ref, sa_ref, sb_ref, e_ref, o_ref, ot_ref):
        u = x_ref[...].astype(F32)
        r = lax.rsqrt(_seg64(u * u, e_ref[...]) * (1.0 / HEAD_DIM) + EPS)
        nv = (u * r) * w_ref[...]
        ro = nv * cos_ref[...] + pltpu.roll(nv, 112, 1) * sa_ref[...] + pltpu.roll(nv, 16, 1) * sb_ref[...]
        out = ro * sc_ref[...]
        o_ref[...] = out.astype(o_ref.dtype)
        ot_ref[...] = out.T.astype(ot_ref.dtype)

    tab = pl.BlockSpec((tm, 128), lambda i, j: (i, 0))
    row = pl.BlockSpec((1, 128), lambda i, j: (0, j))
    return pl.pallas_call(
        body, name=name, grid=(s // tm, QKW // 128),
        in_specs=[pl.BlockSpec((tm, 128), lambda i, j: (i, Q0 // 128 + j)), row, row, tab, tab, tab,
                  pl.BlockSpec((128, 128), lambda i, j: (0, 0))],
        out_specs=[pl.BlockSpec((tm, 128), lambda i, j: (i, j)), pl.BlockSpec((128, tm), lambda i, j: (j, i))],
        out_shape=[jax.ShapeDtypeStruct((s, QKW), MMD), jax.ShapeDtypeStruct((QKW, s), MMD)],
        compiler_params=_cp(("arbitrary", "arbitrary")),
    )(proj, wrow, scrow, *tabs, _e128())


def _qk_bwd(dqkt, proj, wrow, scrow, tabs, *, name):
    s = proj.shape[0]
    tm = _tile(s, 1024)

    def body(d_ref, x_ref, w_ref, sc_ref, cos_ref, sa_ref, sb_ref, e_ref, du_ref, dw_ref):
        e = e_ref[...]
        d = d_ref[...].T * sc_ref[...]
        dn = d * cos_ref[...] + pltpu.roll(d * sa_ref[...], 16, 1) + pltpu.roll(d * sb_ref[...], 112, 1)
        u = x_ref[...].astype(F32)
        r = lax.rsqrt(_seg64(u * u, e) * (1.0 / HEAD_DIM) + EPS)
        uh = u * r
        _acc_rows(dw_ref, jnp.sum(dn * uh, axis=0, keepdims=True), pl.program_id(1) == 0)
        dnw = dn * w_ref[...]
        du_ref[...] = (r * (dnw - uh * (_seg64(dnw * uh, e) * (1.0 / HEAD_DIM)))).astype(du_ref.dtype)

    tab = pl.BlockSpec((tm, 128), lambda j, i: (i, 0))
    row = pl.BlockSpec((1, 128), lambda j, i: (0, j))
    return pl.pallas_call(
        body, name=name, grid=(QKW // 128, s // tm),
        in_specs=[pl.BlockSpec((128, tm), lambda j, i: (j, i)), pl.BlockSpec((tm, 128), lambda j, i: (i, Q0 // 128 + j)),
                  row, row, tab, tab, tab, pl.BlockSpec((128, 128), lambda j, i: (0, 0))],
        out_specs=[pl.BlockSpec((tm, 128), lambda j, i: (i, j)), row],
        out_shape=[jax.ShapeDtypeStruct((s, QKW), MMD), jax.ShapeDtypeStruct((1, QKW), F32)],
        compiler_params=_cp(("arbitrary", "arbitrary")),
    )(dqkt, proj, wrow, scrow, *tabs, _e128())


REP = N_Q_HEADS // N_KV_HEADS


def _lanes(ref):
    return jnp.concatenate([ref[r] for r in range(REP)], axis=1)


V_AUG = HEAD_DIM + 8
LOG2E = math.log2(math.e)


def _flash_fwd(qkt, vta, *, name):
    s = qkt.shape[2]
    tq, tk = _tile(s, 1024), _tile(s, 512)
    nk = s // tk
    lanes = REP * tq

    def body(q_ref, k_ref, v_ref, o_ref, lse_ref, m_ref, acc_ref):
        j = pl.program_id(2)

        @pl.when(j == 0)
        def _():
            m_ref[...] = jnp.full_like(m_ref, NEG)
            acc_ref[...] = jnp.zeros_like(acc_ref)

        st = _dot(k_ref[0], _lanes(q_ref), TN)
        m_prev = m_ref[...]
        m_new = jnp.maximum(m_prev, jnp.max(st, axis=0, keepdims=True))
        p = jnp.exp2(st - m_new).astype(MMD)
        acc_ref[...] = jnp.exp2(m_prev - m_new) * acc_ref[...] + _dot(v_ref[0], p)
        m_ref[...] = m_new

        @pl.when(j == nk - 1)
        def _():
            acc = acc_ref[...]
            l = acc[HEAD_DIM:HEAD_DIM + 1]
            o = acc[0:HEAD_DIM] / l
            ls = m_ref[...] + jnp.log(l) * LOG2E
            for r in range(REP):
                o_ref[r] = o[:, r * tq:(r + 1) * tq].astype(o_ref.dtype)
                lse_ref[r] = ls[:, r * tq:(r + 1) * tq]

    qspec = pl.BlockSpec((REP, HEAD_DIM, tq), lambda g, i, j: (g, 0, i))
    return pl.pallas_call(
        body, name=name, grid=(N_KV_HEADS, s // tq, nk),
        in_specs=[qspec, pl.BlockSpec((1, HEAD_DIM, tk), lambda g, i, j: (N_Q_HEADS + g, 0, j)),
                  pl.BlockSpec((1, V_AUG, tk), lambda g, i, j: (g, 0, j))],
        out_specs=[qspec, pl.BlockSpec((REP, 1, tq), lambda g, i, j: (g, 0, i))],
        out_shape=[jax.ShapeDtypeStruct((N_Q_HEADS, HEAD_DIM, s), MMD), jax.ShapeDtypeStruct((N_Q_HEADS, 1, s), F32)],
        scratch_shapes=[pltpu.VMEM((1, lanes), F32), pltpu.VMEM((V_AUG, lanes), F32)],
        compiler_params=_cp(("arbitrary", "arbitrary", "arbitrary"), VMEM_BIG),
    )(qkt, qkt, vta)


def _flash_bwd(qkt, k_h, v_h, dot, ot, lse, *, name):
    s = qkt.shape[2]
    tq, tk = _tile(s, 512), _tile(s, 1024)
    nk = s // tk

    def body(q_ref, kt_ref, k_ref, v_ref, do_ref, o_ref, lse_ref, dq_ref, dk_ref, dv_ref, dq_acc):
        i, j = pl.program_id(1), pl.program_id(2)
        q, do = _lanes(q_ref), _lanes(do_ref)
        delta = jnp.sum(do.astype(F32) * _lanes(o_ref).astype(F32), axis=0, keepdims=True)
        k, v = k_ref[0], v_ref[0]
        p = jnp.exp2(_dot(k, q) - _lanes(lse_ref))
        dvc = _dot(p.astype(MMD), do, NT)
        ds = (p * (_dot(v, do) - delta)).astype(MMD)
        dkc = _dot(ds, q, NT) * (1.0 / LOG2E)
        dqc = _dot(kt_ref[0], ds)
        rows = pl.ds(pl.multiple_of(j * tk, tk), tk)

        @pl.when(i == 0)
        def _():
            dk_ref[0, rows, :] = dkc
            dv_ref[0, rows, :] = dvc

        @pl.when(i > 0)
        def _():
            dk_ref[0, rows, :] += dkc
            dv_ref[0, rows, :] += dvc

        @pl.when(j == 0)
        def _():
            dq_acc[...] = dqc

        @pl.when(j > 0)
        def _():
            dq_acc[...] += dqc

        @pl.when(j == nk - 1)
        def _():
            acc = dq_acc[...]
            for r in range(REP):
                dq_ref[r] = acc[:, r * tq:(r + 1) * tq]

    qspec = pl.BlockSpec((REP, HEAD_DIM, tq), lambda g, i, j: (g, 0, i))
    kvin = pl.BlockSpec((1, tk, HEAD_DIM), lambda g, i, j: (g, j, 0))
    kvres = pl.BlockSpec((1, s, HEAD_DIM), lambda g, i, j: (g, 0, 0))
    return pl.pallas_call(
        body, name=name, grid=(N_KV_HEADS, s // tq, nk),
        in_specs=[qspec, pl.BlockSpec((1, HEAD_DIM, tk), lambda g, i, j: (N_Q_HEADS + g, 0, j)), kvin, kvin,
                  qspec, qspec, pl.BlockSpec((REP, 1, tq), lambda g, i, j: (g, 0, i))],
        out_specs=[qspec, kvres, kvres],
        out_shape=[jax.ShapeDtypeStruct((N_Q_HEADS, HEAD_DIM, s), F32), jax.ShapeDtypeStruct((N_KV_HEADS, s, HEAD_DIM), F32),
                   jax.ShapeDtypeStruct((N_KV_HEADS, s, HEAD_DIM), F32)],
        scratch_shapes=[pltpu.VMEM((HEAD_DIM, REP * tq), F32)],
        compiler_params=_cp(("arbitrary", "arbitrary", "arbitrary"), VMEM_BIG),
    )(qkt, qkt, k_h, v_h, dot, ot, lse)


HALO = 8
CONV_W = 2048 + 2 * SSD_GROUPS * SSD_N


def _shifted(win, off, r):
    return pltpu.roll(win, (r + 2 * HALO - off) % (r + 2 * HALO), 0)[0:r]


def _conv_fwd(proj, w8, brow, *, name):
    s = proj.shape[0]
    cb = 256
    r = _tile(s, 512)

    def body(x_ref, w_ref, b_ref, o_ref, pad_ref):
        zeros = jnp.zeros((HALO, cb), F32)
        pad_ref[0:HALO, :] = zeros
        pad_ref[s + HALO:s + 2 * HALO, :] = zeros

        def fill(i, carry):
            st = pl.multiple_of(i * r, r)
            pad_ref[pl.ds(st + HALO, r), :] = x_ref[pl.ds(st, r), :].astype(F32)
            return carry

        lax.fori_loop(0, s // r, fill, 0)
        wv = w_ref[...]
        bv = b_ref[...]

        def step(i, carry):
            st = pl.multiple_of(i * r, r)
            win = pad_ref[pl.ds(st, r + 2 * HALO), :]
            acc = bv + wv[0:1, :] * _shifted(win, HALO - 2, r)
            for t in range(1, D_CONV):
                acc = acc + wv[t:t + 1, :] * _shifted(win, HALO - 2 + t, r)
            o_ref[pl.ds(st, r), :] = (acc * _sigmoid(acc)).astype(o_ref.dtype)
            return carry

        lax.fori_loop(0, s // r, step, 0)

    return pl.pallas_call(
        body, name=name, grid=(CONV_W // cb,),
        in_specs=[pl.BlockSpec((s, cb), lambda j: (0, XS0 // cb + j)), pl.BlockSpec((8, cb), lambda j: (0, j)),
                  pl.BlockSpec((1, cb), lambda j: (0, j))],
        out_specs=pl.BlockSpec((s, cb), lambda j: (0, j)),
        out_shape=jax.ShapeDtypeStruct((s, CONV_W), MMD),
        scratch_shapes=[pltpu.VMEM((s + 2 * HALO, cb), F32)],
        compiler_params=_cp(("arbitrary",), VMEM_MID),
    )(proj, w8, brow)


def _conv_bwd(proj, col0, ga, gb, w8, brow, *, name):
    s = proj.shape[0]
    width = ga.shape[1]
    cb = 128
    c0 = col0 // cb
    r = _tile(s, 512)

    def body(x_ref, ga_ref, gb_ref, w_ref, b_ref, dx_ref, dw_ref, db_ref, xpad, dpad):
        zeros = jnp.zeros((HALO, cb), F32)
        for ref in (xpad, dpad):
            ref[0:HALO, :] = zeros
            ref[s + HALO:s + 2 * HALO, :] = zeros

        def fill(i, carry):
            st = pl.multiple_of(i * r, r)
            xpad[pl.ds(st + HALO, r), :] = x_ref[pl.ds(st, r), :].astype(F32)
            return carry

        lax.fori_loop(0, s // r, fill, 0)
        wv = w_ref[...]
        bv = b_ref[...]

        def first(i, carry):
            st = pl.multiple_of(i * r, r)
            win = xpad[pl.ds(st, r + 2 * HALO), :]
            taps = [_shifted(win, HALO - 2 + t, r) for t in range(D_CONV)]
            u = bv
            for t in range(D_CONV):
                u = u + wv[t:t + 1, :] * taps[t]
            sg = _sigmoid(u)
            du = (ga_ref[pl.ds(st, r), :] + gb_ref[pl.ds(st, r), :]) * (sg * (1.0 + u * (1.0 - sg)))
            dpad[pl.ds(st + HALO, r), :] = du
            out = [carry[0] + jnp.sum(du, axis=0, keepdims=True)]
            for t in range(D_CONV):
                out.append(carry[1 + t] + jnp.sum(du * taps[t], axis=0, keepdims=True))
            return tuple(out)

        sums = lax.fori_loop(0, s // r, first, tuple(jnp.zeros((1, cb), F32) for _ in range(1 + D_CONV)))
        db_ref[...] = sums[0]
        for t in range(D_CONV):
            dw_ref[t:t + 1, :] = sums[1 + t]
        dw_ref[D_CONV:8, :] = jnp.zeros((8 - D_CONV, cb), F32)

        def second(i, carry):
            st = pl.multiple_of(i * r, r)
            win = dpad[pl.ds(st, r + 2 * HALO), :]
            acc = wv[0:1, :] * _shifted(win, HALO + 2, r)
            for t in range(1, D_CONV):
                acc = acc + wv[t:t + 1, :] * _shifted(win, HALO + 2 - t, r)
            dx_ref[pl.ds(st, r), :] = acc.astype(dx_ref.dtype)
            return carry

        lax.fori_loop(0, s // r, second, 0)

    col = pl.BlockSpec((s, cb), lambda j: (0, j))
    return pl.pallas_call(
        body, name=name, grid=(width // cb,),
        in_specs=[pl.BlockSpec((s, cb), lambda j: (0, XS0 // cb + c0 + j)), col, col,
                  pl.BlockSpec((8, cb), lambda j: (0, c0 + j)), pl.BlockSpec((1, cb), lambda j: (0, c0 + j))],
        out_specs=[col, pl.BlockSpec((8, cb), lambda j: (0, j)), pl.BlockSpec((1, cb), lambda j: (0, j))],
        out_shape=[jax.ShapeDtypeStruct((s, width), MMD), jax.ShapeDtypeStruct((8, width), F32),
                   jax.ShapeDtypeStruct((1, width), F32)],
        scratch_shapes=[pltpu.VMEM((s + 2 * HALO, cb), F32), pltpu.VMEM((s + 2 * HALO, cb), F32)],
        compiler_params=_cp(("arbitrary",), VMEM_BIG),
    )(proj, ga, gb, w8, brow)


def _tri(lower):
    i = jnp.arange(CHUNK)
    return ((i[:, None] >= i[None, :]) if lower else (i[:, None] <= i[None, :])).astype(F32)


def _dt_fwd(raw, bias, arow, *, name):
    s = raw.shape[0]

    def body(r_ref, b_ref, a_ref, lo_ref, up_ref, dt_ref, cs_ref):
        u = r_ref[...] + b_ref[...]
        dt = jnp.maximum(u, 0.0) + jnp.log1p(jnp.exp(-jnp.abs(u)))
        dt_ref[...] = dt
        a = dt * a_ref[...]
        lane = lax.broadcasted_iota(jnp.int32, (CHUNK, 128), 1)
        cs_ref[...] = jnp.where(lane < SSD_HEADS, _dot_hi(lo_ref[...], a), _dot_hi(up_ref[...], a))

    blk = pl.BlockSpec((CHUNK, 128), lambda i: (i, 0))
    row = pl.BlockSpec((1, 128), lambda i: (0, 0))
    tri = pl.BlockSpec((CHUNK, CHUNK), lambda i: (0, 0))
    return pl.pallas_call(
        body, name=name, grid=(s // CHUNK,), in_specs=[blk, row, row, tri, tri], out_specs=[blk, blk],
        out_shape=[jax.ShapeDtypeStruct((s, 128), F32)] * 2, compiler_params=_cp(("arbitrary",)),
    )(raw, bias, arow, _tri(True), _tri(False))


def _dt_bwd(ddt, raw, bias, *, name):
    s = raw.shape[0]
    tm = _tile(s, 1024)

    def body(d_ref, r_ref, b_ref, o_ref, db_ref):
        g = d_ref[...] * _sigmoid(r_ref[...] + b_ref[...])
        o_ref[...] = g.astype(o_ref.dtype)
        _acc_rows(db_ref, jnp.sum(g, axis=0, keepdims=True), pl.program_id(0) == 0)

    blk = pl.BlockSpec((tm, 128), lambda i: (i, 0))
    row = pl.BlockSpec((1, 128), lambda i: (0, 0))
    return pl.pallas_call(
        body, name=name, grid=(s // tm,), in_specs=[blk, blk, row], out_specs=[blk, row],
        out_shape=[jax.ShapeDtypeStruct((s, 128), MMD), jax.ShapeDtypeStruct((1, 128), F32)],
        compiler_params=_cp(("arbitrary",)),
    )(ddt, raw, bias)


GW = HPG * SSD_P


GPS = 4


def _ssd_specs(nc, rev):
    cc = (lambda c: nc - 1 - c) if rev else (lambda c: c)
    nb = SSD_GROUPS // GPS
    return dict(
        x=pl.BlockSpec((CHUNK, GPS * GW), lambda g, c: (cc(c), g)),
        b=pl.BlockSpec((CHUNK, GPS * SSD_N), lambda g, c: (cc(c), 2048 // (GPS * SSD_N) + g)),
        c=pl.BlockSpec((CHUNK, GPS * SSD_N), lambda g, c: (cc(c), 2048 // (GPS * SSD_N) + nb + g)),
        col=pl.BlockSpec((GPS, CHUNK, HPG), lambda g, c: (g, cc(c), 0)),
        lanes=pl.BlockSpec((CHUNK, 128), lambda g, c: (cc(c), 0)),
        rowt=pl.BlockSpec((GPS, 1, HPG, CHUNK), lambda g, c: (g, cc(c), 0, 0)),
        drow=pl.BlockSpec((1, GPS * GW), lambda g, c: (0, g)),
        y=pl.BlockSpec((CHUNK, GPS * GW), lambda g, c: (cc(c), g)),
        h=pl.BlockSpec((GPS, 1, SSD_N, GW), lambda g, c: (g, cc(c), 0, 0)),
        n=pl.BlockSpec((CHUNK, GPS * SSD_N), lambda g, c: (cc(c), g)),
    )


def _ssd_mask(anti):
    ii = lax.broadcasted_iota(jnp.int32, (CHUNK, CHUNK), 0)
    jj = lax.broadcasted_iota(jnp.int32, (CHUNK, CHUNK), 1)
    return ii, jj, (ii <= jj) if anti else (ii >= jj)


def _expand(x, ex):
    h1 = x.astype(jnp.bfloat16)
    r1 = x - h1.astype(F32)
    h2 = r1.astype(jnp.bfloat16)
    h3 = (r1 - h2.astype(F32)).astype(jnp.bfloat16)
    return _dot(h1, ex) + _dot(h2, ex) + _dot(h3, ex)


def _headsum(a, e):
    hi = a.astype(jnp.bfloat16)
    return _dot(hi, e) + _dot((a - hi.astype(F32)).astype(jnp.bfloat16), e)


def _expand_mats():
    lane = jnp.arange(128)[None, :, None]
    col = jnp.arange(GW)[None, None, :]
    base = (jnp.arange(2)[:, None] * SSD_HEADS + jnp.arange(SSD_GROUPS)[None, :] * HPG).reshape(2 * SSD_GROUPS, 1, 1)
    return (lane == base + col // SSD_P).astype(jnp.bfloat16)


def _headsum_mats():
    e1 = (jnp.arange(GW)[:, None] // SSD_P == jnp.arange(128)[None, :]).astype(jnp.bfloat16)
    e2 = (jnp.arange(HPG * CHUNK)[:, None] // CHUNK == jnp.arange(128)[None, :]).astype(jnp.bfloat16)
    return e1, e2


def _ssd_fwd(xc, dt, cs, cst, ex, drow, di, *, name):
    s = xc.shape[0]
    nc = s // CHUNK
    anti = di == 1
    sp = _ssd_specs(nc, anti)
    trow = 0 if anti else CHUNK - 1

    def body(x_ref, b_ref, c_ref, dt_ref, cs_ref, cst_ref, ex_ref, d_ref, y_ref, hp_ref, h_ref):
        @pl.when(pl.program_id(1) == 0)
        def _():
            h_ref[...] = jnp.zeros_like(h_ref)

        mask = _ssd_mask(anti)[2]
        dtv, csv = dt_ref[...], cs_ref[...]
        for gi in range(GPS):
            cols = slice(gi * GW, (gi + 1) * GW)
            ncols = slice(gi * SSD_N, (gi + 1) * SSD_N)
            ex = ex_ref[gi]
            xb = x_ref[:, cols].astype(F32)
            bm, cm = b_ref[:, ncols], c_ref[:, ncols]
            csr = cst_ref[gi, 0]
            dtf = _expand(dtv, ex)
            csf = _expand(csv, ex)
            tl = csf[trow:trow + 1, :]
            h = h_ref[gi]
            hp_ref[gi, 0] = h
            g = _dot(cm, bm, NT)
            xs = xb * dtf
            xsm = xs.astype(MMD)
            base = jnp.exp(csf) * _dot(cm, h.astype(MMD)) + d_ref[:, cols] * xb
            for r in range(HPG):
                sl = slice(r * SSD_P, (r + 1) * SSD_P)
                lm = jnp.exp(jnp.where(mask, csf[:, r * SSD_P:r * SSD_P + 1] - csr[r:r + 1, :], NEG))
                y_ref[:, gi * GW + r * SSD_P:gi * GW + (r + 1) * SSD_P] = _dot((g * lm).astype(MMD), xsm[:, sl]) + base[:, sl]
            xd = (xs * jnp.exp(tl - csf)).astype(MMD)
            h_ref[gi] = h * jnp.exp(tl) + _dot(bm, xd, TN)

    nb = SSD_GROUPS // GPS
    return pl.pallas_call(
        body, name=name, grid=(nb, nc),
        in_specs=[sp["x"], sp["b"], sp["c"], sp["lanes"], sp["lanes"], sp["rowt"],
                  pl.BlockSpec((GPS, 128, GW), lambda g, c: (di * nb + g, 0, 0)), sp["drow"]],
        out_specs=[sp["y"], sp["h"]],
        out_shape=[jax.ShapeDtypeStruct((s, 2048), F32), jax.ShapeDtypeStruct((SSD_GROUPS, nc, SSD_N, GW), F32)],
        scratch_shapes=[pltpu.VMEM((GPS, SSD_N, GW), F32)],
        compiler_params=_cp(("arbitrary", "arbitrary")),
    )(xc, xc, xc, dt, cs, cst, ex, drow)


def _ssd_bwd(xc, dt, cs, dt4, cst, ex, drow, arow4, dy, hprev, di, *, name):
    s = xc.shape[0]
    nc = s // CHUNK
    anti = di == 1
    sp = _ssd_specs(nc, not anti)
    trow = 0 if anti else CHUNK - 1
    e1, e2 = _headsum_mats()

    def body(x_ref, b_ref, c_ref, dt_ref, cs_ref, dt4_ref, cst_ref, ex_ref, d_ref, a_ref, dy_ref, hp_ref, tri_ref,
             e1_ref, e2_ref, dx_ref, db_ref, dc_ref, ddt_ref, da_ref, dh_ref, w_ref, dxs_ref):
        @pl.when(pl.program_id(1) == 0)
        def _():
            dh_ref[...] = jnp.zeros_like(dh_ref)
            da_ref[...] = jnp.zeros_like(da_ref)

        e1v = e1_ref[...]
        ii, _, mask = _ssd_mask(anti)
        dtv, csv = dt_ref[...], cs_ref[...]
        for gi in range(GPS):
            cols = slice(gi * GW, (gi + 1) * GW)
            ncols = slice(gi * SSD_N, (gi + 1) * SSD_N)
            ex = ex_ref[gi]
            xb = x_ref[:, cols].astype(F32)
            bm, cm = b_ref[:, ncols], c_ref[:, ncols]
            csr = cst_ref[gi, 0]
            dyb = dy_ref[:, cols]
            dym = dyb.astype(MMD)
            hp = hp_ref[gi, 0]
            hpm = hp.astype(MMD)
            dh = dh_ref[gi]
            dhm = dh.astype(MMD)
            dtf = _expand(dtv, ex)
            csf = _expand(csv, ex)
            tl = csf[trow:trow + 1, :]
            e = jnp.exp(csf)
            dec = jnp.exp(tl - csf)
            et = jnp.exp(tl)
            xs = xb * dtf
            xsm = xs.astype(MMD)
            g = _dot(cm, bm, NT)
            z = _dot(cm, hpm)
            bdh = _dot(bm, dhm)
            dg = jnp.zeros((CHUNK, CHUNK), F32)
            wcols = jnp.zeros((CHUNK, CHUNK), F32)
            for r in range(HPG):
                sl = slice(r * SSD_P, (r + 1) * SSD_P)
                lm = jnp.exp(jnp.where(mask, csf[:, r * SSD_P:r * SSD_P + 1] - csr[r:r + 1, :], NEG))
                mm = g * lm
                dm = _dot(dym[:, sl], xsm[:, sl], NT)
                w = dm * mm
                w_ref[gi, :, r * CHUNK:(r + 1) * CHUNK] = w
                wcols = jnp.where(ii == r, jnp.sum(w, axis=0, keepdims=True), wcols)
                dg = dg + dm * lm
                dxs_ref[gi, :, sl] = _dot(mm.astype(MMD), dym[:, sl], TN)
            dxs = dxs_ref[gi] + dec * bdh
            dx_ref[:, cols] = dxs * dtf + d_ref[:, cols] * dyb
            tb = xs * bdh * dec
            d_tot = jnp.sum(tb, axis=0, keepdims=True) + et * jnp.sum(dh * hp, axis=0, keepdims=True)
            d_tot = _headsum(jnp.broadcast_to(d_tot, (8, GW)), e1v)[0:1]
            dcs = (_headsum(dyb * (e * z) - tb, e1v) + _headsum(w_ref[gi], e2_ref[...]) - wcols.T
                   + jnp.where(ii == trow, d_tot, 0.0))
            da = _dot_hi(tri_ref[...], dcs)
            ddt_ref[gi] = (da * a_ref[gi] + _headsum(dxs * xb, e1v))[:, 0:HPG]
            da_ref[gi] += jnp.sum(da[:, 0:HPG] * dt4_ref[gi], axis=0, keepdims=True)
            dgm = dg.astype(MMD)
            dz = (e * dyb).astype(MMD)
            dc_ref[:, ncols] = _dot(dgm, bm) + _dot(dz, hpm, NT)
            db_ref[:, ncols] = _dot(dgm, cm, TN) + _dot((xs * dec).astype(MMD), dhm, NT)
            dh_ref[gi] = dh * et + _dot(cm, dz, TN)

    nb = SSD_GROUPS // GPS
    const = lambda shape: pl.BlockSpec(shape, lambda g, c: (0,) * len(shape))
    return pl.pallas_call(
        body, name=name, grid=(nb, nc),
        in_specs=[sp["x"], sp["b"], sp["c"], sp["lanes"], sp["lanes"], sp["col"], sp["rowt"],
                  pl.BlockSpec((GPS, 128, GW), lambda g, c: (di * nb + g, 0, 0)), sp["drow"],
                  pl.BlockSpec((GPS, 1, 128), lambda g, c: (g, 0, 0)), sp["y"], sp["h"],
                  const((CHUNK, CHUNK)), const((GW, 128)), const((HPG * CHUNK, 128))],
        out_specs=[sp["y"], sp["n"], sp["n"], sp["col"], pl.BlockSpec((GPS, 1, HPG), lambda g, c: (g, 0, 0))],
        out_shape=[jax.ShapeDtypeStruct((s, 2048), F32), jax.ShapeDtypeStruct((s, SSD_GROUPS * SSD_N), F32),
                   jax.ShapeDtypeStruct((s, SSD_GROUPS * SSD_N), F32), jax.ShapeDtypeStruct((SSD_GROUPS, s, HPG), F32),
                   jax.ShapeDtypeStruct((SSD_GROUPS, 1, HPG), F32)],
        scratch_shapes=[pltpu.VMEM((GPS, SSD_N, GW), F32), pltpu.VMEM((GPS, CHUNK, HPG * CHUNK), F32),
                        pltpu.VMEM((GPS, CHUNK, GW), F32)],
        compiler_params=_cp(("arbitrary", "arbitrary")),
    )(xc, xc, xc, dt, cs, dt4, cst, ex, drow, arow4, dy, hprev, _tri(anti), e1, e2)


def _gnorm_fwd(ya, yb, proj, w, *, name):
    s = ya.shape[0]
    tm = _tile(s, 256)

    def body(a_ref, b_ref, z_ref, w_ref, o_ref):
        zv = z_ref[...].astype(F32)
        t = (a_ref[...] + b_ref[...]) * (zv * _sigmoid(zv))
        r = lax.rsqrt(jnp.mean(t * t, axis=-1, keepdims=True) + EPS)
        o_ref[...] = ((t * r) * w_ref[...]).astype(o_ref.dtype)

    big = pl.BlockSpec((tm, 2048), lambda i: (i, 0))
    row = pl.BlockSpec((1, 2048), lambda i: (0, 0))
    return pl.pallas_call(
        body, name=name, grid=(s // tm,), in_specs=[big, big, big, row], out_specs=big,
        out_shape=jax.ShapeDtypeStruct((s, 2048), MMD), compiler_params=_cp(("arbitrary",)),
    )(ya, yb, proj, w)


def _gnorm_bwd(dout, ya, yb, proj, w, *, name):
    s = ya.shape[0]
    tm = _tile(s, 256)

    def body(do_ref, a_ref, b_ref, z_ref, w_ref, dy_ref, dz_ref, dw_ref):
        zv = z_ref[...].astype(F32)
        sg = _sigmoid(zv)
        sz = zv * sg
        y = a_ref[...] + b_ref[...]
        t = y * sz
        r = lax.rsqrt(jnp.mean(t * t, axis=-1, keepdims=True) + EPS)
        nv = t * r
        dov = do_ref[...].astype(F32)
        _acc_rows(dw_ref, jnp.sum(dov * nv, axis=0, keepdims=True), pl.program_id(0) == 0)
        dn = dov * w_ref[...]
        dt_ = r * (dn - nv * jnp.mean(dn * nv, axis=-1, keepdims=True))
        dy_ref[...] = dt_ * sz
        dz_ref[...] = (dt_ * y * (sg * (1.0 + zv * (1.0 - sg)))).astype(dz_ref.dtype)

    big = pl.BlockSpec((tm, 2048), lambda i: (i, 0))
    row = pl.BlockSpec((1, 2048), lambda i: (0, 0))
    return pl.pallas_call(
        body, name=name, grid=(s // tm,), in_specs=[big, big, big, big, row], out_specs=[big, big, row],
        out_shape=[jax.ShapeDtypeStruct((s, 2048), F32), jax.ShapeDtypeStruct((s, 2048), MMD),
                   jax.ShapeDtypeStruct((1, 2048), F32)],
        compiler_params=_cp(("arbitrary",)),
    )(dout, ya, yb, proj, w)


def _colsum_prod(a, b, *, name):
    s, n = a.shape
    tm = _tile(s, 256)

    def body(a_ref, b_ref, o_ref):
        _acc_rows(o_ref, jnp.sum(a_ref[...].astype(F32) * b_ref[...].astype(F32), axis=0, keepdims=True),
                  pl.program_id(0) == 0)

    big = pl.BlockSpec((tm, n), lambda i: (i, 0))
    return pl.pallas_call(
        body, name=name, grid=(s // tm,), in_specs=[big, big], out_specs=pl.BlockSpec((1, n), lambda i: (0, 0)),
        out_shape=jax.ShapeDtypeStruct((1, n), F32), compiler_params=_cp(("arbitrary",)),
    )(a, b)


def _heads(a, n):
    return a.reshape(a.shape[0], n, HEAD_DIM).transpose(1, 0, 2)


def _unheads(a):
    return a.transpose(1, 0, 2).reshape(a.shape[1], a.shape[0] * HEAD_DIM)


def _per_group(a):
    return a.reshape(a.shape[0], SSD_GROUPS, HPG).transpose(1, 0, 2)


def _per_group_t(a):
    s = a.shape[0]
    return a.reshape(s // CHUNK, CHUNK, SSD_GROUPS, HPG).transpose(2, 0, 3, 1)


def _local_step(x, target, mod, wts, small, late_weights=None, late_grads=None):
    s, d = x.shape
    shift1, scale1, gate1, shift2, scale2, gate2 = [mod[i:i + 1] for i in range(6)]

    h1 = _ln_mod(x, small["norm1_w"], scale1, shift1, name="ln1")
    proj = _mm(h1, wts["w_in_p"], name="in_proj", outs=[MMD], tm=512, tn=2944, b_outer=True)
    dt_raw = _mm(h1, wts["w_dt"], name="dt_proj", outs=[F32], tm=512, tn=128)

    qk_w = jnp.concatenate([jnp.tile(small["q_norm_w"], (1, N_Q_HEADS)), jnp.tile(small["k_norm_w"], (1, N_KV_HEADS))], axis=1)
    qk_sc = jnp.concatenate([jnp.full((1, N_Q_HEADS * HEAD_DIM), HEAD_DIM ** -0.5, F32),
                             jnp.ones((1, N_KV_HEADS * HEAD_DIM), F32)], axis=1)
    qk_sc2 = jnp.concatenate([jnp.full((1, N_Q_HEADS * HEAD_DIM), HEAD_DIM ** -0.5 * LOG2E, F32),
                              jnp.ones((1, N_KV_HEADS * HEAD_DIM), F32)], axis=1)
    tabs = _rope_tables(s)
    qk, qkt = _qk_fwd(proj, qk_w, qk_sc2, tabs, name="qk_fwd")
    qkt = qkt.reshape(N_Q_HEADS + N_KV_HEADS, HEAD_DIM, s)
    k_h = _heads(qk[:, N_Q_HEADS * HEAD_DIM:], N_KV_HEADS)
    v_sd = proj[:, V0:V0 + N_KV_HEADS * HEAD_DIM]
    v_h = _heads(v_sd, N_KV_HEADS)
    vta = jnp.concatenate([v_sd.T.reshape(N_KV_HEADS, HEAD_DIM, s), jnp.ones((N_KV_HEADS, V_AUG - HEAD_DIM, s), MMD)], axis=1)
    ot, lse = _flash_fwd(qkt, vta, name="flash_fwd")
    ot2 = ot.reshape(N_Q_HEADS * HEAD_DIM, s)
    if late_weights is not None:
        wts = {**wts, **late_weights(ot)}

    w8 = jnp.pad(small["conv_w"], ((0, 8 - D_CONV), (0, 0)))
    xc = _conv_fwd(proj, w8, small["conv_b"], name="conv_fwd")
    a_neg = -jnp.exp(small["A_log"])
    arow = jnp.pad(a_neg.reshape(1, 2 * SSD_HEADS), ((0, 0), (0, 128 - 2 * SSD_HEADS)))
    bias_row = jnp.pad(small["dt_bias"].reshape(1, 2 * SSD_HEADS), ((0, 0), (0, 128 - 2 * SSD_HEADS)))
    dt, cs = _dt_fwd(dt_raw, bias_row, arow, name="dt_fwd")
    drow = jnp.repeat(small["ssd_D"], SSD_P, axis=1)
    dirs = []
    for di in range(2):
        cols = slice(di * SSD_HEADS, (di + 1) * SSD_HEADS)
        dirs.append(dict(
            dt4=_per_group(dt[:, cols]), cst=_per_group_t(cs[:, cols]),
            drow=drow if di == 0 else jnp.zeros_like(drow),
            arow4=jnp.pad(a_neg[di].reshape(SSD_GROUPS, 1, HPG), ((0, 0), (0, 0), (0, 128 - HPG)))))
    ex = _expand_mats()
    ys = []
    for di, dd in enumerate(dirs):
        y, dd["hprev"] = _ssd_fwd(xc, dt, cs, dd["cst"], ex, dd["drow"], di, name=f"ssd_fwd{di}")
        ys.append(y)
    ssdn = _gnorm_fwd(ys[0], ys[1], proj, small["ssd_norm_w"], name="gnorm_fwd")

    a_o = _mm(ot2, wts["w_attn_out"], name="attn_out", outs=[MMD], ta=True, tm=512, tn=1024)

    def merge_epi(acc, ao, ga, gs):
        return (_sigmoid(ga.astype(F32)) * ao.astype(F32) + _sigmoid(gs.astype(F32)) * acc, acc)

    merged, b_o = _mm(ssdn, wts["w_ssd_out"], name="ssd_out", outs=[MMD, MMD], tm=512, tn=512,
                      extras=[(a_o, "tile", 0), (proj, "tile", GA0), (proj, "tile", GS0)], epi=merge_epi)

    def res_epi(acc, res, gate):
        return (res + gate * acc, acc)

    x1, mo = _mm(merged, wts["w_o"], name="w_o", outs=[F32, MMD], tm=512, tn=512,
                 extras=[(x, "tile", 0), (gate1, "row", 0)], epi=res_epi)
    h2 = _ln_mod(x1, small["norm2_w"], scale2, shift2, name="ln2")

    def relu2_epi(acc):
        rl = jnp.maximum(acc, 0.0)
        return (rl * rl, rl)

    act, rl = _mm(h2, wts["w_mlp1"], name="mlp1", outs=[MMD, MMD], tm=512, tn=1024, epi=relu2_epi, b_outer=True)

    def loss_epi(acc, res, gate, tgt):
        return ((res + gate * acc - tgt) * (1.0 / d), acc)

    dy, ffo = _mm(act, wts["w_mlp2"], name="mlp2", outs=[F32, MMD], tm=512, tn=1024, vmem=VMEM_BIG,
                  extras=[(x1, "tile", 0), (gate2, "row", 0), (target, "tile", 0)], epi=loss_epi)
    loss = _sumsq(dy, name="loss") * (0.5 * d)

    gw = {}
    gs_ = {}
    dffo, dgate2 = _gate_bwd(dy, ffo, gate2, name="gate2_bwd")
    dpre = _mm(dffo, wts["w_mlp2"], name="mlp2_dx", outs=[MMD], nt=True, tm=512, tn=1024,
               extras=[(rl, "tile", 0)], epi=lambda acc, r: (acc * (2.0 * r.astype(F32)),))
    gw["w_mlp2"] = _mm_tn(act, dffo, name="mlp2_dw")
    dh2 = _mm(dpre, wts["w_mlp1"], name="mlp1_dx", outs=[F32], nt=True, tm=512, tn=1024)
    gw["w_mlp1"] = _mm_tn(h2, dpre, name="mlp1_dw")
    dx1, dshift2, dscale2, gs_["norm2_w"] = _ln_mod_bwd(dh2, x1, small["norm2_w"], scale2, dy, name="ln2_bwd")
    dmo, dgate1 = _gate_bwd(dx1, mo, gate1, name="gate1_bwd")

    def merge_bwd_epi(acc, ao, bo, ga, gs):
        sa, ss = _sigmoid(ga.astype(F32)), _sigmoid(gs.astype(F32))
        return (acc * sa, acc * ss, acc * ao.astype(F32) * sa * (1.0 - sa), acc * bo.astype(F32) * ss * (1.0 - ss))

    da_o, db_o, dga, dgs = _mm(dmo, wts["w_o"], name="w_o_dx", outs=[MMD] * 4, nt=True, tm=512, tn=512,
                               extras=[(a_o, "tile", 0), (b_o, "tile", 0), (proj, "tile", GA0), (proj, "tile", GS0)],
                               epi=merge_bwd_epi)
    gw["w_o"] = _mm_tn(merged, dmo, name="w_o_dw")
    dot = _mm(wts["w_attn_out"], da_o, name="attn_out_dx", outs=[MMD], nt=True, tm=512, tn=512)
    gw["w_attn_out"] = _mm(ot2, da_o, name="attn_out_dw", outs=[F32], tm=256, tn=512, vmem=VMEM_BIG)
    dssdn = _mm(db_o, wts["w_ssd_out"], name="ssd_out_dx", outs=[MMD], nt=True, tm=512, tn=512)
    gw["w_ssd_out"] = _mm_tn(ssdn, db_o, name="ssd_out_dw")

    norm_w = small["ssd_norm_w"] if late_grads is None else small["ssd_norm_w"] + late_grads(gw)
    dyssd, dz, gs_["ssd_norm_w"] = _gnorm_bwd(dssdn, ys[0], ys[1], proj, norm_w, name="gnorm_bwd")
    gs_["ssd_D"] = _colsum_prod(dyssd, xc[:, 0:2048], name="ssd_d_grad").reshape(SSD_HEADS, SSD_P).sum(axis=1).reshape(1, SSD_HEADS)
    dxc, ddts, das = [], [], []
    for di, dd in enumerate(dirs):
        dxs, dbm, dcm, ddt4, da4 = _ssd_bwd(xc, dt, cs, dd["dt4"], dd["cst"], ex, dd["drow"], dd["arow4"],
                                            dyssd, dd["hprev"], di, name=f"ssd_bwd{di}")
        dxc.append((dxs, dbm, dcm))
        ddts.append(ddt4.transpose(1, 0, 2).reshape(s, SSD_HEADS))
        das.append(da4.reshape(1, SSD_HEADS))
    conv_parts, col0 = [], 0
    for part, (ga, gb) in enumerate(zip(*dxc)):
        conv_parts.append(_conv_bwd(proj, col0, ga, gb, w8, small["conv_b"], name=f"conv_bwd{part}"))
        col0 += ga.shape[1]
    dxbc, dw8, gs_["conv_b"] = [jnp.concatenate(t, axis=1) for t in zip(*conv_parts)]
    gs_["conv_w"] = dw8[0:D_CONV]
    gs_["A_log"] = jnp.concatenate(das, axis=0) * a_neg
    ddt = jnp.pad(jnp.concatenate(ddts, axis=1), ((0, 0), (0, 128 - 2 * SSD_HEADS)))
    ddt_raw, dbias = _dt_bwd(ddt, dt_raw, bias_row, name="dt_bwd")
    gs_["dt_bias"] = dbias[:, 0:2 * SSD_HEADS].reshape(2, SSD_HEADS)

    dqt, dk_h, dv_h = _flash_bwd(qkt, k_h, v_h, dot.reshape(N_Q_HEADS, HEAD_DIM, s), ot, lse, name="flash_bwd")
    dqkt = jnp.concatenate([dqt.reshape(N_Q_HEADS * HEAD_DIM, s),
                            dk_h.transpose(0, 2, 1).reshape(N_KV_HEADS * HEAD_DIM, s)], axis=0)
    dqk_u, dqk_w = _qk_bwd(dqkt, proj, qk_w, qk_sc, tabs, name="qk_bwd")
    gs_["q_norm_w"] = dqk_w[:, 0:N_Q_HEADS * HEAD_DIM].reshape(N_Q_HEADS, HEAD_DIM).sum(axis=0, keepdims=True)
    gs_["k_norm_w"] = dqk_w[:, N_Q_HEADS * HEAD_DIM:].reshape(N_KV_HEADS, HEAD_DIM).sum(axis=0, keepdims=True)
    dv = _unheads(dv_h).astype(MMD)

    dproj = jnp.concatenate([dz, dga, dgs, dxbc, dqk_u, dv, ddt_raw], axis=1)
    dh1 = _mm(dproj, wts["w_in_p"], name="in_proj_dx", outs=[F32], nt=True, tm=256, tn=1024, vmem=VMEM_BIG)
    gw["w_in_p"] = _mm_tn(h1, dproj, name="in_proj_dw", tk=512, tn=2944, tmm=2048, vmem=VMEM_BIG)
    grad_x, dshift1, dscale1, gs_["norm1_w"] = _ln_mod_bwd(dh1, x, small["norm1_w"], scale1, dx1, name="ln1_bwd")
    dmod = jnp.concatenate([dshift1, dscale1, dgate1, dshift2, dscale2, dgate2], axis=0)
    return loss, grad_x, dmod, gw, gs_


N_DEV = 8
N_CHIP = 4
ANY = pl.BlockSpec(memory_space=pl.ANY)


def _place():
    return lax.axis_index("x"), lax.axis_index("y"), lax.axis_index("c")


def _allgather8(v, *, name):
    m_per, n = v.shape

    def body(x_ref, out_ref, send_sems, recv_sems, local_sem):
        x, y, c = _place()
        me, sibling = (x, y, c), (x, y, 1 - c)
        chips = [(1 - x, y), (x, 1 - y), (1 - x, 1 - y)]

        def rows(px, py, pc):
            return out_ref.at[pl.ds((4 * px + 2 * py + pc) * m_per, m_per), :]

        def copy(k, block, to, src=None):
            return pltpu.make_async_remote_copy(
                src_ref=rows(*block) if src is None else src, dst_ref=rows(*block),
                send_sem=send_sems.at[k], recv_sem=recv_sems.at[k], device_id=to, device_id_type=MESH)

        mine = pltpu.make_async_copy(x_ref, rows(*me), local_sem)
        mine.start()
        first = [copy(0, me, sibling, src=x_ref)]
        first += [copy(1 + j, me, (*chip, c), src=x_ref) for j, chip in enumerate(chips)]
        for cp in first:
            cp.start()
        passed = [copy(4 + j, (*chip, c), sibling) for j, chip in enumerate(chips)]
        for j, chip in enumerate(chips):
            copy(1 + j, (*chip, c), me).wait_recv()
            passed[j].start()
        copy(0, sibling, me).wait_recv()
        for j, chip in enumerate(chips):
            copy(4 + j, (*chip, 1 - c), me).wait_recv()
        for cp in first + passed:
            cp.wait_send()
        mine.wait()

    return pl.pallas_call(
        body, name=name, out_shape=jax.ShapeDtypeStruct((N_DEV * m_per, n), v.dtype),
        in_specs=[pl.BlockSpec(memory_space=pltpu.VMEM)], out_specs=pl.BlockSpec(memory_space=pltpu.VMEM),
        scratch_shapes=[pltpu.SemaphoreType.DMA((7,)), pltpu.SemaphoreType.DMA((7,)), pltpu.SemaphoreType.DMA],
    )(v)


def _scatter_chips(src, *, name):
    def body(x_ref, out_ref, send_sems, recv_sems):
        x, y, c = _place()
        k = 2 * x + y
        chips = [(1 - x, y), (x, 1 - y), (1 - x, 1 - y)]
        ids = [2 * cx + cy for cx, cy in chips]

        def copy(j, slot):
            return pltpu.make_async_remote_copy(
                src_ref=x_ref.at[ids[j]], dst_ref=out_ref.at[slot], send_sem=send_sems.at[j], recv_sem=recv_sems.at[j],
                device_id=(*chips[j], c), device_id_type=MESH)

        sends = [copy(j, k) for j in range(3)]
        for cp in sends:
            cp.start()
        for j in range(3):
            copy(j, ids[j]).wait_recv()
        for cp in sends:
            cp.wait_send()

    return pl.pallas_call(
        body, name=name, out_shape=jax.ShapeDtypeStruct(src.shape, src.dtype), in_specs=[ANY], out_specs=ANY,
        scratch_shapes=[pltpu.SemaphoreType.DMA((3,)), pltpu.SemaphoreType.DMA((3,))],
    )(src)


HBM = pl.BlockSpec(memory_space=pltpu.HBM)
SEM = pl.BlockSpec(memory_space=pltpu.SEMAPHORE)


def _chips_copies(x_ref, land_ref, sems, scatter):
    x, y, c = _place()
    k = 2 * x + y
    chips = [(1 - x, y), (x, 1 - y), (1 - x, 1 - y)]
    ids = [2 * cx + cy for cx, cy in chips]

    def copy(j, slot):
        return pltpu.make_async_remote_copy(
            src_ref=x_ref.at[ids[j]] if scatter else x_ref, dst_ref=land_ref.at[slot], send_sem=sems[j],
            recv_sem=sems[3 + j], device_id=(*chips[j], c), device_id_type=MESH)

    return [copy(j, k) for j in range(3)], [copy(j, ids[j]) for j in range(3)]


def _chips_start(src, scatter, *, name):
    shape = src.shape if scatter else (N_CHIP,) + tuple(src.shape)

    def body(x_ref, land_ref, *rest):
        sems, token = rest[0:6], rest[8]
        for cp in _chips_copies(x_ref, land_ref, sems, scatter)[0]:
            cp.start()
        token[...] = jnp.zeros_like(token)

    out = pl.pallas_call(
        body, name=name,
        out_shape=(pltpu.SemaphoreType.DMA(()),) * 6 + (pltpu.HBM(src.shape, src.dtype), pltpu.HBM(shape, src.dtype),
                                                       jax.ShapeDtypeStruct((8, 128), F32)),
        in_specs=(HBM, HBM), out_specs=(SEM,) * 6 + (HBM, HBM, pl.BlockSpec(memory_space=pltpu.VMEM)),
        input_output_aliases={0: 6, 1: 7},
        compiler_params=pltpu.CompilerParams(has_side_effects=pltpu.SideEffectType.DATAFLOW_SIDE_EFFECTING),
    )(pltpu.with_memory_space_constraint(src, pltpu.HBM),
      pltpu.with_memory_space_constraint(lax.empty(shape, src.dtype), pltpu.HBM))
    return out[0:6], out[6], out[7], out[8]


def _chips_wait(sems, src, land, after, scatter, *, name):
    def body(x_ref, land_ref, *rest):
        sems_ = rest[0:6]
        for cp in _chips_copies(x_ref, land_ref, sems_, scatter)[1]:
            cp.wait_send()
            cp.wait_recv()

    return pl.pallas_call(
        body, name=name, out_shape=(pltpu.HBM(src.shape, src.dtype), pltpu.HBM(land.shape, land.dtype)),
        in_specs=(HBM, HBM) + (SEM,) * 6 + (ANY,), out_specs=(HBM, HBM), input_output_aliases={0: 0, 1: 1},
        compiler_params=pltpu.CompilerParams(has_side_effects=pltpu.SideEffectType.DATAFLOW_SIDE_EFFECTING),
    )(src, land, *sems, after)


def _row_tile(r, pref=512):
    return max(t for t in range(16, pref + 1, 16) if r % t == 0)


def _gather_weights(src, *, name):
    r = src.shape[0]
    hr = r // 2
    assert r == 2 * hr and hr % 16 == 0

    def body(x_ref, out_ref, send_sems, recv_sems):
        x, y, c = _place()
        k = 2 * x + y
        chips = [(1 - x, y), (x, 1 - y), (1 - x, 1 - y)]
        ids = [2 * cx + cy for cx, cy in chips]
        mine_rows = pl.ds(pl.multiple_of(c * hr, 16), hr)
        other_rows = pl.ds(pl.multiple_of((1 - c) * hr, 16), hr)

        def copy(sem, src_ref, slot, rows, to):
            return pltpu.make_async_remote_copy(
                src_ref=src_ref, dst_ref=out_ref.at[slot, rows], send_sem=send_sems.at[sem], recv_sem=recv_sems.at[sem],
                device_id=to, device_id_type=MESH)

        sends = [copy(j, x_ref.at[mine_rows], k, mine_rows, (cx, cy, c)) for j, (cx, cy) in enumerate(chips)]
        for cp in sends:
            cp.start()
        passed = [copy(3 + j, out_ref.at[ids[j], mine_rows], ids[j], mine_rows, (x, y, 1 - c)) for j in range(3)]
        for j, (cx, cy) in enumerate(chips):
            copy(j, x_ref.at[mine_rows], ids[j], mine_rows, (cx, cy, c)).wait_recv()
            passed[j].start()
        for j in range(3):
            copy(3 + j, out_ref.at[ids[j], other_rows], ids[j], other_rows, (x, y, 1 - c)).wait_recv()
        for cp in sends + passed:
            cp.wait_send()

    return pl.pallas_call(
        body, name=name, out_shape=jax.ShapeDtypeStruct((N_CHIP,) + tuple(src.shape), src.dtype),
        in_specs=[ANY], out_specs=ANY,
        scratch_shapes=[pltpu.SemaphoreType.DMA((6,)), pltpu.SemaphoreType.DMA((6,))],
    )(src)


def _pair_swap(a, *, name):
    n, r, cols = a.shape
    hr = r // 2

    def body(x_ref, out_ref, send_sem, recv_sem):
        x, y, c = _place()
        other_rows = pl.ds(pl.multiple_of((1 - c) * hr, 16), hr)
        cp = pltpu.make_async_remote_copy(src_ref=x_ref.at[:, other_rows], dst_ref=out_ref, send_sem=send_sem,
                                          recv_sem=recv_sem, device_id=(x, y, 1 - c), device_id_type=MESH)
        cp.start()
        cp.wait()

    return pl.pallas_call(
        body, name=name, out_shape=jax.ShapeDtypeStruct((n, hr, cols), a.dtype), in_specs=[ANY], out_specs=ANY,
        scratch_shapes=[pltpu.SemaphoreType.DMA, pltpu.SemaphoreType.DMA],
    )(a)


def _sibling_copy(a, *, name):
    def body(x_ref, out_ref, send_sem, recv_sem):
        x, y, c = _place()
        cp = pltpu.make_async_remote_copy(src_ref=x_ref, dst_ref=out_ref, send_sem=send_sem, recv_sem=recv_sem,
                                          device_id=(x, y, 1 - c), device_id_type=MESH)
        cp.start()
        cp.wait()

    return pl.pallas_call(
        body, name=name, out_shape=jax.ShapeDtypeStruct(a.shape, a.dtype), in_specs=[ANY], out_specs=ANY,
        scratch_shapes=[pltpu.SemaphoreType.DMA, pltpu.SemaphoreType.DMA],
    )(a)


def _sum_slots(a, *, name):
    _, r, c = a.shape
    tr = _row_tile(r, 256)

    def body(a_ref, o_ref):
        acc = a_ref[0].astype(F32)
        for j in range(1, N_CHIP):
            acc = acc + a_ref[j].astype(F32)
        o_ref[...] = acc

    return pl.pallas_call(
        body, name=name, grid=(r // tr,), in_specs=[pl.BlockSpec((N_CHIP, tr, c), lambda i: (0, i, 0))],
        out_specs=pl.BlockSpec((tr, c), lambda i: (i, 0)), out_shape=jax.ShapeDtypeStruct((r, c), F32),
        compiler_params=_cp(("arbitrary",)),
    )(a)


def _add2(a, b, *, name):
    r, c = a.shape
    tr = _row_tile(r)

    def body(a_ref, b_ref, o_ref):
        o_ref[...] = (a_ref[...].astype(F32) + b_ref[...].astype(F32)).astype(o_ref.dtype)

    spec = pl.BlockSpec((tr, c), lambda i: (i, 0))
    return pl.pallas_call(
        body, name=name, grid=(r // tr,), in_specs=[spec, spec], out_specs=spec,
        out_shape=jax.ShapeDtypeStruct((r, c), a.dtype), compiler_params=_cp(("arbitrary",)),
    )(a, b)


BIG = ("w_in", "w_mlp1", "w_attn_out", "w_ssd_out", "w_o", "w_mlp2")
COL_SHARDED = ("w_mlp1", "w_in")
ROW_SHARDED = ("w_attn_out", "w_ssd_out", "w_o", "w_mlp2")
LATE = ROW_SHARDED + ("w_mlp1",)
SMALL = ("b_ada", "norm1_w", "norm2_w", "q_norm_w", "k_norm_w", "conv_b", "A_log", "dt_bias", "ssd_D", "ssd_norm_w")
NAMES = ("w_ada", "b_ada", "norm1_w", "norm2_w", "w_in", "q_norm_w", "k_norm_w", "conv_w", "conv_b", "A_log", "dt_bias",
         "ssd_D", "ssd_norm_w", "w_attn_out", "w_ssd_out", "w_o", "w_mlp1", "w_mlp2")
W_IN_COLS = 8768


def _permute_in(w):
    return jnp.concatenate([w[:, 4608:6656], w[:, 6720:8768], w[:, 1536:4608], w[:, 0:1536], w[:, 6656:6720],
                            jnp.zeros((w.shape[0], PW - W_IN_COLS), w.dtype)], axis=1)


def _unpermute_in(wp):
    return jnp.concatenate([wp[:, Q0:DT0], wp[:, XS0:Q0], wp[:, Z0:GA0], wp[:, DT0:DT0 + 64], wp[:, GA0:XS0]], axis=1)


def _pad_to(v, n):
    return jnp.pad(v, (0, n - v.shape[0]))


def _step(w, m, v, loss_target):
    xi, yi, ci = _place()
    chip = 2 * xi + yi
    dev = 4 * xi + 2 * yi + ci
    x, tgt = w["x"], loss_target
    d = x.shape[1]

    cw = w["conv_w"].shape[1]
    v0 = _pad_to(jnp.concatenate([w["c"].reshape(-1), w["conv_w"].reshape(-1)]), 5120).reshape(8, 640)
    g0 = _allgather8(v0, name="ag_cond").reshape(N_DEV, 5120)
    c_all = g0[:, 0:d]
    conv_w = jnp.concatenate([g0[2 * k, d:d + D_CONV * cw].reshape(D_CONV, cw) for k in range(N_CHIP)], axis=1)
    sc = _silu_cast(c_all, name="silu_c")
    modp = _mm(sc, w["w_ada"].astype(MMD), name="ada_fwd", outs=[F32], tm=8, tn=512)
    g1 = _allgather8(modp, name="ag_mod").reshape(N_DEV, N_DEV, modp.shape[1])
    mod_all = jnp.concatenate([g1[2 * k] for k in range(N_CHIP)], axis=1)
    mod = (lax.dynamic_slice_in_dim(mod_all, dev, 1, axis=0) + w["b_ada"]).reshape(6, d)

    mine = w["w_in"].astype(MMD)
    gath = lax.dynamic_update_slice_in_dim(_gather_weights(mine, name="ag_w_in"), mine[None], chip, axis=0)
    late_mine = jnp.concatenate([w[n].astype(MMD) for n in LATE], axis=0)
    late_mine, gath = lax.optimization_barrier((late_mine, gath))
    ag_sems, ag_src, ag_land, ag_token = _chips_start(late_mine, False, name="ag_late_start")
    mod = mod + ag_token[0:1, 0:1]
    w_in = jnp.concatenate([gath[k] for k in range(N_CHIP)], axis=1)
    wts = {"w_in_p": _permute_in(w_in), "w_dt": jnp.pad(w_in[:, 6656:6720], ((0, 0), (0, 64)))}
    small = {n: w[n] for n in SMALL if n != "b_ada"}
    small["conv_w"] = conv_w

    def own_slot(land, src):
        return lax.dynamic_update_slice_in_dim(land, src, chip, axis=0)

    def late_weights(after):
        src, land = _chips_wait(ag_sems, ag_src, ag_land, after, False, name="ag_late_wait")
        land = own_slot(land, src[None])
        out, o = {}, 0
        for n in LATE:
            rows = w[n].shape[0]
            part = land[:, o:o + rows]
            out[n] = (jnp.concatenate([part[k] for k in range(N_CHIP)], axis=1) if n in COL_SHARDED
                      else part.reshape(N_CHIP * rows, w[n].shape[1]))
            o += rows
        return out

    def pair_sums(slots, tag):
        _, rows, cols = slots.shape
        hr = rows // 2
        theirs = _pair_swap(slots, name="rs_pair_" + tag)
        ours = lax.dynamic_slice_in_dim(slots, ci * hr, hr, axis=1)
        pair = _add2(ours.reshape(N_CHIP * hr, cols), theirs.reshape(N_CHIP * hr, cols), name="rs_pair_sum_" + tag)
        return pair.reshape(N_CHIP, hr, cols)

    def finish(recv, pair, tag):
        recv = own_slot(recv, lax.dynamic_slice_in_dim(pair, chip, 1, axis=0))
        half = _sum_slots(recv, name="rs_sum_" + tag)
        other = _sibling_copy(half, name="rs_sibling_" + tag)
        return jnp.where(ci == 0, jnp.concatenate([half, other], axis=0), jnp.concatenate([other, half], axis=0))

    started = {}

    def late_grads(gw):
        slots = []
        for k in range(N_CHIP):
            parts = []
            for n in LATE:
                rows = w[n].shape[0]
                blk = gw[n][:, k * rows:(k + 1) * rows] if n in COL_SHARDED else gw[n][k * rows:(k + 1) * rows]
                parts.append(blk.astype(MMD))
            slots.append(jnp.concatenate(parts, axis=0))
        pair = pair_sums(jnp.stack(slots), "late")
        sems, src, land, token = _chips_start(pair, True, name="rs_late_start")
        started["late"] = (sems, src, land)
        return token[0:1, 0:1]

    loss, grad_x, dmod, gw, gs = _local_step(x, tgt, mod, wts, small, late_weights, late_grads)

    grads = {}
    g_in = _unpermute_in(gw.pop("w_in_p"))
    cols_in = w["w_in"].shape[1]
    pair = pair_sums(jnp.stack([g_in[:, k * cols_in:(k + 1) * cols_in].astype(MMD) for k in range(N_CHIP)]), "w_in")
    grads["w_in"] = finish(_scatter_chips(pair, name="rs_w_in"), pair, "w_in")
    pair, land = _chips_wait(*started["late"], grad_x, True, name="rs_late_wait")
    total, o = finish(land, pair, "late"), 0
    for n in LATE:
        rows = w[n].shape[0]
        grads[n] = total[o:o + rows]
        o += rows

    order = ([dmod.reshape(-1)] + [gs[n].reshape(-1) for n in SMALL if n != "b_ada"] + [gs["conv_w"].reshape(-1)]
             + [loss.reshape(-1)])
    vec = jnp.concatenate(order)
    n_small = vec.shape[0]
    n_pad = -(-n_small // 1024) * 1024
    g2 = _allgather8(_pad_to(vec, n_pad).reshape(8, n_pad // 8), name="ag_small")
    tot = _rows_sum(g2, N_DEV, name="small_sum").reshape(-1)
    loss = tot[n_small - 1]
    dmod_all = g2.reshape(N_DEV, n_pad)[:, 0:6 * d]
    off = 0
    for n in SMALL:
        grads[n] = tot[off:off + w[n].size].reshape(w[n].shape)
        off += w[n].size
    conv_full = tot[off:off + D_CONV * N_CHIP * cw].reshape(D_CONV, N_CHIP * cw)
    grads["conv_w"] = lax.dynamic_slice_in_dim(conv_full, chip * cw, cw, axis=1)
    ada_cols = w["w_ada"].shape[1]
    dmod_mine = lax.dynamic_slice_in_dim(dmod_all, chip * ada_cols, ada_cols, axis=1).astype(MMD)
    grads["w_ada"] = _mm_tn(sc, dmod_mine, name="ada_dw", tk=512, tn=512, tmm=8)

    delta, new_m, new_v = {}, {}, {}
    pack = lambda t: jnp.concatenate([t[n].reshape(-1) for n in SMALL]).reshape(1, -1)
    ds_, ms_, vs_ = _adamw(pack(w), pack(grads), pack(m), pack(v), name="adamw_small")
    off = 0
    for n in SMALL:
        for dst, src in ((delta, ds_), (new_m, ms_), (new_v, vs_)):
            dst[n] = src[0, off:off + w[n].size].reshape(w[n].shape)
        off += w[n].size
    for n in ("w_ada", "conv_w") + BIG:
        delta[n], new_m[n], new_v[n] = _adamw(w[n], grads[n], m[n], v[n], name="adamw_" + n)
    return loss, grad_x, grads, delta, new_m, new_v


def kernel(x, c, w_ada, b_ada, norm1_w, norm2_w, w_in, q_norm_w, k_norm_w, conv_w, conv_b, A_log, dt_bias, ssd_D, ssd_norm_w, w_attn_out, w_ssd_out, w_o, w_mlp1, w_mlp2, loss_target, m_w_ada, m_b_ada, m_norm1_w, m_norm2_w, m_w_in, m_q_norm_w, m_k_norm_w, m_conv_w, m_conv_b, m_A_log, m_dt_bias, m_ssd_D, m_ssd_norm_w, m_w_attn_out, m_w_ssd_out, m_w_o, m_w_mlp1, m_w_mlp2, v_w_ada, v_b_ada, v_norm1_w, v_norm2_w, v_w_in, v_q_norm_w, v_k_norm_w, v_conv_w, v_conv_b, v_A_log, v_dt_bias, v_ssd_D, v_ssd_norm_w, v_w_attn_out, v_w_ssd_out, v_w_o, v_w_mlp1, v_w_mlp2):
    args = dict(locals())
    strip = lambda a: a[0] if a.ndim == 3 else a
    w = {n: strip(args[n]) for n in NAMES + ("x", "c")}
    m = {n: strip(args["m_" + n]) for n in NAMES}
    v = {n: strip(args["v_" + n]) for n in NAMES}
    loss, grad_x, grads, delta, new_m, new_v = _step(w, m, v, loss_target[0])
    like = lambda t, n: t.reshape(args[n].shape)
    return (loss, grad_x[None], *[like(grads[n], n) for n in NAMES], *[like(delta[n], n) for n in NAMES],
            *[like(new_m[n], n) for n in NAMES], *[like(new_v[n], n) for n in NAMES])
```

```python
import functools
import math

import jax
import jax.numpy as jnp
from jax import lax
from jax.experimental import pallas as pl
from jax.experimental.pallas import tpu as pltpu

F32 = jnp.float32
MMD = jnp.bfloat16
EPS = 1e-6
NEG = -1e30
MIB = 1024 * 1024
VMEM_BIG = 56 * MIB
VMEM_MID = 40 * MIB

GRID_W = 64
N_Q_HEADS, N_KV_HEADS, HEAD_DIM = 16, 4, 64
ROPE_THETA = 10000.0
SSD_HEADS, SSD_GROUPS, SSD_P, SSD_N, CHUNK = 32, 4, 64, 128, 128
HPG = SSD_HEADS // SSD_GROUPS
D_CONV = 5
ADAM_LR, ADAM_B1, ADAM_B2, ADAM_EPS, ADAM_WD, ADAM_STEP = 0.001, 0.9, 0.999, 1e-08, 0.01, 10

Z0, GA0, GS0, XS0, B0, C0, Q0, K0, V0, DT0, PW = 0, 2048, 3072, 4096, 6144, 6656, 7168, 8192, 8448, 8704, 8832

MESH = pl.DeviceIdType.MESH
NT = (((1,), (1,)), ((), ()))
TN = (((0,), (0,)), ((), ()))


def _cp(sem=None, vmem=VMEM_MID):
    return pltpu.CompilerParams(dimension_semantics=sem, vmem_limit_bytes=vmem)


def _tile(n, pref):
    t = min(n, pref)
    while n % t:
        t //= 2
    return t


def _dot(a, b, dims=None):
    if dims is None:
        return jnp.dot(a, b, preferred_element_type=F32)
    return lax.dot_general(a, b, dims, preferred_element_type=F32)


def _dot_hi(a, b):
    return jnp.dot(a, b, precision=lax.Precision.HIGHEST, preferred_element_type=F32)


def _sigmoid(x):
    return jax.nn.sigmoid(x)


def _mm(a, b, *, name, outs, nt=False, ta=False, extras=(), epi=None, tm=512, tn=512, n=None, b_outer=False,
        vmem=VMEM_MID):
    assert not (nt and ta)
    k, m = a.shape if ta else a.shape[::-1]
    if n is None:
        n = b.shape[0] if nt else b.shape[1]
    tm, tn = _tile(m, tm), _tile(n, tn)
    gi, gj = m // tm, n // tn
    if b_outer:
        grid = (gj, gi)
        ij = lambda p, q: (q, p)
    else:
        grid = (gi, gj)
        ij = lambda p, q: (p, q)
    if ta:
        a_spec = pl.BlockSpec((k, tm), lambda p, q: (0, ij(p, q)[0]))
    else:
        a_spec = pl.BlockSpec((tm, k), lambda p, q: (ij(p, q)[0], 0))
    if nt:
        b_spec = pl.BlockSpec((tn, k), lambda p, q: (ij(p, q)[1], 0))
    else:
        b_spec = pl.BlockSpec((k, tn), lambda p, q: (0, ij(p, q)[1]))
    e_specs = []
    for arr, kind, off in extras:
        ob = off // tn
        assert off % tn == 0
        if kind == "tile":
            e_specs.append(pl.BlockSpec((tm, tn), lambda p, q, ob=ob: (ij(p, q)[0], ob + ij(p, q)[1])))
        else:
            e_specs.append(pl.BlockSpec((1, tn), lambda p, q, ob=ob: (0, ob + ij(p, q)[1])))
    ne = len(extras)

    def body(a_ref, b_ref, *rest):
        acc = _dot(a_ref[...], b_ref[...], NT if nt else (TN if ta else None))
        res = epi(acc, *[e[...] for e in rest[:ne]]) if epi is not None else (acc,)
        for o_ref, r in zip(rest[ne:], res):
            o_ref[...] = r.astype(o_ref.dtype)

    out = pl.pallas_call(
        body, name=name, grid=grid,
        in_specs=[a_spec, b_spec] + e_specs,
        out_specs=[pl.BlockSpec((tm, tn), lambda p, q: ij(p, q)) for _ in outs],
        out_shape=[jax.ShapeDtypeStruct((m, n), dt) for dt in outs],
        compiler_params=_cp(("arbitrary", "arbitrary"), vmem),
    )(a, b, *[e[0] for e in extras])
    return out if len(outs) > 1 else out[0]


def _mm_tn(a, g, *, name, tk=512, tn=1024, tmm=4096, vmem=VMEM_MID):
    m, k = a.shape
    n = g.shape[1]
    tk, tn, tmm = _tile(k, tk), _tile(n, tn), _tile(m, tmm)

    def body(a_ref, g_ref, o_ref):
        p = _dot(a_ref[...], g_ref[...], TN)

        @pl.when(pl.program_id(2) == 0)
        def _():
            o_ref[...] = p

        @pl.when(pl.program_id(2) > 0)
        def _():
            o_ref[...] += p

    return pl.pallas_call(
        body, name=name, grid=(k // tk, n // tn, m // tmm),
        in_specs=[pl.BlockSpec((tmm, tk), lambda i, j, r: (r, i)), pl.BlockSpec((tmm, tn), lambda i, j, r: (r, j))],
        out_specs=pl.BlockSpec((tk, tn), lambda i, j, r: (i, j)),
        out_shape=jax.ShapeDtypeStruct((k, n), F32),
        compiler_params=_cp(("arbitrary", "arbitrary", "arbitrary"), vmem),
    )(a, g)


def _adamw(w, g, m, v, *, name):
    r, c = w.shape
    tr = _tile(r, 256) if r % 8 == 0 else r

    def body(w_ref, g_ref, m_ref, v_ref, d_ref, nm_ref, nv_ref):
        gg = g_ref[...]
        nm = ADAM_B1 * m_ref[...] + (1.0 - ADAM_B1) * gg
        nv = ADAM_B2 * v_ref[...] + (1.0 - ADAM_B2) * jnp.square(gg)
        m_hat = nm / (1.0 - ADAM_B1 ** ADAM_STEP)
        v_hat = nv / (1.0 - ADAM_B2 ** ADAM_STEP)
        d_ref[...] = -ADAM_LR * (m_hat / (jnp.sqrt(v_hat) + ADAM_EPS) + ADAM_WD * w_ref[...])
        nm_ref[...] = nm
        nv_ref[...] = nv

    spec = pl.BlockSpec((tr, c), lambda i: (i, 0))
    return pl.pallas_call(
        body, name=name, grid=(r // tr,), in_specs=[spec] * 4, out_specs=[spec] * 3,
        out_shape=[jax.ShapeDtypeStruct((r, c), F32)] * 3, compiler_params=_cp(("arbitrary",)),
    )(w, g, m, v)


def _rows_sum(a, groups, *, name):
    r = a.shape[0] // groups

    def body(a_ref, o_ref):
        acc = a_ref[0:r, :]
        for d in range(1, groups):
            acc = acc + a_ref[d * r:(d + 1) * r, :]
        o_ref[...] = acc

    return pl.pallas_call(body, name=name, out_shape=jax.ShapeDtypeStruct((r, a.shape[1]), F32))(a)


def _silu_cast(a, *, name):
    def body(a_ref, o_ref):
        x = a_ref[...]
        o_ref[...] = (x * _sigmoid(x)).astype(o_ref.dtype)

    return pl.pallas_call(body, name=name, out_shape=jax.ShapeDtypeStruct(a.shape, MMD))(a)


def _sumsq(a, *, name):
    m, n = a.shape
    tm = _tile(m, 512)

    def body(a_ref, o_ref):
        x = a_ref[...]
        p = jnp.sum(jnp.sum(x * x, axis=1, keepdims=True), axis=0, keepdims=True)

        @pl.when(pl.program_id(0) == 0)
        def _():
            o_ref[...] = p

        @pl.when(pl.program_id(0) > 0)
        def _():
            o_ref[...] += p

    return pl.pallas_call(
        body, name=name, grid=(m // tm,), in_specs=[pl.BlockSpec((tm, n), lambda i: (i, 0))],
        out_specs=pl.BlockSpec((1, 1), lambda i: (0, 0)), out_shape=jax.ShapeDtypeStruct((1, 1), F32),
        compiler_params=_cp(("arbitrary",)),
    )(a)


def _acc_rows(o_ref, p, first):
    @pl.when(first)
    def _():
        o_ref[...] = p

    @pl.when(jnp.logical_not(first))
    def _():
        o_ref[...] += p


def _ln_mod(x, w, scale, shift, *, name):
    s, d = x.shape
    tm = _tile(s, 512)

    def body(x_ref, w_ref, sc_ref, sh_ref, o_ref):
        xv = x_ref[...]
        r = lax.rsqrt(jnp.mean(xv * xv, axis=-1, keepdims=True) + EPS)
        o_ref[...] = ((xv * r) * w_ref[...] * (1.0 + sc_ref[...]) + sh_ref[...]).astype(o_ref.dtype)

    row = pl.BlockSpec((1, d), lambda i: (0, 0))
    big = pl.BlockSpec((tm, d), lambda i: (i, 0))
    return pl.pallas_call(
        body, name=name, grid=(s // tm,), in_specs=[big, row, row, row], out_specs=big,
        out_shape=jax.ShapeDtypeStruct((s, d), MMD), compiler_params=_cp(("arbitrary",)),
    )(x, w, scale, shift)


def _ln_mod_bwd(dh, x, w, scale, dres, *, name):
    s, d = x.shape
    tm = _tile(s, 512)

    def body(dh_ref, x_ref, w_ref, sc_ref, dres_ref, dx_ref, dsh_ref, dsc_ref, dw_ref):
        xv = x_ref[...]
        dhv = dh_ref[...].astype(F32)
        r = lax.rsqrt(jnp.mean(xv * xv, axis=-1, keepdims=True) + EPS)
        nv = xv * r
        wv = w_ref[...]
        g1 = 1.0 + sc_ref[...]
        dn = dhv * (wv * g1)
        dx_ref[...] = dres_ref[...] + r * (dn - nv * jnp.mean(dn * nv, axis=-1, keepdims=True))
        first = pl.program_id(0) == 0
        _acc_rows(dsh_ref, jnp.sum(dhv, axis=0, keepdims=True), first)
        _acc_rows(dsc_ref, jnp.sum(dhv * nv * wv, axis=0, keepdims=True), first)
        _acc_rows(dw_ref, jnp.sum(dhv * nv * g1, axis=0, keepdims=True), first)

    row = pl.BlockSpec((1, d), lambda i: (0, 0))
    big = pl.BlockSpec((tm, d), lambda i: (i, 0))
    return pl.pallas_call(
        body, name=name, grid=(s // tm,), in_specs=[big, big, row, row, big], out_specs=[big, row, row, row],
        out_shape=[jax.ShapeDtypeStruct((s, d), F32)] + [jax.ShapeDtypeStruct((1, d), F32)] * 3,
        compiler_params=_cp(("arbitrary",)),
    )(dh, x, w, scale, dres)


def _gate_bwd(dy, u, gate, *, name):
    s, d = dy.shape
    tm = _tile(s, 512)

    def body(dy_ref, u_ref, g_ref, du_ref, dg_ref):
        dyv = dy_ref[...]
        du_ref[...] = (dyv * g_ref[...]).astype(du_ref.dtype)
        _acc_rows(dg_ref, jnp.sum(dyv * u_ref[...].astype(F32), axis=0, keepdims=True), pl.program_id(0) == 0)

    row = pl.BlockSpec((1, d), lambda i: (0, 0))
    big = pl.BlockSpec((tm, d), lambda i: (i, 0))
    return pl.pallas_call(
        body, name=name, grid=(s // tm,), in_specs=[big, big, row], out_specs=[big, row],
        out_shape=[jax.ShapeDtypeStruct((s, d), MMD), jax.ShapeDtypeStruct((1, d), F32)],
        compiler_params=_cp(("arbitrary",)),
    )(dy, u, gate)


def _seg64(v, e):
    hi = v.astype(jnp.bfloat16)
    lo = (v - hi.astype(F32)).astype(jnp.bfloat16)
    return _dot(hi, e) + _dot(lo, e)


def _rope_tables(s):
    rows = s // GRID_W
    pos_row = jnp.repeat(jnp.arange(rows, dtype=jnp.int32), GRID_W).astype(F32)
    pos_col = jnp.tile(jnp.arange(GRID_W, dtype=jnp.int32), rows).astype(F32)
    axis_dim = HEAD_DIM // 2
    inv_freq = ROPE_THETA ** (-jnp.arange(0, axis_dim, 2, dtype=F32) / axis_dim)
    ang_r = pos_row[:, None] * inv_freq[None, :]
    ang_c = pos_col[:, None] * inv_freq[None, :]
    zero = jnp.zeros_like(ang_r)
    cos = jnp.concatenate([jnp.cos(ang_r), jnp.cos(ang_r), jnp.cos(ang_c), jnp.cos(ang_c)], axis=1)
    s_a = jnp.concatenate([-jnp.sin(ang_r), zero, -jnp.sin(ang_c), zero], axis=1)
    s_b = jnp.concatenate([zero, jnp.sin(ang_r), zero, jnp.sin(ang_c)], axis=1)
    return [jnp.tile(t, (1, 2)) for t in (cos, s_a, s_b)]


def _e128():
    i = jnp.arange(128)
    return (i[:, None] // 64 == i[None, :] // 64).astype(jnp.bfloat16)


QKW = N_Q_HEADS * HEAD_DIM + N_KV_HEADS * HEAD_DIM


def _qk_fwd(proj, wrow, scrow, tabs, *, name):
    s = proj.shape[0]
    tm = _tile(s, 1024)

    def body(x_ref, w_ref, sc_ref, cos_ref, sa_ref, sb_ref, e_ref, o_ref, ot_ref):
        u = x_ref[...].astype(F32)
        r = lax.rsqrt(_seg64(u * u, e_ref[...]) * (1.0 / HEAD_DIM) + EPS)
        nv = (u * r) * w_ref[...]
        ro = nv * cos_ref[...] + pltpu.roll(nv, 112, 1) * sa_ref[...] + pltpu.roll(nv, 16, 1) * sb_ref[...]
        out = ro * sc_ref[...]
        o_ref[...] = out.astype(o_ref.dtype)
        ot_ref[...] = out.T.astype(ot_ref.dtype)

    tab = pl.BlockSpec((tm, 128), lambda i, j: (i, 0))
    row = pl.BlockSpec((1, 128), lambda i, j: (0, j))
    return pl.pallas_call(
        body, name=name, grid=(s // tm, QKW // 128),
        in_specs=[pl.BlockSpec((tm, 128), lambda i, j: (i, Q0 // 128 + j)), row, row, tab, tab, tab,
                  pl.BlockSpec((128, 128), lambda i, j: (0, 0))],
        out_specs=[pl.BlockSpec((tm, 128), lambda i, j: (i, j)), pl.BlockSpec((128, tm), lambda i, j: (j, i))],
        out_shape=[jax.ShapeDtypeStruct((s, QKW), MMD), jax.ShapeDtypeStruct((QKW, s), MMD)],
        compiler_params=_cp(("arbitrary", "arbitrary")),
    )(proj, wrow, scrow, *tabs, _e128())


def _qk_bwd(dqkt, proj, wrow, scrow, tabs, *, name):
    s = proj.shape[0]
    tm = _tile(s, 1024)

    def body(d_ref, x_ref, w_ref, sc_ref, cos_ref, sa_ref, sb_ref, e_ref, du_ref, dw_ref):
        e = e_ref[...]
        d = d_ref[...].T * sc_ref[...]
        dn = d * cos_ref[...] + pltpu.roll(d * sa_ref[...], 16, 1) + pltpu.roll(d * sb_ref[...], 112, 1)
        u = x_ref[...].astype(F32)
        r = lax.rsqrt(_seg64(u * u, e) * (1.0 / HEAD_DIM) + EPS)
        uh = u * r
        _acc_rows(dw_ref, jnp.sum(dn * uh, axis=0, keepdims=True), pl.program_id(1) == 0)
        dnw = dn * w_ref[...]
        du_ref[...] = (r * (dnw - uh * (_seg64(dnw * uh, e) * (1.0 / HEAD_DIM)))).astype(du_ref.dtype)

    tab = pl.BlockSpec((tm, 128), lambda j, i: (i, 0))
    row = pl.BlockSpec((1, 128), lambda j, i: (0, j))
    return pl.pallas_call(
        body, name=name, grid=(QKW // 128, s // tm),
        in_specs=[pl.BlockSpec((128, tm), lambda j, i: (j, i)), pl.BlockSpec((tm, 128), lambda j, i: (i, Q0 // 128 + j)),
                  row, row, tab, tab, tab, pl.BlockSpec((128, 128), lambda j, i: (0, 0))],
        out_specs=[pl.BlockSpec((tm, 128), lambda j, i: (i, j)), row],
        out_shape=[jax.ShapeDtypeStruct((s, QKW), MMD), jax.ShapeDtypeStruct((1, QKW), F32)],
        compiler_params=_cp(("arbitrary", "arbitrary")),
    )(dqkt, proj, wrow, scrow, *tabs, _e128())


REP = N_Q_HEADS // N_KV_HEADS


def _lanes(ref):
    return jnp.concatenate([ref[r] for r in range(REP)], axis=1)


V_AUG = HEAD_DIM + 8
LOG2E = math.log2(math.e)


def _flash_fwd(qkt, vta, *, name):
    s = qkt.shape[2]
    tq, tk = _tile(s, 1024), _tile(s, 512)
    nk = s // tk
    lanes = REP * tq

    def body(q_ref, k_ref, v_ref, o_ref, lse_ref, m_ref, acc_ref):
        j = pl.program_id(2)

        @pl.when(j == 0)
        def _():
            m_ref[...] = jnp.full_like(m_ref, NEG)
            acc_ref[...] = jnp.zeros_like(acc_ref)

        st = _dot(k_ref[0], _lanes(q_ref), TN)
        m_prev = m_ref[...]
        m_new = jnp.maximum(m_prev, jnp.max(st, axis=0, keepdims=True))
        p = jnp.exp2(st - m_new).astype(MMD)
        acc_ref[...] = jnp.exp2(m_prev - m_new) * acc_ref[...] + _dot(v_ref[0], p)
        m_ref[...] = m_new

        @pl.when(j == nk - 1)
        def _():
            acc = acc_ref[...]
            l = acc[HEAD_DIM:HEAD_DIM + 1]
            o = acc[0:HEAD_DIM] / l
            ls = m_ref[...] + jnp.log(l) * LOG2E
            for r in range(REP):
                o_ref[r] = o[:, r * tq:(r + 1) * tq].astype(o_ref.dtype)
                lse_ref[r] = ls[:, r * tq:(r + 1) * tq]

    qspec = pl.BlockSpec((REP, HEAD_DIM, tq), lambda g, i, j: (g, 0, i))
    return pl.pallas_call(
        body, name=name, grid=(N_KV_HEADS, s // tq, nk),
        in_specs=[qspec, pl.BlockSpec((1, HEAD_DIM, tk), lambda g, i, j: (N_Q_HEADS + g, 0, j)),
                  pl.BlockSpec((1, V_AUG, tk), lambda g, i, j: (g, 0, j))],
        out_specs=[qspec, pl.BlockSpec((REP, 1, tq), lambda g, i, j: (g, 0, i))],
        out_shape=[jax.ShapeDtypeStruct((N_Q_HEADS, HEAD_DIM, s), MMD), jax.ShapeDtypeStruct((N_Q_HEADS, 1, s), F32)],
        scratch_shapes=[pltpu.VMEM((1, lanes), F32), pltpu.VMEM((V_AUG, lanes), F32)],
        compiler_params=_cp(("arbitrary", "arbitrary", "arbitrary"), VMEM_BIG),
    )(qkt, qkt, vta)


def _flash_bwd(qkt, k_h, v_h, dot, ot, lse, *, name):
    s = qkt.shape[2]
    tq, tk = _tile(s, 512), _tile(s, 1024)
    nk = s // tk

    def body(q_ref, kt_ref, k_ref, v_ref, do_ref, o_ref, lse_ref, dq_ref, dk_ref, dv_ref, dq_acc):
        i, j = pl.program_id(1), pl.program_id(2)
        q, do = _lanes(q_ref), _lanes(do_ref)
        delta = jnp.sum(do.astype(F32) * _lanes(o_ref).astype(F32), axis=0, keepdims=True)
        k, v = k_ref[0], v_ref[0]
        p = jnp.exp2(_dot(k, q) - _lanes(lse_ref))
        dvc = _dot(p.astype(MMD), do, NT)
        ds = (p * (_dot(v, do) - delta)).astype(MMD)
        dkc = _dot(ds, q, NT) * (1.0 / LOG2E)
        dqc = _dot(kt_ref[0], ds)
        rows = pl.ds(pl.multiple_of(j * tk, tk), tk)

        @pl.when(i == 0)
        def _():
            dk_ref[0, rows, :] = dkc
            dv_ref[0, rows, :] = dvc

        @pl.when(i > 0)
        def _():
            dk_ref[0, rows, :] += dkc
            dv_ref[0, rows, :] += dvc

        @pl.when(j == 0)
        def _():
            dq_acc[...] = dqc

        @pl.when(j > 0)
        def _():
            dq_acc[...] += dqc

        @pl.when(j == nk - 1)
        def _():
            acc = dq_acc[...]
            for r in range(REP):
                dq_ref[r] = acc[:, r * tq:(r + 1) * tq]

    qspec = pl.BlockSpec((REP, HEAD_DIM, tq), lambda g, i, j: (g, 0, i))
    kvin = pl.BlockSpec((1, tk, HEAD_DIM), lambda g, i, j: (g, j, 0))
    kvres = pl.BlockSpec((1, s, HEAD_DIM), lambda g, i, j: (g, 0, 0))
    return pl.pallas_call(
        body, name=name, grid=(N_KV_HEADS, s // tq, nk),
        in_specs=[qspec, pl.BlockSpec((1, HEAD_DIM, tk), lambda g, i, j: (N_Q_HEADS + g, 0, j)), kvin, kvin,
                  qspec, qspec, pl.BlockSpec((REP, 1, tq), lambda g, i, j: (g, 0, i))],
        out_specs=[qspec, kvres, kvres],
        out_shape=[jax.ShapeDtypeStruct((N_Q_HEADS, HEAD_DIM, s), F32), jax.ShapeDtypeStruct((N_KV_HEADS, s, HEAD_DIM), F32),
                   jax.ShapeDtypeStruct((N_KV_HEADS, s, HEAD_DIM), F32)],
        scratch_shapes=[pltpu.VMEM((HEAD_DIM, REP * tq), F32)],
        compiler_params=_cp(("arbitrary", "arbitrary", "arbitrary"), VMEM_BIG),
    )(qkt, qkt, k_h, v_h, dot, ot, lse)


HALO = 8
CONV_W = 2048 + 2 * SSD_GROUPS * SSD_N


def _shifted(win, off, r):
    return pltpu.roll(win, (r + 2 * HALO - off) % (r + 2 * HALO), 0)[0:r]


def _conv_fwd(proj, w8, brow, *, name):
    s = proj.shape[0]
    cb = 256
    r = _tile(s, 512)

    def body(x_ref, w_ref, b_ref, o_ref, pad_ref):
        zeros = jnp.zeros((HALO, cb), F32)
        pad_ref[0:HALO, :] = zeros
        pad_ref[s + HALO:s + 2 * HALO, :] = zeros

        def fill(i, carry):
            st = pl.multiple_of(i * r, r)
            pad_ref[pl.ds(st + HALO, r), :] = x_ref[pl.ds(st, r), :].astype(F32)
            return carry

        lax.fori_loop(0, s // r, fill, 0)
        wv = w_ref[...]
        bv = b_ref[...]

        def step(i, carry):
            st = pl.multiple_of(i * r, r)
            win = pad_ref[pl.ds(st, r + 2 * HALO), :]
            acc = bv + wv[0:1, :] * _shifted(win, HALO - 2, r)
            for t in range(1, D_CONV):
                acc = acc + wv[t:t + 1, :] * _shifted(win, HALO - 2 + t, r)
            o_ref[pl.ds(st, r), :] = (acc * _sigmoid(acc)).astype(o_ref.dtype)
            return carry

        lax.fori_loop(0, s // r, step, 0)

    return pl.pallas_call(
        body, name=name, grid=(CONV_W // cb,),
        in_specs=[pl.BlockSpec((s, cb), lambda j: (0, XS0 // cb + j)), pl.BlockSpec((8, cb), lambda j: (0, j)),
                  pl.BlockSpec((1, cb), lambda j: (0, j))],
        out_specs=pl.BlockSpec((s, cb), lambda j: (0, j)),
        out_shape=jax.ShapeDtypeStruct((s, CONV_W), MMD),
        scratch_shapes=[pltpu.VMEM((s + 2 * HALO, cb), F32)],
        compiler_params=_cp(("arbitrary",), VMEM_MID),
    )(proj, w8, brow)


def _conv_bwd(proj, col0, ga, gb, w8, brow, *, name):
    s = proj.shape[0]
    width = ga.shape[1]
    cb = 128
    c0 = col0 // cb
    r = _tile(s, 512)

    def body(x_ref, ga_ref, gb_ref, w_ref, b_ref, dx_ref, dw_ref, db_ref, xpad, dpad):
        zeros = jnp.zeros((HALO, cb), F32)
        for ref in (xpad, dpad):
            ref[0:HALO, :] = zeros
            ref[s + HALO:s + 2 * HALO, :] = zeros

        def fill(i, carry):
            st = pl.multiple_of(i * r, r)
            xpad[pl.ds(st + HALO, r), :] = x_ref[pl.ds(st, r), :].astype(F32)
            return carry

        lax.fori_loop(0, s // r, fill, 0)
        wv = w_ref[...]
        bv = b_ref[...]

        def first(i, carry):
            st = pl.multiple_of(i * r, r)
            win = xpad[pl.ds(st, r + 2 * HALO), :]
            taps = [_shifted(win, HALO - 2 + t, r) for t in range(D_CONV)]
            u = bv
            for t in range(D_CONV):
                u = u + wv[t:t + 1, :] * taps[t]
            sg = _sigmoid(u)
            du = (ga_ref[pl.ds(st, r), :] + gb_ref[pl.ds(st, r), :]) * (sg * (1.0 + u * (1.0 - sg)))
            dpad[pl.ds(st + HALO, r), :] = du
            out = [carry[0] + jnp.sum(du, axis=0, keepdims=True)]
            for t in range(D_CONV):
                out.append(carry[1 + t] + jnp.sum(du * taps[t], axis=0, keepdims=True))
            return tuple(out)

        sums = lax.fori_loop(0, s // r, first, tuple(jnp.zeros((1, cb), F32) for _ in range(1 + D_CONV)))
        db_ref[...] = sums[0]
        for t in range(D_CONV):
            dw_ref[t:t + 1, :] = sums[1 + t]
        dw_ref[D_CONV:8, :] = jnp.zeros((8 - D_CONV, cb), F32)

        def second(i, carry):
            st = pl.multiple_of(i * r, r)
            win = dpad[pl.ds(st, r + 2 * HALO), :]
            acc = wv[0:1, :] * _shifted(win, HALO + 2, r)
            for t in range(1, D_CONV):
                acc = acc + wv[t:t + 1, :] * _shifted(win, HALO + 2 - t, r)
            dx_ref[pl.ds(st, r), :] = acc.astype(dx_ref.dtype)
            return carry

        lax.fori_loop(0, s // r, second, 0)

    col = pl.BlockSpec((s, cb), lambda j: (0, j))
    return pl.pallas_call(
        body, name=name, grid=(width // cb,),
        in_specs=[pl.BlockSpec((s, cb), lambda j: (0, XS0 // cb + c0 + j)), col, col,
                  pl.BlockSpec((8, cb), lambda j: (0, c0 + j)), pl.BlockSpec((1, cb), lambda j: (0, c0 + j))],
        out_specs=[col, pl.BlockSpec((8, cb), lambda j: (0, j)), pl.BlockSpec((1, cb), lambda j: (0, j))],
        out_shape=[jax.ShapeDtypeStruct((s, width), MMD), jax.ShapeDtypeStruct((8, width), F32),
                   jax.ShapeDtypeStruct((1, width), F32)],
        scratch_shapes=[pltpu.VMEM((s + 2 * HALO, cb), F32), pltpu.VMEM((s + 2 * HALO, cb), F32)],
        compiler_params=_cp(("arbitrary",), VMEM_BIG),
    )(proj, ga, gb, w8, brow)


def _tri(lower):
    i = jnp.arange(CHUNK)
    return ((i[:, None] >= i[None, :]) if lower else (i[:, None] <= i[None, :])).astype(F32)


def _dt_fwd(raw, bias, arow, *, name):
    s = raw.shape[0]

    def body(r_ref, b_ref, a_ref, lo_ref, up_ref, dt_ref, cs_ref):
        u = r_ref[...] + b_ref[...]
        dt = jnp.maximum(u, 0.0) + jnp.log1p(jnp.exp(-jnp.abs(u)))
        dt_ref[...] = dt
        a = dt * a_ref[...]
        lane = lax.broadcasted_iota(jnp.int32, (CHUNK, 128), 1)
        cs_ref[...] = jnp.where(lane < SSD_HEADS, _dot_hi(lo_ref[...], a), _dot_hi(up_ref[...], a))

    blk = pl.BlockSpec((CHUNK, 128), lambda i: (i, 0))
    row = pl.BlockSpec((1, 128), lambda i: (0, 0))
    tri = pl.BlockSpec((CHUNK, CHUNK), lambda i: (0, 0))
    return pl.pallas_call(
        body, name=name, grid=(s // CHUNK,), in_specs=[blk, row, row, tri, tri], out_specs=[blk, blk],
        out_shape=[jax.ShapeDtypeStruct((s, 128), F32)] * 2, compiler_params=_cp(("arbitrary",)),
    )(raw, bias, arow, _tri(True), _tri(False))


def _dt_bwd(ddt, raw, bias, *, name):
    s = raw.shape[0]
    tm = _tile(s, 1024)

    def body(d_ref, r_ref, b_ref, o_ref, db_ref):
        g = d_ref[...] * _sigmoid(r_ref[...] + b_ref[...])
        o_ref[...] = g.astype(o_ref.dtype)
        _acc_rows(db_ref, jnp.sum(g, axis=0, keepdims=True), pl.program_id(0) == 0)

    blk = pl.BlockSpec((tm, 128), lambda i: (i, 0))
    row = pl.BlockSpec((1, 128), lambda i: (0, 0))
    return pl.pallas_call(
        body, name=name, grid=(s // tm,), in_specs=[blk, blk, row], out_specs=[blk, row],
        out_shape=[jax.ShapeDtypeStruct((s, 128), MMD), jax.ShapeDtypeStruct((1, 128), F32)],
        compiler_params=_cp(("arbitrary",)),
    )(ddt, raw, bias)


GW = HPG * SSD_P


GPS = 4


def _ssd_specs(nc, rev):
    cc = (lambda c: nc - 1 - c) if rev else (lambda c: c)
    nb = SSD_GROUPS // GPS
    return dict(
        x=pl.BlockSpec((CHUNK, GPS * GW), lambda g, c: (cc(c), g)),
        b=pl.BlockSpec((CHUNK, GPS * SSD_N), lambda g, c: (cc(c), 2048 // (GPS * SSD_N) + g)),
        c=pl.BlockSpec((CHUNK, GPS * SSD_N), lambda g, c: (cc(c), 2048 // (GPS * SSD_N) + nb + g)),
        col=pl.BlockSpec((GPS, CHUNK, HPG), lambda g, c: (g, cc(c), 0)),
        lanes=pl.BlockSpec((CHUNK, 128), lambda g, c: (cc(c), 0)),
        rowt=pl.BlockSpec((GPS, 1, HPG, CHUNK), lambda g, c: (g, cc(c), 0, 0)),
        drow=pl.BlockSpec((1, GPS * GW), lambda g, c: (0, g)),
        y=pl.BlockSpec((CHUNK, GPS * GW), lambda g, c: (cc(c), g)),
        h=pl.BlockSpec((GPS, 1, SSD_N, GW), lambda g, c: (g, cc(c), 0, 0)),
        n=pl.BlockSpec((CHUNK, GPS * SSD_N), lambda g, c: (cc(c), g)),
    )


def _ssd_mask(anti):
    ii = lax.broadcasted_iota(jnp.int32, (CHUNK, CHUNK), 0)
    jj = lax.broadcasted_iota(jnp.int32, (CHUNK, CHUNK), 1)
    return ii, jj, (ii <= jj) if anti else (ii >= jj)


def _expand(x, ex):
    h1 = x.astype(jnp.bfloat16)
    r1 = x - h1.astype(F32)
    h2 = r1.astype(jnp.bfloat16)
    h3 = (r1 - h2.astype(F32)).astype(jnp.bfloat16)
    return _dot(h1, ex) + _dot(h2, ex) + _dot(h3, ex)


def _headsum(a, e):
    hi = a.astype(jnp.bfloat16)
    return _dot(hi, e) + _dot((a - hi.astype(F32)).astype(jnp.bfloat16), e)


def _expand_mats():
    lane = jnp.arange(128)[None, :, None]
    col = jnp.arange(GW)[None, None, :]
    base = (jnp.arange(2)[:, None] * SSD_HEADS + jnp.arange(SSD_GROUPS)[None, :] * HPG).reshape(2 * SSD_GROUPS, 1, 1)
    return (lane == base + col // SSD_P).astype(jnp.bfloat16)


def _headsum_mats():
    e1 = (jnp.arange(GW)[:, None] // SSD_P == jnp.arange(128)[None, :]).astype(jnp.bfloat16)
    e2 = (jnp.arange(HPG * CHUNK)[:, None] // CHUNK == jnp.arange(128)[None, :]).astype(jnp.bfloat16)
    return e1, e2


def _ssd_fwd(xc, dt, cs, cst, ex, drow, di, *, name):
    s = xc.shape[0]
    nc = s // CHUNK
    anti = di == 1
    sp = _ssd_specs(nc, anti)
    trow = 0 if anti else CHUNK - 1

    def body(x_ref, b_ref, c_ref, dt_ref, cs_ref, cst_ref, ex_ref, d_ref, y_ref, hp_ref, h_ref):
        @pl.when(pl.program_id(1) == 0)
        def _():
            h_ref[...] = jnp.zeros_like(h_ref)

        mask = _ssd_mask(anti)[2]
        dtv, csv = dt_ref[...], cs_ref[...]
        for gi in range(GPS):
            cols = slice(gi * GW, (gi + 1) * GW)
            ncols = slice(gi * SSD_N, (gi + 1) * SSD_N)
            ex = ex_ref[gi]
            xb = x_ref[:, cols].astype(F32)
            bm, cm = b_ref[:, ncols], c_ref[:, ncols]
            csr = cst_ref[gi, 0]
            dtf = _expand(dtv, ex)
            csf = _expand(csv, ex)
            tl = csf[trow:trow + 1, :]
            h = h_ref[gi]
            hp_ref[gi, 0] = h
            g = _dot(cm, bm, NT)
            xs = xb * dtf
            xsm = xs.astype(MMD)
            base = jnp.exp(csf) * _dot(cm, h.astype(MMD)) + d_ref[:, cols] * xb
            for r in range(HPG):
                sl = slice(r * SSD_P, (r + 1) * SSD_P)
                lm = jnp.exp(jnp.where(mask, csf[:, r * SSD_P:r * SSD_P + 1] - csr[r:r + 1, :], NEG))
                y_ref[:, gi * GW + r * SSD_P:gi * GW + (r + 1) * SSD_P] = _dot((g * lm).astype(MMD), xsm[:, sl]) + base[:, sl]
            xd = (xs * jnp.exp(tl - csf)).astype(MMD)
            h_ref[gi] = h * jnp.exp(tl) + _dot(bm, xd, TN)

    nb = SSD_GROUPS // GPS
    return pl.pallas_call(
        body, name=name, grid=(nb, nc),
        in_specs=[sp["x"], sp["b"], sp["c"], sp["lanes"], sp["lanes"], sp["rowt"],
                  pl.BlockSpec((GPS, 128, GW), lambda g, c: (di * nb + g, 0, 0)), sp["drow"]],
        out_specs=[sp["y"], sp["h"]],
        out_shape=[jax.ShapeDtypeStruct((s, 2048), F32), jax.ShapeDtypeStruct((SSD_GROUPS, nc, SSD_N, GW), F32)],
        scratch_shapes=[pltpu.VMEM((GPS, SSD_N, GW), F32)],
        compiler_params=_cp(("arbitrary", "arbitrary")),
    )(xc, xc, xc, dt, cs, cst, ex, drow)


def _ssd_bwd(xc, dt, cs, dt4, cst, ex, drow, arow4, dy, hprev, di, *, name):
    s = xc.shape[0]
    nc = s // CHUNK
    anti = di == 1
    sp = _ssd_specs(nc, not anti)
    trow = 0 if anti else CHUNK - 1
    e1, e2 = _headsum_mats()

    def body(x_ref, b_ref, c_ref, dt_ref, cs_ref, dt4_ref, cst_ref, ex_ref, d_ref, a_ref, dy_ref, hp_ref, tri_ref,
             e1_ref, e2_ref, dx_ref, db_ref, dc_ref, ddt_ref, da_ref, dh_ref, w_ref, dxs_ref):
        @pl.when(pl.program_id(1) == 0)
        def _():
            dh_ref[...] = jnp.zeros_like(dh_ref)
            da_ref[...] = jnp.zeros_like(da_ref)

        e1v = e1_ref[...]
        ii, _, mask = _ssd_mask(anti)
        dtv, csv = dt_ref[...], cs_ref[...]
        for gi in range(GPS):
            cols = slice(gi * GW, (gi + 1) * GW)
            ncols = slice(gi * SSD_N, (gi + 1) * SSD_N)
            ex = ex_ref[gi]
            xb = x_ref[:, cols].astype(F32)
            bm, cm = b_ref[:, ncols], c_ref[:, ncols]
            csr = cst_ref[gi, 0]
            dyb = dy_ref[:, cols]
            dym = dyb.astype(MMD)
            hp = hp_ref[gi, 0]
            hpm = hp.astype(MMD)
            dh = dh_ref[gi]
            dhm = dh.astype(MMD)
            dtf = _expand(dtv, ex)
            csf = _expand(csv, ex)
            tl = csf[trow:trow + 1, :]
            e = jnp.exp(csf)
            dec = jnp.exp(tl - csf)
            et = jnp.exp(tl)
            xs = xb * dtf
            xsm = xs.astype(MMD)
            g = _dot(cm, bm, NT)
            z = _dot(cm, hpm)
            bdh = _dot(bm, dhm)
            dg = jnp.zeros((CHUNK, CHUNK), F32)
            wcols = jnp.zeros((CHUNK, CHUNK), F32)
            for r in range(HPG):
                sl = slice(r * SSD_P, (r + 1) * SSD_P)
                lm = jnp.exp(jnp.where(mask, csf[:, r * SSD_P:r * SSD_P + 1] - csr[r:r + 1, :], NEG))
                mm = g * lm
                dm = _dot(dym[:, sl], xsm[:, sl], NT)
                w = dm * mm
                w_ref[gi, :, r * CHUNK:(r + 1) * CHUNK] = w
                wcols = jnp.where(ii == r, jnp.sum(w, axis=0, keepdims=True), wcols)
                dg = dg + dm * lm
                dxs_ref[gi, :, sl] = _dot(mm.astype(MMD), dym[:, sl], TN)
            dxs = dxs_ref[gi] + dec * bdh
            dx_ref[:, cols] = dxs * dtf + d_ref[:, cols] * dyb
            tb = xs * bdh * dec
            d_tot = jnp.sum(tb, axis=0, keepdims=True) + et * jnp.sum(dh * hp, axis=0, keepdims=True)
            d_tot = _headsum(jnp.broadcast_to(d_tot, (8, GW)), e1v)[0:1]
            dcs = (_headsum(dyb * (e * z) - tb, e1v) + _headsum(w_ref[gi], e2_ref[...]) - wcols.T
                   + jnp.where(ii == trow, d_tot, 0.0))
            da = _dot_hi(tri_ref[...], dcs)
            ddt_ref[gi] = (da * a_ref[gi] + _headsum(dxs * xb, e1v))[:, 0:HPG]
            da_ref[gi] += jnp.sum(da[:, 0:HPG] * dt4_ref[gi], axis=0, keepdims=True)
            dgm = dg.astype(MMD)
            dz = (e * dyb).astype(MMD)
            dc_ref[:, ncols] = _dot(dgm, bm) + _dot(dz, hpm, NT)
            db_ref[:, ncols] = _dot(dgm, cm, TN) + _dot((xs * dec).astype(MMD), dhm, NT)
            dh_ref[gi] = dh * et + _dot(cm, dz, TN)

    nb = SSD_GROUPS // GPS
    const = lambda shape: pl.BlockSpec(shape, lambda g, c: (0,) * len(shape))
    return pl.pallas_call(
        body, name=name, grid=(nb, nc),
        in_specs=[sp["x"], sp["b"], sp["c"], sp["lanes"], sp["lanes"], sp["col"], sp["rowt"],
                  pl.BlockSpec((GPS, 128, GW), lambda g, c: (di * nb + g, 0, 0)), sp["drow"],
                  pl.BlockSpec((GPS, 1, 128), lambda g, c: (g, 0, 0)), sp["y"], sp["h"],
                  const((CHUNK, CHUNK)), const((GW, 128)), const((HPG * CHUNK, 128))],
        out_specs=[sp["y"], sp["n"], sp["n"], sp["col"], pl.BlockSpec((GPS, 1, HPG), lambda g, c: (g, 0, 0))],
        out_shape=[jax.ShapeDtypeStruct((s, 2048), F32), jax.ShapeDtypeStruct((s, SSD_GROUPS * SSD_N), F32),
                   jax.ShapeDtypeStruct((s, SSD_GROUPS * SSD_N), F32), jax.ShapeDtypeStruct((SSD_GROUPS, s, HPG), F32),
                   jax.ShapeDtypeStruct((SSD_GROUPS, 1, HPG), F32)],
        scratch_shapes=[pltpu.VMEM((GPS, SSD_N, GW), F32), pltpu.VMEM((GPS, CHUNK, HPG * CHUNK), F32),
                        pltpu.VMEM((GPS, CHUNK, GW), F32)],
        compiler_params=_cp(("arbitrary", "arbitrary")),
    )(xc, xc, xc, dt, cs, dt4, cst, ex, drow, arow4, dy, hprev, _tri(anti), e1, e2)


def _gnorm_fwd(ya, yb, proj, w, *, name):
    s = ya.shape[0]
    tm = _tile(s, 256)

    def body(a_ref, b_ref, z_ref, w_ref, o_ref):
        zv = z_ref[...].astype(F32)
        t = (a_ref[...] + b_ref[...]) * (zv * _sigmoid(zv))
        r = lax.rsqrt(jnp.mean(t * t, axis=-1, keepdims=True) + EPS)
        o_ref[...] = ((t * r) * w_ref[...]).astype(o_ref.dtype)

    big = pl.BlockSpec((tm, 2048), lambda i: (i, 0))
    row = pl.BlockSpec((1, 2048), lambda i: (0, 0))
    return pl.pallas_call(
        body, name=name, grid=(s // tm,), in_specs=[big, big, big, row], out_specs=big,
        out_shape=jax.ShapeDtypeStruct((s, 2048), MMD), compiler_params=_cp(("arbitrary",)),
    )(ya, yb, proj, w)


def _gnorm_bwd(dout, ya, yb, proj, w, *, name):
    s = ya.shape[0]
    tm = _tile(s, 256)

    def body(do_ref, a_ref, b_ref, z_ref, w_ref, dy_ref, dz_ref, dw_ref):
        zv = z_ref[...].astype(F32)
        sg = _sigmoid(zv)
        sz = zv * sg
        y = a_ref[...] + b_ref[...]
        t = y * sz
        r = lax.rsqrt(jnp.mean(t * t, axis=-1, keepdims=True) + EPS)
        nv = t * r
        dov = do_ref[...].astype(F32)
        _acc_rows(dw_ref, jnp.sum(dov * nv, axis=0, keepdims=True), pl.program_id(0) == 0)
        dn = dov * w_ref[...]
        dt_ = r * (dn - nv * jnp.mean(dn * nv, axis=-1, keepdims=True))
        dy_ref[...] = dt_ * sz
        dz_ref[...] = (dt_ * y * (sg * (1.0 + zv * (1.0 - sg)))).astype(dz_ref.dtype)

    big = pl.BlockSpec((tm, 2048), lambda i: (i, 0))
    row = pl.BlockSpec((1, 2048), lambda i: (0, 0))
    return pl.pallas_call(
        body, name=name, grid=(s // tm,), in_specs=[big, big, big, big, row], out_specs=[big, big, row],
        out_shape=[jax.ShapeDtypeStruct((s, 2048), F32), jax.ShapeDtypeStruct((s, 2048), MMD),
                   jax.ShapeDtypeStruct((1, 2048), F32)],
        compiler_params=_cp(("arbitrary",)),
    )(dout, ya, yb, proj, w)


def _colsum_prod(a, b, *, name):
    s, n = a.shape
    tm = _tile(s, 256)

    def body(a_ref, b_ref, o_ref):
        _acc_rows(o_ref, jnp.sum(a_ref[...].astype(F32) * b_ref[...].astype(F32), axis=0, keepdims=True),
                  pl.program_id(0) == 0)

    big = pl.BlockSpec((tm, n), lambda i: (i, 0))
    return pl.pallas_call(
        body, name=name, grid=(s // tm,), in_specs=[big, big], out_specs=pl.BlockSpec((1, n), lambda i: (0, 0)),
        out_shape=jax.ShapeDtypeStruct((1, n), F32), compiler_params=_cp(("arbitrary",)),
    )(a, b)


def _heads(a, n):
    return a.reshape(a.shape[0], n, HEAD_DIM).transpose(1, 0, 2)


def _unheads(a):
    return a.transpose(1, 0, 2).reshape(a.shape[1], a.shape[0] * HEAD_DIM)


def _per_group(a):
    return a.reshape(a.shape[0], SSD_GROUPS, HPG).transpose(1, 0, 2)


def _per_group_t(a):
    s = a.shape[0]
    return a.reshape(s // CHUNK, CHUNK, SSD_GROUPS, HPG).transpose(2, 0, 3, 1)


def _local_step(x, target, mod, wts, small, late_weights=None, late_grads=None, in_grad=None):
    s, d = x.shape
    shift1, scale1, gate1, shift2, scale2, gate2 = [mod[i:i + 1] for i in range(6)]

    h1 = _ln_mod(x, small["norm1_w"], scale1, shift1, name="ln1")
    proj = _mm(h1, wts["w_in_p"], name="in_proj", outs=[MMD], tm=512, tn=2944, b_outer=True)
    dt_raw = _mm(h1, wts["w_dt"], name="dt_proj", outs=[F32], tm=512, tn=128)

    qk_w = jnp.concatenate([jnp.tile(small["q_norm_w"], (1, N_Q_HEADS)), jnp.tile(small["k_norm_w"], (1, N_KV_HEADS))], axis=1)
    qk_sc = jnp.concatenate([jnp.full((1, N_Q_HEADS * HEAD_DIM), HEAD_DIM ** -0.5, F32),
                             jnp.ones((1, N_KV_HEADS * HEAD_DIM), F32)], axis=1)
    qk_sc2 = jnp.concatenate([jnp.full((1, N_Q_HEADS * HEAD_DIM), HEAD_DIM ** -0.5 * LOG2E, F32),
                              jnp.ones((1, N_KV_HEADS * HEAD_DIM), F32)], axis=1)
    tabs = _rope_tables(s)
    qk, qkt = _qk_fwd(proj, qk_w, qk_sc2, tabs, name="qk_fwd")
    qkt = qkt.reshape(N_Q_HEADS + N_KV_HEADS, HEAD_DIM, s)
    k_h = _heads(qk[:, N_Q_HEADS * HEAD_DIM:], N_KV_HEADS)
    v_sd = proj[:, V0:V0 + N_KV_HEADS * HEAD_DIM]
    v_h = _heads(v_sd, N_KV_HEADS)
    vta = jnp.concatenate([v_sd.T.reshape(N_KV_HEADS, HEAD_DIM, s), jnp.ones((N_KV_HEADS, V_AUG - HEAD_DIM, s), MMD)], axis=1)
    ot, lse = _flash_fwd(qkt, vta, name="flash_fwd")
    ot2 = ot.reshape(N_Q_HEADS * HEAD_DIM, s)
    if late_weights is not None:
        wts = {**wts, **late_weights(ot)}

    w8 = jnp.pad(small["conv_w"], ((0, 8 - D_CONV), (0, 0)))
    xc = _conv_fwd(proj, w8, small["conv_b"], name="conv_fwd")
    a_neg = -jnp.exp(small["A_log"])
    arow = jnp.pad(a_neg.reshape(1, 2 * SSD_HEADS), ((0, 0), (0, 128 - 2 * SSD_HEADS)))
    bias_row = jnp.pad(small["dt_bias"].reshape(1, 2 * SSD_HEADS), ((0, 0), (0, 128 - 2 * SSD_HEADS)))
    dt, cs = _dt_fwd(dt_raw, bias_row, arow, name="dt_fwd")
    drow = jnp.repeat(small["ssd_D"], SSD_P, axis=1)
    dirs = []
    for di in range(2):
        cols = slice(di * SSD_HEADS, (di + 1) * SSD_HEADS)
        dirs.append(dict(
            dt4=_per_group(dt[:, cols]), cst=_per_group_t(cs[:, cols]),
            drow=drow if di == 0 else jnp.zeros_like(drow),
            arow4=jnp.pad(a_neg[di].reshape(SSD_GROUPS, 1, HPG), ((0, 0), (0, 0), (0, 128 - HPG)))))
    ex = _expand_mats()
    ys = []
    for di, dd in enumerate(dirs):
        y, dd["hprev"] = _ssd_fwd(xc, dt, cs, dd["cst"], ex, dd["drow"], di, name=f"ssd_fwd{di}")
        ys.append(y)
    ssdn = _gnorm_fwd(ys[0], ys[1], proj, small["ssd_norm_w"], name="gnorm_fwd")

    a_o = _mm(ot2, wts["w_attn_out"], name="attn_out", outs=[MMD], ta=True, tm=512, tn=1024)

    def merge_epi(acc, ao, ga, gs):
        return (_sigmoid(ga.astype(F32)) * ao.astype(F32) + _sigmoid(gs.astype(F32)) * acc, acc)

    merged, b_o = _mm(ssdn, wts["w_ssd_out"], name="ssd_out", outs=[MMD, MMD], tm=512, tn=512,
                      extras=[(a_o, "tile", 0), (proj, "tile", GA0), (proj, "tile", GS0)], epi=merge_epi)

    def res_epi(acc, res, gate):
        return (res + gate * acc, acc)

    x1, mo = _mm(merged, wts["w_o"], name="w_o", outs=[F32, MMD], tm=512, tn=512,
                 extras=[(x, "tile", 0), (gate1, "row", 0)], epi=res_epi)
    h2 = _ln_mod(x1, small["norm2_w"], scale2, shift2, name="ln2")

    def relu2_epi(acc):
        rl = jnp.maximum(acc, 0.0)
        return (rl * rl, rl)

    act, rl = _mm(h2, wts["w_mlp1"], name="mlp1", outs=[MMD, MMD], tm=512, tn=1024, epi=relu2_epi, b_outer=True)

    def loss_epi(acc, res, gate, tgt):
        return ((res + gate * acc - tgt) * (1.0 / d), acc)

    dy, ffo = _mm(act, wts["w_mlp2"], name="mlp2", outs=[F32, MMD], tm=512, tn=1024, vmem=VMEM_BIG,
                  extras=[(x1, "tile", 0), (gate2, "row", 0), (target, "tile", 0)], epi=loss_epi)
    loss = _sumsq(dy, name="loss") * (0.5 * d)

    gw = {}
    gs_ = {}
    dffo, dgate2 = _gate_bwd(dy, ffo, gate2, name="gate2_bwd")
    dpre = _mm(dffo, wts["w_mlp2"], name="mlp2_dx", outs=[MMD], nt=True, tm=512, tn=1024,
               extras=[(rl, "tile", 0)], epi=lambda acc, r: (acc * (2.0 * r.astype(F32)),))
    gw["w_mlp2"] = _mm_tn(act, dffo, name="mlp2_dw")
    dh2 = _mm(dpre, wts["w_mlp1"], name="mlp1_dx", outs=[F32], nt=True, tm=512, tn=1024)
    gw["w_mlp1"] = _mm_tn(h2, dpre, name="mlp1_dw")
    dx1, dshift2, dscale2, gs_["norm2_w"] = _ln_mod_bwd(dh2, x1, small["norm2_w"], scale2, dy, name="ln2_bwd")
    dmo, dgate1 = _gate_bwd(dx1, mo, gate1, name="gate1_bwd")

    def merge_bwd_epi(acc, ao, bo, ga, gs):
        sa, ss = _sigmoid(ga.astype(F32)), _sigmoid(gs.astype(F32))
        return (acc * sa, acc * ss, acc * ao.astype(F32) * sa * (1.0 - sa), acc * bo.astype(F32) * ss * (1.0 - ss))

    da_o, db_o, dga, dgs = _mm(dmo, wts["w_o"], name="w_o_dx", outs=[MMD] * 4, nt=True, tm=512, tn=512,
                               extras=[(a_o, "tile", 0), (b_o, "tile", 0), (proj, "tile", GA0), (proj, "tile", GS0)],
                               epi=merge_bwd_epi)
    gw["w_o"] = _mm_tn(merged, dmo, name="w_o_dw")
    dot = _mm(wts["w_attn_out"], da_o, name="attn_out_dx", outs=[MMD], nt=True, tm=512, tn=512)
    gw["w_attn_out"] = _mm(ot2, da_o, name="attn_out_dw", outs=[F32], tm=256, tn=512, vmem=VMEM_BIG)
    dssdn = _mm(db_o, wts["w_ssd_out"], name="ssd_out_dx", outs=[MMD], nt=True, tm=512, tn=512)
    gw["w_ssd_out"] = _mm_tn(ssdn, db_o, name="ssd_out_dw")

    norm_w = small["ssd_norm_w"] if late_grads is None else small["ssd_norm_w"] + late_grads(gw)
    dyssd, dz, gs_["ssd_norm_w"] = _gnorm_bwd(dssdn, ys[0], ys[1], proj, norm_w, name="gnorm_bwd")
    gs_["ssd_D"] = _colsum_prod(dyssd, xc[:, 0:2048], name="ssd_d_grad").reshape(SSD_HEADS, SSD_P).sum(axis=1).reshape(1, SSD_HEADS)
    dxc, ddts, das = [], [], []
    for di, dd in enumerate(dirs):
        dxs, dbm, dcm, ddt4, da4 = _ssd_bwd(xc, dt, cs, dd["dt4"], dd["cst"], ex, dd["drow"], dd["arow4"],
                                            dyssd, dd["hprev"], di, name=f"ssd_bwd{di}")
        dxc.append((dxs, dbm, dcm))
        ddts.append(ddt4.transpose(1, 0, 2).reshape(s, SSD_HEADS))
        das.append(da4.reshape(1, SSD_HEADS))
    conv_parts, col0 = [], 0
    for part, (ga, gb) in enumerate(zip(*dxc)):
        conv_parts.append(_conv_bwd(proj, col0, ga, gb, w8, small["conv_b"], name=f"conv_bwd{part}"))
        col0 += ga.shape[1]
    dxbc, dw8, gs_["conv_b"] = [jnp.concatenate(t, axis=1) for t in zip(*conv_parts)]
    gs_["conv_w"] = dw8[0:D_CONV]
    gs_["A_log"] = jnp.concatenate(das, axis=0) * a_neg
    ddt = jnp.pad(jnp.concatenate(ddts, axis=1), ((0, 0), (0, 128 - 2 * SSD_HEADS)))
    ddt_raw, dbias = _dt_bwd(ddt, dt_raw, bias_row, name="dt_bwd")
    gs_["dt_bias"] = dbias[:, 0:2 * SSD_HEADS].reshape(2, SSD_HEADS)

    dqt, dk_h, dv_h = _flash_bwd(qkt, k_h, v_h, dot.reshape(N_Q_HEADS, HEAD_DIM, s), ot, lse, name="flash_bwd")
    dqkt = jnp.concatenate([dqt.reshape(N_Q_HEADS * HEAD_DIM, s),
                            dk_h.transpose(0, 2, 1).reshape(N_KV_HEADS * HEAD_DIM, s)], axis=0)
    dqk_u, dqk_w = _qk_bwd(dqkt, proj, qk_w, qk_sc, tabs, name="qk_bwd")
    gs_["q_norm_w"] = dqk_w[:, 0:N_Q_HEADS * HEAD_DIM].reshape(N_Q_HEADS, HEAD_DIM).sum(axis=0, keepdims=True)
    gs_["k_norm_w"] = dqk_w[:, N_Q_HEADS * HEAD_DIM:].reshape(N_KV_HEADS, HEAD_DIM).sum(axis=0, keepdims=True)
    dv = _unheads(dv_h).astype(MMD)

    dproj = jnp.concatenate([dz, dga, dgs, dxbc, dqk_u, dv, ddt_raw], axis=1)
    gw["w_in_p"] = _mm_tn(h1, dproj, name="in_proj_dw", tk=512, tn=2944, tmm=2048, vmem=VMEM_BIG)
    if in_grad is not None:
        dproj, _ = lax.optimization_barrier((dproj, in_grad(gw["w_in_p"])))
    dh1 = _mm(dproj, wts["w_in_p"], name="in_proj_dx", outs=[F32], nt=True, tm=256, tn=1024, vmem=VMEM_BIG)
    grad_x, dshift1, dscale1, gs_["norm1_w"] = _ln_mod_bwd(dh1, x, small["norm1_w"], scale1, dx1, name="ln1_bwd")
    dmod = jnp.concatenate([dshift1, dscale1, dgate1, dshift2, dscale2, dgate2], axis=0)
    return loss, grad_x, dmod, gw, gs_


N_DEV = 8
N_CHIP = 4
ANY = pl.BlockSpec(memory_space=pl.ANY)


def _place():
    return lax.axis_index("x"), lax.axis_index("y"), lax.axis_index("c")


def _allgather8(v, *, name):
    m_per, n = v.shape

    def body(x_ref, out_ref, send_sems, recv_sems, local_sem):
        x, y, c = _place()
        me, sibling = (x, y, c), (x, y, 1 - c)
        chips = [(1 - x, y), (x, 1 - y), (1 - x, 1 - y)]

        def rows(px, py, pc):
            return out_ref.at[pl.ds((4 * px + 2 * py + pc) * m_per, m_per), :]

        def copy(k, block, to, src=None):
            return pltpu.make_async_remote_copy(
                src_ref=rows(*block) if src is None else src, dst_ref=rows(*block),
                send_sem=send_sems.at[k], recv_sem=recv_sems.at[k], device_id=to, device_id_type=MESH)

        mine = pltpu.make_async_copy(x_ref, rows(*me), local_sem)
        mine.start()
        first = [copy(0, me, sibling, src=x_ref)]
        first += [copy(1 + j, me, (*chip, c), src=x_ref) for j, chip in enumerate(chips)]
        for cp in first:
            cp.start()
        passed = [copy(4 + j, (*chip, c), sibling) for j, chip in enumerate(chips)]
        for j, chip in enumerate(chips):
            copy(1 + j, (*chip, c), me).wait_recv()
            passed[j].start()
        copy(0, sibling, me).wait_recv()
        for j, chip in enumerate(chips):
            copy(4 + j, (*chip, 1 - c), me).wait_recv()
        for cp in first + passed:
            cp.wait_send()
        mine.wait()

    return pl.pallas_call(
        body, name=name, out_shape=jax.ShapeDtypeStruct((N_DEV * m_per, n), v.dtype),
        in_specs=[pl.BlockSpec(memory_space=pltpu.VMEM)], out_specs=pl.BlockSpec(memory_space=pltpu.VMEM),
        scratch_shapes=[pltpu.SemaphoreType.DMA((7,)), pltpu.SemaphoreType.DMA((7,)), pltpu.SemaphoreType.DMA],
    )(v)


def _scatter_chips(src, *, name):
    def body(x_ref, out_ref, send_sems, recv_sems):
        x, y, c = _place()
        k = 2 * x + y
        chips = [(1 - x, y), (x, 1 - y), (1 - x, 1 - y)]
        ids = [2 * cx + cy for cx, cy in chips]

        def copy(j, slot):
            return pltpu.make_async_remote_copy(
                src_ref=x_ref.at[ids[j]], dst_ref=out_ref.at[slot], send_sem=send_sems.at[j], recv_sem=recv_sems.at[j],
                device_id=(*chips[j], c), device_id_type=MESH)

        sends = [copy(j, k) for j in range(3)]
        for cp in sends:
            cp.start()
        for j in range(3):
            copy(j, ids[j]).wait_recv()
        for cp in sends:
            cp.wait_send()

    return pl.pallas_call(
        body, name=name, out_shape=jax.ShapeDtypeStruct(src.shape, src.dtype), in_specs=[ANY], out_specs=ANY,
        scratch_shapes=[pltpu.SemaphoreType.DMA((3,)), pltpu.SemaphoreType.DMA((3,))],
    )(src)


HBM = pl.BlockSpec(memory_space=pltpu.HBM)
SEM = pl.BlockSpec(memory_space=pltpu.SEMAPHORE)


def _chips_copies(x_ref, land_ref, sems, scatter):
    x, y, c = _place()
    k = 2 * x + y
    chips = [(1 - x, y), (x, 1 - y), (1 - x, 1 - y)]
    ids = [2 * cx + cy for cx, cy in chips]

    def copy(j, slot):
        return pltpu.make_async_remote_copy(
            src_ref=x_ref.at[ids[j]] if scatter else x_ref, dst_ref=land_ref.at[slot], send_sem=sems[j],
            recv_sem=sems[3 + j], device_id=(*chips[j], c), device_id_type=MESH)

    return [copy(j, k) for j in range(3)], [copy(j, ids[j]) for j in range(3)]


def _chips_start(src, scatter, *, name):
    shape = src.shape if scatter else (N_CHIP,) + tuple(src.shape)

    def body(x_ref, land_ref, *rest):
        sems, token = rest[0:6], rest[8]
        for cp in _chips_copies(x_ref, land_ref, sems, scatter)[0]:
            cp.start()
        token[...] = jnp.zeros_like(token)

    out = pl.pallas_call(
        body, name=name,
        out_shape=(pltpu.SemaphoreType.DMA(()),) * 6 + (pltpu.HBM(src.shape, src.dtype), pltpu.HBM(shape, src.dtype),
                                                       jax.ShapeDtypeStruct((8, 128), F32)),
        in_specs=(HBM, HBM), out_specs=(SEM,) * 6 + (HBM, HBM, pl.BlockSpec(memory_space=pltpu.VMEM)),
        input_output_aliases={0: 6, 1: 7},
        compiler_params=pltpu.CompilerParams(has_side_effects=pltpu.SideEffectType.DATAFLOW_SIDE_EFFECTING),
    )(pltpu.with_memory_space_constraint(src, pltpu.HBM),
      pltpu.with_memory_space_constraint(lax.empty(shape, src.dtype), pltpu.HBM))
    return out[0:6], out[6], out[7], out[8]


def _chips_wait(sems, src, land, after, scatter, *, name):
    def body(x_ref, land_ref, *rest):
        sems_ = rest[0:6]
        for cp in _chips_copies(x_ref, land_ref, sems_, scatter)[1]:
            cp.wait_send()
            cp.wait_recv()

    return pl.pallas_call(
        body, name=name, out_shape=(pltpu.HBM(src.shape, src.dtype), pltpu.HBM(land.shape, land.dtype)),
        in_specs=(HBM, HBM) + (SEM,) * 6 + (ANY,), out_specs=(HBM, HBM), input_output_aliases={0: 0, 1: 1},
        compiler_params=pltpu.CompilerParams(has_side_effects=pltpu.SideEffectType.DATAFLOW_SIDE_EFFECTING),
    )(src, land, *sems, after)


def _row_tile(r, pref=512):
    return max(t for t in range(16, pref + 1, 16) if r % t == 0)


def _gather_weights(src, *, name):
    r = src.shape[0]
    hr = r // 2
    assert r == 2 * hr and hr % 16 == 0

    def body(x_ref, out_ref, send_sems, recv_sems):
        x, y, c = _place()
        k = 2 * x + y
        chips = [(1 - x, y), (x, 1 - y), (1 - x, 1 - y)]
        ids = [2 * cx + cy for cx, cy in chips]
        mine_rows = pl.ds(pl.multiple_of(c * hr, 16), hr)
        other_rows = pl.ds(pl.multiple_of((1 - c) * hr, 16), hr)

        def copy(sem, src_ref, slot, rows, to):
            return pltpu.make_async_remote_copy(
                src_ref=src_ref, dst_ref=out_ref.at[slot, rows], send_sem=send_sems.at[sem], recv_sem=recv_sems.at[sem],
                device_id=to, device_id_type=MESH)

        sends = [copy(j, x_ref.at[mine_rows], k, mine_rows, (cx, cy, c)) for j, (cx, cy) in enumerate(chips)]
        for cp in sends:
            cp.start()
        passed = [copy(3 + j, out_ref.at[ids[j], mine_rows], ids[j], mine_rows, (x, y, 1 - c)) for j in range(3)]
        for j, (cx, cy) in enumerate(chips):
            copy(j, x_ref.at[mine_rows], ids[j], mine_rows, (cx, cy, c)).wait_recv()
            passed[j].start()
        for j in range(3):
            copy(3 + j, out_ref.at[ids[j], other_rows], ids[j], other_rows, (x, y, 1 - c)).wait_recv()
        for cp in sends + passed:
            cp.wait_send()

    return pl.pallas_call(
        body, name=name, out_shape=jax.ShapeDtypeStruct((N_CHIP,) + tuple(src.shape), src.dtype),
        in_specs=[ANY], out_specs=ANY,
        scratch_shapes=[pltpu.SemaphoreType.DMA((6,)), pltpu.SemaphoreType.DMA((6,))],
    )(src)


def _pair_swap(a, *, name):
    n, r, cols = a.shape
    hr = r // 2

    def body(x_ref, out_ref, send_sem, recv_sem):
        x, y, c = _place()
        other_rows = pl.ds(pl.multiple_of((1 - c) * hr, 16), hr)
        cp = pltpu.make_async_remote_copy(src_ref=x_ref.at[:, other_rows], dst_ref=out_ref, send_sem=send_sem,
                                          recv_sem=recv_sem, device_id=(x, y, 1 - c), device_id_type=MESH)
        cp.start()
        cp.wait()

    return pl.pallas_call(
        body, name=name, out_shape=jax.ShapeDtypeStruct((n, hr, cols), a.dtype), in_specs=[ANY], out_specs=ANY,
        scratch_shapes=[pltpu.SemaphoreType.DMA, pltpu.SemaphoreType.DMA],
    )(a)


def _sibling_copy(a, *, name):
    def body(x_ref, out_ref, send_sem, recv_sem):
        x, y, c = _place()
        cp = pltpu.make_async_remote_copy(src_ref=x_ref, dst_ref=out_ref, send_sem=send_sem, recv_sem=recv_sem,
                                          device_id=(x, y, 1 - c), device_id_type=MESH)
        cp.start()
        cp.wait()

    return pl.pallas_call(
        body, name=name, out_shape=jax.ShapeDtypeStruct(a.shape, a.dtype), in_specs=[ANY], out_specs=ANY,
        scratch_shapes=[pltpu.SemaphoreType.DMA, pltpu.SemaphoreType.DMA],
    )(a)


def _sum_slots(a, *, name):
    _, r, c = a.shape
    tr = _row_tile(r, 256)

    def body(a_ref, o_ref):
        acc = a_ref[0].astype(F32)
        for j in range(1, N_CHIP):
            acc = acc + a_ref[j].astype(F32)
        o_ref[...] = acc

    return pl.pallas_call(
        body, name=name, grid=(r // tr,), in_specs=[pl.BlockSpec((N_CHIP, tr, c), lambda i: (0, i, 0))],
        out_specs=pl.BlockSpec((tr, c), lambda i: (i, 0)), out_shape=jax.ShapeDtypeStruct((r, c), F32),
        compiler_params=_cp(("arbitrary",)),
    )(a)


def _add2(a, b, *, name):
    r, c = a.shape
    tr = _row_tile(r)

    def body(a_ref, b_ref, o_ref):
        o_ref[...] = (a_ref[...].astype(F32) + b_ref[...].astype(F32)).astype(o_ref.dtype)

    spec = pl.BlockSpec((tr, c), lambda i: (i, 0))
    return pl.pallas_call(
        body, name=name, grid=(r // tr,), in_specs=[spec, spec], out_specs=spec,
        out_shape=jax.ShapeDtypeStruct((r, c), a.dtype), compiler_params=_cp(("arbitrary",)),
    )(a, b)


BIG = ("w_in", "w_mlp1", "w_attn_out", "w_ssd_out", "w_o", "w_mlp2")
COL_SHARDED = ("w_mlp1", "w_in")
ROW_SHARDED = ("w_attn_out", "w_ssd_out", "w_o", "w_mlp2")
LATE = ROW_SHARDED + ("w_mlp1",)
SMALL = ("b_ada", "norm1_w", "norm2_w", "q_norm_w", "k_norm_w", "conv_b", "A_log", "dt_bias", "ssd_D", "ssd_norm_w")
NAMES = ("w_ada", "b_ada", "norm1_w", "norm2_w", "w_in", "q_norm_w", "k_norm_w", "conv_w", "conv_b", "A_log", "dt_bias",
         "ssd_D", "ssd_norm_w", "w_attn_out", "w_ssd_out", "w_o", "w_mlp1", "w_mlp2")
W_IN_COLS = 8768


def _permute_in(w):
    return jnp.concatenate([w[:, 4608:6656], w[:, 6720:8768], w[:, 1536:4608], w[:, 0:1536], w[:, 6656:6720],
                            jnp.zeros((w.shape[0], PW - W_IN_COLS), w.dtype)], axis=1)


def _unpermute_in(wp):
    return jnp.concatenate([wp[:, Q0:DT0], wp[:, XS0:Q0], wp[:, Z0:GA0], wp[:, DT0:DT0 + 64], wp[:, GA0:XS0]], axis=1)


def _pad_to(v, n):
    return jnp.pad(v, (0, n - v.shape[0]))


def _step(w, m, v, loss_target):
    xi, yi, ci = _place()
    chip = 2 * xi + yi
    dev = 4 * xi + 2 * yi + ci
    x, tgt = w["x"], loss_target
    d = x.shape[1]

    cw = w["conv_w"].shape[1]
    v0 = _pad_to(jnp.concatenate([w["c"].reshape(-1), w["conv_w"].reshape(-1)]), 5120).reshape(8, 640)
    g0 = _allgather8(v0, name="ag_cond").reshape(N_DEV, 5120)
    c_all = g0[:, 0:d]
    conv_w = jnp.concatenate([g0[2 * k, d:d + D_CONV * cw].reshape(D_CONV, cw) for k in range(N_CHIP)], axis=1)
    sc = _silu_cast(c_all, name="silu_c")
    modp = _mm(sc, w["w_ada"].astype(MMD), name="ada_fwd", outs=[F32], tm=8, tn=512)
    g1 = _allgather8(modp, name="ag_mod").reshape(N_DEV, N_DEV, modp.shape[1])
    mod_all = jnp.concatenate([g1[2 * k] for k in range(N_CHIP)], axis=1)
    mod = (lax.dynamic_slice_in_dim(mod_all, dev, 1, axis=0) + w["b_ada"]).reshape(6, d)

    mine, mod = lax.optimization_barrier((w["w_in"].astype(MMD), mod))
    gath = lax.dynamic_update_slice_in_dim(_gather_weights(mine, name="ag_w_in"), mine[None], chip, axis=0)
    late_mine = jnp.concatenate([w[n].astype(MMD) for n in LATE], axis=0)
    late_mine, gath = lax.optimization_barrier((late_mine, gath))
    ag_sems, ag_src, ag_land, ag_token = _chips_start(late_mine, False, name="ag_late_start")
    mod = mod + ag_token[0:1, 0:1]
    w_in = jnp.concatenate([gath[k] for k in range(N_CHIP)], axis=1)
    wts = {"w_in_p": _permute_in(w_in), "w_dt": jnp.pad(w_in[:, 6656:6720], ((0, 0), (0, 64)))}
    small = {n: w[n] for n in SMALL if n != "b_ada"}
    small["conv_w"] = conv_w

    def own_slot(land, src):
        return lax.dynamic_update_slice_in_dim(land, src, chip, axis=0)

    def late_weights(after):
        src, land = _chips_wait(ag_sems, ag_src, ag_land, after, False, name="ag_late_wait")
        land = own_slot(land, src[None])
        out, o = {}, 0
        for n in LATE:
            rows = w[n].shape[0]
            part = land[:, o:o + rows]
            out[n] = (jnp.concatenate([part[k] for k in range(N_CHIP)], axis=1) if n in COL_SHARDED
                      else part.reshape(N_CHIP * rows, w[n].shape[1]))
            o += rows
        return out

    def pair_sums(slots, tag):
        _, rows, cols = slots.shape
        hr = rows // 2
        theirs = _pair_swap(slots, name="rs_pair_" + tag)
        ours = lax.dynamic_slice_in_dim(slots, ci * hr, hr, axis=1)
        pair = _add2(ours.reshape(N_CHIP * hr, cols), theirs.reshape(N_CHIP * hr, cols), name="rs_pair_sum_" + tag)
        return pair.reshape(N_CHIP, hr, cols)

    def finish(recv, pair, tag):
        recv = own_slot(recv, lax.dynamic_slice_in_dim(pair, chip, 1, axis=0))
        half = _sum_slots(recv, name="rs_sum_" + tag)
        other = _sibling_copy(half, name="rs_sibling_" + tag)
        return jnp.where(ci == 0, jnp.concatenate([half, other], axis=0), jnp.concatenate([other, half], axis=0))

    started = {}

    def late_grads(gw):
        slots = []
        for k in range(N_CHIP):
            parts = []
            for n in LATE:
                rows = w[n].shape[0]
                blk = gw[n][:, k * rows:(k + 1) * rows] if n in COL_SHARDED else gw[n][k * rows:(k + 1) * rows]
                parts.append(blk.astype(MMD))
            slots.append(jnp.concatenate(parts, axis=0))
        pair = pair_sums(jnp.stack(slots), "late")
        sems, src, land, token = _chips_start(pair, True, name="rs_late_start")
        started["late"] = (sems, src, land)
        return token[0:1, 0:1]

    def in_grad(g):
        g_in = _unpermute_in(g)
        cols_in = w["w_in"].shape[1]
        pair = pair_sums(jnp.stack([g_in[:, k * cols_in:(k + 1) * cols_in].astype(MMD) for k in range(N_CHIP)]), "w_in")
        sems, src, land, token = _chips_start(pair, True, name="rs_w_in_start")
        started["w_in"] = (sems, src, land)
        return token

    loss, grad_x, dmod, gw, gs = _local_step(x, tgt, mod, wts, small, late_weights, late_grads, in_grad)

    grads = {}
    pair, land = _chips_wait(*started["w_in"], grad_x, True, name="rs_w_in_wait")
    grads["w_in"] = finish(land, pair, "w_in")
    pair, land = _chips_wait(*started["late"], grad_x, True, name="rs_late_wait")
    total, o = finish(land, pair, "late"), 0
    for n in LATE:
        rows = w[n].shape[0]
        grads[n] = total[o:o + rows]
        o += rows

    order = ([dmod.reshape(-1)] + [gs[n].reshape(-1) for n in SMALL if n != "b_ada"] + [gs["conv_w"].reshape(-1)]
             + [loss.reshape(-1)])
    vec = jnp.concatenate(order)
    n_small = vec.shape[0]
    n_pad = -(-n_small // 1024) * 1024
    g2 = _allgather8(_pad_to(vec, n_pad).reshape(8, n_pad // 8), name="ag_small")
    tot = _rows_sum(g2, N_DEV, name="small_sum").reshape(-1)
    loss = tot[n_small - 1]
    dmod_all = g2.reshape(N_DEV, n_pad)[:, 0:6 * d]
    off = 0
    for n in SMALL:
        grads[n] = tot[off:off + w[n].size].reshape(w[n].shape)
        off += w[n].size
    conv_full = tot[off:off + D_CONV * N_CHIP * cw].reshape(D_CONV, N_CHIP * cw)
    grads["conv_w"] = lax.dynamic_slice_in_dim(conv_full, chip * cw, cw, axis=1)
    ada_cols = w["w_ada"].shape[1]
    dmod_mine = lax.dynamic_slice_in_dim(dmod_all, chip * ada_cols, ada_cols, axis=1).astype(MMD)
    grads["w_ada"] = _mm_tn(sc, dmod_mine, name="ada_dw", tk=512, tn=512, tmm=8)

    delta, new_m, new_v = {}, {}, {}
    pack = lambda t: jnp.concatenate([t[n].reshape(-1) for n in SMALL]).reshape(1, -1)
    ds_, ms_, vs_ = _adamw(pack(w), pack(grads), pack(m), pack(v), name="adamw_small")
    off = 0
    for n in SMALL:
        for dst, src in ((delta, ds_), (new_m, ms_), (new_v, vs_)):
            dst[n] = src[0, off:off + w[n].size].reshape(w[n].shape)
        off += w[n].size
    for n in ("w_ada", "conv_w") + BIG:
        delta[n], new_m[n], new_v[n] = _adamw(w[n], grads[n], m[n], v[n], name="adamw_" + n)
    return loss, grad_x, grads, delta, new_m, new_v


def kernel(x, c, w_ada, b_ada, norm1_w, norm2_w, w_in, q_norm_w, k_norm_w, conv_w, conv_b, A_log, dt_bias, ssd_D, ssd_norm_w, w_attn_out, w_ssd_out, w_o, w_mlp1, w_mlp2, loss_target, m_w_ada, m_b_ada, m_norm1_w, m_norm2_w, m_w_in, m_q_norm_w, m_k_norm_w, m_conv_w, m_conv_b, m_A_log, m_dt_bias, m_ssd_D, m_ssd_norm_w, m_w_attn_out, m_w_ssd_out, m_w_o, m_w_mlp1, m_w_mlp2, v_w_ada, v_b_ada, v_norm1_w, v_norm2_w, v_w_in, v_q_norm_w, v_k_norm_w, v_conv_w, v_conv_b, v_A_log, v_dt_bias, v_ssd_D, v_ssd_norm_w, v_w_attn_out, v_w_ssd_out, v_w_o, v_w_mlp1, v_w_mlp2):
    args = dict(locals())
    strip = lambda a: a[0] if a.ndim == 3 else a
    w = {n: strip(args[n]) for n in NAMES + ("x", "c")}
    m = {n: strip(args["m_" + n]) for n in NAMES}
    v = {n: strip(args["v_" + n]) for n in NAMES}
    loss, grad_x, grads, delta, new_m, new_v = _step(w, m, v, loss_target[0])
    like = lambda t, n: t.reshape(args[n].shape)
    return (loss, grad_x[None], *[like(grads[n], n) for n in NAMES], *[like(delta[n], n) for n in NAMES],
            *[like(new_m[n], n) for n in NAMES], *[like(new_v[n], n) for n in NAMES])
```

```python
import functools
import math

import jax
import jax.numpy as jnp
from jax import lax
from jax.experimental import pallas as pl
from jax.experimental.pallas import tpu as pltpu

F32 = jnp.float32
MMD = jnp.bfloat16
EPS = 1e-6
NEG = -1e30
MIB = 1024 * 1024
VMEM_BIG = 56 * MIB
VMEM_MID = 40 * MIB

GRID_W = 64
N_Q_HEADS, N_KV_HEADS, HEAD_DIM = 16, 4, 64
ROPE_THETA = 10000.0
SSD_HEADS, SSD_GROUPS, SSD_P, SSD_N, CHUNK = 32, 4, 64, 128, 128
HPG = SSD_HEADS // SSD_GROUPS
D_CONV = 5
ADAM_LR, ADAM_B1, ADAM_B2, ADAM_EPS, ADAM_WD, ADAM_STEP = 0.001, 0.9, 0.999, 1e-08, 0.01, 10

Z0, GA0, GS0, XS0, B0, C0, Q0, K0, V0, DT0, PW = 0, 2048, 3072, 4096, 6144, 6656, 7168, 8192, 8448, 8704, 8832

MESH = pl.DeviceIdType.MESH
NT = (((1,), (1,)), ((), ()))
TN = (((0,), (0,)), ((), ()))


def _cp(sem=None, vmem=VMEM_MID):
    return pltpu.CompilerParams(dimension_semantics=sem, vmem_limit_bytes=vmem)


def _tile(n, pref):
    t = min(n, pref)
    while n % t:
        t //= 2
    return t


def _dot(a, b, dims=None):
    if dims is None:
        return jnp.dot(a, b, preferred_element_type=F32)
    return lax.dot_general(a, b, dims, preferred_element_type=F32)


def _dot_hi(a, b):
    return jnp.dot(a, b, precision=lax.Precision.HIGHEST, preferred_element_type=F32)


def _sigmoid(x):
    return jax.nn.sigmoid(x)


def _mm(a, b, *, name, outs, nt=False, ta=False, extras=(), epi=None, tm=512, tn=512, n=None, b_outer=False,
        vmem=VMEM_MID):
    assert not (nt and ta)
    k, m = a.shape if ta else a.shape[::-1]
    if n is None:
        n = b.shape[0] if nt else b.shape[1]
    tm, tn = _tile(m, tm), _tile(n, tn)
    gi, gj = m // tm, n // tn
    if b_outer:
        grid = (gj, gi)
        ij = lambda p, q: (q, p)
    else:
        grid = (gi, gj)
        ij = lambda p, q: (p, q)
    if ta:
        a_spec = pl.BlockSpec((k, tm), lambda p, q: (0, ij(p, q)[0]))
    else:
        a_spec = pl.BlockSpec((tm, k), lambda p, q: (ij(p, q)[0], 0))
    if nt:
        b_spec = pl.BlockSpec((tn, k), lambda p, q: (ij(p, q)[1], 0))
    else:
        b_spec = pl.BlockSpec((k, tn), lambda p, q: (0, ij(p, q)[1]))
    e_specs = []
    for arr, kind, off in extras:
        ob = off // tn
        assert off % tn == 0
        if kind == "tile":
            e_specs.append(pl.BlockSpec((tm, tn), lambda p, q, ob=ob: (ij(p, q)[0], ob + ij(p, q)[1])))
        else:
            e_specs.append(pl.BlockSpec((1, tn), lambda p, q, ob=ob: (0, ob + ij(p, q)[1])))
    ne = len(extras)

    def body(a_ref, b_ref, *rest):
        acc = _dot(a_ref[...], b_ref[...], NT if nt else (TN if ta else None))
        res = epi(acc, *[e[...] for e in rest[:ne]]) if epi is not None else (acc,)
        for o_ref, r in zip(rest[ne:], res):
            o_ref[...] = r.astype(o_ref.dtype)

    out = pl.pallas_call(
        body, name=name, grid=grid,
        in_specs=[a_spec, b_spec] + e_specs,
        out_specs=[pl.BlockSpec((tm, tn), lambda p, q: ij(p, q)) for _ in outs],
        out_shape=[jax.ShapeDtypeStruct((m, n), dt) for dt in outs],
        compiler_params=_cp(("arbitrary", "arbitrary"), vmem),
    )(a, b, *[e[0] for e in extras])
    return out if len(outs) > 1 else out[0]


def _mm_tn(a, g, *, name, tk=512, tn=1024, tmm=4096, vmem=VMEM_MID):
    m, k = a.shape
    n = g.shape[1]
    tk, tn, tmm = _tile(k, tk), _tile(n, tn), _tile(m, tmm)

    def body(a_ref, g_ref, o_ref):
        p = _dot(a_ref[...], g_ref[...], TN)

        @pl.when(pl.program_id(2) == 0)
        def _():
            o_ref[...] = p

        @pl.when(pl.program_id(2) > 0)
        def _():
            o_ref[...] += p

    return pl.pallas_call(
        body, name=name, grid=(k // tk, n // tn, m // tmm),
        in_specs=[pl.BlockSpec((tmm, tk), lambda i, j, r: (r, i)), pl.BlockSpec((tmm, tn), lambda i, j, r: (r, j))],
        out_specs=pl.BlockSpec((tk, tn), lambda i, j, r: (i, j)),
        out_shape=jax.ShapeDtypeStruct((k, n), F32),
        compiler_params=_cp(("arbitrary", "arbitrary", "arbitrary"), vmem),
    )(a, g)


def _adamw(w, g, m, v, *, name):
    r, c = w.shape
    tr = _tile(r, 256) if r % 8 == 0 else r

    def body(w_ref, g_ref, m_ref, v_ref, d_ref, nm_ref, nv_ref):
        gg = g_ref[...]
        nm = ADAM_B1 * m_ref[...] + (1.0 - ADAM_B1) * gg
        nv = ADAM_B2 * v_ref[...] + (1.0 - ADAM_B2) * jnp.square(gg)
        m_hat = nm / (1.0 - ADAM_B1 ** ADAM_STEP)
        v_hat = nv / (1.0 - ADAM_B2 ** ADAM_STEP)
        d_ref[...] = -ADAM_LR * (m_hat / (jnp.sqrt(v_hat) + ADAM_EPS) + ADAM_WD * w_ref[...])
        nm_ref[...] = nm
        nv_ref[...] = nv

    spec = pl.BlockSpec((tr, c), lambda i: (i, 0))
    return pl.pallas_call(
        body, name=name, grid=(r // tr,), in_specs=[spec] * 4, out_specs=[spec] * 3,
        out_shape=[jax.ShapeDtypeStruct((r, c), F32)] * 3, compiler_params=_cp(("arbitrary",)),
    )(w, g, m, v)


def _rows_sum(a, groups, *, name):
    r = a.shape[0] // groups

    def body(a_ref, o_ref):
        acc = a_ref[0:r, :]
        for d in range(1, groups):
            acc = acc + a_ref[d * r:(d + 1) * r, :]
        o_ref[...] = acc

    return pl.pallas_call(body, name=name, out_shape=jax.ShapeDtypeStruct((r, a.shape[1]), F32))(a)


def _silu_cast(a, *, name):
    def body(a_ref, o_ref):
        x = a_ref[...]
        o_ref[...] = (x * _sigmoid(x)).astype(o_ref.dtype)

    return pl.pallas_call(body, name=name, out_shape=jax.ShapeDtypeStruct(a.shape, MMD))(a)


def _sumsq(a, *, name):
    m, n = a.shape
    tm = _tile(m, 512)

    def body(a_ref, o_ref):
        x = a_ref[...]
        p = jnp.sum(jnp.sum(x * x, axis=1, keepdims=True), axis=0, keepdims=True)

        @pl.when(pl.program_id(0) == 0)
        def _():
            o_ref[...] = p

        @pl.when(pl.program_id(0) > 0)
        def _():
            o_ref[...] += p

    return pl.pallas_call(
        body, name=name, grid=(m // tm,), in_specs=[pl.BlockSpec((tm, n), lambda i: (i, 0))],
        out_specs=pl.BlockSpec((1, 1), lambda i: (0, 0)), out_shape=jax.ShapeDtypeStruct((1, 1), F32),
        compiler_params=_cp(("arbitrary",)),
    )(a)


def _acc_rows(o_ref, p, first):
    @pl.when(first)
    def _():
        o_ref[...] = p

    @pl.when(jnp.logical_not(first))
    def _():
        o_ref[...] += p


def _ln_mod(x, w, scale, shift, *, name):
    s, d = x.shape
    tm = _tile(s, 512)

    def body(x_ref, w_ref, sc_ref, sh_ref, o_ref):
        xv = x_ref[...]
        r = lax.rsqrt(jnp.mean(xv * xv, axis=-1, keepdims=True) + EPS)
        o_ref[...] = ((xv * r) * w_ref[...] * (1.0 + sc_ref[...]) + sh_ref[...]).astype(o_ref.dtype)

    row = pl.BlockSpec((1, d), lambda i: (0, 0))
    big = pl.BlockSpec((tm, d), lambda i: (i, 0))
    return pl.pallas_call(
        body, name=name, grid=(s // tm,), in_specs=[big, row, row, row], out_specs=big,
        out_shape=jax.ShapeDtypeStruct((s, d), MMD), compiler_params=_cp(("arbitrary",)),
    )(x, w, scale, shift)


def _ln_mod_bwd(dh, x, w, scale, dres, *, name):
    s, d = x.shape
    tm = _tile(s, 512)

    def body(dh_ref, x_ref, w_ref, sc_ref, dres_ref, dx_ref, dsh_ref, dsc_ref, dw_ref):
        xv = x_ref[...]
        dhv = dh_ref[...].astype(F32)
        r = lax.rsqrt(jnp.mean(xv * xv, axis=-1, keepdims=True) + EPS)
        nv = xv * r
        wv = w_ref[...]
        g1 = 1.0 + sc_ref[...]
        dn = dhv * (wv * g1)
        dx_ref[...] = dres_ref[...] + r * (dn - nv * jnp.mean(dn * nv, axis=-1, keepdims=True))
        first = pl.program_id(0) == 0
        _acc_rows(dsh_ref, jnp.sum(dhv, axis=0, keepdims=True), first)
        _acc_rows(dsc_ref, jnp.sum(dhv * nv * wv, axis=0, keepdims=True), first)
        _acc_rows(dw_ref, jnp.sum(dhv * nv * g1, axis=0, keepdims=True), first)

    row = pl.BlockSpec((1, d), lambda i: (0, 0))
    big = pl.BlockSpec((tm, d), lambda i: (i, 0))
    return pl.pallas_call(
        body, name=name, grid=(s // tm,), in_specs=[big, big, row, row, big], out_specs=[big, row, row, row],
        out_shape=[jax.ShapeDtypeStruct((s, d), F32)] + [jax.ShapeDtypeStruct((1, d), F32)] * 3,
        compiler_params=_cp(("arbitrary",)),
    )(dh, x, w, scale, dres)


def _gate_bwd(dy, u, gate, *, name):
    s, d = dy.shape
    tm = _tile(s, 512)

    def body(dy_ref, u_ref, g_ref, du_ref, dg_ref):
        dyv = dy_ref[...]
        du_ref[...] = (dyv * g_ref[...]).astype(du_ref.dtype)
        _acc_rows(dg_ref, jnp.sum(dyv * u_ref[...].astype(F32), axis=0, keepdims=True), pl.program_id(0) == 0)

    row = pl.BlockSpec((1, d), lambda i: (0, 0))
    big = pl.BlockSpec((tm, d), lambda i: (i, 0))
    return pl.pallas_call(
        body, name=name, grid=(s // tm,), in_specs=[big, big, row], out_specs=[big, row],
        out_shape=[jax.ShapeDtypeStruct((s, d), MMD), jax.ShapeDtypeStruct((1, d), F32)],
        compiler_params=_cp(("arbitrary",)),
    )(dy, u, gate)


def _seg64(v, e):
    hi = v.astype(jnp.bfloat16)
    lo = (v - hi.astype(F32)).astype(jnp.bfloat16)
    return _dot(hi, e) + _dot(lo, e)


def _rope_tables(s):
    rows = s // GRID_W
    pos_row = jnp.repeat(jnp.arange(rows, dtype=jnp.int32), GRID_W).astype(F32)
    pos_col = jnp.tile(jnp.arange(GRID_W, dtype=jnp.int32), rows).astype(F32)
    axis_dim = HEAD_DIM // 2
    inv_freq = ROPE_THETA ** (-jnp.arange(0, axis_dim, 2, dtype=F32) / axis_dim)
    ang_r = pos_row[:, None] * inv_freq[None, :]
    ang_c = pos_col[:, None] * inv_freq[None, :]
    zero = jnp.zeros_like(ang_r)
    cos = jnp.concatenate([jnp.cos(ang_r), jnp.cos(ang_r), jnp.cos(ang_c), jnp.cos(ang_c)], axis=1)
    s_a = jnp.concatenate([-jnp.sin(ang_r), zero, -jnp.sin(ang_c), zero], axis=1)
    s_b = jnp.concatenate([zero, jnp.sin(ang_r), zero, jnp.sin(ang_c)], axis=1)
    return [jnp.tile(t, (1, 2)) for t in (cos, s_a, s_b)]


def _e128():
    i = jnp.arange(128)
    return (i[:, None] // 64 == i[None, :] // 64).astype(jnp.bfloat16)


QKW = N_Q_HEADS * HEAD_DIM + N_KV_HEADS * HEAD_DIM


def _qk_fwd(proj, wrow, scrow, tabs, *, name):
    s = proj.shape[0]
    tm = _tile(s, 1024)

    def body(x_ref, w_ref, sc_ref, cos_ref, sa_ref, sb_ref, e_ref, o_ref, ot_ref):
        u = x_ref[...].astype(F32)
        r = lax.rsqrt(_seg64(u * u, e_ref[...]) * (1.0 / HEAD_DIM) + EPS)
        nv = (u * r) * w_ref[...]
        ro = nv * cos_ref[...] + pltpu.roll(nv, 112, 1) * sa_ref[...] + pltpu.roll(nv, 16, 1) * sb_ref[...]
        out = ro * sc_ref[...]
        o_ref[...] = out.astype(o_ref.dtype)
        ot_ref[...] = out.T.astype(ot_ref.dtype)

    tab = pl.BlockSpec((tm, 128), lambda i, j: (i, 0))
    row = pl.BlockSpec((1, 128), lambda i, j: (0, j))
    return pl.pallas_call(
        body, name=name, grid=(s // tm, QKW // 128),
        in_specs=[pl.BlockSpec((tm, 128), lambda i, j: (i, Q0 // 128 + j)), row, row, tab, tab, tab,
                  pl.BlockSpec((128, 128), lambda i, j: (0, 0))],
        out_specs=[pl.BlockSpec((tm, 128), lambda i, j: (i, j)), pl.BlockSpec((128, tm), lambda i, j: (j, i))],
        out_shape=[jax.ShapeDtypeStruct((s, QKW), MMD), jax.ShapeDtypeStruct((QKW, s), MMD)],
        compiler_params=_cp(("arbitrary", "arbitrary")),
    )(proj, wrow, scrow, *tabs, _e128())


def _qk_bwd(dqkt, proj, wrow, scrow, tabs, *, name):
    s = proj.shape[0]
    tm = _tile(s, 1024)

    def body(d_ref, x_ref, w_ref, sc_ref, cos_ref, sa_ref, sb_ref, e_ref, du_ref, dw_ref):
        e = e_ref[...]
        d = d_ref[...].T * sc_ref[...]
        dn = d * cos_ref[...] + pltpu.roll(d * sa_ref[...], 16, 1) + pltpu.roll(d * sb_ref[...], 112, 1)
        u = x_ref[...].astype(F32)
        r = lax.rsqrt(_seg64(u * u, e) * (1.0 / HEAD_DIM) + EPS)
        uh = u * r
        _acc_rows(dw_ref, jnp.sum(dn * uh, axis=0, keepdims=True), pl.program_id(1) == 0)
        dnw = dn * w_ref[...]
        du_ref[...] = (r * (dnw - uh * (_seg64(dnw * uh, e) * (1.0 / HEAD_DIM)))).astype(du_ref.dtype)

    tab = pl.BlockSpec((tm, 128), lambda j, i: (i, 0))
    row = pl.BlockSpec((1, 128), lambda j, i: (0, j))
    return pl.pallas_call(
        body, name=name, grid=(QKW // 128, s // tm),
        in_specs=[pl.BlockSpec((128, tm), lambda j, i: (j, i)), pl.BlockSpec((tm, 128), lambda j, i: (i, Q0 // 128 + j)),
                  row, row, tab, tab, tab, pl.BlockSpec((128, 128), lambda j, i: (0, 0))],
        out_specs=[pl.BlockSpec((tm, 128), lambda j, i: (i, j)), row],
        out_shape=[jax.ShapeDtypeStruct((s, QKW), MMD), jax.ShapeDtypeStruct((1, QKW), F32)],
        compiler_params=_cp(("arbitrary", "arbitrary")),
    )(dqkt, proj, wrow, scrow, *tabs, _e128())


REP = N_Q_HEADS // N_KV_HEADS


def _lanes(ref):
    return jnp.concatenate([ref[r] for r in range(REP)], axis=1)


V_AUG = HEAD_DIM + 8
LOG2E = math.log2(math.e)


def _flash_fwd(qkt, vta, *, name):
    s = qkt.shape[2]
    tq, tk = _tile(s, 1024), _tile(s, 512)
    nk = s // tk
    lanes = REP * tq

    def body(q_ref, k_ref, v_ref, o_ref, lse_ref, m_ref, acc_ref):
        j = pl.program_id(2)

        @pl.when(j == 0)
        def _():
            m_ref[...] = jnp.full_like(m_ref, NEG)
            acc_ref[...] = jnp.zeros_like(acc_ref)

        st = _dot(k_ref[0], _lanes(q_ref), TN)
        m_prev = m_ref[...]
        m_new = jnp.maximum(m_prev, jnp.max(st, axis=0, keepdims=True))
        p = jnp.exp2(st - m_new).astype(MMD)
        acc_ref[...] = jnp.exp2(m_prev - m_new) * acc_ref[...] + _dot(v_ref[0], p)
        m_ref[...] = m_new

        @pl.when(j == nk - 1)
        def _():
            acc = acc_ref[...]
            l = acc[HEAD_DIM:HEAD_DIM + 1]
            o = acc[0:HEAD_DIM] / l
            ls = m_ref[...] + jnp.log(l) * LOG2E
            for r in range(REP):
                o_ref[r] = o[:, r * tq:(r + 1) * tq].astype(o_ref.dtype)
                lse_ref[r] = ls[:, r * tq:(r + 1) * tq]

    qspec = pl.BlockSpec((REP, HEAD_DIM, tq), lambda g, i, j: (g, 0, i))
    return pl.pallas_call(
        body, name=name, grid=(N_KV_HEADS, s // tq, nk),
        in_specs=[qspec, pl.BlockSpec((1, HEAD_DIM, tk), lambda g, i, j: (N_Q_HEADS + g, 0, j)),
                  pl.BlockSpec((1, V_AUG, tk), lambda g, i, j: (g, 0, j))],
        out_specs=[qspec, pl.BlockSpec((REP, 1, tq), lambda g, i, j: (g, 0, i))],
        out_shape=[jax.ShapeDtypeStruct((N_Q_HEADS, HEAD_DIM, s), MMD), jax.ShapeDtypeStruct((N_Q_HEADS, 1, s), F32)],
        scratch_shapes=[pltpu.VMEM((1, lanes), F32), pltpu.VMEM((V_AUG, lanes), F32)],
        compiler_params=_cp(("arbitrary", "arbitrary", "arbitrary"), VMEM_BIG),
    )(qkt, qkt, vta)


def _flash_bwd(qkt, k_h, v_h, dot, ot, lse, *, name):
    s = qkt.shape[2]
    tq, tk = _tile(s, 512), _tile(s, 1024)
    nk = s // tk

    def body(q_ref, kt_ref, k_ref, v_ref, do_ref, o_ref, lse_ref, dq_ref, dk_ref, dv_ref, dq_acc):
        i, j = pl.program_id(1), pl.program_id(2)
        q, do = _lanes(q_ref), _lanes(do_ref)
        delta = jnp.sum(do.astype(F32) * _lanes(o_ref).astype(F32), axis=0, keepdims=True)
        k, v = k_ref[0], v_ref[0]
        p = jnp.exp2(_dot(k, q) - _lanes(lse_ref))
        dvc = _dot(p.astype(MMD), do, NT)
        ds = (p * (_dot(v, do) - delta)).astype(MMD)
        dkc = _dot(ds, q, NT) * (1.0 / LOG2E)
        dqc = _dot(kt_ref[0], ds)
        rows = pl.ds(pl.multiple_of(j * tk, tk), tk)

        @pl.when(i == 0)
        def _():
            dk_ref[0, rows, :] = dkc
            dv_ref[0, rows, :] = dvc

        @pl.when(i > 0)
        def _():
            dk_ref[0, rows, :] += dkc
            dv_ref[0, rows, :] += dvc

        @pl.when(j == 0)
        def _():
            dq_acc[...] = dqc

        @pl.when(j > 0)
        def _():
            dq_acc[...] += dqc

        @pl.when(j == nk - 1)
        def _():
            acc = dq_acc[...]
            for r in range(REP):
                dq_ref[r] = acc[:, r * tq:(r + 1) * tq]

    qspec = pl.BlockSpec((REP, HEAD_DIM, tq), lambda g, i, j: (g, 0, i))
    kvin = pl.BlockSpec((1, tk, HEAD_DIM), lambda g, i, j: (g, j, 0))
    kvres = pl.BlockSpec((1, s, HEAD_DIM), lambda g, i, j: (g, 0, 0))
    return pl.pallas_call(
        body, name=name, grid=(N_KV_HEADS, s // tq, nk),
        in_specs=[qspec, pl.BlockSpec((1, HEAD_DIM, tk), lambda g, i, j: (N_Q_HEADS + g, 0, j)), kvin, kvin,
                  qspec, qspec, pl.BlockSpec((REP, 1, tq), lambda g, i, j: (g, 0, i))],
        out_specs=[qspec, kvres, kvres],
        out_shape=[jax.ShapeDtypeStruct((N_Q_HEADS, HEAD_DIM, s), F32), jax.ShapeDtypeStruct((N_KV_HEADS, s, HEAD_DIM), F32),
                   jax.ShapeDtypeStruct((N_KV_HEADS, s, HEAD_DIM), F32)],
        scratch_shapes=[pltpu.VMEM((HEAD_DIM, REP * tq), F32)],
        compiler_params=_cp(("arbitrary", "arbitrary", "arbitrary"), VMEM_BIG),
    )(qkt, qkt, k_h, v_h, dot, ot, lse)


HALO = 8
CONV_W = 2048 + 2 * SSD_GROUPS * SSD_N


def _shifted(win, off, r):
    return pltpu.roll(win, (r + 2 * HALO - off) % (r + 2 * HALO), 0)[0:r]


def _conv_fwd(proj, w8, brow, *, name):
    s = proj.shape[0]
    cb = 256
    r = _tile(s, 512)

    def body(x_ref, w_ref, b_ref, o_ref, pad_ref):
        zeros = jnp.zeros((HALO, cb), F32)
        pad_ref[0:HALO, :] = zeros
        pad_ref[s + HALO:s + 2 * HALO, :] = zeros

        def fill(i, carry):
            st = pl.multiple_of(i * r, r)
            pad_ref[pl.ds(st + HALO, r), :] = x_ref[pl.ds(st, r), :].astype(F32)
            return carry

        lax.fori_loop(0, s // r, fill, 0)
        wv = w_ref[...]
        bv = b_ref[...]

        def step(i, carry):
            st = pl.multiple_of(i * r, r)
            win = pad_ref[pl.ds(st, r + 2 * HALO), :]
            acc = bv + wv[0:1, :] * _shifted(win, HALO - 2, r)
            for t in range(1, D_CONV):
                acc = acc + wv[t:t + 1, :] * _shifted(win, HALO - 2 + t, r)
            o_ref[pl.ds(st, r), :] = (acc * _sigmoid(acc)).astype(o_ref.dtype)
            return carry

        lax.fori_loop(0, s // r, step, 0)

    return pl.pallas_call(
        body, name=name, grid=(CONV_W // cb,),
        in_specs=[pl.BlockSpec((s, cb), lambda j: (0, XS0 // cb + j)), pl.BlockSpec((8, cb), lambda j: (0, j)),
                  pl.BlockSpec((1, cb), lambda j: (0, j))],
        out_specs=pl.BlockSpec((s, cb), lambda j: (0, j)),
        out_shape=jax.ShapeDtypeStruct((s, CONV_W), MMD),
        scratch_shapes=[pltpu.VMEM((s + 2 * HALO, cb), F32)],
        compiler_params=_cp(("arbitrary",), VMEM_MID),
    )(proj, w8, brow)


def _conv_bwd(proj, col0, ga, gb, w8, brow, *, name):
    s = proj.shape[0]
    width = ga.shape[1]
    cb = 128
    c0 = col0 // cb
    r = _tile(s, 512)

    def body(x_ref, ga_ref, gb_ref, w_ref, b_ref, dx_ref, dw_ref, db_ref, xpad, dpad):
        zeros = jnp.zeros((HALO, cb), F32)
        for ref in (xpad, dpad):
            ref[0:HALO, :] = zeros
            ref[s + HALO:s + 2 * HALO, :] = zeros

        def fill(i, carry):
            st = pl.multiple_of(i * r, r)
            xpad[pl.ds(st + HALO, r), :] = x_ref[pl.ds(st, r), :].astype(F32)
            return carry

        lax.fori_loop(0, s // r, fill, 0)
        wv = w_ref[...]
        bv = b_ref[...]

        def first(i, carry):
            st = pl.multiple_of(i * r, r)
            win = xpad[pl.ds(st, r + 2 * HALO), :]
            taps = [_shifted(win, HALO - 2 + t, r) for t in range(D_CONV)]
            u = bv
            for t in range(D_CONV):
                u = u + wv[t:t + 1, :] * taps[t]
            sg = _sigmoid(u)
            du = (ga_ref[pl.ds(st, r), :] + gb_ref[pl.ds(st, r), :]) * (sg * (1.0 + u * (1.0 - sg)))
            dpad[pl.ds(st + HALO, r), :] = du
            out = [carry[0] + jnp.sum(du, axis=0, keepdims=True)]
            for t in range(D_CONV):
                out.append(carry[1 + t] + jnp.sum(du * taps[t], axis=0, keepdims=True))
            return tuple(out)

        sums = lax.fori_loop(0, s // r, first, tuple(jnp.zeros((1, cb), F32) for _ in range(1 + D_CONV)))
        db_ref[...] = sums[0]
        for t in range(D_CONV):
            dw_ref[t:t + 1, :] = sums[1 + t]
        dw_ref[D_CONV:8, :] = jnp.zeros((8 - D_CONV, cb), F32)

        def second(i, carry):
            st = pl.multiple_of(i * r, r)
            win = dpad[pl.ds(st, r + 2 * HALO), :]
            acc = wv[0:1, :] * _shifted(win, HALO + 2, r)
            for t in range(1, D_CONV):
                acc = acc + wv[t:t + 1, :] * _shifted(win, HALO + 2 - t, r)
            dx_ref[pl.ds(st, r), :] = acc.astype(dx_ref.dtype)
            return carry

        lax.fori_loop(0, s // r, second, 0)

    col = pl.BlockSpec((s, cb), lambda j: (0, j))
    return pl.pallas_call(
        body, name=name, grid=(width // cb,),
        in_specs=[pl.BlockSpec((s, cb), lambda j: (0, XS0 // cb + c0 + j)), col, col,
                  pl.BlockSpec((8, cb), lambda j: (0, c0 + j)), pl.BlockSpec((1, cb), lambda j: (0, c0 + j))],
        out_specs=[col, pl.BlockSpec((8, cb), lambda j: (0, j)), pl.BlockSpec((1, cb), lambda j: (0, j))],
        out_shape=[jax.ShapeDtypeStruct((s, width), MMD), jax.ShapeDtypeStruct((8, width), F32),
                   jax.ShapeDtypeStruct((1, width), F32)],
        scratch_shapes=[pltpu.VMEM((s + 2 * HALO, cb), F32), pltpu.VMEM((s + 2 * HALO, cb), F32)],
        compiler_params=_cp(("arbitrary",), VMEM_BIG),
    )(proj, ga, gb, w8, brow)


def _tri(lower):
    i = jnp.arange(CHUNK)
    return ((i[:, None] >= i[None, :]) if lower else (i[:, None] <= i[None, :])).astype(F32)


def _dt_fwd(raw, bias, arow, *, name):
    s = raw.shape[0]

    def body(r_ref, b_ref, a_ref, lo_ref, up_ref, dt_ref, cs_ref):
        u = r_ref[...] + b_ref[...]
        dt = jnp.maximum(u, 0.0) + jnp.log1p(jnp.exp(-jnp.abs(u)))
        dt_ref[...] = dt
        a = dt * a_ref[...]
        lane = lax.broadcasted_iota(jnp.int32, (CHUNK, 128), 1)
        cs_ref[...] = jnp.where(lane < SSD_HEADS, _dot_hi(lo_ref[...], a), _dot_hi(up_ref[...], a))

    blk = pl.BlockSpec((CHUNK, 128), lambda i: (i, 0))
    row = pl.BlockSpec((1, 128), lambda i: (0, 0))
    tri = pl.BlockSpec((CHUNK, CHUNK), lambda i: (0, 0))
    return pl.pallas_call(
        body, name=name, grid=(s // CHUNK,), in_specs=[blk, row, row, tri, tri], out_specs=[blk, blk],
        out_shape=[jax.ShapeDtypeStruct((s, 128), F32)] * 2, compiler_params=_cp(("arbitrary",)),
    )(raw, bias, arow, _tri(True), _tri(False))


def _dt_bwd(ddt, raw, bias, *, name):
    s = raw.shape[0]
    tm = _tile(s, 1024)

    def body(d_ref, r_ref, b_ref, o_ref, db_ref):
        g = d_ref[...] * _sigmoid(r_ref[...] + b_ref[...])
        o_ref[...] = g.astype(o_ref.dtype)
        _acc_rows(db_ref, jnp.sum(g, axis=0, keepdims=True), pl.program_id(0) == 0)

    blk = pl.BlockSpec((tm, 128), lambda i: (i, 0))
    row = pl.BlockSpec((1, 128), lambda i: (0, 0))
    return pl.pallas_call(
        body, name=name, grid=(s // tm,), in_specs=[blk, blk, row], out_specs=[blk, row],
        out_shape=[jax.ShapeDtypeStruct((s, 128), MMD), jax.ShapeDtypeStruct((1, 128), F32)],
        compiler_params=_cp(("arbitrary",)),
    )(ddt, raw, bias)


GW = HPG * SSD_P


GPS = 4


def _ssd_specs(nc, rev):
    cc = (lambda c: nc - 1 - c) if rev else (lambda c: c)
    nb = SSD_GROUPS // GPS
    return dict(
        x=pl.BlockSpec((CHUNK, GPS * GW), lambda g, c: (cc(c), g)),
        b=pl.BlockSpec((CHUNK, GPS * SSD_N), lambda g, c: (cc(c), 2048 // (GPS * SSD_N) + g)),
        c=pl.BlockSpec((CHUNK, GPS * SSD_N), lambda g, c: (cc(c), 2048 // (GPS * SSD_N) + nb + g)),
        col=pl.BlockSpec((GPS, CHUNK, HPG), lambda g, c: (g, cc(c), 0)),
        lanes=pl.BlockSpec((CHUNK, 128), lambda g, c: (cc(c), 0)),
        rowt=pl.BlockSpec((GPS, 1, HPG, CHUNK), lambda g, c: (g, cc(c), 0, 0)),
        drow=pl.BlockSpec((1, GPS * GW), lambda g, c: (0, g)),
        y=pl.BlockSpec((CHUNK, GPS * GW), lambda g, c: (cc(c), g)),
        h=pl.BlockSpec((GPS, 1, SSD_N, GW), lambda g, c: (g, cc(c), 0, 0)),
        n=pl.BlockSpec((CHUNK, GPS * SSD_N), lambda g, c: (cc(c), g)),
    )


def _ssd_mask(anti):
    ii = lax.broadcasted_iota(jnp.int32, (CHUNK, CHUNK), 0)
    jj = lax.broadcasted_iota(jnp.int32, (CHUNK, CHUNK), 1)
    return ii, jj, (ii <= jj) if anti else (ii >= jj)


def _expand(x, ex):
    h1 = x.astype(jnp.bfloat16)
    r1 = x - h1.astype(F32)
    h2 = r1.astype(jnp.bfloat16)
    h3 = (r1 - h2.astype(F32)).astype(jnp.bfloat16)
    return _dot(h1, ex) + _dot(h2, ex) + _dot(h3, ex)


def _headsum(a, e):
    hi = a.astype(jnp.bfloat16)
    return _dot(hi, e) + _dot((a - hi.astype(F32)).astype(jnp.bfloat16), e)


def _expand_mats():
    lane = jnp.arange(128)[None, :, None]
    col = jnp.arange(GW)[None, None, :]
    base = (jnp.arange(2)[:, None] * SSD_HEADS + jnp.arange(SSD_GROUPS)[None, :] * HPG).reshape(2 * SSD_GROUPS, 1, 1)
    return (lane == base + col // SSD_P).astype(jnp.bfloat16)


def _headsum_mats():
    e1 = (jnp.arange(GW)[:, None] // SSD_P == jnp.arange(128)[None, :]).astype(jnp.bfloat16)
    e2 = (jnp.arange(HPG * CHUNK)[:, None] // CHUNK == jnp.arange(128)[None, :]).astype(jnp.bfloat16)
    return e1, e2


def _ssd_fwd(xc, dt, cs, cst, ex, drow, di, *, name):
    s = xc.shape[0]
    nc = s // CHUNK
    anti = di == 1
    sp = _ssd_specs(nc, anti)
    trow = 0 if anti else CHUNK - 1

    def body(x_ref, b_ref, c_ref, dt_ref, cs_ref, cst_ref, ex_ref, d_ref, y_ref, hp_ref, h_ref):
        @pl.when(pl.program_id(1) == 0)
        def _():
            h_ref[...] = jnp.zeros_like(h_ref)

        mask = _ssd_mask(anti)[2]
        dtv, csv = dt_ref[...], cs_ref[...]
        for gi in range(GPS):
            cols = slice(gi * GW, (gi + 1) * GW)
            ncols = slice(gi * SSD_N, (gi + 1) * SSD_N)
            ex = ex_ref[gi]
            xb = x_ref[:, cols].astype(F32)
            bm, cm = b_ref[:, ncols], c_ref[:, ncols]
            csr = cst_ref[gi, 0]
            dtf = _expand(dtv, ex)
            csf = _expand(csv, ex)
            tl = csf[trow:trow + 1, :]
            h = h_ref[gi]
            hp_ref[gi, 0] = h
            g = _dot(cm, bm, NT)
            xs = xb * dtf
            xsm = xs.astype(MMD)
            base = jnp.exp(csf) * _dot(cm, h.astype(MMD)) + d_ref[:, cols] * xb
            for r in range(HPG):
                sl = slice(r * SSD_P, (r + 1) * SSD_P)
                lm = jnp.exp(jnp.where(mask, csf[:, r * SSD_P:r * SSD_P + 1] - csr[r:r + 1, :], NEG))
                y_ref[:, gi * GW + r * SSD_P:gi * GW + (r + 1) * SSD_P] = _dot((g * lm).astype(MMD), xsm[:, sl]) + base[:, sl]
            xd = (xs * jnp.exp(tl - csf)).astype(MMD)
            h_ref[gi] = h * jnp.exp(tl) + _dot(bm, xd, TN)

    nb = SSD_GROUPS // GPS
    return pl.pallas_call(
        body, name=name, grid=(nb, nc),
        in_specs=[sp["x"], sp["b"], sp["c"], sp["lanes"], sp["lanes"], sp["rowt"],
                  pl.BlockSpec((GPS, 128, GW), lambda g, c: (di * nb + g, 0, 0)), sp["drow"]],
        out_specs=[sp["y"], sp["h"]],
        out_shape=[jax.ShapeDtypeStruct((s, 2048), F32), jax.ShapeDtypeStruct((SSD_GROUPS, nc, SSD_N, GW), F32)],
        scratch_shapes=[pltpu.VMEM((GPS, SSD_N, GW), F32)],
        compiler_params=_cp(("arbitrary", "arbitrary")),
    )(xc, xc, xc, dt, cs, cst, ex, drow)


def _ssd_bwd(xc, dt, cs, dt4, cst, ex, drow, arow4, dy, hprev, di, *, name):
    s = xc.shape[0]
    nc = s // CHUNK
    anti = di == 1
    sp = _ssd_specs(nc, not anti)
    trow = 0 if anti else CHUNK - 1
    e1, e2 = _headsum_mats()

    def body(x_ref, b_ref, c_ref, dt_ref, cs_ref, dt4_ref, cst_ref, ex_ref, d_ref, a_ref, dy_ref, hp_ref, tri_ref,
             e1_ref, e2_ref, dx_ref, db_ref, dc_ref, ddt_ref, da_ref, dh_ref, w_ref, dxs_ref):
        @pl.when(pl.program_id(1) == 0)
        def _():
            dh_ref[...] = jnp.zeros_like(dh_ref)
            da_ref[...] = jnp.zeros_like(da_ref)

        e1v = e1_ref[...]
        ii, _, mask = _ssd_mask(anti)
        dtv, csv = dt_ref[...], cs_ref[...]
        for gi in range(GPS):
            cols = slice(gi * GW, (gi + 1) * GW)
            ncols = slice(gi * SSD_N, (gi + 1) * SSD_N)
            ex = ex_ref[gi]
            xb = x_ref[:, cols].astype(F32)
            bm, cm = b_ref[:, ncols], c_ref[:, ncols]
            csr = cst_ref[gi, 0]
            dyb = dy_ref[:, cols]
            dym = dyb.astype(MMD)
            hp = hp_ref[gi, 0]
            hpm = hp.astype(MMD)
            dh = dh_ref[gi]
            dhm = dh.astype(MMD)
            dtf = _expand(dtv, ex)
            csf = _expand(csv, ex)
            tl = csf[trow:trow + 1, :]
            e = jnp.exp(csf)
            dec = jnp.exp(tl - csf)
            et = jnp.exp(tl)
            xs = xb * dtf
            xsm = xs.astype(MMD)
            g = _dot(cm, bm, NT)
            z = _dot(cm, hpm)
            bdh = _dot(bm, dhm)
            dg = jnp.zeros((CHUNK, CHUNK), F32)
            wcols = jnp.zeros((CHUNK, CHUNK), F32)
            for r in range(HPG):
                sl = slice(r * SSD_P, (r + 1) * SSD_P)
                lm = jnp.exp(jnp.where(mask, csf[:, r * SSD_P:r * SSD_P + 1] - csr[r:r + 1, :], NEG))
                mm = g * lm
                dm = _dot(dym[:, sl], xsm[:, sl], NT)
                w = dm * mm
                w_ref[gi, :, r * CHUNK:(r + 1) * CHUNK] = w
                wcols = jnp.where(ii == r, jnp.sum(w, axis=0, keepdims=True), wcols)
                dg = dg + dm * lm
                dxs_ref[gi, :, sl] = _dot(mm.astype(MMD), dym[:, sl], TN)
            dxs = dxs_ref[gi] + dec * bdh
            dx_ref[:, cols] = dxs * dtf + d_ref[:, cols] * dyb
            tb = xs * bdh * dec
            d_tot = jnp.sum(tb, axis=0, keepdims=True) + et * jnp.sum(dh * hp, axis=0, keepdims=True)
            d_tot = _headsum(jnp.broadcast_to(d_tot, (8, GW)), e1v)[0:1]
            dcs = (_headsum(dyb * (e * z) - tb, e1v) + _headsum(w_ref[gi], e2_ref[...]) - wcols.T
                   + jnp.where(ii == trow, d_tot, 0.0))
            da = _dot_hi(tri_ref[...], dcs)
            ddt_ref[gi] = (da * a_ref[gi] + _headsum(dxs * xb, e1v))[:, 0:HPG]
            da_ref[gi] += jnp.sum(da[:, 0:HPG] * dt4_ref[gi], axis=0, keepdims=True)
            dgm = dg.astype(MMD)
            dz = (e * dyb).astype(MMD)
            dc_ref[:, ncols] = _dot(dgm, bm) + _dot(dz, hpm, NT)
            db_ref[:, ncols] = _dot(dgm, cm, TN) + _dot((xs * dec).astype(MMD), dhm, NT)
            dh_ref[gi] = dh * et + _dot(cm, dz, TN)

    nb = SSD_GROUPS // GPS
    const = lambda shape: pl.BlockSpec(shape, lambda g, c: (0,) * len(shape))
    return pl.pallas_call(
        body, name=name, grid=(nb, nc),
        in_specs=[sp["x"], sp["b"], sp["c"], sp["lanes"], sp["lanes"], sp["col"], sp["rowt"],
                  pl.BlockSpec((GPS, 128, GW), lambda g, c: (di * nb + g, 0, 0)), sp["drow"],
                  pl.BlockSpec((GPS, 1, 128), lambda g, c: (g, 0, 0)), sp["y"], sp["h"],
                  const((CHUNK, CHUNK)), const((GW, 128)), const((HPG * CHUNK, 128))],
        out_specs=[sp["y"], sp["n"], sp["n"], sp["col"], pl.BlockSpec((GPS, 1, HPG), lambda g, c: (g, 0, 0))],
        out_shape=[jax.ShapeDtypeStruct((s, 2048), F32), jax.ShapeDtypeStruct((s, SSD_GROUPS * SSD_N), F32),
                   jax.ShapeDtypeStruct((s, SSD_GROUPS * SSD_N), F32), jax.ShapeDtypeStruct((SSD_GROUPS, s, HPG), F32),
                   jax.ShapeDtypeStruct((SSD_GROUPS, 1, HPG), F32)],
        scratch_shapes=[pltpu.VMEM((GPS, SSD_N, GW), F32), pltpu.VMEM((GPS, CHUNK, HPG * CHUNK), F32),
                        pltpu.VMEM((GPS, CHUNK, GW), F32)],
        compiler_params=_cp(("arbitrary", "arbitrary")),
    )(xc, xc, xc, dt, cs, dt4, cst, ex, drow, arow4, dy, hprev, _tri(anti), e1, e2)


def _gnorm_fwd(ya, yb, proj, w, *, name):
    s = ya.shape[0]
    tm = _tile(s, 256)

    def body(a_ref, b_ref, z_ref, w_ref, o_ref):
        zv = z_ref[...].astype(F32)
        t = (a_ref[...] + b_ref[...]) * (zv * _sigmoid(zv))
        r = lax.rsqrt(jnp.mean(t * t, axis=-1, keepdims=True) + EPS)
        o_ref[...] = ((t * r) * w_ref[...]).astype(o_ref.dtype)

    big = pl.BlockSpec((tm, 2048), lambda i: (i, 0))
    row = pl.BlockSpec((1, 2048), lambda i: (0, 0))
    return pl.pallas_call(
        body, name=name, grid=(s // tm,), in_specs=[big, big, big, row], out_specs=big,
        out_shape=jax.ShapeDtypeStruct((s, 2048), MMD), compiler_params=_cp(("arbitrary",)),
    )(ya, yb, proj, w)


def _gnorm_bwd(dout, ya, yb, proj, w, *, name):
    s = ya.shape[0]
    tm = _tile(s, 256)

    def body(do_ref, a_ref, b_ref, z_ref, w_ref, dy_ref, dz_ref, dw_ref):
        zv = z_ref[...].astype(F32)
        sg = _sigmoid(zv)
        sz = zv * sg
        y = a_ref[...] + b_ref[...]
        t = y * sz
        r = lax.rsqrt(jnp.mean(t * t, axis=-1, keepdims=True) + EPS)
        nv = t * r
        dov = do_ref[...].astype(F32)
        _acc_rows(dw_ref, jnp.sum(dov * nv, axis=0, keepdims=True), pl.program_id(0) == 0)
        dn = dov * w_ref[...]
        dt_ = r * (dn - nv * jnp.mean(dn * nv, axis=-1, keepdims=True))
        dy_ref[...] = dt_ * sz
        dz_ref[...] = (dt_ * y * (sg * (1.0 + zv * (1.0 - sg)))).astype(dz_ref.dtype)

    big = pl.BlockSpec((tm, 2048), lambda i: (i, 0))
    row = pl.BlockSpec((1, 2048), lambda i: (0, 0))
    return pl.pallas_call(
        body, name=name, grid=(s // tm,), in_specs=[big, big, big, big, row], out_specs=[big, big, row],
        out_shape=[jax.ShapeDtypeStruct((s, 2048), F32), jax.ShapeDtypeStruct((s, 2048), MMD),
                   jax.ShapeDtypeStruct((1, 2048), F32)],
        compiler_params=_cp(("arbitrary",)),
    )(dout, ya, yb, proj, w)


def _colsum_prod(a, b, *, name):
    s, n = a.shape
    tm = _tile(s, 256)

    def body(a_ref, b_ref, o_ref):
        _acc_rows(o_ref, jnp.sum(a_ref[...].astype(F32) * b_ref[...].astype(F32), axis=0, keepdims=True),
                  pl.program_id(0) == 0)

    big = pl.BlockSpec((tm, n), lambda i: (i, 0))
    return pl.pallas_call(
        body, name=name, grid=(s // tm,), in_specs=[big, big], out_specs=pl.BlockSpec((1, n), lambda i: (0, 0)),
        out_shape=jax.ShapeDtypeStruct((1, n), F32), compiler_params=_cp(("arbitrary",)),
    )(a, b)


def _heads(a, n):
    return a.reshape(a.shape[0], n, HEAD_DIM).transpose(1, 0, 2)


def _unheads(a):
    return a.transpose(1, 0, 2).reshape(a.shape[1], a.shape[0] * HEAD_DIM)


def _per_group(a):
    return a.reshape(a.shape[0], SSD_GROUPS, HPG).transpose(1, 0, 2)


def _per_group_t(a):
    s = a.shape[0]
    return a.reshape(s // CHUNK, CHUNK, SSD_GROUPS, HPG).transpose(2, 0, 3, 1)


def _local_step(x, target, mod, wts, small, late_weights=None, late_grads=None, in_grad=None):
    s, d = x.shape
    shift1, scale1, gate1, shift2, scale2, gate2 = [mod[i:i + 1] for i in range(6)]

    h1 = _ln_mod(x, small["norm1_w"], scale1, shift1, name="ln1")
    proj = _mm(h1, wts["w_in_p"], name="in_proj", outs=[MMD], tm=512, tn=2944, b_outer=True)
    dt_raw = _mm(h1, wts["w_dt"], name="dt_proj", outs=[F32], tm=512, tn=128)

    qk_w = jnp.concatenate([jnp.tile(small["q_norm_w"], (1, N_Q_HEADS)), jnp.tile(small["k_norm_w"], (1, N_KV_HEADS))], axis=1)
    qk_sc = jnp.concatenate([jnp.full((1, N_Q_HEADS * HEAD_DIM), HEAD_DIM ** -0.5, F32),
                             jnp.ones((1, N_KV_HEADS * HEAD_DIM), F32)], axis=1)
    qk_sc2 = jnp.concatenate([jnp.full((1, N_Q_HEADS * HEAD_DIM), HEAD_DIM ** -0.5 * LOG2E, F32),
                              jnp.ones((1, N_KV_HEADS * HEAD_DIM), F32)], axis=1)
    tabs = _rope_tables(s)
    qk, qkt = _qk_fwd(proj, qk_w, qk_sc2, tabs, name="qk_fwd")
    qkt = qkt.reshape(N_Q_HEADS + N_KV_HEADS, HEAD_DIM, s)
    k_h = _heads(qk[:, N_Q_HEADS * HEAD_DIM:], N_KV_HEADS)
    v_sd = proj[:, V0:V0 + N_KV_HEADS * HEAD_DIM]
    v_h = _heads(v_sd, N_KV_HEADS)
    vta = jnp.concatenate([v_sd.T.reshape(N_KV_HEADS, HEAD_DIM, s), jnp.ones((N_KV_HEADS, V_AUG - HEAD_DIM, s), MMD)], axis=1)
    ot, lse = _flash_fwd(qkt, vta, name="flash_fwd")
    ot2 = ot.reshape(N_Q_HEADS * HEAD_DIM, s)
    if late_weights is not None:
        wts = {**wts, **late_weights(ot)}

    w8 = jnp.pad(small["conv_w"], ((0, 8 - D_CONV), (0, 0)))
    xc = _conv_fwd(proj, w8, small["conv_b"], name="conv_fwd")
    a_neg = -jnp.exp(small["A_log"])
    arow = jnp.pad(a_neg.reshape(1, 2 * SSD_HEADS), ((0, 0), (0, 128 - 2 * SSD_HEADS)))
    bias_row = jnp.pad(small["dt_bias"].reshape(1, 2 * SSD_HEADS), ((0, 0), (0, 128 - 2 * SSD_HEADS)))
    dt, cs = _dt_fwd(dt_raw, bias_row, arow, name="dt_fwd")
    drow = jnp.repeat(small["ssd_D"], SSD_P, axis=1)
    dirs = []
    for di in range(2):
        cols = slice(di * SSD_HEADS, (di + 1) * SSD_HEADS)
        dirs.append(dict(
            dt4=_per_group(dt[:, cols]), cst=_per_group_t(cs[:, cols]),
            drow=drow if di == 0 else jnp.zeros_like(drow),
            arow4=jnp.pad(a_neg[di].reshape(SSD_GROUPS, 1, HPG), ((0, 0), (0, 0), (0, 128 - HPG)))))
    ex = _expand_mats()
    ys = []
    for di, dd in enumerate(dirs):
        y, dd["hprev"] = _ssd_fwd(xc, dt, cs, dd["cst"], ex, dd["drow"], di, name=f"ssd_fwd{di}")
        ys.append(y)
    ssdn = _gnorm_fwd(ys[0], ys[1], proj, small["ssd_norm_w"], name="gnorm_fwd")

    a_o = _mm(ot2, wts["w_attn_out"], name="attn_out", outs=[MMD], ta=True, tm=512, tn=1024)

    def merge_epi(acc, ao, ga, gs):
        return (_sigmoid(ga.astype(F32)) * ao.astype(F32) + _sigmoid(gs.astype(F32)) * acc, acc)

    merged, b_o = _mm(ssdn, wts["w_ssd_out"], name="ssd_out", outs=[MMD, MMD], tm=512, tn=512,
                      extras=[(a_o, "tile", 0), (proj, "tile", GA0), (proj, "tile", GS0)], epi=merge_epi)

    def res_epi(acc, res, gate):
        return (res + gate * acc, acc)

    x1, mo = _mm(merged, wts["w_o"], name="w_o", outs=[F32, MMD], tm=512, tn=512,
                 extras=[(x, "tile", 0), (gate1, "row", 0)], epi=res_epi)
    h2 = _ln_mod(x1, small["norm2_w"], scale2, shift2, name="ln2")

    def relu2_epi(acc):
        rl = jnp.maximum(acc, 0.0)
        return (rl * rl, rl)

    act, rl = _mm(h2, wts["w_mlp1"], name="mlp1", outs=[MMD, MMD], tm=512, tn=1024, epi=relu2_epi, b_outer=True)

    def loss_epi(acc, res, gate, tgt):
        return ((res + gate * acc - tgt) * (1.0 / d), acc)

    dy, ffo = _mm(act, wts["w_mlp2"], name="mlp2", outs=[F32, MMD], tm=512, tn=1024, vmem=VMEM_BIG,
                  extras=[(x1, "tile", 0), (gate2, "row", 0), (target, "tile", 0)], epi=loss_epi)
    loss = _sumsq(dy, name="loss") * (0.5 * d)

    gw = {}
    gs_ = {}
    dffo, dgate2 = _gate_bwd(dy, ffo, gate2, name="gate2_bwd")
    dpre = _mm(dffo, wts["w_mlp2"], name="mlp2_dx", outs=[MMD], nt=True, tm=512, tn=1024,
               extras=[(rl, "tile", 0)], epi=lambda acc, r: (acc * (2.0 * r.astype(F32)),))
    gw["w_mlp2"] = _mm_tn(act, dffo, name="mlp2_dw")
    dh2 = _mm(dpre, wts["w_mlp1"], name="mlp1_dx", outs=[F32], nt=True, tm=512, tn=1024)
    gw["w_mlp1"] = _mm_tn(h2, dpre, name="mlp1_dw")
    dx1, dshift2, dscale2, gs_["norm2_w"] = _ln_mod_bwd(dh2, x1, small["norm2_w"], scale2, dy, name="ln2_bwd")
    dmo, dgate1 = _gate_bwd(dx1, mo, gate1, name="gate1_bwd")

    def merge_bwd_epi(acc, ao, bo, ga, gs):
        sa, ss = _sigmoid(ga.astype(F32)), _sigmoid(gs.astype(F32))
        return (acc * sa, acc * ss, acc * ao.astype(F32) * sa * (1.0 - sa), acc * bo.astype(F32) * ss * (1.0 - ss))

    da_o, db_o, dga, dgs = _mm(dmo, wts["w_o"], name="w_o_dx", outs=[MMD] * 4, nt=True, tm=512, tn=512,
                               extras=[(a_o, "tile", 0), (b_o, "tile", 0), (proj, "tile", GA0), (proj, "tile", GS0)],
                               epi=merge_bwd_epi)
    gw["w_o"] = _mm_tn(merged, dmo, name="w_o_dw")
    dot = _mm(wts["w_attn_out"], da_o, name="attn_out_dx", outs=[MMD], nt=True, tm=512, tn=512)
    gw["w_attn_out"] = _mm(ot2, da_o, name="attn_out_dw", outs=[F32], tm=256, tn=512, vmem=VMEM_BIG)
    dssdn = _mm(db_o, wts["w_ssd_out"], name="ssd_out_dx", outs=[MMD], nt=True, tm=512, tn=512)
    gw["w_ssd_out"] = _mm_tn(ssdn, db_o, name="ssd_out_dw")

    norm_w = small["ssd_norm_w"] if late_grads is None else small["ssd_norm_w"] + late_grads(gw)
    dyssd, dz, gs_["ssd_norm_w"] = _gnorm_bwd(dssdn, ys[0], ys[1], proj, norm_w, name="gnorm_bwd")
    gs_["ssd_D"] = _colsum_prod(dyssd, xc[:, 0:2048], name="ssd_d_grad").reshape(SSD_HEADS, SSD_P).sum(axis=1).reshape(1, SSD_HEADS)
    dxc, ddts, das = [], [], []
    for di, dd in enumerate(dirs):
        dxs, dbm, dcm, ddt4, da4 = _ssd_bwd(xc, dt, cs, dd["dt4"], dd["cst"], ex, dd["drow"], dd["arow4"],
                                            dyssd, dd["hprev"], di, name=f"ssd_bwd{di}")
        dxc.append((dxs, dbm, dcm))
        ddts.append(ddt4.transpose(1, 0, 2).reshape(s, SSD_HEADS))
        das.append(da4.reshape(1, SSD_HEADS))
    conv_parts, col0 = [], 0
    for part, (ga, gb) in enumerate(zip(*dxc)):
        conv_parts.append(_conv_bwd(proj, col0, ga, gb, w8, small["conv_b"], name=f"conv_bwd{part}"))
        col0 += ga.shape[1]
    dxbc, dw8, gs_["conv_b"] = [jnp.concatenate(t, axis=1) for t in zip(*conv_parts)]
    gs_["conv_w"] = dw8[0:D_CONV]
    gs_["A_log"] = jnp.concatenate(das, axis=0) * a_neg
    ddt = jnp.pad(jnp.concatenate(ddts, axis=1), ((0, 0), (0, 128 - 2 * SSD_HEADS)))
    ddt_raw, dbias = _dt_bwd(ddt, dt_raw, bias_row, name="dt_bwd")
    gs_["dt_bias"] = dbias[:, 0:2 * SSD_HEADS].reshape(2, SSD_HEADS)

    dqt, dk_h, dv_h = _flash_bwd(qkt, k_h, v_h, dot.reshape(N_Q_HEADS, HEAD_DIM, s), ot, lse, name="flash_bwd")
    dqkt = jnp.concatenate([dqt.reshape(N_Q_HEADS * HEAD_DIM, s),
                            dk_h.transpose(0, 2, 1).reshape(N_KV_HEADS * HEAD_DIM, s)], axis=0)
    dqk_u, dqk_w = _qk_bwd(dqkt, proj, qk_w, qk_sc, tabs, name="qk_bwd")
    gs_["q_norm_w"] = dqk_w[:, 0:N_Q_HEADS * HEAD_DIM].reshape(N_Q_HEADS, HEAD_DIM).sum(axis=0, keepdims=True)
    gs_["k_norm_w"] = dqk_w[:, N_Q_HEADS * HEAD_DIM:].reshape(N_KV_HEADS, HEAD_DIM).sum(axis=0, keepdims=True)
    dv = _unheads(dv_h).astype(MMD)

    dproj = jnp.concatenate([dz, dga, dgs, dxbc, dqk_u, dv, ddt_raw], axis=1)
    gw["w_in_p"] = _mm_tn(h1, dproj, name="in_proj_dw", tk=512, tn=2944, tmm=2048, vmem=VMEM_BIG)
    zero_row = jnp.zeros((1, d), F32) if in_grad is None else jnp.zeros((1, d), F32) + in_grad(gw["w_in_p"])[0:1, 0:1]
    dh1 = _mm(dproj, wts["w_in_p"], name="in_proj_dx", outs=[F32], nt=True, tm=256, tn=1024, vmem=VMEM_BIG,
              extras=[(zero_row, "row", 0)], epi=lambda acc, r: (acc + r,))
    grad_x, dshift1, dscale1, gs_["norm1_w"] = _ln_mod_bwd(dh1, x, small["norm1_w"], scale1, dx1, name="ln1_bwd")
    dmod = jnp.concatenate([dshift1, dscale1, dgate1, dshift2, dscale2, dgate2], axis=0)
    return loss, grad_x, dmod, gw, gs_


N_DEV = 8
N_CHIP = 4
ANY = pl.BlockSpec(memory_space=pl.ANY)


def _place():
    return lax.axis_index("x"), lax.axis_index("y"), lax.axis_index("c")


def _allgather8(v, *, name):
    m_per, n = v.shape

    def body(x_ref, out_ref, send_sems, recv_sems, local_sem):
        x, y, c = _place()
        me, sibling = (x, y, c), (x, y, 1 - c)
        chips = [(1 - x, y), (x, 1 - y), (1 - x, 1 - y)]

        def rows(px, py, pc):
            return out_ref.at[pl.ds((4 * px + 2 * py + pc) * m_per, m_per), :]

        def copy(k, block, to, src=None):
            return pltpu.make_async_remote_copy(
                src_ref=rows(*block) if src is None else src, dst_ref=rows(*block),
                send_sem=send_sems.at[k], recv_sem=recv_sems.at[k], device_id=to, device_id_type=MESH)

        mine = pltpu.make_async_copy(x_ref, rows(*me), local_sem)
        mine.start()
        first = [copy(0, me, sibling, src=x_ref)]
        first += [copy(1 + j, me, (*chip, c), src=x_ref) for j, chip in enumerate(chips)]
        for cp in first:
            cp.start()
        passed = [copy(4 + j, (*chip, c), sibling) for j, chip in enumerate(chips)]
        for j, chip in enumerate(chips):
            copy(1 + j, (*chip, c), me).wait_recv()
            passed[j].start()
        copy(0, sibling, me).wait_recv()
        for j, chip in enumerate(chips):
            copy(4 + j, (*chip, 1 - c), me).wait_recv()
        for cp in first + passed:
            cp.wait_send()
        mine.wait()

    return pl.pallas_call(
        body, name=name, out_shape=jax.ShapeDtypeStruct((N_DEV * m_per, n), v.dtype),
        in_specs=[pl.BlockSpec(memory_space=pltpu.VMEM)], out_specs=pl.BlockSpec(memory_space=pltpu.VMEM),
        scratch_shapes=[pltpu.SemaphoreType.DMA((7,)), pltpu.SemaphoreType.DMA((7,)), pltpu.SemaphoreType.DMA],
    )(v)


HBM = pl.BlockSpec(memory_space=pltpu.HBM)
SEM = pl.BlockSpec(memory_space=pltpu.SEMAPHORE)


def _chips_copies(x_ref, land_ref, sems, scatter):
    x, y, c = _place()
    k = 2 * x + y
    chips = [(1 - x, y), (x, 1 - y), (1 - x, 1 - y)]
    ids = [2 * cx + cy for cx, cy in chips]

    def copy(j, slot):
        return pltpu.make_async_remote_copy(
            src_ref=x_ref.at[ids[j]] if scatter else x_ref, dst_ref=land_ref.at[slot], send_sem=sems[j],
            recv_sem=sems[3 + j], device_id=(*chips[j], c), device_id_type=MESH)

    return [copy(j, k) for j in range(3)], [copy(j, ids[j]) for j in range(3)]


def _chips_start(src, scatter, *, name):
    shape = src.shape if scatter else (N_CHIP,) + tuple(src.shape)

    def body(x_ref, land_ref, *rest):
        sems, token = rest[0:6], rest[8]
        for cp in _chips_copies(x_ref, land_ref, sems, scatter)[0]:
            cp.start()
        token[...] = jnp.zeros_like(token)

    out = pl.pallas_call(
        body, name=name,
        out_shape=(pltpu.SemaphoreType.DMA(()),) * 6 + (pltpu.HBM(src.shape, src.dtype), pltpu.HBM(shape, src.dtype),
                                                       jax.ShapeDtypeStruct((8, 128), F32)),
        in_specs=(HBM, HBM), out_specs=(SEM,) * 6 + (HBM, HBM, pl.BlockSpec(memory_space=pltpu.VMEM)),
        input_output_aliases={0: 6, 1: 7},
        compiler_params=pltpu.CompilerParams(has_side_effects=pltpu.SideEffectType.DATAFLOW_SIDE_EFFECTING),
    )(pltpu.with_memory_space_constraint(src, pltpu.HBM),
      pltpu.with_memory_space_constraint(lax.empty(shape, src.dtype), pltpu.HBM))
    return out[0:6], out[6], out[7], out[8]


def _chips_wait(sems, src, land, after, scatter, *, name):
    def body(x_ref, land_ref, *rest):
        sems_ = rest[0:6]
        for cp in _chips_copies(x_ref, land_ref, sems_, scatter)[1]:
            cp.wait_send()
            cp.wait_recv()

    return pl.pallas_call(
        body, name=name, out_shape=(pltpu.HBM(src.shape, src.dtype), pltpu.HBM(land.shape, land.dtype)),
        in_specs=(HBM, HBM) + (SEM,) * 6 + (ANY,), out_specs=(HBM, HBM), input_output_aliases={0: 0, 1: 1},
        compiler_params=pltpu.CompilerParams(has_side_effects=pltpu.SideEffectType.DATAFLOW_SIDE_EFFECTING),
    )(src, land, *sems, after)


def _row_tile(r, pref=512):
    return max(t for t in range(16, pref + 1, 16) if r % t == 0)


def _gather_weights(src, *, name):
    r = src.shape[0]
    hr = r // 2
    assert r == 2 * hr and hr % 16 == 0

    def body(x_ref, out_ref, send_sems, recv_sems):
        x, y, c = _place()
        k = 2 * x + y
        chips = [(1 - x, y), (x, 1 - y), (1 - x, 1 - y)]
        ids = [2 * cx + cy for cx, cy in chips]
        mine_rows = pl.ds(pl.multiple_of(c * hr, 16), hr)
        other_rows = pl.ds(pl.multiple_of((1 - c) * hr, 16), hr)

        def copy(sem, src_ref, slot, rows, to):
            return pltpu.make_async_remote_copy(
                src_ref=src_ref, dst_ref=out_ref.at[slot, rows], send_sem=send_sems.at[sem], recv_sem=recv_sems.at[sem],
                device_id=to, device_id_type=MESH)

        sends = [copy(j, x_ref.at[mine_rows], k, mine_rows, (cx, cy, c)) for j, (cx, cy) in enumerate(chips)]
        for cp in sends:
            cp.start()
        passed = [copy(3 + j, out_ref.at[ids[j], mine_rows], ids[j], mine_rows, (x, y, 1 - c)) for j in range(3)]
        for j, (cx, cy) in enumerate(chips):
            copy(j, x_ref.at[mine_rows], ids[j], mine_rows, (cx, cy, c)).wait_recv()
            passed[j].start()
        for j in range(3):
            copy(3 + j, out_ref.at[ids[j], other_rows], ids[j], other_rows, (x, y, 1 - c)).wait_recv()
        for cp in sends + passed:
            cp.wait_send()

    return pl.pallas_call(
        body, name=name, out_shape=jax.ShapeDtypeStruct((N_CHIP,) + tuple(src.shape), src.dtype),
        in_specs=[ANY], out_specs=ANY,
        scratch_shapes=[pltpu.SemaphoreType.DMA((6,)), pltpu.SemaphoreType.DMA((6,))],
    )(src)


def _pair_swap(a, *, name):
    n, r, cols = a.shape
    hr = r // 2

    def body(x_ref, out_ref, send_sem, recv_sem):
        x, y, c = _place()
        other_rows = pl.ds(pl.multiple_of((1 - c) * hr, 16), hr)
        cp = pltpu.make_async_remote_copy(src_ref=x_ref.at[:, other_rows], dst_ref=out_ref, send_sem=send_sem,
                                          recv_sem=recv_sem, device_id=(x, y, 1 - c), device_id_type=MESH)
        cp.start()
        cp.wait()

    return pl.pallas_call(
        body, name=name, out_shape=jax.ShapeDtypeStruct((n, hr, cols), a.dtype), in_specs=[ANY], out_specs=ANY,
        scratch_shapes=[pltpu.SemaphoreType.DMA, pltpu.SemaphoreType.DMA],
    )(a)


def _sibling_copy(a, *, name):
    def body(x_ref, out_ref, send_sem, recv_sem):
        x, y, c = _place()
        cp = pltpu.make_async_remote_copy(src_ref=x_ref, dst_ref=out_ref, send_sem=send_sem, recv_sem=recv_sem,
                                          device_id=(x, y, 1 - c), device_id_type=MESH)
        cp.start()
        cp.wait()

    return pl.pallas_call(
        body, name=name, out_shape=jax.ShapeDtypeStruct(a.shape, a.dtype), in_specs=[ANY], out_specs=ANY,
        scratch_shapes=[pltpu.SemaphoreType.DMA, pltpu.SemaphoreType.DMA],
    )(a)


def _sum_slots(a, *, name):
    _, r, c = a.shape
    tr = _row_tile(r, 256)

    def body(a_ref, o_ref):
        acc = a_ref[0].astype(F32)
        for j in range(1, N_CHIP):
            acc = acc + a_ref[j].astype(F32)
        o_ref[...] = acc

    return pl.pallas_call(
        body, name=name, grid=(r // tr,), in_specs=[pl.BlockSpec((N_CHIP, tr, c), lambda i: (0, i, 0))],
        out_specs=pl.BlockSpec((tr, c), lambda i: (i, 0)), out_shape=jax.ShapeDtypeStruct((r, c), F32),
        compiler_params=_cp(("arbitrary",)),
    )(a)


def _add2(a, b, *, name):
    r, c = a.shape
    tr = _row_tile(r)

    def body(a_ref, b_ref, o_ref):
        o_ref[...] = (a_ref[...].astype(F32) + b_ref[...].astype(F32)).astype(o_ref.dtype)

    spec = pl.BlockSpec((tr, c), lambda i: (i, 0))
    return pl.pallas_call(
        body, name=name, grid=(r // tr,), in_specs=[spec, spec], out_specs=spec,
        out_shape=jax.ShapeDtypeStruct((r, c), a.dtype), compiler_params=_cp(("arbitrary",)),
    )(a, b)


BIG = ("w_in", "w_mlp1", "w_attn_out", "w_ssd_out", "w_o", "w_mlp2")
COL_SHARDED = ("w_mlp1", "w_in")
ROW_SHARDED = ("w_attn_out", "w_ssd_out", "w_o", "w_mlp2")
LATE = ROW_SHARDED + ("w_mlp1",)
SMALL = ("b_ada", "norm1_w", "norm2_w", "q_norm_w", "k_norm_w", "conv_b", "A_log", "dt_bias", "ssd_D", "ssd_norm_w")
NAMES = ("w_ada", "b_ada", "norm1_w", "norm2_w", "w_in", "q_norm_w", "k_norm_w", "conv_w", "conv_b", "A_log", "dt_bias",
         "ssd_D", "ssd_norm_w", "w_attn_out", "w_ssd_out", "w_o", "w_mlp1", "w_mlp2")
W_IN_COLS = 8768


def _permute_in(w):
    return jnp.concatenate([w[:, 4608:6656], w[:, 6720:8768], w[:, 1536:4608], w[:, 0:1536], w[:, 6656:6720],
                            jnp.zeros((w.shape[0], PW - W_IN_COLS), w.dtype)], axis=1)


def _unpermute_in(wp):
    return jnp.concatenate([wp[:, Q0:DT0], wp[:, XS0:Q0], wp[:, Z0:GA0], wp[:, DT0:DT0 + 64], wp[:, GA0:XS0]], axis=1)


def _pad_to(v, n):
    return jnp.pad(v, (0, n - v.shape[0]))


def _step(w, m, v, loss_target):
    xi, yi, ci = _place()
    chip = 2 * xi + yi
    dev = 4 * xi + 2 * yi + ci
    x, tgt = w["x"], loss_target
    d = x.shape[1]

    cw = w["conv_w"].shape[1]
    v0 = _pad_to(jnp.concatenate([w["c"].reshape(-1), w["conv_w"].reshape(-1)]), 5120).reshape(8, 640)
    g0 = _allgather8(v0, name="ag_cond").reshape(N_DEV, 5120)
    c_all = g0[:, 0:d]
    conv_w = jnp.concatenate([g0[2 * k, d:d + D_CONV * cw].reshape(D_CONV, cw) for k in range(N_CHIP)], axis=1)
    sc = _silu_cast(c_all, name="silu_c")
    modp = _mm(sc, w["w_ada"].astype(MMD), name="ada_fwd", outs=[F32], tm=8, tn=512)
    g1 = _allgather8(modp, name="ag_mod").reshape(N_DEV, N_DEV, modp.shape[1])
    mod_all = jnp.concatenate([g1[2 * k] for k in range(N_CHIP)], axis=1)
    mod = (lax.dynamic_slice_in_dim(mod_all, dev, 1, axis=0) + w["b_ada"]).reshape(6, d)

    mine, mod = lax.optimization_barrier((w["w_in"].astype(MMD), mod))
    gath = lax.dynamic_update_slice_in_dim(_gather_weights(mine, name="ag_w_in"), mine[None], chip, axis=0)
    late_mine = jnp.concatenate([w[n].astype(MMD) for n in LATE], axis=0)
    late_mine, gath = lax.optimization_barrier((late_mine, gath))
    ag_sems, ag_src, ag_land, ag_token = _chips_start(late_mine, False, name="ag_late_start")
    mod = mod + ag_token[0:1, 0:1]
    w_in = jnp.concatenate([gath[k] for k in range(N_CHIP)], axis=1)
    wts = {"w_in_p": _permute_in(w_in), "w_dt": jnp.pad(w_in[:, 6656:6720], ((0, 0), (0, 64)))}
    small = {n: w[n] for n in SMALL if n != "b_ada"}
    small["conv_w"] = conv_w

    def own_slot(land, src):
        return lax.dynamic_update_slice_in_dim(land, src, chip, axis=0)

    def late_weights(after):
        src, land = _chips_wait(ag_sems, ag_src, ag_land, after, False, name="ag_late_wait")
        land = own_slot(land, src[None])
        out, o = {}, 0
        for n in LATE:
            rows = w[n].shape[0]
            part = land[:, o:o + rows]
            out[n] = (jnp.concatenate([part[k] for k in range(N_CHIP)], axis=1) if n in COL_SHARDED
                      else part.reshape(N_CHIP * rows, w[n].shape[1]))
            o += rows
        return out

    def pair_sums(slots, tag):
        _, rows, cols = slots.shape
        hr = rows // 2
        theirs = _pair_swap(slots, name="rs_pair_" + tag)
        ours = lax.dynamic_slice_in_dim(slots, ci * hr, hr, axis=1)
        pair = _add2(ours.reshape(N_CHIP * hr, cols), theirs.reshape(N_CHIP * hr, cols), name="rs_pair_sum_" + tag)
        return pair.reshape(N_CHIP, hr, cols)

    def finish(recv, pair, tag):
        recv = own_slot(recv, lax.dynamic_slice_in_dim(pair, chip, 1, axis=0))
        half = _sum_slots(recv, name="rs_sum_" + tag)
        other = _sibling_copy(half, name="rs_sibling_" + tag)
        return jnp.where(ci == 0, jnp.concatenate([half, other], axis=0), jnp.concatenate([other, half], axis=0))

    started = {}

    def late_grads(gw):
        slots = []
        for k in range(N_CHIP):
            parts = []
            for n in LATE:
                rows = w[n].shape[0]
                blk = gw[n][:, k * rows:(k + 1) * rows] if n in COL_SHARDED else gw[n][k * rows:(k + 1) * rows]
                parts.append(blk.astype(MMD))
            slots.append(jnp.concatenate(parts, axis=0))
        pair = pair_sums(jnp.stack(slots), "late")
        sems, src, land, token = _chips_start(pair, True, name="rs_late_start")
        started["late"] = (sems, src, land)
        return token[0:1, 0:1]

    def in_grad(g):
        g_in = _unpermute_in(g)
        cols_in = w["w_in"].shape[1]
        pair = pair_sums(jnp.stack([g_in[:, k * cols_in:(k + 1) * cols_in].astype(MMD) for k in range(N_CHIP)]), "w_in")
        sems, src, land, token = _chips_start(pair, True, name="rs_w_in_start")
        started["w_in"] = (sems, src, land)
        return token

    loss, grad_x, dmod, gw, gs = _local_step(x, tgt, mod, wts, small, late_weights, late_grads, in_grad)

    grads = {}
    pair, land = _chips_wait(*started["w_in"], grad_x, True, name="rs_w_in_wait")
    grads["w_in"] = finish(land, pair, "w_in")
    pair, land = _chips_wait(*started["late"], grad_x, True, name="rs_late_wait")
    total, o = finish(land, pair, "late"), 0
    for n in LATE:
        rows = w[n].shape[0]
        grads[n] = total[o:o + rows]
        o += rows

    order = ([dmod.reshape(-1)] + [gs[n].reshape(-1) for n in SMALL if n != "b_ada"] + [gs["conv_w"].reshape(-1)]
             + [loss.reshape(-1)])
    vec = jnp.concatenate(order)
    n_small = vec.shape[0]
    n_pad = -(-n_small // 1024) * 1024
    g2 = _allgather8(_pad_to(vec, n_pad).reshape(8, n_pad // 8), name="ag_small")
    tot = _rows_sum(g2, N_DEV, name="small_sum").reshape(-1)
    loss = tot[n_small - 1]
    dmod_all = g2.reshape(N_DEV, n_pad)[:, 0:6 * d]
    off = 0
    for n in SMALL:
        grads[n] = tot[off:off + w[n].size].reshape(w[n].shape)
        off += w[n].size
    conv_full = tot[off:off + D_CONV * N_CHIP * cw].reshape(D_CONV, N_CHIP * cw)
    grads["conv_w"] = lax.dynamic_slice_in_dim(conv_full, chip * cw, cw, axis=1)
    ada_cols = w["w_ada"].shape[1]
    dmod_mine = lax.dynamic_slice_in_dim(dmod_all, chip * ada_cols, ada_cols, axis=1).astype(MMD)
    grads["w_ada"] = _mm_tn(sc, dmod_mine, name="ada_dw", tk=512, tn=512, tmm=8)

    delta, new_m, new_v = {}, {}, {}
    pack = lambda t: jnp.concatenate([t[n].reshape(-1) for n in SMALL]).reshape(1, -1)
    ds_, ms_, vs_ = _adamw(pack(w), pack(grads), pack(m), pack(v), name="adamw_small")
    off = 0
    for n in SMALL:
        for dst, src in ((delta, ds_), (new_m, ms_), (new_v, vs_)):
            dst[n] = src[0, off:off + w[n].size].reshape(w[n].shape)
        off += w[n].size
    for n in ("w_ada", "conv_w") + BIG:
        delta[n], new_m[n], new_v[n] = _adamw(w[n], grads[n], m[n], v[n], name="adamw_" + n)
    return loss, grad_x, grads, delta, new_m, new_v


def kernel(x, c, w_ada, b_ada, norm1_w, norm2_w, w_in, q_norm_w, k_norm_w, conv_w, conv_b, A_log, dt_bias, ssd_D, ssd_norm_w, w_attn_out, w_ssd_out, w_o, w_mlp1, w_mlp2, loss_target, m_w_ada, m_b_ada, m_norm1_w, m_norm2_w, m_w_in, m_q_norm_w, m_k_norm_w, m_conv_w, m_conv_b, m_A_log, m_dt_bias, m_ssd_D, m_ssd_norm_w, m_w_attn_out, m_w_ssd_out, m_w_o, m_w_mlp1, m_w_mlp2, v_w_ada, v_b_ada, v_norm1_w, v_norm2_w, v_w_in, v_q_norm_w, v_k_norm_w, v_conv_w, v_conv_b, v_A_log, v_dt_bias, v_ssd_D, v_ssd_norm_w, v_w_attn_out, v_w_ssd_out, v_w_o, v_w_mlp1, v_w_mlp2):
    args = dict(locals())
    strip = lambda a: a[0] if a.ndim == 3 else a
    w = {n: strip(args[n]) for n in NAMES + ("x", "c")}
    m = {n: strip(args["m_" + n]) for n in NAMES}
    v = {n: strip(args["v_" + n]) for n in NAMES}
    loss, grad_x, grads, delta, new_m, new_v = _step(w, m, v, loss_target[0])
    like = lambda t, n: t.reshape(args[n].shape)
    return (loss, grad_x[None], *[like(grads[n], n) for n in NAMES], *[like(delta[n], n) for n in NAMES],
            *[like(new_m[n], n) for n in NAMES], *[like(new_v[n], n) for n in NAMES])
```

```python
import functools
import math

import jax
import jax.numpy as jnp
from jax import lax
from jax.experimental import pallas as pl
from jax.experimental.pallas import tpu as pltpu

F32 = jnp.float32
MMD = jnp.bfloat16
EPS = 1e-6
NEG = -1e30
MIB = 1024 * 1024
VMEM_BIG = 56 * MIB
VMEM_MID = 40 * MIB

GRID_W = 64
N_Q_HEADS, N_KV_HEADS, HEAD_DIM = 16, 4, 64
ROPE_THETA = 10000.0
SSD_HEADS, SSD_GROUPS, SSD_P, SSD_N, CHUNK = 32, 4, 64, 128, 128
HPG = SSD_HEADS // SSD_GROUPS
D_CONV = 5
ADAM_LR, ADAM_B1, ADAM_B2, ADAM_EPS, ADAM_WD, ADAM_STEP = 0.001, 0.9, 0.999, 1e-08, 0.01, 10

Z0, GA0, GS0, XS0, B0, C0, Q0, K0, V0, DT0, PW = 0, 2048, 3072, 4096, 6144, 6656, 7168, 8192, 8448, 8704, 8832

MESH = pl.DeviceIdType.MESH
NT = (((1,), (1,)), ((), ()))
TN = (((0,), (0,)), ((), ()))


def _cp(sem=None, vmem=VMEM_MID):
    return pltpu.CompilerParams(dimension_semantics=sem, vmem_limit_bytes=vmem)


def _tile(n, pref):
    t = min(n, pref)
    while n % t:
        t //= 2
    return t


def _dot(a, b, dims=None):
    if dims is None:
        return jnp.dot(a, b, preferred_element_type=F32)
    return lax.dot_general(a, b, dims, preferred_element_type=F32)


def _dot_hi(a, b):
    return jnp.dot(a, b, precision=lax.Precision.HIGHEST, preferred_element_type=F32)


def _sigmoid(x):
    return jax.nn.sigmoid(x)


def _mm(a, b, *, name, outs, nt=False, ta=False, extras=(), epi=None, tm=512, tn=512, n=None, b_outer=False,
        vmem=VMEM_MID):
    assert not (nt and ta)
    k, m = a.shape if ta else a.shape[::-1]
    if n is None:
        n = b.shape[0] if nt else b.shape[1]
    tm, tn = _tile(m, tm), _tile(n, tn)
    gi, gj = m // tm, n // tn
    if b_outer:
        grid = (gj, gi)
        ij = lambda p, q: (q, p)
    else:
        grid = (gi, gj)
        ij = lambda p, q: (p, q)
    if ta:
        a_spec = pl.BlockSpec((k, tm), lambda p, q: (0, ij(p, q)[0]))
    else:
        a_spec = pl.BlockSpec((tm, k), lambda p, q: (ij(p, q)[0], 0))
    if nt:
        b_spec = pl.BlockSpec((tn, k), lambda p, q: (ij(p, q)[1], 0))
    else:
        b_spec = pl.BlockSpec((k, tn), lambda p, q: (0, ij(p, q)[1]))
    e_specs = []
    for arr, kind, off in extras:
        ob = off // tn
        assert off % tn == 0
        if kind == "tile":
            e_specs.append(pl.BlockSpec((tm, tn), lambda p, q, ob=ob: (ij(p, q)[0], ob + ij(p, q)[1])))
        else:
            e_specs.append(pl.BlockSpec((1, tn), lambda p, q, ob=ob: (0, ob + ij(p, q)[1])))
    ne = len(extras)

    def body(a_ref, b_ref, *rest):
        acc = _dot(a_ref[...], b_ref[...], NT if nt else (TN if ta else None))
        res = epi(acc, *[e[...] for e in rest[:ne]]) if epi is not None else (acc,)
        for o_ref, r in zip(rest[ne:], res):
            o_ref[...] = r.astype(o_ref.dtype)

    out = pl.pallas_call(
        body, name=name, grid=grid,
        in_specs=[a_spec, b_spec] + e_specs,
        out_specs=[pl.BlockSpec((tm, tn), lambda p, q: ij(p, q)) for _ in outs],
        out_shape=[jax.ShapeDtypeStruct((m, n), dt) for dt in outs],
        compiler_params=_cp(("arbitrary", "arbitrary"), vmem),
    )(a, b, *[e[0] for e in extras])
    return out if len(outs) > 1 else out[0]


def _mm_tn(a, g, *, name, tk=512, tn=1024, tmm=4096, vmem=VMEM_MID):
    m, k = a.shape
    n = g.shape[1]
    tk, tn, tmm = _tile(k, tk), _tile(n, tn), _tile(m, tmm)

    def body(a_ref, g_ref, o_ref):
        p = _dot(a_ref[...], g_ref[...], TN)

        @pl.when(pl.program_id(2) == 0)
        def _():
            o_ref[...] = p

        @pl.when(pl.program_id(2) > 0)
        def _():
            o_ref[...] += p

    return pl.pallas_call(
        body, name=name, grid=(k // tk, n // tn, m // tmm),
        in_specs=[pl.BlockSpec((tmm, tk), lambda i, j, r: (r, i)), pl.BlockSpec((tmm, tn), lambda i, j, r: (r, j))],
        out_specs=pl.BlockSpec((tk, tn), lambda i, j, r: (i, j)),
        out_shape=jax.ShapeDtypeStruct((k, n), F32),
        compiler_params=_cp(("arbitrary", "arbitrary", "arbitrary"), vmem),
    )(a, g)


def _adamw(w, g, m, v, *, name):
    r, c = w.shape
    tr = _tile(r, 256) if r % 8 == 0 else r

    def body(w_ref, g_ref, m_ref, v_ref, d_ref, nm_ref, nv_ref):
        gg = g_ref[...]
        nm = ADAM_B1 * m_ref[...] + (1.0 - ADAM_B1) * gg
        nv = ADAM_B2 * v_ref[...] + (1.0 - ADAM_B2) * jnp.square(gg)
        m_hat = nm / (1.0 - ADAM_B1 ** ADAM_STEP)
        v_hat = nv / (1.0 - ADAM_B2 ** ADAM_STEP)
        d_ref[...] = -ADAM_LR * (m_hat / (jnp.sqrt(v_hat) + ADAM_EPS) + ADAM_WD * w_ref[...])
        nm_ref[...] = nm
        nv_ref[...] = nv

    spec = pl.BlockSpec((tr, c), lambda i: (i, 0))
    return pl.pallas_call(
        body, name=name, grid=(r // tr,), in_specs=[spec] * 4, out_specs=[spec] * 3,
        out_shape=[jax.ShapeDtypeStruct((r, c), F32)] * 3, compiler_params=_cp(("arbitrary",)),
    )(w, g, m, v)


def _rows_sum(a, groups, *, name):
    r = a.shape[0] // groups

    def body(a_ref, o_ref):
        acc = a_ref[0:r, :]
        for d in range(1, groups):
            acc = acc + a_ref[d * r:(d + 1) * r, :]
        o_ref[...] = acc

    return pl.pallas_call(body, name=name, out_shape=jax.ShapeDtypeStruct((r, a.shape[1]), F32))(a)


def _silu_cast(a, *, name):
    def body(a_ref, o_ref):
        x = a_ref[...]
        o_ref[...] = (x * _sigmoid(x)).astype(o_ref.dtype)

    return pl.pallas_call(body, name=name, out_shape=jax.ShapeDtypeStruct(a.shape, MMD))(a)


def _sumsq(a, *, name):
    m, n = a.shape
    tm = _tile(m, 512)

    def body(a_ref, o_ref):
        x = a_ref[...]
        p = jnp.sum(jnp.sum(x * x, axis=1, keepdims=True), axis=0, keepdims=True)

        @pl.when(pl.program_id(0) == 0)
        def _():
            o_ref[...] = p

        @pl.when(pl.program_id(0) > 0)
        def _():
            o_ref[...] += p

    return pl.pallas_call(
        body, name=name, grid=(m // tm,), in_specs=[pl.BlockSpec((tm, n), lambda i: (i, 0))],
        out_specs=pl.BlockSpec((1, 1), lambda i: (0, 0)), out_shape=jax.ShapeDtypeStruct((1, 1), F32),
        compiler_params=_cp(("arbitrary",)),
    )(a)


def _acc_rows(o_ref, p, first):
    @pl.when(first)
    def _():
        o_ref[...] = p

    @pl.when(jnp.logical_not(first))
    def _():
        o_ref[...] += p


def _ln_mod(x, w, scale, shift, *, name):
    s, d = x.shape
    tm = _tile(s, 512)

    def body(x_ref, w_ref, sc_ref, sh_ref, o_ref):
        xv = x_ref[...]
        r = lax.rsqrt(jnp.mean(xv * xv, axis=-1, keepdims=True) + EPS)
        o_ref[...] = ((xv * r) * w_ref[...] * (1.0 + sc_ref[...]) + sh_ref[...]).astype(o_ref.dtype)

    row = pl.BlockSpec((1, d), lambda i: (0, 0))
    big = pl.BlockSpec((tm, d), lambda i: (i, 0))
    return pl.pallas_call(
        body, name=name, grid=(s // tm,), in_specs=[big, row, row, row], out_specs=big,
        out_shape=jax.ShapeDtypeStruct((s, d), MMD), compiler_params=_cp(("arbitrary",)),
    )(x, w, scale, shift)


def _ln_mod_bwd(dh, x, w, scale, dres, *, name):
    s, d = x.shape
    tm = _tile(s, 512)

    def body(dh_ref, x_ref, w_ref, sc_ref, dres_ref, dx_ref, dsh_ref, dsc_ref, dw_ref):
        xv = x_ref[...]
        dhv = dh_ref[...].astype(F32)
        r = lax.rsqrt(jnp.mean(xv * xv, axis=-1, keepdims=True) + EPS)
        nv = xv * r
        wv = w_ref[...]
        g1 = 1.0 + sc_ref[...]
        dn = dhv * (wv * g1)
        dx_ref[...] = dres_ref[...] + r * (dn - nv * jnp.mean(dn * nv, axis=-1, keepdims=True))
        first = pl.program_id(0) == 0
        _acc_rows(dsh_ref, jnp.sum(dhv, axis=0, keepdims=True), first)
        _acc_rows(dsc_ref, jnp.sum(dhv * nv * wv, axis=0, keepdims=True), first)
        _acc_rows(dw_ref, jnp.sum(dhv * nv * g1, axis=0, keepdims=True), first)

    row = pl.BlockSpec((1, d), lambda i: (0, 0))
    big = pl.BlockSpec((tm, d), lambda i: (i, 0))
    return pl.pallas_call(
        body, name=name, grid=(s // tm,), in_specs=[big, big, row, row, big], out_specs=[big, row, row, row],
        out_shape=[jax.ShapeDtypeStruct((s, d), F32)] + [jax.ShapeDtypeStruct((1, d), F32)] * 3,
        compiler_params=_cp(("arbitrary",)),
    )(dh, x, w, scale, dres)


def _gate_bwd(dy, u, gate, *, name):
    s, d = dy.shape
    tm = _tile(s, 512)

    def body(dy_ref, u_ref, g_ref, du_ref, dg_ref):
        dyv = dy_ref[...]
        du_ref[...] = (dyv * g_ref[...]).astype(du_ref.dtype)
        _acc_rows(dg_ref, jnp.sum(dyv * u_ref[...].astype(F32), axis=0, keepdims=True), pl.program_id(0) == 0)

    row = pl.BlockSpec((1, d), lambda i: (0, 0))
    big = pl.BlockSpec((tm, d), lambda i: (i, 0))
    return pl.pallas_call(
        body, name=name, grid=(s // tm,), in_specs=[big, big, row], out_specs=[big, row],
        out_shape=[jax.ShapeDtypeStruct((s, d), MMD), jax.ShapeDtypeStruct((1, d), F32)],
        compiler_params=_cp(("arbitrary",)),
    )(dy, u, gate)


def _seg64(v, e):
    hi = v.astype(jnp.bfloat16)
    lo = (v - hi.astype(F32)).astype(jnp.bfloat16)
    return _dot(hi, e) + _dot(lo, e)


def _rope_tables(s):
    rows = s // GRID_W
    pos_row = jnp.repeat(jnp.arange(rows, dtype=jnp.int32), GRID_W).astype(F32)
    pos_col = jnp.tile(jnp.arange(GRID_W, dtype=jnp.int32), rows).astype(F32)
    axis_dim = HEAD_DIM // 2
    inv_freq = ROPE_THETA ** (-jnp.arange(0, axis_dim, 2, dtype=F32) / axis_dim)
    ang_r = pos_row[:, None] * inv_freq[None, :]
    ang_c = pos_col[:, None] * inv_freq[None, :]
    zero = jnp.zeros_like(ang_r)
    cos = jnp.concatenate([jnp.cos(ang_r), jnp.cos(ang_r), jnp.cos(ang_c), jnp.cos(ang_c)], axis=1)
    s_a = jnp.concatenate([-jnp.sin(ang_r), zero, -jnp.sin(ang_c), zero], axis=1)
    s_b = jnp.concatenate([zero, jnp.sin(ang_r), zero, jnp.sin(ang_c)], axis=1)
    return [jnp.tile(t, (1, 2)) for t in (cos, s_a, s_b)]


def _e128():
    i = jnp.arange(128)
    return (i[:, None] // 64 == i[None, :] // 64).astype(jnp.bfloat16)


QKW = N_Q_HEADS * HEAD_DIM + N_KV_HEADS * HEAD_DIM


def _qk_fwd(proj, wrow, scrow, tabs, *, name):
    s = proj.shape[0]
    tm = _tile(s, 1024)

    def body(x_ref, w_ref, sc_ref, cos_ref, sa_ref, sb_ref, e_ref, o_ref, ot_ref):
        u = x_ref[...].astype(F32)
        r = lax.rsqrt(_seg64(u * u, e_ref[...]) * (1.0 / HEAD_DIM) + EPS)
        nv = (u * r) * w_ref[...]
        ro = nv * cos_ref[...] + pltpu.roll(nv, 112, 1) * sa_ref[...] + pltpu.roll(nv, 16, 1) * sb_ref[...]
        out = ro * sc_ref[...]
        o_ref[...] = out.astype(o_ref.dtype)
        ot_ref[...] = out.T.astype(ot_ref.dtype)

    tab = pl.BlockSpec((tm, 128), lambda i, j: (i, 0))
    row = pl.BlockSpec((1, 128), lambda i, j: (0, j))
    return pl.pallas_call(
        body, name=name, grid=(s // tm, QKW // 128),
        in_specs=[pl.BlockSpec((tm, 128), lambda i, j: (i, Q0 // 128 + j)), row, row, tab, tab, tab,
                  pl.BlockSpec((128, 128), lambda i, j: (0, 0))],
        out_specs=[pl.BlockSpec((tm, 128), lambda i, j: (i, j)), pl.BlockSpec((128, tm), lambda i, j: (j, i))],
        out_shape=[jax.ShapeDtypeStruct((s, QKW), MMD), jax.ShapeDtypeStruct((QKW, s), MMD)],
        compiler_params=_cp(("arbitrary", "arbitrary")),
    )(proj, wrow, scrow, *tabs, _e128())


def _qk_bwd(dqkt, proj, wrow, scrow, tabs, *, name):
    s = proj.shape[0]
    tm = _tile(s, 1024)

    def body(d_ref, x_ref, w_ref, sc_ref, cos_ref, sa_ref, sb_ref, e_ref, du_ref, dw_ref):
        e = e_ref[...]
        d = d_ref[...].T * sc_ref[...]
        dn = d * cos_ref[...] + pltpu.roll(d * sa_ref[...], 16, 1) + pltpu.roll(d * sb_ref[...], 112, 1)
        u = x_ref[...].astype(F32)
        r = lax.rsqrt(_seg64(u * u, e) * (1.0 / HEAD_DIM) + EPS)
        uh = u * r
        _acc_rows(dw_ref, jnp.sum(dn * uh, axis=0, keepdims=True), pl.program_id(1) == 0)
        dnw = dn * w_ref[...]
        du_ref[...] = (r * (dnw - uh * (_seg64(dnw * uh, e) * (1.0 / HEAD_DIM)))).astype(du_ref.dtype)

    tab = pl.BlockSpec((tm, 128), lambda j, i: (i, 0))
    row = pl.BlockSpec((1, 128), lambda j, i: (0, j))
    return pl.pallas_call(
        body, name=name, grid=(QKW // 128, s // tm),
        in_specs=[pl.BlockSpec((128, tm), lambda j, i: (j, i)), pl.BlockSpec((tm, 128), lambda j, i: (i, Q0 // 128 + j)),
                  row, row, tab, tab, tab, pl.BlockSpec((128, 128), lambda j, i: (0, 0))],
        out_specs=[pl.BlockSpec((tm, 128), lambda j, i: (i, j)), row],
        out_shape=[jax.ShapeDtypeStruct((s, QKW), MMD), jax.ShapeDtypeStruct((1, QKW), F32)],
        compiler_params=_cp(("arbitrary", "arbitrary")),
    )(dqkt, proj, wrow, scrow, *tabs, _e128())


REP = N_Q_HEADS // N_KV_HEADS


def _lanes(ref):
    return jnp.concatenate([ref[r] for r in range(REP)], axis=1)


V_AUG = HEAD_DIM + 8
LOG2E = math.log2(math.e)


def _flash_fwd(qkt, vta, *, name):
    s = qkt.shape[2]
    tq, tk = _tile(s, 1024), _tile(s, 512)
    nk = s // tk
    lanes = REP * tq

    def body(q_ref, k_ref, v_ref, o_ref, lse_ref, m_ref, acc_ref):
        j = pl.program_id(2)

        @pl.when(j == 0)
        def _():
            m_ref[...] = jnp.full_like(m_ref, NEG)
            acc_ref[...] = jnp.zeros_like(acc_ref)

        st = _dot(k_ref[0], _lanes(q_ref), TN)
        m_prev = m_ref[...]
        m_new = jnp.maximum(m_prev, jnp.max(st, axis=0, keepdims=True))
        p = jnp.exp2(st - m_new).astype(MMD)
        acc_ref[...] = jnp.exp2(m_prev - m_new) * acc_ref[...] + _dot(v_ref[0], p)
        m_ref[...] = m_new

        @pl.when(j == nk - 1)
        def _():
            acc = acc_ref[...]
            l = acc[HEAD_DIM:HEAD_DIM + 1]
            o = acc[0:HEAD_DIM] / l
            ls = m_ref[...] + jnp.log(l) * LOG2E
            for r in range(REP):
                o_ref[r] = o[:, r * tq:(r + 1) * tq].astype(o_ref.dtype)
                lse_ref[r] = ls[:, r * tq:(r + 1) * tq]

    qspec = pl.BlockSpec((REP, HEAD_DIM, tq), lambda g, i, j: (g, 0, i))
    return pl.pallas_call(
        body, name=name, grid=(N_KV_HEADS, s // tq, nk),
        in_specs=[qspec, pl.BlockSpec((1, HEAD_DIM, tk), lambda g, i, j: (N_Q_HEADS + g, 0, j)),
                  pl.BlockSpec((1, V_AUG, tk), lambda g, i, j: (g, 0, j))],
        out_specs=[qspec, pl.BlockSpec((REP, 1, tq), lambda g, i, j: (g, 0, i))],
        out_shape=[jax.ShapeDtypeStruct((N_Q_HEADS, HEAD_DIM, s), MMD), jax.ShapeDtypeStruct((N_Q_HEADS, 1, s), F32)],
        scratch_shapes=[pltpu.VMEM((1, lanes), F32), pltpu.VMEM((V_AUG, lanes), F32)],
        compiler_params=_cp(("arbitrary", "arbitrary", "arbitrary"), VMEM_BIG),
    )(qkt, qkt, vta)


def _flash_bwd(qkt, k_h, v_h, dot, ot, lse, *, name):
    s = qkt.shape[2]
    tq, tk = _tile(s, 512), _tile(s, 1024)
    nk = s // tk

    def body(q_ref, kt_ref, k_ref, v_ref, do_ref, o_ref, lse_ref, dq_ref, dk_ref, dv_ref, dq_acc):
        i, j = pl.program_id(1), pl.program_id(2)
        q, do = _lanes(q_ref), _lanes(do_ref)
        delta = jnp.sum(do.astype(F32) * _lanes(o_ref).astype(F32), axis=0, keepdims=True)
        k, v = k_ref[0], v_ref[0]
        p = jnp.exp2(_dot(k, q) - _lanes(lse_ref))
        dvc = _dot(p.astype(MMD), do, NT)
        ds = (p * (_dot(v, do) - delta)).astype(MMD)
        dkc = _dot(ds, q, NT) * (1.0 / LOG2E)
        dqc = _dot(kt_ref[0], ds)
        rows = pl.ds(pl.multiple_of(j * tk, tk), tk)

        @pl.when(i == 0)
        def _():
            dk_ref[0, rows, :] = dkc
            dv_ref[0, rows, :] = dvc

        @pl.when(i > 0)
        def _():
            dk_ref[0, rows, :] += dkc
            dv_ref[0, rows, :] += dvc

        @pl.when(j == 0)
        def _():
            dq_acc[...] = dqc

        @pl.when(j > 0)
        def _():
            dq_acc[...] += dqc

        @pl.when(j == nk - 1)
        def _():
            acc = dq_acc[...]
            for r in range(REP):
                dq_ref[r] = acc[:, r * tq:(r + 1) * tq]

    qspec = pl.BlockSpec((REP, HEAD_DIM, tq), lambda g, i, j: (g, 0, i))
    kvin = pl.BlockSpec((1, tk, HEAD_DIM), lambda g, i, j: (g, j, 0))
    kvres = pl.BlockSpec((1, s, HEAD_DIM), lambda g, i, j: (g, 0, 0))
    return pl.pallas_call(
        body, name=name, grid=(N_KV_HEADS, s // tq, nk),
        in_specs=[qspec, pl.BlockSpec((1, HEAD_DIM, tk), lambda g, i, j: (N_Q_HEADS + g, 0, j)), kvin, kvin,
                  qspec, qspec, pl.BlockSpec((REP, 1, tq), lambda g, i, j: (g, 0, i))],
        out_specs=[qspec, kvres, kvres],
        out_shape=[jax.ShapeDtypeStruct((N_Q_HEADS, HEAD_DIM, s), F32), jax.ShapeDtypeStruct((N_KV_HEADS, s, HEAD_DIM), F32),
                   jax.ShapeDtypeStruct((N_KV_HEADS, s, HEAD_DIM), F32)],
        scratch_shapes=[pltpu.VMEM((HEAD_DIM, REP * tq), F32)],
        compiler_params=_cp(("arbitrary", "arbitrary", "arbitrary"), VMEM_BIG),
    )(qkt, qkt, k_h, v_h, dot, ot, lse)


HALO = 8
CONV_W = 2048 + 2 * SSD_GROUPS * SSD_N


def _shifted(win, off, r):
    return pltpu.roll(win, (r + 2 * HALO - off) % (r + 2 * HALO), 0)[0:r]


def _conv_fwd(proj, w8, brow, *, name):
    s = proj.shape[0]
    cb = 256
    r = _tile(s, 512)

    def body(x_ref, w_ref, b_ref, o_ref, pad_ref):
        zeros = jnp.zeros((HALO, cb), F32)
        pad_ref[0:HALO, :] = zeros
        pad_ref[s + HALO:s + 2 * HALO, :] = zeros

        def fill(i, carry):
            st = pl.multiple_of(i * r, r)
            pad_ref[pl.ds(st + HALO, r), :] = x_ref[pl.ds(st, r), :].astype(F32)
            return carry

        lax.fori_loop(0, s // r, fill, 0)
        wv = w_ref[...]
        bv = b_ref[...]

        def step(i, carry):
            st = pl.multiple_of(i * r, r)
            win = pad_ref[pl.ds(st, r + 2 * HALO), :]
            acc = bv + wv[0:1, :] * _shifted(win, HALO - 2, r)
            for t in range(1, D_CONV):
                acc = acc + wv[t:t + 1, :] * _shifted(win, HALO - 2 + t, r)
            o_ref[pl.ds(st, r), :] = (acc * _sigmoid(acc)).astype(o_ref.dtype)
            return carry

        lax.fori_loop(0, s // r, step, 0)

    return pl.pallas_call(
        body, name=name, grid=(CONV_W // cb,),
        in_specs=[pl.BlockSpec((s, cb), lambda j: (0, XS0 // cb + j)), pl.BlockSpec((8, cb), lambda j: (0, j)),
                  pl.BlockSpec((1, cb), lambda j: (0, j))],
        out_specs=pl.BlockSpec((s, cb), lambda j: (0, j)),
        out_shape=jax.ShapeDtypeStruct((s, CONV_W), MMD),
        scratch_shapes=[pltpu.VMEM((s + 2 * HALO, cb), F32)],
        compiler_params=_cp(("arbitrary",), VMEM_MID),
    )(proj, w8, brow)


def _conv_bwd(proj, col0, ga, gb, w8, brow, *, name):
    s = proj.shape[0]
    width = ga.shape[1]
    cb = 128
    c0 = col0 // cb
    r = _tile(s, 512)

    def body(x_ref, ga_ref, gb_ref, w_ref, b_ref, dx_ref, dw_ref, db_ref, xpad, dpad):
        zeros = jnp.zeros((HALO, cb), F32)
        for ref in (xpad, dpad):
            ref[0:HALO, :] = zeros
            ref[s + HALO:s + 2 * HALO, :] = zeros

        def fill(i, carry):
            st = pl.multiple_of(i * r, r)
            xpad[pl.ds(st + HALO, r), :] = x_ref[pl.ds(st, r), :].astype(F32)
            return carry

        lax.fori_loop(0, s // r, fill, 0)
        wv = w_ref[...]
        bv = b_ref[...]

        def first(i, carry):
            st = pl.multiple_of(i * r, r)
            win = xpad[pl.ds(st, r + 2 * HALO), :]
            taps = [_shifted(win, HALO - 2 + t, r) for t in range(D_CONV)]
            u = bv
            for t in range(D_CONV):
                u = u + wv[t:t + 1, :] * taps[t]
            sg = _sigmoid(u)
            du = ((ga_ref[pl.ds(st, r), :].astype(F32) + gb_ref[pl.ds(st, r), :].astype(F32))
                  * (sg * (1.0 + u * (1.0 - sg))))
            dpad[pl.ds(st + HALO, r), :] = du
            out = [carry[0] + jnp.sum(du, axis=0, keepdims=True)]
            for t in range(D_CONV):
                out.append(carry[1 + t] + jnp.sum(du * taps[t], axis=0, keepdims=True))
            return tuple(out)

        sums = lax.fori_loop(0, s // r, first, tuple(jnp.zeros((1, cb), F32) for _ in range(1 + D_CONV)))
        db_ref[...] = sums[0]
        for t in range(D_CONV):
            dw_ref[t:t + 1, :] = sums[1 + t]
        dw_ref[D_CONV:8, :] = jnp.zeros((8 - D_CONV, cb), F32)

        def second(i, carry):
            st = pl.multiple_of(i * r, r)
            win = dpad[pl.ds(st, r + 2 * HALO), :]
            acc = wv[0:1, :] * _shifted(win, HALO + 2, r)
            for t in range(1, D_CONV):
                acc = acc + wv[t:t + 1, :] * _shifted(win, HALO + 2 - t, r)
            dx_ref[pl.ds(st, r), :] = acc.astype(dx_ref.dtype)
            return carry

        lax.fori_loop(0, s // r, second, 0)

    col = pl.BlockSpec((s, cb), lambda j: (0, j))
    return pl.pallas_call(
        body, name=name, grid=(width // cb,),
        in_specs=[pl.BlockSpec((s, cb), lambda j: (0, XS0 // cb + c0 + j)), col, col,
                  pl.BlockSpec((8, cb), lambda j: (0, c0 + j)), pl.BlockSpec((1, cb), lambda j: (0, c0 + j))],
        out_specs=[col, pl.BlockSpec((8, cb), lambda j: (0, j)), pl.BlockSpec((1, cb), lambda j: (0, j))],
        out_shape=[jax.ShapeDtypeStruct((s, width), MMD), jax.ShapeDtypeStruct((8, width), F32),
                   jax.ShapeDtypeStruct((1, width), F32)],
        scratch_shapes=[pltpu.VMEM((s + 2 * HALO, cb), F32), pltpu.VMEM((s + 2 * HALO, cb), F32)],
        compiler_params=_cp(("arbitrary",), VMEM_BIG),
    )(proj, ga, gb, w8, brow)


def _tri(lower):
    i = jnp.arange(CHUNK)
    return ((i[:, None] >= i[None, :]) if lower else (i[:, None] <= i[None, :])).astype(F32)


def _dt_fwd(raw, bias, arow, *, name):
    s = raw.shape[0]

    def body(r_ref, b_ref, a_ref, lo_ref, up_ref, dt_ref, cs_ref):
        u = r_ref[...] + b_ref[...]
        dt = jnp.maximum(u, 0.0) + jnp.log1p(jnp.exp(-jnp.abs(u)))
        dt_ref[...] = dt
        a = dt * a_ref[...]
        lane = lax.broadcasted_iota(jnp.int32, (CHUNK, 128), 1)
        cs_ref[...] = jnp.where(lane < SSD_HEADS, _dot_hi(lo_ref[...], a), _dot_hi(up_ref[...], a))

    blk = pl.BlockSpec((CHUNK, 128), lambda i: (i, 0))
    row = pl.BlockSpec((1, 128), lambda i: (0, 0))
    tri = pl.BlockSpec((CHUNK, CHUNK), lambda i: (0, 0))
    return pl.pallas_call(
        body, name=name, grid=(s // CHUNK,), in_specs=[blk, row, row, tri, tri], out_specs=[blk, blk],
        out_shape=[jax.ShapeDtypeStruct((s, 128), F32)] * 2, compiler_params=_cp(("arbitrary",)),
    )(raw, bias, arow, _tri(True), _tri(False))


def _dt_bwd(ddt, raw, bias, *, name):
    s = raw.shape[0]
    tm = _tile(s, 1024)

    def body(d_ref, r_ref, b_ref, o_ref, db_ref):
        g = d_ref[...] * _sigmoid(r_ref[...] + b_ref[...])
        o_ref[...] = g.astype(o_ref.dtype)
        _acc_rows(db_ref, jnp.sum(g, axis=0, keepdims=True), pl.program_id(0) == 0)

    blk = pl.BlockSpec((tm, 128), lambda i: (i, 0))
    row = pl.BlockSpec((1, 128), lambda i: (0, 0))
    return pl.pallas_call(
        body, name=name, grid=(s // tm,), in_specs=[blk, blk, row], out_specs=[blk, row],
        out_shape=[jax.ShapeDtypeStruct((s, 128), MMD), jax.ShapeDtypeStruct((1, 128), F32)],
        compiler_params=_cp(("arbitrary",)),
    )(ddt, raw, bias)


GW = HPG * SSD_P


GPS = 4


def _ssd_specs(nc, rev):
    cc = (lambda c: nc - 1 - c) if rev else (lambda c: c)
    nb = SSD_GROUPS // GPS
    return dict(
        x=pl.BlockSpec((CHUNK, GPS * GW), lambda g, c: (cc(c), g)),
        b=pl.BlockSpec((CHUNK, GPS * SSD_N), lambda g, c: (cc(c), 2048 // (GPS * SSD_N) + g)),
        c=pl.BlockSpec((CHUNK, GPS * SSD_N), lambda g, c: (cc(c), 2048 // (GPS * SSD_N) + nb + g)),
        col=pl.BlockSpec((GPS, CHUNK, HPG), lambda g, c: (g, cc(c), 0)),
        lanes=pl.BlockSpec((CHUNK, 128), lambda g, c: (cc(c), 0)),
        rowt=pl.BlockSpec((GPS, 1, HPG, CHUNK), lambda g, c: (g, cc(c), 0, 0)),
        drow=pl.BlockSpec((1, GPS * GW), lambda g, c: (0, g)),
        y=pl.BlockSpec((CHUNK, GPS * GW), lambda g, c: (cc(c), g)),
        h=pl.BlockSpec((GPS, 1, SSD_N, GW), lambda g, c: (g, cc(c), 0, 0)),
        n=pl.BlockSpec((CHUNK, GPS * SSD_N), lambda g, c: (cc(c), g)),
    )


def _ssd_mask(anti):
    ii = lax.broadcasted_iota(jnp.int32, (CHUNK, CHUNK), 0)
    jj = lax.broadcasted_iota(jnp.int32, (CHUNK, CHUNK), 1)
    return ii, jj, (ii <= jj) if anti else (ii >= jj)


def _expand(x, ex):
    h1 = x.astype(jnp.bfloat16)
    r1 = x - h1.astype(F32)
    h2 = r1.astype(jnp.bfloat16)
    h3 = (r1 - h2.astype(F32)).astype(jnp.bfloat16)
    return _dot(h1, ex) + _dot(h2, ex) + _dot(h3, ex)


def _headsum(a, e):
    hi = a.astype(jnp.bfloat16)
    return _dot(hi, e) + _dot((a - hi.astype(F32)).astype(jnp.bfloat16), e)


def _expand_mats():
    lane = jnp.arange(128)[None, :, None]
    col = jnp.arange(GW)[None, None, :]
    base = (jnp.arange(2)[:, None] * SSD_HEADS + jnp.arange(SSD_GROUPS)[None, :] * HPG).reshape(2 * SSD_GROUPS, 1, 1)
    return (lane == base + col // SSD_P).astype(jnp.bfloat16)


def _headsum_mats():
    e1 = (jnp.arange(GW)[:, None] // SSD_P == jnp.arange(128)[None, :]).astype(jnp.bfloat16)
    e2 = (jnp.arange(HPG * CHUNK)[:, None] // CHUNK == jnp.arange(128)[None, :]).astype(jnp.bfloat16)
    return e1, e2


def _ssd_fwd(xc, dt, cs, cst, ex, drow, di, *, name):
    s = xc.shape[0]
    nc = s // CHUNK
    anti = di == 1
    sp = _ssd_specs(nc, anti)
    trow = 0 if anti else CHUNK - 1

    def body(x_ref, b_ref, c_ref, dt_ref, cs_ref, cst_ref, ex_ref, d_ref, y_ref, hp_ref, h_ref):
        @pl.when(pl.program_id(1) == 0)
        def _():
            h_ref[...] = jnp.zeros_like(h_ref)

        mask = _ssd_mask(anti)[2]
        dtv, csv = dt_ref[...], cs_ref[...]
        for gi in range(GPS):
            cols = slice(gi * GW, (gi + 1) * GW)
            ncols = slice(gi * SSD_N, (gi + 1) * SSD_N)
            ex = ex_ref[gi]
            xb = x_ref[:, cols].astype(F32)
            bm, cm = b_ref[:, ncols], c_ref[:, ncols]
            csr = cst_ref[gi, 0]
            dtf = _expand(dtv, ex)
            csf = _expand(csv, ex)
            tl = csf[trow:trow + 1, :]
            h = h_ref[gi]
            hp_ref[gi, 0] = h.astype(hp_ref.dtype)
            g = _dot(cm, bm, NT)
            xs = xb * dtf
            xsm = xs.astype(MMD)
            base = jnp.exp(csf) * _dot(cm, h.astype(MMD)) + d_ref[:, cols] * xb
            for r in range(HPG):
                sl = slice(r * SSD_P, (r + 1) * SSD_P)
                lm = jnp.exp(jnp.where(mask, csf[:, r * SSD_P:r * SSD_P + 1] - csr[r:r + 1, :], NEG))
                y_ref[:, gi * GW + r * SSD_P:gi * GW + (r + 1) * SSD_P] = (
                    _dot((g * lm).astype(MMD), xsm[:, sl]) + base[:, sl]).astype(y_ref.dtype)
            xd = (xs * jnp.exp(tl - csf)).astype(MMD)
            h_ref[gi] = h * jnp.exp(tl) + _dot(bm, xd, TN)

    nb = SSD_GROUPS // GPS
    return pl.pallas_call(
        body, name=name, grid=(nb, nc),
        in_specs=[sp["x"], sp["b"], sp["c"], sp["lanes"], sp["lanes"], sp["rowt"],
                  pl.BlockSpec((GPS, 128, GW), lambda g, c: (di * nb + g, 0, 0)), sp["drow"]],
        out_specs=[sp["y"], sp["h"]],
        out_shape=[jax.ShapeDtypeStruct((s, 2048), MMD), jax.ShapeDtypeStruct((SSD_GROUPS, nc, SSD_N, GW), MMD)],
        scratch_shapes=[pltpu.VMEM((GPS, SSD_N, GW), F32)],
        compiler_params=_cp(("arbitrary", "arbitrary")),
    )(xc, xc, xc, dt, cs, cst, ex, drow)


def _ssd_bwd(xc, dt, cs, dt4, cst, ex, drow, arow4, dy, hprev, di, *, name):
    s = xc.shape[0]
    nc = s // CHUNK
    anti = di == 1
    sp = _ssd_specs(nc, not anti)
    trow = 0 if anti else CHUNK - 1
    e1, e2 = _headsum_mats()

    def body(x_ref, b_ref, c_ref, dt_ref, cs_ref, dt4_ref, cst_ref, ex_ref, d_ref, a_ref, dy_ref, hp_ref, tri_ref,
             e1_ref, e2_ref, dx_ref, db_ref, dc_ref, ddt_ref, da_ref, dh_ref, w_ref, dxs_ref):
        @pl.when(pl.program_id(1) == 0)
        def _():
            dh_ref[...] = jnp.zeros_like(dh_ref)
            da_ref[...] = jnp.zeros_like(da_ref)

        e1v = e1_ref[...]
        ii, _, mask = _ssd_mask(anti)
        dtv, csv = dt_ref[...], cs_ref[...]
        for gi in range(GPS):
            cols = slice(gi * GW, (gi + 1) * GW)
            ncols = slice(gi * SSD_N, (gi + 1) * SSD_N)
            ex = ex_ref[gi]
            xb = x_ref[:, cols].astype(F32)
            bm, cm = b_ref[:, ncols], c_ref[:, ncols]
            csr = cst_ref[gi, 0]
            dym = dy_ref[:, cols]
            dyb = dym.astype(F32)
            hpm = hp_ref[gi, 0]
            hp = hpm.astype(F32)
            dh = dh_ref[gi]
            dhm = dh.astype(MMD)
            dtf = _expand(dtv, ex)
            csf = _expand(csv, ex)
            tl = csf[trow:trow + 1, :]
            e = jnp.exp(csf)
            dec = jnp.exp(tl - csf)
            et = jnp.exp(tl)
            xs = xb * dtf
            xsm = xs.astype(MMD)
            g = _dot(cm, bm, NT)
            z = _dot(cm, hpm)
            bdh = _dot(bm, dhm)
            dg = jnp.zeros((CHUNK, CHUNK), F32)
            wcols = jnp.zeros((CHUNK, CHUNK), F32)
            for r in range(HPG):
                sl = slice(r * SSD_P, (r + 1) * SSD_P)
                lm = jnp.exp(jnp.where(mask, csf[:, r * SSD_P:r * SSD_P + 1] - csr[r:r + 1, :], NEG))
                mm = g * lm
                dm = _dot(dym[:, sl], xsm[:, sl], NT)
                w = dm * mm
                w_ref[gi, :, r * CHUNK:(r + 1) * CHUNK] = w
                wcols = jnp.where(ii == r, jnp.sum(w, axis=0, keepdims=True), wcols)
                dg = dg + dm * lm
                dxs_ref[gi, :, sl] = _dot(mm.astype(MMD), dym[:, sl], TN)
            dxs = dxs_ref[gi] + dec * bdh
            dx_ref[:, cols] = (dxs * dtf + d_ref[:, cols] * dyb).astype(dx_ref.dtype)
            tb = xs * bdh * dec
            d_tot = jnp.sum(tb, axis=0, keepdims=True) + et * jnp.sum(dh * hp, axis=0, keepdims=True)
            d_tot = _headsum(jnp.broadcast_to(d_tot, (8, GW)), e1v)[0:1]
            dcs = (_headsum(dyb * (e * z) - tb, e1v) + _headsum(w_ref[gi], e2_ref[...]) - wcols.T
                   + jnp.where(ii == trow, d_tot, 0.0))
            da = _dot_hi(tri_ref[...], dcs)
            ddt_ref[gi] = (da * a_ref[gi] + _headsum(dxs * xb, e1v))[:, 0:HPG]
            da_ref[gi] += jnp.sum(da[:, 0:HPG] * dt4_ref[gi], axis=0, keepdims=True)
            dgm = dg.astype(MMD)
            dz = (e * dyb).astype(MMD)
            dc_ref[:, ncols] = (_dot(dgm, bm) + _dot(dz, hpm, NT)).astype(dc_ref.dtype)
            db_ref[:, ncols] = (_dot(dgm, cm, TN) + _dot((xs * dec).astype(MMD), dhm, NT)).astype(db_ref.dtype)
            dh_ref[gi] = dh * et + _dot(cm, dz, TN)

    nb = SSD_GROUPS // GPS
    const = lambda shape: pl.BlockSpec(shape, lambda g, c: (0,) * len(shape))
    return pl.pallas_call(
        body, name=name, grid=(nb, nc),
        in_specs=[sp["x"], sp["b"], sp["c"], sp["lanes"], sp["lanes"], sp["col"], sp["rowt"],
                  pl.BlockSpec((GPS, 128, GW), lambda g, c: (di * nb + g, 0, 0)), sp["drow"],
                  pl.BlockSpec((GPS, 1, 128), lambda g, c: (g, 0, 0)), sp["y"], sp["h"],
                  const((CHUNK, CHUNK)), const((GW, 128)), const((HPG * CHUNK, 128))],
        out_specs=[sp["y"], sp["n"], sp["n"], sp["col"], pl.BlockSpec((GPS, 1, HPG), lambda g, c: (g, 0, 0))],
        out_shape=[jax.ShapeDtypeStruct((s, 2048), MMD), jax.ShapeDtypeStruct((s, SSD_GROUPS * SSD_N), MMD),
                   jax.ShapeDtypeStruct((s, SSD_GROUPS * SSD_N), MMD), jax.ShapeDtypeStruct((SSD_GROUPS, s, HPG), F32),
                   jax.ShapeDtypeStruct((SSD_GROUPS, 1, HPG), F32)],
        scratch_shapes=[pltpu.VMEM((GPS, SSD_N, GW), F32), pltpu.VMEM((GPS, CHUNK, HPG * CHUNK), F32),
                        pltpu.VMEM((GPS, CHUNK, GW), F32)],
        compiler_params=_cp(("arbitrary", "arbitrary")),
    )(xc, xc, xc, dt, cs, dt4, cst, ex, drow, arow4, dy, hprev, _tri(anti), e1, e2)


def _gnorm_fwd(ya, yb, proj, w, *, name):
    s = ya.shape[0]
    tm = _tile(s, 256)

    def body(a_ref, b_ref, z_ref, w_ref, o_ref):
        zv = z_ref[...].astype(F32)
        t = (a_ref[...].astype(F32) + b_ref[...].astype(F32)) * (zv * _sigmoid(zv))
        r = lax.rsqrt(jnp.mean(t * t, axis=-1, keepdims=True) + EPS)
        o_ref[...] = ((t * r) * w_ref[...]).astype(o_ref.dtype)

    big = pl.BlockSpec((tm, 2048), lambda i: (i, 0))
    row = pl.BlockSpec((1, 2048), lambda i: (0, 0))
    return pl.pallas_call(
        body, name=name, grid=(s // tm,), in_specs=[big, big, big, row], out_specs=big,
        out_shape=jax.ShapeDtypeStruct((s, 2048), MMD), compiler_params=_cp(("arbitrary",)),
    )(ya, yb, proj, w)


def _gnorm_bwd(dout, ya, yb, proj, w, *, name):
    s = ya.shape[0]
    tm = _tile(s, 256)

    def body(do_ref, a_ref, b_ref, z_ref, w_ref, dy_ref, dz_ref, dw_ref):
        zv = z_ref[...].astype(F32)
        sg = _sigmoid(zv)
        sz = zv * sg
        y = a_ref[...].astype(F32) + b_ref[...].astype(F32)
        t = y * sz
        r = lax.rsqrt(jnp.mean(t * t, axis=-1, keepdims=True) + EPS)
        nv = t * r
        dov = do_ref[...].astype(F32)
        _acc_rows(dw_ref, jnp.sum(dov * nv, axis=0, keepdims=True), pl.program_id(0) == 0)
        dn = dov * w_ref[...]
        dt_ = r * (dn - nv * jnp.mean(dn * nv, axis=-1, keepdims=True))
        dy_ref[...] = (dt_ * sz).astype(dy_ref.dtype)
        dz_ref[...] = (dt_ * y * (sg * (1.0 + zv * (1.0 - sg)))).astype(dz_ref.dtype)

    big = pl.BlockSpec((tm, 2048), lambda i: (i, 0))
    row = pl.BlockSpec((1, 2048), lambda i: (0, 0))
    return pl.pallas_call(
        body, name=name, grid=(s // tm,), in_specs=[big, big, big, big, row], out_specs=[big, big, row],
        out_shape=[jax.ShapeDtypeStruct((s, 2048), MMD), jax.ShapeDtypeStruct((s, 2048), MMD),
                   jax.ShapeDtypeStruct((1, 2048), F32)],
        compiler_params=_cp(("arbitrary",)),
    )(dout, ya, yb, proj, w)


def _colsum_prod(a, b, *, name):
    s, n = a.shape
    tm = _tile(s, 256)

    def body(a_ref, b_ref, o_ref):
        _acc_rows(o_ref, jnp.sum(a_ref[...].astype(F32) * b_ref[...].astype(F32), axis=0, keepdims=True),
                  pl.program_id(0) == 0)

    big = pl.BlockSpec((tm, n), lambda i: (i, 0))
    return pl.pallas_call(
        body, name=name, grid=(s // tm,), in_specs=[big, big], out_specs=pl.BlockSpec((1, n), lambda i: (0, 0)),
        out_shape=jax.ShapeDtypeStruct((1, n), F32), compiler_params=_cp(("arbitrary",)),
    )(a, b)


def _heads(a, n):
    return a.reshape(a.shape[0], n, HEAD_DIM).transpose(1, 0, 2)


def _unheads(a):
    return a.transpose(1, 0, 2).reshape(a.shape[1], a.shape[0] * HEAD_DIM)


def _per_group(a):
    return a.reshape(a.shape[0], SSD_GROUPS, HPG).transpose(1, 0, 2)


def _per_group_t(a):
    s = a.shape[0]
    return a.reshape(s // CHUNK, CHUNK, SSD_GROUPS, HPG).transpose(2, 0, 3, 1)


def _local_step(x, target, mod, wts, small, late_weights=None, late_grads=None, in_grad=None):
    s, d = x.shape
    shift1, scale1, gate1, shift2, scale2, gate2 = [mod[i:i + 1] for i in range(6)]

    h1 = _ln_mod(x, small["norm1_w"], scale1, shift1, name="ln1")
    proj = _mm(h1, wts["w_in_p"], name="in_proj", outs=[MMD], tm=512, tn=2944, b_outer=True)
    dt_raw = _mm(h1, wts["w_dt"], name="dt_proj", outs=[F32], tm=512, tn=128)

    qk_w = jnp.concatenate([jnp.tile(small["q_norm_w"], (1, N_Q_HEADS)), jnp.tile(small["k_norm_w"], (1, N_KV_HEADS))], axis=1)
    qk_sc = jnp.concatenate([jnp.full((1, N_Q_HEADS * HEAD_DIM), HEAD_DIM ** -0.5, F32),
                             jnp.ones((1, N_KV_HEADS * HEAD_DIM), F32)], axis=1)
    qk_sc2 = jnp.concatenate([jnp.full((1, N_Q_HEADS * HEAD_DIM), HEAD_DIM ** -0.5 * LOG2E, F32),
                              jnp.ones((1, N_KV_HEADS * HEAD_DIM), F32)], axis=1)
    tabs = _rope_tables(s)
    qk, qkt = _qk_fwd(proj, qk_w, qk_sc2, tabs, name="qk_fwd")
    qkt = qkt.reshape(N_Q_HEADS + N_KV_HEADS, HEAD_DIM, s)
    k_h = _heads(qk[:, N_Q_HEADS * HEAD_DIM:], N_KV_HEADS)
    v_sd = proj[:, V0:V0 + N_KV_HEADS * HEAD_DIM]
    v_h = _heads(v_sd, N_KV_HEADS)
    vta = jnp.concatenate([v_sd.T.reshape(N_KV_HEADS, HEAD_DIM, s), jnp.ones((N_KV_HEADS, V_AUG - HEAD_DIM, s), MMD)], axis=1)
    ot, lse = _flash_fwd(qkt, vta, name="flash_fwd")
    ot2 = ot.reshape(N_Q_HEADS * HEAD_DIM, s)
    if late_weights is not None:
        wts = {**wts, **late_weights(ot)}

    w8 = jnp.pad(small["conv_w"], ((0, 8 - D_CONV), (0, 0)))
    xc = _conv_fwd(proj, w8, small["conv_b"], name="conv_fwd")
    a_neg = -jnp.exp(small["A_log"])
    arow = jnp.pad(a_neg.reshape(1, 2 * SSD_HEADS), ((0, 0), (0, 128 - 2 * SSD_HEADS)))
    bias_row = jnp.pad(small["dt_bias"].reshape(1, 2 * SSD_HEADS), ((0, 0), (0, 128 - 2 * SSD_HEADS)))
    dt, cs = _dt_fwd(dt_raw, bias_row, arow, name="dt_fwd")
    drow = jnp.repeat(small["ssd_D"], SSD_P, axis=1)
    dirs = []
    for di in range(2):
        cols = slice(di * SSD_HEADS, (di + 1) * SSD_HEADS)
        dirs.append(dict(
            dt4=_per_group(dt[:, cols]), cst=_per_group_t(cs[:, cols]),
            drow=drow if di == 0 else jnp.zeros_like(drow),
            arow4=jnp.pad(a_neg[di].reshape(SSD_GROUPS, 1, HPG), ((0, 0), (0, 0), (0, 128 - HPG)))))
    ex = _expand_mats()
    ys = []
    for di, dd in enumerate(dirs):
        y, dd["hprev"] = _ssd_fwd(xc, dt, cs, dd["cst"], ex, dd["drow"], di, name=f"ssd_fwd{di}")
        ys.append(y)
    ssdn = _gnorm_fwd(ys[0], ys[1], proj, small["ssd_norm_w"], name="gnorm_fwd")

    a_o = _mm(ot2, wts["w_attn_out"], name="attn_out", outs=[MMD], ta=True, tm=512, tn=1024)

    def merge_epi(acc, ao, ga, gs):
        return (_sigmoid(ga.astype(F32)) * ao.astype(F32) + _sigmoid(gs.astype(F32)) * acc, acc)

    merged, b_o = _mm(ssdn, wts["w_ssd_out"], name="ssd_out", outs=[MMD, MMD], tm=512, tn=512,
                      extras=[(a_o, "tile", 0), (proj, "tile", GA0), (proj, "tile", GS0)], epi=merge_epi)

    def res_epi(acc, res, gate):
        return (res + gate * acc, acc)

    x1, mo = _mm(merged, wts["w_o"], name="w_o", outs=[F32, MMD], tm=512, tn=512,
                 extras=[(x, "tile", 0), (gate1, "row", 0)], epi=res_epi)
    h2 = _ln_mod(x1, small["norm2_w"], scale2, shift2, name="ln2")

    def relu2_epi(acc):
        rl = jnp.maximum(acc, 0.0)
        return (rl * rl, rl)

    act, rl = _mm(h2, wts["w_mlp1"], name="mlp1", outs=[MMD, MMD], tm=512, tn=1024, epi=relu2_epi, b_outer=True)

    def loss_epi(acc, res, gate, tgt):
        return ((res + gate * acc - tgt) * (1.0 / d), acc)

    dy, ffo = _mm(act, wts["w_mlp2"], name="mlp2", outs=[F32, MMD], tm=512, tn=1024, vmem=VMEM_BIG,
                  extras=[(x1, "tile", 0), (gate2, "row", 0), (target, "tile", 0)], epi=loss_epi)
    loss = _sumsq(dy, name="loss") * (0.5 * d)

    gw = {}
    gs_ = {}
    dffo, dgate2 = _gate_bwd(dy, ffo, gate2, name="gate2_bwd")
    dpre = _mm(dffo, wts["w_mlp2"], name="mlp2_dx", outs=[MMD], nt=True, tm=512, tn=1024,
               extras=[(rl, "tile", 0)], epi=lambda acc, r: (acc * (2.0 * r.astype(F32)),))
    gw["w_mlp2"] = _mm_tn(act, dffo, name="mlp2_dw")
    dh2 = _mm(dpre, wts["w_mlp1"], name="mlp1_dx", outs=[F32], nt=True, tm=512, tn=1024)
    gw["w_mlp1"] = _mm_tn(h2, dpre, name="mlp1_dw")
    dx1, dshift2, dscale2, gs_["norm2_w"] = _ln_mod_bwd(dh2, x1, small["norm2_w"], scale2, dy, name="ln2_bwd")
    dmo, dgate1 = _gate_bwd(dx1, mo, gate1, name="gate1_bwd")

    def merge_bwd_epi(acc, ao, bo, ga, gs):
        sa, ss = _sigmoid(ga.astype(F32)), _sigmoid(gs.astype(F32))
        return (acc * sa, acc * ss, acc * ao.astype(F32) * sa * (1.0 - sa), acc * bo.astype(F32) * ss * (1.0 - ss))

    da_o, db_o, dga, dgs = _mm(dmo, wts["w_o"], name="w_o_dx", outs=[MMD] * 4, nt=True, tm=512, tn=512,
                               extras=[(a_o, "tile", 0), (b_o, "tile", 0), (proj, "tile", GA0), (proj, "tile", GS0)],
                               epi=merge_bwd_epi)
    gw["w_o"] = _mm_tn(merged, dmo, name="w_o_dw")
    dot = _mm(wts["w_attn_out"], da_o, name="attn_out_dx", outs=[MMD], nt=True, tm=512, tn=512)
    gw["w_attn_out"] = _mm(ot2, da_o, name="attn_out_dw", outs=[F32], tm=256, tn=512, vmem=VMEM_BIG)
    dssdn = _mm(db_o, wts["w_ssd_out"], name="ssd_out_dx", outs=[MMD], nt=True, tm=512, tn=512)
    gw["w_ssd_out"] = _mm_tn(ssdn, db_o, name="ssd_out_dw")

    norm_w = small["ssd_norm_w"] if late_grads is None else small["ssd_norm_w"] + late_grads(gw)
    dyssd, dz, gs_["ssd_norm_w"] = _gnorm_bwd(dssdn, ys[0], ys[1], proj, norm_w, name="gnorm_bwd")
    gs_["ssd_D"] = _colsum_prod(dyssd, xc[:, 0:2048], name="ssd_d_grad").reshape(SSD_HEADS, SSD_P).sum(axis=1).reshape(1, SSD_HEADS)
    dxc, ddts, das = [], [], []
    for di, dd in enumerate(dirs):
        dxs, dbm, dcm, ddt4, da4 = _ssd_bwd(xc, dt, cs, dd["dt4"], dd["cst"], ex, dd["drow"], dd["arow4"],
                                            dyssd, dd["hprev"], di, name=f"ssd_bwd{di}")
        dxc.append((dxs, dbm, dcm))
        ddts.append(ddt4.transpose(1, 0, 2).reshape(s, SSD_HEADS))
        das.append(da4.reshape(1, SSD_HEADS))
    conv_parts, col0 = [], 0
    for part, (ga, gb) in enumerate(zip(*dxc)):
        conv_parts.append(_conv_bwd(proj, col0, ga, gb, w8, small["conv_b"], name=f"conv_bwd{part}"))
        col0 += ga.shape[1]
    dxbc, dw8, gs_["conv_b"] = [jnp.concatenate(t, axis=1) for t in zip(*conv_parts)]
    gs_["conv_w"] = dw8[0:D_CONV]
    gs_["A_log"] = jnp.concatenate(das, axis=0) * a_neg
    ddt = jnp.pad(jnp.concatenate(ddts, axis=1), ((0, 0), (0, 128 - 2 * SSD_HEADS)))
    ddt_raw, dbias = _dt_bwd(ddt, dt_raw, bias_row, name="dt_bwd")
    gs_["dt_bias"] = dbias[:, 0:2 * SSD_HEADS].reshape(2, SSD_HEADS)

    dqt, dk_h, dv_h = _flash_bwd(qkt, k_h, v_h, dot.reshape(N_Q_HEADS, HEAD_DIM, s), ot, lse, name="flash_bwd")
    dqkt = jnp.concatenate([dqt.reshape(N_Q_HEADS * HEAD_DIM, s),
                            dk_h.transpose(0, 2, 1).reshape(N_KV_HEADS * HEAD_DIM, s)], axis=0)
    dqk_u, dqk_w = _qk_bwd(dqkt, proj, qk_w, qk_sc, tabs, name="qk_bwd")
    gs_["q_norm_w"] = dqk_w[:, 0:N_Q_HEADS * HEAD_DIM].reshape(N_Q_HEADS, HEAD_DIM).sum(axis=0, keepdims=True)
    gs_["k_norm_w"] = dqk_w[:, N_Q_HEADS * HEAD_DIM:].reshape(N_KV_HEADS, HEAD_DIM).sum(axis=0, keepdims=True)
    dv = _unheads(dv_h).astype(MMD)

    dproj = jnp.concatenate([dz, dga, dgs, dxbc, dqk_u, dv, ddt_raw], axis=1)
    gw["w_in_p"] = _mm_tn(h1, dproj, name="in_proj_dw", tk=512, tn=2944, tmm=2048, vmem=VMEM_BIG)
    zero_row = jnp.zeros((1, d), F32) if in_grad is None else jnp.zeros((1, d), F32) + in_grad(gw["w_in_p"])[0:1, 0:1]
    dh1 = _mm(dproj, wts["w_in_p"], name="in_proj_dx", outs=[F32], nt=True, tm=256, tn=1024, vmem=VMEM_BIG,
              extras=[(zero_row, "row", 0)], epi=lambda acc, r: (acc + r,))
    grad_x, dshift1, dscale1, gs_["norm1_w"] = _ln_mod_bwd(dh1, x, small["norm1_w"], scale1, dx1, name="ln1_bwd")
    dmod = jnp.concatenate([dshift1, dscale1, dgate1, dshift2, dscale2, dgate2], axis=0)
    return loss, grad_x, dmod, gw, gs_


N_DEV = 8
N_CHIP = 4
ANY = pl.BlockSpec(memory_space=pl.ANY)


def _place():
    return lax.axis_index("x"), lax.axis_index("y"), lax.axis_index("c")


def _allgather8(v, *, name):
    m_per, n = v.shape

    def body(x_ref, out_ref, send_sems, recv_sems, local_sem):
        x, y, c = _place()
        me, sibling = (x, y, c), (x, y, 1 - c)
        chips = [(1 - x, y), (x, 1 - y), (1 - x, 1 - y)]

        def rows(px, py, pc):
            return out_ref.at[pl.ds((4 * px + 2 * py + pc) * m_per, m_per), :]

        def copy(k, block, to, src=None):
            return pltpu.make_async_remote_copy(
                src_ref=rows(*block) if src is None else src, dst_ref=rows(*block),
                send_sem=send_sems.at[k], recv_sem=recv_sems.at[k], device_id=to, device_id_type=MESH)

        mine = pltpu.make_async_copy(x_ref, rows(*me), local_sem)
        mine.start()
        first = [copy(0, me, sibling, src=x_ref)]
        first += [copy(1 + j, me, (*chip, c), src=x_ref) for j, chip in enumerate(chips)]
        for cp in first:
            cp.start()
        passed = [copy(4 + j, (*chip, c), sibling) for j, chip in enumerate(chips)]
        for j, chip in enumerate(chips):
            copy(1 + j, (*chip, c), me).wait_recv()
            passed[j].start()
        copy(0, sibling, me).wait_recv()
        for j, chip in enumerate(chips):
            copy(4 + j, (*chip, 1 - c), me).wait_recv()
        for cp in first + passed:
            cp.wait_send()
        mine.wait()

    return pl.pallas_call(
        body, name=name, out_shape=jax.ShapeDtypeStruct((N_DEV * m_per, n), v.dtype),
        in_specs=[pl.BlockSpec(memory_space=pltpu.VMEM)], out_specs=pl.BlockSpec(memory_space=pltpu.VMEM),
        scratch_shapes=[pltpu.SemaphoreType.DMA((7,)), pltpu.SemaphoreType.DMA((7,)), pltpu.SemaphoreType.DMA],
    )(v)


HBM = pl.BlockSpec(memory_space=pltpu.HBM)
SEM = pl.BlockSpec(memory_space=pltpu.SEMAPHORE)


def _chips_copies(x_ref, land_ref, sems, scatter):
    x, y, c = _place()
    k = 2 * x + y
    chips = [(1 - x, y), (x, 1 - y), (1 - x, 1 - y)]
    ids = [2 * cx + cy for cx, cy in chips]

    def copy(j, slot):
        return pltpu.make_async_remote_copy(
            src_ref=x_ref.at[ids[j]] if scatter else x_ref, dst_ref=land_ref.at[slot], send_sem=sems[j],
            recv_sem=sems[3 + j], device_id=(*chips[j], c), device_id_type=MESH)

    return [copy(j, k) for j in range(3)], [copy(j, ids[j]) for j in range(3)]


def _chips_start(src, scatter, *, name):
    shape = src.shape if scatter else (N_CHIP,) + tuple(src.shape)

    def body(x_ref, land_ref, *rest):
        sems, token = rest[0:6], rest[8]
        for cp in _chips_copies(x_ref, land_ref, sems, scatter)[0]:
            cp.start()
        token[...] = jnp.zeros_like(token)

    out = pl.pallas_call(
        body, name=name,
        out_shape=(pltpu.SemaphoreType.DMA(()),) * 6 + (pltpu.HBM(src.shape, src.dtype), pltpu.HBM(shape, src.dtype),
                                                       jax.ShapeDtypeStruct((8, 128), F32)),
        in_specs=(HBM, HBM), out_specs=(SEM,) * 6 + (HBM, HBM, pl.BlockSpec(memory_space=pltpu.VMEM)),
        input_output_aliases={0: 6, 1: 7},
        compiler_params=pltpu.CompilerParams(has_side_effects=pltpu.SideEffectType.DATAFLOW_SIDE_EFFECTING),
    )(pltpu.with_memory_space_constraint(src, pltpu.HBM),
      pltpu.with_memory_space_constraint(lax.empty(shape, src.dtype), pltpu.HBM))
    return out[0:6], out[6], out[7], out[8]


def _chips_wait(sems, src, land, after, scatter, *, name):
    def body(x_ref, land_ref, *rest):
        sems_ = rest[0:6]
        for cp in _chips_copies(x_ref, land_ref, sems_, scatter)[1]:
            cp.wait_send()
            cp.wait_recv()

    return pl.pallas_call(
        body, name=name, out_shape=(pltpu.HBM(src.shape, src.dtype), pltpu.HBM(land.shape, land.dtype)),
        in_specs=(HBM, HBM) + (SEM,) * 6 + (ANY,), out_specs=(HBM, HBM), input_output_aliases={0: 0, 1: 1},
        compiler_params=pltpu.CompilerParams(has_side_effects=pltpu.SideEffectType.DATAFLOW_SIDE_EFFECTING),
    )(src, land, *sems, after)


def _row_tile(r, pref=512):
    return max(t for t in range(16, pref + 1, 16) if r % t == 0)


def _gather_weights(src, *, name):
    r = src.shape[0]
    hr = r // 2
    assert r == 2 * hr and hr % 16 == 0

    def body(x_ref, out_ref, send_sems, recv_sems):
        x, y, c = _place()
        k = 2 * x + y
        chips = [(1 - x, y), (x, 1 - y), (1 - x, 1 - y)]
        ids = [2 * cx + cy for cx, cy in chips]
        mine_rows = pl.ds(pl.multiple_of(c * hr, 16), hr)
        other_rows = pl.ds(pl.multiple_of((1 - c) * hr, 16), hr)

        def copy(sem, src_ref, slot, rows, to):
            return pltpu.make_async_remote_copy(
                src_ref=src_ref, dst_ref=out_ref.at[slot, rows], send_sem=send_sems.at[sem], recv_sem=recv_sems.at[sem],
                device_id=to, device_id_type=MESH)

        sends = [copy(j, x_ref.at[mine_rows], k, mine_rows, (cx, cy, c)) for j, (cx, cy) in enumerate(chips)]
        for cp in sends:
            cp.start()
        passed = [copy(3 + j, out_ref.at[ids[j], mine_rows], ids[j], mine_rows, (x, y, 1 - c)) for j in range(3)]
        for j, (cx, cy) in enumerate(chips):
            copy(j, x_ref.at[mine_rows], ids[j], mine_rows, (cx, cy, c)).wait_recv()
            passed[j].start()
        for j in range(3):
            copy(3 + j, out_ref.at[ids[j], other_rows], ids[j], other_rows, (x, y, 1 - c)).wait_recv()
        for cp in sends + passed:
            cp.wait_send()

    return pl.pallas_call(
        body, name=name, out_shape=jax.ShapeDtypeStruct((N_CHIP,) + tuple(src.shape), src.dtype),
        in_specs=[ANY], out_specs=ANY,
        scratch_shapes=[pltpu.SemaphoreType.DMA((6,)), pltpu.SemaphoreType.DMA((6,))],
    )(src)


def _pair_swap(a, *, name):
    n, r, cols = a.shape
    hr = r // 2

    def body(x_ref, out_ref, send_sem, recv_sem):
        x, y, c = _place()
        other_rows = pl.ds(pl.multiple_of((1 - c) * hr, 16), hr)
        cp = pltpu.make_async_remote_copy(src_ref=x_ref.at[:, other_rows], dst_ref=out_ref, send_sem=send_sem,
                                          recv_sem=recv_sem, device_id=(x, y, 1 - c), device_id_type=MESH)
        cp.start()
        cp.wait()

    return pl.pallas_call(
        body, name=name, out_shape=jax.ShapeDtypeStruct((n, hr, cols), a.dtype), in_specs=[ANY], out_specs=ANY,
        scratch_shapes=[pltpu.SemaphoreType.DMA, pltpu.SemaphoreType.DMA],
    )(a)


def _sibling_copy(a, *, name):
    def body(x_ref, out_ref, send_sem, recv_sem):
        x, y, c = _place()
        cp = pltpu.make_async_remote_copy(src_ref=x_ref, dst_ref=out_ref, send_sem=send_sem, recv_sem=recv_sem,
                                          device_id=(x, y, 1 - c), device_id_type=MESH)
        cp.start()
        cp.wait()

    return pl.pallas_call(
        body, name=name, out_shape=jax.ShapeDtypeStruct(a.shape, a.dtype), in_specs=[ANY], out_specs=ANY,
        scratch_shapes=[pltpu.SemaphoreType.DMA, pltpu.SemaphoreType.DMA],
    )(a)


def _sum_slots(a, *, name):
    _, r, c = a.shape
    tr = _row_tile(r, 256)

    def body(a_ref, o_ref):
        acc = a_ref[0].astype(F32)
        for j in range(1, N_CHIP):
            acc = acc + a_ref[j].astype(F32)
        o_ref[...] = acc

    return pl.pallas_call(
        body, name=name, grid=(r // tr,), in_specs=[pl.BlockSpec((N_CHIP, tr, c), lambda i: (0, i, 0))],
        out_specs=pl.BlockSpec((tr, c), lambda i: (i, 0)), out_shape=jax.ShapeDtypeStruct((r, c), F32),
        compiler_params=_cp(("arbitrary",)),
    )(a)


def _add2(a, b, *, name):
    r, c = a.shape
    tr = _row_tile(r)

    def body(a_ref, b_ref, o_ref):
        o_ref[...] = (a_ref[...].astype(F32) + b_ref[...].astype(F32)).astype(o_ref.dtype)

    spec = pl.BlockSpec((tr, c), lambda i: (i, 0))
    return pl.pallas_call(
        body, name=name, grid=(r // tr,), in_specs=[spec, spec], out_specs=spec,
        out_shape=jax.ShapeDtypeStruct((r, c), a.dtype), compiler_params=_cp(("arbitrary",)),
    )(a, b)


BIG = ("w_in", "w_mlp1", "w_attn_out", "w_ssd_out", "w_o", "w_mlp2")
COL_SHARDED = ("w_mlp1", "w_in")
ROW_SHARDED = ("w_attn_out", "w_ssd_out", "w_o", "w_mlp2")
LATE = ROW_SHARDED + ("w_mlp1",)
SMALL = ("b_ada", "norm1_w", "norm2_w", "q_norm_w", "k_norm_w", "conv_b", "A_log", "dt_bias", "ssd_D", "ssd_norm_w")
NAMES = ("w_ada", "b_ada", "norm1_w", "norm2_w", "w_in", "q_norm_w", "k_norm_w", "conv_w", "conv_b", "A_log", "dt_bias",
         "ssd_D", "ssd_norm_w", "w_attn_out", "w_ssd_out", "w_o", "w_mlp1", "w_mlp2")
W_IN_COLS = 8768


def _permute_in(w):
    return jnp.concatenate([w[:, 4608:6656], w[:, 6720:8768], w[:, 1536:4608], w[:, 0:1536], w[:, 6656:6720],
                            jnp.zeros((w.shape[0], PW - W_IN_COLS), w.dtype)], axis=1)


def _unpermute_in(wp):
    return jnp.concatenate([wp[:, Q0:DT0], wp[:, XS0:Q0], wp[:, Z0:GA0], wp[:, DT0:DT0 + 64], wp[:, GA0:XS0]], axis=1)


def _pad_to(v, n):
    return jnp.pad(v, (0, n - v.shape[0]))


def _step(w, m, v, loss_target):
    xi, yi, ci = _place()
    chip = 2 * xi + yi
    dev = 4 * xi + 2 * yi + ci
    x, tgt = w["x"], loss_target
    d = x.shape[1]

    cw = w["conv_w"].shape[1]
    v0 = _pad_to(jnp.concatenate([w["c"].reshape(-1), w["conv_w"].reshape(-1)]), 5120).reshape(8, 640)
    g0 = _allgather8(v0, name="ag_cond").reshape(N_DEV, 5120)
    c_all = g0[:, 0:d]
    conv_w = jnp.concatenate([g0[2 * k, d:d + D_CONV * cw].reshape(D_CONV, cw) for k in range(N_CHIP)], axis=1)
    sc = _silu_cast(c_all, name="silu_c")
    modp = _mm(sc, w["w_ada"].astype(MMD), name="ada_fwd", outs=[F32], tm=8, tn=512)
    g1 = _allgather8(modp, name="ag_mod").reshape(N_DEV, N_DEV, modp.shape[1])
    mod_all = jnp.concatenate([g1[2 * k] for k in range(N_CHIP)], axis=1)
    mod = (lax.dynamic_slice_in_dim(mod_all, dev, 1, axis=0) + w["b_ada"]).reshape(6, d)

    mine, mod = lax.optimization_barrier((w["w_in"].astype(MMD), mod))
    gath = lax.dynamic_update_slice_in_dim(_gather_weights(mine, name="ag_w_in"), mine[None], chip, axis=0)
    late_mine = jnp.concatenate([w[n].astype(MMD) for n in LATE], axis=0)
    late_mine, gath = lax.optimization_barrier((late_mine, gath))
    ag_sems, ag_src, ag_land, ag_token = _chips_start(late_mine, False, name="ag_late_start")
    mod = mod + ag_token[0:1, 0:1]
    w_in = jnp.concatenate([gath[k] for k in range(N_CHIP)], axis=1)
    wts = {"w_in_p": _permute_in(w_in), "w_dt": jnp.pad(w_in[:, 6656:6720], ((0, 0), (0, 64)))}
    small = {n: w[n] for n in SMALL if n != "b_ada"}
    small["conv_w"] = conv_w

    def own_slot(land, src):
        return lax.dynamic_update_slice_in_dim(land, src, chip, axis=0)

    def late_weights(after):
        src, land = _chips_wait(ag_sems, ag_src, ag_land, after, False, name="ag_late_wait")
        land = own_slot(land, src[None])
        out, o = {}, 0
        for n in LATE:
            rows = w[n].shape[0]
            part = land[:, o:o + rows]
            out[n] = (jnp.concatenate([part[k] for k in range(N_CHIP)], axis=1) if n in COL_SHARDED
                      else part.reshape(N_CHIP * rows, w[n].shape[1]))
            o += rows
        return out

    def pair_sums(slots, tag):
        _, rows, cols = slots.shape
        hr = rows // 2
        theirs = _pair_swap(slots, name="rs_pair_" + tag)
        ours = lax.dynamic_slice_in_dim(slots, ci * hr, hr, axis=1)
        pair = _add2(ours.reshape(N_CHIP * hr, cols), theirs.reshape(N_CHIP * hr, cols), name="rs_pair_sum_" + tag)
        return pair.reshape(N_CHIP, hr, cols)

    def finish(recv, pair, tag):
        recv = own_slot(recv, lax.dynamic_slice_in_dim(pair, chip, 1, axis=0))
        half = _sum_slots(recv, name="rs_sum_" + tag)
        other = _sibling_copy(half, name="rs_sibling_" + tag)
        return jnp.where(ci == 0, jnp.concatenate([half, other], axis=0), jnp.concatenate([other, half], axis=0))

    started = {}

    def late_grads(gw):
        slots = []
        for k in range(N_CHIP):
            parts = []
            for n in LATE:
                rows = w[n].shape[0]
                blk = gw[n][:, k * rows:(k + 1) * rows] if n in COL_SHARDED else gw[n][k * rows:(k + 1) * rows]
                parts.append(blk.astype(MMD))
            slots.append(jnp.concatenate(parts, axis=0))
        pair = pair_sums(jnp.stack(slots), "late")
        sems, src, land, token = _chips_start(pair, True, name="rs_late_start")
        started["late"] = (sems, src, land)
        return token[0:1, 0:1]

    def in_grad(g):
        g_in = _unpermute_in(g)
        cols_in = w["w_in"].shape[1]
        pair = pair_sums(jnp.stack([g_in[:, k * cols_in:(k + 1) * cols_in].astype(MMD) for k in range(N_CHIP)]), "w_in")
        sems, src, land, token = _chips_start(pair, True, name="rs_w_in_start")
        started["w_in"] = (sems, src, land)
        return token

    loss, grad_x, dmod, gw, gs = _local_step(x, tgt, mod, wts, small, late_weights, late_grads, in_grad)

    grads = {}
    pair, land = _chips_wait(*started["w_in"], grad_x, True, name="rs_w_in_wait")
    grads["w_in"] = finish(land, pair, "w_in")
    pair, land = _chips_wait(*started["late"], grad_x, True, name="rs_late_wait")
    total, o = finish(land, pair, "late"), 0
    for n in LATE:
        rows = w[n].shape[0]
        grads[n] = total[o:o + rows]
        o += rows

    order = ([dmod.reshape(-1)] + [gs[n].reshape(-1) for n in SMALL if n != "b_ada"] + [gs["conv_w"].reshape(-1)]
             + [loss.reshape(-1)])
    vec = jnp.concatenate(order)
    n_small = vec.shape[0]
    n_pad = -(-n_small // 1024) * 1024
    g2 = _allgather8(_pad_to(vec, n_pad).reshape(8, n_pad // 8), name="ag_small")
    tot = _rows_sum(g2, N_DEV, name="small_sum").reshape(-1)
    loss = tot[n_small - 1]
    dmod_all = g2.reshape(N_DEV, n_pad)[:, 0:6 * d]
    off = 0
    for n in SMALL:
        grads[n] = tot[off:off + w[n].size].reshape(w[n].shape)
        off += w[n].size
    conv_full = tot[off:off + D_CONV * N_CHIP * cw].reshape(D_CONV, N_CHIP * cw)
    grads["conv_w"] = lax.dynamic_slice_in_dim(conv_full, chip * cw, cw, axis=1)
    ada_cols = w["w_ada"].shape[1]
    dmod_mine = lax.dynamic_slice_in_dim(dmod_all, chip * ada_cols, ada_cols, axis=1).astype(MMD)
    grads["w_ada"] = _mm_tn(sc, dmod_mine, name="ada_dw", tk=512, tn=512, tmm=8)

    delta, new_m, new_v = {}, {}, {}
    pack = lambda t: jnp.concatenate([t[n].reshape(-1) for n in SMALL]).reshape(1, -1)
    ds_, ms_, vs_ = _adamw(pack(w), pack(grads), pack(m), pack(v), name="adamw_small")
    off = 0
    for n in SMALL:
        for dst, src in ((delta, ds_), (new_m, ms_), (new_v, vs_)):
            dst[n] = src[0, off:off + w[n].size].reshape(w[n].shape)
        off += w[n].size
    for n in ("w_ada", "conv_w") + BIG:
        delta[n], new_m[n], new_v[n] = _adamw(w[n], grads[n], m[n], v[n], name="adamw_" + n)
    return loss, grad_x, grads, delta, new_m, new_v


def kernel(x, c, w_ada, b_ada, norm1_w, norm2_w, w_in, q_norm_w, k_norm_w, conv_w, conv_b, A_log, dt_bias, ssd_D, ssd_norm_w, w_attn_out, w_ssd_out, w_o, w_mlp1, w_mlp2, loss_target, m_w_ada, m_b_ada, m_norm1_w, m_norm2_w, m_w_in, m_q_norm_w, m_k_norm_w, m_conv_w, m_conv_b, m_A_log, m_dt_bias, m_ssd_D, m_ssd_norm_w, m_w_attn_out, m_w_ssd_out, m_w_o, m_w_mlp1, m_w_mlp2, v_w_ada, v_b_ada, v_norm1_w, v_norm2_w, v_w_in, v_q_norm_w, v_k_norm_w, v_conv_w, v_conv_b, v_A_log, v_dt_bias, v_ssd_D, v_ssd_norm_w, v_w_attn_out, v_w_ssd_out, v_w_o, v_w_mlp1, v_w_mlp2):
    args = dict(locals())
    strip = lambda a: a[0] if a.ndim == 3 else a
    w = {n: strip(args[n]) for n in NAMES + ("x", "c")}
    m = {n: strip(args["m_" + n]) for n in NAMES}
    v = {n: strip(args["v_" + n]) for n in NAMES}
    loss, grad_x, grads, delta, new_m, new_v = _step(w, m, v, loss_target[0])
    like = lambda t, n: t.reshape(args[n].shape)
    return (loss, grad_x[None], *[like(grads[n], n) for n in NAMES], *[like(delta[n], n) for n in NAMES],
            *[like(new_m[n], n) for n in NAMES], *[like(new_v[n], n) for n in NAMES])
```

```python
import functools
import math

import jax
import jax.numpy as jnp
from jax import lax
from jax.experimental import pallas as pl
from jax.experimental.pallas import tpu as pltpu

F32 = jnp.float32
MMD = jnp.bfloat16
EPS = 1e-6
NEG = -1e30
MIB = 1024 * 1024
VMEM_BIG = 56 * MIB
VMEM_MID = 40 * MIB

GRID_W = 64
N_Q_HEADS, N_KV_HEADS, HEAD_DIM = 16, 4, 64
ROPE_THETA = 10000.0
SSD_HEADS, SSD_GROUPS, SSD_P, SSD_N, CHUNK = 32, 4, 64, 128, 128
HPG = SSD_HEADS // SSD_GROUPS
D_CONV = 5
ADAM_LR, ADAM_B1, ADAM_B2, ADAM_EPS, ADAM_WD, ADAM_STEP = 0.001, 0.9, 0.999, 1e-08, 0.01, 10

Z0, GA0, GS0, XS0, B0, C0, Q0, K0, V0, DT0, PW = 0, 2048, 3072, 4096, 6144, 6656, 7168, 8192, 8448, 8704, 8832

MESH = pl.DeviceIdType.MESH
NT = (((1,), (1,)), ((), ()))
TN = (((0,), (0,)), ((), ()))


def _cp(sem=None, vmem=VMEM_MID):
    return pltpu.CompilerParams(dimension_semantics=sem, vmem_limit_bytes=vmem)


def _tile(n, pref):
    t = min(n, pref)
    while n % t:
        t //= 2
    return t


def _dot(a, b, dims=None):
    if dims is None:
        return jnp.dot(a, b, preferred_element_type=F32)
    return lax.dot_general(a, b, dims, preferred_element_type=F32)


def _dot_hi(a, b):
    return jnp.dot(a, b, precision=lax.Precision.HIGHEST, preferred_element_type=F32)


def _sigmoid(x):
    return jax.nn.sigmoid(x)


def _mm(a, b, *, name, outs, nt=False, ta=False, extras=(), epi=None, tm=512, tn=512, n=None, b_outer=False,
        vmem=VMEM_MID):
    assert not (nt and ta)
    k, m = a.shape if ta else a.shape[::-1]
    if n is None:
        n = b.shape[0] if nt else b.shape[1]
    tm, tn = _tile(m, tm), _tile(n, tn)
    gi, gj = m // tm, n // tn
    if b_outer:
        grid = (gj, gi)
        ij = lambda p, q: (q, p)
    else:
        grid = (gi, gj)
        ij = lambda p, q: (p, q)
    if ta:
        a_spec = pl.BlockSpec((k, tm), lambda p, q: (0, ij(p, q)[0]))
    else:
        a_spec = pl.BlockSpec((tm, k), lambda p, q: (ij(p, q)[0], 0))
    if nt:
        b_spec = pl.BlockSpec((tn, k), lambda p, q: (ij(p, q)[1], 0))
    else:
        b_spec = pl.BlockSpec((k, tn), lambda p, q: (0, ij(p, q)[1]))
    e_specs = []
    for arr, kind, off in extras:
        ob = off // tn
        assert off % tn == 0
        if kind == "tile":
            e_specs.append(pl.BlockSpec((tm, tn), lambda p, q, ob=ob: (ij(p, q)[0], ob + ij(p, q)[1])))
        else:
            e_specs.append(pl.BlockSpec((1, tn), lambda p, q, ob=ob: (0, ob + ij(p, q)[1])))
    ne = len(extras)

    def body(a_ref, b_ref, *rest):
        acc = _dot(a_ref[...], b_ref[...], NT if nt else (TN if ta else None))
        res = epi(acc, *[e[...] for e in rest[:ne]]) if epi is not None else (acc,)
        for o_ref, r in zip(rest[ne:], res):
            o_ref[...] = r.astype(o_ref.dtype)

    out = pl.pallas_call(
        body, name=name, grid=grid,
        in_specs=[a_spec, b_spec] + e_specs,
        out_specs=[pl.BlockSpec((tm, tn), lambda p, q: ij(p, q)) for _ in outs],
        out_shape=[jax.ShapeDtypeStruct((m, n), dt) for dt in outs],
        compiler_params=_cp(("arbitrary", "arbitrary"), vmem),
    )(a, b, *[e[0] for e in extras])
    return out if len(outs) > 1 else out[0]


def _mm_tn(a, g, *, name, tk=512, tn=1024, tmm=4096, vmem=VMEM_MID):
    m, k = a.shape
    n = g.shape[1]
    tk, tn, tmm = _tile(k, tk), _tile(n, tn), _tile(m, tmm)

    def body(a_ref, g_ref, o_ref):
        p = _dot(a_ref[...], g_ref[...], TN)

        @pl.when(pl.program_id(2) == 0)
        def _():
            o_ref[...] = p

        @pl.when(pl.program_id(2) > 0)
        def _():
            o_ref[...] += p

    return pl.pallas_call(
        body, name=name, grid=(k // tk, n // tn, m // tmm),
        in_specs=[pl.BlockSpec((tmm, tk), lambda i, j, r: (r, i)), pl.BlockSpec((tmm, tn), lambda i, j, r: (r, j))],
        out_specs=pl.BlockSpec((tk, tn), lambda i, j, r: (i, j)),
        out_shape=jax.ShapeDtypeStruct((k, n), F32),
        compiler_params=_cp(("arbitrary", "arbitrary", "arbitrary"), vmem),
    )(a, g)


def _adamw(w, g, m, v, *, name):
    r, c = w.shape
    tr = _tile(r, 256) if r % 8 == 0 else r

    def body(w_ref, g_ref, m_ref, v_ref, d_ref, nm_ref, nv_ref):
        gg = g_ref[...]
        nm = ADAM_B1 * m_ref[...] + (1.0 - ADAM_B1) * gg
        nv = ADAM_B2 * v_ref[...] + (1.0 - ADAM_B2) * jnp.square(gg)
        m_hat = nm / (1.0 - ADAM_B1 ** ADAM_STEP)
        v_hat = nv / (1.0 - ADAM_B2 ** ADAM_STEP)
        d_ref[...] = -ADAM_LR * (m_hat / (jnp.sqrt(v_hat) + ADAM_EPS) + ADAM_WD * w_ref[...])
        nm_ref[...] = nm
        nv_ref[...] = nv

    spec = pl.BlockSpec((tr, c), lambda i: (i, 0))
    return pl.pallas_call(
        body, name=name, grid=(r // tr,), in_specs=[spec] * 4, out_specs=[spec] * 3,
        out_shape=[jax.ShapeDtypeStruct((r, c), F32)] * 3, compiler_params=_cp(("arbitrary",)),
    )(w, g, m, v)


def _rows_sum(a, groups, *, name):
    r = a.shape[0] // groups

    def body(a_ref, o_ref):
        acc = a_ref[0:r, :]
        for d in range(1, groups):
            acc = acc + a_ref[d * r:(d + 1) * r, :]
        o_ref[...] = acc

    return pl.pallas_call(body, name=name, out_shape=jax.ShapeDtypeStruct((r, a.shape[1]), F32))(a)


def _silu_cast(a, *, name):
    def body(a_ref, o_ref):
        x = a_ref[...]
        o_ref[...] = (x * _sigmoid(x)).astype(o_ref.dtype)

    return pl.pallas_call(body, name=name, out_shape=jax.ShapeDtypeStruct(a.shape, MMD))(a)


def _sumsq(a, *, name):
    m, n = a.shape
    tm = _tile(m, 512)

    def body(a_ref, o_ref):
        x = a_ref[...]
        p = jnp.sum(jnp.sum(x * x, axis=1, keepdims=True), axis=0, keepdims=True)

        @pl.when(pl.program_id(0) == 0)
        def _():
            o_ref[...] = p

        @pl.when(pl.program_id(0) > 0)
        def _():
            o_ref[...] += p

    return pl.pallas_call(
        body, name=name, grid=(m // tm,), in_specs=[pl.BlockSpec((tm, n), lambda i: (i, 0))],
        out_specs=pl.BlockSpec((1, 1), lambda i: (0, 0)), out_shape=jax.ShapeDtypeStruct((1, 1), F32),
        compiler_params=_cp(("arbitrary",)),
    )(a)


def _acc_rows(o_ref, p, first):
    @pl.when(first)
    def _():
        o_ref[...] = p

    @pl.when(jnp.logical_not(first))
    def _():
        o_ref[...] += p


def _ln_mod(x, w, scale, shift, *, name):
    s, d = x.shape
    tm = _tile(s, 512)

    def body(x_ref, w_ref, sc_ref, sh_ref, o_ref):
        xv = x_ref[...]
        r = lax.rsqrt(jnp.mean(xv * xv, axis=-1, keepdims=True) + EPS)
        o_ref[...] = ((xv * r) * w_ref[...] * (1.0 + sc_ref[...]) + sh_ref[...]).astype(o_ref.dtype)

    row = pl.BlockSpec((1, d), lambda i: (0, 0))
    big = pl.BlockSpec((tm, d), lambda i: (i, 0))
    return pl.pallas_call(
        body, name=name, grid=(s // tm,), in_specs=[big, row, row, row], out_specs=big,
        out_shape=jax.ShapeDtypeStruct((s, d), MMD), compiler_params=_cp(("arbitrary",)),
    )(x, w, scale, shift)


def _ln_mod_bwd(dh, x, w, scale, dres, *, name):
    s, d = x.shape
    tm = _tile(s, 512)

    def body(dh_ref, x_ref, w_ref, sc_ref, dres_ref, dx_ref, dsh_ref, dsc_ref, dw_ref):
        xv = x_ref[...]
        dhv = dh_ref[...].astype(F32)
        r = lax.rsqrt(jnp.mean(xv * xv, axis=-1, keepdims=True) + EPS)
        nv = xv * r
        wv = w_ref[...]
        g1 = 1.0 + sc_ref[...]
        dn = dhv * (wv * g1)
        dx_ref[...] = dres_ref[...] + r * (dn - nv * jnp.mean(dn * nv, axis=-1, keepdims=True))
        first = pl.program_id(0) == 0
        _acc_rows(dsh_ref, jnp.sum(dhv, axis=0, keepdims=True), first)
        _acc_rows(dsc_ref, jnp.sum(dhv * nv * wv, axis=0, keepdims=True), first)
        _acc_rows(dw_ref, jnp.sum(dhv * nv * g1, axis=0, keepdims=True), first)

    row = pl.BlockSpec((1, d), lambda i: (0, 0))
    big = pl.BlockSpec((tm, d), lambda i: (i, 0))
    return pl.pallas_call(
        body, name=name, grid=(s // tm,), in_specs=[big, big, row, row, big], out_specs=[big, row, row, row],
        out_shape=[jax.ShapeDtypeStruct((s, d), F32)] + [jax.ShapeDtypeStruct((1, d), F32)] * 3,
        compiler_params=_cp(("arbitrary",)),
    )(dh, x, w, scale, dres)


def _gate_bwd(dy, u, gate, *, name):
    s, d = dy.shape
    tm = _tile(s, 512)

    def body(dy_ref, u_ref, g_ref, du_ref, dg_ref):
        dyv = dy_ref[...]
        du_ref[...] = (dyv * g_ref[...]).astype(du_ref.dtype)
        _acc_rows(dg_ref, jnp.sum(dyv * u_ref[...].astype(F32), axis=0, keepdims=True), pl.program_id(0) == 0)

    row = pl.BlockSpec((1, d), lambda i: (0, 0))
    big = pl.BlockSpec((tm, d), lambda i: (i, 0))
    return pl.pallas_call(
        body, name=name, grid=(s // tm,), in_specs=[big, big, row], out_specs=[big, row],
        out_shape=[jax.ShapeDtypeStruct((s, d), MMD), jax.ShapeDtypeStruct((1, d), F32)],
        compiler_params=_cp(("arbitrary",)),
    )(dy, u, gate)


def _seg64(v, e):
    hi = v.astype(jnp.bfloat16)
    lo = (v - hi.astype(F32)).astype(jnp.bfloat16)
    return _dot(hi, e) + _dot(lo, e)


def _rope_tables(s):
    rows = s // GRID_W
    pos_row = jnp.repeat(jnp.arange(rows, dtype=jnp.int32), GRID_W).astype(F32)
    pos_col = jnp.tile(jnp.arange(GRID_W, dtype=jnp.int32), rows).astype(F32)
    axis_dim = HEAD_DIM // 2
    inv_freq = ROPE_THETA ** (-jnp.arange(0, axis_dim, 2, dtype=F32) / axis_dim)
    ang_r = pos_row[:, None] * inv_freq[None, :]
    ang_c = pos_col[:, None] * inv_freq[None, :]
    zero = jnp.zeros_like(ang_r)
    cos = jnp.concatenate([jnp.cos(ang_r), jnp.cos(ang_r), jnp.cos(ang_c), jnp.cos(ang_c)], axis=1)
    s_a = jnp.concatenate([-jnp.sin(ang_r), zero, -jnp.sin(ang_c), zero], axis=1)
    s_b = jnp.concatenate([zero, jnp.sin(ang_r), zero, jnp.sin(ang_c)], axis=1)
    return [jnp.tile(t, (1, 2)) for t in (cos, s_a, s_b)]


def _e128():
    i = jnp.arange(128)
    return (i[:, None] // 64 == i[None, :] // 64).astype(jnp.bfloat16)


QKW = N_Q_HEADS * HEAD_DIM + N_KV_HEADS * HEAD_DIM


def _qk_fwd(proj, wrow, scrow, tabs, *, name):
    s = proj.shape[0]
    tm = _tile(s, 1024)

    def body(x_ref, w_ref, sc_ref, cos_ref, sa_ref, sb_ref, e_ref, o_ref, ot_ref):
        u = x_ref[...].astype(F32)
        r = lax.rsqrt(_seg64(u * u, e_ref[...]) * (1.0 / HEAD_DIM) + EPS)
        nv = (u * r) * w_ref[...]
        ro = nv * cos_ref[...] + pltpu.roll(nv, 112, 1) * sa_ref[...] + pltpu.roll(nv, 16, 1) * sb_ref[...]
        out = ro * sc_ref[...]
        o_ref[...] = out.astype(o_ref.dtype)
        ot_ref[...] = out.T.astype(ot_ref.dtype)

    tab = pl.BlockSpec((tm, 128), lambda i, j: (i, 0))
    row = pl.BlockSpec((1, 128), lambda i, j: (0, j))
    return pl.pallas_call(
        body, name=name, grid=(s // tm, QKW // 128),
        in_specs=[pl.BlockSpec((tm, 128), lambda i, j: (i, Q0 // 128 + j)), row, row, tab, tab, tab,
                  pl.BlockSpec((128, 128), lambda i, j: (0, 0))],
        out_specs=[pl.BlockSpec((tm, 128), lambda i, j: (i, j)), pl.BlockSpec((128, tm), lambda i, j: (j, i))],
        out_shape=[jax.ShapeDtypeStruct((s, QKW), MMD), jax.ShapeDtypeStruct((QKW, s), MMD)],
        compiler_params=_cp(("arbitrary", "arbitrary")),
    )(proj, wrow, scrow, *tabs, _e128())


def _qk_bwd(dqkt, proj, wrow, scrow, tabs, *, name):
    s = proj.shape[0]
    tm = _tile(s, 1024)

    def body(d_ref, x_ref, w_ref, sc_ref, cos_ref, sa_ref, sb_ref, e_ref, du_ref, dw_ref):
        e = e_ref[...]
        d = d_ref[...].T * sc_ref[...]
        dn = d * cos_ref[...] + pltpu.roll(d * sa_ref[...], 16, 1) + pltpu.roll(d * sb_ref[...], 112, 1)
        u = x_ref[...].astype(F32)
        r = lax.rsqrt(_seg64(u * u, e) * (1.0 / HEAD_DIM) + EPS)
        uh = u * r
        _acc_rows(dw_ref, jnp.sum(dn * uh, axis=0, keepdims=True), pl.program_id(1) == 0)
        dnw = dn * w_ref[...]
        du_ref[...] = (r * (dnw - uh * (_seg64(dnw * uh, e) * (1.0 / HEAD_DIM)))).astype(du_ref.dtype)

    tab = pl.BlockSpec((tm, 128), lambda j, i: (i, 0))
    row = pl.BlockSpec((1, 128), lambda j, i: (0, j))
    return pl.pallas_call(
        body, name=name, grid=(QKW // 128, s // tm),
        in_specs=[pl.BlockSpec((128, tm), lambda j, i: (j, i)), pl.BlockSpec((tm, 128), lambda j, i: (i, Q0 // 128 + j)),
                  row, row, tab, tab, tab, pl.BlockSpec((128, 128), lambda j, i: (0, 0))],
        out_specs=[pl.BlockSpec((tm, 128), lambda j, i: (i, j)), row],
        out_shape=[jax.ShapeDtypeStruct((s, QKW), MMD), jax.ShapeDtypeStruct((1, QKW), F32)],
        compiler_params=_cp(("arbitrary", "arbitrary")),
    )(dqkt, proj, wrow, scrow, *tabs, _e128())


REP = N_Q_HEADS // N_KV_HEADS


def _lanes(ref):
    return jnp.concatenate([ref[r] for r in range(REP)], axis=1)


V_AUG = HEAD_DIM + 8
LOG2E = math.log2(math.e)


def _flash_fwd(qkt, vta, *, name):
    s = qkt.shape[2]
    tq, tk = _tile(s, 1024), _tile(s, 512)
    nk = s // tk
    lanes = REP * tq

    def body(q_ref, k_ref, v_ref, o_ref, lse_ref, m_ref, acc_ref):
        j = pl.program_id(2)

        @pl.when(j == 0)
        def _():
            m_ref[...] = jnp.full_like(m_ref, NEG)
            acc_ref[...] = jnp.zeros_like(acc_ref)

        st = _dot(k_ref[0], _lanes(q_ref), TN)
        m_prev = m_ref[...]
        m_new = jnp.maximum(m_prev, jnp.max(st, axis=0, keepdims=True))
        p = jnp.exp2(st - m_new).astype(MMD)
        acc_ref[...] = jnp.exp2(m_prev - m_new) * acc_ref[...] + _dot(v_ref[0], p)
        m_ref[...] = m_new

        @pl.when(j == nk - 1)
        def _():
            acc = acc_ref[...]
            l = acc[HEAD_DIM:HEAD_DIM + 1]
            o = acc[0:HEAD_DIM] / l
            ls = m_ref[...] + jnp.log(l) * LOG2E
            for r in range(REP):
                o_ref[r] = o[:, r * tq:(r + 1) * tq].astype(o_ref.dtype)
                lse_ref[r] = ls[:, r * tq:(r + 1) * tq]

    qspec = pl.BlockSpec((REP, HEAD_DIM, tq), lambda g, i, j: (g, 0, i))
    return pl.pallas_call(
        body, name=name, grid=(N_KV_HEADS, s // tq, nk),
        in_specs=[qspec, pl.BlockSpec((1, HEAD_DIM, tk), lambda g, i, j: (N_Q_HEADS + g, 0, j)),
                  pl.BlockSpec((1, V_AUG, tk), lambda g, i, j: (g, 0, j))],
        out_specs=[qspec, pl.BlockSpec((REP, 1, tq), lambda g, i, j: (g, 0, i))],
        out_shape=[jax.ShapeDtypeStruct((N_Q_HEADS, HEAD_DIM, s), MMD), jax.ShapeDtypeStruct((N_Q_HEADS, 1, s), F32)],
        scratch_shapes=[pltpu.VMEM((1, lanes), F32), pltpu.VMEM((V_AUG, lanes), F32)],
        compiler_params=_cp(("arbitrary", "arbitrary", "arbitrary"), VMEM_BIG),
    )(qkt, qkt, vta)


def _flash_bwd(qkt, k_h, v_h, dot, ot, lse, *, name):
    s = qkt.shape[2]
    tq, tk = _tile(s, 512), _tile(s, 1024)
    nk = s // tk

    def body(q_ref, kt_ref, k_ref, v_ref, do_ref, o_ref, lse_ref, dq_ref, dk_ref, dv_ref, dq_acc):
        i, j = pl.program_id(1), pl.program_id(2)
        q, do = _lanes(q_ref), _lanes(do_ref)
        delta = jnp.sum(do.astype(F32) * _lanes(o_ref).astype(F32), axis=0, keepdims=True)
        k, v = k_ref[0], v_ref[0]
        p = jnp.exp2(_dot(k, q) - _lanes(lse_ref))
        dvc = _dot(p.astype(MMD), do, NT)
        ds = (p * (_dot(v, do) - delta)).astype(MMD)
        dkc = _dot(ds, q, NT) * (1.0 / LOG2E)
        dqc = _dot(kt_ref[0], ds)
        rows = pl.ds(pl.multiple_of(j * tk, tk), tk)

        @pl.when(i == 0)
        def _():
            dk_ref[0, rows, :] = dkc
            dv_ref[0, rows, :] = dvc

        @pl.when(i > 0)
        def _():
            dk_ref[0, rows, :] += dkc
            dv_ref[0, rows, :] += dvc

        @pl.when(j == 0)
        def _():
            dq_acc[...] = dqc

        @pl.when(j > 0)
        def _():
            dq_acc[...] += dqc

        @pl.when(j == nk - 1)
        def _():
            acc = dq_acc[...]
            for r in range(REP):
                dq_ref[r] = acc[:, r * tq:(r + 1) * tq]

    qspec = pl.BlockSpec((REP, HEAD_DIM, tq), lambda g, i, j: (g, 0, i))
    kvin = pl.BlockSpec((1, tk, HEAD_DIM), lambda g, i, j: (g, j, 0))
    kvres = pl.BlockSpec((1, s, HEAD_DIM), lambda g, i, j: (g, 0, 0))
    return pl.pallas_call(
        body, name=name, grid=(N_KV_HEADS, s // tq, nk),
        in_specs=[qspec, pl.BlockSpec((1, HEAD_DIM, tk), lambda g, i, j: (N_Q_HEADS + g, 0, j)), kvin, kvin,
                  qspec, qspec, pl.BlockSpec((REP, 1, tq), lambda g, i, j: (g, 0, i))],
        out_specs=[qspec, kvres, kvres],
        out_shape=[jax.ShapeDtypeStruct((N_Q_HEADS, HEAD_DIM, s), F32), jax.ShapeDtypeStruct((N_KV_HEADS, s, HEAD_DIM), F32),
                   jax.ShapeDtypeStruct((N_KV_HEADS, s, HEAD_DIM), F32)],
        scratch_shapes=[pltpu.VMEM((HEAD_DIM, REP * tq), F32)],
        compiler_params=_cp(("arbitrary", "arbitrary", "arbitrary"), VMEM_BIG),
    )(qkt, qkt, k_h, v_h, dot, ot, lse)


HALO = 8
CONV_W = 2048 + 2 * SSD_GROUPS * SSD_N


def _shifted(win, off, r):
    return pltpu.roll(win, (r + 2 * HALO - off) % (r + 2 * HALO), 0)[0:r]


def _conv_fwd(proj, w8, brow, *, name):
    s = proj.shape[0]
    cb = 256
    r = _tile(s, 512)

    def body(x_ref, w_ref, b_ref, o_ref, pad_ref):
        zeros = jnp.zeros((HALO, cb), F32)
        pad_ref[0:HALO, :] = zeros
        pad_ref[s + HALO:s + 2 * HALO, :] = zeros

        def fill(i, carry):
            st = pl.multiple_of(i * r, r)
            pad_ref[pl.ds(st + HALO, r), :] = x_ref[pl.ds(st, r), :].astype(F32)
            return carry

        lax.fori_loop(0, s // r, fill, 0)
        wv = w_ref[...]
        bv = b_ref[...]

        def step(i, carry):
            st = pl.multiple_of(i * r, r)
            win = pad_ref[pl.ds(st, r + 2 * HALO), :]
            acc = bv + wv[0:1, :] * _shifted(win, HALO - 2, r)
            for t in range(1, D_CONV):
                acc = acc + wv[t:t + 1, :] * _shifted(win, HALO - 2 + t, r)
            o_ref[pl.ds(st, r), :] = (acc * _sigmoid(acc)).astype(o_ref.dtype)
            return carry

        lax.fori_loop(0, s // r, step, 0)

    return pl.pallas_call(
        body, name=name, grid=(CONV_W // cb,),
        in_specs=[pl.BlockSpec((s, cb), lambda j: (0, XS0 // cb + j)), pl.BlockSpec((8, cb), lambda j: (0, j)),
                  pl.BlockSpec((1, cb), lambda j: (0, j))],
        out_specs=pl.BlockSpec((s, cb), lambda j: (0, j)),
        out_shape=jax.ShapeDtypeStruct((s, CONV_W), MMD),
        scratch_shapes=[pltpu.VMEM((s + 2 * HALO, cb), F32)],
        compiler_params=_cp(("arbitrary",), VMEM_MID),
    )(proj, w8, brow)


def _conv_bwd(proj, col0, ga, gb, w8, brow, *, name):
    s = proj.shape[0]
    width = ga.shape[1]
    cb = 128
    c0 = col0 // cb
    r = _tile(s, 512)

    def body(x_ref, ga_ref, gb_ref, w_ref, b_ref, dx_ref, dw_ref, db_ref, xpad, dpad):
        zeros = jnp.zeros((HALO, cb), F32)
        for ref in (xpad, dpad):
            ref[0:HALO, :] = zeros
            ref[s + HALO:s + 2 * HALO, :] = zeros

        def fill(i, carry):
            st = pl.multiple_of(i * r, r)
            xpad[pl.ds(st + HALO, r), :] = x_ref[pl.ds(st, r), :].astype(F32)
            return carry

        lax.fori_loop(0, s // r, fill, 0)
        wv = w_ref[...]
        bv = b_ref[...]

        def first(i, carry):
            st = pl.multiple_of(i * r, r)
            win = xpad[pl.ds(st, r + 2 * HALO), :]
            taps = [_shifted(win, HALO - 2 + t, r) for t in range(D_CONV)]
            u = bv
            for t in range(D_CONV):
                u = u + wv[t:t + 1, :] * taps[t]
            sg = _sigmoid(u)
            du = ((ga_ref[pl.ds(st, r), :].astype(F32) + gb_ref[pl.ds(st, r), :].astype(F32))
                  * (sg * (1.0 + u * (1.0 - sg))))
            dpad[pl.ds(st + HALO, r), :] = du
            out = [carry[0] + jnp.sum(du, axis=0, keepdims=True)]
            for t in range(D_CONV):
                out.append(carry[1 + t] + jnp.sum(du * taps[t], axis=0, keepdims=True))
            return tuple(out)

        sums = lax.fori_loop(0, s // r, first, tuple(jnp.zeros((1, cb), F32) for _ in range(1 + D_CONV)))
        db_ref[...] = sums[0]
        for t in range(D_CONV):
            dw_ref[t:t + 1, :] = sums[1 + t]
        dw_ref[D_CONV:8, :] = jnp.zeros((8 - D_CONV, cb), F32)

        def second(i, carry):
            st = pl.multiple_of(i * r, r)
            win = dpad[pl.ds(st, r + 2 * HALO), :]
            acc = wv[0:1, :] * _shifted(win, HALO + 2, r)
            for t in range(1, D_CONV):
                acc = acc + wv[t:t + 1, :] * _shifted(win, HALO + 2 - t, r)
            dx_ref[pl.ds(st, r), :] = acc.astype(dx_ref.dtype)
            return carry

        lax.fori_loop(0, s // r, second, 0)

    col = pl.BlockSpec((s, cb), lambda j: (0, j))
    return pl.pallas_call(
        body, name=name, grid=(width // cb,),
        in_specs=[pl.BlockSpec((s, cb), lambda j: (0, XS0 // cb + c0 + j)), col, col,
                  pl.BlockSpec((8, cb), lambda j: (0, c0 + j)), pl.BlockSpec((1, cb), lambda j: (0, c0 + j))],
        out_specs=[col, pl.BlockSpec((8, cb), lambda j: (0, j)), pl.BlockSpec((1, cb), lambda j: (0, j))],
        out_shape=[jax.ShapeDtypeStruct((s, width), MMD), jax.ShapeDtypeStruct((8, width), F32),
                   jax.ShapeDtypeStruct((1, width), F32)],
        scratch_shapes=[pltpu.VMEM((s + 2 * HALO, cb), F32), pltpu.VMEM((s + 2 * HALO, cb), F32)],
        compiler_params=_cp(("arbitrary",), VMEM_BIG),
    )(proj, ga, gb, w8, brow)


def _tri(lower):
    i = jnp.arange(CHUNK)
    return ((i[:, None] >= i[None, :]) if lower else (i[:, None] <= i[None, :])).astype(F32)


def _dt_fwd(raw, bias, arow, *, name):
    s = raw.shape[0]

    def body(r_ref, b_ref, a_ref, lo_ref, up_ref, dt_ref, cs_ref):
        u = r_ref[...] + b_ref[...]
        dt = jnp.maximum(u, 0.0) + jnp.log1p(jnp.exp(-jnp.abs(u)))
        dt_ref[...] = dt
        a = dt * a_ref[...]
        lane = lax.broadcasted_iota(jnp.int32, (CHUNK, 128), 1)
        cs_ref[...] = jnp.where(lane < SSD_HEADS, _dot_hi(lo_ref[...], a), _dot_hi(up_ref[...], a))

    blk = pl.BlockSpec((CHUNK, 128), lambda i: (i, 0))
    row = pl.BlockSpec((1, 128), lambda i: (0, 0))
    tri = pl.BlockSpec((CHUNK, CHUNK), lambda i: (0, 0))
    return pl.pallas_call(
        body, name=name, grid=(s // CHUNK,), in_specs=[blk, row, row, tri, tri], out_specs=[blk, blk],
        out_shape=[jax.ShapeDtypeStruct((s, 128), F32)] * 2, compiler_params=_cp(("arbitrary",)),
    )(raw, bias, arow, _tri(True), _tri(False))


def _dt_bwd(ddt, raw, bias, *, name):
    s = raw.shape[0]
    tm = _tile(s, 1024)

    def body(d_ref, r_ref, b_ref, o_ref, db_ref):
        g = d_ref[...] * _sigmoid(r_ref[...] + b_ref[...])
        o_ref[...] = g.astype(o_ref.dtype)
        _acc_rows(db_ref, jnp.sum(g, axis=0, keepdims=True), pl.program_id(0) == 0)

    blk = pl.BlockSpec((tm, 128), lambda i: (i, 0))
    row = pl.BlockSpec((1, 128), lambda i: (0, 0))
    return pl.pallas_call(
        body, name=name, grid=(s // tm,), in_specs=[blk, blk, row], out_specs=[blk, row],
        out_shape=[jax.ShapeDtypeStruct((s, 128), MMD), jax.ShapeDtypeStruct((1, 128), F32)],
        compiler_params=_cp(("arbitrary",)),
    )(ddt, raw, bias)


GW = HPG * SSD_P


GPS = 4


def _ssd_specs(nc, rev):
    cc = (lambda c: nc - 1 - c) if rev else (lambda c: c)
    nb = SSD_GROUPS // GPS
    return dict(
        x=pl.BlockSpec((CHUNK, GPS * GW), lambda g, c: (cc(c), g)),
        b=pl.BlockSpec((CHUNK, GPS * SSD_N), lambda g, c: (cc(c), 2048 // (GPS * SSD_N) + g)),
        c=pl.BlockSpec((CHUNK, GPS * SSD_N), lambda g, c: (cc(c), 2048 // (GPS * SSD_N) + nb + g)),
        col=pl.BlockSpec((GPS, CHUNK, HPG), lambda g, c: (g, cc(c), 0)),
        lanes=pl.BlockSpec((CHUNK, 128), lambda g, c: (cc(c), 0)),
        rowt=pl.BlockSpec((GPS, 1, HPG, CHUNK), lambda g, c: (g, cc(c), 0, 0)),
        drow=pl.BlockSpec((1, GPS * GW), lambda g, c: (0, g)),
        y=pl.BlockSpec((CHUNK, GPS * GW), lambda g, c: (cc(c), g)),
        h=pl.BlockSpec((GPS, 1, SSD_N, GW), lambda g, c: (g, cc(c), 0, 0)),
        n=pl.BlockSpec((CHUNK, GPS * SSD_N), lambda g, c: (cc(c), g)),
    )


def _ssd_mask(anti):
    ii = lax.broadcasted_iota(jnp.int32, (CHUNK, CHUNK), 0)
    jj = lax.broadcasted_iota(jnp.int32, (CHUNK, CHUNK), 1)
    return ii, jj, (ii <= jj) if anti else (ii >= jj)


def _expand(x, ex):
    h1 = x.astype(jnp.bfloat16)
    r1 = x - h1.astype(F32)
    h2 = r1.astype(jnp.bfloat16)
    h3 = (r1 - h2.astype(F32)).astype(jnp.bfloat16)
    return _dot(h1, ex) + _dot(h2, ex) + _dot(h3, ex)


def _headsum(a, e):
    hi = a.astype(jnp.bfloat16)
    return _dot(hi, e) + _dot((a - hi.astype(F32)).astype(jnp.bfloat16), e)


def _expand_mats():
    lane = jnp.arange(128)[None, :, None]
    col = jnp.arange(GW)[None, None, :]
    base = (jnp.arange(2)[:, None] * SSD_HEADS + jnp.arange(SSD_GROUPS)[None, :] * HPG).reshape(2 * SSD_GROUPS, 1, 1)
    return (lane == base + col // SSD_P).astype(jnp.bfloat16)


def _headsum_mats():
    e1 = (jnp.arange(GW)[:, None] // SSD_P == jnp.arange(128)[None, :]).astype(jnp.bfloat16)
    e2 = (jnp.arange(HPG * CHUNK)[:, None] // CHUNK == jnp.arange(128)[None, :]).astype(jnp.bfloat16)
    return e1, e2


def _ssd_fwd(xc, dt, cs, cst, ex, drow, di, *, name):
    s = xc.shape[0]
    nc = s // CHUNK
    anti = di == 1
    sp = _ssd_specs(nc, anti)
    trow = 0 if anti else CHUNK - 1

    def body(x_ref, b_ref, c_ref, dt_ref, cs_ref, cst_ref, ex_ref, d_ref, y_ref, hp_ref, h_ref):
        @pl.when(pl.program_id(1) == 0)
        def _():
            h_ref[...] = jnp.zeros_like(h_ref)

        mask = _ssd_mask(anti)[2]
        dtv, csv = dt_ref[...], cs_ref[...]
        for gi in range(GPS):
            cols = slice(gi * GW, (gi + 1) * GW)
            ncols = slice(gi * SSD_N, (gi + 1) * SSD_N)
            ex = ex_ref[gi]
            xb = x_ref[:, cols].astype(F32)
            bm, cm = b_ref[:, ncols], c_ref[:, ncols]
            csr = cst_ref[gi, 0]
            dtf = _expand(dtv, ex)
            csf = _expand(csv, ex)
            tl = csf[trow:trow + 1, :]
            h = h_ref[gi]
            hp_ref[gi, 0] = h.astype(hp_ref.dtype)
            g = _dot(cm, bm, NT)
            xs = xb * dtf
            xsm = xs.astype(MMD)
            base = jnp.exp(csf) * _dot(cm, h.astype(MMD)) + d_ref[:, cols] * xb
            for r in range(HPG):
                sl = slice(r * SSD_P, (r + 1) * SSD_P)
                lm = jnp.exp(jnp.where(mask, csf[:, r * SSD_P:r * SSD_P + 1] - csr[r:r + 1, :], NEG))
                y_ref[:, gi * GW + r * SSD_P:gi * GW + (r + 1) * SSD_P] = (
                    _dot((g * lm).astype(MMD), xsm[:, sl]) + base[:, sl]).astype(y_ref.dtype)
            xd = (xs * jnp.exp(tl - csf)).astype(MMD)
            h_ref[gi] = h * jnp.exp(tl) + _dot(bm, xd, TN)

    nb = SSD_GROUPS // GPS
    return pl.pallas_call(
        body, name=name, grid=(nb, nc),
        in_specs=[sp["x"], sp["b"], sp["c"], sp["lanes"], sp["lanes"], sp["rowt"],
                  pl.BlockSpec((GPS, 128, GW), lambda g, c: (di * nb + g, 0, 0)), sp["drow"]],
        out_specs=[sp["y"], sp["h"]],
        out_shape=[jax.ShapeDtypeStruct((s, 2048), MMD), jax.ShapeDtypeStruct((SSD_GROUPS, nc, SSD_N, GW), MMD)],
        scratch_shapes=[pltpu.VMEM((GPS, SSD_N, GW), F32)],
        compiler_params=_cp(("arbitrary", "arbitrary")),
    )(xc, xc, xc, dt, cs, cst, ex, drow)


def _ssd_bwd(xc, dt, cs, dt4, cst, ex, drow, arow4, dy, hprev, di, *, name):
    s = xc.shape[0]
    nc = s // CHUNK
    anti = di == 1
    sp = _ssd_specs(nc, not anti)
    trow = 0 if anti else CHUNK - 1
    e1, e2 = _headsum_mats()

    def body(x_ref, b_ref, c_ref, dt_ref, cs_ref, dt4_ref, cst_ref, ex_ref, d_ref, a_ref, dy_ref, hp_ref, tri_ref,
             e1_ref, e2_ref, dx_ref, db_ref, dc_ref, ddt_ref, da_ref, dh_ref, w_ref, dxs_ref):
        @pl.when(pl.program_id(1) == 0)
        def _():
            dh_ref[...] = jnp.zeros_like(dh_ref)
            da_ref[...] = jnp.zeros_like(da_ref)

        e1v = e1_ref[...]
        ii, _, mask = _ssd_mask(anti)
        dtv, csv = dt_ref[...], cs_ref[...]
        for gi in range(GPS):
            cols = slice(gi * GW, (gi + 1) * GW)
            ncols = slice(gi * SSD_N, (gi + 1) * SSD_N)
            ex = ex_ref[gi]
            xb = x_ref[:, cols].astype(F32)
            bm, cm = b_ref[:, ncols], c_ref[:, ncols]
            csr = cst_ref[gi, 0]
            dym = dy_ref[:, cols]
            dyb = dym.astype(F32)
            hpm = hp_ref[gi, 0]
            hp = hpm.astype(F32)
            dh = dh_ref[gi]
            dhm = dh.astype(MMD)
            dtf = _expand(dtv, ex)
            csf = _expand(csv, ex)
            tl = csf[trow:trow + 1, :]
            e = jnp.exp(csf)
            dec = jnp.exp(tl - csf)
            et = jnp.exp(tl)
            xs = xb * dtf
            xsm = xs.astype(MMD)
            g = _dot(cm, bm, NT)
            z = _dot(cm, hpm)
            bdh = _dot(bm, dhm)
            dg = jnp.zeros((CHUNK, CHUNK), F32)
            wcols = jnp.zeros((CHUNK, CHUNK), F32)
            for r in range(HPG):
                sl = slice(r * SSD_P, (r + 1) * SSD_P)
                lm = jnp.exp(jnp.where(mask, csf[:, r * SSD_P:r * SSD_P + 1] - csr[r:r + 1, :], NEG))
                mm = g * lm
                dm = _dot(dym[:, sl], xsm[:, sl], NT)
                w = dm * mm
                w_ref[gi, :, r * CHUNK:(r + 1) * CHUNK] = w
                wcols = jnp.where(ii == r, jnp.sum(w, axis=0, keepdims=True), wcols)
                dg = dg + dm * lm
                dxs_ref[gi, :, sl] = _dot(mm.astype(MMD), dym[:, sl], TN)
            dxs = dxs_ref[gi] + dec * bdh
            dx_ref[:, cols] = (dxs * dtf + d_ref[:, cols] * dyb).astype(dx_ref.dtype)
            tb = xs * bdh * dec
            d_tot = jnp.sum(tb, axis=0, keepdims=True) + et * jnp.sum(dh * hp, axis=0, keepdims=True)
            d_tot = _headsum(jnp.broadcast_to(d_tot, (8, GW)), e1v)[0:1]
            dcs = (_headsum(dyb * (e * z) - tb, e1v) + _headsum(w_ref[gi], e2_ref[...]) - wcols.T
                   + jnp.where(ii == trow, d_tot, 0.0))
            da = _dot_hi(tri_ref[...], dcs)
            ddt_ref[gi] = (da * a_ref[gi] + _headsum(dxs * xb, e1v))[:, 0:HPG]
            da_ref[gi] += jnp.sum(da[:, 0:HPG] * dt4_ref[gi], axis=0, keepdims=True)
            dgm = dg.astype(MMD)
            dz = (e * dyb).astype(MMD)
            dc_ref[:, ncols] = (_dot(dgm, bm) + _dot(dz, hpm, NT)).astype(dc_ref.dtype)
            db_ref[:, ncols] = (_dot(dgm, cm, TN) + _dot((xs * dec).astype(MMD), dhm, NT)).astype(db_ref.dtype)
            dh_ref[gi] = dh * et + _dot(cm, dz, TN)

    nb = SSD_GROUPS // GPS
    const = lambda shape: pl.BlockSpec(shape, lambda g, c: (0,) * len(shape))
    return pl.pallas_call(
        body, name=name, grid=(nb, nc),
        in_specs=[sp["x"], sp["b"], sp["c"], sp["lanes"], sp["lanes"], sp["col"], sp["rowt"],
                  pl.BlockSpec((GPS, 128, GW), lambda g, c: (di * nb + g, 0, 0)), sp["drow"],
                  pl.BlockSpec((GPS, 1, 128), lambda g, c: (g, 0, 0)), sp["y"], sp["h"],
                  const((CHUNK, CHUNK)), const((GW, 128)), const((HPG * CHUNK, 128))],
        out_specs=[sp["y"], sp["n"], sp["n"], sp["col"], pl.BlockSpec((GPS, 1, HPG), lambda g, c: (g, 0, 0))],
        out_shape=[jax.ShapeDtypeStruct((s, 2048), MMD), jax.ShapeDtypeStruct((s, SSD_GROUPS * SSD_N), MMD),
                   jax.ShapeDtypeStruct((s, SSD_GROUPS * SSD_N), MMD), jax.ShapeDtypeStruct((SSD_GROUPS, s, HPG), F32),
                   jax.ShapeDtypeStruct((SSD_GROUPS, 1, HPG), F32)],
        scratch_shapes=[pltpu.VMEM((GPS, SSD_N, GW), F32), pltpu.VMEM((GPS, CHUNK, HPG * CHUNK), F32),
                        pltpu.VMEM((GPS, CHUNK, GW), F32)],
        compiler_params=_cp(("arbitrary", "arbitrary")),
    )(xc, xc, xc, dt, cs, dt4, cst, ex, drow, arow4, dy, hprev, _tri(anti), e1, e2)


def _gnorm_fwd(ya, yb, proj, w, *, name):
    s = ya.shape[0]
    tm = _tile(s, 256)

    def body(a_ref, b_ref, z_ref, w_ref, o_ref):
        zv = z_ref[...].astype(F32)
        t = (a_ref[...].astype(F32) + b_ref[...].astype(F32)) * (zv * _sigmoid(zv))
        r = lax.rsqrt(jnp.mean(t * t, axis=-1, keepdims=True) + EPS)
        o_ref[...] = ((t * r) * w_ref[...]).astype(o_ref.dtype)

    big = pl.BlockSpec((tm, 2048), lambda i: (i, 0))
    row = pl.BlockSpec((1, 2048), lambda i: (0, 0))
    return pl.pallas_call(
        body, name=name, grid=(s // tm,), in_specs=[big, big, big, row], out_specs=big,
        out_shape=jax.ShapeDtypeStruct((s, 2048), MMD), compiler_params=_cp(("arbitrary",)),
    )(ya, yb, proj, w)


def _gnorm_bwd(dout, ya, yb, proj, w, *, name):
    s = ya.shape[0]
    tm = _tile(s, 256)

    def body(do_ref, a_ref, b_ref, z_ref, w_ref, dy_ref, dz_ref, dw_ref):
        zv = z_ref[...].astype(F32)
        sg = _sigmoid(zv)
        sz = zv * sg
        y = a_ref[...].astype(F32) + b_ref[...].astype(F32)
        t = y * sz
        r = lax.rsqrt(jnp.mean(t * t, axis=-1, keepdims=True) + EPS)
        nv = t * r
        dov = do_ref[...].astype(F32)
        _acc_rows(dw_ref, jnp.sum(dov * nv, axis=0, keepdims=True), pl.program_id(0) == 0)
        dn = dov * w_ref[...]
        dt_ = r * (dn - nv * jnp.mean(dn * nv, axis=-1, keepdims=True))
        dy_ref[...] = (dt_ * sz).astype(dy_ref.dtype)
        dz_ref[...] = (dt_ * y * (sg * (1.0 + zv * (1.0 - sg)))).astype(dz_ref.dtype)

    big = pl.BlockSpec((tm, 2048), lambda i: (i, 0))
    row = pl.BlockSpec((1, 2048), lambda i: (0, 0))
    return pl.pallas_call(
        body, name=name, grid=(s // tm,), in_specs=[big, big, big, big, row], out_specs=[big, big, row],
        out_shape=[jax.ShapeDtypeStruct((s, 2048), MMD), jax.ShapeDtypeStruct((s, 2048), MMD),
                   jax.ShapeDtypeStruct((1, 2048), F32)],
        compiler_params=_cp(("arbitrary",)),
    )(dout, ya, yb, proj, w)


def _colsum_prod(a, b, *, name):
    s, n = a.shape
    tm = _tile(s, 256)

    def body(a_ref, b_ref, o_ref):
        _acc_rows(o_ref, jnp.sum(a_ref[...].astype(F32) * b_ref[...].astype(F32), axis=0, keepdims=True),
                  pl.program_id(0) == 0)

    big = pl.BlockSpec((tm, n), lambda i: (i, 0))
    return pl.pallas_call(
        body, name=name, grid=(s // tm,), in_specs=[big, big], out_specs=pl.BlockSpec((1, n), lambda i: (0, 0)),
        out_shape=jax.ShapeDtypeStruct((1, n), F32), compiler_params=_cp(("arbitrary",)),
    )(a, b)


def _heads(a, n):
    return a.reshape(a.shape[0], n, HEAD_DIM).transpose(1, 0, 2)


def _unheads(a):
    return a.transpose(1, 0, 2).reshape(a.shape[1], a.shape[0] * HEAD_DIM)


def _per_group(a):
    return a.reshape(a.shape[0], SSD_GROUPS, HPG).transpose(1, 0, 2)


def _per_group_t(a):
    s = a.shape[0]
    return a.reshape(s // CHUNK, CHUNK, SSD_GROUPS, HPG).transpose(2, 0, 3, 1)


def _local_step(x, target, mod, wts, small, late_weights=None, late_grads=None, in_grad=None):
    s, d = x.shape
    shift1, scale1, gate1, shift2, scale2, gate2 = [mod[i:i + 1] for i in range(6)]

    h1 = _ln_mod(x, small["norm1_w"], scale1, shift1, name="ln1")
    proj = _mm(h1, wts["w_in_p"], name="in_proj", outs=[MMD], tm=512, tn=2944, b_outer=True)
    dt_raw = _mm(h1, wts["w_dt"], name="dt_proj", outs=[F32], tm=512, tn=128)

    qk_w = jnp.concatenate([jnp.tile(small["q_norm_w"], (1, N_Q_HEADS)), jnp.tile(small["k_norm_w"], (1, N_KV_HEADS))], axis=1)
    qk_sc = jnp.concatenate([jnp.full((1, N_Q_HEADS * HEAD_DIM), HEAD_DIM ** -0.5, F32),
                             jnp.ones((1, N_KV_HEADS * HEAD_DIM), F32)], axis=1)
    qk_sc2 = jnp.concatenate([jnp.full((1, N_Q_HEADS * HEAD_DIM), HEAD_DIM ** -0.5 * LOG2E, F32),
                              jnp.ones((1, N_KV_HEADS * HEAD_DIM), F32)], axis=1)
    tabs = _rope_tables(s)
    qk, qkt = _qk_fwd(proj, qk_w, qk_sc2, tabs, name="qk_fwd")
    qkt = qkt.reshape(N_Q_HEADS + N_KV_HEADS, HEAD_DIM, s)
    k_h = _heads(qk[:, N_Q_HEADS * HEAD_DIM:], N_KV_HEADS)
    v_sd = proj[:, V0:V0 + N_KV_HEADS * HEAD_DIM]
    v_h = _heads(v_sd, N_KV_HEADS)
    vta = jnp.concatenate([v_sd.T.reshape(N_KV_HEADS, HEAD_DIM, s), jnp.ones((N_KV_HEADS, V_AUG - HEAD_DIM, s), MMD)], axis=1)
    ot, lse = _flash_fwd(qkt, vta, name="flash_fwd")
    ot2 = ot.reshape(N_Q_HEADS * HEAD_DIM, s)
    if late_weights is not None:
        wts = {**wts, **late_weights(ot)}

    w8 = jnp.pad(small["conv_w"], ((0, 8 - D_CONV), (0, 0)))
    xc = _conv_fwd(proj, w8, small["conv_b"], name="conv_fwd")
    a_neg = -jnp.exp(small["A_log"])
    arow = jnp.pad(a_neg.reshape(1, 2 * SSD_HEADS), ((0, 0), (0, 128 - 2 * SSD_HEADS)))
    bias_row = jnp.pad(small["dt_bias"].reshape(1, 2 * SSD_HEADS), ((0, 0), (0, 128 - 2 * SSD_HEADS)))
    dt, cs = _dt_fwd(dt_raw, bias_row, arow, name="dt_fwd")
    drow = jnp.repeat(small["ssd_D"], SSD_P, axis=1)
    dirs = []
    for di in range(2):
        cols = slice(di * SSD_HEADS, (di + 1) * SSD_HEADS)
        dirs.append(dict(
            dt4=_per_group(dt[:, cols]), cst=_per_group_t(cs[:, cols]),
            drow=drow if di == 0 else jnp.zeros_like(drow),
            arow4=jnp.pad(a_neg[di].reshape(SSD_GROUPS, 1, HPG), ((0, 0), (0, 0), (0, 128 - HPG)))))
    ex = _expand_mats()
    ys = []
    for di, dd in enumerate(dirs):
        y, dd["hprev"] = _ssd_fwd(xc, dt, cs, dd["cst"], ex, dd["drow"], di, name=f"ssd_fwd{di}")
        ys.append(y)
    ssdn = _gnorm_fwd(ys[0], ys[1], proj, small["ssd_norm_w"], name="gnorm_fwd")

    a_o = _mm(ot2, wts["w_attn_out"], name="attn_out", outs=[MMD], ta=True, tm=512, tn=1024)

    def merge_epi(acc, ao, ga, gs):
        return (_sigmoid(ga.astype(F32)) * ao.astype(F32) + _sigmoid(gs.astype(F32)) * acc, acc)

    merged, b_o = _mm(ssdn, wts["w_ssd_out"], name="ssd_out", outs=[MMD, MMD], tm=512, tn=1024,
                      extras=[(a_o, "tile", 0), (proj, "tile", GA0), (proj, "tile", GS0)], epi=merge_epi)

    def res_epi(acc, res, gate):
        return (res + gate * acc, acc)

    x1, mo = _mm(merged, wts["w_o"], name="w_o", outs=[F32, MMD], tm=512, tn=1024,
                 extras=[(x, "tile", 0), (gate1, "row", 0)], epi=res_epi)
    h2 = _ln_mod(x1, small["norm2_w"], scale2, shift2, name="ln2")

    def relu2_epi(acc):
        rl = jnp.maximum(acc, 0.0)
        return (rl * rl, rl)

    act, rl = _mm(h2, wts["w_mlp1"], name="mlp1", outs=[MMD, MMD], tm=512, tn=1024, epi=relu2_epi, b_outer=True)

    def loss_epi(acc, res, gate, tgt):
        return ((res + gate * acc - tgt) * (1.0 / d), acc)

    dy, ffo = _mm(act, wts["w_mlp2"], name="mlp2", outs=[F32, MMD], tm=512, tn=1024, vmem=VMEM_BIG,
                  extras=[(x1, "tile", 0), (gate2, "row", 0), (target, "tile", 0)], epi=loss_epi)
    loss = _sumsq(dy, name="loss") * (0.5 * d)

    gw = {}
    gs_ = {}
    dffo, dgate2 = _gate_bwd(dy, ffo, gate2, name="gate2_bwd")
    dpre = _mm(dffo, wts["w_mlp2"], name="mlp2_dx", outs=[MMD], nt=True, tm=512, tn=1024,
               extras=[(rl, "tile", 0)], epi=lambda acc, r: (acc * (2.0 * r.astype(F32)),))
    gw["w_mlp2"] = _mm_tn(act, dffo, name="mlp2_dw")
    dh2 = _mm(dpre, wts["w_mlp1"], name="mlp1_dx", outs=[F32], nt=True, tm=512, tn=1024)
    gw["w_mlp1"] = _mm_tn(h2, dpre, name="mlp1_dw")
    dx1, dshift2, dscale2, gs_["norm2_w"] = _ln_mod_bwd(dh2, x1, small["norm2_w"], scale2, dy, name="ln2_bwd")
    dmo, dgate1 = _gate_bwd(dx1, mo, gate1, name="gate1_bwd")

    def merge_bwd_epi(acc, ao, bo, ga, gs):
        sa, ss = _sigmoid(ga.astype(F32)), _sigmoid(gs.astype(F32))
        return (acc * sa, acc * ss, acc * ao.astype(F32) * sa * (1.0 - sa), acc * bo.astype(F32) * ss * (1.0 - ss))

    da_o, db_o, dga, dgs = _mm(dmo, wts["w_o"], name="w_o_dx", outs=[MMD] * 4, nt=True, tm=512, tn=1024,
                               extras=[(a_o, "tile", 0), (b_o, "tile", 0), (proj, "tile", GA0), (proj, "tile", GS0)],
                               epi=merge_bwd_epi)
    gw["w_o"] = _mm_tn(merged, dmo, name="w_o_dw")
    dot = _mm(wts["w_attn_out"], da_o, name="attn_out_dx", outs=[MMD], nt=True, tm=1024, tn=1024)
    gw["w_attn_out"] = _mm(ot2, da_o, name="attn_out_dw", outs=[F32], tm=256, tn=512, vmem=VMEM_BIG)
    dssdn = _mm(db_o, wts["w_ssd_out"], name="ssd_out_dx", outs=[MMD], nt=True, tm=512, tn=2048)
    gw["w_ssd_out"] = _mm_tn(ssdn, db_o, name="ssd_out_dw")

    norm_w = small["ssd_norm_w"] if late_grads is None else small["ssd_norm_w"] + late_grads(gw)
    dyssd, dz, gs_["ssd_norm_w"] = _gnorm_bwd(dssdn, ys[0], ys[1], proj, norm_w, name="gnorm_bwd")
    gs_["ssd_D"] = _colsum_prod(dyssd, xc[:, 0:2048], name="ssd_d_grad").reshape(SSD_HEADS, SSD_P).sum(axis=1).reshape(1, SSD_HEADS)
    dxc, ddts, das = [], [], []
    for di, dd in enumerate(dirs):
        dxs, dbm, dcm, ddt4, da4 = _ssd_bwd(xc, dt, cs, dd["dt4"], dd["cst"], ex, dd["drow"], dd["arow4"],
                                            dyssd, dd["hprev"], di, name=f"ssd_bwd{di}")
        dxc.append((dxs, dbm, dcm))
        ddts.append(ddt4.transpose(1, 0, 2).reshape(s, SSD_HEADS))
        das.append(da4.reshape(1, SSD_HEADS))
    conv_parts, col0 = [], 0
    for part, (ga, gb) in enumerate(zip(*dxc)):
        conv_parts.append(_conv_bwd(proj, col0, ga, gb, w8, small["conv_b"], name=f"conv_bwd{part}"))
        col0 += ga.shape[1]
    dxbc, dw8, gs_["conv_b"] = [jnp.concatenate(t, axis=1) for t in zip(*conv_parts)]
    gs_["conv_w"] = dw8[0:D_CONV]
    gs_["A_log"] = jnp.concatenate(das, axis=0) * a_neg
    ddt = jnp.pad(jnp.concatenate(ddts, axis=1), ((0, 0), (0, 128 - 2 * SSD_HEADS)))
    ddt_raw, dbias = _dt_bwd(ddt, dt_raw, bias_row, name="dt_bwd")
    gs_["dt_bias"] = dbias[:, 0:2 * SSD_HEADS].reshape(2, SSD_HEADS)

    dqt, dk_h, dv_h = _flash_bwd(qkt, k_h, v_h, dot.reshape(N_Q_HEADS, HEAD_DIM, s), ot, lse, name="flash_bwd")
    dqkt = jnp.concatenate([dqt.reshape(N_Q_HEADS * HEAD_DIM, s),
                            dk_h.transpose(0, 2, 1).reshape(N_KV_HEADS * HEAD_DIM, s)], axis=0)
    dqk_u, dqk_w = _qk_bwd(dqkt, proj, qk_w, qk_sc, tabs, name="qk_bwd")
    gs_["q_norm_w"] = dqk_w[:, 0:N_Q_HEADS * HEAD_DIM].reshape(N_Q_HEADS, HEAD_DIM).sum(axis=0, keepdims=True)
    gs_["k_norm_w"] = dqk_w[:, N_Q_HEADS * HEAD_DIM:].reshape(N_KV_HEADS, HEAD_DIM).sum(axis=0, keepdims=True)
    dv = _unheads(dv_h).astype(MMD)

    dproj = jnp.concatenate([dz, dga, dgs, dxbc, dqk_u, dv, ddt_raw], axis=1)
    gw["w_in_p"] = _mm_tn(h1, dproj, name="in_proj_dw", tk=512, tn=2944, tmm=2048, vmem=VMEM_BIG)
    zero_row = jnp.zeros((1, d), F32) if in_grad is None else jnp.zeros((1, d), F32) + in_grad(gw["w_in_p"])[0:1, 0:1]
    dh1 = _mm(dproj, wts["w_in_p"], name="in_proj_dx", outs=[F32], nt=True, tm=256, tn=1024, vmem=VMEM_BIG,
              extras=[(zero_row, "row", 0)], epi=lambda acc, r: (acc + r,))
    grad_x, dshift1, dscale1, gs_["norm1_w"] = _ln_mod_bwd(dh1, x, small["norm1_w"], scale1, dx1, name="ln1_bwd")
    dmod = jnp.concatenate([dshift1, dscale1, dgate1, dshift2, dscale2, dgate2], axis=0)
    return loss, grad_x, dmod, gw, gs_


N_DEV = 8
N_CHIP = 4
ANY = pl.BlockSpec(memory_space=pl.ANY)


def _place():
    return lax.axis_index("x"), lax.axis_index("y"), lax.axis_index("c")


def _allgather8(v, *, name):
    m_per, n = v.shape

    def body(x_ref, out_ref, send_sems, recv_sems, local_sem):
        x, y, c = _place()
        me, sibling = (x, y, c), (x, y, 1 - c)
        chips = [(1 - x, y), (x, 1 - y), (1 - x, 1 - y)]

        def rows(px, py, pc):
            return out_ref.at[pl.ds((4 * px + 2 * py + pc) * m_per, m_per), :]

        def copy(k, block, to, src=None):
            return pltpu.make_async_remote_copy(
                src_ref=rows(*block) if src is None else src, dst_ref=rows(*block),
                send_sem=send_sems.at[k], recv_sem=recv_sems.at[k], device_id=to, device_id_type=MESH)

        mine = pltpu.make_async_copy(x_ref, rows(*me), local_sem)
        mine.start()
        first = [copy(0, me, sibling, src=x_ref)]
        first += [copy(1 + j, me, (*chip, c), src=x_ref) for j, chip in enumerate(chips)]
        for cp in first:
            cp.start()
        passed = [copy(4 + j, (*chip, c), sibling) for j, chip in enumerate(chips)]
        for j, chip in enumerate(chips):
            copy(1 + j, (*chip, c), me).wait_recv()
            passed[j].start()
        copy(0, sibling, me).wait_recv()
        for j, chip in enumerate(chips):
            copy(4 + j, (*chip, 1 - c), me).wait_recv()
        for cp in first + passed:
            cp.wait_send()
        mine.wait()

    return pl.pallas_call(
        body, name=name, out_shape=jax.ShapeDtypeStruct((N_DEV * m_per, n), v.dtype),
        in_specs=[pl.BlockSpec(memory_space=pltpu.VMEM)], out_specs=pl.BlockSpec(memory_space=pltpu.VMEM),
        scratch_shapes=[pltpu.SemaphoreType.DMA((7,)), pltpu.SemaphoreType.DMA((7,)), pltpu.SemaphoreType.DMA],
    )(v)


HBM = pl.BlockSpec(memory_space=pltpu.HBM)
SEM = pl.BlockSpec(memory_space=pltpu.SEMAPHORE)


def _chips_copies(x_ref, land_ref, sems, scatter):
    x, y, c = _place()
    k = 2 * x + y
    chips = [(1 - x, y), (x, 1 - y), (1 - x, 1 - y)]
    ids = [2 * cx + cy for cx, cy in chips]

    def copy(j, slot):
        return pltpu.make_async_remote_copy(
            src_ref=x_ref.at[ids[j]] if scatter else x_ref, dst_ref=land_ref.at[slot], send_sem=sems[j],
            recv_sem=sems[3 + j], device_id=(*chips[j], c), device_id_type=MESH)

    return [copy(j, k) for j in range(3)], [copy(j, ids[j]) for j in range(3)]


def _chips_start(src, scatter, *, name):
    shape = src.shape if scatter else (N_CHIP,) + tuple(src.shape)

    def body(x_ref, land_ref, *rest):
        sems, token = rest[0:6], rest[8]
        for cp in _chips_copies(x_ref, land_ref, sems, scatter)[0]:
            cp.start()
        token[...] = jnp.zeros_like(token)

    out = pl.pallas_call(
        body, name=name,
        out_shape=(pltpu.SemaphoreType.DMA(()),) * 6 + (pltpu.HBM(src.shape, src.dtype), pltpu.HBM(shape, src.dtype),
                                                       jax.ShapeDtypeStruct((8, 128), F32)),
        in_specs=(HBM, HBM), out_specs=(SEM,) * 6 + (HBM, HBM, pl.BlockSpec(memory_space=pltpu.VMEM)),
        input_output_aliases={0: 6, 1: 7},
        compiler_params=pltpu.CompilerParams(has_side_effects=pltpu.SideEffectType.DATAFLOW_SIDE_EFFECTING),
    )(pltpu.with_memory_space_constraint(src, pltpu.HBM),
      pltpu.with_memory_space_constraint(lax.empty(shape, src.dtype), pltpu.HBM))
    return out[0:6], out[6], out[7], out[8]


def _chips_wait(sems, src, land, after, scatter, *, name):
    def body(x_ref, land_ref, *rest):
        sems_ = rest[0:6]
        for cp in _chips_copies(x_ref, land_ref, sems_, scatter)[1]:
            cp.wait_send()
            cp.wait_recv()

    return pl.pallas_call(
        body, name=name, out_shape=(pltpu.HBM(src.shape, src.dtype), pltpu.HBM(land.shape, land.dtype)),
        in_specs=(HBM, HBM) + (SEM,) * 6 + (ANY,), out_specs=(HBM, HBM), input_output_aliases={0: 0, 1: 1},
        compiler_params=pltpu.CompilerParams(has_side_effects=pltpu.SideEffectType.DATAFLOW_SIDE_EFFECTING),
    )(src, land, *sems, after)


def _row_tile(r, pref=512):
    return max(t for t in range(16, pref + 1, 16) if r % t == 0)


def _gather_weights(src, *, name):
    r = src.shape[0]
    hr = r // 2
    assert r == 2 * hr and hr % 16 == 0

    def body(x_ref, out_ref, send_sems, recv_sems):
        x, y, c = _place()
        k = 2 * x + y
        chips = [(1 - x, y), (x, 1 - y), (1 - x, 1 - y)]
        ids = [2 * cx + cy for cx, cy in chips]
        mine_rows = pl.ds(pl.multiple_of(c * hr, 16), hr)
        other_rows = pl.ds(pl.multiple_of((1 - c) * hr, 16), hr)

        def copy(sem, src_ref, slot, rows, to):
            return pltpu.make_async_remote_copy(
                src_ref=src_ref, dst_ref=out_ref.at[slot, rows], send_sem=send_sems.at[sem], recv_sem=recv_sems.at[sem],
                device_id=to, device_id_type=MESH)

        sends = [copy(j, x_ref.at[mine_rows], k, mine_rows, (cx, cy, c)) for j, (cx, cy) in enumerate(chips)]
        for cp in sends:
            cp.start()
        passed = [copy(3 + j, out_ref.at[ids[j], mine_rows], ids[j], mine_rows, (x, y, 1 - c)) for j in range(3)]
        for j, (cx, cy) in enumerate(chips):
            copy(j, x_ref.at[mine_rows], ids[j], mine_rows, (cx, cy, c)).wait_recv()
            passed[j].start()
        for j in range(3):
            copy(3 + j, out_ref.at[ids[j], other_rows], ids[j], other_rows, (x, y, 1 - c)).wait_recv()
        for cp in sends + passed:
            cp.wait_send()

    return pl.pallas_call(
        body, name=name, out_shape=jax.ShapeDtypeStruct((N_CHIP,) + tuple(src.shape), src.dtype),
        in_specs=[ANY], out_specs=ANY,
        scratch_shapes=[pltpu.SemaphoreType.DMA((6,)), pltpu.SemaphoreType.DMA((6,))],
    )(src)


def _pair_swap(a, *, name):
    n, r, cols = a.shape
    hr = r // 2

    def body(x_ref, out_ref, send_sem, recv_sem):
        x, y, c = _place()
        other_rows = pl.ds(pl.multiple_of((1 - c) * hr, 16), hr)
        cp = pltpu.make_async_remote_copy(src_ref=x_ref.at[:, other_rows], dst_ref=out_ref, send_sem=send_sem,
                                          recv_sem=recv_sem, device_id=(x, y, 1 - c), device_id_type=MESH)
        cp.start()
        cp.wait()

    return pl.pallas_call(
        body, name=name, out_shape=jax.ShapeDtypeStruct((n, hr, cols), a.dtype), in_specs=[ANY], out_specs=ANY,
        scratch_shapes=[pltpu.SemaphoreType.DMA, pltpu.SemaphoreType.DMA],
    )(a)


def _sibling_copy(a, *, name):
    def body(x_ref, out_ref, send_sem, recv_sem):
        x, y, c = _place()
        cp = pltpu.make_async_remote_copy(src_ref=x_ref, dst_ref=out_ref, send_sem=send_sem, recv_sem=recv_sem,
                                          device_id=(x, y, 1 - c), device_id_type=MESH)
        cp.start()
        cp.wait()

    return pl.pallas_call(
        body, name=name, out_shape=jax.ShapeDtypeStruct(a.shape, a.dtype), in_specs=[ANY], out_specs=ANY,
        scratch_shapes=[pltpu.SemaphoreType.DMA, pltpu.SemaphoreType.DMA],
    )(a)


def _sum_slots(a, *, name):
    _, r, c = a.shape
    tr = _row_tile(r, 256)

    def body(a_ref, o_ref):
        acc = a_ref[0].astype(F32)
        for j in range(1, N_CHIP):
            acc = acc + a_ref[j].astype(F32)
        o_ref[...] = acc

    return pl.pallas_call(
        body, name=name, grid=(r // tr,), in_specs=[pl.BlockSpec((N_CHIP, tr, c), lambda i: (0, i, 0))],
        out_specs=pl.BlockSpec((tr, c), lambda i: (i, 0)), out_shape=jax.ShapeDtypeStruct((r, c), F32),
        compiler_params=_cp(("arbitrary",)),
    )(a)


def _add2(a, b, *, name):
    r, c = a.shape
    tr = _row_tile(r)

    def body(a_ref, b_ref, o_ref):
        o_ref[...] = (a_ref[...].astype(F32) + b_ref[...].astype(F32)).astype(o_ref.dtype)

    spec = pl.BlockSpec((tr, c), lambda i: (i, 0))
    return pl.pallas_call(
        body, name=name, grid=(r // tr,), in_specs=[spec, spec], out_specs=spec,
        out_shape=jax.ShapeDtypeStruct((r, c), a.dtype), compiler_params=_cp(("arbitrary",)),
    )(a, b)


BIG = ("w_in", "w_mlp1", "w_attn_out", "w_ssd_out", "w_o", "w_mlp2")
COL_SHARDED = ("w_mlp1", "w_in")
ROW_SHARDED = ("w_attn_out", "w_ssd_out", "w_o", "w_mlp2")
LATE = ROW_SHARDED + ("w_mlp1",)
SMALL = ("b_ada", "norm1_w", "norm2_w", "q_norm_w", "k_norm_w", "conv_b", "A_log", "dt_bias", "ssd_D", "ssd_norm_w")
NAMES = ("w_ada", "b_ada", "norm1_w", "norm2_w", "w_in", "q_norm_w", "k_norm_w", "conv_w", "conv_b", "A_log", "dt_bias",
         "ssd_D", "ssd_norm_w", "w_attn_out", "w_ssd_out", "w_o", "w_mlp1", "w_mlp2")
W_IN_COLS = 8768


def _permute_in(w):
    return jnp.concatenate([w[:, 4608:6656], w[:, 6720:8768], w[:, 1536:4608], w[:, 0:1536], w[:, 6656:6720],
                            jnp.zeros((w.shape[0], PW - W_IN_COLS), w.dtype)], axis=1)


def _unpermute_in(wp):
    return jnp.concatenate([wp[:, Q0:DT0], wp[:, XS0:Q0], wp[:, Z0:GA0], wp[:, DT0:DT0 + 64], wp[:, GA0:XS0]], axis=1)


def _pad_to(v, n):
    return jnp.pad(v, (0, n - v.shape[0]))


def _step(w, m, v, loss_target):
    xi, yi, ci = _place()
    chip = 2 * xi + yi
    dev = 4 * xi + 2 * yi + ci
    x, tgt = w["x"], loss_target
    d = x.shape[1]

    cw = w["conv_w"].shape[1]
    v0 = _pad_to(jnp.concatenate([w["c"].reshape(-1), w["conv_w"].reshape(-1)]), 5120).reshape(8, 640)
    g0 = _allgather8(v0, name="ag_cond").reshape(N_DEV, 5120)
    c_all = g0[:, 0:d]
    conv_w = jnp.concatenate([g0[2 * k, d:d + D_CONV * cw].reshape(D_CONV, cw) for k in range(N_CHIP)], axis=1)
    sc = _silu_cast(c_all, name="silu_c")
    modp = _mm(sc, w["w_ada"].astype(MMD), name="ada_fwd", outs=[F32], tm=8, tn=512)
    g1 = _allgather8(modp, name="ag_mod").reshape(N_DEV, N_DEV, modp.shape[1])
    mod_all = jnp.concatenate([g1[2 * k] for k in range(N_CHIP)], axis=1)
    mod = (lax.dynamic_slice_in_dim(mod_all, dev, 1, axis=0) + w["b_ada"]).reshape(6, d)

    mine, mod = lax.optimization_barrier((w["w_in"].astype(MMD), mod))
    gath = lax.dynamic_update_slice_in_dim(_gather_weights(mine, name="ag_w_in"), mine[None], chip, axis=0)
    late_mine = jnp.concatenate([w[n].astype(MMD) for n in LATE], axis=0)
    late_mine, gath = lax.optimization_barrier((late_mine, gath))
    ag_sems, ag_src, ag_land, ag_token = _chips_start(late_mine, False, name="ag_late_start")
    mod = mod + ag_token[0:1, 0:1]
    w_in = jnp.concatenate([gath[k] for k in range(N_CHIP)], axis=1)
    wts = {"w_in_p": _permute_in(w_in), "w_dt": jnp.pad(w_in[:, 6656:6720], ((0, 0), (0, 64)))}
    small = {n: w[n] for n in SMALL if n != "b_ada"}
    small["conv_w"] = conv_w

    def own_slot(land, src):
        return lax.dynamic_update_slice_in_dim(land, src, chip, axis=0)

    def late_weights(after):
        src, land = _chips_wait(ag_sems, ag_src, ag_land, after, False, name="ag_late_wait")
        land = own_slot(land, src[None])
        out, o = {}, 0
        for n in LATE:
            rows = w[n].shape[0]
            part = land[:, o:o + rows]
            out[n] = (jnp.concatenate([part[k] for k in range(N_CHIP)], axis=1) if n in COL_SHARDED
                      else part.reshape(N_CHIP * rows, w[n].shape[1]))
            o += rows
        return out

    def pair_sums(slots, tag):
        _, rows, cols = slots.shape
        hr = rows // 2
        theirs = _pair_swap(slots, name="rs_pair_" + tag)
        ours = lax.dynamic_slice_in_dim(slots, ci * hr, hr, axis=1)
        pair = _add2(ours.reshape(N_CHIP * hr, cols), theirs.reshape(N_CHIP * hr, cols), name="rs_pair_sum_" + tag)
        return pair.reshape(N_CHIP, hr, cols)

    def finish(recv, pair, tag):
        recv = own_slot(recv, lax.dynamic_slice_in_dim(pair, chip, 1, axis=0))
        half = _sum_slots(recv, name="rs_sum_" + tag)
        other = _sibling_copy(half, name="rs_sibling_" + tag)
        return jnp.where(ci == 0, jnp.concatenate([half, other], axis=0), jnp.concatenate([other, half], axis=0))

    started = {}

    def late_grads(gw):
        slots = []
        for k in range(N_CHIP):
            parts = []
            for n in LATE:
                rows = w[n].shape[0]
                blk = gw[n][:, k * rows:(k + 1) * rows] if n in COL_SHARDED else gw[n][k * rows:(k + 1) * rows]
                parts.append(blk.astype(MMD))
            slots.append(jnp.concatenate(parts, axis=0))
        pair = pair_sums(jnp.stack(slots), "late")
        sems, src, land, token = _chips_start(pair, True, name="rs_late_start")
        started["late"] = (sems, src, land)
        return token[0:1, 0:1]

    def in_grad(g):
        g_in = _unpermute_in(g)
        cols_in = w["w_in"].shape[1]
        pair = pair_sums(jnp.stack([g_in[:, k * cols_in:(k + 1) * cols_in].astype(MMD) for k in range(N_CHIP)]), "w_in")
        sems, src, land, token = _chips_start(pair, True, name="rs_w_in_start")
        started["w_in"] = (sems, src, land)
        return token

    loss, grad_x, dmod, gw, gs = _local_step(x, tgt, mod, wts, small, late_weights, late_grads, in_grad)

    grads = {}
    pair, land = _chips_wait(*started["w_in"], grad_x, True, name="rs_w_in_wait")
    grads["w_in"] = finish(land, pair, "w_in")
    pair, land = _chips_wait(*started["late"], grad_x, True, name="rs_late_wait")
    total, o = finish(land, pair, "late"), 0
    for n in LATE:
        rows = w[n].shape[0]
        grads[n] = total[o:o + rows]
        o += rows

    order = ([dmod.reshape(-1)] + [gs[n].reshape(-1) for n in SMALL if n != "b_ada"] + [gs["conv_w"].reshape(-1)]
             + [loss.reshape(-1)])
    vec = jnp.concatenate(order)
    n_small = vec.shape[0]
    n_pad = -(-n_small // 1024) * 1024
    g2 = _allgather8(_pad_to(vec, n_pad).reshape(8, n_pad // 8), name="ag_small")
    tot = _rows_sum(g2, N_DEV, name="small_sum").reshape(-1)
    loss = tot[n_small - 1]
    dmod_all = g2.reshape(N_DEV, n_pad)[:, 0:6 * d]
    off = 0
    for n in SMALL:
        grads[n] = tot[off:off + w[n].size].reshape(w[n].shape)
        off += w[n].size
    conv_full = tot[off:off + D_CONV * N_CHIP * cw].reshape(D_CONV, N_CHIP * cw)
    grads["conv_w"] = lax.dynamic_slice_in_dim(conv_full, chip * cw, cw, axis=1)
    ada_cols = w["w_ada"].shape[1]
    dmod_mine = lax.dynamic_slice_in_dim(dmod_all, chip * ada_cols, ada_cols, axis=1).astype(MMD)
    grads["w_ada"] = _mm_tn(sc, dmod_mine, name="ada_dw", tk=512, tn=512, tmm=8)

    delta, new_m, new_v = {}, {}, {}
    pack = lambda t: jnp.concatenate([t[n].reshape(-1) for n in SMALL]).reshape(1, -1)
    ds_, ms_, vs_ = _adamw(pack(w), pack(grads), pack(m), pack(v), name="adamw_small")
    off = 0
    for n in SMALL:
        for dst, src in ((delta, ds_), (new_m, ms_), (new_v, vs_)):
            dst[n] = src[0, off:off + w[n].size].reshape(w[n].shape)
        off += w[n].size
    for n in ("w_ada", "conv_w") + BIG:
        delta[n], new_m[n], new_v[n] = _adamw(w[n], grads[n], m[n], v[n], name="adamw_" + n)
    return loss, grad_x, grads, delta, new_m, new_v


def kernel(x, c, w_ada, b_ada, norm1_w, norm2_w, w_in, q_norm_w, k_norm_w, conv_w, conv_b, A_log, dt_bias, ssd_D, ssd_norm_w, w_attn_out, w_ssd_out, w_o, w_mlp1, w_mlp2, loss_target, m_w_ada, m_b_ada, m_norm1_w, m_norm2_w, m_w_in, m_q_norm_w, m_k_norm_w, m_conv_w, m_conv_b, m_A_log, m_dt_bias, m_ssd_D, m_ssd_norm_w, m_w_attn_out, m_w_ssd_out, m_w_o, m_w_mlp1, m_w_mlp2, v_w_ada, v_b_ada, v_norm1_w, v_norm2_w, v_w_in, v_q_norm_w, v_k_norm_w, v_conv_w, v_conv_b, v_A_log, v_dt_bias, v_ssd_D, v_ssd_norm_w, v_w_attn_out, v_w_ssd_out, v_w_o, v_w_mlp1, v_w_mlp2):
    args = dict(locals())
    strip = lambda a: a[0] if a.ndim == 3 else a
    w = {n: strip(args[n]) for n in NAMES + ("x", "c")}
    m = {n: strip(args["m_" + n]) for n in NAMES}
    v = {n: strip(args["v_" + n]) for n in NAMES}
    loss, grad_x, grads, delta, new_m, new_v = _step(w, m, v, loss_target[0])
    like = lambda t, n: t.reshape(args[n].shape)
    return (loss, grad_x[None], *[like(grads[n], n) for n in NAMES], *[like(delta[n], n) for n in NAMES],
            *[like(new_m[n], n) for n in NAMES], *[like(new_v[n], n) for n in NAMES])
```

```python
import math

import jax
import jax.numpy as jnp
from jax import lax
from jax.experimental import pallas as pl
from jax.experimental.pallas import tpu as pltpu

F32 = jnp.float32
MMD = jnp.bfloat16
EPS = 1e-6
NEG = -1e30
MIB = 1024 * 1024
VMEM_BIG = 56 * MIB
VMEM_MID = 40 * MIB

GRID_W = 64
N_Q_HEADS, N_KV_HEADS, HEAD_DIM = 16, 4, 64
ROPE_THETA = 10000.0
SSD_HEADS, SSD_GROUPS, SSD_P, SSD_N, CHUNK = 32, 4, 64, 128, 128
HPG = SSD_HEADS // SSD_GROUPS
D_CONV = 5
ADAM_LR, ADAM_B1, ADAM_B2, ADAM_EPS, ADAM_WD, ADAM_STEP = 0.001, 0.9, 0.999, 1e-08, 0.01, 10

Z0, GA0, GS0, XS0, B0, C0, Q0, K0, V0, DT0, PW = 0, 2048, 3072, 4096, 6144, 6656, 7168, 8192, 8448, 8704, 8832

MESH = pl.DeviceIdType.MESH
NT = (((1,), (1,)), ((), ()))
TN = (((0,), (0,)), ((), ()))


def _cp(sem=None, vmem=VMEM_MID):
    return pltpu.CompilerParams(dimension_semantics=sem, vmem_limit_bytes=vmem)


def _tile(n, pref):
    t = min(n, pref)
    while n % t:
        t //= 2
    return t


def _dot(a, b, dims=None):
    if dims is None:
        return jnp.dot(a, b, preferred_element_type=F32)
    return lax.dot_general(a, b, dims, preferred_element_type=F32)


def _dot_hi(a, b):
    return jnp.dot(a, b, precision=lax.Precision.HIGHEST, preferred_element_type=F32)


def _sigmoid(x):
    return jax.nn.sigmoid(x)


def _mm(a, b, *, name, outs, nt=False, ta=False, extras=(), epi=None, tm=512, tn=512, n=None, b_outer=False,
        vmem=VMEM_MID):
    assert not (nt and ta)
    k, m = a.shape if ta else a.shape[::-1]
    if n is None:
        n = b.shape[0] if nt else b.shape[1]
    tm, tn = _tile(m, tm), _tile(n, tn)
    gi, gj = m // tm, n // tn
    if b_outer:
        grid = (gj, gi)
        ij = lambda p, q: (q, p)
    else:
        grid = (gi, gj)
        ij = lambda p, q: (p, q)
    if ta:
        a_spec = pl.BlockSpec((k, tm), lambda p, q: (0, ij(p, q)[0]))
    else:
        a_spec = pl.BlockSpec((tm, k), lambda p, q: (ij(p, q)[0], 0))
    if nt:
        b_spec = pl.BlockSpec((tn, k), lambda p, q: (ij(p, q)[1], 0))
    else:
        b_spec = pl.BlockSpec((k, tn), lambda p, q: (0, ij(p, q)[1]))
    e_specs = []
    for arr, kind, off in extras:
        ob = off // tn
        assert off % tn == 0
        if kind == "tile":
            e_specs.append(pl.BlockSpec((tm, tn), lambda p, q, ob=ob: (ij(p, q)[0], ob + ij(p, q)[1])))
        else:
            e_specs.append(pl.BlockSpec((1, tn), lambda p, q, ob=ob: (0, ob + ij(p, q)[1])))
    ne = len(extras)

    def body(a_ref, b_ref, *rest):
        acc = _dot(a_ref[...], b_ref[...], NT if nt else (TN if ta else None))
        res = epi(acc, *[e[...] for e in rest[:ne]]) if epi is not None else (acc,)
        for o_ref, r in zip(rest[ne:], res):
            o_ref[...] = r.astype(o_ref.dtype)

    out = pl.pallas_call(
        body, name=name, grid=grid,
        in_specs=[a_spec, b_spec] + e_specs,
        out_specs=[pl.BlockSpec((tm, tn), lambda p, q: ij(p, q)) for _ in outs],
        out_shape=[jax.ShapeDtypeStruct((m, n), dt) for dt in outs],
        compiler_params=_cp(("arbitrary", "arbitrary"), vmem),
    )(a, b, *[e[0] for e in extras])
    return out if len(outs) > 1 else out[0]


def _mm_tn(a, g, *, name, tk=512, tn=1024, tmm=4096, vmem=VMEM_MID):
    m, k = a.shape
    n = g.shape[1]
    tk, tn, tmm = _tile(k, tk), _tile(n, tn), _tile(m, tmm)

    def body(a_ref, g_ref, o_ref):
        p = _dot(a_ref[...], g_ref[...], TN)

        @pl.when(pl.program_id(2) == 0)
        def _():
            o_ref[...] = p

        @pl.when(pl.program_id(2) > 0)
        def _():
            o_ref[...] += p

    return pl.pallas_call(
        body, name=name, grid=(k // tk, n // tn, m // tmm),
        in_specs=[pl.BlockSpec((tmm, tk), lambda i, j, r: (r, i)), pl.BlockSpec((tmm, tn), lambda i, j, r: (r, j))],
        out_specs=pl.BlockSpec((tk, tn), lambda i, j, r: (i, j)),
        out_shape=jax.ShapeDtypeStruct((k, n), F32),
        compiler_params=_cp(("arbitrary", "arbitrary", "arbitrary"), vmem),
    )(a, g)


def _adamw(w, g, m, v, *, name):
    r, c = w.shape
    tr = _tile(r, 256) if r % 8 == 0 else r

    def body(w_ref, g_ref, m_ref, v_ref, d_ref, nm_ref, nv_ref):
        gg = g_ref[...]
        nm = ADAM_B1 * m_ref[...] + (1.0 - ADAM_B1) * gg
        nv = ADAM_B2 * v_ref[...] + (1.0 - ADAM_B2) * jnp.square(gg)
        m_hat = nm / (1.0 - ADAM_B1 ** ADAM_STEP)
        v_hat = nv / (1.0 - ADAM_B2 ** ADAM_STEP)
        d_ref[...] = -ADAM_LR * (m_hat / (jnp.sqrt(v_hat) + ADAM_EPS) + ADAM_WD * w_ref[...])
        nm_ref[...] = nm
        nv_ref[...] = nv

    spec = pl.BlockSpec((tr, c), lambda i: (i, 0))
    return pl.pallas_call(
        body, name=name, grid=(r // tr,), in_specs=[spec] * 4, out_specs=[spec] * 3,
        out_shape=[jax.ShapeDtypeStruct((r, c), F32)] * 3, compiler_params=_cp(("arbitrary",)),
    )(w, g, m, v)


def _rows_sum(a, groups, *, name):
    r = a.shape[0] // groups

    def body(a_ref, o_ref):
        acc = a_ref[0:r, :]
        for d in range(1, groups):
            acc = acc + a_ref[d * r:(d + 1) * r, :]
        o_ref[...] = acc

    return pl.pallas_call(body, name=name, out_shape=jax.ShapeDtypeStruct((r, a.shape[1]), F32))(a)


def _silu_cast(a, *, name):
    def body(a_ref, o_ref):
        x = a_ref[...]
        o_ref[...] = (x * _sigmoid(x)).astype(o_ref.dtype)

    return pl.pallas_call(body, name=name, out_shape=jax.ShapeDtypeStruct(a.shape, MMD))(a)


def _sumsq(a, *, name):
    m, n = a.shape
    tm = _tile(m, 512)

    def body(a_ref, o_ref):
        x = a_ref[...]
        p = jnp.sum(jnp.sum(x * x, axis=1, keepdims=True), axis=0, keepdims=True)

        @pl.when(pl.program_id(0) == 0)
        def _():
            o_ref[...] = p

        @pl.when(pl.program_id(0) > 0)
        def _():
            o_ref[...] += p

    return pl.pallas_call(
        body, name=name, grid=(m // tm,), in_specs=[pl.BlockSpec((tm, n), lambda i: (i, 0))],
        out_specs=pl.BlockSpec((1, 1), lambda i: (0, 0)), out_shape=jax.ShapeDtypeStruct((1, 1), F32),
        compiler_params=_cp(("arbitrary",)),
    )(a)


def _acc_rows(o_ref, p, first):
    @pl.when(first)
    def _():
        o_ref[...] = p

    @pl.when(jnp.logical_not(first))
    def _():
        o_ref[...] += p


def _ln_mod(x, w, scale, shift, *, name):
    s, d = x.shape
    tm = _tile(s, 512)

    def body(x_ref, w_ref, sc_ref, sh_ref, o_ref):
        xv = x_ref[...]
        r = lax.rsqrt(jnp.mean(xv * xv, axis=-1, keepdims=True) + EPS)
        o_ref[...] = ((xv * r) * w_ref[...] * (1.0 + sc_ref[...]) + sh_ref[...]).astype(o_ref.dtype)

    row = pl.BlockSpec((1, d), lambda i: (0, 0))
    big = pl.BlockSpec((tm, d), lambda i: (i, 0))
    return pl.pallas_call(
        body, name=name, grid=(s // tm,), in_specs=[big, row, row, row], out_specs=big,
        out_shape=jax.ShapeDtypeStruct((s, d), MMD), compiler_params=_cp(("arbitrary",)),
    )(x, w, scale, shift)


def _ln_mod_bwd(dh, x, w, scale, dres, *, name):
    s, d = x.shape
    tm = _tile(s, 512)

    def body(dh_ref, x_ref, w_ref, sc_ref, dres_ref, dx_ref, dsh_ref, dsc_ref, dw_ref):
        xv = x_ref[...]
        dhv = dh_ref[...].astype(F32)
        r = lax.rsqrt(jnp.mean(xv * xv, axis=-1, keepdims=True) + EPS)
        nv = xv * r
        wv = w_ref[...]
        g1 = 1.0 + sc_ref[...]
        dn = dhv * (wv * g1)
        dx_ref[...] = dres_ref[...] + r * (dn - nv * jnp.mean(dn * nv, axis=-1, keepdims=True))
        first = pl.program_id(0) == 0
        _acc_rows(dsh_ref, jnp.sum(dhv, axis=0, keepdims=True), first)
        _acc_rows(dsc_ref, jnp.sum(dhv * nv * wv, axis=0, keepdims=True), first)
        _acc_rows(dw_ref, jnp.sum(dhv * nv * g1, axis=0, keepdims=True), first)

    row = pl.BlockSpec((1, d), lambda i: (0, 0))
    big = pl.BlockSpec((tm, d), lambda i: (i, 0))
    return pl.pallas_call(
        body, name=name, grid=(s // tm,), in_specs=[big, big, row, row, big], out_specs=[big, row, row, row],
        out_shape=[jax.ShapeDtypeStruct((s, d), F32)] + [jax.ShapeDtypeStruct((1, d), F32)] * 3,
        compiler_params=_cp(("arbitrary",)),
    )(dh, x, w, scale, dres)


def _gate_bwd(dy, u, gate, *, name):
    s, d = dy.shape
    tm = _tile(s, 512)

    def body(dy_ref, u_ref, g_ref, du_ref, dg_ref):
        dyv = dy_ref[...]
        du_ref[...] = (dyv * g_ref[...]).astype(du_ref.dtype)
        _acc_rows(dg_ref, jnp.sum(dyv * u_ref[...].astype(F32), axis=0, keepdims=True), pl.program_id(0) == 0)

    row = pl.BlockSpec((1, d), lambda i: (0, 0))
    big = pl.BlockSpec((tm, d), lambda i: (i, 0))
    return pl.pallas_call(
        body, name=name, grid=(s // tm,), in_specs=[big, big, row], out_specs=[big, row],
        out_shape=[jax.ShapeDtypeStruct((s, d), MMD), jax.ShapeDtypeStruct((1, d), F32)],
        compiler_params=_cp(("arbitrary",)),
    )(dy, u, gate)


def _seg64(v, e):
    hi = v.astype(jnp.bfloat16)
    lo = (v - hi.astype(F32)).astype(jnp.bfloat16)
    return _dot(hi, e) + _dot(lo, e)


def _rope_tables(s):
    rows = s // GRID_W
    pos_row = jnp.repeat(jnp.arange(rows, dtype=jnp.int32), GRID_W).astype(F32)
    pos_col = jnp.tile(jnp.arange(GRID_W, dtype=jnp.int32), rows).astype(F32)
    axis_dim = HEAD_DIM // 2
    inv_freq = ROPE_THETA ** (-jnp.arange(0, axis_dim, 2, dtype=F32) / axis_dim)
    ang_r = pos_row[:, None] * inv_freq[None, :]
    ang_c = pos_col[:, None] * inv_freq[None, :]
    zero = jnp.zeros_like(ang_r)
    cos = jnp.concatenate([jnp.cos(ang_r), jnp.cos(ang_r), jnp.cos(ang_c), jnp.cos(ang_c)], axis=1)
    s_a = jnp.concatenate([-jnp.sin(ang_r), zero, -jnp.sin(ang_c), zero], axis=1)
    s_b = jnp.concatenate([zero, jnp.sin(ang_r), zero, jnp.sin(ang_c)], axis=1)
    return [jnp.tile(t, (1, 2)) for t in (cos, s_a, s_b)]


def _e128():
    i = jnp.arange(128)
    return (i[:, None] // 64 == i[None, :] // 64).astype(jnp.bfloat16)


QKW = N_Q_HEADS * HEAD_DIM + N_KV_HEADS * HEAD_DIM


def _qk_fwd(proj, wrow, scrow, tabs, *, name):
    s = proj.shape[0]
    tm = _tile(s, 1024)

    def body(x_ref, w_ref, sc_ref, cos_ref, sa_ref, sb_ref, e_ref, o_ref, ot_ref):
        u = x_ref[...].astype(F32)
        r = lax.rsqrt(_seg64(u * u, e_ref[...]) * (1.0 / HEAD_DIM) + EPS)
        nv = (u * r) * w_ref[...]
        ro = nv * cos_ref[...] + pltpu.roll(nv, 112, 1) * sa_ref[...] + pltpu.roll(nv, 16, 1) * sb_ref[...]
        out = ro * sc_ref[...]
        o_ref[...] = out.astype(o_ref.dtype)
        ot_ref[...] = out.T.astype(ot_ref.dtype)

    tab = pl.BlockSpec((tm, 128), lambda i, j: (i, 0))
    row = pl.BlockSpec((1, 128), lambda i, j: (0, j))
    return pl.pallas_call(
        body, name=name, grid=(s // tm, QKW // 128),
        in_specs=[pl.BlockSpec((tm, 128), lambda i, j: (i, Q0 // 128 + j)), row, row, tab, tab, tab,
                  pl.BlockSpec((128, 128), lambda i, j: (0, 0))],
        out_specs=[pl.BlockSpec((tm, 128), lambda i, j: (i, j)), pl.BlockSpec((128, tm), lambda i, j: (j, i))],
        out_shape=[jax.ShapeDtypeStruct((s, QKW), MMD), jax.ShapeDtypeStruct((QKW, s), MMD)],
        compiler_params=_cp(("arbitrary", "arbitrary")),
    )(proj, wrow, scrow, *tabs, _e128())


def _qk_bwd(dqt, dkt, proj, wrow, scrow, tabs, *, name):
    s = proj.shape[0]
    tm = _tile(s, 1024)
    nq = dqt.shape[0] // 128

    def body(dq_ref, dk_ref, x_ref, w_ref, sc_ref, cos_ref, sa_ref, sb_ref, e_ref, du_ref, dw_ref):
        e = e_ref[...]
        d = jnp.where(pl.program_id(0) < nq, dq_ref[...], dk_ref[...]).T * sc_ref[...]
        dn = d * cos_ref[...] + pltpu.roll(d * sa_ref[...], 16, 1) + pltpu.roll(d * sb_ref[...], 112, 1)
        u = x_ref[...].astype(F32)
        r = lax.rsqrt(_seg64(u * u, e) * (1.0 / HEAD_DIM) + EPS)
        uh = u * r
        _acc_rows(dw_ref, jnp.sum(dn * uh, axis=0, keepdims=True), pl.program_id(1) == 0)
        dnw = dn * w_ref[...]
        du_ref[...] = (r * (dnw - uh * (_seg64(dnw * uh, e) * (1.0 / HEAD_DIM)))).astype(du_ref.dtype)

    tab = pl.BlockSpec((tm, 128), lambda j, i: (i, 0))
    row = pl.BlockSpec((1, 128), lambda j, i: (0, j))
    return pl.pallas_call(
        body, name=name, grid=(QKW // 128, s // tm),
        in_specs=[pl.BlockSpec((128, tm), lambda j, i: (jnp.minimum(j, nq - 1), i)),
                  pl.BlockSpec((128, tm), lambda j, i: (jnp.maximum(j - nq, 0), i)),
                  pl.BlockSpec((tm, 128), lambda j, i: (i, Q0 // 128 + j)),
                  row, row, tab, tab, tab, pl.BlockSpec((128, 128), lambda j, i: (0, 0))],
        out_specs=[pl.BlockSpec((tm, 128), lambda j, i: (i, j)), row],
        out_shape=[jax.ShapeDtypeStruct((s, QKW), MMD), jax.ShapeDtypeStruct((1, QKW), F32)],
        compiler_params=_cp(("arbitrary", "arbitrary")),
    )(dqt, dkt, proj, wrow, scrow, *tabs, _e128())


REP = N_Q_HEADS // N_KV_HEADS


def _lanes(ref):
    return jnp.concatenate([ref[r] for r in range(REP)], axis=1)


V_AUG = HEAD_DIM + 8
LOG2E = math.log2(math.e)


def _flash_fwd(qkt, vta, *, name):
    s = qkt.shape[2]
    tq, tk = _tile(s, 1024), _tile(s, 512)
    nk = s // tk
    lanes = REP * tq

    def body(q_ref, k_ref, v_ref, o_ref, lse_ref, m_ref, acc_ref):
        j = pl.program_id(2)

        @pl.when(j == 0)
        def _():
            m_ref[...] = jnp.full_like(m_ref, NEG)
            acc_ref[...] = jnp.zeros_like(acc_ref)

        st = _dot(k_ref[0], _lanes(q_ref), TN)
        m_prev = m_ref[...]
        m_new = jnp.maximum(m_prev, jnp.max(st, axis=0, keepdims=True))
        p = jnp.exp2(st - m_new).astype(MMD)
        acc_ref[...] = jnp.exp2(m_prev - m_new) * acc_ref[...] + _dot(v_ref[0], p)
        m_ref[...] = m_new

        @pl.when(j == nk - 1)
        def _():
            acc = acc_ref[...]
            l = acc[HEAD_DIM:HEAD_DIM + 1]
            o = acc[0:HEAD_DIM] / l
            ls = m_ref[...] + jnp.log(l) * LOG2E
            for r in range(REP):
                o_ref[r] = o[:, r * tq:(r + 1) * tq].astype(o_ref.dtype)
                lse_ref[r] = ls[:, r * tq:(r + 1) * tq]

    qspec = pl.BlockSpec((REP, HEAD_DIM, tq), lambda g, i, j: (g, 0, i))
    return pl.pallas_call(
        body, name=name, grid=(N_KV_HEADS, s // tq, nk),
        in_specs=[qspec, pl.BlockSpec((1, HEAD_DIM, tk), lambda g, i, j: (N_Q_HEADS + g, 0, j)),
                  pl.BlockSpec((1, V_AUG, tk), lambda g, i, j: (g, 0, j))],
        out_specs=[qspec, pl.BlockSpec((REP, 1, tq), lambda g, i, j: (g, 0, i))],
        out_shape=[jax.ShapeDtypeStruct((N_Q_HEADS, HEAD_DIM, s), MMD), jax.ShapeDtypeStruct((N_Q_HEADS, 1, s), F32)],
        scratch_shapes=[pltpu.VMEM((1, lanes), F32), pltpu.VMEM((V_AUG, lanes), F32)],
        compiler_params=_cp(("arbitrary", "arbitrary", "arbitrary"), VMEM_BIG),
    )(qkt, qkt, vta)


def _flash_bwd(qkt, k_h, v_h, dot, ot, lse, *, name):
    s = qkt.shape[2]
    tq, tk = _tile(s, 512), _tile(s, 1024)
    nk = s // tk

    def body(q_ref, kt_ref, k_ref, v_ref, do_ref, o_ref, lse_ref, dq_ref, dk_ref, dv_ref, dq_acc):
        i, j = pl.program_id(1), pl.program_id(2)
        q, do = _lanes(q_ref), _lanes(do_ref)
        delta = jnp.sum(do.astype(F32) * _lanes(o_ref).astype(F32), axis=0, keepdims=True)
        k, v = k_ref[0], v_ref[0]
        p = jnp.exp2(_dot(k, q) - _lanes(lse_ref))
        dvc = _dot(p.astype(MMD), do, NT)
        ds = (p * (_dot(v, do) - delta)).astype(MMD)
        dkc = _dot(ds, q, NT) * (1.0 / LOG2E)
        dqc = _dot(kt_ref[0], ds)
        rows = pl.ds(pl.multiple_of(j * tk, tk), tk)

        @pl.when(i == 0)
        def _():
            dk_ref[0, rows, :] = dkc
            dv_ref[0, rows, :] = dvc

        @pl.when(i > 0)
        def _():
            dk_ref[0, rows, :] += dkc
            dv_ref[0, rows, :] += dvc

        @pl.when(j == 0)
        def _():
            dq_acc[...] = dqc

        @pl.when(j > 0)
        def _():
            dq_acc[...] += dqc

        @pl.when(j == nk - 1)
        def _():
            acc = dq_acc[...]
            for r in range(REP):
                dq_ref[r] = acc[:, r * tq:(r + 1) * tq]

    qspec = pl.BlockSpec((REP, HEAD_DIM, tq), lambda g, i, j: (g, 0, i))
    kvin = pl.BlockSpec((1, tk, HEAD_DIM), lambda g, i, j: (g, j, 0))
    kvres = pl.BlockSpec((1, s, HEAD_DIM), lambda g, i, j: (g, 0, 0))
    return pl.pallas_call(
        body, name=name, grid=(N_KV_HEADS, s // tq, nk),
        in_specs=[qspec, pl.BlockSpec((1, HEAD_DIM, tk), lambda g, i, j: (N_Q_HEADS + g, 0, j)), kvin, kvin,
                  qspec, qspec, pl.BlockSpec((REP, 1, tq), lambda g, i, j: (g, 0, i))],
        out_specs=[qspec, kvres, kvres],
        out_shape=[jax.ShapeDtypeStruct((N_Q_HEADS, HEAD_DIM, s), F32), jax.ShapeDtypeStruct((N_KV_HEADS, s, HEAD_DIM), F32),
                   jax.ShapeDtypeStruct((N_KV_HEADS, s, HEAD_DIM), F32)],
        scratch_shapes=[pltpu.VMEM((HEAD_DIM, REP * tq), F32)],
        compiler_params=_cp(("arbitrary", "arbitrary", "arbitrary"), VMEM_BIG),
    )(qkt, qkt, k_h, v_h, dot, ot, lse)


HALO = 8
CONV_W = 2048 + 2 * SSD_GROUPS * SSD_N


def _shifted(win, off, r):
    return pltpu.roll(win, (r + 2 * HALO - off) % (r + 2 * HALO), 0)[0:r]


def _conv_fwd(proj, w8, brow, *, name):
    s = proj.shape[0]
    cb = 256
    r = _tile(s, 512)

    def body(x_ref, w_ref, b_ref, o_ref, pad_ref):
        zeros = jnp.zeros((HALO, cb), F32)
        pad_ref[0:HALO, :] = zeros
        pad_ref[s + HALO:s + 2 * HALO, :] = zeros

        def fill(i, carry):
            st = pl.multiple_of(i * r, r)
            pad_ref[pl.ds(st + HALO, r), :] = x_ref[pl.ds(st, r), :].astype(F32)
            return carry

        lax.fori_loop(0, s // r, fill, 0)
        wv = w_ref[...]
        bv = b_ref[...]

        def step(i, carry):
            st = pl.multiple_of(i * r, r)
            win = pad_ref[pl.ds(st, r + 2 * HALO), :]
            acc = bv + wv[0:1, :] * _shifted(win, HALO - 2, r)
            for t in range(1, D_CONV):
                acc = acc + wv[t:t + 1, :] * _shifted(win, HALO - 2 + t, r)
            o_ref[pl.ds(st, r), :] = (acc * _sigmoid(acc)).astype(o_ref.dtype)
            return carry

        lax.fori_loop(0, s // r, step, 0)

    return pl.pallas_call(
        body, name=name, grid=(CONV_W // cb,),
        in_specs=[pl.BlockSpec((s, cb), lambda j: (0, XS0 // cb + j)), pl.BlockSpec((8, cb), lambda j: (0, j)),
                  pl.BlockSpec((1, cb), lambda j: (0, j))],
        out_specs=pl.BlockSpec((s, cb), lambda j: (0, j)),
        out_shape=jax.ShapeDtypeStruct((s, CONV_W), MMD),
        scratch_shapes=[pltpu.VMEM((s + 2 * HALO, cb), F32)],
        compiler_params=_cp(("arbitrary",), VMEM_MID),
    )(proj, w8, brow)


def _conv_bwd(proj, col0, ga, gb, w8, brow, *, name):
    s = proj.shape[0]
    width = ga.shape[1]
    cb = 128
    c0 = col0 // cb
    r = _tile(s, 512)

    def body(x_ref, ga_ref, gb_ref, w_ref, b_ref, dx_ref, dw_ref, db_ref, xpad, dpad):
        zeros = jnp.zeros((HALO, cb), F32)
        for ref in (xpad, dpad):
            ref[0:HALO, :] = zeros
            ref[s + HALO:s + 2 * HALO, :] = zeros

        def fill(i, carry):
            st = pl.multiple_of(i * r, r)
            xpad[pl.ds(st + HALO, r), :] = x_ref[pl.ds(st, r), :].astype(F32)
            return carry

        lax.fori_loop(0, s // r, fill, 0)
        wv = w_ref[...]
        bv = b_ref[...]

        def first(i, carry):
            st = pl.multiple_of(i * r, r)
            win = xpad[pl.ds(st, r + 2 * HALO), :]
            taps = [_shifted(win, HALO - 2 + t, r) for t in range(D_CONV)]
            u = bv
            for t in range(D_CONV):
                u = u + wv[t:t + 1, :] * taps[t]
            sg = _sigmoid(u)
            du = ((ga_ref[pl.ds(st, r), :].astype(F32) + gb_ref[pl.ds(st, r), :].astype(F32))
                  * (sg * (1.0 + u * (1.0 - sg))))
            dpad[pl.ds(st + HALO, r), :] = du
            out = [carry[0] + jnp.sum(du, axis=0, keepdims=True)]
            for t in range(D_CONV):
                out.append(carry[1 + t] + jnp.sum(du * taps[t], axis=0, keepdims=True))
            return tuple(out)

        sums = lax.fori_loop(0, s // r, first, tuple(jnp.zeros((1, cb), F32) for _ in range(1 + D_CONV)))
        db_ref[...] = sums[0]
        for t in range(D_CONV):
            dw_ref[t:t + 1, :] = sums[1 + t]
        dw_ref[D_CONV:8, :] = jnp.zeros((8 - D_CONV, cb), F32)

        def second(i, carry):
            st = pl.multiple_of(i * r, r)
            win = dpad[pl.ds(st, r + 2 * HALO), :]
            acc = wv[0:1, :] * _shifted(win, HALO + 2, r)
            for t in range(1, D_CONV):
                acc = acc + wv[t:t + 1, :] * _shifted(win, HALO + 2 - t, r)
            dx_ref[pl.ds(st, r), :] = acc.astype(dx_ref.dtype)
            return carry

        lax.fori_loop(0, s // r, second, 0)

    col = pl.BlockSpec((s, cb), lambda j: (0, j))
    return pl.pallas_call(
        body, name=name, grid=(width // cb,),
        in_specs=[pl.BlockSpec((s, cb), lambda j: (0, XS0 // cb + c0 + j)), col, col,
                  pl.BlockSpec((8, cb), lambda j: (0, c0 + j)), pl.BlockSpec((1, cb), lambda j: (0, c0 + j))],
        out_specs=[col, pl.BlockSpec((8, cb), lambda j: (0, j)), pl.BlockSpec((1, cb), lambda j: (0, j))],
        out_shape=[jax.ShapeDtypeStruct((s, width), MMD), jax.ShapeDtypeStruct((8, width), F32),
                   jax.ShapeDtypeStruct((1, width), F32)],
        scratch_shapes=[pltpu.VMEM((s + 2 * HALO, cb), F32), pltpu.VMEM((s + 2 * HALO, cb), F32)],
        compiler_params=_cp(("arbitrary",), VMEM_BIG),
    )(proj, ga, gb, w8, brow)


def _tri(lower):
    i = jnp.arange(CHUNK)
    return ((i[:, None] >= i[None, :]) if lower else (i[:, None] <= i[None, :])).astype(F32)


def _dt_fwd(raw, bias, arow, *, name):
    s = raw.shape[0]

    def body(r_ref, b_ref, a_ref, lo_ref, up_ref, dt_ref, cs_ref):
        u = r_ref[...] + b_ref[...]
        dt = jnp.maximum(u, 0.0) + jnp.log1p(jnp.exp(-jnp.abs(u)))
        dt_ref[...] = dt
        a = dt * a_ref[...]
        lane = lax.broadcasted_iota(jnp.int32, (CHUNK, 128), 1)
        cs_ref[...] = jnp.where(lane < SSD_HEADS, _dot_hi(lo_ref[...], a), _dot_hi(up_ref[...], a))

    blk = pl.BlockSpec((CHUNK, 128), lambda i: (i, 0))
    row = pl.BlockSpec((1, 128), lambda i: (0, 0))
    tri = pl.BlockSpec((CHUNK, CHUNK), lambda i: (0, 0))
    return pl.pallas_call(
        body, name=name, grid=(s // CHUNK,), in_specs=[blk, row, row, tri, tri], out_specs=[blk, blk],
        out_shape=[jax.ShapeDtypeStruct((s, 128), F32)] * 2, compiler_params=_cp(("arbitrary",)),
    )(raw, bias, arow, _tri(True), _tri(False))


def _dt_bwd(ddt, raw, bias, *, name):
    s = raw.shape[0]
    tm = _tile(s, 1024)

    def body(d_ref, r_ref, b_ref, o_ref, db_ref):
        g = d_ref[...] * _sigmoid(r_ref[...] + b_ref[...])
        o_ref[...] = g.astype(o_ref.dtype)
        _acc_rows(db_ref, jnp.sum(g, axis=0, keepdims=True), pl.program_id(0) == 0)

    blk = pl.BlockSpec((tm, 128), lambda i: (i, 0))
    row = pl.BlockSpec((1, 128), lambda i: (0, 0))
    return pl.pallas_call(
        body, name=name, grid=(s // tm,), in_specs=[blk, blk, row], out_specs=[blk, row],
        out_shape=[jax.ShapeDtypeStruct((s, 128), MMD), jax.ShapeDtypeStruct((1, 128), F32)],
        compiler_params=_cp(("arbitrary",)),
    )(ddt, raw, bias)


GW = HPG * SSD_P


GPS = 4


def _ssd_specs(nc, rev):
    cc = (lambda c: nc - 1 - c) if rev else (lambda c: c)
    nb = SSD_GROUPS // GPS
    return dict(
        x=pl.BlockSpec((CHUNK, GPS * GW), lambda g, c: (cc(c), g)),
        b=pl.BlockSpec((CHUNK, GPS * SSD_N), lambda g, c: (cc(c), 2048 // (GPS * SSD_N) + g)),
        c=pl.BlockSpec((CHUNK, GPS * SSD_N), lambda g, c: (cc(c), 2048 // (GPS * SSD_N) + nb + g)),
        col=pl.BlockSpec((GPS, CHUNK, HPG), lambda g, c: (g, cc(c), 0)),
        lanes=pl.BlockSpec((CHUNK, 128), lambda g, c: (cc(c), 0)),
        rowt=pl.BlockSpec((GPS, 1, HPG, CHUNK), lambda g, c: (g, cc(c), 0, 0)),
        drow=pl.BlockSpec((1, GPS * GW), lambda g, c: (0, g)),
        y=pl.BlockSpec((CHUNK, GPS * GW), lambda g, c: (cc(c), g)),
        h=pl.BlockSpec((GPS, 1, SSD_N, GW), lambda g, c: (g, cc(c), 0, 0)),
        n=pl.BlockSpec((CHUNK, GPS * SSD_N), lambda g, c: (cc(c), g)),
    )


def _ssd_mask(anti):
    ii = lax.broadcasted_iota(jnp.int32, (CHUNK, CHUNK), 0)
    jj = lax.broadcasted_iota(jnp.int32, (CHUNK, CHUNK), 1)
    return ii, jj, (ii <= jj) if anti else (ii >= jj)


def _expand(x, ex, terms=3):
    h1 = x.astype(jnp.bfloat16)
    r1 = x - h1.astype(F32)
    h2 = r1.astype(jnp.bfloat16)
    out = _dot(h1, ex) + _dot(h2, ex)
    if terms == 3:
        out = out + _dot((r1 - h2.astype(F32)).astype(jnp.bfloat16), ex)
    return out


def _headsum(a, e):
    hi = a.astype(jnp.bfloat16)
    return _dot(hi, e) + _dot((a - hi.astype(F32)).astype(jnp.bfloat16), e)


def _expand_mats():
    lane = jnp.arange(128)[None, :, None]
    col = jnp.arange(GW)[None, None, :]
    base = (jnp.arange(2)[:, None] * SSD_HEADS + jnp.arange(SSD_GROUPS)[None, :] * HPG).reshape(2 * SSD_GROUPS, 1, 1)
    return (lane == base + col // SSD_P).astype(jnp.bfloat16)


def _headsum_mats():
    e1 = (jnp.arange(GW)[:, None] // SSD_P == jnp.arange(128)[None, :]).astype(jnp.bfloat16)
    e2 = (jnp.arange(HPG * CHUNK)[:, None] // CHUNK == jnp.arange(128)[None, :]).astype(jnp.bfloat16)
    return e1, e2


def _ssd_fwd(xc, dt, cs, cst, ex, drow, di, *, name):
    s = xc.shape[0]
    nc = s // CHUNK
    anti = di == 1
    sp = _ssd_specs(nc, anti)
    trow = 0 if anti else CHUNK - 1

    def body(x_ref, b_ref, c_ref, dt_ref, cs_ref, cst_ref, ex_ref, d_ref, y_ref, hp_ref, h_ref):
        @pl.when(pl.program_id(1) == 0)
        def _():
            h_ref[...] = jnp.zeros_like(h_ref)

        mask = _ssd_mask(anti)[2]
        dtv, csv = dt_ref[...], cs_ref[...]
        for gi in range(GPS):
            cols = slice(gi * GW, (gi + 1) * GW)
            ncols = slice(gi * SSD_N, (gi + 1) * SSD_N)
            ex = ex_ref[gi]
            xb = x_ref[:, cols].astype(F32)
            bm, cm = b_ref[:, ncols], c_ref[:, ncols]
            csr = cst_ref[gi, 0]
            dtf = _expand(dtv, ex, 2)
            csf = _expand(csv, ex)
            tl = csf[trow:trow + 1, :]
            h = h_ref[gi]
            hp_ref[gi, 0] = h.astype(hp_ref.dtype)
            g = _dot(cm, bm, NT)
            xs = xb * dtf
            xsm = xs.astype(MMD)
            base = jnp.exp(csf) * _dot(cm, h.astype(MMD)) + d_ref[:, cols] * xb
            for r in range(HPG):
                sl = slice(r * SSD_P, (r + 1) * SSD_P)
                lm = jnp.exp(jnp.where(mask, csf[:, r * SSD_P:r * SSD_P + 1] - csr[r:r + 1, :], NEG))
                y_ref[:, gi * GW + r * SSD_P:gi * GW + (r + 1) * SSD_P] = (
                    _dot((g * lm).astype(MMD), xsm[:, sl]) + base[:, sl]).astype(y_ref.dtype)
            xd = (xs * jnp.exp(tl - csf)).astype(MMD)
            h_ref[gi] = h * jnp.exp(tl) + _dot(bm, xd, TN)

    nb = SSD_GROUPS // GPS
    return pl.pallas_call(
        body, name=name, grid=(nb, nc),
        in_specs=[sp["x"], sp["b"], sp["c"], sp["lanes"], sp["lanes"], sp["rowt"],
                  pl.BlockSpec((GPS, 128, GW), lambda g, c: (di * nb + g, 0, 0)), sp["drow"]],
        out_specs=[sp["y"], sp["h"]],
        out_shape=[jax.ShapeDtypeStruct((s, 2048), MMD), jax.ShapeDtypeStruct((SSD_GROUPS, nc, SSD_N, GW), MMD)],
        scratch_shapes=[pltpu.VMEM((GPS, SSD_N, GW), F32)],
        compiler_params=_cp(("arbitrary", "arbitrary")),
    )(xc, xc, xc, dt, cs, cst, ex, drow)


def _ssd_bwd(xc, dt, cs, dt4, cst, ex, drow, arow4, dy, hprev, di, *, name):
    s = xc.shape[0]
    nc = s // CHUNK
    anti = di == 1
    sp = _ssd_specs(nc, not anti)
    trow = 0 if anti else CHUNK - 1
    e1, e2 = _headsum_mats()

    def body(x_ref, b_ref, c_ref, dt_ref, cs_ref, dt4_ref, cst_ref, ex_ref, d_ref, a_ref, dy_ref, hp_ref, tri_ref,
             e1_ref, e2_ref, dx_ref, db_ref, dc_ref, ddt_ref, da_ref, dh_ref, w_ref, dxs_ref):
        @pl.when(pl.program_id(1) == 0)
        def _():
            dh_ref[...] = jnp.zeros_like(dh_ref)
            da_ref[...] = jnp.zeros_like(da_ref)

        e1v = e1_ref[...]
        ii, _, mask = _ssd_mask(anti)
        dtv, csv = dt_ref[...], cs_ref[...]
        for gi in range(GPS):
            cols = slice(gi * GW, (gi + 1) * GW)
            ncols = slice(gi * SSD_N, (gi + 1) * SSD_N)
            ex = ex_ref[gi]
            xb = x_ref[:, cols].astype(F32)
            bm, cm = b_ref[:, ncols], c_ref[:, ncols]
            csr = cst_ref[gi, 0]
            dym = dy_ref[:, cols]
            dyb = dym.astype(F32)
            hpm = hp_ref[gi, 0]
            hp = hpm.astype(F32)
            dh = dh_ref[gi]
            dhm = dh.astype(MMD)
            dtf = _expand(dtv, ex, 2)
            csf = _expand(csv, ex)
            tl = csf[trow:trow + 1, :]
            e = jnp.exp(csf)
            dec = jnp.exp(tl - csf)
            et = jnp.exp(tl)
            xs = xb * dtf
            xsm = xs.astype(MMD)
            g = _dot(cm, bm, NT)
            z = _dot(cm, hpm)
            bdh = _dot(bm, dhm)
            dg = jnp.zeros((CHUNK, CHUNK), F32)
            wcols = jnp.zeros((CHUNK, CHUNK), F32)
            for r in range(HPG):
                sl = slice(r * SSD_P, (r + 1) * SSD_P)
                lm = jnp.exp(jnp.where(mask, csf[:, r * SSD_P:r * SSD_P + 1] - csr[r:r + 1, :], NEG))
                mm = g * lm
                dm = _dot(dym[:, sl], xsm[:, sl], NT)
                w = dm * mm
                w_ref[gi, :, r * CHUNK:(r + 1) * CHUNK] = w
                wcols = jnp.where(ii == r, jnp.sum(w, axis=0, keepdims=True), wcols)
                dg = dg + dm * lm
                dxs_ref[gi, :, sl] = _dot(mm.astype(MMD), dym[:, sl], TN)
            dxs = dxs_ref[gi] + dec * bdh
            dx_ref[:, cols] = (dxs * dtf + d_ref[:, cols] * dyb).astype(dx_ref.dtype)
            tb = xs * bdh * dec
            d_tot = jnp.sum(tb, axis=0, keepdims=True) + et * jnp.sum(dh * hp, axis=0, keepdims=True)
            d_tot = _headsum(jnp.broadcast_to(d_tot, (8, GW)), e1v)[0:1]
            dcs = (_headsum(dyb * (e * z) - tb, e1v) + _headsum(w_ref[gi], e2_ref[...]) - wcols.T
                   + jnp.where(ii == trow, d_tot, 0.0))
            da = _dot_hi(tri_ref[...], dcs)
            ddt_ref[gi] = (da * a_ref[gi] + _headsum(dxs * xb, e1v))[:, 0:HPG]
            da_ref[gi] += jnp.sum(da[:, 0:HPG] * dt4_ref[gi], axis=0, keepdims=True)
            dgm = dg.astype(MMD)
            dz = (e * dyb).astype(MMD)
            dc_ref[:, ncols] = (_dot(dgm, bm) + _dot(dz, hpm, NT)).astype(dc_ref.dtype)
            db_ref[:, ncols] = (_dot(dgm, cm, TN) + _dot((xs * dec).astype(MMD), dhm, NT)).astype(db_ref.dtype)
            dh_ref[gi] = dh * et + _dot(cm, dz, TN)

    nb = SSD_GROUPS // GPS
    const = lambda shape: pl.BlockSpec(shape, lambda g, c: (0,) * len(shape))
    return pl.pallas_call(
        body, name=name, grid=(nb, nc),
        in_specs=[sp["x"], sp["b"], sp["c"], sp["lanes"], sp["lanes"], sp["col"], sp["rowt"],
                  pl.BlockSpec((GPS, 128, GW), lambda g, c: (di * nb + g, 0, 0)), sp["drow"],
                  pl.BlockSpec((GPS, 1, 128), lambda g, c: (g, 0, 0)), sp["y"], sp["h"],
                  const((CHUNK, CHUNK)), const((GW, 128)), const((HPG * CHUNK, 128))],
        out_specs=[sp["y"], sp["n"], sp["n"], sp["col"], pl.BlockSpec((GPS, 1, HPG), lambda g, c: (g, 0, 0))],
        out_shape=[jax.ShapeDtypeStruct((s, 2048), MMD), jax.ShapeDtypeStruct((s, SSD_GROUPS * SSD_N), MMD),
                   jax.ShapeDtypeStruct((s, SSD_GROUPS * SSD_N), MMD), jax.ShapeDtypeStruct((SSD_GROUPS, s, HPG), F32),
                   jax.ShapeDtypeStruct((SSD_GROUPS, 1, HPG), F32)],
        scratch_shapes=[pltpu.VMEM((GPS, SSD_N, GW), F32), pltpu.VMEM((GPS, CHUNK, HPG * CHUNK), F32),
                        pltpu.VMEM((GPS, CHUNK, GW), F32)],
        compiler_params=_cp(("arbitrary", "arbitrary")),
    )(xc, xc, xc, dt, cs, dt4, cst, ex, drow, arow4, dy, hprev, _tri(anti), e1, e2)


def _gnorm_fwd(ya, yb, proj, w, *, name):
    s = ya.shape[0]
    tm = _tile(s, 256)

    def body(a_ref, b_ref, z_ref, w_ref, o_ref):
        zv = z_ref[...].astype(F32)
        t = (a_ref[...].astype(F32) + b_ref[...].astype(F32)) * (zv * _sigmoid(zv))
        r = lax.rsqrt(jnp.mean(t * t, axis=-1, keepdims=True) + EPS)
        o_ref[...] = ((t * r) * w_ref[...]).astype(o_ref.dtype)

    big = pl.BlockSpec((tm, 2048), lambda i: (i, 0))
    row = pl.BlockSpec((1, 2048), lambda i: (0, 0))
    return pl.pallas_call(
        body, name=name, grid=(s // tm,), in_specs=[big, big, big, row], out_specs=big,
        out_shape=jax.ShapeDtypeStruct((s, 2048), MMD), compiler_params=_cp(("arbitrary",)),
    )(ya, yb, proj, w)


def _gnorm_bwd(dout, ya, yb, proj, xc, w, *, name):
    s = ya.shape[0]
    tm = _tile(s, 256)

    def body(do_ref, a_ref, b_ref, z_ref, x_ref, w_ref, dy_ref, dz_ref, dw_ref, dd_ref):
        zv = z_ref[...].astype(F32)
        sg = _sigmoid(zv)
        sz = zv * sg
        y = a_ref[...].astype(F32) + b_ref[...].astype(F32)
        t = y * sz
        r = lax.rsqrt(jnp.mean(t * t, axis=-1, keepdims=True) + EPS)
        nv = t * r
        dov = do_ref[...].astype(F32)
        _acc_rows(dw_ref, jnp.sum(dov * nv, axis=0, keepdims=True), pl.program_id(0) == 0)
        dn = dov * w_ref[...]
        dt_ = r * (dn - nv * jnp.mean(dn * nv, axis=-1, keepdims=True))
        dy = dt_ * sz
        dy_ref[...] = dy.astype(dy_ref.dtype)
        dz_ref[...] = (dt_ * y * (sg * (1.0 + zv * (1.0 - sg)))).astype(dz_ref.dtype)
        _acc_rows(dd_ref, jnp.sum(dy * x_ref[...].astype(F32), axis=0, keepdims=True), pl.program_id(0) == 0)

    big = pl.BlockSpec((tm, 2048), lambda i: (i, 0))
    row = pl.BlockSpec((1, 2048), lambda i: (0, 0))
    return pl.pallas_call(
        body, name=name, grid=(s // tm,), in_specs=[big, big, big, big, big, row], out_specs=[big, big, row, row],
        out_shape=[jax.ShapeDtypeStruct((s, 2048), MMD), jax.ShapeDtypeStruct((s, 2048), MMD),
                   jax.ShapeDtypeStruct((1, 2048), F32), jax.ShapeDtypeStruct((1, 2048), F32)],
        compiler_params=_cp(("arbitrary",)),
    )(dout, ya, yb, proj, xc, w)


def _heads(a, n):
    return a.reshape(a.shape[0], n, HEAD_DIM).transpose(1, 0, 2)


def _unheads(a):
    return a.transpose(1, 0, 2).reshape(a.shape[1], a.shape[0] * HEAD_DIM)


def _per_group(a):
    return a.reshape(a.shape[0], SSD_GROUPS, HPG).transpose(1, 0, 2)


def _per_group_t(a):
    s = a.shape[0]
    return a.reshape(s // CHUNK, CHUNK, SSD_GROUPS, HPG).transpose(2, 0, 3, 1)


def _local_step(x, target, mod, wts, small, late_weights=None, late_grads=None, in_grad=None):
    s, d = x.shape
    shift1, scale1, gate1, shift2, scale2, gate2 = [mod[i:i + 1] for i in range(6)]

    h1 = _ln_mod(x, small["norm1_w"], scale1, shift1, name="ln1")
    proj = _mm(h1, wts["w_in_p"], name="in_proj", outs=[MMD], tm=512, tn=2944, b_outer=True)
    dt_raw = _mm(h1, wts["w_dt"], name="dt_proj", outs=[F32], tm=512, tn=128)

    qk_w = jnp.concatenate([jnp.tile(small["q_norm_w"], (1, N_Q_HEADS)), jnp.tile(small["k_norm_w"], (1, N_KV_HEADS))], axis=1)
    qk_sc = jnp.concatenate([jnp.full((1, N_Q_HEADS * HEAD_DIM), HEAD_DIM ** -0.5, F32),
                             jnp.ones((1, N_KV_HEADS * HEAD_DIM), F32)], axis=1)
    qk_sc2 = jnp.concatenate([jnp.full((1, N_Q_HEADS * HEAD_DIM), HEAD_DIM ** -0.5 * LOG2E, F32),
                              jnp.ones((1, N_KV_HEADS * HEAD_DIM), F32)], axis=1)
    tabs = _rope_tables(s)
    qk, qkt = _qk_fwd(proj, qk_w, qk_sc2, tabs, name="qk_fwd")
    qkt = qkt.reshape(N_Q_HEADS + N_KV_HEADS, HEAD_DIM, s)
    k_h = _heads(qk[:, N_Q_HEADS * HEAD_DIM:], N_KV_HEADS)
    v_sd = proj[:, V0:V0 + N_KV_HEADS * HEAD_DIM]
    v_h = _heads(v_sd, N_KV_HEADS)
    vta = jnp.concatenate([v_sd.T.reshape(N_KV_HEADS, HEAD_DIM, s), jnp.ones((N_KV_HEADS, V_AUG - HEAD_DIM, s), MMD)], axis=1)
    ot, lse = _flash_fwd(qkt, vta, name="flash_fwd")
    ot2 = ot.reshape(N_Q_HEADS * HEAD_DIM, s)
    if late_weights is not None:
        wts = {**wts, **late_weights(ot)}

    w8 = jnp.pad(small["conv_w"], ((0, 8 - D_CONV), (0, 0)))
    xc = _conv_fwd(proj, w8, small["conv_b"], name="conv_fwd")
    a_neg = -jnp.exp(small["A_log"])
    arow = jnp.pad(a_neg.reshape(1, 2 * SSD_HEADS), ((0, 0), (0, 128 - 2 * SSD_HEADS)))
    bias_row = jnp.pad(small["dt_bias"].reshape(1, 2 * SSD_HEADS), ((0, 0), (0, 128 - 2 * SSD_HEADS)))
    dt, cs = _dt_fwd(dt_raw, bias_row, arow, name="dt_fwd")
    drow = jnp.repeat(small["ssd_D"], SSD_P, axis=1)
    dirs = []
    for di in range(2):
        cols = slice(di * SSD_HEADS, (di + 1) * SSD_HEADS)
        dirs.append(dict(
            dt4=_per_group(dt[:, cols]), cst=_per_group_t(cs[:, cols]),
            drow=drow if di == 0 else jnp.zeros_like(drow),
            arow4=jnp.pad(a_neg[di].reshape(SSD_GROUPS, 1, HPG), ((0, 0), (0, 0), (0, 128 - HPG)))))
    ex = _expand_mats()
    ys = []
    for di, dd in enumerate(dirs):
        y, dd["hprev"] = _ssd_fwd(xc, dt, cs, dd["cst"], ex, dd["drow"], di, name=f"ssd_fwd{di}")
        ys.append(y)
    ssdn = _gnorm_fwd(ys[0], ys[1], proj, small["ssd_norm_w"], name="gnorm_fwd")

    a_o = _mm(ot2, wts["w_attn_out"], name="attn_out", outs=[MMD], ta=True, tm=512, tn=1024)

    def merge_epi(acc, ao, ga, gs):
        return (_sigmoid(ga.astype(F32)) * ao.astype(F32) + _sigmoid(gs.astype(F32)) * acc, acc)

    merged, b_o = _mm(ssdn, wts["w_ssd_out"], name="ssd_out", outs=[MMD, MMD], tm=512, tn=1024,
                      extras=[(a_o, "tile", 0), (proj, "tile", GA0), (proj, "tile", GS0)], epi=merge_epi)

    def res_epi(acc, res, gate):
        return (res + gate * acc, acc)

    x1, mo = _mm(merged, wts["w_o"], name="w_o", outs=[F32, MMD], tm=512, tn=1024,
                 extras=[(x, "tile", 0), (gate1, "row", 0)], epi=res_epi)
    h2 = _ln_mod(x1, small["norm2_w"], scale2, shift2, name="ln2")

    def relu2_epi(acc):
        rl = jnp.maximum(acc, 0.0)
        return (rl * rl, rl)

    act, rl = _mm(h2, wts["w_mlp1"], name="mlp1", outs=[MMD, MMD], tm=1024, tn=1024, epi=relu2_epi, b_outer=True)

    def loss_epi(acc, res, gate, tgt):
        return ((res + gate * acc - tgt) * (1.0 / d), acc)

    dy, ffo = _mm(act, wts["w_mlp2"], name="mlp2", outs=[F32, MMD], tm=512, tn=1024, vmem=VMEM_BIG,
                  extras=[(x1, "tile", 0), (gate2, "row", 0), (target, "tile", 0)], epi=loss_epi)
    loss = _sumsq(dy, name="loss") * (0.5 * d)

    gw = {}
    gs_ = {}
    dffo, dgate2 = _gate_bwd(dy, ffo, gate2, name="gate2_bwd")
    dpre = _mm(dffo, wts["w_mlp2"], name="mlp2_dx", outs=[MMD], nt=True, tm=1024, tn=1024, b_outer=True,
               extras=[(rl, "tile", 0)], epi=lambda acc, r: (acc * (2.0 * r.astype(F32)),))
    gw["w_mlp2"] = _mm_tn(act, dffo, name="mlp2_dw")
    dh2 = _mm(dpre, wts["w_mlp1"], name="mlp1_dx", outs=[F32], nt=True, tm=1024, tn=1024, vmem=VMEM_BIG)
    gw["w_mlp1"] = _mm_tn(h2, dpre, name="mlp1_dw")
    dx1, dshift2, dscale2, gs_["norm2_w"] = _ln_mod_bwd(dh2, x1, small["norm2_w"], scale2, dy, name="ln2_bwd")
    dmo, dgate1 = _gate_bwd(dx1, mo, gate1, name="gate1_bwd")

    def merge_bwd_epi(acc, ao, bo, ga, gs):
        sa, ss = _sigmoid(ga.astype(F32)), _sigmoid(gs.astype(F32))
        return (acc * sa, acc * ss, acc * ao.astype(F32) * sa * (1.0 - sa), acc * bo.astype(F32) * ss * (1.0 - ss))

    da_o, db_o, dga, dgs = _mm(dmo, wts["w_o"], name="w_o_dx", outs=[MMD] * 4, nt=True, tm=512, tn=1024,
                               extras=[(a_o, "tile", 0), (b_o, "tile", 0), (proj, "tile", GA0), (proj, "tile", GS0)],
                               epi=merge_bwd_epi)
    gw["w_o"] = _mm_tn(merged, dmo, name="w_o_dw")
    dot = _mm(wts["w_attn_out"], da_o, name="attn_out_dx", outs=[MMD], nt=True, tm=1024, tn=1024)
    gw["w_attn_out"] = _mm(ot2, da_o, name="attn_out_dw", outs=[F32], tm=256, tn=512, vmem=VMEM_BIG)
    dssdn = _mm(db_o, wts["w_ssd_out"], name="ssd_out_dx", outs=[MMD], nt=True, tm=512, tn=2048)
    gw["w_ssd_out"] = _mm_tn(ssdn, db_o, name="ssd_out_dw")

    norm_w = small["ssd_norm_w"] if late_grads is None else small["ssd_norm_w"] + late_grads(gw)
    dyssd, dz, gs_["ssd_norm_w"], dd_row = _gnorm_bwd(dssdn, ys[0], ys[1], proj, xc, norm_w, name="gnorm_bwd")
    gs_["ssd_D"] = dd_row.reshape(SSD_HEADS, SSD_P).sum(axis=1).reshape(1, SSD_HEADS)
    dxc, ddts, das = [], [], []
    for di, dd in enumerate(dirs):
        dxs, dbm, dcm, ddt4, da4 = _ssd_bwd(xc, dt, cs, dd["dt4"], dd["cst"], ex, dd["drow"], dd["arow4"],
                                            dyssd, dd["hprev"], di, name=f"ssd_bwd{di}")
        dxc.append((dxs, dbm, dcm))
        ddts.append(ddt4.transpose(1, 0, 2).reshape(s, SSD_HEADS))
        das.append(da4.reshape(1, SSD_HEADS))
    conv_parts, col0 = [], 0
    for part, (ga, gb) in enumerate(zip(*dxc)):
        conv_parts.append(_conv_bwd(proj, col0, ga, gb, w8, small["conv_b"], name=f"conv_bwd{part}"))
        col0 += ga.shape[1]
    dxbc, dw8, gs_["conv_b"] = [jnp.concatenate(t, axis=1) for t in zip(*conv_parts)]
    gs_["conv_w"] = dw8[0:D_CONV]
    gs_["A_log"] = jnp.concatenate(das, axis=0) * a_neg
    ddt = jnp.pad(jnp.concatenate(ddts, axis=1), ((0, 0), (0, 128 - 2 * SSD_HEADS)))
    ddt_raw, dbias = _dt_bwd(ddt, dt_raw, bias_row, name="dt_bwd")
    gs_["dt_bias"] = dbias[:, 0:2 * SSD_HEADS].reshape(2, SSD_HEADS)

    dqt, dk_h, dv_h = _flash_bwd(qkt, k_h, v_h, dot.reshape(N_Q_HEADS, HEAD_DIM, s), ot, lse, name="flash_bwd")
    dqk_u, dqk_w = _qk_bwd(dqt.reshape(N_Q_HEADS * HEAD_DIM, s), dk_h.transpose(0, 2, 1).reshape(N_KV_HEADS * HEAD_DIM, s),
                           proj, qk_w, qk_sc, tabs, name="qk_bwd")
    gs_["q_norm_w"] = dqk_w[:, 0:N_Q_HEADS * HEAD_DIM].reshape(N_Q_HEADS, HEAD_DIM).sum(axis=0, keepdims=True)
    gs_["k_norm_w"] = dqk_w[:, N_Q_HEADS * HEAD_DIM:].reshape(N_KV_HEADS, HEAD_DIM).sum(axis=0, keepdims=True)
    dv = _unheads(dv_h).astype(MMD)

    dproj = jnp.concatenate([dz, dga, dgs, dxbc, dqk_u, dv, ddt_raw], axis=1)
    gw["w_in_p"] = _mm_tn(h1, dproj, name="in_proj_dw", tk=512, tn=2944, tmm=2048, vmem=VMEM_BIG)
    zero_row = jnp.zeros((1, d), F32) if in_grad is None else jnp.zeros((1, d), F32) + in_grad(gw["w_in_p"])[0:1, 0:1]
    dh1 = _mm(dproj, wts["w_in_p"], name="in_proj_dx", outs=[F32], nt=True, tm=256, tn=1024, vmem=VMEM_BIG,
              extras=[(zero_row, "row", 0)], epi=lambda acc, r: (acc + r,))
    grad_x, dshift1, dscale1, gs_["norm1_w"] = _ln_mod_bwd(dh1, x, small["norm1_w"], scale1, dx1, name="ln1_bwd")
    dmod = jnp.concatenate([dshift1, dscale1, dgate1, dshift2, dscale2, dgate2], axis=0)
    return loss, grad_x, dmod, gw, gs_


N_DEV = 8
N_CHIP = 4
ANY = pl.BlockSpec(memory_space=pl.ANY)


def _place():
    return lax.axis_index("x"), lax.axis_index("y"), lax.axis_index("c")


def _allgather8(v, *, name):
    m_per, n = v.shape

    def body(x_ref, out_ref, send_sems, recv_sems, local_sem):
        x, y, c = _place()
        me, sibling = (x, y, c), (x, y, 1 - c)
        chips = [(1 - x, y), (x, 1 - y), (1 - x, 1 - y)]

        def rows(px, py, pc):
            return out_ref.at[pl.ds((4 * px + 2 * py + pc) * m_per, m_per), :]

        def copy(k, block, to, src=None):
            return pltpu.make_async_remote_copy(
                src_ref=rows(*block) if src is None else src, dst_ref=rows(*block),
                send_sem=send_sems.at[k], recv_sem=recv_sems.at[k], device_id=to, device_id_type=MESH)

        mine = pltpu.make_async_copy(x_ref, rows(*me), local_sem)
        mine.start()
        first = [copy(0, me, sibling, src=x_ref)]
        first += [copy(1 + j, me, (*chip, c), src=x_ref) for j, chip in enumerate(chips)]
        for cp in first:
            cp.start()
        passed = [copy(4 + j, (*chip, c), sibling) for j, chip in enumerate(chips)]
        for j, chip in enumerate(chips):
            copy(1 + j, (*chip, c), me).wait_recv()
            passed[j].start()
        copy(0, sibling, me).wait_recv()
        for j, chip in enumerate(chips):
            copy(4 + j, (*chip, 1 - c), me).wait_recv()
        for cp in first + passed:
            cp.wait_send()
        mine.wait()

    return pl.pallas_call(
        body, name=name, out_shape=jax.ShapeDtypeStruct((N_DEV * m_per, n), v.dtype),
        in_specs=[pl.BlockSpec(memory_space=pltpu.VMEM)], out_specs=pl.BlockSpec(memory_space=pltpu.VMEM),
        scratch_shapes=[pltpu.SemaphoreType.DMA((7,)), pltpu.SemaphoreType.DMA((7,)), pltpu.SemaphoreType.DMA],
    )(v)


HBM = pl.BlockSpec(memory_space=pltpu.HBM)
SEM = pl.BlockSpec(memory_space=pltpu.SEMAPHORE)


def _chips_copies(x_ref, land_ref, sems, scatter):
    x, y, c = _place()
    k = 2 * x + y
    chips = [(1 - x, y), (x, 1 - y), (1 - x, 1 - y)]
    ids = [2 * cx + cy for cx, cy in chips]

    def copy(j, slot):
        return pltpu.make_async_remote_copy(
            src_ref=x_ref.at[ids[j]] if scatter else x_ref, dst_ref=land_ref.at[slot], send_sem=sems[j],
            recv_sem=sems[3 + j], device_id=(*chips[j], c), device_id_type=MESH)

    return [copy(j, k) for j in range(3)], [copy(j, ids[j]) for j in range(3)]


def _chips_start(src, scatter, *, name):
    shape = src.shape if scatter else (N_CHIP,) + tuple(src.shape)

    def body(x_ref, land_ref, *rest):
        sems, token = rest[0:6], rest[8]
        for cp in _chips_copies(x_ref, land_ref, sems, scatter)[0]:
            cp.start()
        token[...] = jnp.zeros_like(token)

    out = pl.pallas_call(
        body, name=name,
        out_shape=(pltpu.SemaphoreType.DMA(()),) * 6 + (pltpu.HBM(src.shape, src.dtype), pltpu.HBM(shape, src.dtype),
                                                       jax.ShapeDtypeStruct((8, 128), F32)),
        in_specs=(HBM, HBM), out_specs=(SEM,) * 6 + (HBM, HBM, pl.BlockSpec(memory_space=pltpu.VMEM)),
        input_output_aliases={0: 6, 1: 7},
        compiler_params=pltpu.CompilerParams(has_side_effects=pltpu.SideEffectType.DATAFLOW_SIDE_EFFECTING),
    )(pltpu.with_memory_space_constraint(src, pltpu.HBM),
      pltpu.with_memory_space_constraint(lax.empty(shape, src.dtype), pltpu.HBM))
    return out[0:6], out[6], out[7], out[8]


def _chips_wait(sems, src, land, after, scatter, *, name):
    def body(x_ref, land_ref, *rest):
        sems_ = rest[0:6]
        for cp in _chips_copies(x_ref, land_ref, sems_, scatter)[1]:
            cp.wait_send()
            cp.wait_recv()

    return pl.pallas_call(
        body, name=name, out_shape=(pltpu.HBM(src.shape, src.dtype), pltpu.HBM(land.shape, land.dtype)),
        in_specs=(HBM, HBM) + (SEM,) * 6 + (ANY,), out_specs=(HBM, HBM), input_output_aliases={0: 0, 1: 1},
        compiler_params=pltpu.CompilerParams(has_side_effects=pltpu.SideEffectType.DATAFLOW_SIDE_EFFECTING),
    )(src, land, *sems, after)


def _row_tile(r, pref=512):
    return max(t for t in range(16, pref + 1, 16) if r % t == 0)


def _gather_weights(src, *, name):
    r = src.shape[0]
    hr = r // 2
    assert r == 2 * hr and hr % 16 == 0

    def body(x_ref, out_ref, send_sems, recv_sems):
        x, y, c = _place()
        k = 2 * x + y
        chips = [(1 - x, y), (x, 1 - y), (1 - x, 1 - y)]
        ids = [2 * cx + cy for cx, cy in chips]
        mine_rows = pl.ds(pl.multiple_of(c * hr, 16), hr)
        other_rows = pl.ds(pl.multiple_of((1 - c) * hr, 16), hr)

        def copy(sem, src_ref, slot, rows, to):
            return pltpu.make_async_remote_copy(
                src_ref=src_ref, dst_ref=out_ref.at[slot, rows], send_sem=send_sems.at[sem], recv_sem=recv_sems.at[sem],
                device_id=to, device_id_type=MESH)

        sends = [copy(j, x_ref.at[mine_rows], k, mine_rows, (cx, cy, c)) for j, (cx, cy) in enumerate(chips)]
        for cp in sends:
            cp.start()
        passed = [copy(3 + j, out_ref.at[ids[j], mine_rows], ids[j], mine_rows, (x, y, 1 - c)) for j in range(3)]
        for j, (cx, cy) in enumerate(chips):
            copy(j, x_ref.at[mine_rows], ids[j], mine_rows, (cx, cy, c)).wait_recv()
            passed[j].start()
        for j in range(3):
            copy(3 + j, out_ref.at[ids[j], other_rows], ids[j], other_rows, (x, y, 1 - c)).wait_recv()
        for cp in sends + passed:
            cp.wait_send()

    return pl.pallas_call(
        body, name=name, out_shape=jax.ShapeDtypeStruct((N_CHIP,) + tuple(src.shape), src.dtype),
        in_specs=[ANY], out_specs=ANY,
        scratch_shapes=[pltpu.SemaphoreType.DMA((6,)), pltpu.SemaphoreType.DMA((6,))],
    )(src)


def _pair_swap(a, *, name):
    n, r, cols = a.shape
    hr = r // 2

    def body(x_ref, out_ref, send_sem, recv_sem):
        x, y, c = _place()
        other_rows = pl.ds(pl.multiple_of((1 - c) * hr, 16), hr)
        cp = pltpu.make_async_remote_copy(src_ref=x_ref.at[:, other_rows], dst_ref=out_ref, send_sem=send_sem,
                                          recv_sem=recv_sem, device_id=(x, y, 1 - c), device_id_type=MESH)
        cp.start()
        cp.wait()

    return pl.pallas_call(
        body, name=name, out_shape=jax.ShapeDtypeStruct((n, hr, cols), a.dtype), in_specs=[ANY], out_specs=ANY,
        scratch_shapes=[pltpu.SemaphoreType.DMA, pltpu.SemaphoreType.DMA],
    )(a)


def _sibling_copy(a, *, name):
    def body(x_ref, out_ref, send_sem, recv_sem):
        x, y, c = _place()
        cp = pltpu.make_async_remote_copy(src_ref=x_ref, dst_ref=out_ref, send_sem=send_sem, recv_sem=recv_sem,
                                          device_id=(x, y, 1 - c), device_id_type=MESH)
        cp.start()
        cp.wait()

    return pl.pallas_call(
        body, name=name, out_shape=jax.ShapeDtypeStruct(a.shape, a.dtype), in_specs=[ANY], out_specs=ANY,
        scratch_shapes=[pltpu.SemaphoreType.DMA, pltpu.SemaphoreType.DMA],
    )(a)


def _sum_slots(a, *, name):
    _, r, c = a.shape
    tr = _row_tile(r, 256)

    def body(a_ref, o_ref):
        acc = a_ref[0].astype(F32)
        for j in range(1, N_CHIP):
            acc = acc + a_ref[j].astype(F32)
        o_ref[...] = acc

    return pl.pallas_call(
        body, name=name, grid=(r // tr,), in_specs=[pl.BlockSpec((N_CHIP, tr, c), lambda i: (0, i, 0))],
        out_specs=pl.BlockSpec((tr, c), lambda i: (i, 0)), out_shape=jax.ShapeDtypeStruct((r, c), F32),
        compiler_params=_cp(("arbitrary",)),
    )(a)


def _add2(a, b, *, name):
    r, c = a.shape
    tr = _row_tile(r)

    def body(a_ref, b_ref, o_ref):
        o_ref[...] = (a_ref[...].astype(F32) + b_ref[...].astype(F32)).astype(o_ref.dtype)

    spec = pl.BlockSpec((tr, c), lambda i: (i, 0))
    return pl.pallas_call(
        body, name=name, grid=(r // tr,), in_specs=[spec, spec], out_specs=spec,
        out_shape=jax.ShapeDtypeStruct((r, c), a.dtype), compiler_params=_cp(("arbitrary",)),
    )(a, b)


BIG = ("w_in", "w_mlp1", "w_attn_out", "w_ssd_out", "w_o", "w_mlp2")
COL_SHARDED = ("w_mlp1", "w_in")
ROW_SHARDED = ("w_attn_out", "w_ssd_out", "w_o", "w_mlp2")
LATE = ROW_SHARDED + ("w_mlp1",)
SMALL = ("b_ada", "norm1_w", "norm2_w", "q_norm_w", "k_norm_w", "conv_b", "A_log", "dt_bias", "ssd_D", "ssd_norm_w")
NAMES = ("w_ada", "b_ada", "norm1_w", "norm2_w", "w_in", "q_norm_w", "k_norm_w", "conv_w", "conv_b", "A_log", "dt_bias",
         "ssd_D", "ssd_norm_w", "w_attn_out", "w_ssd_out", "w_o", "w_mlp1", "w_mlp2")
W_IN_COLS = 8768


def _permute_in(w):
    return jnp.concatenate([w[:, 4608:6656], w[:, 6720:8768], w[:, 1536:4608], w[:, 0:1536], w[:, 6656:6720],
                            jnp.zeros((w.shape[0], PW - W_IN_COLS), w.dtype)], axis=1)


def _unpermute_in(wp):
    return jnp.concatenate([wp[:, Q0:DT0], wp[:, XS0:Q0], wp[:, Z0:GA0], wp[:, DT0:DT0 + 64], wp[:, GA0:XS0]], axis=1)


def _pad_to(v, n):
    return jnp.pad(v, (0, n - v.shape[0]))


def _step(w, m, v, loss_target):
    xi, yi, ci = _place()
    chip = 2 * xi + yi
    dev = 4 * xi + 2 * yi + ci
    x, tgt = w["x"], loss_target
    d = x.shape[1]

    cw = w["conv_w"].shape[1]
    v0 = _pad_to(jnp.concatenate([w["c"].reshape(-1), w["conv_w"].reshape(-1)]), 5120).reshape(8, 640)
    g0 = _allgather8(v0, name="ag_cond").reshape(N_DEV, 5120)
    c_all = g0[:, 0:d]
    conv_w = jnp.concatenate([g0[2 * k, d:d + D_CONV * cw].reshape(D_CONV, cw) for k in range(N_CHIP)], axis=1)
    sc = _silu_cast(c_all, name="silu_c")
    modp = _mm(sc, w["w_ada"].astype(MMD), name="ada_fwd", outs=[F32], tm=8, tn=512)
    g1 = _allgather8(modp, name="ag_mod").reshape(N_DEV, N_DEV, modp.shape[1])
    mod_all = jnp.concatenate([g1[2 * k] for k in range(N_CHIP)], axis=1)
    mod = (lax.dynamic_slice_in_dim(mod_all, dev, 1, axis=0) + w["b_ada"]).reshape(6, d)

    mine, mod = lax.optimization_barrier((w["w_in"].astype(MMD), mod))
    gath = lax.dynamic_update_slice_in_dim(_gather_weights(mine, name="ag_w_in"), mine[None], chip, axis=0)
    late_mine = jnp.concatenate([w[n].astype(MMD) for n in LATE], axis=0)
    late_mine, gath = lax.optimization_barrier((late_mine, gath))
    ag_sems, ag_src, ag_land, ag_token = _chips_start(late_mine, False, name="ag_late_start")
    mod = mod + ag_token[0:1, 0:1]
    w_in = jnp.concatenate([gath[k] for k in range(N_CHIP)], axis=1)
    wts = {"w_in_p": _permute_in(w_in), "w_dt": jnp.pad(w_in[:, 6656:6720], ((0, 0), (0, 64)))}
    small = {n: w[n] for n in SMALL if n != "b_ada"}
    small["conv_w"] = conv_w

    def own_slot(land, src):
        return lax.dynamic_update_slice_in_dim(land, src, chip, axis=0)

    def late_weights(after):
        src, land = _chips_wait(ag_sems, ag_src, ag_land, after, False, name="ag_late_wait")
        land = own_slot(land, src[None])
        out, o = {}, 0
        for n in LATE:
            rows = w[n].shape[0]
            part = land[:, o:o + rows]
            out[n] = (jnp.concatenate([part[k] for k in range(N_CHIP)], axis=1) if n in COL_SHARDED
                      else part.reshape(N_CHIP * rows, w[n].shape[1]))
            o += rows
        return out

    def pair_sums(slots, tag):
        _, rows, cols = slots.shape
        hr = rows // 2
        theirs = _pair_swap(slots, name="rs_pair_" + tag)
        ours = lax.dynamic_slice_in_dim(slots, ci * hr, hr, axis=1)
        pair = _add2(ours.reshape(N_CHIP * hr, cols), theirs.reshape(N_CHIP * hr, cols), name="rs_pair_sum_" + tag)
        return pair.reshape(N_CHIP, hr, cols)

    def finish(recv, pair, tag):
        recv = own_slot(recv, lax.dynamic_slice_in_dim(pair, chip, 1, axis=0))
        half = _sum_slots(recv, name="rs_sum_" + tag)
        other = _sibling_copy(half, name="rs_sibling_" + tag)
        return jnp.where(ci == 0, jnp.concatenate([half, other], axis=0), jnp.concatenate([other, half], axis=0))

    started = {}

    def late_grads(gw):
        slots = []
        for k in range(N_CHIP):
            parts = []
            for n in LATE:
                rows = w[n].shape[0]
                blk = gw[n][:, k * rows:(k + 1) * rows] if n in COL_SHARDED else gw[n][k * rows:(k + 1) * rows]
                parts.append(blk.astype(MMD))
            slots.append(jnp.concatenate(parts, axis=0))
        pair = pair_sums(jnp.stack(slots), "late")
        sems, src, land, token = _chips_start(pair, True, name="rs_late_start")
        started["late"] = (sems, src, land)
        return token[0:1, 0:1]

    def in_grad(g):
        g_in = _unpermute_in(g)
        cols_in = w["w_in"].shape[1]
        pair = pair_sums(jnp.stack([g_in[:, k * cols_in:(k + 1) * cols_in].astype(MMD) for k in range(N_CHIP)]), "w_in")
        sems, src, land, token = _chips_start(pair, True, name="rs_w_in_start")
        started["w_in"] = (sems, src, land)
        return token

    loss, grad_x, dmod, gw, gs = _local_step(x, tgt, mod, wts, small, late_weights, late_grads, in_grad)

    grads = {}
    pair, land = _chips_wait(*started["w_in"], grad_x, True, name="rs_w_in_wait")
    grads["w_in"] = finish(land, pair, "w_in")
    pair, land = _chips_wait(*started["late"], grad_x, True, name="rs_late_wait")
    total, o = finish(land, pair, "late"), 0
    for n in LATE:
        rows = w[n].shape[0]
        grads[n] = total[o:o + rows]
        o += rows

    order = ([dmod.reshape(-1)] + [gs[n].reshape(-1) for n in SMALL if n != "b_ada"] + [gs["conv_w"].reshape(-1)]
             + [loss.reshape(-1)])
    vec = jnp.concatenate(order)
    n_small = vec.shape[0]
    n_pad = -(-n_small // 1024) * 1024
    g2 = _allgather8(_pad_to(vec, n_pad).reshape(8, n_pad // 8), name="ag_small")
    tot = _rows_sum(g2, N_DEV, name="small_sum").reshape(-1)
    loss = tot[n_small - 1]
    dmod_all = g2.reshape(N_DEV, n_pad)[:, 0:6 * d]
    off = 0
    for n in SMALL:
        grads[n] = tot[off:off + w[n].size].reshape(w[n].shape)
        off += w[n].size
    conv_full = tot[off:off + D_CONV * N_CHIP * cw].reshape(D_CONV, N_CHIP * cw)
    grads["conv_w"] = lax.dynamic_slice_in_dim(conv_full, chip * cw, cw, axis=1)
    ada_cols = w["w_ada"].shape[1]
    dmod_mine = lax.dynamic_slice_in_dim(dmod_all, chip * ada_cols, ada_cols, axis=1).astype(MMD)
    grads["w_ada"] = _mm_tn(sc, dmod_mine, name="ada_dw", tk=512, tn=512, tmm=8)

    delta, new_m, new_v = {}, {}, {}
    pack = lambda t: jnp.concatenate([t[n].reshape(-1) for n in SMALL]).reshape(1, -1)
    ds_, ms_, vs_ = _adamw(pack(w), pack(grads), pack(m), pack(v), name="adamw_small")
    off = 0
    for n in SMALL:
        for dst, src in ((delta, ds_), (new_m, ms_), (new_v, vs_)):
            dst[n] = src[0, off:off + w[n].size].reshape(w[n].shape)
        off += w[n].size
    for n in ("w_ada", "conv_w") + BIG:
        delta[n], new_m[n], new_v[n] = _adamw(w[n], grads[n], m[n], v[n], name="adamw_" + n)
    return loss, grad_x, grads, delta, new_m, new_v


def kernel(x, c, w_ada, b_ada, norm1_w, norm2_w, w_in, q_norm_w, k_norm_w, conv_w, conv_b, A_log, dt_bias, ssd_D, ssd_norm_w, w_attn_out, w_ssd_out, w_o, w_mlp1, w_mlp2, loss_target, m_w_ada, m_b_ada, m_norm1_w, m_norm2_w, m_w_in, m_q_norm_w, m_k_norm_w, m_conv_w, m_conv_b, m_A_log, m_dt_bias, m_ssd_D, m_ssd_norm_w, m_w_attn_out, m_w_ssd_out, m_w_o, m_w_mlp1, m_w_mlp2, v_w_ada, v_b_ada, v_norm1_w, v_norm2_w, v_w_in, v_q_norm_w, v_k_norm_w, v_conv_w, v_conv_b, v_A_log, v_dt_bias, v_ssd_D, v_ssd_norm_w, v_w_attn_out, v_w_ssd_out, v_w_o, v_w_mlp1, v_w_mlp2):
    args = dict(locals())
    strip = lambda a: a[0] if a.ndim == 3 else a
    w = {n: strip(args[n]) for n in NAMES + ("x", "c")}
    m = {n: strip(args["m_" + n]) for n in NAMES}
    v = {n: strip(args["v_" + n]) for n in NAMES}
    loss, grad_x, grads, delta, new_m, new_v = _step(w, m, v, loss_target[0])
    like = lambda t, n: t.reshape(args[n].shape)
    return (loss, grad_x[None], *[like(grads[n], n) for n in NAMES], *[like(delta[n], n) for n in NAMES],
            *[like(new_m[n], n) for n in NAMES], *[like(new_v[n], n) for n in NAMES])
```

```python
import math

import jax
import jax.numpy as jnp
from jax import lax
from jax.experimental import pallas as pl
from jax.experimental.pallas import tpu as pltpu

F32 = jnp.float32
MMD = jnp.bfloat16
EPS = 1e-6
NEG = -1e30
MIB = 1024 * 1024
VMEM_BIG = 56 * MIB
VMEM_MID = 40 * MIB

GRID_W = 64
N_Q_HEADS, N_KV_HEADS, HEAD_DIM = 16, 4, 64
ROPE_THETA = 10000.0
SSD_HEADS, SSD_GROUPS, SSD_P, SSD_N, CHUNK = 32, 4, 64, 128, 128
HPG = SSD_HEADS // SSD_GROUPS
D_CONV = 5
ADAM_LR, ADAM_B1, ADAM_B2, ADAM_EPS, ADAM_WD, ADAM_STEP = 0.001, 0.9, 0.999, 1e-08, 0.01, 10

Z0, GA0, GS0, XS0, B0, C0, Q0, K0, V0, DT0, PW = 0, 2048, 3072, 4096, 6144, 6656, 7168, 8192, 8448, 8704, 8832

MESH = pl.DeviceIdType.MESH
NT = (((1,), (1,)), ((), ()))
TN = (((0,), (0,)), ((), ()))


def _cp(sem=None, vmem=VMEM_MID):
    return pltpu.CompilerParams(dimension_semantics=sem, vmem_limit_bytes=vmem)


def _tile(n, pref):
    t = min(n, pref)
    while n % t:
        t //= 2
    return t


def _dot(a, b, dims=None):
    if dims is None:
        return jnp.dot(a, b, preferred_element_type=F32)
    return lax.dot_general(a, b, dims, preferred_element_type=F32)


def _dot_hi(a, b):
    return jnp.dot(a, b, precision=lax.Precision.HIGHEST, preferred_element_type=F32)


def _sigmoid(x):
    return jax.nn.sigmoid(x)


def _mm(a, b, *, name, outs, nt=False, ta=False, extras=(), epi=None, tm=512, tn=512, n=None, b_outer=False,
        vmem=VMEM_MID):
    assert not (nt and ta)
    k, m = a.shape if ta else a.shape[::-1]
    if n is None:
        n = b.shape[0] if nt else b.shape[1]
    tm, tn = _tile(m, tm), _tile(n, tn)
    gi, gj = m // tm, n // tn
    if b_outer:
        grid = (gj, gi)
        ij = lambda p, q: (q, p)
    else:
        grid = (gi, gj)
        ij = lambda p, q: (p, q)
    if ta:
        a_spec = pl.BlockSpec((k, tm), lambda p, q: (0, ij(p, q)[0]))
    else:
        a_spec = pl.BlockSpec((tm, k), lambda p, q: (ij(p, q)[0], 0))
    if nt:
        b_spec = pl.BlockSpec((tn, k), lambda p, q: (ij(p, q)[1], 0))
    else:
        b_spec = pl.BlockSpec((k, tn), lambda p, q: (0, ij(p, q)[1]))
    e_specs = []
    for arr, kind, off in extras:
        ob = off // tn
        assert off % tn == 0
        if kind == "tile":
            e_specs.append(pl.BlockSpec((tm, tn), lambda p, q, ob=ob: (ij(p, q)[0], ob + ij(p, q)[1])))
        else:
            e_specs.append(pl.BlockSpec((1, tn), lambda p, q, ob=ob: (0, ob + ij(p, q)[1])))
    ne = len(extras)

    def body(a_ref, b_ref, *rest):
        acc = _dot(a_ref[...], b_ref[...], NT if nt else (TN if ta else None))
        res = epi(acc, *[e[...] for e in rest[:ne]]) if epi is not None else (acc,)
        for o_ref, r in zip(rest[ne:], res):
            o_ref[...] = r.astype(o_ref.dtype)

    out = pl.pallas_call(
        body, name=name, grid=grid,
        in_specs=[a_spec, b_spec] + e_specs,
        out_specs=[pl.BlockSpec((tm, tn), lambda p, q: ij(p, q)) for _ in outs],
        out_shape=[jax.ShapeDtypeStruct((m, n), dt) for dt in outs],
        compiler_params=_cp(("arbitrary", "arbitrary"), vmem),
    )(a, b, *[e[0] for e in extras])
    return out if len(outs) > 1 else out[0]


def _mm_tn(a, g, *, name, tk=512, tn=1024, tmm=4096, vmem=VMEM_MID):
    m, k = a.shape
    n = g.shape[1]
    tk, tn, tmm = _tile(k, tk), _tile(n, tn), _tile(m, tmm)

    def body(a_ref, g_ref, o_ref):
        p = _dot(a_ref[...], g_ref[...], TN)

        @pl.when(pl.program_id(2) == 0)
        def _():
            o_ref[...] = p

        @pl.when(pl.program_id(2) > 0)
        def _():
            o_ref[...] += p

    return pl.pallas_call(
        body, name=name, grid=(k // tk, n // tn, m // tmm),
        in_specs=[pl.BlockSpec((tmm, tk), lambda i, j, r: (r, i)), pl.BlockSpec((tmm, tn), lambda i, j, r: (r, j))],
        out_specs=pl.BlockSpec((tk, tn), lambda i, j, r: (i, j)),
        out_shape=jax.ShapeDtypeStruct((k, n), F32),
        compiler_params=_cp(("arbitrary", "arbitrary", "arbitrary"), vmem),
    )(a, g)


def _adamw(w, g, m, v, *, name):
    r, c = w.shape
    tr = _tile(r, 256) if r % 8 == 0 else r

    def body(w_ref, g_ref, m_ref, v_ref, d_ref, nm_ref, nv_ref):
        gg = g_ref[...]
        nm = ADAM_B1 * m_ref[...] + (1.0 - ADAM_B1) * gg
        nv = ADAM_B2 * v_ref[...] + (1.0 - ADAM_B2) * jnp.square(gg)
        m_hat = nm / (1.0 - ADAM_B1 ** ADAM_STEP)
        v_hat = nv / (1.0 - ADAM_B2 ** ADAM_STEP)
        d_ref[...] = -ADAM_LR * (m_hat / (jnp.sqrt(v_hat) + ADAM_EPS) + ADAM_WD * w_ref[...])
        nm_ref[...] = nm
        nv_ref[...] = nv

    spec = pl.BlockSpec((tr, c), lambda i: (i, 0))
    return pl.pallas_call(
        body, name=name, grid=(r // tr,), in_specs=[spec] * 4, out_specs=[spec] * 3,
        out_shape=[jax.ShapeDtypeStruct((r, c), F32)] * 3, compiler_params=_cp(("arbitrary",)),
    )(w, g, m, v)


def _rows_sum(a, groups, *, name):
    r = a.shape[0] // groups

    def body(a_ref, o_ref):
        acc = a_ref[0:r, :]
        for d in range(1, groups):
            acc = acc + a_ref[d * r:(d + 1) * r, :]
        o_ref[...] = acc

    return pl.pallas_call(body, name=name, out_shape=jax.ShapeDtypeStruct((r, a.shape[1]), F32))(a)


def _silu_cast(a, *, name):
    def body(a_ref, o_ref):
        x = a_ref[...]
        o_ref[...] = (x * _sigmoid(x)).astype(o_ref.dtype)

    return pl.pallas_call(body, name=name, out_shape=jax.ShapeDtypeStruct(a.shape, MMD))(a)


def _sumsq(a, *, name):
    m, n = a.shape
    tm = _tile(m, 512)

    def body(a_ref, o_ref):
        x = a_ref[...]
        p = jnp.sum(jnp.sum(x * x, axis=1, keepdims=True), axis=0, keepdims=True)

        @pl.when(pl.program_id(0) == 0)
        def _():
            o_ref[...] = p

        @pl.when(pl.program_id(0) > 0)
        def _():
            o_ref[...] += p

    return pl.pallas_call(
        body, name=name, grid=(m // tm,), in_specs=[pl.BlockSpec((tm, n), lambda i: (i, 0))],
        out_specs=pl.BlockSpec((1, 1), lambda i: (0, 0)), out_shape=jax.ShapeDtypeStruct((1, 1), F32),
        compiler_params=_cp(("arbitrary",)),
    )(a)


def _acc_rows(o_ref, p, first):
    @pl.when(first)
    def _():
        o_ref[...] = p

    @pl.when(jnp.logical_not(first))
    def _():
        o_ref[...] += p


def _ln_mod(x, w, scale, shift, *, name):
    s, d = x.shape
    tm = _tile(s, 512)

    def body(x_ref, w_ref, sc_ref, sh_ref, o_ref):
        xv = x_ref[...]
        r = lax.rsqrt(jnp.mean(xv * xv, axis=-1, keepdims=True) + EPS)
        o_ref[...] = ((xv * r) * w_ref[...] * (1.0 + sc_ref[...]) + sh_ref[...]).astype(o_ref.dtype)

    row = pl.BlockSpec((1, d), lambda i: (0, 0))
    big = pl.BlockSpec((tm, d), lambda i: (i, 0))
    return pl.pallas_call(
        body, name=name, grid=(s // tm,), in_specs=[big, row, row, row], out_specs=big,
        out_shape=jax.ShapeDtypeStruct((s, d), MMD), compiler_params=_cp(("arbitrary",)),
    )(x, w, scale, shift)


def _ln_mod_bwd(dh, x, w, scale, dres, *, name):
    s, d = x.shape
    tm = _tile(s, 512)

    def body(dh_ref, x_ref, w_ref, sc_ref, dres_ref, dx_ref, dsh_ref, dsc_ref, dw_ref):
        xv = x_ref[...]
        dhv = dh_ref[...].astype(F32)
        r = lax.rsqrt(jnp.mean(xv * xv, axis=-1, keepdims=True) + EPS)
        nv = xv * r
        wv = w_ref[...]
        g1 = 1.0 + sc_ref[...]
        dn = dhv * (wv * g1)
        dx_ref[...] = dres_ref[...] + r * (dn - nv * jnp.mean(dn * nv, axis=-1, keepdims=True))
        first = pl.program_id(0) == 0
        _acc_rows(dsh_ref, jnp.sum(dhv, axis=0, keepdims=True), first)
        _acc_rows(dsc_ref, jnp.sum(dhv * nv * wv, axis=0, keepdims=True), first)
        _acc_rows(dw_ref, jnp.sum(dhv * nv * g1, axis=0, keepdims=True), first)

    row = pl.BlockSpec((1, d), lambda i: (0, 0))
    big = pl.BlockSpec((tm, d), lambda i: (i, 0))
    return pl.pallas_call(
        body, name=name, grid=(s // tm,), in_specs=[big, big, row, row, big], out_specs=[big, row, row, row],
        out_shape=[jax.ShapeDtypeStruct((s, d), F32)] + [jax.ShapeDtypeStruct((1, d), F32)] * 3,
        compiler_params=_cp(("arbitrary",)),
    )(dh, x, w, scale, dres)


def _gate_bwd(dy, u, gate, *, name):
    s, d = dy.shape
    tm = _tile(s, 512)

    def body(dy_ref, u_ref, g_ref, du_ref, dg_ref):
        dyv = dy_ref[...]
        du_ref[...] = (dyv * g_ref[...]).astype(du_ref.dtype)
        _acc_rows(dg_ref, jnp.sum(dyv * u_ref[...].astype(F32), axis=0, keepdims=True), pl.program_id(0) == 0)

    row = pl.BlockSpec((1, d), lambda i: (0, 0))
    big = pl.BlockSpec((tm, d), lambda i: (i, 0))
    return pl.pallas_call(
        body, name=name, grid=(s // tm,), in_specs=[big, big, row], out_specs=[big, row],
        out_shape=[jax.ShapeDtypeStruct((s, d), MMD), jax.ShapeDtypeStruct((1, d), F32)],
        compiler_params=_cp(("arbitrary",)),
    )(dy, u, gate)


def _seg64(v, e):
    hi = v.astype(jnp.bfloat16)
    lo = (v - hi.astype(F32)).astype(jnp.bfloat16)
    return _dot(hi, e) + _dot(lo, e)


def _rope_tables(s):
    rows = s // GRID_W
    pos_row = jnp.repeat(jnp.arange(rows, dtype=jnp.int32), GRID_W).astype(F32)
    pos_col = jnp.tile(jnp.arange(GRID_W, dtype=jnp.int32), rows).astype(F32)
    axis_dim = HEAD_DIM // 2
    inv_freq = ROPE_THETA ** (-jnp.arange(0, axis_dim, 2, dtype=F32) / axis_dim)
    ang_r = pos_row[:, None] * inv_freq[None, :]
    ang_c = pos_col[:, None] * inv_freq[None, :]
    zero = jnp.zeros_like(ang_r)
    cos = jnp.concatenate([jnp.cos(ang_r), jnp.cos(ang_r), jnp.cos(ang_c), jnp.cos(ang_c)], axis=1)
    s_a = jnp.concatenate([-jnp.sin(ang_r), zero, -jnp.sin(ang_c), zero], axis=1)
    s_b = jnp.concatenate([zero, jnp.sin(ang_r), zero, jnp.sin(ang_c)], axis=1)
    return [jnp.tile(t, (1, 2)) for t in (cos, s_a, s_b)]


def _e128():
    i = jnp.arange(128)
    return (i[:, None] // 64 == i[None, :] // 64).astype(jnp.bfloat16)


QKW = N_Q_HEADS * HEAD_DIM + N_KV_HEADS * HEAD_DIM


def _qk_fwd(proj, wrow, scrow, tabs, *, name):
    s = proj.shape[0]
    tm = _tile(s, 1024)

    def body(x_ref, w_ref, sc_ref, cos_ref, sa_ref, sb_ref, e_ref, o_ref, ot_ref):
        u = x_ref[...].astype(F32)
        r = lax.rsqrt(_seg64(u * u, e_ref[...]) * (1.0 / HEAD_DIM) + EPS)
        nv = (u * r) * w_ref[...]
        ro = nv * cos_ref[...] + pltpu.roll(nv, 112, 1) * sa_ref[...] + pltpu.roll(nv, 16, 1) * sb_ref[...]
        out = ro * sc_ref[...]
        o_ref[...] = out.astype(o_ref.dtype)
        ot_ref[...] = out.T.astype(ot_ref.dtype)

    tab = pl.BlockSpec((tm, 128), lambda i, j: (i, 0))
    row = pl.BlockSpec((1, 128), lambda i, j: (0, j))
    return pl.pallas_call(
        body, name=name, grid=(s // tm, QKW // 128),
        in_specs=[pl.BlockSpec((tm, 128), lambda i, j: (i, Q0 // 128 + j)), row, row, tab, tab, tab,
                  pl.BlockSpec((128, 128), lambda i, j: (0, 0))],
        out_specs=[pl.BlockSpec((tm, 128), lambda i, j: (i, j)), pl.BlockSpec((128, tm), lambda i, j: (j, i))],
        out_shape=[jax.ShapeDtypeStruct((s, QKW), MMD), jax.ShapeDtypeStruct((QKW, s), MMD)],
        compiler_params=_cp(("arbitrary", "arbitrary")),
    )(proj, wrow, scrow, *tabs, _e128())


def _qk_bwd(dqt, dkt, proj, wrow, scrow, tabs, *, name):
    s = proj.shape[0]
    tm = _tile(s, 1024)
    nq = dqt.shape[0] // 128

    def body(dq_ref, dk_ref, x_ref, w_ref, sc_ref, cos_ref, sa_ref, sb_ref, e_ref, du_ref, dw_ref):
        e = e_ref[...]
        d = jnp.where(pl.program_id(0) < nq, dq_ref[...], dk_ref[...]).T * sc_ref[...]
        dn = d * cos_ref[...] + pltpu.roll(d * sa_ref[...], 16, 1) + pltpu.roll(d * sb_ref[...], 112, 1)
        u = x_ref[...].astype(F32)
        r = lax.rsqrt(_seg64(u * u, e) * (1.0 / HEAD_DIM) + EPS)
        uh = u * r
        _acc_rows(dw_ref, jnp.sum(dn * uh, axis=0, keepdims=True), pl.program_id(1) == 0)
        dnw = dn * w_ref[...]
        du_ref[...] = (r * (dnw - uh * (_seg64(dnw * uh, e) * (1.0 / HEAD_DIM)))).astype(du_ref.dtype)

    tab = pl.BlockSpec((tm, 128), lambda j, i: (i, 0))
    row = pl.BlockSpec((1, 128), lambda j, i: (0, j))
    return pl.pallas_call(
        body, name=name, grid=(QKW // 128, s // tm),
        in_specs=[pl.BlockSpec((128, tm), lambda j, i: (jnp.minimum(j, nq - 1), i)),
                  pl.BlockSpec((128, tm), lambda j, i: (jnp.maximum(j - nq, 0), i)),
                  pl.BlockSpec((tm, 128), lambda j, i: (i, Q0 // 128 + j)),
                  row, row, tab, tab, tab, pl.BlockSpec((128, 128), lambda j, i: (0, 0))],
        out_specs=[pl.BlockSpec((tm, 128), lambda j, i: (i, j)), row],
        out_shape=[jax.ShapeDtypeStruct((s, QKW), MMD), jax.ShapeDtypeStruct((1, QKW), F32)],
        compiler_params=_cp(("arbitrary", "arbitrary")),
    )(dqt, dkt, proj, wrow, scrow, *tabs, _e128())


REP = N_Q_HEADS // N_KV_HEADS


def _lanes(ref):
    return jnp.concatenate([ref[r] for r in range(REP)], axis=1)


V_AUG = HEAD_DIM + 8
LOG2E = math.log2(math.e)


def _flash_fwd(qkt, vta, *, name):
    s = qkt.shape[2]
    tq, tk = _tile(s, 1024), _tile(s, 512)
    nk = s // tk
    lanes = REP * tq

    def body(q_ref, k_ref, v_ref, o_ref, lse_ref, m_ref, acc_ref):
        j = pl.program_id(2)

        @pl.when(j == 0)
        def _():
            m_ref[...] = jnp.full_like(m_ref, NEG)
            acc_ref[...] = jnp.zeros_like(acc_ref)

        st = _dot(k_ref[0], _lanes(q_ref), TN)
        m_prev = m_ref[...]
        m_new = jnp.maximum(m_prev, jnp.max(st, axis=0, keepdims=True))
        p = jnp.exp2(st - m_new).astype(MMD)
        acc_ref[...] = jnp.exp2(m_prev - m_new) * acc_ref[...] + _dot(v_ref[0], p)
        m_ref[...] = m_new

        @pl.when(j == nk - 1)
        def _():
            acc = acc_ref[...]
            l = acc[HEAD_DIM:HEAD_DIM + 1]
            o = acc[0:HEAD_DIM] / l
            ls = m_ref[...] + jnp.log(l) * LOG2E
            for r in range(REP):
                o_ref[r] = o[:, r * tq:(r + 1) * tq].astype(o_ref.dtype)
                lse_ref[r] = ls[:, r * tq:(r + 1) * tq]

    qspec = pl.BlockSpec((REP, HEAD_DIM, tq), lambda g, i, j: (g, 0, i))
    return pl.pallas_call(
        body, name=name, grid=(N_KV_HEADS, s // tq, nk),
        in_specs=[qspec, pl.BlockSpec((1, HEAD_DIM, tk), lambda g, i, j: (N_Q_HEADS + g, 0, j)),
                  pl.BlockSpec((1, V_AUG, tk), lambda g, i, j: (g, 0, j))],
        out_specs=[qspec, pl.BlockSpec((REP, 1, tq), lambda g, i, j: (g, 0, i))],
        out_shape=[jax.ShapeDtypeStruct((N_Q_HEADS, HEAD_DIM, s), MMD), jax.ShapeDtypeStruct((N_Q_HEADS, 1, s), F32)],
        scratch_shapes=[pltpu.VMEM((1, lanes), F32), pltpu.VMEM((V_AUG, lanes), F32)],
        compiler_params=_cp(("arbitrary", "arbitrary", "arbitrary"), VMEM_BIG),
    )(qkt, qkt, vta)


def _flash_bwd(qkt, k_h, v_h, dot, ot, lse, *, name):
    s = qkt.shape[2]
    tq, tk = _tile(s, 512), _tile(s, 1024)
    nk = s // tk

    def body(q_ref, kt_ref, k_ref, v_ref, do_ref, o_ref, lse_ref, dq_ref, dk_ref, dv_ref, dq_acc):
        i, j = pl.program_id(1), pl.program_id(2)
        q, do = _lanes(q_ref), _lanes(do_ref)
        delta = jnp.sum(do.astype(F32) * _lanes(o_ref).astype(F32), axis=0, keepdims=True)
        k, v = k_ref[0], v_ref[0]
        p = jnp.exp2(_dot(k, q) - _lanes(lse_ref))
        dvc = _dot(p.astype(MMD), do, NT)
        ds = (p * (_dot(v, do) - delta)).astype(MMD)
        dkc = _dot(ds, q, NT) * (1.0 / LOG2E)
        dqc = _dot(kt_ref[0], ds)
        rows = pl.ds(pl.multiple_of(j * tk, tk), tk)

        @pl.when(i == 0)
        def _():
            dk_ref[0, rows, :] = dkc
            dv_ref[0, rows, :] = dvc

        @pl.when(i > 0)
        def _():
            dk_ref[0, rows, :] += dkc
            dv_ref[0, rows, :] += dvc

        @pl.when(j == 0)
        def _():
            dq_acc[...] = dqc

        @pl.when(j > 0)
        def _():
            dq_acc[...] += dqc

        @pl.when(j == nk - 1)
        def _():
            acc = dq_acc[...]
            for r in range(REP):
                dq_ref[r] = acc[:, r * tq:(r + 1) * tq]

    qspec = pl.BlockSpec((REP, HEAD_DIM, tq), lambda g, i, j: (g, 0, i))
    kvin = pl.BlockSpec((1, tk, HEAD_DIM), lambda g, i, j: (g, j, 0))
    kvres = pl.BlockSpec((1, s, HEAD_DIM), lambda g, i, j: (g, 0, 0))
    return pl.pallas_call(
        body, name=name, grid=(N_KV_HEADS, s // tq, nk),
        in_specs=[qspec, pl.BlockSpec((1, HEAD_DIM, tk), lambda g, i, j: (N_Q_HEADS + g, 0, j)), kvin, kvin,
                  qspec, qspec, pl.BlockSpec((REP, 1, tq), lambda g, i, j: (g, 0, i))],
        out_specs=[qspec, kvres, kvres],
        out_shape=[jax.ShapeDtypeStruct((N_Q_HEADS, HEAD_DIM, s), F32), jax.ShapeDtypeStruct((N_KV_HEADS, s, HEAD_DIM), F32),
                   jax.ShapeDtypeStruct((N_KV_HEADS, s, HEAD_DIM), F32)],
        scratch_shapes=[pltpu.VMEM((HEAD_DIM, REP * tq), F32)],
        compiler_params=_cp(("arbitrary", "arbitrary", "arbitrary"), VMEM_BIG),
    )(qkt, qkt, k_h, v_h, dot, ot, lse)


HALO = 8
CONV_W = 2048 + 2 * SSD_GROUPS * SSD_N


def _shifted(win, off, r):
    return pltpu.roll(win, (r + 2 * HALO - off) % (r + 2 * HALO), 0)[0:r]


def _conv_fwd(proj, w8, brow, *, name):
    s = proj.shape[0]
    cb = 256
    r = _tile(s, 512)

    def body(x_ref, w_ref, b_ref, o_ref, pad_ref):
        zeros = jnp.zeros((HALO, cb), F32)
        pad_ref[0:HALO, :] = zeros
        pad_ref[s + HALO:s + 2 * HALO, :] = zeros

        def fill(i, carry):
            st = pl.multiple_of(i * r, r)
            pad_ref[pl.ds(st + HALO, r), :] = x_ref[pl.ds(st, r), :].astype(F32)
            return carry

        lax.fori_loop(0, s // r, fill, 0)
        wv = w_ref[...]
        bv = b_ref[...]

        def step(i, carry):
            st = pl.multiple_of(i * r, r)
            win = pad_ref[pl.ds(st, r + 2 * HALO), :]
            acc = bv + wv[0:1, :] * _shifted(win, HALO - 2, r)
            for t in range(1, D_CONV):
                acc = acc + wv[t:t + 1, :] * _shifted(win, HALO - 2 + t, r)
            o_ref[pl.ds(st, r), :] = (acc * _sigmoid(acc)).astype(o_ref.dtype)
            return carry

        lax.fori_loop(0, s // r, step, 0)

    return pl.pallas_call(
        body, name=name, grid=(CONV_W // cb,),
        in_specs=[pl.BlockSpec((s, cb), lambda j: (0, XS0 // cb + j)), pl.BlockSpec((8, cb), lambda j: (0, j)),
                  pl.BlockSpec((1, cb), lambda j: (0, j))],
        out_specs=pl.BlockSpec((s, cb), lambda j: (0, j)),
        out_shape=jax.ShapeDtypeStruct((s, CONV_W), MMD),
        scratch_shapes=[pltpu.VMEM((s + 2 * HALO, cb), F32)],
        compiler_params=_cp(("arbitrary",), VMEM_MID),
    )(proj, w8, brow)


def _conv_bwd(proj, col0, ga, gb, w8, brow, *, name):
    s = proj.shape[0]
    width = ga.shape[1]
    cb = 128
    c0 = col0 // cb
    r = _tile(s, 512)

    def body(x_ref, ga_ref, gb_ref, w_ref, b_ref, dx_ref, dw_ref, db_ref, xpad, dpad):
        zeros = jnp.zeros((HALO, cb), F32)
        for ref in (xpad, dpad):
            ref[0:HALO, :] = zeros
            ref[s + HALO:s + 2 * HALO, :] = zeros

        def fill(i, carry):
            st = pl.multiple_of(i * r, r)
            xpad[pl.ds(st + HALO, r), :] = x_ref[pl.ds(st, r), :].astype(F32)
            return carry

        lax.fori_loop(0, s // r, fill, 0)
        wv = w_ref[...]
        bv = b_ref[...]

        def first(i, carry):
            st = pl.multiple_of(i * r, r)
            win = xpad[pl.ds(st, r + 2 * HALO), :]
            taps = [_shifted(win, HALO - 2 + t, r) for t in range(D_CONV)]
            u = bv
            for t in range(D_CONV):
                u = u + wv[t:t + 1, :] * taps[t]
            sg = _sigmoid(u)
            du = ((ga_ref[pl.ds(st, r), :].astype(F32) + gb_ref[pl.ds(st, r), :].astype(F32))
                  * (sg * (1.0 + u * (1.0 - sg))))
            dpad[pl.ds(st + HALO, r), :] = du
            out = [carry[0] + jnp.sum(du, axis=0, keepdims=True)]
            for t in range(D_CONV):
                out.append(carry[1 + t] + jnp.sum(du * taps[t], axis=0, keepdims=True))
            return tuple(out)

        sums = lax.fori_loop(0, s // r, first, tuple(jnp.zeros((1, cb), F32) for _ in range(1 + D_CONV)))
        db_ref[...] = sums[0]
        for t in range(D_CONV):
            dw_ref[t:t + 1, :] = sums[1 + t]
        dw_ref[D_CONV:8, :] = jnp.zeros((8 - D_CONV, cb), F32)

        def second(i, carry):
            st = pl.multiple_of(i * r, r)
            win = dpad[pl.ds(st, r + 2 * HALO), :]
            acc = wv[0:1, :] * _shifted(win, HALO + 2, r)
            for t in range(1, D_CONV):
                acc = acc + wv[t:t + 1, :] * _shifted(win, HALO + 2 - t, r)
            dx_ref[pl.ds(st, r), :] = acc.astype(dx_ref.dtype)
            return carry

        lax.fori_loop(0, s // r, second, 0)

    col = pl.BlockSpec((s, cb), lambda j: (0, j))
    return pl.pallas_call(
        body, name=name, grid=(width // cb,),
        in_specs=[pl.BlockSpec((s, cb), lambda j: (0, XS0 // cb + c0 + j)), col, col,
                  pl.BlockSpec((8, cb), lambda j: (0, c0 + j)), pl.BlockSpec((1, cb), lambda j: (0, c0 + j))],
        out_specs=[col, pl.BlockSpec((8, cb), lambda j: (0, j)), pl.BlockSpec((1, cb), lambda j: (0, j))],
        out_shape=[jax.ShapeDtypeStruct((s, width), MMD), jax.ShapeDtypeStruct((8, width), F32),
                   jax.ShapeDtypeStruct((1, width), F32)],
        scratch_shapes=[pltpu.VMEM((s + 2 * HALO, cb), F32), pltpu.VMEM((s + 2 * HALO, cb), F32)],
        compiler_params=_cp(("arbitrary",), VMEM_BIG),
    )(proj, ga, gb, w8, brow)


def _tri(lower):
    i = jnp.arange(CHUNK)
    return ((i[:, None] >= i[None, :]) if lower else (i[:, None] <= i[None, :])).astype(F32)


def _dt_fwd(raw, bias, arow, *, name):
    s = raw.shape[0]

    def body(r_ref, b_ref, a_ref, lo_ref, up_ref, dt_ref, cs_ref):
        u = r_ref[...] + b_ref[...]
        dt = jnp.maximum(u, 0.0) + jnp.log1p(jnp.exp(-jnp.abs(u)))
        dt_ref[...] = dt
        a = dt * a_ref[...]
        lane = lax.broadcasted_iota(jnp.int32, (CHUNK, 128), 1)
        cs_ref[...] = jnp.where(lane < SSD_HEADS, _dot_hi(lo_ref[...], a), _dot_hi(up_ref[...], a))

    blk = pl.BlockSpec((CHUNK, 128), lambda i: (i, 0))
    row = pl.BlockSpec((1, 128), lambda i: (0, 0))
    tri = pl.BlockSpec((CHUNK, CHUNK), lambda i: (0, 0))
    return pl.pallas_call(
        body, name=name, grid=(s // CHUNK,), in_specs=[blk, row, row, tri, tri], out_specs=[blk, blk],
        out_shape=[jax.ShapeDtypeStruct((s, 128), F32)] * 2, compiler_params=_cp(("arbitrary",)),
    )(raw, bias, arow, _tri(True), _tri(False))


def _dt_bwd(ddt0, ddt1, raw, bias, *, name):
    s = raw.shape[0]
    tm = _tile(s, 1024)

    def body(d0_ref, d1_ref, r_ref, b_ref, o_ref, db_ref):
        g = (d0_ref[...] + d1_ref[...]) * _sigmoid(r_ref[...] + b_ref[...])
        o_ref[...] = g.astype(o_ref.dtype)
        _acc_rows(db_ref, jnp.sum(g, axis=0, keepdims=True), pl.program_id(0) == 0)

    blk = pl.BlockSpec((tm, 128), lambda i: (i, 0))
    row = pl.BlockSpec((1, 128), lambda i: (0, 0))
    return pl.pallas_call(
        body, name=name, grid=(s // tm,), in_specs=[blk, blk, blk, row], out_specs=[blk, row],
        out_shape=[jax.ShapeDtypeStruct((s, 128), MMD), jax.ShapeDtypeStruct((1, 128), F32)],
        compiler_params=_cp(("arbitrary",)),
    )(ddt0, ddt1, raw, bias)


GW = HPG * SSD_P


GPS = SSD_GROUPS


def _ssd_specs(nc, rev):
    cc = (lambda c: nc - 1 - c) if rev else (lambda c: c)
    return dict(
        x=pl.BlockSpec((CHUNK, GPS * GW), lambda g, c: (cc(c), g)),
        b=pl.BlockSpec((CHUNK, GPS * SSD_N), lambda g, c: (cc(c), 2048 // (GPS * SSD_N) + g)),
        c=pl.BlockSpec((CHUNK, GPS * SSD_N), lambda g, c: (cc(c), 2048 // (GPS * SSD_N) + 1 + g)),
        lanes=pl.BlockSpec((CHUNK, 128), lambda g, c: (cc(c), 0)),
        drow=pl.BlockSpec((1, GPS * GW), lambda g, c: (0, g)),
        y=pl.BlockSpec((CHUNK, GPS * GW), lambda g, c: (cc(c), g)),
        h=pl.BlockSpec((GPS, 1, SSD_N, GW), lambda g, c: (g, cc(c), 0, 0)),
        n=pl.BlockSpec((CHUNK, GPS * SSD_N), lambda g, c: (cc(c), g)),
    )


def _ssd_mask(anti):
    ii = lax.broadcasted_iota(jnp.int32, (CHUNK, CHUNK), 0)
    jj = lax.broadcasted_iota(jnp.int32, (CHUNK, CHUNK), 1)
    return ii, jj, (ii <= jj) if anti else (ii >= jj)


def _expand(x, ex, terms=3):
    h1 = x.astype(jnp.bfloat16)
    r1 = x - h1.astype(F32)
    h2 = r1.astype(jnp.bfloat16)
    out = _dot(h1, ex) + _dot(h2, ex)
    if terms == 3:
        out = out + _dot((r1 - h2.astype(F32)).astype(jnp.bfloat16), ex)
    return out


def _headsum(a, e):
    hi = a.astype(jnp.bfloat16)
    return _dot(hi, e) + _dot((a - hi.astype(F32)).astype(jnp.bfloat16), e)


def _expand_mats():
    lane = jnp.arange(128)[None, :, None]
    col = jnp.arange(GW)[None, None, :]
    base = (jnp.arange(2)[:, None] * SSD_HEADS + jnp.arange(SSD_GROUPS)[None, :] * HPG).reshape(2 * SSD_GROUPS, 1, 1)
    return (lane == base + col // SSD_P).astype(jnp.bfloat16)


def _headsum_mats():
    e1 = (jnp.arange(GW)[:, None] // SSD_P == jnp.arange(128)[None, :]).astype(jnp.bfloat16)
    e2 = (jnp.arange(HPG * CHUNK)[:, None] // CHUNK == jnp.arange(128)[None, :]).astype(jnp.bfloat16)
    return e1, e2


def _ssd_fwd(xc, dt, cs, ex, drow, di, *, name):
    s = xc.shape[0]
    nc = s // CHUNK
    anti = di == 1
    sp = _ssd_specs(nc, anti)
    trow = 0 if anti else CHUNK - 1

    def body(x_ref, b_ref, c_ref, dt_ref, cs_ref, ex_ref, d_ref, y_ref, hp_ref, h_ref):
        @pl.when(pl.program_id(1) == 0)
        def _():
            h_ref[...] = jnp.zeros_like(h_ref)

        mask = _ssd_mask(anti)[2]
        dtv, csv = dt_ref[...], cs_ref[...]
        cst = csv.T
        for gi in range(GPS):
            cols = slice(gi * GW, (gi + 1) * GW)
            ncols = slice(gi * SSD_N, (gi + 1) * SSD_N)
            ex = ex_ref[gi]
            xb = x_ref[:, cols].astype(F32)
            bm, cm = b_ref[:, ncols], c_ref[:, ncols]
            csr = cst[SSD_HEADS * di + HPG * gi:SSD_HEADS * di + HPG * (gi + 1)]
            dtf = _expand(dtv, ex, 2)
            csf = _expand(csv, ex)
            tl = csf[trow:trow + 1, :]
            h = h_ref[gi]
            hp_ref[gi, 0] = h.astype(hp_ref.dtype)
            g = _dot(cm, bm, NT)
            xs = xb * dtf
            xsm = xs.astype(MMD)
            base = jnp.exp(csf) * _dot(cm, h.astype(MMD)) + d_ref[:, cols] * xb
            for r in range(HPG):
                sl = slice(r * SSD_P, (r + 1) * SSD_P)
                lm = jnp.exp(jnp.where(mask, csf[:, r * SSD_P:r * SSD_P + 1] - csr[r:r + 1, :], NEG))
                y_ref[:, gi * GW + r * SSD_P:gi * GW + (r + 1) * SSD_P] = (
                    _dot((g * lm).astype(MMD), xsm[:, sl]) + base[:, sl]).astype(y_ref.dtype)
            xd = (xs * jnp.exp(tl - csf)).astype(MMD)
            h_ref[gi] = h * jnp.exp(tl) + _dot(bm, xd, TN)

    return pl.pallas_call(
        body, name=name, grid=(1, nc),
        in_specs=[sp["x"], sp["b"], sp["c"], sp["lanes"], sp["lanes"],
                  pl.BlockSpec((GPS, 128, GW), lambda g, c: (di, 0, 0)), sp["drow"]],
        out_specs=[sp["y"], sp["h"]],
        out_shape=[jax.ShapeDtypeStruct((s, 2048), MMD), jax.ShapeDtypeStruct((SSD_GROUPS, nc, SSD_N, GW), MMD)],
        scratch_shapes=[pltpu.VMEM((GPS, SSD_N, GW), F32)],
        compiler_params=_cp(("arbitrary", "arbitrary")),
    )(xc, xc, xc, dt, cs, ex, drow)


def _ssd_bwd(xc, dt, cs, ex, drow, arow, dy, hprev, di, *, name):
    s = xc.shape[0]
    nc = s // CHUNK
    anti = di == 1
    sp = _ssd_specs(nc, not anti)
    trow = 0 if anti else CHUNK - 1
    e1, e2 = _headsum_mats()

    def body(x_ref, b_ref, c_ref, dt_ref, cs_ref, ex_ref, d_ref, a_ref, dy_ref, hp_ref, tri_ref,
             e1_ref, e2_ref, dx_ref, db_ref, dc_ref, ddt_ref, da_ref, dh_ref, w_ref, dxs_ref):
        @pl.when(pl.program_id(1) == 0)
        def _():
            dh_ref[...] = jnp.zeros_like(dh_ref)
            da_ref[...] = jnp.zeros_like(da_ref)

        e1v = e1_ref[...]
        ii, _, mask = _ssd_mask(anti)
        dtv, csv = dt_ref[...], cs_ref[...]
        cst = csv.T
        ddt_acc = jnp.zeros((CHUNK, 128), F32)
        da_acc = jnp.zeros((1, 128), F32)
        for gi in range(GPS):
            lane0 = SSD_HEADS * di + HPG * gi
            cols = slice(gi * GW, (gi + 1) * GW)
            ncols = slice(gi * SSD_N, (gi + 1) * SSD_N)
            ex = ex_ref[gi]
            xb = x_ref[:, cols].astype(F32)
            bm, cm = b_ref[:, ncols], c_ref[:, ncols]
            csr = cst[lane0:lane0 + HPG]
            dym = dy_ref[:, cols]
            dyb = dym.astype(F32)
            hpm = hp_ref[gi, 0]
            hp = hpm.astype(F32)
            dh = dh_ref[gi]
            dhm = dh.astype(MMD)
            dtf = _expand(dtv, ex, 2)
            csf = _expand(csv, ex)
            tl = csf[trow:trow + 1, :]
            e = jnp.exp(csf)
            dec = jnp.exp(tl - csf)
            et = jnp.exp(tl)
            xs = xb * dtf
            xsm = xs.astype(MMD)
            g = _dot(cm, bm, NT)
            z = _dot(cm, hpm)
            bdh = _dot(bm, dhm)
            dg = jnp.zeros((CHUNK, CHUNK), F32)
            wcols = jnp.zeros((CHUNK, CHUNK), F32)
            for r in range(HPG):
                sl = slice(r * SSD_P, (r + 1) * SSD_P)
                lm = jnp.exp(jnp.where(mask, csf[:, r * SSD_P:r * SSD_P + 1] - csr[r:r + 1, :], NEG))
                mm = g * lm
                dm = _dot(dym[:, sl], xsm[:, sl], NT)
                w = dm * mm
                w_ref[gi, :, r * CHUNK:(r + 1) * CHUNK] = w
                wcols = jnp.where(ii == r, jnp.sum(w, axis=0, keepdims=True), wcols)
                dg = dg + dm * lm
                dxs_ref[gi, :, sl] = _dot(mm.astype(MMD), dym[:, sl], TN)
            dxs = dxs_ref[gi] + dec * bdh
            dx_ref[:, cols] = (dxs * dtf + d_ref[:, cols] * dyb).astype(dx_ref.dtype)
            tb = xs * bdh * dec
            d_tot = jnp.sum(tb, axis=0, keepdims=True) + et * jnp.sum(dh * hp, axis=0, keepdims=True)
            d_tot = _headsum(jnp.broadcast_to(d_tot, (8, GW)), e1v)[0:1]
            dcs = (_headsum(dyb * (e * z) - tb, e1v) + _headsum(w_ref[gi], e2_ref[...]) - wcols.T
                   + jnp.where(ii == trow, d_tot, 0.0))
            da = pltpu.roll(_dot_hi(tri_ref[...], dcs), lane0, 1)
            ddt_acc = ddt_acc + da * a_ref[...] + pltpu.roll(_headsum(dxs * xb, e1v), lane0, 1)
            da_acc = da_acc + jnp.sum(da * dtv, axis=0, keepdims=True)
            dgm = dg.astype(MMD)
            dz = (e * dyb).astype(MMD)
            dc_ref[:, ncols] = (_dot(dgm, bm) + _dot(dz, hpm, NT)).astype(dc_ref.dtype)
            db_ref[:, ncols] = (_dot(dgm, cm, TN) + _dot((xs * dec).astype(MMD), dhm, NT)).astype(db_ref.dtype)
            dh_ref[gi] = dh * et + _dot(cm, dz, TN)
        ddt_ref[...] = ddt_acc
        da_ref[...] += da_acc

    const = lambda shape: pl.BlockSpec(shape, lambda g, c: (0,) * len(shape))
    return pl.pallas_call(
        body, name=name, grid=(1, nc),
        in_specs=[sp["x"], sp["b"], sp["c"], sp["lanes"], sp["lanes"],
                  pl.BlockSpec((GPS, 128, GW), lambda g, c: (di, 0, 0)), sp["drow"],
                  const((1, 128)), sp["y"], sp["h"],
                  const((CHUNK, CHUNK)), const((GW, 128)), const((HPG * CHUNK, 128))],
        out_specs=[sp["y"], sp["n"], sp["n"], sp["lanes"], const((1, 128))],
        out_shape=[jax.ShapeDtypeStruct((s, 2048), MMD), jax.ShapeDtypeStruct((s, SSD_GROUPS * SSD_N), MMD),
                   jax.ShapeDtypeStruct((s, SSD_GROUPS * SSD_N), MMD), jax.ShapeDtypeStruct((s, 128), F32),
                   jax.ShapeDtypeStruct((1, 128), F32)],
        scratch_shapes=[pltpu.VMEM((GPS, SSD_N, GW), F32), pltpu.VMEM((GPS, CHUNK, HPG * CHUNK), F32),
                        pltpu.VMEM((GPS, CHUNK, GW), F32)],
        compiler_params=_cp(("arbitrary", "arbitrary")),
    )(xc, xc, xc, dt, cs, ex, drow, arow, dy, hprev, _tri(anti), e1, e2)


def _gnorm_fwd(ya, yb, proj, w, *, name):
    s = ya.shape[0]
    tm = _tile(s, 256)

    def body(a_ref, b_ref, z_ref, w_ref, o_ref):
        zv = z_ref[...].astype(F32)
        t = (a_ref[...].astype(F32) + b_ref[...].astype(F32)) * (zv * _sigmoid(zv))
        r = lax.rsqrt(jnp.mean(t * t, axis=-1, keepdims=True) + EPS)
        o_ref[...] = ((t * r) * w_ref[...]).astype(o_ref.dtype)

    big = pl.BlockSpec((tm, 2048), lambda i: (i, 0))
    row = pl.BlockSpec((1, 2048), lambda i: (0, 0))
    return pl.pallas_call(
        body, name=name, grid=(s // tm,), in_specs=[big, big, big, row], out_specs=big,
        out_shape=jax.ShapeDtypeStruct((s, 2048), MMD), compiler_params=_cp(("arbitrary",)),
    )(ya, yb, proj, w)


def _gnorm_bwd(dout, ya, yb, proj, xc, w, *, name):
    s = ya.shape[0]
    tm = _tile(s, 256)

    def body(do_ref, a_ref, b_ref, z_ref, x_ref, w_ref, dy_ref, dz_ref, dw_ref, dd_ref):
        zv = z_ref[...].astype(F32)
        sg = _sigmoid(zv)
        sz = zv * sg
        y = a_ref[...].astype(F32) + b_ref[...].astype(F32)
        t = y * sz
        r = lax.rsqrt(jnp.mean(t * t, axis=-1, keepdims=True) + EPS)
        nv = t * r
        dov = do_ref[...].astype(F32)
        _acc_rows(dw_ref, jnp.sum(dov * nv, axis=0, keepdims=True), pl.program_id(0) == 0)
        dn = dov * w_ref[...]
        dt_ = r * (dn - nv * jnp.mean(dn * nv, axis=-1, keepdims=True))
        dy = dt_ * sz
        dy_ref[...] = dy.astype(dy_ref.dtype)
        dz_ref[...] = (dt_ * y * (sg * (1.0 + zv * (1.0 - sg)))).astype(dz_ref.dtype)
        _acc_rows(dd_ref, jnp.sum(dy * x_ref[...].astype(F32), axis=0, keepdims=True), pl.program_id(0) == 0)

    big = pl.BlockSpec((tm, 2048), lambda i: (i, 0))
    row = pl.BlockSpec((1, 2048), lambda i: (0, 0))
    return pl.pallas_call(
        body, name=name, grid=(s // tm,), in_specs=[big, big, big, big, big, row], out_specs=[big, big, row, row],
        out_shape=[jax.ShapeDtypeStruct((s, 2048), MMD), jax.ShapeDtypeStruct((s, 2048), MMD),
                   jax.ShapeDtypeStruct((1, 2048), F32), jax.ShapeDtypeStruct((1, 2048), F32)],
        compiler_params=_cp(("arbitrary",)),
    )(dout, ya, yb, proj, xc, w)


def _heads(a, n):
    return a.reshape(a.shape[0], n, HEAD_DIM).transpose(1, 0, 2)


def _unheads(a):
    return a.transpose(1, 0, 2).reshape(a.shape[1], a.shape[0] * HEAD_DIM)


def _local_step(x, target, mod, wts, small, late_weights=None, late_grads=None, in_grad=None):
    s, d = x.shape
    shift1, scale1, gate1, shift2, scale2, gate2 = [mod[i:i + 1] for i in range(6)]

    h1 = _ln_mod(x, small["norm1_w"], scale1, shift1, name="ln1")
    proj = _mm(h1, wts["w_in_p"], name="in_proj", outs=[MMD], tm=512, tn=2944, b_outer=True)
    dt_raw = _mm(h1, wts["w_dt"], name="dt_proj", outs=[F32], tm=512, tn=128)

    qk_w = jnp.concatenate([jnp.tile(small["q_norm_w"], (1, N_Q_HEADS)), jnp.tile(small["k_norm_w"], (1, N_KV_HEADS))], axis=1)
    qk_sc = jnp.concatenate([jnp.full((1, N_Q_HEADS * HEAD_DIM), HEAD_DIM ** -0.5, F32),
                             jnp.ones((1, N_KV_HEADS * HEAD_DIM), F32)], axis=1)
    qk_sc2 = jnp.concatenate([jnp.full((1, N_Q_HEADS * HEAD_DIM), HEAD_DIM ** -0.5 * LOG2E, F32),
                              jnp.ones((1, N_KV_HEADS * HEAD_DIM), F32)], axis=1)
    tabs = _rope_tables(s)
    qk, qkt = _qk_fwd(proj, qk_w, qk_sc2, tabs, name="qk_fwd")
    qkt = qkt.reshape(N_Q_HEADS + N_KV_HEADS, HEAD_DIM, s)
    k_h = _heads(qk[:, N_Q_HEADS * HEAD_DIM:], N_KV_HEADS)
    v_sd = proj[:, V0:V0 + N_KV_HEADS * HEAD_DIM]
    v_h = _heads(v_sd, N_KV_HEADS)
    vta = jnp.concatenate([v_sd.T.reshape(N_KV_HEADS, HEAD_DIM, s), jnp.ones((N_KV_HEADS, V_AUG - HEAD_DIM, s), MMD)], axis=1)
    ot, lse = _flash_fwd(qkt, vta, name="flash_fwd")
    ot2 = ot.reshape(N_Q_HEADS * HEAD_DIM, s)
    if late_weights is not None:
        wts = {**wts, **late_weights(ot)}

    w8 = jnp.pad(small["conv_w"], ((0, 8 - D_CONV), (0, 0)))
    xc = _conv_fwd(proj, w8, small["conv_b"], name="conv_fwd")
    a_neg = -jnp.exp(small["A_log"])
    arow = jnp.pad(a_neg.reshape(1, 2 * SSD_HEADS), ((0, 0), (0, 128 - 2 * SSD_HEADS)))
    bias_row = jnp.pad(small["dt_bias"].reshape(1, 2 * SSD_HEADS), ((0, 0), (0, 128 - 2 * SSD_HEADS)))
    dt, cs = _dt_fwd(dt_raw, bias_row, arow, name="dt_fwd")
    drow = jnp.repeat(small["ssd_D"], SSD_P, axis=1)
    dirs = [dict(drow=drow), dict(drow=jnp.zeros_like(drow))]
    ex = _expand_mats()
    ys = []
    for di, dd in enumerate(dirs):
        y, dd["hprev"] = _ssd_fwd(xc, dt, cs, ex, dd["drow"], di, name=f"ssd_fwd{di}")
        ys.append(y)
    ssdn = _gnorm_fwd(ys[0], ys[1], proj, small["ssd_norm_w"], name="gnorm_fwd")

    a_o = _mm(ot2, wts["w_attn_out"], name="attn_out", outs=[MMD], ta=True, tm=512, tn=1024)

    def merge_epi(acc, ao, ga, gs):
        return (_sigmoid(ga.astype(F32)) * ao.astype(F32) + _sigmoid(gs.astype(F32)) * acc, acc)

    merged, b_o = _mm(ssdn, wts["w_ssd_out"], name="ssd_out", outs=[MMD, MMD], tm=512, tn=1024,
                      extras=[(a_o, "tile", 0), (proj, "tile", GA0), (proj, "tile", GS0)], epi=merge_epi)

    def res_epi(acc, res, gate):
        return (res + gate * acc, acc)

    x1, mo = _mm(merged, wts["w_o"], name="w_o", outs=[F32, MMD], tm=512, tn=1024,
                 extras=[(x, "tile", 0), (gate1, "row", 0)], epi=res_epi)
    h2 = _ln_mod(x1, small["norm2_w"], scale2, shift2, name="ln2")

    def relu2_epi(acc):
        rl = jnp.maximum(acc, 0.0)
        return (rl * rl, rl)

    act, rl = _mm(h2, wts["w_mlp1"], name="mlp1", outs=[MMD, MMD], tm=1024, tn=1024, epi=relu2_epi, b_outer=True)

    def loss_epi(acc, res, gate, tgt):
        return ((res + gate * acc - tgt) * (1.0 / d), acc)

    dy, ffo = _mm(act, wts["w_mlp2"], name="mlp2", outs=[F32, MMD], tm=512, tn=1024, vmem=VMEM_BIG,
                  extras=[(x1, "tile", 0), (gate2, "row", 0), (target, "tile", 0)], epi=loss_epi)
    loss = _sumsq(dy, name="loss") * (0.5 * d)

    gw = {}
    gs_ = {}
    dffo, dgate2 = _gate_bwd(dy, ffo, gate2, name="gate2_bwd")
    dpre = _mm(dffo, wts["w_mlp2"], name="mlp2_dx", outs=[MMD], nt=True, tm=1024, tn=1024, b_outer=True,
               extras=[(rl, "tile", 0)], epi=lambda acc, r: (acc * (2.0 * r.astype(F32)),))
    gw["w_mlp2"] = _mm_tn(act, dffo, name="mlp2_dw")
    dh2 = _mm(dpre, wts["w_mlp1"], name="mlp1_dx", outs=[F32], nt=True, tm=1024, tn=1024, vmem=VMEM_BIG)
    gw["w_mlp1"] = _mm_tn(h2, dpre, name="mlp1_dw")
    dx1, dshift2, dscale2, gs_["norm2_w"] = _ln_mod_bwd(dh2, x1, small["norm2_w"], scale2, dy, name="ln2_bwd")
    dmo, dgate1 = _gate_bwd(dx1, mo, gate1, name="gate1_bwd")

    def merge_bwd_epi(acc, ao, bo, ga, gs):
        sa, ss = _sigmoid(ga.astype(F32)), _sigmoid(gs.astype(F32))
        return (acc * sa, acc * ss, acc * ao.astype(F32) * sa * (1.0 - sa), acc * bo.astype(F32) * ss * (1.0 - ss))

    da_o, db_o, dga, dgs = _mm(dmo, wts["w_o"], name="w_o_dx", outs=[MMD] * 4, nt=True, tm=512, tn=1024,
                               extras=[(a_o, "tile", 0), (b_o, "tile", 0), (proj, "tile", GA0), (proj, "tile", GS0)],
                               epi=merge_bwd_epi)
    gw["w_o"] = _mm_tn(merged, dmo, name="w_o_dw")
    dot = _mm(wts["w_attn_out"], da_o, name="attn_out_dx", outs=[MMD], nt=True, tm=1024, tn=1024)
    gw["w_attn_out"] = _mm(ot2, da_o, name="attn_out_dw", outs=[F32], tm=256, tn=512, vmem=VMEM_BIG)
    dssdn = _mm(db_o, wts["w_ssd_out"], name="ssd_out_dx", outs=[MMD], nt=True, tm=512, tn=2048)
    gw["w_ssd_out"] = _mm_tn(ssdn, db_o, name="ssd_out_dw")

    norm_w = small["ssd_norm_w"] if late_grads is None else small["ssd_norm_w"] + late_grads(gw)
    dyssd, dz, gs_["ssd_norm_w"], dd_row = _gnorm_bwd(dssdn, ys[0], ys[1], proj, xc, norm_w, name="gnorm_bwd")
    gs_["ssd_D"] = dd_row.reshape(SSD_HEADS, SSD_P).sum(axis=1).reshape(1, SSD_HEADS)
    dxc, ddts, das = [], [], []
    for di, dd in enumerate(dirs):
        dxs, dbm, dcm, ddt_d, da_d = _ssd_bwd(xc, dt, cs, ex, dd["drow"], arow, dyssd, dd["hprev"], di, name=f"ssd_bwd{di}")
        dxc.append((dxs, dbm, dcm))
        ddts.append(ddt_d)
        das.append(da_d)
    conv_parts, col0 = [], 0
    for part, (ga, gb) in enumerate(zip(*dxc)):
        conv_parts.append(_conv_bwd(proj, col0, ga, gb, w8, small["conv_b"], name=f"conv_bwd{part}"))
        col0 += ga.shape[1]
    dxbc, dw8, gs_["conv_b"] = [jnp.concatenate(t, axis=1) for t in zip(*conv_parts)]
    gs_["conv_w"] = dw8[0:D_CONV]
    gs_["A_log"] = (das[0] + das[1])[:, 0:2 * SSD_HEADS].reshape(2, SSD_HEADS) * a_neg
    ddt_raw, dbias = _dt_bwd(ddts[0], ddts[1], dt_raw, bias_row, name="dt_bwd")
    gs_["dt_bias"] = dbias[:, 0:2 * SSD_HEADS].reshape(2, SSD_HEADS)

    dqt, dk_h, dv_h = _flash_bwd(qkt, k_h, v_h, dot.reshape(N_Q_HEADS, HEAD_DIM, s), ot, lse, name="flash_bwd")
    dqk_u, dqk_w = _qk_bwd(dqt.reshape(N_Q_HEADS * HEAD_DIM, s), dk_h.transpose(0, 2, 1).reshape(N_KV_HEADS * HEAD_DIM, s),
                           proj, qk_w, qk_sc, tabs, name="qk_bwd")
    gs_["q_norm_w"] = dqk_w[:, 0:N_Q_HEADS * HEAD_DIM].reshape(N_Q_HEADS, HEAD_DIM).sum(axis=0, keepdims=True)
    gs_["k_norm_w"] = dqk_w[:, N_Q_HEADS * HEAD_DIM:].reshape(N_KV_HEADS, HEAD_DIM).sum(axis=0, keepdims=True)
    dv = _unheads(dv_h).astype(MMD)

    dproj = jnp.concatenate([dz, dga, dgs, dxbc, dqk_u, dv, ddt_raw], axis=1)
    gw["w_in_p"] = _mm_tn(h1, dproj, name="in_proj_dw", tk=512, tn=2944, tmm=2048, vmem=VMEM_BIG)
    zero_row = jnp.zeros((1, d), F32) if in_grad is None else jnp.zeros((1, d), F32) + in_grad(gw["w_in_p"])[0:1, 0:1]
    dh1 = _mm(dproj, wts["w_in_p"], name="in_proj_dx", outs=[F32], nt=True, tm=256, tn=1024, vmem=VMEM_BIG,
              extras=[(zero_row, "row", 0)], epi=lambda acc, r: (acc + r,))
    grad_x, dshift1, dscale1, gs_["norm1_w"] = _ln_mod_bwd(dh1, x, small["norm1_w"], scale1, dx1, name="ln1_bwd")
    dmod = jnp.concatenate([dshift1, dscale1, dgate1, dshift2, dscale2, dgate2], axis=0)
    return loss, grad_x, dmod, gw, gs_


N_DEV = 8
N_CHIP = 4
ANY = pl.BlockSpec(memory_space=pl.ANY)


def _place():
    return lax.axis_index("x"), lax.axis_index("y"), lax.axis_index("c")


def _allgather8(v, *, name):
    m_per, n = v.shape

    def body(x_ref, out_ref, send_sems, recv_sems, local_sem):
        x, y, c = _place()
        me, sibling = (x, y, c), (x, y, 1 - c)
        chips = [(1 - x, y), (x, 1 - y), (1 - x, 1 - y)]

        def rows(px, py, pc):
            return out_ref.at[pl.ds((4 * px + 2 * py + pc) * m_per, m_per), :]

        def copy(k, block, to, src=None):
            return pltpu.make_async_remote_copy(
                src_ref=rows(*block) if src is None else src, dst_ref=rows(*block),
                send_sem=send_sems.at[k], recv_sem=recv_sems.at[k], device_id=to, device_id_type=MESH)

        mine = pltpu.make_async_copy(x_ref, rows(*me), local_sem)
        mine.start()
        first = [copy(0, me, sibling, src=x_ref)]
        first += [copy(1 + j, me, (*chip, c), src=x_ref) for j, chip in enumerate(chips)]
        for cp in first:
            cp.start()
        passed = [copy(4 + j, (*chip, c), sibling) for j, chip in enumerate(chips)]
        for j, chip in enumerate(chips):
            copy(1 + j, (*chip, c), me).wait_recv()
            passed[j].start()
        copy(0, sibling, me).wait_recv()
        for j, chip in enumerate(chips):
            copy(4 + j, (*chip, 1 - c), me).wait_recv()
        for cp in first + passed:
            cp.wait_send()
        mine.wait()

    return pl.pallas_call(
        body, name=name, out_shape=jax.ShapeDtypeStruct((N_DEV * m_per, n), v.dtype),
        in_specs=[pl.BlockSpec(memory_space=pltpu.VMEM)], out_specs=pl.BlockSpec(memory_space=pltpu.VMEM),
        scratch_shapes=[pltpu.SemaphoreType.DMA((7,)), pltpu.SemaphoreType.DMA((7,)), pltpu.SemaphoreType.DMA],
    )(v)


HBM = pl.BlockSpec(memory_space=pltpu.HBM)
SEM = pl.BlockSpec(memory_space=pltpu.SEMAPHORE)


def _chips_copies(x_ref, land_ref, sems, scatter):
    x, y, c = _place()
    k = 2 * x + y
    chips = [(1 - x, y), (x, 1 - y), (1 - x, 1 - y)]
    ids = [2 * cx + cy for cx, cy in chips]

    def copy(j, slot):
        return pltpu.make_async_remote_copy(
            src_ref=x_ref.at[ids[j]] if scatter else x_ref, dst_ref=land_ref.at[slot], send_sem=sems[j],
            recv_sem=sems[3 + j], device_id=(*chips[j], c), device_id_type=MESH)

    return [copy(j, k) for j in range(3)], [copy(j, ids[j]) for j in range(3)]


def _chips_start(src, scatter, *, name):
    shape = src.shape if scatter else (N_CHIP,) + tuple(src.shape)

    def body(x_ref, land_ref, *rest):
        sems, token = rest[0:6], rest[8]
        for cp in _chips_copies(x_ref, land_ref, sems, scatter)[0]:
            cp.start()
        token[...] = jnp.zeros_like(token)

    out = pl.pallas_call(
        body, name=name,
        out_shape=(pltpu.SemaphoreType.DMA(()),) * 6 + (pltpu.HBM(src.shape, src.dtype), pltpu.HBM(shape, src.dtype),
                                                       jax.ShapeDtypeStruct((8, 128), F32)),
        in_specs=(HBM, HBM), out_specs=(SEM,) * 6 + (HBM, HBM, pl.BlockSpec(memory_space=pltpu.VMEM)),
        input_output_aliases={0: 6, 1: 7},
        compiler_params=pltpu.CompilerParams(has_side_effects=pltpu.SideEffectType.DATAFLOW_SIDE_EFFECTING),
    )(pltpu.with_memory_space_constraint(src, pltpu.HBM),
      pltpu.with_memory_space_constraint(lax.empty(shape, src.dtype), pltpu.HBM))
    return out[0:6], out[6], out[7], out[8]


def _chips_wait(sems, src, land, after, scatter, *, name):
    def body(x_ref, land_ref, *rest):
        sems_ = rest[0:6]
        for cp in _chips_copies(x_ref, land_ref, sems_, scatter)[1]:
            cp.wait_send()
            cp.wait_recv()

    return pl.pallas_call(
        body, name=name, out_shape=(pltpu.HBM(src.shape, src.dtype), pltpu.HBM(land.shape, land.dtype)),
        in_specs=(HBM, HBM) + (SEM,) * 6 + (ANY,), out_specs=(HBM, HBM), input_output_aliases={0: 0, 1: 1},
        compiler_params=pltpu.CompilerParams(has_side_effects=pltpu.SideEffectType.DATAFLOW_SIDE_EFFECTING),
    )(src, land, *sems, after)


def _row_tile(r, pref=512):
    return max(t for t in range(16, pref + 1, 16) if r % t == 0)


def _gather_weights(src, *, name):
    r = src.shape[0]
    hr = r // 2
    assert r == 2 * hr and hr % 16 == 0

    def body(x_ref, out_ref, send_sems, recv_sems):
        x, y, c = _place()
        k = 2 * x + y
        chips = [(1 - x, y), (x, 1 - y), (1 - x, 1 - y)]
        ids = [2 * cx + cy for cx, cy in chips]
        mine_rows = pl.ds(pl.multiple_of(c * hr, 16), hr)
        other_rows = pl.ds(pl.multiple_of((1 - c) * hr, 16), hr)

        def copy(sem, src_ref, slot, rows, to):
            return pltpu.make_async_remote_copy(
                src_ref=src_ref, dst_ref=out_ref.at[slot, rows], send_sem=send_sems.at[sem], recv_sem=recv_sems.at[sem],
                device_id=to, device_id_type=MESH)

        sends = [copy(j, x_ref.at[mine_rows], k, mine_rows, (cx, cy, c)) for j, (cx, cy) in enumerate(chips)]
        for cp in sends:
            cp.start()
        passed = [copy(3 + j, out_ref.at[ids[j], mine_rows], ids[j], mine_rows, (x, y, 1 - c)) for j in range(3)]
        for j, (cx, cy) in enumerate(chips):
            copy(j, x_ref.at[mine_rows], ids[j], mine_rows, (cx, cy, c)).wait_recv()
            passed[j].start()
        for j in range(3):
            copy(3 + j, out_ref.at[ids[j], other_rows], ids[j], other_rows, (x, y, 1 - c)).wait_recv()
        for cp in sends + passed:
            cp.wait_send()

    return pl.pallas_call(
        body, name=name, out_shape=jax.ShapeDtypeStruct((N_CHIP,) + tuple(src.shape), src.dtype),
        in_specs=[ANY], out_specs=ANY,
        scratch_shapes=[pltpu.SemaphoreType.DMA((6,)), pltpu.SemaphoreType.DMA((6,))],
    )(src)


def _pair_swap(a, *, name):
    n, r, cols = a.shape
    hr = r // 2

    def body(x_ref, out_ref, send_sem, recv_sem):
        x, y, c = _place()
        other_rows = pl.ds(pl.multiple_of((1 - c) * hr, 16), hr)
        cp = pltpu.make_async_remote_copy(src_ref=x_ref.at[:, other_rows], dst_ref=out_ref, send_sem=send_sem,
                                          recv_sem=recv_sem, device_id=(x, y, 1 - c), device_id_type=MESH)
        cp.start()
        cp.wait()

    return pl.pallas_call(
        body, name=name, out_shape=jax.ShapeDtypeStruct((n, hr, cols), a.dtype), in_specs=[ANY], out_specs=ANY,
        scratch_shapes=[pltpu.SemaphoreType.DMA, pltpu.SemaphoreType.DMA],
    )(a)


def _sibling_copy(a, *, name):
    def body(x_ref, out_ref, send_sem, recv_sem):
        x, y, c = _place()
        cp = pltpu.make_async_remote_copy(src_ref=x_ref, dst_ref=out_ref, send_sem=send_sem, recv_sem=recv_sem,
                                          device_id=(x, y, 1 - c), device_id_type=MESH)
        cp.start()
        cp.wait()

    return pl.pallas_call(
        body, name=name, out_shape=jax.ShapeDtypeStruct(a.shape, a.dtype), in_specs=[ANY], out_specs=ANY,
        scratch_shapes=[pltpu.SemaphoreType.DMA, pltpu.SemaphoreType.DMA],
    )(a)


def _sum_slots(a, *, name):
    _, r, c = a.shape
    tr = _row_tile(r, 256)

    def body(a_ref, o_ref):
        acc = a_ref[0].astype(F32)
        for j in range(1, N_CHIP):
            acc = acc + a_ref[j].astype(F32)
        o_ref[...] = acc

    return pl.pallas_call(
        body, name=name, grid=(r // tr,), in_specs=[pl.BlockSpec((N_CHIP, tr, c), lambda i: (0, i, 0))],
        out_specs=pl.BlockSpec((tr, c), lambda i: (i, 0)), out_shape=jax.ShapeDtypeStruct((r, c), F32),
        compiler_params=_cp(("arbitrary",)),
    )(a)


def _add2(a, b, *, name):
    r, c = a.shape
    tr = _row_tile(r)

    def body(a_ref, b_ref, o_ref):
        o_ref[...] = (a_ref[...].astype(F32) + b_ref[...].astype(F32)).astype(o_ref.dtype)

    spec = pl.BlockSpec((tr, c), lambda i: (i, 0))
    return pl.pallas_call(
        body, name=name, grid=(r // tr,), in_specs=[spec, spec], out_specs=spec,
        out_shape=jax.ShapeDtypeStruct((r, c), a.dtype), compiler_params=_cp(("arbitrary",)),
    )(a, b)


BIG = ("w_in", "w_mlp1", "w_attn_out", "w_ssd_out", "w_o", "w_mlp2")
COL_SHARDED = ("w_mlp1", "w_in")
ROW_SHARDED = ("w_attn_out", "w_ssd_out", "w_o", "w_mlp2")
LATE = ROW_SHARDED + ("w_mlp1",)
SMALL = ("b_ada", "norm1_w", "norm2_w", "q_norm_w", "k_norm_w", "conv_b", "A_log", "dt_bias", "ssd_D", "ssd_norm_w")
NAMES = ("w_ada", "b_ada", "norm1_w", "norm2_w", "w_in", "q_norm_w", "k_norm_w", "conv_w", "conv_b", "A_log", "dt_bias",
         "ssd_D", "ssd_norm_w", "w_attn_out", "w_ssd_out", "w_o", "w_mlp1", "w_mlp2")
W_IN_COLS = 8768


def _permute_in(w):
    return jnp.concatenate([w[:, 4608:6656], w[:, 6720:8768], w[:, 1536:4608], w[:, 0:1536], w[:, 6656:6720],
                            jnp.zeros((w.shape[0], PW - W_IN_COLS), w.dtype)], axis=1)


def _unpermute_in(wp):
    return jnp.concatenate([wp[:, Q0:DT0], wp[:, XS0:Q0], wp[:, Z0:GA0], wp[:, DT0:DT0 + 64], wp[:, GA0:XS0]], axis=1)


def _pad_to(v, n):
    return jnp.pad(v, (0, n - v.shape[0]))


def _step(w, m, v, loss_target):
    xi, yi, ci = _place()
    chip = 2 * xi + yi
    dev = 4 * xi + 2 * yi + ci
    x, tgt = w["x"], loss_target
    d = x.shape[1]

    cw = w["conv_w"].shape[1]
    v0 = _pad_to(jnp.concatenate([w["c"].reshape(-1), w["conv_w"].reshape(-1)]), 5120).reshape(8, 640)
    g0 = _allgather8(v0, name="ag_cond").reshape(N_DEV, 5120)
    c_all = g0[:, 0:d]
    conv_w = jnp.concatenate([g0[2 * k, d:d + D_CONV * cw].reshape(D_CONV, cw) for k in range(N_CHIP)], axis=1)
    sc = _silu_cast(c_all, name="silu_c")
    modp = _mm(sc, w["w_ada"].astype(MMD), name="ada_fwd", outs=[F32], tm=8, tn=512)
    g1 = _allgather8(modp, name="ag_mod").reshape(N_DEV, N_DEV, modp.shape[1])
    mod_all = jnp.concatenate([g1[2 * k] for k in range(N_CHIP)], axis=1)
    mod = (lax.dynamic_slice_in_dim(mod_all, dev, 1, axis=0) + w["b_ada"]).reshape(6, d)

    mine, mod = lax.optimization_barrier((w["w_in"].astype(MMD), mod))
    gath = lax.dynamic_update_slice_in_dim(_gather_weights(mine, name="ag_w_in"), mine[None], chip, axis=0)
    late_mine = jnp.concatenate([w[n].astype(MMD) for n in LATE], axis=0)
    late_mine, gath = lax.optimization_barrier((late_mine, gath))
    ag_sems, ag_src, ag_land, ag_token = _chips_start(late_mine, False, name="ag_late_start")
    mod = mod + ag_token[0:1, 0:1]
    w_in = jnp.concatenate([gath[k] for k in range(N_CHIP)], axis=1)
    wts = {"w_in_p": _permute_in(w_in), "w_dt": jnp.pad(w_in[:, 6656:6720], ((0, 0), (0, 64)))}
    small = {n: w[n] for n in SMALL if n != "b_ada"}
    small["conv_w"] = conv_w

    def own_slot(land, src):
        return lax.dynamic_update_slice_in_dim(land, src, chip, axis=0)

    def late_weights(after):
        src, land = _chips_wait(ag_sems, ag_src, ag_land, after, False, name="ag_late_wait")
        land = own_slot(land, src[None])
        out, o = {}, 0
        for n in LATE:
            rows = w[n].shape[0]
            part = land[:, o:o + rows]
            out[n] = (jnp.concatenate([part[k] for k in range(N_CHIP)], axis=1) if n in COL_SHARDED
                      else part.reshape(N_CHIP * rows, w[n].shape[1]))
            o += rows
        return out

    def pair_sums(slots, tag):
        _, rows, cols = slots.shape
        hr = rows // 2
        theirs = _pair_swap(slots, name="rs_pair_" + tag)
        ours = lax.dynamic_slice_in_dim(slots, ci * hr, hr, axis=1)
        pair = _add2(ours.reshape(N_CHIP * hr, cols), theirs.reshape(N_CHIP * hr, cols), name="rs_pair_sum_" + tag)
        return pair.reshape(N_CHIP, hr, cols)

    def finish(recv, pair, tag):
        recv = own_slot(recv, lax.dynamic_slice_in_dim(pair, chip, 1, axis=0))
        half = _sum_slots(recv, name="rs_sum_" + tag)
        other = _sibling_copy(half, name="rs_sibling_" + tag)
        return jnp.where(ci == 0, jnp.concatenate([half, other], axis=0), jnp.concatenate([other, half], axis=0))

    started = {}

    def late_grads(gw):
        slots = []
        for k in range(N_CHIP):
            parts = []
            for n in LATE:
                rows = w[n].shape[0]
                blk = gw[n][:, k * rows:(k + 1) * rows] if n in COL_SHARDED else gw[n][k * rows:(k + 1) * rows]
                parts.append(blk.astype(MMD))
            slots.append(jnp.concatenate(parts, axis=0))
        pair = pair_sums(jnp.stack(slots), "late")
        sems, src, land, token = _chips_start(pair, True, name="rs_late_start")
        started["late"] = (sems, src, land)
        return token[0:1, 0:1]

    def in_grad(g):
        g_in = _unpermute_in(g)
        cols_in = w["w_in"].shape[1]
        pair = pair_sums(jnp.stack([g_in[:, k * cols_in:(k + 1) * cols_in].astype(MMD) for k in range(N_CHIP)]), "w_in")
        sems, src, land, token = _chips_start(pair, True, name="rs_w_in_start")
        started["w_in"] = (sems, src, land)
        return token

    loss, grad_x, dmod, gw, gs = _local_step(x, tgt, mod, wts, small, late_weights, late_grads, in_grad)

    grads = {}
    pair, land = _chips_wait(*started["w_in"], grad_x, True, name="rs_w_in_wait")
    grads["w_in"] = finish(land, pair, "w_in")
    pair, land = _chips_wait(*started["late"], grad_x, True, name="rs_late_wait")
    total, o = finish(land, pair, "late"), 0
    for n in LATE:
        rows = w[n].shape[0]
        grads[n] = total[o:o + rows]
        o += rows

    order = ([dmod.reshape(-1)] + [gs[n].reshape(-1) for n in SMALL if n != "b_ada"] + [gs["conv_w"].reshape(-1)]
             + [loss.reshape(-1)])
    vec = jnp.concatenate(order)
    n_small = vec.shape[0]
    n_pad = -(-n_small // 1024) * 1024
    g2 = _allgather8(_pad_to(vec, n_pad).reshape(8, n_pad // 8), name="ag_small")
    tot = _rows_sum(g2, N_DEV, name="small_sum").reshape(-1)
    loss = tot[n_small - 1]
    dmod_all = g2.reshape(N_DEV, n_pad)[:, 0:6 * d]
    off = 0
    for n in SMALL:
        grads[n] = tot[off:off + w[n].size].reshape(w[n].shape)
        off += w[n].size
    conv_full = tot[off:off + D_CONV * N_CHIP * cw].reshape(D_CONV, N_CHIP * cw)
    grads["conv_w"] = lax.dynamic_slice_in_dim(conv_full, chip * cw, cw, axis=1)
    ada_cols = w["w_ada"].shape[1]
    dmod_mine = lax.dynamic_slice_in_dim(dmod_all, chip * ada_cols, ada_cols, axis=1).astype(MMD)
    grads["w_ada"] = _mm_tn(sc, dmod_mine, name="ada_dw", tk=512, tn=512, tmm=8)

    delta, new_m, new_v = {}, {}, {}
    pack = lambda t: jnp.concatenate([t[n].reshape(-1) for n in SMALL]).reshape(1, -1)
    ds_, ms_, vs_ = _adamw(pack(w), pack(grads), pack(m), pack(v), name="adamw_small")
    off = 0
    for n in SMALL:
        for dst, src in ((delta, ds_), (new_m, ms_), (new_v, vs_)):
            dst[n] = src[0, off:off + w[n].size].reshape(w[n].shape)
        off += w[n].size
    for n in ("w_ada", "conv_w") + BIG:
        delta[n], new_m[n], new_v[n] = _adamw(w[n], grads[n], m[n], v[n], name="adamw_" + n)
    return loss, grad_x, grads, delta, new_m, new_v


def kernel(x, c, w_ada, b_ada, norm1_w, norm2_w, w_in, q_norm_w, k_norm_w, conv_w, conv_b, A_log, dt_bias, ssd_D, ssd_norm_w, w_attn_out, w_ssd_out, w_o, w_mlp1, w_mlp2, loss_target, m_w_ada, m_b_ada, m_norm1_w, m_norm2_w, m_w_in, m_q_norm_w, m_k_norm_w, m_conv_w, m_conv_b, m_A_log, m_dt_bias, m_ssd_D, m_ssd_norm_w, m_w_attn_out, m_w_ssd_out, m_w_o, m_w_mlp1, m_w_mlp2, v_w_ada, v_b_ada, v_norm1_w, v_norm2_w, v_w_in, v_q_norm_w, v_k_norm_w, v_conv_w, v_conv_b, v_A_log, v_dt_bias, v_ssd_D, v_ssd_norm_w, v_w_attn_out, v_w_ssd_out, v_w_o, v_w_mlp1, v_w_mlp2):
    args = dict(locals())
    strip = lambda a: a[0] if a.ndim == 3 else a
    w = {n: strip(args[n]) for n in NAMES + ("x", "c")}
    m = {n: strip(args["m_" + n]) for n in NAMES}
    v = {n: strip(args["v_" + n]) for n in NAMES}
    loss, grad_x, grads, delta, new_m, new_v = _step(w, m, v, loss_target[0])
    like = lambda t, n: t.reshape(args[n].shape)
    return (loss, grad_x[None], *[like(grads[n], n) for n in NAMES], *[like(delta[n], n) for n in NAMES],
            *[like(new_m[n], n) for n in NAMES], *[like(new_v[n], n) for n in NAMES])
```

```python
import math

import jax
import jax.numpy as jnp
from jax import lax
from jax.experimental import pallas as pl
from jax.experimental.pallas import tpu as pltpu

F32 = jnp.float32
MMD = jnp.bfloat16
EPS = 1e-6
NEG = -1e30
MIB = 1024 * 1024
VMEM_BIG = 56 * MIB
VMEM_MID = 40 * MIB

GRID_W = 64
N_Q_HEADS, N_KV_HEADS, HEAD_DIM = 16, 4, 64
ROPE_THETA = 10000.0
SSD_HEADS, SSD_GROUPS, SSD_P, SSD_N, CHUNK = 32, 4, 64, 128, 128
HPG = SSD_HEADS // SSD_GROUPS
D_CONV = 5
ADAM_LR, ADAM_B1, ADAM_B2, ADAM_EPS, ADAM_WD, ADAM_STEP = 0.001, 0.9, 0.999, 1e-08, 0.01, 10

Z0, GA0, GS0, XS0, B0, C0, Q0, K0, V0, DT0, PW = 0, 2048, 3072, 4096, 6144, 6656, 7168, 8192, 8448, 8704, 8832

MESH = pl.DeviceIdType.MESH
NT = (((1,), (1,)), ((), ()))
TN = (((0,), (0,)), ((), ()))


def _cp(sem=None, vmem=VMEM_MID):
    return pltpu.CompilerParams(dimension_semantics=sem, vmem_limit_bytes=vmem)


def _tile(n, pref):
    t = min(n, pref)
    while n % t:
        t //= 2
    return t


def _dot(a, b, dims=None):
    if dims is None:
        return jnp.dot(a, b, preferred_element_type=F32)
    return lax.dot_general(a, b, dims, preferred_element_type=F32)


def _dot_hi(a01, b):
    a = a01.astype(jnp.bfloat16)
    h1 = b.astype(jnp.bfloat16)
    r1 = b - h1.astype(F32)
    h2 = r1.astype(jnp.bfloat16)
    return _dot(a, h1) + _dot(a, h2) + _dot(a, (r1 - h2.astype(F32)).astype(jnp.bfloat16))


def _sigmoid(x):
    return jax.nn.sigmoid(x)


def _mm(a, b, *, name, outs, nt=False, ta=False, extras=(), epi=None, tm=512, tn=512, n=None, b_outer=False,
        vmem=VMEM_MID):
    assert not (nt and ta)
    k, m = a.shape if ta else a.shape[::-1]
    if n is None:
        n = b.shape[0] if nt else b.shape[1]
    tm, tn = _tile(m, tm), _tile(n, tn)
    gi, gj = m // tm, n // tn
    if b_outer:
        grid = (gj, gi)
        ij = lambda p, q: (q, p)
    else:
        grid = (gi, gj)
        ij = lambda p, q: (p, q)
    if ta:
        a_spec = pl.BlockSpec((k, tm), lambda p, q: (0, ij(p, q)[0]))
    else:
        a_spec = pl.BlockSpec((tm, k), lambda p, q: (ij(p, q)[0], 0))
    if nt:
        b_spec = pl.BlockSpec((tn, k), lambda p, q: (ij(p, q)[1], 0))
    else:
        b_spec = pl.BlockSpec((k, tn), lambda p, q: (0, ij(p, q)[1]))
    e_specs = []
    for arr, kind, off in extras:
        ob = off // tn
        assert off % tn == 0
        if kind == "tile":
            e_specs.append(pl.BlockSpec((tm, tn), lambda p, q, ob=ob: (ij(p, q)[0], ob + ij(p, q)[1])))
        else:
            e_specs.append(pl.BlockSpec((1, tn), lambda p, q, ob=ob: (0, ob + ij(p, q)[1])))
    ne = len(extras)

    def body(a_ref, b_ref, *rest):
        acc = _dot(a_ref[...], b_ref[...], NT if nt else (TN if ta else None))
        res = epi(acc, *[e[...] for e in rest[:ne]]) if epi is not None else (acc,)
        for o_ref, r in zip(rest[ne:], res):
            o_ref[...] = r.astype(o_ref.dtype)

    out = pl.pallas_call(
        body, name=name, grid=grid,
        in_specs=[a_spec, b_spec] + e_specs,
        out_specs=[pl.BlockSpec((tm, tn), lambda p, q: ij(p, q)) for _ in outs],
        out_shape=[jax.ShapeDtypeStruct((m, n), dt) for dt in outs],
        compiler_params=_cp(("arbitrary", "arbitrary"), vmem),
    )(a, b, *[e[0] for e in extras])
    return out if len(outs) > 1 else out[0]


def _mm_tn(a, g, *, name, tk=512, tn=1024, tmm=4096, vmem=VMEM_MID):
    m, k = a.shape
    n = g.shape[1]
    tk, tn, tmm = _tile(k, tk), _tile(n, tn), _tile(m, tmm)

    def body(a_ref, g_ref, o_ref):
        p = _dot(a_ref[...], g_ref[...], TN)

        @pl.when(pl.program_id(2) == 0)
        def _():
            o_ref[...] = p

        @pl.when(pl.program_id(2) > 0)
        def _():
            o_ref[...] += p

    return pl.pallas_call(
        body, name=name, grid=(k // tk, n // tn, m // tmm),
        in_specs=[pl.BlockSpec((tmm, tk), lambda i, j, r: (r, i)), pl.BlockSpec((tmm, tn), lambda i, j, r: (r, j))],
        out_specs=pl.BlockSpec((tk, tn), lambda i, j, r: (i, j)),
        out_shape=jax.ShapeDtypeStruct((k, n), F32),
        compiler_params=_cp(("arbitrary", "arbitrary", "arbitrary"), vmem),
    )(a, g)


def _adamw(w, g, m, v, *, name):
    r, c = w.shape
    tr = _tile(r, 256) if r % 8 == 0 else r

    def body(w_ref, g_ref, m_ref, v_ref, d_ref, nm_ref, nv_ref):
        gg = g_ref[...]
        nm = ADAM_B1 * m_ref[...] + (1.0 - ADAM_B1) * gg
        nv = ADAM_B2 * v_ref[...] + (1.0 - ADAM_B2) * jnp.square(gg)
        m_hat = nm / (1.0 - ADAM_B1 ** ADAM_STEP)
        v_hat = nv / (1.0 - ADAM_B2 ** ADAM_STEP)
        d_ref[...] = -ADAM_LR * (m_hat / (jnp.sqrt(v_hat) + ADAM_EPS) + ADAM_WD * w_ref[...])
        nm_ref[...] = nm
        nv_ref[...] = nv

    spec = pl.BlockSpec((tr, c), lambda i: (i, 0))
    return pl.pallas_call(
        body, name=name, grid=(r // tr,), in_specs=[spec] * 4, out_specs=[spec] * 3,
        out_shape=[jax.ShapeDtypeStruct((r, c), F32)] * 3, compiler_params=_cp(("arbitrary",)),
    )(w, g, m, v)


def _rows_sum(a, groups, *, name):
    r = a.shape[0] // groups

    def body(a_ref, o_ref):
        acc = a_ref[0:r, :]
        for d in range(1, groups):
            acc = acc + a_ref[d * r:(d + 1) * r, :]
        o_ref[...] = acc

    return pl.pallas_call(body, name=name, out_shape=jax.ShapeDtypeStruct((r, a.shape[1]), F32))(a)


def _silu_cast(a, *, name):
    def body(a_ref, o_ref):
        x = a_ref[...]
        o_ref[...] = (x * _sigmoid(x)).astype(o_ref.dtype)

    return pl.pallas_call(body, name=name, out_shape=jax.ShapeDtypeStruct(a.shape, MMD))(a)


def _sumsq(a, *, name):
    m, n = a.shape
    tm = _tile(m, 512)

    def body(a_ref, o_ref):
        x = a_ref[...]
        p = jnp.sum(jnp.sum(x * x, axis=1, keepdims=True), axis=0, keepdims=True)

        @pl.when(pl.program_id(0) == 0)
        def _():
            o_ref[...] = p

        @pl.when(pl.program_id(0) > 0)
        def _():
            o_ref[...] += p

    return pl.pallas_call(
        body, name=name, grid=(m // tm,), in_specs=[pl.BlockSpec((tm, n), lambda i: (i, 0))],
        out_specs=pl.BlockSpec((1, 1), lambda i: (0, 0)), out_shape=jax.ShapeDtypeStruct((1, 1), F32),
        compiler_params=_cp(("arbitrary",)),
    )(a)


def _acc_rows(o_ref, p, first):
    @pl.when(first)
    def _():
        o_ref[...] = p

    @pl.when(jnp.logical_not(first))
    def _():
        o_ref[...] += p


def _ln_mod(x, w, scale, shift, *, name):
    s, d = x.shape
    tm = _tile(s, 512)

    def body(x_ref, w_ref, sc_ref, sh_ref, o_ref):
        xv = x_ref[...]
        r = lax.rsqrt(jnp.mean(xv * xv, axis=-1, keepdims=True) + EPS)
        o_ref[...] = ((xv * r) * w_ref[...] * (1.0 + sc_ref[...]) + sh_ref[...]).astype(o_ref.dtype)

    row = pl.BlockSpec((1, d), lambda i: (0, 0))
    big = pl.BlockSpec((tm, d), lambda i: (i, 0))
    return pl.pallas_call(
        body, name=name, grid=(s // tm,), in_specs=[big, row, row, row], out_specs=big,
        out_shape=jax.ShapeDtypeStruct((s, d), MMD), compiler_params=_cp(("arbitrary",)),
    )(x, w, scale, shift)


def _ln_mod_bwd(dh, x, w, scale, dres, *, name):
    s, d = x.shape
    tm = _tile(s, 512)

    def body(dh_ref, x_ref, w_ref, sc_ref, dres_ref, dx_ref, dsh_ref, dsc_ref, dw_ref):
        xv = x_ref[...]
        dhv = dh_ref[...].astype(F32)
        r = lax.rsqrt(jnp.mean(xv * xv, axis=-1, keepdims=True) + EPS)
        nv = xv * r
        wv = w_ref[...]
        g1 = 1.0 + sc_ref[...]
        dn = dhv * (wv * g1)
        dx_ref[...] = dres_ref[...] + r * (dn - nv * jnp.mean(dn * nv, axis=-1, keepdims=True))
        first = pl.program_id(0) == 0
        _acc_rows(dsh_ref, jnp.sum(dhv, axis=0, keepdims=True), first)
        _acc_rows(dsc_ref, jnp.sum(dhv * nv * wv, axis=0, keepdims=True), first)
        _acc_rows(dw_ref, jnp.sum(dhv * nv * g1, axis=0, keepdims=True), first)

    row = pl.BlockSpec((1, d), lambda i: (0, 0))
    big = pl.BlockSpec((tm, d), lambda i: (i, 0))
    return pl.pallas_call(
        body, name=name, grid=(s // tm,), in_specs=[big, big, row, row, big], out_specs=[big, row, row, row],
        out_shape=[jax.ShapeDtypeStruct((s, d), F32)] + [jax.ShapeDtypeStruct((1, d), F32)] * 3,
        compiler_params=_cp(("arbitrary",)),
    )(dh, x, w, scale, dres)


def _gate_bwd(dy, u, gate, *, name):
    s, d = dy.shape
    tm = _tile(s, 512)

    def body(dy_ref, u_ref, g_ref, du_ref, dg_ref):
        dyv = dy_ref[...]
        du_ref[...] = (dyv * g_ref[...]).astype(du_ref.dtype)
        _acc_rows(dg_ref, jnp.sum(dyv * u_ref[...].astype(F32), axis=0, keepdims=True), pl.program_id(0) == 0)

    row = pl.BlockSpec((1, d), lambda i: (0, 0))
    big = pl.BlockSpec((tm, d), lambda i: (i, 0))
    return pl.pallas_call(
        body, name=name, grid=(s // tm,), in_specs=[big, big, row], out_specs=[big, row],
        out_shape=[jax.ShapeDtypeStruct((s, d), MMD), jax.ShapeDtypeStruct((1, d), F32)],
        compiler_params=_cp(("arbitrary",)),
    )(dy, u, gate)


def _seg64(v, e):
    hi = v.astype(jnp.bfloat16)
    lo = (v - hi.astype(F32)).astype(jnp.bfloat16)
    return _dot(hi, e) + _dot(lo, e)


def _rope_tables(s):
    rows = s // GRID_W
    pos_row = jnp.repeat(jnp.arange(rows, dtype=jnp.int32), GRID_W).astype(F32)
    pos_col = jnp.tile(jnp.arange(GRID_W, dtype=jnp.int32), rows).astype(F32)
    axis_dim = HEAD_DIM // 2
    inv_freq = ROPE_THETA ** (-jnp.arange(0, axis_dim, 2, dtype=F32) / axis_dim)
    ang_r = pos_row[:, None] * inv_freq[None, :]
    ang_c = pos_col[:, None] * inv_freq[None, :]
    zero = jnp.zeros_like(ang_r)
    cos = jnp.concatenate([jnp.cos(ang_r), jnp.cos(ang_r), jnp.cos(ang_c), jnp.cos(ang_c)], axis=1)
    s_a = jnp.concatenate([-jnp.sin(ang_r), zero, -jnp.sin(ang_c), zero], axis=1)
    s_b = jnp.concatenate([zero, jnp.sin(ang_r), zero, jnp.sin(ang_c)], axis=1)
    return [jnp.tile(t, (1, 2)) for t in (cos, s_a, s_b)]


def _e128():
    i = jnp.arange(128)
    return (i[:, None] // 64 == i[None, :] // 64).astype(jnp.bfloat16)


QKW = N_Q_HEADS * HEAD_DIM + N_KV_HEADS * HEAD_DIM


def _qk_fwd(proj, wrow, scrow, tabs, *, name):
    s = proj.shape[0]
    tm = _tile(s, 1024)

    def body(x_ref, w_ref, sc_ref, cos_ref, sa_ref, sb_ref, e_ref, o_ref, ot_ref):
        u = x_ref[...].astype(F32)
        r = lax.rsqrt(_seg64(u * u, e_ref[...]) * (1.0 / HEAD_DIM) + EPS)
        nv = (u * r) * w_ref[...]
        ro = nv * cos_ref[...] + pltpu.roll(nv, 112, 1) * sa_ref[...] + pltpu.roll(nv, 16, 1) * sb_ref[...]
        out = ro * sc_ref[...]
        o_ref[...] = out.astype(o_ref.dtype)
        ot_ref[...] = out.T.astype(ot_ref.dtype)

    tab = pl.BlockSpec((tm, 128), lambda i, j: (i, 0))
    row = pl.BlockSpec((1, 128), lambda i, j: (0, j))
    return pl.pallas_call(
        body, name=name, grid=(s // tm, QKW // 128),
        in_specs=[pl.BlockSpec((tm, 128), lambda i, j: (i, Q0 // 128 + j)), row, row, tab, tab, tab,
                  pl.BlockSpec((128, 128), lambda i, j: (0, 0))],
        out_specs=[pl.BlockSpec((tm, 128), lambda i, j: (i, j)), pl.BlockSpec((128, tm), lambda i, j: (j, i))],
        out_shape=[jax.ShapeDtypeStruct((s, QKW), MMD), jax.ShapeDtypeStruct((QKW, s), MMD)],
        compiler_params=_cp(("arbitrary", "arbitrary")),
    )(proj, wrow, scrow, *tabs, _e128())


def _qk_bwd(dqt, dkt, proj, wrow, scrow, tabs, *, name):
    s = proj.shape[0]
    tm = _tile(s, 1024)
    nq = dqt.shape[0] // 128

    def body(dq_ref, dk_ref, x_ref, w_ref, sc_ref, cos_ref, sa_ref, sb_ref, e_ref, du_ref, dw_ref):
        e = e_ref[...]
        d = jnp.where(pl.program_id(0) < nq, dq_ref[...], dk_ref[...]).T * sc_ref[...]
        dn = d * cos_ref[...] + pltpu.roll(d * sa_ref[...], 16, 1) + pltpu.roll(d * sb_ref[...], 112, 1)
        u = x_ref[...].astype(F32)
        r = lax.rsqrt(_seg64(u * u, e) * (1.0 / HEAD_DIM) + EPS)
        uh = u * r
        _acc_rows(dw_ref, jnp.sum(dn * uh, axis=0, keepdims=True), pl.program_id(1) == 0)
        dnw = dn * w_ref[...]
        du_ref[...] = (r * (dnw - uh * (_seg64(dnw * uh, e) * (1.0 / HEAD_DIM)))).astype(du_ref.dtype)

    tab = pl.BlockSpec((tm, 128), lambda j, i: (i, 0))
    row = pl.BlockSpec((1, 128), lambda j, i: (0, j))
    return pl.pallas_call(
        body, name=name, grid=(QKW // 128, s // tm),
        in_specs=[pl.BlockSpec((128, tm), lambda j, i: (jnp.minimum(j, nq - 1), i)),
                  pl.BlockSpec((128, tm), lambda j, i: (jnp.maximum(j - nq, 0), i)),
                  pl.BlockSpec((tm, 128), lambda j, i: (i, Q0 // 128 + j)),
                  row, row, tab, tab, tab, pl.BlockSpec((128, 128), lambda j, i: (0, 0))],
        out_specs=[pl.BlockSpec((tm, 128), lambda j, i: (i, j)), row],
        out_shape=[jax.ShapeDtypeStruct((s, QKW), MMD), jax.ShapeDtypeStruct((1, QKW), F32)],
        compiler_params=_cp(("arbitrary", "arbitrary")),
    )(dqt, dkt, proj, wrow, scrow, *tabs, _e128())


REP = N_Q_HEADS // N_KV_HEADS


def _lanes(ref):
    return jnp.concatenate([ref[r] for r in range(REP)], axis=1)


V_AUG = HEAD_DIM + 8
LOG2E = math.log2(math.e)


def _flash_fwd(qkt, vta, *, name):
    s = qkt.shape[2]
    tq, tk = _tile(s, 1024), _tile(s, 512)
    nk = s // tk
    lanes = REP * tq

    def body(q_ref, k_ref, v_ref, o_ref, lse_ref, m_ref, acc_ref):
        j = pl.program_id(2)

        @pl.when(j == 0)
        def _():
            m_ref[...] = jnp.full_like(m_ref, NEG)
            acc_ref[...] = jnp.zeros_like(acc_ref)

        st = _dot(k_ref[0], _lanes(q_ref), TN)
        m_prev = m_ref[...]
        m_new = jnp.maximum(m_prev, jnp.max(st, axis=0, keepdims=True))
        p = jnp.exp2(st - m_new).astype(MMD)
        acc_ref[...] = jnp.exp2(m_prev - m_new) * acc_ref[...] + _dot(v_ref[0], p)
        m_ref[...] = m_new

        @pl.when(j == nk - 1)
        def _():
            acc = acc_ref[...]
            l = acc[HEAD_DIM:HEAD_DIM + 1]
            o = acc[0:HEAD_DIM] / l
            ls = m_ref[...] + jnp.log(l) * LOG2E
            for r in range(REP):
                o_ref[r] = o[:, r * tq:(r + 1) * tq].astype(o_ref.dtype)
                lse_ref[r] = ls[:, r * tq:(r + 1) * tq]

    qspec = pl.BlockSpec((REP, HEAD_DIM, tq), lambda g, i, j: (g, 0, i))
    return pl.pallas_call(
        body, name=name, grid=(N_KV_HEADS, s // tq, nk),
        in_specs=[qspec, pl.BlockSpec((1, HEAD_DIM, tk), lambda g, i, j: (N_Q_HEADS + g, 0, j)),
                  pl.BlockSpec((1, V_AUG, tk), lambda g, i, j: (g, 0, j))],
        out_specs=[qspec, pl.BlockSpec((REP, 1, tq), lambda g, i, j: (g, 0, i))],
        out_shape=[jax.ShapeDtypeStruct((N_Q_HEADS, HEAD_DIM, s), MMD), jax.ShapeDtypeStruct((N_Q_HEADS, 1, s), F32)],
        scratch_shapes=[pltpu.VMEM((1, lanes), F32), pltpu.VMEM((V_AUG, lanes), F32)],
        compiler_params=_cp(("arbitrary", "arbitrary", "arbitrary"), VMEM_BIG),
    )(qkt, qkt, vta)


def _flash_bwd(qkt, k_h, v_h, dot, ot, lse, *, name):
    s = qkt.shape[2]
    tq, tk = _tile(s, 512), _tile(s, 1024)
    nk = s // tk

    def body(q_ref, kt_ref, k_ref, v_ref, do_ref, o_ref, lse_ref, dq_ref, dk_ref, dv_ref, dq_acc):
        i, j = pl.program_id(1), pl.program_id(2)
        q, do = _lanes(q_ref), _lanes(do_ref)
        delta = jnp.sum(do.astype(F32) * _lanes(o_ref).astype(F32), axis=0, keepdims=True)
        k, v = k_ref[0], v_ref[0]
        p = jnp.exp2(_dot(k, q) - _lanes(lse_ref))
        dvc = _dot(p.astype(MMD), do, NT)
        ds = (p * (_dot(v, do) - delta)).astype(MMD)
        dkc = _dot(ds, q, NT) * (1.0 / LOG2E)
        dqc = _dot(kt_ref[0], ds)
        rows = pl.ds(pl.multiple_of(j * tk, tk), tk)

        @pl.when(i == 0)
        def _():
            dk_ref[0, rows, :] = dkc
            dv_ref[0, rows, :] = dvc

        @pl.when(i > 0)
        def _():
            dk_ref[0, rows, :] += dkc
            dv_ref[0, rows, :] += dvc

        @pl.when(j == 0)
        def _():
            dq_acc[...] = dqc

        @pl.when(j > 0)
        def _():
            dq_acc[...] += dqc

        @pl.when(j == nk - 1)
        def _():
            acc = dq_acc[...]
            for r in range(REP):
                dq_ref[r] = acc[:, r * tq:(r + 1) * tq]

    qspec = pl.BlockSpec((REP, HEAD_DIM, tq), lambda g, i, j: (g, 0, i))
    kvin = pl.BlockSpec((1, tk, HEAD_DIM), lambda g, i, j: (g, j, 0))
    kvres = pl.BlockSpec((1, s, HEAD_DIM), lambda g, i, j: (g, 0, 0))
    return pl.pallas_call(
        body, name=name, grid=(N_KV_HEADS, s // tq, nk),
        in_specs=[qspec, pl.BlockSpec((1, HEAD_DIM, tk), lambda g, i, j: (N_Q_HEADS + g, 0, j)), kvin, kvin,
                  qspec, qspec, pl.BlockSpec((REP, 1, tq), lambda g, i, j: (g, 0, i))],
        out_specs=[qspec, kvres, kvres],
        out_shape=[jax.ShapeDtypeStruct((N_Q_HEADS, HEAD_DIM, s), F32), jax.ShapeDtypeStruct((N_KV_HEADS, s, HEAD_DIM), F32),
                   jax.ShapeDtypeStruct((N_KV_HEADS, s, HEAD_DIM), F32)],
        scratch_shapes=[pltpu.VMEM((HEAD_DIM, REP * tq), F32)],
        compiler_params=_cp(("arbitrary", "arbitrary", "arbitrary"), VMEM_BIG),
    )(qkt, qkt, k_h, v_h, dot, ot, lse)


HALO = 8
CONV_W = 2048 + 2 * SSD_GROUPS * SSD_N


def _shifted(win, off, r):
    return pltpu.roll(win, (r + 2 * HALO - off) % (r + 2 * HALO), 0)[0:r]


def _conv_fwd(proj, w8, brow, *, name):
    s = proj.shape[0]
    cb = 256
    r = _tile(s, 512)

    def body(x_ref, w_ref, b_ref, o_ref, pad_ref):
        zeros = jnp.zeros((HALO, cb), F32)
        pad_ref[0:HALO, :] = zeros
        pad_ref[s + HALO:s + 2 * HALO, :] = zeros

        def fill(i, carry):
            st = pl.multiple_of(i * r, r)
            pad_ref[pl.ds(st + HALO, r), :] = x_ref[pl.ds(st, r), :].astype(F32)
            return carry

        lax.fori_loop(0, s // r, fill, 0)
        wv = w_ref[...]
        bv = b_ref[...]

        def step(i, carry):
            st = pl.multiple_of(i * r, r)
            win = pad_ref[pl.ds(st, r + 2 * HALO), :]
            acc = bv + wv[0:1, :] * _shifted(win, HALO - 2, r)
            for t in range(1, D_CONV):
                acc = acc + wv[t:t + 1, :] * _shifted(win, HALO - 2 + t, r)
            o_ref[pl.ds(st, r), :] = (acc * _sigmoid(acc)).astype(o_ref.dtype)
            return carry

        lax.fori_loop(0, s // r, step, 0)

    return pl.pallas_call(
        body, name=name, grid=(CONV_W // cb,),
        in_specs=[pl.BlockSpec((s, cb), lambda j: (0, XS0 // cb + j)), pl.BlockSpec((8, cb), lambda j: (0, j)),
                  pl.BlockSpec((1, cb), lambda j: (0, j))],
        out_specs=pl.BlockSpec((s, cb), lambda j: (0, j)),
        out_shape=jax.ShapeDtypeStruct((s, CONV_W), MMD),
        scratch_shapes=[pltpu.VMEM((s + 2 * HALO, cb), F32)],
        compiler_params=_cp(("arbitrary",), VMEM_MID),
    )(proj, w8, brow)


def _conv_bwd(proj, col0, ga, gb, w8, brow, *, name):
    s = proj.shape[0]
    width = ga.shape[1]
    cb = 128
    c0 = col0 // cb
    r = _tile(s, 512)

    def body(x_ref, ga_ref, gb_ref, w_ref, b_ref, dx_ref, dw_ref, db_ref, xpad, dpad):
        zeros = jnp.zeros((HALO, cb), F32)
        for ref in (xpad, dpad):
            ref[0:HALO, :] = zeros
            ref[s + HALO:s + 2 * HALO, :] = zeros

        def fill(i, carry):
            st = pl.multiple_of(i * r, r)
            xpad[pl.ds(st + HALO, r), :] = x_ref[pl.ds(st, r), :].astype(F32)
            return carry

        lax.fori_loop(0, s // r, fill, 0)
        wv = w_ref[...]
        bv = b_ref[...]

        def first(i, carry):
            st = pl.multiple_of(i * r, r)
            win = xpad[pl.ds(st, r + 2 * HALO), :]
            taps = [_shifted(win, HALO - 2 + t, r) for t in range(D_CONV)]
            u = bv
            for t in range(D_CONV):
                u = u + wv[t:t + 1, :] * taps[t]
            sg = _sigmoid(u)
            du = ((ga_ref[pl.ds(st, r), :].astype(F32) + gb_ref[pl.ds(st, r), :].astype(F32))
                  * (sg * (1.0 + u * (1.0 - sg))))
            dpad[pl.ds(st + HALO, r), :] = du
            out = [carry[0] + jnp.sum(du, axis=0, keepdims=True)]
            for t in range(D_CONV):
                out.append(carry[1 + t] + jnp.sum(du * taps[t], axis=0, keepdims=True))
            return tuple(out)

        sums = lax.fori_loop(0, s // r, first, tuple(jnp.zeros((1, cb), F32) for _ in range(1 + D_CONV)))
        db_ref[...] = sums[0]
        for t in range(D_CONV):
            dw_ref[t:t + 1, :] = sums[1 + t]
        dw_ref[D_CONV:8, :] = jnp.zeros((8 - D_CONV, cb), F32)

        def second(i, carry):
            st = pl.multiple_of(i * r, r)
            win = dpad[pl.ds(st, r + 2 * HALO), :]
            acc = wv[0:1, :] * _shifted(win, HALO + 2, r)
            for t in range(1, D_CONV):
                acc = acc + wv[t:t + 1, :] * _shifted(win, HALO + 2 - t, r)
            dx_ref[pl.ds(st, r), :] = acc.astype(dx_ref.dtype)
            return carry

        lax.fori_loop(0, s // r, second, 0)

    col = pl.BlockSpec((s, cb), lambda j: (0, j))
    return pl.pallas_call(
        body, name=name, grid=(width // cb,),
        in_specs=[pl.BlockSpec((s, cb), lambda j: (0, XS0 // cb + c0 + j)), col, col,
                  pl.BlockSpec((8, cb), lambda j: (0, c0 + j)), pl.BlockSpec((1, cb), lambda j: (0, c0 + j))],
        out_specs=[col, pl.BlockSpec((8, cb), lambda j: (0, j)), pl.BlockSpec((1, cb), lambda j: (0, j))],
        out_shape=[jax.ShapeDtypeStruct((s, width), MMD), jax.ShapeDtypeStruct((8, width), F32),
                   jax.ShapeDtypeStruct((1, width), F32)],
        scratch_shapes=[pltpu.VMEM((s + 2 * HALO, cb), F32), pltpu.VMEM((s + 2 * HALO, cb), F32)],
        compiler_params=_cp(("arbitrary",), VMEM_BIG),
    )(proj, ga, gb, w8, brow)


def _tri(lower):
    i = jnp.arange(CHUNK)
    return ((i[:, None] >= i[None, :]) if lower else (i[:, None] <= i[None, :])).astype(F32)


def _dt_fwd(raw, bias, arow, *, name):
    s = raw.shape[0]

    def body(r_ref, b_ref, a_ref, lo_ref, up_ref, dt_ref, cs_ref):
        u = r_ref[...] + b_ref[...]
        dt = jnp.maximum(u, 0.0) + jnp.log1p(jnp.exp(-jnp.abs(u)))
        dt_ref[...] = dt
        a = dt * a_ref[...]
        lane = lax.broadcasted_iota(jnp.int32, (CHUNK, 128), 1)
        cs_ref[...] = jnp.where(lane < SSD_HEADS, _dot_hi(lo_ref[...], a), _dot_hi(up_ref[...], a))

    blk = pl.BlockSpec((CHUNK, 128), lambda i: (i, 0))
    row = pl.BlockSpec((1, 128), lambda i: (0, 0))
    tri = pl.BlockSpec((CHUNK, CHUNK), lambda i: (0, 0))
    return pl.pallas_call(
        body, name=name, grid=(s // CHUNK,), in_specs=[blk, row, row, tri, tri], out_specs=[blk, blk],
        out_shape=[jax.ShapeDtypeStruct((s, 128), F32)] * 2, compiler_params=_cp(("arbitrary",)),
    )(raw, bias, arow, _tri(True), _tri(False))


def _dt_bwd(ddt0, ddt1, raw, bias, *, name):
    s = raw.shape[0]
    tm = _tile(s, 1024)

    def body(d0_ref, d1_ref, r_ref, b_ref, o_ref, db_ref):
        g = (d0_ref[...] + d1_ref[...]) * _sigmoid(r_ref[...] + b_ref[...])
        o_ref[...] = g.astype(o_ref.dtype)
        _acc_rows(db_ref, jnp.sum(g, axis=0, keepdims=True), pl.program_id(0) == 0)

    blk = pl.BlockSpec((tm, 128), lambda i: (i, 0))
    row = pl.BlockSpec((1, 128), lambda i: (0, 0))
    return pl.pallas_call(
        body, name=name, grid=(s // tm,), in_specs=[blk, blk, blk, row], out_specs=[blk, row],
        out_shape=[jax.ShapeDtypeStruct((s, 128), MMD), jax.ShapeDtypeStruct((1, 128), F32)],
        compiler_params=_cp(("arbitrary",)),
    )(ddt0, ddt1, raw, bias)


GW = HPG * SSD_P


GPS = SSD_GROUPS


def _ssd_specs(nc, rev):
    cc = (lambda c: nc - 1 - c) if rev else (lambda c: c)
    return dict(
        x=pl.BlockSpec((CHUNK, GPS * GW), lambda g, c: (cc(c), g)),
        b=pl.BlockSpec((CHUNK, GPS * SSD_N), lambda g, c: (cc(c), 2048 // (GPS * SSD_N) + g)),
        c=pl.BlockSpec((CHUNK, GPS * SSD_N), lambda g, c: (cc(c), 2048 // (GPS * SSD_N) + 1 + g)),
        lanes=pl.BlockSpec((CHUNK, 128), lambda g, c: (cc(c), 0)),
        drow=pl.BlockSpec((1, GPS * GW), lambda g, c: (0, g)),
        y=pl.BlockSpec((CHUNK, GPS * GW), lambda g, c: (cc(c), g)),
        h=pl.BlockSpec((GPS, 1, SSD_N, GW), lambda g, c: (g, cc(c), 0, 0)),
        n=pl.BlockSpec((CHUNK, GPS * SSD_N), lambda g, c: (cc(c), g)),
    )


def _ssd_mask(anti):
    ii = lax.broadcasted_iota(jnp.int32, (CHUNK, CHUNK), 0)
    jj = lax.broadcasted_iota(jnp.int32, (CHUNK, CHUNK), 1)
    return ii, jj, (ii <= jj) if anti else (ii >= jj)


def _expand(x, ex, terms=3):
    h1 = x.astype(jnp.bfloat16)
    r1 = x - h1.astype(F32)
    h2 = r1.astype(jnp.bfloat16)
    out = _dot(h1, ex) + _dot(h2, ex)
    if terms == 3:
        out = out + _dot((r1 - h2.astype(F32)).astype(jnp.bfloat16), ex)
    return out


def _headsum(a, e):
    hi = a.astype(jnp.bfloat16)
    return _dot(hi, e) + _dot((a - hi.astype(F32)).astype(jnp.bfloat16), e)


def _expand_mats():
    lane = jnp.arange(128)[None, :, None]
    col = jnp.arange(GW)[None, None, :]
    base = (jnp.arange(2)[:, None] * SSD_HEADS + jnp.arange(SSD_GROUPS)[None, :] * HPG).reshape(2 * SSD_GROUPS, 1, 1)
    return (lane == base + col // SSD_P).astype(jnp.bfloat16)


def _headsum_mats():
    e1 = (jnp.arange(GW)[:, None] // SSD_P == jnp.arange(128)[None, :]).astype(jnp.bfloat16)
    e2 = (jnp.arange(HPG * CHUNK)[:, None] // CHUNK == jnp.arange(128)[None, :]).astype(jnp.bfloat16)
    return e1, e2


def _ssd_fwd(xc, dt, cs, ex, drow, di, *, name):
    s = xc.shape[0]
    nc = s // CHUNK
    anti = di == 1
    sp = _ssd_specs(nc, anti)
    trow = 0 if anti else CHUNK - 1

    def body(x_ref, b_ref, c_ref, dt_ref, cs_ref, ex_ref, d_ref, y_ref, hp_ref, h_ref):
        @pl.when(pl.program_id(1) == 0)
        def _():
            h_ref[...] = jnp.zeros_like(h_ref)

        mask = _ssd_mask(anti)[2]
        dtv, csv = dt_ref[...], cs_ref[...]
        cst = csv.T
        for gi in range(GPS):
            cols = slice(gi * GW, (gi + 1) * GW)
            ncols = slice(gi * SSD_N, (gi + 1) * SSD_N)
            ex = ex_ref[gi]
            xb = x_ref[:, cols].astype(F32)
            bm, cm = b_ref[:, ncols], c_ref[:, ncols]
            csr = cst[SSD_HEADS * di + HPG * gi:SSD_HEADS * di + HPG * (gi + 1)]
            dtf = _expand(dtv, ex, 2)
            csf = _expand(csv, ex)
            tl = csf[trow:trow + 1, :]
            h = h_ref[gi]
            hp_ref[gi, 0] = h.astype(hp_ref.dtype)
            g = _dot(cm, bm, NT)
            xs = xb * dtf
            xsm = xs.astype(MMD)
            base = jnp.exp(csf) * _dot(cm, h.astype(MMD)) + d_ref[:, cols] * xb
            for r in range(HPG):
                sl = slice(r * SSD_P, (r + 1) * SSD_P)
                lm = jnp.exp(jnp.where(mask, csf[:, r * SSD_P:r * SSD_P + 1] - csr[r:r + 1, :], NEG))
                y_ref[:, gi * GW + r * SSD_P:gi * GW + (r + 1) * SSD_P] = (
                    _dot((g * lm).astype(MMD), xsm[:, sl]) + base[:, sl]).astype(y_ref.dtype)
            xd = (xs * jnp.exp(tl - csf)).astype(MMD)
            h_ref[gi] = h * jnp.exp(tl) + _dot(bm, xd, TN)

    return pl.pallas_call(
        body, name=name, grid=(1, nc),
        in_specs=[sp["x"], sp["b"], sp["c"], sp["lanes"], sp["lanes"],
                  pl.BlockSpec((GPS, 128, GW), lambda g, c: (di, 0, 0)), sp["drow"]],
        out_specs=[sp["y"], sp["h"]],
        out_shape=[jax.ShapeDtypeStruct((s, 2048), MMD), jax.ShapeDtypeStruct((SSD_GROUPS, nc, SSD_N, GW), MMD)],
        scratch_shapes=[pltpu.VMEM((GPS, SSD_N, GW), F32)],
        compiler_params=_cp(("arbitrary", "arbitrary")),
    )(xc, xc, xc, dt, cs, ex, drow)


def _ssd_bwd(xc, dt, cs, ex, drow, arow, dy, hprev, di, *, name):
    s = xc.shape[0]
    nc = s // CHUNK
    anti = di == 1
    sp = _ssd_specs(nc, not anti)
    trow = 0 if anti else CHUNK - 1
    e1, e2 = _headsum_mats()

    def body(x_ref, b_ref, c_ref, dt_ref, cs_ref, ex_ref, d_ref, a_ref, dy_ref, hp_ref, tri_ref,
             e1_ref, e2_ref, dx_ref, db_ref, dc_ref, ddt_ref, da_ref, dh_ref, w_ref, dxs_ref):
        @pl.when(pl.program_id(1) == 0)
        def _():
            dh_ref[...] = jnp.zeros_like(dh_ref)
            da_ref[...] = jnp.zeros_like(da_ref)

        e1v = e1_ref[...]
        ii, _, mask = _ssd_mask(anti)
        dtv, csv = dt_ref[...], cs_ref[...]
        cst = csv.T
        ddt_acc = jnp.zeros((CHUNK, 128), F32)
        da_acc = jnp.zeros((1, 128), F32)
        for gi in range(GPS):
            lane0 = SSD_HEADS * di + HPG * gi
            cols = slice(gi * GW, (gi + 1) * GW)
            ncols = slice(gi * SSD_N, (gi + 1) * SSD_N)
            ex = ex_ref[gi]
            xb = x_ref[:, cols].astype(F32)
            bm, cm = b_ref[:, ncols], c_ref[:, ncols]
            csr = cst[lane0:lane0 + HPG]
            dym = dy_ref[:, cols]
            dyb = dym.astype(F32)
            hpm = hp_ref[gi, 0]
            hp = hpm.astype(F32)
            dh = dh_ref[gi]
            dhm = dh.astype(MMD)
            dtf = _expand(dtv, ex, 2)
            csf = _expand(csv, ex)
            tl = csf[trow:trow + 1, :]
            e = jnp.exp(csf)
            dec = jnp.exp(tl - csf)
            et = jnp.exp(tl)
            xs = xb * dtf
            xsm = xs.astype(MMD)
            g = _dot(cm, bm, NT)
            z = _dot(cm, hpm)
            bdh = _dot(bm, dhm)
            dg = jnp.zeros((CHUNK, CHUNK), F32)
            wcols = jnp.zeros((CHUNK, CHUNK), F32)
            for r in range(HPG):
                sl = slice(r * SSD_P, (r + 1) * SSD_P)
                lm = jnp.exp(jnp.where(mask, csf[:, r * SSD_P:r * SSD_P + 1] - csr[r:r + 1, :], NEG))
                mm = g * lm
                dm = _dot(dym[:, sl], xsm[:, sl], NT)
                w = dm * mm
                w_ref[gi, :, r * CHUNK:(r + 1) * CHUNK] = w
                wcols = jnp.where(ii == r, jnp.sum(w, axis=0, keepdims=True), wcols)
                dg = dg + dm * lm
                dxs_ref[gi, :, sl] = _dot(mm.astype(MMD), dym[:, sl], TN)
            dxs = dxs_ref[gi] + dec * bdh
            dx_ref[:, cols] = (dxs * dtf + d_ref[:, cols] * dyb).astype(dx_ref.dtype)
            tb = xs * bdh * dec
            d_tot = jnp.sum(tb, axis=0, keepdims=True) + et * jnp.sum(dh * hp, axis=0, keepdims=True)
            d_tot = _headsum(jnp.broadcast_to(d_tot, (8, GW)), e1v)[0:1]
            dcs = (_headsum(dyb * (e * z) - tb, e1v) + _headsum(w_ref[gi], e2_ref[...]) - wcols.T
                   + jnp.where(ii == trow, d_tot, 0.0))
            da = pltpu.roll(_dot_hi(tri_ref[...], dcs), lane0, 1)
            ddt_acc = ddt_acc + da * a_ref[...] + pltpu.roll(_headsum(dxs * xb, e1v), lane0, 1)
            da_acc = da_acc + jnp.sum(da * dtv, axis=0, keepdims=True)
            dgm = dg.astype(MMD)
            dz = (e * dyb).astype(MMD)
            dc_ref[:, ncols] = (_dot(dgm, bm) + _dot(dz, hpm, NT)).astype(dc_ref.dtype)
            db_ref[:, ncols] = (_dot(dgm, cm, TN) + _dot((xs * dec).astype(MMD), dhm, NT)).astype(db_ref.dtype)
            dh_ref[gi] = dh * et + _dot(cm, dz, TN)
        ddt_ref[...] = ddt_acc
        da_ref[...] += da_acc

    const = lambda shape: pl.BlockSpec(shape, lambda g, c: (0,) * len(shape))
    return pl.pallas_call(
        body, name=name, grid=(1, nc),
        in_specs=[sp["x"], sp["b"], sp["c"], sp["lanes"], sp["lanes"],
                  pl.BlockSpec((GPS, 128, GW), lambda g, c: (di, 0, 0)), sp["drow"],
                  const((1, 128)), sp["y"], sp["h"],
                  const((CHUNK, CHUNK)), const((GW, 128)), const((HPG * CHUNK, 128))],
        out_specs=[sp["y"], sp["n"], sp["n"], sp["lanes"], const((1, 128))],
        out_shape=[jax.ShapeDtypeStruct((s, 2048), MMD), jax.ShapeDtypeStruct((s, SSD_GROUPS * SSD_N), MMD),
                   jax.ShapeDtypeStruct((s, SSD_GROUPS * SSD_N), MMD), jax.ShapeDtypeStruct((s, 128), F32),
                   jax.ShapeDtypeStruct((1, 128), F32)],
        scratch_shapes=[pltpu.VMEM((GPS, SSD_N, GW), F32), pltpu.VMEM((GPS, CHUNK, HPG * CHUNK), F32),
                        pltpu.VMEM((GPS, CHUNK, GW), F32)],
        compiler_params=_cp(("arbitrary", "arbitrary")),
    )(xc, xc, xc, dt, cs, ex, drow, arow, dy, hprev, _tri(anti), e1, e2)


def _gnorm_fwd(ya, yb, proj, w, *, name):
    s = ya.shape[0]
    tm = _tile(s, 256)

    def body(a_ref, b_ref, z_ref, w_ref, o_ref):
        zv = z_ref[...].astype(F32)
        t = (a_ref[...].astype(F32) + b_ref[...].astype(F32)) * (zv * _sigmoid(zv))
        r = lax.rsqrt(jnp.mean(t * t, axis=-1, keepdims=True) + EPS)
        o_ref[...] = ((t * r) * w_ref[...]).astype(o_ref.dtype)

    big = pl.BlockSpec((tm, 2048), lambda i: (i, 0))
    row = pl.BlockSpec((1, 2048), lambda i: (0, 0))
    return pl.pallas_call(
        body, name=name, grid=(s // tm,), in_specs=[big, big, big, row], out_specs=big,
        out_shape=jax.ShapeDtypeStruct((s, 2048), MMD), compiler_params=_cp(("arbitrary",)),
    )(ya, yb, proj, w)


def _gnorm_bwd(dout, ya, yb, proj, xc, w, *, name):
    s = ya.shape[0]
    tm = _tile(s, 256)

    def body(do_ref, a_ref, b_ref, z_ref, x_ref, w_ref, dy_ref, dz_ref, dw_ref, dd_ref):
        zv = z_ref[...].astype(F32)
        sg = _sigmoid(zv)
        sz = zv * sg
        y = a_ref[...].astype(F32) + b_ref[...].astype(F32)
        t = y * sz
        r = lax.rsqrt(jnp.mean(t * t, axis=-1, keepdims=True) + EPS)
        nv = t * r
        dov = do_ref[...].astype(F32)
        _acc_rows(dw_ref, jnp.sum(dov * nv, axis=0, keepdims=True), pl.program_id(0) == 0)
        dn = dov * w_ref[...]
        dt_ = r * (dn - nv * jnp.mean(dn * nv, axis=-1, keepdims=True))
        dy = dt_ * sz
        dy_ref[...] = dy.astype(dy_ref.dtype)
        dz_ref[...] = (dt_ * y * (sg * (1.0 + zv * (1.0 - sg)))).astype(dz_ref.dtype)
        _acc_rows(dd_ref, jnp.sum(dy * x_ref[...].astype(F32), axis=0, keepdims=True), pl.program_id(0) == 0)

    big = pl.BlockSpec((tm, 2048), lambda i: (i, 0))
    row = pl.BlockSpec((1, 2048), lambda i: (0, 0))
    return pl.pallas_call(
        body, name=name, grid=(s // tm,), in_specs=[big, big, big, big, big, row], out_specs=[big, big, row, row],
        out_shape=[jax.ShapeDtypeStruct((s, 2048), MMD), jax.ShapeDtypeStruct((s, 2048), MMD),
                   jax.ShapeDtypeStruct((1, 2048), F32), jax.ShapeDtypeStruct((1, 2048), F32)],
        compiler_params=_cp(("arbitrary",)),
    )(dout, ya, yb, proj, xc, w)


def _heads(a, n):
    return a.reshape(a.shape[0], n, HEAD_DIM).transpose(1, 0, 2)


def _unheads(a):
    return a.transpose(1, 0, 2).reshape(a.shape[1], a.shape[0] * HEAD_DIM)


def _local_step(x, target, mod, wts, small, late_weights=None, late_grads=None, in_grad=None):
    s, d = x.shape
    shift1, scale1, gate1, shift2, scale2, gate2 = [mod[i:i + 1] for i in range(6)]

    h1 = _ln_mod(x, small["norm1_w"], scale1, shift1, name="ln1")
    proj = _mm(h1, wts["w_in_p"], name="in_proj", outs=[MMD], tm=512, tn=2944, b_outer=True)
    dt_raw = _mm(h1, wts["w_dt"], name="dt_proj", outs=[F32], tm=512, tn=128)

    qk_w = jnp.concatenate([jnp.tile(small["q_norm_w"], (1, N_Q_HEADS)), jnp.tile(small["k_norm_w"], (1, N_KV_HEADS))], axis=1)
    qk_sc = jnp.concatenate([jnp.full((1, N_Q_HEADS * HEAD_DIM), HEAD_DIM ** -0.5, F32),
                             jnp.ones((1, N_KV_HEADS * HEAD_DIM), F32)], axis=1)
    qk_sc2 = jnp.concatenate([jnp.full((1, N_Q_HEADS * HEAD_DIM), HEAD_DIM ** -0.5 * LOG2E, F32),
                              jnp.ones((1, N_KV_HEADS * HEAD_DIM), F32)], axis=1)
    tabs = _rope_tables(s)
    qk, qkt = _qk_fwd(proj, qk_w, qk_sc2, tabs, name="qk_fwd")
    qkt = qkt.reshape(N_Q_HEADS + N_KV_HEADS, HEAD_DIM, s)
    k_h = _heads(qk[:, N_Q_HEADS * HEAD_DIM:], N_KV_HEADS)
    v_sd = proj[:, V0:V0 + N_KV_HEADS * HEAD_DIM]
    v_h = _heads(v_sd, N_KV_HEADS)
    vta = jnp.concatenate([v_sd.T.reshape(N_KV_HEADS, HEAD_DIM, s), jnp.ones((N_KV_HEADS, V_AUG - HEAD_DIM, s), MMD)], axis=1)
    ot, lse = _flash_fwd(qkt, vta, name="flash_fwd")
    ot2 = ot.reshape(N_Q_HEADS * HEAD_DIM, s)
    if late_weights is not None:
        wts = {**wts, **late_weights(ot)}

    w8 = jnp.pad(small["conv_w"], ((0, 8 - D_CONV), (0, 0)))
    xc = _conv_fwd(proj, w8, small["conv_b"], name="conv_fwd")
    a_neg = -jnp.exp(small["A_log"])
    arow = jnp.pad(a_neg.reshape(1, 2 * SSD_HEADS), ((0, 0), (0, 128 - 2 * SSD_HEADS)))
    bias_row = jnp.pad(small["dt_bias"].reshape(1, 2 * SSD_HEADS), ((0, 0), (0, 128 - 2 * SSD_HEADS)))
    dt, cs = _dt_fwd(dt_raw, bias_row, arow, name="dt_fwd")
    drow = jnp.repeat(small["ssd_D"], SSD_P, axis=1)
    dirs = [dict(drow=drow), dict(drow=jnp.zeros_like(drow))]
    ex = _expand_mats()
    ys = []
    for di, dd in enumerate(dirs):
        y, dd["hprev"] = _ssd_fwd(xc, dt, cs, ex, dd["drow"], di, name=f"ssd_fwd{di}")
        ys.append(y)
    ssdn = _gnorm_fwd(ys[0], ys[1], proj, small["ssd_norm_w"], name="gnorm_fwd")

    a_o = _mm(ot2, wts["w_attn_out"], name="attn_out", outs=[MMD], ta=True, tm=512, tn=1024)

    def merge_epi(acc, ao, ga, gs):
        return (_sigmoid(ga.astype(F32)) * ao.astype(F32) + _sigmoid(gs.astype(F32)) * acc, acc)

    merged, b_o = _mm(ssdn, wts["w_ssd_out"], name="ssd_out", outs=[MMD, MMD], tm=512, tn=1024,
                      extras=[(a_o, "tile", 0), (proj, "tile", GA0), (proj, "tile", GS0)], epi=merge_epi)

    def res_epi(acc, res, gate):
        return (res + gate * acc, acc)

    x1, mo = _mm(merged, wts["w_o"], name="w_o", outs=[F32, MMD], tm=512, tn=1024,
                 extras=[(x, "tile", 0), (gate1, "row", 0)], epi=res_epi)
    h2 = _ln_mod(x1, small["norm2_w"], scale2, shift2, name="ln2")

    def relu2_epi(acc):
        rl = jnp.maximum(acc, 0.0)
        return (rl * rl, rl)

    act, rl = _mm(h2, wts["w_mlp1"], name="mlp1", outs=[MMD, MMD], tm=1024, tn=1024, epi=relu2_epi, b_outer=True)

    def loss_epi(acc, res, gate, tgt):
        return ((res + gate * acc - tgt) * (1.0 / d), acc)

    dy, ffo = _mm(act, wts["w_mlp2"], name="mlp2", outs=[F32, MMD], tm=512, tn=1024, vmem=VMEM_BIG,
                  extras=[(x1, "tile", 0), (gate2, "row", 0), (target, "tile", 0)], epi=loss_epi)
    loss = _sumsq(dy, name="loss") * (0.5 * d)

    gw = {}
    gs_ = {}
    dffo, dgate2 = _gate_bwd(dy, ffo, gate2, name="gate2_bwd")
    dpre = _mm(dffo, wts["w_mlp2"], name="mlp2_dx", outs=[MMD], nt=True, tm=1024, tn=1024, b_outer=True,
               extras=[(rl, "tile", 0)], epi=lambda acc, r: (acc * (2.0 * r.astype(F32)),))
    gw["w_mlp2"] = _mm_tn(act, dffo, name="mlp2_dw")
    dh2 = _mm(dpre, wts["w_mlp1"], name="mlp1_dx", outs=[F32], nt=True, tm=1024, tn=1024, vmem=VMEM_BIG)
    gw["w_mlp1"] = _mm_tn(h2, dpre, name="mlp1_dw")
    dx1, dshift2, dscale2, gs_["norm2_w"] = _ln_mod_bwd(dh2, x1, small["norm2_w"], scale2, dy, name="ln2_bwd")
    dmo, dgate1 = _gate_bwd(dx1, mo, gate1, name="gate1_bwd")

    def merge_bwd_epi(acc, ao, bo, ga, gs):
        sa, ss = _sigmoid(ga.astype(F32)), _sigmoid(gs.astype(F32))
        return (acc * sa, acc * ss, acc * ao.astype(F32) * sa * (1.0 - sa), acc * bo.astype(F32) * ss * (1.0 - ss))

    da_o, db_o, dga, dgs = _mm(dmo, wts["w_o"], name="w_o_dx", outs=[MMD] * 4, nt=True, tm=512, tn=1024,
                               extras=[(a_o, "tile", 0), (b_o, "tile", 0), (proj, "tile", GA0), (proj, "tile", GS0)],
                               epi=merge_bwd_epi)
    gw["w_o"] = _mm_tn(merged, dmo, name="w_o_dw")
    dot = _mm(wts["w_attn_out"], da_o, name="attn_out_dx", outs=[MMD], nt=True, tm=1024, tn=1024)
    gw["w_attn_out"] = _mm(ot2, da_o, name="attn_out_dw", outs=[F32], tm=256, tn=512, vmem=VMEM_BIG)
    dssdn = _mm(db_o, wts["w_ssd_out"], name="ssd_out_dx", outs=[MMD], nt=True, tm=512, tn=2048)
    gw["w_ssd_out"] = _mm_tn(ssdn, db_o, name="ssd_out_dw")

    norm_w = small["ssd_norm_w"] if late_grads is None else small["ssd_norm_w"] + late_grads(gw)
    dyssd, dz, gs_["ssd_norm_w"], dd_row = _gnorm_bwd(dssdn, ys[0], ys[1], proj, xc, norm_w, name="gnorm_bwd")
    gs_["ssd_D"] = dd_row.reshape(SSD_HEADS, SSD_P).sum(axis=1).reshape(1, SSD_HEADS)
    dxc, ddts, das = [], [], []
    for di, dd in enumerate(dirs):
        dxs, dbm, dcm, ddt_d, da_d = _ssd_bwd(xc, dt, cs, ex, dd["drow"], arow, dyssd, dd["hprev"], di, name=f"ssd_bwd{di}")
        dxc.append((dxs, dbm, dcm))
        ddts.append(ddt_d)
        das.append(da_d)
    conv_parts, col0 = [], 0
    for part, (ga, gb) in enumerate(zip(*dxc)):
        conv_parts.append(_conv_bwd(proj, col0, ga, gb, w8, small["conv_b"], name=f"conv_bwd{part}"))
        col0 += ga.shape[1]
    dxbc, dw8, gs_["conv_b"] = [jnp.concatenate(t, axis=1) for t in zip(*conv_parts)]
    gs_["conv_w"] = dw8[0:D_CONV]
    gs_["A_log"] = (das[0] + das[1])[:, 0:2 * SSD_HEADS].reshape(2, SSD_HEADS) * a_neg
    ddt_raw, dbias = _dt_bwd(ddts[0], ddts[1], dt_raw, bias_row, name="dt_bwd")
    gs_["dt_bias"] = dbias[:, 0:2 * SSD_HEADS].reshape(2, SSD_HEADS)

    dqt, dk_h, dv_h = _flash_bwd(qkt, k_h, v_h, dot.reshape(N_Q_HEADS, HEAD_DIM, s), ot, lse, name="flash_bwd")
    dqk_u, dqk_w = _qk_bwd(dqt.reshape(N_Q_HEADS * HEAD_DIM, s), dk_h.transpose(0, 2, 1).reshape(N_KV_HEADS * HEAD_DIM, s),
                           proj, qk_w, qk_sc, tabs, name="qk_bwd")
    gs_["q_norm_w"] = dqk_w[:, 0:N_Q_HEADS * HEAD_DIM].reshape(N_Q_HEADS, HEAD_DIM).sum(axis=0, keepdims=True)
    gs_["k_norm_w"] = dqk_w[:, N_Q_HEADS * HEAD_DIM:].reshape(N_KV_HEADS, HEAD_DIM).sum(axis=0, keepdims=True)
    dv = _unheads(dv_h).astype(MMD)

    dproj = jnp.concatenate([dz, dga, dgs, dxbc, dqk_u, dv, ddt_raw], axis=1)
    gw["w_in_p"] = _mm_tn(h1, dproj, name="in_proj_dw", tk=512, tn=2944, tmm=2048, vmem=VMEM_BIG)
    zero_row = jnp.zeros((1, d), F32) if in_grad is None else jnp.zeros((1, d), F32) + in_grad(gw["w_in_p"])[0:1, 0:1]
    dh1 = _mm(dproj, wts["w_in_p"], name="in_proj_dx", outs=[F32], nt=True, tm=256, tn=1024, vmem=VMEM_BIG,
              extras=[(zero_row, "row", 0)], epi=lambda acc, r: (acc + r,))
    grad_x, dshift1, dscale1, gs_["norm1_w"] = _ln_mod_bwd(dh1, x, small["norm1_w"], scale1, dx1, name="ln1_bwd")
    dmod = jnp.concatenate([dshift1, dscale1, dgate1, dshift2, dscale2, dgate2], axis=0)
    return loss, grad_x, dmod, gw, gs_


N_DEV = 8
N_CHIP = 4
ANY = pl.BlockSpec(memory_space=pl.ANY)


def _place():
    return lax.axis_index("x"), lax.axis_index("y"), lax.axis_index("c")


def _allgather8(v, *, name):
    m_per, n = v.shape

    def body(x_ref, out_ref, send_sems, recv_sems, local_sem):
        x, y, c = _place()
        me, sibling = (x, y, c), (x, y, 1 - c)
        chips = [(1 - x, y), (x, 1 - y), (1 - x, 1 - y)]

        def rows(px, py, pc):
            return out_ref.at[pl.ds((4 * px + 2 * py + pc) * m_per, m_per), :]

        def copy(k, block, to, src=None):
            return pltpu.make_async_remote_copy(
                src_ref=rows(*block) if src is None else src, dst_ref=rows(*block),
                send_sem=send_sems.at[k], recv_sem=recv_sems.at[k], device_id=to, device_id_type=MESH)

        mine = pltpu.make_async_copy(x_ref, rows(*me), local_sem)
        mine.start()
        first = [copy(0, me, sibling, src=x_ref)]
        first += [copy(1 + j, me, (*chip, c), src=x_ref) for j, chip in enumerate(chips)]
        for cp in first:
            cp.start()
        passed = [copy(4 + j, (*chip, c), sibling) for j, chip in enumerate(chips)]
        for j, chip in enumerate(chips):
            copy(1 + j, (*chip, c), me).wait_recv()
            passed[j].start()
        copy(0, sibling, me).wait_recv()
        for j, chip in enumerate(chips):
            copy(4 + j, (*chip, 1 - c), me).wait_recv()
        for cp in first + passed:
            cp.wait_send()
        mine.wait()

    return pl.pallas_call(
        body, name=name, out_shape=jax.ShapeDtypeStruct((N_DEV * m_per, n), v.dtype),
        in_specs=[pl.BlockSpec(memory_space=pltpu.VMEM)], out_specs=pl.BlockSpec(memory_space=pltpu.VMEM),
        scratch_shapes=[pltpu.SemaphoreType.DMA((7,)), pltpu.SemaphoreType.DMA((7,)), pltpu.SemaphoreType.DMA],
    )(v)


HBM = pl.BlockSpec(memory_space=pltpu.HBM)
SEM = pl.BlockSpec(memory_space=pltpu.SEMAPHORE)


def _chips_copies(x_ref, land_ref, sems, scatter):
    x, y, c = _place()
    k = 2 * x + y
    chips = [(1 - x, y), (x, 1 - y), (1 - x, 1 - y)]
    ids = [2 * cx + cy for cx, cy in chips]

    def copy(j, slot):
        return pltpu.make_async_remote_copy(
            src_ref=x_ref.at[ids[j]] if scatter else x_ref, dst_ref=land_ref.at[slot], send_sem=sems[j],
            recv_sem=sems[3 + j], device_id=(*chips[j], c), device_id_type=MESH)

    return [copy(j, k) for j in range(3)], [copy(j, ids[j]) for j in range(3)]


def _chips_start(src, scatter, *, name):
    shape = src.shape if scatter else (N_CHIP,) + tuple(src.shape)

    def body(x_ref, land_ref, *rest):
        sems, token = rest[0:6], rest[8]
        for cp in _chips_copies(x_ref, land_ref, sems, scatter)[0]:
            cp.start()
        token[...] = jnp.zeros_like(token)

    out = pl.pallas_call(
        body, name=name,
        out_shape=(pltpu.SemaphoreType.DMA(()),) * 6 + (pltpu.HBM(src.shape, src.dtype), pltpu.HBM(shape, src.dtype),
                                                       jax.ShapeDtypeStruct((8, 128), F32)),
        in_specs=(HBM, HBM), out_specs=(SEM,) * 6 + (HBM, HBM, pl.BlockSpec(memory_space=pltpu.VMEM)),
        input_output_aliases={0: 6, 1: 7},
        compiler_params=pltpu.CompilerParams(has_side_effects=pltpu.SideEffectType.DATAFLOW_SIDE_EFFECTING),
    )(pltpu.with_memory_space_constraint(src, pltpu.HBM),
      pltpu.with_memory_space_constraint(lax.empty(shape, src.dtype), pltpu.HBM))
    return out[0:6], out[6], out[7], out[8]


def _chips_wait(sems, src, land, after, scatter, *, name):
    def body(x_ref, land_ref, *rest):
        sems_ = rest[0:6]
        for cp in _chips_copies(x_ref, land_ref, sems_, scatter)[1]:
            cp.wait_send()
            cp.wait_recv()

    return pl.pallas_call(
        body, name=name, out_shape=(pltpu.HBM(src.shape, src.dtype), pltpu.HBM(land.shape, land.dtype)),
        in_specs=(HBM, HBM) + (SEM,) * 6 + (ANY,), out_specs=(HBM, HBM), input_output_aliases={0: 0, 1: 1},
        compiler_params=pltpu.CompilerParams(has_side_effects=pltpu.SideEffectType.DATAFLOW_SIDE_EFFECTING),
    )(src, land, *sems, after)


def _row_tile(r, pref=512):
    return max(t for t in range(16, pref + 1, 16) if r % t == 0)


def _gather_weights(src, *, name):
    r = src.shape[0]
    hr = r // 2
    assert r == 2 * hr and hr % 16 == 0

    def body(x_ref, out_ref, send_sems, recv_sems):
        x, y, c = _place()
        k = 2 * x + y
        chips = [(1 - x, y), (x, 1 - y), (1 - x, 1 - y)]
        ids = [2 * cx + cy for cx, cy in chips]
        mine_rows = pl.ds(pl.multiple_of(c * hr, 16), hr)
        other_rows = pl.ds(pl.multiple_of((1 - c) * hr, 16), hr)

        def copy(sem, src_ref, slot, rows, to):
            return pltpu.make_async_remote_copy(
                src_ref=src_ref, dst_ref=out_ref.at[slot, rows], send_sem=send_sems.at[sem], recv_sem=recv_sems.at[sem],
                device_id=to, device_id_type=MESH)

        sends = [copy(j, x_ref.at[mine_rows], k, mine_rows, (cx, cy, c)) for j, (cx, cy) in enumerate(chips)]
        for cp in sends:
            cp.start()
        passed = [copy(3 + j, out_ref.at[ids[j], mine_rows], ids[j], mine_rows, (x, y, 1 - c)) for j in range(3)]
        for j, (cx, cy) in enumerate(chips):
            copy(j, x_ref.at[mine_rows], ids[j], mine_rows, (cx, cy, c)).wait_recv()
            passed[j].start()
        for j in range(3):
            copy(3 + j, out_ref.at[ids[j], other_rows], ids[j], other_rows, (x, y, 1 - c)).wait_recv()
        for cp in sends + passed:
            cp.wait_send()

    return pl.pallas_call(
        body, name=name, out_shape=jax.ShapeDtypeStruct((N_CHIP,) + tuple(src.shape), src.dtype),
        in_specs=[ANY], out_specs=ANY,
        scratch_shapes=[pltpu.SemaphoreType.DMA((6,)), pltpu.SemaphoreType.DMA((6,))],
    )(src)


def _pair_swap(a, *, name):
    n, r, cols = a.shape
    hr = r // 2

    def body(x_ref, out_ref, send_sem, recv_sem):
        x, y, c = _place()
        other_rows = pl.ds(pl.multiple_of((1 - c) * hr, 16), hr)
        cp = pltpu.make_async_remote_copy(src_ref=x_ref.at[:, other_rows], dst_ref=out_ref, send_sem=send_sem,
                                          recv_sem=recv_sem, device_id=(x, y, 1 - c), device_id_type=MESH)
        cp.start()
        cp.wait()

    return pl.pallas_call(
        body, name=name, out_shape=jax.ShapeDtypeStruct((n, hr, cols), a.dtype), in_specs=[ANY], out_specs=ANY,
        scratch_shapes=[pltpu.SemaphoreType.DMA, pltpu.SemaphoreType.DMA],
    )(a)


def _sibling_copy(a, *, name):
    def body(x_ref, out_ref, send_sem, recv_sem):
        x, y, c = _place()
        cp = pltpu.make_async_remote_copy(src_ref=x_ref, dst_ref=out_ref, send_sem=send_sem, recv_sem=recv_sem,
                                          device_id=(x, y, 1 - c), device_id_type=MESH)
        cp.start()
        cp.wait()

    return pl.pallas_call(
        body, name=name, out_shape=jax.ShapeDtypeStruct(a.shape, a.dtype), in_specs=[ANY], out_specs=ANY,
        scratch_shapes=[pltpu.SemaphoreType.DMA, pltpu.SemaphoreType.DMA],
    )(a)


def _sum_slots(a, own, *, name):
    _, r, c = a.shape
    tr = _row_tile(r, 256)

    def body(a_ref, own_ref, o_ref):
        k = 2 * lax.axis_index("x") + lax.axis_index("y")
        acc = None
        for j in range(N_CHIP):
            term = jnp.where(k == j, own_ref[j], a_ref[j]).astype(F32)
            acc = term if acc is None else acc + term
        o_ref[...] = acc

    spec = pl.BlockSpec((N_CHIP, tr, c), lambda i: (0, i, 0))
    return pl.pallas_call(
        body, name=name, grid=(r // tr,), in_specs=[spec, spec],
        out_specs=pl.BlockSpec((tr, c), lambda i: (i, 0)), out_shape=jax.ShapeDtypeStruct((r, c), F32),
        compiler_params=_cp(("arbitrary",)),
    )(a, own)


def _add2(a, b, *, name):
    r, c = a.shape
    tr = _row_tile(r)

    def body(a_ref, b_ref, o_ref):
        o_ref[...] = (a_ref[...].astype(F32) + b_ref[...].astype(F32)).astype(o_ref.dtype)

    spec = pl.BlockSpec((tr, c), lambda i: (i, 0))
    return pl.pallas_call(
        body, name=name, grid=(r // tr,), in_specs=[spec, spec], out_specs=spec,
        out_shape=jax.ShapeDtypeStruct((r, c), a.dtype), compiler_params=_cp(("arbitrary",)),
    )(a, b)


BIG = ("w_in", "w_mlp1", "w_attn_out", "w_ssd_out", "w_o", "w_mlp2")
COL_SHARDED = ("w_mlp1", "w_in")
ROW_SHARDED = ("w_attn_out", "w_ssd_out", "w_o", "w_mlp2")
LATE = ROW_SHARDED + ("w_mlp1",)
SMALL = ("b_ada", "norm1_w", "norm2_w", "q_norm_w", "k_norm_w", "conv_b", "A_log", "dt_bias", "ssd_D", "ssd_norm_w")
NAMES = ("w_ada", "b_ada", "norm1_w", "norm2_w", "w_in", "q_norm_w", "k_norm_w", "conv_w", "conv_b", "A_log", "dt_bias",
         "ssd_D", "ssd_norm_w", "w_attn_out", "w_ssd_out", "w_o", "w_mlp1", "w_mlp2")
W_IN_COLS = 8768


def _permute_in(w):
    return jnp.concatenate([w[:, 4608:6656], w[:, 6720:8768], w[:, 1536:4608], w[:, 0:1536], w[:, 6656:6720],
                            jnp.zeros((w.shape[0], PW - W_IN_COLS), w.dtype)], axis=1)


def _unpermute_in(wp):
    return jnp.concatenate([wp[:, Q0:DT0], wp[:, XS0:Q0], wp[:, Z0:GA0], wp[:, DT0:DT0 + 64], wp[:, GA0:XS0]], axis=1)


def _pad_to(v, n):
    return jnp.pad(v, (0, n - v.shape[0]))


def _step(w, m, v, loss_target):
    xi, yi, ci = _place()
    chip = 2 * xi + yi
    dev = 4 * xi + 2 * yi + ci
    x, tgt = w["x"], loss_target
    d = x.shape[1]

    cw = w["conv_w"].shape[1]
    v0 = _pad_to(jnp.concatenate([w["c"].reshape(-1), w["conv_w"].reshape(-1)]), 5120).reshape(8, 640)
    g0 = _allgather8(v0, name="ag_cond").reshape(N_DEV, 5120)
    c_all = g0[:, 0:d]
    conv_w = jnp.concatenate([g0[2 * k, d:d + D_CONV * cw].reshape(D_CONV, cw) for k in range(N_CHIP)], axis=1)
    sc = _silu_cast(c_all, name="silu_c")
    modp = _mm(sc, w["w_ada"].astype(MMD), name="ada_fwd", outs=[F32], tm=8, tn=512)
    g1 = _allgather8(modp, name="ag_mod").reshape(N_DEV, N_DEV, modp.shape[1])
    mod_all = jnp.concatenate([g1[2 * k] for k in range(N_CHIP)], axis=1)
    mod = (lax.dynamic_slice_in_dim(mod_all, dev, 1, axis=0) + w["b_ada"]).reshape(6, d)

    mine, mod = lax.optimization_barrier((w["w_in"].astype(MMD), mod))
    gath = _gather_weights(mine, name="ag_w_in")
    late_mine = jnp.concatenate([w[n].astype(MMD) for n in LATE], axis=0)
    late_mine, gath = lax.optimization_barrier((late_mine, gath))
    ag_sems, ag_src, ag_land, ag_token = _chips_start(late_mine, False, name="ag_late_start")
    mod = mod + ag_token[0:1, 0:1]
    w_in = jnp.concatenate([jnp.where(chip == k, mine, gath[k]) for k in range(N_CHIP)], axis=1)
    wts = {"w_in_p": _permute_in(w_in), "w_dt": jnp.pad(w_in[:, 6656:6720], ((0, 0), (0, 64)))}
    small = {n: w[n] for n in SMALL if n != "b_ada"}
    small["conv_w"] = conv_w

    def late_weights(after):
        src, land = _chips_wait(ag_sems, ag_src, ag_land, after, False, name="ag_late_wait")
        out, o = {}, 0
        for n in LATE:
            rows = w[n].shape[0]
            parts = [jnp.where(chip == k, src[o:o + rows], land[k, o:o + rows]) for k in range(N_CHIP)]
            out[n] = jnp.concatenate(parts, axis=1 if n in COL_SHARDED else 0)
            o += rows
        return out

    def pair_sums(slots, tag):
        _, rows, cols = slots.shape
        hr = rows // 2
        theirs = _pair_swap(slots, name="rs_pair_" + tag)
        ours = lax.dynamic_slice_in_dim(slots, ci * hr, hr, axis=1)
        pair = _add2(ours.reshape(N_CHIP * hr, cols), theirs.reshape(N_CHIP * hr, cols), name="rs_pair_sum_" + tag)
        return pair.reshape(N_CHIP, hr, cols)

    def finish(recv, pair, tag):
        half = _sum_slots(recv, pair, name="rs_sum_" + tag)
        other = _sibling_copy(half, name="rs_sibling_" + tag)
        return jnp.where(ci == 0, jnp.concatenate([half, other], axis=0), jnp.concatenate([other, half], axis=0))

    started = {}

    def late_grads(gw):
        slots = []
        for k in range(N_CHIP):
            parts = []
            for n in LATE:
                rows = w[n].shape[0]
                blk = gw[n][:, k * rows:(k + 1) * rows] if n in COL_SHARDED else gw[n][k * rows:(k + 1) * rows]
                parts.append(blk.astype(MMD))
            slots.append(jnp.concatenate(parts, axis=0))
        pair = pair_sums(jnp.stack(slots), "late")
        sems, src, land, token = _chips_start(pair, True, name="rs_late_start")
        started["late"] = (sems, src, land)
        return token[0:1, 0:1]

    def in_grad(g):
        g_in = _unpermute_in(g)
        cols_in = w["w_in"].shape[1]
        pair = pair_sums(jnp.stack([g_in[:, k * cols_in:(k + 1) * cols_in].astype(MMD) for k in range(N_CHIP)]), "w_in")
        sems, src, land, token = _chips_start(pair, True, name="rs_w_in_start")
        started["w_in"] = (sems, src, land)
        return token

    loss, grad_x, dmod, gw, gs = _local_step(x, tgt, mod, wts, small, late_weights, late_grads, in_grad)

    grads = {}
    pair, land = _chips_wait(*started["w_in"], grad_x, True, name="rs_w_in_wait")
    grads["w_in"] = finish(land, pair, "w_in")
    pair, land = _chips_wait(*started["late"], grad_x, True, name="rs_late_wait")
    total, o = finish(land, pair, "late"), 0
    for n in LATE:
        rows = w[n].shape[0]
        grads[n] = total[o:o + rows]
        o += rows

    order = ([dmod.reshape(-1)] + [gs[n].reshape(-1) for n in SMALL if n != "b_ada"] + [gs["conv_w"].reshape(-1)]
             + [loss.reshape(-1)])
    vec = jnp.concatenate(order)
    n_small = vec.shape[0]
    n_pad = -(-n_small // 1024) * 1024
    g2 = _allgather8(_pad_to(vec, n_pad).reshape(8, n_pad // 8), name="ag_small")
    tot = _rows_sum(g2, N_DEV, name="small_sum").reshape(-1)
    loss = tot[n_small - 1]
    dmod_all = g2.reshape(N_DEV, n_pad)[:, 0:6 * d]
    off = 0
    for n in SMALL:
        grads[n] = tot[off:off + w[n].size].reshape(w[n].shape)
        off += w[n].size
    conv_full = tot[off:off + D_CONV * N_CHIP * cw].reshape(D_CONV, N_CHIP * cw)
    grads["conv_w"] = lax.dynamic_slice_in_dim(conv_full, chip * cw, cw, axis=1)
    ada_cols = w["w_ada"].shape[1]
    dmod_mine = lax.dynamic_slice_in_dim(dmod_all, chip * ada_cols, ada_cols, axis=1).astype(MMD)
    grads["w_ada"] = _mm_tn(sc, dmod_mine, name="ada_dw", tk=512, tn=512, tmm=8)

    delta, new_m, new_v = {}, {}, {}
    pack = lambda t: jnp.concatenate([t[n].reshape(-1) for n in SMALL]).reshape(1, -1)
    ds_, ms_, vs_ = _adamw(pack(w), pack(grads), pack(m), pack(v), name="adamw_small")
    off = 0
    for n in SMALL:
        for dst, src in ((delta, ds_), (new_m, ms_), (new_v, vs_)):
            dst[n] = src[0, off:off + w[n].size].reshape(w[n].shape)
        off += w[n].size
    for n in ("w_ada", "conv_w") + BIG:
        delta[n], new_m[n], new_v[n] = _adamw(w[n], grads[n], m[n], v[n], name="adamw_" + n)
    return loss, grad_x, grads, delta, new_m, new_v


def kernel(x, c, w_ada, b_ada, norm1_w, norm2_w, w_in, q_norm_w, k_norm_w, conv_w, conv_b, A_log, dt_bias, ssd_D, ssd_norm_w, w_attn_out, w_ssd_out, w_o, w_mlp1, w_mlp2, loss_target, m_w_ada, m_b_ada, m_norm1_w, m_norm2_w, m_w_in, m_q_norm_w, m_k_norm_w, m_conv_w, m_conv_b, m_A_log, m_dt_bias, m_ssd_D, m_ssd_norm_w, m_w_attn_out, m_w_ssd_out, m_w_o, m_w_mlp1, m_w_mlp2, v_w_ada, v_b_ada, v_norm1_w, v_norm2_w, v_w_in, v_q_norm_w, v_k_norm_w, v_conv_w, v_conv_b, v_A_log, v_dt_bias, v_ssd_D, v_ssd_norm_w, v_w_attn_out, v_w_ssd_out, v_w_o, v_w_mlp1, v_w_mlp2):
    args = dict(locals())
    strip = lambda a: a[0] if a.ndim == 3 else a
    w = {n: strip(args[n]) for n in NAMES + ("x", "c")}
    m = {n: strip(args["m_" + n]) for n in NAMES}
    v = {n: strip(args["v_" + n]) for n in NAMES}
    loss, grad_x, grads, delta, new_m, new_v = _step(w, m, v, loss_target[0])
    like = lambda t, n: t.reshape(args[n].shape)
    return (loss, grad_x[None], *[like(grads[n], n) for n in NAMES], *[like(delta[n], n) for n in NAMES],
            *[like(new_m[n], n) for n in NAMES], *[like(new_v[n], n) for n in NAMES])
```

```python
import math

import jax
import jax.numpy as jnp
from jax import lax
from jax.experimental import pallas as pl
from jax.experimental.pallas import tpu as pltpu

F32 = jnp.float32
MMD = jnp.bfloat16
EPS = 1e-6
NEG = -1e30
MIB = 1024 * 1024
VMEM_BIG = 56 * MIB
VMEM_MID = 40 * MIB

GRID_W = 64
N_Q_HEADS, N_KV_HEADS, HEAD_DIM = 16, 4, 64
ROPE_THETA = 10000.0
SSD_HEADS, SSD_GROUPS, SSD_P, SSD_N, CHUNK = 32, 4, 64, 128, 128
HPG = SSD_HEADS // SSD_GROUPS
D_CONV = 5
ADAM_LR, ADAM_B1, ADAM_B2, ADAM_EPS, ADAM_WD, ADAM_STEP = 0.001, 0.9, 0.999, 1e-08, 0.01, 10

Z0, GA0, GS0, XS0, B0, C0, Q0, K0, V0, DT0, PW = 0, 2048, 3072, 4096, 6144, 6656, 7168, 8192, 8448, 8704, 8832

MESH = pl.DeviceIdType.MESH
NT = (((1,), (1,)), ((), ()))
TN = (((0,), (0,)), ((), ()))


def _cp(sem=None, vmem=VMEM_MID):
    return pltpu.CompilerParams(dimension_semantics=sem, vmem_limit_bytes=vmem)


def _tile(n, pref):
    t = min(n, pref)
    while n % t:
        t //= 2
    return t


def _dot(a, b, dims=None):
    if dims is None:
        return jnp.dot(a, b, preferred_element_type=F32)
    return lax.dot_general(a, b, dims, preferred_element_type=F32)


def _dot_hi(a01, b):
    a = a01.astype(jnp.bfloat16)
    h1 = b.astype(jnp.bfloat16)
    r1 = b - h1.astype(F32)
    h2 = r1.astype(jnp.bfloat16)
    return _dot(a, h1) + _dot(a, h2) + _dot(a, (r1 - h2.astype(F32)).astype(jnp.bfloat16))


def _sigmoid(x):
    return jax.nn.sigmoid(x)


def _mm(a, b, *, name, outs, nt=False, ta=False, extras=(), epi=None, tm=512, tn=512, n=None, b_outer=False,
        vmem=VMEM_MID):
    assert not (nt and ta)
    k, m = a.shape if ta else a.shape[::-1]
    if n is None:
        n = b.shape[0] if nt else b.shape[1]
    tm, tn = _tile(m, tm), _tile(n, tn)
    gi, gj = m // tm, n // tn
    if b_outer:
        grid = (gj, gi)
        ij = lambda p, q: (q, p)
    else:
        grid = (gi, gj)
        ij = lambda p, q: (p, q)
    if ta:
        a_spec = pl.BlockSpec((k, tm), lambda p, q: (0, ij(p, q)[0]))
    else:
        a_spec = pl.BlockSpec((tm, k), lambda p, q: (ij(p, q)[0], 0))
    if nt:
        b_spec = pl.BlockSpec((tn, k), lambda p, q: (ij(p, q)[1], 0))
    else:
        b_spec = pl.BlockSpec((k, tn), lambda p, q: (0, ij(p, q)[1]))
    e_specs = []
    for arr, kind, off in extras:
        ob = off // tn
        assert off % tn == 0
        if kind == "tile":
            e_specs.append(pl.BlockSpec((tm, tn), lambda p, q, ob=ob: (ij(p, q)[0], ob + ij(p, q)[1])))
        else:
            e_specs.append(pl.BlockSpec((1, tn), lambda p, q, ob=ob: (0, ob + ij(p, q)[1])))
    ne = len(extras)

    def body(a_ref, b_ref, *rest):
        acc = _dot(a_ref[...], b_ref[...], NT if nt else (TN if ta else None))
        res = epi(acc, *[e[...] for e in rest[:ne]]) if epi is not None else (acc,)
        for o_ref, r in zip(rest[ne:], res):
            o_ref[...] = r.astype(o_ref.dtype)

    out = pl.pallas_call(
        body, name=name, grid=grid,
        in_specs=[a_spec, b_spec] + e_specs,
        out_specs=[pl.BlockSpec((tm, tn), lambda p, q: ij(p, q)) for _ in outs],
        out_shape=[jax.ShapeDtypeStruct((m, n), dt) for dt in outs],
        compiler_params=_cp(("arbitrary", "arbitrary"), vmem),
    )(a, b, *[e[0] for e in extras])
    return out if len(outs) > 1 else out[0]


def _mm_tn(a, g, *, name, tk=512, tn=1024, tmm=4096, vmem=VMEM_MID):
    m, k = a.shape
    n = g.shape[1]
    tk, tn, tmm = _tile(k, tk), _tile(n, tn), _tile(m, tmm)

    def body(a_ref, g_ref, o_ref):
        p = _dot(a_ref[...], g_ref[...], TN)

        @pl.when(pl.program_id(2) == 0)
        def _():
            o_ref[...] = p

        @pl.when(pl.program_id(2) > 0)
        def _():
            o_ref[...] += p

    return pl.pallas_call(
        body, name=name, grid=(k // tk, n // tn, m // tmm),
        in_specs=[pl.BlockSpec((tmm, tk), lambda i, j, r: (r, i)), pl.BlockSpec((tmm, tn), lambda i, j, r: (r, j))],
        out_specs=pl.BlockSpec((tk, tn), lambda i, j, r: (i, j)),
        out_shape=jax.ShapeDtypeStruct((k, n), F32),
        compiler_params=_cp(("arbitrary", "arbitrary", "arbitrary"), vmem),
    )(a, g)


def _adamw(w, g, m, v, *, name):
    r, c = w.shape
    tr = _tile(r, 256) if r % 8 == 0 else r

    def body(w_ref, g_ref, m_ref, v_ref, d_ref, nm_ref, nv_ref):
        gg = g_ref[...]
        nm = ADAM_B1 * m_ref[...] + (1.0 - ADAM_B1) * gg
        nv = ADAM_B2 * v_ref[...] + (1.0 - ADAM_B2) * jnp.square(gg)
        m_hat = nm / (1.0 - ADAM_B1 ** ADAM_STEP)
        v_hat = nv / (1.0 - ADAM_B2 ** ADAM_STEP)
        d_ref[...] = -ADAM_LR * (m_hat / (jnp.sqrt(v_hat) + ADAM_EPS) + ADAM_WD * w_ref[...])
        nm_ref[...] = nm
        nv_ref[...] = nv

    spec = pl.BlockSpec((tr, c), lambda i: (i, 0))
    return pl.pallas_call(
        body, name=name, grid=(r // tr,), in_specs=[spec] * 4, out_specs=[spec] * 3,
        out_shape=[jax.ShapeDtypeStruct((r, c), F32)] * 3, compiler_params=_cp(("arbitrary",)),
    )(w, g, m, v)


def _rows_sum(a, groups, *, name):
    r = a.shape[0] // groups

    def body(a_ref, o_ref):
        acc = a_ref[0:r, :]
        for d in range(1, groups):
            acc = acc + a_ref[d * r:(d + 1) * r, :]
        o_ref[...] = acc

    return pl.pallas_call(body, name=name, out_shape=jax.ShapeDtypeStruct((r, a.shape[1]), F32))(a)


def _silu_cast(a, *, name):
    def body(a_ref, o_ref):
        x = a_ref[...]
        o_ref[...] = (x * _sigmoid(x)).astype(o_ref.dtype)

    return pl.pallas_call(body, name=name, out_shape=jax.ShapeDtypeStruct(a.shape, MMD))(a)


def _sumsq(a, *, name):
    m, n = a.shape
    tm = _tile(m, 512)

    def body(a_ref, o_ref):
        x = a_ref[...]
        p = jnp.sum(jnp.sum(x * x, axis=1, keepdims=True), axis=0, keepdims=True)

        @pl.when(pl.program_id(0) == 0)
        def _():
            o_ref[...] = p

        @pl.when(pl.program_id(0) > 0)
        def _():
            o_ref[...] += p

    return pl.pallas_call(
        body, name=name, grid=(m // tm,), in_specs=[pl.BlockSpec((tm, n), lambda i: (i, 0))],
        out_specs=pl.BlockSpec((1, 1), lambda i: (0, 0)), out_shape=jax.ShapeDtypeStruct((1, 1), F32),
        compiler_params=_cp(("arbitrary",)),
    )(a)


def _acc_rows(o_ref, p, first):
    @pl.when(first)
    def _():
        o_ref[...] = p

    @pl.when(jnp.logical_not(first))
    def _():
        o_ref[...] += p


def _ln_mod(x, w, scale, shift, *, name):
    s, d = x.shape
    tm = _tile(s, 512)

    def body(x_ref, w_ref, sc_ref, sh_ref, o_ref):
        xv = x_ref[...]
        r = lax.rsqrt(jnp.mean(xv * xv, axis=-1, keepdims=True) + EPS)
        o_ref[...] = ((xv * r) * w_ref[...] * (1.0 + sc_ref[...]) + sh_ref[...]).astype(o_ref.dtype)

    row = pl.BlockSpec((1, d), lambda i: (0, 0))
    big = pl.BlockSpec((tm, d), lambda i: (i, 0))
    return pl.pallas_call(
        body, name=name, grid=(s // tm,), in_specs=[big, row, row, row], out_specs=big,
        out_shape=jax.ShapeDtypeStruct((s, d), MMD), compiler_params=_cp(("arbitrary",)),
    )(x, w, scale, shift)


def _ln_mod_bwd(dh, x, w, scale, dres, *, name):
    s, d = x.shape
    tm = _tile(s, 512)

    def body(dh_ref, x_ref, w_ref, sc_ref, dres_ref, dx_ref, dsh_ref, dsc_ref, dw_ref):
        xv = x_ref[...]
        dhv = dh_ref[...].astype(F32)
        r = lax.rsqrt(jnp.mean(xv * xv, axis=-1, keepdims=True) + EPS)
        nv = xv * r
        wv = w_ref[...]
        g1 = 1.0 + sc_ref[...]
        dn = dhv * (wv * g1)
        dx_ref[...] = dres_ref[...] + r * (dn - nv * jnp.mean(dn * nv, axis=-1, keepdims=True))
        first = pl.program_id(0) == 0
        _acc_rows(dsh_ref, jnp.sum(dhv, axis=0, keepdims=True), first)
        _acc_rows(dsc_ref, jnp.sum(dhv * nv * wv, axis=0, keepdims=True), first)
        _acc_rows(dw_ref, jnp.sum(dhv * nv * g1, axis=0, keepdims=True), first)

    row = pl.BlockSpec((1, d), lambda i: (0, 0))
    big = pl.BlockSpec((tm, d), lambda i: (i, 0))
    return pl.pallas_call(
        body, name=name, grid=(s // tm,), in_specs=[big, big, row, row, big], out_specs=[big, row, row, row],
        out_shape=[jax.ShapeDtypeStruct((s, d), F32)] + [jax.ShapeDtypeStruct((1, d), F32)] * 3,
        compiler_params=_cp(("arbitrary",)),
    )(dh, x, w, scale, dres)


def _gate_bwd(dy, u, gate, *, name):
    s, d = dy.shape
    tm = _tile(s, 512)

    def body(dy_ref, u_ref, g_ref, du_ref, dg_ref):
        dyv = dy_ref[...]
        du_ref[...] = (dyv * g_ref[...]).astype(du_ref.dtype)
        _acc_rows(dg_ref, jnp.sum(dyv * u_ref[...].astype(F32), axis=0, keepdims=True), pl.program_id(0) == 0)

    row = pl.BlockSpec((1, d), lambda i: (0, 0))
    big = pl.BlockSpec((tm, d), lambda i: (i, 0))
    return pl.pallas_call(
        body, name=name, grid=(s // tm,), in_specs=[big, big, row], out_specs=[big, row],
        out_shape=[jax.ShapeDtypeStruct((s, d), MMD), jax.ShapeDtypeStruct((1, d), F32)],
        compiler_params=_cp(("arbitrary",)),
    )(dy, u, gate)


def _seg64(v, e):
    hi = v.astype(jnp.bfloat16)
    lo = (v - hi.astype(F32)).astype(jnp.bfloat16)
    return _dot(hi, e) + _dot(lo, e)


def _rope_tables(s):
    rows = s // GRID_W
    pos_row = jnp.repeat(jnp.arange(rows, dtype=jnp.int32), GRID_W).astype(F32)
    pos_col = jnp.tile(jnp.arange(GRID_W, dtype=jnp.int32), rows).astype(F32)
    axis_dim = HEAD_DIM // 2
    inv_freq = ROPE_THETA ** (-jnp.arange(0, axis_dim, 2, dtype=F32) / axis_dim)
    ang_r = pos_row[:, None] * inv_freq[None, :]
    ang_c = pos_col[:, None] * inv_freq[None, :]
    zero = jnp.zeros_like(ang_r)
    cos = jnp.concatenate([jnp.cos(ang_r), jnp.cos(ang_r), jnp.cos(ang_c), jnp.cos(ang_c)], axis=1)
    s_a = jnp.concatenate([-jnp.sin(ang_r), zero, -jnp.sin(ang_c), zero], axis=1)
    s_b = jnp.concatenate([zero, jnp.sin(ang_r), zero, jnp.sin(ang_c)], axis=1)
    return [jnp.tile(t, (1, 2)) for t in (cos, s_a, s_b)]


def _e128():
    i = jnp.arange(128)
    return (i[:, None] // 64 == i[None, :] // 64).astype(jnp.bfloat16)


QKW = N_Q_HEADS * HEAD_DIM + N_KV_HEADS * HEAD_DIM


def _qk_fwd(proj, wrow, scrow, tabs, *, name):
    s = proj.shape[0]
    tm = _tile(s, 1024)

    def body(x_ref, w_ref, sc_ref, cos_ref, sa_ref, sb_ref, e_ref, o_ref, ot_ref):
        u = x_ref[...].astype(F32)
        r = lax.rsqrt(_seg64(u * u, e_ref[...]) * (1.0 / HEAD_DIM) + EPS)
        nv = (u * r) * w_ref[...]
        ro = nv * cos_ref[...] + pltpu.roll(nv, 112, 1) * sa_ref[...] + pltpu.roll(nv, 16, 1) * sb_ref[...]
        out = ro * sc_ref[...]
        o_ref[...] = out.astype(o_ref.dtype)
        ot_ref[...] = out.T.astype(ot_ref.dtype)

    tab = pl.BlockSpec((tm, 128), lambda i, j: (i, 0))
    row = pl.BlockSpec((1, 128), lambda i, j: (0, j))
    return pl.pallas_call(
        body, name=name, grid=(s // tm, QKW // 128),
        in_specs=[pl.BlockSpec((tm, 128), lambda i, j: (i, Q0 // 128 + j)), row, row, tab, tab, tab,
                  pl.BlockSpec((128, 128), lambda i, j: (0, 0))],
        out_specs=[pl.BlockSpec((tm, 128), lambda i, j: (i, j)), pl.BlockSpec((128, tm), lambda i, j: (j, i))],
        out_shape=[jax.ShapeDtypeStruct((s, QKW), MMD), jax.ShapeDtypeStruct((QKW, s), MMD)],
        compiler_params=_cp(("arbitrary", "arbitrary")),
    )(proj, wrow, scrow, *tabs, _e128())


def _qk_bwd(dqt, dkt, proj, wrow, scrow, tabs, dproj, *, name):
    s = proj.shape[0]
    tm = _tile(s, 1024)
    nq = dqt.shape[0] // 128

    def body(dq_ref, dk_ref, x_ref, w_ref, sc_ref, cos_ref, sa_ref, sb_ref, e_ref, _, du_ref, dw_ref):
        e = e_ref[...]
        d = jnp.where(pl.program_id(0) < nq, dq_ref[...], dk_ref[...]).T * sc_ref[...]
        dn = d * cos_ref[...] + pltpu.roll(d * sa_ref[...], 16, 1) + pltpu.roll(d * sb_ref[...], 112, 1)
        u = x_ref[...].astype(F32)
        r = lax.rsqrt(_seg64(u * u, e) * (1.0 / HEAD_DIM) + EPS)
        uh = u * r
        _acc_rows(dw_ref, jnp.sum(dn * uh, axis=0, keepdims=True), pl.program_id(1) == 0)
        dnw = dn * w_ref[...]
        du_ref[...] = (r * (dnw - uh * (_seg64(dnw * uh, e) * (1.0 / HEAD_DIM)))).astype(du_ref.dtype)

    tab = pl.BlockSpec((tm, 128), lambda j, i: (i, 0))
    row = pl.BlockSpec((1, 128), lambda j, i: (0, j))
    qcol = pl.BlockSpec((tm, 128), lambda j, i: (i, Q0 // 128 + j))
    return pl.pallas_call(
        body, name=name, grid=(QKW // 128, s // tm),
        in_specs=[pl.BlockSpec((128, tm), lambda j, i: (jnp.minimum(j, nq - 1), i)),
                  pl.BlockSpec((128, tm), lambda j, i: (jnp.maximum(j - nq, 0), i)),
                  qcol, row, row, tab, tab, tab, pl.BlockSpec((128, 128), lambda j, i: (0, 0)), ANY],
        out_specs=[qcol, row],
        out_shape=[jax.ShapeDtypeStruct(dproj.shape, dproj.dtype), jax.ShapeDtypeStruct((1, QKW), F32)],
        input_output_aliases={9: 0}, compiler_params=_cp(("arbitrary", "arbitrary")),
    )(dqt, dkt, proj, wrow, scrow, *tabs, _e128(), dproj)


REP = N_Q_HEADS // N_KV_HEADS


def _lanes(ref):
    return jnp.concatenate([ref[r] for r in range(REP)], axis=1)


V_AUG = HEAD_DIM + 8
LOG2E = math.log2(math.e)


def _flash_fwd(qkt, vta, *, name):
    s = qkt.shape[2]
    tq, tk = _tile(s, 1024), _tile(s, 512)
    nk = s // tk
    lanes = REP * tq

    def body(q_ref, k_ref, v_ref, o_ref, lse_ref, m_ref, acc_ref):
        j = pl.program_id(2)

        @pl.when(j == 0)
        def _():
            m_ref[...] = jnp.full_like(m_ref, NEG)
            acc_ref[...] = jnp.zeros_like(acc_ref)

        st = _dot(k_ref[0], _lanes(q_ref), TN)
        m_prev = m_ref[...]
        m_new = jnp.maximum(m_prev, jnp.max(st, axis=0, keepdims=True))
        p = jnp.exp2(st - m_new).astype(MMD)
        acc_ref[...] = jnp.exp2(m_prev - m_new) * acc_ref[...] + _dot(v_ref[0], p)
        m_ref[...] = m_new

        @pl.when(j == nk - 1)
        def _():
            acc = acc_ref[...]
            l = acc[HEAD_DIM:HEAD_DIM + 1]
            o = acc[0:HEAD_DIM] / l
            ls = m_ref[...] + jnp.log(l) * LOG2E
            for r in range(REP):
                o_ref[r] = o[:, r * tq:(r + 1) * tq].astype(o_ref.dtype)
                lse_ref[r] = ls[:, r * tq:(r + 1) * tq]

    qspec = pl.BlockSpec((REP, HEAD_DIM, tq), lambda g, i, j: (g, 0, i))
    return pl.pallas_call(
        body, name=name, grid=(N_KV_HEADS, s // tq, nk),
        in_specs=[qspec, pl.BlockSpec((1, HEAD_DIM, tk), lambda g, i, j: (N_Q_HEADS + g, 0, j)),
                  pl.BlockSpec((1, V_AUG, tk), lambda g, i, j: (g, 0, j))],
        out_specs=[qspec, pl.BlockSpec((REP, 1, tq), lambda g, i, j: (g, 0, i))],
        out_shape=[jax.ShapeDtypeStruct((N_Q_HEADS, HEAD_DIM, s), MMD), jax.ShapeDtypeStruct((N_Q_HEADS, 1, s), F32)],
        scratch_shapes=[pltpu.VMEM((1, lanes), F32), pltpu.VMEM((V_AUG, lanes), F32)],
        compiler_params=_cp(("arbitrary", "arbitrary", "arbitrary"), VMEM_BIG),
    )(qkt, qkt, vta)


def _flash_bwd(qkt, k_h, v_h, dot, ot, lse, *, name):
    s = qkt.shape[2]
    tq, tk = _tile(s, 512), _tile(s, 1024)
    nk = s // tk

    def body(q_ref, kt_ref, k_ref, v_ref, do_ref, o_ref, lse_ref, dq_ref, dk_ref, dv_ref, dq_acc):
        i, j = pl.program_id(1), pl.program_id(2)
        q, do = _lanes(q_ref), _lanes(do_ref)
        delta = jnp.sum(do.astype(F32) * _lanes(o_ref).astype(F32), axis=0, keepdims=True)
        k, v = k_ref[0], v_ref[0]
        p = jnp.exp2(_dot(k, q) - _lanes(lse_ref))
        dvc = _dot(p.astype(MMD), do, NT)
        ds = (p * (_dot(v, do) - delta)).astype(MMD)
        dkc = _dot(ds, q, NT) * (1.0 / LOG2E)
        dqc = _dot(kt_ref[0], ds)
        rows = pl.ds(pl.multiple_of(j * tk, tk), tk)

        @pl.when(i == 0)
        def _():
            dk_ref[0, rows, :] = dkc
            dv_ref[0, rows, :] = dvc

        @pl.when(i > 0)
        def _():
            dk_ref[0, rows, :] += dkc
            dv_ref[0, rows, :] += dvc

        @pl.when(j == 0)
        def _():
            dq_acc[...] = dqc

        @pl.when(j > 0)
        def _():
            dq_acc[...] += dqc

        @pl.when(j == nk - 1)
        def _():
            acc = dq_acc[...]
            for r in range(REP):
                dq_ref[r] = acc[:, r * tq:(r + 1) * tq]

    qspec = pl.BlockSpec((REP, HEAD_DIM, tq), lambda g, i, j: (g, 0, i))
    kvin = pl.BlockSpec((1, tk, HEAD_DIM), lambda g, i, j: (g, j, 0))
    kvres = pl.BlockSpec((1, s, HEAD_DIM), lambda g, i, j: (g, 0, 0))
    return pl.pallas_call(
        body, name=name, grid=(N_KV_HEADS, s // tq, nk),
        in_specs=[qspec, pl.BlockSpec((1, HEAD_DIM, tk), lambda g, i, j: (N_Q_HEADS + g, 0, j)), kvin, kvin,
                  qspec, qspec, pl.BlockSpec((REP, 1, tq), lambda g, i, j: (g, 0, i))],
        out_specs=[qspec, kvres, kvres],
        out_shape=[jax.ShapeDtypeStruct((N_Q_HEADS, HEAD_DIM, s), F32), jax.ShapeDtypeStruct((N_KV_HEADS, s, HEAD_DIM), F32),
                   jax.ShapeDtypeStruct((N_KV_HEADS, s, HEAD_DIM), F32)],
        scratch_shapes=[pltpu.VMEM((HEAD_DIM, REP * tq), F32)],
        compiler_params=_cp(("arbitrary", "arbitrary", "arbitrary"), VMEM_BIG),
    )(qkt, qkt, k_h, v_h, dot, ot, lse)


HALO = 8
CONV_W = 2048 + 2 * SSD_GROUPS * SSD_N


def _shifted(win, off, r):
    return pltpu.roll(win, (r + 2 * HALO - off) % (r + 2 * HALO), 0)[0:r]


def _conv_fwd(proj, w8, brow, *, name):
    s = proj.shape[0]
    cb = 256
    r = _tile(s, 512)

    def body(x_ref, w_ref, b_ref, o_ref, pad_ref):
        zeros = jnp.zeros((HALO, cb), F32)
        pad_ref[0:HALO, :] = zeros
        pad_ref[s + HALO:s + 2 * HALO, :] = zeros

        def fill(i, carry):
            st = pl.multiple_of(i * r, r)
            pad_ref[pl.ds(st + HALO, r), :] = x_ref[pl.ds(st, r), :].astype(F32)
            return carry

        lax.fori_loop(0, s // r, fill, 0)
        wv = w_ref[...]
        bv = b_ref[...]

        def step(i, carry):
            st = pl.multiple_of(i * r, r)
            win = pad_ref[pl.ds(st, r + 2 * HALO), :]
            acc = bv + wv[0:1, :] * _shifted(win, HALO - 2, r)
            for t in range(1, D_CONV):
                acc = acc + wv[t:t + 1, :] * _shifted(win, HALO - 2 + t, r)
            o_ref[pl.ds(st, r), :] = (acc * _sigmoid(acc)).astype(o_ref.dtype)
            return carry

        lax.fori_loop(0, s // r, step, 0)

    return pl.pallas_call(
        body, name=name, grid=(CONV_W // cb,),
        in_specs=[pl.BlockSpec((s, cb), lambda j: (0, XS0 // cb + j)), pl.BlockSpec((8, cb), lambda j: (0, j)),
                  pl.BlockSpec((1, cb), lambda j: (0, j))],
        out_specs=pl.BlockSpec((s, cb), lambda j: (0, j)),
        out_shape=jax.ShapeDtypeStruct((s, CONV_W), MMD),
        scratch_shapes=[pltpu.VMEM((s + 2 * HALO, cb), F32)],
        compiler_params=_cp(("arbitrary",), VMEM_MID),
    )(proj, w8, brow)


def _conv_bwd(proj, col0, ga, gb, w8, brow, dproj, *, name):
    s = proj.shape[0]
    width = ga.shape[1]
    cb = 128
    c0 = col0 // cb
    r = _tile(s, 512)

    def body(x_ref, ga_ref, gb_ref, w_ref, b_ref, _, dx_ref, dw_ref, db_ref, xpad, dpad):
        zeros = jnp.zeros((HALO, cb), F32)
        for ref in (xpad, dpad):
            ref[0:HALO, :] = zeros
            ref[s + HALO:s + 2 * HALO, :] = zeros

        def fill(i, carry):
            st = pl.multiple_of(i * r, r)
            xpad[pl.ds(st + HALO, r), :] = x_ref[pl.ds(st, r), :].astype(F32)
            return carry

        lax.fori_loop(0, s // r, fill, 0)
        wv = w_ref[...]
        bv = b_ref[...]

        def first(i, carry):
            st = pl.multiple_of(i * r, r)
            win = xpad[pl.ds(st, r + 2 * HALO), :]
            taps = [_shifted(win, HALO - 2 + t, r) for t in range(D_CONV)]
            u = bv
            for t in range(D_CONV):
                u = u + wv[t:t + 1, :] * taps[t]
            sg = _sigmoid(u)
            du = ((ga_ref[pl.ds(st, r), :].astype(F32) + gb_ref[pl.ds(st, r), :].astype(F32))
                  * (sg * (1.0 + u * (1.0 - sg))))
            dpad[pl.ds(st + HALO, r), :] = du
            out = [carry[0] + jnp.sum(du, axis=0, keepdims=True)]
            for t in range(D_CONV):
                out.append(carry[1 + t] + jnp.sum(du * taps[t], axis=0, keepdims=True))
            return tuple(out)

        sums = lax.fori_loop(0, s // r, first, tuple(jnp.zeros((1, cb), F32) for _ in range(1 + D_CONV)))
        db_ref[...] = sums[0]
        for t in range(D_CONV):
            dw_ref[t:t + 1, :] = sums[1 + t]
        dw_ref[D_CONV:8, :] = jnp.zeros((8 - D_CONV, cb), F32)

        def second(i, carry):
            st = pl.multiple_of(i * r, r)
            win = dpad[pl.ds(st, r + 2 * HALO), :]
            acc = wv[0:1, :] * _shifted(win, HALO + 2, r)
            for t in range(1, D_CONV):
                acc = acc + wv[t:t + 1, :] * _shifted(win, HALO + 2 - t, r)
            dx_ref[pl.ds(st, r), :] = acc.astype(dx_ref.dtype)
            return carry

        lax.fori_loop(0, s // r, second, 0)

    col = pl.BlockSpec((s, cb), lambda j: (0, j))
    xcol = pl.BlockSpec((s, cb), lambda j: (0, XS0 // cb + c0 + j))
    return pl.pallas_call(
        body, name=name, grid=(width // cb,),
        in_specs=[xcol, col, col, pl.BlockSpec((8, cb), lambda j: (0, c0 + j)),
                  pl.BlockSpec((1, cb), lambda j: (0, c0 + j)), ANY],
        out_specs=[xcol, pl.BlockSpec((8, cb), lambda j: (0, j)), pl.BlockSpec((1, cb), lambda j: (0, j))],
        out_shape=[jax.ShapeDtypeStruct(dproj.shape, dproj.dtype), jax.ShapeDtypeStruct((8, width), F32),
                   jax.ShapeDtypeStruct((1, width), F32)],
        scratch_shapes=[pltpu.VMEM((s + 2 * HALO, cb), F32), pltpu.VMEM((s + 2 * HALO, cb), F32)],
        input_output_aliases={5: 0}, compiler_params=_cp(("arbitrary",), VMEM_BIG),
    )(proj, ga, gb, w8, brow, dproj)


def _tri(lower):
    i = jnp.arange(CHUNK)
    return ((i[:, None] >= i[None, :]) if lower else (i[:, None] <= i[None, :])).astype(F32)


def _dt_fwd(raw, bias, arow, *, name):
    s = raw.shape[0]

    def body(r_ref, b_ref, a_ref, lo_ref, up_ref, dt_ref, cs_ref):
        u = r_ref[...] + b_ref[...]
        dt = jnp.maximum(u, 0.0) + jnp.log1p(jnp.exp(-jnp.abs(u)))
        dt_ref[...] = dt
        a = dt * a_ref[...]
        lane = lax.broadcasted_iota(jnp.int32, (CHUNK, 128), 1)
        cs_ref[...] = jnp.where(lane < SSD_HEADS, _dot_hi(lo_ref[...], a), _dot_hi(up_ref[...], a))

    blk = pl.BlockSpec((CHUNK, 128), lambda i: (i, 0))
    row = pl.BlockSpec((1, 128), lambda i: (0, 0))
    tri = pl.BlockSpec((CHUNK, CHUNK), lambda i: (0, 0))
    return pl.pallas_call(
        body, name=name, grid=(s // CHUNK,), in_specs=[blk, row, row, tri, tri], out_specs=[blk, blk],
        out_shape=[jax.ShapeDtypeStruct((s, 128), F32)] * 2, compiler_params=_cp(("arbitrary",)),
    )(raw, bias, arow, _tri(True), _tri(False))


def _dt_bwd(ddt0, ddt1, raw, bias, dproj, *, name):
    s = raw.shape[0]
    tm = _tile(s, 1024)

    def body(d0_ref, d1_ref, r_ref, b_ref, _, o_ref, db_ref):
        g = (d0_ref[...] + d1_ref[...]) * _sigmoid(r_ref[...] + b_ref[...])
        o_ref[...] = g.astype(o_ref.dtype)
        _acc_rows(db_ref, jnp.sum(g, axis=0, keepdims=True), pl.program_id(0) == 0)

    blk = pl.BlockSpec((tm, 128), lambda i: (i, 0))
    row = pl.BlockSpec((1, 128), lambda i: (0, 0))
    return pl.pallas_call(
        body, name=name, grid=(s // tm,), in_specs=[blk, blk, blk, row, ANY],
        out_specs=[pl.BlockSpec((tm, 128), lambda i: (i, DT0 // 128)), row],
        out_shape=[jax.ShapeDtypeStruct(dproj.shape, dproj.dtype), jax.ShapeDtypeStruct((1, 128), F32)],
        input_output_aliases={4: 0}, compiler_params=_cp(("arbitrary",)),
    )(ddt0, ddt1, raw, bias, dproj)


GW = HPG * SSD_P


GPS = SSD_GROUPS


def _ssd_specs(nc, rev):
    cc = (lambda c: nc - 1 - c) if rev else (lambda c: c)
    return dict(
        x=pl.BlockSpec((CHUNK, GPS * GW), lambda g, c: (cc(c), g)),
        b=pl.BlockSpec((CHUNK, GPS * SSD_N), lambda g, c: (cc(c), 2048 // (GPS * SSD_N) + g)),
        c=pl.BlockSpec((CHUNK, GPS * SSD_N), lambda g, c: (cc(c), 2048 // (GPS * SSD_N) + 1 + g)),
        lanes=pl.BlockSpec((CHUNK, 128), lambda g, c: (cc(c), 0)),
        drow=pl.BlockSpec((1, GPS * GW), lambda g, c: (0, g)),
        y=pl.BlockSpec((CHUNK, GPS * GW), lambda g, c: (cc(c), g)),
        h=pl.BlockSpec((GPS, 1, SSD_N, GW), lambda g, c: (g, cc(c), 0, 0)),
        n=pl.BlockSpec((CHUNK, GPS * SSD_N), lambda g, c: (cc(c), g)),
    )


def _ssd_mask(anti):
    ii = lax.broadcasted_iota(jnp.int32, (CHUNK, CHUNK), 0)
    jj = lax.broadcasted_iota(jnp.int32, (CHUNK, CHUNK), 1)
    return ii, jj, (ii <= jj) if anti else (ii >= jj)


def _expand(x, ex, terms=3):
    h1 = x.astype(jnp.bfloat16)
    r1 = x - h1.astype(F32)
    h2 = r1.astype(jnp.bfloat16)
    out = _dot(h1, ex) + _dot(h2, ex)
    if terms == 3:
        out = out + _dot((r1 - h2.astype(F32)).astype(jnp.bfloat16), ex)
    return out


def _headsum(a, e):
    hi = a.astype(jnp.bfloat16)
    return _dot(hi, e) + _dot((a - hi.astype(F32)).astype(jnp.bfloat16), e)


def _expand_mats():
    lane = jnp.arange(128)[None, :, None]
    col = jnp.arange(GW)[None, None, :]
    base = (jnp.arange(2)[:, None] * SSD_HEADS + jnp.arange(SSD_GROUPS)[None, :] * HPG).reshape(2 * SSD_GROUPS, 1, 1)
    return (lane == base + col // SSD_P).astype(jnp.bfloat16)


def _headsum_mats():
    e1 = (jnp.arange(GW)[:, None] // SSD_P == jnp.arange(128)[None, :]).astype(jnp.bfloat16)
    e2 = (jnp.arange(HPG * CHUNK)[:, None] // CHUNK == jnp.arange(128)[None, :]).astype(jnp.bfloat16)
    return e1, e2


def _ssd_fwd(xc, dt, cs, ex, drow, di, *, name):
    s = xc.shape[0]
    nc = s // CHUNK
    anti = di == 1
    sp = _ssd_specs(nc, anti)
    trow = 0 if anti else CHUNK - 1

    def body(x_ref, b_ref, c_ref, dt_ref, cs_ref, ex_ref, d_ref, y_ref, hp_ref, h_ref):
        @pl.when(pl.program_id(1) == 0)
        def _():
            h_ref[...] = jnp.zeros_like(h_ref)

        mask = _ssd_mask(anti)[2]
        dtv, csv = dt_ref[...], cs_ref[...]
        cst = csv.T
        for gi in range(GPS):
            cols = slice(gi * GW, (gi + 1) * GW)
            ncols = slice(gi * SSD_N, (gi + 1) * SSD_N)
            ex = ex_ref[gi]
            xb = x_ref[:, cols].astype(F32)
            bm, cm = b_ref[:, ncols], c_ref[:, ncols]
            csr = cst[SSD_HEADS * di + HPG * gi:SSD_HEADS * di + HPG * (gi + 1)]
            dtf = _expand(dtv, ex, 2)
            csf = _expand(csv, ex)
            tl = csf[trow:trow + 1, :]
            h = h_ref[gi]
            hp_ref[gi, 0] = h.astype(hp_ref.dtype)
            g = _dot(cm, bm, NT)
            xs = xb * dtf
            xsm = xs.astype(MMD)
            base = jnp.exp(csf) * _dot(cm, h.astype(MMD)) + d_ref[:, cols] * xb
            for r in range(HPG):
                sl = slice(r * SSD_P, (r + 1) * SSD_P)
                lm = jnp.exp(jnp.where(mask, csf[:, r * SSD_P:r * SSD_P + 1] - csr[r:r + 1, :], NEG))
                y_ref[:, gi * GW + r * SSD_P:gi * GW + (r + 1) * SSD_P] = (
                    _dot((g * lm).astype(MMD), xsm[:, sl]) + base[:, sl]).astype(y_ref.dtype)
            xd = (xs * jnp.exp(tl - csf)).astype(MMD)
            h_ref[gi] = h * jnp.exp(tl) + _dot(bm, xd, TN)

    return pl.pallas_call(
        body, name=name, grid=(1, nc),
        in_specs=[sp["x"], sp["b"], sp["c"], sp["lanes"], sp["lanes"],
                  pl.BlockSpec((GPS, 128, GW), lambda g, c: (di, 0, 0)), sp["drow"]],
        out_specs=[sp["y"], sp["h"]],
        out_shape=[jax.ShapeDtypeStruct((s, 2048), MMD), jax.ShapeDtypeStruct((SSD_GROUPS, nc, SSD_N, GW), MMD)],
        scratch_shapes=[pltpu.VMEM((GPS, SSD_N, GW), F32)],
        compiler_params=_cp(("arbitrary", "arbitrary")),
    )(xc, xc, xc, dt, cs, ex, drow)


def _ssd_bwd(xc, dt, cs, ex, drow, arow, dy, hprev, di, *, name):
    s = xc.shape[0]
    nc = s // CHUNK
    anti = di == 1
    sp = _ssd_specs(nc, not anti)
    trow = 0 if anti else CHUNK - 1
    e1, e2 = _headsum_mats()

    def body(x_ref, b_ref, c_ref, dt_ref, cs_ref, ex_ref, d_ref, a_ref, dy_ref, hp_ref, tri_ref,
             e1_ref, e2_ref, dx_ref, db_ref, dc_ref, ddt_ref, da_ref, dh_ref, w_ref, dxs_ref):
        @pl.when(pl.program_id(1) == 0)
        def _():
            dh_ref[...] = jnp.zeros_like(dh_ref)
            da_ref[...] = jnp.zeros_like(da_ref)

        e1v = e1_ref[...]
        ii, _, mask = _ssd_mask(anti)
        dtv, csv = dt_ref[...], cs_ref[...]
        cst = csv.T
        ddt_acc = jnp.zeros((CHUNK, 128), F32)
        da_acc = jnp.zeros((1, 128), F32)
        for gi in range(GPS):
            lane0 = SSD_HEADS * di + HPG * gi
            cols = slice(gi * GW, (gi + 1) * GW)
            ncols = slice(gi * SSD_N, (gi + 1) * SSD_N)
            ex = ex_ref[gi]
            xb = x_ref[:, cols].astype(F32)
            bm, cm = b_ref[:, ncols], c_ref[:, ncols]
            csr = cst[lane0:lane0 + HPG]
            dym = dy_ref[:, cols]
            dyb = dym.astype(F32)
            hpm = hp_ref[gi, 0]
            hp = hpm.astype(F32)
            dh = dh_ref[gi]
            dhm = dh.astype(MMD)
            dtf = _expand(dtv, ex, 2)
            csf = _expand(csv, ex)
            tl = csf[trow:trow + 1, :]
            e = jnp.exp(csf)
            dec = jnp.exp(tl - csf)
            et = jnp.exp(tl)
            xs = xb * dtf
            xsm = xs.astype(MMD)
            g = _dot(cm, bm, NT)
            z = _dot(cm, hpm)
            bdh = _dot(bm, dhm)
            dg = jnp.zeros((CHUNK, CHUNK), F32)
            wcols = jnp.zeros((CHUNK, CHUNK), F32)
            for r in range(HPG):
                sl = slice(r * SSD_P, (r + 1) * SSD_P)
                lm = jnp.exp(jnp.where(mask, csf[:, r * SSD_P:r * SSD_P + 1] - csr[r:r + 1, :], NEG))
                mm = g * lm
                dm = _dot(dym[:, sl], xsm[:, sl], NT)
                w = dm * mm
                w_ref[gi, :, r * CHUNK:(r + 1) * CHUNK] = w
                wcols = jnp.where(ii == r, jnp.sum(w, axis=0, keepdims=True), wcols)
                dg = dg + dm * lm
                dxs_ref[gi, :, sl] = _dot(mm.astype(MMD), dym[:, sl], TN)
            dxs = dxs_ref[gi] + dec * bdh
            dx_ref[:, cols] = (dxs * dtf + d_ref[:, cols] * dyb).astype(dx_ref.dtype)
            tb = xs * bdh * dec
            d_tot = jnp.sum(tb, axis=0, keepdims=True) + et * jnp.sum(dh * hp, axis=0, keepdims=True)
            d_tot = _headsum(jnp.broadcast_to(d_tot, (8, GW)), e1v)[0:1]
            dcs = (_headsum(dyb * (e * z) - tb, e1v) + _headsum(w_ref[gi], e2_ref[...]) - wcols.T
                   + jnp.where(ii == trow, d_tot, 0.0))
            da = pltpu.roll(_dot_hi(tri_ref[...], dcs), lane0, 1)
            ddt_acc = ddt_acc + da * a_ref[...] + pltpu.roll(_headsum(dxs * xb, e1v), lane0, 1)
            da_acc = da_acc + jnp.sum(da * dtv, axis=0, keepdims=True)
            dgm = dg.astype(MMD)
            dz = (e * dyb).astype(MMD)
            dc_ref[:, ncols] = (_dot(dgm, bm) + _dot(dz, hpm, NT)).astype(dc_ref.dtype)
            db_ref[:, ncols] = (_dot(dgm, cm, TN) + _dot((xs * dec).astype(MMD), dhm, NT)).astype(db_ref.dtype)
            dh_ref[gi] = dh * et + _dot(cm, dz, TN)
        ddt_ref[...] = ddt_acc
        da_ref[...] += da_acc

    const = lambda shape: pl.BlockSpec(shape, lambda g, c: (0,) * len(shape))
    return pl.pallas_call(
        body, name=name, grid=(1, nc),
        in_specs=[sp["x"], sp["b"], sp["c"], sp["lanes"], sp["lanes"],
                  pl.BlockSpec((GPS, 128, GW), lambda g, c: (di, 0, 0)), sp["drow"],
                  const((1, 128)), sp["y"], sp["h"],
                  const((CHUNK, CHUNK)), const((GW, 128)), const((HPG * CHUNK, 128))],
        out_specs=[sp["y"], sp["n"], sp["n"], sp["lanes"], const((1, 128))],
        out_shape=[jax.ShapeDtypeStruct((s, 2048), MMD), jax.ShapeDtypeStruct((s, SSD_GROUPS * SSD_N), MMD),
                   jax.ShapeDtypeStruct((s, SSD_GROUPS * SSD_N), MMD), jax.ShapeDtypeStruct((s, 128), F32),
                   jax.ShapeDtypeStruct((1, 128), F32)],
        scratch_shapes=[pltpu.VMEM((GPS, SSD_N, GW), F32), pltpu.VMEM((GPS, CHUNK, HPG * CHUNK), F32),
                        pltpu.VMEM((GPS, CHUNK, GW), F32)],
        compiler_params=_cp(("arbitrary", "arbitrary")),
    )(xc, xc, xc, dt, cs, ex, drow, arow, dy, hprev, _tri(anti), e1, e2)


def _gnorm_fwd(ya, yb, proj, w, *, name):
    s = ya.shape[0]
    tm = _tile(s, 256)

    def body(a_ref, b_ref, z_ref, w_ref, o_ref):
        zv = z_ref[...].astype(F32)
        t = (a_ref[...].astype(F32) + b_ref[...].astype(F32)) * (zv * _sigmoid(zv))
        r = lax.rsqrt(jnp.mean(t * t, axis=-1, keepdims=True) + EPS)
        o_ref[...] = ((t * r) * w_ref[...]).astype(o_ref.dtype)

    big = pl.BlockSpec((tm, 2048), lambda i: (i, 0))
    row = pl.BlockSpec((1, 2048), lambda i: (0, 0))
    return pl.pallas_call(
        body, name=name, grid=(s // tm,), in_specs=[big, big, big, row], out_specs=big,
        out_shape=jax.ShapeDtypeStruct((s, 2048), MMD), compiler_params=_cp(("arbitrary",)),
    )(ya, yb, proj, w)


def _gnorm_bwd(dout, ya, yb, proj, xc, w, dproj, *, name):
    s = ya.shape[0]
    tm = _tile(s, 256)

    def body(do_ref, a_ref, b_ref, z_ref, x_ref, w_ref, _, dy_ref, dz_ref, dw_ref, dd_ref):
        zv = z_ref[...].astype(F32)
        sg = _sigmoid(zv)
        sz = zv * sg
        y = a_ref[...].astype(F32) + b_ref[...].astype(F32)
        t = y * sz
        r = lax.rsqrt(jnp.mean(t * t, axis=-1, keepdims=True) + EPS)
        nv = t * r
        dov = do_ref[...].astype(F32)
        _acc_rows(dw_ref, jnp.sum(dov * nv, axis=0, keepdims=True), pl.program_id(0) == 0)
        dn = dov * w_ref[...]
        dt_ = r * (dn - nv * jnp.mean(dn * nv, axis=-1, keepdims=True))
        dy = dt_ * sz
        dy_ref[...] = dy.astype(dy_ref.dtype)
        dz_ref[...] = (dt_ * y * (sg * (1.0 + zv * (1.0 - sg)))).astype(dz_ref.dtype)
        _acc_rows(dd_ref, jnp.sum(dy * x_ref[...].astype(F32), axis=0, keepdims=True), pl.program_id(0) == 0)

    big = pl.BlockSpec((tm, 2048), lambda i: (i, 0))
    row = pl.BlockSpec((1, 2048), lambda i: (0, 0))
    return pl.pallas_call(
        body, name=name, grid=(s // tm,), in_specs=[big, big, big, big, big, row, ANY], out_specs=[big, big, row, row],
        out_shape=[jax.ShapeDtypeStruct((s, 2048), MMD), jax.ShapeDtypeStruct(dproj.shape, dproj.dtype),
                   jax.ShapeDtypeStruct((1, 2048), F32), jax.ShapeDtypeStruct((1, 2048), F32)],
        input_output_aliases={6: 1}, compiler_params=_cp(("arbitrary",)),
    )(dout, ya, yb, proj, xc, w, dproj)


def _heads(a, n):
    return a.reshape(a.shape[0], n, HEAD_DIM).transpose(1, 0, 2)


def _unheads(a):
    return a.transpose(1, 0, 2).reshape(a.shape[1], a.shape[0] * HEAD_DIM)


def _local_step(x, target, mod, wts, small, late_weights=None, late_grads=None, in_grad=None):
    s, d = x.shape
    shift1, scale1, gate1, shift2, scale2, gate2 = [mod[i:i + 1] for i in range(6)]

    h1 = _ln_mod(x, small["norm1_w"], scale1, shift1, name="ln1")
    proj = _mm(h1, wts["w_in_p"], name="in_proj", outs=[MMD], tm=512, tn=2944, b_outer=True)
    dt_raw = _mm(h1, wts["w_dt"], name="dt_proj", outs=[F32], tm=512, tn=128)

    qk_w = jnp.concatenate([jnp.tile(small["q_norm_w"], (1, N_Q_HEADS)), jnp.tile(small["k_norm_w"], (1, N_KV_HEADS))], axis=1)
    qk_sc = jnp.concatenate([jnp.full((1, N_Q_HEADS * HEAD_DIM), HEAD_DIM ** -0.5, F32),
                             jnp.ones((1, N_KV_HEADS * HEAD_DIM), F32)], axis=1)
    qk_sc2 = jnp.concatenate([jnp.full((1, N_Q_HEADS * HEAD_DIM), HEAD_DIM ** -0.5 * LOG2E, F32),
                              jnp.ones((1, N_KV_HEADS * HEAD_DIM), F32)], axis=1)
    tabs = _rope_tables(s)
    qk, qkt = _qk_fwd(proj, qk_w, qk_sc2, tabs, name="qk_fwd")
    qkt = qkt.reshape(N_Q_HEADS + N_KV_HEADS, HEAD_DIM, s)
    k_h = _heads(qk[:, N_Q_HEADS * HEAD_DIM:], N_KV_HEADS)
    v_sd = proj[:, V0:V0 + N_KV_HEADS * HEAD_DIM]
    v_h = _heads(v_sd, N_KV_HEADS)
    vta = jnp.concatenate([v_sd.T.reshape(N_KV_HEADS, HEAD_DIM, s), jnp.ones((N_KV_HEADS, V_AUG - HEAD_DIM, s), MMD)], axis=1)
    ot, lse = _flash_fwd(qkt, vta, name="flash_fwd")
    ot2 = ot.reshape(N_Q_HEADS * HEAD_DIM, s)
    if late_weights is not None:
        wts = {**wts, **late_weights(ot)}

    w8 = jnp.pad(small["conv_w"], ((0, 8 - D_CONV), (0, 0)))
    xc = _conv_fwd(proj, w8, small["conv_b"], name="conv_fwd")
    a_neg = -jnp.exp(small["A_log"])
    arow = jnp.pad(a_neg.reshape(1, 2 * SSD_HEADS), ((0, 0), (0, 128 - 2 * SSD_HEADS)))
    bias_row = jnp.pad(small["dt_bias"].reshape(1, 2 * SSD_HEADS), ((0, 0), (0, 128 - 2 * SSD_HEADS)))
    dt, cs = _dt_fwd(dt_raw, bias_row, arow, name="dt_fwd")
    drow = jnp.repeat(small["ssd_D"], SSD_P, axis=1)
    dirs = [dict(drow=drow), dict(drow=jnp.zeros_like(drow))]
    ex = _expand_mats()
    ys = []
    for di, dd in enumerate(dirs):
        y, dd["hprev"] = _ssd_fwd(xc, dt, cs, ex, dd["drow"], di, name=f"ssd_fwd{di}")
        ys.append(y)
    ssdn = _gnorm_fwd(ys[0], ys[1], proj, small["ssd_norm_w"], name="gnorm_fwd")

    a_o = _mm(ot2, wts["w_attn_out"], name="attn_out", outs=[MMD], ta=True, tm=512, tn=1024)

    def merge_epi(acc, ao, ga, gs):
        return (_sigmoid(ga.astype(F32)) * ao.astype(F32) + _sigmoid(gs.astype(F32)) * acc, acc)

    merged, b_o = _mm(ssdn, wts["w_ssd_out"], name="ssd_out", outs=[MMD, MMD], tm=512, tn=1024,
                      extras=[(a_o, "tile", 0), (proj, "tile", GA0), (proj, "tile", GS0)], epi=merge_epi)

    def res_epi(acc, res, gate):
        return (res + gate * acc, acc)

    x1, mo = _mm(merged, wts["w_o"], name="w_o", outs=[F32, MMD], tm=512, tn=1024,
                 extras=[(x, "tile", 0), (gate1, "row", 0)], epi=res_epi)
    h2 = _ln_mod(x1, small["norm2_w"], scale2, shift2, name="ln2")

    def relu2_epi(acc):
        rl = jnp.maximum(acc, 0.0)
        return (rl * rl, rl)

    act, rl = _mm(h2, wts["w_mlp1"], name="mlp1", outs=[MMD, MMD], tm=1024, tn=1024, epi=relu2_epi, b_outer=True)

    def loss_epi(acc, res, gate, tgt):
        return ((res + gate * acc - tgt) * (1.0 / d), acc)

    dy, ffo = _mm(act, wts["w_mlp2"], name="mlp2", outs=[F32, MMD], tm=512, tn=1024, vmem=VMEM_BIG,
                  extras=[(x1, "tile", 0), (gate2, "row", 0), (target, "tile", 0)], epi=loss_epi)
    loss = _sumsq(dy, name="loss") * (0.5 * d)

    gw = {}
    gs_ = {}
    dffo, dgate2 = _gate_bwd(dy, ffo, gate2, name="gate2_bwd")
    dpre = _mm(dffo, wts["w_mlp2"], name="mlp2_dx", outs=[MMD], nt=True, tm=1024, tn=1024, b_outer=True,
               extras=[(rl, "tile", 0)], epi=lambda acc, r: (acc * (2.0 * r.astype(F32)),))
    gw["w_mlp2"] = _mm_tn(act, dffo, name="mlp2_dw")
    dh2 = _mm(dpre, wts["w_mlp1"], name="mlp1_dx", outs=[F32], nt=True, tm=1024, tn=1024, vmem=VMEM_BIG)
    gw["w_mlp1"] = _mm_tn(h2, dpre, name="mlp1_dw")
    dx1, dshift2, dscale2, gs_["norm2_w"] = _ln_mod_bwd(dh2, x1, small["norm2_w"], scale2, dy, name="ln2_bwd")
    dmo, dgate1 = _gate_bwd(dx1, mo, gate1, name="gate1_bwd")

    def merge_bwd_epi(acc, ao, bo, ga, gs):
        sa, ss = _sigmoid(ga.astype(F32)), _sigmoid(gs.astype(F32))
        return (acc * sa, acc * ss, acc * ao.astype(F32) * sa * (1.0 - sa), acc * bo.astype(F32) * ss * (1.0 - ss))

    da_o, db_o, dga, dgs = _mm(dmo, wts["w_o"], name="w_o_dx", outs=[MMD] * 4, nt=True, tm=512, tn=1024,
                               extras=[(a_o, "tile", 0), (b_o, "tile", 0), (proj, "tile", GA0), (proj, "tile", GS0)],
                               epi=merge_bwd_epi)
    gw["w_o"] = _mm_tn(merged, dmo, name="w_o_dw")
    dot = _mm(wts["w_attn_out"], da_o, name="attn_out_dx", outs=[MMD], nt=True, tm=1024, tn=1024)
    gw["w_attn_out"] = _mm(ot2, da_o, name="attn_out_dw", outs=[F32], tm=256, tn=512, vmem=VMEM_BIG)
    dssdn = _mm(db_o, wts["w_ssd_out"], name="ssd_out_dx", outs=[MMD], nt=True, tm=512, tn=2048)
    gw["w_ssd_out"] = _mm_tn(ssdn, db_o, name="ssd_out_dw")

    dproj = lax.dynamic_update_slice(lax.empty((s, PW), MMD), jnp.concatenate([dga, dgs], axis=1), (0, GA0))

    norm_w = small["ssd_norm_w"] if late_grads is None else small["ssd_norm_w"] + late_grads(gw)
    dyssd, dproj, gs_["ssd_norm_w"], dd_row = _gnorm_bwd(dssdn, ys[0], ys[1], proj, xc, norm_w, dproj, name="gnorm_bwd")
    gs_["ssd_D"] = dd_row.reshape(SSD_HEADS, SSD_P).sum(axis=1).reshape(1, SSD_HEADS)
    dxc, ddts, das = [], [], []
    for di, dd in enumerate(dirs):
        dxs, dbm, dcm, ddt_d, da_d = _ssd_bwd(xc, dt, cs, ex, dd["drow"], arow, dyssd, dd["hprev"], di, name=f"ssd_bwd{di}")
        dxc.append((dxs, dbm, dcm))
        ddts.append(ddt_d)
        das.append(da_d)
    dw8, db, col0 = [], [], 0
    for part, (ga, gb) in enumerate(zip(*dxc)):
        dproj, dw_part, db_part = _conv_bwd(proj, col0, ga, gb, w8, small["conv_b"], dproj, name=f"conv_bwd{part}")
        dw8.append(dw_part)
        db.append(db_part)
        col0 += ga.shape[1]
    gs_["conv_w"] = jnp.concatenate(dw8, axis=1)[0:D_CONV]
    gs_["conv_b"] = jnp.concatenate(db, axis=1)
    gs_["A_log"] = (das[0] + das[1])[:, 0:2 * SSD_HEADS].reshape(2, SSD_HEADS) * a_neg
    dproj, dbias = _dt_bwd(ddts[0], ddts[1], dt_raw, bias_row, dproj, name="dt_bwd")
    gs_["dt_bias"] = dbias[:, 0:2 * SSD_HEADS].reshape(2, SSD_HEADS)

    dqt, dk_h, dv_h = _flash_bwd(qkt, k_h, v_h, dot.reshape(N_Q_HEADS, HEAD_DIM, s), ot, lse, name="flash_bwd")
    dproj, dqk_w = _qk_bwd(dqt.reshape(N_Q_HEADS * HEAD_DIM, s), dk_h.transpose(0, 2, 1).reshape(N_KV_HEADS * HEAD_DIM, s),
                           proj, qk_w, qk_sc, tabs, dproj, name="qk_bwd")
    gs_["q_norm_w"] = dqk_w[:, 0:N_Q_HEADS * HEAD_DIM].reshape(N_Q_HEADS, HEAD_DIM).sum(axis=0, keepdims=True)
    gs_["k_norm_w"] = dqk_w[:, N_Q_HEADS * HEAD_DIM:].reshape(N_KV_HEADS, HEAD_DIM).sum(axis=0, keepdims=True)
    dproj = lax.dynamic_update_slice(dproj, _unheads(dv_h).astype(MMD), (0, V0))

    gw["w_in_p"] = _mm_tn(h1, dproj, name="in_proj_dw", tk=512, tn=2944, tmm=2048, vmem=VMEM_BIG)
    zero_row = jnp.zeros((1, d), F32) if in_grad is None else jnp.zeros((1, d), F32) + in_grad(gw["w_in_p"])[0:1, 0:1]
    dh1 = _mm(dproj, wts["w_in_p"], name="in_proj_dx", outs=[F32], nt=True, tm=256, tn=1024, vmem=VMEM_BIG,
              extras=[(zero_row, "row", 0)], epi=lambda acc, r: (acc + r,))
    grad_x, dshift1, dscale1, gs_["norm1_w"] = _ln_mod_bwd(dh1, x, small["norm1_w"], scale1, dx1, name="ln1_bwd")
    dmod = jnp.concatenate([dshift1, dscale1, dgate1, dshift2, dscale2, dgate2], axis=0)
    return loss, grad_x, dmod, gw, gs_


N_DEV = 8
N_CHIP = 4
ANY = pl.BlockSpec(memory_space=pl.ANY)


def _place():
    return lax.axis_index("x"), lax.axis_index("y"), lax.axis_index("c")


def _allgather8(v, *, name):
    m_per, n = v.shape

    def body(x_ref, out_ref, send_sems, recv_sems, local_sem):
        x, y, c = _place()
        me, sibling = (x, y, c), (x, y, 1 - c)
        chips = [(1 - x, y), (x, 1 - y), (1 - x, 1 - y)]

        def rows(px, py, pc):
            return out_ref.at[pl.ds((4 * px + 2 * py + pc) * m_per, m_per), :]

        def copy(k, block, to, src=None):
            return pltpu.make_async_remote_copy(
                src_ref=rows(*block) if src is None else src, dst_ref=rows(*block),
                send_sem=send_sems.at[k], recv_sem=recv_sems.at[k], device_id=to, device_id_type=MESH)

        mine = pltpu.make_async_copy(x_ref, rows(*me), local_sem)
        mine.start()
        first = [copy(0, me, sibling, src=x_ref)]
        first += [copy(1 + j, me, (*chip, c), src=x_ref) for j, chip in enumerate(chips)]
        for cp in first:
            cp.start()
        passed = [copy(4 + j, (*chip, c), sibling) for j, chip in enumerate(chips)]
        for j, chip in enumerate(chips):
            copy(1 + j, (*chip, c), me).wait_recv()
            passed[j].start()
        copy(0, sibling, me).wait_recv()
        for j, chip in enumerate(chips):
            copy(4 + j, (*chip, 1 - c), me).wait_recv()
        for cp in first + passed:
            cp.wait_send()
        mine.wait()

    return pl.pallas_call(
        body, name=name, out_shape=jax.ShapeDtypeStruct((N_DEV * m_per, n), v.dtype),
        in_specs=[pl.BlockSpec(memory_space=pltpu.VMEM)], out_specs=pl.BlockSpec(memory_space=pltpu.VMEM),
        scratch_shapes=[pltpu.SemaphoreType.DMA((7,)), pltpu.SemaphoreType.DMA((7,)), pltpu.SemaphoreType.DMA],
    )(v)


HBM = pl.BlockSpec(memory_space=pltpu.HBM)
SEM = pl.BlockSpec(memory_space=pltpu.SEMAPHORE)


def _chips_copies(x_ref, land_ref, sems, scatter):
    x, y, c = _place()
    k = 2 * x + y
    chips = [(1 - x, y), (x, 1 - y), (1 - x, 1 - y)]
    ids = [2 * cx + cy for cx, cy in chips]

    def copy(j, slot):
        return pltpu.make_async_remote_copy(
            src_ref=x_ref.at[ids[j]] if scatter else x_ref, dst_ref=land_ref.at[slot], send_sem=sems[j],
            recv_sem=sems[3 + j], device_id=(*chips[j], c), device_id_type=MESH)

    return [copy(j, k) for j in range(3)], [copy(j, ids[j]) for j in range(3)]


def _chips_start(src, scatter, *, name):
    shape = src.shape if scatter else (N_CHIP,) + tuple(src.shape)

    def body(x_ref, land_ref, *rest):
        sems, token = rest[0:6], rest[8]
        for cp in _chips_copies(x_ref, land_ref, sems, scatter)[0]:
            cp.start()
        token[...] = jnp.zeros_like(token)

    out = pl.pallas_call(
        body, name=name,
        out_shape=(pltpu.SemaphoreType.DMA(()),) * 6 + (pltpu.HBM(src.shape, src.dtype), pltpu.HBM(shape, src.dtype),
                                                       jax.ShapeDtypeStruct((8, 128), F32)),
        in_specs=(HBM, HBM), out_specs=(SEM,) * 6 + (HBM, HBM, pl.BlockSpec(memory_space=pltpu.VMEM)),
        input_output_aliases={0: 6, 1: 7},
        compiler_params=pltpu.CompilerParams(has_side_effects=pltpu.SideEffectType.DATAFLOW_SIDE_EFFECTING),
    )(pltpu.with_memory_space_constraint(src, pltpu.HBM),
      pltpu.with_memory_space_constraint(lax.empty(shape, src.dtype), pltpu.HBM))
    return out[0:6], out[6], out[7], out[8]


def _chips_wait(sems, src, land, after, scatter, *, name):
    def body(x_ref, land_ref, *rest):
        sems_ = rest[0:6]
        for cp in _chips_copies(x_ref, land_ref, sems_, scatter)[1]:
            cp.wait_send()
            cp.wait_recv()

    return pl.pallas_call(
        body, name=name, out_shape=(pltpu.HBM(src.shape, src.dtype), pltpu.HBM(land.shape, land.dtype)),
        in_specs=(HBM, HBM) + (SEM,) * 6 + (ANY,), out_specs=(HBM, HBM), input_output_aliases={0: 0, 1: 1},
        compiler_params=pltpu.CompilerParams(has_side_effects=pltpu.SideEffectType.DATAFLOW_SIDE_EFFECTING),
    )(src, land, *sems, after)


def _row_tile(r, pref=512):
    return max(t for t in range(16, pref + 1, 16) if r % t == 0)


def _gather_weights(src, *, name):
    r = src.shape[0]
    hr = r // 2
    assert r == 2 * hr and hr % 16 == 0

    def body(x_ref, out_ref, send_sems, recv_sems):
        x, y, c = _place()
        k = 2 * x + y
        chips = [(1 - x, y), (x, 1 - y), (1 - x, 1 - y)]
        ids = [2 * cx + cy for cx, cy in chips]
        mine_rows = pl.ds(pl.multiple_of(c * hr, 16), hr)
        other_rows = pl.ds(pl.multiple_of((1 - c) * hr, 16), hr)

        def copy(sem, src_ref, slot, rows, to):
            return pltpu.make_async_remote_copy(
                src_ref=src_ref, dst_ref=out_ref.at[slot, rows], send_sem=send_sems.at[sem], recv_sem=recv_sems.at[sem],
                device_id=to, device_id_type=MESH)

        sends = [copy(j, x_ref.at[mine_rows], k, mine_rows, (cx, cy, c)) for j, (cx, cy) in enumerate(chips)]
        for cp in sends:
            cp.start()
        passed = [copy(3 + j, out_ref.at[ids[j], mine_rows], ids[j], mine_rows, (x, y, 1 - c)) for j in range(3)]
        for j, (cx, cy) in enumerate(chips):
            copy(j, x_ref.at[mine_rows], ids[j], mine_rows, (cx, cy, c)).wait_recv()
            passed[j].start()
        for j in range(3):
            copy(3 + j, out_ref.at[ids[j], other_rows], ids[j], other_rows, (x, y, 1 - c)).wait_recv()
        for cp in sends + passed:
            cp.wait_send()

    return pl.pallas_call(
        body, name=name, out_shape=jax.ShapeDtypeStruct((N_CHIP,) + tuple(src.shape), src.dtype),
        in_specs=[ANY], out_specs=ANY,
        scratch_shapes=[pltpu.SemaphoreType.DMA((6,)), pltpu.SemaphoreType.DMA((6,))],
    )(src)


def _pair_swap(a, *, name):
    n, r, cols = a.shape
    hr = r // 2

    def body(x_ref, out_ref, send_sem, recv_sem):
        x, y, c = _place()
        other_rows = pl.ds(pl.multiple_of((1 - c) * hr, 16), hr)
        cp = pltpu.make_async_remote_copy(src_ref=x_ref.at[:, other_rows], dst_ref=out_ref, send_sem=send_sem,
                                          recv_sem=recv_sem, device_id=(x, y, 1 - c), device_id_type=MESH)
        cp.start()
        cp.wait()

    return pl.pallas_call(
        body, name=name, out_shape=jax.ShapeDtypeStruct((n, hr, cols), a.dtype), in_specs=[ANY], out_specs=ANY,
        scratch_shapes=[pltpu.SemaphoreType.DMA, pltpu.SemaphoreType.DMA],
    )(a)


def _sibling_copy(a, *, name):
    def body(x_ref, out_ref, send_sem, recv_sem):
        x, y, c = _place()
        cp = pltpu.make_async_remote_copy(src_ref=x_ref, dst_ref=out_ref, send_sem=send_sem, recv_sem=recv_sem,
                                          device_id=(x, y, 1 - c), device_id_type=MESH)
        cp.start()
        cp.wait()

    return pl.pallas_call(
        body, name=name, out_shape=jax.ShapeDtypeStruct(a.shape, a.dtype), in_specs=[ANY], out_specs=ANY,
        scratch_shapes=[pltpu.SemaphoreType.DMA, pltpu.SemaphoreType.DMA],
    )(a)


def _sum_slots(a, own, *, name):
    _, r, c = a.shape
    tr = _row_tile(r, 256)

    def body(a_ref, own_ref, o_ref):
        k = 2 * lax.axis_index("x") + lax.axis_index("y")
        acc = None
        for j in range(N_CHIP):
            term = jnp.where(k == j, own_ref[j], a_ref[j]).astype(F32)
            acc = term if acc is None else acc + term
        o_ref[...] = acc

    spec = pl.BlockSpec((N_CHIP, tr, c), lambda i: (0, i, 0))
    return pl.pallas_call(
        body, name=name, grid=(r // tr,), in_specs=[spec, spec],
        out_specs=pl.BlockSpec((tr, c), lambda i: (i, 0)), out_shape=jax.ShapeDtypeStruct((r, c), F32),
        compiler_params=_cp(("arbitrary",)),
    )(a, own)


def _add2(a, b, *, name):
    r, c = a.shape
    tr = _row_tile(r)

    def body(a_ref, b_ref, o_ref):
        o_ref[...] = (a_ref[...].astype(F32) + b_ref[...].astype(F32)).astype(o_ref.dtype)

    spec = pl.BlockSpec((tr, c), lambda i: (i, 0))
    return pl.pallas_call(
        body, name=name, grid=(r // tr,), in_specs=[spec, spec], out_specs=spec,
        out_shape=jax.ShapeDtypeStruct((r, c), a.dtype), compiler_params=_cp(("arbitrary",)),
    )(a, b)


BIG = ("w_in", "w_mlp1", "w_attn_out", "w_ssd_out", "w_o", "w_mlp2")
COL_SHARDED = ("w_mlp1", "w_in")
ROW_SHARDED = ("w_attn_out", "w_ssd_out", "w_o", "w_mlp2")
LATE = ROW_SHARDED + ("w_mlp1",)
SMALL = ("b_ada", "norm1_w", "norm2_w", "q_norm_w", "k_norm_w", "conv_b", "A_log", "dt_bias", "ssd_D", "ssd_norm_w")
NAMES = ("w_ada", "b_ada", "norm1_w", "norm2_w", "w_in", "q_norm_w", "k_norm_w", "conv_w", "conv_b", "A_log", "dt_bias",
         "ssd_D", "ssd_norm_w", "w_attn_out", "w_ssd_out", "w_o", "w_mlp1", "w_mlp2")
W_IN_COLS = 8768


def _permute_in(w):
    return jnp.concatenate([w[:, 4608:6656], w[:, 6720:8768], w[:, 1536:4608], w[:, 0:1536], w[:, 6656:6720],
                            jnp.zeros((w.shape[0], PW - W_IN_COLS), w.dtype)], axis=1)


def _unpermute_in(wp):
    return jnp.concatenate([wp[:, Q0:DT0], wp[:, XS0:Q0], wp[:, Z0:GA0], wp[:, DT0:DT0 + 64], wp[:, GA0:XS0]], axis=1)


def _pad_to(v, n):
    return jnp.pad(v, (0, n - v.shape[0]))


def _step(w, m, v, loss_target):
    xi, yi, ci = _place()
    chip = 2 * xi + yi
    dev = 4 * xi + 2 * yi + ci
    x, tgt = w["x"], loss_target
    d = x.shape[1]

    cw = w["conv_w"].shape[1]
    v0 = _pad_to(jnp.concatenate([w["c"].reshape(-1), w["conv_w"].reshape(-1)]), 5120).reshape(8, 640)
    g0 = _allgather8(v0, name="ag_cond").reshape(N_DEV, 5120)
    c_all = g0[:, 0:d]
    conv_w = jnp.concatenate([g0[2 * k, d:d + D_CONV * cw].reshape(D_CONV, cw) for k in range(N_CHIP)], axis=1)
    sc = _silu_cast(c_all, name="silu_c")
    modp = _mm(sc, w["w_ada"].astype(MMD), name="ada_fwd", outs=[F32], tm=8, tn=512)
    g1 = _allgather8(modp, name="ag_mod").reshape(N_DEV, N_DEV, modp.shape[1])
    mod_all = jnp.concatenate([g1[2 * k] for k in range(N_CHIP)], axis=1)
    mod = (lax.dynamic_slice_in_dim(mod_all, dev, 1, axis=0) + w["b_ada"]).reshape(6, d)

    mine, mod = lax.optimization_barrier((w["w_in"].astype(MMD), mod))
    gath = _gather_weights(mine, name="ag_w_in")
    late_mine = jnp.concatenate([w[n].astype(MMD) for n in LATE], axis=0)
    late_mine, gath = lax.optimization_barrier((late_mine, gath))
    ag_sems, ag_src, ag_land, ag_token = _chips_start(late_mine, False, name="ag_late_start")
    mod = mod + ag_token[0:1, 0:1]
    w_in = jnp.concatenate([jnp.where(chip == k, mine, gath[k]) for k in range(N_CHIP)], axis=1)
    wts = {"w_in_p": _permute_in(w_in), "w_dt": jnp.pad(w_in[:, 6656:6720], ((0, 0), (0, 64)))}
    small = {n: w[n] for n in SMALL if n != "b_ada"}
    small["conv_w"] = conv_w

    def late_weights(after):
        src, land = _chips_wait(ag_sems, ag_src, ag_land, after, False, name="ag_late_wait")
        out, o = {}, 0
        for n in LATE:
            rows = w[n].shape[0]
            parts = [jnp.where(chip == k, src[o:o + rows], land[k, o:o + rows]) for k in range(N_CHIP)]
            out[n] = jnp.concatenate(parts, axis=1 if n in COL_SHARDED else 0)
            o += rows
        return out

    def pair_sums(slots, tag):
        _, rows, cols = slots.shape
        hr = rows // 2
        theirs = _pair_swap(slots, name="rs_pair_" + tag)
        ours = lax.dynamic_slice_in_dim(slots, ci * hr, hr, axis=1)
        pair = _add2(ours.reshape(N_CHIP * hr, cols), theirs.reshape(N_CHIP * hr, cols), name="rs_pair_sum_" + tag)
        return pair.reshape(N_CHIP, hr, cols)

    def finish(recv, pair, tag):
        half = _sum_slots(recv, pair, name="rs_sum_" + tag)
        other = _sibling_copy(half, name="rs_sibling_" + tag)
        return jnp.where(ci == 0, jnp.concatenate([half, other], axis=0), jnp.concatenate([other, half], axis=0))

    started = {}

    def late_grads(gw):
        slots = []
        for k in range(N_CHIP):
            parts = []
            for n in LATE:
                rows = w[n].shape[0]
                blk = gw[n][:, k * rows:(k + 1) * rows] if n in COL_SHARDED else gw[n][k * rows:(k + 1) * rows]
                parts.append(blk.astype(MMD))
            slots.append(jnp.concatenate(parts, axis=0))
        pair = pair_sums(jnp.stack(slots), "late")
        sems, src, land, token = _chips_start(pair, True, name="rs_late_start")
        started["late"] = (sems, src, land)
        return token[0:1, 0:1]

    def in_grad(g):
        g_in = _unpermute_in(g)
        cols_in = w["w_in"].shape[1]
        pair = pair_sums(jnp.stack([g_in[:, k * cols_in:(k + 1) * cols_in].astype(MMD) for k in range(N_CHIP)]), "w_in")
        sems, src, land, token = _chips_start(pair, True, name="rs_w_in_start")
        started["w_in"] = (sems, src, land)
        return token

    loss, grad_x, dmod, gw, gs = _local_step(x, tgt, mod, wts, small, late_weights, late_grads, in_grad)

    grads = {}
    pair, land = _chips_wait(*started["w_in"], grad_x, True, name="rs_w_in_wait")
    grads["w_in"] = finish(land, pair, "w_in")
    pair, land = _chips_wait(*started["late"], grad_x, True, name="rs_late_wait")
    total, o = finish(land, pair, "late"), 0
    for n in LATE:
        rows = w[n].shape[0]
        grads[n] = total[o:o + rows]
        o += rows

    order = ([dmod.reshape(-1)] + [gs[n].reshape(-1) for n in SMALL if n != "b_ada"] + [gs["conv_w"].reshape(-1)]
             + [loss.reshape(-1)])
    vec = jnp.concatenate(order)
    n_small = vec.shape[0]
    n_pad = -(-n_small // 1024) * 1024
    g2 = _allgather8(_pad_to(vec, n_pad).reshape(8, n_pad // 8), name="ag_small")
    tot = _rows_sum(g2, N_DEV, name="small_sum").reshape(-1)
    loss = tot[n_small - 1]
    dmod_all = g2.reshape(N_DEV, n_pad)[:, 0:6 * d]
    off = 0
    for n in SMALL:
        grads[n] = tot[off:off + w[n].size].reshape(w[n].shape)
        off += w[n].size
    conv_full = tot[off:off + D_CONV * N_CHIP * cw].reshape(D_CONV, N_CHIP * cw)
    grads["conv_w"] = lax.dynamic_slice_in_dim(conv_full, chip * cw, cw, axis=1)
    ada_cols = w["w_ada"].shape[1]
    dmod_mine = lax.dynamic_slice_in_dim(dmod_all, chip * ada_cols, ada_cols, axis=1).astype(MMD)
    grads["w_ada"] = _mm_tn(sc, dmod_mine, name="ada_dw", tk=512, tn=512, tmm=8)

    delta, new_m, new_v = {}, {}, {}
    pack = lambda t: jnp.concatenate([t[n].reshape(-1) for n in SMALL]).reshape(1, -1)
    ds_, ms_, vs_ = _adamw(pack(w), pack(grads), pack(m), pack(v), name="adamw_small")
    off = 0
    for n in SMALL:
        for dst, src in ((delta, ds_), (new_m, ms_), (new_v, vs_)):
            dst[n] = src[0, off:off + w[n].size].reshape(w[n].shape)
        off += w[n].size
    for n in ("w_ada", "conv_w") + BIG:
        delta[n], new_m[n], new_v[n] = _adamw(w[n], grads[n], m[n], v[n], name="adamw_" + n)
    return loss, grad_x, grads, delta, new_m, new_v


def kernel(x, c, w_ada, b_ada, norm1_w, norm2_w, w_in, q_norm_w, k_norm_w, conv_w, conv_b, A_log, dt_bias, ssd_D, ssd_norm_w, w_attn_out, w_ssd_out, w_o, w_mlp1, w_mlp2, loss_target, m_w_ada, m_b_ada, m_norm1_w, m_norm2_w, m_w_in, m_q_norm_w, m_k_norm_w, m_conv_w, m_conv_b, m_A_log, m_dt_bias, m_ssd_D, m_ssd_norm_w, m_w_attn_out, m_w_ssd_out, m_w_o, m_w_mlp1, m_w_mlp2, v_w_ada, v_b_ada, v_norm1_w, v_norm2_w, v_w_in, v_q_norm_w, v_k_norm_w, v_conv_w, v_conv_b, v_A_log, v_dt_bias, v_ssd_D, v_ssd_norm_w, v_w_attn_out, v_w_ssd_out, v_w_o, v_w_mlp1, v_w_mlp2):
    args = dict(locals())
    strip = lambda a: a[0] if a.ndim == 3 else a
    w = {n: strip(args[n]) for n in NAMES + ("x", "c")}
    m = {n: strip(args["m_" + n]) for n in NAMES}
    v = {n: strip(args["v_" + n]) for n in NAMES}
    loss, grad_x, grads, delta, new_m, new_v = _step(w, m, v, loss_target[0])
    like = lambda t, n: t.reshape(args[n].shape)
    return (loss, grad_x[None], *[like(grads[n], n) for n in NAMES], *[like(delta[n], n) for n in NAMES],
            *[like(new_m[n], n) for n in NAMES], *[like(new_v[n], n) for n in NAMES])
```

```python
import math

import jax
import jax.numpy as jnp
from jax import lax
from jax.experimental import pallas as pl
from jax.experimental.pallas import tpu as pltpu

F32 = jnp.float32
MMD = jnp.bfloat16
EPS = 1e-6
NEG = -1e30
MIB = 1024 * 1024
VMEM_BIG = 56 * MIB
VMEM_MID = 40 * MIB

GRID_W = 64
N_Q_HEADS, N_KV_HEADS, HEAD_DIM = 16, 4, 64
ROPE_THETA = 10000.0
SSD_HEADS, SSD_GROUPS, SSD_P, SSD_N, CHUNK = 32, 4, 64, 128, 128
HPG = SSD_HEADS // SSD_GROUPS
D_CONV = 5
ADAM_LR, ADAM_B1, ADAM_B2, ADAM_EPS, ADAM_WD, ADAM_STEP = 0.001, 0.9, 0.999, 1e-08, 0.01, 10

Z0, GA0, GS0, XS0, B0, C0, Q0, K0, V0, DT0, PW = 0, 2048, 3072, 4096, 6144, 6656, 7168, 8192, 8448, 8704, 8832

MESH = pl.DeviceIdType.MESH
NT = (((1,), (1,)), ((), ()))
TN = (((0,), (0,)), ((), ()))


def _cp(sem=None, vmem=VMEM_MID):
    return pltpu.CompilerParams(dimension_semantics=sem, vmem_limit_bytes=vmem)


def _tile(n, pref):
    t = min(n, pref)
    while n % t:
        t //= 2
    return t


def _dot(a, b, dims=None):
    if dims is None:
        return jnp.dot(a, b, preferred_element_type=F32)
    return lax.dot_general(a, b, dims, preferred_element_type=F32)


def _dot_hi(a01, b):
    a = a01.astype(jnp.bfloat16)
    h1 = b.astype(jnp.bfloat16)
    r1 = b - h1.astype(F32)
    h2 = r1.astype(jnp.bfloat16)
    return _dot(a, h1) + _dot(a, h2) + _dot(a, (r1 - h2.astype(F32)).astype(jnp.bfloat16))


def _sigmoid(x):
    return jax.nn.sigmoid(x)


def _mm(a, b, *, name, outs, nt=False, ta=False, extras=(), epi=None, tm=512, tn=512, n=None, b_outer=False,
        vmem=VMEM_MID):
    assert not (nt and ta)
    k, m = a.shape if ta else a.shape[::-1]
    if n is None:
        n = b.shape[0] if nt else b.shape[1]
    tm, tn = _tile(m, tm), _tile(n, tn)
    gi, gj = m // tm, n // tn
    if b_outer:
        grid = (gj, gi)
        ij = lambda p, q: (q, p)
    else:
        grid = (gi, gj)
        ij = lambda p, q: (p, q)
    if ta:
        a_spec = pl.BlockSpec((k, tm), lambda p, q: (0, ij(p, q)[0]))
    else:
        a_spec = pl.BlockSpec((tm, k), lambda p, q: (ij(p, q)[0], 0))
    if nt:
        b_spec = pl.BlockSpec((tn, k), lambda p, q: (ij(p, q)[1], 0))
    else:
        b_spec = pl.BlockSpec((k, tn), lambda p, q: (0, ij(p, q)[1]))
    e_specs = []
    for arr, kind, off in extras:
        ob = off // tn
        assert off % tn == 0
        if kind == "tile":
            e_specs.append(pl.BlockSpec((tm, tn), lambda p, q, ob=ob: (ij(p, q)[0], ob + ij(p, q)[1])))
        else:
            e_specs.append(pl.BlockSpec((1, tn), lambda p, q, ob=ob: (0, ob + ij(p, q)[1])))
    ne = len(extras)

    def body(a_ref, b_ref, *rest):
        acc = _dot(a_ref[...], b_ref[...], NT if nt else (TN if ta else None))
        res = epi(acc, *[e[...] for e in rest[:ne]]) if epi is not None else (acc,)
        for o_ref, r in zip(rest[ne:], res):
            o_ref[...] = r.astype(o_ref.dtype)

    out = pl.pallas_call(
        body, name=name, grid=grid,
        in_specs=[a_spec, b_spec] + e_specs,
        out_specs=[pl.BlockSpec((tm, tn), lambda p, q: ij(p, q)) for _ in outs],
        out_shape=[jax.ShapeDtypeStruct((m, n), dt) for dt in outs],
        compiler_params=_cp(("arbitrary", "arbitrary"), vmem),
    )(a, b, *[e[0] for e in extras])
    return out if len(outs) > 1 else out[0]


def _mm_tn(a, g, *, name, tk=512, tn=1024, tmm=4096, vmem=VMEM_MID):
    m, k = a.shape
    n = g.shape[1]
    tk, tn, tmm = _tile(k, tk), _tile(n, tn), _tile(m, tmm)

    def body(a_ref, g_ref, o_ref):
        p = _dot(a_ref[...], g_ref[...], TN)

        @pl.when(pl.program_id(2) == 0)
        def _():
            o_ref[...] = p

        @pl.when(pl.program_id(2) > 0)
        def _():
            o_ref[...] += p

    return pl.pallas_call(
        body, name=name, grid=(k // tk, n // tn, m // tmm),
        in_specs=[pl.BlockSpec((tmm, tk), lambda i, j, r: (r, i)), pl.BlockSpec((tmm, tn), lambda i, j, r: (r, j))],
        out_specs=pl.BlockSpec((tk, tn), lambda i, j, r: (i, j)),
        out_shape=jax.ShapeDtypeStruct((k, n), F32),
        compiler_params=_cp(("arbitrary", "arbitrary", "arbitrary"), vmem),
    )(a, g)


def _adamw(w, g, m, v, *, name):
    r, c = w.shape
    tr = _tile(r, 256) if r % 8 == 0 else r

    def body(w_ref, g_ref, m_ref, v_ref, d_ref, nm_ref, nv_ref):
        gg = g_ref[...]
        nm = ADAM_B1 * m_ref[...] + (1.0 - ADAM_B1) * gg
        nv = ADAM_B2 * v_ref[...] + (1.0 - ADAM_B2) * jnp.square(gg)
        m_hat = nm / (1.0 - ADAM_B1 ** ADAM_STEP)
        v_hat = nv / (1.0 - ADAM_B2 ** ADAM_STEP)
        d_ref[...] = -ADAM_LR * (m_hat / (jnp.sqrt(v_hat) + ADAM_EPS) + ADAM_WD * w_ref[...])
        nm_ref[...] = nm
        nv_ref[...] = nv

    spec = pl.BlockSpec((tr, c), lambda i: (i, 0))
    return pl.pallas_call(
        body, name=name, grid=(r // tr,), in_specs=[spec] * 4, out_specs=[spec] * 3,
        out_shape=[jax.ShapeDtypeStruct((r, c), F32)] * 3, compiler_params=_cp(("arbitrary",)),
    )(w, g, m, v)


def _rows_sum(a, groups, *, name):
    r = a.shape[0] // groups

    def body(a_ref, o_ref):
        acc = a_ref[0:r, :]
        for d in range(1, groups):
            acc = acc + a_ref[d * r:(d + 1) * r, :]
        o_ref[...] = acc

    return pl.pallas_call(body, name=name, out_shape=jax.ShapeDtypeStruct((r, a.shape[1]), F32))(a)


def _silu_cast(a, *, name):
    def body(a_ref, o_ref):
        x = a_ref[...]
        o_ref[...] = (x * _sigmoid(x)).astype(o_ref.dtype)

    return pl.pallas_call(body, name=name, out_shape=jax.ShapeDtypeStruct(a.shape, MMD))(a)


def _sumsq(a, *, name):
    m, n = a.shape
    tm = _tile(m, 512)

    def body(a_ref, o_ref):
        x = a_ref[...]
        p = jnp.sum(jnp.sum(x * x, axis=1, keepdims=True), axis=0, keepdims=True)

        @pl.when(pl.program_id(0) == 0)
        def _():
            o_ref[...] = p

        @pl.when(pl.program_id(0) > 0)
        def _():
            o_ref[...] += p

    return pl.pallas_call(
        body, name=name, grid=(m // tm,), in_specs=[pl.BlockSpec((tm, n), lambda i: (i, 0))],
        out_specs=pl.BlockSpec((1, 1), lambda i: (0, 0)), out_shape=jax.ShapeDtypeStruct((1, 1), F32),
        compiler_params=_cp(("arbitrary",)),
    )(a)


def _acc_rows(o_ref, p, first):
    @pl.when(first)
    def _():
        o_ref[...] = p

    @pl.when(jnp.logical_not(first))
    def _():
        o_ref[...] += p


def _ln_mod(x, w, scale, shift, *, name):
    s, d = x.shape
    tm = _tile(s, 512)

    def body(x_ref, w_ref, sc_ref, sh_ref, o_ref):
        xv = x_ref[...]
        r = lax.rsqrt(jnp.mean(xv * xv, axis=-1, keepdims=True) + EPS)
        o_ref[...] = ((xv * r) * w_ref[...] * (1.0 + sc_ref[...]) + sh_ref[...]).astype(o_ref.dtype)

    row = pl.BlockSpec((1, d), lambda i: (0, 0))
    big = pl.BlockSpec((tm, d), lambda i: (i, 0))
    return pl.pallas_call(
        body, name=name, grid=(s // tm,), in_specs=[big, row, row, row], out_specs=big,
        out_shape=jax.ShapeDtypeStruct((s, d), MMD), compiler_params=_cp(("arbitrary",)),
    )(x, w, scale, shift)


def _ln_mod_bwd(dh, x, w, scale, dres, *, name):
    s, d = x.shape
    tm = _tile(s, 512)

    def body(dh_ref, x_ref, w_ref, sc_ref, dres_ref, dx_ref, dsh_ref, dsc_ref, dw_ref):
        xv = x_ref[...]
        dhv = dh_ref[...].astype(F32)
        r = lax.rsqrt(jnp.mean(xv * xv, axis=-1, keepdims=True) + EPS)
        nv = xv * r
        wv = w_ref[...]
        g1 = 1.0 + sc_ref[...]
        dn = dhv * (wv * g1)
        dx_ref[...] = dres_ref[...] + r * (dn - nv * jnp.mean(dn * nv, axis=-1, keepdims=True))
        first = pl.program_id(0) == 0
        _acc_rows(dsh_ref, jnp.sum(dhv, axis=0, keepdims=True), first)
        _acc_rows(dsc_ref, jnp.sum(dhv * nv * wv, axis=0, keepdims=True), first)
        _acc_rows(dw_ref, jnp.sum(dhv * nv * g1, axis=0, keepdims=True), first)

    row = pl.BlockSpec((1, d), lambda i: (0, 0))
    big = pl.BlockSpec((tm, d), lambda i: (i, 0))
    return pl.pallas_call(
        body, name=name, grid=(s // tm,), in_specs=[big, big, row, row, big], out_specs=[big, row, row, row],
        out_shape=[jax.ShapeDtypeStruct((s, d), F32)] + [jax.ShapeDtypeStruct((1, d), F32)] * 3,
        compiler_params=_cp(("arbitrary",)),
    )(dh, x, w, scale, dres)


def _gate_bwd(dy, u, gate, *, name):
    s, d = dy.shape
    tm = _tile(s, 512)

    def body(dy_ref, u_ref, g_ref, du_ref, dg_ref):
        dyv = dy_ref[...]
        du_ref[...] = (dyv * g_ref[...]).astype(du_ref.dtype)
        _acc_rows(dg_ref, jnp.sum(dyv * u_ref[...].astype(F32), axis=0, keepdims=True), pl.program_id(0) == 0)

    row = pl.BlockSpec((1, d), lambda i: (0, 0))
    big = pl.BlockSpec((tm, d), lambda i: (i, 0))
    return pl.pallas_call(
        body, name=name, grid=(s // tm,), in_specs=[big, big, row], out_specs=[big, row],
        out_shape=[jax.ShapeDtypeStruct((s, d), MMD), jax.ShapeDtypeStruct((1, d), F32)],
        compiler_params=_cp(("arbitrary",)),
    )(dy, u, gate)


def _seg64(v, e):
    hi = v.astype(jnp.bfloat16)
    lo = (v - hi.astype(F32)).astype(jnp.bfloat16)
    return _dot(hi, e) + _dot(lo, e)


def _rope_tables(s):
    rows = s // GRID_W
    pos_row = jnp.repeat(jnp.arange(rows, dtype=jnp.int32), GRID_W).astype(F32)
    pos_col = jnp.tile(jnp.arange(GRID_W, dtype=jnp.int32), rows).astype(F32)
    axis_dim = HEAD_DIM // 2
    inv_freq = ROPE_THETA ** (-jnp.arange(0, axis_dim, 2, dtype=F32) / axis_dim)
    ang_r = pos_row[:, None] * inv_freq[None, :]
    ang_c = pos_col[:, None] * inv_freq[None, :]
    zero = jnp.zeros_like(ang_r)
    cos = jnp.concatenate([jnp.cos(ang_r), jnp.cos(ang_r), jnp.cos(ang_c), jnp.cos(ang_c)], axis=1)
    s_a = jnp.concatenate([-jnp.sin(ang_r), zero, -jnp.sin(ang_c), zero], axis=1)
    s_b = jnp.concatenate([zero, jnp.sin(ang_r), zero, jnp.sin(ang_c)], axis=1)
    return [jnp.tile(t, (1, 2)) for t in (cos, s_a, s_b)]


def _e128():
    i = jnp.arange(128)
    return (i[:, None] // 64 == i[None, :] // 64).astype(jnp.bfloat16)


QKW = N_Q_HEADS * HEAD_DIM + N_KV_HEADS * HEAD_DIM


def _qk_fwd(proj, wrow, scrow, tabs, *, name):
    s = proj.shape[0]
    tm = _tile(s, 1024)

    def body(x_ref, w_ref, sc_ref, cos_ref, sa_ref, sb_ref, e_ref, o_ref, ot_ref):
        u = x_ref[...].astype(F32)
        r = lax.rsqrt(_seg64(u * u, e_ref[...]) * (1.0 / HEAD_DIM) + EPS)
        nv = (u * r) * w_ref[...]
        ro = nv * cos_ref[...] + pltpu.roll(nv, 112, 1) * sa_ref[...] + pltpu.roll(nv, 16, 1) * sb_ref[...]
        out = ro * sc_ref[...]
        o_ref[...] = out.astype(o_ref.dtype)
        ot_ref[...] = out.T.astype(ot_ref.dtype)

    tab = pl.BlockSpec((tm, 128), lambda i, j: (i, 0))
    row = pl.BlockSpec((1, 128), lambda i, j: (0, j))
    return pl.pallas_call(
        body, name=name, grid=(s // tm, QKW // 128),
        in_specs=[pl.BlockSpec((tm, 128), lambda i, j: (i, Q0 // 128 + j)), row, row, tab, tab, tab,
                  pl.BlockSpec((128, 128), lambda i, j: (0, 0))],
        out_specs=[pl.BlockSpec((tm, 128), lambda i, j: (i, j)), pl.BlockSpec((128, tm), lambda i, j: (j, i))],
        out_shape=[jax.ShapeDtypeStruct((s, QKW), MMD), jax.ShapeDtypeStruct((QKW, s), MMD)],
        compiler_params=_cp(("arbitrary", "arbitrary")),
    )(proj, wrow, scrow, *tabs, _e128())


def _qk_bwd(dqt, dkt, proj, wrow, scrow, tabs, dproj, *, name):
    s = proj.shape[0]
    tm = _tile(s, 1024)
    nq = dqt.shape[0] // 128

    def body(dq_ref, dk_ref, x_ref, w_ref, sc_ref, cos_ref, sa_ref, sb_ref, e_ref, _, du_ref, dw_ref):
        e = e_ref[...]
        d = jnp.where(pl.program_id(0) < nq, dq_ref[...], dk_ref[...]).T * sc_ref[...]
        dn = d * cos_ref[...] + pltpu.roll(d * sa_ref[...], 16, 1) + pltpu.roll(d * sb_ref[...], 112, 1)
        u = x_ref[...].astype(F32)
        r = lax.rsqrt(_seg64(u * u, e) * (1.0 / HEAD_DIM) + EPS)
        uh = u * r
        _acc_rows(dw_ref, jnp.sum(dn * uh, axis=0, keepdims=True), pl.program_id(1) == 0)
        dnw = dn * w_ref[...]
        du_ref[...] = (r * (dnw - uh * (_seg64(dnw * uh, e) * (1.0 / HEAD_DIM)))).astype(du_ref.dtype)

    tab = pl.BlockSpec((tm, 128), lambda j, i: (i, 0))
    row = pl.BlockSpec((1, 128), lambda j, i: (0, j))
    qcol = pl.BlockSpec((tm, 128), lambda j, i: (i, Q0 // 128 + j))
    return pl.pallas_call(
        body, name=name, grid=(QKW // 128, s // tm),
        in_specs=[pl.BlockSpec((128, tm), lambda j, i: (jnp.minimum(j, nq - 1), i)),
                  pl.BlockSpec((128, tm), lambda j, i: (jnp.maximum(j - nq, 0), i)),
                  qcol, row, row, tab, tab, tab, pl.BlockSpec((128, 128), lambda j, i: (0, 0)), ANY],
        out_specs=[qcol, row],
        out_shape=[jax.ShapeDtypeStruct(dproj.shape, dproj.dtype), jax.ShapeDtypeStruct((1, QKW), F32)],
        input_output_aliases={9: 0}, compiler_params=_cp(("arbitrary", "arbitrary")),
    )(dqt, dkt, proj, wrow, scrow, *tabs, _e128(), dproj)


REP = N_Q_HEADS // N_KV_HEADS


def _lanes(ref):
    return jnp.concatenate([ref[r] for r in range(REP)], axis=1)


V_AUG = HEAD_DIM + 8
LOG2E = math.log2(math.e)


def _flash_fwd(qkt, vta, *, name):
    s = qkt.shape[2]
    tq, tk = _tile(s, 1024), _tile(s, 512)
    nk = s // tk
    lanes = REP * tq

    def body(q_ref, k_ref, v_ref, o_ref, lse_ref, m_ref, acc_ref):
        j = pl.program_id(2)

        @pl.when(j == 0)
        def _():
            m_ref[...] = jnp.full_like(m_ref, NEG)
            acc_ref[...] = jnp.zeros_like(acc_ref)

        st = _dot(k_ref[0], _lanes(q_ref), TN)
        m_prev = m_ref[...]
        m_new = jnp.maximum(m_prev, jnp.max(st, axis=0, keepdims=True))
        p = jnp.exp2(st - m_new).astype(MMD)
        acc_ref[...] = jnp.exp2(m_prev - m_new) * acc_ref[...] + _dot(v_ref[0], p)
        m_ref[...] = m_new

        @pl.when(j == nk - 1)
        def _():
            acc = acc_ref[...]
            l = acc[HEAD_DIM:HEAD_DIM + 1]
            o = acc[0:HEAD_DIM] / l
            ls = m_ref[...] + jnp.log(l) * LOG2E
            for r in range(REP):
                o_ref[r] = o[:, r * tq:(r + 1) * tq].astype(o_ref.dtype)
                lse_ref[r] = ls[:, r * tq:(r + 1) * tq]

    qspec = pl.BlockSpec((REP, HEAD_DIM, tq), lambda g, i, j: (g, 0, i))
    return pl.pallas_call(
        body, name=name, grid=(N_KV_HEADS, s // tq, nk),
        in_specs=[qspec, pl.BlockSpec((1, HEAD_DIM, tk), lambda g, i, j: (N_Q_HEADS + g, 0, j)),
                  pl.BlockSpec((1, V_AUG, tk), lambda g, i, j: (g, 0, j))],
        out_specs=[qspec, pl.BlockSpec((REP, 1, tq), lambda g, i, j: (g, 0, i))],
        out_shape=[jax.ShapeDtypeStruct((N_Q_HEADS, HEAD_DIM, s), MMD), jax.ShapeDtypeStruct((N_Q_HEADS, 1, s), F32)],
        scratch_shapes=[pltpu.VMEM((1, lanes), F32), pltpu.VMEM((V_AUG, lanes), F32)],
        compiler_params=_cp(("arbitrary", "arbitrary", "arbitrary"), VMEM_BIG),
    )(qkt, qkt, vta)


def _flash_bwd(qkt, k_h, v_h, dot, ot, lse, *, name):
    s = qkt.shape[2]
    tq, tk = _tile(s, 512), _tile(s, 1024)
    nk = s // tk

    def body(q_ref, kt_ref, k_ref, v_ref, do_ref, o_ref, lse_ref, dq_ref, dk_ref, dv_ref, dq_acc):
        i, j = pl.program_id(1), pl.program_id(2)
        q, do = _lanes(q_ref), _lanes(do_ref)
        delta = jnp.sum(do.astype(F32) * _lanes(o_ref).astype(F32), axis=0, keepdims=True)
        k, v = k_ref[0], v_ref[0]
        p = jnp.exp2(_dot(k, q) - _lanes(lse_ref))
        dvc = _dot(p.astype(MMD), do, NT)
        ds = (p * (_dot(v, do) - delta)).astype(MMD)
        dkc = _dot(ds, q, NT) * (1.0 / LOG2E)
        dqc = _dot(kt_ref[0], ds)
        rows = pl.ds(pl.multiple_of(j * tk, tk), tk)

        @pl.when(i == 0)
        def _():
            dk_ref[0, rows, :] = dkc
            dv_ref[0, rows, :] = dvc

        @pl.when(i > 0)
        def _():
            dk_ref[0, rows, :] += dkc
            dv_ref[0, rows, :] += dvc

        @pl.when(j == 0)
        def _():
            dq_acc[...] = dqc

        @pl.when(j > 0)
        def _():
            dq_acc[...] += dqc

        @pl.when(j == nk - 1)
        def _():
            acc = dq_acc[...]
            for r in range(REP):
                dq_ref[r] = acc[:, r * tq:(r + 1) * tq]

    qspec = pl.BlockSpec((REP, HEAD_DIM, tq), lambda g, i, j: (g, 0, i))
    kvin = pl.BlockSpec((1, tk, HEAD_DIM), lambda g, i, j: (g, j, 0))
    kvres = pl.BlockSpec((1, s, HEAD_DIM), lambda g, i, j: (g, 0, 0))
    return pl.pallas_call(
        body, name=name, grid=(N_KV_HEADS, s // tq, nk),
        in_specs=[qspec, pl.BlockSpec((1, HEAD_DIM, tk), lambda g, i, j: (N_Q_HEADS + g, 0, j)), kvin, kvin,
                  qspec, qspec, pl.BlockSpec((REP, 1, tq), lambda g, i, j: (g, 0, i))],
        out_specs=[qspec, kvres, kvres],
        out_shape=[jax.ShapeDtypeStruct((N_Q_HEADS, HEAD_DIM, s), F32), jax.ShapeDtypeStruct((N_KV_HEADS, s, HEAD_DIM), F32),
                   jax.ShapeDtypeStruct((N_KV_HEADS, s, HEAD_DIM), F32)],
        scratch_shapes=[pltpu.VMEM((HEAD_DIM, REP * tq), F32)],
        compiler_params=_cp(("arbitrary", "arbitrary", "arbitrary"), VMEM_BIG),
    )(qkt, qkt, k_h, v_h, dot, ot, lse)


HALO = 8
CONV_W = 2048 + 2 * SSD_GROUPS * SSD_N


def _shifted(win, off, r):
    return pltpu.roll(win, (r + 2 * HALO - off) % (r + 2 * HALO), 0)[0:r]


def _conv_fwd(proj, w8, brow, *, name):
    s = proj.shape[0]
    cb = 256
    r = _tile(s, 512)

    def body(x_ref, w_ref, b_ref, o_ref, pad_ref):
        zeros = jnp.zeros((HALO, cb), F32)
        pad_ref[0:HALO, :] = zeros
        pad_ref[s + HALO:s + 2 * HALO, :] = zeros

        def fill(i, carry):
            st = pl.multiple_of(i * r, r)
            pad_ref[pl.ds(st + HALO, r), :] = x_ref[pl.ds(st, r), :].astype(F32)
            return carry

        lax.fori_loop(0, s // r, fill, 0)
        wv = w_ref[...]
        bv = b_ref[...]

        def step(i, carry):
            st = pl.multiple_of(i * r, r)
            win = pad_ref[pl.ds(st, r + 2 * HALO), :]
            acc = bv + wv[0:1, :] * _shifted(win, HALO - 2, r)
            for t in range(1, D_CONV):
                acc = acc + wv[t:t + 1, :] * _shifted(win, HALO - 2 + t, r)
            o_ref[pl.ds(st, r), :] = (acc * _sigmoid(acc)).astype(o_ref.dtype)
            return carry

        lax.fori_loop(0, s // r, step, 0)

    return pl.pallas_call(
        body, name=name, grid=(CONV_W // cb,),
        in_specs=[pl.BlockSpec((s, cb), lambda j: (0, XS0 // cb + j)), pl.BlockSpec((8, cb), lambda j: (0, j)),
                  pl.BlockSpec((1, cb), lambda j: (0, j))],
        out_specs=pl.BlockSpec((s, cb), lambda j: (0, j)),
        out_shape=jax.ShapeDtypeStruct((s, CONV_W), MMD),
        scratch_shapes=[pltpu.VMEM((s + 2 * HALO, cb), F32)],
        compiler_params=_cp(("arbitrary",), VMEM_MID),
    )(proj, w8, brow)


def _conv_bwd(proj, col0, ga, gb, w8, brow, dproj, *, name):
    s = proj.shape[0]
    width = ga.shape[1]
    cb = 128
    c0 = col0 // cb
    r = _tile(s, 512)

    def body(x_ref, ga_ref, gb_ref, w_ref, b_ref, _, dx_ref, dw_ref, db_ref, xpad, dpad):
        zeros = jnp.zeros((HALO, cb), F32)
        for ref in (xpad, dpad):
            ref[0:HALO, :] = zeros
            ref[s + HALO:s + 2 * HALO, :] = zeros

        def fill(i, carry):
            st = pl.multiple_of(i * r, r)
            xpad[pl.ds(st + HALO, r), :] = x_ref[pl.ds(st, r), :].astype(F32)
            return carry

        lax.fori_loop(0, s // r, fill, 0)
        wv = w_ref[...]
        bv = b_ref[...]

        def first(i, carry):
            st = pl.multiple_of(i * r, r)
            win = xpad[pl.ds(st, r + 2 * HALO), :]
            taps = [_shifted(win, HALO - 2 + t, r) for t in range(D_CONV)]
            u = bv
            for t in range(D_CONV):
                u = u + wv[t:t + 1, :] * taps[t]
            sg = _sigmoid(u)
            du = ((ga_ref[pl.ds(st, r), :].astype(F32) + gb_ref[pl.ds(st, r), :].astype(F32))
                  * (sg * (1.0 + u * (1.0 - sg))))
            dpad[pl.ds(st + HALO, r), :] = du
            out = [carry[0] + jnp.sum(du, axis=0, keepdims=True)]
            for t in range(D_CONV):
                out.append(carry[1 + t] + jnp.sum(du * taps[t], axis=0, keepdims=True))
            return tuple(out)

        sums = lax.fori_loop(0, s // r, first, tuple(jnp.zeros((1, cb), F32) for _ in range(1 + D_CONV)))
        db_ref[...] = sums[0]
        for t in range(D_CONV):
            dw_ref[t:t + 1, :] = sums[1 + t]
        dw_ref[D_CONV:8, :] = jnp.zeros((8 - D_CONV, cb), F32)

        def second(i, carry):
            st = pl.multiple_of(i * r, r)
            win = dpad[pl.ds(st, r + 2 * HALO), :]
            acc = wv[0:1, :] * _shifted(win, HALO + 2, r)
            for t in range(1, D_CONV):
                acc = acc + wv[t:t + 1, :] * _shifted(win, HALO + 2 - t, r)
            dx_ref[pl.ds(st, r), :] = acc.astype(dx_ref.dtype)
            return carry

        lax.fori_loop(0, s // r, second, 0)

    col = pl.BlockSpec((s, cb), lambda j: (0, j))
    xcol = pl.BlockSpec((s, cb), lambda j: (0, XS0 // cb + c0 + j))
    return pl.pallas_call(
        body, name=name, grid=(width // cb,),
        in_specs=[xcol, col, col, pl.BlockSpec((8, cb), lambda j: (0, c0 + j)),
                  pl.BlockSpec((1, cb), lambda j: (0, c0 + j)), ANY],
        out_specs=[xcol, pl.BlockSpec((8, cb), lambda j: (0, j)), pl.BlockSpec((1, cb), lambda j: (0, j))],
        out_shape=[jax.ShapeDtypeStruct(dproj.shape, dproj.dtype), jax.ShapeDtypeStruct((8, width), F32),
                   jax.ShapeDtypeStruct((1, width), F32)],
        scratch_shapes=[pltpu.VMEM((s + 2 * HALO, cb), F32), pltpu.VMEM((s + 2 * HALO, cb), F32)],
        input_output_aliases={5: 0}, compiler_params=_cp(("arbitrary",), VMEM_BIG),
    )(proj, ga, gb, w8, brow, dproj)


def _tri(lower):
    i = jnp.arange(CHUNK)
    return ((i[:, None] >= i[None, :]) if lower else (i[:, None] <= i[None, :])).astype(F32)


def _dt_fwd(raw, bias, arow, *, name):
    s = raw.shape[0]

    def body(r_ref, b_ref, a_ref, lo_ref, up_ref, dt_ref, cs_ref):
        u = r_ref[...] + b_ref[...]
        dt = jnp.maximum(u, 0.0) + jnp.log1p(jnp.exp(-jnp.abs(u)))
        dt_ref[...] = dt
        a = dt * a_ref[...]
        lane = lax.broadcasted_iota(jnp.int32, (CHUNK, 128), 1)
        cs_ref[...] = jnp.where(lane < SSD_HEADS, _dot_hi(lo_ref[...], a), _dot_hi(up_ref[...], a))

    blk = pl.BlockSpec((CHUNK, 128), lambda i: (i, 0))
    row = pl.BlockSpec((1, 128), lambda i: (0, 0))
    tri = pl.BlockSpec((CHUNK, CHUNK), lambda i: (0, 0))
    return pl.pallas_call(
        body, name=name, grid=(s // CHUNK,), in_specs=[blk, row, row, tri, tri], out_specs=[blk, blk],
        out_shape=[jax.ShapeDtypeStruct((s, 128), F32)] * 2, compiler_params=_cp(("arbitrary",)),
    )(raw, bias, arow, _tri(True), _tri(False))


def _dt_bwd(ddt0, ddt1, raw, bias, dproj, *, name):
    s = raw.shape[0]
    tm = _tile(s, 1024)

    def body(d0_ref, d1_ref, r_ref, b_ref, _, o_ref, db_ref):
        g = (d0_ref[...] + d1_ref[...]) * _sigmoid(r_ref[...] + b_ref[...])
        o_ref[...] = g.astype(o_ref.dtype)
        _acc_rows(db_ref, jnp.sum(g, axis=0, keepdims=True), pl.program_id(0) == 0)

    blk = pl.BlockSpec((tm, 128), lambda i: (i, 0))
    row = pl.BlockSpec((1, 128), lambda i: (0, 0))
    return pl.pallas_call(
        body, name=name, grid=(s // tm,), in_specs=[blk, blk, blk, row, ANY],
        out_specs=[pl.BlockSpec((tm, 128), lambda i: (i, DT0 // 128)), row],
        out_shape=[jax.ShapeDtypeStruct(dproj.shape, dproj.dtype), jax.ShapeDtypeStruct((1, 128), F32)],
        input_output_aliases={4: 0}, compiler_params=_cp(("arbitrary",)),
    )(ddt0, ddt1, raw, bias, dproj)


GW = HPG * SSD_P


GPS = SSD_GROUPS


def _ssd_specs(nc, rev):
    cc = (lambda c: nc - 1 - c) if rev else (lambda c: c)
    return dict(
        x=pl.BlockSpec((CHUNK, GPS * GW), lambda g, c: (cc(c), g)),
        b=pl.BlockSpec((CHUNK, GPS * SSD_N), lambda g, c: (cc(c), 2048 // (GPS * SSD_N) + g)),
        c=pl.BlockSpec((CHUNK, GPS * SSD_N), lambda g, c: (cc(c), 2048 // (GPS * SSD_N) + 1 + g)),
        lanes=pl.BlockSpec((CHUNK, 128), lambda g, c: (cc(c), 0)),
        drow=pl.BlockSpec((1, GPS * GW), lambda g, c: (0, g)),
        y=pl.BlockSpec((CHUNK, GPS * GW), lambda g, c: (cc(c), g)),
        h=pl.BlockSpec((GPS, 1, SSD_N, GW), lambda g, c: (g, cc(c), 0, 0)),
        n=pl.BlockSpec((CHUNK, GPS * SSD_N), lambda g, c: (cc(c), g)),
    )


def _ssd_mask(anti):
    ii = lax.broadcasted_iota(jnp.int32, (CHUNK, CHUNK), 0)
    jj = lax.broadcasted_iota(jnp.int32, (CHUNK, CHUNK), 1)
    return ii, jj, (ii <= jj) if anti else (ii >= jj)


def _expand(x, ex, terms=3):
    h1 = x.astype(jnp.bfloat16)
    r1 = x - h1.astype(F32)
    h2 = r1.astype(jnp.bfloat16)
    out = _dot(h1, ex) + _dot(h2, ex)
    if terms == 3:
        out = out + _dot((r1 - h2.astype(F32)).astype(jnp.bfloat16), ex)
    return out


def _headsum(a, e):
    hi = a.astype(jnp.bfloat16)
    return _dot(hi, e) + _dot((a - hi.astype(F32)).astype(jnp.bfloat16), e)


def _expand_mats():
    lane = jnp.arange(128)[None, :, None]
    col = jnp.arange(GW)[None, None, :]
    base = (jnp.arange(2)[:, None] * SSD_HEADS + jnp.arange(SSD_GROUPS)[None, :] * HPG).reshape(2 * SSD_GROUPS, 1, 1)
    return (lane == base + col // SSD_P).astype(jnp.bfloat16)


def _headsum_mats():
    e1 = (jnp.arange(GW)[:, None] // SSD_P == jnp.arange(128)[None, :]).astype(jnp.bfloat16)
    e2 = (jnp.arange(HPG * CHUNK)[:, None] // CHUNK == jnp.arange(128)[None, :]).astype(jnp.bfloat16)
    return e1, e2


def _ssd_fwd(xc, dt, cs, ex, drow, di, *, name):
    s = xc.shape[0]
    nc = s // CHUNK
    anti = di == 1
    sp = _ssd_specs(nc, anti)
    trow = 0 if anti else CHUNK - 1

    def body(x_ref, b_ref, c_ref, dt_ref, cs_ref, ex_ref, d_ref, y_ref, hp_ref, h_ref):
        @pl.when(pl.program_id(1) == 0)
        def _():
            h_ref[...] = jnp.zeros_like(h_ref)

        mask = _ssd_mask(anti)[2]
        dtv, csv = dt_ref[...], cs_ref[...]
        cst = csv.T
        for gi in range(GPS):
            cols = slice(gi * GW, (gi + 1) * GW)
            ncols = slice(gi * SSD_N, (gi + 1) * SSD_N)
            ex = ex_ref[gi]
            xb = x_ref[:, cols].astype(F32)
            bm, cm = b_ref[:, ncols], c_ref[:, ncols]
            csr = cst[SSD_HEADS * di + HPG * gi:SSD_HEADS * di + HPG * (gi + 1)]
            dtf = _expand(dtv, ex, 2)
            csf = _expand(csv, ex)
            tl = csf[trow:trow + 1, :]
            h = h_ref[gi]
            hp_ref[gi, 0] = h.astype(hp_ref.dtype)
            g = _dot(cm, bm, NT)
            xs = xb * dtf
            xsm = xs.astype(MMD)
            base = jnp.exp(csf) * _dot(cm, h.astype(MMD)) + d_ref[:, cols] * xb
            for r in range(HPG):
                sl = slice(r * SSD_P, (r + 1) * SSD_P)
                lm = jnp.exp(jnp.where(mask, csf[:, r * SSD_P:r * SSD_P + 1] - csr[r:r + 1, :], NEG))
                y_ref[:, gi * GW + r * SSD_P:gi * GW + (r + 1) * SSD_P] = (
                    _dot((g * lm).astype(MMD), xsm[:, sl]) + base[:, sl]).astype(y_ref.dtype)
            xd = (xs * jnp.exp(tl - csf)).astype(MMD)
            h_ref[gi] = h * jnp.exp(tl) + _dot(bm, xd, TN)

    return pl.pallas_call(
        body, name=name, grid=(1, nc),
        in_specs=[sp["x"], sp["b"], sp["c"], sp["lanes"], sp["lanes"],
                  pl.BlockSpec((GPS, 128, GW), lambda g, c: (di, 0, 0)), sp["drow"]],
        out_specs=[sp["y"], sp["h"]],
        out_shape=[jax.ShapeDtypeStruct((s, 2048), MMD), jax.ShapeDtypeStruct((SSD_GROUPS, nc, SSD_N, GW), MMD)],
        scratch_shapes=[pltpu.VMEM((GPS, SSD_N, GW), F32)],
        compiler_params=_cp(("arbitrary", "arbitrary")),
    )(xc, xc, xc, dt, cs, ex, drow)


def _ssd_bwd(xc, dt, cs, ex, drow, arow, dy, hprev, di, *, name):
    s = xc.shape[0]
    nc = s // CHUNK
    anti = di == 1
    sp = _ssd_specs(nc, not anti)
    trow = 0 if anti else CHUNK - 1
    e1, e2 = _headsum_mats()

    def body(x_ref, b_ref, c_ref, dt_ref, cs_ref, ex_ref, d_ref, a_ref, dy_ref, hp_ref, tri_ref,
             e1_ref, e2_ref, dx_ref, db_ref, dc_ref, ddt_ref, da_ref, dh_ref, w_ref, dxs_ref):
        @pl.when(pl.program_id(1) == 0)
        def _():
            dh_ref[...] = jnp.zeros_like(dh_ref)
            da_ref[...] = jnp.zeros_like(da_ref)

        e1v = e1_ref[...]
        ii, _, mask = _ssd_mask(anti)
        dtv, csv = dt_ref[...], cs_ref[...]
        cst = csv.T
        ddt_acc = jnp.zeros((CHUNK, 128), F32)
        da_acc = jnp.zeros((1, 128), F32)
        for gi in range(GPS):
            lane0 = SSD_HEADS * di + HPG * gi
            cols = slice(gi * GW, (gi + 1) * GW)
            ncols = slice(gi * SSD_N, (gi + 1) * SSD_N)
            ex = ex_ref[gi]
            xb = x_ref[:, cols].astype(F32)
            bm, cm = b_ref[:, ncols], c_ref[:, ncols]
            csr = cst[lane0:lane0 + HPG]
            dym = dy_ref[:, cols]
            dyb = dym.astype(F32)
            hpm = hp_ref[gi, 0]
            hp = hpm.astype(F32)
            dh = dh_ref[gi]
            dhm = dh.astype(MMD)
            dtf = _expand(dtv, ex, 2)
            csf = _expand(csv, ex)
            tl = csf[trow:trow + 1, :]
            e = jnp.exp(csf)
            dec = jnp.exp(tl - csf)
            et = jnp.exp(tl)
            xs = xb * dtf
            xsm = xs.astype(MMD)
            g = _dot(cm, bm, NT)
            z = _dot(cm, hpm)
            bdh = _dot(bm, dhm)
            dg = jnp.zeros((CHUNK, CHUNK), F32)
            wcols = jnp.zeros((CHUNK, CHUNK), F32)
            for r in range(HPG):
                sl = slice(r * SSD_P, (r + 1) * SSD_P)
                lm = jnp.exp(jnp.where(mask, csf[:, r * SSD_P:r * SSD_P + 1] - csr[r:r + 1, :], NEG))
                mm = g * lm
                dm = _dot(dym[:, sl], xsm[:, sl], NT)
                w = dm * mm
                w_ref[gi, :, r * CHUNK:(r + 1) * CHUNK] = w
                wcols = jnp.where(ii == r, jnp.sum(w, axis=0, keepdims=True), wcols)
                dg = dg + dm * lm
                dxs_ref[gi, :, sl] = _dot(mm.astype(MMD), dym[:, sl], TN)
            dxs = dxs_ref[gi] + dec * bdh
            dx_ref[:, cols] = (dxs * dtf + d_ref[:, cols] * dyb).astype(dx_ref.dtype)
            tb = xs * bdh * dec
            d_tot = jnp.sum(tb, axis=0, keepdims=True) + et * jnp.sum(dh * hp, axis=0, keepdims=True)
            d_tot = _headsum(jnp.broadcast_to(d_tot, (8, GW)), e1v)[0:1]
            dcs = (_headsum(dyb * (e * z) - tb, e1v) + _headsum(w_ref[gi], e2_ref[...]) - wcols.T
                   + jnp.where(ii == trow, d_tot, 0.0))
            da = pltpu.roll(_dot_hi(tri_ref[...], dcs), lane0, 1)
            ddt_acc = ddt_acc + da * a_ref[...] + pltpu.roll(_headsum(dxs * xb, e1v), lane0, 1)
            da_acc = da_acc + jnp.sum(da * dtv, axis=0, keepdims=True)
            dgm = dg.astype(MMD)
            dz = (e * dyb).astype(MMD)
            dc_ref[:, ncols] = (_dot(dgm, bm) + _dot(dz, hpm, NT)).astype(dc_ref.dtype)
            db_ref[:, ncols] = (_dot(dgm, cm, TN) + _dot((xs * dec).astype(MMD), dhm, NT)).astype(db_ref.dtype)
            dh_ref[gi] = dh * et + _dot(cm, dz, TN)
        ddt_ref[...] = ddt_acc
        da_ref[...] += da_acc

    const = lambda shape: pl.BlockSpec(shape, lambda g, c: (0,) * len(shape))
    return pl.pallas_call(
        body, name=name, grid=(1, nc),
        in_specs=[sp["x"], sp["b"], sp["c"], sp["lanes"], sp["lanes"],
                  pl.BlockSpec((GPS, 128, GW), lambda g, c: (di, 0, 0)), sp["drow"],
                  const((1, 128)), sp["y"], sp["h"],
                  const((CHUNK, CHUNK)), const((GW, 128)), const((HPG * CHUNK, 128))],
        out_specs=[sp["y"], sp["n"], sp["n"], sp["lanes"], const((1, 128))],
        out_shape=[jax.ShapeDtypeStruct((s, 2048), MMD), jax.ShapeDtypeStruct((s, SSD_GROUPS * SSD_N), MMD),
                   jax.ShapeDtypeStruct((s, SSD_GROUPS * SSD_N), MMD), jax.ShapeDtypeStruct((s, 128), F32),
                   jax.ShapeDtypeStruct((1, 128), F32)],
        scratch_shapes=[pltpu.VMEM((GPS, SSD_N, GW), F32), pltpu.VMEM((GPS, CHUNK, HPG * CHUNK), F32),
                        pltpu.VMEM((GPS, CHUNK, GW), F32)],
        compiler_params=_cp(("arbitrary", "arbitrary")),
    )(xc, xc, xc, dt, cs, ex, drow, arow, dy, hprev, _tri(anti), e1, e2)


def _gnorm_fwd(ya, yb, proj, w, *, name):
    s = ya.shape[0]
    tm = _tile(s, 256)

    def body(a_ref, b_ref, z_ref, w_ref, o_ref):
        zv = z_ref[...].astype(F32)
        t = (a_ref[...].astype(F32) + b_ref[...].astype(F32)) * (zv * _sigmoid(zv))
        r = lax.rsqrt(jnp.mean(t * t, axis=-1, keepdims=True) + EPS)
        o_ref[...] = ((t * r) * w_ref[...]).astype(o_ref.dtype)

    big = pl.BlockSpec((tm, 2048), lambda i: (i, 0))
    row = pl.BlockSpec((1, 2048), lambda i: (0, 0))
    return pl.pallas_call(
        body, name=name, grid=(s // tm,), in_specs=[big, big, big, row], out_specs=big,
        out_shape=jax.ShapeDtypeStruct((s, 2048), MMD), compiler_params=_cp(("arbitrary",)),
    )(ya, yb, proj, w)


def _gnorm_bwd(dout, ya, yb, proj, xc, w, dproj, *, name):
    s = ya.shape[0]
    tm = _tile(s, 256)

    def body(do_ref, a_ref, b_ref, z_ref, x_ref, w_ref, _, dy_ref, dz_ref, dw_ref, dd_ref):
        zv = z_ref[...].astype(F32)
        sg = _sigmoid(zv)
        sz = zv * sg
        y = a_ref[...].astype(F32) + b_ref[...].astype(F32)
        t = y * sz
        r = lax.rsqrt(jnp.mean(t * t, axis=-1, keepdims=True) + EPS)
        nv = t * r
        dov = do_ref[...].astype(F32)
        _acc_rows(dw_ref, jnp.sum(dov * nv, axis=0, keepdims=True), pl.program_id(0) == 0)
        dn = dov * w_ref[...]
        dt_ = r * (dn - nv * jnp.mean(dn * nv, axis=-1, keepdims=True))
        dy = dt_ * sz
        dy_ref[...] = dy.astype(dy_ref.dtype)
        dz_ref[...] = (dt_ * y * (sg * (1.0 + zv * (1.0 - sg)))).astype(dz_ref.dtype)
        _acc_rows(dd_ref, jnp.sum(dy * x_ref[...].astype(F32), axis=0, keepdims=True), pl.program_id(0) == 0)

    big = pl.BlockSpec((tm, 2048), lambda i: (i, 0))
    row = pl.BlockSpec((1, 2048), lambda i: (0, 0))
    return pl.pallas_call(
        body, name=name, grid=(s // tm,), in_specs=[big, big, big, big, big, row, ANY], out_specs=[big, big, row, row],
        out_shape=[jax.ShapeDtypeStruct((s, 2048), MMD), jax.ShapeDtypeStruct(dproj.shape, dproj.dtype),
                   jax.ShapeDtypeStruct((1, 2048), F32), jax.ShapeDtypeStruct((1, 2048), F32)],
        input_output_aliases={6: 1}, compiler_params=_cp(("arbitrary",)),
    )(dout, ya, yb, proj, xc, w, dproj)


def _heads(a, n):
    return a.reshape(a.shape[0], n, HEAD_DIM).transpose(1, 0, 2)


def _unheads(a):
    return a.transpose(1, 0, 2).reshape(a.shape[1], a.shape[0] * HEAD_DIM)


def _local_step(x, target, mod, wts, small, in_weights=None, late_weights=None, late_grads=None, in_grad=None):
    s, d = x.shape
    shift1, scale1, gate1, shift2, scale2, gate2 = [mod[i:i + 1] for i in range(6)]

    h1 = _ln_mod(x, small["norm1_w"], scale1, shift1, name="ln1")
    if in_weights is not None:
        wts = {**wts, **in_weights(h1)}
    proj = _mm(h1, wts["w_in_p"], name="in_proj", outs=[MMD], tm=512, tn=2944, b_outer=True)
    dt_raw = _mm(h1, wts["w_dt"], name="dt_proj", outs=[F32], tm=512, tn=128)

    qk_w = jnp.concatenate([jnp.tile(small["q_norm_w"], (1, N_Q_HEADS)), jnp.tile(small["k_norm_w"], (1, N_KV_HEADS))], axis=1)
    qk_sc = jnp.concatenate([jnp.full((1, N_Q_HEADS * HEAD_DIM), HEAD_DIM ** -0.5, F32),
                             jnp.ones((1, N_KV_HEADS * HEAD_DIM), F32)], axis=1)
    qk_sc2 = jnp.concatenate([jnp.full((1, N_Q_HEADS * HEAD_DIM), HEAD_DIM ** -0.5 * LOG2E, F32),
                              jnp.ones((1, N_KV_HEADS * HEAD_DIM), F32)], axis=1)
    tabs = _rope_tables(s)
    qk, qkt = _qk_fwd(proj, qk_w, qk_sc2, tabs, name="qk_fwd")
    qkt = qkt.reshape(N_Q_HEADS + N_KV_HEADS, HEAD_DIM, s)
    k_h = _heads(qk[:, N_Q_HEADS * HEAD_DIM:], N_KV_HEADS)
    v_sd = proj[:, V0:V0 + N_KV_HEADS * HEAD_DIM]
    v_h = _heads(v_sd, N_KV_HEADS)
    vta = jnp.concatenate([v_sd.T.reshape(N_KV_HEADS, HEAD_DIM, s), jnp.ones((N_KV_HEADS, V_AUG - HEAD_DIM, s), MMD)], axis=1)
    ot, lse = _flash_fwd(qkt, vta, name="flash_fwd")
    ot2 = ot.reshape(N_Q_HEADS * HEAD_DIM, s)
    if late_weights is not None:
        wts = {**wts, **late_weights(ot)}

    w8 = jnp.pad(small["conv_w"], ((0, 8 - D_CONV), (0, 0)))
    xc = _conv_fwd(proj, w8, small["conv_b"], name="conv_fwd")
    a_neg = -jnp.exp(small["A_log"])
    arow = jnp.pad(a_neg.reshape(1, 2 * SSD_HEADS), ((0, 0), (0, 128 - 2 * SSD_HEADS)))
    bias_row = jnp.pad(small["dt_bias"].reshape(1, 2 * SSD_HEADS), ((0, 0), (0, 128 - 2 * SSD_HEADS)))
    dt, cs = _dt_fwd(dt_raw, bias_row, arow, name="dt_fwd")
    drow = jnp.repeat(small["ssd_D"], SSD_P, axis=1)
    dirs = [dict(drow=drow), dict(drow=jnp.zeros_like(drow))]
    ex = _expand_mats()
    ys = []
    for di, dd in enumerate(dirs):
        y, dd["hprev"] = _ssd_fwd(xc, dt, cs, ex, dd["drow"], di, name=f"ssd_fwd{di}")
        ys.append(y)
    ssdn = _gnorm_fwd(ys[0], ys[1], proj, small["ssd_norm_w"], name="gnorm_fwd")

    a_o = _mm(ot2, wts["w_attn_out"], name="attn_out", outs=[MMD], ta=True, tm=512, tn=1024)

    def merge_epi(acc, ao, ga, gs):
        return (_sigmoid(ga.astype(F32)) * ao.astype(F32) + _sigmoid(gs.astype(F32)) * acc, acc)

    merged, b_o = _mm(ssdn, wts["w_ssd_out"], name="ssd_out", outs=[MMD, MMD], tm=512, tn=1024,
                      extras=[(a_o, "tile", 0), (proj, "tile", GA0), (proj, "tile", GS0)], epi=merge_epi)

    def res_epi(acc, res, gate):
        return (res + gate * acc, acc)

    x1, mo = _mm(merged, wts["w_o"], name="w_o", outs=[F32, MMD], tm=512, tn=1024,
                 extras=[(x, "tile", 0), (gate1, "row", 0)], epi=res_epi)
    h2 = _ln_mod(x1, small["norm2_w"], scale2, shift2, name="ln2")

    def relu2_epi(acc):
        rl = jnp.maximum(acc, 0.0)
        return (rl * rl, rl)

    act, rl = _mm(h2, wts["w_mlp1"], name="mlp1", outs=[MMD, MMD], tm=1024, tn=1024, epi=relu2_epi, b_outer=True)

    def loss_epi(acc, res, gate, tgt):
        return ((res + gate * acc - tgt) * (1.0 / d), acc)

    dy, ffo = _mm(act, wts["w_mlp2"], name="mlp2", outs=[F32, MMD], tm=512, tn=1024, vmem=VMEM_BIG,
                  extras=[(x1, "tile", 0), (gate2, "row", 0), (target, "tile", 0)], epi=loss_epi)
    loss = _sumsq(dy, name="loss") * (0.5 * d)

    gw = {}
    gs_ = {}
    dffo, dgate2 = _gate_bwd(dy, ffo, gate2, name="gate2_bwd")
    dpre = _mm(dffo, wts["w_mlp2"], name="mlp2_dx", outs=[MMD], nt=True, tm=1024, tn=1024, b_outer=True,
               extras=[(rl, "tile", 0)], epi=lambda acc, r: (acc * (2.0 * r.astype(F32)),))
    gw["w_mlp2"] = _mm_tn(act, dffo, name="mlp2_dw")
    dh2 = _mm(dpre, wts["w_mlp1"], name="mlp1_dx", outs=[F32], nt=True, tm=1024, tn=1024, vmem=VMEM_BIG)
    gw["w_mlp1"] = _mm_tn(h2, dpre, name="mlp1_dw")
    dx1, dshift2, dscale2, gs_["norm2_w"] = _ln_mod_bwd(dh2, x1, small["norm2_w"], scale2, dy, name="ln2_bwd")
    dmo, dgate1 = _gate_bwd(dx1, mo, gate1, name="gate1_bwd")

    def merge_bwd_epi(acc, ao, bo, ga, gs):
        sa, ss = _sigmoid(ga.astype(F32)), _sigmoid(gs.astype(F32))
        return (acc * sa, acc * ss, acc * ao.astype(F32) * sa * (1.0 - sa), acc * bo.astype(F32) * ss * (1.0 - ss))

    da_o, db_o, dga, dgs = _mm(dmo, wts["w_o"], name="w_o_dx", outs=[MMD] * 4, nt=True, tm=512, tn=1024,
                               extras=[(a_o, "tile", 0), (b_o, "tile", 0), (proj, "tile", GA0), (proj, "tile", GS0)],
                               epi=merge_bwd_epi)
    gw["w_o"] = _mm_tn(merged, dmo, name="w_o_dw")
    dot = _mm(wts["w_attn_out"], da_o, name="attn_out_dx", outs=[MMD], nt=True, tm=1024, tn=1024)
    gw["w_attn_out"] = _mm(ot2, da_o, name="attn_out_dw", outs=[F32], tm=256, tn=512, vmem=VMEM_BIG)
    dssdn = _mm(db_o, wts["w_ssd_out"], name="ssd_out_dx", outs=[MMD], nt=True, tm=512, tn=2048)
    gw["w_ssd_out"] = _mm_tn(ssdn, db_o, name="ssd_out_dw")

    dproj = lax.dynamic_update_slice(lax.empty((s, PW), MMD), jnp.concatenate([dga, dgs], axis=1), (0, GA0))

    norm_w = small["ssd_norm_w"] if late_grads is None else small["ssd_norm_w"] + late_grads(gw)
    dyssd, dproj, gs_["ssd_norm_w"], dd_row = _gnorm_bwd(dssdn, ys[0], ys[1], proj, xc, norm_w, dproj, name="gnorm_bwd")
    gs_["ssd_D"] = dd_row.reshape(SSD_HEADS, SSD_P).sum(axis=1).reshape(1, SSD_HEADS)
    dxc, ddts, das = [], [], []
    for di, dd in enumerate(dirs):
        dxs, dbm, dcm, ddt_d, da_d = _ssd_bwd(xc, dt, cs, ex, dd["drow"], arow, dyssd, dd["hprev"], di, name=f"ssd_bwd{di}")
        dxc.append((dxs, dbm, dcm))
        ddts.append(ddt_d)
        das.append(da_d)
    dw8, db, col0 = [], [], 0
    for part, (ga, gb) in enumerate(zip(*dxc)):
        dproj, dw_part, db_part = _conv_bwd(proj, col0, ga, gb, w8, small["conv_b"], dproj, name=f"conv_bwd{part}")
        dw8.append(dw_part)
        db.append(db_part)
        col0 += ga.shape[1]
    gs_["conv_w"] = jnp.concatenate(dw8, axis=1)[0:D_CONV]
    gs_["conv_b"] = jnp.concatenate(db, axis=1)
    gs_["A_log"] = (das[0] + das[1])[:, 0:2 * SSD_HEADS].reshape(2, SSD_HEADS) * a_neg
    dproj, dbias = _dt_bwd(ddts[0], ddts[1], dt_raw, bias_row, dproj, name="dt_bwd")
    gs_["dt_bias"] = dbias[:, 0:2 * SSD_HEADS].reshape(2, SSD_HEADS)

    dqt, dk_h, dv_h = _flash_bwd(qkt, k_h, v_h, dot.reshape(N_Q_HEADS, HEAD_DIM, s), ot, lse, name="flash_bwd")
    dproj, dqk_w = _qk_bwd(dqt.reshape(N_Q_HEADS * HEAD_DIM, s), dk_h.transpose(0, 2, 1).reshape(N_KV_HEADS * HEAD_DIM, s),
                           proj, qk_w, qk_sc, tabs, dproj, name="qk_bwd")
    gs_["q_norm_w"] = dqk_w[:, 0:N_Q_HEADS * HEAD_DIM].reshape(N_Q_HEADS, HEAD_DIM).sum(axis=0, keepdims=True)
    gs_["k_norm_w"] = dqk_w[:, N_Q_HEADS * HEAD_DIM:].reshape(N_KV_HEADS, HEAD_DIM).sum(axis=0, keepdims=True)
    dproj = lax.dynamic_update_slice(dproj, _unheads(dv_h).astype(MMD), (0, V0))

    gw["w_in_p"] = _mm_tn(h1, dproj, name="in_proj_dw", tk=512, tn=2944, tmm=2048, vmem=VMEM_BIG)
    zero_row = jnp.zeros((1, d), F32) if in_grad is None else jnp.zeros((1, d), F32) + in_grad(gw["w_in_p"])[0:1, 0:1]
    dh1 = _mm(dproj, wts["w_in_p"], name="in_proj_dx", outs=[F32], nt=True, tm=256, tn=1024, vmem=VMEM_BIG,
              extras=[(zero_row, "row", 0)], epi=lambda acc, r: (acc + r,))
    grad_x, dshift1, dscale1, gs_["norm1_w"] = _ln_mod_bwd(dh1, x, small["norm1_w"], scale1, dx1, name="ln1_bwd")
    dmod = jnp.concatenate([dshift1, dscale1, dgate1, dshift2, dscale2, dgate2], axis=0)
    return loss, grad_x, dmod, gw, gs_


N_DEV = 8
N_CHIP = 4
ANY = pl.BlockSpec(memory_space=pl.ANY)


def _place():
    return lax.axis_index("x"), lax.axis_index("y"), lax.axis_index("c")


def _allgather8(v, *, name):
    m_per, n = v.shape

    def body(x_ref, out_ref, send_sems, recv_sems, local_sem):
        x, y, c = _place()
        me, sibling = (x, y, c), (x, y, 1 - c)
        chips = [(1 - x, y), (x, 1 - y), (1 - x, 1 - y)]

        def rows(px, py, pc):
            return out_ref.at[pl.ds((4 * px + 2 * py + pc) * m_per, m_per), :]

        def copy(k, block, to, src=None):
            return pltpu.make_async_remote_copy(
                src_ref=rows(*block) if src is None else src, dst_ref=rows(*block),
                send_sem=send_sems.at[k], recv_sem=recv_sems.at[k], device_id=to, device_id_type=MESH)

        mine = pltpu.make_async_copy(x_ref, rows(*me), local_sem)
        mine.start()
        first = [copy(0, me, sibling, src=x_ref)]
        first += [copy(1 + j, me, (*chip, c), src=x_ref) for j, chip in enumerate(chips)]
        for cp in first:
            cp.start()
        passed = [copy(4 + j, (*chip, c), sibling) for j, chip in enumerate(chips)]
        for j, chip in enumerate(chips):
            copy(1 + j, (*chip, c), me).wait_recv()
            passed[j].start()
        copy(0, sibling, me).wait_recv()
        for j, chip in enumerate(chips):
            copy(4 + j, (*chip, 1 - c), me).wait_recv()
        for cp in first + passed:
            cp.wait_send()
        mine.wait()

    return pl.pallas_call(
        body, name=name, out_shape=jax.ShapeDtypeStruct((N_DEV * m_per, n), v.dtype),
        in_specs=[pl.BlockSpec(memory_space=pltpu.VMEM)], out_specs=pl.BlockSpec(memory_space=pltpu.VMEM),
        scratch_shapes=[pltpu.SemaphoreType.DMA((7,)), pltpu.SemaphoreType.DMA((7,)), pltpu.SemaphoreType.DMA],
    )(v)


HBM = pl.BlockSpec(memory_space=pltpu.HBM)
SEM = pl.BlockSpec(memory_space=pltpu.SEMAPHORE)


def _chips_copies(x_ref, land_ref, sems, scatter, half=False):
    x, y, c = _place()
    k = 2 * x + y
    chips = [(1 - x, y), (x, 1 - y), (1 - x, 1 - y)]
    ids = [2 * cx + cy for cx, cy in chips]
    if half:
        hr = x_ref.shape[0] // 2
        rows = pl.ds(pl.multiple_of(c * hr, 16), hr)

    def copy(j, slot):
        src = x_ref.at[ids[j]] if scatter else (x_ref.at[rows] if half else x_ref)
        dst = land_ref.at[slot, rows] if half else land_ref.at[slot]
        return pltpu.make_async_remote_copy(src_ref=src, dst_ref=dst, send_sem=sems[j], recv_sem=sems[3 + j],
                                            device_id=(*chips[j], c), device_id_type=MESH)

    return [copy(j, k) for j in range(3)], [copy(j, ids[j]) for j in range(3)]


def _chips_start(src, scatter, half=False, *, name):
    shape = src.shape if scatter else (N_CHIP,) + tuple(src.shape)

    def body(x_ref, land_ref, *rest):
        sems, token = rest[0:6], rest[8]
        for cp in _chips_copies(x_ref, land_ref, sems, scatter, half)[0]:
            cp.start()
        token[...] = jnp.zeros_like(token)

    out = pl.pallas_call(
        body, name=name,
        out_shape=(pltpu.SemaphoreType.DMA(()),) * 6 + (pltpu.HBM(src.shape, src.dtype), pltpu.HBM(shape, src.dtype),
                                                       jax.ShapeDtypeStruct((8, 128), F32)),
        in_specs=(HBM, HBM), out_specs=(SEM,) * 6 + (HBM, HBM, pl.BlockSpec(memory_space=pltpu.VMEM)),
        input_output_aliases={0: 6, 1: 7},
        compiler_params=pltpu.CompilerParams(has_side_effects=pltpu.SideEffectType.DATAFLOW_SIDE_EFFECTING),
    )(pltpu.with_memory_space_constraint(src, pltpu.HBM),
      pltpu.with_memory_space_constraint(lax.empty(shape, src.dtype), pltpu.HBM))
    return out[0:6], out[6], out[7], out[8]


def _chips_wait(sems, src, land, after, scatter, half=False, *, name):
    def body(x_ref, land_ref, *rest):
        sems_ = rest[0:6]
        for cp in _chips_copies(x_ref, land_ref, sems_, scatter, half)[1]:
            cp.wait_send()
            cp.wait_recv()

    return pl.pallas_call(
        body, name=name, out_shape=(pltpu.HBM(src.shape, src.dtype), pltpu.HBM(land.shape, land.dtype)),
        in_specs=(HBM, HBM) + (SEM,) * 6 + (ANY,), out_specs=(HBM, HBM), input_output_aliases={0: 0, 1: 1},
        compiler_params=pltpu.CompilerParams(has_side_effects=pltpu.SideEffectType.DATAFLOW_SIDE_EFFECTING),
    )(src, land, *sems, after)


def _row_tile(r, pref=512):
    return max(t for t in range(16, pref + 1, 16) if r % t == 0)


def _pair_complete(land, *, name):
    r = land.shape[1]
    hr = r // 2
    assert r == 2 * hr and hr % 16 == 0

    def body(in_ref, out_ref, send_sems, recv_sems):
        x, y, c = _place()
        ids = [2 * cx + cy for cx, cy in [(1 - x, y), (x, 1 - y), (1 - x, 1 - y)]]
        mine_rows = pl.ds(pl.multiple_of(c * hr, 16), hr)
        other_rows = pl.ds(pl.multiple_of((1 - c) * hr, 16), hr)

        def copy(j, rows):
            return pltpu.make_async_remote_copy(
                src_ref=in_ref.at[ids[j], mine_rows], dst_ref=out_ref.at[ids[j], rows], send_sem=send_sems.at[j],
                recv_sem=recv_sems.at[j], device_id=(x, y, 1 - c), device_id_type=MESH)

        sends = [copy(j, mine_rows) for j in range(3)]
        for cp in sends:
            cp.start()
        for j in range(3):
            copy(j, other_rows).wait_recv()
        for cp in sends:
            cp.wait_send()

    return pl.pallas_call(
        body, name=name, out_shape=jax.ShapeDtypeStruct(land.shape, land.dtype), in_specs=[ANY], out_specs=ANY,
        input_output_aliases={0: 0},
        scratch_shapes=[pltpu.SemaphoreType.DMA((3,)), pltpu.SemaphoreType.DMA((3,))],
    )(land)


def _pair_swap(a, *, name):
    n, r, cols = a.shape
    hr = r // 2

    def body(x_ref, out_ref, send_sem, recv_sem):
        x, y, c = _place()
        other_rows = pl.ds(pl.multiple_of((1 - c) * hr, 16), hr)
        cp = pltpu.make_async_remote_copy(src_ref=x_ref.at[:, other_rows], dst_ref=out_ref, send_sem=send_sem,
                                          recv_sem=recv_sem, device_id=(x, y, 1 - c), device_id_type=MESH)
        cp.start()
        cp.wait()

    return pl.pallas_call(
        body, name=name, out_shape=jax.ShapeDtypeStruct((n, hr, cols), a.dtype), in_specs=[ANY], out_specs=ANY,
        scratch_shapes=[pltpu.SemaphoreType.DMA, pltpu.SemaphoreType.DMA],
    )(a)


def _sibling_copy(a, *, name):
    def body(x_ref, out_ref, send_sem, recv_sem):
        x, y, c = _place()
        cp = pltpu.make_async_remote_copy(src_ref=x_ref, dst_ref=out_ref, send_sem=send_sem, recv_sem=recv_sem,
                                          device_id=(x, y, 1 - c), device_id_type=MESH)
        cp.start()
        cp.wait()

    return pl.pallas_call(
        body, name=name, out_shape=jax.ShapeDtypeStruct(a.shape, a.dtype), in_specs=[ANY], out_specs=ANY,
        scratch_shapes=[pltpu.SemaphoreType.DMA, pltpu.SemaphoreType.DMA],
    )(a)


def _sum_slots(a, own, *, name):
    _, r, c = a.shape
    tr = _row_tile(r, 256)

    def body(a_ref, own_ref, o_ref):
        k = 2 * lax.axis_index("x") + lax.axis_index("y")
        acc = None
        for j in range(N_CHIP):
            term = jnp.where(k == j, own_ref[j], a_ref[j]).astype(F32)
            acc = term if acc is None else acc + term
        o_ref[...] = acc

    spec = pl.BlockSpec((N_CHIP, tr, c), lambda i: (0, i, 0))
    return pl.pallas_call(
        body, name=name, grid=(r // tr,), in_specs=[spec, spec],
        out_specs=pl.BlockSpec((tr, c), lambda i: (i, 0)), out_shape=jax.ShapeDtypeStruct((r, c), F32),
        compiler_params=_cp(("arbitrary",)),
    )(a, own)


def _add2(a, b, *, name):
    r, c = a.shape
    tr = _row_tile(r)

    def body(a_ref, b_ref, o_ref):
        o_ref[...] = (a_ref[...].astype(F32) + b_ref[...].astype(F32)).astype(o_ref.dtype)

    spec = pl.BlockSpec((tr, c), lambda i: (i, 0))
    return pl.pallas_call(
        body, name=name, grid=(r // tr,), in_specs=[spec, spec], out_specs=spec,
        out_shape=jax.ShapeDtypeStruct((r, c), a.dtype), compiler_params=_cp(("arbitrary",)),
    )(a, b)


BIG = ("w_in", "w_mlp1", "w_attn_out", "w_ssd_out", "w_o", "w_mlp2")
COL_SHARDED = ("w_mlp1", "w_in")
ROW_SHARDED = ("w_attn_out", "w_ssd_out", "w_o", "w_mlp2")
LATE = ROW_SHARDED + ("w_mlp1",)
SMALL = ("b_ada", "norm1_w", "norm2_w", "q_norm_w", "k_norm_w", "conv_b", "A_log", "dt_bias", "ssd_D", "ssd_norm_w")
NAMES = ("w_ada", "b_ada", "norm1_w", "norm2_w", "w_in", "q_norm_w", "k_norm_w", "conv_w", "conv_b", "A_log", "dt_bias",
         "ssd_D", "ssd_norm_w", "w_attn_out", "w_ssd_out", "w_o", "w_mlp1", "w_mlp2")
W_IN_COLS = 8768


def _permute_in(w):
    return jnp.concatenate([w[:, 4608:6656], w[:, 6720:8768], w[:, 1536:4608], w[:, 0:1536], w[:, 6656:6720],
                            jnp.zeros((w.shape[0], PW - W_IN_COLS), w.dtype)], axis=1)


def _unpermute_in(wp):
    return jnp.concatenate([wp[:, Q0:DT0], wp[:, XS0:Q0], wp[:, Z0:GA0], wp[:, DT0:DT0 + 64], wp[:, GA0:XS0]], axis=1)


def _pad_to(v, n):
    return jnp.pad(v, (0, n - v.shape[0]))


def _step(w, m, v, loss_target):
    xi, yi, ci = _place()
    chip = 2 * xi + yi
    dev = 4 * xi + 2 * yi + ci
    x, tgt = w["x"], loss_target
    d = x.shape[1]

    cw = w["conv_w"].shape[1]
    v0 = _pad_to(jnp.concatenate([w["c"].reshape(-1), w["conv_w"].reshape(-1)]), 5120).reshape(8, 640)
    g0 = _allgather8(v0, name="ag_cond").reshape(N_DEV, 5120)
    c_all = g0[:, 0:d]
    conv_w = jnp.concatenate([g0[2 * k, d:d + D_CONV * cw].reshape(D_CONV, cw) for k in range(N_CHIP)], axis=1)
    sc = _silu_cast(c_all, name="silu_c")
    modp = _mm(sc, w["w_ada"].astype(MMD), name="ada_fwd", outs=[F32], tm=8, tn=512)
    g1 = _allgather8(modp, name="ag_mod").reshape(N_DEV, N_DEV, modp.shape[1])
    mod_all = jnp.concatenate([g1[2 * k] for k in range(N_CHIP)], axis=1)
    mod = (lax.dynamic_slice_in_dim(mod_all, dev, 1, axis=0) + w["b_ada"]).reshape(6, d)

    mine, mod = lax.optimization_barrier((w["w_in"].astype(MMD), mod))
    in_sems, in_src, in_land, in_token = _chips_start(mine, False, True, name="ag_w_in_start")
    mod = mod + in_token[0:1, 0:1]
    small = {n: w[n] for n in SMALL if n != "b_ada"}
    small["conv_w"] = conv_w
    started = {}

    def in_weights(after):
        src, land = _chips_wait(in_sems, in_src, in_land, after, False, True, name="ag_w_in_wait")
        land = _pair_complete(land, name="ag_w_in_pair")
        late_mine = jnp.concatenate([w[n].astype(MMD) for n in LATE], axis=0)
        late_mine, land = lax.optimization_barrier((late_mine, land))
        sems, late_src, late_land, token = _chips_start(late_mine, False, name="ag_late_start")
        started["ag_late"] = (sems, late_src, late_land)
        w_in = jnp.concatenate([jnp.where(chip == k, src, land[k]) for k in range(N_CHIP)], axis=1)
        w_dt = jnp.pad(w_in[:, 6656:6720], ((0, 0), (0, 64))) + token[0:1, 0:1].astype(MMD)
        return {"w_in_p": _permute_in(w_in), "w_dt": w_dt}

    def late_weights(after):
        src, land = _chips_wait(*started["ag_late"], after, False, name="ag_late_wait")
        out, o = {}, 0
        for n in LATE:
            rows = w[n].shape[0]
            parts = [jnp.where(chip == k, src[o:o + rows], land[k, o:o + rows]) for k in range(N_CHIP)]
            out[n] = jnp.concatenate(parts, axis=1 if n in COL_SHARDED else 0)
            o += rows
        return out

    def pair_sums(slots, tag):
        _, rows, cols = slots.shape
        hr = rows // 2
        theirs = _pair_swap(slots, name="rs_pair_" + tag)
        ours = lax.dynamic_slice_in_dim(slots, ci * hr, hr, axis=1)
        pair = _add2(ours.reshape(N_CHIP * hr, cols), theirs.reshape(N_CHIP * hr, cols), name="rs_pair_sum_" + tag)
        return pair.reshape(N_CHIP, hr, cols)

    def finish(recv, pair, tag):
        half = _sum_slots(recv, pair, name="rs_sum_" + tag)
        other = _sibling_copy(half, name="rs_sibling_" + tag)
        return jnp.where(ci == 0, jnp.concatenate([half, other], axis=0), jnp.concatenate([other, half], axis=0))

    def late_grads(gw):
        slots = []
        for k in range(N_CHIP):
            parts = []
            for n in LATE:
                rows = w[n].shape[0]
                blk = gw[n][:, k * rows:(k + 1) * rows] if n in COL_SHARDED else gw[n][k * rows:(k + 1) * rows]
                parts.append(blk.astype(MMD))
            slots.append(jnp.concatenate(parts, axis=0))
        pair = pair_sums(jnp.stack(slots), "late")
        sems, src, land, token = _chips_start(pair, True, name="rs_late_start")
        started["late"] = (sems, src, land)
        return token[0:1, 0:1]

    def in_grad(g):
        g_in = _unpermute_in(g)
        cols_in = w["w_in"].shape[1]
        pair = pair_sums(jnp.stack([g_in[:, k * cols_in:(k + 1) * cols_in].astype(MMD) for k in range(N_CHIP)]), "w_in")
        sems, src, land, token = _chips_start(pair, True, name="rs_w_in_start")
        started["w_in"] = (sems, src, land)
        return token

    loss, grad_x, dmod, gw, gs = _local_step(x, tgt, mod, {}, small, in_weights, late_weights, late_grads, in_grad)

    grads = {}
    pair, land = _chips_wait(*started["w_in"], grad_x, True, name="rs_w_in_wait")
    grads["w_in"] = finish(land, pair, "w_in")
    pair, land = _chips_wait(*started["late"], grad_x, True, name="rs_late_wait")
    total, o = finish(land, pair, "late"), 0
    for n in LATE:
        rows = w[n].shape[0]
        grads[n] = total[o:o + rows]
        o += rows

    order = ([dmod.reshape(-1)] + [gs[n].reshape(-1) for n in SMALL if n != "b_ada"] + [gs["conv_w"].reshape(-1)]
             + [loss.reshape(-1)])
    vec = jnp.concatenate(order)
    n_small = vec.shape[0]
    n_pad = -(-n_small // 1024) * 1024
    g2 = _allgather8(_pad_to(vec, n_pad).reshape(8, n_pad // 8), name="ag_small")
    tot = _rows_sum(g2, N_DEV, name="small_sum").reshape(-1)
    loss = tot[n_small - 1]
    dmod_all = g2.reshape(N_DEV, n_pad)[:, 0:6 * d]
    off = 0
    for n in SMALL:
        grads[n] = tot[off:off + w[n].size].reshape(w[n].shape)
        off += w[n].size
    conv_full = tot[off:off + D_CONV * N_CHIP * cw].reshape(D_CONV, N_CHIP * cw)
    grads["conv_w"] = lax.dynamic_slice_in_dim(conv_full, chip * cw, cw, axis=1)
    ada_cols = w["w_ada"].shape[1]
    dmod_mine = lax.dynamic_slice_in_dim(dmod_all, chip * ada_cols, ada_cols, axis=1).astype(MMD)
    grads["w_ada"] = _mm_tn(sc, dmod_mine, name="ada_dw", tk=512, tn=512, tmm=8)

    delta, new_m, new_v = {}, {}, {}
    pack = lambda t: jnp.concatenate([t[n].reshape(-1) for n in SMALL]).reshape(1, -1)
    ds_, ms_, vs_ = _adamw(pack(w), pack(grads), pack(m), pack(v), name="adamw_small")
    off = 0
    for n in SMALL:
        for dst, src in ((delta, ds_), (new_m, ms_), (new_v, vs_)):
            dst[n] = src[0, off:off + w[n].size].reshape(w[n].shape)
        off += w[n].size
    for n in ("w_ada", "conv_w") + BIG:
        delta[n], new_m[n], new_v[n] = _adamw(w[n], grads[n], m[n], v[n], name="adamw_" + n)
    return loss, grad_x, grads, delta, new_m, new_v


def kernel(x, c, w_ada, b_ada, norm1_w, norm2_w, w_in, q_norm_w, k_norm_w, conv_w, conv_b, A_log, dt_bias, ssd_D, ssd_norm_w, w_attn_out, w_ssd_out, w_o, w_mlp1, w_mlp2, loss_target, m_w_ada, m_b_ada, m_norm1_w, m_norm2_w, m_w_in, m_q_norm_w, m_k_norm_w, m_conv_w, m_conv_b, m_A_log, m_dt_bias, m_ssd_D, m_ssd_norm_w, m_w_attn_out, m_w_ssd_out, m_w_o, m_w_mlp1, m_w_mlp2, v_w_ada, v_b_ada, v_norm1_w, v_norm2_w, v_w_in, v_q_norm_w, v_k_norm_w, v_conv_w, v_conv_b, v_A_log, v_dt_bias, v_ssd_D, v_ssd_norm_w, v_w_attn_out, v_w_ssd_out, v_w_o, v_w_mlp1, v_w_mlp2):
    args = dict(locals())
    strip = lambda a: a[0] if a.ndim == 3 else a
    w = {n: strip(args[n]) for n in NAMES + ("x", "c")}
    m = {n: strip(args["m_" + n]) for n in NAMES}
    v = {n: strip(args["v_" + n]) for n in NAMES}
    loss, grad_x, grads, delta, new_m, new_v = _step(w, m, v, loss_target[0])
    like = lambda t, n: t.reshape(args[n].shape)
    return (loss, grad_x[None], *[like(grads[n], n) for n in NAMES], *[like(delta[n], n) for n in NAMES],
            *[like(new_m[n], n) for n in NAMES], *[like(new_v[n], n) for n in NAMES])
```

```python
import math

import jax
import jax.numpy as jnp
from jax import lax
from jax.experimental import pallas as pl
from jax.experimental.pallas import tpu as pltpu

F32 = jnp.float32
MMD = jnp.bfloat16
EPS = 1e-6
NEG = -1e30
MIB = 1024 * 1024
VMEM_BIG = 56 * MIB
VMEM_MID = 40 * MIB

GRID_W = 64
N_Q_HEADS, N_KV_HEADS, HEAD_DIM = 16, 4, 64
ROPE_THETA = 10000.0
SSD_HEADS, SSD_GROUPS, SSD_P, SSD_N, CHUNK = 32, 4, 64, 128, 128
HPG = SSD_HEADS // SSD_GROUPS
D_CONV = 5
ADAM_LR, ADAM_B1, ADAM_B2, ADAM_EPS, ADAM_WD, ADAM_STEP = 0.001, 0.9, 0.999, 1e-08, 0.01, 10

Z0, GA0, GS0, XS0, B0, C0, Q0, K0, V0, DT0, PW = 0, 2048, 3072, 4096, 6144, 6656, 7168, 8192, 8448, 8704, 8832

MESH = pl.DeviceIdType.MESH
NT = (((1,), (1,)), ((), ()))
TN = (((0,), (0,)), ((), ()))


def _cp(sem=None, vmem=VMEM_MID):
    return pltpu.CompilerParams(dimension_semantics=sem, vmem_limit_bytes=vmem)


def _tile(n, pref):
    t = min(n, pref)
    while n % t:
        t //= 2
    return t


def _dot(a, b, dims=None):
    if dims is None:
        return jnp.dot(a, b, preferred_element_type=F32)
    return lax.dot_general(a, b, dims, preferred_element_type=F32)


def _dot_hi(a01, b):
    a = a01.astype(jnp.bfloat16)
    h1 = b.astype(jnp.bfloat16)
    r1 = b - h1.astype(F32)
    h2 = r1.astype(jnp.bfloat16)
    return _dot(a, h1) + _dot(a, h2) + _dot(a, (r1 - h2.astype(F32)).astype(jnp.bfloat16))


def _sigmoid(x):
    return jax.nn.sigmoid(x)


def _mm(a, b, *, name, outs, nt=False, ta=False, extras=(), epi=None, tm=512, tn=512, n=None, b_outer=False,
        vmem=VMEM_MID):
    assert not (nt and ta)
    k, m = a.shape if ta else a.shape[::-1]
    if n is None:
        n = b.shape[0] if nt else b.shape[1]
    tm, tn = _tile(m, tm), _tile(n, tn)
    gi, gj = m // tm, n // tn
    if b_outer:
        grid = (gj, gi)
        ij = lambda p, q: (q, p)
    else:
        grid = (gi, gj)
        ij = lambda p, q: (p, q)
    if ta:
        a_spec = pl.BlockSpec((k, tm), lambda p, q: (0, ij(p, q)[0]))
    else:
        a_spec = pl.BlockSpec((tm, k), lambda p, q: (ij(p, q)[0], 0))
    if nt:
        b_spec = pl.BlockSpec((tn, k), lambda p, q: (ij(p, q)[1], 0))
    else:
        b_spec = pl.BlockSpec((k, tn), lambda p, q: (0, ij(p, q)[1]))
    e_specs = []
    for arr, kind, off in extras:
        ob = off // tn
        assert off % tn == 0
        if kind == "tile":
            e_specs.append(pl.BlockSpec((tm, tn), lambda p, q, ob=ob: (ij(p, q)[0], ob + ij(p, q)[1])))
        else:
            e_specs.append(pl.BlockSpec((1, tn), lambda p, q, ob=ob: (0, ob + ij(p, q)[1])))
    ne = len(extras)

    def body(a_ref, b_ref, *rest):
        acc = _dot(a_ref[...], b_ref[...], NT if nt else (TN if ta else None))
        res = epi(acc, *[e[...] for e in rest[:ne]]) if epi is not None else (acc,)
        for o_ref, r in zip(rest[ne:], res):
            o_ref[...] = r.astype(o_ref.dtype)

    out = pl.pallas_call(
        body, name=name, grid=grid,
        in_specs=[a_spec, b_spec] + e_specs,
        out_specs=[pl.BlockSpec((tm, tn), lambda p, q: ij(p, q)) for _ in outs],
        out_shape=[jax.ShapeDtypeStruct((m, n), dt) for dt in outs],
        compiler_params=_cp(("arbitrary", "arbitrary"), vmem),
    )(a, b, *[e[0] for e in extras])
    return out if len(outs) > 1 else out[0]


def _mm_tn(a, g, *, name, tk=512, tn=1024, tmm=4096, vmem=VMEM_MID):
    m, k = a.shape
    n = g.shape[1]
    tk, tn, tmm = _tile(k, tk), _tile(n, tn), _tile(m, tmm)

    def body(a_ref, g_ref, o_ref):
        p = _dot(a_ref[...], g_ref[...], TN)

        @pl.when(pl.program_id(2) == 0)
        def _():
            o_ref[...] = p

        @pl.when(pl.program_id(2) > 0)
        def _():
            o_ref[...] += p

    return pl.pallas_call(
        body, name=name, grid=(k // tk, n // tn, m // tmm),
        in_specs=[pl.BlockSpec((tmm, tk), lambda i, j, r: (r, i)), pl.BlockSpec((tmm, tn), lambda i, j, r: (r, j))],
        out_specs=pl.BlockSpec((tk, tn), lambda i, j, r: (i, j)),
        out_shape=jax.ShapeDtypeStruct((k, n), F32),
        compiler_params=_cp(("arbitrary", "arbitrary", "arbitrary"), vmem),
    )(a, g)


def _adamw(w, g, m, v, *, name):
    r, c = w.shape
    tr = _tile(r, 256) if r % 8 == 0 else r

    def body(w_ref, g_ref, m_ref, v_ref, d_ref, nm_ref, nv_ref):
        gg = g_ref[...]
        nm = ADAM_B1 * m_ref[...] + (1.0 - ADAM_B1) * gg
        nv = ADAM_B2 * v_ref[...] + (1.0 - ADAM_B2) * jnp.square(gg)
        m_hat = nm / (1.0 - ADAM_B1 ** ADAM_STEP)
        v_hat = nv / (1.0 - ADAM_B2 ** ADAM_STEP)
        d_ref[...] = -ADAM_LR * (m_hat / (jnp.sqrt(v_hat) + ADAM_EPS) + ADAM_WD * w_ref[...])
        nm_ref[...] = nm
        nv_ref[...] = nv

    spec = pl.BlockSpec((tr, c), lambda i: (i, 0))
    return pl.pallas_call(
        body, name=name, grid=(r // tr,), in_specs=[spec] * 4, out_specs=[spec] * 3,
        out_shape=[jax.ShapeDtypeStruct((r, c), F32)] * 3, compiler_params=_cp(("arbitrary",)),
    )(w, g, m, v)


def _rows_sum(a, groups, *, name):
    r = a.shape[0] // groups

    def body(a_ref, o_ref):
        acc = a_ref[0:r, :]
        for d in range(1, groups):
            acc = acc + a_ref[d * r:(d + 1) * r, :]
        o_ref[...] = acc

    return pl.pallas_call(body, name=name, out_shape=jax.ShapeDtypeStruct((r, a.shape[1]), F32))(a)


def _silu_cast(a, *, name):
    def body(a_ref, o_ref):
        x = a_ref[...]
        o_ref[...] = (x * _sigmoid(x)).astype(o_ref.dtype)

    return pl.pallas_call(body, name=name, out_shape=jax.ShapeDtypeStruct(a.shape, MMD))(a)


def _sumsq(a, *, name):
    m, n = a.shape
    tm = _tile(m, 512)

    def body(a_ref, o_ref):
        x = a_ref[...]
        p = jnp.sum(jnp.sum(x * x, axis=1, keepdims=True), axis=0, keepdims=True)

        @pl.when(pl.program_id(0) == 0)
        def _():
            o_ref[...] = p

        @pl.when(pl.program_id(0) > 0)
        def _():
            o_ref[...] += p

    return pl.pallas_call(
        body, name=name, grid=(m // tm,), in_specs=[pl.BlockSpec((tm, n), lambda i: (i, 0))],
        out_specs=pl.BlockSpec((1, 1), lambda i: (0, 0)), out_shape=jax.ShapeDtypeStruct((1, 1), F32),
        compiler_params=_cp(("arbitrary",)),
    )(a)


def _acc_rows(o_ref, p, first):
    @pl.when(first)
    def _():
        o_ref[...] = p

    @pl.when(jnp.logical_not(first))
    def _():
        o_ref[...] += p


def _ln_mod(x, w, scale, shift, *, name):
    s, d = x.shape
    tm = _tile(s, 512)

    def body(x_ref, w_ref, sc_ref, sh_ref, o_ref):
        xv = x_ref[...]
        r = lax.rsqrt(jnp.mean(xv * xv, axis=-1, keepdims=True) + EPS)
        o_ref[...] = ((xv * r) * w_ref[...] * (1.0 + sc_ref[...]) + sh_ref[...]).astype(o_ref.dtype)

    row = pl.BlockSpec((1, d), lambda i: (0, 0))
    big = pl.BlockSpec((tm, d), lambda i: (i, 0))
    return pl.pallas_call(
        body, name=name, grid=(s // tm,), in_specs=[big, row, row, row], out_specs=big,
        out_shape=jax.ShapeDtypeStruct((s, d), MMD), compiler_params=_cp(("arbitrary",)),
    )(x, w, scale, shift)


def _ln_mod_bwd(dh, x, w, scale, dres, *, name):
    s, d = x.shape
    tm = _tile(s, 512)

    def body(dh_ref, x_ref, w_ref, sc_ref, dres_ref, dx_ref, dsh_ref, dsc_ref, dw_ref):
        xv = x_ref[...]
        dhv = dh_ref[...].astype(F32)
        r = lax.rsqrt(jnp.mean(xv * xv, axis=-1, keepdims=True) + EPS)
        nv = xv * r
        wv = w_ref[...]
        g1 = 1.0 + sc_ref[...]
        dn = dhv * (wv * g1)
        dx_ref[...] = dres_ref[...] + r * (dn - nv * jnp.mean(dn * nv, axis=-1, keepdims=True))
        first = pl.program_id(0) == 0
        _acc_rows(dsh_ref, jnp.sum(dhv, axis=0, keepdims=True), first)
        _acc_rows(dsc_ref, jnp.sum(dhv * nv * wv, axis=0, keepdims=True), first)
        _acc_rows(dw_ref, jnp.sum(dhv * nv * g1, axis=0, keepdims=True), first)

    row = pl.BlockSpec((1, d), lambda i: (0, 0))
    big = pl.BlockSpec((tm, d), lambda i: (i, 0))
    return pl.pallas_call(
        body, name=name, grid=(s // tm,), in_specs=[big, big, row, row, big], out_specs=[big, row, row, row],
        out_shape=[jax.ShapeDtypeStruct((s, d), F32)] + [jax.ShapeDtypeStruct((1, d), F32)] * 3,
        compiler_params=_cp(("arbitrary",)),
    )(dh, x, w, scale, dres)


def _gate_bwd(dy, u, gate, *, name):
    s, d = dy.shape
    tm = _tile(s, 512)

    def body(dy_ref, u_ref, g_ref, du_ref, dg_ref):
        dyv = dy_ref[...]
        du_ref[...] = (dyv * g_ref[...]).astype(du_ref.dtype)
        _acc_rows(dg_ref, jnp.sum(dyv * u_ref[...].astype(F32), axis=0, keepdims=True), pl.program_id(0) == 0)

    row = pl.BlockSpec((1, d), lambda i: (0, 0))
    big = pl.BlockSpec((tm, d), lambda i: (i, 0))
    return pl.pallas_call(
        body, name=name, grid=(s // tm,), in_specs=[big, big, row], out_specs=[big, row],
        out_shape=[jax.ShapeDtypeStruct((s, d), MMD), jax.ShapeDtypeStruct((1, d), F32)],
        compiler_params=_cp(("arbitrary",)),
    )(dy, u, gate)


def _seg64(v, e):
    hi = v.astype(jnp.bfloat16)
    lo = (v - hi.astype(F32)).astype(jnp.bfloat16)
    return _dot(hi, e) + _dot(lo, e)


def _rope_tables(s):
    rows = s // GRID_W
    pos_row = jnp.repeat(jnp.arange(rows, dtype=jnp.int32), GRID_W).astype(F32)
    pos_col = jnp.tile(jnp.arange(GRID_W, dtype=jnp.int32), rows).astype(F32)
    axis_dim = HEAD_DIM // 2
    inv_freq = ROPE_THETA ** (-jnp.arange(0, axis_dim, 2, dtype=F32) / axis_dim)
    ang_r = pos_row[:, None] * inv_freq[None, :]
    ang_c = pos_col[:, None] * inv_freq[None, :]
    zero = jnp.zeros_like(ang_r)
    cos = jnp.concatenate([jnp.cos(ang_r), jnp.cos(ang_r), jnp.cos(ang_c), jnp.cos(ang_c)], axis=1)
    s_a = jnp.concatenate([-jnp.sin(ang_r), zero, -jnp.sin(ang_c), zero], axis=1)
    s_b = jnp.concatenate([zero, jnp.sin(ang_r), zero, jnp.sin(ang_c)], axis=1)
    return [jnp.tile(t, (1, 2)) for t in (cos, s_a, s_b)]


def _e128():
    i = jnp.arange(128)
    return (i[:, None] // 64 == i[None, :] // 64).astype(jnp.bfloat16)


QKW = N_Q_HEADS * HEAD_DIM + N_KV_HEADS * HEAD_DIM


def _qk_fwd(proj, wrow, scrow, tabs, *, name):
    s = proj.shape[0]
    tm = _tile(s, 1024)

    def body(x_ref, w_ref, sc_ref, cos_ref, sa_ref, sb_ref, e_ref, o_ref, ot_ref):
        u = x_ref[...].astype(F32)
        r = lax.rsqrt(_seg64(u * u, e_ref[...]) * (1.0 / HEAD_DIM) + EPS)
        nv = (u * r) * w_ref[...]
        ro = nv * cos_ref[...] + pltpu.roll(nv, 112, 1) * sa_ref[...] + pltpu.roll(nv, 16, 1) * sb_ref[...]
        out = ro * sc_ref[...]
        o_ref[...] = out.astype(o_ref.dtype)
        ot_ref[...] = out.T.astype(ot_ref.dtype)

    tab = pl.BlockSpec((tm, 128), lambda i, j: (i, 0))
    row = pl.BlockSpec((1, 128), lambda i, j: (0, j))
    return pl.pallas_call(
        body, name=name, grid=(s // tm, QKW // 128),
        in_specs=[pl.BlockSpec((tm, 128), lambda i, j: (i, Q0 // 128 + j)), row, row, tab, tab, tab,
                  pl.BlockSpec((128, 128), lambda i, j: (0, 0))],
        out_specs=[pl.BlockSpec((tm, 128), lambda i, j: (i, j)), pl.BlockSpec((128, tm), lambda i, j: (j, i))],
        out_shape=[jax.ShapeDtypeStruct((s, QKW), MMD), jax.ShapeDtypeStruct((QKW, s), MMD)],
        compiler_params=_cp(("arbitrary", "arbitrary")),
    )(proj, wrow, scrow, *tabs, _e128())


def _qk_bwd(dqt, dkt, proj, wrow, scrow, tabs, dproj, *, name):
    s = proj.shape[0]
    tm = _tile(s, 1024)
    nq = dqt.shape[0] // 128

    def body(dq_ref, dk_ref, x_ref, w_ref, sc_ref, cos_ref, sa_ref, sb_ref, e_ref, _, du_ref, dw_ref):
        e = e_ref[...]
        d = jnp.where(pl.program_id(0) < nq, dq_ref[...], dk_ref[...]).T * sc_ref[...]
        dn = d * cos_ref[...] + pltpu.roll(d * sa_ref[...], 16, 1) + pltpu.roll(d * sb_ref[...], 112, 1)
        u = x_ref[...].astype(F32)
        r = lax.rsqrt(_seg64(u * u, e) * (1.0 / HEAD_DIM) + EPS)
        uh = u * r
        _acc_rows(dw_ref, jnp.sum(dn * uh, axis=0, keepdims=True), pl.program_id(1) == 0)
        dnw = dn * w_ref[...]
        du_ref[...] = (r * (dnw - uh * (_seg64(dnw * uh, e) * (1.0 / HEAD_DIM)))).astype(du_ref.dtype)

    tab = pl.BlockSpec((tm, 128), lambda j, i: (i, 0))
    row = pl.BlockSpec((1, 128), lambda j, i: (0, j))
    qcol = pl.BlockSpec((tm, 128), lambda j, i: (i, Q0 // 128 + j))
    return pl.pallas_call(
        body, name=name, grid=(QKW // 128, s // tm),
        in_specs=[pl.BlockSpec((128, tm), lambda j, i: (jnp.minimum(j, nq - 1), i)),
                  pl.BlockSpec((128, tm), lambda j, i: (jnp.maximum(j - nq, 0), i)),
                  qcol, row, row, tab, tab, tab, pl.BlockSpec((128, 128), lambda j, i: (0, 0)), ANY],
        out_specs=[qcol, row],
        out_shape=[jax.ShapeDtypeStruct(dproj.shape, dproj.dtype), jax.ShapeDtypeStruct((1, QKW), F32)],
        input_output_aliases={9: 0}, compiler_params=_cp(("arbitrary", "arbitrary")),
    )(dqt, dkt, proj, wrow, scrow, *tabs, _e128(), dproj)


REP = N_Q_HEADS // N_KV_HEADS


def _lanes(ref):
    return jnp.concatenate([ref[r] for r in range(REP)], axis=1)


V_AUG = HEAD_DIM + 8
LOG2E = math.log2(math.e)


def _flash_fwd(qkt, vta, *, name):
    s = qkt.shape[2]
    tq, tk = _tile(s, 1024), _tile(s, 512)
    nk = s // tk
    lanes = REP * tq

    def body(q_ref, k_ref, v_ref, o_ref, lse_ref, m_ref, acc_ref):
        j = pl.program_id(2)

        @pl.when(j == 0)
        def _():
            m_ref[...] = jnp.full_like(m_ref, NEG)
            acc_ref[...] = jnp.zeros_like(acc_ref)

        st = _dot(k_ref[0], _lanes(q_ref), TN)
        m_prev = m_ref[...]
        m_new = jnp.maximum(m_prev, jnp.max(st, axis=0, keepdims=True))
        p = jnp.exp2(st - m_new).astype(MMD)
        acc_ref[...] = jnp.exp2(m_prev - m_new) * acc_ref[...] + _dot(v_ref[0], p)
        m_ref[...] = m_new

        @pl.when(j == nk - 1)
        def _():
            acc = acc_ref[...]
            l = acc[HEAD_DIM:HEAD_DIM + 1]
            o = acc[0:HEAD_DIM] / l
            ls = m_ref[...] + jnp.log(l) * LOG2E
            for r in range(REP):
                o_ref[r] = o[:, r * tq:(r + 1) * tq].astype(o_ref.dtype)
                lse_ref[r] = ls[:, r * tq:(r + 1) * tq]

    qspec = pl.BlockSpec((REP, HEAD_DIM, tq), lambda g, i, j: (g, 0, i))
    return pl.pallas_call(
        body, name=name, grid=(N_KV_HEADS, s // tq, nk),
        in_specs=[qspec, pl.BlockSpec((1, HEAD_DIM, tk), lambda g, i, j: (N_Q_HEADS + g, 0, j)),
                  pl.BlockSpec((1, V_AUG, tk), lambda g, i, j: (g, 0, j))],
        out_specs=[qspec, pl.BlockSpec((REP, 1, tq), lambda g, i, j: (g, 0, i))],
        out_shape=[jax.ShapeDtypeStruct((N_Q_HEADS, HEAD_DIM, s), MMD), jax.ShapeDtypeStruct((N_Q_HEADS, 1, s), F32)],
        scratch_shapes=[pltpu.VMEM((1, lanes), F32), pltpu.VMEM((V_AUG, lanes), F32)],
        compiler_params=_cp(("arbitrary", "arbitrary", "arbitrary"), VMEM_BIG),
    )(qkt, qkt, vta)


def _flash_bwd(qkt, k_h, v_h, dot, ot, lse, *, name):
    s = qkt.shape[2]
    tq, tk = _tile(s, 512), _tile(s, 1024)
    nk = s // tk

    def body(q_ref, kt_ref, k_ref, v_ref, do_ref, o_ref, lse_ref, dq_ref, dk_ref, dv_ref, dq_acc):
        i, j = pl.program_id(1), pl.program_id(2)
        q, do = _lanes(q_ref), _lanes(do_ref)
        delta = jnp.sum(do.astype(F32) * _lanes(o_ref).astype(F32), axis=0, keepdims=True)
        k, v = k_ref[0], v_ref[0]
        p = jnp.exp2(_dot(k, q) - _lanes(lse_ref))
        dvc = _dot(p.astype(MMD), do, NT)
        ds = (p * (_dot(v, do) - delta)).astype(MMD)
        dkc = _dot(ds, q, NT) * (1.0 / LOG2E)
        dqc = _dot(kt_ref[0], ds)
        rows = pl.ds(pl.multiple_of(j * tk, tk), tk)

        @pl.when(i == 0)
        def _():
            dk_ref[0, rows, :] = dkc
            dv_ref[0, rows, :] = dvc

        @pl.when(i > 0)
        def _():
            dk_ref[0, rows, :] += dkc
            dv_ref[0, rows, :] += dvc

        @pl.when(j == 0)
        def _():
            dq_acc[...] = dqc

        @pl.when(j > 0)
        def _():
            dq_acc[...] += dqc

        @pl.when(j == nk - 1)
        def _():
            acc = dq_acc[...]
            for r in range(REP):
                dq_ref[r] = acc[:, r * tq:(r + 1) * tq]

    qspec = pl.BlockSpec((REP, HEAD_DIM, tq), lambda g, i, j: (g, 0, i))
    kvin = pl.BlockSpec((1, tk, HEAD_DIM), lambda g, i, j: (g, j, 0))
    kvres = pl.BlockSpec((1, s, HEAD_DIM), lambda g, i, j: (g, 0, 0))
    return pl.pallas_call(
        body, name=name, grid=(N_KV_HEADS, s // tq, nk),
        in_specs=[qspec, pl.BlockSpec((1, HEAD_DIM, tk), lambda g, i, j: (N_Q_HEADS + g, 0, j)), kvin, kvin,
                  qspec, qspec, pl.BlockSpec((REP, 1, tq), lambda g, i, j: (g, 0, i))],
        out_specs=[qspec, kvres, kvres],
        out_shape=[jax.ShapeDtypeStruct((N_Q_HEADS, HEAD_DIM, s), F32), jax.ShapeDtypeStruct((N_KV_HEADS, s, HEAD_DIM), F32),
                   jax.ShapeDtypeStruct((N_KV_HEADS, s, HEAD_DIM), F32)],
        scratch_shapes=[pltpu.VMEM((HEAD_DIM, REP * tq), F32)],
        compiler_params=_cp(("arbitrary", "arbitrary", "arbitrary"), VMEM_BIG),
    )(qkt, qkt, k_h, v_h, dot, ot, lse)


HALO = 8
CONV_W = 2048 + 2 * SSD_GROUPS * SSD_N


def _shifted(win, off, r):
    return pltpu.roll(win, (r + 2 * HALO - off) % (r + 2 * HALO), 0)[0:r]


def _conv_fwd(proj, w8, brow, *, name):
    s = proj.shape[0]
    cb = 256
    r = _tile(s, 512)

    def body(x_ref, w_ref, b_ref, o_ref, pad_ref):
        zeros = jnp.zeros((HALO, cb), F32)
        pad_ref[0:HALO, :] = zeros
        pad_ref[s + HALO:s + 2 * HALO, :] = zeros

        def fill(i, carry):
            st = pl.multiple_of(i * r, r)
            pad_ref[pl.ds(st + HALO, r), :] = x_ref[pl.ds(st, r), :].astype(F32)
            return carry

        lax.fori_loop(0, s // r, fill, 0)
        wv = w_ref[...]
        bv = b_ref[...]

        def step(i, carry):
            st = pl.multiple_of(i * r, r)
            win = pad_ref[pl.ds(st, r + 2 * HALO), :]
            acc = bv + wv[0:1, :] * _shifted(win, HALO - 2, r)
            for t in range(1, D_CONV):
                acc = acc + wv[t:t + 1, :] * _shifted(win, HALO - 2 + t, r)
            o_ref[pl.ds(st, r), :] = (acc * _sigmoid(acc)).astype(o_ref.dtype)
            return carry

        lax.fori_loop(0, s // r, step, 0)

    return pl.pallas_call(
        body, name=name, grid=(CONV_W // cb,),
        in_specs=[pl.BlockSpec((s, cb), lambda j: (0, XS0 // cb + j)), pl.BlockSpec((8, cb), lambda j: (0, j)),
                  pl.BlockSpec((1, cb), lambda j: (0, j))],
        out_specs=pl.BlockSpec((s, cb), lambda j: (0, j)),
        out_shape=jax.ShapeDtypeStruct((s, CONV_W), MMD),
        scratch_shapes=[pltpu.VMEM((s + 2 * HALO, cb), F32)],
        compiler_params=_cp(("arbitrary",), VMEM_MID),
    )(proj, w8, brow)


def _conv_bwd(proj, col0, ga, gb, w8, brow, dproj, *, name):
    s = proj.shape[0]
    width = ga.shape[1]
    cb = 128
    c0 = col0 // cb
    r = _tile(s, 512)

    def body(x_ref, ga_ref, gb_ref, w_ref, b_ref, _, dx_ref, dw_ref, db_ref, xpad, dpad):
        zeros = jnp.zeros((HALO, cb), F32)
        for ref in (xpad, dpad):
            ref[0:HALO, :] = zeros
            ref[s + HALO:s + 2 * HALO, :] = zeros

        def fill(i, carry):
            st = pl.multiple_of(i * r, r)
            xpad[pl.ds(st + HALO, r), :] = x_ref[pl.ds(st, r), :].astype(F32)
            return carry

        lax.fori_loop(0, s // r, fill, 0)
        wv = w_ref[...]
        bv = b_ref[...]

        def first(i, carry):
            st = pl.multiple_of(i * r, r)
            win = xpad[pl.ds(st, r + 2 * HALO), :]
            taps = [_shifted(win, HALO - 2 + t, r) for t in range(D_CONV)]
            u = bv
            for t in range(D_CONV):
                u = u + wv[t:t + 1, :] * taps[t]
            sg = _sigmoid(u)
            du = ((ga_ref[pl.ds(st, r), :].astype(F32) + gb_ref[pl.ds(st, r), :].astype(F32))
                  * (sg * (1.0 + u * (1.0 - sg))))
            dpad[pl.ds(st + HALO, r), :] = du
            out = [carry[0] + jnp.sum(du, axis=0, keepdims=True)]
            for t in range(D_CONV):
                out.append(carry[1 + t] + jnp.sum(du * taps[t], axis=0, keepdims=True))
            return tuple(out)

        sums = lax.fori_loop(0, s // r, first, tuple(jnp.zeros((1, cb), F32) for _ in range(1 + D_CONV)))
        db_ref[...] = sums[0]
        for t in range(D_CONV):
            dw_ref[t:t + 1, :] = sums[1 + t]
        dw_ref[D_CONV:8, :] = jnp.zeros((8 - D_CONV, cb), F32)

        def second(i, carry):
            st = pl.multiple_of(i * r, r)
            win = dpad[pl.ds(st, r + 2 * HALO), :]
            acc = wv[0:1, :] * _shifted(win, HALO + 2, r)
            for t in range(1, D_CONV):
                acc = acc + wv[t:t + 1, :] * _shifted(win, HALO + 2 - t, r)
            dx_ref[pl.ds(st, r), :] = acc.astype(dx_ref.dtype)
            return carry

        lax.fori_loop(0, s // r, second, 0)

    col = pl.BlockSpec((s, cb), lambda j: (0, j))
    xcol = pl.BlockSpec((s, cb), lambda j: (0, XS0 // cb + c0 + j))
    return pl.pallas_call(
        body, name=name, grid=(width // cb,),
        in_specs=[xcol, col, col, pl.BlockSpec((8, cb), lambda j: (0, c0 + j)),
                  pl.BlockSpec((1, cb), lambda j: (0, c0 + j)), ANY],
        out_specs=[xcol, pl.BlockSpec((8, cb), lambda j: (0, j)), pl.BlockSpec((1, cb), lambda j: (0, j))],
        out_shape=[jax.ShapeDtypeStruct(dproj.shape, dproj.dtype), jax.ShapeDtypeStruct((8, width), F32),
                   jax.ShapeDtypeStruct((1, width), F32)],
        scratch_shapes=[pltpu.VMEM((s + 2 * HALO, cb), F32), pltpu.VMEM((s + 2 * HALO, cb), F32)],
        input_output_aliases={5: 0}, compiler_params=_cp(("arbitrary",), VMEM_BIG),
    )(proj, ga, gb, w8, brow, dproj)


def _tri(lower):
    i = jnp.arange(CHUNK)
    return ((i[:, None] >= i[None, :]) if lower else (i[:, None] <= i[None, :])).astype(F32)


def _dt_fwd(raw, bias, arow, *, name):
    s = raw.shape[0]

    def body(r_ref, b_ref, a_ref, lo_ref, up_ref, dt_ref, cs_ref):
        u = r_ref[...] + b_ref[...]
        dt = jnp.maximum(u, 0.0) + jnp.log1p(jnp.exp(-jnp.abs(u)))
        dt_ref[...] = dt
        a = dt * a_ref[...]
        lane = lax.broadcasted_iota(jnp.int32, (CHUNK, 128), 1)
        cs_ref[...] = jnp.where(lane < SSD_HEADS, _dot_hi(lo_ref[...], a), _dot_hi(up_ref[...], a))

    blk = pl.BlockSpec((CHUNK, 128), lambda i: (i, 0))
    row = pl.BlockSpec((1, 128), lambda i: (0, 0))
    tri = pl.BlockSpec((CHUNK, CHUNK), lambda i: (0, 0))
    return pl.pallas_call(
        body, name=name, grid=(s // CHUNK,), in_specs=[blk, row, row, tri, tri], out_specs=[blk, blk],
        out_shape=[jax.ShapeDtypeStruct((s, 128), F32)] * 2, compiler_params=_cp(("arbitrary",)),
    )(raw, bias, arow, _tri(True), _tri(False))


def _dt_bwd(ddt0, ddt1, raw, bias, dproj, *, name):
    s = raw.shape[0]
    tm = _tile(s, 1024)

    def body(d0_ref, d1_ref, r_ref, b_ref, _, o_ref, db_ref):
        g = (d0_ref[...] + d1_ref[...]) * _sigmoid(r_ref[...] + b_ref[...])
        o_ref[...] = g.astype(o_ref.dtype)
        _acc_rows(db_ref, jnp.sum(g, axis=0, keepdims=True), pl.program_id(0) == 0)

    blk = pl.BlockSpec((tm, 128), lambda i: (i, 0))
    row = pl.BlockSpec((1, 128), lambda i: (0, 0))
    return pl.pallas_call(
        body, name=name, grid=(s // tm,), in_specs=[blk, blk, blk, row, ANY],
        out_specs=[pl.BlockSpec((tm, 128), lambda i: (i, DT0 // 128)), row],
        out_shape=[jax.ShapeDtypeStruct(dproj.shape, dproj.dtype), jax.ShapeDtypeStruct((1, 128), F32)],
        input_output_aliases={4: 0}, compiler_params=_cp(("arbitrary",)),
    )(ddt0, ddt1, raw, bias, dproj)


GW = HPG * SSD_P


GPS = SSD_GROUPS


def _ssd_specs(nc, rev):
    cc = (lambda c: nc - 1 - c) if rev else (lambda c: c)
    return dict(
        x=pl.BlockSpec((CHUNK, GPS * GW), lambda g, c: (cc(c), g)),
        b=pl.BlockSpec((CHUNK, GPS * SSD_N), lambda g, c: (cc(c), 2048 // (GPS * SSD_N) + g)),
        c=pl.BlockSpec((CHUNK, GPS * SSD_N), lambda g, c: (cc(c), 2048 // (GPS * SSD_N) + 1 + g)),
        lanes=pl.BlockSpec((CHUNK, 128), lambda g, c: (cc(c), 0)),
        drow=pl.BlockSpec((1, GPS * GW), lambda g, c: (0, g)),
        y=pl.BlockSpec((CHUNK, GPS * GW), lambda g, c: (cc(c), g)),
        h=pl.BlockSpec((GPS, 1, SSD_N, GW), lambda g, c: (g, cc(c), 0, 0)),
        n=pl.BlockSpec((CHUNK, GPS * SSD_N), lambda g, c: (cc(c), g)),
    )


def _ssd_mask(anti):
    ii = lax.broadcasted_iota(jnp.int32, (CHUNK, CHUNK), 0)
    jj = lax.broadcasted_iota(jnp.int32, (CHUNK, CHUNK), 1)
    return ii, jj, (ii <= jj) if anti else (ii >= jj)


def _expand(x, ex, terms=3):
    h1 = x.astype(jnp.bfloat16)
    r1 = x - h1.astype(F32)
    h2 = r1.astype(jnp.bfloat16)
    out = _dot(h1, ex) + _dot(h2, ex)
    if terms == 3:
        out = out + _dot((r1 - h2.astype(F32)).astype(jnp.bfloat16), ex)
    return out


def _headsum(a, e):
    hi = a.astype(jnp.bfloat16)
    return _dot(hi, e) + _dot((a - hi.astype(F32)).astype(jnp.bfloat16), e)


def _expand_mats():
    lane = jnp.arange(128)[None, :, None]
    col = jnp.arange(GW)[None, None, :]
    base = (jnp.arange(2)[:, None] * SSD_HEADS + jnp.arange(SSD_GROUPS)[None, :] * HPG).reshape(2 * SSD_GROUPS, 1, 1)
    return (lane == base + col // SSD_P).astype(jnp.bfloat16)


def _headsum_mats():
    e1 = (jnp.arange(GW)[:, None] // SSD_P == jnp.arange(128)[None, :]).astype(jnp.bfloat16)
    e2 = (jnp.arange(HPG * CHUNK)[:, None] // CHUNK == jnp.arange(128)[None, :]).astype(jnp.bfloat16)
    return e1, e2


def _ssd_fwd(xc, dt, cs, ex, drow, di, *, name):
    s = xc.shape[0]
    nc = s // CHUNK
    anti = di == 1
    sp = _ssd_specs(nc, anti)
    trow = 0 if anti else CHUNK - 1

    def body(x_ref, b_ref, c_ref, dt_ref, cs_ref, ex_ref, d_ref, y_ref, hp_ref, h_ref):
        @pl.when(pl.program_id(1) == 0)
        def _():
            h_ref[...] = jnp.zeros_like(h_ref)

        mask = _ssd_mask(anti)[2]
        dtv, csv = dt_ref[...], cs_ref[...]
        cst = csv.T
        for gi in range(GPS):
            cols = slice(gi * GW, (gi + 1) * GW)
            ncols = slice(gi * SSD_N, (gi + 1) * SSD_N)
            ex = ex_ref[gi]
            xb = x_ref[:, cols].astype(F32)
            bm, cm = b_ref[:, ncols], c_ref[:, ncols]
            csr = cst[SSD_HEADS * di + HPG * gi:SSD_HEADS * di + HPG * (gi + 1)]
            dtf = _expand(dtv, ex, 2)
            csf = _expand(csv, ex)
            tl = csf[trow:trow + 1, :]
            h = h_ref[gi]
            hp_ref[gi, 0] = h.astype(hp_ref.dtype)
            g = _dot(cm, bm, NT)
            xs = xb * dtf
            xsm = xs.astype(MMD)
            base = jnp.exp(csf) * _dot(cm, h.astype(MMD)) + d_ref[:, cols] * xb
            for r in range(HPG):
                sl = slice(r * SSD_P, (r + 1) * SSD_P)
                lm = jnp.exp(jnp.where(mask, csf[:, r * SSD_P:r * SSD_P + 1] - csr[r:r + 1, :], NEG))
                y_ref[:, gi * GW + r * SSD_P:gi * GW + (r + 1) * SSD_P] = (
                    _dot((g * lm).astype(MMD), xsm[:, sl]) + base[:, sl]).astype(y_ref.dtype)
            xd = (xs * jnp.exp(tl - csf)).astype(MMD)
            h_ref[gi] = h * jnp.exp(tl) + _dot(bm, xd, TN)

    return pl.pallas_call(
        body, name=name, grid=(1, nc),
        in_specs=[sp["x"], sp["b"], sp["c"], sp["lanes"], sp["lanes"],
                  pl.BlockSpec((GPS, 128, GW), lambda g, c: (di, 0, 0)), sp["drow"]],
        out_specs=[sp["y"], sp["h"]],
        out_shape=[jax.ShapeDtypeStruct((s, 2048), MMD), jax.ShapeDtypeStruct((SSD_GROUPS, nc, SSD_N, GW), MMD)],
        scratch_shapes=[pltpu.VMEM((GPS, SSD_N, GW), F32)],
        compiler_params=_cp(("arbitrary", "arbitrary")),
    )(xc, xc, xc, dt, cs, ex, drow)


def _ssd_bwd(xc, dt, cs, ex, drow, arow, dy, hprev, di, *, name):
    s = xc.shape[0]
    nc = s // CHUNK
    anti = di == 1
    sp = _ssd_specs(nc, not anti)
    trow = 0 if anti else CHUNK - 1
    e1, e2 = _headsum_mats()

    def body(x_ref, b_ref, c_ref, dt_ref, cs_ref, ex_ref, d_ref, a_ref, dy_ref, hp_ref, tri_ref,
             e1_ref, e2_ref, dx_ref, db_ref, dc_ref, ddt_ref, da_ref, dh_ref, w_ref, dxs_ref):
        @pl.when(pl.program_id(1) == 0)
        def _():
            dh_ref[...] = jnp.zeros_like(dh_ref)
            da_ref[...] = jnp.zeros_like(da_ref)

        e1v = e1_ref[...]
        ii, _, mask = _ssd_mask(anti)
        dtv, csv = dt_ref[...], cs_ref[...]
        cst = csv.T
        ddt_acc = jnp.zeros((CHUNK, 128), F32)
        da_acc = jnp.zeros((1, 128), F32)
        for gi in range(GPS):
            lane0 = SSD_HEADS * di + HPG * gi
            cols = slice(gi * GW, (gi + 1) * GW)
            ncols = slice(gi * SSD_N, (gi + 1) * SSD_N)
            ex = ex_ref[gi]
            xb = x_ref[:, cols].astype(F32)
            bm, cm = b_ref[:, ncols], c_ref[:, ncols]
            csr = cst[lane0:lane0 + HPG]
            dym = dy_ref[:, cols]
            dyb = dym.astype(F32)
            hpm = hp_ref[gi, 0]
            hp = hpm.astype(F32)
            dh = dh_ref[gi]
            dhm = dh.astype(MMD)
            dtf = _expand(dtv, ex, 2)
            csf = _expand(csv, ex)
            tl = csf[trow:trow + 1, :]
            e = jnp.exp(csf)
            dec = jnp.exp(tl - csf)
            et = jnp.exp(tl)
            xs = xb * dtf
            xsm = xs.astype(MMD)
            g = _dot(cm, bm, NT)
            z = _dot(cm, hpm)
            bdh = _dot(bm, dhm)
            dg = jnp.zeros((CHUNK, CHUNK), F32)
            wcols = jnp.zeros((CHUNK, CHUNK), F32)
            for r in range(HPG):
                sl = slice(r * SSD_P, (r + 1) * SSD_P)
                lm = jnp.exp(jnp.where(mask, csf[:, r * SSD_P:r * SSD_P + 1] - csr[r:r + 1, :], NEG))
                mm = g * lm
                dm = _dot(dym[:, sl], xsm[:, sl], NT)
                w = dm * mm
                w_ref[gi, :, r * CHUNK:(r + 1) * CHUNK] = w
                wcols = jnp.where(ii == r, jnp.sum(w, axis=0, keepdims=True), wcols)
                dg = dg + dm * lm
                dxs_ref[gi, :, sl] = _dot(mm.astype(MMD), dym[:, sl], TN)
            dxs = dxs_ref[gi] + dec * bdh
            dx_ref[:, cols] = (dxs * dtf + d_ref[:, cols] * dyb).astype(dx_ref.dtype)
            tb = xs * bdh * dec
            d_tot = jnp.sum(tb, axis=0, keepdims=True) + et * jnp.sum(dh * hp, axis=0, keepdims=True)
            d_tot = _headsum(jnp.broadcast_to(d_tot, (8, GW)), e1v)[0:1]
            dcs = (_headsum(dyb * (e * z) - tb, e1v) + _headsum(w_ref[gi], e2_ref[...]) - wcols.T
                   + jnp.where(ii == trow, d_tot, 0.0))
            da = pltpu.roll(_dot_hi(tri_ref[...], dcs), lane0, 1)
            ddt_acc = ddt_acc + da * a_ref[...] + pltpu.roll(_headsum(dxs * xb, e1v), lane0, 1)
            da_acc = da_acc + jnp.sum(da * dtv, axis=0, keepdims=True)
            dgm = dg.astype(MMD)
            dz = (e * dyb).astype(MMD)
            dc_ref[:, ncols] = (_dot(dgm, bm) + _dot(dz, hpm, NT)).astype(dc_ref.dtype)
            db_ref[:, ncols] = (_dot(dgm, cm, TN) + _dot((xs * dec).astype(MMD), dhm, NT)).astype(db_ref.dtype)
            dh_ref[gi] = dh * et + _dot(cm, dz, TN)
        ddt_ref[...] = ddt_acc
        da_ref[...] += da_acc

    const = lambda shape: pl.BlockSpec(shape, lambda g, c: (0,) * len(shape))
    return pl.pallas_call(
        body, name=name, grid=(1, nc),
        in_specs=[sp["x"], sp["b"], sp["c"], sp["lanes"], sp["lanes"],
                  pl.BlockSpec((GPS, 128, GW), lambda g, c: (di, 0, 0)), sp["drow"],
                  const((1, 128)), sp["y"], sp["h"],
                  const((CHUNK, CHUNK)), const((GW, 128)), const((HPG * CHUNK, 128))],
        out_specs=[sp["y"], sp["n"], sp["n"], sp["lanes"], const((1, 128))],
        out_shape=[jax.ShapeDtypeStruct((s, 2048), MMD), jax.ShapeDtypeStruct((s, SSD_GROUPS * SSD_N), MMD),
                   jax.ShapeDtypeStruct((s, SSD_GROUPS * SSD_N), MMD), jax.ShapeDtypeStruct((s, 128), F32),
                   jax.ShapeDtypeStruct((1, 128), F32)],
        scratch_shapes=[pltpu.VMEM((GPS, SSD_N, GW), F32), pltpu.VMEM((GPS, CHUNK, HPG * CHUNK), F32),
                        pltpu.VMEM((GPS, CHUNK, GW), F32)],
        compiler_params=_cp(("arbitrary", "arbitrary")),
    )(xc, xc, xc, dt, cs, ex, drow, arow, dy, hprev, _tri(anti), e1, e2)


def _gnorm_fwd(ya, yb, proj, w, *, name):
    s = ya.shape[0]
    tm = _tile(s, 256)

    def body(a_ref, b_ref, z_ref, w_ref, o_ref):
        zv = z_ref[...].astype(F32)
        t = (a_ref[...].astype(F32) + b_ref[...].astype(F32)) * (zv * _sigmoid(zv))
        r = lax.rsqrt(jnp.mean(t * t, axis=-1, keepdims=True) + EPS)
        o_ref[...] = ((t * r) * w_ref[...]).astype(o_ref.dtype)

    big = pl.BlockSpec((tm, 2048), lambda i: (i, 0))
    row = pl.BlockSpec((1, 2048), lambda i: (0, 0))
    return pl.pallas_call(
        body, name=name, grid=(s // tm,), in_specs=[big, big, big, row], out_specs=big,
        out_shape=jax.ShapeDtypeStruct((s, 2048), MMD), compiler_params=_cp(("arbitrary",)),
    )(ya, yb, proj, w)


def _gnorm_bwd(dout, ya, yb, proj, xc, w, dproj, *, name):
    s = ya.shape[0]
    tm = _tile(s, 256)

    def body(do_ref, a_ref, b_ref, z_ref, x_ref, w_ref, _, dy_ref, dz_ref, dw_ref, dd_ref):
        zv = z_ref[...].astype(F32)
        sg = _sigmoid(zv)
        sz = zv * sg
        y = a_ref[...].astype(F32) + b_ref[...].astype(F32)
        t = y * sz
        r = lax.rsqrt(jnp.mean(t * t, axis=-1, keepdims=True) + EPS)
        nv = t * r
        dov = do_ref[...].astype(F32)
        _acc_rows(dw_ref, jnp.sum(dov * nv, axis=0, keepdims=True), pl.program_id(0) == 0)
        dn = dov * w_ref[...]
        dt_ = r * (dn - nv * jnp.mean(dn * nv, axis=-1, keepdims=True))
        dy = dt_ * sz
        dy_ref[...] = dy.astype(dy_ref.dtype)
        dz_ref[...] = (dt_ * y * (sg * (1.0 + zv * (1.0 - sg)))).astype(dz_ref.dtype)
        _acc_rows(dd_ref, jnp.sum(dy * x_ref[...].astype(F32), axis=0, keepdims=True), pl.program_id(0) == 0)

    big = pl.BlockSpec((tm, 2048), lambda i: (i, 0))
    row = pl.BlockSpec((1, 2048), lambda i: (0, 0))
    return pl.pallas_call(
        body, name=name, grid=(s // tm,), in_specs=[big, big, big, big, big, row, ANY], out_specs=[big, big, row, row],
        out_shape=[jax.ShapeDtypeStruct((s, 2048), MMD), jax.ShapeDtypeStruct(dproj.shape, dproj.dtype),
                   jax.ShapeDtypeStruct((1, 2048), F32), jax.ShapeDtypeStruct((1, 2048), F32)],
        input_output_aliases={6: 1}, compiler_params=_cp(("arbitrary",)),
    )(dout, ya, yb, proj, xc, w, dproj)


def _heads(a, n):
    return a.reshape(a.shape[0], n, HEAD_DIM).transpose(1, 0, 2)


def _unheads(a):
    return a.transpose(1, 0, 2).reshape(a.shape[1], a.shape[0] * HEAD_DIM)


def _local_step(x, target, mod, wts, small, in_weights=None, late_weights=None, late_grads=None, in_grad=None):
    s, d = x.shape
    shift1, scale1, gate1, shift2, scale2, gate2 = [mod[i:i + 1] for i in range(6)]

    h1 = _ln_mod(x, small["norm1_w"], scale1, shift1, name="ln1")
    qk_w = jnp.concatenate([jnp.tile(small["q_norm_w"], (1, N_Q_HEADS)), jnp.tile(small["k_norm_w"], (1, N_KV_HEADS))], axis=1)
    qk_sc = jnp.concatenate([jnp.full((1, N_Q_HEADS * HEAD_DIM), HEAD_DIM ** -0.5, F32),
                             jnp.ones((1, N_KV_HEADS * HEAD_DIM), F32)], axis=1)
    qk_sc2 = jnp.concatenate([jnp.full((1, N_Q_HEADS * HEAD_DIM), HEAD_DIM ** -0.5 * LOG2E, F32),
                              jnp.ones((1, N_KV_HEADS * HEAD_DIM), F32)], axis=1)
    tabs = _rope_tables(s)
    if in_weights is not None:
        wts = {**wts, **in_weights([h1, *tabs])}
    proj = _mm(h1, wts["w_in_p"], name="in_proj", outs=[MMD], tm=512, tn=2944, b_outer=True)
    dt_raw = _mm(h1, wts["w_dt"], name="dt_proj", outs=[F32], tm=512, tn=128)
    qk, qkt = _qk_fwd(proj, qk_w, qk_sc2, tabs, name="qk_fwd")
    qkt = qkt.reshape(N_Q_HEADS + N_KV_HEADS, HEAD_DIM, s)
    k_h = _heads(qk[:, N_Q_HEADS * HEAD_DIM:], N_KV_HEADS)
    v_sd = proj[:, V0:V0 + N_KV_HEADS * HEAD_DIM]
    v_h = _heads(v_sd, N_KV_HEADS)
    vta = jnp.concatenate([v_sd.T.reshape(N_KV_HEADS, HEAD_DIM, s), jnp.ones((N_KV_HEADS, V_AUG - HEAD_DIM, s), MMD)], axis=1)
    ot, lse = _flash_fwd(qkt, vta, name="flash_fwd")
    ot2 = ot.reshape(N_Q_HEADS * HEAD_DIM, s)
    if late_weights is not None:
        wts = {**wts, **late_weights(ot)}

    w8 = jnp.pad(small["conv_w"], ((0, 8 - D_CONV), (0, 0)))
    xc = _conv_fwd(proj, w8, small["conv_b"], name="conv_fwd")
    a_neg = -jnp.exp(small["A_log"])
    arow = jnp.pad(a_neg.reshape(1, 2 * SSD_HEADS), ((0, 0), (0, 128 - 2 * SSD_HEADS)))
    bias_row = jnp.pad(small["dt_bias"].reshape(1, 2 * SSD_HEADS), ((0, 0), (0, 128 - 2 * SSD_HEADS)))
    dt, cs = _dt_fwd(dt_raw, bias_row, arow, name="dt_fwd")
    drow = jnp.repeat(small["ssd_D"], SSD_P, axis=1)
    dirs = [dict(drow=drow), dict(drow=jnp.zeros_like(drow))]
    ex = _expand_mats()
    ys = []
    for di, dd in enumerate(dirs):
        y, dd["hprev"] = _ssd_fwd(xc, dt, cs, ex, dd["drow"], di, name=f"ssd_fwd{di}")
        ys.append(y)
    ssdn = _gnorm_fwd(ys[0], ys[1], proj, small["ssd_norm_w"], name="gnorm_fwd")

    a_o = _mm(ot2, wts["w_attn_out"], name="attn_out", outs=[MMD], ta=True, tm=512, tn=1024)

    def merge_epi(acc, ao, ga, gs):
        return (_sigmoid(ga.astype(F32)) * ao.astype(F32) + _sigmoid(gs.astype(F32)) * acc, acc)

    merged, b_o = _mm(ssdn, wts["w_ssd_out"], name="ssd_out", outs=[MMD, MMD], tm=512, tn=1024,
                      extras=[(a_o, "tile", 0), (proj, "tile", GA0), (proj, "tile", GS0)], epi=merge_epi)

    def res_epi(acc, res, gate):
        return (res + gate * acc, acc)

    x1, mo = _mm(merged, wts["w_o"], name="w_o", outs=[F32, MMD], tm=512, tn=1024,
                 extras=[(x, "tile", 0), (gate1, "row", 0)], epi=res_epi)
    h2 = _ln_mod(x1, small["norm2_w"], scale2, shift2, name="ln2")

    def relu2_epi(acc):
        rl = jnp.maximum(acc, 0.0)
        return (rl * rl, rl)

    act, rl = _mm(h2, wts["w_mlp1"], name="mlp1", outs=[MMD, MMD], tm=1024, tn=1024, epi=relu2_epi, b_outer=True)

    def loss_epi(acc, res, gate, tgt):
        return ((res + gate * acc - tgt) * (1.0 / d), acc)

    dy, ffo = _mm(act, wts["w_mlp2"], name="mlp2", outs=[F32, MMD], tm=512, tn=1024, vmem=VMEM_BIG,
                  extras=[(x1, "tile", 0), (gate2, "row", 0), (target, "tile", 0)], epi=loss_epi)
    loss = _sumsq(dy, name="loss") * (0.5 * d)

    gw = {}
    gs_ = {}
    dffo, dgate2 = _gate_bwd(dy, ffo, gate2, name="gate2_bwd")
    dpre = _mm(dffo, wts["w_mlp2"], name="mlp2_dx", outs=[MMD], nt=True, tm=1024, tn=1024, b_outer=True,
               extras=[(rl, "tile", 0)], epi=lambda acc, r: (acc * (2.0 * r.astype(F32)),))
    gw["w_mlp2"] = _mm_tn(act, dffo, name="mlp2_dw")
    dh2 = _mm(dpre, wts["w_mlp1"], name="mlp1_dx", outs=[F32], nt=True, tm=1024, tn=1024, vmem=VMEM_BIG)
    gw["w_mlp1"] = _mm_tn(h2, dpre, name="mlp1_dw")
    dx1, dshift2, dscale2, gs_["norm2_w"] = _ln_mod_bwd(dh2, x1, small["norm2_w"], scale2, dy, name="ln2_bwd")
    dmo, dgate1 = _gate_bwd(dx1, mo, gate1, name="gate1_bwd")

    def merge_bwd_epi(acc, ao, bo, ga, gs):
        sa, ss = _sigmoid(ga.astype(F32)), _sigmoid(gs.astype(F32))
        return (acc * sa, acc * ss, acc * ao.astype(F32) * sa * (1.0 - sa), acc * bo.astype(F32) * ss * (1.0 - ss))

    da_o, db_o, dga, dgs = _mm(dmo, wts["w_o"], name="w_o_dx", outs=[MMD] * 4, nt=True, tm=512, tn=1024,
                               extras=[(a_o, "tile", 0), (b_o, "tile", 0), (proj, "tile", GA0), (proj, "tile", GS0)],
                               epi=merge_bwd_epi)
    gw["w_o"] = _mm_tn(merged, dmo, name="w_o_dw")
    dot = _mm(wts["w_attn_out"], da_o, name="attn_out_dx", outs=[MMD], nt=True, tm=1024, tn=1024)
    gw["w_attn_out"] = _mm(ot2, da_o, name="attn_out_dw", outs=[F32], tm=256, tn=512, vmem=VMEM_BIG)
    dssdn = _mm(db_o, wts["w_ssd_out"], name="ssd_out_dx", outs=[MMD], nt=True, tm=512, tn=2048)
    gw["w_ssd_out"] = _mm_tn(ssdn, db_o, name="ssd_out_dw")

    dproj = lax.dynamic_update_slice(lax.empty((s, PW), MMD), jnp.concatenate([dga, dgs], axis=1), (0, GA0))

    norm_w = small["ssd_norm_w"] if late_grads is None else small["ssd_norm_w"] + late_grads(gw)
    dyssd, dproj, gs_["ssd_norm_w"], dd_row = _gnorm_bwd(dssdn, ys[0], ys[1], proj, xc, norm_w, dproj, name="gnorm_bwd")
    gs_["ssd_D"] = dd_row.reshape(SSD_HEADS, SSD_P).sum(axis=1).reshape(1, SSD_HEADS)
    dxc, ddts, das = [], [], []
    for di, dd in enumerate(dirs):
        dxs, dbm, dcm, ddt_d, da_d = _ssd_bwd(xc, dt, cs, ex, dd["drow"], arow, dyssd, dd["hprev"], di, name=f"ssd_bwd{di}")
        dxc.append((dxs, dbm, dcm))
        ddts.append(ddt_d)
        das.append(da_d)
    dw8, db, col0 = [], [], 0
    for part, (ga, gb) in enumerate(zip(*dxc)):
        dproj, dw_part, db_part = _conv_bwd(proj, col0, ga, gb, w8, small["conv_b"], dproj, name=f"conv_bwd{part}")
        dw8.append(dw_part)
        db.append(db_part)
        col0 += ga.shape[1]
    gs_["conv_w"] = jnp.concatenate(dw8, axis=1)[0:D_CONV]
    gs_["conv_b"] = jnp.concatenate(db, axis=1)
    gs_["A_log"] = (das[0] + das[1])[:, 0:2 * SSD_HEADS].reshape(2, SSD_HEADS) * a_neg
    dproj, dbias = _dt_bwd(ddts[0], ddts[1], dt_raw, bias_row, dproj, name="dt_bwd")
    gs_["dt_bias"] = dbias[:, 0:2 * SSD_HEADS].reshape(2, SSD_HEADS)

    dqt, dk_h, dv_h = _flash_bwd(qkt, k_h, v_h, dot.reshape(N_Q_HEADS, HEAD_DIM, s), ot, lse, name="flash_bwd")
    dproj, dqk_w = _qk_bwd(dqt.reshape(N_Q_HEADS * HEAD_DIM, s), dk_h.transpose(0, 2, 1).reshape(N_KV_HEADS * HEAD_DIM, s),
                           proj, qk_w, qk_sc, tabs, dproj, name="qk_bwd")
    gs_["q_norm_w"] = dqk_w[:, 0:N_Q_HEADS * HEAD_DIM].reshape(N_Q_HEADS, HEAD_DIM).sum(axis=0, keepdims=True)
    gs_["k_norm_w"] = dqk_w[:, N_Q_HEADS * HEAD_DIM:].reshape(N_KV_HEADS, HEAD_DIM).sum(axis=0, keepdims=True)
    dproj = lax.dynamic_update_slice(dproj, _unheads(dv_h).astype(MMD), (0, V0))

    gw["w_in_p"] = _mm_tn(h1, dproj, name="in_proj_dw", tk=512, tn=2944, tmm=2048, vmem=VMEM_BIG)
    zero_row = jnp.zeros((1, d), F32) if in_grad is None else jnp.zeros((1, d), F32) + in_grad(gw["w_in_p"])[0:1, 0:1]
    dh1 = _mm(dproj, wts["w_in_p"], name="in_proj_dx", outs=[F32], nt=True, tm=256, tn=1024, vmem=VMEM_BIG,
              extras=[(zero_row, "row", 0)], epi=lambda acc, r: (acc + r,))
    grad_x, dshift1, dscale1, gs_["norm1_w"] = _ln_mod_bwd(dh1, x, small["norm1_w"], scale1, dx1, name="ln1_bwd")
    dmod = jnp.concatenate([dshift1, dscale1, dgate1, dshift2, dscale2, dgate2], axis=0)
    return loss, grad_x, dmod, gw, gs_


N_DEV = 8
N_CHIP = 4
ANY = pl.BlockSpec(memory_space=pl.ANY)


def _place():
    return lax.axis_index("x"), lax.axis_index("y"), lax.axis_index("c")


def _allgather8(v, *, name):
    m_per, n = v.shape

    def body(x_ref, out_ref, send_sems, recv_sems, local_sem):
        x, y, c = _place()
        me, sibling = (x, y, c), (x, y, 1 - c)
        chips = [(1 - x, y), (x, 1 - y), (1 - x, 1 - y)]

        def rows(px, py, pc):
            return out_ref.at[pl.ds((4 * px + 2 * py + pc) * m_per, m_per), :]

        def copy(k, block, to, src=None):
            return pltpu.make_async_remote_copy(
                src_ref=rows(*block) if src is None else src, dst_ref=rows(*block),
                send_sem=send_sems.at[k], recv_sem=recv_sems.at[k], device_id=to, device_id_type=MESH)

        mine = pltpu.make_async_copy(x_ref, rows(*me), local_sem)
        mine.start()
        first = [copy(0, me, sibling, src=x_ref)]
        first += [copy(1 + j, me, (*chip, c), src=x_ref) for j, chip in enumerate(chips)]
        for cp in first:
            cp.start()
        passed = [copy(4 + j, (*chip, c), sibling) for j, chip in enumerate(chips)]
        for j, chip in enumerate(chips):
            copy(1 + j, (*chip, c), me).wait_recv()
            passed[j].start()
        copy(0, sibling, me).wait_recv()
        for j, chip in enumerate(chips):
            copy(4 + j, (*chip, 1 - c), me).wait_recv()
        for cp in first + passed:
            cp.wait_send()
        mine.wait()

    return pl.pallas_call(
        body, name=name, out_shape=jax.ShapeDtypeStruct((N_DEV * m_per, n), v.dtype),
        in_specs=[pl.BlockSpec(memory_space=pltpu.VMEM)], out_specs=pl.BlockSpec(memory_space=pltpu.VMEM),
        scratch_shapes=[pltpu.SemaphoreType.DMA((7,)), pltpu.SemaphoreType.DMA((7,)), pltpu.SemaphoreType.DMA],
    )(v)


HBM = pl.BlockSpec(memory_space=pltpu.HBM)
SEM = pl.BlockSpec(memory_space=pltpu.SEMAPHORE)


def _chips_copies(x_ref, land_ref, sems, scatter, half=False):
    x, y, c = _place()
    k = 2 * x + y
    chips = [(1 - x, y), (x, 1 - y), (1 - x, 1 - y)]
    ids = [2 * cx + cy for cx, cy in chips]
    if half:
        hr = x_ref.shape[0] // 2
        rows = pl.ds(pl.multiple_of(c * hr, 16), hr)

    def copy(j, slot):
        src = x_ref.at[ids[j]] if scatter else (x_ref.at[rows] if half else x_ref)
        dst = land_ref.at[slot, rows] if half else land_ref.at[slot]
        return pltpu.make_async_remote_copy(src_ref=src, dst_ref=dst, send_sem=sems[j], recv_sem=sems[3 + j],
                                            device_id=(*chips[j], c), device_id_type=MESH)

    return [copy(j, k) for j in range(3)], [copy(j, ids[j]) for j in range(3)]


def _chips_start(src, scatter, half=False, *, name):
    shape = src.shape if scatter else (N_CHIP,) + tuple(src.shape)

    def body(x_ref, land_ref, *rest):
        sems, token = rest[0:6], rest[8]
        for cp in _chips_copies(x_ref, land_ref, sems, scatter, half)[0]:
            cp.start()
        token[...] = jnp.zeros_like(token)

    out = pl.pallas_call(
        body, name=name,
        out_shape=(pltpu.SemaphoreType.DMA(()),) * 6 + (pltpu.HBM(src.shape, src.dtype), pltpu.HBM(shape, src.dtype),
                                                       jax.ShapeDtypeStruct((8, 128), F32)),
        in_specs=(HBM, HBM), out_specs=(SEM,) * 6 + (HBM, HBM, pl.BlockSpec(memory_space=pltpu.VMEM)),
        input_output_aliases={0: 6, 1: 7},
        compiler_params=pltpu.CompilerParams(has_side_effects=pltpu.SideEffectType.DATAFLOW_SIDE_EFFECTING),
    )(pltpu.with_memory_space_constraint(src, pltpu.HBM),
      pltpu.with_memory_space_constraint(lax.empty(shape, src.dtype), pltpu.HBM))
    return out[0:6], out[6], out[7], out[8]


def _chips_wait(sems, src, land, after, scatter, half=False, *, name):
    after = list(after) if isinstance(after, (list, tuple)) else [after]

    def body(x_ref, land_ref, *rest):
        sems_ = rest[0:6]
        for cp in _chips_copies(x_ref, land_ref, sems_, scatter, half)[1]:
            cp.wait_send()
            cp.wait_recv()

    return pl.pallas_call(
        body, name=name, out_shape=(pltpu.HBM(src.shape, src.dtype), pltpu.HBM(land.shape, land.dtype)),
        in_specs=(HBM, HBM) + (SEM,) * 6 + (ANY,) * len(after), out_specs=(HBM, HBM), input_output_aliases={0: 0, 1: 1},
        compiler_params=pltpu.CompilerParams(has_side_effects=pltpu.SideEffectType.DATAFLOW_SIDE_EFFECTING),
    )(src, land, *sems, *after)


def _row_tile(r, pref=512):
    return max(t for t in range(16, pref + 1, 16) if r % t == 0)


def _pair_complete(land, *, name):
    r = land.shape[1]
    hr = r // 2
    assert r == 2 * hr and hr % 16 == 0

    def body(in_ref, out_ref, send_sems, recv_sems):
        x, y, c = _place()
        ids = [2 * cx + cy for cx, cy in [(1 - x, y), (x, 1 - y), (1 - x, 1 - y)]]
        mine_rows = pl.ds(pl.multiple_of(c * hr, 16), hr)
        other_rows = pl.ds(pl.multiple_of((1 - c) * hr, 16), hr)

        def copy(j, rows):
            return pltpu.make_async_remote_copy(
                src_ref=in_ref.at[ids[j], mine_rows], dst_ref=out_ref.at[ids[j], rows], send_sem=send_sems.at[j],
                recv_sem=recv_sems.at[j], device_id=(x, y, 1 - c), device_id_type=MESH)

        sends = [copy(j, mine_rows) for j in range(3)]
        for cp in sends:
            cp.start()
        for j in range(3):
            copy(j, other_rows).wait_recv()
        for cp in sends:
            cp.wait_send()

    return pl.pallas_call(
        body, name=name, out_shape=jax.ShapeDtypeStruct(land.shape, land.dtype), in_specs=[ANY], out_specs=ANY,
        input_output_aliases={0: 0},
        scratch_shapes=[pltpu.SemaphoreType.DMA((3,)), pltpu.SemaphoreType.DMA((3,))],
    )(land)


def _pair_swap(a, *, name):
    n, r, cols = a.shape
    hr = r // 2

    def body(x_ref, out_ref, send_sem, recv_sem):
        x, y, c = _place()
        other_rows = pl.ds(pl.multiple_of((1 - c) * hr, 16), hr)
        cp = pltpu.make_async_remote_copy(src_ref=x_ref.at[:, other_rows], dst_ref=out_ref, send_sem=send_sem,
                                          recv_sem=recv_sem, device_id=(x, y, 1 - c), device_id_type=MESH)
        cp.start()
        cp.wait()

    return pl.pallas_call(
        body, name=name, out_shape=jax.ShapeDtypeStruct((n, hr, cols), a.dtype), in_specs=[ANY], out_specs=ANY,
        scratch_shapes=[pltpu.SemaphoreType.DMA, pltpu.SemaphoreType.DMA],
    )(a)


def _sibling_copy(a, *, name):
    def body(x_ref, out_ref, send_sem, recv_sem):
        x, y, c = _place()
        cp = pltpu.make_async_remote_copy(src_ref=x_ref, dst_ref=out_ref, send_sem=send_sem, recv_sem=recv_sem,
                                          device_id=(x, y, 1 - c), device_id_type=MESH)
        cp.start()
        cp.wait()

    return pl.pallas_call(
        body, name=name, out_shape=jax.ShapeDtypeStruct(a.shape, a.dtype), in_specs=[ANY], out_specs=ANY,
        scratch_shapes=[pltpu.SemaphoreType.DMA, pltpu.SemaphoreType.DMA],
    )(a)


def _sum_slots(a, own, *, name):
    _, r, c = a.shape
    tr = _row_tile(r, 256)

    def body(a_ref, own_ref, o_ref):
        k = 2 * lax.axis_index("x") + lax.axis_index("y")
        acc = None
        for j in range(N_CHIP):
            term = jnp.where(k == j, own_ref[j], a_ref[j]).astype(F32)
            acc = term if acc is None else acc + term
        o_ref[...] = acc

    spec = pl.BlockSpec((N_CHIP, tr, c), lambda i: (0, i, 0))
    return pl.pallas_call(
        body, name=name, grid=(r // tr,), in_specs=[spec, spec],
        out_specs=pl.BlockSpec((tr, c), lambda i: (i, 0)), out_shape=jax.ShapeDtypeStruct((r, c), F32),
        compiler_params=_cp(("arbitrary",)),
    )(a, own)


def _add2(a, b, *, name):
    r, c = a.shape
    tr = _row_tile(r)

    def body(a_ref, b_ref, o_ref):
        o_ref[...] = (a_ref[...].astype(F32) + b_ref[...].astype(F32)).astype(o_ref.dtype)

    spec = pl.BlockSpec((tr, c), lambda i: (i, 0))
    return pl.pallas_call(
        body, name=name, grid=(r // tr,), in_specs=[spec, spec], out_specs=spec,
        out_shape=jax.ShapeDtypeStruct((r, c), a.dtype), compiler_params=_cp(("arbitrary",)),
    )(a, b)


BIG = ("w_in", "w_mlp1", "w_attn_out", "w_ssd_out", "w_o", "w_mlp2")
COL_SHARDED = ("w_mlp1", "w_in")
ROW_SHARDED = ("w_attn_out", "w_ssd_out", "w_o", "w_mlp2")
LATE = ROW_SHARDED + ("w_mlp1",)
SMALL = ("b_ada", "norm1_w", "norm2_w", "q_norm_w", "k_norm_w", "conv_b", "A_log", "dt_bias", "ssd_D", "ssd_norm_w")
NAMES = ("w_ada", "b_ada", "norm1_w", "norm2_w", "w_in", "q_norm_w", "k_norm_w", "conv_w", "conv_b", "A_log", "dt_bias",
         "ssd_D", "ssd_norm_w", "w_attn_out", "w_ssd_out", "w_o", "w_mlp1", "w_mlp2")
W_IN_COLS = 8768


def _permute_in(w):
    return jnp.concatenate([w[:, 4608:6656], w[:, 6720:8768], w[:, 1536:4608], w[:, 0:1536], w[:, 6656:6720],
                            jnp.zeros((w.shape[0], PW - W_IN_COLS), w.dtype)], axis=1)


def _unpermute_in(wp):
    return jnp.concatenate([wp[:, Q0:DT0], wp[:, XS0:Q0], wp[:, Z0:GA0], wp[:, DT0:DT0 + 64], wp[:, GA0:XS0]], axis=1)


def _pad_to(v, n):
    return jnp.pad(v, (0, n - v.shape[0]))


def _step(w, m, v, loss_target):
    xi, yi, ci = _place()
    chip = 2 * xi + yi
    dev = 4 * xi + 2 * yi + ci
    x, tgt = w["x"], loss_target
    d = x.shape[1]

    cw = w["conv_w"].shape[1]
    v0 = _pad_to(jnp.concatenate([w["c"].reshape(-1), w["conv_w"].reshape(-1)]), 5120).reshape(8, 640)
    g0 = _allgather8(v0, name="ag_cond").reshape(N_DEV, 5120)
    c_all = g0[:, 0:d]
    conv_w = jnp.concatenate([g0[2 * k, d:d + D_CONV * cw].reshape(D_CONV, cw) for k in range(N_CHIP)], axis=1)
    sc = _silu_cast(c_all, name="silu_c")
    modp = _mm(sc, w["w_ada"].astype(MMD), name="ada_fwd", outs=[F32], tm=8, tn=512)
    g1 = _allgather8(modp, name="ag_mod").reshape(N_DEV, N_DEV, modp.shape[1])
    mod_all = jnp.concatenate([g1[2 * k] for k in range(N_CHIP)], axis=1)
    mod = (lax.dynamic_slice_in_dim(mod_all, dev, 1, axis=0) + w["b_ada"]).reshape(6, d)

    mine, mod = lax.optimization_barrier((w["w_in"].astype(MMD), mod))
    in_sems, in_src, in_land, in_token = _chips_start(mine, False, True, name="ag_w_in_start")
    mod = mod + in_token[0:1, 0:1]
    small = {n: w[n] for n in SMALL if n != "b_ada"}
    small["conv_w"] = conv_w
    started = {}

    late_mine = jnp.concatenate([w[n].astype(MMD) for n in LATE], axis=0)

    def in_weights(after):
        src, land = _chips_wait(in_sems, in_src, in_land, [*after, late_mine], False, True, name="ag_w_in_wait")
        land = _pair_complete(land, name="ag_w_in_pair")
        late, land = lax.optimization_barrier((late_mine, land))
        sems, late_src, late_land, token = _chips_start(late, False, name="ag_late_start")
        started["ag_late"] = (sems, late_src, late_land)
        w_in = jnp.concatenate([jnp.where(chip == k, src, land[k]) for k in range(N_CHIP)], axis=1)
        w_dt = jnp.pad(w_in[:, 6656:6720], ((0, 0), (0, 64))) + token[0:1, 0:1].astype(MMD)
        return {"w_in_p": _permute_in(w_in), "w_dt": w_dt}

    def late_weights(after):
        src, land = _chips_wait(*started["ag_late"], after, False, name="ag_late_wait")
        out, o = {}, 0
        for n in LATE:
            rows = w[n].shape[0]
            parts = [jnp.where(chip == k, src[o:o + rows], land[k, o:o + rows]) for k in range(N_CHIP)]
            out[n] = jnp.concatenate(parts, axis=1 if n in COL_SHARDED else 0)
            o += rows
        return out

    def pair_sums(slots, tag):
        _, rows, cols = slots.shape
        hr = rows // 2
        theirs = _pair_swap(slots, name="rs_pair_" + tag)
        ours = lax.dynamic_slice_in_dim(slots, ci * hr, hr, axis=1)
        pair = _add2(ours.reshape(N_CHIP * hr, cols), theirs.reshape(N_CHIP * hr, cols), name="rs_pair_sum_" + tag)
        return pair.reshape(N_CHIP, hr, cols)

    def finish(recv, pair, tag):
        half = _sum_slots(recv, pair, name="rs_sum_" + tag)
        other = _sibling_copy(half, name="rs_sibling_" + tag)
        return jnp.where(ci == 0, jnp.concatenate([half, other], axis=0), jnp.concatenate([other, half], axis=0))

    def late_grads(gw):
        slots = []
        for k in range(N_CHIP):
            parts = []
            for n in LATE:
                rows = w[n].shape[0]
                blk = gw[n][:, k * rows:(k + 1) * rows] if n in COL_SHARDED else gw[n][k * rows:(k + 1) * rows]
                parts.append(blk.astype(MMD))
            slots.append(jnp.concatenate(parts, axis=0))
        pair = pair_sums(jnp.stack(slots), "late")
        sems, src, land, token = _chips_start(pair, True, name="rs_late_start")
        started["late"] = (sems, src, land)
        return token[0:1, 0:1]

    def in_grad(g):
        g_in = _unpermute_in(g)
        cols_in = w["w_in"].shape[1]
        pair = pair_sums(jnp.stack([g_in[:, k * cols_in:(k + 1) * cols_in].astype(MMD) for k in range(N_CHIP)]), "w_in")
        sems, src, land, token = _chips_start(pair, True, name="rs_w_in_start")
        started["w_in"] = (sems, src, land)
        return token

    loss, grad_x, dmod, gw, gs = _local_step(x, tgt, mod, {}, small, in_weights, late_weights, late_grads, in_grad)

    grads = {}
    pair, land = _chips_wait(*started["w_in"], grad_x, True, name="rs_w_in_wait")
    grads["w_in"] = finish(land, pair, "w_in")
    pair, land = _chips_wait(*started["late"], grad_x, True, name="rs_late_wait")
    total, o = finish(land, pair, "late"), 0
    for n in LATE:
        rows = w[n].shape[0]
        grads[n] = total[o:o + rows]
        o += rows

    order = ([dmod.reshape(-1)] + [gs[n].reshape(-1) for n in SMALL if n != "b_ada"] + [gs["conv_w"].reshape(-1)]
             + [loss.reshape(-1)])
    vec = jnp.concatenate(order)
    n_small = vec.shape[0]
    n_pad = -(-n_small // 1024) * 1024
    g2 = _allgather8(_pad_to(vec, n_pad).reshape(8, n_pad // 8), name="ag_small")
    tot = _rows_sum(g2, N_DEV, name="small_sum").reshape(-1)
    loss = tot[n_small - 1]
    dmod_all = g2.reshape(N_DEV, n_pad)[:, 0:6 * d]
    off = 0
    for n in SMALL:
        grads[n] = tot[off:off + w[n].size].reshape(w[n].shape)
        off += w[n].size
    conv_full = tot[off:off + D_CONV * N_CHIP * cw].reshape(D_CONV, N_CHIP * cw)
    grads["conv_w"] = lax.dynamic_slice_in_dim(conv_full, chip * cw, cw, axis=1)
    ada_cols = w["w_ada"].shape[1]
    dmod_mine = lax.dynamic_slice_in_dim(dmod_all, chip * ada_cols, ada_cols, axis=1).astype(MMD)
    grads["w_ada"] = _mm_tn(sc, dmod_mine, name="ada_dw", tk=512, tn=512, tmm=8)

    delta, new_m, new_v = {}, {}, {}
    pack = lambda t: jnp.concatenate([t[n].reshape(-1) for n in SMALL]).reshape(1, -1)
    ds_, ms_, vs_ = _adamw(pack(w), pack(grads), pack(m), pack(v), name="adamw_small")
    off = 0
    for n in SMALL:
        for dst, src in ((delta, ds_), (new_m, ms_), (new_v, vs_)):
            dst[n] = src[0, off:off + w[n].size].reshape(w[n].shape)
        off += w[n].size
    for n in ("w_ada", "conv_w") + BIG:
        delta[n], new_m[n], new_v[n] = _adamw(w[n], grads[n], m[n], v[n], name="adamw_" + n)
    return loss, grad_x, grads, delta, new_m, new_v


def kernel(x, c, w_ada, b_ada, norm1_w, norm2_w, w_in, q_norm_w, k_norm_w, conv_w, conv_b, A_log, dt_bias, ssd_D, ssd_norm_w, w_attn_out, w_ssd_out, w_o, w_mlp1, w_mlp2, loss_target, m_w_ada, m_b_ada, m_norm1_w, m_norm2_w, m_w_in, m_q_norm_w, m_k_norm_w, m_conv_w, m_conv_b, m_A_log, m_dt_bias, m_ssd_D, m_ssd_norm_w, m_w_attn_out, m_w_ssd_out, m_w_o, m_w_mlp1, m_w_mlp2, v_w_ada, v_b_ada, v_norm1_w, v_norm2_w, v_w_in, v_q_norm_w, v_k_norm_w, v_conv_w, v_conv_b, v_A_log, v_dt_bias, v_ssd_D, v_ssd_norm_w, v_w_attn_out, v_w_ssd_out, v_w_o, v_w_mlp1, v_w_mlp2):
    args = dict(locals())
    strip = lambda a: a[0] if a.ndim == 3 else a
    w = {n: strip(args[n]) for n in NAMES + ("x", "c")}
    m = {n: strip(args["m_" + n]) for n in NAMES}
    v = {n: strip(args["v_" + n]) for n in NAMES}
    loss, grad_x, grads, delta, new_m, new_v = _step(w, m, v, loss_target[0])
    like = lambda t, n: t.reshape(args[n].shape)
    return (loss, grad_x[None], *[like(grads[n], n) for n in NAMES], *[like(delta[n], n) for n in NAMES],
            *[like(new_m[n], n) for n in NAMES], *[like(new_v[n], n) for n in NAMES])
```

```python
import math

import jax
import jax.numpy as jnp
from jax import lax
from jax.experimental import pallas as pl
from jax.experimental.pallas import tpu as pltpu

F32 = jnp.float32
MMD = jnp.bfloat16
EPS = 1e-6
NEG = -1e30
MIB = 1024 * 1024
VMEM_BIG = 56 * MIB
VMEM_MID = 40 * MIB

GRID_W = 64
N_Q_HEADS, N_KV_HEADS, HEAD_DIM = 16, 4, 64
ROPE_THETA = 10000.0
SSD_HEADS, SSD_GROUPS, SSD_P, SSD_N, CHUNK = 32, 4, 64, 128, 128
HPG = SSD_HEADS // SSD_GROUPS
D_CONV = 5
ADAM_LR, ADAM_B1, ADAM_B2, ADAM_EPS, ADAM_WD, ADAM_STEP = 0.001, 0.9, 0.999, 1e-08, 0.01, 10

Z0, GA0, GS0, XS0, B0, C0, Q0, K0, V0, DT0, PW = 0, 2048, 3072, 4096, 6144, 6656, 7168, 8192, 8448, 8704, 8832

MESH = pl.DeviceIdType.MESH
NT = (((1,), (1,)), ((), ()))
TN = (((0,), (0,)), ((), ()))


def _cp(sem=None, vmem=VMEM_MID):
    return pltpu.CompilerParams(dimension_semantics=sem, vmem_limit_bytes=vmem)


def _tile(n, pref):
    t = min(n, pref)
    while n % t:
        t //= 2
    return t


def _dot(a, b, dims=None):
    if dims is None:
        return jnp.dot(a, b, preferred_element_type=F32)
    return lax.dot_general(a, b, dims, preferred_element_type=F32)


def _dot_hi(a01, b):
    a = a01.astype(jnp.bfloat16)
    h1 = b.astype(jnp.bfloat16)
    r1 = b - h1.astype(F32)
    h2 = r1.astype(jnp.bfloat16)
    return _dot(a, h1) + _dot(a, h2) + _dot(a, (r1 - h2.astype(F32)).astype(jnp.bfloat16))


def _sigmoid(x):
    return jax.nn.sigmoid(x)


def _mm(a, b, *, name, outs, nt=False, ta=False, extras=(), epi=None, tm=512, tn=512, n=None, b_outer=False,
        vmem=VMEM_MID):
    assert not (nt and ta)
    k, m = a.shape if ta else a.shape[::-1]
    if n is None:
        n = b.shape[0] if nt else b.shape[1]
    tm, tn = _tile(m, tm), _tile(n, tn)
    gi, gj = m // tm, n // tn
    if b_outer:
        grid = (gj, gi)
        ij = lambda p, q: (q, p)
    else:
        grid = (gi, gj)
        ij = lambda p, q: (p, q)
    if ta:
        a_spec = pl.BlockSpec((k, tm), lambda p, q: (0, ij(p, q)[0]))
    else:
        a_spec = pl.BlockSpec((tm, k), lambda p, q: (ij(p, q)[0], 0))
    if nt:
        b_spec = pl.BlockSpec((tn, k), lambda p, q: (ij(p, q)[1], 0))
    else:
        b_spec = pl.BlockSpec((k, tn), lambda p, q: (0, ij(p, q)[1]))
    e_specs = []
    for arr, kind, off in extras:
        ob = off // tn
        assert off % tn == 0
        if kind == "tile":
            e_specs.append(pl.BlockSpec((tm, tn), lambda p, q, ob=ob: (ij(p, q)[0], ob + ij(p, q)[1])))
        else:
            e_specs.append(pl.BlockSpec((1, tn), lambda p, q, ob=ob: (0, ob + ij(p, q)[1])))
    ne = len(extras)

    def body(a_ref, b_ref, *rest):
        acc = _dot(a_ref[...], b_ref[...], NT if nt else (TN if ta else None))
        res = epi(acc, *[e[...] for e in rest[:ne]]) if epi is not None else (acc,)
        for o_ref, r in zip(rest[ne:], res):
            o_ref[...] = r.astype(o_ref.dtype)

    out = pl.pallas_call(
        body, name=name, grid=grid,
        in_specs=[a_spec, b_spec] + e_specs,
        out_specs=[pl.BlockSpec((tm, tn), lambda p, q: ij(p, q)) for _ in outs],
        out_shape=[jax.ShapeDtypeStruct((m, n), dt) for dt in outs],
        compiler_params=_cp(("arbitrary", "arbitrary"), vmem),
    )(a, b, *[e[0] for e in extras])
    return out if len(outs) > 1 else out[0]


def _mm_tn(a, g, *, name, tk=512, tn=1024, tmm=4096, vmem=VMEM_MID):
    m, k = a.shape
    n = g.shape[1]
    tk, tn, tmm = _tile(k, tk), _tile(n, tn), _tile(m, tmm)

    def body(a_ref, g_ref, o_ref):
        p = _dot(a_ref[...], g_ref[...], TN)

        @pl.when(pl.program_id(2) == 0)
        def _():
            o_ref[...] = p

        @pl.when(pl.program_id(2) > 0)
        def _():
            o_ref[...] += p

    return pl.pallas_call(
        body, name=name, grid=(k // tk, n // tn, m // tmm),
        in_specs=[pl.BlockSpec((tmm, tk), lambda i, j, r: (r, i)), pl.BlockSpec((tmm, tn), lambda i, j, r: (r, j))],
        out_specs=pl.BlockSpec((tk, tn), lambda i, j, r: (i, j)),
        out_shape=jax.ShapeDtypeStruct((k, n), F32),
        compiler_params=_cp(("arbitrary", "arbitrary", "arbitrary"), vmem),
    )(a, g)


def _adamw(w, g, m, v, *, name):
    r, c = w.shape
    tr = _tile(r, 256) if r % 8 == 0 else r

    def body(w_ref, g_ref, m_ref, v_ref, d_ref, nm_ref, nv_ref):
        gg = g_ref[...]
        nm = ADAM_B1 * m_ref[...] + (1.0 - ADAM_B1) * gg
        nv = ADAM_B2 * v_ref[...] + (1.0 - ADAM_B2) * jnp.square(gg)
        m_hat = nm / (1.0 - ADAM_B1 ** ADAM_STEP)
        v_hat = nv / (1.0 - ADAM_B2 ** ADAM_STEP)
        d_ref[...] = -ADAM_LR * (m_hat / (jnp.sqrt(v_hat) + ADAM_EPS) + ADAM_WD * w_ref[...])
        nm_ref[...] = nm
        nv_ref[...] = nv

    spec = pl.BlockSpec((tr, c), lambda i: (i, 0))
    return pl.pallas_call(
        body, name=name, grid=(r // tr,), in_specs=[spec] * 4, out_specs=[spec] * 3,
        out_shape=[jax.ShapeDtypeStruct((r, c), F32)] * 3, compiler_params=_cp(("arbitrary",)),
    )(w, g, m, v)


def _rows_sum(a, groups, *, name):
    r = a.shape[0] // groups

    def body(a_ref, o_ref):
        acc = a_ref[0:r, :]
        for d in range(1, groups):
            acc = acc + a_ref[d * r:(d + 1) * r, :]
        o_ref[...] = acc

    return pl.pallas_call(body, name=name, out_shape=jax.ShapeDtypeStruct((r, a.shape[1]), F32))(a)


def _silu_cast(a, *, name):
    def body(a_ref, o_ref):
        x = a_ref[...]
        o_ref[...] = (x * _sigmoid(x)).astype(o_ref.dtype)

    return pl.pallas_call(body, name=name, out_shape=jax.ShapeDtypeStruct(a.shape, MMD))(a)


def _sumsq(a, *, name):
    m, n = a.shape
    tm = _tile(m, 512)

    def body(a_ref, o_ref):
        x = a_ref[...]
        p = jnp.sum(jnp.sum(x * x, axis=1, keepdims=True), axis=0, keepdims=True)

        @pl.when(pl.program_id(0) == 0)
        def _():
            o_ref[...] = p

        @pl.when(pl.program_id(0) > 0)
        def _():
            o_ref[...] += p

    return pl.pallas_call(
        body, name=name, grid=(m // tm,), in_specs=[pl.BlockSpec((tm, n), lambda i: (i, 0))],
        out_specs=pl.BlockSpec((1, 1), lambda i: (0, 0)), out_shape=jax.ShapeDtypeStruct((1, 1), F32),
        compiler_params=_cp(("arbitrary",)),
    )(a)


def _acc_rows(o_ref, p, first):
    @pl.when(first)
    def _():
        o_ref[...] = p

    @pl.when(jnp.logical_not(first))
    def _():
        o_ref[...] += p


def _ln_mod(x, w, scale, shift, *, name):
    s, d = x.shape
    tm = _tile(s, 512)

    def body(x_ref, w_ref, sc_ref, sh_ref, o_ref):
        xv = x_ref[...]
        r = lax.rsqrt(jnp.mean(xv * xv, axis=-1, keepdims=True) + EPS)
        o_ref[...] = ((xv * r) * w_ref[...] * (1.0 + sc_ref[...]) + sh_ref[...]).astype(o_ref.dtype)

    row = pl.BlockSpec((1, d), lambda i: (0, 0))
    big = pl.BlockSpec((tm, d), lambda i: (i, 0))
    return pl.pallas_call(
        body, name=name, grid=(s // tm,), in_specs=[big, row, row, row], out_specs=big,
        out_shape=jax.ShapeDtypeStruct((s, d), MMD), compiler_params=_cp(("arbitrary",)),
    )(x, w, scale, shift)


def _ln_mod_bwd(dh, x, w, scale, dres, *, name):
    s, d = x.shape
    tm = _tile(s, 512)

    def body(dh_ref, x_ref, w_ref, sc_ref, dres_ref, dx_ref, dsh_ref, dsc_ref, dw_ref):
        xv = x_ref[...]
        dhv = dh_ref[...].astype(F32)
        r = lax.rsqrt(jnp.mean(xv * xv, axis=-1, keepdims=True) + EPS)
        nv = xv * r
        wv = w_ref[...]
        g1 = 1.0 + sc_ref[...]
        dn = dhv * (wv * g1)
        dx_ref[...] = dres_ref[...] + r * (dn - nv * jnp.mean(dn * nv, axis=-1, keepdims=True))
        first = pl.program_id(0) == 0
        _acc_rows(dsh_ref, jnp.sum(dhv, axis=0, keepdims=True), first)
        _acc_rows(dsc_ref, jnp.sum(dhv * nv * wv, axis=0, keepdims=True), first)
        _acc_rows(dw_ref, jnp.sum(dhv * nv * g1, axis=0, keepdims=True), first)

    row = pl.BlockSpec((1, d), lambda i: (0, 0))
    big = pl.BlockSpec((tm, d), lambda i: (i, 0))
    return pl.pallas_call(
        body, name=name, grid=(s // tm,), in_specs=[big, big, row, row, big], out_specs=[big, row, row, row],
        out_shape=[jax.ShapeDtypeStruct((s, d), F32)] + [jax.ShapeDtypeStruct((1, d), F32)] * 3,
        compiler_params=_cp(("arbitrary",)),
    )(dh, x, w, scale, dres)


def _gate_bwd(dy, u, gate, *, name):
    s, d = dy.shape
    tm = _tile(s, 512)

    def body(dy_ref, u_ref, g_ref, du_ref, dg_ref):
        dyv = dy_ref[...]
        du_ref[...] = (dyv * g_ref[...]).astype(du_ref.dtype)
        _acc_rows(dg_ref, jnp.sum(dyv * u_ref[...].astype(F32), axis=0, keepdims=True), pl.program_id(0) == 0)

    row = pl.BlockSpec((1, d), lambda i: (0, 0))
    big = pl.BlockSpec((tm, d), lambda i: (i, 0))
    return pl.pallas_call(
        body, name=name, grid=(s // tm,), in_specs=[big, big, row], out_specs=[big, row],
        out_shape=[jax.ShapeDtypeStruct((s, d), MMD), jax.ShapeDtypeStruct((1, d), F32)],
        compiler_params=_cp(("arbitrary",)),
    )(dy, u, gate)


def _seg64(v, e):
    hi = v.astype(jnp.bfloat16)
    lo = (v - hi.astype(F32)).astype(jnp.bfloat16)
    return _dot(hi, e) + _dot(lo, e)


def _rope_tables(s, zero=0.0):
    rows = s // GRID_W
    pos_row = jnp.repeat(jnp.arange(rows, dtype=jnp.int32), GRID_W).astype(F32) + zero
    pos_col = jnp.tile(jnp.arange(GRID_W, dtype=jnp.int32), rows).astype(F32) + zero
    axis_dim = HEAD_DIM // 2
    inv_freq = ROPE_THETA ** (-jnp.arange(0, axis_dim, 2, dtype=F32) / axis_dim)
    ang_r = pos_row[:, None] * inv_freq[None, :]
    ang_c = pos_col[:, None] * inv_freq[None, :]
    zero = jnp.zeros_like(ang_r)
    cos = jnp.concatenate([jnp.cos(ang_r), jnp.cos(ang_r), jnp.cos(ang_c), jnp.cos(ang_c)], axis=1)
    s_a = jnp.concatenate([-jnp.sin(ang_r), zero, -jnp.sin(ang_c), zero], axis=1)
    s_b = jnp.concatenate([zero, jnp.sin(ang_r), zero, jnp.sin(ang_c)], axis=1)
    return [jnp.tile(t, (1, 2)) for t in (cos, s_a, s_b)]


def _e128():
    i = jnp.arange(128)
    return (i[:, None] // 64 == i[None, :] // 64).astype(jnp.bfloat16)


QKW = N_Q_HEADS * HEAD_DIM + N_KV_HEADS * HEAD_DIM


def _qk_fwd(proj, wrow, scrow, tabs, *, name):
    s = proj.shape[0]
    tm = _tile(s, 1024)

    def body(x_ref, w_ref, sc_ref, cos_ref, sa_ref, sb_ref, e_ref, o_ref, ot_ref):
        u = x_ref[...].astype(F32)
        r = lax.rsqrt(_seg64(u * u, e_ref[...]) * (1.0 / HEAD_DIM) + EPS)
        nv = (u * r) * w_ref[...]
        ro = nv * cos_ref[...] + pltpu.roll(nv, 112, 1) * sa_ref[...] + pltpu.roll(nv, 16, 1) * sb_ref[...]
        out = ro * sc_ref[...]
        o_ref[...] = out.astype(o_ref.dtype)
        ot_ref[...] = out.T.astype(ot_ref.dtype)

    tab = pl.BlockSpec((tm, 128), lambda i, j: (i, 0))
    row = pl.BlockSpec((1, 128), lambda i, j: (0, j))
    return pl.pallas_call(
        body, name=name, grid=(s // tm, QKW // 128),
        in_specs=[pl.BlockSpec((tm, 128), lambda i, j: (i, Q0 // 128 + j)), row, row, tab, tab, tab,
                  pl.BlockSpec((128, 128), lambda i, j: (0, 0))],
        out_specs=[pl.BlockSpec((tm, 128), lambda i, j: (i, j)), pl.BlockSpec((128, tm), lambda i, j: (j, i))],
        out_shape=[jax.ShapeDtypeStruct((s, QKW), MMD), jax.ShapeDtypeStruct((QKW, s), MMD)],
        compiler_params=_cp(("arbitrary", "arbitrary")),
    )(proj, wrow, scrow, *tabs, _e128())


def _qk_bwd(dqt, dkt, proj, wrow, scrow, tabs, dproj, *, name):
    s = proj.shape[0]
    tm = _tile(s, 1024)
    nq = dqt.shape[0] // 128

    def body(dq_ref, dk_ref, x_ref, w_ref, sc_ref, cos_ref, sa_ref, sb_ref, e_ref, _, du_ref, dw_ref):
        e = e_ref[...]
        d = jnp.where(pl.program_id(0) < nq, dq_ref[...], dk_ref[...]).T * sc_ref[...]
        dn = d * cos_ref[...] + pltpu.roll(d * sa_ref[...], 16, 1) + pltpu.roll(d * sb_ref[...], 112, 1)
        u = x_ref[...].astype(F32)
        r = lax.rsqrt(_seg64(u * u, e) * (1.0 / HEAD_DIM) + EPS)
        uh = u * r
        _acc_rows(dw_ref, jnp.sum(dn * uh, axis=0, keepdims=True), pl.program_id(1) == 0)
        dnw = dn * w_ref[...]
        du_ref[...] = (r * (dnw - uh * (_seg64(dnw * uh, e) * (1.0 / HEAD_DIM)))).astype(du_ref.dtype)

    tab = pl.BlockSpec((tm, 128), lambda j, i: (i, 0))
    row = pl.BlockSpec((1, 128), lambda j, i: (0, j))
    qcol = pl.BlockSpec((tm, 128), lambda j, i: (i, Q0 // 128 + j))
    return pl.pallas_call(
        body, name=name, grid=(QKW // 128, s // tm),
        in_specs=[pl.BlockSpec((128, tm), lambda j, i: (jnp.minimum(j, nq - 1), i)),
                  pl.BlockSpec((128, tm), lambda j, i: (jnp.maximum(j - nq, 0), i)),
                  qcol, row, row, tab, tab, tab, pl.BlockSpec((128, 128), lambda j, i: (0, 0)), ANY],
        out_specs=[qcol, row],
        out_shape=[jax.ShapeDtypeStruct(dproj.shape, dproj.dtype), jax.ShapeDtypeStruct((1, QKW), F32)],
        input_output_aliases={9: 0}, compiler_params=_cp(("arbitrary", "arbitrary")),
    )(dqt, dkt, proj, wrow, scrow, *tabs, _e128(), dproj)


REP = N_Q_HEADS // N_KV_HEADS


def _lanes(ref):
    return jnp.concatenate([ref[r] for r in range(REP)], axis=1)


V_AUG = HEAD_DIM + 8
LOG2E = math.log2(math.e)


def _flash_fwd(qkt, vta, *, name):
    s = qkt.shape[2]
    tq, tk = _tile(s, 1024), _tile(s, 512)
    nk = s // tk
    lanes = REP * tq

    def body(q_ref, k_ref, v_ref, o_ref, lse_ref, m_ref, acc_ref):
        j = pl.program_id(2)

        @pl.when(j == 0)
        def _():
            m_ref[...] = jnp.full_like(m_ref, NEG)
            acc_ref[...] = jnp.zeros_like(acc_ref)

        st = _dot(k_ref[0], _lanes(q_ref), TN)
        m_prev = m_ref[...]
        m_new = jnp.maximum(m_prev, jnp.max(st, axis=0, keepdims=True))
        p = jnp.exp2(st - m_new).astype(MMD)
        acc_ref[...] = jnp.exp2(m_prev - m_new) * acc_ref[...] + _dot(v_ref[0], p)
        m_ref[...] = m_new

        @pl.when(j == nk - 1)
        def _():
            acc = acc_ref[...]
            l = acc[HEAD_DIM:HEAD_DIM + 1]
            o = acc[0:HEAD_DIM] / l
            ls = m_ref[...] + jnp.log(l) * LOG2E
            for r in range(REP):
                o_ref[r] = o[:, r * tq:(r + 1) * tq].astype(o_ref.dtype)
                lse_ref[r] = ls[:, r * tq:(r + 1) * tq]

    qspec = pl.BlockSpec((REP, HEAD_DIM, tq), lambda g, i, j: (g, 0, i))
    return pl.pallas_call(
        body, name=name, grid=(N_KV_HEADS, s // tq, nk),
        in_specs=[qspec, pl.BlockSpec((1, HEAD_DIM, tk), lambda g, i, j: (N_Q_HEADS + g, 0, j)),
                  pl.BlockSpec((1, V_AUG, tk), lambda g, i, j: (g, 0, j))],
        out_specs=[qspec, pl.BlockSpec((REP, 1, tq), lambda g, i, j: (g, 0, i))],
        out_shape=[jax.ShapeDtypeStruct((N_Q_HEADS, HEAD_DIM, s), MMD), jax.ShapeDtypeStruct((N_Q_HEADS, 1, s), F32)],
        scratch_shapes=[pltpu.VMEM((1, lanes), F32), pltpu.VMEM((V_AUG, lanes), F32)],
        compiler_params=_cp(("arbitrary", "arbitrary", "arbitrary"), VMEM_BIG),
    )(qkt, qkt, vta)


def _flash_bwd(qkt, k_h, v_h, dot, ot, lse, *, name):
    s = qkt.shape[2]
    tq, tk = _tile(s, 512), _tile(s, 1024)
    nk = s // tk

    def body(q_ref, kt_ref, k_ref, v_ref, do_ref, o_ref, lse_ref, dq_ref, dk_ref, dv_ref, dq_acc):
        i, j = pl.program_id(1), pl.program_id(2)
        q, do = _lanes(q_ref), _lanes(do_ref)
        delta = jnp.sum(do.astype(F32) * _lanes(o_ref).astype(F32), axis=0, keepdims=True)
        k, v = k_ref[0], v_ref[0]
        p = jnp.exp2(_dot(k, q) - _lanes(lse_ref))
        dvc = _dot(p.astype(MMD), do, NT)
        ds = (p * (_dot(v, do) - delta)).astype(MMD)
        dkc = _dot(ds, q, NT) * (1.0 / LOG2E)
        dqc = _dot(kt_ref[0], ds)
        rows = pl.ds(pl.multiple_of(j * tk, tk), tk)

        @pl.when(i == 0)
        def _():
            dk_ref[0, rows, :] = dkc
            dv_ref[0, rows, :] = dvc

        @pl.when(i > 0)
        def _():
            dk_ref[0, rows, :] += dkc
            dv_ref[0, rows, :] += dvc

        @pl.when(j == 0)
        def _():
            dq_acc[...] = dqc

        @pl.when(j > 0)
        def _():
            dq_acc[...] += dqc

        @pl.when(j == nk - 1)
        def _():
            acc = dq_acc[...]
            for r in range(REP):
                dq_ref[r] = acc[:, r * tq:(r + 1) * tq]

    qspec = pl.BlockSpec((REP, HEAD_DIM, tq), lambda g, i, j: (g, 0, i))
    kvin = pl.BlockSpec((1, tk, HEAD_DIM), lambda g, i, j: (g, j, 0))
    kvres = pl.BlockSpec((1, s, HEAD_DIM), lambda g, i, j: (g, 0, 0))
    return pl.pallas_call(
        body, name=name, grid=(N_KV_HEADS, s // tq, nk),
        in_specs=[qspec, pl.BlockSpec((1, HEAD_DIM, tk), lambda g, i, j: (N_Q_HEADS + g, 0, j)), kvin, kvin,
                  qspec, qspec, pl.BlockSpec((REP, 1, tq), lambda g, i, j: (g, 0, i))],
        out_specs=[qspec, kvres, kvres],
        out_shape=[jax.ShapeDtypeStruct((N_Q_HEADS, HEAD_DIM, s), F32), jax.ShapeDtypeStruct((N_KV_HEADS, s, HEAD_DIM), F32),
                   jax.ShapeDtypeStruct((N_KV_HEADS, s, HEAD_DIM), F32)],
        scratch_shapes=[pltpu.VMEM((HEAD_DIM, REP * tq), F32)],
        compiler_params=_cp(("arbitrary", "arbitrary", "arbitrary"), VMEM_BIG),
    )(qkt, qkt, k_h, v_h, dot, ot, lse)


HALO = 8
CONV_W = 2048 + 2 * SSD_GROUPS * SSD_N


def _shifted(win, off, r):
    return pltpu.roll(win, (r + 2 * HALO - off) % (r + 2 * HALO), 0)[0:r]


def _conv_fwd(proj, w8, brow, *, name):
    s = proj.shape[0]
    cb = 256
    r = _tile(s, 512)

    def body(x_ref, w_ref, b_ref, o_ref, pad_ref):
        zeros = jnp.zeros((HALO, cb), F32)
        pad_ref[0:HALO, :] = zeros
        pad_ref[s + HALO:s + 2 * HALO, :] = zeros

        def fill(i, carry):
            st = pl.multiple_of(i * r, r)
            pad_ref[pl.ds(st + HALO, r), :] = x_ref[pl.ds(st, r), :].astype(F32)
            return carry

        lax.fori_loop(0, s // r, fill, 0)
        wv = w_ref[...]
        bv = b_ref[...]

        def step(i, carry):
            st = pl.multiple_of(i * r, r)
            win = pad_ref[pl.ds(st, r + 2 * HALO), :]
            acc = bv + wv[0:1, :] * _shifted(win, HALO - 2, r)
            for t in range(1, D_CONV):
                acc = acc + wv[t:t + 1, :] * _shifted(win, HALO - 2 + t, r)
            o_ref[pl.ds(st, r), :] = (acc * _sigmoid(acc)).astype(o_ref.dtype)
            return carry

        lax.fori_loop(0, s // r, step, 0)

    return pl.pallas_call(
        body, name=name, grid=(CONV_W // cb,),
        in_specs=[pl.BlockSpec((s, cb), lambda j: (0, XS0 // cb + j)), pl.BlockSpec((8, cb), lambda j: (0, j)),
                  pl.BlockSpec((1, cb), lambda j: (0, j))],
        out_specs=pl.BlockSpec((s, cb), lambda j: (0, j)),
        out_shape=jax.ShapeDtypeStruct((s, CONV_W), MMD),
        scratch_shapes=[pltpu.VMEM((s + 2 * HALO, cb), F32)],
        compiler_params=_cp(("arbitrary",), VMEM_MID),
    )(proj, w8, brow)


def _conv_bwd(proj, col0, ga, gb, w8, brow, dproj, *, name):
    s = proj.shape[0]
    width = ga.shape[1]
    cb = 128
    c0 = col0 // cb
    r = _tile(s, 512)

    def body(x_ref, ga_ref, gb_ref, w_ref, b_ref, _, dx_ref, dw_ref, db_ref, xpad, dpad):
        zeros = jnp.zeros((HALO, cb), F32)
        for ref in (xpad, dpad):
            ref[0:HALO, :] = zeros
            ref[s + HALO:s + 2 * HALO, :] = zeros

        def fill(i, carry):
            st = pl.multiple_of(i * r, r)
            xpad[pl.ds(st + HALO, r), :] = x_ref[pl.ds(st, r), :].astype(F32)
            return carry

        lax.fori_loop(0, s // r, fill, 0)
        wv = w_ref[...]
        bv = b_ref[...]

        def first(i, carry):
            st = pl.multiple_of(i * r, r)
            win = xpad[pl.ds(st, r + 2 * HALO), :]
            taps = [_shifted(win, HALO - 2 + t, r) for t in range(D_CONV)]
            u = bv
            for t in range(D_CONV):
                u = u + wv[t:t + 1, :] * taps[t]
            sg = _sigmoid(u)
            du = ((ga_ref[pl.ds(st, r), :].astype(F32) + gb_ref[pl.ds(st, r), :].astype(F32))
                  * (sg * (1.0 + u * (1.0 - sg))))
            dpad[pl.ds(st + HALO, r), :] = du
            out = [carry[0] + jnp.sum(du, axis=0, keepdims=True)]
            for t in range(D_CONV):
                out.append(carry[1 + t] + jnp.sum(du * taps[t], axis=0, keepdims=True))
            return tuple(out)

        sums = lax.fori_loop(0, s // r, first, tuple(jnp.zeros((1, cb), F32) for _ in range(1 + D_CONV)))
        db_ref[...] = sums[0]
        for t in range(D_CONV):
            dw_ref[t:t + 1, :] = sums[1 + t]
        dw_ref[D_CONV:8, :] = jnp.zeros((8 - D_CONV, cb), F32)

        def second(i, carry):
            st = pl.multiple_of(i * r, r)
            win = dpad[pl.ds(st, r + 2 * HALO), :]
            acc = wv[0:1, :] * _shifted(win, HALO + 2, r)
            for t in range(1, D_CONV):
                acc = acc + wv[t:t + 1, :] * _shifted(win, HALO + 2 - t, r)
            dx_ref[pl.ds(st, r), :] = acc.astype(dx_ref.dtype)
            return carry

        lax.fori_loop(0, s // r, second, 0)

    col = pl.BlockSpec((s, cb), lambda j: (0, j))
    xcol = pl.BlockSpec((s, cb), lambda j: (0, XS0 // cb + c0 + j))
    return pl.pallas_call(
        body, name=name, grid=(width // cb,),
        in_specs=[xcol, col, col, pl.BlockSpec((8, cb), lambda j: (0, c0 + j)),
                  pl.BlockSpec((1, cb), lambda j: (0, c0 + j)), ANY],
        out_specs=[xcol, pl.BlockSpec((8, cb), lambda j: (0, j)), pl.BlockSpec((1, cb), lambda j: (0, j))],
        out_shape=[jax.ShapeDtypeStruct(dproj.shape, dproj.dtype), jax.ShapeDtypeStruct((8, width), F32),
                   jax.ShapeDtypeStruct((1, width), F32)],
        scratch_shapes=[pltpu.VMEM((s + 2 * HALO, cb), F32), pltpu.VMEM((s + 2 * HALO, cb), F32)],
        input_output_aliases={5: 0}, compiler_params=_cp(("arbitrary",), VMEM_BIG),
    )(proj, ga, gb, w8, brow, dproj)


def _tri(lower):
    i = jnp.arange(CHUNK)
    return ((i[:, None] >= i[None, :]) if lower else (i[:, None] <= i[None, :])).astype(F32)


def _dt_fwd(raw, bias, arow, *, name):
    s = raw.shape[0]

    def body(r_ref, b_ref, a_ref, lo_ref, up_ref, dt_ref, cs_ref):
        u = r_ref[...] + b_ref[...]
        dt = jnp.maximum(u, 0.0) + jnp.log1p(jnp.exp(-jnp.abs(u)))
        dt_ref[...] = dt
        a = dt * a_ref[...]
        lane = lax.broadcasted_iota(jnp.int32, (CHUNK, 128), 1)
        cs_ref[...] = jnp.where(lane < SSD_HEADS, _dot_hi(lo_ref[...], a), _dot_hi(up_ref[...], a))

    blk = pl.BlockSpec((CHUNK, 128), lambda i: (i, 0))
    row = pl.BlockSpec((1, 128), lambda i: (0, 0))
    tri = pl.BlockSpec((CHUNK, CHUNK), lambda i: (0, 0))
    return pl.pallas_call(
        body, name=name, grid=(s // CHUNK,), in_specs=[blk, row, row, tri, tri], out_specs=[blk, blk],
        out_shape=[jax.ShapeDtypeStruct((s, 128), F32)] * 2, compiler_params=_cp(("arbitrary",)),
    )(raw, bias, arow, _tri(True), _tri(False))


def _dt_bwd(ddt0, ddt1, raw, bias, dproj, *, name):
    s = raw.shape[0]
    tm = _tile(s, 1024)

    def body(d0_ref, d1_ref, r_ref, b_ref, _, o_ref, db_ref):
        g = (d0_ref[...] + d1_ref[...]) * _sigmoid(r_ref[...] + b_ref[...])
        o_ref[...] = g.astype(o_ref.dtype)
        _acc_rows(db_ref, jnp.sum(g, axis=0, keepdims=True), pl.program_id(0) == 0)

    blk = pl.BlockSpec((tm, 128), lambda i: (i, 0))
    row = pl.BlockSpec((1, 128), lambda i: (0, 0))
    return pl.pallas_call(
        body, name=name, grid=(s // tm,), in_specs=[blk, blk, blk, row, ANY],
        out_specs=[pl.BlockSpec((tm, 128), lambda i: (i, DT0 // 128)), row],
        out_shape=[jax.ShapeDtypeStruct(dproj.shape, dproj.dtype), jax.ShapeDtypeStruct((1, 128), F32)],
        input_output_aliases={4: 0}, compiler_params=_cp(("arbitrary",)),
    )(ddt0, ddt1, raw, bias, dproj)


GW = HPG * SSD_P


GPS = SSD_GROUPS


def _ssd_specs(nc, rev):
    cc = (lambda c: nc - 1 - c) if rev else (lambda c: c)
    return dict(
        x=pl.BlockSpec((CHUNK, GPS * GW), lambda g, c: (cc(c), g)),
        b=pl.BlockSpec((CHUNK, GPS * SSD_N), lambda g, c: (cc(c), 2048 // (GPS * SSD_N) + g)),
        c=pl.BlockSpec((CHUNK, GPS * SSD_N), lambda g, c: (cc(c), 2048 // (GPS * SSD_N) + 1 + g)),
        lanes=pl.BlockSpec((CHUNK, 128), lambda g, c: (cc(c), 0)),
        drow=pl.BlockSpec((1, GPS * GW), lambda g, c: (0, g)),
        y=pl.BlockSpec((CHUNK, GPS * GW), lambda g, c: (cc(c), g)),
        h=pl.BlockSpec((GPS, 1, SSD_N, GW), lambda g, c: (g, cc(c), 0, 0)),
        n=pl.BlockSpec((CHUNK, GPS * SSD_N), lambda g, c: (cc(c), g)),
    )


def _ssd_mask(anti):
    ii = lax.broadcasted_iota(jnp.int32, (CHUNK, CHUNK), 0)
    jj = lax.broadcasted_iota(jnp.int32, (CHUNK, CHUNK), 1)
    return ii, jj, (ii <= jj) if anti else (ii >= jj)


def _expand(x, ex, terms=3):
    h1 = x.astype(jnp.bfloat16)
    r1 = x - h1.astype(F32)
    h2 = r1.astype(jnp.bfloat16)
    out = _dot(h1, ex) + _dot(h2, ex)
    if terms == 3:
        out = out + _dot((r1 - h2.astype(F32)).astype(jnp.bfloat16), ex)
    return out


def _headsum(a, e):
    hi = a.astype(jnp.bfloat16)
    return _dot(hi, e) + _dot((a - hi.astype(F32)).astype(jnp.bfloat16), e)


def _expand_mats():
    lane = jnp.arange(128)[None, :, None]
    col = jnp.arange(GW)[None, None, :]
    base = (jnp.arange(2)[:, None] * SSD_HEADS + jnp.arange(SSD_GROUPS)[None, :] * HPG).reshape(2 * SSD_GROUPS, 1, 1)
    return (lane == base + col // SSD_P).astype(jnp.bfloat16)


def _headsum_mats():
    e1 = (jnp.arange(GW)[:, None] // SSD_P == jnp.arange(128)[None, :]).astype(jnp.bfloat16)
    e2 = (jnp.arange(HPG * CHUNK)[:, None] // CHUNK == jnp.arange(128)[None, :]).astype(jnp.bfloat16)
    return e1, e2


def _ssd_fwd(xc, dt, cs, ex, drow, di, *, name):
    s = xc.shape[0]
    nc = s // CHUNK
    anti = di == 1
    sp = _ssd_specs(nc, anti)
    trow = 0 if anti else CHUNK - 1

    def body(x_ref, b_ref, c_ref, dt_ref, cs_ref, ex_ref, d_ref, y_ref, hp_ref, h_ref):
        @pl.when(pl.program_id(1) == 0)
        def _():
            h_ref[...] = jnp.zeros_like(h_ref)

        mask = _ssd_mask(anti)[2]
        dtv, csv = dt_ref[...], cs_ref[...]
        cst = csv.T
        for gi in range(GPS):
            cols = slice(gi * GW, (gi + 1) * GW)
            ncols = slice(gi * SSD_N, (gi + 1) * SSD_N)
            ex = ex_ref[gi]
            xb = x_ref[:, cols].astype(F32)
            bm, cm = b_ref[:, ncols], c_ref[:, ncols]
            csr = cst[SSD_HEADS * di + HPG * gi:SSD_HEADS * di + HPG * (gi + 1)]
            dtf = _expand(dtv, ex, 2)
            csf = _expand(csv, ex)
            tl = csf[trow:trow + 1, :]
            h = h_ref[gi]
            hp_ref[gi, 0] = h.astype(hp_ref.dtype)
            g = _dot(cm, bm, NT)
            xs = xb * dtf
            xsm = xs.astype(MMD)
            base = jnp.exp(csf) * _dot(cm, h.astype(MMD)) + d_ref[:, cols] * xb
            for r in range(HPG):
                sl = slice(r * SSD_P, (r + 1) * SSD_P)
                lm = jnp.exp(jnp.where(mask, csf[:, r * SSD_P:r * SSD_P + 1] - csr[r:r + 1, :], NEG))
                y_ref[:, gi * GW + r * SSD_P:gi * GW + (r + 1) * SSD_P] = (
                    _dot((g * lm).astype(MMD), xsm[:, sl]) + base[:, sl]).astype(y_ref.dtype)
            xd = (xs * jnp.exp(tl - csf)).astype(MMD)
            h_ref[gi] = h * jnp.exp(tl) + _dot(bm, xd, TN)

    return pl.pallas_call(
        body, name=name, grid=(1, nc),
        in_specs=[sp["x"], sp["b"], sp["c"], sp["lanes"], sp["lanes"],
                  pl.BlockSpec((GPS, 128, GW), lambda g, c: (di, 0, 0)), sp["drow"]],
        out_specs=[sp["y"], sp["h"]],
        out_shape=[jax.ShapeDtypeStruct((s, 2048), MMD), jax.ShapeDtypeStruct((SSD_GROUPS, nc, SSD_N, GW), MMD)],
        scratch_shapes=[pltpu.VMEM((GPS, SSD_N, GW), F32)],
        compiler_params=_cp(("arbitrary", "arbitrary")),
    )(xc, xc, xc, dt, cs, ex, drow)


def _ssd_bwd(xc, dt, cs, ex, drow, arow, dy, hprev, di, *, name):
    s = xc.shape[0]
    nc = s // CHUNK
    anti = di == 1
    sp = _ssd_specs(nc, not anti)
    trow = 0 if anti else CHUNK - 1
    e1, e2 = _headsum_mats()

    def body(x_ref, b_ref, c_ref, dt_ref, cs_ref, ex_ref, d_ref, a_ref, dy_ref, hp_ref, tri_ref,
             e1_ref, e2_ref, dx_ref, db_ref, dc_ref, ddt_ref, da_ref, dh_ref, w_ref, dxs_ref):
        @pl.when(pl.program_id(1) == 0)
        def _():
            dh_ref[...] = jnp.zeros_like(dh_ref)
            da_ref[...] = jnp.zeros_like(da_ref)

        e1v = e1_ref[...]
        ii, _, mask = _ssd_mask(anti)
        dtv, csv = dt_ref[...], cs_ref[...]
        cst = csv.T
        ddt_acc = jnp.zeros((CHUNK, 128), F32)
        da_acc = jnp.zeros((1, 128), F32)
        for gi in range(GPS):
            lane0 = SSD_HEADS * di + HPG * gi
            cols = slice(gi * GW, (gi + 1) * GW)
            ncols = slice(gi * SSD_N, (gi + 1) * SSD_N)
            ex = ex_ref[gi]
            xb = x_ref[:, cols].astype(F32)
            bm, cm = b_ref[:, ncols], c_ref[:, ncols]
            csr = cst[lane0:lane0 + HPG]
            dym = dy_ref[:, cols]
            dyb = dym.astype(F32)
            hpm = hp_ref[gi, 0]
            hp = hpm.astype(F32)
            dh = dh_ref[gi]
            dhm = dh.astype(MMD)
            dtf = _expand(dtv, ex, 2)
            csf = _expand(csv, ex)
            tl = csf[trow:trow + 1, :]
            e = jnp.exp(csf)
            dec = jnp.exp(tl - csf)
            et = jnp.exp(tl)
            xs = xb * dtf
            xsm = xs.astype(MMD)
            g = _dot(cm, bm, NT)
            z = _dot(cm, hpm)
            bdh = _dot(bm, dhm)
            dg = jnp.zeros((CHUNK, CHUNK), F32)
            wcols = jnp.zeros((CHUNK, CHUNK), F32)
            for r in range(HPG):
                sl = slice(r * SSD_P, (r + 1) * SSD_P)
                lm = jnp.exp(jnp.where(mask, csf[:, r * SSD_P:r * SSD_P + 1] - csr[r:r + 1, :], NEG))
                mm = g * lm
                dm = _dot(dym[:, sl], xsm[:, sl], NT)
                w = dm * mm
                w_ref[gi, :, r * CHUNK:(r + 1) * CHUNK] = w
                wcols = jnp.where(ii == r, jnp.sum(w, axis=0, keepdims=True), wcols)
                dg = dg + dm * lm
                dxs_ref[gi, :, sl] = _dot(mm.astype(MMD), dym[:, sl], TN)
            dxs = dxs_ref[gi] + dec * bdh
            dx_ref[:, cols] = (dxs * dtf + d_ref[:, cols] * dyb).astype(dx_ref.dtype)
            tb = xs * bdh * dec
            d_tot = jnp.sum(tb, axis=0, keepdims=True) + et * jnp.sum(dh * hp, axis=0, keepdims=True)
            d_tot = _headsum(jnp.broadcast_to(d_tot, (8, GW)), e1v)[0:1]
            dcs = (_headsum(dyb * (e * z) - tb, e1v) + _headsum(w_ref[gi], e2_ref[...]) - wcols.T
                   + jnp.where(ii == trow, d_tot, 0.0))
            da = pltpu.roll(_dot_hi(tri_ref[...], dcs), lane0, 1)
            ddt_acc = ddt_acc + da * a_ref[...] + pltpu.roll(_headsum(dxs * xb, e1v), lane0, 1)
            da_acc = da_acc + jnp.sum(da * dtv, axis=0, keepdims=True)
            dgm = dg.astype(MMD)
            dz = (e * dyb).astype(MMD)
            dc_ref[:, ncols] = (_dot(dgm, bm) + _dot(dz, hpm, NT)).astype(dc_ref.dtype)
            db_ref[:, ncols] = (_dot(dgm, cm, TN) + _dot((xs * dec).astype(MMD), dhm, NT)).astype(db_ref.dtype)
            dh_ref[gi] = dh * et + _dot(cm, dz, TN)
        ddt_ref[...] = ddt_acc
        da_ref[...] += da_acc

    const = lambda shape: pl.BlockSpec(shape, lambda g, c: (0,) * len(shape))
    return pl.pallas_call(
        body, name=name, grid=(1, nc),
        in_specs=[sp["x"], sp["b"], sp["c"], sp["lanes"], sp["lanes"],
                  pl.BlockSpec((GPS, 128, GW), lambda g, c: (di, 0, 0)), sp["drow"],
                  const((1, 128)), sp["y"], sp["h"],
                  const((CHUNK, CHUNK)), const((GW, 128)), const((HPG * CHUNK, 128))],
        out_specs=[sp["y"], sp["n"], sp["n"], sp["lanes"], const((1, 128))],
        out_shape=[jax.ShapeDtypeStruct((s, 2048), MMD), jax.ShapeDtypeStruct((s, SSD_GROUPS * SSD_N), MMD),
                   jax.ShapeDtypeStruct((s, SSD_GROUPS * SSD_N), MMD), jax.ShapeDtypeStruct((s, 128), F32),
                   jax.ShapeDtypeStruct((1, 128), F32)],
        scratch_shapes=[pltpu.VMEM((GPS, SSD_N, GW), F32), pltpu.VMEM((GPS, CHUNK, HPG * CHUNK), F32),
                        pltpu.VMEM((GPS, CHUNK, GW), F32)],
        compiler_params=_cp(("arbitrary", "arbitrary")),
    )(xc, xc, xc, dt, cs, ex, drow, arow, dy, hprev, _tri(anti), e1, e2)


def _gnorm_fwd(ya, yb, proj, w, *, name):
    s = ya.shape[0]
    tm = _tile(s, 256)

    def body(a_ref, b_ref, z_ref, w_ref, o_ref):
        zv = z_ref[...].astype(F32)
        t = (a_ref[...].astype(F32) + b_ref[...].astype(F32)) * (zv * _sigmoid(zv))
        r = lax.rsqrt(jnp.mean(t * t, axis=-1, keepdims=True) + EPS)
        o_ref[...] = ((t * r) * w_ref[...]).astype(o_ref.dtype)

    big = pl.BlockSpec((tm, 2048), lambda i: (i, 0))
    row = pl.BlockSpec((1, 2048), lambda i: (0, 0))
    return pl.pallas_call(
        body, name=name, grid=(s // tm,), in_specs=[big, big, big, row], out_specs=big,
        out_shape=jax.ShapeDtypeStruct((s, 2048), MMD), compiler_params=_cp(("arbitrary",)),
    )(ya, yb, proj, w)


def _gnorm_bwd(dout, ya, yb, proj, xc, w, dproj, *, name):
    s = ya.shape[0]
    tm = _tile(s, 256)

    def body(do_ref, a_ref, b_ref, z_ref, x_ref, w_ref, _, dy_ref, dz_ref, dw_ref, dd_ref):
        zv = z_ref[...].astype(F32)
        sg = _sigmoid(zv)
        sz = zv * sg
        y = a_ref[...].astype(F32) + b_ref[...].astype(F32)
        t = y * sz
        r = lax.rsqrt(jnp.mean(t * t, axis=-1, keepdims=True) + EPS)
        nv = t * r
        dov = do_ref[...].astype(F32)
        _acc_rows(dw_ref, jnp.sum(dov * nv, axis=0, keepdims=True), pl.program_id(0) == 0)
        dn = dov * w_ref[...]
        dt_ = r * (dn - nv * jnp.mean(dn * nv, axis=-1, keepdims=True))
        dy = dt_ * sz
        dy_ref[...] = dy.astype(dy_ref.dtype)
        dz_ref[...] = (dt_ * y * (sg * (1.0 + zv * (1.0 - sg)))).astype(dz_ref.dtype)
        _acc_rows(dd_ref, jnp.sum(dy * x_ref[...].astype(F32), axis=0, keepdims=True), pl.program_id(0) == 0)

    big = pl.BlockSpec((tm, 2048), lambda i: (i, 0))
    row = pl.BlockSpec((1, 2048), lambda i: (0, 0))
    return pl.pallas_call(
        body, name=name, grid=(s // tm,), in_specs=[big, big, big, big, big, row, ANY], out_specs=[big, big, row, row],
        out_shape=[jax.ShapeDtypeStruct((s, 2048), MMD), jax.ShapeDtypeStruct(dproj.shape, dproj.dtype),
                   jax.ShapeDtypeStruct((1, 2048), F32), jax.ShapeDtypeStruct((1, 2048), F32)],
        input_output_aliases={6: 1}, compiler_params=_cp(("arbitrary",)),
    )(dout, ya, yb, proj, xc, w, dproj)


def _heads(a, n):
    return a.reshape(a.shape[0], n, HEAD_DIM).transpose(1, 0, 2)


def _unheads(a):
    return a.transpose(1, 0, 2).reshape(a.shape[1], a.shape[0] * HEAD_DIM)


def _local_step(x, target, mod, wts, small, in_weights=None, late_weights=None, late_grads=None, in_grad=None,
                zero=0.0):
    s, d = x.shape
    shift1, scale1, gate1, shift2, scale2, gate2 = [mod[i:i + 1] for i in range(6)]

    h1 = _ln_mod(x, small["norm1_w"], scale1, shift1, name="ln1")
    qk_w = jnp.concatenate([jnp.tile(small["q_norm_w"], (1, N_Q_HEADS)), jnp.tile(small["k_norm_w"], (1, N_KV_HEADS))], axis=1)
    qk_sc = jnp.concatenate([jnp.full((1, N_Q_HEADS * HEAD_DIM), HEAD_DIM ** -0.5, F32),
                             jnp.ones((1, N_KV_HEADS * HEAD_DIM), F32)], axis=1)
    qk_sc2 = jnp.concatenate([jnp.full((1, N_Q_HEADS * HEAD_DIM), HEAD_DIM ** -0.5 * LOG2E, F32),
                              jnp.ones((1, N_KV_HEADS * HEAD_DIM), F32)], axis=1)
    tabs = _rope_tables(s, zero)
    if in_weights is not None:
        wts = {**wts, **in_weights([h1, *tabs])}
    proj = _mm(h1, wts["w_in_p"], name="in_proj", outs=[MMD], tm=512, tn=2944, b_outer=True)
    dt_raw = _mm(h1, wts["w_dt"], name="dt_proj", outs=[F32], tm=512, tn=128)
    qk, qkt = _qk_fwd(proj, qk_w, qk_sc2, tabs, name="qk_fwd")
    qkt = qkt.reshape(N_Q_HEADS + N_KV_HEADS, HEAD_DIM, s)
    k_h = _heads(qk[:, N_Q_HEADS * HEAD_DIM:], N_KV_HEADS)
    v_sd = proj[:, V0:V0 + N_KV_HEADS * HEAD_DIM]
    v_h = _heads(v_sd, N_KV_HEADS)
    vta = jnp.concatenate([v_sd.T.reshape(N_KV_HEADS, HEAD_DIM, s), jnp.ones((N_KV_HEADS, V_AUG - HEAD_DIM, s), MMD)], axis=1)
    ot, lse = _flash_fwd(qkt, vta, name="flash_fwd")
    ot2 = ot.reshape(N_Q_HEADS * HEAD_DIM, s)
    if late_weights is not None:
        wts = {**wts, **late_weights(ot)}

    w8 = jnp.pad(small["conv_w"], ((0, 8 - D_CONV), (0, 0)))
    xc = _conv_fwd(proj, w8, small["conv_b"], name="conv_fwd")
    a_neg = -jnp.exp(small["A_log"])
    arow = jnp.pad(a_neg.reshape(1, 2 * SSD_HEADS), ((0, 0), (0, 128 - 2 * SSD_HEADS)))
    bias_row = jnp.pad(small["dt_bias"].reshape(1, 2 * SSD_HEADS), ((0, 0), (0, 128 - 2 * SSD_HEADS)))
    dt, cs = _dt_fwd(dt_raw, bias_row, arow, name="dt_fwd")
    drow = jnp.repeat(small["ssd_D"], SSD_P, axis=1)
    dirs = [dict(drow=drow), dict(drow=jnp.zeros_like(drow))]
    ex = _expand_mats()
    ys = []
    for di, dd in enumerate(dirs):
        y, dd["hprev"] = _ssd_fwd(xc, dt, cs, ex, dd["drow"], di, name=f"ssd_fwd{di}")
        ys.append(y)
    ssdn = _gnorm_fwd(ys[0], ys[1], proj, small["ssd_norm_w"], name="gnorm_fwd")

    a_o = _mm(ot2, wts["w_attn_out"], name="attn_out", outs=[MMD], ta=True, tm=512, tn=1024)

    def merge_epi(acc, ao, ga, gs):
        return (_sigmoid(ga.astype(F32)) * ao.astype(F32) + _sigmoid(gs.astype(F32)) * acc, acc)

    merged, b_o = _mm(ssdn, wts["w_ssd_out"], name="ssd_out", outs=[MMD, MMD], tm=512, tn=1024,
                      extras=[(a_o, "tile", 0), (proj, "tile", GA0), (proj, "tile", GS0)], epi=merge_epi)

    def res_epi(acc, res, gate):
        return (res + gate * acc, acc)

    x1, mo = _mm(merged, wts["w_o"], name="w_o", outs=[F32, MMD], tm=512, tn=1024,
                 extras=[(x, "tile", 0), (gate1, "row", 0)], epi=res_epi)
    h2 = _ln_mod(x1, small["norm2_w"], scale2, shift2, name="ln2")

    def relu2_epi(acc):
        rl = jnp.maximum(acc, 0.0)
        return (rl * rl, rl)

    act, rl = _mm(h2, wts["w_mlp1"], name="mlp1", outs=[MMD, MMD], tm=1024, tn=1024, epi=relu2_epi, b_outer=True)

    def loss_epi(acc, res, gate, tgt):
        return ((res + gate * acc - tgt) * (1.0 / d), acc)

    dy, ffo = _mm(act, wts["w_mlp2"], name="mlp2", outs=[F32, MMD], tm=512, tn=1024, vmem=VMEM_BIG,
                  extras=[(x1, "tile", 0), (gate2, "row", 0), (target, "tile", 0)], epi=loss_epi)
    loss = _sumsq(dy, name="loss") * (0.5 * d)

    gw = {}
    gs_ = {}
    dffo, dgate2 = _gate_bwd(dy, ffo, gate2, name="gate2_bwd")
    dpre = _mm(dffo, wts["w_mlp2"], name="mlp2_dx", outs=[MMD], nt=True, tm=1024, tn=1024, b_outer=True,
               extras=[(rl, "tile", 0)], epi=lambda acc, r: (acc * (2.0 * r.astype(F32)),))
    gw["w_mlp2"] = _mm_tn(act, dffo, name="mlp2_dw")
    dh2 = _mm(dpre, wts["w_mlp1"], name="mlp1_dx", outs=[F32], nt=True, tm=1024, tn=1024, vmem=VMEM_BIG)
    gw["w_mlp1"] = _mm_tn(h2, dpre, name="mlp1_dw")
    dx1, dshift2, dscale2, gs_["norm2_w"] = _ln_mod_bwd(dh2, x1, small["norm2_w"], scale2, dy, name="ln2_bwd")
    dmo, dgate1 = _gate_bwd(dx1, mo, gate1, name="gate1_bwd")

    def merge_bwd_epi(acc, ao, bo, ga, gs):
        sa, ss = _sigmoid(ga.astype(F32)), _sigmoid(gs.astype(F32))
        return (acc * sa, acc * ss, acc * ao.astype(F32) * sa * (1.0 - sa), acc * bo.astype(F32) * ss * (1.0 - ss))

    da_o, db_o, dga, dgs = _mm(dmo, wts["w_o"], name="w_o_dx", outs=[MMD] * 4, nt=True, tm=512, tn=1024,
                               extras=[(a_o, "tile", 0), (b_o, "tile", 0), (proj, "tile", GA0), (proj, "tile", GS0)],
                               epi=merge_bwd_epi)
    gw["w_o"] = _mm_tn(merged, dmo, name="w_o_dw")
    dot = _mm(wts["w_attn_out"], da_o, name="attn_out_dx", outs=[MMD], nt=True, tm=1024, tn=1024)
    gw["w_attn_out"] = _mm(ot2, da_o, name="attn_out_dw", outs=[F32], tm=256, tn=512, vmem=VMEM_BIG)
    dssdn = _mm(db_o, wts["w_ssd_out"], name="ssd_out_dx", outs=[MMD], nt=True, tm=512, tn=2048)
    gw["w_ssd_out"] = _mm_tn(ssdn, db_o, name="ssd_out_dw")

    dproj = lax.dynamic_update_slice(lax.empty((s, PW), MMD), jnp.concatenate([dga, dgs], axis=1), (0, GA0))

    norm_w = small["ssd_norm_w"] if late_grads is None else small["ssd_norm_w"] + late_grads(gw)
    dyssd, dproj, gs_["ssd_norm_w"], dd_row = _gnorm_bwd(dssdn, ys[0], ys[1], proj, xc, norm_w, dproj, name="gnorm_bwd")
    gs_["ssd_D"] = dd_row.reshape(SSD_HEADS, SSD_P).sum(axis=1).reshape(1, SSD_HEADS)
    dxc, ddts, das = [], [], []
    for di, dd in enumerate(dirs):
        dxs, dbm, dcm, ddt_d, da_d = _ssd_bwd(xc, dt, cs, ex, dd["drow"], arow, dyssd, dd["hprev"], di, name=f"ssd_bwd{di}")
        dxc.append((dxs, dbm, dcm))
        ddts.append(ddt_d)
        das.append(da_d)
    dw8, db, col0 = [], [], 0
    for part, (ga, gb) in enumerate(zip(*dxc)):
        dproj, dw_part, db_part = _conv_bwd(proj, col0, ga, gb, w8, small["conv_b"], dproj, name=f"conv_bwd{part}")
        dw8.append(dw_part)
        db.append(db_part)
        col0 += ga.shape[1]
    gs_["conv_w"] = jnp.concatenate(dw8, axis=1)[0:D_CONV]
    gs_["conv_b"] = jnp.concatenate(db, axis=1)
    gs_["A_log"] = (das[0] + das[1])[:, 0:2 * SSD_HEADS].reshape(2, SSD_HEADS) * a_neg
    dproj, dbias = _dt_bwd(ddts[0], ddts[1], dt_raw, bias_row, dproj, name="dt_bwd")
    gs_["dt_bias"] = dbias[:, 0:2 * SSD_HEADS].reshape(2, SSD_HEADS)

    dqt, dk_h, dv_h = _flash_bwd(qkt, k_h, v_h, dot.reshape(N_Q_HEADS, HEAD_DIM, s), ot, lse, name="flash_bwd")
    dproj, dqk_w = _qk_bwd(dqt.reshape(N_Q_HEADS * HEAD_DIM, s), dk_h.transpose(0, 2, 1).reshape(N_KV_HEADS * HEAD_DIM, s),
                           proj, qk_w, qk_sc, tabs, dproj, name="qk_bwd")
    gs_["q_norm_w"] = dqk_w[:, 0:N_Q_HEADS * HEAD_DIM].reshape(N_Q_HEADS, HEAD_DIM).sum(axis=0, keepdims=True)
    gs_["k_norm_w"] = dqk_w[:, N_Q_HEADS * HEAD_DIM:].reshape(N_KV_HEADS, HEAD_DIM).sum(axis=0, keepdims=True)
    dproj = lax.dynamic_update_slice(dproj, _unheads(dv_h).astype(MMD), (0, V0))

    gw["w_in_p"] = _mm_tn(h1, dproj, name="in_proj_dw", tk=512, tn=2944, tmm=2048, vmem=VMEM_BIG)
    zero_row = jnp.zeros((1, d), F32) if in_grad is None else jnp.zeros((1, d), F32) + in_grad(gw["w_in_p"])[0:1, 0:1]
    dh1 = _mm(dproj, wts["w_in_p"], name="in_proj_dx", outs=[F32], nt=True, tm=256, tn=1024, vmem=VMEM_BIG,
              extras=[(zero_row, "row", 0)], epi=lambda acc, r: (acc + r,))
    grad_x, dshift1, dscale1, gs_["norm1_w"] = _ln_mod_bwd(dh1, x, small["norm1_w"], scale1, dx1, name="ln1_bwd")
    dmod = jnp.concatenate([dshift1, dscale1, dgate1, dshift2, dscale2, dgate2], axis=0)
    return loss, grad_x, dmod, gw, gs_


N_DEV = 8
N_CHIP = 4
ANY = pl.BlockSpec(memory_space=pl.ANY)


def _place():
    return lax.axis_index("x"), lax.axis_index("y"), lax.axis_index("c")


def _allgather8(v, *, name):
    m_per, n = v.shape

    def body(x_ref, out_ref, send_sems, recv_sems, local_sem):
        x, y, c = _place()
        me, sibling = (x, y, c), (x, y, 1 - c)
        chips = [(1 - x, y), (x, 1 - y), (1 - x, 1 - y)]

        def rows(px, py, pc):
            return out_ref.at[pl.ds((4 * px + 2 * py + pc) * m_per, m_per), :]

        def copy(k, block, to, src=None):
            return pltpu.make_async_remote_copy(
                src_ref=rows(*block) if src is None else src, dst_ref=rows(*block),
                send_sem=send_sems.at[k], recv_sem=recv_sems.at[k], device_id=to, device_id_type=MESH)

        mine = pltpu.make_async_copy(x_ref, rows(*me), local_sem)
        mine.start()
        first = [copy(0, me, sibling, src=x_ref)]
        first += [copy(1 + j, me, (*chip, c), src=x_ref) for j, chip in enumerate(chips)]
        for cp in first:
            cp.start()
        passed = [copy(4 + j, (*chip, c), sibling) for j, chip in enumerate(chips)]
        for j, chip in enumerate(chips):
            copy(1 + j, (*chip, c), me).wait_recv()
            passed[j].start()
        copy(0, sibling, me).wait_recv()
        for j, chip in enumerate(chips):
            copy(4 + j, (*chip, 1 - c), me).wait_recv()
        for cp in first + passed:
            cp.wait_send()
        mine.wait()

    return pl.pallas_call(
        body, name=name, out_shape=jax.ShapeDtypeStruct((N_DEV * m_per, n), v.dtype),
        in_specs=[pl.BlockSpec(memory_space=pltpu.VMEM)], out_specs=pl.BlockSpec(memory_space=pltpu.VMEM),
        scratch_shapes=[pltpu.SemaphoreType.DMA((7,)), pltpu.SemaphoreType.DMA((7,)), pltpu.SemaphoreType.DMA],
    )(v)


HBM = pl.BlockSpec(memory_space=pltpu.HBM)
SEM = pl.BlockSpec(memory_space=pltpu.SEMAPHORE)


def _chips_copies(x_ref, land_ref, sems, scatter, half=False):
    x, y, c = _place()
    k = 2 * x + y
    chips = [(1 - x, y), (x, 1 - y), (1 - x, 1 - y)]
    ids = [2 * cx + cy for cx, cy in chips]
    if half:
        hr = x_ref.shape[0] // 2
        rows = pl.ds(pl.multiple_of(c * hr, 16), hr)

    def copy(j, slot):
        src = x_ref.at[ids[j]] if scatter else (x_ref.at[rows] if half else x_ref)
        dst = land_ref.at[slot, rows] if half else land_ref.at[slot]
        return pltpu.make_async_remote_copy(src_ref=src, dst_ref=dst, send_sem=sems[j], recv_sem=sems[3 + j],
                                            device_id=(*chips[j], c), device_id_type=MESH)

    return [copy(j, k) for j in range(3)], [copy(j, ids[j]) for j in range(3)]


def _chips_start(src, scatter, half=False, *, name):
    shape = src.shape if scatter else (N_CHIP,) + tuple(src.shape)

    def body(x_ref, land_ref, *rest):
        sems, token = rest[0:6], rest[8]
        for cp in _chips_copies(x_ref, land_ref, sems, scatter, half)[0]:
            cp.start()
        token[...] = jnp.zeros_like(token)

    out = pl.pallas_call(
        body, name=name,
        out_shape=(pltpu.SemaphoreType.DMA(()),) * 6 + (pltpu.HBM(src.shape, src.dtype), pltpu.HBM(shape, src.dtype),
                                                       jax.ShapeDtypeStruct((8, 128), F32)),
        in_specs=(HBM, HBM), out_specs=(SEM,) * 6 + (HBM, HBM, pl.BlockSpec(memory_space=pltpu.VMEM)),
        input_output_aliases={0: 6, 1: 7},
        compiler_params=pltpu.CompilerParams(has_side_effects=pltpu.SideEffectType.DATAFLOW_SIDE_EFFECTING),
    )(pltpu.with_memory_space_constraint(src, pltpu.HBM),
      pltpu.with_memory_space_constraint(lax.empty(shape, src.dtype), pltpu.HBM))
    return out[0:6], out[6], out[7], out[8]


def _chips_wait(sems, src, land, after, scatter, half=False, *, name):
    after = list(after) if isinstance(after, (list, tuple)) else [after]

    def body(x_ref, land_ref, *rest):
        sems_ = rest[0:6]
        for cp in _chips_copies(x_ref, land_ref, sems_, scatter, half)[1]:
            cp.wait_send()
            cp.wait_recv()

    return pl.pallas_call(
        body, name=name, out_shape=(pltpu.HBM(src.shape, src.dtype), pltpu.HBM(land.shape, land.dtype)),
        in_specs=(HBM, HBM) + (SEM,) * 6 + (ANY,) * len(after), out_specs=(HBM, HBM), input_output_aliases={0: 0, 1: 1},
        compiler_params=pltpu.CompilerParams(has_side_effects=pltpu.SideEffectType.DATAFLOW_SIDE_EFFECTING),
    )(src, land, *sems, *after)


def _row_tile(r, pref=512):
    return max(t for t in range(16, pref + 1, 16) if r % t == 0)


def _pair_complete(land, *, name):
    r = land.shape[1]
    hr = r // 2
    assert r == 2 * hr and hr % 16 == 0

    def body(in_ref, out_ref, send_sems, recv_sems):
        x, y, c = _place()
        ids = [2 * cx + cy for cx, cy in [(1 - x, y), (x, 1 - y), (1 - x, 1 - y)]]
        mine_rows = pl.ds(pl.multiple_of(c * hr, 16), hr)
        other_rows = pl.ds(pl.multiple_of((1 - c) * hr, 16), hr)

        def copy(j, rows):
            return pltpu.make_async_remote_copy(
                src_ref=in_ref.at[ids[j], mine_rows], dst_ref=out_ref.at[ids[j], rows], send_sem=send_sems.at[j],
                recv_sem=recv_sems.at[j], device_id=(x, y, 1 - c), device_id_type=MESH)

        sends = [copy(j, mine_rows) for j in range(3)]
        for cp in sends:
            cp.start()
        for j in range(3):
            copy(j, other_rows).wait_recv()
        for cp in sends:
            cp.wait_send()

    return pl.pallas_call(
        body, name=name, out_shape=jax.ShapeDtypeStruct(land.shape, land.dtype), in_specs=[ANY], out_specs=ANY,
        input_output_aliases={0: 0},
        scratch_shapes=[pltpu.SemaphoreType.DMA((3,)), pltpu.SemaphoreType.DMA((3,))],
    )(land)


def _pair_swap(a, *, name):
    n, r, cols = a.shape
    hr = r // 2

    def body(x_ref, out_ref, send_sem, recv_sem):
        x, y, c = _place()
        other_rows = pl.ds(pl.multiple_of((1 - c) * hr, 16), hr)
        cp = pltpu.make_async_remote_copy(src_ref=x_ref.at[:, other_rows], dst_ref=out_ref, send_sem=send_sem,
                                          recv_sem=recv_sem, device_id=(x, y, 1 - c), device_id_type=MESH)
        cp.start()
        cp.wait()

    return pl.pallas_call(
        body, name=name, out_shape=jax.ShapeDtypeStruct((n, hr, cols), a.dtype), in_specs=[ANY], out_specs=ANY,
        scratch_shapes=[pltpu.SemaphoreType.DMA, pltpu.SemaphoreType.DMA],
    )(a)


def _sibling_copy(a, *, name):
    def body(x_ref, out_ref, send_sem, recv_sem):
        x, y, c = _place()
        cp = pltpu.make_async_remote_copy(src_ref=x_ref, dst_ref=out_ref, send_sem=send_sem, recv_sem=recv_sem,
                                          device_id=(x, y, 1 - c), device_id_type=MESH)
        cp.start()
        cp.wait()

    return pl.pallas_call(
        body, name=name, out_shape=jax.ShapeDtypeStruct(a.shape, a.dtype), in_specs=[ANY], out_specs=ANY,
        scratch_shapes=[pltpu.SemaphoreType.DMA, pltpu.SemaphoreType.DMA],
    )(a)


def _sum_slots(a, own, *, name):
    _, r, c = a.shape
    tr = _row_tile(r, 256)

    def body(a_ref, own_ref, o_ref):
        k = 2 * lax.axis_index("x") + lax.axis_index("y")
        acc = None
        for j in range(N_CHIP):
            term = jnp.where(k == j, own_ref[j], a_ref[j]).astype(F32)
            acc = term if acc is None else acc + term
        o_ref[...] = acc

    spec = pl.BlockSpec((N_CHIP, tr, c), lambda i: (0, i, 0))
    return pl.pallas_call(
        body, name=name, grid=(r // tr,), in_specs=[spec, spec],
        out_specs=pl.BlockSpec((tr, c), lambda i: (i, 0)), out_shape=jax.ShapeDtypeStruct((r, c), F32),
        compiler_params=_cp(("arbitrary",)),
    )(a, own)


def _add2(a, b, *, name):
    r, c = a.shape
    tr = _row_tile(r)

    def body(a_ref, b_ref, o_ref):
        o_ref[...] = (a_ref[...].astype(F32) + b_ref[...].astype(F32)).astype(o_ref.dtype)

    spec = pl.BlockSpec((tr, c), lambda i: (i, 0))
    return pl.pallas_call(
        body, name=name, grid=(r // tr,), in_specs=[spec, spec], out_specs=spec,
        out_shape=jax.ShapeDtypeStruct((r, c), a.dtype), compiler_params=_cp(("arbitrary",)),
    )(a, b)


BIG = ("w_in", "w_mlp1", "w_attn_out", "w_ssd_out", "w_o", "w_mlp2")
COL_SHARDED = ("w_mlp1", "w_in")
ROW_SHARDED = ("w_attn_out", "w_ssd_out", "w_o", "w_mlp2")
LATE = ROW_SHARDED + ("w_mlp1",)
SMALL = ("b_ada", "norm1_w", "norm2_w", "q_norm_w", "k_norm_w", "conv_b", "A_log", "dt_bias", "ssd_D", "ssd_norm_w")
NAMES = ("w_ada", "b_ada", "norm1_w", "norm2_w", "w_in", "q_norm_w", "k_norm_w", "conv_w", "conv_b", "A_log", "dt_bias",
         "ssd_D", "ssd_norm_w", "w_attn_out", "w_ssd_out", "w_o", "w_mlp1", "w_mlp2")
W_IN_COLS = 8768


def _permute_in(w):
    return jnp.concatenate([w[:, 4608:6656], w[:, 6720:8768], w[:, 1536:4608], w[:, 0:1536], w[:, 6656:6720],
                            jnp.zeros((w.shape[0], PW - W_IN_COLS), w.dtype)], axis=1)


def _unpermute_in(wp):
    return jnp.concatenate([wp[:, Q0:DT0], wp[:, XS0:Q0], wp[:, Z0:GA0], wp[:, DT0:DT0 + 64], wp[:, GA0:XS0]], axis=1)


def _pad_to(v, n):
    return jnp.pad(v, (0, n - v.shape[0]))


def _step(w, m, v, loss_target):
    xi, yi, ci = _place()
    chip = 2 * xi + yi
    dev = 4 * xi + 2 * yi + ci
    x, tgt = w["x"], loss_target
    d = x.shape[1]

    cw = w["conv_w"].shape[1]
    v0 = _pad_to(jnp.concatenate([w["c"].reshape(-1), w["conv_w"].reshape(-1)]), 5120).reshape(8, 640)
    g0 = _allgather8(v0, name="ag_cond").reshape(N_DEV, 5120)
    c_all = g0[:, 0:d]
    conv_w = jnp.concatenate([g0[2 * k, d:d + D_CONV * cw].reshape(D_CONV, cw) for k in range(N_CHIP)], axis=1)
    sc = _silu_cast(c_all, name="silu_c")
    modp = _mm(sc, w["w_ada"].astype(MMD), name="ada_fwd", outs=[F32], tm=8, tn=512)
    g1 = _allgather8(modp, name="ag_mod").reshape(N_DEV, N_DEV, modp.shape[1])
    mod_all = jnp.concatenate([g1[2 * k] for k in range(N_CHIP)], axis=1)
    mod = (lax.dynamic_slice_in_dim(mod_all, dev, 1, axis=0) + w["b_ada"]).reshape(6, d)

    mine, mod = lax.optimization_barrier((w["w_in"].astype(MMD), mod))
    in_sems, in_src, in_land, in_token = _chips_start(mine, False, True, name="ag_w_in_start")
    mod = mod + in_token[0:1, 0:1]
    small = {n: w[n] for n in SMALL if n != "b_ada"}
    small["conv_w"] = conv_w
    started = {}

    late_mine = jnp.concatenate([w[n].astype(MMD) for n in LATE], axis=0) + in_token[0:1, 0:1].astype(MMD)

    def in_weights(after):
        src, land = _chips_wait(in_sems, in_src, in_land, [*after, late_mine], False, True, name="ag_w_in_wait")
        land = _pair_complete(land, name="ag_w_in_pair")
        late, land = lax.optimization_barrier((late_mine, land))
        sems, late_src, late_land, token = _chips_start(late, False, name="ag_late_start")
        started["ag_late"] = (sems, late_src, late_land)
        w_in = jnp.concatenate([jnp.where(chip == k, src, land[k]) for k in range(N_CHIP)], axis=1)
        w_dt = jnp.pad(w_in[:, 6656:6720], ((0, 0), (0, 64))) + token[0:1, 0:1].astype(MMD)
        return {"w_in_p": _permute_in(w_in), "w_dt": w_dt}

    def late_weights(after):
        src, land = _chips_wait(*started["ag_late"], after, False, name="ag_late_wait")
        out, o = {}, 0
        for n in LATE:
            rows = w[n].shape[0]
            parts = [jnp.where(chip == k, src[o:o + rows], land[k, o:o + rows]) for k in range(N_CHIP)]
            out[n] = jnp.concatenate(parts, axis=1 if n in COL_SHARDED else 0)
            o += rows
        return out

    def pair_sums(slots, tag):
        _, rows, cols = slots.shape
        hr = rows // 2
        theirs = _pair_swap(slots, name="rs_pair_" + tag)
        ours = lax.dynamic_slice_in_dim(slots, ci * hr, hr, axis=1)
        pair = _add2(ours.reshape(N_CHIP * hr, cols), theirs.reshape(N_CHIP * hr, cols), name="rs_pair_sum_" + tag)
        return pair.reshape(N_CHIP, hr, cols)

    def finish(recv, pair, tag):
        half = _sum_slots(recv, pair, name="rs_sum_" + tag)
        other = _sibling_copy(half, name="rs_sibling_" + tag)
        return jnp.where(ci == 0, jnp.concatenate([half, other], axis=0), jnp.concatenate([other, half], axis=0))

    def late_grads(gw):
        slots = []
        for k in range(N_CHIP):
            parts = []
            for n in LATE:
                rows = w[n].shape[0]
                blk = gw[n][:, k * rows:(k + 1) * rows] if n in COL_SHARDED else gw[n][k * rows:(k + 1) * rows]
                parts.append(blk.astype(MMD))
            slots.append(jnp.concatenate(parts, axis=0))
        pair = pair_sums(jnp.stack(slots), "late")
        sems, src, land, token = _chips_start(pair, True, name="rs_late_start")
        started["late"] = (sems, src, land)
        return token[0:1, 0:1]

    def in_grad(g):
        g_in = _unpermute_in(g)
        cols_in = w["w_in"].shape[1]
        pair = pair_sums(jnp.stack([g_in[:, k * cols_in:(k + 1) * cols_in].astype(MMD) for k in range(N_CHIP)]), "w_in")
        sems, src, land, token = _chips_start(pair, True, name="rs_w_in_start")
        started["w_in"] = (sems, src, land)
        return token

    loss, grad_x, dmod, gw, gs = _local_step(x, tgt, mod, {}, small, in_weights, late_weights, late_grads, in_grad,
                                             in_token[0, 0])

    grads = {}
    pair, land = _chips_wait(*started["w_in"], grad_x, True, name="rs_w_in_wait")
    grads["w_in"] = finish(land, pair, "w_in")
    pair, land = _chips_wait(*started["late"], grad_x, True, name="rs_late_wait")
    total, o = finish(land, pair, "late"), 0
    for n in LATE:
        rows = w[n].shape[0]
        grads[n] = total[o:o + rows]
        o += rows

    order = ([dmod.reshape(-1)] + [gs[n].reshape(-1) for n in SMALL if n != "b_ada"] + [gs["conv_w"].reshape(-1)]
             + [loss.reshape(-1)])
    vec = jnp.concatenate(order)
    n_small = vec.shape[0]
    n_pad = -(-n_small // 1024) * 1024
    g2 = _allgather8(_pad_to(vec, n_pad).reshape(8, n_pad // 8), name="ag_small")
    tot = _rows_sum(g2, N_DEV, name="small_sum").reshape(-1)
    loss = tot[n_small - 1]
    dmod_all = g2.reshape(N_DEV, n_pad)[:, 0:6 * d]
    off = 0
    for n in SMALL:
        grads[n] = tot[off:off + w[n].size].reshape(w[n].shape)
        off += w[n].size
    conv_full = tot[off:off + D_CONV * N_CHIP * cw].reshape(D_CONV, N_CHIP * cw)
    grads["conv_w"] = lax.dynamic_slice_in_dim(conv_full, chip * cw, cw, axis=1)
    ada_cols = w["w_ada"].shape[1]
    dmod_mine = lax.dynamic_slice_in_dim(dmod_all, chip * ada_cols, ada_cols, axis=1).astype(MMD)
    grads["w_ada"] = _mm_tn(sc, dmod_mine, name="ada_dw", tk=512, tn=512, tmm=8)

    delta, new_m, new_v = {}, {}, {}
    pack = lambda t: jnp.concatenate([t[n].reshape(-1) for n in SMALL]).reshape(1, -1)
    ds_, ms_, vs_ = _adamw(pack(w), pack(grads), pack(m), pack(v), name="adamw_small")
    off = 0
    for n in SMALL:
        for dst, src in ((delta, ds_), (new_m, ms_), (new_v, vs_)):
            dst[n] = src[0, off:off + w[n].size].reshape(w[n].shape)
        off += w[n].size
    for n in ("w_ada", "conv_w") + BIG:
        delta[n], new_m[n], new_v[n] = _adamw(w[n], grads[n], m[n], v[n], name="adamw_" + n)
    return loss, grad_x, grads, delta, new_m, new_v


def kernel(x, c, w_ada, b_ada, norm1_w, norm2_w, w_in, q_norm_w, k_norm_w, conv_w, conv_b, A_log, dt_bias, ssd_D, ssd_norm_w, w_attn_out, w_ssd_out, w_o, w_mlp1, w_mlp2, loss_target, m_w_ada, m_b_ada, m_norm1_w, m_norm2_w, m_w_in, m_q_norm_w, m_k_norm_w, m_conv_w, m_conv_b, m_A_log, m_dt_bias, m_ssd_D, m_ssd_norm_w, m_w_attn_out, m_w_ssd_out, m_w_o, m_w_mlp1, m_w_mlp2, v_w_ada, v_b_ada, v_norm1_w, v_norm2_w, v_w_in, v_q_norm_w, v_k_norm_w, v_conv_w, v_conv_b, v_A_log, v_dt_bias, v_ssd_D, v_ssd_norm_w, v_w_attn_out, v_w_ssd_out, v_w_o, v_w_mlp1, v_w_mlp2):
    args = dict(locals())
    strip = lambda a: a[0] if a.ndim == 3 else a
    w = {n: strip(args[n]) for n in NAMES + ("x", "c")}
    m = {n: strip(args["m_" + n]) for n in NAMES}
    v = {n: strip(args["v_" + n]) for n in NAMES}
    loss, grad_x, grads, delta, new_m, new_v = _step(w, m, v, loss_target[0])
    like = lambda t, n: t.reshape(args[n].shape)
    return (loss, grad_x[None], *[like(grads[n], n) for n in NAMES], *[like(delta[n], n) for n in NAMES],
            *[like(new_m[n], n) for n in NAMES], *[like(new_v[n], n) for n in NAMES])
```

```python
import math

import jax
import jax.numpy as jnp
from jax import lax
from jax.experimental import pallas as pl
from jax.experimental.pallas import tpu as pltpu

F32 = jnp.float32
MMD = jnp.bfloat16
EPS = 1e-6
NEG = -1e30
MIB = 1024 * 1024
VMEM_BIG = 56 * MIB
VMEM_MID = 40 * MIB

GRID_W = 64
N_Q_HEADS, N_KV_HEADS, HEAD_DIM = 16, 4, 64
ROPE_THETA = 10000.0
SSD_HEADS, SSD_GROUPS, SSD_P, SSD_N, CHUNK = 32, 4, 64, 128, 128
HPG = SSD_HEADS // SSD_GROUPS
D_CONV = 5
ADAM_LR, ADAM_B1, ADAM_B2, ADAM_EPS, ADAM_WD, ADAM_STEP = 0.001, 0.9, 0.999, 1e-08, 0.01, 10

Z0, GA0, GS0, XS0, B0, C0, Q0, K0, V0, DT0, PW = 0, 2048, 3072, 4096, 6144, 6656, 7168, 8192, 8448, 8704, 8832

MESH = pl.DeviceIdType.MESH
NT = (((1,), (1,)), ((), ()))
TN = (((0,), (0,)), ((), ()))


def _cp(sem=None, vmem=VMEM_MID):
    return pltpu.CompilerParams(dimension_semantics=sem, vmem_limit_bytes=vmem)


def _tile(n, pref):
    t = min(n, pref)
    while n % t:
        t //= 2
    return t


def _dot(a, b, dims=None):
    if dims is None:
        return jnp.dot(a, b, preferred_element_type=F32)
    return lax.dot_general(a, b, dims, preferred_element_type=F32)


def _dot_hi(a01, b):
    a = a01.astype(jnp.bfloat16)
    h1 = b.astype(jnp.bfloat16)
    r1 = b - h1.astype(F32)
    h2 = r1.astype(jnp.bfloat16)
    return _dot(a, h1) + _dot(a, h2) + _dot(a, (r1 - h2.astype(F32)).astype(jnp.bfloat16))


def _sigmoid(x):
    return jax.nn.sigmoid(x)


def _mm(a, b, *, name, outs, nt=False, ta=False, extras=(), epi=None, tm=512, tn=512, n=None, b_outer=False,
        vmem=VMEM_MID):
    assert not (nt and ta)
    k, m = a.shape if ta else a.shape[::-1]
    if n is None:
        n = b.shape[0] if nt else b.shape[1]
    tm, tn = _tile(m, tm), _tile(n, tn)
    gi, gj = m // tm, n // tn
    if b_outer:
        grid = (gj, gi)
        ij = lambda p, q: (q, p)
    else:
        grid = (gi, gj)
        ij = lambda p, q: (p, q)
    if ta:
        a_spec = pl.BlockSpec((k, tm), lambda p, q: (0, ij(p, q)[0]))
    else:
        a_spec = pl.BlockSpec((tm, k), lambda p, q: (ij(p, q)[0], 0))
    if nt:
        b_spec = pl.BlockSpec((tn, k), lambda p, q: (ij(p, q)[1], 0))
    else:
        b_spec = pl.BlockSpec((k, tn), lambda p, q: (0, ij(p, q)[1]))
    e_specs = []
    for arr, kind, off in extras:
        ob = off // tn
        assert off % tn == 0
        if kind == "tile":
            e_specs.append(pl.BlockSpec((tm, tn), lambda p, q, ob=ob: (ij(p, q)[0], ob + ij(p, q)[1])))
        else:
            e_specs.append(pl.BlockSpec((1, tn), lambda p, q, ob=ob: (0, ob + ij(p, q)[1])))
    ne = len(extras)

    def body(a_ref, b_ref, *rest):
        acc = _dot(a_ref[...], b_ref[...], NT if nt else (TN if ta else None))
        res = epi(acc, *[e[...] for e in rest[:ne]]) if epi is not None else (acc,)
        for o_ref, r in zip(rest[ne:], res):
            o_ref[...] = r.astype(o_ref.dtype)

    out = pl.pallas_call(
        body, name=name, grid=grid,
        in_specs=[a_spec, b_spec] + e_specs,
        out_specs=[pl.BlockSpec((tm, tn), lambda p, q: ij(p, q)) for _ in outs],
        out_shape=[jax.ShapeDtypeStruct((m, n), dt) for dt in outs],
        compiler_params=_cp(("arbitrary", "arbitrary"), vmem),
    )(a, b, *[e[0] for e in extras])
    return out if len(outs) > 1 else out[0]


def _mm_tn(a, g, *, name, tk=512, tn=1024, tmm=4096, vmem=VMEM_MID):
    m, k = a.shape
    n = g.shape[1]
    tk, tn, tmm = _tile(k, tk), _tile(n, tn), _tile(m, tmm)

    def body(a_ref, g_ref, o_ref):
        p = _dot(a_ref[...], g_ref[...], TN)

        @pl.when(pl.program_id(2) == 0)
        def _():
            o_ref[...] = p

        @pl.when(pl.program_id(2) > 0)
        def _():
            o_ref[...] += p

    return pl.pallas_call(
        body, name=name, grid=(k // tk, n // tn, m // tmm),
        in_specs=[pl.BlockSpec((tmm, tk), lambda i, j, r: (r, i)), pl.BlockSpec((tmm, tn), lambda i, j, r: (r, j))],
        out_specs=pl.BlockSpec((tk, tn), lambda i, j, r: (i, j)),
        out_shape=jax.ShapeDtypeStruct((k, n), F32),
        compiler_params=_cp(("arbitrary", "arbitrary", "arbitrary"), vmem),
    )(a, g)


def _adamw(w, g, m, v, *, name):
    r, c = w.shape
    tr = _tile(r, 256) if r % 8 == 0 else r

    def body(w_ref, g_ref, m_ref, v_ref, d_ref, nm_ref, nv_ref):
        gg = g_ref[...]
        nm = ADAM_B1 * m_ref[...] + (1.0 - ADAM_B1) * gg
        nv = ADAM_B2 * v_ref[...] + (1.0 - ADAM_B2) * jnp.square(gg)
        m_hat = nm / (1.0 - ADAM_B1 ** ADAM_STEP)
        v_hat = nv / (1.0 - ADAM_B2 ** ADAM_STEP)
        d_ref[...] = -ADAM_LR * (m_hat / (jnp.sqrt(v_hat) + ADAM_EPS) + ADAM_WD * w_ref[...])
        nm_ref[...] = nm
        nv_ref[...] = nv

    spec = pl.BlockSpec((tr, c), lambda i: (i, 0))
    return pl.pallas_call(
        body, name=name, grid=(r // tr,), in_specs=[spec] * 4, out_specs=[spec] * 3,
        out_shape=[jax.ShapeDtypeStruct((r, c), F32)] * 3, compiler_params=_cp(("arbitrary",)),
    )(w, g, m, v)


def _rows_sum(a, groups, *, name):
    r = a.shape[0] // groups

    def body(a_ref, o_ref):
        acc = a_ref[0:r, :]
        for d in range(1, groups):
            acc = acc + a_ref[d * r:(d + 1) * r, :]
        o_ref[...] = acc

    return pl.pallas_call(body, name=name, out_shape=jax.ShapeDtypeStruct((r, a.shape[1]), F32))(a)


def _silu_cast(a, *, name):
    def body(a_ref, o_ref):
        x = a_ref[...]
        o_ref[...] = (x * _sigmoid(x)).astype(o_ref.dtype)

    return pl.pallas_call(body, name=name, out_shape=jax.ShapeDtypeStruct(a.shape, MMD))(a)


def _sumsq(a, *, name):
    m, n = a.shape
    tm = _tile(m, 512)

    def body(a_ref, o_ref):
        x = a_ref[...]
        p = jnp.sum(jnp.sum(x * x, axis=1, keepdims=True), axis=0, keepdims=True)

        @pl.when(pl.program_id(0) == 0)
        def _():
            o_ref[...] = p

        @pl.when(pl.program_id(0) > 0)
        def _():
            o_ref[...] += p

    return pl.pallas_call(
        body, name=name, grid=(m // tm,), in_specs=[pl.BlockSpec((tm, n), lambda i: (i, 0))],
        out_specs=pl.BlockSpec((1, 1), lambda i: (0, 0)), out_shape=jax.ShapeDtypeStruct((1, 1), F32),
        compiler_params=_cp(("arbitrary",)),
    )(a)


def _acc_rows(o_ref, p, first):
    @pl.when(first)
    def _():
        o_ref[...] = p

    @pl.when(jnp.logical_not(first))
    def _():
        o_ref[...] += p


def _ln_mod(x, w, scale, shift, *, name):
    s, d = x.shape
    tm = _tile(s, 512)

    def body(x_ref, w_ref, sc_ref, sh_ref, o_ref):
        xv = x_ref[...]
        r = lax.rsqrt(jnp.mean(xv * xv, axis=-1, keepdims=True) + EPS)
        o_ref[...] = ((xv * r) * w_ref[...] * (1.0 + sc_ref[...]) + sh_ref[...]).astype(o_ref.dtype)

    row = pl.BlockSpec((1, d), lambda i: (0, 0))
    big = pl.BlockSpec((tm, d), lambda i: (i, 0))
    return pl.pallas_call(
        body, name=name, grid=(s // tm,), in_specs=[big, row, row, row], out_specs=big,
        out_shape=jax.ShapeDtypeStruct((s, d), MMD), compiler_params=_cp(("arbitrary",)),
    )(x, w, scale, shift)


def _ln_mod_bwd(dh, x, w, scale, dres, *, name):
    s, d = x.shape
    tm = _tile(s, 512)

    def body(dh_ref, x_ref, w_ref, sc_ref, dres_ref, dx_ref, dsh_ref, dsc_ref, dw_ref):
        xv = x_ref[...]
        dhv = dh_ref[...].astype(F32)
        r = lax.rsqrt(jnp.mean(xv * xv, axis=-1, keepdims=True) + EPS)
        nv = xv * r
        wv = w_ref[...]
        g1 = 1.0 + sc_ref[...]
        dn = dhv * (wv * g1)
        dx_ref[...] = dres_ref[...] + r * (dn - nv * jnp.mean(dn * nv, axis=-1, keepdims=True))
        first = pl.program_id(0) == 0
        _acc_rows(dsh_ref, jnp.sum(dhv, axis=0, keepdims=True), first)
        _acc_rows(dsc_ref, jnp.sum(dhv * nv * wv, axis=0, keepdims=True), first)
        _acc_rows(dw_ref, jnp.sum(dhv * nv * g1, axis=0, keepdims=True), first)

    row = pl.BlockSpec((1, d), lambda i: (0, 0))
    big = pl.BlockSpec((tm, d), lambda i: (i, 0))
    return pl.pallas_call(
        body, name=name, grid=(s // tm,), in_specs=[big, big, row, row, big], out_specs=[big, row, row, row],
        out_shape=[jax.ShapeDtypeStruct((s, d), F32)] + [jax.ShapeDtypeStruct((1, d), F32)] * 3,
        compiler_params=_cp(("arbitrary",)),
    )(dh, x, w, scale, dres)


def _gate_bwd(dy, u, gate, *, name):
    s, d = dy.shape
    tm = _tile(s, 512)

    def body(dy_ref, u_ref, g_ref, du_ref, dg_ref):
        dyv = dy_ref[...]
        du_ref[...] = (dyv * g_ref[...]).astype(du_ref.dtype)
        _acc_rows(dg_ref, jnp.sum(dyv * u_ref[...].astype(F32), axis=0, keepdims=True), pl.program_id(0) == 0)

    row = pl.BlockSpec((1, d), lambda i: (0, 0))
    big = pl.BlockSpec((tm, d), lambda i: (i, 0))
    return pl.pallas_call(
        body, name=name, grid=(s // tm,), in_specs=[big, big, row], out_specs=[big, row],
        out_shape=[jax.ShapeDtypeStruct((s, d), MMD), jax.ShapeDtypeStruct((1, d), F32)],
        compiler_params=_cp(("arbitrary",)),
    )(dy, u, gate)


def _seg64(v, e):
    hi = v.astype(jnp.bfloat16)
    lo = (v - hi.astype(F32)).astype(jnp.bfloat16)
    return _dot(hi, e) + _dot(lo, e)


def _rope_tables(s, zero=0.0):
    rows = s // GRID_W
    pos_row = jnp.repeat(jnp.arange(rows, dtype=jnp.int32), GRID_W).astype(F32) + zero
    pos_col = jnp.tile(jnp.arange(GRID_W, dtype=jnp.int32), rows).astype(F32) + zero
    axis_dim = HEAD_DIM // 2
    inv_freq = ROPE_THETA ** (-jnp.arange(0, axis_dim, 2, dtype=F32) / axis_dim)
    ang_r = pos_row[:, None] * inv_freq[None, :]
    ang_c = pos_col[:, None] * inv_freq[None, :]
    zero = jnp.zeros_like(ang_r)
    cos = jnp.concatenate([jnp.cos(ang_r), jnp.cos(ang_r), jnp.cos(ang_c), jnp.cos(ang_c)], axis=1)
    s_a = jnp.concatenate([-jnp.sin(ang_r), zero, -jnp.sin(ang_c), zero], axis=1)
    s_b = jnp.concatenate([zero, jnp.sin(ang_r), zero, jnp.sin(ang_c)], axis=1)
    return [jnp.tile(t, (1, 2)) for t in (cos, s_a, s_b)]


def _e128():
    i = jnp.arange(128)
    return (i[:, None] // 64 == i[None, :] // 64).astype(jnp.bfloat16)


QKW = N_Q_HEADS * HEAD_DIM + N_KV_HEADS * HEAD_DIM


def _qk_fwd(proj, wrow, scrow, tabs, *, name):
    s = proj.shape[0]
    tm = _tile(s, 1024)

    def body(x_ref, w_ref, sc_ref, cos_ref, sa_ref, sb_ref, e_ref, ot_ref):
        u = x_ref[...].astype(F32)
        r = lax.rsqrt(_seg64(u * u, e_ref[...]) * (1.0 / HEAD_DIM) + EPS)
        nv = (u * r) * w_ref[...]
        ro = nv * cos_ref[...] + pltpu.roll(nv, 112, 1) * sa_ref[...] + pltpu.roll(nv, 16, 1) * sb_ref[...]
        ot_ref[...] = (ro * sc_ref[...]).T.astype(ot_ref.dtype)

    tab = pl.BlockSpec((tm, 128), lambda i, j: (i, 0))
    row = pl.BlockSpec((1, 128), lambda i, j: (0, j))
    return pl.pallas_call(
        body, name=name, grid=(s // tm, QKW // 128),
        in_specs=[pl.BlockSpec((tm, 128), lambda i, j: (i, Q0 // 128 + j)), row, row, tab, tab, tab,
                  pl.BlockSpec((128, 128), lambda i, j: (0, 0))],
        out_specs=pl.BlockSpec((128, tm), lambda i, j: (j, i)),
        out_shape=jax.ShapeDtypeStruct((QKW, s), MMD), compiler_params=_cp(("arbitrary", "arbitrary")),
    )(proj, wrow, scrow, *tabs, _e128())


def _qk_bwd(dqt, dkt, proj, wrow, scrow, tabs, dproj, *, name):
    s = proj.shape[0]
    tm = _tile(s, 1024)
    nq = dqt.shape[0] // 128

    def body(dq_ref, dk_ref, x_ref, w_ref, sc_ref, cos_ref, sa_ref, sb_ref, e_ref, _, du_ref, dw_ref):
        e = e_ref[...]
        d = jnp.where(pl.program_id(0) < nq, dq_ref[...], dk_ref[...]).T * sc_ref[...]
        dn = d * cos_ref[...] + pltpu.roll(d * sa_ref[...], 16, 1) + pltpu.roll(d * sb_ref[...], 112, 1)
        u = x_ref[...].astype(F32)
        r = lax.rsqrt(_seg64(u * u, e) * (1.0 / HEAD_DIM) + EPS)
        uh = u * r
        _acc_rows(dw_ref, jnp.sum(dn * uh, axis=0, keepdims=True), pl.program_id(1) == 0)
        dnw = dn * w_ref[...]
        du_ref[...] = (r * (dnw - uh * (_seg64(dnw * uh, e) * (1.0 / HEAD_DIM)))).astype(du_ref.dtype)

    tab = pl.BlockSpec((tm, 128), lambda j, i: (i, 0))
    row = pl.BlockSpec((1, 128), lambda j, i: (0, j))
    qcol = pl.BlockSpec((tm, 128), lambda j, i: (i, Q0 // 128 + j))
    return pl.pallas_call(
        body, name=name, grid=(QKW // 128, s // tm),
        in_specs=[pl.BlockSpec((128, tm), lambda j, i: (jnp.minimum(j, nq - 1), i)),
                  pl.BlockSpec((128, tm), lambda j, i: (jnp.maximum(j - nq, 0), i)),
                  qcol, row, row, tab, tab, tab, pl.BlockSpec((128, 128), lambda j, i: (0, 0)), ANY],
        out_specs=[qcol, row],
        out_shape=[jax.ShapeDtypeStruct(dproj.shape, dproj.dtype), jax.ShapeDtypeStruct((1, QKW), F32)],
        input_output_aliases={9: 0}, compiler_params=_cp(("arbitrary", "arbitrary")),
    )(dqt, dkt, proj, wrow, scrow, *tabs, _e128(), dproj)


REP = N_Q_HEADS // N_KV_HEADS


def _lanes(ref):
    return jnp.concatenate([ref[r] for r in range(REP)], axis=1)


V_AUG = HEAD_DIM + 8
LOG2E = math.log2(math.e)


def _flash_fwd(qkt, vta, *, name):
    s = qkt.shape[2]
    tq, tk = _tile(s, 1024), _tile(s, 512)
    nk = s // tk
    lanes = REP * tq

    def body(q_ref, k_ref, v_ref, o_ref, lse_ref, m_ref, acc_ref):
        j = pl.program_id(2)

        @pl.when(j == 0)
        def _():
            m_ref[...] = jnp.full_like(m_ref, NEG)
            acc_ref[...] = jnp.zeros_like(acc_ref)

        st = _dot(k_ref[0], _lanes(q_ref), TN)
        m_prev = m_ref[...]
        m_new = jnp.maximum(m_prev, jnp.max(st, axis=0, keepdims=True))
        p = jnp.exp2(st - m_new).astype(MMD)
        acc_ref[...] = jnp.exp2(m_prev - m_new) * acc_ref[...] + _dot(v_ref[0], p)
        m_ref[...] = m_new

        @pl.when(j == nk - 1)
        def _():
            acc = acc_ref[...]
            l = acc[HEAD_DIM:HEAD_DIM + 1]
            o = acc[0:HEAD_DIM] / l
            ls = m_ref[...] + jnp.log(l) * LOG2E
            for r in range(REP):
                o_ref[r] = o[:, r * tq:(r + 1) * tq].astype(o_ref.dtype)
                lse_ref[r] = ls[:, r * tq:(r + 1) * tq]

    qspec = pl.BlockSpec((REP, HEAD_DIM, tq), lambda g, i, j: (g, 0, i))
    return pl.pallas_call(
        body, name=name, grid=(N_KV_HEADS, s // tq, nk),
        in_specs=[qspec, pl.BlockSpec((1, HEAD_DIM, tk), lambda g, i, j: (N_Q_HEADS + g, 0, j)),
                  pl.BlockSpec((1, V_AUG, tk), lambda g, i, j: (g, 0, j))],
        out_specs=[qspec, pl.BlockSpec((REP, 1, tq), lambda g, i, j: (g, 0, i))],
        out_shape=[jax.ShapeDtypeStruct((N_Q_HEADS, HEAD_DIM, s), MMD), jax.ShapeDtypeStruct((N_Q_HEADS, 1, s), F32)],
        scratch_shapes=[pltpu.VMEM((1, lanes), F32), pltpu.VMEM((V_AUG, lanes), F32)],
        compiler_params=_cp(("arbitrary", "arbitrary", "arbitrary"), VMEM_BIG),
    )(qkt, qkt, vta)


def _flash_bwd(qkt, vta, dot, ot, lse, *, name):
    s = qkt.shape[2]
    tq, tk = _tile(s, 512), _tile(s, 1024)
    nk = s // tk

    def body(q_ref, kt_ref, vt_ref, do_ref, o_ref, lse_ref, dq_ref, dk_ref, dv_ref, dq_acc):
        i, j = pl.program_id(1), pl.program_id(2)
        q, do = _lanes(q_ref), _lanes(do_ref)
        delta = jnp.sum(do.astype(F32) * _lanes(o_ref).astype(F32), axis=0, keepdims=True)
        kt, vt = kt_ref[0], vt_ref[0, 0:HEAD_DIM, :]
        p = jnp.exp2(_dot(kt, q, TN) - _lanes(lse_ref))
        dvc = _dot(p.astype(MMD), do, NT)
        ds = (p * (_dot(vt, do, TN) - delta)).astype(MMD)
        dkc = _dot(ds, q, NT) * (1.0 / LOG2E)
        dqc = _dot(kt, ds)
        rows = pl.ds(pl.multiple_of(j * tk, tk), tk)

        @pl.when(i == 0)
        def _():
            dk_ref[0, rows, :] = dkc
            dv_ref[0, rows, :] = dvc

        @pl.when(i > 0)
        def _():
            dk_ref[0, rows, :] += dkc
            dv_ref[0, rows, :] += dvc

        @pl.when(j == 0)
        def _():
            dq_acc[...] = dqc

        @pl.when(j > 0)
        def _():
            dq_acc[...] += dqc

        @pl.when(j == nk - 1)
        def _():
            acc = dq_acc[...]
            for r in range(REP):
                dq_ref[r] = acc[:, r * tq:(r + 1) * tq]

    qspec = pl.BlockSpec((REP, HEAD_DIM, tq), lambda g, i, j: (g, 0, i))
    kvres = pl.BlockSpec((1, s, HEAD_DIM), lambda g, i, j: (g, 0, 0))
    return pl.pallas_call(
        body, name=name, grid=(N_KV_HEADS, s // tq, nk),
        in_specs=[qspec, pl.BlockSpec((1, HEAD_DIM, tk), lambda g, i, j: (N_Q_HEADS + g, 0, j)),
                  pl.BlockSpec((1, V_AUG, tk), lambda g, i, j: (g, 0, j)),
                  qspec, qspec, pl.BlockSpec((REP, 1, tq), lambda g, i, j: (g, 0, i))],
        out_specs=[qspec, kvres, kvres],
        out_shape=[jax.ShapeDtypeStruct((N_Q_HEADS, HEAD_DIM, s), F32), jax.ShapeDtypeStruct((N_KV_HEADS, s, HEAD_DIM), F32),
                   jax.ShapeDtypeStruct((N_KV_HEADS, s, HEAD_DIM), F32)],
        scratch_shapes=[pltpu.VMEM((HEAD_DIM, REP * tq), F32)],
        compiler_params=_cp(("arbitrary", "arbitrary", "arbitrary"), VMEM_BIG),
    )(qkt, qkt, vta, dot, ot, lse)


HALO = 8
CONV_W = 2048 + 2 * SSD_GROUPS * SSD_N


def _shifted(win, off, r):
    return pltpu.roll(win, (r + 2 * HALO - off) % (r + 2 * HALO), 0)[0:r]


def _conv_fwd(proj, w8, brow, *, name):
    s = proj.shape[0]
    cb = 256
    r = _tile(s, 512)

    def body(x_ref, w_ref, b_ref, o_ref, pad_ref):
        zeros = jnp.zeros((HALO, cb), F32)
        pad_ref[0:HALO, :] = zeros
        pad_ref[s + HALO:s + 2 * HALO, :] = zeros

        def fill(i, carry):
            st = pl.multiple_of(i * r, r)
            pad_ref[pl.ds(st + HALO, r), :] = x_ref[pl.ds(st, r), :].astype(F32)
            return carry

        lax.fori_loop(0, s // r, fill, 0)
        wv = w_ref[...]
        bv = b_ref[...]

        def step(i, carry):
            st = pl.multiple_of(i * r, r)
            win = pad_ref[pl.ds(st, r + 2 * HALO), :]
            acc = bv + wv[0:1, :] * _shifted(win, HALO - 2, r)
            for t in range(1, D_CONV):
                acc = acc + wv[t:t + 1, :] * _shifted(win, HALO - 2 + t, r)
            o_ref[pl.ds(st, r), :] = (acc * _sigmoid(acc)).astype(o_ref.dtype)
            return carry

        lax.fori_loop(0, s // r, step, 0)

    return pl.pallas_call(
        body, name=name, grid=(CONV_W // cb,),
        in_specs=[pl.BlockSpec((s, cb), lambda j: (0, XS0 // cb + j)), pl.BlockSpec((8, cb), lambda j: (0, j)),
                  pl.BlockSpec((1, cb), lambda j: (0, j))],
        out_specs=pl.BlockSpec((s, cb), lambda j: (0, j)),
        out_shape=jax.ShapeDtypeStruct((s, CONV_W), MMD),
        scratch_shapes=[pltpu.VMEM((s + 2 * HALO, cb), F32)],
        compiler_params=_cp(("arbitrary",), VMEM_MID),
    )(proj, w8, brow)


def _conv_bwd(proj, col0, ga, gb, w8, brow, dproj, *, name):
    s = proj.shape[0]
    width = ga.shape[1]
    cb = 128
    c0 = col0 // cb
    r = _tile(s, 512)

    def body(x_ref, ga_ref, gb_ref, w_ref, b_ref, _, dx_ref, dw_ref, db_ref, xpad, dpad):
        zeros = jnp.zeros((HALO, cb), F32)
        for ref in (xpad, dpad):
            ref[0:HALO, :] = zeros
            ref[s + HALO:s + 2 * HALO, :] = zeros

        def fill(i, carry):
            st = pl.multiple_of(i * r, r)
            xpad[pl.ds(st + HALO, r), :] = x_ref[pl.ds(st, r), :].astype(F32)
            return carry

        lax.fori_loop(0, s // r, fill, 0)
        wv = w_ref[...]
        bv = b_ref[...]

        def first(i, carry):
            st = pl.multiple_of(i * r, r)
            win = xpad[pl.ds(st, r + 2 * HALO), :]
            taps = [_shifted(win, HALO - 2 + t, r) for t in range(D_CONV)]
            u = bv
            for t in range(D_CONV):
                u = u + wv[t:t + 1, :] * taps[t]
            sg = _sigmoid(u)
            du = ((ga_ref[pl.ds(st, r), :].astype(F32) + gb_ref[pl.ds(st, r), :].astype(F32))
                  * (sg * (1.0 + u * (1.0 - sg))))
            dpad[pl.ds(st + HALO, r), :] = du
            out = [carry[0] + jnp.sum(du, axis=0, keepdims=True)]
            for t in range(D_CONV):
                out.append(carry[1 + t] + jnp.sum(du * taps[t], axis=0, keepdims=True))
            return tuple(out)

        sums = lax.fori_loop(0, s // r, first, tuple(jnp.zeros((1, cb), F32) for _ in range(1 + D_CONV)))
        db_ref[...] = sums[0]
        for t in range(D_CONV):
            dw_ref[t:t + 1, :] = sums[1 + t]
        dw_ref[D_CONV:8, :] = jnp.zeros((8 - D_CONV, cb), F32)

        def second(i, carry):
            st = pl.multiple_of(i * r, r)
            win = dpad[pl.ds(st, r + 2 * HALO), :]
            acc = wv[0:1, :] * _shifted(win, HALO + 2, r)
            for t in range(1, D_CONV):
                acc = acc + wv[t:t + 1, :] * _shifted(win, HALO + 2 - t, r)
            dx_ref[pl.ds(st, r), :] = acc.astype(dx_ref.dtype)
            return carry

        lax.fori_loop(0, s // r, second, 0)

    col = pl.BlockSpec((s, cb), lambda j: (0, j))
    xcol = pl.BlockSpec((s, cb), lambda j: (0, XS0 // cb + c0 + j))
    return pl.pallas_call(
        body, name=name, grid=(width // cb,),
        in_specs=[xcol, col, col, pl.BlockSpec((8, cb), lambda j: (0, c0 + j)),
                  pl.BlockSpec((1, cb), lambda j: (0, c0 + j)), ANY],
        out_specs=[xcol, pl.BlockSpec((8, cb), lambda j: (0, j)), pl.BlockSpec((1, cb), lambda j: (0, j))],
        out_shape=[jax.ShapeDtypeStruct(dproj.shape, dproj.dtype), jax.ShapeDtypeStruct((8, width), F32),
                   jax.ShapeDtypeStruct((1, width), F32)],
        scratch_shapes=[pltpu.VMEM((s + 2 * HALO, cb), F32), pltpu.VMEM((s + 2 * HALO, cb), F32)],
        input_output_aliases={5: 0}, compiler_params=_cp(("arbitrary",), VMEM_BIG),
    )(proj, ga, gb, w8, brow, dproj)


def _tri(lower):
    i = jnp.arange(CHUNK)
    return ((i[:, None] >= i[None, :]) if lower else (i[:, None] <= i[None, :])).astype(F32)


def _dt_fwd(raw, bias, arow, *, name):
    s = raw.shape[0]

    def body(r_ref, b_ref, a_ref, lo_ref, up_ref, dt_ref, cs_ref):
        u = r_ref[...] + b_ref[...]
        dt = jnp.maximum(u, 0.0) + jnp.log1p(jnp.exp(-jnp.abs(u)))
        dt_ref[...] = dt
        a = dt * a_ref[...]
        lane = lax.broadcasted_iota(jnp.int32, (CHUNK, 128), 1)
        cs_ref[...] = jnp.where(lane < SSD_HEADS, _dot_hi(lo_ref[...], a), _dot_hi(up_ref[...], a))

    blk = pl.BlockSpec((CHUNK, 128), lambda i: (i, 0))
    row = pl.BlockSpec((1, 128), lambda i: (0, 0))
    tri = pl.BlockSpec((CHUNK, CHUNK), lambda i: (0, 0))
    return pl.pallas_call(
        body, name=name, grid=(s // CHUNK,), in_specs=[blk, row, row, tri, tri], out_specs=[blk, blk],
        out_shape=[jax.ShapeDtypeStruct((s, 128), F32)] * 2, compiler_params=_cp(("arbitrary",)),
    )(raw, bias, arow, _tri(True), _tri(False))


def _dt_bwd(ddt0, ddt1, raw, bias, dproj, *, name):
    s = raw.shape[0]
    tm = _tile(s, 1024)

    def body(d0_ref, d1_ref, r_ref, b_ref, _, o_ref, db_ref):
        g = (d0_ref[...] + d1_ref[...]) * _sigmoid(r_ref[...] + b_ref[...])
        o_ref[...] = g.astype(o_ref.dtype)
        _acc_rows(db_ref, jnp.sum(g, axis=0, keepdims=True), pl.program_id(0) == 0)

    blk = pl.BlockSpec((tm, 128), lambda i: (i, 0))
    row = pl.BlockSpec((1, 128), lambda i: (0, 0))
    return pl.pallas_call(
        body, name=name, grid=(s // tm,), in_specs=[blk, blk, blk, row, ANY],
        out_specs=[pl.BlockSpec((tm, 128), lambda i: (i, DT0 // 128)), row],
        out_shape=[jax.ShapeDtypeStruct(dproj.shape, dproj.dtype), jax.ShapeDtypeStruct((1, 128), F32)],
        input_output_aliases={4: 0}, compiler_params=_cp(("arbitrary",)),
    )(ddt0, ddt1, raw, bias, dproj)


GW = HPG * SSD_P


GPS = SSD_GROUPS


def _ssd_specs(nc, rev):
    cc = (lambda c: nc - 1 - c) if rev else (lambda c: c)
    return dict(
        x=pl.BlockSpec((CHUNK, GPS * GW), lambda g, c: (cc(c), g)),
        b=pl.BlockSpec((CHUNK, GPS * SSD_N), lambda g, c: (cc(c), 2048 // (GPS * SSD_N) + g)),
        c=pl.BlockSpec((CHUNK, GPS * SSD_N), lambda g, c: (cc(c), 2048 // (GPS * SSD_N) + 1 + g)),
        lanes=pl.BlockSpec((CHUNK, 128), lambda g, c: (cc(c), 0)),
        drow=pl.BlockSpec((1, GPS * GW), lambda g, c: (0, g)),
        y=pl.BlockSpec((CHUNK, GPS * GW), lambda g, c: (cc(c), g)),
        h=pl.BlockSpec((GPS, 1, SSD_N, GW), lambda g, c: (g, cc(c), 0, 0)),
        n=pl.BlockSpec((CHUNK, GPS * SSD_N), lambda g, c: (cc(c), g)),
    )


def _ssd_mask(anti):
    ii = lax.broadcasted_iota(jnp.int32, (CHUNK, CHUNK), 0)
    jj = lax.broadcasted_iota(jnp.int32, (CHUNK, CHUNK), 1)
    return ii, jj, (ii <= jj) if anti else (ii >= jj)


def _expand(x, ex, terms=3):
    h1 = x.astype(jnp.bfloat16)
    r1 = x - h1.astype(F32)
    h2 = r1.astype(jnp.bfloat16)
    out = _dot(h1, ex) + _dot(h2, ex)
    if terms == 3:
        out = out + _dot((r1 - h2.astype(F32)).astype(jnp.bfloat16), ex)
    return out


def _headsum(a, e):
    hi = a.astype(jnp.bfloat16)
    return _dot(hi, e) + _dot((a - hi.astype(F32)).astype(jnp.bfloat16), e)


def _expand_mats():
    lane = jnp.arange(128)[None, :, None]
    col = jnp.arange(GW)[None, None, :]
    base = (jnp.arange(2)[:, None] * SSD_HEADS + jnp.arange(SSD_GROUPS)[None, :] * HPG).reshape(2 * SSD_GROUPS, 1, 1)
    return (lane == base + col // SSD_P).astype(jnp.bfloat16)


def _headsum_mats():
    e1 = (jnp.arange(GW)[:, None] // SSD_P == jnp.arange(128)[None, :]).astype(jnp.bfloat16)
    e2 = (jnp.arange(HPG * CHUNK)[:, None] // CHUNK == jnp.arange(128)[None, :]).astype(jnp.bfloat16)
    return e1, e2


def _ssd_fwd(xc, dt, cs, ex, drow, di, *, name):
    s = xc.shape[0]
    nc = s // CHUNK
    anti = di == 1
    sp = _ssd_specs(nc, anti)
    trow = 0 if anti else CHUNK - 1

    def body(x_ref, b_ref, c_ref, dt_ref, cs_ref, ex_ref, d_ref, y_ref, hp_ref, h_ref):
        @pl.when(pl.program_id(1) == 0)
        def _():
            h_ref[...] = jnp.zeros_like(h_ref)

        mask = _ssd_mask(anti)[2]
        dtv, csv = dt_ref[...], cs_ref[...]
        cst = csv.T
        for gi in range(GPS):
            cols = slice(gi * GW, (gi + 1) * GW)
            ncols = slice(gi * SSD_N, (gi + 1) * SSD_N)
            ex = ex_ref[gi]
            xb = x_ref[:, cols].astype(F32)
            bm, cm = b_ref[:, ncols], c_ref[:, ncols]
            csr = cst[SSD_HEADS * di + HPG * gi:SSD_HEADS * di + HPG * (gi + 1)]
            dtf = _expand(dtv, ex, 2)
            csf = _expand(csv, ex)
            tl = csf[trow:trow + 1, :]
            h = h_ref[gi]
            hp_ref[gi, 0] = h.astype(hp_ref.dtype)
            g = _dot(cm, bm, NT)
            xs = xb * dtf
            xsm = xs.astype(MMD)
            base = jnp.exp(csf) * _dot(cm, h.astype(MMD)) + d_ref[:, cols] * xb
            for r in range(HPG):
                sl = slice(r * SSD_P, (r + 1) * SSD_P)
                lm = jnp.exp(jnp.where(mask, csf[:, r * SSD_P:r * SSD_P + 1] - csr[r:r + 1, :], NEG))
                y_ref[:, gi * GW + r * SSD_P:gi * GW + (r + 1) * SSD_P] = (
                    _dot((g * lm).astype(MMD), xsm[:, sl]) + base[:, sl]).astype(y_ref.dtype)
            xd = (xs * jnp.exp(tl - csf)).astype(MMD)
            h_ref[gi] = h * jnp.exp(tl) + _dot(bm, xd, TN)

    return pl.pallas_call(
        body, name=name, grid=(1, nc),
        in_specs=[sp["x"], sp["b"], sp["c"], sp["lanes"], sp["lanes"],
                  pl.BlockSpec((GPS, 128, GW), lambda g, c: (di, 0, 0)), sp["drow"]],
        out_specs=[sp["y"], sp["h"]],
        out_shape=[jax.ShapeDtypeStruct((s, 2048), MMD), jax.ShapeDtypeStruct((SSD_GROUPS, nc, SSD_N, GW), MMD)],
        scratch_shapes=[pltpu.VMEM((GPS, SSD_N, GW), F32)],
        compiler_params=_cp(("arbitrary", "arbitrary")),
    )(xc, xc, xc, dt, cs, ex, drow)


def _ssd_bwd(xc, dt, cs, ex, drow, arow, dy, hprev, di, *, name):
    s = xc.shape[0]
    nc = s // CHUNK
    anti = di == 1
    sp = _ssd_specs(nc, not anti)
    trow = 0 if anti else CHUNK - 1
    e1, e2 = _headsum_mats()

    def body(x_ref, b_ref, c_ref, dt_ref, cs_ref, ex_ref, d_ref, a_ref, dy_ref, hp_ref, tri_ref,
             e1_ref, e2_ref, dx_ref, db_ref, dc_ref, ddt_ref, da_ref, dh_ref, w_ref, dxs_ref):
        @pl.when(pl.program_id(1) == 0)
        def _():
            dh_ref[...] = jnp.zeros_like(dh_ref)
            da_ref[...] = jnp.zeros_like(da_ref)

        e1v = e1_ref[...]
        ii, _, mask = _ssd_mask(anti)
        dtv, csv = dt_ref[...], cs_ref[...]
        cst = csv.T
        ddt_acc = jnp.zeros((CHUNK, 128), F32)
        da_acc = jnp.zeros((1, 128), F32)
        for gi in range(GPS):
            lane0 = SSD_HEADS * di + HPG * gi
            cols = slice(gi * GW, (gi + 1) * GW)
            ncols = slice(gi * SSD_N, (gi + 1) * SSD_N)
            ex = ex_ref[gi]
            xb = x_ref[:, cols].astype(F32)
            bm, cm = b_ref[:, ncols], c_ref[:, ncols]
            csr = cst[lane0:lane0 + HPG]
            dym = dy_ref[:, cols]
            dyb = dym.astype(F32)
            hpm = hp_ref[gi, 0]
            hp = hpm.astype(F32)
            dh = dh_ref[gi]
            dhm = dh.astype(MMD)
            dtf = _expand(dtv, ex, 2)
            csf = _expand(csv, ex)
            tl = csf[trow:trow + 1, :]
            e = jnp.exp(csf)
            dec = jnp.exp(tl - csf)
            et = jnp.exp(tl)
            xs = xb * dtf
            xsm = xs.astype(MMD)
            g = _dot(cm, bm, NT)
            z = _dot(cm, hpm)
            bdh = _dot(bm, dhm)
            dg = jnp.zeros((CHUNK, CHUNK), F32)
            wcols = jnp.zeros((CHUNK, CHUNK), F32)
            for r in range(HPG):
                sl = slice(r * SSD_P, (r + 1) * SSD_P)
                lm = jnp.exp(jnp.where(mask, csf[:, r * SSD_P:r * SSD_P + 1] - csr[r:r + 1, :], NEG))
                mm = g * lm
                dm = _dot(dym[:, sl], xsm[:, sl], NT)
                w = dm * mm
                w_ref[gi, :, r * CHUNK:(r + 1) * CHUNK] = w
                wcols = jnp.where(ii == r, jnp.sum(w, axis=0, keepdims=True), wcols)
                dg = dg + dm * lm
                dxs_ref[gi, :, sl] = _dot(mm.astype(MMD), dym[:, sl], TN)
            dxs = dxs_ref[gi] + dec * bdh
            dx_ref[:, cols] = (dxs * dtf + d_ref[:, cols] * dyb).astype(dx_ref.dtype)
            tb = xs * bdh * dec
            d_tot = jnp.sum(tb, axis=0, keepdims=True) + et * jnp.sum(dh * hp, axis=0, keepdims=True)
            d_tot = _headsum(jnp.broadcast_to(d_tot, (8, GW)), e1v)[0:1]
            dcs = (_headsum(dyb * (e * z) - tb, e1v) + _headsum(w_ref[gi], e2_ref[...]) - wcols.T
                   + jnp.where(ii == trow, d_tot, 0.0))
            da = pltpu.roll(_dot_hi(tri_ref[...], dcs), lane0, 1)
            ddt_acc = ddt_acc + da * a_ref[...] + pltpu.roll(_headsum(dxs * xb, e1v), lane0, 1)
            da_acc = da_acc + jnp.sum(da * dtv, axis=0, keepdims=True)
            dgm = dg.astype(MMD)
            dz = (e * dyb).astype(MMD)
            dc_ref[:, ncols] = (_dot(dgm, bm) + _dot(dz, hpm, NT)).astype(dc_ref.dtype)
            db_ref[:, ncols] = (_dot(dgm, cm, TN) + _dot((xs * dec).astype(MMD), dhm, NT)).astype(db_ref.dtype)
            dh_ref[gi] = dh * et + _dot(cm, dz, TN)
        ddt_ref[...] = ddt_acc
        da_ref[...] += da_acc

    const = lambda shape: pl.BlockSpec(shape, lambda g, c: (0,) * len(shape))
    return pl.pallas_call(
        body, name=name, grid=(1, nc),
        in_specs=[sp["x"], sp["b"], sp["c"], sp["lanes"], sp["lanes"],
                  pl.BlockSpec((GPS, 128, GW), lambda g, c: (di, 0, 0)), sp["drow"],
                  const((1, 128)), sp["y"], sp["h"],
                  const((CHUNK, CHUNK)), const((GW, 128)), const((HPG * CHUNK, 128))],
        out_specs=[sp["y"], sp["n"], sp["n"], sp["lanes"], const((1, 128))],
        out_shape=[jax.ShapeDtypeStruct((s, 2048), MMD), jax.ShapeDtypeStruct((s, SSD_GROUPS * SSD_N), MMD),
                   jax.ShapeDtypeStruct((s, SSD_GROUPS * SSD_N), MMD), jax.ShapeDtypeStruct((s, 128), F32),
                   jax.ShapeDtypeStruct((1, 128), F32)],
        scratch_shapes=[pltpu.VMEM((GPS, SSD_N, GW), F32), pltpu.VMEM((GPS, CHUNK, HPG * CHUNK), F32),
                        pltpu.VMEM((GPS, CHUNK, GW), F32)],
        compiler_params=_cp(("arbitrary", "arbitrary")),
    )(xc, xc, xc, dt, cs, ex, drow, arow, dy, hprev, _tri(anti), e1, e2)


def _gnorm_fwd(ya, yb, proj, w, *, name):
    s = ya.shape[0]
    tm = _tile(s, 256)

    def body(a_ref, b_ref, z_ref, w_ref, o_ref):
        zv = z_ref[...].astype(F32)
        t = (a_ref[...].astype(F32) + b_ref[...].astype(F32)) * (zv * _sigmoid(zv))
        r = lax.rsqrt(jnp.mean(t * t, axis=-1, keepdims=True) + EPS)
        o_ref[...] = ((t * r) * w_ref[...]).astype(o_ref.dtype)

    big = pl.BlockSpec((tm, 2048), lambda i: (i, 0))
    row = pl.BlockSpec((1, 2048), lambda i: (0, 0))
    return pl.pallas_call(
        body, name=name, grid=(s // tm,), in_specs=[big, big, big, row], out_specs=big,
        out_shape=jax.ShapeDtypeStruct((s, 2048), MMD), compiler_params=_cp(("arbitrary",)),
    )(ya, yb, proj, w)


def _gnorm_bwd(dout, ya, yb, proj, xc, w, dproj, *, name):
    s = ya.shape[0]
    tm = _tile(s, 256)

    def body(do_ref, a_ref, b_ref, z_ref, x_ref, w_ref, _, dy_ref, dz_ref, dw_ref, dd_ref):
        zv = z_ref[...].astype(F32)
        sg = _sigmoid(zv)
        sz = zv * sg
        y = a_ref[...].astype(F32) + b_ref[...].astype(F32)
        t = y * sz
        r = lax.rsqrt(jnp.mean(t * t, axis=-1, keepdims=True) + EPS)
        nv = t * r
        dov = do_ref[...].astype(F32)
        _acc_rows(dw_ref, jnp.sum(dov * nv, axis=0, keepdims=True), pl.program_id(0) == 0)
        dn = dov * w_ref[...]
        dt_ = r * (dn - nv * jnp.mean(dn * nv, axis=-1, keepdims=True))
        dy = dt_ * sz
        dy_ref[...] = dy.astype(dy_ref.dtype)
        dz_ref[...] = (dt_ * y * (sg * (1.0 + zv * (1.0 - sg)))).astype(dz_ref.dtype)
        _acc_rows(dd_ref, jnp.sum(dy * x_ref[...].astype(F32), axis=0, keepdims=True), pl.program_id(0) == 0)

    big = pl.BlockSpec((tm, 2048), lambda i: (i, 0))
    row = pl.BlockSpec((1, 2048), lambda i: (0, 0))
    return pl.pallas_call(
        body, name=name, grid=(s // tm,), in_specs=[big, big, big, big, big, row, ANY], out_specs=[big, big, row, row],
        out_shape=[jax.ShapeDtypeStruct((s, 2048), MMD), jax.ShapeDtypeStruct(dproj.shape, dproj.dtype),
                   jax.ShapeDtypeStruct((1, 2048), F32), jax.ShapeDtypeStruct((1, 2048), F32)],
        input_output_aliases={6: 1}, compiler_params=_cp(("arbitrary",)),
    )(dout, ya, yb, proj, xc, w, dproj)


def _unheads(a):
    return a.transpose(1, 0, 2).reshape(a.shape[1], a.shape[0] * HEAD_DIM)


def _local_step(x, target, mod, wts, small, in_weights=None, late_weights=None, late_grads=None, in_grad=None,
                zero=0.0):
    s, d = x.shape
    shift1, scale1, gate1, shift2, scale2, gate2 = [mod[i:i + 1] for i in range(6)]

    h1 = _ln_mod(x, small["norm1_w"], scale1, shift1, name="ln1")
    qk_w = jnp.concatenate([jnp.tile(small["q_norm_w"], (1, N_Q_HEADS)), jnp.tile(small["k_norm_w"], (1, N_KV_HEADS))], axis=1)
    qk_sc = jnp.concatenate([jnp.full((1, N_Q_HEADS * HEAD_DIM), HEAD_DIM ** -0.5, F32),
                             jnp.ones((1, N_KV_HEADS * HEAD_DIM), F32)], axis=1)
    qk_sc2 = jnp.concatenate([jnp.full((1, N_Q_HEADS * HEAD_DIM), HEAD_DIM ** -0.5 * LOG2E, F32),
                              jnp.ones((1, N_KV_HEADS * HEAD_DIM), F32)], axis=1)
    tabs = _rope_tables(s, zero)
    if in_weights is not None:
        wts = {**wts, **in_weights([h1, *tabs])}
    proj = _mm(h1, wts["w_in_p"], name="in_proj", outs=[MMD], tm=512, tn=2944, b_outer=True)
    dt_raw = _mm(h1, wts["w_dt"], name="dt_proj", outs=[F32], tm=512, tn=128)
    qkt = _qk_fwd(proj, qk_w, qk_sc2, tabs, name="qk_fwd").reshape(N_Q_HEADS + N_KV_HEADS, HEAD_DIM, s)
    v_sd = proj[:, V0:V0 + N_KV_HEADS * HEAD_DIM]
    vta = jnp.concatenate([v_sd.T.reshape(N_KV_HEADS, HEAD_DIM, s), jnp.ones((N_KV_HEADS, V_AUG - HEAD_DIM, s), MMD)], axis=1)
    ot, lse = _flash_fwd(qkt, vta, name="flash_fwd")
    ot2 = ot.reshape(N_Q_HEADS * HEAD_DIM, s)
    if late_weights is not None:
        wts = {**wts, **late_weights(ot)}

    w8 = jnp.pad(small["conv_w"], ((0, 8 - D_CONV), (0, 0)))
    xc = _conv_fwd(proj, w8, small["conv_b"], name="conv_fwd")
    a_neg = -jnp.exp(small["A_log"])
    arow = jnp.pad(a_neg.reshape(1, 2 * SSD_HEADS), ((0, 0), (0, 128 - 2 * SSD_HEADS)))
    bias_row = jnp.pad(small["dt_bias"].reshape(1, 2 * SSD_HEADS), ((0, 0), (0, 128 - 2 * SSD_HEADS)))
    dt, cs = _dt_fwd(dt_raw, bias_row, arow, name="dt_fwd")
    drow = jnp.repeat(small["ssd_D"], SSD_P, axis=1)
    dirs = [dict(drow=drow), dict(drow=jnp.zeros_like(drow))]
    ex = _expand_mats()
    ys = []
    for di, dd in enumerate(dirs):
        y, dd["hprev"] = _ssd_fwd(xc, dt, cs, ex, dd["drow"], di, name=f"ssd_fwd{di}")
        ys.append(y)
    ssdn = _gnorm_fwd(ys[0], ys[1], proj, small["ssd_norm_w"], name="gnorm_fwd")

    a_o = _mm(ot2, wts["w_attn_out"], name="attn_out", outs=[MMD], ta=True, tm=512, tn=1024)

    def merge_epi(acc, ao, ga, gs):
        return (_sigmoid(ga.astype(F32)) * ao.astype(F32) + _sigmoid(gs.astype(F32)) * acc, acc)

    merged, b_o = _mm(ssdn, wts["w_ssd_out"], name="ssd_out", outs=[MMD, MMD], tm=512, tn=1024,
                      extras=[(a_o, "tile", 0), (proj, "tile", GA0), (proj, "tile", GS0)], epi=merge_epi)

    def res_epi(acc, res, gate):
        return (res + gate * acc, acc)

    x1, mo = _mm(merged, wts["w_o"], name="w_o", outs=[F32, MMD], tm=512, tn=1024,
                 extras=[(x, "tile", 0), (gate1, "row", 0)], epi=res_epi)
    h2 = _ln_mod(x1, small["norm2_w"], scale2, shift2, name="ln2")

    def relu2_epi(acc):
        rl = jnp.maximum(acc, 0.0)
        return (rl * rl, rl)

    act, rl = _mm(h2, wts["w_mlp1"], name="mlp1", outs=[MMD, MMD], tm=1024, tn=1024, epi=relu2_epi, b_outer=True)

    def loss_epi(acc, res, gate, tgt):
        return ((res + gate * acc - tgt) * (1.0 / d), acc)

    dy, ffo = _mm(act, wts["w_mlp2"], name="mlp2", outs=[F32, MMD], tm=512, tn=1024, vmem=VMEM_BIG,
                  extras=[(x1, "tile", 0), (gate2, "row", 0), (target, "tile", 0)], epi=loss_epi)
    loss = _sumsq(dy, name="loss") * (0.5 * d)

    gw = {}
    gs_ = {}
    dffo, dgate2 = _gate_bwd(dy, ffo, gate2, name="gate2_bwd")
    dpre = _mm(dffo, wts["w_mlp2"], name="mlp2_dx", outs=[MMD], nt=True, tm=1024, tn=1024, b_outer=True,
               extras=[(rl, "tile", 0)], epi=lambda acc, r: (acc * (2.0 * r.astype(F32)),))
    gw["w_mlp2"] = _mm_tn(act, dffo, name="mlp2_dw")
    dh2 = _mm(dpre, wts["w_mlp1"], name="mlp1_dx", outs=[F32], nt=True, tm=1024, tn=1024, vmem=VMEM_BIG)
    gw["w_mlp1"] = _mm_tn(h2, dpre, name="mlp1_dw")
    dx1, dshift2, dscale2, gs_["norm2_w"] = _ln_mod_bwd(dh2, x1, small["norm2_w"], scale2, dy, name="ln2_bwd")
    dmo, dgate1 = _gate_bwd(dx1, mo, gate1, name="gate1_bwd")

    def merge_bwd_epi(acc, ao, bo, ga, gs):
        sa, ss = _sigmoid(ga.astype(F32)), _sigmoid(gs.astype(F32))
        return (acc * sa, acc * ss, acc * ao.astype(F32) * sa * (1.0 - sa), acc * bo.astype(F32) * ss * (1.0 - ss))

    da_o, db_o, dga, dgs = _mm(dmo, wts["w_o"], name="w_o_dx", outs=[MMD] * 4, nt=True, tm=512, tn=1024,
                               extras=[(a_o, "tile", 0), (b_o, "tile", 0), (proj, "tile", GA0), (proj, "tile", GS0)],
                               epi=merge_bwd_epi)
    gw["w_o"] = _mm_tn(merged, dmo, name="w_o_dw")
    dot = _mm(wts["w_attn_out"], da_o, name="attn_out_dx", outs=[MMD], nt=True, tm=1024, tn=1024)
    gw["w_attn_out"] = _mm(ot2, da_o, name="attn_out_dw", outs=[F32], tm=256, tn=512, vmem=VMEM_BIG)
    dssdn = _mm(db_o, wts["w_ssd_out"], name="ssd_out_dx", outs=[MMD], nt=True, tm=512, tn=2048)
    gw["w_ssd_out"] = _mm_tn(ssdn, db_o, name="ssd_out_dw")

    dproj = lax.dynamic_update_slice(lax.empty((s, PW), MMD), jnp.concatenate([dga, dgs], axis=1), (0, GA0))

    norm_w = small["ssd_norm_w"] if late_grads is None else small["ssd_norm_w"] + late_grads(gw)
    dyssd, dproj, gs_["ssd_norm_w"], dd_row = _gnorm_bwd(dssdn, ys[0], ys[1], proj, xc, norm_w, dproj, name="gnorm_bwd")
    gs_["ssd_D"] = dd_row.reshape(SSD_HEADS, SSD_P).sum(axis=1).reshape(1, SSD_HEADS)
    dxc, ddts, das = [], [], []
    for di, dd in enumerate(dirs):
        dxs, dbm, dcm, ddt_d, da_d = _ssd_bwd(xc, dt, cs, ex, dd["drow"], arow, dyssd, dd["hprev"], di, name=f"ssd_bwd{di}")
        dxc.append((dxs, dbm, dcm))
        ddts.append(ddt_d)
        das.append(da_d)
    dw8, db, col0 = [], [], 0
    for part, (ga, gb) in enumerate(zip(*dxc)):
        dproj, dw_part, db_part = _conv_bwd(proj, col0, ga, gb, w8, small["conv_b"], dproj, name=f"conv_bwd{part}")
        dw8.append(dw_part)
        db.append(db_part)
        col0 += ga.shape[1]
    gs_["conv_w"] = jnp.concatenate(dw8, axis=1)[0:D_CONV]
    gs_["conv_b"] = jnp.concatenate(db, axis=1)
    gs_["A_log"] = (das[0] + das[1])[:, 0:2 * SSD_HEADS].reshape(2, SSD_HEADS) * a_neg
    dproj, dbias = _dt_bwd(ddts[0], ddts[1], dt_raw, bias_row, dproj, name="dt_bwd")
    gs_["dt_bias"] = dbias[:, 0:2 * SSD_HEADS].reshape(2, SSD_HEADS)

    dqt, dk_h, dv_h = _flash_bwd(qkt, vta, dot.reshape(N_Q_HEADS, HEAD_DIM, s), ot, lse, name="flash_bwd")
    dproj, dqk_w = _qk_bwd(dqt.reshape(N_Q_HEADS * HEAD_DIM, s), dk_h.transpose(0, 2, 1).reshape(N_KV_HEADS * HEAD_DIM, s),
                           proj, qk_w, qk_sc, tabs, dproj, name="qk_bwd")
    gs_["q_norm_w"] = dqk_w[:, 0:N_Q_HEADS * HEAD_DIM].reshape(N_Q_HEADS, HEAD_DIM).sum(axis=0, keepdims=True)
    gs_["k_norm_w"] = dqk_w[:, N_Q_HEADS * HEAD_DIM:].reshape(N_KV_HEADS, HEAD_DIM).sum(axis=0, keepdims=True)
    dproj = lax.dynamic_update_slice(dproj, _unheads(dv_h).astype(MMD), (0, V0))

    gw["w_in_p"] = _mm_tn(h1, dproj, name="in_proj_dw", tk=512, tn=2944, tmm=2048, vmem=VMEM_BIG)
    zero_row = jnp.zeros((1, d), F32) if in_grad is None else jnp.zeros((1, d), F32) + in_grad(gw["w_in_p"])[0:1, 0:1]
    dh1 = _mm(dproj, wts["w_in_p"], name="in_proj_dx", outs=[F32], nt=True, tm=256, tn=1024, vmem=VMEM_BIG,
              extras=[(zero_row, "row", 0)], epi=lambda acc, r: (acc + r,))
    grad_x, dshift1, dscale1, gs_["norm1_w"] = _ln_mod_bwd(dh1, x, small["norm1_w"], scale1, dx1, name="ln1_bwd")
    dmod = jnp.concatenate([dshift1, dscale1, dgate1, dshift2, dscale2, dgate2], axis=0)
    return loss, grad_x, dmod, gw, gs_


N_DEV = 8
N_CHIP = 4
ANY = pl.BlockSpec(memory_space=pl.ANY)


def _place():
    return lax.axis_index("x"), lax.axis_index("y"), lax.axis_index("c")


def _allgather8(v, *, name):
    m_per, n = v.shape

    def body(x_ref, out_ref, send_sems, recv_sems, local_sem):
        x, y, c = _place()
        me, sibling = (x, y, c), (x, y, 1 - c)
        chips = [(1 - x, y), (x, 1 - y), (1 - x, 1 - y)]

        def rows(px, py, pc):
            return out_ref.at[pl.ds((4 * px + 2 * py + pc) * m_per, m_per), :]

        def copy(k, block, to, src=None):
            return pltpu.make_async_remote_copy(
                src_ref=rows(*block) if src is None else src, dst_ref=rows(*block),
                send_sem=send_sems.at[k], recv_sem=recv_sems.at[k], device_id=to, device_id_type=MESH)

        mine = pltpu.make_async_copy(x_ref, rows(*me), local_sem)
        mine.start()
        first = [copy(0, me, sibling, src=x_ref)]
        first += [copy(1 + j, me, (*chip, c), src=x_ref) for j, chip in enumerate(chips)]
        for cp in first:
            cp.start()
        passed = [copy(4 + j, (*chip, c), sibling) for j, chip in enumerate(chips)]
        for j, chip in enumerate(chips):
            copy(1 + j, (*chip, c), me).wait_recv()
            passed[j].start()
        copy(0, sibling, me).wait_recv()
        for j, chip in enumerate(chips):
            copy(4 + j, (*chip, 1 - c), me).wait_recv()
        for cp in first + passed:
            cp.wait_send()
        mine.wait()

    return pl.pallas_call(
        body, name=name, out_shape=jax.ShapeDtypeStruct((N_DEV * m_per, n), v.dtype),
        in_specs=[pl.BlockSpec(memory_space=pltpu.VMEM)], out_specs=pl.BlockSpec(memory_space=pltpu.VMEM),
        scratch_shapes=[pltpu.SemaphoreType.DMA((7,)), pltpu.SemaphoreType.DMA((7,)), pltpu.SemaphoreType.DMA],
    )(v)


HBM = pl.BlockSpec(memory_space=pltpu.HBM)
SEM = pl.BlockSpec(memory_space=pltpu.SEMAPHORE)


def _chips_copies(x_ref, land_ref, sems, scatter, half=False):
    x, y, c = _place()
    k = 2 * x + y
    chips = [(1 - x, y), (x, 1 - y), (1 - x, 1 - y)]
    ids = [2 * cx + cy for cx, cy in chips]
    if half:
        hr = x_ref.shape[0] // 2
        rows = pl.ds(pl.multiple_of(c * hr, 16), hr)

    def copy(j, slot):
        src = x_ref.at[ids[j]] if scatter else (x_ref.at[rows] if half else x_ref)
        dst = land_ref.at[slot, rows] if half else land_ref.at[slot]
        return pltpu.make_async_remote_copy(src_ref=src, dst_ref=dst, send_sem=sems[j], recv_sem=sems[3 + j],
                                            device_id=(*chips[j], c), device_id_type=MESH)

    return [copy(j, k) for j in range(3)], [copy(j, ids[j]) for j in range(3)]


def _chips_start(src, scatter, half=False, *, name):
    shape = src.shape if scatter else (N_CHIP,) + tuple(src.shape)

    def body(x_ref, land_ref, *rest):
        sems, token = rest[0:6], rest[8]
        for cp in _chips_copies(x_ref, land_ref, sems, scatter, half)[0]:
            cp.start()
        token[...] = jnp.zeros_like(token)

    out = pl.pallas_call(
        body, name=name,
        out_shape=(pltpu.SemaphoreType.DMA(()),) * 6 + (pltpu.HBM(src.shape, src.dtype), pltpu.HBM(shape, src.dtype),
                                                       jax.ShapeDtypeStruct((8, 128), F32)),
        in_specs=(HBM, HBM), out_specs=(SEM,) * 6 + (HBM, HBM, pl.BlockSpec(memory_space=pltpu.VMEM)),
        input_output_aliases={0: 6, 1: 7},
        compiler_params=pltpu.CompilerParams(has_side_effects=pltpu.SideEffectType.DATAFLOW_SIDE_EFFECTING),
    )(pltpu.with_memory_space_constraint(src, pltpu.HBM),
      pltpu.with_memory_space_constraint(lax.empty(shape, src.dtype), pltpu.HBM))
    return out[0:6], out[6], out[7], out[8]


def _chips_wait(sems, src, land, after, scatter, half=False, *, name):
    after = list(after) if isinstance(after, (list, tuple)) else [after]

    def body(x_ref, land_ref, *rest):
        sems_ = rest[0:6]
        for cp in _chips_copies(x_ref, land_ref, sems_, scatter, half)[1]:
            cp.wait_send()
            cp.wait_recv()

    return pl.pallas_call(
        body, name=name, out_shape=(pltpu.HBM(src.shape, src.dtype), pltpu.HBM(land.shape, land.dtype)),
        in_specs=(HBM, HBM) + (SEM,) * 6 + (ANY,) * len(after), out_specs=(HBM, HBM), input_output_aliases={0: 0, 1: 1},
        compiler_params=pltpu.CompilerParams(has_side_effects=pltpu.SideEffectType.DATAFLOW_SIDE_EFFECTING),
    )(src, land, *sems, *after)


def _row_tile(r, pref=512):
    return max(t for t in range(16, pref + 1, 16) if r % t == 0)


def _pair_complete(land, *, name):
    r = land.shape[1]
    hr = r // 2
    assert r == 2 * hr and hr % 16 == 0

    def body(in_ref, out_ref, send_sems, recv_sems):
        x, y, c = _place()
        ids = [2 * cx + cy for cx, cy in [(1 - x, y), (x, 1 - y), (1 - x, 1 - y)]]
        mine_rows = pl.ds(pl.multiple_of(c * hr, 16), hr)
        other_rows = pl.ds(pl.multiple_of((1 - c) * hr, 16), hr)

        def copy(j, rows):
            return pltpu.make_async_remote_copy(
                src_ref=in_ref.at[ids[j], mine_rows], dst_ref=out_ref.at[ids[j], rows], send_sem=send_sems.at[j],
                recv_sem=recv_sems.at[j], device_id=(x, y, 1 - c), device_id_type=MESH)

        sends = [copy(j, mine_rows) for j in range(3)]
        for cp in sends:
            cp.start()
        for j in range(3):
            copy(j, other_rows).wait_recv()
        for cp in sends:
            cp.wait_send()

    return pl.pallas_call(
        body, name=name, out_shape=jax.ShapeDtypeStruct(land.shape, land.dtype), in_specs=[ANY], out_specs=ANY,
        input_output_aliases={0: 0},
        scratch_shapes=[pltpu.SemaphoreType.DMA((3,)), pltpu.SemaphoreType.DMA((3,))],
    )(land)


def _pair_swap(a, *, name):
    n, r, cols = a.shape
    hr = r // 2

    def body(x_ref, out_ref, send_sem, recv_sem):
        x, y, c = _place()
        other_rows = pl.ds(pl.multiple_of((1 - c) * hr, 16), hr)
        cp = pltpu.make_async_remote_copy(src_ref=x_ref.at[:, other_rows], dst_ref=out_ref, send_sem=send_sem,
                                          recv_sem=recv_sem, device_id=(x, y, 1 - c), device_id_type=MESH)
        cp.start()
        cp.wait()

    return pl.pallas_call(
        body, name=name, out_shape=jax.ShapeDtypeStruct((n, hr, cols), a.dtype), in_specs=[ANY], out_specs=ANY,
        scratch_shapes=[pltpu.SemaphoreType.DMA, pltpu.SemaphoreType.DMA],
    )(a)


def _sibling_copy(a, *, name):
    def body(x_ref, out_ref, send_sem, recv_sem):
        x, y, c = _place()
        cp = pltpu.make_async_remote_copy(src_ref=x_ref, dst_ref=out_ref, send_sem=send_sem, recv_sem=recv_sem,
                                          device_id=(x, y, 1 - c), device_id_type=MESH)
        cp.start()
        cp.wait()

    return pl.pallas_call(
        body, name=name, out_shape=jax.ShapeDtypeStruct(a.shape, a.dtype), in_specs=[ANY], out_specs=ANY,
        scratch_shapes=[pltpu.SemaphoreType.DMA, pltpu.SemaphoreType.DMA],
    )(a)


def _sum_slots(a, own, *, name):
    _, r, c = a.shape
    tr = _row_tile(r, 256)

    def body(a_ref, own_ref, o_ref):
        k = 2 * lax.axis_index("x") + lax.axis_index("y")
        acc = None
        for j in range(N_CHIP):
            term = jnp.where(k == j, own_ref[j], a_ref[j]).astype(F32)
            acc = term if acc is None else acc + term
        o_ref[...] = acc

    spec = pl.BlockSpec((N_CHIP, tr, c), lambda i: (0, i, 0))
    return pl.pallas_call(
        body, name=name, grid=(r // tr,), in_specs=[spec, spec],
        out_specs=pl.BlockSpec((tr, c), lambda i: (i, 0)), out_shape=jax.ShapeDtypeStruct((r, c), F32),
        compiler_params=_cp(("arbitrary",)),
    )(a, own)


def _add2(a, b, *, name):
    r, c = a.shape
    tr = _row_tile(r)

    def body(a_ref, b_ref, o_ref):
        o_ref[...] = (a_ref[...].astype(F32) + b_ref[...].astype(F32)).astype(o_ref.dtype)

    spec = pl.BlockSpec((tr, c), lambda i: (i, 0))
    return pl.pallas_call(
        body, name=name, grid=(r // tr,), in_specs=[spec, spec], out_specs=spec,
        out_shape=jax.ShapeDtypeStruct((r, c), a.dtype), compiler_params=_cp(("arbitrary",)),
    )(a, b)


BIG = ("w_in", "w_mlp1", "w_attn_out", "w_ssd_out", "w_o", "w_mlp2")
COL_SHARDED = ("w_mlp1", "w_in")
ROW_SHARDED = ("w_attn_out", "w_ssd_out", "w_o", "w_mlp2")
LATE = ROW_SHARDED + ("w_mlp1",)
SMALL = ("b_ada", "norm1_w", "norm2_w", "q_norm_w", "k_norm_w", "conv_b", "A_log", "dt_bias", "ssd_D", "ssd_norm_w")
NAMES = ("w_ada", "b_ada", "norm1_w", "norm2_w", "w_in", "q_norm_w", "k_norm_w", "conv_w", "conv_b", "A_log", "dt_bias",
         "ssd_D", "ssd_norm_w", "w_attn_out", "w_ssd_out", "w_o", "w_mlp1", "w_mlp2")
W_IN_COLS = 8768


def _permute_in(w):
    return jnp.concatenate([w[:, 4608:6656], w[:, 6720:8768], w[:, 1536:4608], w[:, 0:1536], w[:, 6656:6720],
                            jnp.zeros((w.shape[0], PW - W_IN_COLS), w.dtype)], axis=1)


def _unpermute_in(wp):
    return jnp.concatenate([wp[:, Q0:DT0], wp[:, XS0:Q0], wp[:, Z0:GA0], wp[:, DT0:DT0 + 64], wp[:, GA0:XS0]], axis=1)


def _pad_to(v, n):
    return jnp.pad(v, (0, n - v.shape[0]))


def _step(w, m, v, loss_target):
    xi, yi, ci = _place()
    chip = 2 * xi + yi
    dev = 4 * xi + 2 * yi + ci
    x, tgt = w["x"], loss_target
    d = x.shape[1]

    cw = w["conv_w"].shape[1]
    v0 = _pad_to(jnp.concatenate([w["c"].reshape(-1), w["conv_w"].reshape(-1)]), 5120).reshape(8, 640)
    g0 = _allgather8(v0, name="ag_cond").reshape(N_DEV, 5120)
    c_all = g0[:, 0:d]
    conv_w = jnp.concatenate([g0[2 * k, d:d + D_CONV * cw].reshape(D_CONV, cw) for k in range(N_CHIP)], axis=1)
    sc = _silu_cast(c_all, name="silu_c")
    modp = _mm(sc, w["w_ada"].astype(MMD), name="ada_fwd", outs=[F32], tm=8, tn=512)
    g1 = _allgather8(modp, name="ag_mod").reshape(N_DEV, N_DEV, modp.shape[1])
    mod_all = jnp.concatenate([g1[2 * k] for k in range(N_CHIP)], axis=1)
    mod = (lax.dynamic_slice_in_dim(mod_all, dev, 1, axis=0) + w["b_ada"]).reshape(6, d)

    mine, mod = lax.optimization_barrier((w["w_in"].astype(MMD), mod))
    in_sems, in_src, in_land, in_token = _chips_start(mine, False, True, name="ag_w_in_start")
    mod = mod + in_token[0:1, 0:1]
    small = {n: w[n] for n in SMALL if n != "b_ada"}
    small["conv_w"] = conv_w
    started = {}

    late_mine = jnp.concatenate([w[n].astype(MMD) for n in LATE], axis=0) + in_token[0:1, 0:1].astype(MMD)

    def in_weights(after):
        src, land = _chips_wait(in_sems, in_src, in_land, [*after, late_mine], False, True, name="ag_w_in_wait")
        land = _pair_complete(land, name="ag_w_in_pair")
        late, land = lax.optimization_barrier((late_mine, land))
        sems, late_src, late_land, token = _chips_start(late, False, name="ag_late_start")
        started["ag_late"] = (sems, late_src, late_land)
        w_in = jnp.concatenate([jnp.where(chip == k, src, land[k]) for k in range(N_CHIP)], axis=1)
        w_dt = jnp.pad(w_in[:, 6656:6720], ((0, 0), (0, 64))) + token[0:1, 0:1].astype(MMD)
        return {"w_in_p": _permute_in(w_in), "w_dt": w_dt}

    def late_weights(after):
        src, land = _chips_wait(*started["ag_late"], after, False, name="ag_late_wait")
        out, o = {}, 0
        for n in LATE:
            rows = w[n].shape[0]
            parts = [jnp.where(chip == k, src[o:o + rows], land[k, o:o + rows]) for k in range(N_CHIP)]
            out[n] = jnp.concatenate(parts, axis=1 if n in COL_SHARDED else 0)
            o += rows
        return out

    def pair_sums(slots, tag):
        _, rows, cols = slots.shape
        hr = rows // 2
        theirs = _pair_swap(slots, name="rs_pair_" + tag)
        ours = lax.dynamic_slice_in_dim(slots, ci * hr, hr, axis=1)
        pair = _add2(ours.reshape(N_CHIP * hr, cols), theirs.reshape(N_CHIP * hr, cols), name="rs_pair_sum_" + tag)
        return pair.reshape(N_CHIP, hr, cols)

    def finish(recv, pair, tag):
        half = _sum_slots(recv, pair, name="rs_sum_" + tag)
        other = _sibling_copy(half, name="rs_sibling_" + tag)
        return jnp.where(ci == 0, jnp.concatenate([half, other], axis=0), jnp.concatenate([other, half], axis=0))

    def late_grads(gw):
        slots = []
        for k in range(N_CHIP):
            parts = []
            for n in LATE:
                rows = w[n].shape[0]
                blk = gw[n][:, k * rows:(k + 1) * rows] if n in COL_SHARDED else gw[n][k * rows:(k + 1) * rows]
                parts.append(blk.astype(MMD))
            slots.append(jnp.concatenate(parts, axis=0))
        pair = pair_sums(jnp.stack(slots), "late")
        sems, src, land, token = _chips_start(pair, True, name="rs_late_start")
        started["late"] = (sems, src, land)
        return token[0:1, 0:1]

    def in_grad(g):
        g_in = _unpermute_in(g)
        cols_in = w["w_in"].shape[1]
        pair = pair_sums(jnp.stack([g_in[:, k * cols_in:(k + 1) * cols_in].astype(MMD) for k in range(N_CHIP)]), "w_in")
        sems, src, land, token = _chips_start(pair, True, name="rs_w_in_start")
        started["w_in"] = (sems, src, land)
        return token

    loss, grad_x, dmod, gw, gs = _local_step(x, tgt, mod, {}, small, in_weights, late_weights, late_grads, in_grad,
                                             in_token[0, 0])

    grads = {}
    pair, land = _chips_wait(*started["w_in"], grad_x, True, name="rs_w_in_wait")
    grads["w_in"] = finish(land, pair, "w_in")
    pair, land = _chips_wait(*started["late"], grad_x, True, name="rs_late_wait")
    total, o = finish(land, pair, "late"), 0
    for n in LATE:
        rows = w[n].shape[0]
        grads[n] = total[o:o + rows]
        o += rows

    order = ([dmod.reshape(-1)] + [gs[n].reshape(-1) for n in SMALL if n != "b_ada"] + [gs["conv_w"].reshape(-1)]
             + [loss.reshape(-1)])
    vec = jnp.concatenate(order)
    n_small = vec.shape[0]
    n_pad = -(-n_small // 1024) * 1024
    g2 = _allgather8(_pad_to(vec, n_pad).reshape(8, n_pad // 8), name="ag_small")
    tot = _rows_sum(g2, N_DEV, name="small_sum").reshape(-1)
    loss = tot[n_small - 1]
    dmod_all = g2.reshape(N_DEV, n_pad)[:, 0:6 * d]
    off = 0
    for n in SMALL:
        grads[n] = tot[off:off + w[n].size].reshape(w[n].shape)
        off += w[n].size
    conv_full = tot[off:off + D_CONV * N_CHIP * cw].reshape(D_CONV, N_CHIP * cw)
    grads["conv_w"] = lax.dynamic_slice_in_dim(conv_full, chip * cw, cw, axis=1)
    ada_cols = w["w_ada"].shape[1]
    dmod_mine = lax.dynamic_slice_in_dim(dmod_all, chip * ada_cols, ada_cols, axis=1).astype(MMD)
    grads["w_ada"] = _mm_tn(sc, dmod_mine, name="ada_dw", tk=512, tn=512, tmm=8)

    delta, new_m, new_v = {}, {}, {}
    pack = lambda t: jnp.concatenate([t[n].reshape(-1) for n in SMALL]).reshape(1, -1)
    ds_, ms_, vs_ = _adamw(pack(w), pack(grads), pack(m), pack(v), name="adamw_small")
    off = 0
    for n in SMALL:
        for dst, src in ((delta, ds_), (new_m, ms_), (new_v, vs_)):
            dst[n] = src[0, off:off + w[n].size].reshape(w[n].shape)
        off += w[n].size
    for n in ("w_ada", "conv_w") + BIG:
        delta[n], new_m[n], new_v[n] = _adamw(w[n], grads[n], m[n], v[n], name="adamw_" + n)
    return loss, grad_x, grads, delta, new_m, new_v


def kernel(x, c, w_ada, b_ada, norm1_w, norm2_w, w_in, q_norm_w, k_norm_w, conv_w, conv_b, A_log, dt_bias, ssd_D, ssd_norm_w, w_attn_out, w_ssd_out, w_o, w_mlp1, w_mlp2, loss_target, m_w_ada, m_b_ada, m_norm1_w, m_norm2_w, m_w_in, m_q_norm_w, m_k_norm_w, m_conv_w, m_conv_b, m_A_log, m_dt_bias, m_ssd_D, m_ssd_norm_w, m_w_attn_out, m_w_ssd_out, m_w_o, m_w_mlp1, m_w_mlp2, v_w_ada, v_b_ada, v_norm1_w, v_norm2_w, v_w_in, v_q_norm_w, v_k_norm_w, v_conv_w, v_conv_b, v_A_log, v_dt_bias, v_ssd_D, v_ssd_norm_w, v_w_attn_out, v_w_ssd_out, v_w_o, v_w_mlp1, v_w_mlp2):
    args = dict(locals())
    strip = lambda a: a[0] if a.ndim == 3 else a
    w = {n: strip(args[n]) for n in NAMES + ("x", "c")}
    m = {n: strip(args["m_" + n]) for n in NAMES}
    v = {n: strip(args["v_" + n]) for n in NAMES}
    loss, grad_x, grads, delta, new_m, new_v = _step(w, m, v, loss_target[0])
    like = lambda t, n: t.reshape(args[n].shape)
    return (loss, grad_x[None], *[like(grads[n], n) for n in NAMES], *[like(delta[n], n) for n in NAMES],
            *[like(new_m[n], n) for n in NAMES], *[like(new_v[n], n) for n in NAMES])
```

```python
import math

import jax
import jax.numpy as jnp
from jax import lax
from jax.experimental import pallas as pl
from jax.experimental.pallas import tpu as pltpu

F32 = jnp.float32
MMD = jnp.bfloat16
EPS = 1e-6
NEG = -1e30
MIB = 1024 * 1024
VMEM_BIG = 56 * MIB
VMEM_MID = 40 * MIB

GRID_W = 64
N_Q_HEADS, N_KV_HEADS, HEAD_DIM = 16, 4, 64
ROPE_THETA = 10000.0
SSD_HEADS, SSD_GROUPS, SSD_P, SSD_N, CHUNK = 32, 4, 64, 128, 128
HPG = SSD_HEADS // SSD_GROUPS
D_CONV = 5
ADAM_LR, ADAM_B1, ADAM_B2, ADAM_EPS, ADAM_WD, ADAM_STEP = 0.001, 0.9, 0.999, 1e-08, 0.01, 10

Z0, GA0, GS0, XS0, B0, C0, Q0, K0, V0, DT0, PW = 0, 2048, 3072, 4096, 6144, 6656, 7168, 8192, 8448, 8704, 8832

MESH = pl.DeviceIdType.MESH
NT = (((1,), (1,)), ((), ()))
TN = (((0,), (0,)), ((), ()))


def _cp(sem=None, vmem=VMEM_MID):
    return pltpu.CompilerParams(dimension_semantics=sem, vmem_limit_bytes=vmem)


def _tile(n, pref):
    t = min(n, pref)
    while n % t:
        t //= 2
    return t


def _dot(a, b, dims=None):
    if dims is None:
        return jnp.dot(a, b, preferred_element_type=F32)
    return lax.dot_general(a, b, dims, preferred_element_type=F32)


def _dot_hi(a01, b):
    a = a01.astype(jnp.bfloat16)
    h1 = b.astype(jnp.bfloat16)
    r1 = b - h1.astype(F32)
    h2 = r1.astype(jnp.bfloat16)
    return _dot(a, h1) + _dot(a, h2) + _dot(a, (r1 - h2.astype(F32)).astype(jnp.bfloat16))


def _sigmoid(x):
    return jax.nn.sigmoid(x)


def _mm(a, b, *, name, outs, nt=False, ta=False, extras=(), epi=None, tm=512, tn=512, n=None, b_outer=False,
        vmem=VMEM_MID):
    assert not (nt and ta)
    k, m = a.shape if ta else a.shape[::-1]
    if n is None:
        n = b.shape[0] if nt else b.shape[1]
    tm, tn = _tile(m, tm), _tile(n, tn)
    gi, gj = m // tm, n // tn
    if b_outer:
        grid = (gj, gi)
        ij = lambda p, q: (q, p)
    else:
        grid = (gi, gj)
        ij = lambda p, q: (p, q)
    if ta:
        a_spec = pl.BlockSpec((k, tm), lambda p, q: (0, ij(p, q)[0]))
    else:
        a_spec = pl.BlockSpec((tm, k), lambda p, q: (ij(p, q)[0], 0))
    if nt:
        b_spec = pl.BlockSpec((tn, k), lambda p, q: (ij(p, q)[1], 0))
    else:
        b_spec = pl.BlockSpec((k, tn), lambda p, q: (0, ij(p, q)[1]))
    e_specs = []
    for arr, kind, off in extras:
        ob = off // tn
        assert off % tn == 0
        if kind == "tile":
            e_specs.append(pl.BlockSpec((tm, tn), lambda p, q, ob=ob: (ij(p, q)[0], ob + ij(p, q)[1])))
        else:
            e_specs.append(pl.BlockSpec((1, tn), lambda p, q, ob=ob: (0, ob + ij(p, q)[1])))
    ne = len(extras)

    def body(a_ref, b_ref, *rest):
        acc = _dot(a_ref[...], b_ref[...], NT if nt else (TN if ta else None))
        res = epi(acc, *[e[...] for e in rest[:ne]]) if epi is not None else (acc,)
        for o_ref, r in zip(rest[ne:], res):
            o_ref[...] = r.astype(o_ref.dtype)

    out = pl.pallas_call(
        body, name=name, grid=grid,
        in_specs=[a_spec, b_spec] + e_specs,
        out_specs=[pl.BlockSpec((tm, tn), lambda p, q: ij(p, q)) for _ in outs],
        out_shape=[jax.ShapeDtypeStruct((m, n), dt) for dt in outs],
        compiler_params=_cp(("arbitrary", "arbitrary"), vmem),
    )(a, b, *[e[0] for e in extras])
    return out if len(outs) > 1 else out[0]


def _mm_tn(a, g, *, name, tk=512, tn=1024, tmm=4096, vmem=VMEM_MID):
    m, k = a.shape
    n = g.shape[1]
    tk, tn, tmm = _tile(k, tk), _tile(n, tn), _tile(m, tmm)

    def body(a_ref, g_ref, o_ref):
        p = _dot(a_ref[...], g_ref[...], TN)

        @pl.when(pl.program_id(2) == 0)
        def _():
            o_ref[...] = p

        @pl.when(pl.program_id(2) > 0)
        def _():
            o_ref[...] += p

    return pl.pallas_call(
        body, name=name, grid=(k // tk, n // tn, m // tmm),
        in_specs=[pl.BlockSpec((tmm, tk), lambda i, j, r: (r, i)), pl.BlockSpec((tmm, tn), lambda i, j, r: (r, j))],
        out_specs=pl.BlockSpec((tk, tn), lambda i, j, r: (i, j)),
        out_shape=jax.ShapeDtypeStruct((k, n), F32),
        compiler_params=_cp(("arbitrary", "arbitrary", "arbitrary"), vmem),
    )(a, g)


def _adamw(w, g, m, v, *, name):
    r, c = w.shape
    tr = _tile(r, 256) if r % 8 == 0 else r

    def body(w_ref, g_ref, m_ref, v_ref, d_ref, nm_ref, nv_ref):
        gg = g_ref[...]
        nm = ADAM_B1 * m_ref[...] + (1.0 - ADAM_B1) * gg
        nv = ADAM_B2 * v_ref[...] + (1.0 - ADAM_B2) * jnp.square(gg)
        m_hat = nm / (1.0 - ADAM_B1 ** ADAM_STEP)
        v_hat = nv / (1.0 - ADAM_B2 ** ADAM_STEP)
        d_ref[...] = -ADAM_LR * (m_hat / (jnp.sqrt(v_hat) + ADAM_EPS) + ADAM_WD * w_ref[...])
        nm_ref[...] = nm
        nv_ref[...] = nv

    spec = pl.BlockSpec((tr, c), lambda i: (i, 0))
    return pl.pallas_call(
        body, name=name, grid=(r // tr,), in_specs=[spec] * 4, out_specs=[spec] * 3,
        out_shape=[jax.ShapeDtypeStruct((r, c), F32)] * 3, compiler_params=_cp(("arbitrary",)),
    )(w, g, m, v)


def _rows_sum(a, groups, *, name):
    r = a.shape[0] // groups

    def body(a_ref, o_ref):
        acc = a_ref[0:r, :]
        for d in range(1, groups):
            acc = acc + a_ref[d * r:(d + 1) * r, :]
        o_ref[...] = acc

    return pl.pallas_call(body, name=name, out_shape=jax.ShapeDtypeStruct((r, a.shape[1]), F32))(a)


def _silu_cast(a, *, name):
    def body(a_ref, o_ref):
        x = a_ref[...]
        o_ref[...] = (x * _sigmoid(x)).astype(o_ref.dtype)

    return pl.pallas_call(body, name=name, out_shape=jax.ShapeDtypeStruct(a.shape, MMD))(a)


def _sumsq(a, *, name):
    m, n = a.shape
    tm = _tile(m, 512)

    def body(a_ref, o_ref):
        x = a_ref[...]
        p = jnp.sum(jnp.sum(x * x, axis=1, keepdims=True), axis=0, keepdims=True)

        @pl.when(pl.program_id(0) == 0)
        def _():
            o_ref[...] = p

        @pl.when(pl.program_id(0) > 0)
        def _():
            o_ref[...] += p

    return pl.pallas_call(
        body, name=name, grid=(m // tm,), in_specs=[pl.BlockSpec((tm, n), lambda i: (i, 0))],
        out_specs=pl.BlockSpec((1, 1), lambda i: (0, 0)), out_shape=jax.ShapeDtypeStruct((1, 1), F32),
        compiler_params=_cp(("arbitrary",)),
    )(a)


def _acc_rows(o_ref, p, first):
    @pl.when(first)
    def _():
        o_ref[...] = p

    @pl.when(jnp.logical_not(first))
    def _():
        o_ref[...] += p


def _ln_mod(x, w, scale, shift, *, name):
    s, d = x.shape
    tm = _tile(s, 512)

    def body(x_ref, w_ref, sc_ref, sh_ref, o_ref):
        xv = x_ref[...]
        r = lax.rsqrt(jnp.mean(xv * xv, axis=-1, keepdims=True) + EPS)
        o_ref[...] = ((xv * r) * w_ref[...] * (1.0 + sc_ref[...]) + sh_ref[...]).astype(o_ref.dtype)

    row = pl.BlockSpec((1, d), lambda i: (0, 0))
    big = pl.BlockSpec((tm, d), lambda i: (i, 0))
    return pl.pallas_call(
        body, name=name, grid=(s // tm,), in_specs=[big, row, row, row], out_specs=big,
        out_shape=jax.ShapeDtypeStruct((s, d), MMD), compiler_params=_cp(("arbitrary",)),
    )(x, w, scale, shift)


def _ln_mod_bwd(dh, x, w, scale, dres, *, name):
    s, d = x.shape
    tm = _tile(s, 512)

    def body(dh_ref, x_ref, w_ref, sc_ref, dres_ref, dx_ref, dsh_ref, dsc_ref, dw_ref):
        xv = x_ref[...]
        dhv = dh_ref[...].astype(F32)
        r = lax.rsqrt(jnp.mean(xv * xv, axis=-1, keepdims=True) + EPS)
        nv = xv * r
        wv = w_ref[...]
        g1 = 1.0 + sc_ref[...]
        dn = dhv * (wv * g1)
        dx_ref[...] = dres_ref[...] + r * (dn - nv * jnp.mean(dn * nv, axis=-1, keepdims=True))
        first = pl.program_id(0) == 0
        _acc_rows(dsh_ref, jnp.sum(dhv, axis=0, keepdims=True), first)
        _acc_rows(dsc_ref, jnp.sum(dhv * nv * wv, axis=0, keepdims=True), first)
        _acc_rows(dw_ref, jnp.sum(dhv * nv * g1, axis=0, keepdims=True), first)

    row = pl.BlockSpec((1, d), lambda i: (0, 0))
    big = pl.BlockSpec((tm, d), lambda i: (i, 0))
    return pl.pallas_call(
        body, name=name, grid=(s // tm,), in_specs=[big, big, row, row, big], out_specs=[big, row, row, row],
        out_shape=[jax.ShapeDtypeStruct((s, d), F32)] + [jax.ShapeDtypeStruct((1, d), F32)] * 3,
        compiler_params=_cp(("arbitrary",)),
    )(dh, x, w, scale, dres)


def _gate_bwd(dy, u, gate, *, name):
    s, d = dy.shape
    tm = _tile(s, 1024)

    def body(dy_ref, u_ref, g_ref, du_ref, dg_ref):
        dyv = dy_ref[...]
        du_ref[...] = (dyv * g_ref[...]).astype(du_ref.dtype)
        _acc_rows(dg_ref, jnp.sum(dyv * u_ref[...].astype(F32), axis=0, keepdims=True), pl.program_id(0) == 0)

    row = pl.BlockSpec((1, d), lambda i: (0, 0))
    big = pl.BlockSpec((tm, d), lambda i: (i, 0))
    return pl.pallas_call(
        body, name=name, grid=(s // tm,), in_specs=[big, big, row], out_specs=[big, row],
        out_shape=[jax.ShapeDtypeStruct((s, d), MMD), jax.ShapeDtypeStruct((1, d), F32)],
        compiler_params=_cp(("arbitrary",)),
    )(dy, u, gate)


def _seg64(v, e):
    hi = v.astype(jnp.bfloat16)
    lo = (v - hi.astype(F32)).astype(jnp.bfloat16)
    return _dot(hi, e) + _dot(lo, e)


def _rope_tables(s, zero=0.0):
    rows = s // GRID_W
    pos_row = jnp.repeat(jnp.arange(rows, dtype=jnp.int32), GRID_W).astype(F32) + zero
    pos_col = jnp.tile(jnp.arange(GRID_W, dtype=jnp.int32), rows).astype(F32) + zero
    axis_dim = HEAD_DIM // 2
    inv_freq = ROPE_THETA ** (-jnp.arange(0, axis_dim, 2, dtype=F32) / axis_dim)
    ang_r = pos_row[:, None] * inv_freq[None, :]
    ang_c = pos_col[:, None] * inv_freq[None, :]
    zero = jnp.zeros_like(ang_r)
    cos = jnp.concatenate([jnp.cos(ang_r), jnp.cos(ang_r), jnp.cos(ang_c), jnp.cos(ang_c)], axis=1)
    s_a = jnp.concatenate([-jnp.sin(ang_r), zero, -jnp.sin(ang_c), zero], axis=1)
    s_b = jnp.concatenate([zero, jnp.sin(ang_r), zero, jnp.sin(ang_c)], axis=1)
    return [jnp.tile(t, (1, 2)) for t in (cos, s_a, s_b)]


def _e128():
    i = jnp.arange(128)
    return (i[:, None] // 64 == i[None, :] // 64).astype(jnp.bfloat16)


QKW = N_Q_HEADS * HEAD_DIM + N_KV_HEADS * HEAD_DIM


def _qk_fwd(proj, wrow, scrow, tabs, *, name):
    s = proj.shape[0]
    tm = _tile(s, 1024)

    def body(x_ref, w_ref, sc_ref, cos_ref, sa_ref, sb_ref, e_ref, ot_ref):
        u = x_ref[...].astype(F32)
        r = lax.rsqrt(_seg64(u * u, e_ref[...]) * (1.0 / HEAD_DIM) + EPS)
        nv = (u * r) * w_ref[...]
        ro = nv * cos_ref[...] + pltpu.roll(nv, 112, 1) * sa_ref[...] + pltpu.roll(nv, 16, 1) * sb_ref[...]
        ot_ref[...] = (ro * sc_ref[...]).T.astype(ot_ref.dtype)

    tab = pl.BlockSpec((tm, 128), lambda i, j: (i, 0))
    row = pl.BlockSpec((1, 128), lambda i, j: (0, j))
    return pl.pallas_call(
        body, name=name, grid=(s // tm, QKW // 128),
        in_specs=[pl.BlockSpec((tm, 128), lambda i, j: (i, Q0 // 128 + j)), row, row, tab, tab, tab,
                  pl.BlockSpec((128, 128), lambda i, j: (0, 0))],
        out_specs=pl.BlockSpec((128, tm), lambda i, j: (j, i)),
        out_shape=jax.ShapeDtypeStruct((QKW, s), MMD), compiler_params=_cp(("arbitrary", "arbitrary")),
    )(proj, wrow, scrow, *tabs, _e128())


def _qk_bwd(dqt, dkt, proj, wrow, scrow, tabs, dproj, *, name):
    s = proj.shape[0]
    tm = _tile(s, 1024)
    nq = dqt.shape[0] // 128

    def body(dq_ref, dk_ref, x_ref, w_ref, sc_ref, cos_ref, sa_ref, sb_ref, e_ref, _, du_ref, dw_ref):
        e = e_ref[...]
        d = jnp.where(pl.program_id(0) < nq, dq_ref[...], dk_ref[...]).T * sc_ref[...]
        dn = d * cos_ref[...] + pltpu.roll(d * sa_ref[...], 16, 1) + pltpu.roll(d * sb_ref[...], 112, 1)
        u = x_ref[...].astype(F32)
        r = lax.rsqrt(_seg64(u * u, e) * (1.0 / HEAD_DIM) + EPS)
        uh = u * r
        _acc_rows(dw_ref, jnp.sum(dn * uh, axis=0, keepdims=True), pl.program_id(1) == 0)
        dnw = dn * w_ref[...]
        du_ref[...] = (r * (dnw - uh * (_seg64(dnw * uh, e) * (1.0 / HEAD_DIM)))).astype(du_ref.dtype)

    tab = pl.BlockSpec((tm, 128), lambda j, i: (i, 0))
    row = pl.BlockSpec((1, 128), lambda j, i: (0, j))
    qcol = pl.BlockSpec((tm, 128), lambda j, i: (i, Q0 // 128 + j))
    return pl.pallas_call(
        body, name=name, grid=(QKW // 128, s // tm),
        in_specs=[pl.BlockSpec((128, tm), lambda j, i: (jnp.minimum(j, nq - 1), i)),
                  pl.BlockSpec((128, tm), lambda j, i: (jnp.maximum(j - nq, 0), i)),
                  qcol, row, row, tab, tab, tab, pl.BlockSpec((128, 128), lambda j, i: (0, 0)), ANY],
        out_specs=[qcol, row],
        out_shape=[jax.ShapeDtypeStruct(dproj.shape, dproj.dtype), jax.ShapeDtypeStruct((1, QKW), F32)],
        input_output_aliases={9: 0}, compiler_params=_cp(("arbitrary", "arbitrary")),
    )(dqt, dkt, proj, wrow, scrow, *tabs, _e128(), dproj)


REP = N_Q_HEADS // N_KV_HEADS


def _lanes(ref):
    return jnp.concatenate([ref[r] for r in range(REP)], axis=1)


V_AUG = HEAD_DIM + 8
LOG2E = math.log2(math.e)


def _flash_fwd(qkt, vta, *, name):
    s = qkt.shape[2]
    tq, tk = _tile(s, 1024), _tile(s, 512)
    nk = s // tk
    lanes = REP * tq

    def body(q_ref, k_ref, v_ref, o_ref, lse_ref, m_ref, acc_ref):
        j = pl.program_id(2)

        @pl.when(j == 0)
        def _():
            m_ref[...] = jnp.full_like(m_ref, NEG)
            acc_ref[...] = jnp.zeros_like(acc_ref)

        st = _dot(k_ref[0], _lanes(q_ref), TN)
        m_prev = m_ref[...]
        m_new = jnp.maximum(m_prev, jnp.max(st, axis=0, keepdims=True))
        p = jnp.exp2(st - m_new).astype(MMD)
        acc_ref[...] = jnp.exp2(m_prev - m_new) * acc_ref[...] + _dot(v_ref[0], p)
        m_ref[...] = m_new

        @pl.when(j == nk - 1)
        def _():
            acc = acc_ref[...]
            l = acc[HEAD_DIM:HEAD_DIM + 1]
            o = acc[0:HEAD_DIM] / l
            ls = m_ref[...] + jnp.log(l) * LOG2E
            for r in range(REP):
                o_ref[r] = o[:, r * tq:(r + 1) * tq].astype(o_ref.dtype)
                lse_ref[r] = ls[:, r * tq:(r + 1) * tq]

    qspec = pl.BlockSpec((REP, HEAD_DIM, tq), lambda g, i, j: (g, 0, i))
    return pl.pallas_call(
        body, name=name, grid=(N_KV_HEADS, s // tq, nk),
        in_specs=[qspec, pl.BlockSpec((1, HEAD_DIM, tk), lambda g, i, j: (N_Q_HEADS + g, 0, j)),
                  pl.BlockSpec((1, V_AUG, tk), lambda g, i, j: (g, 0, j))],
        out_specs=[qspec, pl.BlockSpec((REP, 1, tq), lambda g, i, j: (g, 0, i))],
        out_shape=[jax.ShapeDtypeStruct((N_Q_HEADS, HEAD_DIM, s), MMD), jax.ShapeDtypeStruct((N_Q_HEADS, 1, s), F32)],
        scratch_shapes=[pltpu.VMEM((1, lanes), F32), pltpu.VMEM((V_AUG, lanes), F32)],
        compiler_params=_cp(("arbitrary", "arbitrary", "arbitrary"), VMEM_BIG),
    )(qkt, qkt, vta)


def _flash_bwd(qkt, vta, dot, ot, lse, *, name):
    s = qkt.shape[2]
    tq, tk = _tile(s, 512), _tile(s, 1024)
    nk = s // tk

    def body(q_ref, kt_ref, vt_ref, do_ref, o_ref, lse_ref, dq_ref, dk_ref, dv_ref, dq_acc):
        i, j = pl.program_id(1), pl.program_id(2)
        q, do = _lanes(q_ref), _lanes(do_ref)
        delta = jnp.sum(do.astype(F32) * _lanes(o_ref).astype(F32), axis=0, keepdims=True)
        kt, vt = kt_ref[0], vt_ref[0, 0:HEAD_DIM, :]
        p = jnp.exp2(_dot(kt, q, TN) - _lanes(lse_ref))
        dvc = _dot(p.astype(MMD), do, NT)
        ds = (p * (_dot(vt, do, TN) - delta)).astype(MMD)
        dkc = _dot(ds, q, NT) * (1.0 / LOG2E)
        dqc = _dot(kt, ds)
        rows = pl.ds(pl.multiple_of(j * tk, tk), tk)

        @pl.when(i == 0)
        def _():
            dk_ref[0, rows, :] = dkc
            dv_ref[0, rows, :] = dvc

        @pl.when(i > 0)
        def _():
            dk_ref[0, rows, :] += dkc
            dv_ref[0, rows, :] += dvc

        @pl.when(j == 0)
        def _():
            dq_acc[...] = dqc

        @pl.when(j > 0)
        def _():
            dq_acc[...] += dqc

        @pl.when(j == nk - 1)
        def _():
            acc = dq_acc[...]
            for r in range(REP):
                dq_ref[r] = acc[:, r * tq:(r + 1) * tq]

    qspec = pl.BlockSpec((REP, HEAD_DIM, tq), lambda g, i, j: (g, 0, i))
    kvres = pl.BlockSpec((1, s, HEAD_DIM), lambda g, i, j: (g, 0, 0))
    return pl.pallas_call(
        body, name=name, grid=(N_KV_HEADS, s // tq, nk),
        in_specs=[qspec, pl.BlockSpec((1, HEAD_DIM, tk), lambda g, i, j: (N_Q_HEADS + g, 0, j)),
                  pl.BlockSpec((1, V_AUG, tk), lambda g, i, j: (g, 0, j)),
                  qspec, qspec, pl.BlockSpec((REP, 1, tq), lambda g, i, j: (g, 0, i))],
        out_specs=[qspec, kvres, kvres],
        out_shape=[jax.ShapeDtypeStruct((N_Q_HEADS, HEAD_DIM, s), F32), jax.ShapeDtypeStruct((N_KV_HEADS, s, HEAD_DIM), F32),
                   jax.ShapeDtypeStruct((N_KV_HEADS, s, HEAD_DIM), F32)],
        scratch_shapes=[pltpu.VMEM((HEAD_DIM, REP * tq), F32)],
        compiler_params=_cp(("arbitrary", "arbitrary", "arbitrary"), VMEM_BIG),
    )(qkt, qkt, vta, dot, ot, lse)


HALO = 8
CONV_W = 2048 + 2 * SSD_GROUPS * SSD_N


def _shifted(win, off, r):
    return pltpu.roll(win, (r + 2 * HALO - off) % (r + 2 * HALO), 0)[0:r]


def _conv_fwd(proj, w8, brow, *, name):
    s = proj.shape[0]
    cb = 256
    r = _tile(s, 512)

    def body(x_ref, w_ref, b_ref, o_ref, pad_ref):
        zeros = jnp.zeros((HALO, cb), F32)
        pad_ref[0:HALO, :] = zeros
        pad_ref[s + HALO:s + 2 * HALO, :] = zeros

        def fill(i, carry):
            st = pl.multiple_of(i * r, r)
            pad_ref[pl.ds(st + HALO, r), :] = x_ref[pl.ds(st, r), :].astype(F32)
            return carry

        lax.fori_loop(0, s // r, fill, 0)
        wv = w_ref[...]
        bv = b_ref[...]

        def step(i, carry):
            st = pl.multiple_of(i * r, r)
            win = pad_ref[pl.ds(st, r + 2 * HALO), :]
            acc = bv + wv[0:1, :] * _shifted(win, HALO - 2, r)
            for t in range(1, D_CONV):
                acc = acc + wv[t:t + 1, :] * _shifted(win, HALO - 2 + t, r)
            o_ref[pl.ds(st, r), :] = (acc * _sigmoid(acc)).astype(o_ref.dtype)
            return carry

        lax.fori_loop(0, s // r, step, 0)

    return pl.pallas_call(
        body, name=name, grid=(CONV_W // cb,),
        in_specs=[pl.BlockSpec((s, cb), lambda j: (0, XS0 // cb + j)), pl.BlockSpec((8, cb), lambda j: (0, j)),
                  pl.BlockSpec((1, cb), lambda j: (0, j))],
        out_specs=pl.BlockSpec((s, cb), lambda j: (0, j)),
        out_shape=jax.ShapeDtypeStruct((s, CONV_W), MMD),
        scratch_shapes=[pltpu.VMEM((s + 2 * HALO, cb), F32)],
        compiler_params=_cp(("arbitrary",), VMEM_MID),
    )(proj, w8, brow)


def _conv_bwd(proj, col0, ga, gb, w8, brow, dproj, *, name):
    s = proj.shape[0]
    width = ga.shape[1]
    cb = 128
    c0 = col0 // cb
    r = _tile(s, 512)

    def body(x_ref, ga_ref, gb_ref, w_ref, b_ref, _, dx_ref, dw_ref, db_ref, xpad, dpad):
        zeros = jnp.zeros((HALO, cb), F32)
        for ref in (xpad, dpad):
            ref[0:HALO, :] = zeros
            ref[s + HALO:s + 2 * HALO, :] = zeros

        def fill(i, carry):
            st = pl.multiple_of(i * r, r)
            xpad[pl.ds(st + HALO, r), :] = x_ref[pl.ds(st, r), :].astype(F32)
            return carry

        lax.fori_loop(0, s // r, fill, 0)
        wv = w_ref[...]
        bv = b_ref[...]

        def first(i, carry):
            st = pl.multiple_of(i * r, r)
            win = xpad[pl.ds(st, r + 2 * HALO), :]
            taps = [_shifted(win, HALO - 2 + t, r) for t in range(D_CONV)]
            u = bv
            for t in range(D_CONV):
                u = u + wv[t:t + 1, :] * taps[t]
            sg = _sigmoid(u)
            du = ((ga_ref[pl.ds(st, r), :].astype(F32) + gb_ref[pl.ds(st, r), :].astype(F32))
                  * (sg * (1.0 + u * (1.0 - sg))))
            dpad[pl.ds(st + HALO, r), :] = du
            out = [carry[0] + jnp.sum(du, axis=0, keepdims=True)]
            for t in range(D_CONV):
                out.append(carry[1 + t] + jnp.sum(du * taps[t], axis=0, keepdims=True))
            return tuple(out)

        sums = lax.fori_loop(0, s // r, first, tuple(jnp.zeros((1, cb), F32) for _ in range(1 + D_CONV)))
        db_ref[...] = sums[0]
        for t in range(D_CONV):
            dw_ref[t:t + 1, :] = sums[1 + t]
        dw_ref[D_CONV:8, :] = jnp.zeros((8 - D_CONV, cb), F32)

        def second(i, carry):
            st = pl.multiple_of(i * r, r)
            win = dpad[pl.ds(st, r + 2 * HALO), :]
            acc = wv[0:1, :] * _shifted(win, HALO + 2, r)
            for t in range(1, D_CONV):
                acc = acc + wv[t:t + 1, :] * _shifted(win, HALO + 2 - t, r)
            dx_ref[pl.ds(st, r), :] = acc.astype(dx_ref.dtype)
            return carry

        lax.fori_loop(0, s // r, second, 0)

    col = pl.BlockSpec((s, cb), lambda j: (0, j))
    xcol = pl.BlockSpec((s, cb), lambda j: (0, XS0 // cb + c0 + j))
    return pl.pallas_call(
        body, name=name, grid=(width // cb,),
        in_specs=[xcol, col, col, pl.BlockSpec((8, cb), lambda j: (0, c0 + j)),
                  pl.BlockSpec((1, cb), lambda j: (0, c0 + j)), ANY],
        out_specs=[xcol, pl.BlockSpec((8, cb), lambda j: (0, j)), pl.BlockSpec((1, cb), lambda j: (0, j))],
        out_shape=[jax.ShapeDtypeStruct(dproj.shape, dproj.dtype), jax.ShapeDtypeStruct((8, width), F32),
                   jax.ShapeDtypeStruct((1, width), F32)],
        scratch_shapes=[pltpu.VMEM((s + 2 * HALO, cb), F32), pltpu.VMEM((s + 2 * HALO, cb), F32)],
        input_output_aliases={5: 0}, compiler_params=_cp(("arbitrary",), VMEM_BIG),
    )(proj, ga, gb, w8, brow, dproj)


def _tri(lower):
    i = jnp.arange(CHUNK)
    return ((i[:, None] >= i[None, :]) if lower else (i[:, None] <= i[None, :])).astype(F32)


def _dt_fwd(raw, bias, arow, *, name):
    s = raw.shape[0]

    def body(r_ref, b_ref, a_ref, lo_ref, up_ref, dt_ref, cs_ref):
        u = r_ref[...] + b_ref[...]
        dt = jnp.maximum(u, 0.0) + jnp.log1p(jnp.exp(-jnp.abs(u)))
        dt_ref[...] = dt
        a = dt * a_ref[...]
        lane = lax.broadcasted_iota(jnp.int32, (CHUNK, 128), 1)
        cs_ref[...] = jnp.where(lane < SSD_HEADS, _dot_hi(lo_ref[...], a), _dot_hi(up_ref[...], a))

    blk = pl.BlockSpec((CHUNK, 128), lambda i: (i, 0))
    row = pl.BlockSpec((1, 128), lambda i: (0, 0))
    tri = pl.BlockSpec((CHUNK, CHUNK), lambda i: (0, 0))
    return pl.pallas_call(
        body, name=name, grid=(s // CHUNK,), in_specs=[blk, row, row, tri, tri], out_specs=[blk, blk],
        out_shape=[jax.ShapeDtypeStruct((s, 128), F32)] * 2, compiler_params=_cp(("arbitrary",)),
    )(raw, bias, arow, _tri(True), _tri(False))


def _dt_bwd(ddt0, ddt1, raw, bias, dproj, *, name):
    s = raw.shape[0]
    tm = _tile(s, 1024)

    def body(d0_ref, d1_ref, r_ref, b_ref, _, o_ref, db_ref):
        g = (d0_ref[...] + d1_ref[...]) * _sigmoid(r_ref[...] + b_ref[...])
        o_ref[...] = g.astype(o_ref.dtype)
        _acc_rows(db_ref, jnp.sum(g, axis=0, keepdims=True), pl.program_id(0) == 0)

    blk = pl.BlockSpec((tm, 128), lambda i: (i, 0))
    row = pl.BlockSpec((1, 128), lambda i: (0, 0))
    return pl.pallas_call(
        body, name=name, grid=(s // tm,), in_specs=[blk, blk, blk, row, ANY],
        out_specs=[pl.BlockSpec((tm, 128), lambda i: (i, DT0 // 128)), row],
        out_shape=[jax.ShapeDtypeStruct(dproj.shape, dproj.dtype), jax.ShapeDtypeStruct((1, 128), F32)],
        input_output_aliases={4: 0}, compiler_params=_cp(("arbitrary",)),
    )(ddt0, ddt1, raw, bias, dproj)


GW = HPG * SSD_P


GPS = SSD_GROUPS


def _ssd_specs(nc, rev):
    cc = (lambda c: nc - 1 - c) if rev else (lambda c: c)
    return dict(
        x=pl.BlockSpec((CHUNK, GPS * GW), lambda g, c: (cc(c), g)),
        b=pl.BlockSpec((CHUNK, GPS * SSD_N), lambda g, c: (cc(c), 2048 // (GPS * SSD_N) + g)),
        c=pl.BlockSpec((CHUNK, GPS * SSD_N), lambda g, c: (cc(c), 2048 // (GPS * SSD_N) + 1 + g)),
        lanes=pl.BlockSpec((CHUNK, 128), lambda g, c: (cc(c), 0)),
        drow=pl.BlockSpec((1, GPS * GW), lambda g, c: (0, g)),
        y=pl.BlockSpec((CHUNK, GPS * GW), lambda g, c: (cc(c), g)),
        h=pl.BlockSpec((GPS, 1, SSD_N, GW), lambda g, c: (g, cc(c), 0, 0)),
        n=pl.BlockSpec((CHUNK, GPS * SSD_N), lambda g, c: (cc(c), g)),
    )


def _ssd_mask(anti):
    ii = lax.broadcasted_iota(jnp.int32, (CHUNK, CHUNK), 0)
    jj = lax.broadcasted_iota(jnp.int32, (CHUNK, CHUNK), 1)
    return ii, jj, (ii <= jj) if anti else (ii >= jj)


def _expand(x, ex, terms=3):
    h1 = x.astype(jnp.bfloat16)
    r1 = x - h1.astype(F32)
    h2 = r1.astype(jnp.bfloat16)
    out = _dot(h1, ex) + _dot(h2, ex)
    if terms == 3:
        out = out + _dot((r1 - h2.astype(F32)).astype(jnp.bfloat16), ex)
    return out


def _headsum(a, e, lo=True):
    hi = a.astype(jnp.bfloat16)
    return _dot(hi, e) + _dot((a - hi.astype(F32)).astype(jnp.bfloat16), e) if lo else _dot(hi, e)


def _expand_mats():
    lane = jnp.arange(128)[None, :, None]
    col = jnp.arange(GW)[None, None, :]
    base = (jnp.arange(2)[:, None] * SSD_HEADS + jnp.arange(SSD_GROUPS)[None, :] * HPG).reshape(2 * SSD_GROUPS, 1, 1)
    return (lane == base + col // SSD_P).astype(jnp.bfloat16)


def _headsum_mats():
    e1 = (jnp.arange(GW)[:, None] // SSD_P == jnp.arange(128)[None, :]).astype(jnp.bfloat16)
    e2 = (jnp.arange(HPG * CHUNK)[:, None] // CHUNK == jnp.arange(128)[None, :]).astype(jnp.bfloat16)
    return e1, e2


def _ssd_fwd(xc, dt, cs, ex, drow, di, *, name):
    s = xc.shape[0]
    nc = s // CHUNK
    anti = di == 1
    sp = _ssd_specs(nc, anti)
    trow = 0 if anti else CHUNK - 1

    def body(x_ref, b_ref, c_ref, dt_ref, cs_ref, ex_ref, d_ref, y_ref, hp_ref, h_ref):
        @pl.when(pl.program_id(1) == 0)
        def _():
            h_ref[...] = jnp.zeros_like(h_ref)

        mask = _ssd_mask(anti)[2]
        dtv, csv = dt_ref[...], cs_ref[...]
        cst = csv.T
        for gi in range(GPS):
            cols = slice(gi * GW, (gi + 1) * GW)
            ncols = slice(gi * SSD_N, (gi + 1) * SSD_N)
            ex = ex_ref[gi]
            xb = x_ref[:, cols].astype(F32)
            bm, cm = b_ref[:, ncols], c_ref[:, ncols]
            csr = cst[SSD_HEADS * di + HPG * gi:SSD_HEADS * di + HPG * (gi + 1)]
            dtf = _expand(dtv, ex, 2)
            csf = _expand(csv, ex)
            tl = csf[trow:trow + 1, :]
            h = h_ref[gi]
            hp_ref[gi, 0] = h.astype(hp_ref.dtype)
            g = _dot(cm, bm, NT)
            xs = xb * dtf
            xsm = xs.astype(MMD)
            base = jnp.exp(csf) * _dot(cm, h.astype(MMD)) + d_ref[:, cols] * xb
            for r in range(HPG):
                sl = slice(r * SSD_P, (r + 1) * SSD_P)
                lm = jnp.exp(jnp.where(mask, csf[:, r * SSD_P:r * SSD_P + 1] - csr[r:r + 1, :], NEG))
                y_ref[:, gi * GW + r * SSD_P:gi * GW + (r + 1) * SSD_P] = (
                    _dot((g * lm).astype(MMD), xsm[:, sl]) + base[:, sl]).astype(y_ref.dtype)
            xd = (xs * jnp.exp(tl - csf)).astype(MMD)
            h_ref[gi] = h * jnp.exp(tl) + _dot(bm, xd, TN)

    return pl.pallas_call(
        body, name=name, grid=(1, nc),
        in_specs=[sp["x"], sp["b"], sp["c"], sp["lanes"], sp["lanes"],
                  pl.BlockSpec((GPS, 128, GW), lambda g, c: (di, 0, 0)), sp["drow"]],
        out_specs=[sp["y"], sp["h"]],
        out_shape=[jax.ShapeDtypeStruct((s, 2048), MMD), jax.ShapeDtypeStruct((SSD_GROUPS, nc, SSD_N, GW), MMD)],
        scratch_shapes=[pltpu.VMEM((GPS, SSD_N, GW), F32)],
        compiler_params=_cp(("arbitrary", "arbitrary")),
    )(xc, xc, xc, dt, cs, ex, drow)


def _ssd_bwd(xc, dt, cs, ex, drow, arow, dy, hprev, di, *, name):
    s = xc.shape[0]
    nc = s // CHUNK
    anti = di == 1
    sp = _ssd_specs(nc, not anti)
    trow = 0 if anti else CHUNK - 1
    e1, e2 = _headsum_mats()

    def body(x_ref, b_ref, c_ref, dt_ref, cs_ref, ex_ref, d_ref, a_ref, dy_ref, hp_ref, tri_ref,
             e1_ref, e2_ref, dx_ref, db_ref, dc_ref, ddt_ref, da_ref, dh_ref, w_ref, dxs_ref):
        @pl.when(pl.program_id(1) == 0)
        def _():
            dh_ref[...] = jnp.zeros_like(dh_ref)
            da_ref[...] = jnp.zeros_like(da_ref)

        e1v = e1_ref[...]
        ii, _, mask = _ssd_mask(anti)
        dtv, csv = dt_ref[...], cs_ref[...]
        cst = csv.T
        ddt_acc = jnp.zeros((CHUNK, 128), F32)
        da_acc = jnp.zeros((1, 128), F32)
        for gi in range(GPS):
            lane0 = SSD_HEADS * di + HPG * gi
            cols = slice(gi * GW, (gi + 1) * GW)
            ncols = slice(gi * SSD_N, (gi + 1) * SSD_N)
            ex = ex_ref[gi]
            xb = x_ref[:, cols].astype(F32)
            bm, cm = b_ref[:, ncols], c_ref[:, ncols]
            csr = cst[lane0:lane0 + HPG]
            dym = dy_ref[:, cols]
            dyb = dym.astype(F32)
            hpm = hp_ref[gi, 0]
            hp = hpm.astype(F32)
            dh = dh_ref[gi]
            dhm = dh.astype(MMD)
            dtf = _expand(dtv, ex, 2)
            csf = _expand(csv, ex)
            tl = csf[trow:trow + 1, :]
            e = jnp.exp(csf)
            dec = jnp.exp(tl - csf)
            et = jnp.exp(tl)
            xs = xb * dtf
            xsm = xs.astype(MMD)
            g = _dot(cm, bm, NT)
            z = _dot(cm, hpm)
            bdh = _dot(bm, dhm)
            dg = jnp.zeros((CHUNK, CHUNK), F32)
            wcols = jnp.zeros((CHUNK, CHUNK), F32)
            for r in range(HPG):
                sl = slice(r * SSD_P, (r + 1) * SSD_P)
                lm = jnp.exp(jnp.where(mask, csf[:, r * SSD_P:r * SSD_P + 1] - csr[r:r + 1, :], NEG))
                mm = g * lm
                dm = _dot(dym[:, sl], xsm[:, sl], NT)
                w = dm * mm
                w_ref[gi, :, r * CHUNK:(r + 1) * CHUNK] = w
                wcols = jnp.where(ii == r, jnp.sum(w, axis=0, keepdims=True), wcols)
                dg = dg + dm * lm
                dxs_ref[gi, :, sl] = _dot(mm.astype(MMD), dym[:, sl], TN)
            dxs = dxs_ref[gi] + dec * bdh
            dx_ref[:, cols] = (dxs * dtf + d_ref[:, cols] * dyb).astype(dx_ref.dtype)
            tb = xs * bdh * dec
            d_tot = jnp.sum(tb, axis=0, keepdims=True) + et * jnp.sum(dh * hp, axis=0, keepdims=True)
            d_tot = _headsum(jnp.broadcast_to(d_tot, (8, GW)), e1v)[0:1]
            dcs = (_headsum(dyb * (e * z) - tb, e1v) + _headsum(w_ref[gi], e2_ref[...]) - wcols.T
                   + jnp.where(ii == trow, d_tot, 0.0))
            da = pltpu.roll(_dot_hi(tri_ref[...], dcs), lane0, 1)
            ddt_acc = ddt_acc + da * a_ref[...] + pltpu.roll(_headsum(dxs * xb, e1v, lo=False), lane0, 1)
            da_acc = da_acc + jnp.sum(da * dtv, axis=0, keepdims=True)
            dgm = dg.astype(MMD)
            dz = (e * dyb).astype(MMD)
            dc_ref[:, ncols] = (_dot(dgm, bm) + _dot(dz, hpm, NT)).astype(dc_ref.dtype)
            db_ref[:, ncols] = (_dot(dgm, cm, TN) + _dot((xs * dec).astype(MMD), dhm, NT)).astype(db_ref.dtype)
            dh_ref[gi] = dh * et + _dot(cm, dz, TN)
        ddt_ref[...] = ddt_acc
        da_ref[...] += da_acc

    const = lambda shape: pl.BlockSpec(shape, lambda g, c: (0,) * len(shape))
    return pl.pallas_call(
        body, name=name, grid=(1, nc),
        in_specs=[sp["x"], sp["b"], sp["c"], sp["lanes"], sp["lanes"],
                  pl.BlockSpec((GPS, 128, GW), lambda g, c: (di, 0, 0)), sp["drow"],
                  const((1, 128)), sp["y"], sp["h"],
                  const((CHUNK, CHUNK)), const((GW, 128)), const((HPG * CHUNK, 128))],
        out_specs=[sp["y"], sp["n"], sp["n"], sp["lanes"], const((1, 128))],
        out_shape=[jax.ShapeDtypeStruct((s, 2048), MMD), jax.ShapeDtypeStruct((s, SSD_GROUPS * SSD_N), MMD),
                   jax.ShapeDtypeStruct((s, SSD_GROUPS * SSD_N), MMD), jax.ShapeDtypeStruct((s, 128), F32),
                   jax.ShapeDtypeStruct((1, 128), F32)],
        scratch_shapes=[pltpu.VMEM((GPS, SSD_N, GW), F32), pltpu.VMEM((GPS, CHUNK, HPG * CHUNK), F32),
                        pltpu.VMEM((GPS, CHUNK, GW), F32)],
        compiler_params=_cp(("arbitrary", "arbitrary")),
    )(xc, xc, xc, dt, cs, ex, drow, arow, dy, hprev, _tri(anti), e1, e2)


def _gnorm_fwd(ya, yb, proj, w, *, name):
    s = ya.shape[0]
    tm = _tile(s, 256)

    def body(a_ref, b_ref, z_ref, w_ref, o_ref):
        zv = z_ref[...].astype(F32)
        t = (a_ref[...].astype(F32) + b_ref[...].astype(F32)) * (zv * _sigmoid(zv))
        r = lax.rsqrt(jnp.mean(t * t, axis=-1, keepdims=True) + EPS)
        o_ref[...] = ((t * r) * w_ref[...]).astype(o_ref.dtype)

    big = pl.BlockSpec((tm, 2048), lambda i: (i, 0))
    row = pl.BlockSpec((1, 2048), lambda i: (0, 0))
    return pl.pallas_call(
        body, name=name, grid=(s // tm,), in_specs=[big, big, big, row], out_specs=big,
        out_shape=jax.ShapeDtypeStruct((s, 2048), MMD), compiler_params=_cp(("arbitrary",)),
    )(ya, yb, proj, w)


def _gnorm_bwd(dout, ya, yb, proj, xc, w, dproj, *, name):
    s = ya.shape[0]
    tm = _tile(s, 256)

    def body(do_ref, a_ref, b_ref, z_ref, x_ref, w_ref, _, dy_ref, dz_ref, dw_ref, dd_ref):
        zv = z_ref[...].astype(F32)
        sg = _sigmoid(zv)
        sz = zv * sg
        y = a_ref[...].astype(F32) + b_ref[...].astype(F32)
        t = y * sz
        r = lax.rsqrt(jnp.mean(t * t, axis=-1, keepdims=True) + EPS)
        nv = t * r
        dov = do_ref[...].astype(F32)
        _acc_rows(dw_ref, jnp.sum(dov * nv, axis=0, keepdims=True), pl.program_id(0) == 0)
        dn = dov * w_ref[...]
        dt_ = r * (dn - nv * jnp.mean(dn * nv, axis=-1, keepdims=True))
        dy = dt_ * sz
        dy_ref[...] = dy.astype(dy_ref.dtype)
        dz_ref[...] = (dt_ * y * (sg * (1.0 + zv * (1.0 - sg)))).astype(dz_ref.dtype)
        _acc_rows(dd_ref, jnp.sum(dy * x_ref[...].astype(F32), axis=0, keepdims=True), pl.program_id(0) == 0)

    big = pl.BlockSpec((tm, 2048), lambda i: (i, 0))
    row = pl.BlockSpec((1, 2048), lambda i: (0, 0))
    return pl.pallas_call(
        body, name=name, grid=(s // tm,), in_specs=[big, big, big, big, big, row, ANY], out_specs=[big, big, row, row],
        out_shape=[jax.ShapeDtypeStruct((s, 2048), MMD), jax.ShapeDtypeStruct(dproj.shape, dproj.dtype),
                   jax.ShapeDtypeStruct((1, 2048), F32), jax.ShapeDtypeStruct((1, 2048), F32)],
        input_output_aliases={6: 1}, compiler_params=_cp(("arbitrary",)),
    )(dout, ya, yb, proj, xc, w, dproj)


def _unheads(a):
    return a.transpose(1, 0, 2).reshape(a.shape[1], a.shape[0] * HEAD_DIM)


def _local_step(x, target, mod, wts, small, in_weights=None, late_weights=None, late_grads=None, in_grad=None,
                zero=0.0):
    s, d = x.shape
    shift1, scale1, gate1, shift2, scale2, gate2 = [mod[i:i + 1] for i in range(6)]

    h1 = _ln_mod(x, small["norm1_w"], scale1, shift1, name="ln1")
    qk_w = jnp.concatenate([jnp.tile(small["q_norm_w"], (1, N_Q_HEADS)), jnp.tile(small["k_norm_w"], (1, N_KV_HEADS))], axis=1)
    qk_sc = jnp.concatenate([jnp.full((1, N_Q_HEADS * HEAD_DIM), HEAD_DIM ** -0.5, F32),
                             jnp.ones((1, N_KV_HEADS * HEAD_DIM), F32)], axis=1)
    qk_sc2 = jnp.concatenate([jnp.full((1, N_Q_HEADS * HEAD_DIM), HEAD_DIM ** -0.5 * LOG2E, F32),
                              jnp.ones((1, N_KV_HEADS * HEAD_DIM), F32)], axis=1)
    tabs = _rope_tables(s, zero)
    if in_weights is not None:
        wts = {**wts, **in_weights([h1, *tabs])}
    proj = _mm(h1, wts["w_in_p"], name="in_proj", outs=[MMD], tm=512, tn=2944, b_outer=True)
    dt_raw = _mm(h1, wts["w_dt"], name="dt_proj", outs=[F32], tm=512, tn=128)
    qkt = _qk_fwd(proj, qk_w, qk_sc2, tabs, name="qk_fwd").reshape(N_Q_HEADS + N_KV_HEADS, HEAD_DIM, s)
    v_sd = proj[:, V0:V0 + N_KV_HEADS * HEAD_DIM]
    vta = jnp.concatenate([v_sd.T.reshape(N_KV_HEADS, HEAD_DIM, s), jnp.ones((N_KV_HEADS, V_AUG - HEAD_DIM, s), MMD)], axis=1)
    ot, lse = _flash_fwd(qkt, vta, name="flash_fwd")
    ot2 = ot.reshape(N_Q_HEADS * HEAD_DIM, s)
    if late_weights is not None:
        wts = {**wts, **late_weights(ot)}

    w8 = jnp.pad(small["conv_w"], ((0, 8 - D_CONV), (0, 0)))
    xc = _conv_fwd(proj, w8, small["conv_b"], name="conv_fwd")
    a_neg = -jnp.exp(small["A_log"])
    arow = jnp.pad(a_neg.reshape(1, 2 * SSD_HEADS), ((0, 0), (0, 128 - 2 * SSD_HEADS)))
    bias_row = jnp.pad(small["dt_bias"].reshape(1, 2 * SSD_HEADS), ((0, 0), (0, 128 - 2 * SSD_HEADS)))
    dt, cs = _dt_fwd(dt_raw, bias_row, arow, name="dt_fwd")
    drow = jnp.repeat(small["ssd_D"], SSD_P, axis=1)
    dirs = [dict(drow=drow), dict(drow=jnp.zeros_like(drow))]
    ex = _expand_mats()
    ys = []
    for di, dd in enumerate(dirs):
        y, dd["hprev"] = _ssd_fwd(xc, dt, cs, ex, dd["drow"], di, name=f"ssd_fwd{di}")
        ys.append(y)
    ssdn = _gnorm_fwd(ys[0], ys[1], proj, small["ssd_norm_w"], name="gnorm_fwd")

    a_o = _mm(ot2, wts["w_attn_out"], name="attn_out", outs=[MMD], ta=True, tm=512, tn=1024)

    def merge_epi(acc, ao, ga, gs):
        return (_sigmoid(ga.astype(F32)) * ao.astype(F32) + _sigmoid(gs.astype(F32)) * acc, acc)

    merged, b_o = _mm(ssdn, wts["w_ssd_out"], name="ssd_out", outs=[MMD, MMD], tm=512, tn=1024,
                      extras=[(a_o, "tile", 0), (proj, "tile", GA0), (proj, "tile", GS0)], epi=merge_epi)

    def res_epi(acc, res, gate):
        return (res + gate * acc, acc)

    x1, mo = _mm(merged, wts["w_o"], name="w_o", outs=[F32, MMD], tm=512, tn=1024,
                 extras=[(x, "tile", 0), (gate1, "row", 0)], epi=res_epi)
    h2 = _ln_mod(x1, small["norm2_w"], scale2, shift2, name="ln2")

    def relu2_epi(acc):
        rl = jnp.maximum(acc, 0.0)
        return (rl * rl, rl)

    act, rl = _mm(h2, wts["w_mlp1"], name="mlp1", outs=[MMD, MMD], tm=1024, tn=1024, epi=relu2_epi, b_outer=True)

    def loss_epi(acc, res, gate, tgt):
        return ((res + gate * acc - tgt) * (1.0 / d), acc)

    dy, ffo = _mm(act, wts["w_mlp2"], name="mlp2", outs=[F32, MMD], tm=512, tn=1024, vmem=VMEM_BIG,
                  extras=[(x1, "tile", 0), (gate2, "row", 0), (target, "tile", 0)], epi=loss_epi)
    loss = _sumsq(dy, name="loss") * (0.5 * d)

    gw = {}
    gs_ = {}
    dffo, dgate2 = _gate_bwd(dy, ffo, gate2, name="gate2_bwd")
    dpre = _mm(dffo, wts["w_mlp2"], name="mlp2_dx", outs=[MMD], nt=True, tm=1024, tn=1024, b_outer=True,
               extras=[(rl, "tile", 0)], epi=lambda acc, r: (acc * (2.0 * r.astype(F32)),))
    gw["w_mlp2"] = _mm_tn(act, dffo, name="mlp2_dw")
    dh2 = _mm(dpre, wts["w_mlp1"], name="mlp1_dx", outs=[F32], nt=True, tm=1024, tn=1024, vmem=VMEM_BIG)
    gw["w_mlp1"] = _mm_tn(h2, dpre, name="mlp1_dw")
    dx1, dshift2, dscale2, gs_["norm2_w"] = _ln_mod_bwd(dh2, x1, small["norm2_w"], scale2, dy, name="ln2_bwd")
    dmo, dgate1 = _gate_bwd(dx1, mo, gate1, name="gate1_bwd")

    def merge_bwd_epi(acc, ao, bo, ga, gs):
        sa, ss = _sigmoid(ga.astype(F32)), _sigmoid(gs.astype(F32))
        return (acc * sa, acc * ss, acc * ao.astype(F32) * sa * (1.0 - sa), acc * bo.astype(F32) * ss * (1.0 - ss))

    da_o, db_o, dga, dgs = _mm(dmo, wts["w_o"], name="w_o_dx", outs=[MMD] * 4, nt=True, tm=512, tn=1024,
                               extras=[(a_o, "tile", 0), (b_o, "tile", 0), (proj, "tile", GA0), (proj, "tile", GS0)],
                               epi=merge_bwd_epi)
    gw["w_o"] = _mm_tn(merged, dmo, name="w_o_dw")
    dot = _mm(wts["w_attn_out"], da_o, name="attn_out_dx", outs=[MMD], nt=True, tm=1024, tn=1024)
    gw["w_attn_out"] = _mm(ot2, da_o, name="attn_out_dw", outs=[F32], tm=256, tn=512, vmem=VMEM_BIG)
    dssdn = _mm(db_o, wts["w_ssd_out"], name="ssd_out_dx", outs=[MMD], nt=True, tm=512, tn=2048)
    gw["w_ssd_out"] = _mm_tn(ssdn, db_o, name="ssd_out_dw")

    dproj = lax.dynamic_update_slice(lax.empty((s, PW), MMD), jnp.concatenate([dga, dgs], axis=1), (0, GA0))

    norm_w = small["ssd_norm_w"] if late_grads is None else small["ssd_norm_w"] + late_grads(gw)
    dyssd, dproj, gs_["ssd_norm_w"], dd_row = _gnorm_bwd(dssdn, ys[0], ys[1], proj, xc, norm_w, dproj, name="gnorm_bwd")
    gs_["ssd_D"] = dd_row.reshape(SSD_HEADS, SSD_P).sum(axis=1).reshape(1, SSD_HEADS)
    dxc, ddts, das = [], [], []
    for di, dd in enumerate(dirs):
        dxs, dbm, dcm, ddt_d, da_d = _ssd_bwd(xc, dt, cs, ex, dd["drow"], arow, dyssd, dd["hprev"], di, name=f"ssd_bwd{di}")
        dxc.append((dxs, dbm, dcm))
        ddts.append(ddt_d)
        das.append(da_d)
    dw8, db, col0 = [], [], 0
    for part, (ga, gb) in enumerate(zip(*dxc)):
        dproj, dw_part, db_part = _conv_bwd(proj, col0, ga, gb, w8, small["conv_b"], dproj, name=f"conv_bwd{part}")
        dw8.append(dw_part)
        db.append(db_part)
        col0 += ga.shape[1]
    gs_["conv_w"] = jnp.concatenate(dw8, axis=1)[0:D_CONV]
    gs_["conv_b"] = jnp.concatenate(db, axis=1)
    gs_["A_log"] = (das[0] + das[1])[:, 0:2 * SSD_HEADS].reshape(2, SSD_HEADS) * a_neg
    dproj, dbias = _dt_bwd(ddts[0], ddts[1], dt_raw, bias_row, dproj, name="dt_bwd")
    gs_["dt_bias"] = dbias[:, 0:2 * SSD_HEADS].reshape(2, SSD_HEADS)

    dqt, dk_h, dv_h = _flash_bwd(qkt, vta, dot.reshape(N_Q_HEADS, HEAD_DIM, s), ot, lse, name="flash_bwd")
    dproj, dqk_w = _qk_bwd(dqt.reshape(N_Q_HEADS * HEAD_DIM, s), dk_h.transpose(0, 2, 1).reshape(N_KV_HEADS * HEAD_DIM, s),
                           proj, qk_w, qk_sc, tabs, dproj, name="qk_bwd")
    gs_["q_norm_w"] = dqk_w[:, 0:N_Q_HEADS * HEAD_DIM].reshape(N_Q_HEADS, HEAD_DIM).sum(axis=0, keepdims=True)
    gs_["k_norm_w"] = dqk_w[:, N_Q_HEADS * HEAD_DIM:].reshape(N_KV_HEADS, HEAD_DIM).sum(axis=0, keepdims=True)
    dproj = lax.dynamic_update_slice(dproj, _unheads(dv_h).astype(MMD), (0, V0))

    gw["w_in_p"] = _mm_tn(h1, dproj, name="in_proj_dw", tk=512, tn=2944, tmm=2048, vmem=VMEM_BIG)
    zero_row = jnp.zeros((1, d), F32) if in_grad is None else jnp.zeros((1, d), F32) + in_grad(gw["w_in_p"])[0:1, 0:1]
    dh1 = _mm(dproj, wts["w_in_p"], name="in_proj_dx", outs=[F32], nt=True, tm=256, tn=1024, vmem=VMEM_BIG,
              extras=[(zero_row, "row", 0)], epi=lambda acc, r: (acc + r,))
    grad_x, dshift1, dscale1, gs_["norm1_w"] = _ln_mod_bwd(dh1, x, small["norm1_w"], scale1, dx1, name="ln1_bwd")
    dmod = jnp.concatenate([dshift1, dscale1, dgate1, dshift2, dscale2, dgate2], axis=0)
    return loss, grad_x, dmod, gw, gs_


N_DEV = 8
N_CHIP = 4
ANY = pl.BlockSpec(memory_space=pl.ANY)


def _place():
    return lax.axis_index("x"), lax.axis_index("y"), lax.axis_index("c")


def _allgather8(v, *, name):
    m_per, n = v.shape

    def body(x_ref, out_ref, send_sems, recv_sems, local_sem):
        x, y, c = _place()
        me, sibling = (x, y, c), (x, y, 1 - c)
        chips = [(1 - x, y), (x, 1 - y), (1 - x, 1 - y)]

        def rows(px, py, pc):
            return out_ref.at[pl.ds((4 * px + 2 * py + pc) * m_per, m_per), :]

        def copy(k, block, to, src=None):
            return pltpu.make_async_remote_copy(
                src_ref=rows(*block) if src is None else src, dst_ref=rows(*block),
                send_sem=send_sems.at[k], recv_sem=recv_sems.at[k], device_id=to, device_id_type=MESH)

        mine = pltpu.make_async_copy(x_ref, rows(*me), local_sem)
        mine.start()
        first = [copy(0, me, sibling, src=x_ref)]
        first += [copy(1 + j, me, (*chip, c), src=x_ref) for j, chip in enumerate(chips)]
        for cp in first:
            cp.start()
        passed = [copy(4 + j, (*chip, c), sibling) for j, chip in enumerate(chips)]
        for j, chip in enumerate(chips):
            copy(1 + j, (*chip, c), me).wait_recv()
            passed[j].start()
        copy(0, sibling, me).wait_recv()
        for j, chip in enumerate(chips):
            copy(4 + j, (*chip, 1 - c), me).wait_recv()
        for cp in first + passed:
            cp.wait_send()
        mine.wait()

    return pl.pallas_call(
        body, name=name, out_shape=jax.ShapeDtypeStruct((N_DEV * m_per, n), v.dtype),
        in_specs=[pl.BlockSpec(memory_space=pltpu.VMEM)], out_specs=pl.BlockSpec(memory_space=pltpu.VMEM),
        scratch_shapes=[pltpu.SemaphoreType.DMA((7,)), pltpu.SemaphoreType.DMA((7,)), pltpu.SemaphoreType.DMA],
    )(v)


HBM = pl.BlockSpec(memory_space=pltpu.HBM)
SEM = pl.BlockSpec(memory_space=pltpu.SEMAPHORE)


def _chips_copies(x_ref, land_ref, sems, scatter, half=False):
    x, y, c = _place()
    k = 2 * x + y
    chips = [(1 - x, y), (x, 1 - y), (1 - x, 1 - y)]
    ids = [2 * cx + cy for cx, cy in chips]
    if half:
        hr = x_ref.shape[0] // 2
        rows = pl.ds(pl.multiple_of(c * hr, 16), hr)

    def copy(j, slot):
        src = x_ref.at[ids[j]] if scatter else (x_ref.at[rows] if half else x_ref)
        dst = land_ref.at[slot, rows] if half else land_ref.at[slot]
        return pltpu.make_async_remote_copy(src_ref=src, dst_ref=dst, send_sem=sems[j], recv_sem=sems[3 + j],
                                            device_id=(*chips[j], c), device_id_type=MESH)

    return [copy(j, k) for j in range(3)], [copy(j, ids[j]) for j in range(3)]


def _chips_start(src, scatter, half=False, *, name):
    shape = src.shape if scatter else (N_CHIP,) + tuple(src.shape)

    def body(x_ref, land_ref, *rest):
        sems, token = rest[0:6], rest[8]
        for cp in _chips_copies(x_ref, land_ref, sems, scatter, half)[0]:
            cp.start()
        token[...] = jnp.zeros_like(token)

    out = pl.pallas_call(
        body, name=name,
        out_shape=(pltpu.SemaphoreType.DMA(()),) * 6 + (pltpu.HBM(src.shape, src.dtype), pltpu.HBM(shape, src.dtype),
                                                       jax.ShapeDtypeStruct((8, 128), F32)),
        in_specs=(HBM, HBM), out_specs=(SEM,) * 6 + (HBM, HBM, pl.BlockSpec(memory_space=pltpu.VMEM)),
        input_output_aliases={0: 6, 1: 7},
        compiler_params=pltpu.CompilerParams(has_side_effects=pltpu.SideEffectType.DATAFLOW_SIDE_EFFECTING),
    )(pltpu.with_memory_space_constraint(src, pltpu.HBM),
      pltpu.with_memory_space_constraint(lax.empty(shape, src.dtype), pltpu.HBM))
    return out[0:6], out[6], out[7], out[8]


def _chips_wait(sems, src, land, after, scatter, half=False, *, name):
    after = list(after) if isinstance(after, (list, tuple)) else [after]

    def body(x_ref, land_ref, *rest):
        sems_ = rest[0:6]
        for cp in _chips_copies(x_ref, land_ref, sems_, scatter, half)[1]:
            cp.wait_send()
            cp.wait_recv()

    return pl.pallas_call(
        body, name=name, out_shape=(pltpu.HBM(src.shape, src.dtype), pltpu.HBM(land.shape, land.dtype)),
        in_specs=(HBM, HBM) + (SEM,) * 6 + (ANY,) * len(after), out_specs=(HBM, HBM), input_output_aliases={0: 0, 1: 1},
        compiler_params=pltpu.CompilerParams(has_side_effects=pltpu.SideEffectType.DATAFLOW_SIDE_EFFECTING),
    )(src, land, *sems, *after)


def _row_tile(r, pref=512):
    return max(t for t in range(16, pref + 1, 16) if r % t == 0)


def _pair_complete(land, *, name):
    r = land.shape[1]
    hr = r // 2
    assert r == 2 * hr and hr % 16 == 0

    def body(in_ref, out_ref, send_sems, recv_sems):
        x, y, c = _place()
        ids = [2 * cx + cy for cx, cy in [(1 - x, y), (x, 1 - y), (1 - x, 1 - y)]]
        mine_rows = pl.ds(pl.multiple_of(c * hr, 16), hr)
        other_rows = pl.ds(pl.multiple_of((1 - c) * hr, 16), hr)

        def copy(j, rows):
            return pltpu.make_async_remote_copy(
                src_ref=in_ref.at[ids[j], mine_rows], dst_ref=out_ref.at[ids[j], rows], send_sem=send_sems.at[j],
                recv_sem=recv_sems.at[j], device_id=(x, y, 1 - c), device_id_type=MESH)

        sends = [copy(j, mine_rows) for j in range(3)]
        for cp in sends:
            cp.start()
        for j in range(3):
            copy(j, other_rows).wait_recv()
        for cp in sends:
            cp.wait_send()

    return pl.pallas_call(
        body, name=name, out_shape=jax.ShapeDtypeStruct(land.shape, land.dtype), in_specs=[ANY], out_specs=ANY,
        input_output_aliases={0: 0},
        scratch_shapes=[pltpu.SemaphoreType.DMA((3,)), pltpu.SemaphoreType.DMA((3,))],
    )(land)


def _pair_swap(a, *, name):
    n, r, cols = a.shape
    hr = r // 2

    def body(x_ref, out_ref, send_sem, recv_sem):
        x, y, c = _place()
        other_rows = pl.ds(pl.multiple_of((1 - c) * hr, 16), hr)
        cp = pltpu.make_async_remote_copy(src_ref=x_ref.at[:, other_rows], dst_ref=out_ref, send_sem=send_sem,
                                          recv_sem=recv_sem, device_id=(x, y, 1 - c), device_id_type=MESH)
        cp.start()
        cp.wait()

    return pl.pallas_call(
        body, name=name, out_shape=jax.ShapeDtypeStruct((n, hr, cols), a.dtype), in_specs=[ANY], out_specs=ANY,
        scratch_shapes=[pltpu.SemaphoreType.DMA, pltpu.SemaphoreType.DMA],
    )(a)


def _sibling_copy(a, *, name):
    def body(x_ref, out_ref, send_sem, recv_sem):
        x, y, c = _place()
        cp = pltpu.make_async_remote_copy(src_ref=x_ref, dst_ref=out_ref, send_sem=send_sem, recv_sem=recv_sem,
                                          device_id=(x, y, 1 - c), device_id_type=MESH)
        cp.start()
        cp.wait()

    return pl.pallas_call(
        body, name=name, out_shape=jax.ShapeDtypeStruct(a.shape, a.dtype), in_specs=[ANY], out_specs=ANY,
        scratch_shapes=[pltpu.SemaphoreType.DMA, pltpu.SemaphoreType.DMA],
    )(a)


def _sum_slots(a, own, *, name):
    _, r, c = a.shape
    tr = _row_tile(r, 256)

    def body(a_ref, own_ref, o_ref):
        k = 2 * lax.axis_index("x") + lax.axis_index("y")
        acc = None
        for j in range(N_CHIP):
            term = jnp.where(k == j, own_ref[j], a_ref[j]).astype(F32)
            acc = term if acc is None else acc + term
        o_ref[...] = acc

    spec = pl.BlockSpec((N_CHIP, tr, c), lambda i: (0, i, 0))
    return pl.pallas_call(
        body, name=name, grid=(r // tr,), in_specs=[spec, spec],
        out_specs=pl.BlockSpec((tr, c), lambda i: (i, 0)), out_shape=jax.ShapeDtypeStruct((r, c), F32),
        compiler_params=_cp(("arbitrary",)),
    )(a, own)


def _add2(a, b, *, name):
    r, c = a.shape
    tr = _row_tile(r)

    def body(a_ref, b_ref, o_ref):
        o_ref[...] = (a_ref[...].astype(F32) + b_ref[...].astype(F32)).astype(o_ref.dtype)

    spec = pl.BlockSpec((tr, c), lambda i: (i, 0))
    return pl.pallas_call(
        body, name=name, grid=(r // tr,), in_specs=[spec, spec], out_specs=spec,
        out_shape=jax.ShapeDtypeStruct((r, c), a.dtype), compiler_params=_cp(("arbitrary",)),
    )(a, b)


BIG = ("w_in", "w_mlp1", "w_attn_out", "w_ssd_out", "w_o", "w_mlp2")
COL_SHARDED = ("w_mlp1", "w_in")
ROW_SHARDED = ("w_attn_out", "w_ssd_out", "w_o", "w_mlp2")
LATE = ROW_SHARDED + ("w_mlp1",)
SMALL = ("b_ada", "norm1_w", "norm2_w", "q_norm_w", "k_norm_w", "conv_b", "A_log", "dt_bias", "ssd_D", "ssd_norm_w")
NAMES = ("w_ada", "b_ada", "norm1_w", "norm2_w", "w_in", "q_norm_w", "k_norm_w", "conv_w", "conv_b", "A_log", "dt_bias",
         "ssd_D", "ssd_norm_w", "w_attn_out", "w_ssd_out", "w_o", "w_mlp1", "w_mlp2")
W_IN_COLS = 8768


def _permute_in(w):
    return jnp.concatenate([w[:, 4608:6656], w[:, 6720:8768], w[:, 1536:4608], w[:, 0:1536], w[:, 6656:6720],
                            jnp.zeros((w.shape[0], PW - W_IN_COLS), w.dtype)], axis=1)


def _unpermute_in(wp):
    return jnp.concatenate([wp[:, Q0:DT0], wp[:, XS0:Q0], wp[:, Z0:GA0], wp[:, DT0:DT0 + 64], wp[:, GA0:XS0]], axis=1)


def _pad_to(v, n):
    return jnp.pad(v, (0, n - v.shape[0]))


def _step(w, m, v, loss_target):
    xi, yi, ci = _place()
    chip = 2 * xi + yi
    dev = 4 * xi + 2 * yi + ci
    x, tgt = w["x"], loss_target
    d = x.shape[1]

    cw = w["conv_w"].shape[1]
    v0 = _pad_to(jnp.concatenate([w["c"].reshape(-1), w["conv_w"].reshape(-1)]), 5120).reshape(8, 640)
    g0 = _allgather8(v0, name="ag_cond").reshape(N_DEV, 5120)
    c_all = g0[:, 0:d]
    conv_w = jnp.concatenate([g0[2 * k, d:d + D_CONV * cw].reshape(D_CONV, cw) for k in range(N_CHIP)], axis=1)
    sc = _silu_cast(c_all, name="silu_c")
    modp = _mm(sc, w["w_ada"].astype(MMD), name="ada_fwd", outs=[F32], tm=8, tn=512)
    g1 = _allgather8(modp, name="ag_mod").reshape(N_DEV, N_DEV, modp.shape[1])
    mod_all = jnp.concatenate([g1[2 * k] for k in range(N_CHIP)], axis=1)
    mod = (lax.dynamic_slice_in_dim(mod_all, dev, 1, axis=0) + w["b_ada"]).reshape(6, d)

    mine, mod = lax.optimization_barrier((w["w_in"].astype(MMD), mod))
    in_sems, in_src, in_land, in_token = _chips_start(mine, False, True, name="ag_w_in_start")
    mod = mod + in_token[0:1, 0:1]
    small = {n: w[n] for n in SMALL if n != "b_ada"}
    small["conv_w"] = conv_w
    started = {}

    late_mine = jnp.concatenate([w[n].astype(MMD) for n in LATE], axis=0) + in_token[0:1, 0:1].astype(MMD)

    def in_weights(after):
        src, land = _chips_wait(in_sems, in_src, in_land, [*after, late_mine], False, True, name="ag_w_in_wait")
        land = _pair_complete(land, name="ag_w_in_pair")
        late, land = lax.optimization_barrier((late_mine, land))
        sems, late_src, late_land, token = _chips_start(late, False, name="ag_late_start")
        started["ag_late"] = (sems, late_src, late_land)
        w_in = jnp.concatenate([jnp.where(chip == k, src, land[k]) for k in range(N_CHIP)], axis=1)
        w_dt = jnp.pad(w_in[:, 6656:6720], ((0, 0), (0, 64))) + token[0:1, 0:1].astype(MMD)
        return {"w_in_p": _permute_in(w_in), "w_dt": w_dt}

    def late_weights(after):
        src, land = _chips_wait(*started["ag_late"], after, False, name="ag_late_wait")
        out, o = {}, 0
        for n in LATE:
            rows = w[n].shape[0]
            parts = [jnp.where(chip == k, src[o:o + rows], land[k, o:o + rows]) for k in range(N_CHIP)]
            out[n] = jnp.concatenate(parts, axis=1 if n in COL_SHARDED else 0)
            o += rows
        return out

    def pair_sums(slots, tag):
        _, rows, cols = slots.shape
        hr = rows // 2
        theirs = _pair_swap(slots, name="rs_pair_" + tag)
        ours = lax.dynamic_slice_in_dim(slots, ci * hr, hr, axis=1)
        pair = _add2(ours.reshape(N_CHIP * hr, cols), theirs.reshape(N_CHIP * hr, cols), name="rs_pair_sum_" + tag)
        return pair.reshape(N_CHIP, hr, cols)

    def finish(recv, pair, tag):
        half = _sum_slots(recv, pair, name="rs_sum_" + tag)
        other = _sibling_copy(half, name="rs_sibling_" + tag)
        return jnp.where(ci == 0, jnp.concatenate([half, other], axis=0), jnp.concatenate([other, half], axis=0))

    def late_grads(gw):
        slots = []
        for k in range(N_CHIP):
            parts = []
            for n in LATE:
                rows = w[n].shape[0]
                blk = gw[n][:, k * rows:(k + 1) * rows] if n in COL_SHARDED else gw[n][k * rows:(k + 1) * rows]
                parts.append(blk.astype(MMD))
            slots.append(jnp.concatenate(parts, axis=0))
        pair = pair_sums(jnp.stack(slots), "late")
        sems, src, land, token = _chips_start(pair, True, name="rs_late_start")
        started["late"] = (sems, src, land)
        return token[0:1, 0:1]

    def in_grad(g):
        g_in = _unpermute_in(g)
        cols_in = w["w_in"].shape[1]
        pair = pair_sums(jnp.stack([g_in[:, k * cols_in:(k + 1) * cols_in].astype(MMD) for k in range(N_CHIP)]), "w_in")
        sems, src, land, token = _chips_start(pair, True, name="rs_w_in_start")
        started["w_in"] = (sems, src, land)
        return token

    loss, grad_x, dmod, gw, gs = _local_step(x, tgt, mod, {}, small, in_weights, late_weights, late_grads, in_grad,
                                             in_token[0, 0])

    grads = {}
    pair, land = _chips_wait(*started["w_in"], grad_x, True, name="rs_w_in_wait")
    grads["w_in"] = finish(land, pair, "w_in")
    pair, land = _chips_wait(*started["late"], grad_x, True, name="rs_late_wait")
    total, o = finish(land, pair, "late"), 0
    for n in LATE:
        rows = w[n].shape[0]
        grads[n] = total[o:o + rows]
        o += rows

    order = ([dmod.reshape(-1)] + [gs[n].reshape(-1) for n in SMALL if n != "b_ada"] + [gs["conv_w"].reshape(-1)]
             + [loss.reshape(-1)])
    vec = jnp.concatenate(order)
    n_small = vec.shape[0]
    n_pad = -(-n_small // 1024) * 1024
    g2 = _allgather8(_pad_to(vec, n_pad).reshape(8, n_pad // 8), name="ag_small")
    tot = _rows_sum(g2, N_DEV, name="small_sum").reshape(-1)
    loss = tot[n_small - 1]
    dmod_all = g2.reshape(N_DEV, n_pad)[:, 0:6 * d]
    off = 0
    for n in SMALL:
        grads[n] = tot[off:off + w[n].size].reshape(w[n].shape)
        off += w[n].size
    conv_full = tot[off:off + D_CONV * N_CHIP * cw].reshape(D_CONV, N_CHIP * cw)
    grads["conv_w"] = lax.dynamic_slice_in_dim(conv_full, chip * cw, cw, axis=1)
    ada_cols = w["w_ada"].shape[1]
    dmod_mine = lax.dynamic_slice_in_dim(dmod_all, chip * ada_cols, ada_cols, axis=1).astype(MMD)
    grads["w_ada"] = _mm_tn(sc, dmod_mine, name="ada_dw", tk=512, tn=512, tmm=8)

    delta, new_m, new_v = {}, {}, {}
    pack = lambda t: jnp.concatenate([t[n].reshape(-1) for n in SMALL]).reshape(1, -1)
    ds_, ms_, vs_ = _adamw(pack(w), pack(grads), pack(m), pack(v), name="adamw_small")
    off = 0
    for n in SMALL:
        for dst, src in ((delta, ds_), (new_m, ms_), (new_v, vs_)):
            dst[n] = src[0, off:off + w[n].size].reshape(w[n].shape)
        off += w[n].size
    for n in ("w_ada", "conv_w") + BIG:
        delta[n], new_m[n], new_v[n] = _adamw(w[n], grads[n], m[n], v[n], name="adamw_" + n)
    return loss, grad_x, grads, delta, new_m, new_v


def kernel(x, c, w_ada, b_ada, norm1_w, norm2_w, w_in, q_norm_w, k_norm_w, conv_w, conv_b, A_log, dt_bias, ssd_D, ssd_norm_w, w_attn_out, w_ssd_out, w_o, w_mlp1, w_mlp2, loss_target, m_w_ada, m_b_ada, m_norm1_w, m_norm2_w, m_w_in, m_q_norm_w, m_k_norm_w, m_conv_w, m_conv_b, m_A_log, m_dt_bias, m_ssd_D, m_ssd_norm_w, m_w_attn_out, m_w_ssd_out, m_w_o, m_w_mlp1, m_w_mlp2, v_w_ada, v_b_ada, v_norm1_w, v_norm2_w, v_w_in, v_q_norm_w, v_k_norm_w, v_conv_w, v_conv_b, v_A_log, v_dt_bias, v_ssd_D, v_ssd_norm_w, v_w_attn_out, v_w_ssd_out, v_w_o, v_w_mlp1, v_w_mlp2):
    args = dict(locals())
    strip = lambda a: a[0] if a.ndim == 3 else a
    w = {n: strip(args[n]) for n in NAMES + ("x", "c")}
    m = {n: strip(args["m_" + n]) for n in NAMES}
    v = {n: strip(args["v_" + n]) for n in NAMES}
    loss, grad_x, grads, delta, new_m, new_v = _step(w, m, v, loss_target[0])
    like = lambda t, n: t.reshape(args[n].shape)
    return (loss, grad_x[None], *[like(grads[n], n) for n in NAMES], *[like(delta[n], n) for n in NAMES],
            *[like(new_m[n], n) for n in NAMES], *[like(new_v[n], n) for n in NAMES])
```

```python
import math

import jax
import jax.numpy as jnp
from jax import lax
from jax.experimental import pallas as pl
from jax.experimental.pallas import tpu as pltpu

F32 = jnp.float32
MMD = jnp.bfloat16
EPS = 1e-6
NEG = -1e30
MIB = 1024 * 1024
VMEM_BIG = 56 * MIB
VMEM_MID = 40 * MIB

GRID_W = 64
N_Q_HEADS, N_KV_HEADS, HEAD_DIM = 16, 4, 64
ROPE_THETA = 10000.0
SSD_HEADS, SSD_GROUPS, SSD_P, SSD_N, CHUNK = 32, 4, 64, 128, 128
HPG = SSD_HEADS // SSD_GROUPS
D_CONV = 5
ADAM_LR, ADAM_B1, ADAM_B2, ADAM_EPS, ADAM_WD, ADAM_STEP = 0.001, 0.9, 0.999, 1e-08, 0.01, 10

Z0, GA0, GS0, XS0, B0, C0, Q0, K0, V0, DT0, PW = 0, 2048, 3072, 4096, 6144, 6656, 7168, 8192, 8448, 8704, 8832

MESH = pl.DeviceIdType.MESH
NT = (((1,), (1,)), ((), ()))
TN = (((0,), (0,)), ((), ()))


def _cp(sem=None, vmem=VMEM_MID):
    return pltpu.CompilerParams(dimension_semantics=sem, vmem_limit_bytes=vmem)


def _tile(n, pref):
    t = min(n, pref)
    while n % t:
        t //= 2
    return t


def _dot(a, b, dims=None):
    if dims is None:
        return jnp.dot(a, b, preferred_element_type=F32)
    return lax.dot_general(a, b, dims, preferred_element_type=F32)


def _dot_hi(a01, b):
    a = a01.astype(jnp.bfloat16)
    h1 = b.astype(jnp.bfloat16)
    r1 = b - h1.astype(F32)
    h2 = r1.astype(jnp.bfloat16)
    return _dot(a, h1) + _dot(a, h2) + _dot(a, (r1 - h2.astype(F32)).astype(jnp.bfloat16))


def _sigmoid(x):
    return jax.nn.sigmoid(x)


def _mm(a, b, *, name, outs, nt=False, ta=False, extras=(), epi=None, tm=512, tn=512, n=None, b_outer=False,
        vmem=VMEM_MID):
    assert not (nt and ta)
    k, m = a.shape if ta else a.shape[::-1]
    if n is None:
        n = b.shape[0] if nt else b.shape[1]
    tm, tn = _tile(m, tm), _tile(n, tn)
    gi, gj = m // tm, n // tn
    if b_outer:
        grid = (gj, gi)
        ij = lambda p, q: (q, p)
    else:
        grid = (gi, gj)
        ij = lambda p, q: (p, q)
    if ta:
        a_spec = pl.BlockSpec((k, tm), lambda p, q: (0, ij(p, q)[0]))
    else:
        a_spec = pl.BlockSpec((tm, k), lambda p, q: (ij(p, q)[0], 0))
    if nt:
        b_spec = pl.BlockSpec((tn, k), lambda p, q: (ij(p, q)[1], 0))
    else:
        b_spec = pl.BlockSpec((k, tn), lambda p, q: (0, ij(p, q)[1]))
    e_specs = []
    for arr, kind, off in extras:
        ob = off // tn
        assert off % tn == 0
        if kind == "tile":
            e_specs.append(pl.BlockSpec((tm, tn), lambda p, q, ob=ob: (ij(p, q)[0], ob + ij(p, q)[1])))
        else:
            e_specs.append(pl.BlockSpec((1, tn), lambda p, q, ob=ob: (0, ob + ij(p, q)[1])))
    ne = len(extras)

    def body(a_ref, b_ref, *rest):
        acc = _dot(a_ref[...], b_ref[...], NT if nt else (TN if ta else None))
        res = epi(acc, *[e[...] for e in rest[:ne]]) if epi is not None else (acc,)
        for o_ref, r in zip(rest[ne:], res):
            o_ref[...] = r.astype(o_ref.dtype)

    out = pl.pallas_call(
        body, name=name, grid=grid,
        in_specs=[a_spec, b_spec] + e_specs,
        out_specs=[pl.BlockSpec((tm, tn), lambda p, q: ij(p, q)) for _ in outs],
        out_shape=[jax.ShapeDtypeStruct((m, n), dt) for dt in outs],
        compiler_params=_cp(("arbitrary", "arbitrary"), vmem),
    )(a, b, *[e[0] for e in extras])
    return out if len(outs) > 1 else out[0]


def _mm_tn(a, g, *, name, tk=512, tn=1024, tmm=4096, vmem=VMEM_MID):
    m, k = a.shape
    n = g.shape[1]
    tk, tn, tmm = _tile(k, tk), _tile(n, tn), _tile(m, tmm)

    def body(a_ref, g_ref, o_ref):
        p = _dot(a_ref[...], g_ref[...], TN)

        @pl.when(pl.program_id(2) == 0)
        def _():
            o_ref[...] = p

        @pl.when(pl.program_id(2) > 0)
        def _():
            o_ref[...] += p

    return pl.pallas_call(
        body, name=name, grid=(k // tk, n // tn, m // tmm),
        in_specs=[pl.BlockSpec((tmm, tk), lambda i, j, r: (r, i)), pl.BlockSpec((tmm, tn), lambda i, j, r: (r, j))],
        out_specs=pl.BlockSpec((tk, tn), lambda i, j, r: (i, j)),
        out_shape=jax.ShapeDtypeStruct((k, n), F32),
        compiler_params=_cp(("arbitrary", "arbitrary", "arbitrary"), vmem),
    )(a, g)


def _adamw(w, g, m, v, *, name):
    r, c = w.shape
    tr = _tile(r, 256) if r % 8 == 0 else r

    def body(w_ref, g_ref, m_ref, v_ref, d_ref, nm_ref, nv_ref):
        gg = g_ref[...]
        nm = ADAM_B1 * m_ref[...] + (1.0 - ADAM_B1) * gg
        nv = ADAM_B2 * v_ref[...] + (1.0 - ADAM_B2) * jnp.square(gg)
        m_hat = nm / (1.0 - ADAM_B1 ** ADAM_STEP)
        v_hat = nv / (1.0 - ADAM_B2 ** ADAM_STEP)
        d_ref[...] = -ADAM_LR * (m_hat / (jnp.sqrt(v_hat) + ADAM_EPS) + ADAM_WD * w_ref[...])
        nm_ref[...] = nm
        nv_ref[...] = nv

    spec = pl.BlockSpec((tr, c), lambda i: (i, 0))
    return pl.pallas_call(
        body, name=name, grid=(r // tr,), in_specs=[spec] * 4, out_specs=[spec] * 3,
        out_shape=[jax.ShapeDtypeStruct((r, c), F32)] * 3, compiler_params=_cp(("arbitrary",)),
    )(w, g, m, v)


def _rows_sum(a, groups, *, name):
    r = a.shape[0] // groups

    def body(a_ref, o_ref):
        acc = a_ref[0:r, :]
        for d in range(1, groups):
            acc = acc + a_ref[d * r:(d + 1) * r, :]
        o_ref[...] = acc

    return pl.pallas_call(body, name=name, out_shape=jax.ShapeDtypeStruct((r, a.shape[1]), F32))(a)


def _silu_cast(a, *, name):
    def body(a_ref, o_ref):
        x = a_ref[...]
        o_ref[...] = (x * _sigmoid(x)).astype(o_ref.dtype)

    return pl.pallas_call(body, name=name, out_shape=jax.ShapeDtypeStruct(a.shape, MMD))(a)


def _sumsq(a, *, name):
    m, n = a.shape
    tm = _tile(m, 512)

    def body(a_ref, o_ref):
        x = a_ref[...]
        p = jnp.sum(jnp.sum(x * x, axis=1, keepdims=True), axis=0, keepdims=True)

        @pl.when(pl.program_id(0) == 0)
        def _():
            o_ref[...] = p

        @pl.when(pl.program_id(0) > 0)
        def _():
            o_ref[...] += p

    return pl.pallas_call(
        body, name=name, grid=(m // tm,), in_specs=[pl.BlockSpec((tm, n), lambda i: (i, 0))],
        out_specs=pl.BlockSpec((1, 1), lambda i: (0, 0)), out_shape=jax.ShapeDtypeStruct((1, 1), F32),
        compiler_params=_cp(("arbitrary",)),
    )(a)


def _acc_rows(o_ref, p, first):
    @pl.when(first)
    def _():
        o_ref[...] = p

    @pl.when(jnp.logical_not(first))
    def _():
        o_ref[...] += p


def _ln_mod(x, w, scale, shift, *, name):
    s, d = x.shape
    tm = _tile(s, 512)

    def body(x_ref, w_ref, sc_ref, sh_ref, o_ref):
        xv = x_ref[...]
        r = lax.rsqrt(jnp.mean(xv * xv, axis=-1, keepdims=True) + EPS)
        o_ref[...] = ((xv * r) * w_ref[...] * (1.0 + sc_ref[...]) + sh_ref[...]).astype(o_ref.dtype)

    row = pl.BlockSpec((1, d), lambda i: (0, 0))
    big = pl.BlockSpec((tm, d), lambda i: (i, 0))
    return pl.pallas_call(
        body, name=name, grid=(s // tm,), in_specs=[big, row, row, row], out_specs=big,
        out_shape=jax.ShapeDtypeStruct((s, d), MMD), compiler_params=_cp(("arbitrary",)),
    )(x, w, scale, shift)


def _ln_mod_bwd(dh, x, w, scale, dres, *, name):
    s, d = x.shape
    tm = _tile(s, 512)

    def body(dh_ref, x_ref, w_ref, sc_ref, dres_ref, dx_ref, dsh_ref, dsc_ref, dw_ref):
        xv = x_ref[...]
        dhv = dh_ref[...].astype(F32)
        r = lax.rsqrt(jnp.mean(xv * xv, axis=-1, keepdims=True) + EPS)
        nv = xv * r
        wv = w_ref[...]
        g1 = 1.0 + sc_ref[...]
        dn = dhv * (wv * g1)
        dx_ref[...] = dres_ref[...] + r * (dn - nv * jnp.mean(dn * nv, axis=-1, keepdims=True))
        first = pl.program_id(0) == 0
        _acc_rows(dsh_ref, jnp.sum(dhv, axis=0, keepdims=True), first)
        _acc_rows(dsc_ref, jnp.sum(dhv * nv * wv, axis=0, keepdims=True), first)
        _acc_rows(dw_ref, jnp.sum(dhv * nv * g1, axis=0, keepdims=True), first)

    row = pl.BlockSpec((1, d), lambda i: (0, 0))
    big = pl.BlockSpec((tm, d), lambda i: (i, 0))
    return pl.pallas_call(
        body, name=name, grid=(s // tm,), in_specs=[big, big, row, row, big], out_specs=[big, row, row, row],
        out_shape=[jax.ShapeDtypeStruct((s, d), F32)] + [jax.ShapeDtypeStruct((1, d), F32)] * 3,
        compiler_params=_cp(("arbitrary",)),
    )(dh, x, w, scale, dres)


def _gate_bwd(dy, u, gate, *, name):
    s, d = dy.shape
    tm = _tile(s, 512)

    def body(dy_ref, u_ref, g_ref, du_ref, dg_ref):
        dyv = dy_ref[...]
        du_ref[...] = (dyv * g_ref[...]).astype(du_ref.dtype)
        _acc_rows(dg_ref, jnp.sum(dyv * u_ref[...].astype(F32), axis=0, keepdims=True), pl.program_id(0) == 0)

    row = pl.BlockSpec((1, d), lambda i: (0, 0))
    big = pl.BlockSpec((tm, d), lambda i: (i, 0))
    return pl.pallas_call(
        body, name=name, grid=(s // tm,), in_specs=[big, big, row], out_specs=[big, row],
        out_shape=[jax.ShapeDtypeStruct((s, d), MMD), jax.ShapeDtypeStruct((1, d), F32)],
        compiler_params=_cp(("arbitrary",)),
    )(dy, u, gate)


def _seg64(v, e):
    hi = v.astype(jnp.bfloat16)
    lo = (v - hi.astype(F32)).astype(jnp.bfloat16)
    return _dot(hi, e) + _dot(lo, e)


def _rope_tables(s, zero=0.0):
    rows = s // GRID_W
    pos_row = jnp.repeat(jnp.arange(rows, dtype=jnp.int32), GRID_W).astype(F32) + zero
    pos_col = jnp.tile(jnp.arange(GRID_W, dtype=jnp.int32), rows).astype(F32) + zero
    axis_dim = HEAD_DIM // 2
    inv_freq = ROPE_THETA ** (-jnp.arange(0, axis_dim, 2, dtype=F32) / axis_dim)
    ang_r = pos_row[:, None] * inv_freq[None, :]
    ang_c = pos_col[:, None] * inv_freq[None, :]
    zero = jnp.zeros_like(ang_r)
    cos = jnp.concatenate([jnp.cos(ang_r), jnp.cos(ang_r), jnp.cos(ang_c), jnp.cos(ang_c)], axis=1)
    s_a = jnp.concatenate([-jnp.sin(ang_r), zero, -jnp.sin(ang_c), zero], axis=1)
    s_b = jnp.concatenate([zero, jnp.sin(ang_r), zero, jnp.sin(ang_c)], axis=1)
    return [jnp.tile(t, (1, 2)) for t in (cos, s_a, s_b)]


def _e128():
    i = jnp.arange(128)
    return (i[:, None] // 64 == i[None, :] // 64).astype(jnp.bfloat16)


QKW = N_Q_HEADS * HEAD_DIM + N_KV_HEADS * HEAD_DIM


def _qk_fwd(proj, wrow, scrow, tabs, *, name):
    s = proj.shape[0]
    tm = _tile(s, 1024)

    def body(x_ref, w_ref, sc_ref, cos_ref, sa_ref, sb_ref, e_ref, ot_ref):
        u = x_ref[...].astype(F32)
        r = lax.rsqrt(_seg64(u * u, e_ref[...]) * (1.0 / HEAD_DIM) + EPS)
        nv = (u * r) * w_ref[...]
        ro = nv * cos_ref[...] + pltpu.roll(nv, 112, 1) * sa_ref[...] + pltpu.roll(nv, 16, 1) * sb_ref[...]
        ot_ref[...] = (ro * sc_ref[...]).T.astype(ot_ref.dtype)

    tab = pl.BlockSpec((tm, 128), lambda i, j: (i, 0))
    row = pl.BlockSpec((1, 128), lambda i, j: (0, j))
    return pl.pallas_call(
        body, name=name, grid=(s // tm, QKW // 128),
        in_specs=[pl.BlockSpec((tm, 128), lambda i, j: (i, Q0 // 128 + j)), row, row, tab, tab, tab,
                  pl.BlockSpec((128, 128), lambda i, j: (0, 0))],
        out_specs=pl.BlockSpec((128, tm), lambda i, j: (j, i)),
        out_shape=jax.ShapeDtypeStruct((QKW, s), MMD), compiler_params=_cp(("arbitrary", "arbitrary")),
    )(proj, wrow, scrow, *tabs, _e128())


def _qk_bwd(dqt, dkt, proj, wrow, scrow, tabs, dproj, *, name):
    s = proj.shape[0]
    tm = _tile(s, 1024)
    nq = dqt.shape[0] // 128

    def body(dq_ref, dk_ref, x_ref, w_ref, sc_ref, cos_ref, sa_ref, sb_ref, e_ref, _, du_ref, dw_ref):
        e = e_ref[...]
        d = jnp.where(pl.program_id(0) < nq, dq_ref[...], dk_ref[...]).T * sc_ref[...]
        dn = d * cos_ref[...] + pltpu.roll(d * sa_ref[...], 16, 1) + pltpu.roll(d * sb_ref[...], 112, 1)
        u = x_ref[...].astype(F32)
        r = lax.rsqrt(_seg64(u * u, e) * (1.0 / HEAD_DIM) + EPS)
        uh = u * r
        _acc_rows(dw_ref, jnp.sum(dn * uh, axis=0, keepdims=True), pl.program_id(1) == 0)
        dnw = dn * w_ref[...]
        du_ref[...] = (r * (dnw - uh * (_seg64(dnw * uh, e) * (1.0 / HEAD_DIM)))).astype(du_ref.dtype)

    tab = pl.BlockSpec((tm, 128), lambda j, i: (i, 0))
    row = pl.BlockSpec((1, 128), lambda j, i: (0, j))
    qcol = pl.BlockSpec((tm, 128), lambda j, i: (i, Q0 // 128 + j))
    return pl.pallas_call(
        body, name=name, grid=(QKW // 128, s // tm),
        in_specs=[pl.BlockSpec((128, tm), lambda j, i: (jnp.minimum(j, nq - 1), i)),
                  pl.BlockSpec((128, tm), lambda j, i: (jnp.maximum(j - nq, 0), i)),
                  qcol, row, row, tab, tab, tab, pl.BlockSpec((128, 128), lambda j, i: (0, 0)), ANY],
        out_specs=[qcol, row],
        out_shape=[jax.ShapeDtypeStruct(dproj.shape, dproj.dtype), jax.ShapeDtypeStruct((1, QKW), F32)],
        input_output_aliases={9: 0}, compiler_params=_cp(("arbitrary", "arbitrary")),
    )(dqt, dkt, proj, wrow, scrow, *tabs, _e128(), dproj)


REP = N_Q_HEADS // N_KV_HEADS


def _lanes(ref):
    return jnp.concatenate([ref[r] for r in range(REP)], axis=1)


V_AUG = HEAD_DIM + 8
LOG2E = math.log2(math.e)


def _flash_fwd(qkt, vta, *, name):
    s = qkt.shape[2]
    tq, tk = _tile(s, 1024), _tile(s, 512)
    nk = s // tk
    lanes = REP * tq

    def body(q_ref, k_ref, v_ref, o_ref, lse_ref, m_ref, acc_ref):
        j = pl.program_id(2)

        @pl.when(j == 0)
        def _():
            m_ref[...] = jnp.full_like(m_ref, NEG)
            acc_ref[...] = jnp.zeros_like(acc_ref)

        st = _dot(k_ref[0], _lanes(q_ref), TN)
        m_prev = m_ref[...]
        m_new = jnp.maximum(m_prev, jnp.max(st, axis=0, keepdims=True))
        p = jnp.exp2(st - m_new).astype(MMD)
        acc_ref[...] = jnp.exp2(m_prev - m_new) * acc_ref[...] + _dot(v_ref[0], p)
        m_ref[...] = m_new

        @pl.when(j == nk - 1)
        def _():
            acc = acc_ref[...]
            l = acc[HEAD_DIM:HEAD_DIM + 1]
            o = acc[0:HEAD_DIM] / l
            ls = m_ref[...] + jnp.log(l) * LOG2E
            for r in range(REP):
                o_ref[r] = o[:, r * tq:(r + 1) * tq].astype(o_ref.dtype)
                lse_ref[r] = ls[:, r * tq:(r + 1) * tq]

    qspec = pl.BlockSpec((REP, HEAD_DIM, tq), lambda g, i, j: (g, 0, i))
    return pl.pallas_call(
        body, name=name, grid=(N_KV_HEADS, s // tq, nk),
        in_specs=[qspec, pl.BlockSpec((1, HEAD_DIM, tk), lambda g, i, j: (N_Q_HEADS + g, 0, j)),
                  pl.BlockSpec((1, V_AUG, tk), lambda g, i, j: (g, 0, j))],
        out_specs=[qspec, pl.BlockSpec((REP, 1, tq), lambda g, i, j: (g, 0, i))],
        out_shape=[jax.ShapeDtypeStruct((N_Q_HEADS, HEAD_DIM, s), MMD), jax.ShapeDtypeStruct((N_Q_HEADS, 1, s), F32)],
        scratch_shapes=[pltpu.VMEM((1, lanes), F32), pltpu.VMEM((V_AUG, lanes), F32)],
        compiler_params=_cp(("arbitrary", "arbitrary", "arbitrary"), VMEM_BIG),
    )(qkt, qkt, vta)


def _flash_bwd(qkt, vta, dot, ot, lse, *, name):
    s = qkt.shape[2]
    tq, tk = _tile(s, 512), _tile(s, 1024)
    nk = s // tk

    def body(q_ref, kt_ref, vt_ref, do_ref, o_ref, lse_ref, dq_ref, dk_ref, dv_ref, dq_acc):
        i, j = pl.program_id(1), pl.program_id(2)
        q, do = _lanes(q_ref), _lanes(do_ref)
        delta = jnp.sum(do.astype(F32) * _lanes(o_ref).astype(F32), axis=0, keepdims=True)
        kt, vt = kt_ref[0], vt_ref[0, 0:HEAD_DIM, :]
        p = jnp.exp2(_dot(kt, q, TN) - _lanes(lse_ref))
        dvc = _dot(p.astype(MMD), do, NT)
        ds = (p * (_dot(vt, do, TN) - delta)).astype(MMD)
        dkc = _dot(ds, q, NT) * (1.0 / LOG2E)
        dqc = _dot(kt, ds)
        rows = pl.ds(pl.multiple_of(j * tk, tk), tk)

        @pl.when(i == 0)
        def _():
            dk_ref[0, rows, :] = dkc
            dv_ref[0, rows, :] = dvc

        @pl.when(i > 0)
        def _():
            dk_ref[0, rows, :] += dkc
            dv_ref[0, rows, :] += dvc

        @pl.when(j == 0)
        def _():
            dq_acc[...] = dqc

        @pl.when(j > 0)
        def _():
            dq_acc[...] += dqc

        @pl.when(j == nk - 1)
        def _():
            acc = dq_acc[...]
            for r in range(REP):
                dq_ref[r] = acc[:, r * tq:(r + 1) * tq]

    qspec = pl.BlockSpec((REP, HEAD_DIM, tq), lambda g, i, j: (g, 0, i))
    kvres = pl.BlockSpec((1, s, HEAD_DIM), lambda g, i, j: (g, 0, 0))
    return pl.pallas_call(
        body, name=name, grid=(N_KV_HEADS, s // tq, nk),
        in_specs=[qspec, pl.BlockSpec((1, HEAD_DIM, tk), lambda g, i, j: (N_Q_HEADS + g, 0, j)),
                  pl.BlockSpec((1, V_AUG, tk), lambda g, i, j: (g, 0, j)),
                  qspec, qspec, pl.BlockSpec((REP, 1, tq), lambda g, i, j: (g, 0, i))],
        out_specs=[qspec, kvres, kvres],
        out_shape=[jax.ShapeDtypeStruct((N_Q_HEADS, HEAD_DIM, s), F32), jax.ShapeDtypeStruct((N_KV_HEADS, s, HEAD_DIM), F32),
                   jax.ShapeDtypeStruct((N_KV_HEADS, s, HEAD_DIM), F32)],
        scratch_shapes=[pltpu.VMEM((HEAD_DIM, REP * tq), F32)],
        compiler_params=_cp(("arbitrary", "arbitrary", "arbitrary"), VMEM_BIG),
    )(qkt, qkt, vta, dot, ot, lse)


HALO = 8
CONV_W = 2048 + 2 * SSD_GROUPS * SSD_N


def _shifted(win, off, r):
    return pltpu.roll(win, (r + 2 * HALO - off) % (r + 2 * HALO), 0)[0:r]


def _conv_fwd(proj, w8, brow, *, name):
    s = proj.shape[0]
    cb = 256
    r = _tile(s, 512)

    def body(x_ref, w_ref, b_ref, o_ref, pad_ref):
        zeros = jnp.zeros((HALO, cb), F32)
        pad_ref[0:HALO, :] = zeros
        pad_ref[s + HALO:s + 2 * HALO, :] = zeros

        def fill(i, carry):
            st = pl.multiple_of(i * r, r)
            pad_ref[pl.ds(st + HALO, r), :] = x_ref[pl.ds(st, r), :].astype(F32)
            return carry

        lax.fori_loop(0, s // r, fill, 0)
        wv = w_ref[...]
        bv = b_ref[...]

        def step(i, carry):
            st = pl.multiple_of(i * r, r)
            win = pad_ref[pl.ds(st, r + 2 * HALO), :]
            acc = bv + wv[0:1, :] * _shifted(win, HALO - 2, r)
            for t in range(1, D_CONV):
                acc = acc + wv[t:t + 1, :] * _shifted(win, HALO - 2 + t, r)
            o_ref[pl.ds(st, r), :] = (acc * _sigmoid(acc)).astype(o_ref.dtype)
            return carry

        lax.fori_loop(0, s // r, step, 0)

    return pl.pallas_call(
        body, name=name, grid=(CONV_W // cb,),
        in_specs=[pl.BlockSpec((s, cb), lambda j: (0, XS0 // cb + j)), pl.BlockSpec((8, cb), lambda j: (0, j)),
                  pl.BlockSpec((1, cb), lambda j: (0, j))],
        out_specs=pl.BlockSpec((s, cb), lambda j: (0, j)),
        out_shape=jax.ShapeDtypeStruct((s, CONV_W), MMD),
        scratch_shapes=[pltpu.VMEM((s + 2 * HALO, cb), F32)],
        compiler_params=_cp(("arbitrary",), VMEM_MID),
    )(proj, w8, brow)


def _conv_bwd(proj, col0, ga, gb, w8, brow, dproj, *, name):
    s = proj.shape[0]
    width = ga.shape[1]
    cb = 128
    c0 = col0 // cb
    r = _tile(s, 512)

    def body(x_ref, ga_ref, gb_ref, w_ref, b_ref, _, dx_ref, dw_ref, db_ref, xpad, dpad):
        zeros = jnp.zeros((HALO, cb), F32)
        for ref in (xpad, dpad):
            ref[0:HALO, :] = zeros
            ref[s + HALO:s + 2 * HALO, :] = zeros

        def fill(i, carry):
            st = pl.multiple_of(i * r, r)
            xpad[pl.ds(st + HALO, r), :] = x_ref[pl.ds(st, r), :].astype(F32)
            return carry

        lax.fori_loop(0, s // r, fill, 0)
        wv = w_ref[...]
        bv = b_ref[...]

        def first(i, carry):
            st = pl.multiple_of(i * r, r)
            win = xpad[pl.ds(st, r + 2 * HALO), :]
            taps = [_shifted(win, HALO - 2 + t, r) for t in range(D_CONV)]
            u = bv
            for t in range(D_CONV):
                u = u + wv[t:t + 1, :] * taps[t]
            sg = _sigmoid(u)
            du = ((ga_ref[pl.ds(st, r), :].astype(F32) + gb_ref[pl.ds(st, r), :].astype(F32))
                  * (sg * (1.0 + u * (1.0 - sg))))
            dpad[pl.ds(st + HALO, r), :] = du
            out = [carry[0] + jnp.sum(du, axis=0, keepdims=True)]
            for t in range(D_CONV):
                out.append(carry[1 + t] + jnp.sum(du * taps[t], axis=0, keepdims=True))
            return tuple(out)

        sums = lax.fori_loop(0, s // r, first, tuple(jnp.zeros((1, cb), F32) for _ in range(1 + D_CONV)))
        db_ref[...] = sums[0]
        for t in range(D_CONV):
            dw_ref[t:t + 1, :] = sums[1 + t]
        dw_ref[D_CONV:8, :] = jnp.zeros((8 - D_CONV, cb), F32)

        def second(i, carry):
            st = pl.multiple_of(i * r, r)
            win = dpad[pl.ds(st, r + 2 * HALO), :]
            acc = wv[0:1, :] * _shifted(win, HALO + 2, r)
            for t in range(1, D_CONV):
                acc = acc + wv[t:t + 1, :] * _shifted(win, HALO + 2 - t, r)
            dx_ref[pl.ds(st, r), :] = acc.astype(dx_ref.dtype)
            return carry

        lax.fori_loop(0, s // r, second, 0)

    col = pl.BlockSpec((s, cb), lambda j: (0, j))
    xcol = pl.BlockSpec((s, cb), lambda j: (0, XS0 // cb + c0 + j))
    return pl.pallas_call(
        body, name=name, grid=(width // cb,),
        in_specs=[xcol, col, col, pl.BlockSpec((8, cb), lambda j: (0, c0 + j)),
                  pl.BlockSpec((1, cb), lambda j: (0, c0 + j)), ANY],
        out_specs=[xcol, pl.BlockSpec((8, cb), lambda j: (0, j)), pl.BlockSpec((1, cb), lambda j: (0, j))],
        out_shape=[jax.ShapeDtypeStruct(dproj.shape, dproj.dtype), jax.ShapeDtypeStruct((8, width), F32),
                   jax.ShapeDtypeStruct((1, width), F32)],
        scratch_shapes=[pltpu.VMEM((s + 2 * HALO, cb), F32), pltpu.VMEM((s + 2 * HALO, cb), F32)],
        input_output_aliases={5: 0}, compiler_params=_cp(("arbitrary",), VMEM_BIG),
    )(proj, ga, gb, w8, brow, dproj)


def _tri(lower):
    i = jnp.arange(CHUNK)
    return ((i[:, None] >= i[None, :]) if lower else (i[:, None] <= i[None, :])).astype(F32)


def _dt_fwd(raw, bias, arow, *, name):
    s = raw.shape[0]

    def body(r_ref, b_ref, a_ref, lo_ref, up_ref, dt_ref, cs_ref):
        u = r_ref[...] + b_ref[...]
        dt = jnp.maximum(u, 0.0) + jnp.log1p(jnp.exp(-jnp.abs(u)))
        dt_ref[...] = dt
        a = dt * a_ref[...]
        lane = lax.broadcasted_iota(jnp.int32, (CHUNK, 128), 1)
        cs_ref[...] = jnp.where(lane < SSD_HEADS, _dot_hi(lo_ref[...], a), _dot_hi(up_ref[...], a))

    blk = pl.BlockSpec((CHUNK, 128), lambda i: (i, 0))
    row = pl.BlockSpec((1, 128), lambda i: (0, 0))
    tri = pl.BlockSpec((CHUNK, CHUNK), lambda i: (0, 0))
    return pl.pallas_call(
        body, name=name, grid=(s // CHUNK,), in_specs=[blk, row, row, tri, tri], out_specs=[blk, blk],
        out_shape=[jax.ShapeDtypeStruct((s, 128), F32)] * 2, compiler_params=_cp(("arbitrary",)),
    )(raw, bias, arow, _tri(True), _tri(False))


def _dt_bwd(ddt0, ddt1, raw, bias, dproj, *, name):
    s = raw.shape[0]
    tm = _tile(s, 1024)

    def body(d0_ref, d1_ref, r_ref, b_ref, _, o_ref, db_ref):
        g = (d0_ref[...] + d1_ref[...]) * _sigmoid(r_ref[...] + b_ref[...])
        o_ref[...] = g.astype(o_ref.dtype)
        _acc_rows(db_ref, jnp.sum(g, axis=0, keepdims=True), pl.program_id(0) == 0)

    blk = pl.BlockSpec((tm, 128), lambda i: (i, 0))
    row = pl.BlockSpec((1, 128), lambda i: (0, 0))
    return pl.pallas_call(
        body, name=name, grid=(s // tm,), in_specs=[blk, blk, blk, row, ANY],
        out_specs=[pl.BlockSpec((tm, 128), lambda i: (i, DT0 // 128)), row],
        out_shape=[jax.ShapeDtypeStruct(dproj.shape, dproj.dtype), jax.ShapeDtypeStruct((1, 128), F32)],
        input_output_aliases={4: 0}, compiler_params=_cp(("arbitrary",)),
    )(ddt0, ddt1, raw, bias, dproj)


GW = HPG * SSD_P


GPS = SSD_GROUPS


def _ssd_specs(nc, rev):
    cc = (lambda c: nc - 1 - c) if rev else (lambda c: c)
    return dict(
        x=pl.BlockSpec((CHUNK, GPS * GW), lambda g, c: (cc(c), g)),
        b=pl.BlockSpec((CHUNK, GPS * SSD_N), lambda g, c: (cc(c), 2048 // (GPS * SSD_N) + g)),
        c=pl.BlockSpec((CHUNK, GPS * SSD_N), lambda g, c: (cc(c), 2048 // (GPS * SSD_N) + 1 + g)),
        lanes=pl.BlockSpec((CHUNK, 128), lambda g, c: (cc(c), 0)),
        drow=pl.BlockSpec((1, GPS * GW), lambda g, c: (0, g)),
        y=pl.BlockSpec((CHUNK, GPS * GW), lambda g, c: (cc(c), g)),
        h=pl.BlockSpec((GPS, 1, SSD_N, GW), lambda g, c: (g, cc(c), 0, 0)),
        n=pl.BlockSpec((CHUNK, GPS * SSD_N), lambda g, c: (cc(c), g)),
    )


def _ssd_mask(anti):
    ii = lax.broadcasted_iota(jnp.int32, (CHUNK, CHUNK), 0)
    jj = lax.broadcasted_iota(jnp.int32, (CHUNK, CHUNK), 1)
    return ii, jj, (ii <= jj) if anti else (ii >= jj)


def _expand(x, ex, terms=3):
    h1 = x.astype(jnp.bfloat16)
    r1 = x - h1.astype(F32)
    h2 = r1.astype(jnp.bfloat16)
    out = _dot(h1, ex) + _dot(h2, ex)
    if terms == 3:
        out = out + _dot((r1 - h2.astype(F32)).astype(jnp.bfloat16), ex)
    return out


def _headsum(a, e):
    hi = a.astype(jnp.bfloat16)
    return _dot(hi, e) + _dot((a - hi.astype(F32)).astype(jnp.bfloat16), e)


def _expand_mats():
    lane = jnp.arange(128)[None, :, None]
    col = jnp.arange(GW)[None, None, :]
    base = (jnp.arange(2)[:, None] * SSD_HEADS + jnp.arange(SSD_GROUPS)[None, :] * HPG).reshape(2 * SSD_GROUPS, 1, 1)
    return (lane == base + col // SSD_P).astype(jnp.bfloat16)


def _headsum_mats():
    e1 = (jnp.arange(GW)[:, None] // SSD_P == jnp.arange(128)[None, :]).astype(jnp.bfloat16)
    e2 = (jnp.arange(HPG * CHUNK)[:, None] // CHUNK == jnp.arange(128)[None, :]).astype(jnp.bfloat16)
    return e1, e2


def _ssd_fwd(xc, dt, cs, ex, drow, di, *, name):
    s = xc.shape[0]
    nc = s // CHUNK
    anti = di == 1
    sp = _ssd_specs(nc, anti)
    trow = 0 if anti else CHUNK - 1

    def body(x_ref, b_ref, c_ref, dt_ref, cs_ref, ex_ref, d_ref, y_ref, hp_ref, h_ref):
        @pl.when(pl.program_id(1) == 0)
        def _():
            h_ref[...] = jnp.zeros_like(h_ref)

        mask = _ssd_mask(anti)[2]
        dtv, csv = dt_ref[...], cs_ref[...]
        cst = csv.T
        for gi in range(GPS):
            cols = slice(gi * GW, (gi + 1) * GW)
            ncols = slice(gi * SSD_N, (gi + 1) * SSD_N)
            ex = ex_ref[gi]
            xb = x_ref[:, cols].astype(F32)
            bm, cm = b_ref[:, ncols], c_ref[:, ncols]
            csr = cst[SSD_HEADS * di + HPG * gi:SSD_HEADS * di + HPG * (gi + 1)]
            dtf = _expand(dtv, ex, 2)
            csf = _expand(csv, ex, 2)
            tl = csf[trow:trow + 1, :]
            h = h_ref[gi]
            hp_ref[gi, 0] = h.astype(hp_ref.dtype)
            g = _dot(cm, bm, NT)
            xs = xb * dtf
            xsm = xs.astype(MMD)
            base = jnp.exp(csf) * _dot(cm, h.astype(MMD)) + d_ref[:, cols] * xb
            for r in range(HPG):
                sl = slice(r * SSD_P, (r + 1) * SSD_P)
                lm = jnp.exp(jnp.where(mask, csf[:, r * SSD_P:r * SSD_P + 1] - csr[r:r + 1, :], NEG))
                y_ref[:, gi * GW + r * SSD_P:gi * GW + (r + 1) * SSD_P] = (
                    _dot((g * lm).astype(MMD), xsm[:, sl]) + base[:, sl]).astype(y_ref.dtype)
            xd = (xs * jnp.exp(tl - csf)).astype(MMD)
            h_ref[gi] = h * jnp.exp(tl) + _dot(bm, xd, TN)

    return pl.pallas_call(
        body, name=name, grid=(1, nc),
        in_specs=[sp["x"], sp["b"], sp["c"], sp["lanes"], sp["lanes"],
                  pl.BlockSpec((GPS, 128, GW), lambda g, c: (di, 0, 0)), sp["drow"]],
        out_specs=[sp["y"], sp["h"]],
        out_shape=[jax.ShapeDtypeStruct((s, 2048), MMD), jax.ShapeDtypeStruct((SSD_GROUPS, nc, SSD_N, GW), MMD)],
        scratch_shapes=[pltpu.VMEM((GPS, SSD_N, GW), F32)],
        compiler_params=_cp(("arbitrary", "arbitrary")),
    )(xc, xc, xc, dt, cs, ex, drow)


def _ssd_bwd(xc, dt, cs, ex, drow, arow, dy, hprev, di, *, name):
    s = xc.shape[0]
    nc = s // CHUNK
    anti = di == 1
    sp = _ssd_specs(nc, not anti)
    trow = 0 if anti else CHUNK - 1
    e1, e2 = _headsum_mats()

    def body(x_ref, b_ref, c_ref, dt_ref, cs_ref, ex_ref, d_ref, a_ref, dy_ref, hp_ref, tri_ref,
             e1_ref, e2_ref, dx_ref, db_ref, dc_ref, ddt_ref, da_ref, dh_ref, w_ref, dxs_ref):
        @pl.when(pl.program_id(1) == 0)
        def _():
            dh_ref[...] = jnp.zeros_like(dh_ref)
            da_ref[...] = jnp.zeros_like(da_ref)

        e1v = e1_ref[...]
        ii, _, mask = _ssd_mask(anti)
        dtv, csv = dt_ref[...], cs_ref[...]
        cst = csv.T
        ddt_acc = jnp.zeros((CHUNK, 128), F32)
        da_acc = jnp.zeros((1, 128), F32)
        for gi in range(GPS):
            lane0 = SSD_HEADS * di + HPG * gi
            cols = slice(gi * GW, (gi + 1) * GW)
            ncols = slice(gi * SSD_N, (gi + 1) * SSD_N)
            ex = ex_ref[gi]
            xb = x_ref[:, cols].astype(F32)
            bm, cm = b_ref[:, ncols], c_ref[:, ncols]
            csr = cst[lane0:lane0 + HPG]
            dym = dy_ref[:, cols]
            dyb = dym.astype(F32)
            hpm = hp_ref[gi, 0]
            hp = hpm.astype(F32)
            dh = dh_ref[gi]
            dhm = dh.astype(MMD)
            dtf = _expand(dtv, ex, 2)
            csf = _expand(csv, ex, 2)
            tl = csf[trow:trow + 1, :]
            e = jnp.exp(csf)
            dec = jnp.exp(tl - csf)
            et = jnp.exp(tl)
            xs = xb * dtf
            xsm = xs.astype(MMD)
            g = _dot(cm, bm, NT)
            z = _dot(cm, hpm)
            bdh = _dot(bm, dhm)
            dg = jnp.zeros((CHUNK, CHUNK), F32)
            wcols = jnp.zeros((CHUNK, CHUNK), F32)
            for r in range(HPG):
                sl = slice(r * SSD_P, (r + 1) * SSD_P)
                lm = jnp.exp(jnp.where(mask, csf[:, r * SSD_P:r * SSD_P + 1] - csr[r:r + 1, :], NEG))
                mm = g * lm
                dm = _dot(dym[:, sl], xsm[:, sl], NT)
                w = dm * mm
                w_ref[gi, :, r * CHUNK:(r + 1) * CHUNK] = w
                wcols = jnp.where(ii == r, jnp.sum(w, axis=0, keepdims=True), wcols)
                dg = dg + dm * lm
                dxs_ref[gi, :, sl] = _dot(mm.astype(MMD), dym[:, sl], TN)
            dxs = dxs_ref[gi] + dec * bdh
            dx_ref[:, cols] = (dxs * dtf + d_ref[:, cols] * dyb).astype(dx_ref.dtype)
            tb = xs * bdh * dec
            d_tot = jnp.sum(tb, axis=0, keepdims=True) + et * jnp.sum(dh * hp, axis=0, keepdims=True)
            d_tot = _headsum(jnp.broadcast_to(d_tot, (8, GW)), e1v)[0:1]
            dcs = (_headsum(dyb * (e * z) - tb, e1v) + _headsum(w_ref[gi], e2_ref[...]) - wcols.T
                   + jnp.where(ii == trow, d_tot, 0.0))
            da = pltpu.roll(_dot_hi(tri_ref[...], dcs), lane0, 1)
            ddt_acc = ddt_acc + da * a_ref[...] + pltpu.roll(_headsum(dxs * xb, e1v), lane0, 1)
            da_acc = da_acc + jnp.sum(da * dtv, axis=0, keepdims=True)
            dgm = dg.astype(MMD)
            dz = (e * dyb).astype(MMD)
            dc_ref[:, ncols] = (_dot(dgm, bm) + _dot(dz, hpm, NT)).astype(dc_ref.dtype)
            db_ref[:, ncols] = (_dot(dgm, cm, TN) + _dot((xs * dec).astype(MMD), dhm, NT)).astype(db_ref.dtype)
            dh_ref[gi] = dh * et + _dot(cm, dz, TN)
        ddt_ref[...] = ddt_acc
        da_ref[...] += da_acc

    const = lambda shape: pl.BlockSpec(shape, lambda g, c: (0,) * len(shape))
    return pl.pallas_call(
        body, name=name, grid=(1, nc),
        in_specs=[sp["x"], sp["b"], sp["c"], sp["lanes"], sp["lanes"],
                  pl.BlockSpec((GPS, 128, GW), lambda g, c: (di, 0, 0)), sp["drow"],
                  const((1, 128)), sp["y"], sp["h"],
                  const((CHUNK, CHUNK)), const((GW, 128)), const((HPG * CHUNK, 128))],
        out_specs=[sp["y"], sp["n"], sp["n"], sp["lanes"], const((1, 128))],
        out_shape=[jax.ShapeDtypeStruct((s, 2048), MMD), jax.ShapeDtypeStruct((s, SSD_GROUPS * SSD_N), MMD),
                   jax.ShapeDtypeStruct((s, SSD_GROUPS * SSD_N), MMD), jax.ShapeDtypeStruct((s, 128), F32),
                   jax.ShapeDtypeStruct((1, 128), F32)],
        scratch_shapes=[pltpu.VMEM((GPS, SSD_N, GW), F32), pltpu.VMEM((GPS, CHUNK, HPG * CHUNK), F32),
                        pltpu.VMEM((GPS, CHUNK, GW), F32)],
        compiler_params=_cp(("arbitrary", "arbitrary")),
    )(xc, xc, xc, dt, cs, ex, drow, arow, dy, hprev, _tri(anti), e1, e2)


def _gnorm_fwd(ya, yb, proj, w, *, name):
    s = ya.shape[0]
    tm = _tile(s, 256)

    def body(a_ref, b_ref, z_ref, w_ref, o_ref):
        zv = z_ref[...].astype(F32)
        t = (a_ref[...].astype(F32) + b_ref[...].astype(F32)) * (zv * _sigmoid(zv))
        r = lax.rsqrt(jnp.mean(t * t, axis=-1, keepdims=True) + EPS)
        o_ref[...] = ((t * r) * w_ref[...]).astype(o_ref.dtype)

    big = pl.BlockSpec((tm, 2048), lambda i: (i, 0))
    row = pl.BlockSpec((1, 2048), lambda i: (0, 0))
    return pl.pallas_call(
        body, name=name, grid=(s // tm,), in_specs=[big, big, big, row], out_specs=big,
        out_shape=jax.ShapeDtypeStruct((s, 2048), MMD), compiler_params=_cp(("arbitrary",)),
    )(ya, yb, proj, w)


def _gnorm_bwd(dout, ya, yb, proj, xc, w, dproj, *, name):
    s = ya.shape[0]
    tm = _tile(s, 256)

    def body(do_ref, a_ref, b_ref, z_ref, x_ref, w_ref, _, dy_ref, dz_ref, dw_ref, dd_ref):
        zv = z_ref[...].astype(F32)
        sg = _sigmoid(zv)
        sz = zv * sg
        y = a_ref[...].astype(F32) + b_ref[...].astype(F32)
        t = y * sz
        r = lax.rsqrt(jnp.mean(t * t, axis=-1, keepdims=True) + EPS)
        nv = t * r
        dov = do_ref[...].astype(F32)
        _acc_rows(dw_ref, jnp.sum(dov * nv, axis=0, keepdims=True), pl.program_id(0) == 0)
        dn = dov * w_ref[...]
        dt_ = r * (dn - nv * jnp.mean(dn * nv, axis=-1, keepdims=True))
        dy = dt_ * sz
        dy_ref[...] = dy.astype(dy_ref.dtype)
        dz_ref[...] = (dt_ * y * (sg * (1.0 + zv * (1.0 - sg)))).astype(dz_ref.dtype)
        _acc_rows(dd_ref, jnp.sum(dy * x_ref[...].astype(F32), axis=0, keepdims=True), pl.program_id(0) == 0)

    big = pl.BlockSpec((tm, 2048), lambda i: (i, 0))
    row = pl.BlockSpec((1, 2048), lambda i: (0, 0))
    return pl.pallas_call(
        body, name=name, grid=(s // tm,), in_specs=[big, big, big, big, big, row, ANY], out_specs=[big, big, row, row],
        out_shape=[jax.ShapeDtypeStruct((s, 2048), MMD), jax.ShapeDtypeStruct(dproj.shape, dproj.dtype),
                   jax.ShapeDtypeStruct((1, 2048), F32), jax.ShapeDtypeStruct((1, 2048), F32)],
        input_output_aliases={6: 1}, compiler_params=_cp(("arbitrary",)),
    )(dout, ya, yb, proj, xc, w, dproj)


def _unheads(a):
    return a.transpose(1, 0, 2).reshape(a.shape[1], a.shape[0] * HEAD_DIM)


def _local_step(x, target, mod, wts, small, in_weights=None, late_weights=None, late_grads=None, in_grad=None,
                zero=0.0):
    s, d = x.shape
    shift1, scale1, gate1, shift2, scale2, gate2 = [mod[i:i + 1] for i in range(6)]

    h1 = _ln_mod(x, small["norm1_w"], scale1, shift1, name="ln1")
    qk_w = jnp.concatenate([jnp.tile(small["q_norm_w"], (1, N_Q_HEADS)), jnp.tile(small["k_norm_w"], (1, N_KV_HEADS))], axis=1)
    qk_sc = jnp.concatenate([jnp.full((1, N_Q_HEADS * HEAD_DIM), HEAD_DIM ** -0.5, F32),
                             jnp.ones((1, N_KV_HEADS * HEAD_DIM), F32)], axis=1)
    qk_sc2 = jnp.concatenate([jnp.full((1, N_Q_HEADS * HEAD_DIM), HEAD_DIM ** -0.5 * LOG2E, F32),
                              jnp.ones((1, N_KV_HEADS * HEAD_DIM), F32)], axis=1)
    tabs = _rope_tables(s, zero)
    if in_weights is not None:
        wts = {**wts, **in_weights([h1, *tabs])}
    proj = _mm(h1, wts["w_in_p"], name="in_proj", outs=[MMD], tm=512, tn=2944, b_outer=True)
    dt_raw = _mm(h1, wts["w_dt"], name="dt_proj", outs=[F32], tm=512, tn=128)
    qkt = _qk_fwd(proj, qk_w, qk_sc2, tabs, name="qk_fwd").reshape(N_Q_HEADS + N_KV_HEADS, HEAD_DIM, s)
    v_sd = proj[:, V0:V0 + N_KV_HEADS * HEAD_DIM]
    vta = jnp.concatenate([v_sd.T.reshape(N_KV_HEADS, HEAD_DIM, s), jnp.ones((N_KV_HEADS, V_AUG - HEAD_DIM, s), MMD)], axis=1)
    ot, lse = _flash_fwd(qkt, vta, name="flash_fwd")
    ot2 = ot.reshape(N_Q_HEADS * HEAD_DIM, s)
    if late_weights is not None:
        wts = {**wts, **late_weights(ot)}

    w8 = jnp.pad(small["conv_w"], ((0, 8 - D_CONV), (0, 0)))
    xc = _conv_fwd(proj, w8, small["conv_b"], name="conv_fwd")
    a_neg = -jnp.exp(small["A_log"])
    arow = jnp.pad(a_neg.reshape(1, 2 * SSD_HEADS), ((0, 0), (0, 128 - 2 * SSD_HEADS)))
    bias_row = jnp.pad(small["dt_bias"].reshape(1, 2 * SSD_HEADS), ((0, 0), (0, 128 - 2 * SSD_HEADS)))
    dt, cs = _dt_fwd(dt_raw, bias_row, arow, name="dt_fwd")
    drow = jnp.repeat(small["ssd_D"], SSD_P, axis=1)
    dirs = [dict(drow=drow), dict(drow=jnp.zeros_like(drow))]
    ex = _expand_mats()
    ys = []
    for di, dd in enumerate(dirs):
        y, dd["hprev"] = _ssd_fwd(xc, dt, cs, ex, dd["drow"], di, name=f"ssd_fwd{di}")
        ys.append(y)
    ssdn = _gnorm_fwd(ys[0], ys[1], proj, small["ssd_norm_w"], name="gnorm_fwd")

    a_o = _mm(ot2, wts["w_attn_out"], name="attn_out", outs=[MMD], ta=True, tm=512, tn=1024)

    def merge_epi(acc, ao, ga, gs):
        return (_sigmoid(ga.astype(F32)) * ao.astype(F32) + _sigmoid(gs.astype(F32)) * acc, acc)

    merged, b_o = _mm(ssdn, wts["w_ssd_out"], name="ssd_out", outs=[MMD, MMD], tm=512, tn=1024,
                      extras=[(a_o, "tile", 0), (proj, "tile", GA0), (proj, "tile", GS0)], epi=merge_epi)

    def res_epi(acc, res, gate):
        return (res + gate * acc, acc)

    x1, mo = _mm(merged, wts["w_o"], name="w_o", outs=[F32, MMD], tm=512, tn=1024,
                 extras=[(x, "tile", 0), (gate1, "row", 0)], epi=res_epi)
    h2 = _ln_mod(x1, small["norm2_w"], scale2, shift2, name="ln2")

    def relu2_epi(acc):
        rl = jnp.maximum(acc, 0.0)
        return (rl * rl, rl)

    act, rl = _mm(h2, wts["w_mlp1"], name="mlp1", outs=[MMD, MMD], tm=1024, tn=1024, epi=relu2_epi, b_outer=True)

    def loss_epi(acc, res, gate, tgt):
        return ((res + gate * acc - tgt) * (1.0 / d), acc)

    dy, ffo = _mm(act, wts["w_mlp2"], name="mlp2", outs=[F32, MMD], tm=512, tn=1024, vmem=VMEM_BIG,
                  extras=[(x1, "tile", 0), (gate2, "row", 0), (target, "tile", 0)], epi=loss_epi)
    loss = _sumsq(dy, name="loss") * (0.5 * d)

    gw = {}
    gs_ = {}
    dffo, dgate2 = _gate_bwd(dy, ffo, gate2, name="gate2_bwd")
    dpre = _mm(dffo, wts["w_mlp2"], name="mlp2_dx", outs=[MMD], nt=True, tm=1024, tn=1024, b_outer=True,
               extras=[(rl, "tile", 0)], epi=lambda acc, r: (acc * (2.0 * r.astype(F32)),))
    gw["w_mlp2"] = _mm_tn(act, dffo, name="mlp2_dw")
    dh2 = _mm(dpre, wts["w_mlp1"], name="mlp1_dx", outs=[F32], nt=True, tm=1024, tn=1024, vmem=VMEM_BIG)
    gw["w_mlp1"] = _mm_tn(h2, dpre, name="mlp1_dw")
    dx1, dshift2, dscale2, gs_["norm2_w"] = _ln_mod_bwd(dh2, x1, small["norm2_w"], scale2, dy, name="ln2_bwd")
    dmo, dgate1 = _gate_bwd(dx1, mo, gate1, name="gate1_bwd")

    def merge_bwd_epi(acc, ao, bo, ga, gs):
        sa, ss = _sigmoid(ga.astype(F32)), _sigmoid(gs.astype(F32))
        return (acc * sa, acc * ss, acc * ao.astype(F32) * sa * (1.0 - sa), acc * bo.astype(F32) * ss * (1.0 - ss))

    da_o, db_o, dga, dgs = _mm(dmo, wts["w_o"], name="w_o_dx", outs=[MMD] * 4, nt=True, tm=512, tn=1024,
                               extras=[(a_o, "tile", 0), (b_o, "tile", 0), (proj, "tile", GA0), (proj, "tile", GS0)],
                               epi=merge_bwd_epi)
    gw["w_o"] = _mm_tn(merged, dmo, name="w_o_dw")
    dot = _mm(wts["w_attn_out"], da_o, name="attn_out_dx", outs=[MMD], nt=True, tm=1024, tn=1024)
    gw["w_attn_out"] = _mm(ot2, da_o, name="attn_out_dw", outs=[F32], tm=256, tn=512, vmem=VMEM_BIG)
    dssdn = _mm(db_o, wts["w_ssd_out"], name="ssd_out_dx", outs=[MMD], nt=True, tm=512, tn=2048)
    gw["w_ssd_out"] = _mm_tn(ssdn, db_o, name="ssd_out_dw")

    dproj = lax.dynamic_update_slice(lax.empty((s, PW), MMD), jnp.concatenate([dga, dgs], axis=1), (0, GA0))

    norm_w = small["ssd_norm_w"] if late_grads is None else small["ssd_norm_w"] + late_grads(gw)
    dyssd, dproj, gs_["ssd_norm_w"], dd_row = _gnorm_bwd(dssdn, ys[0], ys[1], proj, xc, norm_w, dproj, name="gnorm_bwd")
    gs_["ssd_D"] = dd_row.reshape(SSD_HEADS, SSD_P).sum(axis=1).reshape(1, SSD_HEADS)
    dxc, ddts, das = [], [], []
    for di, dd in enumerate(dirs):
        dxs, dbm, dcm, ddt_d, da_d = _ssd_bwd(xc, dt, cs, ex, dd["drow"], arow, dyssd, dd["hprev"], di, name=f"ssd_bwd{di}")
        dxc.append((dxs, dbm, dcm))
        ddts.append(ddt_d)
        das.append(da_d)
    dw8, db, col0 = [], [], 0
    for part, (ga, gb) in enumerate(zip(*dxc)):
        dproj, dw_part, db_part = _conv_bwd(proj, col0, ga, gb, w8, small["conv_b"], dproj, name=f"conv_bwd{part}")
        dw8.append(dw_part)
        db.append(db_part)
        col0 += ga.shape[1]
    gs_["conv_w"] = jnp.concatenate(dw8, axis=1)[0:D_CONV]
    gs_["conv_b"] = jnp.concatenate(db, axis=1)
    gs_["A_log"] = (das[0] + das[1])[:, 0:2 * SSD_HEADS].reshape(2, SSD_HEADS) * a_neg
    dproj, dbias = _dt_bwd(ddts[0], ddts[1], dt_raw, bias_row, dproj, name="dt_bwd")
    gs_["dt_bias"] = dbias[:, 0:2 * SSD_HEADS].reshape(2, SSD_HEADS)

    dqt, dk_h, dv_h = _flash_bwd(qkt, vta, dot.reshape(N_Q_HEADS, HEAD_DIM, s), ot, lse, name="flash_bwd")
    dproj, dqk_w = _qk_bwd(dqt.reshape(N_Q_HEADS * HEAD_DIM, s), dk_h.transpose(0, 2, 1).reshape(N_KV_HEADS * HEAD_DIM, s),
                           proj, qk_w, qk_sc, tabs, dproj, name="qk_bwd")
    gs_["q_norm_w"] = dqk_w[:, 0:N_Q_HEADS * HEAD_DIM].reshape(N_Q_HEADS, HEAD_DIM).sum(axis=0, keepdims=True)
    gs_["k_norm_w"] = dqk_w[:, N_Q_HEADS * HEAD_DIM:].reshape(N_KV_HEADS, HEAD_DIM).sum(axis=0, keepdims=True)
    dproj = lax.dynamic_update_slice(dproj, _unheads(dv_h).astype(MMD), (0, V0))

    gw["w_in_p"] = _mm_tn(h1, dproj, name="in_proj_dw", tk=512, tn=2944, tmm=2048, vmem=VMEM_BIG)
    zero_row = jnp.zeros((1, d), F32) if in_grad is None else jnp.zeros((1, d), F32) + in_grad(gw["w_in_p"])[0:1, 0:1]
    dh1 = _mm(dproj, wts["w_in_p"], name="in_proj_dx", outs=[F32], nt=True, tm=256, tn=1024, vmem=VMEM_BIG,
              extras=[(zero_row, "row", 0)], epi=lambda acc, r: (acc + r,))
    grad_x, dshift1, dscale1, gs_["norm1_w"] = _ln_mod_bwd(dh1, x, small["norm1_w"], scale1, dx1, name="ln1_bwd")
    dmod = jnp.concatenate([dshift1, dscale1, dgate1, dshift2, dscale2, dgate2], axis=0)
    return loss, grad_x, dmod, gw, gs_


N_DEV = 8
N_CHIP = 4
ANY = pl.BlockSpec(memory_space=pl.ANY)


def _place():
    return lax.axis_index("x"), lax.axis_index("y"), lax.axis_index("c")


def _allgather8(v, *, name):
    m_per, n = v.shape

    def body(x_ref, out_ref, send_sems, recv_sems, local_sem):
        x, y, c = _place()
        me, sibling = (x, y, c), (x, y, 1 - c)
        chips = [(1 - x, y), (x, 1 - y), (1 - x, 1 - y)]

        def rows(px, py, pc):
            return out_ref.at[pl.ds((4 * px + 2 * py + pc) * m_per, m_per), :]

        def copy(k, block, to, src=None):
            return pltpu.make_async_remote_copy(
                src_ref=rows(*block) if src is None else src, dst_ref=rows(*block),
                send_sem=send_sems.at[k], recv_sem=recv_sems.at[k], device_id=to, device_id_type=MESH)

        mine = pltpu.make_async_copy(x_ref, rows(*me), local_sem)
        mine.start()
        first = [copy(0, me, sibling, src=x_ref)]
        first += [copy(1 + j, me, (*chip, c), src=x_ref) for j, chip in enumerate(chips)]
        for cp in first:
            cp.start()
        passed = [copy(4 + j, (*chip, c), sibling) for j, chip in enumerate(chips)]
        for j, chip in enumerate(chips):
            copy(1 + j, (*chip, c), me).wait_recv()
            passed[j].start()
        copy(0, sibling, me).wait_recv()
        for j, chip in enumerate(chips):
            copy(4 + j, (*chip, 1 - c), me).wait_recv()
        for cp in first + passed:
            cp.wait_send()
        mine.wait()

    return pl.pallas_call(
        body, name=name, out_shape=jax.ShapeDtypeStruct((N_DEV * m_per, n), v.dtype),
        in_specs=[pl.BlockSpec(memory_space=pltpu.VMEM)], out_specs=pl.BlockSpec(memory_space=pltpu.VMEM),
        scratch_shapes=[pltpu.SemaphoreType.DMA((7,)), pltpu.SemaphoreType.DMA((7,)), pltpu.SemaphoreType.DMA],
    )(v)


HBM = pl.BlockSpec(memory_space=pltpu.HBM)
SEM = pl.BlockSpec(memory_space=pltpu.SEMAPHORE)


def _chips_copies(x_ref, land_ref, sems, scatter, half=False):
    x, y, c = _place()
    k = 2 * x + y
    chips = [(1 - x, y), (x, 1 - y), (1 - x, 1 - y)]
    ids = [2 * cx + cy for cx, cy in chips]
    if half:
        hr = x_ref.shape[0] // 2
        rows = pl.ds(pl.multiple_of(c * hr, 16), hr)

    def copy(j, slot):
        src = x_ref.at[ids[j]] if scatter else (x_ref.at[rows] if half else x_ref)
        dst = land_ref.at[slot, rows] if half else land_ref.at[slot]
        return pltpu.make_async_remote_copy(src_ref=src, dst_ref=dst, send_sem=sems[j], recv_sem=sems[3 + j],
                                            device_id=(*chips[j], c), device_id_type=MESH)

    return [copy(j, k) for j in range(3)], [copy(j, ids[j]) for j in range(3)]


def _chips_start(src, scatter, half=False, *, name):
    shape = src.shape if scatter else (N_CHIP,) + tuple(src.shape)

    def body(x_ref, land_ref, *rest):
        sems, token = rest[0:6], rest[8]
        for cp in _chips_copies(x_ref, land_ref, sems, scatter, half)[0]:
            cp.start()
        token[...] = jnp.zeros_like(token)

    out = pl.pallas_call(
        body, name=name,
        out_shape=(pltpu.SemaphoreType.DMA(()),) * 6 + (pltpu.HBM(src.shape, src.dtype), pltpu.HBM(shape, src.dtype),
                                                       jax.ShapeDtypeStruct((8, 128), F32)),
        in_specs=(HBM, HBM), out_specs=(SEM,) * 6 + (HBM, HBM, pl.BlockSpec(memory_space=pltpu.VMEM)),
        input_output_aliases={0: 6, 1: 7},
        compiler_params=pltpu.CompilerParams(has_side_effects=pltpu.SideEffectType.DATAFLOW_SIDE_EFFECTING),
    )(pltpu.with_memory_space_constraint(src, pltpu.HBM),
      pltpu.with_memory_space_constraint(lax.empty(shape, src.dtype), pltpu.HBM))
    return out[0:6], out[6], out[7], out[8]


def _chips_wait(sems, src, land, after, scatter, half=False, *, name):
    after = list(after) if isinstance(after, (list, tuple)) else [after]

    def body(x_ref, land_ref, *rest):
        sems_ = rest[0:6]
        for cp in _chips_copies(x_ref, land_ref, sems_, scatter, half)[1]:
            cp.wait_send()
            cp.wait_recv()

    return pl.pallas_call(
        body, name=name, out_shape=(pltpu.HBM(src.shape, src.dtype), pltpu.HBM(land.shape, land.dtype)),
        in_specs=(HBM, HBM) + (SEM,) * 6 + (ANY,) * len(after), out_specs=(HBM, HBM), input_output_aliases={0: 0, 1: 1},
        compiler_params=pltpu.CompilerParams(has_side_effects=pltpu.SideEffectType.DATAFLOW_SIDE_EFFECTING),
    )(src, land, *sems, *after)


def _row_tile(r, pref=512):
    return max(t for t in range(16, pref + 1, 16) if r % t == 0)


def _pair_complete(land, *, name):
    r = land.shape[1]
    hr = r // 2
    assert r == 2 * hr and hr % 16 == 0

    def body(in_ref, out_ref, send_sems, recv_sems):
        x, y, c = _place()
        ids = [2 * cx + cy for cx, cy in [(1 - x, y), (x, 1 - y), (1 - x, 1 - y)]]
        mine_rows = pl.ds(pl.multiple_of(c * hr, 16), hr)
        other_rows = pl.ds(pl.multiple_of((1 - c) * hr, 16), hr)

        def copy(j, rows):
            return pltpu.make_async_remote_copy(
                src_ref=in_ref.at[ids[j], mine_rows], dst_ref=out_ref.at[ids[j], rows], send_sem=send_sems.at[j],
                recv_sem=recv_sems.at[j], device_id=(x, y, 1 - c), device_id_type=MESH)

        sends = [copy(j, mine_rows) for j in range(3)]
        for cp in sends:
            cp.start()
        for j in range(3):
            copy(j, other_rows).wait_recv()
        for cp in sends:
            cp.wait_send()

    return pl.pallas_call(
        body, name=name, out_shape=jax.ShapeDtypeStruct(land.shape, land.dtype), in_specs=[ANY], out_specs=ANY,
        input_output_aliases={0: 0},
        scratch_shapes=[pltpu.SemaphoreType.DMA((3,)), pltpu.SemaphoreType.DMA((3,))],
    )(land)


def _pair_swap(a, *, name):
    n, r, cols = a.shape
    hr = r // 2

    def body(x_ref, out_ref, send_sem, recv_sem):
        x, y, c = _place()
        other_rows = pl.ds(pl.multiple_of((1 - c) * hr, 16), hr)
        cp = pltpu.make_async_remote_copy(src_ref=x_ref.at[:, other_rows], dst_ref=out_ref, send_sem=send_sem,
                                          recv_sem=recv_sem, device_id=(x, y, 1 - c), device_id_type=MESH)
        cp.start()
        cp.wait()

    return pl.pallas_call(
        body, name=name, out_shape=jax.ShapeDtypeStruct((n, hr, cols), a.dtype), in_specs=[ANY], out_specs=ANY,
        scratch_shapes=[pltpu.SemaphoreType.DMA, pltpu.SemaphoreType.DMA],
    )(a)


def _sibling_copy(a, *, name):
    def body(x_ref, out_ref, send_sem, recv_sem):
        x, y, c = _place()
        cp = pltpu.make_async_remote_copy(src_ref=x_ref, dst_ref=out_ref, send_sem=send_sem, recv_sem=recv_sem,
                                          device_id=(x, y, 1 - c), device_id_type=MESH)
        cp.start()
        cp.wait()

    return pl.pallas_call(
        body, name=name, out_shape=jax.ShapeDtypeStruct(a.shape, a.dtype), in_specs=[ANY], out_specs=ANY,
        scratch_shapes=[pltpu.SemaphoreType.DMA, pltpu.SemaphoreType.DMA],
    )(a)


def _sum_slots(a, own, *, name):
    _, r, c = a.shape
    tr = _row_tile(r, 256)

    def body(a_ref, own_ref, o_ref):
        k = 2 * lax.axis_index("x") + lax.axis_index("y")
        acc = None
        for j in range(N_CHIP):
            term = jnp.where(k == j, own_ref[j], a_ref[j]).astype(F32)
            acc = term if acc is None else acc + term
        o_ref[...] = acc

    spec = pl.BlockSpec((N_CHIP, tr, c), lambda i: (0, i, 0))
    return pl.pallas_call(
        body, name=name, grid=(r // tr,), in_specs=[spec, spec],
        out_specs=pl.BlockSpec((tr, c), lambda i: (i, 0)), out_shape=jax.ShapeDtypeStruct((r, c), F32),
        compiler_params=_cp(("arbitrary",)),
    )(a, own)


def _add2(a, b, *, name):
    r, c = a.shape
    tr = _row_tile(r)

    def body(a_ref, b_ref, o_ref):
        o_ref[...] = (a_ref[...].astype(F32) + b_ref[...].astype(F32)).astype(o_ref.dtype)

    spec = pl.BlockSpec((tr, c), lambda i: (i, 0))
    return pl.pallas_call(
        body, name=name, grid=(r // tr,), in_specs=[spec, spec], out_specs=spec,
        out_shape=jax.ShapeDtypeStruct((r, c), a.dtype), compiler_params=_cp(("arbitrary",)),
    )(a, b)


BIG = ("w_in", "w_mlp1", "w_attn_out", "w_ssd_out", "w_o", "w_mlp2")
COL_SHARDED = ("w_mlp1", "w_in")
ROW_SHARDED = ("w_attn_out", "w_ssd_out", "w_o", "w_mlp2")
LATE = ROW_SHARDED + ("w_mlp1",)
SMALL = ("b_ada", "norm1_w", "norm2_w", "q_norm_w", "k_norm_w", "conv_b", "A_log", "dt_bias", "ssd_D", "ssd_norm_w")
NAMES = ("w_ada", "b_ada", "norm1_w", "norm2_w", "w_in", "q_norm_w", "k_norm_w", "conv_w", "conv_b", "A_log", "dt_bias",
         "ssd_D", "ssd_norm_w", "w_attn_out", "w_ssd_out", "w_o", "w_mlp1", "w_mlp2")
W_IN_COLS = 8768


def _permute_in(w):
    return jnp.concatenate([w[:, 4608:6656], w[:, 6720:8768], w[:, 1536:4608], w[:, 0:1536], w[:, 6656:6720],
                            jnp.zeros((w.shape[0], PW - W_IN_COLS), w.dtype)], axis=1)


def _unpermute_in(wp):
    return jnp.concatenate([wp[:, Q0:DT0], wp[:, XS0:Q0], wp[:, Z0:GA0], wp[:, DT0:DT0 + 64], wp[:, GA0:XS0]], axis=1)


def _pad_to(v, n):
    return jnp.pad(v, (0, n - v.shape[0]))


def _step(w, m, v, loss_target):
    xi, yi, ci = _place()
    chip = 2 * xi + yi
    dev = 4 * xi + 2 * yi + ci
    x, tgt = w["x"], loss_target
    d = x.shape[1]

    cw = w["conv_w"].shape[1]
    v0 = _pad_to(jnp.concatenate([w["c"].reshape(-1), w["conv_w"].reshape(-1)]), 5120).reshape(8, 640)
    g0 = _allgather8(v0, name="ag_cond").reshape(N_DEV, 5120)
    c_all = g0[:, 0:d]
    conv_w = jnp.concatenate([g0[2 * k, d:d + D_CONV * cw].reshape(D_CONV, cw) for k in range(N_CHIP)], axis=1)
    sc = _silu_cast(c_all, name="silu_c")
    modp = _mm(sc, w["w_ada"].astype(MMD), name="ada_fwd", outs=[F32], tm=8, tn=512)
    g1 = _allgather8(modp, name="ag_mod").reshape(N_DEV, N_DEV, modp.shape[1])
    mod_all = jnp.concatenate([g1[2 * k] for k in range(N_CHIP)], axis=1)
    mod = (lax.dynamic_slice_in_dim(mod_all, dev, 1, axis=0) + w["b_ada"]).reshape(6, d)

    mine, mod = lax.optimization_barrier((w["w_in"].astype(MMD), mod))
    in_sems, in_src, in_land, in_token = _chips_start(mine, False, True, name="ag_w_in_start")
    mod = mod + in_token[0:1, 0:1]
    small = {n: w[n] for n in SMALL if n != "b_ada"}
    small["conv_w"] = conv_w
    started = {}

    late_mine = jnp.concatenate([w[n].astype(MMD) for n in LATE], axis=0) + in_token[0:1, 0:1].astype(MMD)

    def in_weights(after):
        src, land = _chips_wait(in_sems, in_src, in_land, [*after, late_mine], False, True, name="ag_w_in_wait")
        land = _pair_complete(land, name="ag_w_in_pair")
        late, land = lax.optimization_barrier((late_mine, land))
        sems, late_src, late_land, token = _chips_start(late, False, name="ag_late_start")
        started["ag_late"] = (sems, late_src, late_land)
        w_in = jnp.concatenate([jnp.where(chip == k, src, land[k]) for k in range(N_CHIP)], axis=1)
        w_dt = jnp.pad(w_in[:, 6656:6720], ((0, 0), (0, 64))) + token[0:1, 0:1].astype(MMD)
        return {"w_in_p": _permute_in(w_in), "w_dt": w_dt}

    def late_weights(after):
        src, land = _chips_wait(*started["ag_late"], after, False, name="ag_late_wait")
        out, o = {}, 0
        for n in LATE:
            rows = w[n].shape[0]
            parts = [jnp.where(chip == k, src[o:o + rows], land[k, o:o + rows]) for k in range(N_CHIP)]
            out[n] = jnp.concatenate(parts, axis=1 if n in COL_SHARDED else 0)
            o += rows
        return out

    def pair_sums(slots, tag):
        _, rows, cols = slots.shape
        hr = rows // 2
        theirs = _pair_swap(slots, name="rs_pair_" + tag)
        ours = lax.dynamic_slice_in_dim(slots, ci * hr, hr, axis=1)
        pair = _add2(ours.reshape(N_CHIP * hr, cols), theirs.reshape(N_CHIP * hr, cols), name="rs_pair_sum_" + tag)
        return pair.reshape(N_CHIP, hr, cols)

    def finish(recv, pair, tag):
        half = _sum_slots(recv, pair, name="rs_sum_" + tag)
        other = _sibling_copy(half, name="rs_sibling_" + tag)
        return jnp.where(ci == 0, jnp.concatenate([half, other], axis=0), jnp.concatenate([other, half], axis=0))

    def late_grads(gw):
        slots = []
        for k in range(N_CHIP):
            parts = []
            for n in LATE:
                rows = w[n].shape[0]
                blk = gw[n][:, k * rows:(k + 1) * rows] if n in COL_SHARDED else gw[n][k * rows:(k + 1) * rows]
                parts.append(blk.astype(MMD))
            slots.append(jnp.concatenate(parts, axis=0))
        pair = pair_sums(jnp.stack(slots), "late")
        sems, src, land, token = _chips_start(pair, True, name="rs_late_start")
        started["late"] = (sems, src, land)
        return token[0:1, 0:1]

    def in_grad(g):
        g_in = _unpermute_in(g)
        cols_in = w["w_in"].shape[1]
        pair = pair_sums(jnp.stack([g_in[:, k * cols_in:(k + 1) * cols_in].astype(MMD) for k in range(N_CHIP)]), "w_in")
        sems, src, land, token = _chips_start(pair, True, name="rs_w_in_start")
        started["w_in"] = (sems, src, land)
        return token

    loss, grad_x, dmod, gw, gs = _local_step(x, tgt, mod, {}, small, in_weights, late_weights, late_grads, in_grad,
                                             in_token[0, 0])

    grads = {}
    pair, land = _chips_wait(*started["w_in"], grad_x, True, name="rs_w_in_wait")
    grads["w_in"] = finish(land, pair, "w_in")
    pair, land = _chips_wait(*started["late"], grad_x, True, name="rs_late_wait")
    total, o = finish(land, pair, "late"), 0
    for n in LATE:
        rows = w[n].shape[0]
        grads[n] = total[o:o + rows]
        o += rows

    order = ([dmod.reshape(-1)] + [gs[n].reshape(-1) for n in SMALL if n != "b_ada"] + [gs["conv_w"].reshape(-1)]
             + [loss.reshape(-1)])
    vec = jnp.concatenate(order)
    n_small = vec.shape[0]
    n_pad = -(-n_small // 1024) * 1024
    g2 = _allgather8(_pad_to(vec, n_pad).reshape(8, n_pad // 8), name="ag_small")
    tot = _rows_sum(g2, N_DEV, name="small_sum").reshape(-1)
    loss = tot[n_small - 1]
    dmod_all = g2.reshape(N_DEV, n_pad)[:, 0:6 * d]
    off = 0
    for n in SMALL:
        grads[n] = tot[off:off + w[n].size].reshape(w[n].shape)
        off += w[n].size
    conv_full = tot[off:off + D_CONV * N_CHIP * cw].reshape(D_CONV, N_CHIP * cw)
    grads["conv_w"] = lax.dynamic_slice_in_dim(conv_full, chip * cw, cw, axis=1)
    ada_cols = w["w_ada"].shape[1]
    dmod_mine = lax.dynamic_slice_in_dim(dmod_all, chip * ada_cols, ada_cols, axis=1).astype(MMD)
    grads["w_ada"] = _mm_tn(sc, dmod_mine, name="ada_dw", tk=512, tn=512, tmm=8)

    delta, new_m, new_v = {}, {}, {}
    pack = lambda t: jnp.concatenate([t[n].reshape(-1) for n in SMALL]).reshape(1, -1)
    ds_, ms_, vs_ = _adamw(pack(w), pack(grads), pack(m), pack(v), name="adamw_small")
    off = 0
    for n in SMALL:
        for dst, src in ((delta, ds_), (new_m, ms_), (new_v, vs_)):
            dst[n] = src[0, off:off + w[n].size].reshape(w[n].shape)
        off += w[n].size
    for n in ("w_ada", "conv_w") + BIG:
        delta[n], new_m[n], new_v[n] = _adamw(w[n], grads[n], m[n], v[n], name="adamw_" + n)
    return loss, grad_x, grads, delta, new_m, new_v


def kernel(x, c, w_ada, b_ada, norm1_w, norm2_w, w_in, q_norm_w, k_norm_w, conv_w, conv_b, A_log, dt_bias, ssd_D, ssd_norm_w, w_attn_out, w_ssd_out, w_o, w_mlp1, w_mlp2, loss_target, m_w_ada, m_b_ada, m_norm1_w, m_norm2_w, m_w_in, m_q_norm_w, m_k_norm_w, m_conv_w, m_conv_b, m_A_log, m_dt_bias, m_ssd_D, m_ssd_norm_w, m_w_attn_out, m_w_ssd_out, m_w_o, m_w_mlp1, m_w_mlp2, v_w_ada, v_b_ada, v_norm1_w, v_norm2_w, v_w_in, v_q_norm_w, v_k_norm_w, v_conv_w, v_conv_b, v_A_log, v_dt_bias, v_ssd_D, v_ssd_norm_w, v_w_attn_out, v_w_ssd_out, v_w_o, v_w_mlp1, v_w_mlp2):
    args = dict(locals())
    strip = lambda a: a[0] if a.ndim == 3 else a
    w = {n: strip(args[n]) for n in NAMES + ("x", "c")}
    m = {n: strip(args["m_" + n]) for n in NAMES}
    v = {n: strip(args["v_" + n]) for n in NAMES}
    loss, grad_x, grads, delta, new_m, new_v = _step(w, m, v, loss_target[0])
    like = lambda t, n: t.reshape(args[n].shape)
    return (loss, grad_x[None], *[like(grads[n], n) for n in NAMES], *[like(delta[n], n) for n in NAMES],
            *[like(new_m[n], n) for n in NAMES], *[like(new_v[n], n) for n in NAMES])
```

```python
import math

import jax
import jax.numpy as jnp
from jax import lax
from jax.experimental import pallas as pl
from jax.experimental.pallas import tpu as pltpu

F32 = jnp.float32
MMD = jnp.bfloat16
EPS = 1e-6
NEG = -1e30
MIB = 1024 * 1024
VMEM_BIG = 56 * MIB
VMEM_MID = 40 * MIB

GRID_W = 64
N_Q_HEADS, N_KV_HEADS, HEAD_DIM = 16, 4, 64
ROPE_THETA = 10000.0
SSD_HEADS, SSD_GROUPS, SSD_P, SSD_N, CHUNK = 32, 4, 64, 128, 128
HPG = SSD_HEADS // SSD_GROUPS
D_CONV = 5
ADAM_LR, ADAM_B1, ADAM_B2, ADAM_EPS, ADAM_WD, ADAM_STEP = 0.001, 0.9, 0.999, 1e-08, 0.01, 10

Z0, GA0, GS0, XS0, B0, C0, Q0, K0, V0, DT0, PW = 0, 2048, 3072, 4096, 6144, 6656, 7168, 8192, 8448, 8704, 8832

MESH = pl.DeviceIdType.MESH
NT = (((1,), (1,)), ((), ()))
TN = (((0,), (0,)), ((), ()))


def _cp(sem=None, vmem=VMEM_MID):
    return pltpu.CompilerParams(dimension_semantics=sem, vmem_limit_bytes=vmem)


def _tile(n, pref):
    t = min(n, pref)
    while n % t:
        t //= 2
    return t


def _dot(a, b, dims=None):
    if dims is None:
        return jnp.dot(a, b, preferred_element_type=F32)
    return lax.dot_general(a, b, dims, preferred_element_type=F32)


def _dot_hi(a01, b):
    a = a01.astype(jnp.bfloat16)
    h1 = b.astype(jnp.bfloat16)
    r1 = b - h1.astype(F32)
    h2 = r1.astype(jnp.bfloat16)
    return _dot(a, h1) + _dot(a, h2) + _dot(a, (r1 - h2.astype(F32)).astype(jnp.bfloat16))


def _sigmoid(x):
    return jax.nn.sigmoid(x)


def _mm(a, b, *, name, outs, nt=False, ta=False, extras=(), epi=None, tm=512, tn=512, n=None, b_outer=False,
        vmem=VMEM_MID):
    assert not (nt and ta)
    k, m = a.shape if ta else a.shape[::-1]
    if n is None:
        n = b.shape[0] if nt else b.shape[1]
    tm, tn = _tile(m, tm), _tile(n, tn)
    gi, gj = m // tm, n // tn
    if b_outer:
        grid = (gj, gi)
        ij = lambda p, q: (q, p)
    else:
        grid = (gi, gj)
        ij = lambda p, q: (p, q)
    if ta:
        a_spec = pl.BlockSpec((k, tm), lambda p, q: (0, ij(p, q)[0]))
    else:
        a_spec = pl.BlockSpec((tm, k), lambda p, q: (ij(p, q)[0], 0))
    if nt:
        b_spec = pl.BlockSpec((tn, k), lambda p, q: (ij(p, q)[1], 0))
    else:
        b_spec = pl.BlockSpec((k, tn), lambda p, q: (0, ij(p, q)[1]))
    e_specs = []
    for arr, kind, off in extras:
        ob = off // tn
        assert off % tn == 0
        if kind == "tile":
            e_specs.append(pl.BlockSpec((tm, tn), lambda p, q, ob=ob: (ij(p, q)[0], ob + ij(p, q)[1])))
        else:
            e_specs.append(pl.BlockSpec((1, tn), lambda p, q, ob=ob: (0, ob + ij(p, q)[1])))
    ne = len(extras)

    def body(a_ref, b_ref, *rest):
        acc = _dot(a_ref[...], b_ref[...], NT if nt else (TN if ta else None))
        res = epi(acc, *[e[...] for e in rest[:ne]]) if epi is not None else (acc,)
        for o_ref, r in zip(rest[ne:], res):
            o_ref[...] = r.astype(o_ref.dtype)

    out = pl.pallas_call(
        body, name=name, grid=grid,
        in_specs=[a_spec, b_spec] + e_specs,
        out_specs=[pl.BlockSpec((tm, tn), lambda p, q: ij(p, q)) for _ in outs],
        out_shape=[jax.ShapeDtypeStruct((m, n), dt) for dt in outs],
        compiler_params=_cp(("arbitrary", "arbitrary"), vmem),
    )(a, b, *[e[0] for e in extras])
    return out if len(outs) > 1 else out[0]


def _mm_tn(a, g, *, name, tk=512, tn=1024, tmm=4096, vmem=VMEM_MID):
    m, k = a.shape
    n = g.shape[1]
    tk, tn, tmm = _tile(k, tk), _tile(n, tn), _tile(m, tmm)

    def body(a_ref, g_ref, o_ref):
        p = _dot(a_ref[...], g_ref[...], TN)

        @pl.when(pl.program_id(2) == 0)
        def _():
            o_ref[...] = p

        @pl.when(pl.program_id(2) > 0)
        def _():
            o_ref[...] += p

    return pl.pallas_call(
        body, name=name, grid=(k // tk, n // tn, m // tmm),
        in_specs=[pl.BlockSpec((tmm, tk), lambda i, j, r: (r, i)), pl.BlockSpec((tmm, tn), lambda i, j, r: (r, j))],
        out_specs=pl.BlockSpec((tk, tn), lambda i, j, r: (i, j)),
        out_shape=jax.ShapeDtypeStruct((k, n), F32),
        compiler_params=_cp(("arbitrary", "arbitrary", "arbitrary"), vmem),
    )(a, g)


def _adamw(w, g, m, v, *, name):
    r, c = w.shape
    tr = _tile(r, 256) if r % 8 == 0 else r

    def body(w_ref, g_ref, m_ref, v_ref, d_ref, nm_ref, nv_ref):
        gg = g_ref[...]
        nm = ADAM_B1 * m_ref[...] + (1.0 - ADAM_B1) * gg
        nv = ADAM_B2 * v_ref[...] + (1.0 - ADAM_B2) * jnp.square(gg)
        m_hat = nm / (1.0 - ADAM_B1 ** ADAM_STEP)
        v_hat = nv / (1.0 - ADAM_B2 ** ADAM_STEP)
        d_ref[...] = -ADAM_LR * (m_hat / (jnp.sqrt(v_hat) + ADAM_EPS) + ADAM_WD * w_ref[...])
        nm_ref[...] = nm
        nv_ref[...] = nv

    spec = pl.BlockSpec((tr, c), lambda i: (i, 0))
    return pl.pallas_call(
        body, name=name, grid=(r // tr,), in_specs=[spec] * 4, out_specs=[spec] * 3,
        out_shape=[jax.ShapeDtypeStruct((r, c), F32)] * 3, compiler_params=_cp(("arbitrary",)),
    )(w, g, m, v)


def _rows_sum(a, groups, *, name):
    r = a.shape[0] // groups

    def body(a_ref, o_ref):
        acc = a_ref[0:r, :]
        for d in range(1, groups):
            acc = acc + a_ref[d * r:(d + 1) * r, :]
        o_ref[...] = acc

    return pl.pallas_call(body, name=name, out_shape=jax.ShapeDtypeStruct((r, a.shape[1]), F32))(a)


def _silu_cast(a, *, name):
    def body(a_ref, o_ref):
        x = a_ref[...]
        o_ref[...] = (x * _sigmoid(x)).astype(o_ref.dtype)

    return pl.pallas_call(body, name=name, out_shape=jax.ShapeDtypeStruct(a.shape, MMD))(a)


def _sumsq(a, *, name):
    m, n = a.shape
    tm = _tile(m, 512)

    def body(a_ref, o_ref):
        x = a_ref[...]
        p = jnp.sum(jnp.sum(x * x, axis=1, keepdims=True), axis=0, keepdims=True)

        @pl.when(pl.program_id(0) == 0)
        def _():
            o_ref[...] = p

        @pl.when(pl.program_id(0) > 0)
        def _():
            o_ref[...] += p

    return pl.pallas_call(
        body, name=name, grid=(m // tm,), in_specs=[pl.BlockSpec((tm, n), lambda i: (i, 0))],
        out_specs=pl.BlockSpec((1, 1), lambda i: (0, 0)), out_shape=jax.ShapeDtypeStruct((1, 1), F32),
        compiler_params=_cp(("arbitrary",)),
    )(a)


def _acc_rows(o_ref, p, first):
    @pl.when(first)
    def _():
        o_ref[...] = p

    @pl.when(jnp.logical_not(first))
    def _():
        o_ref[...] += p


def _ln_mod(x, w, scale, shift, *, name):
    s, d = x.shape
    tm = _tile(s, 512)

    def body(x_ref, w_ref, sc_ref, sh_ref, o_ref):
        xv = x_ref[...]
        r = lax.rsqrt(jnp.mean(xv * xv, axis=-1, keepdims=True) + EPS)
        o_ref[...] = ((xv * r) * w_ref[...] * (1.0 + sc_ref[...]) + sh_ref[...]).astype(o_ref.dtype)

    row = pl.BlockSpec((1, d), lambda i: (0, 0))
    big = pl.BlockSpec((tm, d), lambda i: (i, 0))
    return pl.pallas_call(
        body, name=name, grid=(s // tm,), in_specs=[big, row, row, row], out_specs=big,
        out_shape=jax.ShapeDtypeStruct((s, d), MMD), compiler_params=_cp(("arbitrary",)),
    )(x, w, scale, shift)


def _ln_mod_bwd(dh, x, w, scale, dres, *, name):
    s, d = x.shape
    tm = _tile(s, 512)

    def body(dh_ref, x_ref, w_ref, sc_ref, dres_ref, dx_ref, dsh_ref, dsc_ref, dw_ref):
        xv = x_ref[...]
        dhv = dh_ref[...].astype(F32)
        r = lax.rsqrt(jnp.mean(xv * xv, axis=-1, keepdims=True) + EPS)
        nv = xv * r
        wv = w_ref[...]
        g1 = 1.0 + sc_ref[...]
        dn = dhv * (wv * g1)
        dx_ref[...] = dres_ref[...] + r * (dn - nv * jnp.mean(dn * nv, axis=-1, keepdims=True))
        first = pl.program_id(0) == 0
        _acc_rows(dsh_ref, jnp.sum(dhv, axis=0, keepdims=True), first)
        _acc_rows(dsc_ref, jnp.sum(dhv * nv * wv, axis=0, keepdims=True), first)
        _acc_rows(dw_ref, jnp.sum(dhv * nv * g1, axis=0, keepdims=True), first)

    row = pl.BlockSpec((1, d), lambda i: (0, 0))
    big = pl.BlockSpec((tm, d), lambda i: (i, 0))
    return pl.pallas_call(
        body, name=name, grid=(s // tm,), in_specs=[big, big, row, row, big], out_specs=[big, row, row, row],
        out_shape=[jax.ShapeDtypeStruct((s, d), F32)] + [jax.ShapeDtypeStruct((1, d), F32)] * 3,
        compiler_params=_cp(("arbitrary",)),
    )(dh, x, w, scale, dres)


def _gate_bwd(dy, u, gate, *, name):
    s, d = dy.shape
    tm = _tile(s, 512)

    def body(dy_ref, u_ref, g_ref, du_ref, dg_ref):
        dyv = dy_ref[...]
        du_ref[...] = (dyv * g_ref[...]).astype(du_ref.dtype)
        _acc_rows(dg_ref, jnp.sum(dyv * u_ref[...].astype(F32), axis=0, keepdims=True), pl.program_id(0) == 0)

    row = pl.BlockSpec((1, d), lambda i: (0, 0))
    big = pl.BlockSpec((tm, d), lambda i: (i, 0))
    return pl.pallas_call(
        body, name=name, grid=(s // tm,), in_specs=[big, big, row], out_specs=[big, row],
        out_shape=[jax.ShapeDtypeStruct((s, d), MMD), jax.ShapeDtypeStruct((1, d), F32)],
        compiler_params=_cp(("arbitrary",)),
    )(dy, u, gate)


def _seg64(v, e):
    hi = v.astype(jnp.bfloat16)
    lo = (v - hi.astype(F32)).astype(jnp.bfloat16)
    return _dot(hi, e) + _dot(lo, e)


def _rope_tables(s, zero=0.0):
    rows = s // GRID_W
    pos_row = jnp.repeat(jnp.arange(rows, dtype=jnp.int32), GRID_W).astype(F32) + zero
    pos_col = jnp.tile(jnp.arange(GRID_W, dtype=jnp.int32), rows).astype(F32) + zero
    axis_dim = HEAD_DIM // 2
    inv_freq = ROPE_THETA ** (-jnp.arange(0, axis_dim, 2, dtype=F32) / axis_dim)
    ang_r = pos_row[:, None] * inv_freq[None, :]
    ang_c = pos_col[:, None] * inv_freq[None, :]
    zero = jnp.zeros_like(ang_r)
    cos = jnp.concatenate([jnp.cos(ang_r), jnp.cos(ang_r), jnp.cos(ang_c), jnp.cos(ang_c)], axis=1)
    s_a = jnp.concatenate([-jnp.sin(ang_r), zero, -jnp.sin(ang_c), zero], axis=1)
    s_b = jnp.concatenate([zero, jnp.sin(ang_r), zero, jnp.sin(ang_c)], axis=1)
    return [jnp.tile(t, (1, 2)) for t in (cos, s_a, s_b)]


def _e128():
    i = jnp.arange(128)
    return (i[:, None] // 64 == i[None, :] // 64).astype(jnp.bfloat16)


QKW = N_Q_HEADS * HEAD_DIM + N_KV_HEADS * HEAD_DIM


def _qk_fwd(proj, wrow, scrow, tabs, *, name):
    s = proj.shape[0]
    tm = _tile(s, 1024)

    def body(x_ref, w_ref, sc_ref, cos_ref, sa_ref, sb_ref, e_ref, ot_ref):
        u = x_ref[...].astype(F32)
        r = lax.rsqrt(_seg64(u * u, e_ref[...]) * (1.0 / HEAD_DIM) + EPS)
        nv = (u * r) * w_ref[...]
        ro = nv * cos_ref[...] + pltpu.roll(nv, 112, 1) * sa_ref[...] + pltpu.roll(nv, 16, 1) * sb_ref[...]
        ot_ref[...] = (ro * sc_ref[...]).T.astype(ot_ref.dtype)

    tab = pl.BlockSpec((tm, 128), lambda i, j: (i, 0))
    row = pl.BlockSpec((1, 128), lambda i, j: (0, j))
    return pl.pallas_call(
        body, name=name, grid=(s // tm, QKW // 128),
        in_specs=[pl.BlockSpec((tm, 128), lambda i, j: (i, Q0 // 128 + j)), row, row, tab, tab, tab,
                  pl.BlockSpec((128, 128), lambda i, j: (0, 0))],
        out_specs=pl.BlockSpec((128, tm), lambda i, j: (j, i)),
        out_shape=jax.ShapeDtypeStruct((QKW, s), MMD), compiler_params=_cp(("arbitrary", "arbitrary")),
    )(proj, wrow, scrow, *tabs, _e128())


def _qk_bwd(dqt, dkt, proj, wrow, scrow, tabs, dproj, *, name):
    s = proj.shape[0]
    tm = _tile(s, 1024)
    nq = dqt.shape[0] // 128

    def body(dq_ref, dk_ref, x_ref, w_ref, sc_ref, cos_ref, sa_ref, sb_ref, e_ref, _, du_ref, dw_ref):
        e = e_ref[...]
        d = jnp.where(pl.program_id(0) < nq, dq_ref[...], dk_ref[...]).T * sc_ref[...]
        dn = d * cos_ref[...] + pltpu.roll(d * sa_ref[...], 16, 1) + pltpu.roll(d * sb_ref[...], 112, 1)
        u = x_ref[...].astype(F32)
        r = lax.rsqrt(_seg64(u * u, e) * (1.0 / HEAD_DIM) + EPS)
        uh = u * r
        _acc_rows(dw_ref, jnp.sum(dn * uh, axis=0, keepdims=True), pl.program_id(1) == 0)
        dnw = dn * w_ref[...]
        du_ref[...] = (r * (dnw - uh * (_seg64(dnw * uh, e) * (1.0 / HEAD_DIM)))).astype(du_ref.dtype)

    tab = pl.BlockSpec((tm, 128), lambda j, i: (i, 0))
    row = pl.BlockSpec((1, 128), lambda j, i: (0, j))
    qcol = pl.BlockSpec((tm, 128), lambda j, i: (i, Q0 // 128 + j))
    return pl.pallas_call(
        body, name=name, grid=(QKW // 128, s // tm),
        in_specs=[pl.BlockSpec((128, tm), lambda j, i: (jnp.minimum(j, nq - 1), i)),
                  pl.BlockSpec((128, tm), lambda j, i: (jnp.maximum(j - nq, 0), i)),
                  qcol, row, row, tab, tab, tab, pl.BlockSpec((128, 128), lambda j, i: (0, 0)), ANY],
        out_specs=[qcol, row],
        out_shape=[jax.ShapeDtypeStruct(dproj.shape, dproj.dtype), jax.ShapeDtypeStruct((1, QKW), F32)],
        input_output_aliases={9: 0}, compiler_params=_cp(("arbitrary", "arbitrary")),
    )(dqt, dkt, proj, wrow, scrow, *tabs, _e128(), dproj)


REP = N_Q_HEADS // N_KV_HEADS


def _lanes(ref):
    return jnp.concatenate([ref[r] for r in range(REP)], axis=1)


V_AUG = HEAD_DIM + 8
LOG2E = math.log2(math.e)


def _flash_fwd(qkt, vta, *, name):
    s = qkt.shape[2]
    tq, tk = _tile(s, 1024), _tile(s, 1024)
    nk = s // tk
    lanes = REP * tq

    def body(q_ref, k_ref, v_ref, o_ref, lse_ref, m_ref, acc_ref):
        j = pl.program_id(2)

        @pl.when(j == 0)
        def _():
            m_ref[...] = jnp.full_like(m_ref, NEG)
            acc_ref[...] = jnp.zeros_like(acc_ref)

        st = _dot(k_ref[0], _lanes(q_ref), TN)
        m_prev = m_ref[...]
        m_new = jnp.maximum(m_prev, jnp.max(st, axis=0, keepdims=True))
        p = jnp.exp2(st - m_new).astype(MMD)
        acc_ref[...] = jnp.exp2(m_prev - m_new) * acc_ref[...] + _dot(v_ref[0], p)
        m_ref[...] = m_new

        @pl.when(j == nk - 1)
        def _():
            acc = acc_ref[...]
            l = acc[HEAD_DIM:HEAD_DIM + 1]
            o = acc[0:HEAD_DIM] / l
            ls = m_ref[...] + jnp.log(l) * LOG2E
            for r in range(REP):
                o_ref[r] = o[:, r * tq:(r + 1) * tq].astype(o_ref.dtype)
                lse_ref[r] = ls[:, r * tq:(r + 1) * tq]

    qspec = pl.BlockSpec((REP, HEAD_DIM, tq), lambda g, i, j: (g, 0, i))
    return pl.pallas_call(
        body, name=name, grid=(N_KV_HEADS, s // tq, nk),
        in_specs=[qspec, pl.BlockSpec((1, HEAD_DIM, tk), lambda g, i, j: (N_Q_HEADS + g, 0, j)),
                  pl.BlockSpec((1, V_AUG, tk), lambda g, i, j: (g, 0, j))],
        out_specs=[qspec, pl.BlockSpec((REP, 1, tq), lambda g, i, j: (g, 0, i))],
        out_shape=[jax.ShapeDtypeStruct((N_Q_HEADS, HEAD_DIM, s), MMD), jax.ShapeDtypeStruct((N_Q_HEADS, 1, s), F32)],
        scratch_shapes=[pltpu.VMEM((1, lanes), F32), pltpu.VMEM((V_AUG, lanes), F32)],
        compiler_params=_cp(("arbitrary", "arbitrary", "arbitrary"), VMEM_BIG),
    )(qkt, qkt, vta)


def _flash_bwd(qkt, vta, dot, ot, lse, *, name):
    s = qkt.shape[2]
    tq, tk = _tile(s, 512), _tile(s, 1024)
    nk = s // tk

    def body(q_ref, kt_ref, vt_ref, do_ref, o_ref, lse_ref, dq_ref, dk_ref, dv_ref, dq_acc):
        i, j = pl.program_id(1), pl.program_id(2)
        q, do = _lanes(q_ref), _lanes(do_ref)
        delta = jnp.sum(do.astype(F32) * _lanes(o_ref).astype(F32), axis=0, keepdims=True)
        kt, vt = kt_ref[0], vt_ref[0, 0:HEAD_DIM, :]
        p = jnp.exp2(_dot(kt, q, TN) - _lanes(lse_ref))
        dvc = _dot(p.astype(MMD), do, NT)
        ds = (p * (_dot(vt, do, TN) - delta)).astype(MMD)
        dkc = _dot(ds, q, NT) * (1.0 / LOG2E)
        dqc = _dot(kt, ds)
        rows = pl.ds(pl.multiple_of(j * tk, tk), tk)

        @pl.when(i == 0)
        def _():
            dk_ref[0, rows, :] = dkc
            dv_ref[0, rows, :] = dvc

        @pl.when(i > 0)
        def _():
            dk_ref[0, rows, :] += dkc
            dv_ref[0, rows, :] += dvc

        @pl.when(j == 0)
        def _():
            dq_acc[...] = dqc

        @pl.when(j > 0)
        def _():
            dq_acc[...] += dqc

        @pl.when(j == nk - 1)
        def _():
            acc = dq_acc[...]
            for r in range(REP):
                dq_ref[r] = acc[:, r * tq:(r + 1) * tq]

    qspec = pl.BlockSpec((REP, HEAD_DIM, tq), lambda g, i, j: (g, 0, i))
    kvres = pl.BlockSpec((1, s, HEAD_DIM), lambda g, i, j: (g, 0, 0))
    return pl.pallas_call(
        body, name=name, grid=(N_KV_HEADS, s // tq, nk),
        in_specs=[qspec, pl.BlockSpec((1, HEAD_DIM, tk), lambda g, i, j: (N_Q_HEADS + g, 0, j)),
                  pl.BlockSpec((1, V_AUG, tk), lambda g, i, j: (g, 0, j)),
                  qspec, qspec, pl.BlockSpec((REP, 1, tq), lambda g, i, j: (g, 0, i))],
        out_specs=[qspec, kvres, kvres],
        out_shape=[jax.ShapeDtypeStruct((N_Q_HEADS, HEAD_DIM, s), F32), jax.ShapeDtypeStruct((N_KV_HEADS, s, HEAD_DIM), F32),
                   jax.ShapeDtypeStruct((N_KV_HEADS, s, HEAD_DIM), F32)],
        scratch_shapes=[pltpu.VMEM((HEAD_DIM, REP * tq), F32)],
        compiler_params=_cp(("arbitrary", "arbitrary", "arbitrary"), VMEM_BIG),
    )(qkt, qkt, vta, dot, ot, lse)


HALO = 8
CONV_W = 2048 + 2 * SSD_GROUPS * SSD_N


def _shifted(win, off, r):
    return pltpu.roll(win, (r + 2 * HALO - off) % (r + 2 * HALO), 0)[0:r]


def _conv_fwd(proj, w8, brow, *, name):
    s = proj.shape[0]
    cb = 256
    r = _tile(s, 512)

    def body(x_ref, w_ref, b_ref, o_ref, pad_ref):
        zeros = jnp.zeros((HALO, cb), F32)
        pad_ref[0:HALO, :] = zeros
        pad_ref[s + HALO:s + 2 * HALO, :] = zeros

        def fill(i, carry):
            st = pl.multiple_of(i * r, r)
            pad_ref[pl.ds(st + HALO, r), :] = x_ref[pl.ds(st, r), :].astype(F32)
            return carry

        lax.fori_loop(0, s // r, fill, 0)
        wv = w_ref[...]
        bv = b_ref[...]

        def step(i, carry):
            st = pl.multiple_of(i * r, r)
            win = pad_ref[pl.ds(st, r + 2 * HALO), :]
            acc = bv + wv[0:1, :] * _shifted(win, HALO - 2, r)
            for t in range(1, D_CONV):
                acc = acc + wv[t:t + 1, :] * _shifted(win, HALO - 2 + t, r)
            o_ref[pl.ds(st, r), :] = (acc * _sigmoid(acc)).astype(o_ref.dtype)
            return carry

        lax.fori_loop(0, s // r, step, 0)

    return pl.pallas_call(
        body, name=name, grid=(CONV_W // cb,),
        in_specs=[pl.BlockSpec((s, cb), lambda j: (0, XS0 // cb + j)), pl.BlockSpec((8, cb), lambda j: (0, j)),
                  pl.BlockSpec((1, cb), lambda j: (0, j))],
        out_specs=pl.BlockSpec((s, cb), lambda j: (0, j)),
        out_shape=jax.ShapeDtypeStruct((s, CONV_W), MMD),
        scratch_shapes=[pltpu.VMEM((s + 2 * HALO, cb), F32)],
        compiler_params=_cp(("arbitrary",), VMEM_MID),
    )(proj, w8, brow)


def _conv_bwd(proj, col0, ga, gb, w8, brow, dproj, *, name):
    s = proj.shape[0]
    width = ga.shape[1]
    cb = 128
    c0 = col0 // cb
    r = _tile(s, 512)

    def body(x_ref, ga_ref, gb_ref, w_ref, b_ref, _, dx_ref, dw_ref, db_ref, xpad, dpad):
        zeros = jnp.zeros((HALO, cb), F32)
        for ref in (xpad, dpad):
            ref[0:HALO, :] = zeros
            ref[s + HALO:s + 2 * HALO, :] = zeros

        def fill(i, carry):
            st = pl.multiple_of(i * r, r)
            xpad[pl.ds(st + HALO, r), :] = x_ref[pl.ds(st, r), :].astype(F32)
            return carry

        lax.fori_loop(0, s // r, fill, 0)
        wv = w_ref[...]
        bv = b_ref[...]

        def first(i, carry):
            st = pl.multiple_of(i * r, r)
            win = xpad[pl.ds(st, r + 2 * HALO), :]
            taps = [_shifted(win, HALO - 2 + t, r) for t in range(D_CONV)]
            u = bv
            for t in range(D_CONV):
                u = u + wv[t:t + 1, :] * taps[t]
            sg = _sigmoid(u)
            du = ((ga_ref[pl.ds(st, r), :].astype(F32) + gb_ref[pl.ds(st, r), :].astype(F32))
                  * (sg * (1.0 + u * (1.0 - sg))))
            dpad[pl.ds(st + HALO, r), :] = du
            out = [carry[0] + jnp.sum(du, axis=0, keepdims=True)]
            for t in range(D_CONV):
                out.append(carry[1 + t] + jnp.sum(du * taps[t], axis=0, keepdims=True))
            return tuple(out)

        sums = lax.fori_loop(0, s // r, first, tuple(jnp.zeros((1, cb), F32) for _ in range(1 + D_CONV)))
        db_ref[...] = sums[0]
        for t in range(D_CONV):
            dw_ref[t:t + 1, :] = sums[1 + t]
        dw_ref[D_CONV:8, :] = jnp.zeros((8 - D_CONV, cb), F32)

        def second(i, carry):
            st = pl.multiple_of(i * r, r)
            win = dpad[pl.ds(st, r + 2 * HALO), :]
            acc = wv[0:1, :] * _shifted(win, HALO + 2, r)
            for t in range(1, D_CONV):
                acc = acc + wv[t:t + 1, :] * _shifted(win, HALO + 2 - t, r)
            dx_ref[pl.ds(st, r), :] = acc.astype(dx_ref.dtype)
            return carry

        lax.fori_loop(0, s // r, second, 0)

    col = pl.BlockSpec((s, cb), lambda j: (0, j))
    xcol = pl.BlockSpec((s, cb), lambda j: (0, XS0 // cb + c0 + j))
    return pl.pallas_call(
        body, name=name, grid=(width // cb,),
        in_specs=[xcol, col, col, pl.BlockSpec((8, cb), lambda j: (0, c0 + j)),
                  pl.BlockSpec((1, cb), lambda j: (0, c0 + j)), ANY],
        out_specs=[xcol, pl.BlockSpec((8, cb), lambda j: (0, j)), pl.BlockSpec((1, cb), lambda j: (0, j))],
        out_shape=[jax.ShapeDtypeStruct(dproj.shape, dproj.dtype), jax.ShapeDtypeStruct((8, width), F32),
                   jax.ShapeDtypeStruct((1, width), F32)],
        scratch_shapes=[pltpu.VMEM((s + 2 * HALO, cb), F32), pltpu.VMEM((s + 2 * HALO, cb), F32)],
        input_output_aliases={5: 0}, compiler_params=_cp(("arbitrary",), VMEM_BIG),
    )(proj, ga, gb, w8, brow, dproj)


def _tri(lower):
    i = jnp.arange(CHUNK)
    return ((i[:, None] >= i[None, :]) if lower else (i[:, None] <= i[None, :])).astype(F32)


def _dt_fwd(raw, bias, arow, *, name):
    s = raw.shape[0]

    def body(r_ref, b_ref, a_ref, lo_ref, up_ref, dt_ref, cs_ref):
        u = r_ref[...] + b_ref[...]
        dt = jnp.maximum(u, 0.0) + jnp.log1p(jnp.exp(-jnp.abs(u)))
        dt_ref[...] = dt
        a = dt * a_ref[...]
        lane = lax.broadcasted_iota(jnp.int32, (CHUNK, 128), 1)
        cs_ref[...] = jnp.where(lane < SSD_HEADS, _dot_hi(lo_ref[...], a), _dot_hi(up_ref[...], a))

    blk = pl.BlockSpec((CHUNK, 128), lambda i: (i, 0))
    row = pl.BlockSpec((1, 128), lambda i: (0, 0))
    tri = pl.BlockSpec((CHUNK, CHUNK), lambda i: (0, 0))
    return pl.pallas_call(
        body, name=name, grid=(s // CHUNK,), in_specs=[blk, row, row, tri, tri], out_specs=[blk, blk],
        out_shape=[jax.ShapeDtypeStruct((s, 128), F32)] * 2, compiler_params=_cp(("arbitrary",)),
    )(raw, bias, arow, _tri(True), _tri(False))


def _dt_bwd(ddt0, ddt1, raw, bias, dproj, *, name):
    s = raw.shape[0]
    tm = _tile(s, 1024)

    def body(d0_ref, d1_ref, r_ref, b_ref, _, o_ref, db_ref):
        g = (d0_ref[...] + d1_ref[...]) * _sigmoid(r_ref[...] + b_ref[...])
        o_ref[...] = g.astype(o_ref.dtype)
        _acc_rows(db_ref, jnp.sum(g, axis=0, keepdims=True), pl.program_id(0) == 0)

    blk = pl.BlockSpec((tm, 128), lambda i: (i, 0))
    row = pl.BlockSpec((1, 128), lambda i: (0, 0))
    return pl.pallas_call(
        body, name=name, grid=(s // tm,), in_specs=[blk, blk, blk, row, ANY],
        out_specs=[pl.BlockSpec((tm, 128), lambda i: (i, DT0 // 128)), row],
        out_shape=[jax.ShapeDtypeStruct(dproj.shape, dproj.dtype), jax.ShapeDtypeStruct((1, 128), F32)],
        input_output_aliases={4: 0}, compiler_params=_cp(("arbitrary",)),
    )(ddt0, ddt1, raw, bias, dproj)


GW = HPG * SSD_P


GPS = SSD_GROUPS


def _ssd_specs(nc, rev):
    cc = (lambda c: nc - 1 - c) if rev else (lambda c: c)
    return dict(
        x=pl.BlockSpec((CHUNK, GPS * GW), lambda g, c: (cc(c), g)),
        b=pl.BlockSpec((CHUNK, GPS * SSD_N), lambda g, c: (cc(c), 2048 // (GPS * SSD_N) + g)),
        c=pl.BlockSpec((CHUNK, GPS * SSD_N), lambda g, c: (cc(c), 2048 // (GPS * SSD_N) + 1 + g)),
        lanes=pl.BlockSpec((CHUNK, 128), lambda g, c: (cc(c), 0)),
        drow=pl.BlockSpec((1, GPS * GW), lambda g, c: (0, g)),
        y=pl.BlockSpec((CHUNK, GPS * GW), lambda g, c: (cc(c), g)),
        h=pl.BlockSpec((GPS, 1, SSD_N, GW), lambda g, c: (g, cc(c), 0, 0)),
        n=pl.BlockSpec((CHUNK, GPS * SSD_N), lambda g, c: (cc(c), g)),
    )


def _ssd_mask(anti):
    ii = lax.broadcasted_iota(jnp.int32, (CHUNK, CHUNK), 0)
    jj = lax.broadcasted_iota(jnp.int32, (CHUNK, CHUNK), 1)
    return ii, jj, (ii <= jj) if anti else (ii >= jj)


def _expand(x, ex, terms=3):
    h1 = x.astype(jnp.bfloat16)
    r1 = x - h1.astype(F32)
    h2 = r1.astype(jnp.bfloat16)
    out = _dot(h1, ex) + _dot(h2, ex)
    if terms == 3:
        out = out + _dot((r1 - h2.astype(F32)).astype(jnp.bfloat16), ex)
    return out


def _headsum(a, e):
    hi = a.astype(jnp.bfloat16)
    return _dot(hi, e) + _dot((a - hi.astype(F32)).astype(jnp.bfloat16), e)


def _expand_mats():
    lane = jnp.arange(128)[None, :, None]
    col = jnp.arange(GW)[None, None, :]
    base = (jnp.arange(2)[:, None] * SSD_HEADS + jnp.arange(SSD_GROUPS)[None, :] * HPG).reshape(2 * SSD_GROUPS, 1, 1)
    return (lane == base + col // SSD_P).astype(jnp.bfloat16)


def _headsum_mats():
    e1 = (jnp.arange(GW)[:, None] // SSD_P == jnp.arange(128)[None, :]).astype(jnp.bfloat16)
    e2 = (jnp.arange(HPG * CHUNK)[:, None] // CHUNK == jnp.arange(128)[None, :]).astype(jnp.bfloat16)
    return e1, e2


def _ssd_fwd(xc, dt, cs, ex, drow, di, *, name):
    s = xc.shape[0]
    nc = s // CHUNK
    anti = di == 1
    sp = _ssd_specs(nc, anti)
    trow = 0 if anti else CHUNK - 1

    def body(x_ref, b_ref, c_ref, dt_ref, cs_ref, ex_ref, d_ref, y_ref, hp_ref, h_ref):
        @pl.when(pl.program_id(1) == 0)
        def _():
            h_ref[...] = jnp.zeros_like(h_ref)

        mask = _ssd_mask(anti)[2]
        dtv, csv = dt_ref[...], cs_ref[...]
        cst = csv.T
        for gi in range(GPS):
            cols = slice(gi * GW, (gi + 1) * GW)
            ncols = slice(gi * SSD_N, (gi + 1) * SSD_N)
            ex = ex_ref[gi]
            xb = x_ref[:, cols].astype(F32)
            bm, cm = b_ref[:, ncols], c_ref[:, ncols]
            csr = cst[SSD_HEADS * di + HPG * gi:SSD_HEADS * di + HPG * (gi + 1)]
            dtf = _expand(dtv, ex, 2)
            csf = _expand(csv, ex, 2)
            tl = csf[trow:trow + 1, :]
            h = h_ref[gi]
            hp_ref[gi, 0] = h.astype(hp_ref.dtype)
            g = _dot(cm, bm, NT)
            xs = xb * dtf
            xsm = xs.astype(MMD)
            base = jnp.exp(csf) * _dot(cm, h.astype(MMD)) + d_ref[:, cols] * xb
            for r in range(HPG):
                sl = slice(r * SSD_P, (r + 1) * SSD_P)
                lm = jnp.exp(jnp.where(mask, csf[:, r * SSD_P:r * SSD_P + 1] - csr[r:r + 1, :], NEG))
                y_ref[:, gi * GW + r * SSD_P:gi * GW + (r + 1) * SSD_P] = (
                    _dot((g * lm).astype(MMD), xsm[:, sl]) + base[:, sl]).astype(y_ref.dtype)
            xd = (xs * jnp.exp(tl - csf)).astype(MMD)
            h_ref[gi] = h * jnp.exp(tl) + _dot(bm, xd, TN)

    return pl.pallas_call(
        body, name=name, grid=(1, nc),
        in_specs=[sp["x"], sp["b"], sp["c"], sp["lanes"], sp["lanes"],
                  pl.BlockSpec((GPS, 128, GW), lambda g, c: (di, 0, 0)), sp["drow"]],
        out_specs=[sp["y"], sp["h"]],
        out_shape=[jax.ShapeDtypeStruct((s, 2048), MMD), jax.ShapeDtypeStruct((SSD_GROUPS, nc, SSD_N, GW), MMD)],
        scratch_shapes=[pltpu.VMEM((GPS, SSD_N, GW), F32)],
        compiler_params=_cp(("arbitrary", "arbitrary")),
    )(xc, xc, xc, dt, cs, ex, drow)


def _ssd_bwd(xc, dt, cs, ex, drow, arow, dy, hprev, di, *, name):
    s = xc.shape[0]
    nc = s // CHUNK
    anti = di == 1
    sp = _ssd_specs(nc, not anti)
    trow = 0 if anti else CHUNK - 1
    e1, e2 = _headsum_mats()

    def body(x_ref, b_ref, c_ref, dt_ref, cs_ref, ex_ref, d_ref, a_ref, dy_ref, hp_ref, tri_ref,
             e1_ref, e2_ref, dx_ref, db_ref, dc_ref, ddt_ref, da_ref, dh_ref, w_ref, dxs_ref):
        @pl.when(pl.program_id(1) == 0)
        def _():
            dh_ref[...] = jnp.zeros_like(dh_ref)
            da_ref[...] = jnp.zeros_like(da_ref)

        e1v = e1_ref[...]
        ii, _, mask = _ssd_mask(anti)
        dtv, csv = dt_ref[...], cs_ref[...]
        cst = csv.T
        ddt_acc = jnp.zeros((CHUNK, 128), F32)
        da_acc = jnp.zeros((1, 128), F32)
        for gi in range(GPS):
            lane0 = SSD_HEADS * di + HPG * gi
            cols = slice(gi * GW, (gi + 1) * GW)
            ncols = slice(gi * SSD_N, (gi + 1) * SSD_N)
            ex = ex_ref[gi]
            xb = x_ref[:, cols].astype(F32)
            bm, cm = b_ref[:, ncols], c_ref[:, ncols]
            csr = cst[lane0:lane0 + HPG]
            dym = dy_ref[:, cols]
            dyb = dym.astype(F32)
            hpm = hp_ref[gi, 0]
            hp = hpm.astype(F32)
            dh = dh_ref[gi]
            dhm = dh.astype(MMD)
            dtf = _expand(dtv, ex, 2)
            csf = _expand(csv, ex, 2)
            tl = csf[trow:trow + 1, :]
            e = jnp.exp(csf)
            dec = jnp.exp(tl - csf)
            et = jnp.exp(tl)
            xs = xb * dtf
            xsm = xs.astype(MMD)
            g = _dot(cm, bm, NT)
            z = _dot(cm, hpm)
            bdh = _dot(bm, dhm)
            dg = jnp.zeros((CHUNK, CHUNK), F32)
            wcols = jnp.zeros((CHUNK, CHUNK), F32)
            for r in range(HPG):
                sl = slice(r * SSD_P, (r + 1) * SSD_P)
                lm = jnp.exp(jnp.where(mask, csf[:, r * SSD_P:r * SSD_P + 1] - csr[r:r + 1, :], NEG))
                mm = g * lm
                dm = _dot(dym[:, sl], xsm[:, sl], NT)
                w = dm * mm
                w_ref[gi, :, r * CHUNK:(r + 1) * CHUNK] = w
                wcols = jnp.where(ii == r, jnp.sum(w, axis=0, keepdims=True), wcols)
                dg = dg + dm * lm
                dxs_ref[gi, :, sl] = _dot(mm.astype(MMD), dym[:, sl], TN)
            dxs = dxs_ref[gi] + dec * bdh
            dx_ref[:, cols] = (dxs * dtf + d_ref[:, cols] * dyb).astype(dx_ref.dtype)
            tb = xs * bdh * dec
            d_tot = jnp.sum(tb, axis=0, keepdims=True) + et * jnp.sum(dh * hp, axis=0, keepdims=True)
            d_tot = _headsum(jnp.broadcast_to(d_tot, (8, GW)), e1v)[0:1]
            dcs = (_headsum(dyb * (e * z) - tb, e1v) + _headsum(w_ref[gi], e2_ref[...]) - wcols.T
                   + jnp.where(ii == trow, d_tot, 0.0))
            da = pltpu.roll(_dot_hi(tri_ref[...], dcs), lane0, 1)
            ddt_acc = ddt_acc + da * a_ref[...] + pltpu.roll(_headsum(dxs * xb, e1v), lane0, 1)
            da_acc = da_acc + jnp.sum(da * dtv, axis=0, keepdims=True)
            dgm = dg.astype(MMD)
            dz = (e * dyb).astype(MMD)
            dc_ref[:, ncols] = (_dot(dgm, bm) + _dot(dz, hpm, NT)).astype(dc_ref.dtype)
            db_ref[:, ncols] = (_dot(dgm, cm, TN) + _dot((xs * dec).astype(MMD), dhm, NT)).astype(db_ref.dtype)
            dh_ref[gi] = dh * et + _dot(cm, dz, TN)
        ddt_ref[...] = ddt_acc
        da_ref[...] += da_acc

    const = lambda shape: pl.BlockSpec(shape, lambda g, c: (0,) * len(shape))
    return pl.pallas_call(
        body, name=name, grid=(1, nc),
        in_specs=[sp["x"], sp["b"], sp["c"], sp["lanes"], sp["lanes"],
                  pl.BlockSpec((GPS, 128, GW), lambda g, c: (di, 0, 0)), sp["drow"],
                  const((1, 128)), sp["y"], sp["h"],
                  const((CHUNK, CHUNK)), const((GW, 128)), const((HPG * CHUNK, 128))],
        out_specs=[sp["y"], sp["n"], sp["n"], sp["lanes"], const((1, 128))],
        out_shape=[jax.ShapeDtypeStruct((s, 2048), MMD), jax.ShapeDtypeStruct((s, SSD_GROUPS * SSD_N), MMD),
                   jax.ShapeDtypeStruct((s, SSD_GROUPS * SSD_N), MMD), jax.ShapeDtypeStruct((s, 128), F32),
                   jax.ShapeDtypeStruct((1, 128), F32)],
        scratch_shapes=[pltpu.VMEM((GPS, SSD_N, GW), F32), pltpu.VMEM((GPS, CHUNK, HPG * CHUNK), F32),
                        pltpu.VMEM((GPS, CHUNK, GW), F32)],
        compiler_params=_cp(("arbitrary", "arbitrary")),
    )(xc, xc, xc, dt, cs, ex, drow, arow, dy, hprev, _tri(anti), e1, e2)


def _gnorm_fwd(ya, yb, proj, w, *, name):
    s = ya.shape[0]
    tm = _tile(s, 256)

    def body(a_ref, b_ref, z_ref, w_ref, o_ref):
        zv = z_ref[...].astype(F32)
        t = (a_ref[...].astype(F32) + b_ref[...].astype(F32)) * (zv * _sigmoid(zv))
        r = lax.rsqrt(jnp.mean(t * t, axis=-1, keepdims=True) + EPS)
        o_ref[...] = ((t * r) * w_ref[...]).astype(o_ref.dtype)

    big = pl.BlockSpec((tm, 2048), lambda i: (i, 0))
    row = pl.BlockSpec((1, 2048), lambda i: (0, 0))
    return pl.pallas_call(
        body, name=name, grid=(s // tm,), in_specs=[big, big, big, row], out_specs=big,
        out_shape=jax.ShapeDtypeStruct((s, 2048), MMD), compiler_params=_cp(("arbitrary",)),
    )(ya, yb, proj, w)


def _gnorm_bwd(dout, ya, yb, proj, xc, w, dproj, *, name):
    s = ya.shape[0]
    tm = _tile(s, 256)

    def body(do_ref, a_ref, b_ref, z_ref, x_ref, w_ref, _, dy_ref, dz_ref, dw_ref, dd_ref):
        zv = z_ref[...].astype(F32)
        sg = _sigmoid(zv)
        sz = zv * sg
        y = a_ref[...].astype(F32) + b_ref[...].astype(F32)
        t = y * sz
        r = lax.rsqrt(jnp.mean(t * t, axis=-1, keepdims=True) + EPS)
        nv = t * r
        dov = do_ref[...].astype(F32)
        _acc_rows(dw_ref, jnp.sum(dov * nv, axis=0, keepdims=True), pl.program_id(0) == 0)
        dn = dov * w_ref[...]
        dt_ = r * (dn - nv * jnp.mean(dn * nv, axis=-1, keepdims=True))
        dy = dt_ * sz
        dy_ref[...] = dy.astype(dy_ref.dtype)
        dz_ref[...] = (dt_ * y * (sg * (1.0 + zv * (1.0 - sg)))).astype(dz_ref.dtype)
        _acc_rows(dd_ref, jnp.sum(dy * x_ref[...].astype(F32), axis=0, keepdims=True), pl.program_id(0) == 0)

    big = pl.BlockSpec((tm, 2048), lambda i: (i, 0))
    row = pl.BlockSpec((1, 2048), lambda i: (0, 0))
    return pl.pallas_call(
        body, name=name, grid=(s // tm,), in_specs=[big, big, big, big, big, row, ANY], out_specs=[big, big, row, row],
        out_shape=[jax.ShapeDtypeStruct((s, 2048), MMD), jax.ShapeDtypeStruct(dproj.shape, dproj.dtype),
                   jax.ShapeDtypeStruct((1, 2048), F32), jax.ShapeDtypeStruct((1, 2048), F32)],
        input_output_aliases={6: 1}, compiler_params=_cp(("arbitrary",)),
    )(dout, ya, yb, proj, xc, w, dproj)


def _unheads(a):
    return a.transpose(1, 0, 2).reshape(a.shape[1], a.shape[0] * HEAD_DIM)


def _local_step(x, target, mod, wts, small, in_weights=None, late_weights=None, late_grads=None, in_grad=None,
                zero=0.0):
    s, d = x.shape
    shift1, scale1, gate1, shift2, scale2, gate2 = [mod[i:i + 1] for i in range(6)]

    h1 = _ln_mod(x, small["norm1_w"], scale1, shift1, name="ln1")
    qk_w = jnp.concatenate([jnp.tile(small["q_norm_w"], (1, N_Q_HEADS)), jnp.tile(small["k_norm_w"], (1, N_KV_HEADS))], axis=1)
    qk_sc = jnp.concatenate([jnp.full((1, N_Q_HEADS * HEAD_DIM), HEAD_DIM ** -0.5, F32),
                             jnp.ones((1, N_KV_HEADS * HEAD_DIM), F32)], axis=1)
    qk_sc2 = jnp.concatenate([jnp.full((1, N_Q_HEADS * HEAD_DIM), HEAD_DIM ** -0.5 * LOG2E, F32),
                              jnp.ones((1, N_KV_HEADS * HEAD_DIM), F32)], axis=1)
    tabs = _rope_tables(s, zero)
    if in_weights is not None:
        wts = {**wts, **in_weights([h1, *tabs])}
    proj = _mm(h1, wts["w_in_p"], name="in_proj", outs=[MMD], tm=512, tn=2944, b_outer=True)
    dt_raw = _mm(h1, wts["w_dt"], name="dt_proj", outs=[F32], tm=512, tn=128)
    qkt = _qk_fwd(proj, qk_w, qk_sc2, tabs, name="qk_fwd").reshape(N_Q_HEADS + N_KV_HEADS, HEAD_DIM, s)
    v_sd = proj[:, V0:V0 + N_KV_HEADS * HEAD_DIM]
    vta = jnp.concatenate([v_sd.T.reshape(N_KV_HEADS, HEAD_DIM, s), jnp.ones((N_KV_HEADS, V_AUG - HEAD_DIM, s), MMD)], axis=1)
    ot, lse = _flash_fwd(qkt, vta, name="flash_fwd")
    ot2 = ot.reshape(N_Q_HEADS * HEAD_DIM, s)
    if late_weights is not None:
        wts = {**wts, **late_weights(ot)}

    w8 = jnp.pad(small["conv_w"], ((0, 8 - D_CONV), (0, 0)))
    xc = _conv_fwd(proj, w8, small["conv_b"], name="conv_fwd")
    a_neg = -jnp.exp(small["A_log"])
    arow = jnp.pad(a_neg.reshape(1, 2 * SSD_HEADS), ((0, 0), (0, 128 - 2 * SSD_HEADS)))
    bias_row = jnp.pad(small["dt_bias"].reshape(1, 2 * SSD_HEADS), ((0, 0), (0, 128 - 2 * SSD_HEADS)))
    dt, cs = _dt_fwd(dt_raw, bias_row, arow, name="dt_fwd")
    drow = jnp.repeat(small["ssd_D"], SSD_P, axis=1)
    dirs = [dict(drow=drow), dict(drow=jnp.zeros_like(drow))]
    ex = _expand_mats()
    ys = []
    for di, dd in enumerate(dirs):
        y, dd["hprev"] = _ssd_fwd(xc, dt, cs, ex, dd["drow"], di, name=f"ssd_fwd{di}")
        ys.append(y)
    ssdn = _gnorm_fwd(ys[0], ys[1], proj, small["ssd_norm_w"], name="gnorm_fwd")

    a_o = _mm(ot2, wts["w_attn_out"], name="attn_out", outs=[MMD], ta=True, tm=512, tn=1024)

    def merge_epi(acc, ao, ga, gs):
        return (_sigmoid(ga.astype(F32)) * ao.astype(F32) + _sigmoid(gs.astype(F32)) * acc, acc)

    merged, b_o = _mm(ssdn, wts["w_ssd_out"], name="ssd_out", outs=[MMD, MMD], tm=512, tn=1024,
                      extras=[(a_o, "tile", 0), (proj, "tile", GA0), (proj, "tile", GS0)], epi=merge_epi)

    def res_epi(acc, res, gate):
        return (res + gate * acc, acc)

    x1, mo = _mm(merged, wts["w_o"], name="w_o", outs=[F32, MMD], tm=512, tn=1024,
                 extras=[(x, "tile", 0), (gate1, "row", 0)], epi=res_epi)
    h2 = _ln_mod(x1, small["norm2_w"], scale2, shift2, name="ln2")

    def relu2_epi(acc):
        rl = jnp.maximum(acc, 0.0)
        return (rl * rl, rl)

    act, rl = _mm(h2, wts["w_mlp1"], name="mlp1", outs=[MMD, MMD], tm=1024, tn=1024, epi=relu2_epi, b_outer=True)

    def loss_epi(acc, res, gate, tgt):
        return ((res + gate * acc - tgt) * (1.0 / d), acc)

    dy, ffo = _mm(act, wts["w_mlp2"], name="mlp2", outs=[F32, MMD], tm=512, tn=1024, vmem=VMEM_BIG,
                  extras=[(x1, "tile", 0), (gate2, "row", 0), (target, "tile", 0)], epi=loss_epi)
    loss = _sumsq(dy, name="loss") * (0.5 * d)

    gw = {}
    gs_ = {}
    dffo, dgate2 = _gate_bwd(dy, ffo, gate2, name="gate2_bwd")
    dpre = _mm(dffo, wts["w_mlp2"], name="mlp2_dx", outs=[MMD], nt=True, tm=1024, tn=1024, b_outer=True,
               extras=[(rl, "tile", 0)], epi=lambda acc, r: (acc * (2.0 * r.astype(F32)),))
    gw["w_mlp2"] = _mm_tn(act, dffo, name="mlp2_dw")
    dh2 = _mm(dpre, wts["w_mlp1"], name="mlp1_dx", outs=[F32], nt=True, tm=1024, tn=1024, vmem=VMEM_BIG)
    gw["w_mlp1"] = _mm_tn(h2, dpre, name="mlp1_dw")
    dx1, dshift2, dscale2, gs_["norm2_w"] = _ln_mod_bwd(dh2, x1, small["norm2_w"], scale2, dy, name="ln2_bwd")
    dmo, dgate1 = _gate_bwd(dx1, mo, gate1, name="gate1_bwd")

    def merge_bwd_epi(acc, ao, bo, ga, gs):
        sa, ss = _sigmoid(ga.astype(F32)), _sigmoid(gs.astype(F32))
        return (acc * sa, acc * ss, acc * ao.astype(F32) * sa * (1.0 - sa), acc * bo.astype(F32) * ss * (1.0 - ss))

    da_o, db_o, dga, dgs = _mm(dmo, wts["w_o"], name="w_o_dx", outs=[MMD] * 4, nt=True, tm=512, tn=1024,
                               extras=[(a_o, "tile", 0), (b_o, "tile", 0), (proj, "tile", GA0), (proj, "tile", GS0)],
                               epi=merge_bwd_epi)
    gw["w_o"] = _mm_tn(merged, dmo, name="w_o_dw")
    dot = _mm(wts["w_attn_out"], da_o, name="attn_out_dx", outs=[MMD], nt=True, tm=1024, tn=1024)
    gw["w_attn_out"] = _mm(ot2, da_o, name="attn_out_dw", outs=[F32], tm=256, tn=512, vmem=VMEM_BIG)
    dssdn = _mm(db_o, wts["w_ssd_out"], name="ssd_out_dx", outs=[MMD], nt=True, tm=512, tn=2048)
    gw["w_ssd_out"] = _mm_tn(ssdn, db_o, name="ssd_out_dw")

    dproj = lax.dynamic_update_slice(lax.empty((s, PW), MMD), jnp.concatenate([dga, dgs], axis=1), (0, GA0))

    norm_w = small["ssd_norm_w"] if late_grads is None else small["ssd_norm_w"] + late_grads(gw)
    dyssd, dproj, gs_["ssd_norm_w"], dd_row = _gnorm_bwd(dssdn, ys[0], ys[1], proj, xc, norm_w, dproj, name="gnorm_bwd")
    gs_["ssd_D"] = dd_row.reshape(SSD_HEADS, SSD_P).sum(axis=1).reshape(1, SSD_HEADS)
    dxc, ddts, das = [], [], []
    for di, dd in enumerate(dirs):
        dxs, dbm, dcm, ddt_d, da_d = _ssd_bwd(xc, dt, cs, ex, dd["drow"], arow, dyssd, dd["hprev"], di, name=f"ssd_bwd{di}")
        dxc.append((dxs, dbm, dcm))
        ddts.append(ddt_d)
        das.append(da_d)
    dw8, db, col0 = [], [], 0
    for part, (ga, gb) in enumerate(zip(*dxc)):
        dproj, dw_part, db_part = _conv_bwd(proj, col0, ga, gb, w8, small["conv_b"], dproj, name=f"conv_bwd{part}")
        dw8.append(dw_part)
        db.append(db_part)
        col0 += ga.shape[1]
    gs_["conv_w"] = jnp.concatenate(dw8, axis=1)[0:D_CONV]
    gs_["conv_b"] = jnp.concatenate(db, axis=1)
    gs_["A_log"] = (das[0] + das[1])[:, 0:2 * SSD_HEADS].reshape(2, SSD_HEADS) * a_neg
    dproj, dbias = _dt_bwd(ddts[0], ddts[1], dt_raw, bias_row, dproj, name="dt_bwd")
    gs_["dt_bias"] = dbias[:, 0:2 * SSD_HEADS].reshape(2, SSD_HEADS)

    dqt, dk_h, dv_h = _flash_bwd(qkt, vta, dot.reshape(N_Q_HEADS, HEAD_DIM, s), ot, lse, name="flash_bwd")
    dproj, dqk_w = _qk_bwd(dqt.reshape(N_Q_HEADS * HEAD_DIM, s), dk_h.transpose(0, 2, 1).reshape(N_KV_HEADS * HEAD_DIM, s),
                           proj, qk_w, qk_sc, tabs, dproj, name="qk_bwd")
    gs_["q_norm_w"] = dqk_w[:, 0:N_Q_HEADS * HEAD_DIM].reshape(N_Q_HEADS, HEAD_DIM).sum(axis=0, keepdims=True)
    gs_["k_norm_w"] = dqk_w[:, N_Q_HEADS * HEAD_DIM:].reshape(N_KV_HEADS, HEAD_DIM).sum(axis=0, keepdims=True)
    dproj = lax.dynamic_update_slice(dproj, _unheads(dv_h).astype(MMD), (0, V0))

    gw["w_in_p"] = _mm_tn(h1, dproj, name="in_proj_dw", tk=512, tn=2944, tmm=2048, vmem=VMEM_BIG)
    zero_row = jnp.zeros((1, d), F32) if in_grad is None else jnp.zeros((1, d), F32) + in_grad(gw["w_in_p"])[0:1, 0:1]
    dh1 = _mm(dproj, wts["w_in_p"], name="in_proj_dx", outs=[F32], nt=True, tm=256, tn=1024, vmem=VMEM_BIG,
              extras=[(zero_row, "row", 0)], epi=lambda acc, r: (acc + r,))
    grad_x, dshift1, dscale1, gs_["norm1_w"] = _ln_mod_bwd(dh1, x, small["norm1_w"], scale1, dx1, name="ln1_bwd")
    dmod = jnp.concatenate([dshift1, dscale1, dgate1, dshift2, dscale2, dgate2], axis=0)
    return loss, grad_x, dmod, gw, gs_


N_DEV = 8
N_CHIP = 4
ANY = pl.BlockSpec(memory_space=pl.ANY)


def _place():
    return lax.axis_index("x"), lax.axis_index("y"), lax.axis_index("c")


def _allgather8(v, *, name):
    m_per, n = v.shape

    def body(x_ref, out_ref, send_sems, recv_sems, local_sem):
        x, y, c = _place()
        me, sibling = (x, y, c), (x, y, 1 - c)
        chips = [(1 - x, y), (x, 1 - y), (1 - x, 1 - y)]

        def rows(px, py, pc):
            return out_ref.at[pl.ds((4 * px + 2 * py + pc) * m_per, m_per), :]

        def copy(k, block, to, src=None):
            return pltpu.make_async_remote_copy(
                src_ref=rows(*block) if src is None else src, dst_ref=rows(*block),
                send_sem=send_sems.at[k], recv_sem=recv_sems.at[k], device_id=to, device_id_type=MESH)

        mine = pltpu.make_async_copy(x_ref, rows(*me), local_sem)
        mine.start()
        first = [copy(0, me, sibling, src=x_ref)]
        first += [copy(1 + j, me, (*chip, c), src=x_ref) for j, chip in enumerate(chips)]
        for cp in first:
            cp.start()
        passed = [copy(4 + j, (*chip, c), sibling) for j, chip in enumerate(chips)]
        for j, chip in enumerate(chips):
            copy(1 + j, (*chip, c), me).wait_recv()
            passed[j].start()
        copy(0, sibling, me).wait_recv()
        for j, chip in enumerate(chips):
            copy(4 + j, (*chip, 1 - c), me).wait_recv()
        for cp in first + passed:
            cp.wait_send()
        mine.wait()

    return pl.pallas_call(
        body, name=name, out_shape=jax.ShapeDtypeStruct((N_DEV * m_per, n), v.dtype),
        in_specs=[pl.BlockSpec(memory_space=pltpu.VMEM)], out_specs=pl.BlockSpec(memory_space=pltpu.VMEM),
        scratch_shapes=[pltpu.SemaphoreType.DMA((7,)), pltpu.SemaphoreType.DMA((7,)), pltpu.SemaphoreType.DMA],
    )(v)


HBM = pl.BlockSpec(memory_space=pltpu.HBM)
SEM = pl.BlockSpec(memory_space=pltpu.SEMAPHORE)


def _chips_copies(x_ref, land_ref, sems, scatter, half=False):
    x, y, c = _place()
    k = 2 * x + y
    chips = [(1 - x, y), (x, 1 - y), (1 - x, 1 - y)]
    ids = [2 * cx + cy for cx, cy in chips]
    if half:
        hr = x_ref.shape[0] // 2
        rows = pl.ds(pl.multiple_of(c * hr, 16), hr)

    def copy(j, slot):
        src = x_ref.at[ids[j]] if scatter else (x_ref.at[rows] if half else x_ref)
        dst = land_ref.at[slot, rows] if half else land_ref.at[slot]
        return pltpu.make_async_remote_copy(src_ref=src, dst_ref=dst, send_sem=sems[j], recv_sem=sems[3 + j],
                                            device_id=(*chips[j], c), device_id_type=MESH)

    return [copy(j, k) for j in range(3)], [copy(j, ids[j]) for j in range(3)]


def _chips_start(src, scatter, half=False, *, name):
    shape = src.shape if scatter else (N_CHIP,) + tuple(src.shape)

    def body(x_ref, land_ref, *rest):
        sems, token = rest[0:6], rest[8]
        for cp in _chips_copies(x_ref, land_ref, sems, scatter, half)[0]:
            cp.start()
        token[...] = jnp.zeros_like(token)

    out = pl.pallas_call(
        body, name=name,
        out_shape=(pltpu.SemaphoreType.DMA(()),) * 6 + (pltpu.HBM(src.shape, src.dtype), pltpu.HBM(shape, src.dtype),
                                                       jax.ShapeDtypeStruct((8, 128), F32)),
        in_specs=(HBM, HBM), out_specs=(SEM,) * 6 + (HBM, HBM, pl.BlockSpec(memory_space=pltpu.VMEM)),
        input_output_aliases={0: 6, 1: 7},
        compiler_params=pltpu.CompilerParams(has_side_effects=pltpu.SideEffectType.DATAFLOW_SIDE_EFFECTING),
    )(pltpu.with_memory_space_constraint(src, pltpu.HBM),
      pltpu.with_memory_space_constraint(lax.empty(shape, src.dtype), pltpu.HBM))
    return out[0:6], out[6], out[7], out[8]


def _chips_wait(sems, src, land, after, scatter, half=False, *, name):
    after = list(after) if isinstance(after, (list, tuple)) else [after]

    def body(x_ref, land_ref, *rest):
        sems_ = rest[0:6]
        for cp in _chips_copies(x_ref, land_ref, sems_, scatter, half)[1]:
            cp.wait_send()
            cp.wait_recv()

    return pl.pallas_call(
        body, name=name, out_shape=(pltpu.HBM(src.shape, src.dtype), pltpu.HBM(land.shape, land.dtype)),
        in_specs=(HBM, HBM) + (SEM,) * 6 + (ANY,) * len(after), out_specs=(HBM, HBM), input_output_aliases={0: 0, 1: 1},
        compiler_params=pltpu.CompilerParams(has_side_effects=pltpu.SideEffectType.DATAFLOW_SIDE_EFFECTING),
    )(src, land, *sems, *after)


def _row_tile(r, pref=512):
    return max(t for t in range(16, pref + 1, 16) if r % t == 0)


def _pair_complete(land, *, name):
    r = land.shape[1]
    hr = r // 2
    assert r == 2 * hr and hr % 16 == 0

    def body(in_ref, out_ref, send_sems, recv_sems):
        x, y, c = _place()
        ids = [2 * cx + cy for cx, cy in [(1 - x, y), (x, 1 - y), (1 - x, 1 - y)]]
        mine_rows = pl.ds(pl.multiple_of(c * hr, 16), hr)
        other_rows = pl.ds(pl.multiple_of((1 - c) * hr, 16), hr)

        def copy(j, rows):
            return pltpu.make_async_remote_copy(
                src_ref=in_ref.at[ids[j], mine_rows], dst_ref=out_ref.at[ids[j], rows], send_sem=send_sems.at[j],
                recv_sem=recv_sems.at[j], device_id=(x, y, 1 - c), device_id_type=MESH)

        sends = [copy(j, mine_rows) for j in range(3)]
        for cp in sends:
            cp.start()
        for j in range(3):
            copy(j, other_rows).wait_recv()
        for cp in sends:
            cp.wait_send()

    return pl.pallas_call(
        body, name=name, out_shape=jax.ShapeDtypeStruct(land.shape, land.dtype), in_specs=[ANY], out_specs=ANY,
        input_output_aliases={0: 0},
        scratch_shapes=[pltpu.SemaphoreType.DMA((3,)), pltpu.SemaphoreType.DMA((3,))],
    )(land)


def _pair_swap(a, *, name):
    n, r, cols = a.shape
    hr = r // 2

    def body(x_ref, out_ref, send_sem, recv_sem):
        x, y, c = _place()
        other_rows = pl.ds(pl.multiple_of((1 - c) * hr, 16), hr)
        cp = pltpu.make_async_remote_copy(src_ref=x_ref.at[:, other_rows], dst_ref=out_ref, send_sem=send_sem,
                                          recv_sem=recv_sem, device_id=(x, y, 1 - c), device_id_type=MESH)
        cp.start()
        cp.wait()

    return pl.pallas_call(
        body, name=name, out_shape=jax.ShapeDtypeStruct((n, hr, cols), a.dtype), in_specs=[ANY], out_specs=ANY,
        scratch_shapes=[pltpu.SemaphoreType.DMA, pltpu.SemaphoreType.DMA],
    )(a)


def _sibling_copy(a, *, name):
    def body(x_ref, out_ref, send_sem, recv_sem):
        x, y, c = _place()
        cp = pltpu.make_async_remote_copy(src_ref=x_ref, dst_ref=out_ref, send_sem=send_sem, recv_sem=recv_sem,
                                          device_id=(x, y, 1 - c), device_id_type=MESH)
        cp.start()
        cp.wait()

    return pl.pallas_call(
        body, name=name, out_shape=jax.ShapeDtypeStruct(a.shape, a.dtype), in_specs=[ANY], out_specs=ANY,
        scratch_shapes=[pltpu.SemaphoreType.DMA, pltpu.SemaphoreType.DMA],
    )(a)


def _sum_slots(a, own, *, name):
    _, r, c = a.shape
    tr = _row_tile(r, 256)

    def body(a_ref, own_ref, o_ref):
        k = 2 * lax.axis_index("x") + lax.axis_index("y")
        acc = None
        for j in range(N_CHIP):
            term = jnp.where(k == j, own_ref[j], a_ref[j]).astype(F32)
            acc = term if acc is None else acc + term
        o_ref[...] = acc

    spec = pl.BlockSpec((N_CHIP, tr, c), lambda i: (0, i, 0))
    return pl.pallas_call(
        body, name=name, grid=(r // tr,), in_specs=[spec, spec],
        out_specs=pl.BlockSpec((tr, c), lambda i: (i, 0)), out_shape=jax.ShapeDtypeStruct((r, c), F32),
        compiler_params=_cp(("arbitrary",)),
    )(a, own)


def _add2(a, b, *, name):
    r, c = a.shape
    tr = _row_tile(r)

    def body(a_ref, b_ref, o_ref):
        o_ref[...] = (a_ref[...].astype(F32) + b_ref[...].astype(F32)).astype(o_ref.dtype)

    spec = pl.BlockSpec((tr, c), lambda i: (i, 0))
    return pl.pallas_call(
        body, name=name, grid=(r // tr,), in_specs=[spec, spec], out_specs=spec,
        out_shape=jax.ShapeDtypeStruct((r, c), a.dtype), compiler_params=_cp(("arbitrary",)),
    )(a, b)


BIG = ("w_in", "w_mlp1", "w_attn_out", "w_ssd_out", "w_o", "w_mlp2")
COL_SHARDED = ("w_mlp1", "w_in")
ROW_SHARDED = ("w_attn_out", "w_ssd_out", "w_o", "w_mlp2")
LATE = ROW_SHARDED + ("w_mlp1",)
SMALL = ("b_ada", "norm1_w", "norm2_w", "q_norm_w", "k_norm_w", "conv_b", "A_log", "dt_bias", "ssd_D", "ssd_norm_w")
NAMES = ("w_ada", "b_ada", "norm1_w", "norm2_w", "w_in", "q_norm_w", "k_norm_w", "conv_w", "conv_b", "A_log", "dt_bias",
         "ssd_D", "ssd_norm_w", "w_attn_out", "w_ssd_out", "w_o", "w_mlp1", "w_mlp2")
W_IN_COLS = 8768


def _permute_in(w):
    return jnp.concatenate([w[:, 4608:6656], w[:, 6720:8768], w[:, 1536:4608], w[:, 0:1536], w[:, 6656:6720],
                            jnp.zeros((w.shape[0], PW - W_IN_COLS), w.dtype)], axis=1)


def _unpermute_in(wp):
    return jnp.concatenate([wp[:, Q0:DT0], wp[:, XS0:Q0], wp[:, Z0:GA0], wp[:, DT0:DT0 + 64], wp[:, GA0:XS0]], axis=1)


def _pad_to(v, n):
    return jnp.pad(v, (0, n - v.shape[0]))


def _step(w, m, v, loss_target):
    xi, yi, ci = _place()
    chip = 2 * xi + yi
    dev = 4 * xi + 2 * yi + ci
    x, tgt = w["x"], loss_target
    d = x.shape[1]

    cw = w["conv_w"].shape[1]
    v0 = _pad_to(jnp.concatenate([w["c"].reshape(-1), w["conv_w"].reshape(-1)]), 5120).reshape(8, 640)
    g0 = _allgather8(v0, name="ag_cond").reshape(N_DEV, 5120)
    c_all = g0[:, 0:d]
    conv_w = jnp.concatenate([g0[2 * k, d:d + D_CONV * cw].reshape(D_CONV, cw) for k in range(N_CHIP)], axis=1)
    sc = _silu_cast(c_all, name="silu_c")
    modp = _mm(sc, w["w_ada"].astype(MMD), name="ada_fwd", outs=[F32], tm=8, tn=512)
    g1 = _allgather8(modp, name="ag_mod").reshape(N_DEV, N_DEV, modp.shape[1])
    mod_all = jnp.concatenate([g1[2 * k] for k in range(N_CHIP)], axis=1)
    mod = (lax.dynamic_slice_in_dim(mod_all, dev, 1, axis=0) + w["b_ada"]).reshape(6, d)

    mine, mod = lax.optimization_barrier((w["w_in"].astype(MMD), mod))
    in_sems, in_src, in_land, in_token = _chips_start(mine, False, True, name="ag_w_in_start")
    mod = mod + in_token[0:1, 0:1]
    small = {n: w[n] for n in SMALL if n != "b_ada"}
    small["conv_w"] = conv_w
    started = {}

    late_mine = jnp.concatenate([w[n].astype(MMD) for n in LATE], axis=0) + in_token[0:1, 0:1].astype(MMD)

    def in_weights(after):
        src, land = _chips_wait(in_sems, in_src, in_land, [*after, late_mine], False, True, name="ag_w_in_wait")
        land = _pair_complete(land, name="ag_w_in_pair")
        late, land = lax.optimization_barrier((late_mine, land))
        sems, late_src, late_land, token = _chips_start(late, False, name="ag_late_start")
        started["ag_late"] = (sems, late_src, late_land)
        w_in = jnp.concatenate([jnp.where(chip == k, src, land[k]) for k in range(N_CHIP)], axis=1)
        w_dt = jnp.pad(w_in[:, 6656:6720], ((0, 0), (0, 64))) + token[0:1, 0:1].astype(MMD)
        return {"w_in_p": _permute_in(w_in), "w_dt": w_dt}

    def late_weights(after):
        src, land = _chips_wait(*started["ag_late"], after, False, name="ag_late_wait")
        out, o = {}, 0
        for n in LATE:
            rows = w[n].shape[0]
            parts = [jnp.where(chip == k, src[o:o + rows], land[k, o:o + rows]) for k in range(N_CHIP)]
            out[n] = jnp.concatenate(parts, axis=1 if n in COL_SHARDED else 0)
            o += rows
        return out

    def pair_sums(slots, tag):
        _, rows, cols = slots.shape
        hr = rows // 2
        theirs = _pair_swap(slots, name="rs_pair_" + tag)
        ours = lax.dynamic_slice_in_dim(slots, ci * hr, hr, axis=1)
        pair = _add2(ours.reshape(N_CHIP * hr, cols), theirs.reshape(N_CHIP * hr, cols), name="rs_pair_sum_" + tag)
        return pair.reshape(N_CHIP, hr, cols)

    def finish(recv, pair, tag):
        half = _sum_slots(recv, pair, name="rs_sum_" + tag)
        other = _sibling_copy(half, name="rs_sibling_" + tag)
        return jnp.where(ci == 0, jnp.concatenate([half, other], axis=0), jnp.concatenate([other, half], axis=0))

    def late_grads(gw):
        slots = []
        for k in range(N_CHIP):
            parts = []
            for n in LATE:
                rows = w[n].shape[0]
                blk = gw[n][:, k * rows:(k + 1) * rows] if n in COL_SHARDED else gw[n][k * rows:(k + 1) * rows]
                parts.append(blk.astype(MMD))
            slots.append(jnp.concatenate(parts, axis=0))
        pair = pair_sums(jnp.stack(slots), "late")
        sems, src, land, token = _chips_start(pair, True, name="rs_late_start")
        started["late"] = (sems, src, land)
        return token[0:1, 0:1]

    def in_grad(g):
        g_in = _unpermute_in(g)
        cols_in = w["w_in"].shape[1]
        pair = pair_sums(jnp.stack([g_in[:, k * cols_in:(k + 1) * cols_in].astype(MMD) for k in range(N_CHIP)]), "w_in")
        sems, src, land, token = _chips_start(pair, True, name="rs_w_in_start")
        started["w_in"] = (sems, src, land)
        return token

    loss, grad_x, dmod, gw, gs = _local_step(x, tgt, mod, {}, small, in_weights, late_weights, late_grads, in_grad,
                                             in_token[0, 0])

    grads = {}
    pair, land = _chips_wait(*started["w_in"], grad_x, True, name="rs_w_in_wait")
    grads["w_in"] = finish(land, pair, "w_in")
    pair, land = _chips_wait(*started["late"], grad_x, True, name="rs_late_wait")
    total, o = finish(land, pair, "late"), 0
    for n in LATE:
        rows = w[n].shape[0]
        grads[n] = total[o:o + rows]
        o += rows

    order = ([dmod.reshape(-1)] + [gs[n].reshape(-1) for n in SMALL if n != "b_ada"] + [gs["conv_w"].reshape(-1)]
             + [loss.reshape(-1)])
    vec = jnp.concatenate(order)
    n_small = vec.shape[0]
    n_pad = -(-n_small // 1024) * 1024
    g2 = _allgather8(_pad_to(vec, n_pad).reshape(8, n_pad // 8), name="ag_small")
    tot = _rows_sum(g2, N_DEV, name="small_sum").reshape(-1)
    loss = tot[n_small - 1]
    dmod_all = g2.reshape(N_DEV, n_pad)[:, 0:6 * d]
    off = 0
    for n in SMALL:
        grads[n] = tot[off:off + w[n].size].reshape(w[n].shape)
        off += w[n].size
    conv_full = tot[off:off + D_CONV * N_CHIP * cw].reshape(D_CONV, N_CHIP * cw)
    grads["conv_w"] = lax.dynamic_slice_in_dim(conv_full, chip * cw, cw, axis=1)
    ada_cols = w["w_ada"].shape[1]
    dmod_mine = lax.dynamic_slice_in_dim(dmod_all, chip * ada_cols, ada_cols, axis=1).astype(MMD)
    grads["w_ada"] = _mm_tn(sc, dmod_mine, name="ada_dw", tk=512, tn=512, tmm=8)

    delta, new_m, new_v = {}, {}, {}
    pack = lambda t: jnp.concatenate([t[n].reshape(-1) for n in SMALL]).reshape(1, -1)
    ds_, ms_, vs_ = _adamw(pack(w), pack(grads), pack(m), pack(v), name="adamw_small")
    off = 0
    for n in SMALL:
        for dst, src in ((delta, ds_), (new_m, ms_), (new_v, vs_)):
            dst[n] = src[0, off:off + w[n].size].reshape(w[n].shape)
        off += w[n].size
    for n in ("w_ada", "conv_w") + BIG:
        delta[n], new_m[n], new_v[n] = _adamw(w[n], grads[n], m[n], v[n], name="adamw_" + n)
    return loss, grad_x, grads, delta, new_m, new_v


def kernel(x, c, w_ada, b_ada, norm1_w, norm2_w, w_in, q_norm_w, k_norm_w, conv_w, conv_b, A_log, dt_bias, ssd_D, ssd_norm_w, w_attn_out, w_ssd_out, w_o, w_mlp1, w_mlp2, loss_target, m_w_ada, m_b_ada, m_norm1_w, m_norm2_w, m_w_in, m_q_norm_w, m_k_norm_w, m_conv_w, m_conv_b, m_A_log, m_dt_bias, m_ssd_D, m_ssd_norm_w, m_w_attn_out, m_w_ssd_out, m_w_o, m_w_mlp1, m_w_mlp2, v_w_ada, v_b_ada, v_norm1_w, v_norm2_w, v_w_in, v_q_norm_w, v_k_norm_w, v_conv_w, v_conv_b, v_A_log, v_dt_bias, v_ssd_D, v_ssd_norm_w, v_w_attn_out, v_w_ssd_out, v_w_o, v_w_mlp1, v_w_mlp2):
    args = dict(locals())
    strip = lambda a: a[0] if a.ndim == 3 else a
    w = {n: strip(args[n]) for n in NAMES + ("x", "c")}
    m = {n: strip(args["m_" + n]) for n in NAMES}
    v = {n: strip(args["v_" + n]) for n in NAMES}
    loss, grad_x, grads, delta, new_m, new_v = _step(w, m, v, loss_target[0])
    like = lambda t, n: t.reshape(args[n].shape)
    return (loss, grad_x[None], *[like(grads[n], n) for n in NAMES], *[like(delta[n], n) for n in NAMES],
            *[like(new_m[n], n) for n in NAMES], *[like(new_v[n], n) for n in NAMES])
```

```python
import math

import jax
import jax.numpy as jnp
from jax import lax
from jax.experimental import pallas as pl
from jax.experimental.pallas import tpu as pltpu

F32 = jnp.float32
MMD = jnp.bfloat16
EPS = 1e-6
NEG = -1e30
MIB = 1024 * 1024
VMEM_BIG = 56 * MIB
VMEM_MID = 40 * MIB

GRID_W = 64
N_Q_HEADS, N_KV_HEADS, HEAD_DIM = 16, 4, 64
ROPE_THETA = 10000.0
SSD_HEADS, SSD_GROUPS, SSD_P, SSD_N, CHUNK = 32, 4, 64, 128, 128
HPG = SSD_HEADS // SSD_GROUPS
D_CONV = 5
ADAM_LR, ADAM_B1, ADAM_B2, ADAM_EPS, ADAM_WD, ADAM_STEP = 0.001, 0.9, 0.999, 1e-08, 0.01, 10

Z0, GA0, GS0, XS0, B0, C0, Q0, K0, V0, DT0, PW = 0, 2048, 3072, 4096, 6144, 6656, 7168, 8192, 8448, 8704, 8832

MESH = pl.DeviceIdType.MESH
NT = (((1,), (1,)), ((), ()))
TN = (((0,), (0,)), ((), ()))


def _cp(sem=None, vmem=VMEM_MID):
    return pltpu.CompilerParams(dimension_semantics=sem, vmem_limit_bytes=vmem)


def _tile(n, pref):
    t = min(n, pref)
    while n % t:
        t //= 2
    return t


def _dot(a, b, dims=None):
    if dims is None:
        return jnp.dot(a, b, preferred_element_type=F32)
    return lax.dot_general(a, b, dims, preferred_element_type=F32)


def _dot_hi(a01, b):
    a = a01.astype(jnp.bfloat16)
    h1 = b.astype(jnp.bfloat16)
    r1 = b - h1.astype(F32)
    h2 = r1.astype(jnp.bfloat16)
    return _dot(a, h1) + _dot(a, h2) + _dot(a, (r1 - h2.astype(F32)).astype(jnp.bfloat16))


def _sigmoid(x):
    return jax.nn.sigmoid(x)


def _mm(a, b, *, name, outs, nt=False, ta=False, extras=(), epi=None, tm=512, tn=512, n=None, b_outer=False,
        vmem=VMEM_MID):
    assert not (nt and ta)
    k, m = a.shape if ta else a.shape[::-1]
    if n is None:
        n = b.shape[0] if nt else b.shape[1]
    tm, tn = _tile(m, tm), _tile(n, tn)
    gi, gj = m // tm, n // tn
    if b_outer:
        grid = (gj, gi)
        ij = lambda p, q: (q, p)
    else:
        grid = (gi, gj)
        ij = lambda p, q: (p, q)
    if ta:
        a_spec = pl.BlockSpec((k, tm), lambda p, q: (0, ij(p, q)[0]))
    else:
        a_spec = pl.BlockSpec((tm, k), lambda p, q: (ij(p, q)[0], 0))
    if nt:
        b_spec = pl.BlockSpec((tn, k), lambda p, q: (ij(p, q)[1], 0))
    else:
        b_spec = pl.BlockSpec((k, tn), lambda p, q: (0, ij(p, q)[1]))
    e_specs = []
    for arr, kind, off in extras:
        ob = off // tn
        assert off % tn == 0
        if kind == "tile":
            e_specs.append(pl.BlockSpec((tm, tn), lambda p, q, ob=ob: (ij(p, q)[0], ob + ij(p, q)[1])))
        else:
            e_specs.append(pl.BlockSpec((1, tn), lambda p, q, ob=ob: (0, ob + ij(p, q)[1])))
    ne = len(extras)

    def body(a_ref, b_ref, *rest):
        acc = _dot(a_ref[...], b_ref[...], NT if nt else (TN if ta else None))
        res = epi(acc, *[e[...] for e in rest[:ne]]) if epi is not None else (acc,)
        for o_ref, r in zip(rest[ne:], res):
            o_ref[...] = r.astype(o_ref.dtype)

    out = pl.pallas_call(
        body, name=name, grid=grid,
        in_specs=[a_spec, b_spec] + e_specs,
        out_specs=[pl.BlockSpec((tm, tn), lambda p, q: ij(p, q)) for _ in outs],
        out_shape=[jax.ShapeDtypeStruct((m, n), dt) for dt in outs],
        compiler_params=_cp(("arbitrary", "arbitrary"), vmem),
    )(a, b, *[e[0] for e in extras])
    return out if len(outs) > 1 else out[0]


def _mm_tn(a, g, *, name, tk=512, tn=1024, tmm=4096, vmem=VMEM_MID):
    m, k = a.shape
    n = g.shape[1]
    tk, tn, tmm = _tile(k, tk), _tile(n, tn), _tile(m, tmm)

    def body(a_ref, g_ref, o_ref):
        p = _dot(a_ref[...], g_ref[...], TN)

        @pl.when(pl.program_id(2) == 0)
        def _():
            o_ref[...] = p

        @pl.when(pl.program_id(2) > 0)
        def _():
            o_ref[...] += p

    return pl.pallas_call(
        body, name=name, grid=(k // tk, n // tn, m // tmm),
        in_specs=[pl.BlockSpec((tmm, tk), lambda i, j, r: (r, i)), pl.BlockSpec((tmm, tn), lambda i, j, r: (r, j))],
        out_specs=pl.BlockSpec((tk, tn), lambda i, j, r: (i, j)),
        out_shape=jax.ShapeDtypeStruct((k, n), F32),
        compiler_params=_cp(("arbitrary", "arbitrary", "arbitrary"), vmem),
    )(a, g)


def _adamw(w, g, m, v, *, name):
    r, c = w.shape
    tr = _tile(r, 256) if r % 8 == 0 else r

    def body(w_ref, g_ref, m_ref, v_ref, d_ref, nm_ref, nv_ref):
        gg = g_ref[...]
        nm = ADAM_B1 * m_ref[...] + (1.0 - ADAM_B1) * gg
        nv = ADAM_B2 * v_ref[...] + (1.0 - ADAM_B2) * jnp.square(gg)
        m_hat = nm / (1.0 - ADAM_B1 ** ADAM_STEP)
        v_hat = nv / (1.0 - ADAM_B2 ** ADAM_STEP)
        d_ref[...] = -ADAM_LR * (m_hat / (jnp.sqrt(v_hat) + ADAM_EPS) + ADAM_WD * w_ref[...])
        nm_ref[...] = nm
        nv_ref[...] = nv

    spec = pl.BlockSpec((tr, c), lambda i: (i, 0))
    return pl.pallas_call(
        body, name=name, grid=(r // tr,), in_specs=[spec] * 4, out_specs=[spec] * 3,
        out_shape=[jax.ShapeDtypeStruct((r, c), F32)] * 3, compiler_params=_cp(("arbitrary",)),
    )(w, g, m, v)


def _rows_sum(a, groups, *, name):
    r = a.shape[0] // groups

    def body(a_ref, o_ref):
        acc = a_ref[0:r, :]
        for d in range(1, groups):
            acc = acc + a_ref[d * r:(d + 1) * r, :]
        o_ref[...] = acc

    return pl.pallas_call(body, name=name, out_shape=jax.ShapeDtypeStruct((r, a.shape[1]), F32))(a)


def _silu_cast(a, *, name):
    def body(a_ref, o_ref):
        x = a_ref[...]
        o_ref[...] = (x * _sigmoid(x)).astype(o_ref.dtype)

    return pl.pallas_call(body, name=name, out_shape=jax.ShapeDtypeStruct(a.shape, MMD))(a)


def _sumsq(a, *, name):
    m, n = a.shape
    tm = _tile(m, 512)

    def body(a_ref, o_ref):
        x = a_ref[...]
        p = jnp.sum(jnp.sum(x * x, axis=1, keepdims=True), axis=0, keepdims=True)

        @pl.when(pl.program_id(0) == 0)
        def _():
            o_ref[...] = p

        @pl.when(pl.program_id(0) > 0)
        def _():
            o_ref[...] += p

    return pl.pallas_call(
        body, name=name, grid=(m // tm,), in_specs=[pl.BlockSpec((tm, n), lambda i: (i, 0))],
        out_specs=pl.BlockSpec((1, 1), lambda i: (0, 0)), out_shape=jax.ShapeDtypeStruct((1, 1), F32),
        compiler_params=_cp(("arbitrary",)),
    )(a)


def _acc_rows(o_ref, p, first):
    @pl.when(first)
    def _():
        o_ref[...] = p

    @pl.when(jnp.logical_not(first))
    def _():
        o_ref[...] += p


def _ln_mod(x, w, scale, shift, *, name):
    s, d = x.shape
    tm = _tile(s, 512)

    def body(x_ref, w_ref, sc_ref, sh_ref, o_ref):
        xv = x_ref[...]
        r = lax.rsqrt(jnp.mean(xv * xv, axis=-1, keepdims=True) + EPS)
        o_ref[...] = ((xv * r) * w_ref[...] * (1.0 + sc_ref[...]) + sh_ref[...]).astype(o_ref.dtype)

    row = pl.BlockSpec((1, d), lambda i: (0, 0))
    big = pl.BlockSpec((tm, d), lambda i: (i, 0))
    return pl.pallas_call(
        body, name=name, grid=(s // tm,), in_specs=[big, row, row, row], out_specs=big,
        out_shape=jax.ShapeDtypeStruct((s, d), MMD), compiler_params=_cp(("arbitrary",)),
    )(x, w, scale, shift)


def _ln_mod_bwd(dh, x, w, scale, dres, *, name):
    s, d = x.shape
    tm = _tile(s, 512)

    def body(dh_ref, x_ref, w_ref, sc_ref, dres_ref, dx_ref, dsh_ref, dsc_ref, dw_ref):
        xv = x_ref[...]
        dhv = dh_ref[...].astype(F32)
        r = lax.rsqrt(jnp.mean(xv * xv, axis=-1, keepdims=True) + EPS)
        nv = xv * r
        wv = w_ref[...]
        g1 = 1.0 + sc_ref[...]
        dn = dhv * (wv * g1)
        dx_ref[...] = dres_ref[...] + r * (dn - nv * jnp.mean(dn * nv, axis=-1, keepdims=True))
        first = pl.program_id(0) == 0
        _acc_rows(dsh_ref, jnp.sum(dhv, axis=0, keepdims=True), first)
        _acc_rows(dsc_ref, jnp.sum(dhv * nv * wv, axis=0, keepdims=True), first)
        _acc_rows(dw_ref, jnp.sum(dhv * nv * g1, axis=0, keepdims=True), first)

    row = pl.BlockSpec((1, d), lambda i: (0, 0))
    big = pl.BlockSpec((tm, d), lambda i: (i, 0))
    return pl.pallas_call(
        body, name=name, grid=(s // tm,), in_specs=[big, big, row, row, big], out_specs=[big, row, row, row],
        out_shape=[jax.ShapeDtypeStruct((s, d), F32)] + [jax.ShapeDtypeStruct((1, d), F32)] * 3,
        compiler_params=_cp(("arbitrary",)),
    )(dh, x, w, scale, dres)


def _gate_bwd(dy, u, gate, *, name):
    s, d = dy.shape
    tm = _tile(s, 512)

    def body(dy_ref, u_ref, g_ref, du_ref, dg_ref):
        dyv = dy_ref[...]
        du_ref[...] = (dyv * g_ref[...]).astype(du_ref.dtype)
        _acc_rows(dg_ref, jnp.sum(dyv * u_ref[...].astype(F32), axis=0, keepdims=True), pl.program_id(0) == 0)

    row = pl.BlockSpec((1, d), lambda i: (0, 0))
    big = pl.BlockSpec((tm, d), lambda i: (i, 0))
    return pl.pallas_call(
        body, name=name, grid=(s // tm,), in_specs=[big, big, row], out_specs=[big, row],
        out_shape=[jax.ShapeDtypeStruct((s, d), MMD), jax.ShapeDtypeStruct((1, d), F32)],
        compiler_params=_cp(("arbitrary",)),
    )(dy, u, gate)


def _seg64(v, e):
    hi = v.astype(jnp.bfloat16)
    lo = (v - hi.astype(F32)).astype(jnp.bfloat16)
    return _dot(hi, e) + _dot(lo, e)


def _rope_tables(s, zero=0.0):
    rows = s // GRID_W
    pos_row = jnp.repeat(jnp.arange(rows, dtype=jnp.int32), GRID_W).astype(F32) + zero
    pos_col = jnp.tile(jnp.arange(GRID_W, dtype=jnp.int32), rows).astype(F32) + zero
    axis_dim = HEAD_DIM // 2
    inv_freq = ROPE_THETA ** (-jnp.arange(0, axis_dim, 2, dtype=F32) / axis_dim)
    ang_r = pos_row[:, None] * inv_freq[None, :]
    ang_c = pos_col[:, None] * inv_freq[None, :]
    zero = jnp.zeros_like(ang_r)
    cos = jnp.concatenate([jnp.cos(ang_r), jnp.cos(ang_r), jnp.cos(ang_c), jnp.cos(ang_c)], axis=1)
    s_a = jnp.concatenate([-jnp.sin(ang_r), zero, -jnp.sin(ang_c), zero], axis=1)
    s_b = jnp.concatenate([zero, jnp.sin(ang_r), zero, jnp.sin(ang_c)], axis=1)
    return [jnp.tile(t, (1, 2)) for t in (cos, s_a, s_b)]


def _e128():
    i = jnp.arange(128)
    return (i[:, None] // 64 == i[None, :] // 64).astype(jnp.bfloat16)


QKW = N_Q_HEADS * HEAD_DIM + N_KV_HEADS * HEAD_DIM


def _qk_fwd(proj, wrow, scrow, tabs, *, name):
    s = proj.shape[0]
    tm = _tile(s, 1024)

    def body(x_ref, w_ref, sc_ref, cos_ref, sa_ref, sb_ref, e_ref, ot_ref):
        u = x_ref[...].astype(F32)
        r = lax.rsqrt(_seg64(u * u, e_ref[...]) * (1.0 / HEAD_DIM) + EPS)
        nv = (u * r) * w_ref[...]
        ro = nv * cos_ref[...] + pltpu.roll(nv, 112, 1) * sa_ref[...] + pltpu.roll(nv, 16, 1) * sb_ref[...]
        ot_ref[...] = (ro * sc_ref[...]).T.astype(ot_ref.dtype)

    tab = pl.BlockSpec((tm, 128), lambda i, j: (i, 0))
    row = pl.BlockSpec((1, 128), lambda i, j: (0, j))
    return pl.pallas_call(
        body, name=name, grid=(s // tm, QKW // 128),
        in_specs=[pl.BlockSpec((tm, 128), lambda i, j: (i, Q0 // 128 + j)), row, row, tab, tab, tab,
                  pl.BlockSpec((128, 128), lambda i, j: (0, 0))],
        out_specs=pl.BlockSpec((128, tm), lambda i, j: (j, i)),
        out_shape=jax.ShapeDtypeStruct((QKW, s), MMD), compiler_params=_cp(("arbitrary", "arbitrary")),
    )(proj, wrow, scrow, *tabs, _e128())


def _qk_bwd(dqt, dkt, proj, wrow, scrow, tabs, dproj, *, name):
    s = proj.shape[0]
    tm = _tile(s, 1024)
    nq = dqt.shape[0] // 128

    def body(dq_ref, dk_ref, x_ref, w_ref, sc_ref, cos_ref, sa_ref, sb_ref, e_ref, _, du_ref, dw_ref):
        e = e_ref[...]
        d = jnp.where(pl.program_id(0) < nq, dq_ref[...], dk_ref[...]).T * sc_ref[...]
        dn = d * cos_ref[...] + pltpu.roll(d * sa_ref[...], 16, 1) + pltpu.roll(d * sb_ref[...], 112, 1)
        u = x_ref[...].astype(F32)
        r = lax.rsqrt(_seg64(u * u, e) * (1.0 / HEAD_DIM) + EPS)
        uh = u * r
        _acc_rows(dw_ref, jnp.sum(dn * uh, axis=0, keepdims=True), pl.program_id(1) == 0)
        dnw = dn * w_ref[...]
        du_ref[...] = (r * (dnw - uh * (_seg64(dnw * uh, e) * (1.0 / HEAD_DIM)))).astype(du_ref.dtype)

    tab = pl.BlockSpec((tm, 128), lambda j, i: (i, 0))
    row = pl.BlockSpec((1, 128), lambda j, i: (0, j))
    qcol = pl.BlockSpec((tm, 128), lambda j, i: (i, Q0 // 128 + j))
    return pl.pallas_call(
        body, name=name, grid=(QKW // 128, s // tm),
        in_specs=[pl.BlockSpec((128, tm), lambda j, i: (jnp.minimum(j, nq - 1), i)),
                  pl.BlockSpec((128, tm), lambda j, i: (jnp.maximum(j - nq, 0), i)),
                  qcol, row, row, tab, tab, tab, pl.BlockSpec((128, 128), lambda j, i: (0, 0)), ANY],
        out_specs=[qcol, row],
        out_shape=[jax.ShapeDtypeStruct(dproj.shape, dproj.dtype), jax.ShapeDtypeStruct((1, QKW), F32)],
        input_output_aliases={9: 0}, compiler_params=_cp(("arbitrary", "arbitrary")),
    )(dqt, dkt, proj, wrow, scrow, *tabs, _e128(), dproj)


REP = N_Q_HEADS // N_KV_HEADS


def _lanes(ref):
    return jnp.concatenate([ref[r] for r in range(REP)], axis=1)


V_AUG = HEAD_DIM + 8
LOG2E = math.log2(math.e)


def _flash_fwd(qkt, vta, *, name):
    s = qkt.shape[2]
    tq, tk = _tile(s, 1024), _tile(s, 512)
    nk = s // tk
    lanes = REP * tq

    def body(q_ref, k_ref, v_ref, o_ref, lse_ref, m_ref, acc_ref):
        j = pl.program_id(2)

        @pl.when(j == 0)
        def _():
            m_ref[...] = jnp.full_like(m_ref, NEG)
            acc_ref[...] = jnp.zeros_like(acc_ref)

        st = _dot(k_ref[0], _lanes(q_ref), TN)
        m_prev = m_ref[...]
        m_new = jnp.maximum(m_prev, jnp.max(st, axis=0, keepdims=True))
        p = jnp.exp2(st - m_new).astype(MMD)
        acc_ref[...] = jnp.exp2(m_prev - m_new) * acc_ref[...] + _dot(v_ref[0], p)
        m_ref[...] = m_new

        @pl.when(j == nk - 1)
        def _():
            acc = acc_ref[...]
            l = acc[HEAD_DIM:HEAD_DIM + 1]
            o = acc[0:HEAD_DIM] / l
            ls = m_ref[...] + jnp.log(l) * LOG2E
            for r in range(REP):
                o_ref[r] = o[:, r * tq:(r + 1) * tq].astype(o_ref.dtype)
                lse_ref[r] = ls[:, r * tq:(r + 1) * tq]

    qspec = pl.BlockSpec((REP, HEAD_DIM, tq), lambda g, i, j: (g, 0, i))
    return pl.pallas_call(
        body, name=name, grid=(N_KV_HEADS, s // tq, nk),
        in_specs=[qspec, pl.BlockSpec((1, HEAD_DIM, tk), lambda g, i, j: (N_Q_HEADS + g, 0, j)),
                  pl.BlockSpec((1, V_AUG, tk), lambda g, i, j: (g, 0, j))],
        out_specs=[qspec, pl.BlockSpec((REP, 1, tq), lambda g, i, j: (g, 0, i))],
        out_shape=[jax.ShapeDtypeStruct((N_Q_HEADS, HEAD_DIM, s), MMD), jax.ShapeDtypeStruct((N_Q_HEADS, 1, s), F32)],
        scratch_shapes=[pltpu.VMEM((1, lanes), F32), pltpu.VMEM((V_AUG, lanes), F32)],
        compiler_params=_cp(("arbitrary", "arbitrary", "arbitrary"), VMEM_BIG),
    )(qkt, qkt, vta)


def _flash_bwd(qkt, vta, dot, ot, lse, *, name):
    s = qkt.shape[2]
    tq, tk = _tile(s, 512), _tile(s, 1024)
    nk = s // tk

    def body(q_ref, kt_ref, vt_ref, do_ref, o_ref, lse_ref, dq_ref, dk_ref, dv_ref, dq_acc):
        i, j = pl.program_id(1), pl.program_id(2)
        q, do = _lanes(q_ref), _lanes(do_ref)
        delta = jnp.sum(do.astype(F32) * _lanes(o_ref).astype(F32), axis=0, keepdims=True)
        kt, vt = kt_ref[0], vt_ref[0, 0:HEAD_DIM, :]
        p = jnp.exp2(_dot(kt, q, TN) - _lanes(lse_ref))
        dvc = _dot(p.astype(MMD), do, NT)
        ds = (p * (_dot(vt, do, TN) - delta)).astype(MMD)
        dkc = _dot(ds, q, NT) * (1.0 / LOG2E)
        dqc = _dot(kt, ds)
        rows = pl.ds(pl.multiple_of(j * tk, tk), tk)

        @pl.when(i == 0)
        def _():
            dk_ref[0, rows, :] = dkc
            dv_ref[0, rows, :] = dvc

        @pl.when(i > 0)
        def _():
            dk_ref[0, rows, :] += dkc
            dv_ref[0, rows, :] += dvc

        @pl.when(j == 0)
        def _():
            dq_acc[...] = dqc

        @pl.when(j > 0)
        def _():
            dq_acc[...] += dqc

        @pl.when(j == nk - 1)
        def _():
            acc = dq_acc[...]
            for r in range(REP):
                dq_ref[r] = acc[:, r * tq:(r + 1) * tq]

    qspec = pl.BlockSpec((REP, HEAD_DIM, tq), lambda g, i, j: (g, 0, i))
    kvres = pl.BlockSpec((1, s, HEAD_DIM), lambda g, i, j: (g, 0, 0))
    return pl.pallas_call(
        body, name=name, grid=(N_KV_HEADS, s // tq, nk),
        in_specs=[qspec, pl.BlockSpec((1, HEAD_DIM, tk), lambda g, i, j: (N_Q_HEADS + g, 0, j)),
                  pl.BlockSpec((1, V_AUG, tk), lambda g, i, j: (g, 0, j)),
                  qspec, qspec, pl.BlockSpec((REP, 1, tq), lambda g, i, j: (g, 0, i))],
        out_specs=[qspec, kvres, kvres],
        out_shape=[jax.ShapeDtypeStruct((N_Q_HEADS, HEAD_DIM, s), F32), jax.ShapeDtypeStruct((N_KV_HEADS, s, HEAD_DIM), F32),
                   jax.ShapeDtypeStruct((N_KV_HEADS, s, HEAD_DIM), F32)],
        scratch_shapes=[pltpu.VMEM((HEAD_DIM, REP * tq), F32)],
        compiler_params=_cp(("arbitrary", "arbitrary", "arbitrary"), VMEM_BIG),
    )(qkt, qkt, vta, dot, ot, lse)


HALO = 8
CONV_W = 2048 + 2 * SSD_GROUPS * SSD_N


def _shifted(win, off, r):
    return pltpu.roll(win, (r + 2 * HALO - off) % (r + 2 * HALO), 0)[0:r]


def _conv_fwd(proj, w8, brow, *, name):
    s = proj.shape[0]
    cb = 256
    r = _tile(s, 512)

    def body(x_ref, w_ref, b_ref, o_ref, pad_ref):
        zeros = jnp.zeros((HALO, cb), F32)
        pad_ref[0:HALO, :] = zeros
        pad_ref[s + HALO:s + 2 * HALO, :] = zeros

        def fill(i, carry):
            st = pl.multiple_of(i * r, r)
            pad_ref[pl.ds(st + HALO, r), :] = x_ref[pl.ds(st, r), :].astype(F32)
            return carry

        lax.fori_loop(0, s // r, fill, 0)
        wv = w_ref[...]
        bv = b_ref[...]

        def step(i, carry):
            st = pl.multiple_of(i * r, r)
            win = pad_ref[pl.ds(st, r + 2 * HALO), :]
            acc = bv + wv[0:1, :] * _shifted(win, HALO - 2, r)
            for t in range(1, D_CONV):
                acc = acc + wv[t:t + 1, :] * _shifted(win, HALO - 2 + t, r)
            o_ref[pl.ds(st, r), :] = (acc * _sigmoid(acc)).astype(o_ref.dtype)
            return carry

        lax.fori_loop(0, s // r, step, 0)

    return pl.pallas_call(
        body, name=name, grid=(CONV_W // cb,),
        in_specs=[pl.BlockSpec((s, cb), lambda j: (0, XS0 // cb + j)), pl.BlockSpec((8, cb), lambda j: (0, j)),
                  pl.BlockSpec((1, cb), lambda j: (0, j))],
        out_specs=pl.BlockSpec((s, cb), lambda j: (0, j)),
        out_shape=jax.ShapeDtypeStruct((s, CONV_W), MMD),
        scratch_shapes=[pltpu.VMEM((s + 2 * HALO, cb), F32)],
        compiler_params=_cp(("arbitrary",), VMEM_MID),
    )(proj, w8, brow)


def _conv_bwd(proj, col0, ga, gb, w8, brow, dproj, *, name):
    s = proj.shape[0]
    width = ga.shape[1]
    cb = 128
    c0 = col0 // cb
    r = _tile(s, 512)

    def body(x_ref, ga_ref, gb_ref, w_ref, b_ref, _, dx_ref, dw_ref, db_ref, xpad, dpad):
        zeros = jnp.zeros((HALO, cb), F32)
        for ref in (xpad, dpad):
            ref[0:HALO, :] = zeros
            ref[s + HALO:s + 2 * HALO, :] = zeros

        def fill(i, carry):
            st = pl.multiple_of(i * r, r)
            xpad[pl.ds(st + HALO, r), :] = x_ref[pl.ds(st, r), :].astype(F32)
            return carry

        lax.fori_loop(0, s // r, fill, 0)
        wv = w_ref[...]
        bv = b_ref[...]

        def first(i, carry):
            st = pl.multiple_of(i * r, r)
            win = xpad[pl.ds(st, r + 2 * HALO), :]
            taps = [_shifted(win, HALO - 2 + t, r) for t in range(D_CONV)]
            u = bv
            for t in range(D_CONV):
                u = u + wv[t:t + 1, :] * taps[t]
            sg = _sigmoid(u)
            du = ((ga_ref[pl.ds(st, r), :].astype(F32) + gb_ref[pl.ds(st, r), :].astype(F32))
                  * (sg * (1.0 + u * (1.0 - sg))))
            dpad[pl.ds(st + HALO, r), :] = du
            out = [carry[0] + jnp.sum(du, axis=0, keepdims=True)]
            for t in range(D_CONV):
                out.append(carry[1 + t] + jnp.sum(du * taps[t], axis=0, keepdims=True))
            return tuple(out)

        sums = lax.fori_loop(0, s // r, first, tuple(jnp.zeros((1, cb), F32) for _ in range(1 + D_CONV)))
        db_ref[...] = sums[0]
        for t in range(D_CONV):
            dw_ref[t:t + 1, :] = sums[1 + t]
        dw_ref[D_CONV:8, :] = jnp.zeros((8 - D_CONV, cb), F32)

        def second(i, carry):
            st = pl.multiple_of(i * r, r)
            win = dpad[pl.ds(st, r + 2 * HALO), :]
            acc = wv[0:1, :] * _shifted(win, HALO + 2, r)
            for t in range(1, D_CONV):
                acc = acc + wv[t:t + 1, :] * _shifted(win, HALO + 2 - t, r)
            dx_ref[pl.ds(st, r), :] = acc.astype(dx_ref.dtype)
            return carry

        lax.fori_loop(0, s // r, second, 0)

    col = pl.BlockSpec((s, cb), lambda j: (0, j))
    xcol = pl.BlockSpec((s, cb), lambda j: (0, XS0 // cb + c0 + j))
    return pl.pallas_call(
        body, name=name, grid=(width // cb,),
        in_specs=[xcol, col, col, pl.BlockSpec((8, cb), lambda j: (0, c0 + j)),
                  pl.BlockSpec((1, cb), lambda j: (0, c0 + j)), ANY],
        out_specs=[xcol, pl.BlockSpec((8, cb), lambda j: (0, j)), pl.BlockSpec((1, cb), lambda j: (0, j))],
        out_shape=[jax.ShapeDtypeStruct(dproj.shape, dproj.dtype), jax.ShapeDtypeStruct((8, width), F32),
                   jax.ShapeDtypeStruct((1, width), F32)],
        scratch_shapes=[pltpu.VMEM((s + 2 * HALO, cb), F32), pltpu.VMEM((s + 2 * HALO, cb), F32)],
        input_output_aliases={5: 0}, compiler_params=_cp(("arbitrary",), VMEM_BIG),
    )(proj, ga, gb, w8, brow, dproj)


def _tri(lower):
    i = jnp.arange(CHUNK)
    return ((i[:, None] >= i[None, :]) if lower else (i[:, None] <= i[None, :])).astype(F32)


def _dt_fwd(raw, bias, arow, *, name):
    s = raw.shape[0]

    def body(r_ref, b_ref, a_ref, lo_ref, up_ref, dt_ref, cs_ref):
        u = r_ref[...] + b_ref[...]
        dt = jnp.maximum(u, 0.0) + jnp.log1p(jnp.exp(-jnp.abs(u)))
        dt_ref[...] = dt
        a = dt * a_ref[...]
        lane = lax.broadcasted_iota(jnp.int32, (CHUNK, 128), 1)
        cs_ref[...] = jnp.where(lane < SSD_HEADS, _dot_hi(lo_ref[...], a), _dot_hi(up_ref[...], a))

    blk = pl.BlockSpec((CHUNK, 128), lambda i: (i, 0))
    row = pl.BlockSpec((1, 128), lambda i: (0, 0))
    tri = pl.BlockSpec((CHUNK, CHUNK), lambda i: (0, 0))
    return pl.pallas_call(
        body, name=name, grid=(s // CHUNK,), in_specs=[blk, row, row, tri, tri], out_specs=[blk, blk],
        out_shape=[jax.ShapeDtypeStruct((s, 128), F32)] * 2, compiler_params=_cp(("arbitrary",)),
    )(raw, bias, arow, _tri(True), _tri(False))


def _dt_bwd(ddt0, ddt1, raw, bias, dproj, *, name):
    s = raw.shape[0]
    tm = _tile(s, 1024)

    def body(d0_ref, d1_ref, r_ref, b_ref, _, o_ref, db_ref):
        g = (d0_ref[...] + d1_ref[...]) * _sigmoid(r_ref[...] + b_ref[...])
        o_ref[...] = g.astype(o_ref.dtype)
        _acc_rows(db_ref, jnp.sum(g, axis=0, keepdims=True), pl.program_id(0) == 0)

    blk = pl.BlockSpec((tm, 128), lambda i: (i, 0))
    row = pl.BlockSpec((1, 128), lambda i: (0, 0))
    return pl.pallas_call(
        body, name=name, grid=(s // tm,), in_specs=[blk, blk, blk, row, ANY],
        out_specs=[pl.BlockSpec((tm, 128), lambda i: (i, DT0 // 128)), row],
        out_shape=[jax.ShapeDtypeStruct(dproj.shape, dproj.dtype), jax.ShapeDtypeStruct((1, 128), F32)],
        input_output_aliases={4: 0}, compiler_params=_cp(("arbitrary",)),
    )(ddt0, ddt1, raw, bias, dproj)


GW = HPG * SSD_P


GPS = SSD_GROUPS


def _ssd_specs(nc, rev):
    cc = (lambda c: nc - 1 - c) if rev else (lambda c: c)
    return dict(
        x=pl.BlockSpec((CHUNK, GPS * GW), lambda g, c: (cc(c), g)),
        b=pl.BlockSpec((CHUNK, GPS * SSD_N), lambda g, c: (cc(c), 2048 // (GPS * SSD_N) + g)),
        c=pl.BlockSpec((CHUNK, GPS * SSD_N), lambda g, c: (cc(c), 2048 // (GPS * SSD_N) + 1 + g)),
        lanes=pl.BlockSpec((CHUNK, 128), lambda g, c: (cc(c), 0)),
        drow=pl.BlockSpec((1, GPS * GW), lambda g, c: (0, g)),
        y=pl.BlockSpec((CHUNK, GPS * GW), lambda g, c: (cc(c), g)),
        h=pl.BlockSpec((GPS, 1, SSD_N, GW), lambda g, c: (g, cc(c), 0, 0)),
        n=pl.BlockSpec((CHUNK, GPS * SSD_N), lambda g, c: (cc(c), g)),
    )


def _ssd_mask(anti):
    ii = lax.broadcasted_iota(jnp.int32, (CHUNK, CHUNK), 0)
    jj = lax.broadcasted_iota(jnp.int32, (CHUNK, CHUNK), 1)
    return ii, jj, (ii <= jj) if anti else (ii >= jj)


def _expand(x, ex, terms=3):
    h1 = x.astype(jnp.bfloat16)
    r1 = x - h1.astype(F32)
    h2 = r1.astype(jnp.bfloat16)
    out = _dot(h1, ex) + _dot(h2, ex)
    if terms == 3:
        out = out + _dot((r1 - h2.astype(F32)).astype(jnp.bfloat16), ex)
    return out


def _headsum(a, e):
    hi = a.astype(jnp.bfloat16)
    return _dot(hi, e) + _dot((a - hi.astype(F32)).astype(jnp.bfloat16), e)


def _expand_mats():
    lane = jnp.arange(128)[None, :, None]
    col = jnp.arange(GW)[None, None, :]
    base = (jnp.arange(2)[:, None] * SSD_HEADS + jnp.arange(SSD_GROUPS)[None, :] * HPG).reshape(2 * SSD_GROUPS, 1, 1)
    return (lane == base + col // SSD_P).astype(jnp.bfloat16)


def _headsum_mats():
    e1 = (jnp.arange(GW)[:, None] // SSD_P == jnp.arange(128)[None, :]).astype(jnp.bfloat16)
    e2 = (jnp.arange(HPG * CHUNK)[:, None] // CHUNK == jnp.arange(128)[None, :]).astype(jnp.bfloat16)
    return e1, e2


def _ssd_fwd(xc, dt, cs, ex, drow, di, *, name):
    s = xc.shape[0]
    nc = s // CHUNK
    anti = di == 1
    sp = _ssd_specs(nc, anti)
    trow = 0 if anti else CHUNK - 1

    def body(x_ref, b_ref, c_ref, dt_ref, cs_ref, ex_ref, d_ref, y_ref, hp_ref, h_ref):
        @pl.when(pl.program_id(1) == 0)
        def _():
            h_ref[...] = jnp.zeros_like(h_ref)

        mask = _ssd_mask(anti)[2]
        dtv, csv = dt_ref[...], cs_ref[...]
        cst = csv.T
        for gi in range(GPS):
            cols = slice(gi * GW, (gi + 1) * GW)
            ncols = slice(gi * SSD_N, (gi + 1) * SSD_N)
            ex = ex_ref[gi]
            xb = x_ref[:, cols].astype(F32)
            bm, cm = b_ref[:, ncols], c_ref[:, ncols]
            csr = cst[SSD_HEADS * di + HPG * gi:SSD_HEADS * di + HPG * (gi + 1)]
            dtf = _expand(dtv, ex, 2)
            csf = _expand(csv, ex, 2)
            tl = csf[trow:trow + 1, :]
            h = h_ref[gi]
            hp_ref[gi, 0] = h.astype(hp_ref.dtype)
            g = _dot(cm, bm, NT)
            xs = xb * dtf
            xsm = xs.astype(MMD)
            base = jnp.exp(csf) * _dot(cm, h.astype(MMD)) + d_ref[:, cols] * xb
            for r in range(HPG):
                sl = slice(r * SSD_P, (r + 1) * SSD_P)
                lm = jnp.exp(jnp.where(mask, csf[:, r * SSD_P:r * SSD_P + 1] - csr[r:r + 1, :], NEG))
                y_ref[:, gi * GW + r * SSD_P:gi * GW + (r + 1) * SSD_P] = (
                    _dot((g * lm).astype(MMD), xsm[:, sl]) + base[:, sl]).astype(y_ref.dtype)
            xd = (xs * jnp.exp(tl - csf)).astype(MMD)
            h_ref[gi] = h * jnp.exp(tl) + _dot(bm, xd, TN)

    return pl.pallas_call(
        body, name=name, grid=(1, nc),
        in_specs=[sp["x"], sp["b"], sp["c"], sp["lanes"], sp["lanes"],
                  pl.BlockSpec((GPS, 128, GW), lambda g, c: (di, 0, 0)), sp["drow"]],
        out_specs=[sp["y"], sp["h"]],
        out_shape=[jax.ShapeDtypeStruct((s, 2048), MMD), jax.ShapeDtypeStruct((SSD_GROUPS, nc, SSD_N, GW), MMD)],
        scratch_shapes=[pltpu.VMEM((GPS, SSD_N, GW), F32)],
        compiler_params=_cp(("arbitrary", "arbitrary")),
    )(xc, xc, xc, dt, cs, ex, drow)


def _ssd_bwd(xc, dt, cs, ex, drow, arow, dy, hprev, di, *, name):
    s = xc.shape[0]
    nc = s // CHUNK
    anti = di == 1
    sp = _ssd_specs(nc, not anti)
    trow = 0 if anti else CHUNK - 1
    e1, e2 = _headsum_mats()

    def body(x_ref, b_ref, c_ref, dt_ref, cs_ref, ex_ref, d_ref, a_ref, dy_ref, hp_ref, tri_ref,
             e1_ref, e2_ref, dx_ref, db_ref, dc_ref, ddt_ref, da_ref, dh_ref, w_ref, dxs_ref):
        @pl.when(pl.program_id(1) == 0)
        def _():
            dh_ref[...] = jnp.zeros_like(dh_ref)
            da_ref[...] = jnp.zeros_like(da_ref)

        e1v = e1_ref[...]
        ii, _, mask = _ssd_mask(anti)
        dtv, csv = dt_ref[...], cs_ref[...]
        cst = csv.T
        ddt_acc = jnp.zeros((CHUNK, 128), F32)
        da_acc = jnp.zeros((1, 128), F32)
        for gi in range(GPS):
            lane0 = SSD_HEADS * di + HPG * gi
            cols = slice(gi * GW, (gi + 1) * GW)
            ncols = slice(gi * SSD_N, (gi + 1) * SSD_N)
            ex = ex_ref[gi]
            xb = x_ref[:, cols].astype(F32)
            bm, cm = b_ref[:, ncols], c_ref[:, ncols]
            csr = cst[lane0:lane0 + HPG]
            dym = dy_ref[:, cols]
            dyb = dym.astype(F32)
            hpm = hp_ref[gi, 0]
            hp = hpm.astype(F32)
            dh = dh_ref[gi]
            dhm = dh.astype(MMD)
            dtf = _expand(dtv, ex, 2)
            csf = _expand(csv, ex, 2)
            tl = csf[trow:trow + 1, :]
            e = jnp.exp(csf)
            dec = jnp.exp(tl - csf)
            et = jnp.exp(tl)
            xs = xb * dtf
            xsm = xs.astype(MMD)
            g = _dot(cm, bm, NT)
            z = _dot(cm, hpm)
            bdh = _dot(bm, dhm)
            dg = jnp.zeros((CHUNK, CHUNK), F32)
            wcols = jnp.zeros((CHUNK, CHUNK), F32)
            for r in range(HPG):
                sl = slice(r * SSD_P, (r + 1) * SSD_P)
                lm = jnp.exp(jnp.where(mask, csf[:, r * SSD_P:r * SSD_P + 1] - csr[r:r + 1, :], NEG))
                mm = g * lm
                dm = _dot(dym[:, sl], xsm[:, sl], NT)
                w = dm * mm
                w_ref[gi, :, r * CHUNK:(r + 1) * CHUNK] = w
                wcols = jnp.where(ii == r, jnp.sum(w, axis=0, keepdims=True), wcols)
                dg = dg + dm * lm
                dxs_ref[gi, :, sl] = _dot(mm.astype(MMD), dym[:, sl], TN)
            dxs = dxs_ref[gi] + dec * bdh
            dx_ref[:, cols] = (dxs * dtf + d_ref[:, cols] * dyb).astype(dx_ref.dtype)
            tb = xs * bdh * dec
            d_tot = jnp.sum(tb, axis=0, keepdims=True) + et * jnp.sum(dh * hp, axis=0, keepdims=True)
            d_tot = _headsum(jnp.broadcast_to(d_tot, (8, GW)), e1v)[0:1]
            dcs = (_headsum(dyb * (e * z) - tb, e1v) + _headsum(w_ref[gi], e2_ref[...]) - wcols.T
                   + jnp.where(ii == trow, d_tot, 0.0))
            da = pltpu.roll(_dot_hi(tri_ref[...], dcs), lane0, 1)
            ddt_acc = ddt_acc + da * a_ref[...] + pltpu.roll(_headsum(dxs * xb, e1v), lane0, 1)
            da_acc = da_acc + jnp.sum(da * dtv, axis=0, keepdims=True)
            dgm = dg.astype(MMD)
            dz = (e * dyb).astype(MMD)
            dc_ref[:, ncols] = (_dot(dgm, bm) + _dot(dz, hpm, NT)).astype(dc_ref.dtype)
            db_ref[:, ncols] = (_dot(dgm, cm, TN) + _dot((xs * dec).astype(MMD), dhm, NT)).astype(db_ref.dtype)
            dh_ref[gi] = dh * et + _dot(cm, dz, TN)
        ddt_ref[...] = ddt_acc
        da_ref[...] += da_acc

    const = lambda shape: pl.BlockSpec(shape, lambda g, c: (0,) * len(shape))
    return pl.pallas_call(
        body, name=name, grid=(1, nc),
        in_specs=[sp["x"], sp["b"], sp["c"], sp["lanes"], sp["lanes"],
                  pl.BlockSpec((GPS, 128, GW), lambda g, c: (di, 0, 0)), sp["drow"],
                  const((1, 128)), sp["y"], sp["h"],
                  const((CHUNK, CHUNK)), const((GW, 128)), const((HPG * CHUNK, 128))],
        out_specs=[sp["y"], sp["n"], sp["n"], sp["lanes"], const((1, 128))],
        out_shape=[jax.ShapeDtypeStruct((s, 2048), MMD), jax.ShapeDtypeStruct((s, SSD_GROUPS * SSD_N), MMD),
                   jax.ShapeDtypeStruct((s, SSD_GROUPS * SSD_N), MMD), jax.ShapeDtypeStruct((s, 128), F32),
                   jax.ShapeDtypeStruct((1, 128), F32)],
        scratch_shapes=[pltpu.VMEM((GPS, SSD_N, GW), F32), pltpu.VMEM((GPS, CHUNK, HPG * CHUNK), F32),
                        pltpu.VMEM((GPS, CHUNK, GW), F32)],
        compiler_params=_cp(("arbitrary", "arbitrary")),
    )(xc, xc, xc, dt, cs, ex, drow, arow, dy, hprev, _tri(anti), e1, e2)


def _gnorm_fwd(ya, yb, proj, w, *, name):
    s = ya.shape[0]
    tm = _tile(s, 256)

    def body(a_ref, b_ref, z_ref, w_ref, o_ref):
        zv = z_ref[...].astype(F32)
        t = (a_ref[...].astype(F32) + b_ref[...].astype(F32)) * (zv * _sigmoid(zv))
        r = lax.rsqrt(jnp.mean(t * t, axis=-1, keepdims=True) + EPS)
        o_ref[...] = ((t * r) * w_ref[...]).astype(o_ref.dtype)

    big = pl.BlockSpec((tm, 2048), lambda i: (i, 0))
    row = pl.BlockSpec((1, 2048), lambda i: (0, 0))
    return pl.pallas_call(
        body, name=name, grid=(s // tm,), in_specs=[big, big, big, row], out_specs=big,
        out_shape=jax.ShapeDtypeStruct((s, 2048), MMD), compiler_params=_cp(("arbitrary",)),
    )(ya, yb, proj, w)


def _gnorm_bwd(dout, ya, yb, proj, xc, w, dproj, *, name):
    s = ya.shape[0]
    tm = _tile(s, 256)

    def body(do_ref, a_ref, b_ref, z_ref, x_ref, w_ref, _, dy_ref, dz_ref, dw_ref, dd_ref):
        zv = z_ref[...].astype(F32)
        sg = _sigmoid(zv)
        sz = zv * sg
        y = a_ref[...].astype(F32) + b_ref[...].astype(F32)
        t = y * sz
        r = lax.rsqrt(jnp.mean(t * t, axis=-1, keepdims=True) + EPS)
        nv = t * r
        dov = do_ref[...].astype(F32)
        _acc_rows(dw_ref, jnp.sum(dov * nv, axis=0, keepdims=True), pl.program_id(0) == 0)
        dn = dov * w_ref[...]
        dt_ = r * (dn - nv * jnp.mean(dn * nv, axis=-1, keepdims=True))
        dy = dt_ * sz
        dy_ref[...] = dy.astype(dy_ref.dtype)
        dz_ref[...] = (dt_ * y * (sg * (1.0 + zv * (1.0 - sg)))).astype(dz_ref.dtype)
        _acc_rows(dd_ref, jnp.sum(dy * x_ref[...].astype(F32), axis=0, keepdims=True), pl.program_id(0) == 0)

    big = pl.BlockSpec((tm, 2048), lambda i: (i, 0))
    row = pl.BlockSpec((1, 2048), lambda i: (0, 0))
    return pl.pallas_call(
        body, name=name, grid=(s // tm,), in_specs=[big, big, big, big, big, row, ANY], out_specs=[big, big, row, row],
        out_shape=[jax.ShapeDtypeStruct((s, 2048), MMD), jax.ShapeDtypeStruct(dproj.shape, dproj.dtype),
                   jax.ShapeDtypeStruct((1, 2048), F32), jax.ShapeDtypeStruct((1, 2048), F32)],
        input_output_aliases={6: 1}, compiler_params=_cp(("arbitrary",)),
    )(dout, ya, yb, proj, xc, w, dproj)


def _unheads(a):
    return a.transpose(1, 0, 2).reshape(a.shape[1], a.shape[0] * HEAD_DIM)


def _local_step(x, target, mod, wts, small, in_weights=None, late_weights=None, late_grads=None, in_grad=None,
                zero=0.0):
    s, d = x.shape
    shift1, scale1, gate1, shift2, scale2, gate2 = [mod[i:i + 1] for i in range(6)]

    h1 = _ln_mod(x, small["norm1_w"], scale1, shift1, name="ln1")
    qk_w = jnp.concatenate([jnp.tile(small["q_norm_w"], (1, N_Q_HEADS)), jnp.tile(small["k_norm_w"], (1, N_KV_HEADS))], axis=1)
    qk_sc = jnp.concatenate([jnp.full((1, N_Q_HEADS * HEAD_DIM), HEAD_DIM ** -0.5, F32),
                             jnp.ones((1, N_KV_HEADS * HEAD_DIM), F32)], axis=1)
    qk_sc2 = jnp.concatenate([jnp.full((1, N_Q_HEADS * HEAD_DIM), HEAD_DIM ** -0.5 * LOG2E, F32),
                              jnp.ones((1, N_KV_HEADS * HEAD_DIM), F32)], axis=1)
    tabs = _rope_tables(s, zero)
    if in_weights is not None:
        wts = {**wts, **in_weights([h1, *tabs])}
    proj = _mm(h1, wts["w_in_p"], name="in_proj", outs=[MMD], tm=512, tn=2944, b_outer=True)
    dt_raw = _mm(h1, wts["w_dt"], name="dt_proj", outs=[F32], tm=512, tn=128)
    qkt = _qk_fwd(proj, qk_w, qk_sc2, tabs, name="qk_fwd").reshape(N_Q_HEADS + N_KV_HEADS, HEAD_DIM, s)
    v_sd = proj[:, V0:V0 + N_KV_HEADS * HEAD_DIM]
    vta = jnp.concatenate([v_sd.T.reshape(N_KV_HEADS, HEAD_DIM, s), jnp.ones((N_KV_HEADS, V_AUG - HEAD_DIM, s), MMD)], axis=1)
    ot, lse = _flash_fwd(qkt, vta, name="flash_fwd")
    ot2 = ot.reshape(N_Q_HEADS * HEAD_DIM, s)
    if late_weights is not None:
        wts = {**wts, **late_weights(ot)}

    w8 = jnp.pad(small["conv_w"], ((0, 8 - D_CONV), (0, 0)))
    xc = _conv_fwd(proj, w8, small["conv_b"], name="conv_fwd")
    a_neg = -jnp.exp(small["A_log"])
    arow = jnp.pad(a_neg.reshape(1, 2 * SSD_HEADS), ((0, 0), (0, 128 - 2 * SSD_HEADS)))
    bias_row = jnp.pad(small["dt_bias"].reshape(1, 2 * SSD_HEADS), ((0, 0), (0, 128 - 2 * SSD_HEADS)))
    dt, cs = _dt_fwd(dt_raw, bias_row, arow, name="dt_fwd")
    drow = jnp.repeat(small["ssd_D"], SSD_P, axis=1)
    dirs = [dict(drow=drow), dict(drow=jnp.zeros_like(drow))]
    ex = _expand_mats()
    ys = []
    for di, dd in enumerate(dirs):
        y, dd["hprev"] = _ssd_fwd(xc, dt, cs, ex, dd["drow"], di, name=f"ssd_fwd{di}")
        ys.append(y)
    ssdn = _gnorm_fwd(ys[0], ys[1], proj, small["ssd_norm_w"], name="gnorm_fwd")

    a_o = _mm(ot2, wts["w_attn_out"], name="attn_out", outs=[MMD], ta=True, tm=512, tn=1024)

    def merge_epi(acc, ao, ga, gs):
        return (_sigmoid(ga.astype(F32)) * ao.astype(F32) + _sigmoid(gs.astype(F32)) * acc, acc)

    merged, b_o = _mm(ssdn, wts["w_ssd_out"], name="ssd_out", outs=[MMD, MMD], tm=512, tn=1024,
                      extras=[(a_o, "tile", 0), (proj, "tile", GA0), (proj, "tile", GS0)], epi=merge_epi)

    def res_epi(acc, res, gate):
        return (res + gate * acc, acc)

    x1, mo = _mm(merged, wts["w_o"], name="w_o", outs=[F32, MMD], tm=512, tn=1024,
                 extras=[(x, "tile", 0), (gate1, "row", 0)], epi=res_epi)
    h2 = _ln_mod(x1, small["norm2_w"], scale2, shift2, name="ln2")

    def relu2_epi(acc):
        rl = jnp.maximum(acc, 0.0)
        return (rl * rl, rl)

    act, rl = _mm(h2, wts["w_mlp1"], name="mlp1", outs=[MMD, MMD], tm=1024, tn=1024, epi=relu2_epi, b_outer=True)

    def loss_epi(acc, res, gate, tgt):
        return ((res + gate * acc - tgt) * (1.0 / d), acc)

    dy, ffo = _mm(act, wts["w_mlp2"], name="mlp2", outs=[F32, MMD], tm=512, tn=1024, vmem=VMEM_BIG,
                  extras=[(x1, "tile", 0), (gate2, "row", 0), (target, "tile", 0)], epi=loss_epi)
    loss = _sumsq(dy, name="loss") * (0.5 * d)

    gw = {}
    gs_ = {}
    dffo, dgate2 = _gate_bwd(dy, ffo, gate2, name="gate2_bwd")
    dpre = _mm(dffo, wts["w_mlp2"], name="mlp2_dx", outs=[MMD], nt=True, tm=1024, tn=1024, b_outer=True,
               extras=[(rl, "tile", 0)], epi=lambda acc, r: (acc * (2.0 * r.astype(F32)),))
    gw["w_mlp2"] = _mm_tn(act, dffo, name="mlp2_dw")
    dh2 = _mm(dpre, wts["w_mlp1"], name="mlp1_dx", outs=[F32], nt=True, tm=1024, tn=1024, vmem=VMEM_BIG)
    gw["w_mlp1"] = _mm_tn(h2, dpre, name="mlp1_dw")
    dx1, dshift2, dscale2, gs_["norm2_w"] = _ln_mod_bwd(dh2, x1, small["norm2_w"], scale2, dy, name="ln2_bwd")
    dmo, dgate1 = _gate_bwd(dx1, mo, gate1, name="gate1_bwd")

    def merge_bwd_epi(acc, ao, bo, ga, gs):
        sa, ss = _sigmoid(ga.astype(F32)), _sigmoid(gs.astype(F32))
        return (acc * sa, acc * ss, acc * ao.astype(F32) * sa * (1.0 - sa), acc * bo.astype(F32) * ss * (1.0 - ss))

    da_o, db_o, dga, dgs = _mm(dmo, wts["w_o"], name="w_o_dx", outs=[MMD] * 4, nt=True, tm=512, tn=1024,
                               extras=[(a_o, "tile", 0), (b_o, "tile", 0), (proj, "tile", GA0), (proj, "tile", GS0)],
                               epi=merge_bwd_epi)
    gw["w_o"] = _mm_tn(merged, dmo, name="w_o_dw")
    dot = _mm(wts["w_attn_out"], da_o, name="attn_out_dx", outs=[MMD], nt=True, tm=1024, tn=1024)
    gw["w_attn_out"] = _mm(ot2, da_o, name="attn_out_dw", outs=[F32], tm=256, tn=512, vmem=VMEM_BIG)
    dssdn = _mm(db_o, wts["w_ssd_out"], name="ssd_out_dx", outs=[MMD], nt=True, tm=512, tn=2048)
    gw["w_ssd_out"] = _mm_tn(ssdn, db_o, name="ssd_out_dw")

    dproj = lax.dynamic_update_slice(lax.empty((s, PW), MMD), jnp.concatenate([dga, dgs], axis=1), (0, GA0))

    norm_w = small["ssd_norm_w"] if late_grads is None else small["ssd_norm_w"] + late_grads(gw)
    dyssd, dproj, gs_["ssd_norm_w"], dd_row = _gnorm_bwd(dssdn, ys[0], ys[1], proj, xc, norm_w, dproj, name="gnorm_bwd")
    gs_["ssd_D"] = dd_row.reshape(SSD_HEADS, SSD_P).sum(axis=1).reshape(1, SSD_HEADS)
    dxc, ddts, das = [], [], []
    for di, dd in enumerate(dirs):
        dxs, dbm, dcm, ddt_d, da_d = _ssd_bwd(xc, dt, cs, ex, dd["drow"], arow, dyssd, dd["hprev"], di, name=f"ssd_bwd{di}")
        dxc.append((dxs, dbm, dcm))
        ddts.append(ddt_d)
        das.append(da_d)
    dw8, db, col0 = [], [], 0
    for part, (ga, gb) in enumerate(zip(*dxc)):
        dproj, dw_part, db_part = _conv_bwd(proj, col0, ga, gb, w8, small["conv_b"], dproj, name=f"conv_bwd{part}")
        dw8.append(dw_part)
        db.append(db_part)
        col0 += ga.shape[1]
    gs_["conv_w"] = jnp.concatenate(dw8, axis=1)[0:D_CONV]
    gs_["conv_b"] = jnp.concatenate(db, axis=1)
    gs_["A_log"] = (das[0] + das[1])[:, 0:2 * SSD_HEADS].reshape(2, SSD_HEADS) * a_neg
    dproj, dbias = _dt_bwd(ddts[0], ddts[1], dt_raw, bias_row, dproj, name="dt_bwd")
    gs_["dt_bias"] = dbias[:, 0:2 * SSD_HEADS].reshape(2, SSD_HEADS)

    dqt, dk_h, dv_h = _flash_bwd(qkt, vta, dot.reshape(N_Q_HEADS, HEAD_DIM, s), ot, lse, name="flash_bwd")
    dproj, dqk_w = _qk_bwd(dqt.reshape(N_Q_HEADS * HEAD_DIM, s), dk_h.transpose(0, 2, 1).reshape(N_KV_HEADS * HEAD_DIM, s),
                           proj, qk_w, qk_sc, tabs, dproj, name="qk_bwd")
    gs_["q_norm_w"] = dqk_w[:, 0:N_Q_HEADS * HEAD_DIM].reshape(N_Q_HEADS, HEAD_DIM).sum(axis=0, keepdims=True)
    gs_["k_norm_w"] = dqk_w[:, N_Q_HEADS * HEAD_DIM:].reshape(N_KV_HEADS, HEAD_DIM).sum(axis=0, keepdims=True)
    dproj = lax.dynamic_update_slice(dproj, _unheads(dv_h).astype(MMD), (0, V0))

    gw["w_in_p"] = _mm_tn(h1, dproj, name="in_proj_dw", tk=512, tn=2944, tmm=2048, vmem=VMEM_BIG)
    zero_row = jnp.zeros((1, d), F32) if in_grad is None else jnp.zeros((1, d), F32) + in_grad(gw["w_in_p"])[0:1, 0:1]
    dh1 = _mm(dproj, wts["w_in_p"], name="in_proj_dx", outs=[F32], nt=True, tm=256, tn=1024, vmem=VMEM_BIG,
              extras=[(zero_row, "row", 0)], epi=lambda acc, r: (acc + r,))
    grad_x, dshift1, dscale1, gs_["norm1_w"] = _ln_mod_bwd(dh1, x, small["norm1_w"], scale1, dx1, name="ln1_bwd")
    dmod = jnp.concatenate([dshift1, dscale1, dgate1, dshift2, dscale2, dgate2], axis=0)
    return loss, grad_x, dmod, gw, gs_


N_DEV = 8
N_CHIP = 4
ANY = pl.BlockSpec(memory_space=pl.ANY)


def _place():
    return lax.axis_index("x"), lax.axis_index("y"), lax.axis_index("c")


def _allgather8(v, *, name):
    m_per, n = v.shape

    def body(x_ref, out_ref, send_sems, recv_sems, local_sem):
        x, y, c = _place()
        me, sibling = (x, y, c), (x, y, 1 - c)
        chips = [(1 - x, y), (x, 1 - y), (1 - x, 1 - y)]

        def rows(px, py, pc):
            return out_ref.at[pl.ds((4 * px + 2 * py + pc) * m_per, m_per), :]

        def copy(k, block, to, src=None):
            return pltpu.make_async_remote_copy(
                src_ref=rows(*block) if src is None else src, dst_ref=rows(*block),
                send_sem=send_sems.at[k], recv_sem=recv_sems.at[k], device_id=to, device_id_type=MESH)

        mine = pltpu.make_async_copy(x_ref, rows(*me), local_sem)
        mine.start()
        first = [copy(0, me, sibling, src=x_ref)]
        first += [copy(1 + j, me, (*chip, c), src=x_ref) for j, chip in enumerate(chips)]
        for cp in first:
            cp.start()
        passed = [copy(4 + j, (*chip, c), sibling) for j, chip in enumerate(chips)]
        for j, chip in enumerate(chips):
            copy(1 + j, (*chip, c), me).wait_recv()
            passed[j].start()
        copy(0, sibling, me).wait_recv()
        for j, chip in enumerate(chips):
            copy(4 + j, (*chip, 1 - c), me).wait_recv()
        for cp in first + passed:
            cp.wait_send()
        mine.wait()

    return pl.pallas_call(
        body, name=name, out_shape=jax.ShapeDtypeStruct((N_DEV * m_per, n), v.dtype),
        in_specs=[pl.BlockSpec(memory_space=pltpu.VMEM)], out_specs=pl.BlockSpec(memory_space=pltpu.VMEM),
        scratch_shapes=[pltpu.SemaphoreType.DMA((7,)), pltpu.SemaphoreType.DMA((7,)), pltpu.SemaphoreType.DMA],
    )(v)


HBM = pl.BlockSpec(memory_space=pltpu.HBM)
SEM = pl.BlockSpec(memory_space=pltpu.SEMAPHORE)


def _chips_copies(x_ref, land_ref, sems, scatter, half=False):
    x, y, c = _place()
    k = 2 * x + y
    chips = [(1 - x, y), (x, 1 - y), (1 - x, 1 - y)]
    ids = [2 * cx + cy for cx, cy in chips]
    if half:
        hr = x_ref.shape[0] // 2
        rows = pl.ds(pl.multiple_of(c * hr, 16), hr)

    def copy(j, slot):
        src = x_ref.at[ids[j]] if scatter else (x_ref.at[rows] if half else x_ref)
        dst = land_ref.at[slot, rows] if half else land_ref.at[slot]
        return pltpu.make_async_remote_copy(src_ref=src, dst_ref=dst, send_sem=sems[j], recv_sem=sems[3 + j],
                                            device_id=(*chips[j], c), device_id_type=MESH)

    return [copy(j, k) for j in range(3)], [copy(j, ids[j]) for j in range(3)]


def _chips_start(src, scatter, half=False, *, name):
    shape = src.shape if scatter else (N_CHIP,) + tuple(src.shape)

    def body(x_ref, land_ref, *rest):
        sems, token = rest[0:6], rest[8]
        for cp in _chips_copies(x_ref, land_ref, sems, scatter, half)[0]:
            cp.start()
        token[...] = jnp.zeros_like(token)

    out = pl.pallas_call(
        body, name=name,
        out_shape=(pltpu.SemaphoreType.DMA(()),) * 6 + (pltpu.HBM(src.shape, src.dtype), pltpu.HBM(shape, src.dtype),
                                                       jax.ShapeDtypeStruct((8, 128), F32)),
        in_specs=(HBM, HBM), out_specs=(SEM,) * 6 + (HBM, HBM, pl.BlockSpec(memory_space=pltpu.VMEM)),
        input_output_aliases={0: 6, 1: 7},
        compiler_params=pltpu.CompilerParams(has_side_effects=pltpu.SideEffectType.DATAFLOW_SIDE_EFFECTING),
    )(pltpu.with_memory_space_constraint(src, pltpu.HBM),
      pltpu.with_memory_space_constraint(lax.empty(shape, src.dtype), pltpu.HBM))
    return out[0:6], out[6], out[7], out[8]


def _chips_wait(sems, src, land, after, scatter, half=False, *, name):
    after = list(after) if isinstance(after, (list, tuple)) else [after]

    def body(x_ref, land_ref, *rest):
        sems_ = rest[0:6]
        for cp in _chips_copies(x_ref, land_ref, sems_, scatter, half)[1]:
            cp.wait_send()
            cp.wait_recv()

    return pl.pallas_call(
        body, name=name, out_shape=(pltpu.HBM(src.shape, src.dtype), pltpu.HBM(land.shape, land.dtype)),
        in_specs=(HBM, HBM) + (SEM,) * 6 + (ANY,) * len(after), out_specs=(HBM, HBM), input_output_aliases={0: 0, 1: 1},
        compiler_params=pltpu.CompilerParams(has_side_effects=pltpu.SideEffectType.DATAFLOW_SIDE_EFFECTING),
    )(src, land, *sems, *after)


def _row_tile(r, pref=512):
    return max(t for t in range(16, pref + 1, 16) if r % t == 0)


def _pair_complete(land, *, name):
    r = land.shape[1]
    hr = r // 2
    assert r == 2 * hr and hr % 16 == 0

    def body(in_ref, out_ref, send_sems, recv_sems):
        x, y, c = _place()
        ids = [2 * cx + cy for cx, cy in [(1 - x, y), (x, 1 - y), (1 - x, 1 - y)]]
        mine_rows = pl.ds(pl.multiple_of(c * hr, 16), hr)
        other_rows = pl.ds(pl.multiple_of((1 - c) * hr, 16), hr)

        def copy(j, rows):
            return pltpu.make_async_remote_copy(
                src_ref=in_ref.at[ids[j], mine_rows], dst_ref=out_ref.at[ids[j], rows], send_sem=send_sems.at[j],
                recv_sem=recv_sems.at[j], device_id=(x, y, 1 - c), device_id_type=MESH)

        sends = [copy(j, mine_rows) for j in range(3)]
        for cp in sends:
            cp.start()
        for j in range(3):
            copy(j, other_rows).wait_recv()
        for cp in sends:
            cp.wait_send()

    return pl.pallas_call(
        body, name=name, out_shape=jax.ShapeDtypeStruct(land.shape, land.dtype), in_specs=[ANY], out_specs=ANY,
        input_output_aliases={0: 0},
        scratch_shapes=[pltpu.SemaphoreType.DMA((3,)), pltpu.SemaphoreType.DMA((3,))],
    )(land)


def _pair_swap(a, *, name):
    n, r, cols = a.shape
    hr = r // 2

    def body(x_ref, out_ref, send_sem, recv_sem):
        x, y, c = _place()
        other_rows = pl.ds(pl.multiple_of((1 - c) * hr, 16), hr)
        cp = pltpu.make_async_remote_copy(src_ref=x_ref.at[:, other_rows], dst_ref=out_ref, send_sem=send_sem,
                                          recv_sem=recv_sem, device_id=(x, y, 1 - c), device_id_type=MESH)
        cp.start()
        cp.wait()

    return pl.pallas_call(
        body, name=name, out_shape=jax.ShapeDtypeStruct((n, hr, cols), a.dtype), in_specs=[ANY], out_specs=ANY,
        scratch_shapes=[pltpu.SemaphoreType.DMA, pltpu.SemaphoreType.DMA],
    )(a)


def _sibling_copy(a, *, name):
    def body(x_ref, out_ref, send_sem, recv_sem):
        x, y, c = _place()
        cp = pltpu.make_async_remote_copy(src_ref=x_ref, dst_ref=out_ref, send_sem=send_sem, recv_sem=recv_sem,
                                          device_id=(x, y, 1 - c), device_id_type=MESH)
        cp.start()
        cp.wait()

    return pl.pallas_call(
        body, name=name, out_shape=jax.ShapeDtypeStruct(a.shape, a.dtype), in_specs=[ANY], out_specs=ANY,
        scratch_shapes=[pltpu.SemaphoreType.DMA, pltpu.SemaphoreType.DMA],
    )(a)


def _sum_slots(a, own, *, name):
    _, r, c = a.shape
    tr = _row_tile(r, 256)

    def body(a_ref, own_ref, o_ref):
        k = 2 * lax.axis_index("x") + lax.axis_index("y")
        acc = None
        for j in range(N_CHIP):
            term = jnp.where(k == j, own_ref[j], a_ref[j]).astype(F32)
            acc = term if acc is None else acc + term
        o_ref[...] = acc

    spec = pl.BlockSpec((N_CHIP, tr, c), lambda i: (0, i, 0))
    return pl.pallas_call(
        body, name=name, grid=(r // tr,), in_specs=[spec, spec],
        out_specs=pl.BlockSpec((tr, c), lambda i: (i, 0)), out_shape=jax.ShapeDtypeStruct((r, c), F32),
        compiler_params=_cp(("arbitrary",)),
    )(a, own)


def _add2(a, b, *, name):
    r, c = a.shape
    tr = _row_tile(r)

    def body(a_ref, b_ref, o_ref):
        o_ref[...] = (a_ref[...].astype(F32) + b_ref[...].astype(F32)).astype(o_ref.dtype)

    spec = pl.BlockSpec((tr, c), lambda i: (i, 0))
    return pl.pallas_call(
        body, name=name, grid=(r // tr,), in_specs=[spec, spec], out_specs=spec,
        out_shape=jax.ShapeDtypeStruct((r, c), a.dtype), compiler_params=_cp(("arbitrary",)),
    )(a, b)


BIG = ("w_in", "w_mlp1", "w_attn_out", "w_ssd_out", "w_o", "w_mlp2")
COL_SHARDED = ("w_mlp1", "w_in")
ROW_SHARDED = ("w_attn_out", "w_ssd_out", "w_o", "w_mlp2")
LATE = ROW_SHARDED + ("w_mlp1",)
SMALL = ("b_ada", "norm1_w", "norm2_w", "q_norm_w", "k_norm_w", "conv_b", "A_log", "dt_bias", "ssd_D", "ssd_norm_w")
NAMES = ("w_ada", "b_ada", "norm1_w", "norm2_w", "w_in", "q_norm_w", "k_norm_w", "conv_w", "conv_b", "A_log", "dt_bias",
         "ssd_D", "ssd_norm_w", "w_attn_out", "w_ssd_out", "w_o", "w_mlp1", "w_mlp2")
W_IN_COLS = 8768


def _permute_in(w):
    return jnp.concatenate([w[:, 4608:6656], w[:, 6720:8768], w[:, 1536:4608], w[:, 0:1536], w[:, 6656:6720],
                            jnp.zeros((w.shape[0], PW - W_IN_COLS), w.dtype)], axis=1)


def _unpermute_in(wp):
    return jnp.concatenate([wp[:, Q0:DT0], wp[:, XS0:Q0], wp[:, Z0:GA0], wp[:, DT0:DT0 + 64], wp[:, GA0:XS0]], axis=1)


def _pad_to(v, n):
    return jnp.pad(v, (0, n - v.shape[0]))


def _step(w, m, v, loss_target):
    xi, yi, ci = _place()
    chip = 2 * xi + yi
    dev = 4 * xi + 2 * yi + ci
    x, tgt = w["x"], loss_target
    d = x.shape[1]

    cw = w["conv_w"].shape[1]
    v0 = _pad_to(jnp.concatenate([w["c"].reshape(-1), w["conv_w"].reshape(-1)]), 5120).reshape(8, 640)
    g0 = _allgather8(v0, name="ag_cond").reshape(N_DEV, 5120)
    c_all = g0[:, 0:d]
    conv_w = jnp.concatenate([g0[2 * k, d:d + D_CONV * cw].reshape(D_CONV, cw) for k in range(N_CHIP)], axis=1)
    sc = _silu_cast(c_all, name="silu_c")
    modp = _mm(sc, w["w_ada"].astype(MMD), name="ada_fwd", outs=[F32], tm=8, tn=512)
    g1 = _allgather8(modp, name="ag_mod").reshape(N_DEV, N_DEV, modp.shape[1])
    mod_all = jnp.concatenate([g1[2 * k] for k in range(N_CHIP)], axis=1)
    mod = (lax.dynamic_slice_in_dim(mod_all, dev, 1, axis=0) + w["b_ada"]).reshape(6, d)

    mine, mod = lax.optimization_barrier((w["w_in"].astype(MMD), mod))
    in_sems, in_src, in_land, in_token = _chips_start(mine, False, True, name="ag_w_in_start")
    mod = mod + in_token[0:1, 0:1]
    small = {n: w[n] for n in SMALL if n != "b_ada"}
    small["conv_w"] = conv_w
    started = {}

    late_mine = jnp.concatenate([w[n].astype(MMD) for n in LATE], axis=0) + in_token[0:1, 0:1].astype(MMD)

    def in_weights(after):
        src, land = _chips_wait(in_sems, in_src, in_land, [*after, late_mine], False, True, name="ag_w_in_wait")
        land = _pair_complete(land, name="ag_w_in_pair")
        late, land = lax.optimization_barrier((late_mine, land))
        sems, late_src, late_land, token = _chips_start(late, False, name="ag_late_start")
        started["ag_late"] = (sems, late_src, late_land)
        w_in = jnp.concatenate([jnp.where(chip == k, src, land[k]) for k in range(N_CHIP)], axis=1)
        w_dt = jnp.pad(w_in[:, 6656:6720], ((0, 0), (0, 64))) + token[0:1, 0:1].astype(MMD)
        return {"w_in_p": _permute_in(w_in), "w_dt": w_dt}

    def late_weights(after):
        src, land = _chips_wait(*started["ag_late"], after, False, name="ag_late_wait")
        out, o = {}, 0
        for n in LATE:
            rows = w[n].shape[0]
            parts = [jnp.where(chip == k, src[o:o + rows], land[k, o:o + rows]) for k in range(N_CHIP)]
            out[n] = jnp.concatenate(parts, axis=1 if n in COL_SHARDED else 0)
            o += rows
        return out

    def pair_sums(slots, tag):
        _, rows, cols = slots.shape
        hr = rows // 2
        theirs = _pair_swap(slots, name="rs_pair_" + tag)
        ours = lax.dynamic_slice_in_dim(slots, ci * hr, hr, axis=1)
        pair = _add2(ours.reshape(N_CHIP * hr, cols), theirs.reshape(N_CHIP * hr, cols), name="rs_pair_sum_" + tag)
        return pair.reshape(N_CHIP, hr, cols)

    def finish(recv, pair, tag):
        half = _sum_slots(recv, pair, name="rs_sum_" + tag)
        other = _sibling_copy(half, name="rs_sibling_" + tag)
        hr = half.shape[0]
        both = lax.dynamic_update_slice_in_dim(lax.empty((2 * hr, half.shape[1]), F32), half, ci * hr, axis=0)
        return lax.dynamic_update_slice_in_dim(both, other, (1 - ci) * hr, axis=0)

    def late_grads(gw):
        slots = []
        for k in range(N_CHIP):
            parts = []
            for n in LATE:
                rows = w[n].shape[0]
                blk = gw[n][:, k * rows:(k + 1) * rows] if n in COL_SHARDED else gw[n][k * rows:(k + 1) * rows]
                parts.append(blk.astype(MMD))
            slots.append(jnp.concatenate(parts, axis=0))
        pair = pair_sums(jnp.stack(slots), "late")
        sems, src, land, token = _chips_start(pair, True, name="rs_late_start")
        started["late"] = (sems, src, land)
        return token[0:1, 0:1]

    def in_grad(g):
        g_in = _unpermute_in(g)
        cols_in = w["w_in"].shape[1]
        pair = pair_sums(jnp.stack([g_in[:, k * cols_in:(k + 1) * cols_in].astype(MMD) for k in range(N_CHIP)]), "w_in")
        sems, src, land, token = _chips_start(pair, True, name="rs_w_in_start")
        started["w_in"] = (sems, src, land)
        return token

    loss, grad_x, dmod, gw, gs = _local_step(x, tgt, mod, {}, small, in_weights, late_weights, late_grads, in_grad,
                                             in_token[0, 0])

    grads = {}
    pair, land = _chips_wait(*started["w_in"], grad_x, True, name="rs_w_in_wait")
    grads["w_in"] = finish(land, pair, "w_in")
    pair, land = _chips_wait(*started["late"], grad_x, True, name="rs_late_wait")
    total, o = finish(land, pair, "late"), 0
    for n in LATE:
        rows = w[n].shape[0]
        grads[n] = total[o:o + rows]
        o += rows

    order = ([dmod.reshape(-1)] + [gs[n].reshape(-1) for n in SMALL if n != "b_ada"] + [gs["conv_w"].reshape(-1)]
             + [loss.reshape(-1)])
    vec = jnp.concatenate(order)
    n_small = vec.shape[0]
    n_pad = -(-n_small // 1024) * 1024
    g2 = _allgather8(_pad_to(vec, n_pad).reshape(8, n_pad // 8), name="ag_small")
    tot = _rows_sum(g2, N_DEV, name="small_sum").reshape(-1)
    loss = tot[n_small - 1]
    dmod_all = g2.reshape(N_DEV, n_pad)[:, 0:6 * d]
    off = 0
    for n in SMALL:
        grads[n] = tot[off:off + w[n].size].reshape(w[n].shape)
        off += w[n].size
    conv_full = tot[off:off + D_CONV * N_CHIP * cw].reshape(D_CONV, N_CHIP * cw)
    grads["conv_w"] = lax.dynamic_slice_in_dim(conv_full, chip * cw, cw, axis=1)
    ada_cols = w["w_ada"].shape[1]
    dmod_mine = lax.dynamic_slice_in_dim(dmod_all, chip * ada_cols, ada_cols, axis=1).astype(MMD)
    grads["w_ada"] = _mm_tn(sc, dmod_mine, name="ada_dw", tk=512, tn=512, tmm=8)

    delta, new_m, new_v = {}, {}, {}
    pack = lambda t: jnp.concatenate([t[n].reshape(-1) for n in SMALL]).reshape(1, -1)
    ds_, ms_, vs_ = _adamw(pack(w), pack(grads), pack(m), pack(v), name="adamw_small")
    off = 0
    for n in SMALL:
        for dst, src in ((delta, ds_), (new_m, ms_), (new_v, vs_)):
            dst[n] = src[0, off:off + w[n].size].reshape(w[n].shape)
        off += w[n].size
    for n in ("w_ada", "conv_w") + BIG:
        delta[n], new_m[n], new_v[n] = _adamw(w[n], grads[n], m[n], v[n], name="adamw_" + n)
    return loss, grad_x, grads, delta, new_m, new_v


def kernel(x, c, w_ada, b_ada, norm1_w, norm2_w, w_in, q_norm_w, k_norm_w, conv_w, conv_b, A_log, dt_bias, ssd_D, ssd_norm_w, w_attn_out, w_ssd_out, w_o, w_mlp1, w_mlp2, loss_target, m_w_ada, m_b_ada, m_norm1_w, m_norm2_w, m_w_in, m_q_norm_w, m_k_norm_w, m_conv_w, m_conv_b, m_A_log, m_dt_bias, m_ssd_D, m_ssd_norm_w, m_w_attn_out, m_w_ssd_out, m_w_o, m_w_mlp1, m_w_mlp2, v_w_ada, v_b_ada, v_norm1_w, v_norm2_w, v_w_in, v_q_norm_w, v_k_norm_w, v_conv_w, v_conv_b, v_A_log, v_dt_bias, v_ssd_D, v_ssd_norm_w, v_w_attn_out, v_w_ssd_out, v_w_o, v_w_mlp1, v_w_mlp2):
    args = dict(locals())
    strip = lambda a: a[0] if a.ndim == 3 else a
    w = {n: strip(args[n]) for n in NAMES + ("x", "c")}
    m = {n: strip(args["m_" + n]) for n in NAMES}
    v = {n: strip(args["v_" + n]) for n in NAMES}
    loss, grad_x, grads, delta, new_m, new_v = _step(w, m, v, loss_target[0])
    like = lambda t, n: t.reshape(args[n].shape)
    return (loss, grad_x[None], *[like(grads[n], n) for n in NAMES], *[like(delta[n], n) for n in NAMES],
            *[like(new_m[n], n) for n in NAMES], *[like(new_v[n], n) for n in NAMES])
```

```python
import math

import jax
import jax.numpy as jnp
from jax import lax
from jax.experimental import pallas as pl
from jax.experimental.pallas import tpu as pltpu

F32 = jnp.float32
MMD = jnp.bfloat16
EPS = 1e-6
NEG = -1e30
MIB = 1024 * 1024
VMEM_BIG = 56 * MIB
VMEM_MID = 40 * MIB

GRID_W = 64
N_Q_HEADS, N_KV_HEADS, HEAD_DIM = 16, 4, 64
ROPE_THETA = 10000.0
SSD_HEADS, SSD_GROUPS, SSD_P, SSD_N, CHUNK = 32, 4, 64, 128, 128
HPG = SSD_HEADS // SSD_GROUPS
D_CONV = 5
ADAM_LR, ADAM_B1, ADAM_B2, ADAM_EPS, ADAM_WD, ADAM_STEP = 0.001, 0.9, 0.999, 1e-08, 0.01, 10

Z0, GA0, GS0, XS0, B0, C0, Q0, K0, V0, DT0, PW = 0, 2048, 3072, 4096, 6144, 6656, 7168, 8192, 8448, 8704, 8832

MESH = pl.DeviceIdType.MESH
NT = (((1,), (1,)), ((), ()))
TN = (((0,), (0,)), ((), ()))


def _cp(sem=None, vmem=VMEM_MID):
    return pltpu.CompilerParams(dimension_semantics=sem, vmem_limit_bytes=vmem)


def _tile(n, pref):
    t = min(n, pref)
    while n % t:
        t //= 2
    return t


def _dot(a, b, dims=None):
    if dims is None:
        return jnp.dot(a, b, preferred_element_type=F32)
    return lax.dot_general(a, b, dims, preferred_element_type=F32)


def _dot_hi(a01, b):
    a = a01.astype(jnp.bfloat16)
    h1 = b.astype(jnp.bfloat16)
    r1 = b - h1.astype(F32)
    h2 = r1.astype(jnp.bfloat16)
    return _dot(a, h1) + _dot(a, h2) + _dot(a, (r1 - h2.astype(F32)).astype(jnp.bfloat16))


def _sigmoid(x):
    return jax.nn.sigmoid(x)


def _mm(a, b, *, name, outs, nt=False, ta=False, extras=(), epi=None, tm=512, tn=512, n=None, b_outer=False,
        vmem=VMEM_MID, row_sums=0):
    assert not (nt and ta)
    k, m = a.shape if ta else a.shape[::-1]
    if n is None:
        n = b.shape[0] if nt else b.shape[1]
    tm, tn = _tile(m, tm), _tile(n, tn)
    gi, gj = m // tm, n // tn
    if b_outer:
        grid = (gj, gi)
        ij = lambda p, q: (q, p)
    else:
        grid = (gi, gj)
        ij = lambda p, q: (p, q)
    if ta:
        a_spec = pl.BlockSpec((k, tm), lambda p, q: (0, ij(p, q)[0]))
    else:
        a_spec = pl.BlockSpec((tm, k), lambda p, q: (ij(p, q)[0], 0))
    if nt:
        b_spec = pl.BlockSpec((tn, k), lambda p, q: (ij(p, q)[1], 0))
    else:
        b_spec = pl.BlockSpec((k, tn), lambda p, q: (0, ij(p, q)[1]))
    e_specs = []
    for arr, kind, off in extras:
        ob = off // tn
        assert off % tn == 0
        if kind == "tile":
            e_specs.append(pl.BlockSpec((tm, tn), lambda p, q, ob=ob: (ij(p, q)[0], ob + ij(p, q)[1])))
        else:
            e_specs.append(pl.BlockSpec((1, tn), lambda p, q, ob=ob: (0, ob + ij(p, q)[1])))
    ne = len(extras)

    assert row_sums == 0 or (gj == 1 and not b_outer)

    def body(a_ref, b_ref, *rest):
        acc = _dot(a_ref[...], b_ref[...], NT if nt else (TN if ta else None))
        res = epi(acc, *[e[...] for e in rest[:ne]]) if epi is not None else (acc,)
        for o_ref, r in zip(rest[ne:ne + len(outs)], res):
            o_ref[...] = r.astype(o_ref.dtype)
        for o_ref, r in zip(rest[ne + len(outs):], res[len(outs):]):
            _acc_rows(o_ref, r, pl.program_id(0) == 0)

    out = pl.pallas_call(
        body, name=name, grid=grid,
        in_specs=[a_spec, b_spec] + e_specs,
        out_specs=[pl.BlockSpec((tm, tn), lambda p, q: ij(p, q)) for _ in outs]
        + [pl.BlockSpec((1, tn), lambda p, q: (0, 0))] * row_sums,
        out_shape=[jax.ShapeDtypeStruct((m, n), dt) for dt in outs] + [jax.ShapeDtypeStruct((1, n), F32)] * row_sums,
        compiler_params=_cp(("arbitrary", "arbitrary"), vmem),
    )(a, b, *[e[0] for e in extras])
    return out if len(outs) + row_sums > 1 else out[0]


def _mm_tn(a, g, *, name, tk=512, tn=1024, tmm=4096, vmem=VMEM_MID):
    m, k = a.shape
    n = g.shape[1]
    tk, tn, tmm = _tile(k, tk), _tile(n, tn), _tile(m, tmm)

    def body(a_ref, g_ref, o_ref):
        p = _dot(a_ref[...], g_ref[...], TN)

        @pl.when(pl.program_id(2) == 0)
        def _():
            o_ref[...] = p

        @pl.when(pl.program_id(2) > 0)
        def _():
            o_ref[...] += p

    return pl.pallas_call(
        body, name=name, grid=(k // tk, n // tn, m // tmm),
        in_specs=[pl.BlockSpec((tmm, tk), lambda i, j, r: (r, i)), pl.BlockSpec((tmm, tn), lambda i, j, r: (r, j))],
        out_specs=pl.BlockSpec((tk, tn), lambda i, j, r: (i, j)),
        out_shape=jax.ShapeDtypeStruct((k, n), F32),
        compiler_params=_cp(("arbitrary", "arbitrary", "arbitrary"), vmem),
    )(a, g)


def _adamw(w, g, m, v, *, name):
    r, c = w.shape
    tr = _tile(r, 256) if r % 8 == 0 else r

    def body(w_ref, g_ref, m_ref, v_ref, d_ref, nm_ref, nv_ref):
        gg = g_ref[...]
        nm = ADAM_B1 * m_ref[...] + (1.0 - ADAM_B1) * gg
        nv = ADAM_B2 * v_ref[...] + (1.0 - ADAM_B2) * jnp.square(gg)
        m_hat = nm / (1.0 - ADAM_B1 ** ADAM_STEP)
        v_hat = nv / (1.0 - ADAM_B2 ** ADAM_STEP)
        d_ref[...] = -ADAM_LR * (m_hat / (jnp.sqrt(v_hat) + ADAM_EPS) + ADAM_WD * w_ref[...])
        nm_ref[...] = nm
        nv_ref[...] = nv

    spec = pl.BlockSpec((tr, c), lambda i: (i, 0))
    return pl.pallas_call(
        body, name=name, grid=(r // tr,), in_specs=[spec] * 4, out_specs=[spec] * 3,
        out_shape=[jax.ShapeDtypeStruct((r, c), F32)] * 3, compiler_params=_cp(("arbitrary",)),
    )(w, g, m, v)


def _rows_sum(a, groups, *, name):
    r = a.shape[0] // groups

    def body(a_ref, o_ref):
        acc = a_ref[0:r, :]
        for d in range(1, groups):
            acc = acc + a_ref[d * r:(d + 1) * r, :]
        o_ref[...] = acc

    return pl.pallas_call(body, name=name, out_shape=jax.ShapeDtypeStruct((r, a.shape[1]), F32))(a)


def _silu_cast(a, *, name):
    def body(a_ref, o_ref):
        x = a_ref[...]
        o_ref[...] = (x * _sigmoid(x)).astype(o_ref.dtype)

    return pl.pallas_call(body, name=name, out_shape=jax.ShapeDtypeStruct(a.shape, MMD))(a)


def _acc_rows(o_ref, p, first):
    @pl.when(first)
    def _():
        o_ref[...] = p

    @pl.when(jnp.logical_not(first))
    def _():
        o_ref[...] += p


def _ln_mod(x, w, scale, shift, *, name):
    s, d = x.shape
    tm = _tile(s, 512)

    def body(x_ref, w_ref, sc_ref, sh_ref, o_ref):
        xv = x_ref[...]
        r = lax.rsqrt(jnp.mean(xv * xv, axis=-1, keepdims=True) + EPS)
        o_ref[...] = ((xv * r) * w_ref[...] * (1.0 + sc_ref[...]) + sh_ref[...]).astype(o_ref.dtype)

    row = pl.BlockSpec((1, d), lambda i: (0, 0))
    big = pl.BlockSpec((tm, d), lambda i: (i, 0))
    return pl.pallas_call(
        body, name=name, grid=(s // tm,), in_specs=[big, row, row, row], out_specs=big,
        out_shape=jax.ShapeDtypeStruct((s, d), MMD), compiler_params=_cp(("arbitrary",)),
    )(x, w, scale, shift)


def _ln_mod_bwd(dh, x, w, scale, dres, *, name):
    s, d = x.shape
    tm = _tile(s, 512)

    def body(dh_ref, x_ref, w_ref, sc_ref, dres_ref, dx_ref, dsh_ref, dsc_ref, dw_ref):
        xv = x_ref[...]
        dhv = dh_ref[...].astype(F32)
        r = lax.rsqrt(jnp.mean(xv * xv, axis=-1, keepdims=True) + EPS)
        nv = xv * r
        wv = w_ref[...]
        g1 = 1.0 + sc_ref[...]
        dn = dhv * (wv * g1)
        dx_ref[...] = dres_ref[...] + r * (dn - nv * jnp.mean(dn * nv, axis=-1, keepdims=True))
        first = pl.program_id(0) == 0
        _acc_rows(dsh_ref, jnp.sum(dhv, axis=0, keepdims=True), first)
        _acc_rows(dsc_ref, jnp.sum(dhv * nv * wv, axis=0, keepdims=True), first)
        _acc_rows(dw_ref, jnp.sum(dhv * nv * g1, axis=0, keepdims=True), first)

    row = pl.BlockSpec((1, d), lambda i: (0, 0))
    big = pl.BlockSpec((tm, d), lambda i: (i, 0))
    return pl.pallas_call(
        body, name=name, grid=(s // tm,), in_specs=[big, big, row, row, big], out_specs=[big, row, row, row],
        out_shape=[jax.ShapeDtypeStruct((s, d), F32)] + [jax.ShapeDtypeStruct((1, d), F32)] * 3,
        compiler_params=_cp(("arbitrary",)),
    )(dh, x, w, scale, dres)


def _gate_bwd(dy, u, gate, *, name):
    s, d = dy.shape
    tm = _tile(s, 512)

    def body(dy_ref, u_ref, g_ref, du_ref, dg_ref):
        dyv = dy_ref[...]
        du_ref[...] = (dyv * g_ref[...]).astype(du_ref.dtype)
        _acc_rows(dg_ref, jnp.sum(dyv * u_ref[...].astype(F32), axis=0, keepdims=True), pl.program_id(0) == 0)

    row = pl.BlockSpec((1, d), lambda i: (0, 0))
    big = pl.BlockSpec((tm, d), lambda i: (i, 0))
    return pl.pallas_call(
        body, name=name, grid=(s // tm,), in_specs=[big, big, row], out_specs=[big, row],
        out_shape=[jax.ShapeDtypeStruct((s, d), MMD), jax.ShapeDtypeStruct((1, d), F32)],
        compiler_params=_cp(("arbitrary",)),
    )(dy, u, gate)


def _seg64(v, e):
    hi = v.astype(jnp.bfloat16)
    lo = (v - hi.astype(F32)).astype(jnp.bfloat16)
    return _dot(hi, e) + _dot(lo, e)


def _rope_tables(s, zero=0.0):
    rows = s // GRID_W
    pos_row = jnp.repeat(jnp.arange(rows, dtype=jnp.int32), GRID_W).astype(F32) + zero
    pos_col = jnp.tile(jnp.arange(GRID_W, dtype=jnp.int32), rows).astype(F32) + zero
    axis_dim = HEAD_DIM // 2
    inv_freq = ROPE_THETA ** (-jnp.arange(0, axis_dim, 2, dtype=F32) / axis_dim)
    ang_r = pos_row[:, None] * inv_freq[None, :]
    ang_c = pos_col[:, None] * inv_freq[None, :]
    zero = jnp.zeros_like(ang_r)
    cos = jnp.concatenate([jnp.cos(ang_r), jnp.cos(ang_r), jnp.cos(ang_c), jnp.cos(ang_c)], axis=1)
    s_a = jnp.concatenate([-jnp.sin(ang_r), zero, -jnp.sin(ang_c), zero], axis=1)
    s_b = jnp.concatenate([zero, jnp.sin(ang_r), zero, jnp.sin(ang_c)], axis=1)
    return [jnp.tile(t, (1, 2)) for t in (cos, s_a, s_b)]


def _e128():
    i = jnp.arange(128)
    return (i[:, None] // 64 == i[None, :] // 64).astype(jnp.bfloat16)


QKW = N_Q_HEADS * HEAD_DIM + N_KV_HEADS * HEAD_DIM


def _qk_fwd(proj, wrow, scrow, tabs, *, name):
    s = proj.shape[0]
    tm = _tile(s, 1024)

    def body(x_ref, w_ref, sc_ref, cos_ref, sa_ref, sb_ref, e_ref, ot_ref):
        u = x_ref[...].astype(F32)
        r = lax.rsqrt(_seg64(u * u, e_ref[...]) * (1.0 / HEAD_DIM) + EPS)
        nv = (u * r) * w_ref[...]
        ro = nv * cos_ref[...] + pltpu.roll(nv, 112, 1) * sa_ref[...] + pltpu.roll(nv, 16, 1) * sb_ref[...]
        ot_ref[...] = (ro * sc_ref[...]).T.astype(ot_ref.dtype)

    tab = pl.BlockSpec((tm, 128), lambda i, j: (i, 0))
    row = pl.BlockSpec((1, 128), lambda i, j: (0, j))
    return pl.pallas_call(
        body, name=name, grid=(s // tm, QKW // 128),
        in_specs=[pl.BlockSpec((tm, 128), lambda i, j: (i, Q0 // 128 + j)), row, row, tab, tab, tab,
                  pl.BlockSpec((128, 128), lambda i, j: (0, 0))],
        out_specs=pl.BlockSpec((128, tm), lambda i, j: (j, i)),
        out_shape=jax.ShapeDtypeStruct((QKW, s), MMD), compiler_params=_cp(("arbitrary", "arbitrary")),
    )(proj, wrow, scrow, *tabs, _e128())


def _qk_bwd(dqt, dkt, proj, wrow, scrow, tabs, dproj, *, name):
    s = proj.shape[0]
    tm = _tile(s, 1024)
    nq = dqt.shape[0] // 128

    def body(dq_ref, dk_ref, x_ref, w_ref, sc_ref, cos_ref, sa_ref, sb_ref, e_ref, _, du_ref, dw_ref):
        e = e_ref[...]
        d = jnp.where(pl.program_id(0) < nq, dq_ref[...], dk_ref[...]).T * sc_ref[...]
        dn = d * cos_ref[...] + pltpu.roll(d * sa_ref[...], 16, 1) + pltpu.roll(d * sb_ref[...], 112, 1)
        u = x_ref[...].astype(F32)
        r = lax.rsqrt(_seg64(u * u, e) * (1.0 / HEAD_DIM) + EPS)
        uh = u * r
        _acc_rows(dw_ref, jnp.sum(dn * uh, axis=0, keepdims=True), pl.program_id(1) == 0)
        dnw = dn * w_ref[...]
        du_ref[...] = (r * (dnw - uh * (_seg64(dnw * uh, e) * (1.0 / HEAD_DIM)))).astype(du_ref.dtype)

    tab = pl.BlockSpec((tm, 128), lambda j, i: (i, 0))
    row = pl.BlockSpec((1, 128), lambda j, i: (0, j))
    qcol = pl.BlockSpec((tm, 128), lambda j, i: (i, Q0 // 128 + j))
    return pl.pallas_call(
        body, name=name, grid=(QKW // 128, s // tm),
        in_specs=[pl.BlockSpec((128, tm), lambda j, i: (jnp.minimum(j, nq - 1), i)),
                  pl.BlockSpec((128, tm), lambda j, i: (jnp.maximum(j - nq, 0), i)),
                  qcol, row, row, tab, tab, tab, pl.BlockSpec((128, 128), lambda j, i: (0, 0)), ANY],
        out_specs=[qcol, row],
        out_shape=[jax.ShapeDtypeStruct(dproj.shape, dproj.dtype), jax.ShapeDtypeStruct((1, QKW), F32)],
        input_output_aliases={9: 0}, compiler_params=_cp(("arbitrary", "arbitrary")),
    )(dqt, dkt, proj, wrow, scrow, *tabs, _e128(), dproj)


REP = N_Q_HEADS // N_KV_HEADS


def _lanes(ref):
    return jnp.concatenate([ref[r] for r in range(REP)], axis=1)


V_AUG = HEAD_DIM + 8
LOG2E = math.log2(math.e)


def _flash_fwd(qkt, vta, *, name):
    s = qkt.shape[2]
    tq, tk = _tile(s, 1024), _tile(s, 512)
    nk = s // tk
    lanes = REP * tq

    def body(q_ref, k_ref, v_ref, o_ref, lse_ref, m_ref, acc_ref):
        j = pl.program_id(2)

        @pl.when(j == 0)
        def _():
            m_ref[...] = jnp.full_like(m_ref, NEG)
            acc_ref[...] = jnp.zeros_like(acc_ref)

        st = _dot(k_ref[0], _lanes(q_ref), TN)
        m_prev = m_ref[...]
        m_new = jnp.maximum(m_prev, jnp.max(st, axis=0, keepdims=True))
        p = jnp.exp2(st - m_new).astype(MMD)
        acc_ref[...] = jnp.exp2(m_prev - m_new) * acc_ref[...] + _dot(v_ref[0], p)
        m_ref[...] = m_new

        @pl.when(j == nk - 1)
        def _():
            acc = acc_ref[...]
            l = acc[HEAD_DIM:HEAD_DIM + 1]
            o = acc[0:HEAD_DIM] / l
            ls = m_ref[...] + jnp.log(l) * LOG2E
            for r in range(REP):
                o_ref[r] = o[:, r * tq:(r + 1) * tq].astype(o_ref.dtype)
                lse_ref[r] = ls[:, r * tq:(r + 1) * tq]

    qspec = pl.BlockSpec((REP, HEAD_DIM, tq), lambda g, i, j: (g, 0, i))
    return pl.pallas_call(
        body, name=name, grid=(N_KV_HEADS, s // tq, nk),
        in_specs=[qspec, pl.BlockSpec((1, HEAD_DIM, tk), lambda g, i, j: (N_Q_HEADS + g, 0, j)),
                  pl.BlockSpec((1, V_AUG, tk), lambda g, i, j: (g, 0, j))],
        out_specs=[qspec, pl.BlockSpec((REP, 1, tq), lambda g, i, j: (g, 0, i))],
        out_shape=[jax.ShapeDtypeStruct((N_Q_HEADS, HEAD_DIM, s), MMD), jax.ShapeDtypeStruct((N_Q_HEADS, 1, s), F32)],
        scratch_shapes=[pltpu.VMEM((1, lanes), F32), pltpu.VMEM((V_AUG, lanes), F32)],
        compiler_params=_cp(("arbitrary", "arbitrary", "arbitrary"), VMEM_BIG),
    )(qkt, qkt, vta)


def _flash_bwd(qkt, vta, dot, ot, lse, *, name):
    s = qkt.shape[2]
    tq, tk = _tile(s, 512), _tile(s, 1024)
    nk = s // tk

    def body(q_ref, kt_ref, vt_ref, do_ref, o_ref, lse_ref, dq_ref, dk_ref, dv_ref, dq_acc):
        i, j = pl.program_id(1), pl.program_id(2)
        q, do = _lanes(q_ref), _lanes(do_ref)
        delta = jnp.sum(do.astype(F32) * _lanes(o_ref).astype(F32), axis=0, keepdims=True)
        kt, vt = kt_ref[0], vt_ref[0, 0:HEAD_DIM, :]
        p = jnp.exp2(_dot(kt, q, TN) - _lanes(lse_ref))
        dvc = _dot(p.astype(MMD), do, NT)
        ds = (p * (_dot(vt, do, TN) - delta)).astype(MMD)
        dkc = _dot(ds, q, NT) * (1.0 / LOG2E)
        dqc = _dot(kt, ds)
        rows = pl.ds(pl.multiple_of(j * tk, tk), tk)

        @pl.when(i == 0)
        def _():
            dk_ref[0, rows, :] = dkc
            dv_ref[0, rows, :] = dvc

        @pl.when(i > 0)
        def _():
            dk_ref[0, rows, :] += dkc
            dv_ref[0, rows, :] += dvc

        @pl.when(j == 0)
        def _():
            dq_acc[...] = dqc

        @pl.when(j > 0)
        def _():
            dq_acc[...] += dqc

        @pl.when(j == nk - 1)
        def _():
            acc = dq_acc[...]
            for r in range(REP):
                dq_ref[r] = acc[:, r * tq:(r + 1) * tq]

    qspec = pl.BlockSpec((REP, HEAD_DIM, tq), lambda g, i, j: (g, 0, i))
    kvres = pl.BlockSpec((1, s, HEAD_DIM), lambda g, i, j: (g, 0, 0))
    return pl.pallas_call(
        body, name=name, grid=(N_KV_HEADS, s // tq, nk),
        in_specs=[qspec, pl.BlockSpec((1, HEAD_DIM, tk), lambda g, i, j: (N_Q_HEADS + g, 0, j)),
                  pl.BlockSpec((1, V_AUG, tk), lambda g, i, j: (g, 0, j)),
                  qspec, qspec, pl.BlockSpec((REP, 1, tq), lambda g, i, j: (g, 0, i))],
        out_specs=[qspec, kvres, kvres],
        out_shape=[jax.ShapeDtypeStruct((N_Q_HEADS, HEAD_DIM, s), F32), jax.ShapeDtypeStruct((N_KV_HEADS, s, HEAD_DIM), F32),
                   jax.ShapeDtypeStruct((N_KV_HEADS, s, HEAD_DIM), F32)],
        scratch_shapes=[pltpu.VMEM((HEAD_DIM, REP * tq), F32)],
        compiler_params=_cp(("arbitrary", "arbitrary", "arbitrary"), VMEM_BIG),
    )(qkt, qkt, vta, dot, ot, lse)


HALO = 8
CONV_W = 2048 + 2 * SSD_GROUPS * SSD_N


def _shifted(win, off, r):
    return pltpu.roll(win, (r + 2 * HALO - off) % (r + 2 * HALO), 0)[0:r]


def _conv_fwd(proj, w8, brow, *, name):
    s = proj.shape[0]
    cb = 256
    r = _tile(s, 512)

    def body(x_ref, w_ref, b_ref, o_ref, pad_ref):
        zeros = jnp.zeros((HALO, cb), F32)
        pad_ref[0:HALO, :] = zeros
        pad_ref[s + HALO:s + 2 * HALO, :] = zeros

        def fill(i, carry):
            st = pl.multiple_of(i * r, r)
            pad_ref[pl.ds(st + HALO, r), :] = x_ref[pl.ds(st, r), :].astype(F32)
            return carry

        lax.fori_loop(0, s // r, fill, 0)
        wv = w_ref[...]
        bv = b_ref[...]

        def step(i, carry):
            st = pl.multiple_of(i * r, r)
            win = pad_ref[pl.ds(st, r + 2 * HALO), :]
            acc = bv + wv[0:1, :] * _shifted(win, HALO - 2, r)
            for t in range(1, D_CONV):
                acc = acc + wv[t:t + 1, :] * _shifted(win, HALO - 2 + t, r)
            o_ref[pl.ds(st, r), :] = (acc * _sigmoid(acc)).astype(o_ref.dtype)
            return carry

        lax.fori_loop(0, s // r, step, 0)

    return pl.pallas_call(
        body, name=name, grid=(CONV_W // cb,),
        in_specs=[pl.BlockSpec((s, cb), lambda j: (0, XS0 // cb + j)), pl.BlockSpec((8, cb), lambda j: (0, j)),
                  pl.BlockSpec((1, cb), lambda j: (0, j))],
        out_specs=pl.BlockSpec((s, cb), lambda j: (0, j)),
        out_shape=jax.ShapeDtypeStruct((s, CONV_W), MMD),
        scratch_shapes=[pltpu.VMEM((s + 2 * HALO, cb), F32)],
        compiler_params=_cp(("arbitrary",), VMEM_MID),
    )(proj, w8, brow)


def _conv_bwd(proj, col0, ga, gb, w8, brow, dproj, *, name):
    s = proj.shape[0]
    width = ga.shape[1]
    cb = 128
    c0 = col0 // cb
    r = _tile(s, 512)

    def body(x_ref, ga_ref, gb_ref, w_ref, b_ref, _, dx_ref, dw_ref, db_ref, xpad, dpad):
        zeros = jnp.zeros((HALO, cb), F32)
        for ref in (xpad, dpad):
            ref[0:HALO, :] = zeros
            ref[s + HALO:s + 2 * HALO, :] = zeros

        def fill(i, carry):
            st = pl.multiple_of(i * r, r)
            xpad[pl.ds(st + HALO, r), :] = x_ref[pl.ds(st, r), :].astype(F32)
            return carry

        lax.fori_loop(0, s // r, fill, 0)
        wv = w_ref[...]
        bv = b_ref[...]

        def first(i, carry):
            st = pl.multiple_of(i * r, r)
            win = xpad[pl.ds(st, r + 2 * HALO), :]
            taps = [_shifted(win, HALO - 2 + t, r) for t in range(D_CONV)]
            u = bv
            for t in range(D_CONV):
                u = u + wv[t:t + 1, :] * taps[t]
            sg = _sigmoid(u)
            du = ((ga_ref[pl.ds(st, r), :].astype(F32) + gb_ref[pl.ds(st, r), :].astype(F32))
                  * (sg * (1.0 + u * (1.0 - sg))))
            dpad[pl.ds(st + HALO, r), :] = du
            out = [carry[0] + jnp.sum(du, axis=0, keepdims=True)]
            for t in range(D_CONV):
                out.append(carry[1 + t] + jnp.sum(du * taps[t], axis=0, keepdims=True))
            return tuple(out)

        sums = lax.fori_loop(0, s // r, first, tuple(jnp.zeros((1, cb), F32) for _ in range(1 + D_CONV)))
        db_ref[...] = sums[0]
        for t in range(D_CONV):
            dw_ref[t:t + 1, :] = sums[1 + t]
        dw_ref[D_CONV:8, :] = jnp.zeros((8 - D_CONV, cb), F32)

        def second(i, carry):
            st = pl.multiple_of(i * r, r)
            win = dpad[pl.ds(st, r + 2 * HALO), :]
            acc = wv[0:1, :] * _shifted(win, HALO + 2, r)
            for t in range(1, D_CONV):
                acc = acc + wv[t:t + 1, :] * _shifted(win, HALO + 2 - t, r)
            dx_ref[pl.ds(st, r), :] = acc.astype(dx_ref.dtype)
            return carry

        lax.fori_loop(0, s // r, second, 0)

    col = pl.BlockSpec((s, cb), lambda j: (0, j))
    xcol = pl.BlockSpec((s, cb), lambda j: (0, XS0 // cb + c0 + j))
    return pl.pallas_call(
        body, name=name, grid=(width // cb,),
        in_specs=[xcol, col, col, pl.BlockSpec((8, cb), lambda j: (0, c0 + j)),
                  pl.BlockSpec((1, cb), lambda j: (0, c0 + j)), ANY],
        out_specs=[xcol, pl.BlockSpec((8, cb), lambda j: (0, j)), pl.BlockSpec((1, cb), lambda j: (0, j))],
        out_shape=[jax.ShapeDtypeStruct(dproj.shape, dproj.dtype), jax.ShapeDtypeStruct((8, width), F32),
                   jax.ShapeDtypeStruct((1, width), F32)],
        scratch_shapes=[pltpu.VMEM((s + 2 * HALO, cb), F32), pltpu.VMEM((s + 2 * HALO, cb), F32)],
        input_output_aliases={5: 0}, compiler_params=_cp(("arbitrary",), VMEM_BIG),
    )(proj, ga, gb, w8, brow, dproj)


def _tri(lower):
    i = jnp.arange(CHUNK)
    return ((i[:, None] >= i[None, :]) if lower else (i[:, None] <= i[None, :])).astype(F32)


def _dt_fwd(raw, bias, arow, *, name):
    s = raw.shape[0]

    def body(r_ref, b_ref, a_ref, lo_ref, up_ref, dt_ref, cs_ref):
        u = r_ref[...] + b_ref[...]
        dt = jnp.maximum(u, 0.0) + jnp.log1p(jnp.exp(-jnp.abs(u)))
        dt_ref[...] = dt
        a = dt * a_ref[...]
        lane = lax.broadcasted_iota(jnp.int32, (CHUNK, 128), 1)
        cs_ref[...] = jnp.where(lane < SSD_HEADS, _dot_hi(lo_ref[...], a), _dot_hi(up_ref[...], a))

    blk = pl.BlockSpec((CHUNK, 128), lambda i: (i, 0))
    row = pl.BlockSpec((1, 128), lambda i: (0, 0))
    tri = pl.BlockSpec((CHUNK, CHUNK), lambda i: (0, 0))
    return pl.pallas_call(
        body, name=name, grid=(s // CHUNK,), in_specs=[blk, row, row, tri, tri], out_specs=[blk, blk],
        out_shape=[jax.ShapeDtypeStruct((s, 128), F32)] * 2, compiler_params=_cp(("arbitrary",)),
    )(raw, bias, arow, _tri(True), _tri(False))


def _dt_bwd(ddt0, ddt1, raw, bias, dproj, *, name):
    s = raw.shape[0]
    tm = _tile(s, 1024)

    def body(d0_ref, d1_ref, r_ref, b_ref, _, o_ref, db_ref):
        g = (d0_ref[...] + d1_ref[...]) * _sigmoid(r_ref[...] + b_ref[...])
        o_ref[...] = g.astype(o_ref.dtype)
        _acc_rows(db_ref, jnp.sum(g, axis=0, keepdims=True), pl.program_id(0) == 0)

    blk = pl.BlockSpec((tm, 128), lambda i: (i, 0))
    row = pl.BlockSpec((1, 128), lambda i: (0, 0))
    return pl.pallas_call(
        body, name=name, grid=(s // tm,), in_specs=[blk, blk, blk, row, ANY],
        out_specs=[pl.BlockSpec((tm, 128), lambda i: (i, DT0 // 128)), row],
        out_shape=[jax.ShapeDtypeStruct(dproj.shape, dproj.dtype), jax.ShapeDtypeStruct((1, 128), F32)],
        input_output_aliases={4: 0}, compiler_params=_cp(("arbitrary",)),
    )(ddt0, ddt1, raw, bias, dproj)


GW = HPG * SSD_P


GPS = SSD_GROUPS


def _ssd_specs(nc, rev):
    cc = (lambda c: nc - 1 - c) if rev else (lambda c: c)
    return dict(
        x=pl.BlockSpec((CHUNK, GPS * GW), lambda g, c: (cc(c), g)),
        b=pl.BlockSpec((CHUNK, GPS * SSD_N), lambda g, c: (cc(c), 2048 // (GPS * SSD_N) + g)),
        c=pl.BlockSpec((CHUNK, GPS * SSD_N), lambda g, c: (cc(c), 2048 // (GPS * SSD_N) + 1 + g)),
        lanes=pl.BlockSpec((CHUNK, 128), lambda g, c: (cc(c), 0)),
        drow=pl.BlockSpec((1, GPS * GW), lambda g, c: (0, g)),
        y=pl.BlockSpec((CHUNK, GPS * GW), lambda g, c: (cc(c), g)),
        h=pl.BlockSpec((GPS, 1, SSD_N, GW), lambda g, c: (g, cc(c), 0, 0)),
        n=pl.BlockSpec((CHUNK, GPS * SSD_N), lambda g, c: (cc(c), g)),
    )


def _ssd_mask(anti):
    ii = lax.broadcasted_iota(jnp.int32, (CHUNK, CHUNK), 0)
    jj = lax.broadcasted_iota(jnp.int32, (CHUNK, CHUNK), 1)
    return ii, jj, (ii <= jj) if anti else (ii >= jj)


def _expand(x, ex, terms=3):
    h1 = x.astype(jnp.bfloat16)
    r1 = x - h1.astype(F32)
    h2 = r1.astype(jnp.bfloat16)
    out = _dot(h1, ex) + _dot(h2, ex)
    if terms == 3:
        out = out + _dot((r1 - h2.astype(F32)).astype(jnp.bfloat16), ex)
    return out


def _headsum(a, e):
    hi = a.astype(jnp.bfloat16)
    return _dot(hi, e) + _dot((a - hi.astype(F32)).astype(jnp.bfloat16), e)


def _expand_mats():
    lane = jnp.arange(128)[None, :, None]
    col = jnp.arange(GW)[None, None, :]
    base = (jnp.arange(2)[:, None] * SSD_HEADS + jnp.arange(SSD_GROUPS)[None, :] * HPG).reshape(2 * SSD_GROUPS, 1, 1)
    return (lane == base + col // SSD_P).astype(jnp.bfloat16)


def _headsum_mats():
    e1 = (jnp.arange(GW)[:, None] // SSD_P == jnp.arange(128)[None, :]).astype(jnp.bfloat16)
    e2 = (jnp.arange(HPG * CHUNK)[:, None] // CHUNK == jnp.arange(128)[None, :]).astype(jnp.bfloat16)
    return e1, e2


def _ssd_fwd(xc, dt, cs, ex, drow, di, *, name):
    s = xc.shape[0]
    nc = s // CHUNK
    anti = di == 1
    sp = _ssd_specs(nc, anti)
    trow = 0 if anti else CHUNK - 1

    def body(x_ref, b_ref, c_ref, dt_ref, cs_ref, ex_ref, d_ref, y_ref, hp_ref, h_ref):
        @pl.when(pl.program_id(1) == 0)
        def _():
            h_ref[...] = jnp.zeros_like(h_ref)

        mask = _ssd_mask(anti)[2]
        dtv, csv = dt_ref[...], cs_ref[...]
        cst = csv.T
        for gi in range(GPS):
            cols = slice(gi * GW, (gi + 1) * GW)
            ncols = slice(gi * SSD_N, (gi + 1) * SSD_N)
            ex = ex_ref[gi]
            xb = x_ref[:, cols].astype(F32)
            bm, cm = b_ref[:, ncols], c_ref[:, ncols]
            csr = cst[SSD_HEADS * di + HPG * gi:SSD_HEADS * di + HPG * (gi + 1)]
            dtf = _expand(dtv, ex, 2)
            csf = _expand(csv, ex, 2)
            tl = csf[trow:trow + 1, :]
            h = h_ref[gi]
            hp_ref[gi, 0] = h.astype(hp_ref.dtype)
            g = _dot(cm, bm, NT)
            xs = xb * dtf
            xsm = xs.astype(MMD)
            base = jnp.exp(csf) * _dot(cm, h.astype(MMD)) + d_ref[:, cols] * xb
            for r in range(HPG):
                sl = slice(r * SSD_P, (r + 1) * SSD_P)
                lm = jnp.exp(jnp.where(mask, csf[:, r * SSD_P:r * SSD_P + 1] - csr[r:r + 1, :], NEG))
                y_ref[:, gi * GW + r * SSD_P:gi * GW + (r + 1) * SSD_P] = (
                    _dot((g * lm).astype(MMD), xsm[:, sl]) + base[:, sl]).astype(y_ref.dtype)
            xd = (xs * jnp.exp(tl - csf)).astype(MMD)
            h_ref[gi] = h * jnp.exp(tl) + _dot(bm, xd, TN)

    return pl.pallas_call(
        body, name=name, grid=(1, nc),
        in_specs=[sp["x"], sp["b"], sp["c"], sp["lanes"], sp["lanes"],
                  pl.BlockSpec((GPS, 128, GW), lambda g, c: (di, 0, 0)), sp["drow"]],
        out_specs=[sp["y"], sp["h"]],
        out_shape=[jax.ShapeDtypeStruct((s, 2048), MMD), jax.ShapeDtypeStruct((SSD_GROUPS, nc, SSD_N, GW), MMD)],
        scratch_shapes=[pltpu.VMEM((GPS, SSD_N, GW), F32)],
        compiler_params=_cp(("arbitrary", "arbitrary")),
    )(xc, xc, xc, dt, cs, ex, drow)


def _ssd_bwd(xc, dt, cs, ex, drow, arow, dy, hprev, di, *, name):
    s = xc.shape[0]
    nc = s // CHUNK
    anti = di == 1
    sp = _ssd_specs(nc, not anti)
    trow = 0 if anti else CHUNK - 1
    e1, e2 = _headsum_mats()

    def body(x_ref, b_ref, c_ref, dt_ref, cs_ref, ex_ref, d_ref, a_ref, dy_ref, hp_ref, tri_ref,
             e1_ref, e2_ref, dx_ref, db_ref, dc_ref, ddt_ref, da_ref, dh_ref, w_ref, dxs_ref):
        @pl.when(pl.program_id(1) == 0)
        def _():
            dh_ref[...] = jnp.zeros_like(dh_ref)
            da_ref[...] = jnp.zeros_like(da_ref)

        e1v = e1_ref[...]
        ii, _, mask = _ssd_mask(anti)
        dtv, csv = dt_ref[...], cs_ref[...]
        cst = csv.T
        ddt_acc = jnp.zeros((CHUNK, 128), F32)
        da_acc = jnp.zeros((1, 128), F32)
        for gi in range(GPS):
            lane0 = SSD_HEADS * di + HPG * gi
            cols = slice(gi * GW, (gi + 1) * GW)
            ncols = slice(gi * SSD_N, (gi + 1) * SSD_N)
            ex = ex_ref[gi]
            xb = x_ref[:, cols].astype(F32)
            bm, cm = b_ref[:, ncols], c_ref[:, ncols]
            csr = cst[lane0:lane0 + HPG]
            dym = dy_ref[:, cols]
            dyb = dym.astype(F32)
            hpm = hp_ref[gi, 0]
            hp = hpm.astype(F32)
            dh = dh_ref[gi]
            dhm = dh.astype(MMD)
            dtf = _expand(dtv, ex, 2)
            csf = _expand(csv, ex, 2)
            tl = csf[trow:trow + 1, :]
            e = jnp.exp(csf)
            dec = jnp.exp(tl - csf)
            et = jnp.exp(tl)
            xs = xb * dtf
            xsm = xs.astype(MMD)
            g = _dot(cm, bm, NT)
            z = _dot(cm, hpm)
            bdh = _dot(bm, dhm)
            dg = jnp.zeros((CHUNK, CHUNK), F32)
            wcols = jnp.zeros((CHUNK, CHUNK), F32)
            for r in range(HPG):
                sl = slice(r * SSD_P, (r + 1) * SSD_P)
                lm = jnp.exp(jnp.where(mask, csf[:, r * SSD_P:r * SSD_P + 1] - csr[r:r + 1, :], NEG))
                mm = g * lm
                dm = _dot(dym[:, sl], xsm[:, sl], NT)
                w = dm * mm
                w_ref[gi, :, r * CHUNK:(r + 1) * CHUNK] = w
                wcols = jnp.where(ii == r, jnp.sum(w, axis=0, keepdims=True), wcols)
                dg = dg + dm * lm
                dxs_ref[gi, :, sl] = _dot(mm.astype(MMD), dym[:, sl], TN)
            dxs = dxs_ref[gi] + dec * bdh
            dx_ref[:, cols] = (dxs * dtf + d_ref[:, cols] * dyb).astype(dx_ref.dtype)
            tb = xs * bdh * dec
            d_tot = jnp.sum(tb, axis=0, keepdims=True) + et * jnp.sum(dh * hp, axis=0, keepdims=True)
            d_tot = _headsum(jnp.broadcast_to(d_tot, (8, GW)), e1v)[0:1]
            dcs = (_headsum(dyb * (e * z) - tb, e1v) + _headsum(w_ref[gi], e2_ref[...]) - wcols.T
                   + jnp.where(ii == trow, d_tot, 0.0))
            da = pltpu.roll(_dot_hi(tri_ref[...], dcs), lane0, 1)
            ddt_acc = ddt_acc + da * a_ref[...] + pltpu.roll(_headsum(dxs * xb, e1v), lane0, 1)
            da_acc = da_acc + jnp.sum(da * dtv, axis=0, keepdims=True)
            dgm = dg.astype(MMD)
            dz = (e * dyb).astype(MMD)
            dc_ref[:, ncols] = (_dot(dgm, bm) + _dot(dz, hpm, NT)).astype(dc_ref.dtype)
            db_ref[:, ncols] = (_dot(dgm, cm, TN) + _dot((xs * dec).astype(MMD), dhm, NT)).astype(db_ref.dtype)
            dh_ref[gi] = dh * et + _dot(cm, dz, TN)
        ddt_ref[...] = ddt_acc
        da_ref[...] += da_acc

    const = lambda shape: pl.BlockSpec(shape, lambda g, c: (0,) * len(shape))
    return pl.pallas_call(
        body, name=name, grid=(1, nc),
        in_specs=[sp["x"], sp["b"], sp["c"], sp["lanes"], sp["lanes"],
                  pl.BlockSpec((GPS, 128, GW), lambda g, c: (di, 0, 0)), sp["drow"],
                  const((1, 128)), sp["y"], sp["h"],
                  const((CHUNK, CHUNK)), const((GW, 128)), const((HPG * CHUNK, 128))],
        out_specs=[sp["y"], sp["n"], sp["n"], sp["lanes"], const((1, 128))],
        out_shape=[jax.ShapeDtypeStruct((s, 2048), MMD), jax.ShapeDtypeStruct((s, SSD_GROUPS * SSD_N), MMD),
                   jax.ShapeDtypeStruct((s, SSD_GROUPS * SSD_N), MMD), jax.ShapeDtypeStruct((s, 128), F32),
                   jax.ShapeDtypeStruct((1, 128), F32)],
        scratch_shapes=[pltpu.VMEM((GPS, SSD_N, GW), F32), pltpu.VMEM((GPS, CHUNK, HPG * CHUNK), F32),
                        pltpu.VMEM((GPS, CHUNK, GW), F32)],
        compiler_params=_cp(("arbitrary", "arbitrary")),
    )(xc, xc, xc, dt, cs, ex, drow, arow, dy, hprev, _tri(anti), e1, e2)


def _gnorm_fwd(ya, yb, proj, w, *, name):
    s = ya.shape[0]
    tm = _tile(s, 256)

    def body(a_ref, b_ref, z_ref, w_ref, o_ref):
        zv = z_ref[...].astype(F32)
        t = (a_ref[...].astype(F32) + b_ref[...].astype(F32)) * (zv * _sigmoid(zv))
        r = lax.rsqrt(jnp.mean(t * t, axis=-1, keepdims=True) + EPS)
        o_ref[...] = ((t * r) * w_ref[...]).astype(o_ref.dtype)

    big = pl.BlockSpec((tm, 2048), lambda i: (i, 0))
    row = pl.BlockSpec((1, 2048), lambda i: (0, 0))
    return pl.pallas_call(
        body, name=name, grid=(s // tm,), in_specs=[big, big, big, row], out_specs=big,
        out_shape=jax.ShapeDtypeStruct((s, 2048), MMD), compiler_params=_cp(("arbitrary",)),
    )(ya, yb, proj, w)


def _gnorm_bwd(dout, ya, yb, proj, xc, w, dproj, *, name):
    s = ya.shape[0]
    tm = _tile(s, 256)

    def body(do_ref, a_ref, b_ref, z_ref, x_ref, w_ref, _, dy_ref, dz_ref, dw_ref, dd_ref):
        zv = z_ref[...].astype(F32)
        sg = _sigmoid(zv)
        sz = zv * sg
        y = a_ref[...].astype(F32) + b_ref[...].astype(F32)
        t = y * sz
        r = lax.rsqrt(jnp.mean(t * t, axis=-1, keepdims=True) + EPS)
        nv = t * r
        dov = do_ref[...].astype(F32)
        _acc_rows(dw_ref, jnp.sum(dov * nv, axis=0, keepdims=True), pl.program_id(0) == 0)
        dn = dov * w_ref[...]
        dt_ = r * (dn - nv * jnp.mean(dn * nv, axis=-1, keepdims=True))
        dy = dt_ * sz
        dy_ref[...] = dy.astype(dy_ref.dtype)
        dz_ref[...] = (dt_ * y * (sg * (1.0 + zv * (1.0 - sg)))).astype(dz_ref.dtype)
        _acc_rows(dd_ref, jnp.sum(dy * x_ref[...].astype(F32), axis=0, keepdims=True), pl.program_id(0) == 0)

    big = pl.BlockSpec((tm, 2048), lambda i: (i, 0))
    row = pl.BlockSpec((1, 2048), lambda i: (0, 0))
    return pl.pallas_call(
        body, name=name, grid=(s // tm,), in_specs=[big, big, big, big, big, row, ANY], out_specs=[big, big, row, row],
        out_shape=[jax.ShapeDtypeStruct((s, 2048), MMD), jax.ShapeDtypeStruct(dproj.shape, dproj.dtype),
                   jax.ShapeDtypeStruct((1, 2048), F32), jax.ShapeDtypeStruct((1, 2048), F32)],
        input_output_aliases={6: 1}, compiler_params=_cp(("arbitrary",)),
    )(dout, ya, yb, proj, xc, w, dproj)


def _unheads(a):
    return a.transpose(1, 0, 2).reshape(a.shape[1], a.shape[0] * HEAD_DIM)


def _local_step(x, target, mod, wts, small, in_weights=None, late_weights=None, late_grads=None, in_grad=None,
                zero=0.0):
    s, d = x.shape
    shift1, scale1, gate1, shift2, scale2, gate2 = [mod[i:i + 1] for i in range(6)]

    h1 = _ln_mod(x, small["norm1_w"], scale1, shift1, name="ln1")
    qk_w = jnp.concatenate([jnp.tile(small["q_norm_w"], (1, N_Q_HEADS)), jnp.tile(small["k_norm_w"], (1, N_KV_HEADS))], axis=1)
    qk_sc = jnp.concatenate([jnp.full((1, N_Q_HEADS * HEAD_DIM), HEAD_DIM ** -0.5, F32),
                             jnp.ones((1, N_KV_HEADS * HEAD_DIM), F32)], axis=1)
    qk_sc2 = jnp.concatenate([jnp.full((1, N_Q_HEADS * HEAD_DIM), HEAD_DIM ** -0.5 * LOG2E, F32),
                              jnp.ones((1, N_KV_HEADS * HEAD_DIM), F32)], axis=1)
    tabs = _rope_tables(s, zero)
    if in_weights is not None:
        wts = {**wts, **in_weights([h1, *tabs])}
    proj = _mm(h1, wts["w_in_p"], name="in_proj", outs=[MMD], tm=512, tn=2944, b_outer=True)
    dt_raw = _mm(h1, wts["w_dt"], name="dt_proj", outs=[F32], tm=512, tn=128)
    qkt = _qk_fwd(proj, qk_w, qk_sc2, tabs, name="qk_fwd").reshape(N_Q_HEADS + N_KV_HEADS, HEAD_DIM, s)
    v_sd = proj[:, V0:V0 + N_KV_HEADS * HEAD_DIM]
    vta = jnp.concatenate([v_sd.T.reshape(N_KV_HEADS, HEAD_DIM, s), jnp.ones((N_KV_HEADS, V_AUG - HEAD_DIM, s), MMD)], axis=1)
    ot, lse = _flash_fwd(qkt, vta, name="flash_fwd")
    ot2 = ot.reshape(N_Q_HEADS * HEAD_DIM, s)
    if late_weights is not None:
        wts = {**wts, **late_weights(ot)}

    w8 = jnp.pad(small["conv_w"], ((0, 8 - D_CONV), (0, 0)))
    xc = _conv_fwd(proj, w8, small["conv_b"], name="conv_fwd")
    a_neg = -jnp.exp(small["A_log"])
    arow = jnp.pad(a_neg.reshape(1, 2 * SSD_HEADS), ((0, 0), (0, 128 - 2 * SSD_HEADS)))
    bias_row = jnp.pad(small["dt_bias"].reshape(1, 2 * SSD_HEADS), ((0, 0), (0, 128 - 2 * SSD_HEADS)))
    dt, cs = _dt_fwd(dt_raw, bias_row, arow, name="dt_fwd")
    drow = jnp.repeat(small["ssd_D"], SSD_P, axis=1)
    dirs = [dict(drow=drow), dict(drow=jnp.zeros_like(drow))]
    ex = _expand_mats()
    ys = []
    for di, dd in enumerate(dirs):
        y, dd["hprev"] = _ssd_fwd(xc, dt, cs, ex, dd["drow"], di, name=f"ssd_fwd{di}")
        ys.append(y)
    ssdn = _gnorm_fwd(ys[0], ys[1], proj, small["ssd_norm_w"], name="gnorm_fwd")

    a_o = _mm(ot2, wts["w_attn_out"], name="attn_out", outs=[MMD], ta=True, tm=512, tn=1024)

    def merge_epi(acc, ao, ga, gs):
        return (_sigmoid(ga.astype(F32)) * ao.astype(F32) + _sigmoid(gs.astype(F32)) * acc, acc)

    merged, b_o = _mm(ssdn, wts["w_ssd_out"], name="ssd_out", outs=[MMD, MMD], tm=512, tn=1024,
                      extras=[(a_o, "tile", 0), (proj, "tile", GA0), (proj, "tile", GS0)], epi=merge_epi)

    def res_epi(acc, res, gate):
        return (res + gate * acc, acc)

    x1, mo = _mm(merged, wts["w_o"], name="w_o", outs=[F32, MMD], tm=512, tn=1024,
                 extras=[(x, "tile", 0), (gate1, "row", 0)], epi=res_epi)
    h2 = _ln_mod(x1, small["norm2_w"], scale2, shift2, name="ln2")

    def relu2_epi(acc):
        rl = jnp.maximum(acc, 0.0)
        return (rl * rl, rl)

    act, rl = _mm(h2, wts["w_mlp1"], name="mlp1", outs=[MMD, MMD], tm=1024, tn=1024, epi=relu2_epi, b_outer=True)

    def loss_epi(acc, res, gate, tgt):
        dy_ = (res + gate * acc - tgt) * (1.0 / d)
        return (dy_, dy_ * gate, jnp.sum(dy_ * acc, axis=0, keepdims=True), jnp.sum(dy_ * dy_, axis=0, keepdims=True))

    dy, dffo, dgate2, sq = _mm(act, wts["w_mlp2"], name="mlp2", outs=[F32, MMD], tm=512, tn=1024, vmem=VMEM_BIG, row_sums=2,
                               extras=[(x1, "tile", 0), (gate2, "row", 0), (target, "tile", 0)], epi=loss_epi)
    loss = jnp.sum(sq, axis=1, keepdims=True) * (0.5 * d)

    gw = {}
    gs_ = {}
    dpre = _mm(dffo, wts["w_mlp2"], name="mlp2_dx", outs=[MMD], nt=True, tm=1024, tn=1024, b_outer=True,
               extras=[(rl, "tile", 0)], epi=lambda acc, r: (acc * (2.0 * r.astype(F32)),))
    gw["w_mlp2"] = _mm_tn(act, dffo, name="mlp2_dw")
    dh2 = _mm(dpre, wts["w_mlp1"], name="mlp1_dx", outs=[F32], nt=True, tm=1024, tn=1024, vmem=VMEM_BIG)
    gw["w_mlp1"] = _mm_tn(h2, dpre, name="mlp1_dw")
    dx1, dshift2, dscale2, gs_["norm2_w"] = _ln_mod_bwd(dh2, x1, small["norm2_w"], scale2, dy, name="ln2_bwd")
    dmo, dgate1 = _gate_bwd(dx1, mo, gate1, name="gate1_bwd")

    def merge_bwd_epi(acc, ao, bo, ga, gs):
        sa, ss = _sigmoid(ga.astype(F32)), _sigmoid(gs.astype(F32))
        return (acc * sa, acc * ss, acc * ao.astype(F32) * sa * (1.0 - sa), acc * bo.astype(F32) * ss * (1.0 - ss))

    da_o, db_o, dga, dgs = _mm(dmo, wts["w_o"], name="w_o_dx", outs=[MMD] * 4, nt=True, tm=512, tn=1024,
                               extras=[(a_o, "tile", 0), (b_o, "tile", 0), (proj, "tile", GA0), (proj, "tile", GS0)],
                               epi=merge_bwd_epi)
    gw["w_o"] = _mm_tn(merged, dmo, name="w_o_dw")
    dot = _mm(wts["w_attn_out"], da_o, name="attn_out_dx", outs=[MMD], nt=True, tm=1024, tn=1024)
    gw["w_attn_out"] = _mm(ot2, da_o, name="attn_out_dw", outs=[F32], tm=256, tn=512, vmem=VMEM_BIG)
    dssdn = _mm(db_o, wts["w_ssd_out"], name="ssd_out_dx", outs=[MMD], nt=True, tm=512, tn=2048)
    gw["w_ssd_out"] = _mm_tn(ssdn, db_o, name="ssd_out_dw")

    dproj = lax.dynamic_update_slice(lax.empty((s, PW), MMD), jnp.concatenate([dga, dgs], axis=1), (0, GA0))

    norm_w = small["ssd_norm_w"] if late_grads is None else small["ssd_norm_w"] + late_grads(gw)
    dyssd, dproj, gs_["ssd_norm_w"], dd_row = _gnorm_bwd(dssdn, ys[0], ys[1], proj, xc, norm_w, dproj, name="gnorm_bwd")
    gs_["ssd_D"] = dd_row.reshape(SSD_HEADS, SSD_P).sum(axis=1).reshape(1, SSD_HEADS)
    dxc, ddts, das = [], [], []
    for di, dd in enumerate(dirs):
        dxs, dbm, dcm, ddt_d, da_d = _ssd_bwd(xc, dt, cs, ex, dd["drow"], arow, dyssd, dd["hprev"], di, name=f"ssd_bwd{di}")
        dxc.append((dxs, dbm, dcm))
        ddts.append(ddt_d)
        das.append(da_d)
    dw8, db, col0 = [], [], 0
    for part, (ga, gb) in enumerate(zip(*dxc)):
        dproj, dw_part, db_part = _conv_bwd(proj, col0, ga, gb, w8, small["conv_b"], dproj, name=f"conv_bwd{part}")
        dw8.append(dw_part)
        db.append(db_part)
        col0 += ga.shape[1]
    gs_["conv_w"] = jnp.concatenate(dw8, axis=1)[0:D_CONV]
    gs_["conv_b"] = jnp.concatenate(db, axis=1)
    gs_["A_log"] = (das[0] + das[1])[:, 0:2 * SSD_HEADS].reshape(2, SSD_HEADS) * a_neg
    dproj, dbias = _dt_bwd(ddts[0], ddts[1], dt_raw, bias_row, dproj, name="dt_bwd")
    gs_["dt_bias"] = dbias[:, 0:2 * SSD_HEADS].reshape(2, SSD_HEADS)

    dqt, dk_h, dv_h = _flash_bwd(qkt, vta, dot.reshape(N_Q_HEADS, HEAD_DIM, s), ot, lse, name="flash_bwd")
    dproj, dqk_w = _qk_bwd(dqt.reshape(N_Q_HEADS * HEAD_DIM, s), dk_h.transpose(0, 2, 1).reshape(N_KV_HEADS * HEAD_DIM, s),
                           proj, qk_w, qk_sc, tabs, dproj, name="qk_bwd")
    gs_["q_norm_w"] = dqk_w[:, 0:N_Q_HEADS * HEAD_DIM].reshape(N_Q_HEADS, HEAD_DIM).sum(axis=0, keepdims=True)
    gs_["k_norm_w"] = dqk_w[:, N_Q_HEADS * HEAD_DIM:].reshape(N_KV_HEADS, HEAD_DIM).sum(axis=0, keepdims=True)
    dproj = lax.dynamic_update_slice(dproj, _unheads(dv_h).astype(MMD), (0, V0))

    gw["w_in_p"] = _mm_tn(h1, dproj, name="in_proj_dw", tk=512, tn=2944, tmm=2048, vmem=VMEM_BIG)
    zero_row = jnp.zeros((1, d), F32) if in_grad is None else jnp.zeros((1, d), F32) + in_grad(gw["w_in_p"])[0:1, 0:1]
    dh1 = _mm(dproj, wts["w_in_p"], name="in_proj_dx", outs=[F32], nt=True, tm=256, tn=1024, vmem=VMEM_BIG,
              extras=[(zero_row, "row", 0)], epi=lambda acc, r: (acc + r,))
    grad_x, dshift1, dscale1, gs_["norm1_w"] = _ln_mod_bwd(dh1, x, small["norm1_w"], scale1, dx1, name="ln1_bwd")
    dmod = jnp.concatenate([dshift1, dscale1, dgate1, dshift2, dscale2, dgate2], axis=0)
    return loss, grad_x, dmod, gw, gs_


N_DEV = 8
N_CHIP = 4
ANY = pl.BlockSpec(memory_space=pl.ANY)


def _place():
    return lax.axis_index("x"), lax.axis_index("y"), lax.axis_index("c")


def _allgather8(v, *, name):
    m_per, n = v.shape

    def body(x_ref, out_ref, send_sems, recv_sems, local_sem):
        x, y, c = _place()
        me, sibling = (x, y, c), (x, y, 1 - c)
        chips = [(1 - x, y), (x, 1 - y), (1 - x, 1 - y)]

        def rows(px, py, pc):
            return out_ref.at[pl.ds((4 * px + 2 * py + pc) * m_per, m_per), :]

        def copy(k, block, to, src=None):
            return pltpu.make_async_remote_copy(
                src_ref=rows(*block) if src is None else src, dst_ref=rows(*block),
                send_sem=send_sems.at[k], recv_sem=recv_sems.at[k], device_id=to, device_id_type=MESH)

        mine = pltpu.make_async_copy(x_ref, rows(*me), local_sem)
        mine.start()
        first = [copy(0, me, sibling, src=x_ref)]
        first += [copy(1 + j, me, (*chip, c), src=x_ref) for j, chip in enumerate(chips)]
        for cp in first:
            cp.start()
        passed = [copy(4 + j, (*chip, c), sibling) for j, chip in enumerate(chips)]
        for j, chip in enumerate(chips):
            copy(1 + j, (*chip, c), me).wait_recv()
            passed[j].start()
        copy(0, sibling, me).wait_recv()
        for j, chip in enumerate(chips):
            copy(4 + j, (*chip, 1 - c), me).wait_recv()
        for cp in first + passed:
            cp.wait_send()
        mine.wait()

    return pl.pallas_call(
        body, name=name, out_shape=jax.ShapeDtypeStruct((N_DEV * m_per, n), v.dtype),
        in_specs=[pl.BlockSpec(memory_space=pltpu.VMEM)], out_specs=pl.BlockSpec(memory_space=pltpu.VMEM),
        scratch_shapes=[pltpu.SemaphoreType.DMA((7,)), pltpu.SemaphoreType.DMA((7,)), pltpu.SemaphoreType.DMA],
    )(v)


HBM = pl.BlockSpec(memory_space=pltpu.HBM)
SEM = pl.BlockSpec(memory_space=pltpu.SEMAPHORE)


def _chips_copies(x_ref, land_ref, sems, scatter, half=False):
    x, y, c = _place()
    k = 2 * x + y
    chips = [(1 - x, y), (x, 1 - y), (1 - x, 1 - y)]
    ids = [2 * cx + cy for cx, cy in chips]
    if half:
        hr = x_ref.shape[0] // 2
        rows = pl.ds(pl.multiple_of(c * hr, 16), hr)

    def copy(j, slot):
        src = x_ref.at[ids[j]] if scatter else (x_ref.at[rows] if half else x_ref)
        dst = land_ref.at[slot, rows] if half else land_ref.at[slot]
        return pltpu.make_async_remote_copy(src_ref=src, dst_ref=dst, send_sem=sems[j], recv_sem=sems[3 + j],
                                            device_id=(*chips[j], c), device_id_type=MESH)

    return [copy(j, k) for j in range(3)], [copy(j, ids[j]) for j in range(3)]


def _chips_start(src, scatter, half=False, *, name):
    shape = src.shape if scatter else (N_CHIP,) + tuple(src.shape)

    def body(x_ref, land_ref, *rest):
        sems, token = rest[0:6], rest[8]
        for cp in _chips_copies(x_ref, land_ref, sems, scatter, half)[0]:
            cp.start()
        token[...] = jnp.zeros_like(token)

    out = pl.pallas_call(
        body, name=name,
        out_shape=(pltpu.SemaphoreType.DMA(()),) * 6 + (pltpu.HBM(src.shape, src.dtype), pltpu.HBM(shape, src.dtype),
                                                       jax.ShapeDtypeStruct((8, 128), F32)),
        in_specs=(HBM, HBM), out_specs=(SEM,) * 6 + (HBM, HBM, pl.BlockSpec(memory_space=pltpu.VMEM)),
        input_output_aliases={0: 6, 1: 7},
        compiler_params=pltpu.CompilerParams(has_side_effects=pltpu.SideEffectType.DATAFLOW_SIDE_EFFECTING),
    )(pltpu.with_memory_space_constraint(src, pltpu.HBM),
      pltpu.with_memory_space_constraint(lax.empty(shape, src.dtype), pltpu.HBM))
    return out[0:6], out[6], out[7], out[8]


def _chips_wait(sems, src, land, after, scatter, half=False, *, name):
    after = list(after) if isinstance(after, (list, tuple)) else [after]

    def body(x_ref, land_ref, *rest):
        sems_ = rest[0:6]
        for cp in _chips_copies(x_ref, land_ref, sems_, scatter, half)[1]:
            cp.wait_send()
            cp.wait_recv()

    return pl.pallas_call(
        body, name=name, out_shape=(pltpu.HBM(src.shape, src.dtype), pltpu.HBM(land.shape, land.dtype)),
        in_specs=(HBM, HBM) + (SEM,) * 6 + (ANY,) * len(after), out_specs=(HBM, HBM), input_output_aliases={0: 0, 1: 1},
        compiler_params=pltpu.CompilerParams(has_side_effects=pltpu.SideEffectType.DATAFLOW_SIDE_EFFECTING),
    )(src, land, *sems, *after)


def _row_tile(r, pref=512):
    return max(t for t in range(16, pref + 1, 16) if r % t == 0)


def _pair_complete(land, *, name):
    r = land.shape[1]
    hr = r // 2
    assert r == 2 * hr and hr % 16 == 0

    def body(in_ref, out_ref, send_sems, recv_sems):
        x, y, c = _place()
        ids = [2 * cx + cy for cx, cy in [(1 - x, y), (x, 1 - y), (1 - x, 1 - y)]]
        mine_rows = pl.ds(pl.multiple_of(c * hr, 16), hr)
        other_rows = pl.ds(pl.multiple_of((1 - c) * hr, 16), hr)

        def copy(j, rows):
            return pltpu.make_async_remote_copy(
                src_ref=in_ref.at[ids[j], mine_rows], dst_ref=out_ref.at[ids[j], rows], send_sem=send_sems.at[j],
                recv_sem=recv_sems.at[j], device_id=(x, y, 1 - c), device_id_type=MESH)

        sends = [copy(j, mine_rows) for j in range(3)]
        for cp in sends:
            cp.start()
        for j in range(3):
            copy(j, other_rows).wait_recv()
        for cp in sends:
            cp.wait_send()

    return pl.pallas_call(
        body, name=name, out_shape=jax.ShapeDtypeStruct(land.shape, land.dtype), in_specs=[ANY], out_specs=ANY,
        input_output_aliases={0: 0},
        scratch_shapes=[pltpu.SemaphoreType.DMA((3,)), pltpu.SemaphoreType.DMA((3,))],
    )(land)


def _pair_swap(a, *, name):
    n, r, cols = a.shape
    hr = r // 2

    def body(x_ref, out_ref, send_sem, recv_sem):
        x, y, c = _place()
        other_rows = pl.ds(pl.multiple_of((1 - c) * hr, 16), hr)
        cp = pltpu.make_async_remote_copy(src_ref=x_ref.at[:, other_rows], dst_ref=out_ref, send_sem=send_sem,
                                          recv_sem=recv_sem, device_id=(x, y, 1 - c), device_id_type=MESH)
        cp.start()
        cp.wait()

    return pl.pallas_call(
        body, name=name, out_shape=jax.ShapeDtypeStruct((n, hr, cols), a.dtype), in_specs=[ANY], out_specs=ANY,
        scratch_shapes=[pltpu.SemaphoreType.DMA, pltpu.SemaphoreType.DMA],
    )(a)


def _sibling_copy(a, *, name):
    def body(x_ref, out_ref, send_sem, recv_sem):
        x, y, c = _place()
        cp = pltpu.make_async_remote_copy(src_ref=x_ref, dst_ref=out_ref, send_sem=send_sem, recv_sem=recv_sem,
                                          device_id=(x, y, 1 - c), device_id_type=MESH)
        cp.start()
        cp.wait()

    return pl.pallas_call(
        body, name=name, out_shape=jax.ShapeDtypeStruct(a.shape, a.dtype), in_specs=[ANY], out_specs=ANY,
        scratch_shapes=[pltpu.SemaphoreType.DMA, pltpu.SemaphoreType.DMA],
    )(a)


def _sum_slots(a, own, *, name):
    _, r, c = a.shape
    tr = _row_tile(r, 256)

    def body(a_ref, own_ref, o_ref):
        k = 2 * lax.axis_index("x") + lax.axis_index("y")
        acc = None
        for j in range(N_CHIP):
            term = jnp.where(k == j, own_ref[j], a_ref[j]).astype(F32)
            acc = term if acc is None else acc + term
        o_ref[...] = acc

    spec = pl.BlockSpec((N_CHIP, tr, c), lambda i: (0, i, 0))
    return pl.pallas_call(
        body, name=name, grid=(r // tr,), in_specs=[spec, spec],
        out_specs=pl.BlockSpec((tr, c), lambda i: (i, 0)), out_shape=jax.ShapeDtypeStruct((r, c), F32),
        compiler_params=_cp(("arbitrary",)),
    )(a, own)


def _add2(a, b, *, name):
    r, c = a.shape
    tr = _row_tile(r)

    def body(a_ref, b_ref, o_ref):
        o_ref[...] = (a_ref[...].astype(F32) + b_ref[...].astype(F32)).astype(o_ref.dtype)

    spec = pl.BlockSpec((tr, c), lambda i: (i, 0))
    return pl.pallas_call(
        body, name=name, grid=(r // tr,), in_specs=[spec, spec], out_specs=spec,
        out_shape=jax.ShapeDtypeStruct((r, c), a.dtype), compiler_params=_cp(("arbitrary",)),
    )(a, b)


BIG = ("w_in", "w_mlp1", "w_attn_out", "w_ssd_out", "w_o", "w_mlp2")
COL_SHARDED = ("w_mlp1", "w_in")
ROW_SHARDED = ("w_attn_out", "w_ssd_out", "w_o", "w_mlp2")
LATE = ROW_SHARDED + ("w_mlp1",)
SMALL = ("b_ada", "norm1_w", "norm2_w", "q_norm_w", "k_norm_w", "conv_b", "A_log", "dt_bias", "ssd_D", "ssd_norm_w")
NAMES = ("w_ada", "b_ada", "norm1_w", "norm2_w", "w_in", "q_norm_w", "k_norm_w", "conv_w", "conv_b", "A_log", "dt_bias",
         "ssd_D", "ssd_norm_w", "w_attn_out", "w_ssd_out", "w_o", "w_mlp1", "w_mlp2")
W_IN_COLS = 8768


def _permute_in(w):
    return jnp.concatenate([w[:, 4608:6656], w[:, 6720:8768], w[:, 1536:4608], w[:, 0:1536], w[:, 6656:6720],
                            jnp.zeros((w.shape[0], PW - W_IN_COLS), w.dtype)], axis=1)


def _unpermute_in(wp):
    return jnp.concatenate([wp[:, Q0:DT0], wp[:, XS0:Q0], wp[:, Z0:GA0], wp[:, DT0:DT0 + 64], wp[:, GA0:XS0]], axis=1)


def _pad_to(v, n):
    return jnp.pad(v, (0, n - v.shape[0]))


def _step(w, m, v, loss_target):
    xi, yi, ci = _place()
    chip = 2 * xi + yi
    dev = 4 * xi + 2 * yi + ci
    x, tgt = w["x"], loss_target
    d = x.shape[1]

    cw = w["conv_w"].shape[1]
    v0 = _pad_to(jnp.concatenate([w["c"].reshape(-1), w["conv_w"].reshape(-1)]), 5120).reshape(8, 640)
    g0 = _allgather8(v0, name="ag_cond").reshape(N_DEV, 5120)
    c_all = g0[:, 0:d]
    conv_w = jnp.concatenate([g0[2 * k, d:d + D_CONV * cw].reshape(D_CONV, cw) for k in range(N_CHIP)], axis=1)
    sc = _silu_cast(c_all, name="silu_c")
    modp = _mm(sc, w["w_ada"].astype(MMD), name="ada_fwd", outs=[F32], tm=8, tn=512)
    g1 = _allgather8(modp, name="ag_mod").reshape(N_DEV, N_DEV, modp.shape[1])
    mod_all = jnp.concatenate([g1[2 * k] for k in range(N_CHIP)], axis=1)
    mod = (lax.dynamic_slice_in_dim(mod_all, dev, 1, axis=0) + w["b_ada"]).reshape(6, d)

    mine, mod = lax.optimization_barrier((w["w_in"].astype(MMD), mod))
    in_sems, in_src, in_land, in_token = _chips_start(mine, False, True, name="ag_w_in_start")
    mod = mod + in_token[0:1, 0:1]
    small = {n: w[n] for n in SMALL if n != "b_ada"}
    small["conv_w"] = conv_w
    started = {}

    late_mine = jnp.concatenate([w[n].astype(MMD) for n in LATE], axis=0) + in_token[0:1, 0:1].astype(MMD)

    def in_weights(after):
        src, land = _chips_wait(in_sems, in_src, in_land, [*after, late_mine], False, True, name="ag_w_in_wait")
        land = _pair_complete(land, name="ag_w_in_pair")
        late, land = lax.optimization_barrier((late_mine, land))
        sems, late_src, late_land, token = _chips_start(late, False, name="ag_late_start")
        started["ag_late"] = (sems, late_src, late_land)
        w_in = jnp.concatenate([jnp.where(chip == k, src, land[k]) for k in range(N_CHIP)], axis=1)
        w_dt = jnp.pad(w_in[:, 6656:6720], ((0, 0), (0, 64))) + token[0:1, 0:1].astype(MMD)
        return {"w_in_p": _permute_in(w_in), "w_dt": w_dt}

    def late_weights(after):
        src, land = _chips_wait(*started["ag_late"], after, False, name="ag_late_wait")
        out, o = {}, 0
        for n in LATE:
            rows = w[n].shape[0]
            parts = [jnp.where(chip == k, src[o:o + rows], land[k, o:o + rows]) for k in range(N_CHIP)]
            out[n] = jnp.concatenate(parts, axis=1 if n in COL_SHARDED else 0)
            o += rows
        return out

    def pair_sums(slots, tag):
        _, rows, cols = slots.shape
        hr = rows // 2
        theirs = _pair_swap(slots, name="rs_pair_" + tag)
        ours = lax.dynamic_slice_in_dim(slots, ci * hr, hr, axis=1)
        pair = _add2(ours.reshape(N_CHIP * hr, cols), theirs.reshape(N_CHIP * hr, cols), name="rs_pair_sum_" + tag)
        return pair.reshape(N_CHIP, hr, cols)

    def finish(recv, pair, tag):
        half = _sum_slots(recv, pair, name="rs_sum_" + tag)
        other = _sibling_copy(half, name="rs_sibling_" + tag)
        return jnp.where(ci == 0, jnp.concatenate([half, other], axis=0), jnp.concatenate([other, half], axis=0))

    def late_grads(gw):
        slots = []
        for k in range(N_CHIP):
            parts = []
            for n in LATE:
                rows = w[n].shape[0]
                blk = gw[n][:, k * rows:(k + 1) * rows] if n in COL_SHARDED else gw[n][k * rows:(k + 1) * rows]
                parts.append(blk.astype(MMD))
            slots.append(jnp.concatenate(parts, axis=0))
        pair = pair_sums(jnp.stack(slots), "late")
        sems, src, land, token = _chips_start(pair, True, name="rs_late_start")
        started["late"] = (sems, src, land)
        return token[0:1, 0:1]

    def in_grad(g):
        g_in = _unpermute_in(g)
        cols_in = w["w_in"].shape[1]
        pair = pair_sums(jnp.stack([g_in[:, k * cols_in:(k + 1) * cols_in].astype(MMD) for k in range(N_CHIP)]), "w_in")
        sems, src, land, token = _chips_start(pair, True, name="rs_w_in_start")
        started["w_in"] = (sems, src, land)
        return token

    loss, grad_x, dmod, gw, gs = _local_step(x, tgt, mod, {}, small, in_weights, late_weights, late_grads, in_grad,
                                             in_token[0, 0])

    grads = {}
    pair, land = _chips_wait(*started["w_in"], grad_x, True, name="rs_w_in_wait")
    grads["w_in"] = finish(land, pair, "w_in")
    pair, land = _chips_wait(*started["late"], grad_x, True, name="rs_late_wait")
    total, o = finish(land, pair, "late"), 0
    for n in LATE:
        rows = w[n].shape[0]
        grads[n] = total[o:o + rows]
        o += rows

    order = ([dmod.reshape(-1)] + [gs[n].reshape(-1) for n in SMALL if n != "b_ada"] + [gs["conv_w"].reshape(-1)]
             + [loss.reshape(-1)])
    vec = jnp.concatenate(order)
    n_small = vec.shape[0]
    n_pad = -(-n_small // 1024) * 1024
    g2 = _allgather8(_pad_to(vec, n_pad).reshape(8, n_pad // 8), name="ag_small")
    tot = _rows_sum(g2, N_DEV, name="small_sum").reshape(-1)
    loss = tot[n_small - 1]
    dmod_all = g2.reshape(N_DEV, n_pad)[:, 0:6 * d]
    off = 0
    for n in SMALL:
        grads[n] = tot[off:off + w[n].size].reshape(w[n].shape)
        off += w[n].size
    conv_full = tot[off:off + D_CONV * N_CHIP * cw].reshape(D_CONV, N_CHIP * cw)
    grads["conv_w"] = lax.dynamic_slice_in_dim(conv_full, chip * cw, cw, axis=1)
    ada_cols = w["w_ada"].shape[1]
    dmod_mine = lax.dynamic_slice_in_dim(dmod_all, chip * ada_cols, ada_cols, axis=1).astype(MMD)
    grads["w_ada"] = _mm_tn(sc, dmod_mine, name="ada_dw", tk=512, tn=512, tmm=8)

    delta, new_m, new_v = {}, {}, {}
    pack = lambda t: jnp.concatenate([t[n].reshape(-1) for n in SMALL]).reshape(1, -1)
    ds_, ms_, vs_ = _adamw(pack(w), pack(grads), pack(m), pack(v), name="adamw_small")
    off = 0
    for n in SMALL:
        for dst, src in ((delta, ds_), (new_m, ms_), (new_v, vs_)):
            dst[n] = src[0, off:off + w[n].size].reshape(w[n].shape)
        off += w[n].size
    for n in ("w_ada", "conv_w") + BIG:
        delta[n], new_m[n], new_v[n] = _adamw(w[n], grads[n], m[n], v[n], name="adamw_" + n)
    return loss, grad_x, grads, delta, new_m, new_v


def kernel(x, c, w_ada, b_ada, norm1_w, norm2_w, w_in, q_norm_w, k_norm_w, conv_w, conv_b, A_log, dt_bias, ssd_D, ssd_norm_w, w_attn_out, w_ssd_out, w_o, w_mlp1, w_mlp2, loss_target, m_w_ada, m_b_ada, m_norm1_w, m_norm2_w, m_w_in, m_q_norm_w, m_k_norm_w, m_conv_w, m_conv_b, m_A_log, m_dt_bias, m_ssd_D, m_ssd_norm_w, m_w_attn_out, m_w_ssd_out, m_w_o, m_w_mlp1, m_w_mlp2, v_w_ada, v_b_ada, v_norm1_w, v_norm2_w, v_w_in, v_q_norm_w, v_k_norm_w, v_conv_w, v_conv_b, v_A_log, v_dt_bias, v_ssd_D, v_ssd_norm_w, v_w_attn_out, v_w_ssd_out, v_w_o, v_w_mlp1, v_w_mlp2):
    args = dict(locals())
    strip = lambda a: a[0] if a.ndim == 3 else a
    w = {n: strip(args[n]) for n in NAMES + ("x", "c")}
    m = {n: strip(args["m_" + n]) for n in NAMES}
    v = {n: strip(args["v_" + n]) for n in NAMES}
    loss, grad_x, grads, delta, new_m, new_v = _step(w, m, v, loss_target[0])
    like = lambda t, n: t.reshape(args[n].shape)
    return (loss, grad_x[None], *[like(grads[n], n) for n in NAMES], *[like(delta[n], n) for n in NAMES],
            *[like(new_m[n], n) for n in NAMES], *[like(new_v[n], n) for n in NAMES])
```
